```python
import math
import jax, jax.numpy as jnp
from jax import lax
import numpy as np

D_MODEL = 1024
BATCH = 8
SEQ = 8192
DEPTH = 1

MEM_LEN = 256
EPS = 1e-6
CONV_K = 4
GDN_HEADS = 8
GDN_DK = 128
GDN_DV = 128
GDN_DIM = GDN_HEADS * GDN_DV
GDN_CHUNK = 64
GDN_COLS = 4 * GDN_DIM + 2 * GDN_HEADS
SSM_DIM = D_MODEL
SSM_HEADDIM = 64
SSM_HEADS = SSM_DIM // SSM_HEADDIM
SSM_GROUPS = 2
SSM_HPG = SSM_HEADS // SSM_GROUPS
SSM_STATE = 128
SSM_CHUNK = 128
SSM_BC = SSM_GROUPS * SSM_STATE
SSM_COLS = 2 * SSM_DIM + 2 * SSM_BC + SSM_HEADS
MIX_DIM = GDN_DIM + SSM_DIM
IN_COLS = GDN_COLS + SSM_COLS
MEM_HEADS = 4
MEM_HEADDIM = D_MODEL // MEM_HEADS
D_FF = 4 * D_MODEL

kernel_name = "hybrid_gdn_ssd_parallel_heads_memxattn"


def rmsnorm(x, w):
    xf = x.astype(jnp.float32)
    y = xf * lax.rsqrt(jnp.mean(xf * xf, axis=-1, keepdims=True) + EPS)
    return (y * w.astype(jnp.float32)).astype(x.dtype)


def l2norm(x):
    return x * lax.rsqrt(jnp.sum(x * x, axis=-1, keepdims=True) + EPS)


def causal_depthwise_conv(x, w, b=None):
    K, C = w.shape
    xp = jnp.pad(x, ((0, 0), (K - 1, 0), (0, 0)))
    y = lax.conv_general_dilated(xp, w[:, None, :].astype(x.dtype), window_strides=(1,),
                                 padding='VALID', dimension_numbers=('NWC', 'WIO', 'NWC'),
                                 feature_group_count=C)
    if b is not None:
        y = y + b.astype(x.dtype)
    return y


def gated_delta_rule_chunked(q, k, v, g, beta):
    Bsz, T, H, Dk = q.shape
    Dv = v.shape[-1]
    C = GDN_CHUNK
    N = T // C
    q = l2norm(q) * (Dk ** -0.5)
    k = l2norm(k)

    def chunks(t):
        return t.reshape(Bsz, N, C, H, -1).transpose(0, 3, 1, 2, 4)

    q, k, v = chunks(q), chunks(k), chunks(v)
    g = g.reshape(Bsz, N, C, H).transpose(0, 3, 1, 2)
    beta = beta.reshape(Bsz, N, C, H).transpose(0, 3, 1, 2)
    g_cs = jnp.cumsum(g, axis=-1)
    causal = jnp.tril(jnp.ones((C, C), dtype=bool))
    strict = jnp.tril(jnp.ones((C, C), dtype=bool), -1)
    decay = jnp.exp(jnp.where(causal, g_cs[..., :, None] - g_cs[..., None, :], -jnp.inf))
    k_beta = k * beta[..., None]
    v_beta = v * beta[..., None]
    Lmat = jnp.where(strict, jnp.einsum('bhncd,bhnsd->bhncs', k_beta, k) * decay, 0.0)
    Amat = Lmat + jnp.eye(C, dtype=jnp.float32)
    rhs = jnp.concatenate([v_beta, k_beta * jnp.exp(g_cs)[..., None]], axis=-1)
    sol = lax.linalg.triangular_solve(Amat, rhs, left_side=True, lower=True, unit_diagonal=True)
    u = sol[..., :Dv]
    w = sol[..., Dv:]
    attn_intra = jnp.einsum('bhncd,bhnsd->bhncs', q, k) * decay
    q_dec = q * jnp.exp(g_cs)[..., None]
    k_dec = k * jnp.exp(g_cs[..., -1:] - g_cs)[..., None]
    chunk_decay = jnp.exp(g_cs[..., -1])

    def step(S, inp):
        qd, kd, u_c, w_c, a_c, cd = inp
        v_new = u_c - jnp.einsum('bhcd,bhde->bhce', w_c, S)
        o = jnp.einsum('bhcd,bhde->bhce', qd, S) + jnp.einsum('bhcs,bhse->bhce', a_c, v_new)
        S = S * cd[..., None, None] + jnp.einsum('bhcd,bhce->bhde', kd, v_new)
        return S, o

    mv = lambda t: jnp.moveaxis(t, 2, 0)
    S0 = jnp.zeros((Bsz, H, Dk, Dv), jnp.float32)
    _, o = lax.scan(step, S0, (mv(q_dec), mv(k_dec), mv(u), mv(w), mv(attn_intra), mv(chunk_decay)))
    return o.transpose(1, 0, 3, 2, 4).reshape(Bsz, T, H, Dv)


def gdn_mixer(p, conv_w, a_log, dt_bias, norm_w):
    Bsz, T, _ = p.shape
    qkv = jax.nn.silu(causal_depthwise_conv(p[..., :3 * GDN_DIM], conv_w))
    z = p[..., 3 * GDN_DIM:4 * GDN_DIM]
    b_raw = p[..., 4 * GDN_DIM:4 * GDN_DIM + GDN_HEADS].astype(jnp.float32)
    a_raw = p[..., 4 * GDN_DIM + GDN_HEADS:].astype(jnp.float32)
    qkv = qkv.astype(jnp.float32)
    q = qkv[..., :GDN_DIM].reshape(Bsz, T, GDN_HEADS, GDN_DK)
    k = qkv[..., GDN_DIM:2 * GDN_DIM].reshape(Bsz, T, GDN_HEADS, GDN_DK)
    v = qkv[..., 2 * GDN_DIM:].reshape(Bsz, T, GDN_HEADS, GDN_DV)
    beta = jax.nn.sigmoid(b_raw)
    g = -jnp.exp(a_log.astype(jnp.float32)) * jax.nn.softplus(a_raw + dt_bias.astype(jnp.float32))
    o = gated_delta_rule_chunked(q, k, v, g, beta)
    zf = z.astype(jnp.float32).reshape(Bsz, T, GDN_HEADS, GDN_DV)
    o = rmsnorm(o, norm_w) * jax.nn.silu(zf)
    return o.reshape(Bsz, T, GDN_DIM).astype(p.dtype)


def ssd_chunked(x, dt, A, Bm, Cm):
    Bsz, T, H, P = x.shape
    L = SSD_L = SSM_CHUNK
    Nc = T // L
    G, J, N = SSM_GROUPS, SSM_HPG, SSM_STATE
    xdt = (x * dt[..., None]).reshape(Bsz, Nc, L, G, J, P)
    a = (dt * A).reshape(Bsz, Nc, L, H).transpose(0, 3, 1, 2)
    Bc = Bm.reshape(Bsz, Nc, L, G, N)
    Cc = Cm.reshape(Bsz, Nc, L, G, N)
    a_cs = jnp.cumsum(a, axis=-1)
    causal = jnp.tril(jnp.ones((L, L), dtype=bool))
    seg = jnp.exp(jnp.where(causal, a_cs[..., :, None] - a_cs[..., None, :], -jnp.inf))
    CB = jnp.einsum('bclgn,bcsgn->bgcls', Cc, Bc)
    scores = seg.reshape(Bsz, G, J, Nc, L, L) * CB[:, :, None]
    y_diag = jnp.einsum('bgjcls,bcsgjp->bclgjp', scores, xdt)
    decay_states = jnp.exp(a_cs[..., -1:] - a_cs).reshape(Bsz, G, J, Nc, L)
    states = jnp.einsum('bclgn,bgjcl,bclgjp->bcgjpn', Bc, decay_states, xdt)
    chunk_decay = jnp.exp(a_cs[..., -1]).reshape(Bsz, G, J, Nc)

    def step(h, inp):
        st, dec = inp
        return h * dec[..., None, None] + st, h

    h0 = jnp.zeros((Bsz, G, J, P, N), jnp.float32)
    _, prev = lax.scan(step, h0, (jnp.moveaxis(states, 1, 0), jnp.moveaxis(chunk_decay, 3, 0)))
    prev = jnp.moveaxis(prev, 0, 1)
    y_off = jnp.einsum('bclgn,bcgjpn,bgjcl->bclgjp', Cc, prev,
                       jnp.exp(a_cs).reshape(Bsz, G, J, Nc, L))
    return (y_diag + y_off).reshape(Bsz, T, H, P)


def ssd_mixer(p, conv_w, conv_b, a_log, dt_bias, d_skip, norm_w):
    Bsz, T, _ = p.shape
    z = p[..., :SSM_DIM]
    xbc = jax.nn.silu(causal_depthwise_conv(p[..., SSM_DIM:2 * SSM_DIM + 2 * SSM_BC], conv_w, conv_b))
    dt_raw = p[..., 2 * SSM_DIM + 2 * SSM_BC:].astype(jnp.float32)
    xbc = xbc.astype(jnp.float32)
    xs = xbc[..., :SSM_DIM].reshape(Bsz, T, SSM_HEADS, SSM_HEADDIM)
    Bm = xbc[..., SSM_DIM:SSM_DIM + SSM_BC].reshape(Bsz, T, SSM_GROUPS, SSM_STATE)
    Cm = xbc[..., SSM_DIM + SSM_BC:].reshape(Bsz, T, SSM_GROUPS, SSM_STATE)
    dt = jax.nn.softplus(dt_raw + dt_bias.astype(jnp.float32))
    A = -jnp.exp(a_log.astype(jnp.float32))
    y = ssd_chunked(xs, dt, A, Bm, Cm) + xs * d_skip.astype(jnp.float32)[:, None]
    yg = (y.reshape(Bsz, T, SSM_DIM) * jax.nn.silu(z.astype(jnp.float32)))
    yg = yg.reshape(Bsz, T, SSM_GROUPS, SSM_DIM // SSM_GROUPS)
    yg = yg * lax.rsqrt(jnp.mean(yg * yg, axis=-1, keepdims=True) + EPS)
    yg = yg.reshape(Bsz, T, SSM_DIM) * norm_w.astype(jnp.float32)
    return yg.astype(p.dtype)


def memory_cross_attention(h, m, wq, wk, wv, wo):
    Bsz, T, _ = h.shape
    M = m.shape[1]
    q = (h @ wq).reshape(Bsz, T, MEM_HEADS, MEM_HEADDIM)
    k = (m @ wk).reshape(Bsz, M, MEM_HEADS, MEM_HEADDIM)
    v = (m @ wv).reshape(Bsz, M, MEM_HEADS, MEM_HEADDIM)
    s = jnp.einsum('bthd,bmhd->bhtm', q, k).astype(jnp.float32) * (MEM_HEADDIM ** -0.5)
    pr = jax.nn.softmax(s, axis=-1).astype(v.dtype)
    o = jnp.einsum('bhtm,bmhd->bthd', pr, v).reshape(Bsz, T, D_MODEL)
    return o @ wo


def _inv_softplus_dt(key, n):
    dt = jnp.exp(jax.random.uniform(key, (n,), minval=math.log(1e-3), maxval=math.log(1e-1)))
    return dt + jnp.log(-jnp.expm1(-dt))


def _fwd_setup_inputs(seed: int = 0) -> dict:
    key = jax.random.key(seed)
    ks = jax.random.split(key, 26)
    nrm = lambda k, shape, s: jax.random.normal(k, shape, jnp.float32) * s
    gain = lambda k, n: 1.0 + 0.02 * jax.random.normal(k, (n,), jnp.float32)
    return {
        "x": nrm(ks[0], (BATCH, SEQ, D_MODEL), 1.0),
        "mem": nrm(ks[1], (BATCH, MEM_LEN, D_MODEL), 1.0),
        "norm1_w": gain(ks[2], D_MODEL),
        "w_in": nrm(ks[3], (D_MODEL, IN_COLS), D_MODEL ** -0.5),
        "gdn_conv_w": nrm(ks[4], (CONV_K, 3 * GDN_DIM), CONV_K ** -0.5),
        "gdn_a_log": jnp.log(jax.random.uniform(ks[5], (GDN_HEADS,), minval=1.0, maxval=16.0)),
        "gdn_dt_bias": _inv_softplus_dt(ks[6], GDN_HEADS),
        "gdn_norm_w": gain(ks[7], GDN_DV),
        "ssm_conv_w": nrm(ks[8], (CONV_K, SSM_DIM + 2 * SSM_BC), CONV_K ** -0.5),
        "ssm_conv_b": nrm(ks[9], (SSM_DIM + 2 * SSM_BC,), 0.02),
        "ssm_a_log": jnp.log(jax.random.uniform(ks[10], (SSM_HEADS,), minval=1.0, maxval=16.0)),
        "ssm_dt_bias": _inv_softplus_dt(ks[11], SSM_HEADS),
        "ssm_d": gain(ks[12], SSM_HEADS),
        "ssm_norm_w": gain(ks[13], SSM_DIM),
        "w_out": nrm(ks[14], (MIX_DIM, D_MODEL), MIX_DIM ** -0.5),
        "norm2_w": gain(ks[15], D_MODEL),
        "mem_norm_w": gain(ks[16], D_MODEL),
        "wq_mem": nrm(ks[17], (D_MODEL, D_MODEL), D_MODEL ** -0.5),
        "wk_mem": nrm(ks[18], (D_MODEL, D_MODEL), D_MODEL ** -0.5),
        "wv_mem": nrm(ks[19], (D_MODEL, D_MODEL), D_MODEL ** -0.5),
        "wo_mem": nrm(ks[20], (D_MODEL, D_MODEL), D_MODEL ** -0.5),
        "norm3_w": gain(ks[21], D_MODEL),
        "w_up": nrm(ks[22], (D_MODEL, D_FF), D_MODEL ** -0.5),
        "w_down": nrm(ks[23], (D_FF, D_MODEL), D_FF ** -0.5),
        "final_norm_w": gain(ks[24], D_MODEL),
    }


def _fwd_reference(x, mem, norm1_w, w_in, gdn_conv_w, gdn_a_log, gdn_dt_bias, gdn_norm_w,
              ssm_conv_w, ssm_conv_b, ssm_a_log, ssm_dt_bias, ssm_d, ssm_norm_w, w_out,
              norm2_w, mem_norm_w, wq_mem, wk_mem, wv_mem, wo_mem, norm3_w, w_up, w_down,
              final_norm_w):
    m = rmsnorm(mem, mem_norm_w)
    for _ in range(DEPTH):
        h = rmsnorm(x, norm1_w)
        p = h @ w_in
        o_a = gdn_mixer(p[..., :GDN_COLS], gdn_conv_w, gdn_a_log, gdn_dt_bias, gdn_norm_w)
        o_b = ssd_mixer(p[..., GDN_COLS:], ssm_conv_w, ssm_conv_b, ssm_a_log, ssm_dt_bias,
                        ssm_d, ssm_norm_w)
        x = x + jnp.concatenate([o_a, o_b], axis=-1) @ w_out
        x = x + memory_cross_attention(rmsnorm(x, norm2_w), m, wq_mem, wk_mem, wv_mem, wo_mem)
        u = jax.nn.relu(rmsnorm(x, norm3_w) @ w_up)
        x = x + (u * u) @ w_down
    return rmsnorm(x, final_norm_w)


import jax as _jax
import jax.numpy as _jnp

TWIN_FORMAT = 'train_step'
FWD_PARAMS = ['x', 'mem', 'norm1_w', 'w_in', 'gdn_conv_w', 'gdn_a_log', 'gdn_dt_bias', 'gdn_norm_w', 'ssm_conv_w', 'ssm_conv_b', 'ssm_a_log', 'ssm_dt_bias', 'ssm_d', 'ssm_norm_w', 'w_out', 'norm2_w', 'mem_norm_w', 'wq_mem', 'wk_mem', 'wv_mem', 'wo_mem', 'norm3_w', 'w_up', 'w_down', 'final_norm_w']
TWIN_WEIGHTS = ['norm1_w', 'w_in', 'gdn_conv_w', 'gdn_a_log', 'gdn_dt_bias', 'gdn_norm_w', 'ssm_conv_w', 'ssm_conv_b', 'ssm_a_log', 'ssm_dt_bias', 'ssm_d', 'ssm_norm_w', 'w_out', 'norm2_w', 'mem_norm_w', 'wq_mem', 'wk_mem', 'wv_mem', 'wo_mem', 'norm3_w', 'w_up', 'w_down', 'final_norm_w']
TWIN_DIFF_INPUT = 'x'
TWIN_INPUTS = ['x', 'mem', 'norm1_w', 'w_in', 'gdn_conv_w', 'gdn_a_log', 'gdn_dt_bias', 'gdn_norm_w', 'ssm_conv_w', 'ssm_conv_b', 'ssm_a_log', 'ssm_dt_bias', 'ssm_d', 'ssm_norm_w', 'w_out', 'norm2_w', 'mem_norm_w', 'wq_mem', 'wk_mem', 'wv_mem', 'wo_mem', 'norm3_w', 'w_up', 'w_down', 'final_norm_w', 'loss_target', 'm_norm1_w', 'm_w_in', 'm_gdn_conv_w', 'm_gdn_a_log', 'm_gdn_dt_bias', 'm_gdn_norm_w', 'm_ssm_conv_w', 'm_ssm_conv_b', 'm_ssm_a_log', 'm_ssm_dt_bias', 'm_ssm_d', 'm_ssm_norm_w', 'm_w_out', 'm_norm2_w', 'm_mem_norm_w', 'm_wq_mem', 'm_wk_mem', 'm_wv_mem', 'm_wo_mem', 'm_norm3_w', 'm_w_up', 'm_w_down', 'm_final_norm_w', 'v_norm1_w', 'v_w_in', 'v_gdn_conv_w', 'v_gdn_a_log', 'v_gdn_dt_bias', 'v_gdn_norm_w', 'v_ssm_conv_w', 'v_ssm_conv_b', 'v_ssm_a_log', 'v_ssm_dt_bias', 'v_ssm_d', 'v_ssm_norm_w', 'v_w_out', 'v_norm2_w', 'v_mem_norm_w', 'v_wq_mem', 'v_wk_mem', 'v_wv_mem', 'v_wo_mem', 'v_norm3_w', 'v_w_up', 'v_w_down', 'v_final_norm_w']
TWIN_OUTPUTS = ['loss', 'grad_x', 'grad_norm1_w', 'grad_w_in', 'grad_gdn_conv_w', 'grad_gdn_a_log', 'grad_gdn_dt_bias', 'grad_gdn_norm_w', 'grad_ssm_conv_w', 'grad_ssm_conv_b', 'grad_ssm_a_log', 'grad_ssm_dt_bias', 'grad_ssm_d', 'grad_ssm_norm_w', 'grad_w_out', 'grad_norm2_w', 'grad_mem_norm_w', 'grad_wq_mem', 'grad_wk_mem', 'grad_wv_mem', 'grad_wo_mem', 'grad_norm3_w', 'grad_w_up', 'grad_w_down', 'grad_final_norm_w', 'delta_norm1_w', 'delta_w_in', 'delta_gdn_conv_w', 'delta_gdn_a_log', 'delta_gdn_dt_bias', 'delta_gdn_norm_w', 'delta_ssm_conv_w', 'delta_ssm_conv_b', 'delta_ssm_a_log', 'delta_ssm_dt_bias', 'delta_ssm_d', 'delta_ssm_norm_w', 'delta_w_out', 'delta_norm2_w', 'delta_mem_norm_w', 'delta_wq_mem', 'delta_wk_mem', 'delta_wv_mem', 'delta_wo_mem', 'delta_norm3_w', 'delta_w_up', 'delta_w_down', 'delta_final_norm_w', 'new_m_norm1_w', 'new_m_w_in', 'new_m_gdn_conv_w', 'new_m_gdn_a_log', 'new_m_gdn_dt_bias', 'new_m_gdn_norm_w', 'new_m_ssm_conv_w', 'new_m_ssm_conv_b', 'new_m_ssm_a_log', 'new_m_ssm_dt_bias', 'new_m_ssm_d', 'new_m_ssm_norm_w', 'new_m_w_out', 'new_m_norm2_w', 'new_m_mem_norm_w', 'new_m_wq_mem', 'new_m_wk_mem', 'new_m_wv_mem', 'new_m_wo_mem', 'new_m_norm3_w', 'new_m_w_up', 'new_m_w_down', 'new_m_final_norm_w', 'new_v_norm1_w', 'new_v_w_in', 'new_v_gdn_conv_w', 'new_v_gdn_a_log', 'new_v_gdn_dt_bias', 'new_v_gdn_norm_w', 'new_v_ssm_conv_w', 'new_v_ssm_conv_b', 'new_v_ssm_a_log', 'new_v_ssm_dt_bias', 'new_v_ssm_d', 'new_v_ssm_norm_w', 'new_v_w_out', 'new_v_norm2_w', 'new_v_mem_norm_w', 'new_v_wq_mem', 'new_v_wk_mem', 'new_v_wv_mem', 'new_v_wo_mem', 'new_v_norm3_w', 'new_v_w_up', 'new_v_w_down', 'new_v_final_norm_w']
TWIN_LEAF_KINDS = {'loss': 'loss', 'grad_x': 'grad_x', 'grad_norm1_w': 'grad_w', 'grad_w_in': 'grad_w', 'grad_gdn_conv_w': 'grad_w', 'grad_gdn_a_log': 'grad_w', 'grad_gdn_dt_bias': 'grad_w', 'grad_gdn_norm_w': 'grad_w', 'grad_ssm_conv_w': 'grad_w', 'grad_ssm_conv_b': 'grad_w', 'grad_ssm_a_log': 'grad_w', 'grad_ssm_dt_bias': 'grad_w', 'grad_ssm_d': 'grad_w', 'grad_ssm_norm_w': 'grad_w', 'grad_w_out': 'grad_w', 'grad_norm2_w': 'grad_w', 'grad_mem_norm_w': 'grad_w', 'grad_wq_mem': 'grad_w', 'grad_wk_mem': 'grad_w', 'grad_wv_mem': 'grad_w', 'grad_wo_mem': 'grad_w', 'grad_norm3_w': 'grad_w', 'grad_w_up': 'grad_w', 'grad_w_down': 'grad_w', 'grad_final_norm_w': 'grad_w', 'delta_norm1_w': 'delta_w', 'delta_w_in': 'delta_w', 'delta_gdn_conv_w': 'delta_w', 'delta_gdn_a_log': 'delta_w', 'delta_gdn_dt_bias': 'delta_w', 'delta_gdn_norm_w': 'delta_w', 'delta_ssm_conv_w': 'delta_w', 'delta_ssm_conv_b': 'delta_w', 'delta_ssm_a_log': 'delta_w', 'delta_ssm_dt_bias': 'delta_w', 'delta_ssm_d': 'delta_w', 'delta_ssm_norm_w': 'delta_w', 'delta_w_out': 'delta_w', 'delta_norm2_w': 'delta_w', 'delta_mem_norm_w': 'delta_w', 'delta_wq_mem': 'delta_w', 'delta_wk_mem': 'delta_w', 'delta_wv_mem': 'delta_w', 'delta_wo_mem': 'delta_w', 'delta_norm3_w': 'delta_w', 'delta_w_up': 'delta_w', 'delta_w_down': 'delta_w', 'delta_final_norm_w': 'delta_w', 'new_m_norm1_w': 'new_m', 'new_m_w_in': 'new_m', 'new_m_gdn_conv_w': 'new_m', 'new_m_gdn_a_log': 'new_m', 'new_m_gdn_dt_bias': 'new_m', 'new_m_gdn_norm_w': 'new_m', 'new_m_ssm_conv_w': 'new_m', 'new_m_ssm_conv_b': 'new_m', 'new_m_ssm_a_log': 'new_m', 'new_m_ssm_dt_bias': 'new_m', 'new_m_ssm_d': 'new_m', 'new_m_ssm_norm_w': 'new_m', 'new_m_w_out': 'new_m', 'new_m_norm2_w': 'new_m', 'new_m_mem_norm_w': 'new_m', 'new_m_wq_mem': 'new_m', 'new_m_wk_mem': 'new_m', 'new_m_wv_mem': 'new_m', 'new_m_wo_mem': 'new_m', 'new_m_norm3_w': 'new_m', 'new_m_w_up': 'new_m', 'new_m_w_down': 'new_m', 'new_m_final_norm_w': 'new_m', 'new_v_norm1_w': 'new_v', 'new_v_w_in': 'new_v', 'new_v_gdn_conv_w': 'new_v', 'new_v_gdn_a_log': 'new_v', 'new_v_gdn_dt_bias': 'new_v', 'new_v_gdn_norm_w': 'new_v', 'new_v_ssm_conv_w': 'new_v', 'new_v_ssm_conv_b': 'new_v', 'new_v_ssm_a_log': 'new_v', 'new_v_ssm_dt_bias': 'new_v', 'new_v_ssm_d': 'new_v', 'new_v_ssm_norm_w': 'new_v', 'new_v_w_out': 'new_v', 'new_v_norm2_w': 'new_v', 'new_v_mem_norm_w': 'new_v', 'new_v_wq_mem': 'new_v', 'new_v_wk_mem': 'new_v', 'new_v_wv_mem': 'new_v', 'new_v_wo_mem': 'new_v', 'new_v_norm3_w': 'new_v', 'new_v_w_up': 'new_v', 'new_v_w_down': 'new_v', 'new_v_final_norm_w': 'new_v'}


def _forward(args):
    return _fwd_reference(*[args[k] for k in FWD_PARAMS])


def _output_shape():
    def fwd():
        inp = _fwd_setup_inputs(0)
        return _fwd_reference(*[inp[k] for k in FWD_PARAMS])
    out = _jax.eval_shape(fwd)
    return out.shape, out.dtype

N_MICROBATCH = 1
ADAM_LR = 0.001
ADAM_B1 = 0.9
ADAM_B2 = 0.999
ADAM_EPS = 1e-08
ADAM_WD = 0.01
ADAM_STEP = 10
PER_EXAMPLE_BATCH_AXIS = {'x': 0, 'mem': 0, 'loss_target': 0}
SHARED_INPUTS = []
_WEIGHT_DTYPES = {'norm1_w': _jnp.float32, 'w_in': _jnp.float32, 'gdn_conv_w': _jnp.float32, 'gdn_a_log': _jnp.float32, 'gdn_dt_bias': _jnp.float32, 'gdn_norm_w': _jnp.float32, 'ssm_conv_w': _jnp.float32, 'ssm_conv_b': _jnp.float32, 'ssm_a_log': _jnp.float32, 'ssm_dt_bias': _jnp.float32, 'ssm_d': _jnp.float32, 'ssm_norm_w': _jnp.float32, 'w_out': _jnp.float32, 'norm2_w': _jnp.float32, 'mem_norm_w': _jnp.float32, 'wq_mem': _jnp.float32, 'wk_mem': _jnp.float32, 'wv_mem': _jnp.float32, 'wo_mem': _jnp.float32, 'norm3_w': _jnp.float32, 'w_up': _jnp.float32, 'w_down': _jnp.float32, 'final_norm_w': _jnp.float32}
MOMENT_SCALE = {'norm1_w': 2.751816e-01, 'w_in': 1.048180e-01, 'gdn_conv_w': 6.775263e-02, 'gdn_a_log': 2.906476e-01, 'gdn_dt_bias': 2.695700e-01, 'gdn_norm_w': 2.633383e-01, 'ssm_conv_w': 1.320642e-01, 'ssm_conv_b': 1.707115e-01, 'ssm_a_log': 3.899132e-01, 'ssm_dt_bias': 4.447645e-01, 'ssm_d': 8.277191e-01, 'ssm_norm_w': 1.520551e-01, 'w_out': 1.756489e-01, 'norm2_w': 2.204271e-02, 'mem_norm_w': 3.292713e-02, 'wq_mem': 2.222308e-02, 'wk_mem': 2.213462e-02, 'wv_mem': 2.273970e-02, 'wo_mem': 2.292845e-02, 'norm3_w': 2.143544e-01, 'w_up': 1.006638e-01, 'w_down': 2.036066e-01, 'final_norm_w': 6.443864e+01}


def _to_microbatches(a, axis):
    t = _jnp.moveaxis(a, axis, 0)
    t = t.reshape((N_MICROBATCH, t.shape[0] // N_MICROBATCH) + t.shape[1:])
    return _jnp.moveaxis(t, 1, axis + 1)


def setup_inputs(seed: int = 0) -> dict:
    inp = _fwd_setup_inputs(seed)
    key = _jax.random.fold_in(_jax.random.key(seed), 7919)
    shape, _ = _output_shape()
    out = dict(inp)
    out["loss_target"] = _jax.random.normal(_jax.random.fold_in(key, 0), shape, _jnp.float32)
    for i, name in enumerate(TWIN_WEIGHTS):
        w = inp[name].astype(_jnp.float32)
        if MOMENT_SCALE is None:
            s = _jnp.sqrt(_jnp.mean(_jnp.square(w)) + 1e-30)
        else:
            s = MOMENT_SCALE[name]
        km, kv = _jax.random.split(_jax.random.fold_in(key, i + 1))
        out[name] = w
        out["m_" + name] = s * _jax.random.normal(km, w.shape, _jnp.float32)
        out["v_" + name] = (s * s) * _jax.random.uniform(kv, w.shape, _jnp.float32, 0.5, 1.5)
    if N_MICROBATCH > 1:
        for name, axis in PER_EXAMPLE_BATCH_AXIS.items():
            out[name] = _to_microbatches(out[name], axis)
    return {'x': out['x'], 'mem': out['mem'], 'norm1_w': out['norm1_w'], 'w_in': out['w_in'], 'gdn_conv_w': out['gdn_conv_w'], 'gdn_a_log': out['gdn_a_log'], 'gdn_dt_bias': out['gdn_dt_bias'], 'gdn_norm_w': out['gdn_norm_w'], 'ssm_conv_w': out['ssm_conv_w'], 'ssm_conv_b': out['ssm_conv_b'], 'ssm_a_log': out['ssm_a_log'], 'ssm_dt_bias': out['ssm_dt_bias'], 'ssm_d': out['ssm_d'], 'ssm_norm_w': out['ssm_norm_w'], 'w_out': out['w_out'], 'norm2_w': out['norm2_w'], 'mem_norm_w': out['mem_norm_w'], 'wq_mem': out['wq_mem'], 'wk_mem': out['wk_mem'], 'wv_mem': out['wv_mem'], 'wo_mem': out['wo_mem'], 'norm3_w': out['norm3_w'], 'w_up': out['w_up'], 'w_down': out['w_down'], 'final_norm_w': out['final_norm_w'], 'loss_target': out['loss_target'], 'm_norm1_w': out['m_norm1_w'], 'm_w_in': out['m_w_in'], 'm_gdn_conv_w': out['m_gdn_conv_w'], 'm_gdn_a_log': out['m_gdn_a_log'], 'm_gdn_dt_bias': out['m_gdn_dt_bias'], 'm_gdn_norm_w': out['m_gdn_norm_w'], 'm_ssm_conv_w': out['m_ssm_conv_w'], 'm_ssm_conv_b': out['m_ssm_conv_b'], 'm_ssm_a_log': out['m_ssm_a_log'], 'm_ssm_dt_bias': out['m_ssm_dt_bias'], 'm_ssm_d': out['m_ssm_d'], 'm_ssm_norm_w': out['m_ssm_norm_w'], 'm_w_out': out['m_w_out'], 'm_norm2_w': out['m_norm2_w'], 'm_mem_norm_w': out['m_mem_norm_w'], 'm_wq_mem': out['m_wq_mem'], 'm_wk_mem': out['m_wk_mem'], 'm_wv_mem': out['m_wv_mem'], 'm_wo_mem': out['m_wo_mem'], 'm_norm3_w': out['m_norm3_w'], 'm_w_up': out['m_w_up'], 'm_w_down': out['m_w_down'], 'm_final_norm_w': out['m_final_norm_w'], 'v_norm1_w': out['v_norm1_w'], 'v_w_in': out['v_w_in'], 'v_gdn_conv_w': out['v_gdn_conv_w'], 'v_gdn_a_log': out['v_gdn_a_log'], 'v_gdn_dt_bias': out['v_gdn_dt_bias'], 'v_gdn_norm_w': out['v_gdn_norm_w'], 'v_ssm_conv_w': out['v_ssm_conv_w'], 'v_ssm_conv_b': out['v_ssm_conv_b'], 'v_ssm_a_log': out['v_ssm_a_log'], 'v_ssm_dt_bias': out['v_ssm_dt_bias'], 'v_ssm_d': out['v_ssm_d'], 'v_ssm_norm_w': out['v_ssm_norm_w'], 'v_w_out': out['v_w_out'], 'v_norm2_w': out['v_norm2_w'], 'v_mem_norm_w': out['v_mem_norm_w'], 'v_wq_mem': out['v_wq_mem'], 'v_wk_mem': out['v_wk_mem'], 'v_wv_mem': out['v_wv_mem'], 'v_wo_mem': out['v_wo_mem'], 'v_norm3_w': out['v_norm3_w'], 'v_w_up': out['v_w_up'], 'v_w_down': out['v_w_down'], 'v_final_norm_w': out['v_final_norm_w']}


def _loss(weights, diff, rest, loss_target):
    with _jax.named_scope("forward"):
        args = {**rest, TWIN_DIFF_INPUT: diff, **{k: w.astype(_WEIGHT_DTYPES[k]) for k, w in weights.items()}}
        y = _forward(args)
    with _jax.named_scope("loss_head"):
        err = _jnp.square(y.astype(_jnp.float32) - loss_target)
        return 0.5 * _jnp.sum(_jnp.mean(err, axis=-1)) if err.ndim else 0.5 * err


def _adamw(w, g, m, v):
    m = ADAM_B1 * m + (1.0 - ADAM_B1) * g
    v = ADAM_B2 * v + (1.0 - ADAM_B2) * _jnp.square(g)
    m_hat = m / (1.0 - ADAM_B1 ** ADAM_STEP)
    v_hat = v / (1.0 - ADAM_B2 ** ADAM_STEP)
    delta = -ADAM_LR * (m_hat / (_jnp.sqrt(v_hat) + ADAM_EPS) + ADAM_WD * w)
    return delta, m, v


def reference(x, mem, norm1_w, w_in, gdn_conv_w, gdn_a_log, gdn_dt_bias, gdn_norm_w, ssm_conv_w, ssm_conv_b, ssm_a_log, ssm_dt_bias, ssm_d, ssm_norm_w, w_out, norm2_w, mem_norm_w, wq_mem, wk_mem, wv_mem, wo_mem, norm3_w, w_up, w_down, final_norm_w, loss_target, m_norm1_w, m_w_in, m_gdn_conv_w, m_gdn_a_log, m_gdn_dt_bias, m_gdn_norm_w, m_ssm_conv_w, m_ssm_conv_b, m_ssm_a_log, m_ssm_dt_bias, m_ssm_d, m_ssm_norm_w, m_w_out, m_norm2_w, m_mem_norm_w, m_wq_mem, m_wk_mem, m_wv_mem, m_wo_mem, m_norm3_w, m_w_up, m_w_down, m_final_norm_w, v_norm1_w, v_w_in, v_gdn_conv_w, v_gdn_a_log, v_gdn_dt_bias, v_gdn_norm_w, v_ssm_conv_w, v_ssm_conv_b, v_ssm_a_log, v_ssm_dt_bias, v_ssm_d, v_ssm_norm_w, v_w_out, v_norm2_w, v_mem_norm_w, v_wq_mem, v_wk_mem, v_wv_mem, v_wo_mem, v_norm3_w, v_w_up, v_w_down, v_final_norm_w):
    given = dict(x=x, mem=mem, norm1_w=norm1_w, w_in=w_in, gdn_conv_w=gdn_conv_w, gdn_a_log=gdn_a_log, gdn_dt_bias=gdn_dt_bias, gdn_norm_w=gdn_norm_w, ssm_conv_w=ssm_conv_w, ssm_conv_b=ssm_conv_b, ssm_a_log=ssm_a_log, ssm_dt_bias=ssm_dt_bias, ssm_d=ssm_d, ssm_norm_w=ssm_norm_w, w_out=w_out, norm2_w=norm2_w, mem_norm_w=mem_norm_w, wq_mem=wq_mem, wk_mem=wk_mem, wv_mem=wv_mem, wo_mem=wo_mem, norm3_w=norm3_w, w_up=w_up, w_down=w_down, final_norm_w=final_norm_w, loss_target=loss_target, m_norm1_w=m_norm1_w, m_w_in=m_w_in, m_gdn_conv_w=m_gdn_conv_w, m_gdn_a_log=m_gdn_a_log, m_gdn_dt_bias=m_gdn_dt_bias, m_gdn_norm_w=m_gdn_norm_w, m_ssm_conv_w=m_ssm_conv_w, m_ssm_conv_b=m_ssm_conv_b, m_ssm_a_log=m_ssm_a_log, m_ssm_dt_bias=m_ssm_dt_bias, m_ssm_d=m_ssm_d, m_ssm_norm_w=m_ssm_norm_w, m_w_out=m_w_out, m_norm2_w=m_norm2_w, m_mem_norm_w=m_mem_norm_w, m_wq_mem=m_wq_mem, m_wk_mem=m_wk_mem, m_wv_mem=m_wv_mem, m_wo_mem=m_wo_mem, m_norm3_w=m_norm3_w, m_w_up=m_w_up, m_w_down=m_w_down, m_final_norm_w=m_final_norm_w, v_norm1_w=v_norm1_w, v_w_in=v_w_in, v_gdn_conv_w=v_gdn_conv_w, v_gdn_a_log=v_gdn_a_log, v_gdn_dt_bias=v_gdn_dt_bias, v_gdn_norm_w=v_gdn_norm_w, v_ssm_conv_w=v_ssm_conv_w, v_ssm_conv_b=v_ssm_conv_b, v_ssm_a_log=v_ssm_a_log, v_ssm_dt_bias=v_ssm_dt_bias, v_ssm_d=v_ssm_d, v_ssm_norm_w=v_ssm_norm_w, v_w_out=v_w_out, v_norm2_w=v_norm2_w, v_mem_norm_w=v_mem_norm_w, v_wq_mem=v_wq_mem, v_wk_mem=v_wk_mem, v_wv_mem=v_wv_mem, v_wo_mem=v_wo_mem, v_norm3_w=v_norm3_w, v_w_up=v_w_up, v_w_down=v_w_down, v_final_norm_w=v_final_norm_w)
    weights = {n: given[n] for n in TWIN_WEIGHTS}
    shared = {n: given[n] for n in SHARED_INPUTS}
    per_example = {n: given[n] for n in ['x', 'mem']}
    grad_fn = _jax.value_and_grad(_loss, argnums=(0, 1))

    def one_microbatch(ex, loss_target):
        ex = dict(ex)
        diff = ex.pop(TWIN_DIFF_INPUT)
        return grad_fn(weights, diff, {**shared, **ex}, loss_target)

    if N_MICROBATCH == 1:
        loss, (grad_w, grad_x) = one_microbatch(per_example, given["loss_target"])
    else:
        def body(carry, xs):
            loss_sum, grad_sum = carry
            l_k, (gw_k, gx_k) = one_microbatch(xs[0], xs[1])
            with _jax.named_scope("update"):
                return (loss_sum + l_k, _jax.tree.map(_jnp.add, grad_sum, gw_k)), gx_k

        init = (_jnp.zeros((), _jnp.float32), _jax.tree.map(_jnp.zeros_like, weights))
        (loss, grad_w), grad_x = _jax.lax.scan(body, init, (per_example, given["loss_target"]))
    with _jax.named_scope("update"):
        delta_w, new_m, new_v = {}, {}, {}
        for n in TWIN_WEIGHTS:
            delta_w[n], new_m[n], new_v[n] = _adamw(weights[n], grad_w[n], given["m_" + n], given["v_" + n])
    return (loss, grad_x, *[grad_w[n] for n in TWIN_WEIGHTS], *[delta_w[n] for n in TWIN_WEIGHTS],
            *[new_m[n] for n in TWIN_WEIGHTS], *[new_v[n] for n in TWIN_WEIGHTS])
```

```python
import functools
import math

import jax
import jax.numpy as jnp
from jax import lax
from jax.experimental import pallas as pl
from jax.experimental.pallas import tpu as pltpu

F32 = jnp.float32
BF16 = jnp.bfloat16
_MXU = BF16

D = 1024
EPS = 1e-6
CONV_K = 4
GDN_H, GDN_DK, GDN_C = 8, 128, 64
SSM_H, SSM_P, SSM_L, SSM_N = 16, 64, 128, 128
MEM_H, MEM_HD = 4, 256
D_FF = 4096
N_DEV = 8

C_QKV, C_ZG, C_ZS, C_XBC, C_GATE, C_DT, C_TOT = 0, 3072, 4096, 5120, 6656, 6784, 6912

ADAM_LR, ADAM_B1, ADAM_B2, ADAM_EPS, ADAM_WD, ADAM_STEP = 0.001, 0.9, 0.999, 1e-08, 0.01, 10

VMEM_LIMIT = 56 * 1024 * 1024

_NN = (((1,), (0,)), ((), ()))
_NT = (((1,), (1,)), ((), ()))
_TN = (((0,), (0,)), ((), ()))


def _dot(a, b, dims=_NN):
    return lax.dot_general(a.astype(_MXU), b.astype(_MXU), dims, preferred_element_type=F32)


def _dot_hi(a, b, dims=_NN):
    return lax.dot_general(a.astype(F32), b.astype(F32), dims, precision=lax.Precision.HIGHEST,
                           preferred_element_type=F32)


def _params(sem):
    return pltpu.CompilerParams(dimension_semantics=sem, vmem_limit_bytes=VMEM_LIMIT)


def _pick(n, cap):
    for d in range(min(cap, n), 0, -128):
        if n % d == 0 and d % 128 == 0:
            return d
    return n


def _sigmoid(x):
    return 1.0 / (1.0 + jnp.exp(-x))


def _silu(x):
    return x * _sigmoid(x)


def _dsilu(x):
    s = _sigmoid(x)
    return s * (1.0 + x * (1.0 - s))


def _softplus(x):
    return jnp.maximum(x, 0.0) + jnp.log(1.0 + jnp.exp(-jnp.abs(x)))


def _iota2(shape, axis):
    return lax.broadcasted_iota(jnp.int32, shape, axis)


def _sum_all(x):
    return jnp.sum(jnp.sum(x, axis=1, keepdims=True), axis=0, keepdims=True)


def _mm(a, b, *, dims="nn", epi="none", extra=None, out_dtype=F32, name, bm=512, bn_cap=1024, bk_cap=1024):
    if dims == "nn":
        (M, K), (K2, N) = a.shape, b.shape
    elif dims == "nt":
        (M, K), (N, K2) = a.shape, b.shape
    else:
        (K, M), (K2, N) = a.shape, b.shape
    assert K == K2, (a.shape, b.shape, dims)
    bm = _pick(M, bm)
    bn = _pick(N, bn_cap)
    bk = _pick(K, bk_cap)
    nk = K // bk
    dn = {"nn": _NN, "nt": _NT, "tn": _TN}[dims]
    a_spec = (pl.BlockSpec((bk, bm), lambda i, j, k: (k, i)) if dims == "tn"
              else pl.BlockSpec((bm, bk), lambda i, j, k: (i, k)))
    b_spec = (pl.BlockSpec((bn, bk), lambda i, j, k: (j, k)) if dims == "nt"
              else pl.BlockSpec((bk, bn), lambda i, j, k: (k, j)))
    o_spec = pl.BlockSpec((bm, bn), lambda i, j, k: (i, j))
    n_extra = 0 if extra is None else 1
    n_out = 2 if epi == "relu2" else 1

    def body(a_ref, b_ref, *rest):
        extra_ref = rest[0] if n_extra else None
        outs = rest[n_extra:n_extra + n_out]
        acc = rest[-1]
        k = pl.program_id(2)

        @pl.when(k == 0)
        def _():
            acc[...] = jnp.zeros_like(acc)

        acc[...] += _dot(a_ref[...], b_ref[...], dn)

        @pl.when(k == nk - 1)
        def _():
            r = acc[...]
            if epi == "res":
                outs[0][...] = (r + extra_ref[...].astype(F32)).astype(outs[0].dtype)
            elif epi == "mul2":
                outs[0][...] = (2.0 * r * extra_ref[...].astype(F32)).astype(outs[0].dtype)
            elif epi == "relu2":
                u = jnp.maximum(r, 0.0)
                outs[0][...] = u.astype(outs[0].dtype)
                outs[1][...] = (u * u).astype(outs[1].dtype)
            else:
                outs[0][...] = r.astype(outs[0].dtype)

    ins = [a, b] + ([extra] if n_extra else [])
    in_specs = [a_spec, b_spec] + ([o_spec] if n_extra else [])
    out_shape = [jax.ShapeDtypeStruct((M, N), out_dtype) for _ in range(n_out)]
    res = pl.pallas_call(
        body, grid=(M // bm, N // bn, nk), in_specs=in_specs, out_specs=[o_spec] * n_out,
        out_shape=out_shape, scratch_shapes=[pltpu.VMEM((bm, bn), F32)], name=name,
        compiler_params=_params(("parallel", "parallel", "arbitrary")))(*ins)
    return res if n_out > 1 else res[0]


def _rmsnorm_fwd(x, w, *, name, bt=256):
    T, Dm = x.shape
    bt = min(bt, T)

    def body(x_ref, w_ref, h_ref):
        xv = x_ref[...]
        r = lax.rsqrt(jnp.mean(xv * xv, axis=1, keepdims=True) + EPS)
        h_ref[...] = (xv * r * w_ref[...]).astype(h_ref.dtype)

    return pl.pallas_call(
        body, grid=(T // bt,),
        in_specs=[pl.BlockSpec((bt, Dm), lambda i: (i, 0)), pl.BlockSpec((1, Dm), lambda i: (0, 0))],
        out_specs=pl.BlockSpec((bt, Dm), lambda i: (i, 0)),
        out_shape=jax.ShapeDtypeStruct((T, Dm), BF16), name=name,
        compiler_params=_params(("parallel",)))(x, w.reshape(1, Dm))


def _rmsnorm_bwd(x, w, dh, dres, *, name, bt=256):
    T, Dm = x.shape
    bt = min(bt, T)
    has_res = dres is not None

    def body(x_ref, w_ref, dh_ref, *rest):
        dres_ref = rest[0] if has_res else None
        dx_ref, dw_ref = rest[-2], rest[-1]
        i = pl.program_id(0)
        xv = x_ref[...]
        r = lax.rsqrt(jnp.mean(xv * xv, axis=1, keepdims=True) + EPS)
        xh = xv * r
        dhv = dh_ref[...].astype(F32)
        dxh = dhv * w_ref[...]
        dx = r * (dxh - xh * jnp.mean(dxh * xh, axis=1, keepdims=True))
        if has_res:
            dx = dx + dres_ref[...]
        dx_ref[...] = dx

        @pl.when(i == 0)
        def _():
            dw_ref[...] = jnp.zeros_like(dw_ref)

        dw_ref[...] += jnp.sum(dhv * xh, axis=0, keepdims=True)

    row = pl.BlockSpec((bt, Dm), lambda i: (i, 0))
    vec = pl.BlockSpec((1, Dm), lambda i: (0, 0))
    ins = [x, w.reshape(1, Dm), dh] + ([dres] if has_res else [])
    dx, dw = pl.pallas_call(
        body, grid=(T // bt,), in_specs=[row, vec, row] + ([row] if has_res else []),
        out_specs=[row, vec],
        out_shape=[jax.ShapeDtypeStruct((T, Dm), F32), jax.ShapeDtypeStruct((1, Dm), F32)],
        name=name, compiler_params=_params(("arbitrary",)))(*ins)
    return dx, dw.reshape(Dm)


def _final_loss(x, w, tgt, *, bt=256):
    T, Dm = x.shape
    bt = min(bt, T)

    def body(x_ref, w_ref, t_ref, loss_ref, dx_ref, dw_ref):
        i = pl.program_id(0)
        xv = x_ref[...]
        wv = w_ref[...]
        r = lax.rsqrt(jnp.mean(xv * xv, axis=1, keepdims=True) + EPS)
        xh = xv * r
        err = xh * wv - t_ref[...]
        part = 0.5 * jnp.sum(jnp.mean(err * err, axis=1, keepdims=True), axis=0, keepdims=True)
        dy = err * (1.0 / Dm)
        dxh = dy * wv
        dx_ref[...] = r * (dxh - xh * jnp.mean(dxh * xh, axis=1, keepdims=True))

        @pl.when(i == 0)
        def _():
            dw_ref[...] = jnp.zeros_like(dw_ref)
            loss_ref[...] = jnp.zeros_like(loss_ref)

        dw_ref[...] += jnp.sum(dy * xh, axis=0, keepdims=True)
        loss_ref[...] += jnp.broadcast_to(part, loss_ref.shape)

    row = pl.BlockSpec((bt, Dm), lambda i: (i, 0))
    vec = pl.BlockSpec((1, Dm), lambda i: (0, 0))
    loss, dx, dw = pl.pallas_call(
        body, grid=(T // bt,), in_specs=[row, vec, row],
        out_specs=[pl.BlockSpec((1, 128), lambda i: (0, 0)), row, vec],
        out_shape=[jax.ShapeDtypeStruct((1, 128), F32), jax.ShapeDtypeStruct((T, Dm), F32),
                   jax.ShapeDtypeStruct((1, Dm), F32)],
        name="final_loss", compiler_params=_params(("arbitrary",)))(x, w.reshape(1, Dm), tgt)
    return loss, dx, dw.reshape(Dm)


def _attn_fwd(q, km, vm, *, bt=256):
    T = q.shape[0]
    M = km.shape[0]
    bt = min(bt, T)
    scale = MEM_HD ** -0.5

    def body(q_ref, k_ref, v_ref, o_ref):
        for h in range(MEM_H):
            sl = slice(h * MEM_HD, (h + 1) * MEM_HD)
            s = _dot(q_ref[:, sl], k_ref[:, sl], _NT) * scale
            s = s - jnp.max(s, axis=1, keepdims=True)
            e = jnp.exp(s)
            p = e / jnp.sum(e, axis=1, keepdims=True)
            o_ref[:, sl] = _dot(p, v_ref[:, sl]).astype(o_ref.dtype)

    row = pl.BlockSpec((bt, D), lambda i: (i, 0))
    mem = pl.BlockSpec((M, D), lambda i: (0, 0))
    return pl.pallas_call(
        body, grid=(T // bt,), in_specs=[row, mem, mem], out_specs=row,
        out_shape=jax.ShapeDtypeStruct((T, D), BF16), name="attn_fwd",
        compiler_params=_params(("parallel",)))(q, km, vm)


def _attn_bwd(q, km, vm, do, *, bt=256):
    T = q.shape[0]
    M = km.shape[0]
    bt = min(bt, T)
    scale = MEM_HD ** -0.5

    def body(q_ref, k_ref, v_ref, do_ref, dq_ref, dk_ref, dv_ref):
        i = pl.program_id(0)

        @pl.when(i == 0)
        def _():
            dk_ref[...] = jnp.zeros_like(dk_ref)
            dv_ref[...] = jnp.zeros_like(dv_ref)

        for h in range(MEM_H):
            sl = slice(h * MEM_HD, (h + 1) * MEM_HD)
            qh, kh, vh, doh = q_ref[:, sl], k_ref[:, sl], v_ref[:, sl], do_ref[:, sl]
            s = _dot(qh, kh, _NT) * scale
            s = s - jnp.max(s, axis=1, keepdims=True)
            e = jnp.exp(s)
            p = e / jnp.sum(e, axis=1, keepdims=True)
            dp = _dot(doh, vh, _NT)
            ds = p * (dp - jnp.sum(dp * p, axis=1, keepdims=True)) * scale
            dq_ref[:, sl] = _dot(ds, kh)
            dk_ref[:, sl] += _dot(ds, qh, _TN)
            dv_ref[:, sl] += _dot(p, doh, _TN)

    row = pl.BlockSpec((bt, D), lambda i: (i, 0))
    mem = pl.BlockSpec((M, D), lambda i: (0, 0))
    return pl.pallas_call(
        body, grid=(T // bt,), in_specs=[row, mem, mem, row], out_specs=[row, mem, mem],
        out_shape=[jax.ShapeDtypeStruct((T, D), F32), jax.ShapeDtypeStruct((M, D), F32),
                   jax.ShapeDtypeStruct((M, D), F32)],
        name="attn_bwd", compiler_params=_params(("arbitrary",)))(q, km, vm, do)


def _conv_apply(halo, x, w_ref, b_ref):
    bt = x.shape[0]
    cat = jnp.concatenate([halo, x], axis=0)
    y = x * w_ref[3:4, :]
    for k in range(CONV_K - 1):
        y = y + pltpu.roll(cat, CONV_K - 1 - k, 0)[8:8 + bt] * w_ref[k:k + 1, :]
    if b_ref is not None:
        y = y + b_ref[...]
    return y


def _l2_parts(act, bc):
    out = []
    for s in range(bc // 128):
        a = act[:, s * 128:(s + 1) * 128]
        r = lax.rsqrt(jnp.sum(a * a, axis=1, keepdims=True) + EPS)
        out.append((a, r))
    return out


def _conv_fwd(p, col0, C, w, b, *, l2, name, bt=256, bc=512):
    T = p.shape[0]
    bt = min(bt, T)
    c0, hb = col0 // bc, bt // 8
    has_b = b is not None

    def body(x_ref, halo_ref, w_ref, *rest):
        b_ref = rest[0] if has_b else None
        o_ref = rest[-1]
        i, j = pl.program_id(0), pl.program_id(1)
        x = x_ref[...]
        halo = jnp.where(i > 0, halo_ref[...], 0.0)
        act = _silu(_conv_apply(halo, x, w_ref, b_ref))
        if l2:
            nrm = jnp.concatenate([a * r for a, r in _l2_parts(act, bc)], axis=1)
            sc = jnp.where(j < 1024 // bc, GDN_DK ** -0.5, 1.0)
            act = jnp.where(j < 2048 // bc, nrm * sc, act)
        o_ref[...] = act

    in_specs = [pl.BlockSpec((bt, bc), lambda i, j: (i, c0 + j)),
                pl.BlockSpec((8, bc), lambda i, j: (jnp.maximum(i * hb - 1, 0), c0 + j)),
                pl.BlockSpec((CONV_K, bc), lambda i, j: (0, j))]
    ins = [p, p, w]
    if has_b:
        in_specs.append(pl.BlockSpec((1, bc), lambda i, j: (0, j)))
        ins.append(b.reshape(1, C))
    return pl.pallas_call(
        body, grid=(T // bt, C // bc), in_specs=in_specs,
        out_specs=pl.BlockSpec((bt, bc), lambda i, j: (i, j)),
        out_shape=jax.ShapeDtypeStruct((T, C), F32), name=name,
        compiler_params=_params(("parallel", "parallel")))(*ins)


def _conv_bwd_act(p, col0, C, w, b, dact, *, l2, name, bt=256, bc=512):
    T = p.shape[0]
    bt = min(bt, T)
    c0, hb = col0 // bc, bt // 8
    has_b = b is not None

    def body(x_ref, halo_ref, w_ref, *rest):
        b_ref = rest[0] if has_b else None
        dact_ref, dy_ref, dw_ref, db_ref = rest[-4:]
        j, i = pl.program_id(0), pl.program_id(1)
        x = x_ref[...]
        halo = jnp.where(i > 0, halo_ref[...], 0.0)
        y = _conv_apply(halo, x, w_ref, b_ref)
        dact = dact_ref[...]
        if l2:
            act = _silu(y)
            sc = jnp.where(j < 1024 // bc, GDN_DK ** -0.5, 1.0)
            parts = []
            for s, (a, r) in enumerate(_l2_parts(act, bc)):
                n = a * r
                dn = dact[:, s * 128:(s + 1) * 128] * sc
                parts.append(r * (dn - n * jnp.sum(dn * n, axis=1, keepdims=True)))
            dact = jnp.where(j < 2048 // bc, jnp.concatenate(parts, axis=1), dact)
        dy = dact * _dsilu(y)
        dy_ref[...] = dy

        @pl.when(i == 0)
        def _():
            dw_ref[...] = jnp.zeros_like(dw_ref)
            db_ref[...] = jnp.zeros_like(db_ref)

        db_ref[...] += jnp.sum(dy, axis=0, keepdims=True)
        cat = jnp.concatenate([halo, x], axis=0)
        dw_ref[3:4, :] += jnp.sum(dy * x, axis=0, keepdims=True)
        for k in range(CONV_K - 1):
            xs = pltpu.roll(cat, CONV_K - 1 - k, 0)[8:8 + bt]
            dw_ref[k:k + 1, :] += jnp.sum(dy * xs, axis=0, keepdims=True)

    in_specs = [pl.BlockSpec((bt, bc), lambda j, i: (i, c0 + j)),
                pl.BlockSpec((8, bc), lambda j, i: (jnp.maximum(i * hb - 1, 0), c0 + j)),
                pl.BlockSpec((CONV_K, bc), lambda j, i: (0, j))]
    ins = [p, p, w]
    if has_b:
        in_specs.append(pl.BlockSpec((1, bc), lambda j, i: (0, j)))
        ins.append(b.reshape(1, C))
    in_specs.append(pl.BlockSpec((bt, bc), lambda j, i: (i, j)))
    ins.append(dact)
    dy, dw, db = pl.pallas_call(
        body, grid=(C // bc, T // bt), in_specs=in_specs,
        out_specs=[pl.BlockSpec((bt, bc), lambda j, i: (i, j)),
                   pl.BlockSpec((CONV_K, bc), lambda j, i: (0, j)),
                   pl.BlockSpec((1, bc), lambda j, i: (0, j))],
        out_shape=[jax.ShapeDtypeStruct((T, C), F32), jax.ShapeDtypeStruct((CONV_K, C), F32),
                   jax.ShapeDtypeStruct((1, C), F32)],
        name=name, compiler_params=_params(("parallel", "arbitrary")))(*ins)
    return dy, dw, db.reshape(C)


def _conv_bwd_in(dy, w, dp_in, col0, T, *, name, bt=256, bc=512):
    C = dy.shape[1]
    bt = min(bt, T)
    c0, hb, nb = col0 // bc, bt // 8, T // bt

    def body(dy_ref, nxt_ref, w_ref, *rest):
        o_ref = rest[-1]
        i = pl.program_id(0)
        dy_v = dy_ref[...]
        nxt = jnp.where(i < nb - 1, nxt_ref[...], 0.0)
        cat = jnp.concatenate([dy_v, nxt], axis=0)
        dx = dy_v * w_ref[3:4, :]
        for k in range(CONV_K - 1):
            s = CONV_K - 1 - k
            dx = dx + pltpu.roll(cat, bt + 8 - s, 0)[0:bt] * w_ref[k:k + 1, :]
        o_ref[...] = dx

    in_specs = [pl.BlockSpec((bt, bc), lambda i, j: (i, j)),
                pl.BlockSpec((8, bc), lambda i, j: (jnp.minimum((i + 1) * hb, T // 8 - 1), j)),
                pl.BlockSpec((CONV_K, bc), lambda i, j: (0, j))]
    ins = [dy, dy, w]
    alias = {}
    if dp_in is not None:
        in_specs.append(pl.BlockSpec(memory_space=pl.ANY))
        ins.append(dp_in)
        alias = {3: 0}
    return pl.pallas_call(
        body, grid=(nb, C // bc), in_specs=in_specs,
        out_specs=pl.BlockSpec((bt, bc), lambda i, j: (i, c0 + j)),
        out_shape=jax.ShapeDtypeStruct((T, C_TOT), F32), input_output_aliases=alias, name=name,
        compiler_params=_params(("parallel", "parallel")))(*ins)


def _expand_mats(shift, row0):
    e = (_iota2((128, D), 0) - row0 == (_iota2((128, D), 1) >> shift)).astype(F32)
    et = ((_iota2((D, 128), 0) >> shift) == _iota2((D, 128), 1) - row0).astype(F32)
    return e, et


def _cum_mats(bt, shift):
    ri, ci = _iota2((bt, bt), 0), _iota2((bt, bt), 1)
    same = (ri >> shift) == (ci >> shift)
    return ((ri >= ci) & same).astype(F32), ((ri <= ci) & same).astype(F32)


def _gdn_gates_fwd(p, alog_row, dtb_row, *, bt=256):
    T = p.shape[0]
    bt = min(bt, T)

    def body(g_ref, al_ref, db_ref, beta_ref, gam_ref):
        gt = g_ref[...]
        eb, _ = _expand_mats(7, 0)
        eg, _ = _expand_mats(7, GDN_H)
        lc, _ = _cum_mats(bt, 6)
        beta_l = _sigmoid(gt)
        g_l = -jnp.exp(al_ref[...]) * _softplus(gt + db_ref[...])
        beta_ref[...] = _dot_hi(beta_l, eb)
        gam_ref[...] = _dot_hi(lc, _dot_hi(g_l, eg))

    vec = pl.BlockSpec((1, 128), lambda i: (0, 0))
    row = pl.BlockSpec((bt, D), lambda i: (i, 0))
    return pl.pallas_call(
        body, grid=(T // bt,),
        in_specs=[pl.BlockSpec((bt, 128), lambda i: (i, C_GATE // 128)), vec, vec],
        out_specs=[row, row],
        out_shape=[jax.ShapeDtypeStruct((T, D), F32)] * 2, name="gdn_gates_fwd",
        compiler_params=_params(("parallel",)))(p, alog_row, dtb_row)


def _gdn_gates_bwd(p, alog_row, dtb_row, dbeta_x, dgam_x, dp_in, *, bt=256):
    T = p.shape[0]
    bt = min(bt, T)

    def body(g_ref, al_ref, db_ref, dbeta_ref, dgam_ref, dpin_ref, dg_out, dal_ref, ddb_ref):
        i = pl.program_id(0)
        gt = g_ref[...]
        _, ebt = _expand_mats(7, 0)
        _, egt = _expand_mats(7, GDN_H)
        _, uc = _cum_mats(bt, 6)
        ea = jnp.exp(al_ref[...])
        zz = gt + db_ref[...]
        g_l = -ea * _softplus(zz)
        beta_l = _sigmoid(gt)
        dg_l = _dot_hi(_dot_hi(uc, dgam_ref[...]), egt)
        dbeta_l = _dot_hi(dbeta_ref[...], ebt)
        da = dg_l * (-ea) * _sigmoid(zz)
        dg_out[...] = da + dbeta_l * beta_l * (1.0 - beta_l)

        @pl.when(i == 0)
        def _():
            dal_ref[...] = jnp.zeros_like(dal_ref)
            ddb_ref[...] = jnp.zeros_like(ddb_ref)

        dal_ref[...] += jnp.sum(dg_l * g_l, axis=0, keepdims=True)
        ddb_ref[...] += jnp.sum(da, axis=0, keepdims=True)

    vec = pl.BlockSpec((1, 128), lambda i: (0, 0))
    row = pl.BlockSpec((bt, D), lambda i: (i, 0))
    gate = pl.BlockSpec((bt, 128), lambda i: (i, C_GATE // 128))
    return pl.pallas_call(
        body, grid=(T // bt,),
        in_specs=[gate, vec, vec, row, row, pl.BlockSpec(memory_space=pl.ANY)],
        out_specs=[gate, vec, vec],
        out_shape=[jax.ShapeDtypeStruct((T, C_TOT), F32), jax.ShapeDtypeStruct((1, 128), F32),
                   jax.ShapeDtypeStruct((1, 128), F32)],
        input_output_aliases={5: 0}, name="gdn_gates_bwd",
        compiler_params=_params(("arbitrary",)))(p, alog_row, dtb_row, dbeta_x, dgam_x, dp_in)


def _ssd_dt_fwd(p, dtb_row, alog_x, *, bt=256):
    T = p.shape[0]
    bt = min(bt, T)

    def body(d_ref, db_ref, al_ref, dt_ref, alpha_ref):
        ed, _ = _expand_mats(6, 0)
        lc, _ = _cum_mats(bt, 7)
        dt_x = _dot_hi(_softplus(d_ref[...] + db_ref[...]), ed)
        dt_ref[...] = dt_x
        alpha_ref[...] = _dot_hi(lc, dt_x * (-jnp.exp(al_ref[...])))

    row = pl.BlockSpec((bt, D), lambda i: (i, 0))
    return pl.pallas_call(
        body, grid=(T // bt,),
        in_specs=[pl.BlockSpec((bt, 128), lambda i: (i, C_DT // 128)),
                  pl.BlockSpec((1, 128), lambda i: (0, 0)), pl.BlockSpec((1, D), lambda i: (0, 0))],
        out_specs=[row, row], out_shape=[jax.ShapeDtypeStruct((T, D), F32)] * 2,
        name="ssd_dt_fwd", compiler_params=_params(("parallel",)))(p, dtb_row, alog_x)


def _ssd_dt_bwd(p, dtb_row, alog_x, ddt_x, dalpha_x, dp_in, *, bt=256):
    T = p.shape[0]
    bt = min(bt, T)

    def body(d_ref, db_ref, al_ref, ddt_ref, dal_ref, dpin_ref, dd_out, ddb_ref, dalog_ref):
        i = pl.program_id(0)
        ed, edt = _expand_mats(6, 0)
        _, uc = _cum_mats(bt, 7)
        zz = d_ref[...] + db_ref[...]
        dt_x = _dot_hi(_softplus(zz), ed)
        a_x = -jnp.exp(al_ref[...])
        da_x = _dot_hi(uc, dal_ref[...])
        ddt_l = _dot_hi(ddt_ref[...] + da_x * a_x, edt)
        draw = ddt_l * _sigmoid(zz)
        dd_out[...] = draw

        @pl.when(i == 0)
        def _():
            ddb_ref[...] = jnp.zeros_like(ddb_ref)
            dalog_ref[...] = jnp.zeros_like(dalog_ref)

        ddb_ref[...] += jnp.sum(draw, axis=0, keepdims=True)
        dalog_ref[...] += jnp.sum(da_x * dt_x, axis=0, keepdims=True) * a_x

    row = pl.BlockSpec((bt, D), lambda i: (i, 0))
    seg = pl.BlockSpec((bt, 128), lambda i: (i, C_DT // 128))
    v128 = pl.BlockSpec((1, 128), lambda i: (0, 0))
    vD = pl.BlockSpec((1, D), lambda i: (0, 0))
    return pl.pallas_call(
        body, grid=(T // bt,),
        in_specs=[seg, v128, vD, row, row, pl.BlockSpec(memory_space=pl.ANY)],
        out_specs=[seg, v128, vD],
        out_shape=[jax.ShapeDtypeStruct((T, C_TOT), F32), jax.ShapeDtypeStruct((1, 128), F32),
                   jax.ShapeDtypeStruct((1, D), F32)],
        input_output_aliases={5: 0}, name="ssd_dt_bwd",
        compiler_params=_params(("arbitrary",)))(p, dtb_row, alog_x, ddt_x, dalpha_x, dp_in)


def _gdn_post_fwd(o, p, w_x, *, bt=256):
    T = o.shape[0]
    bt = min(bt, T)

    def body(o_ref, z_ref, w_ref, out_ref):
        for h in range(GDN_H):
            sl = slice(h * 128, (h + 1) * 128)
            oh = o_ref[:, sl]
            r = lax.rsqrt(jnp.mean(oh * oh, axis=1, keepdims=True) + EPS)
            out_ref[:, sl] = (oh * r * w_ref[:, sl] * _silu(z_ref[:, sl])).astype(out_ref.dtype)

    row = pl.BlockSpec((bt, D), lambda i: (i, 0))
    return pl.pallas_call(
        body, grid=(T // bt,),
        in_specs=[row, pl.BlockSpec((bt, D), lambda i: (i, C_ZG // D)), pl.BlockSpec((1, D), lambda i: (0, 0))],
        out_specs=row, out_shape=jax.ShapeDtypeStruct((T, D), BF16), name="gdn_post_fwd",
        compiler_params=_params(("parallel",)))(o, p, w_x)


def _gdn_post_bwd(dmix, o, p, w_x, *, bt=256):
    T = o.shape[0]
    bt = min(bt, T)

    def body(dm_ref, o_ref, z_ref, w_ref, do_ref, dz_ref, dw_ref):
        i = pl.program_id(0)

        @pl.when(i == 0)
        def _():
            dw_ref[...] = jnp.zeros_like(dw_ref)

        for h in range(GDN_H):
            sl = slice(h * 128, (h + 1) * 128)
            oh, zh, wh, dm = o_ref[:, sl], z_ref[:, sl], w_ref[:, sl], dm_ref[:, sl]
            r = lax.rsqrt(jnp.mean(oh * oh, axis=1, keepdims=True) + EPS)
            ohat = oh * r
            dy = dm * _silu(zh)
            dz_ref[:, sl] = dm * ohat * wh * _dsilu(zh)
            dohat = dy * wh
            do_ref[:, sl] = r * (dohat - ohat * jnp.mean(dohat * ohat, axis=1, keepdims=True))
            dw_ref[:, sl] += jnp.sum(dy * ohat, axis=0, keepdims=True)

    row = pl.BlockSpec((bt, D), lambda i: (i, 0))
    zcol = pl.BlockSpec((bt, D), lambda i: (i, C_ZG // D))
    vec = pl.BlockSpec((1, D), lambda i: (0, 0))
    return pl.pallas_call(
        body, grid=(T // bt,), in_specs=[row, row, zcol, vec], out_specs=[row, zcol, vec],
        out_shape=[jax.ShapeDtypeStruct((T, D), F32), jax.ShapeDtypeStruct((T, C_TOT), F32),
                   jax.ShapeDtypeStruct((1, D), F32)],
        name="gdn_post_bwd", compiler_params=_params(("arbitrary",)))(dmix, o, p, w_x)


def _ssd_post_fwd(y, xs, p, d_x, w, *, bt=256):
    T = y.shape[0]
    bt = min(bt, T)

    def body(y_ref, x_ref, z_ref, d_ref, w_ref, out_ref):
        yg = (y_ref[...] + x_ref[...] * d_ref[...]) * _silu(z_ref[...])
        for g in range(2):
            sl = slice(g * 512, (g + 1) * 512)
            a = yg[:, sl]
            r = lax.rsqrt(jnp.mean(a * a, axis=1, keepdims=True) + EPS)
            out_ref[:, sl] = (a * r * w_ref[:, sl]).astype(out_ref.dtype)

    row = pl.BlockSpec((bt, D), lambda i: (i, 0))
    vec = pl.BlockSpec((1, D), lambda i: (0, 0))
    return pl.pallas_call(
        body, grid=(T // bt,),
        in_specs=[row, row, pl.BlockSpec((bt, D), lambda i: (i, C_ZS // D)), vec, vec],
        out_specs=row, out_shape=jax.ShapeDtypeStruct((T, D), BF16), name="ssd_post_fwd",
        compiler_params=_params(("parallel",)))(y, xs, p, d_x, w)


def _ssd_post_bwd(dmix, y, xs, p, d_x, w, dp_in, *, bt=256):
    T = y.shape[0]
    bt = min(bt, T)

    def body(dm_ref, y_ref, x_ref, z_ref, d_ref, w_ref, dpin_ref, dyy_ref, dz_ref, dd_ref, dw_ref):
        i = pl.program_id(0)

        @pl.when(i == 0)
        def _():
            dd_ref[...] = jnp.zeros_like(dd_ref)
            dw_ref[...] = jnp.zeros_like(dw_ref)

        xv, zv = x_ref[...], z_ref[...]
        yy = y_ref[...] + xv * d_ref[...]
        sz = _silu(zv)
        yg = yy * sz
        parts = []
        for g in range(2):
            sl = slice(g * 512, (g + 1) * 512)
            a = yg[:, sl]
            r = lax.rsqrt(jnp.mean(a * a, axis=1, keepdims=True) + EPS)
            ah = a * r
            dout = dm_ref[:, sl]
            dah = dout * w_ref[:, sl]
            dw_ref[:, sl] += jnp.sum(dout * ah, axis=0, keepdims=True)
            parts.append(r * (dah - ah * jnp.mean(dah * ah, axis=1, keepdims=True)))
        dyg = jnp.concatenate(parts, axis=1)
        dyy = dyg * sz
        dyy_ref[...] = dyy
        dz_ref[...] = dyg * yy * _dsilu(zv)
        dd_ref[...] += jnp.sum(dyy * xv, axis=0, keepdims=True)

    row = pl.BlockSpec((bt, D), lambda i: (i, 0))
    zcol = pl.BlockSpec((bt, D), lambda i: (i, C_ZS // D))
    vec = pl.BlockSpec((1, D), lambda i: (0, 0))
    return pl.pallas_call(
        body, grid=(T // bt,),
        in_specs=[row, row, row, zcol, vec, vec, pl.BlockSpec(memory_space=pl.ANY)],
        out_specs=[row, zcol, vec, vec],
        out_shape=[jax.ShapeDtypeStruct((T, D), F32), jax.ShapeDtypeStruct((T, C_TOT), F32),
                   jax.ShapeDtypeStruct((1, D), F32), jax.ShapeDtypeStruct((1, D), F32)],
        input_output_aliases={6: 1}, name="ssd_post_bwd",
        compiler_params=_params(("arbitrary",)))(dmix, y, xs, p, d_x, w, dp_in)


_NEG = -1e30


def _gdn_terms(q, k, v, bx, gam_c):
    C = GDN_C
    ri, ci = _iota2((C, C), 0), _iota2((C, C), 1)
    eye, low, strict = ri == ci, ri >= ci, ri > ci
    gam_r = jnp.sum(jnp.where(eye, gam_c, 0.0), axis=0, keepdims=True)
    G = jnp.exp(jnp.where(low, gam_c - gam_r, _NEG))
    glast = jnp.sum(jnp.where(_iota2((C, 1), 0) == C - 1, gam_c, 0.0), axis=0, keepdims=True)
    eg, egl, eL = jnp.exp(gam_c), jnp.exp(glast - gam_c), jnp.exp(glast)
    kb, vb = k * bx, v * bx
    M = _dot(kb, k, _NT)
    return dict(eye=eye, low=low, strict=strict, G=G, eg=eg, egl=egl, eL=eL, kb=kb, vb=vb, M=M,
                kbg=kb * eg, qd=q * eg, kd=k * egl)


def _tri_inv(L, eye):
    X = -L
    T = jnp.where(eye, 1.0, 0.0) + X
    Pw = X
    for _ in range(5):
        Pw = _dot_hi(Pw, Pw)
        T = T + _dot_hi(T, Pw)
    return T


def _gdn_chunk_fwd(q, k, v, bx, gam_c, S):
    t = _gdn_terms(q, k, v, bx, gam_c)
    T = _tri_inv(jnp.where(t["strict"], t["M"] * t["G"], 0.0), t["eye"])
    u = _dot(T, t["vb"])
    w = _dot(T, t["kbg"])
    P = _dot(q, k, _NT) * t["G"]
    vn = u - _dot(w, S)
    o = _dot(t["qd"], S) + _dot(P, vn)
    S2 = S * t["eL"] + _dot(t["kd"], vn, _TN)
    return o, S2, T


def _gdn_chunk_bwd(q, k, v, bx, gam_c, S, T, do, dS2):
    t = _gdn_terms(q, k, v, bx, gam_c)
    G, eg, egl, eL, kb, vb, M, kbg, qd, kd = (t[n] for n in
                                              ("G", "eg", "egl", "eL", "kb", "vb", "M", "kbg", "qd", "kd"))
    Q = _dot(q, k, _NT)
    P = Q * G
    u = _dot(T, vb)
    w = _dot(T, kbg)
    vn = u - _dot(w, S)
    dvn = _dot(P, do, _TN) + _dot(kd, dS2)
    dqd = _dot(do, S, _NT)
    dP = jnp.where(t["low"], _dot(do, vn, _NT), 0.0)
    dkd = _dot(vn, dS2, _NT)
    dS = dS2 * eL + _dot(qd, do, _TN) - _dot(w, dvn, _TN)
    dw = -_dot(dvn, S, _NT)
    dT = _dot(dvn, vb, _NT) + _dot(dw, kbg, _NT)
    dvb = _dot(T, dvn, _TN)
    dkbg = _dot(T, dw, _TN)
    dL = jnp.where(t["strict"], -_dot(_dot(T, dT, _TN), T, _NT), 0.0)
    dM = dL * G
    dQ = dP * G
    E = (dL * M + dP * Q) * G
    dq = _dot(dQ, k) + dqd * eg
    dkb = _dot(dM, k) + dkbg * eg
    dk = _dot(dQ, q, _TN) + _dot(dM, kb, _TN) + dkd * egl + dkb * bx
    dbx = dkb * k + dvb * v
    dv = dvb * bx
    rs = lambda a: jnp.sum(a, axis=1, keepdims=True)
    dkd_kd = dkd * kd
    dgam_c = rs(dqd * qd) + rs(dkbg * kbg) - rs(dkd_kd) + rs(E)
    dgam_r = -jnp.sum(E, axis=0, keepdims=True)
    dgam_c = dgam_c + jnp.sum(jnp.where(t["eye"], dgam_r, 0.0), axis=1, keepdims=True)
    dlast = _sum_all(dkd_kd) + eL * _sum_all(S * dS2)
    dgam_c = dgam_c + jnp.where(_iota2((GDN_C, 1), 0) == GDN_C - 1, dlast, 0.0)
    return dq, dk, dv, dbx, dgam_c, dS


def _gdn_core_fwd(qkvn, bx, gx):
    T = qkvn.shape[0]
    N = T // GDN_C
    C = GDN_C

    def body(q_ref, k_ref, v_ref, bx_ref, gx_ref, o_ref, ss_ref, ts_ref, S_scr):
        n = pl.program_id(0)

        @pl.when(n == 0)
        def _():
            S_scr[...] = jnp.zeros_like(S_scr)

        for h in range(GDN_H):
            sl = slice(h * 128, (h + 1) * 128)
            gam_c = jnp.max(gx_ref[:, sl], axis=1, keepdims=True)
            S = S_scr[:, sl]
            o, S2, Tm = _gdn_chunk_fwd(q_ref[:, sl], k_ref[:, sl], v_ref[:, sl], bx_ref[:, sl], gam_c, S)
            o_ref[:, sl] = o
            ss_ref[0, :, sl] = S
            ts_ref[0, h * C:(h + 1) * C, :] = Tm
            S_scr[:, sl] = S2

    blk = lambda c: pl.BlockSpec((C, D), lambda n: (n, c))
    return pl.pallas_call(
        body, grid=(N,), in_specs=[blk(0), blk(1), blk(2), blk(0), blk(0)],
        out_specs=[blk(0), pl.BlockSpec((1, GDN_DK, D), lambda n: (n, 0, 0)),
                   pl.BlockSpec((1, GDN_H * C, C), lambda n: (n, 0, 0))],
        out_shape=[jax.ShapeDtypeStruct((T, D), F32), jax.ShapeDtypeStruct((N, GDN_DK, D), F32),
                   jax.ShapeDtypeStruct((N, GDN_H * C, C), F32)],
        scratch_shapes=[pltpu.VMEM((GDN_DK, D), F32)], name="gdn_core_fwd",
        compiler_params=_params(("arbitrary",)))(qkvn, qkvn, qkvn, bx, gx)


def _gdn_core_bwd(qkvn, bx, gx, s_save, t_save, do):
    T = qkvn.shape[0]
    N = T // GDN_C
    C = GDN_C

    def body(q_ref, k_ref, v_ref, bx_ref, gx_ref, ss_ref, ts_ref, do_ref, dqkv_ref, dbx_ref, dgx_ref, dS_scr):
        n = pl.program_id(0)

        @pl.when(n == 0)
        def _():
            dS_scr[...] = jnp.zeros_like(dS_scr)

        lane0 = _iota2((C, 128), 1) == 0
        for h in range(GDN_H):
            sl = slice(h * 128, (h + 1) * 128)
            gam_c = jnp.max(gx_ref[:, sl], axis=1, keepdims=True)
            dq, dk, dv, dbx, dgam_c, dS = _gdn_chunk_bwd(
                q_ref[:, sl], k_ref[:, sl], v_ref[:, sl], bx_ref[:, sl], gam_c, ss_ref[0, :, sl],
                ts_ref[0, h * C:(h + 1) * C, :], do_ref[:, sl], dS_scr[:, sl])
            dqkv_ref[:, sl] = dq
            dqkv_ref[:, D + h * 128:D + (h + 1) * 128] = dk
            dqkv_ref[:, 2 * D + h * 128:2 * D + (h + 1) * 128] = dv
            dbx_ref[:, sl] = dbx
            dgx_ref[:, sl] = jnp.where(lane0, dgam_c, 0.0)
            dS_scr[:, sl] = dS

    blk = lambda c: pl.BlockSpec((C, D), lambda n: (N - 1 - n, c))
    return pl.pallas_call(
        body, grid=(N,),
        in_specs=[blk(0), blk(1), blk(2), blk(0), blk(0),
                  pl.BlockSpec((1, GDN_DK, D), lambda n: (N - 1 - n, 0, 0)),
                  pl.BlockSpec((1, GDN_H * C, C), lambda n: (N - 1 - n, 0, 0)), blk(0)],
        out_specs=[pl.BlockSpec((C, 3 * D), lambda n: (N - 1 - n, 0)), blk(0), blk(0)],
        out_shape=[jax.ShapeDtypeStruct((T, 3 * D), F32), jax.ShapeDtypeStruct((T, D), F32),
                   jax.ShapeDtypeStruct((T, D), F32)],
        scratch_shapes=[pltpu.VMEM((GDN_DK, D), F32)], name="gdn_core_bwd",
        compiler_params=_params(("arbitrary",)))(qkvn, qkvn, qkvn, bx, gx, s_save, t_save, do)


def _ssd_seg(al_pair, half, s):
    L = SSM_L
    ri, ci = _iota2((L, L), 0), _iota2((L, L), 1)
    ac = jnp.max(jnp.where(half == s, al_pair, _NEG), axis=1, keepdims=True)
    ar = jnp.sum(jnp.where(ri == ci, ac, 0.0), axis=0, keepdims=True)
    return jnp.exp(jnp.where(ri >= ci, ac - ar, _NEG))


def _last_row(a):
    return jnp.sum(jnp.where(_iota2((a.shape[0], 1), 0) == a.shape[0] - 1, a, 0.0), axis=0, keepdims=True)


def _ssd_core_fwd(xbc, dtx, alx):
    T = xbc.shape[0]
    L = SSM_L
    Nc = T // L

    def body(x_ref, bc_ref, dt_ref, al_ref, y_ref, hs_ref, H_scr):
        c = pl.program_id(0)

        @pl.when(c == 0)
        def _():
            H_scr[...] = jnp.zeros_like(H_scr)

        half = _iota2((L, 128), 1) >> 6
        for g in range(2):
            gs = slice(g * 512, (g + 1) * 512)
            Bg = bc_ref[:, g * 128:(g + 1) * 128]
            Cg = bc_ref[:, 256 + g * 128:256 + (g + 1) * 128]
            alg = al_ref[:, gs]
            alast = _last_row(alg)
            xdt = x_ref[:, gs] * dt_ref[:, gs]
            Hg = H_scr[:, gs]
            hs_ref[0, :, gs] = Hg
            CB = _dot(Cg, Bg, _NT)
            y_ref[:, gs] = jnp.exp(alg) * _dot(Cg, Hg)
            H_scr[:, gs] = Hg * jnp.exp(alast) + _dot(Bg, jnp.exp(alast - alg) * xdt, _TN)
            for j in range(4):
                ps = slice(g * 512 + j * 128, g * 512 + (j + 1) * 128)
                al_pair = al_ref[:, ps]
                xp = x_ref[:, ps] * dt_ref[:, ps]
                ys = [_dot(_ssd_seg(al_pair, half, s) * CB, xp) for s in range(2)]
                y_ref[:, ps] += jnp.where(half == 0, ys[0], ys[1])

    row = pl.BlockSpec((L, D), lambda c: (c, 0))
    return pl.pallas_call(
        body, grid=(Nc,), in_specs=[row, pl.BlockSpec((L, 512), lambda c: (c, 2)), row, row],
        out_specs=[row, pl.BlockSpec((1, SSM_N, D), lambda c: (c, 0, 0))],
        out_shape=[jax.ShapeDtypeStruct((T, D), F32), jax.ShapeDtypeStruct((Nc, SSM_N, D), F32)],
        scratch_shapes=[pltpu.VMEM((SSM_N, D), F32)], name="ssd_core_fwd",
        compiler_params=_params(("arbitrary",)))(xbc, xbc, dtx, alx)


def _ssd_core_bwd(xbc, dtx, alx, h_save, dyy, d_x):
    T = xbc.shape[0]
    L = SSM_L
    Nc = T // L

    def body(x_ref, bc_ref, dt_ref, al_ref, hs_ref, dy_ref, d_ref, dx_ref, ddt_ref, dal_ref, dH_scr):
        c = pl.program_id(0)

        @pl.when(c == 0)
        def _():
            dH_scr[...] = jnp.zeros_like(dH_scr)

        lane = _iota2((L, 128), 1)
        half = lane >> 6
        rowi = _iota2((L, 1), 0)
        ri, ci = _iota2((L, L), 0), _iota2((L, L), 1)
        for g in range(2):
            gs = slice(g * 512, (g + 1) * 512)
            Bg = bc_ref[:, g * 128:(g + 1) * 128]
            Cg = bc_ref[:, 256 + g * 128:256 + (g + 1) * 128]
            alg = al_ref[:, gs]
            alast = _last_row(alg)
            eal, edec, eL = jnp.exp(alg), jnp.exp(alast - alg), jnp.exp(alast)
            xg, dtg, dYg = x_ref[:, gs], dt_ref[:, gs], dy_ref[:, gs]
            xdt = xg * dtg
            Hg = hs_ref[0, :, gs]
            dH2 = dH_scr[:, gs]
            CB = _dot(Cg, Bg, _NT)
            dYe = eal * dYg
            dH_scr[:, gs] = dH2 * eL + _dot(Cg, dYe, _TN)
            dC = _dot(dYe, Hg, _NT)
            zg = edec * xdt
            dz = _dot(Bg, dH2)
            dB = _dot(zg, dH2, _NT)
            tz = dz * zg
            dal = dYe * _dot(Cg, Hg) - tz
            dalast = jnp.sum(tz, axis=0, keepdims=True) + eL * jnp.sum(Hg * dH2, axis=0, keepdims=True)
            dal = dal + jnp.where(rowi == L - 1, dalast, 0.0)
            dxdt_g = edec * dz
            dx_ref[:, gs] = dxdt_g * dtg + dYg * d_ref[:, gs]
            ddt_ref[:, gs] = dxdt_g * xg
            dal_ref[:, gs] = dal
            dCB = jnp.zeros((L, L), F32)
            for j in range(4):
                ps = slice(g * 512 + j * 128, g * 512 + (j + 1) * 128)
                al_pair = al_ref[:, ps]
                xp = x_ref[:, ps] * dt_ref[:, ps]
                dYp = dy_ref[:, ps]
                dxp = []
                dal_p = jnp.zeros((L, 128), F32)
                for s in range(2):
                    seg = _ssd_seg(al_pair, half, s)
                    W = seg * CB
                    dW = jnp.where(ri >= ci, _dot(jnp.where(half == s, dYp, 0.0), xp, _NT), 0.0)
                    dxp.append(_dot(W, dYp, _TN))
                    dCB = dCB + dW * seg
                    Es = dW * W
                    dac = jnp.sum(Es, axis=1, keepdims=True) - jnp.sum(
                        jnp.where(ri == ci, jnp.sum(Es, axis=0, keepdims=True), 0.0), axis=1, keepdims=True)
                    dal_p = dal_p + jnp.where(lane == 64 * s, dac, 0.0)
                dxdt_p = jnp.where(half == 0, dxp[0], dxp[1])
                dx_ref[:, ps] += dxdt_p * dt_ref[:, ps]
                ddt_ref[:, ps] += dxdt_p * x_ref[:, ps]
                dal_ref[:, ps] += dal_p
            dx_ref[:, D + g * 128:D + (g + 1) * 128] = dB + _dot(dCB, Cg, _TN)
            dx_ref[:, D + 256 + g * 128:D + 256 + (g + 1) * 128] = dC + _dot(dCB, Bg)

    row = pl.BlockSpec((L, D), lambda c: (Nc - 1 - c, 0))
    bcs = pl.BlockSpec((L, 512), lambda c: (Nc - 1 - c, 2))
    return pl.pallas_call(
        body, grid=(Nc,),
        in_specs=[row, bcs, row, row, pl.BlockSpec((1, SSM_N, D), lambda c: (Nc - 1 - c, 0, 0)), row,
                  pl.BlockSpec((1, D), lambda c: (0, 0))],
        out_specs=[pl.BlockSpec((L, D + 512), lambda c: (Nc - 1 - c, 0)), row, row],
        out_shape=[jax.ShapeDtypeStruct((T, D + 512), F32),
                   jax.ShapeDtypeStruct((T, D), F32), jax.ShapeDtypeStruct((T, D), F32)],
        scratch_shapes=[pltpu.VMEM((SSM_N, D), F32)], name="ssd_core_bwd",
        compiler_params=_params(("arbitrary",)))(xbc, xbc, dtx, alx, h_save, dyy, d_x)


def _local_step(x, mem, tgt, W):
    T = x.shape[0]
    w_out_a, w_out_b = W["w_out"][:D], W["w_out"][D:]
    h1 = _rmsnorm_fwd(x, W["norm1_w"], name="norm1_fwd")
    p = _mm(h1, W["w_in_pad"], name="in_proj")
    qkvn = _conv_fwd(p, C_QKV, 3 * D, W["gdn_conv_w"], None, l2=True, name="gdn_conv_fwd")
    bx, gx = _gdn_gates_fwd(p, W["gdn_alog_row"], W["gdn_dtb_row"])
    o_g, s_save, t_save = _gdn_core_fwd(qkvn, bx, gx)
    mixa = _gdn_post_fwd(o_g, p, W["gdn_norm_x"])
    xbc = _conv_fwd(p, C_XBC, D + 512, W["ssm_conv_w"], W["ssm_conv_b"], l2=False, name="ssm_conv_fwd")
    dtx, alx = _ssd_dt_fwd(p, W["ssm_dtb_row"], W["ssm_alog_x"])
    y_s, h_save = _ssd_core_fwd(xbc, dtx, alx)
    mixb = _ssd_post_fwd(y_s, xbc, p, W["ssm_d_x"], W["ssm_norm_w"].reshape(1, D))
    x1 = _mm(mixa, w_out_a, epi="res", extra=x, name="out_proj_a")
    x1 = _mm(mixb, w_out_b, epi="res", extra=x1, name="out_proj_b")
    h2 = _rmsnorm_fwd(x1, W["norm2_w"], name="norm2_fwd")
    qm = _mm(h2, W["wq_mem"], name="q_proj")
    m = _rmsnorm_fwd(mem, W["mem_norm_w"], name="mem_norm_fwd")
    km = _mm(m, W["wk_mem"], name="k_proj")
    vm = _mm(m, W["wv_mem"], name="v_proj")
    oa = _attn_fwd(qm, km, vm)
    x2 = _mm(oa, W["wo_mem"], epi="res", extra=x1, name="o_proj")
    h3 = _rmsnorm_fwd(x2, W["norm3_w"], name="norm3_fwd")
    u, act = _mm(h3, W["w_up"], epi="relu2", out_dtype=BF16, name="mlp_up")
    x3 = _mm(act, W["w_down"], epi="res", extra=x2, name="mlp_down")
    loss, dx3, g_final = _final_loss(x3, W["final_norm_w"], tgt)
    G = {"final_norm_w": g_final}
    dpre = _mm(dx3, W["w_down"], dims="nt", epi="mul2", extra=u, out_dtype=BF16, name="mlp_down_dx")
    G["w_down"] = _mm(act, dx3, dims="tn", name="mlp_down_dw")
    G["w_up"] = _mm(h3, dpre, dims="tn", name="mlp_up_dw")
    dh3 = _mm(dpre, W["w_up"], dims="nt", name="mlp_up_dx")
    dx2, G["norm3_w"] = _rmsnorm_bwd(x2, W["norm3_w"], dh3, dx3, name="norm3_bwd")
    do_a = _mm(dx2, W["wo_mem"], dims="nt", name="o_proj_dx")
    G["wo_mem"] = _mm(oa, dx2, dims="tn", name="o_proj_dw")
    dq, dk, dv = _attn_bwd(qm, km, vm, do_a)
    G["wq_mem"] = _mm(h2, dq, dims="tn", name="q_proj_dw")
    dh2 = _mm(dq, W["wq_mem"], dims="nt", name="q_proj_dx")
    dx1, G["norm2_w"] = _rmsnorm_bwd(x1, W["norm2_w"], dh2, dx2, name="norm2_bwd")
    G["wk_mem"] = _mm(m, dk, dims="tn", name="k_proj_dw")
    G["wv_mem"] = _mm(m, dv, dims="tn", name="v_proj_dw")
    dm = _mm(dk, W["wk_mem"], dims="nt", name="k_proj_dx")
    dm = _mm(dv, W["wv_mem"], dims="nt", epi="res", extra=dm, name="v_proj_dx")
    _, G["mem_norm_w"] = _rmsnorm_bwd(mem, W["mem_norm_w"], dm, None, name="mem_norm_bwd")
    dmixa = _mm(dx1, w_out_a, dims="nt", name="out_proj_a_dx")
    dmixb = _mm(dx1, w_out_b, dims="nt", name="out_proj_b_dx")
    G["w_out"] = jnp.concatenate([_mm(mixa, dx1, dims="tn", name="out_proj_a_dw"),
                                  _mm(mixb, dx1, dims="tn", name="out_proj_b_dw")], axis=0)
    do_g, dp, G["gdn_norm_x"] = _gdn_post_bwd(dmixa, o_g, p, W["gdn_norm_x"])
    dqkvn, dbx, dgx = _gdn_core_bwd(qkvn, bx, gx, s_save, t_save, do_g)
    dy_g, G["gdn_conv_w"], _ = _conv_bwd_act(p, C_QKV, 3 * D, W["gdn_conv_w"], None, dqkvn, l2=True,
                                             name="gdn_conv_bwd_act")
    dp = _conv_bwd_in(dy_g, W["gdn_conv_w"], dp, C_QKV, T, name="gdn_conv_bwd_in")
    dp, G["gdn_alog_row"], G["gdn_dtb_row"] = _gdn_gates_bwd(p, W["gdn_alog_row"], W["gdn_dtb_row"], dbx, dgx, dp)
    dyy, dp, G["ssm_d_x"], G["ssm_norm_w"] = _ssd_post_bwd(dmixb, y_s, xbc, p, W["ssm_d_x"],
                                                          W["ssm_norm_w"].reshape(1, D), dp)
    dxbc, ddtx, dalx = _ssd_core_bwd(xbc, dtx, alx, h_save, dyy, W["ssm_d_x"])
    dy_s, G["ssm_conv_w"], G["ssm_conv_b"] = _conv_bwd_act(p, C_XBC, D + 512, W["ssm_conv_w"], W["ssm_conv_b"],
                                                           dxbc, l2=False, name="ssm_conv_bwd_act")
    dp = _conv_bwd_in(dy_s, W["ssm_conv_w"], dp, C_XBC, T, name="ssm_conv_bwd_in")
    dp, G["ssm_dtb_row"], G["ssm_alog_x"] = _ssd_dt_bwd(p, W["ssm_dtb_row"], W["ssm_alog_x"], ddtx, dalx, dp)
    dh1 = _mm(dp, W["w_in_pad"], dims="nt", name="in_proj_dx")
    G["w_in_pad"] = _mm(h1, dp, dims="tn", name="in_proj_dw")
    dx, G["norm1_w"] = _rmsnorm_bwd(x, W["norm1_w"], dh1, dx1, name="norm1_bwd")
    return loss, dx, G


_MESH = pl.DeviceIdType.MESH
_ANY = pl.BlockSpec(memory_space=pl.ANY)
_VM = pl.BlockSpec(memory_space=pltpu.VMEM)


def _place():
    return lax.axis_index("x"), lax.axis_index("y"), lax.axis_index("c")


def _all_gather(shards, out_dtype, *, name):
    n = len(shards)

    def body(*refs):
        x_refs, out_refs, stage = refs[:n], refs[n:2 * n], refs[2 * n:3 * n]
        send_sems, recv_sems, local_sems = refs[3 * n:]
        x, y, c = _place()
        me, sibling = (x, y, c), (x, y, 1 - c)
        chips = [(1 - x, y), (x, 1 - y), (1 - x, 1 - y)]

        def slot(px, py, pc):
            return 4 * px + 2 * py + pc

        def copy(a, k, block, to, src=None):
            dst = out_refs[a].at[slot(*block)]
            return pltpu.make_async_remote_copy(
                src_ref=dst if src is None else src, dst_ref=dst, send_sem=send_sems.at[a, k],
                recv_sem=recv_sems.at[a, k], device_id=to, device_id_type=_MESH)

        for a in range(n):
            stage[a][...] = x_refs[a][...].astype(out_dtype)
        mine = [pltpu.make_async_copy(stage[a], out_refs[a].at[slot(*me)], local_sems.at[a]) for a in range(n)]
        for cp in mine:
            cp.start()
        first = []
        for a in range(n):
            first.append(copy(a, 0, me, sibling, src=stage[a]))
            first += [copy(a, 1 + j, me, (*chip, c), src=stage[a]) for j, chip in enumerate(chips)]
        for cp in first:
            cp.start()
        passed = [[copy(a, 4 + j, (*chip, c), sibling) for j, chip in enumerate(chips)] for a in range(n)]
        for j, chip in enumerate(chips):
            for a in range(n):
                copy(a, 1 + j, (*chip, c), me).wait_recv()
                passed[a][j].start()
        for a in range(n):
            copy(a, 0, sibling, me).wait_recv()
            for j, chip in enumerate(chips):
                copy(a, 4 + j, (*chip, 1 - c), me).wait_recv()
        for cp in first + [cp for row in passed for cp in row]:
            cp.wait_send()
        for cp in mine:
            cp.wait()

    outs = pl.pallas_call(
        body, in_specs=[_VM] * n, out_specs=[_ANY] * n,
        out_shape=[jax.ShapeDtypeStruct((N_DEV,) + s.shape, out_dtype) for s in shards],
        scratch_shapes=[pltpu.VMEM(s.shape, out_dtype) for s in shards]
        + [pltpu.SemaphoreType.DMA((n, 7)), pltpu.SemaphoreType.DMA((n, 7)), pltpu.SemaphoreType.DMA((n,))],
        name=name, compiler_params=pltpu.CompilerParams(vmem_limit_bytes=VMEM_LIMIT))(*shards)
    return list(outs)


def _exchange(slabs, *, name):
    n = len(slabs)

    def body(*refs):
        in_refs, out_refs = refs[:n], refs[n:2 * n]
        send_sems, recv_sems, local_sems = refs[2 * n:]
        x, y, c = _place()
        me_slot = 4 * x + 2 * y + c
        rel = [(r >> 2 & 1, r >> 1 & 1, r & 1) for r in range(1, N_DEV)]

        def peer(r):
            rx, ry, rc = r
            return (lax.rem(x + rx, 2), lax.rem(y + ry, 2), lax.rem(c + rc, 2))

        mine = [pltpu.make_async_copy(in_refs[a].at[me_slot], out_refs[a].at[me_slot], local_sems.at[a])
                for a in range(n)]
        for cp in mine:
            cp.start()
        sends = []
        for k, r in enumerate(rel):
            px, py, pc = peer(r)
            for a in range(n):
                sends.append(pltpu.make_async_remote_copy(
                    src_ref=in_refs[a].at[4 * px + 2 * py + pc], dst_ref=out_refs[a].at[me_slot],
                    send_sem=send_sems.at[a, k], recv_sem=recv_sems.at[a, k],
                    device_id=(px, py, pc), device_id_type=_MESH))
        for cp in sends:
            cp.start()
        for k, r in enumerate(rel):
            px, py, pc = peer(r)
            for a in range(n):
                slot = out_refs[a].at[4 * px + 2 * py + pc]
                pltpu.make_async_remote_copy(
                    src_ref=slot, dst_ref=slot, send_sem=send_sems.at[a, k], recv_sem=recv_sems.at[a, k],
                    device_id=(px, py, pc), device_id_type=_MESH).wait_recv()
        for cp in sends:
            cp.wait_send()
        for cp in mine:
            cp.wait()

    outs = pl.pallas_call(
        body, in_specs=[_ANY] * n, out_specs=[_ANY] * n,
        out_shape=[jax.ShapeDtypeStruct(s.shape, s.dtype) for s in slabs],
        scratch_shapes=[pltpu.SemaphoreType.DMA((n, 7)), pltpu.SemaphoreType.DMA((n, 7)),
                        pltpu.SemaphoreType.DMA((n,))],
        name=name, compiler_params=pltpu.CompilerParams(vmem_limit_bytes=VMEM_LIMIT))(*slabs)
    return list(outs)


def _sum8(a, *, name):
    _, R, Cc = a.shape
    br = _pick_rows(R, 128)

    def body(a_ref, o_ref):
        s = a_ref[0]
        for k in range(1, N_DEV):
            s = s + a_ref[k]
        o_ref[...] = s

    return pl.pallas_call(
        body, grid=(R // br,), in_specs=[pl.BlockSpec((N_DEV, br, Cc), lambda i: (0, i, 0))],
        out_specs=pl.BlockSpec((br, Cc), lambda i: (i, 0)), out_shape=jax.ShapeDtypeStruct((R, Cc), F32),
        name=name, compiler_params=_params(("parallel",)))(a)


def _pick_rows(R, cap):
    if R <= cap:
        return R
    for d in range(cap, 7, -8):
        if R % d == 0:
            return d
    return R


def _adamw(w, g, m, v, *, name):
    shape = w.shape
    as2d = (lambda t: t.reshape(1, -1)) if w.ndim == 1 else (lambda t: t)
    w2, g2, m2, v2 = as2d(w), as2d(g), as2d(m), as2d(v)
    R, Cc = w2.shape
    br = _pick_rows(R, 256)
    c1 = 1.0 - ADAM_B1 ** ADAM_STEP
    c2 = 1.0 - ADAM_B2 ** ADAM_STEP

    def body(w_ref, g_ref, m_ref, v_ref, d_ref, nm_ref, nv_ref):
        gv = g_ref[...]
        nm = ADAM_B1 * m_ref[...] + (1.0 - ADAM_B1) * gv
        nv = ADAM_B2 * v_ref[...] + (1.0 - ADAM_B2) * (gv * gv)
        nm_ref[...] = nm
        nv_ref[...] = nv
        d_ref[...] = -ADAM_LR * ((nm / c1) / (jnp.sqrt(nv / c2) + ADAM_EPS) + ADAM_WD * w_ref[...])

    blk = pl.BlockSpec((br, Cc), lambda i: (i, 0))
    outs = pl.pallas_call(
        body, grid=(R // br,), in_specs=[blk] * 4, out_specs=[blk] * 3,
        out_shape=[jax.ShapeDtypeStruct((R, Cc), F32)] * 3, name=name,
        compiler_params=_params(("parallel",)))(w2, g2, m2, v2)
    return tuple(o.reshape(shape) for o in outs)


_BIG = ("w_in", "w_out", "wq_mem", "wk_mem", "wv_mem", "wo_mem", "w_up", "w_down")
_COL_SHARDED = ("w_in", "w_up")
_WEIGHTS = ("norm1_w", "w_in", "gdn_conv_w", "gdn_a_log", "gdn_dt_bias", "gdn_norm_w", "ssm_conv_w", "ssm_conv_b",
            "ssm_a_log", "ssm_dt_bias", "ssm_d", "ssm_norm_w", "w_out", "norm2_w", "mem_norm_w", "wq_mem", "wk_mem",
            "wv_mem", "wo_mem", "norm3_w", "w_up", "w_down", "final_norm_w")
_IN_PAD = 112


def _full_from_slots(name, g):
    if name in _COL_SHARDED:
        return jnp.transpose(g, (1, 0, 2)).reshape(g.shape[1], N_DEV * g.shape[2])
    return g.reshape(N_DEV * g.shape[1], g.shape[2])


def _slots_from_full(name, f):
    if name in _COL_SHARDED:
        return jnp.transpose(f.reshape(f.shape[0], N_DEV, f.shape[1] // N_DEV), (1, 0, 2))
    return f.reshape(N_DEV, f.shape[0] // N_DEV, f.shape[1])


def _pad_w_in(w):
    z = jnp.zeros((w.shape[0], _IN_PAD), w.dtype)
    return jnp.concatenate([w[:, :4096], w[:, 4112:6672], w[:, 4096:4112], z, w[:, 6672:6688], z], axis=1)


def _unpad_w_in(gp):
    return jnp.concatenate([gp[:, :4096], gp[:, C_GATE:C_GATE + 16], gp[:, 4096:C_GATE], gp[:, C_DT:C_DT + 16]],
                           axis=1)


def _pack_rows(vals):
    rows, offs, r = [], [], 0
    for vflat in vals:
        nrow = -(-vflat.shape[0] // 128)
        rows.append(jnp.pad(vflat, (0, nrow * 128 - vflat.shape[0])).reshape(nrow, 128))
        offs.append((r, vflat.shape[0]))
        r += nrow
    pad = (-r) % 8
    if pad:
        rows.append(jnp.zeros((pad, 128), F32))
    return jnp.concatenate(rows, axis=0), offs


def _unpack_rows(packed, offs, shapes):
    out = []
    for (r, nel), shp in zip(offs, shapes):
        nrow = -(-nel // 128)
        out.append(packed[r:r + nrow].reshape(-1)[:nel].reshape(shp))
    return out


def kernel(x, mem, norm1_w, w_in, gdn_conv_w, gdn_a_log, gdn_dt_bias, gdn_norm_w, ssm_conv_w, ssm_conv_b, ssm_a_log, ssm_dt_bias, ssm_d, ssm_norm_w, w_out, norm2_w, mem_norm_w, wq_mem, wk_mem, wv_mem, wo_mem, norm3_w, w_up, w_down, final_norm_w, loss_target, m_norm1_w, m_w_in, m_gdn_conv_w, m_gdn_a_log, m_gdn_dt_bias, m_gdn_norm_w, m_ssm_conv_w, m_ssm_conv_b, m_ssm_a_log, m_ssm_dt_bias, m_ssm_d, m_ssm_norm_w, m_w_out, m_norm2_w, m_mem_norm_w, m_wq_mem, m_wk_mem, m_wv_mem, m_wo_mem, m_norm3_w, m_w_up, m_w_down, m_final_norm_w, v_norm1_w, v_w_in, v_gdn_conv_w, v_gdn_a_log, v_gdn_dt_bias, v_gdn_norm_w, v_ssm_conv_w, v_ssm_conv_b, v_ssm_a_log, v_ssm_dt_bias, v_ssm_d, v_ssm_norm_w, v_w_out, v_norm2_w, v_mem_norm_w, v_wq_mem, v_wk_mem, v_wv_mem, v_wo_mem, v_norm3_w, v_w_up, v_w_down, v_final_norm_w):
    args = dict(locals())
    w_loc = {n: args[n] for n in _WEIGHTS}
    me = 4 * lax.axis_index("x") + 2 * lax.axis_index("y") + lax.axis_index("c")

    big = _all_gather([w_loc[n] for n in _BIG], BF16, name="gather_weights")
    full = {n: _full_from_slots(n, g) for n, g in zip(_BIG, big)}
    conv_pack, conv_offs = _pack_rows([gdn_conv_w.reshape(-1), ssm_conv_w.reshape(-1)])
    conv_all = _all_gather([conv_pack], F32, name="gather_conv")[0]
    gdn_cw, ssm_cw = [], []
    for k in range(N_DEV):
        a, b = _unpack_rows(conv_all[k], conv_offs, [gdn_conv_w.shape, ssm_conv_w.shape])
        gdn_cw.append(a)
        ssm_cw.append(b)
    W = {
        "w_in_pad": _pad_w_in(full["w_in"]), "w_out": full["w_out"], "wq_mem": full["wq_mem"],
        "wk_mem": full["wk_mem"], "wv_mem": full["wv_mem"], "wo_mem": full["wo_mem"], "w_up": full["w_up"],
        "w_down": full["w_down"],
        "norm1_w": norm1_w, "norm2_w": norm2_w, "norm3_w": norm3_w, "mem_norm_w": mem_norm_w,
        "final_norm_w": final_norm_w, "ssm_norm_w": ssm_norm_w, "ssm_conv_b": ssm_conv_b,
        "gdn_conv_w": jnp.concatenate(gdn_cw, axis=1), "ssm_conv_w": jnp.concatenate(ssm_cw, axis=1),
        "gdn_alog_row": jnp.pad(gdn_a_log, (GDN_H, 128 - 2 * GDN_H)).reshape(1, 128),
        "gdn_dtb_row": jnp.pad(gdn_dt_bias, (GDN_H, 128 - 2 * GDN_H)).reshape(1, 128),
        "gdn_norm_x": jnp.tile(gdn_norm_w, GDN_H).reshape(1, D),
        "ssm_dtb_row": jnp.pad(ssm_dt_bias, (0, 128 - SSM_H)).reshape(1, 128),
        "ssm_alog_x": jnp.repeat(ssm_a_log, SSM_P).reshape(1, D),
        "ssm_d_x": jnp.repeat(ssm_d, SSM_P).reshape(1, D),
    }

    loss_part, grad_x, G = _local_step(x[0], mem[0], loss_target[0], W)

    gfull = {n: (G[n] if n != "w_in" else _unpad_w_in(G["w_in_pad"])) for n in _BIG}
    got = _exchange([_slots_from_full(n, gfull[n]) for n in _BIG], name="exchange_grads")
    grads = {n: _sum8(g, name="sum_" + n) for n, g in zip(_BIG, got)}

    small = {
        "norm1_w": G["norm1_w"], "gdn_conv_w": G["gdn_conv_w"], "gdn_a_log": G["gdn_alog_row"][0, GDN_H:2 * GDN_H],
        "gdn_dt_bias": G["gdn_dtb_row"][0, GDN_H:2 * GDN_H], "gdn_norm_w": G["gdn_norm_x"].reshape(GDN_H, 128).sum(0),
        "ssm_conv_w": G["ssm_conv_w"], "ssm_conv_b": G["ssm_conv_b"],
        "ssm_a_log": G["ssm_alog_x"].reshape(SSM_H, SSM_P).sum(1), "ssm_dt_bias": G["ssm_dtb_row"][0, :SSM_H],
        "ssm_d": G["ssm_d_x"].reshape(SSM_H, SSM_P).sum(1), "ssm_norm_w": G["ssm_norm_w"].reshape(D),
        "norm2_w": G["norm2_w"], "mem_norm_w": G["mem_norm_w"], "norm3_w": G["norm3_w"],
        "final_norm_w": G["final_norm_w"], "loss": loss_part[0, :1],
    }
    names = list(small)
    pack, offs = _pack_rows([small[n].reshape(-1) for n in names])
    tot = _sum8(_all_gather([pack], F32, name="gather_small")[0], name="sum_small")
    summed = dict(zip(names, _unpack_rows(tot, offs, [small[n].shape for n in names])))
    loss = summed.pop("loss")[0]
    for n in ("gdn_conv_w", "ssm_conv_w"):
        width = w_loc[n].shape[1]
        summed[n] = lax.dynamic_slice_in_dim(summed[n], me * width, width, axis=1)
    grads.update(summed)

    upd = {n: _adamw(w_loc[n], grads[n], args["m_" + n], args["v_" + n], name="adamw_" + n) for n in _WEIGHTS}
    return (loss, grad_x[None], *[grads[n] for n in _WEIGHTS], *[upd[n][0] for n in _WEIGHTS],
            *[upd[n][1] for n in _WEIGHTS], *[upd[n][2] for n in _WEIGHTS])
```

```python
import functools
import math

import jax
import jax.numpy as jnp
from jax import lax
from jax.experimental import pallas as pl
from jax.experimental.pallas import tpu as pltpu

F32 = jnp.float32
BF16 = jnp.bfloat16
_MXU = BF16

D = 1024
EPS = 1e-6
CONV_K = 4
GDN_H, GDN_DK, GDN_C = 8, 128, 64
SSM_H, SSM_P, SSM_L, SSM_N = 16, 64, 128, 128
MEM_H, MEM_HD = 4, 256
D_FF = 4096
N_DEV = 8

C_QKV, C_ZG, C_ZS, C_XBC, C_GATE, C_DT, C_TOT = 0, 3072, 4096, 5120, 6656, 6784, 6912

ADAM_LR, ADAM_B1, ADAM_B2, ADAM_EPS, ADAM_WD, ADAM_STEP = 0.001, 0.9, 0.999, 1e-08, 0.01, 10

VMEM_LIMIT = 56 * 1024 * 1024

_NN = (((1,), (0,)), ((), ()))
_NT = (((1,), (1,)), ((), ()))
_TN = (((0,), (0,)), ((), ()))


def _dot(a, b, dims=_NN):
    return lax.dot_general(a.astype(_MXU), b.astype(_MXU), dims, preferred_element_type=F32)


def _dot_hi(a, b, dims=_NN):
    return lax.dot_general(a.astype(F32), b.astype(F32), dims, precision=lax.Precision.HIGHEST,
                           preferred_element_type=F32)


def _params(sem):
    return pltpu.CompilerParams(dimension_semantics=sem, vmem_limit_bytes=VMEM_LIMIT)


def _pick(n, cap):
    for d in range(min(cap, n), 0, -128):
        if n % d == 0 and d % 128 == 0:
            return d
    return n


def _sigmoid(x):
    return 1.0 / (1.0 + jnp.exp(-x))


def _silu(x):
    return x * _sigmoid(x)


def _dsilu(x):
    s = _sigmoid(x)
    return s * (1.0 + x * (1.0 - s))


def _softplus(x):
    return jnp.maximum(x, 0.0) + jnp.log(1.0 + jnp.exp(-jnp.abs(x)))


def _iota2(shape, axis):
    return lax.broadcasted_iota(jnp.int32, shape, axis)


def _sum_all(x):
    return jnp.sum(jnp.sum(x, axis=1, keepdims=True), axis=0, keepdims=True)


def _mm(a, b, *, dims="nn", epi="none", extra=None, out_dtype=F32, name, bm=512, bn_cap=1024, bk_cap=1024):
    if dims == "nn":
        (M, K), (K2, N) = a.shape, b.shape
    elif dims == "nt":
        (M, K), (N, K2) = a.shape, b.shape
    else:
        (K, M), (K2, N) = a.shape, b.shape
    assert K == K2, (a.shape, b.shape, dims)
    bm = _pick(M, bm)
    bn = _pick(N, bn_cap)
    bk = _pick(K, bk_cap)
    nk = K // bk
    dn = {"nn": _NN, "nt": _NT, "tn": _TN}[dims]
    a_spec = (pl.BlockSpec((bk, bm), lambda i, j, k: (k, i)) if dims == "tn"
              else pl.BlockSpec((bm, bk), lambda i, j, k: (i, k)))
    b_spec = (pl.BlockSpec((bn, bk), lambda i, j, k: (j, k)) if dims == "nt"
              else pl.BlockSpec((bk, bn), lambda i, j, k: (k, j)))
    o_spec = pl.BlockSpec((bm, bn), lambda i, j, k: (i, j))
    n_extra = 0 if extra is None else 1
    n_out = 2 if epi == "relu2" else 1

    def body(a_ref, b_ref, *rest):
        extra_ref = rest[0] if n_extra else None
        outs = rest[n_extra:n_extra + n_out]
        acc = rest[-1]
        k = pl.program_id(2)

        @pl.when(k == 0)
        def _():
            acc[...] = jnp.zeros_like(acc)

        acc[...] += _dot(a_ref[...], b_ref[...], dn)

        @pl.when(k == nk - 1)
        def _():
            r = acc[...]
            if epi == "res":
                outs[0][...] = (r + extra_ref[...].astype(F32)).astype(outs[0].dtype)
            elif epi == "mul2":
                outs[0][...] = (2.0 * r * extra_ref[...].astype(F32)).astype(outs[0].dtype)
            elif epi == "relu2":
                u = jnp.maximum(r, 0.0)
                outs[0][...] = u.astype(outs[0].dtype)
                outs[1][...] = (u * u).astype(outs[1].dtype)
            else:
                outs[0][...] = r.astype(outs[0].dtype)

    ins = [a, b] + ([extra] if n_extra else [])
    in_specs = [a_spec, b_spec] + ([o_spec] if n_extra else [])
    out_shape = [jax.ShapeDtypeStruct((M, N), out_dtype) for _ in range(n_out)]
    res = pl.pallas_call(
        body, grid=(M // bm, N // bn, nk), in_specs=in_specs, out_specs=[o_spec] * n_out,
        out_shape=out_shape, scratch_shapes=[pltpu.VMEM((bm, bn), F32)], name=name,
        compiler_params=_params(("parallel", "parallel", "arbitrary")))(*ins)
    return res if n_out > 1 else res[0]


def _rmsnorm_fwd(x, w, *, name, bt=256):
    T, Dm = x.shape
    bt = min(bt, T)

    def body(x_ref, w_ref, h_ref):
        xv = x_ref[...]
        r = lax.rsqrt(jnp.mean(xv * xv, axis=1, keepdims=True) + EPS)
        h_ref[...] = (xv * r * w_ref[...]).astype(h_ref.dtype)

    return pl.pallas_call(
        body, grid=(T // bt,),
        in_specs=[pl.BlockSpec((bt, Dm), lambda i: (i, 0)), pl.BlockSpec((1, Dm), lambda i: (0, 0))],
        out_specs=pl.BlockSpec((bt, Dm), lambda i: (i, 0)),
        out_shape=jax.ShapeDtypeStruct((T, Dm), BF16), name=name,
        compiler_params=_params(("parallel",)))(x, w.reshape(1, Dm))


def _rmsnorm_bwd(x, w, dh, dres, *, name, bt=256):
    T, Dm = x.shape
    bt = min(bt, T)
    has_res = dres is not None

    def body(x_ref, w_ref, dh_ref, *rest):
        dres_ref = rest[0] if has_res else None
        dx_ref, dw_ref = rest[-2], rest[-1]
        i = pl.program_id(0)
        xv = x_ref[...]
        r = lax.rsqrt(jnp.mean(xv * xv, axis=1, keepdims=True) + EPS)
        xh = xv * r
        dhv = dh_ref[...].astype(F32)
        dxh = dhv * w_ref[...]
        dx = r * (dxh - xh * jnp.mean(dxh * xh, axis=1, keepdims=True))
        if has_res:
            dx = dx + dres_ref[...]
        dx_ref[...] = dx

        @pl.when(i == 0)
        def _():
            dw_ref[...] = jnp.zeros_like(dw_ref)

        dw_ref[...] += jnp.sum(dhv * xh, axis=0, keepdims=True)

    row = pl.BlockSpec((bt, Dm), lambda i: (i, 0))
    vec = pl.BlockSpec((1, Dm), lambda i: (0, 0))
    ins = [x, w.reshape(1, Dm), dh] + ([dres] if has_res else [])
    dx, dw = pl.pallas_call(
        body, grid=(T // bt,), in_specs=[row, vec, row] + ([row] if has_res else []),
        out_specs=[row, vec],
        out_shape=[jax.ShapeDtypeStruct((T, Dm), F32), jax.ShapeDtypeStruct((1, Dm), F32)],
        name=name, compiler_params=_params(("arbitrary",)))(*ins)
    return dx, dw.reshape(Dm)


def _final_loss(x, w, tgt, *, bt=256):
    T, Dm = x.shape
    bt = min(bt, T)

    def body(x_ref, w_ref, t_ref, loss_ref, dx_ref, dw_ref):
        i = pl.program_id(0)
        xv = x_ref[...]
        wv = w_ref[...]
        r = lax.rsqrt(jnp.mean(xv * xv, axis=1, keepdims=True) + EPS)
        xh = xv * r
        err = xh * wv - t_ref[...]
        part = 0.5 * jnp.sum(jnp.mean(err * err, axis=1, keepdims=True), axis=0, keepdims=True)
        dy = err * (1.0 / Dm)
        dxh = dy * wv
        dx_ref[...] = r * (dxh - xh * jnp.mean(dxh * xh, axis=1, keepdims=True))

        @pl.when(i == 0)
        def _():
            dw_ref[...] = jnp.zeros_like(dw_ref)
            loss_ref[...] = jnp.zeros_like(loss_ref)

        dw_ref[...] += jnp.sum(dy * xh, axis=0, keepdims=True)
        loss_ref[...] += jnp.broadcast_to(part, loss_ref.shape)

    row = pl.BlockSpec((bt, Dm), lambda i: (i, 0))
    vec = pl.BlockSpec((1, Dm), lambda i: (0, 0))
    loss, dx, dw = pl.pallas_call(
        body, grid=(T // bt,), in_specs=[row, vec, row],
        out_specs=[pl.BlockSpec((1, 128), lambda i: (0, 0)), row, vec],
        out_shape=[jax.ShapeDtypeStruct((1, 128), F32), jax.ShapeDtypeStruct((T, Dm), F32),
                   jax.ShapeDtypeStruct((1, Dm), F32)],
        name="final_loss", compiler_params=_params(("arbitrary",)))(x, w.reshape(1, Dm), tgt)
    return loss, dx, dw.reshape(Dm)


def _attn_fwd(q, km, vm, *, bt=256):
    T = q.shape[0]
    M = km.shape[0]
    bt = min(bt, T)
    scale = MEM_HD ** -0.5

    def body(q_ref, k_ref, v_ref, o_ref):
        for h in range(MEM_H):
            sl = slice(h * MEM_HD, (h + 1) * MEM_HD)
            s = _dot(q_ref[:, sl], k_ref[:, sl], _NT) * scale
            s = s - jnp.max(s, axis=1, keepdims=True)
            e = jnp.exp(s)
            p = e / jnp.sum(e, axis=1, keepdims=True)
            o_ref[:, sl] = _dot(p, v_ref[:, sl]).astype(o_ref.dtype)

    row = pl.BlockSpec((bt, D), lambda i: (i, 0))
    mem = pl.BlockSpec((M, D), lambda i: (0, 0))
    return pl.pallas_call(
        body, grid=(T // bt,), in_specs=[row, mem, mem], out_specs=row,
        out_shape=jax.ShapeDtypeStruct((T, D), BF16), name="attn_fwd",
        compiler_params=_params(("parallel",)))(q, km, vm)


def _attn_bwd(q, km, vm, do, *, bt=256):
    T = q.shape[0]
    M = km.shape[0]
    bt = min(bt, T)
    scale = MEM_HD ** -0.5

    def body(q_ref, k_ref, v_ref, do_ref, dq_ref, dk_ref, dv_ref):
        i = pl.program_id(0)

        @pl.when(i == 0)
        def _():
            dk_ref[...] = jnp.zeros_like(dk_ref)
            dv_ref[...] = jnp.zeros_like(dv_ref)

        for h in range(MEM_H):
            sl = slice(h * MEM_HD, (h + 1) * MEM_HD)
            qh, kh, vh, doh = q_ref[:, sl], k_ref[:, sl], v_ref[:, sl], do_ref[:, sl]
            s = _dot(qh, kh, _NT) * scale
            s = s - jnp.max(s, axis=1, keepdims=True)
            e = jnp.exp(s)
            p = e / jnp.sum(e, axis=1, keepdims=True)
            dp = _dot(doh, vh, _NT)
            ds = p * (dp - jnp.sum(dp * p, axis=1, keepdims=True)) * scale
            dq_ref[:, sl] = _dot(ds, kh)
            dk_ref[:, sl] += _dot(ds, qh, _TN)
            dv_ref[:, sl] += _dot(p, doh, _TN)

    row = pl.BlockSpec((bt, D), lambda i: (i, 0))
    mem = pl.BlockSpec((M, D), lambda i: (0, 0))
    return pl.pallas_call(
        body, grid=(T // bt,), in_specs=[row, mem, mem, row], out_specs=[row, mem, mem],
        out_shape=[jax.ShapeDtypeStruct((T, D), F32), jax.ShapeDtypeStruct((M, D), F32),
                   jax.ShapeDtypeStruct((M, D), F32)],
        name="attn_bwd", compiler_params=_params(("arbitrary",)))(q, km, vm, do)


def _conv_apply(halo, x, w_ref, b_ref):
    bt = x.shape[0]
    cat = jnp.concatenate([halo, x], axis=0)
    y = x * w_ref[3:4, :]
    for k in range(CONV_K - 1):
        y = y + pltpu.roll(cat, CONV_K - 1 - k, 0)[8:8 + bt] * w_ref[k:k + 1, :]
    if b_ref is not None:
        y = y + b_ref[...]
    return y


def _l2_parts(act, bc):
    out = []
    for s in range(bc // 128):
        a = act[:, s * 128:(s + 1) * 128]
        r = lax.rsqrt(jnp.sum(a * a, axis=1, keepdims=True) + EPS)
        out.append((a, r))
    return out


def _conv_fwd(p, col0, C, w, b, *, l2, name, bt=256, bc=512):
    T = p.shape[0]
    bt = min(bt, T)
    c0, hb = col0 // bc, bt // 8
    has_b = b is not None

    def body(x_ref, halo_ref, w_ref, *rest):
        b_ref = rest[0] if has_b else None
        o_ref = rest[-1]
        i, j = pl.program_id(0), pl.program_id(1)
        x = x_ref[...]
        halo = jnp.where(i > 0, halo_ref[...], 0.0)
        act = _silu(_conv_apply(halo, x, w_ref, b_ref))
        if l2:
            nrm = jnp.concatenate([a * r for a, r in _l2_parts(act, bc)], axis=1)
            sc = jnp.where(j < 1024 // bc, GDN_DK ** -0.5, 1.0)
            act = jnp.where(j < 2048 // bc, nrm * sc, act)
        o_ref[...] = act

    in_specs = [pl.BlockSpec((bt, bc), lambda i, j: (i, c0 + j)),
                pl.BlockSpec((8, bc), lambda i, j: (jnp.maximum(i * hb - 1, 0), c0 + j)),
                pl.BlockSpec((CONV_K, bc), lambda i, j: (0, j))]
    ins = [p, p, w]
    if has_b:
        in_specs.append(pl.BlockSpec((1, bc), lambda i, j: (0, j)))
        ins.append(b.reshape(1, C))
    return pl.pallas_call(
        body, grid=(T // bt, C // bc), in_specs=in_specs,
        out_specs=pl.BlockSpec((bt, bc), lambda i, j: (i, j)),
        out_shape=jax.ShapeDtypeStruct((T, C), F32), name=name,
        compiler_params=_params(("parallel", "parallel")))(*ins)


def _conv_bwd_act(p, col0, C, w, b, dact, *, l2, name, bt=256, bc=512):
    T = p.shape[0]
    bt = min(bt, T)
    c0, hb = col0 // bc, bt // 8
    has_b = b is not None

    def body(x_ref, halo_ref, w_ref, *rest):
        b_ref = rest[0] if has_b else None
        dact_ref, dy_ref, dw_ref, db_ref = rest[-4:]
        j, i = pl.program_id(0), pl.program_id(1)
        x = x_ref[...]
        halo = jnp.where(i > 0, halo_ref[...], 0.0)
        y = _conv_apply(halo, x, w_ref, b_ref)
        dact = dact_ref[...]
        if l2:
            act = _silu(y)
            sc = jnp.where(j < 1024 // bc, GDN_DK ** -0.5, 1.0)
            parts = []
            for s, (a, r) in enumerate(_l2_parts(act, bc)):
                n = a * r
                dn = dact[:, s * 128:(s + 1) * 128] * sc
                parts.append(r * (dn - n * jnp.sum(dn * n, axis=1, keepdims=True)))
            dact = jnp.where(j < 2048 // bc, jnp.concatenate(parts, axis=1), dact)
        dy = dact * _dsilu(y)
        dy_ref[...] = dy

        @pl.when(i == 0)
        def _():
            dw_ref[...] = jnp.zeros_like(dw_ref)
            db_ref[...] = jnp.zeros_like(db_ref)

        db_ref[...] += jnp.sum(dy, axis=0, keepdims=True)
        cat = jnp.concatenate([halo, x], axis=0)
        dw_ref[3:4, :] += jnp.sum(dy * x, axis=0, keepdims=True)
        for k in range(CONV_K - 1):
            xs = pltpu.roll(cat, CONV_K - 1 - k, 0)[8:8 + bt]
            dw_ref[k:k + 1, :] += jnp.sum(dy * xs, axis=0, keepdims=True)

    in_specs = [pl.BlockSpec((bt, bc), lambda j, i: (i, c0 + j)),
                pl.BlockSpec((8, bc), lambda j, i: (jnp.maximum(i * hb - 1, 0), c0 + j)),
                pl.BlockSpec((CONV_K, bc), lambda j, i: (0, j))]
    ins = [p, p, w]
    if has_b:
        in_specs.append(pl.BlockSpec((1, bc), lambda j, i: (0, j)))
        ins.append(b.reshape(1, C))
    in_specs.append(pl.BlockSpec((bt, bc), lambda j, i: (i, j)))
    ins.append(dact)
    dy, dw, db = pl.pallas_call(
        body, grid=(C // bc, T // bt), in_specs=in_specs,
        out_specs=[pl.BlockSpec((bt, bc), lambda j, i: (i, j)),
                   pl.BlockSpec((CONV_K, bc), lambda j, i: (0, j)),
                   pl.BlockSpec((1, bc), lambda j, i: (0, j))],
        out_shape=[jax.ShapeDtypeStruct((T, C), F32), jax.ShapeDtypeStruct((CONV_K, C), F32),
                   jax.ShapeDtypeStruct((1, C), F32)],
        name=name, compiler_params=_params(("parallel", "arbitrary")))(*ins)
    return dy, dw, db.reshape(C)


def _conv_bwd_in(dy, w, dp_in, col0, T, *, name, bt=256, bc=512):
    C = dy.shape[1]
    bt = min(bt, T)
    c0, hb, nb = col0 // bc, bt // 8, T // bt

    def body(dy_ref, nxt_ref, w_ref, *rest):
        o_ref = rest[-1]
        i = pl.program_id(0)
        dy_v = dy_ref[...]
        nxt = jnp.where(i < nb - 1, nxt_ref[...], 0.0)
        cat = jnp.concatenate([dy_v, nxt], axis=0)
        dx = dy_v * w_ref[3:4, :]
        for k in range(CONV_K - 1):
            s = CONV_K - 1 - k
            dx = dx + pltpu.roll(cat, bt + 8 - s, 0)[0:bt] * w_ref[k:k + 1, :]
        o_ref[...] = dx

    in_specs = [pl.BlockSpec((bt, bc), lambda i, j: (i, j)),
                pl.BlockSpec((8, bc), lambda i, j: (jnp.minimum((i + 1) * hb, T // 8 - 1), j)),
                pl.BlockSpec((CONV_K, bc), lambda i, j: (0, j))]
    ins = [dy, dy, w]
    alias = {}
    if dp_in is not None:
        in_specs.append(pl.BlockSpec(memory_space=pl.ANY))
        ins.append(dp_in)
        alias = {3: 0}
    return pl.pallas_call(
        body, grid=(nb, C // bc), in_specs=in_specs,
        out_specs=pl.BlockSpec((bt, bc), lambda i, j: (i, c0 + j)),
        out_shape=jax.ShapeDtypeStruct((T, C_TOT), F32), input_output_aliases=alias, name=name,
        compiler_params=_params(("parallel", "parallel")))(*ins)


def _expand_mats(shift, row0):
    e = (_iota2((128, D), 0) - row0 == (_iota2((128, D), 1) >> shift)).astype(F32)
    et = ((_iota2((D, 128), 0) >> shift) == _iota2((D, 128), 1) - row0).astype(F32)
    return e, et


def _cum_mats(bt, shift):
    ri, ci = _iota2((bt, bt), 0), _iota2((bt, bt), 1)
    same = (ri >> shift) == (ci >> shift)
    return ((ri >= ci) & same).astype(F32), ((ri <= ci) & same).astype(F32)


def _gdn_gates_fwd(p, alog_row, dtb_row, *, bt=256):
    T = p.shape[0]
    bt = min(bt, T)

    def body(g_ref, al_ref, db_ref, beta_ref, gam_ref):
        gt = g_ref[...]
        eb, _ = _expand_mats(7, 0)
        eg, _ = _expand_mats(7, GDN_H)
        lc, _ = _cum_mats(bt, 6)
        beta_l = _sigmoid(gt)
        g_l = -jnp.exp(al_ref[...]) * _softplus(gt + db_ref[...])
        beta_ref[...] = _dot_hi(beta_l, eb)
        gam_ref[...] = _dot_hi(lc, _dot_hi(g_l, eg))

    vec = pl.BlockSpec((1, 128), lambda i: (0, 0))
    row = pl.BlockSpec((bt, D), lambda i: (i, 0))
    return pl.pallas_call(
        body, grid=(T // bt,),
        in_specs=[pl.BlockSpec((bt, 128), lambda i: (i, C_GATE // 128)), vec, vec],
        out_specs=[row, row],
        out_shape=[jax.ShapeDtypeStruct((T, D), F32)] * 2, name="gdn_gates_fwd",
        compiler_params=_params(("parallel",)))(p, alog_row, dtb_row)


def _gdn_gates_bwd(p, alog_row, dtb_row, dbeta_x, dgam_x, dp_in, *, bt=256):
    T = p.shape[0]
    bt = min(bt, T)

    def body(g_ref, al_ref, db_ref, dbeta_ref, dgam_ref, dpin_ref, dg_out, dal_ref, ddb_ref):
        i = pl.program_id(0)
        gt = g_ref[...]
        _, ebt = _expand_mats(7, 0)
        _, egt = _expand_mats(7, GDN_H)
        _, uc = _cum_mats(bt, 6)
        ea = jnp.exp(al_ref[...])
        zz = gt + db_ref[...]
        g_l = -ea * _softplus(zz)
        beta_l = _sigmoid(gt)
        dg_l = _dot_hi(_dot_hi(uc, dgam_ref[...]), egt)
        dbeta_l = _dot_hi(dbeta_ref[...], ebt)
        da = dg_l * (-ea) * _sigmoid(zz)
        dg_out[...] = da + dbeta_l * beta_l * (1.0 - beta_l)

        @pl.when(i == 0)
        def _():
            dal_ref[...] = jnp.zeros_like(dal_ref)
            ddb_ref[...] = jnp.zeros_like(ddb_ref)

        dal_ref[...] += jnp.sum(dg_l * g_l, axis=0, keepdims=True)
        ddb_ref[...] += jnp.sum(da, axis=0, keepdims=True)

    vec = pl.BlockSpec((1, 128), lambda i: (0, 0))
    row = pl.BlockSpec((bt, D), lambda i: (i, 0))
    gate = pl.BlockSpec((bt, 128), lambda i: (i, C_GATE // 128))
    return pl.pallas_call(
        body, grid=(T // bt,),
        in_specs=[gate, vec, vec, row, row, pl.BlockSpec(memory_space=pl.ANY)],
        out_specs=[gate, vec, vec],
        out_shape=[jax.ShapeDtypeStruct((T, C_TOT), F32), jax.ShapeDtypeStruct((1, 128), F32),
                   jax.ShapeDtypeStruct((1, 128), F32)],
        input_output_aliases={5: 0}, name="gdn_gates_bwd",
        compiler_params=_params(("arbitrary",)))(p, alog_row, dtb_row, dbeta_x, dgam_x, dp_in)


def _ssd_dt_fwd(p, dtb_row, alog_x, *, bt=256):
    T = p.shape[0]
    bt = min(bt, T)

    def body(d_ref, db_ref, al_ref, dt_ref, alpha_ref):
        ed, _ = _expand_mats(6, 0)
        lc, _ = _cum_mats(bt, 7)
        dt_x = _dot_hi(_softplus(d_ref[...] + db_ref[...]), ed)
        dt_ref[...] = dt_x
        alpha_ref[...] = _dot_hi(lc, dt_x * (-jnp.exp(al_ref[...])))

    row = pl.BlockSpec((bt, D), lambda i: (i, 0))
    return pl.pallas_call(
        body, grid=(T // bt,),
        in_specs=[pl.BlockSpec((bt, 128), lambda i: (i, C_DT // 128)),
                  pl.BlockSpec((1, 128), lambda i: (0, 0)), pl.BlockSpec((1, D), lambda i: (0, 0))],
        out_specs=[row, row], out_shape=[jax.ShapeDtypeStruct((T, D), F32)] * 2,
        name="ssd_dt_fwd", compiler_params=_params(("parallel",)))(p, dtb_row, alog_x)


def _ssd_dt_bwd(p, dtb_row, alog_x, ddt_x, dalpha_x, dp_in, *, bt=256):
    T = p.shape[0]
    bt = min(bt, T)

    def body(d_ref, db_ref, al_ref, ddt_ref, dal_ref, dpin_ref, dd_out, ddb_ref, dalog_ref):
        i = pl.program_id(0)
        ed, edt = _expand_mats(6, 0)
        _, uc = _cum_mats(bt, 7)
        zz = d_ref[...] + db_ref[...]
        dt_x = _dot_hi(_softplus(zz), ed)
        a_x = -jnp.exp(al_ref[...])
        da_x = _dot_hi(uc, dal_ref[...])
        ddt_l = _dot_hi(ddt_ref[...] + da_x * a_x, edt)
        draw = ddt_l * _sigmoid(zz)
        dd_out[...] = draw

        @pl.when(i == 0)
        def _():
            ddb_ref[...] = jnp.zeros_like(ddb_ref)
            dalog_ref[...] = jnp.zeros_like(dalog_ref)

        ddb_ref[...] += jnp.sum(draw, axis=0, keepdims=True)
        dalog_ref[...] += jnp.sum(da_x * dt_x, axis=0, keepdims=True) * a_x

    row = pl.BlockSpec((bt, D), lambda i: (i, 0))
    seg = pl.BlockSpec((bt, 128), lambda i: (i, C_DT // 128))
    v128 = pl.BlockSpec((1, 128), lambda i: (0, 0))
    vD = pl.BlockSpec((1, D), lambda i: (0, 0))
    return pl.pallas_call(
        body, grid=(T // bt,),
        in_specs=[seg, v128, vD, row, row, pl.BlockSpec(memory_space=pl.ANY)],
        out_specs=[seg, v128, vD],
        out_shape=[jax.ShapeDtypeStruct((T, C_TOT), F32), jax.ShapeDtypeStruct((1, 128), F32),
                   jax.ShapeDtypeStruct((1, D), F32)],
        input_output_aliases={5: 0}, name="ssd_dt_bwd",
        compiler_params=_params(("arbitrary",)))(p, dtb_row, alog_x, ddt_x, dalpha_x, dp_in)


def _gdn_post_fwd(o, p, w_x, *, bt=256):
    T = o.shape[0]
    bt = min(bt, T)

    def body(o_ref, z_ref, w_ref, out_ref):
        for h in range(GDN_H):
            sl = slice(h * 128, (h + 1) * 128)
            oh = o_ref[:, sl]
            r = lax.rsqrt(jnp.mean(oh * oh, axis=1, keepdims=True) + EPS)
            out_ref[:, sl] = (oh * r * w_ref[:, sl] * _silu(z_ref[:, sl])).astype(out_ref.dtype)

    row = pl.BlockSpec((bt, D), lambda i: (i, 0))
    return pl.pallas_call(
        body, grid=(T // bt,),
        in_specs=[row, pl.BlockSpec((bt, D), lambda i: (i, C_ZG // D)), pl.BlockSpec((1, D), lambda i: (0, 0))],
        out_specs=row, out_shape=jax.ShapeDtypeStruct((T, D), BF16), name="gdn_post_fwd",
        compiler_params=_params(("parallel",)))(o, p, w_x)


def _gdn_post_bwd(dmix, o, p, w_x, *, bt=256):
    T = o.shape[0]
    bt = min(bt, T)

    def body(dm_ref, o_ref, z_ref, w_ref, do_ref, dz_ref, dw_ref):
        i = pl.program_id(0)

        @pl.when(i == 0)
        def _():
            dw_ref[...] = jnp.zeros_like(dw_ref)

        for h in range(GDN_H):
            sl = slice(h * 128, (h + 1) * 128)
            oh, zh, wh, dm = o_ref[:, sl], z_ref[:, sl], w_ref[:, sl], dm_ref[:, sl]
            r = lax.rsqrt(jnp.mean(oh * oh, axis=1, keepdims=True) + EPS)
            ohat = oh * r
            dy = dm * _silu(zh)
            dz_ref[:, sl] = dm * ohat * wh * _dsilu(zh)
            dohat = dy * wh
            do_ref[:, sl] = r * (dohat - ohat * jnp.mean(dohat * ohat, axis=1, keepdims=True))
            dw_ref[:, sl] += jnp.sum(dy * ohat, axis=0, keepdims=True)

    row = pl.BlockSpec((bt, D), lambda i: (i, 0))
    zcol = pl.BlockSpec((bt, D), lambda i: (i, C_ZG // D))
    vec = pl.BlockSpec((1, D), lambda i: (0, 0))
    return pl.pallas_call(
        body, grid=(T // bt,), in_specs=[row, row, zcol, vec], out_specs=[row, zcol, vec],
        out_shape=[jax.ShapeDtypeStruct((T, D), F32), jax.ShapeDtypeStruct((T, C_TOT), F32),
                   jax.ShapeDtypeStruct((1, D), F32)],
        name="gdn_post_bwd", compiler_params=_params(("arbitrary",)))(dmix, o, p, w_x)


def _ssd_post_fwd(y, xs, p, d_x, w, *, bt=256):
    T = y.shape[0]
    bt = min(bt, T)

    def body(y_ref, x_ref, z_ref, d_ref, w_ref, out_ref):
        yg = (y_ref[...] + x_ref[...] * d_ref[...]) * _silu(z_ref[...])
        for g in range(2):
            sl = slice(g * 512, (g + 1) * 512)
            a = yg[:, sl]
            r = lax.rsqrt(jnp.mean(a * a, axis=1, keepdims=True) + EPS)
            out_ref[:, sl] = (a * r * w_ref[:, sl]).astype(out_ref.dtype)

    row = pl.BlockSpec((bt, D), lambda i: (i, 0))
    vec = pl.BlockSpec((1, D), lambda i: (0, 0))
    return pl.pallas_call(
        body, grid=(T // bt,),
        in_specs=[row, row, pl.BlockSpec((bt, D), lambda i: (i, C_ZS // D)), vec, vec],
        out_specs=row, out_shape=jax.ShapeDtypeStruct((T, D), BF16), name="ssd_post_fwd",
        compiler_params=_params(("parallel",)))(y, xs, p, d_x, w)


def _ssd_post_bwd(dmix, y, xs, p, d_x, w, dp_in, *, bt=256):
    T = y.shape[0]
    bt = min(bt, T)

    def body(dm_ref, y_ref, x_ref, z_ref, d_ref, w_ref, dpin_ref, dyy_ref, dz_ref, dd_ref, dw_ref):
        i = pl.program_id(0)

        @pl.when(i == 0)
        def _():
            dd_ref[...] = jnp.zeros_like(dd_ref)
            dw_ref[...] = jnp.zeros_like(dw_ref)

        xv, zv = x_ref[...], z_ref[...]
        yy = y_ref[...] + xv * d_ref[...]
        sz = _silu(zv)
        yg = yy * sz
        parts = []
        for g in range(2):
            sl = slice(g * 512, (g + 1) * 512)
            a = yg[:, sl]
            r = lax.rsqrt(jnp.mean(a * a, axis=1, keepdims=True) + EPS)
            ah = a * r
            dout = dm_ref[:, sl]
            dah = dout * w_ref[:, sl]
            dw_ref[:, sl] += jnp.sum(dout * ah, axis=0, keepdims=True)
            parts.append(r * (dah - ah * jnp.mean(dah * ah, axis=1, keepdims=True)))
        dyg = jnp.concatenate(parts, axis=1)
        dyy = dyg * sz
        dyy_ref[...] = dyy
        dz_ref[...] = dyg * yy * _dsilu(zv)
        dd_ref[...] += jnp.sum(dyy * xv, axis=0, keepdims=True)

    row = pl.BlockSpec((bt, D), lambda i: (i, 0))
    zcol = pl.BlockSpec((bt, D), lambda i: (i, C_ZS // D))
    vec = pl.BlockSpec((1, D), lambda i: (0, 0))
    return pl.pallas_call(
        body, grid=(T // bt,),
        in_specs=[row, row, row, zcol, vec, vec, pl.BlockSpec(memory_space=pl.ANY)],
        out_specs=[row, zcol, vec, vec],
        out_shape=[jax.ShapeDtypeStruct((T, D), F32), jax.ShapeDtypeStruct((T, C_TOT), F32),
                   jax.ShapeDtypeStruct((1, D), F32), jax.ShapeDtypeStruct((1, D), F32)],
        input_output_aliases={6: 1}, name="ssd_post_bwd",
        compiler_params=_params(("arbitrary",)))(dmix, y, xs, p, d_x, w, dp_in)


_NEG = -1e30


def _gdn_terms(q, k, v, bx, gam_c):
    C = GDN_C
    ri, ci = _iota2((C, C), 0), _iota2((C, C), 1)
    eye, low, strict = ri == ci, ri >= ci, ri > ci
    gam_r = jnp.sum(jnp.where(eye, gam_c, 0.0), axis=0, keepdims=True)
    G = jnp.exp(jnp.where(low, gam_c - gam_r, _NEG))
    glast = jnp.sum(jnp.where(_iota2((C, 1), 0) == C - 1, gam_c, 0.0), axis=0, keepdims=True)
    eg, egl, eL = jnp.exp(gam_c), jnp.exp(glast - gam_c), jnp.exp(glast)
    kb, vb = k * bx, v * bx
    M = _dot(kb, k, _NT)
    return dict(eye=eye, low=low, strict=strict, G=G, eg=eg, egl=egl, eL=eL, kb=kb, vb=vb, M=M,
                kbg=kb * eg, qd=q * eg, kd=k * egl, q=q, k=k, v=v, bx=bx)


def _split(a):
    hi = a.astype(_MXU)
    return hi, (a - hi.astype(F32)).astype(_MXU)


def _dot3s(a, b):
    d = lambda p, q: lax.dot_general(p, q, _NN, preferred_element_type=F32)
    return d(a[0], b[0]) + d(a[0], b[1]) + d(a[1], b[0])


def _tri_inv_many(Ls, eye):
    eyef = jnp.where(eye, 1.0, 0.0)
    Ts = [eyef - L for L in Ls]
    Ps = [-L for L in Ls]
    for _ in range(5):
        sp = [_split(p) for p in Ps]
        Ps = [_dot3s(s, s) for s in sp]
        sp = [_split(p) for p in Ps]
        st = [_split(t) for t in Ts]
        Ts = [t + _dot3s(a, b) for t, a, b in zip(Ts, st, sp)]
    return Ts


def _gdn_heads(q_ref, k_ref, v_ref, bx_ref, gx_ref):
    out = []
    for h in range(GDN_H):
        sl = slice(h * 128, (h + 1) * 128)
        gam_c = jnp.max(gx_ref[:, sl], axis=1, keepdims=True)
        out.append(_gdn_terms(q_ref[:, sl], k_ref[:, sl], v_ref[:, sl], bx_ref[:, sl], gam_c))
    return out


def _gdn_prep(qkvn, bx, gx):
    T = qkvn.shape[0]
    N = T // GDN_C
    C = GDN_C

    def body(q_ref, k_ref, v_ref, bx_ref, gx_ref, u_ref, w_ref, qd_ref, kd_ref, p_ref, t_ref):
        ts = _gdn_heads(q_ref, k_ref, v_ref, bx_ref, gx_ref)
        Ts = _tri_inv_many([jnp.where(t["strict"], t["M"] * t["G"], 0.0) for t in ts], ts[0]["eye"])
        for h, (t, Tm) in enumerate(zip(ts, Ts)):
            sl = slice(h * 128, (h + 1) * 128)
            rows = slice(h * C, (h + 1) * C)
            u_ref[:, sl] = _dot(Tm, t["vb"])
            w_ref[:, sl] = _dot(Tm, t["kbg"]).astype(w_ref.dtype)
            qd_ref[:, sl] = t["qd"].astype(qd_ref.dtype)
            kd_ref[:, sl] = t["kd"].astype(kd_ref.dtype)
            p_ref[0, rows, :] = _dot(t["q"], t["k"], _NT) * t["G"]
            t_ref[0, rows, :] = Tm

    blk = lambda c: pl.BlockSpec((C, D), lambda n: (n, c))
    sq = pl.BlockSpec((1, GDN_H * C, C), lambda n: (n, 0, 0))
    return pl.pallas_call(
        body, grid=(N,), in_specs=[blk(0), blk(1), blk(2), blk(0), blk(0)],
        out_specs=[blk(0), blk(0), blk(0), blk(0), sq, sq],
        out_shape=[jax.ShapeDtypeStruct((T, D), F32), jax.ShapeDtypeStruct((T, D), BF16),
                   jax.ShapeDtypeStruct((T, D), BF16), jax.ShapeDtypeStruct((T, D), BF16),
                   jax.ShapeDtypeStruct((N, GDN_H * C, C), F32), jax.ShapeDtypeStruct((N, GDN_H * C, C), F32)],
        name="gdn_prep", compiler_params=_params(("parallel",)))(qkvn, qkvn, qkvn, bx, gx)


def _gdn_scan_fwd(u, w, qd, kd, pm, gx):
    T = u.shape[0]
    N = T // GDN_C
    C = GDN_C

    def body(u_ref, w_ref, qd_ref, kd_ref, p_ref, gx_ref, o_ref, vn_ref, ss_ref, S_scr):
        n = pl.program_id(0)

        @pl.when(n == 0)
        def _():
            S_scr[...] = jnp.zeros_like(S_scr)

        sls = [slice(h * 128, (h + 1) * 128) for h in range(GDN_H)]
        Ss = [S_scr[:, sl] for sl in sls]
        vns = [u_ref[:, sl] - _dot(w_ref[:, sl], S) for sl, S in zip(sls, Ss)]
        for h, (sl, S, vn) in enumerate(zip(sls, Ss, vns)):
            ss_ref[0, :, sl] = S
            vn_ref[:, sl] = vn.astype(vn_ref.dtype)
            o_ref[:, sl] = _dot(qd_ref[:, sl], S) + _dot(p_ref[0, h * C:(h + 1) * C, :], vn)
            S_scr[:, sl] = S * jnp.exp(gx_ref[C - 1:C, sl]) + _dot(kd_ref[:, sl], vn, _TN)

    blk = pl.BlockSpec((C, D), lambda n: (n, 0))
    return pl.pallas_call(
        body, grid=(N,),
        in_specs=[blk, blk, blk, blk, pl.BlockSpec((1, GDN_H * C, C), lambda n: (n, 0, 0)), blk],
        out_specs=[blk, blk, pl.BlockSpec((1, GDN_DK, D), lambda n: (n, 0, 0))],
        out_shape=[jax.ShapeDtypeStruct((T, D), F32), jax.ShapeDtypeStruct((T, D), BF16),
                   jax.ShapeDtypeStruct((N, GDN_DK, D), F32)],
        scratch_shapes=[pltpu.VMEM((GDN_DK, D), F32)], name="gdn_scan_fwd",
        compiler_params=_params(("arbitrary",)))(u, w, qd, kd, pm, gx)


def _gdn_scan_bwd(w, qd, kd, pm, gx, do):
    T = w.shape[0]
    N = T // GDN_C
    C = GDN_C

    def body(w_ref, qd_ref, kd_ref, p_ref, gx_ref, do_ref, dvn_ref, ds_ref, dS_scr):
        n = pl.program_id(0)

        @pl.when(n == 0)
        def _():
            dS_scr[...] = jnp.zeros_like(dS_scr)

        sls = [slice(h * 128, (h + 1) * 128) for h in range(GDN_H)]
        dSs = [dS_scr[:, sl] for sl in sls]
        dvns = [_dot(p_ref[0, h * C:(h + 1) * C, :], do_ref[:, sl], _TN) + _dot(kd_ref[:, sl], dS2)
                for h, (sl, dS2) in enumerate(zip(sls, dSs))]
        for sl, dS2, dvn in zip(sls, dSs, dvns):
            ds_ref[0, :, sl] = dS2
            dvn_ref[:, sl] = dvn.astype(dvn_ref.dtype)
            dS_scr[:, sl] = (dS2 * jnp.exp(gx_ref[C - 1:C, sl]) + _dot(qd_ref[:, sl], do_ref[:, sl], _TN)
                             - _dot(w_ref[:, sl], dvn, _TN))

    blk = pl.BlockSpec((C, D), lambda n: (N - 1 - n, 0))
    return pl.pallas_call(
        body, grid=(N,),
        in_specs=[blk, blk, blk, pl.BlockSpec((1, GDN_H * C, C), lambda n: (N - 1 - n, 0, 0)), blk, blk],
        out_specs=[blk, pl.BlockSpec((1, GDN_DK, D), lambda n: (N - 1 - n, 0, 0))],
        out_shape=[jax.ShapeDtypeStruct((T, D), BF16), jax.ShapeDtypeStruct((N, GDN_DK, D), F32)],
        scratch_shapes=[pltpu.VMEM((GDN_DK, D), F32)], name="gdn_scan_bwd",
        compiler_params=_params(("arbitrary",)))(w, qd, kd, pm, gx, do)


def _gdn_rest_bwd(qkvn, bx, gx, s_save, t_save, vn, dvn, ds_save, do):
    T = qkvn.shape[0]
    N = T // GDN_C
    C = GDN_C

    def body(q_ref, k_ref, v_ref, bx_ref, gx_ref, ss_ref, ts_ref, vn_ref, dvn_ref, ds_ref, do_ref,
             dqkv_ref, dbx_ref, dgx_ref):
        H = range(GDN_H)
        sls = [slice(h * 128, (h + 1) * 128) for h in H]
        ts = _gdn_heads(q_ref, k_ref, v_ref, bx_ref, gx_ref)
        Ss = [ss_ref[0, :, sl] for sl in sls]
        Tms = [ts_ref[0, h * C:(h + 1) * C, :] for h in H]
        dS2s = [ds_ref[0, :, sl] for sl in sls]
        dos = [do_ref[:, sl] for sl in sls]
        vns = [vn_ref[:, sl] for sl in sls]
        dvns = [dvn_ref[:, sl] for sl in sls]
        Qs = [_dot(t["q"], t["k"], _NT) for t in ts]
        dws = [-_dot(dvn, S, _NT) for dvn, S in zip(dvns, Ss)]
        dqds = [_dot(do, S, _NT) for do, S in zip(dos, Ss)]
        dPs = [jnp.where(t["low"], _dot(do, vn, _NT), 0.0) for t, do, vn in zip(ts, dos, vns)]
        dkds = [_dot(vn, dS2, _NT) for vn, dS2 in zip(vns, dS2s)]
        dTs = [_dot(dvn, t["vb"], _NT) + _dot(dw, t["kbg"], _NT) for t, dvn, dw in zip(ts, dvns, dws)]
        dvbs = [_dot(Tm, dvn, _TN) for Tm, dvn in zip(Tms, dvns)]
        dkbgs = [_dot(Tm, dw, _TN) for Tm, dw in zip(Tms, dws)]
        TdTs = [_dot(Tm, dT, _TN) for Tm, dT in zip(Tms, dTs)]
        dLs = [jnp.where(t["strict"], -_dot(TdT, Tm, _NT), 0.0) for t, TdT, Tm in zip(ts, TdTs, Tms)]
        dMs = [dL * t["G"] for t, dL in zip(ts, dLs)]
        dQs = [dP * t["G"] for t, dP in zip(ts, dPs)]
        dkbs = [_dot(dM, t["k"]) + dkbg * t["eg"] for t, dM, dkbg in zip(ts, dMs, dkbgs)]
        rs = lambda a: jnp.sum(a, axis=1, keepdims=True)
        lane0 = _iota2((C, 128), 1) == 0
        last = _iota2((C, 1), 0) == C - 1
        for h in H:
            t, sl = ts[h], sls[h]
            E = (dLs[h] * t["M"] + dPs[h] * Qs[h]) * t["G"]
            dqkv_ref[:, sl] = _dot(dQs[h], t["k"]) + dqds[h] * t["eg"]
            dqkv_ref[:, D + h * 128:D + (h + 1) * 128] = (
                _dot(dQs[h], t["q"], _TN) + _dot(dMs[h], t["kb"], _TN) + dkds[h] * t["egl"] + dkbs[h] * t["bx"])
            dqkv_ref[:, 2 * D + h * 128:2 * D + (h + 1) * 128] = dvbs[h] * t["bx"]
            dbx_ref[:, sl] = dkbs[h] * t["k"] + dvbs[h] * t["v"]
            dkd_kd = dkds[h] * t["kd"]
            dgam_c = rs(dqds[h] * t["qd"]) + rs(dkbgs[h] * t["kbg"]) - rs(dkd_kd) + rs(E)
            dgam_r = -jnp.sum(E, axis=0, keepdims=True)
            dgam_c = dgam_c + jnp.sum(jnp.where(t["eye"], dgam_r, 0.0), axis=1, keepdims=True)
            dlast = _sum_all(dkd_kd) + t["eL"] * _sum_all(Ss[h] * dS2s[h])
            dgx_ref[:, sl] = jnp.where(lane0, dgam_c + jnp.where(last, dlast, 0.0), 0.0)

    blk = lambda c: pl.BlockSpec((C, D), lambda n: (n, c))
    st = pl.BlockSpec((1, GDN_DK, D), lambda n: (n, 0, 0))
    return pl.pallas_call(
        body, grid=(N,),
        in_specs=[blk(0), blk(1), blk(2), blk(0), blk(0), st,
                  pl.BlockSpec((1, GDN_H * C, C), lambda n: (n, 0, 0)), blk(0), blk(0), st, blk(0)],
        out_specs=[pl.BlockSpec((C, 3 * D), lambda n: (n, 0)), blk(0), blk(0)],
        out_shape=[jax.ShapeDtypeStruct((T, 3 * D), F32), jax.ShapeDtypeStruct((T, D), F32),
                   jax.ShapeDtypeStruct((T, D), F32)],
        name="gdn_rest_bwd", compiler_params=_params(("parallel",)))(
            qkvn, qkvn, qkvn, bx, gx, s_save, t_save, vn, dvn, ds_save, do)


def _ssd_seg(al_pair, half, s):
    L = SSM_L
    ri, ci = _iota2((L, L), 0), _iota2((L, L), 1)
    ac = jnp.max(jnp.where(half == s, al_pair, _NEG), axis=1, keepdims=True)
    ar = jnp.sum(jnp.where(ri == ci, ac, 0.0), axis=0, keepdims=True)
    return jnp.exp(jnp.where(ri >= ci, ac - ar, _NEG))


def _last_row(a):
    return jnp.sum(jnp.where(_iota2((a.shape[0], 1), 0) == a.shape[0] - 1, a, 0.0), axis=0, keepdims=True)


def _ssd_core_fwd(xbc, dtx, alx):
    T = xbc.shape[0]
    L = SSM_L
    Nc = T // L

    def body(x_ref, bc_ref, dt_ref, al_ref, y_ref, hs_ref, H_scr):
        c = pl.program_id(0)

        @pl.when(c == 0)
        def _():
            H_scr[...] = jnp.zeros_like(H_scr)

        half = _iota2((L, 128), 1) >> 6
        for g in range(2):
            gs = slice(g * 512, (g + 1) * 512)
            Bg = bc_ref[:, g * 128:(g + 1) * 128]
            Cg = bc_ref[:, 256 + g * 128:256 + (g + 1) * 128]
            alg = al_ref[:, gs]
            alast = _last_row(alg)
            xdt = x_ref[:, gs] * dt_ref[:, gs]
            Hg = H_scr[:, gs]
            hs_ref[0, :, gs] = Hg
            CB = _dot(Cg, Bg, _NT)
            y_ref[:, gs] = jnp.exp(alg) * _dot(Cg, Hg)
            H_scr[:, gs] = Hg * jnp.exp(alast) + _dot(Bg, jnp.exp(alast - alg) * xdt, _TN)
            for j in range(4):
                ps = slice(g * 512 + j * 128, g * 512 + (j + 1) * 128)
                al_pair = al_ref[:, ps]
                xp = x_ref[:, ps] * dt_ref[:, ps]
                ys = [_dot(_ssd_seg(al_pair, half, s) * CB, xp) for s in range(2)]
                y_ref[:, ps] += jnp.where(half == 0, ys[0], ys[1])

    row = pl.BlockSpec((L, D), lambda c: (c, 0))
    return pl.pallas_call(
        body, grid=(Nc,), in_specs=[row, pl.BlockSpec((L, 512), lambda c: (c, 2)), row, row],
        out_specs=[row, pl.BlockSpec((1, SSM_N, D), lambda c: (c, 0, 0))],
        out_shape=[jax.ShapeDtypeStruct((T, D), F32), jax.ShapeDtypeStruct((Nc, SSM_N, D), F32)],
        scratch_shapes=[pltpu.VMEM((SSM_N, D), F32)], name="ssd_core_fwd",
        compiler_params=_params(("arbitrary",)))(xbc, xbc, dtx, alx)


def _ssd_core_bwd(xbc, dtx, alx, h_save, dyy, d_x):
    T = xbc.shape[0]
    L = SSM_L
    Nc = T // L

    def body(x_ref, bc_ref, dt_ref, al_ref, hs_ref, dy_ref, d_ref, dx_ref, ddt_ref, dal_ref, dH_scr):
        c = pl.program_id(0)

        @pl.when(c == 0)
        def _():
            dH_scr[...] = jnp.zeros_like(dH_scr)

        lane = _iota2((L, 128), 1)
        half = lane >> 6
        rowi = _iota2((L, 1), 0)
        ri, ci = _iota2((L, L), 0), _iota2((L, L), 1)
        for g in range(2):
            gs = slice(g * 512, (g + 1) * 512)
            Bg = bc_ref[:, g * 128:(g + 1) * 128]
            Cg = bc_ref[:, 256 + g * 128:256 + (g + 1) * 128]
            alg = al_ref[:, gs]
            alast = _last_row(alg)
            eal, edec, eL = jnp.exp(alg), jnp.exp(alast - alg), jnp.exp(alast)
            xg, dtg, dYg = x_ref[:, gs], dt_ref[:, gs], dy_ref[:, gs]
            xdt = xg * dtg
            Hg = hs_ref[0, :, gs]
            dH2 = dH_scr[:, gs]
            CB = _dot(Cg, Bg, _NT)
            dYe = eal * dYg
            dH_scr[:, gs] = dH2 * eL + _dot(Cg, dYe, _TN)
            dC = _dot(dYe, Hg, _NT)
            zg = edec * xdt
            dz = _dot(Bg, dH2)
            dB = _dot(zg, dH2, _NT)
            tz = dz * zg
            dal = dYe * _dot(Cg, Hg) - tz
            dalast = jnp.sum(tz, axis=0, keepdims=True) + eL * jnp.sum(Hg * dH2, axis=0, keepdims=True)
            dal = dal + jnp.where(rowi == L - 1, dalast, 0.0)
            dxdt_g = edec * dz
            dx_ref[:, gs] = dxdt_g * dtg + dYg * d_ref[:, gs]
            ddt_ref[:, gs] = dxdt_g * xg
            dal_ref[:, gs] = dal
            dCB = jnp.zeros((L, L), F32)
            for j in range(4):
                ps = slice(g * 512 + j * 128, g * 512 + (j + 1) * 128)
                al_pair = al_ref[:, ps]
                xp = x_ref[:, ps] * dt_ref[:, ps]
                dYp = dy_ref[:, ps]
                dxp = []
                dal_p = jnp.zeros((L, 128), F32)
                for s in range(2):
                    seg = _ssd_seg(al_pair, half, s)
                    W = seg * CB
                    dW = jnp.where(ri >= ci, _dot(jnp.where(half == s, dYp, 0.0), xp, _NT), 0.0)
                    dxp.append(_dot(W, dYp, _TN))
                    dCB = dCB + dW * seg
                    Es = dW * W
                    dac = jnp.sum(Es, axis=1, keepdims=True) - jnp.sum(
                        jnp.where(ri == ci, jnp.sum(Es, axis=0, keepdims=True), 0.0), axis=1, keepdims=True)
                    dal_p = dal_p + jnp.where(lane == 64 * s, dac, 0.0)
                dxdt_p = jnp.where(half == 0, dxp[0], dxp[1])
                dx_ref[:, ps] += dxdt_p * dt_ref[:, ps]
                ddt_ref[:, ps] += dxdt_p * x_ref[:, ps]
                dal_ref[:, ps] += dal_p
            dx_ref[:, D + g * 128:D + (g + 1) * 128] = dB + _dot(dCB, Cg, _TN)
            dx_ref[:, D + 256 + g * 128:D + 256 + (g + 1) * 128] = dC + _dot(dCB, Bg)

    row = pl.BlockSpec((L, D), lambda c: (Nc - 1 - c, 0))
    bcs = pl.BlockSpec((L, 512), lambda c: (Nc - 1 - c, 2))
    return pl.pallas_call(
        body, grid=(Nc,),
        in_specs=[row, bcs, row, row, pl.BlockSpec((1, SSM_N, D), lambda c: (Nc - 1 - c, 0, 0)), row,
                  pl.BlockSpec((1, D), lambda c: (0, 0))],
        out_specs=[pl.BlockSpec((L, D + 512), lambda c: (Nc - 1 - c, 0)), row, row],
        out_shape=[jax.ShapeDtypeStruct((T, D + 512), F32),
                   jax.ShapeDtypeStruct((T, D), F32), jax.ShapeDtypeStruct((T, D), F32)],
        scratch_shapes=[pltpu.VMEM((SSM_N, D), F32)], name="ssd_core_bwd",
        compiler_params=_params(("arbitrary",)))(xbc, xbc, dtx, alx, h_save, dyy, d_x)


def _local_step(x, mem, tgt, W):
    T = x.shape[0]
    w_out_a, w_out_b = W["w_out"][:D], W["w_out"][D:]
    h1 = _rmsnorm_fwd(x, W["norm1_w"], name="norm1_fwd")
    p = _mm(h1, W["w_in_pad"], name="in_proj")
    qkvn = _conv_fwd(p, C_QKV, 3 * D, W["gdn_conv_w"], None, l2=True, name="gdn_conv_fwd")
    bx, gx = _gdn_gates_fwd(p, W["gdn_alog_row"], W["gdn_dtb_row"])
    u_g, w_g, qd_g, kd_g, p_g, t_save = _gdn_prep(qkvn, bx, gx)
    o_g, vn_g, s_save = _gdn_scan_fwd(u_g, w_g, qd_g, kd_g, p_g, gx)
    mixa = _gdn_post_fwd(o_g, p, W["gdn_norm_x"])
    xbc = _conv_fwd(p, C_XBC, D + 512, W["ssm_conv_w"], W["ssm_conv_b"], l2=False, name="ssm_conv_fwd")
    dtx, alx = _ssd_dt_fwd(p, W["ssm_dtb_row"], W["ssm_alog_x"])
    y_s, h_save = _ssd_core_fwd(xbc, dtx, alx)
    mixb = _ssd_post_fwd(y_s, xbc, p, W["ssm_d_x"], W["ssm_norm_w"].reshape(1, D))
    x1 = _mm(mixa, w_out_a, epi="res", extra=x, name="out_proj_a")
    x1 = _mm(mixb, w_out_b, epi="res", extra=x1, name="out_proj_b")
    h2 = _rmsnorm_fwd(x1, W["norm2_w"], name="norm2_fwd")
    qm = _mm(h2, W["wq_mem"], name="q_proj")
    m = _rmsnorm_fwd(mem, W["mem_norm_w"], name="mem_norm_fwd")
    km = _mm(m, W["wk_mem"], name="k_proj")
    vm = _mm(m, W["wv_mem"], name="v_proj")
    oa = _attn_fwd(qm, km, vm)
    x2 = _mm(oa, W["wo_mem"], epi="res", extra=x1, name="o_proj")
    h3 = _rmsnorm_fwd(x2, W["norm3_w"], name="norm3_fwd")
    u, act = _mm(h3, W["w_up"], epi="relu2", out_dtype=BF16, name="mlp_up")
    x3 = _mm(act, W["w_down"], epi="res", extra=x2, name="mlp_down")
    loss, dx3, g_final = _final_loss(x3, W["final_norm_w"], tgt)
    G = {"final_norm_w": g_final}
    dpre = _mm(dx3, W["w_down"], dims="nt", epi="mul2", extra=u, out_dtype=BF16, name="mlp_down_dx")
    G["w_down"] = _mm(act, dx3, dims="tn", out_dtype=BF16, name="mlp_down_dw")
    G["w_up"] = _mm(h3, dpre, dims="tn", out_dtype=BF16, name="mlp_up_dw")
    dh3 = _mm(dpre, W["w_up"], dims="nt", name="mlp_up_dx")
    dx2, G["norm3_w"] = _rmsnorm_bwd(x2, W["norm3_w"], dh3, dx3, name="norm3_bwd")
    do_a = _mm(dx2, W["wo_mem"], dims="nt", name="o_proj_dx")
    G["wo_mem"] = _mm(oa, dx2, dims="tn", out_dtype=BF16, name="o_proj_dw")
    dq, dk, dv = _attn_bwd(qm, km, vm, do_a)
    G["wq_mem"] = _mm(h2, dq, dims="tn", out_dtype=BF16, name="q_proj_dw")
    dh2 = _mm(dq, W["wq_mem"], dims="nt", name="q_proj_dx")
    dx1, G["norm2_w"] = _rmsnorm_bwd(x1, W["norm2_w"], dh2, dx2, name="norm2_bwd")
    G["wk_mem"] = _mm(m, dk, dims="tn", out_dtype=BF16, name="k_proj_dw")
    G["wv_mem"] = _mm(m, dv, dims="tn", out_dtype=BF16, name="v_proj_dw")
    dm = _mm(dk, W["wk_mem"], dims="nt", name="k_proj_dx")
    dm = _mm(dv, W["wv_mem"], dims="nt", epi="res", extra=dm, name="v_proj_dx")
    _, G["mem_norm_w"] = _rmsnorm_bwd(mem, W["mem_norm_w"], dm, None, name="mem_norm_bwd")
    dmixa = _mm(dx1, w_out_a, dims="nt", name="out_proj_a_dx")
    dmixb = _mm(dx1, w_out_b, dims="nt", name="out_proj_b_dx")
    G["w_out"] = jnp.concatenate([_mm(mixa, dx1, dims="tn", out_dtype=BF16, name="out_proj_a_dw"),
                                  _mm(mixb, dx1, dims="tn", out_dtype=BF16, name="out_proj_b_dw")], axis=0)
    do_g, dp, G["gdn_norm_x"] = _gdn_post_bwd(dmixa, o_g, p, W["gdn_norm_x"])
    dvn_g, ds_save = _gdn_scan_bwd(w_g, qd_g, kd_g, p_g, gx, do_g)
    dqkvn, dbx, dgx = _gdn_rest_bwd(qkvn, bx, gx, s_save, t_save, vn_g, dvn_g, ds_save, do_g)
    dy_g, G["gdn_conv_w"], _ = _conv_bwd_act(p, C_QKV, 3 * D, W["gdn_conv_w"], None, dqkvn, l2=True,
                                             name="gdn_conv_bwd_act")
    dp = _conv_bwd_in(dy_g, W["gdn_conv_w"], dp, C_QKV, T, name="gdn_conv_bwd_in")
    dp, G["gdn_alog_row"], G["gdn_dtb_row"] = _gdn_gates_bwd(p, W["gdn_alog_row"], W["gdn_dtb_row"], dbx, dgx, dp)
    dyy, dp, G["ssm_d_x"], G["ssm_norm_w"] = _ssd_post_bwd(dmixb, y_s, xbc, p, W["ssm_d_x"],
                                                          W["ssm_norm_w"].reshape(1, D), dp)
    dxbc, ddtx, dalx = _ssd_core_bwd(xbc, dtx, alx, h_save, dyy, W["ssm_d_x"])
    dy_s, G["ssm_conv_w"], G["ssm_conv_b"] = _conv_bwd_act(p, C_XBC, D + 512, W["ssm_conv_w"], W["ssm_conv_b"],
                                                           dxbc, l2=False, name="ssm_conv_bwd_act")
    dp = _conv_bwd_in(dy_s, W["ssm_conv_w"], dp, C_XBC, T, name="ssm_conv_bwd_in")
    dp, G["ssm_dtb_row"], G["ssm_alog_x"] = _ssd_dt_bwd(p, W["ssm_dtb_row"], W["ssm_alog_x"], ddtx, dalx, dp)
    dh1 = _mm(dp, W["w_in_pad"], dims="nt", name="in_proj_dx")
    G["w_in_pad"] = _mm(h1, dp, dims="tn", out_dtype=BF16, name="in_proj_dw")
    dx, G["norm1_w"] = _rmsnorm_bwd(x, W["norm1_w"], dh1, dx1, name="norm1_bwd")
    return loss, dx, G


_MESH = pl.DeviceIdType.MESH
_ANY = pl.BlockSpec(memory_space=pl.ANY)
_VM = pl.BlockSpec(memory_space=pltpu.VMEM)


def _place():
    return lax.axis_index("x"), lax.axis_index("y"), lax.axis_index("c")


def _all_gather(shards, out_dtype, *, name):
    n = len(shards)

    def body(*refs):
        x_refs, out_refs, stage = refs[:n], refs[n:2 * n], refs[2 * n:3 * n]
        send_sems, recv_sems, local_sems = refs[3 * n:]
        x, y, c = _place()
        me, sibling = (x, y, c), (x, y, 1 - c)
        chips = [(1 - x, y), (x, 1 - y), (1 - x, 1 - y)]

        def slot(px, py, pc):
            return 4 * px + 2 * py + pc

        def copy(a, k, block, to, src=None):
            dst = out_refs[a].at[slot(*block)]
            return pltpu.make_async_remote_copy(
                src_ref=dst if src is None else src, dst_ref=dst, send_sem=send_sems.at[a, k],
                recv_sem=recv_sems.at[a, k], device_id=to, device_id_type=_MESH)

        for a in range(n):
            stage[a][...] = x_refs[a][...].astype(out_dtype)
        mine = [pltpu.make_async_copy(stage[a], out_refs[a].at[slot(*me)], local_sems.at[a]) for a in range(n)]
        for cp in mine:
            cp.start()
        first = []
        for a in range(n):
            first.append(copy(a, 0, me, sibling, src=stage[a]))
            first += [copy(a, 1 + j, me, (*chip, c), src=stage[a]) for j, chip in enumerate(chips)]
        for cp in first:
            cp.start()
        passed = [[copy(a, 4 + j, (*chip, c), sibling) for j, chip in enumerate(chips)] for a in range(n)]
        for j, chip in enumerate(chips):
            for a in range(n):
                copy(a, 1 + j, (*chip, c), me).wait_recv()
                passed[a][j].start()
        for a in range(n):
            copy(a, 0, sibling, me).wait_recv()
            for j, chip in enumerate(chips):
                copy(a, 4 + j, (*chip, 1 - c), me).wait_recv()
        for cp in first + [cp for row in passed for cp in row]:
            cp.wait_send()
        for cp in mine:
            cp.wait()

    outs = pl.pallas_call(
        body, in_specs=[_VM] * n, out_specs=[_ANY] * n,
        out_shape=[jax.ShapeDtypeStruct((N_DEV,) + s.shape, out_dtype) for s in shards],
        scratch_shapes=[pltpu.VMEM(s.shape, out_dtype) for s in shards]
        + [pltpu.SemaphoreType.DMA((n, 7)), pltpu.SemaphoreType.DMA((n, 7)), pltpu.SemaphoreType.DMA((n,))],
        name=name, compiler_params=pltpu.CompilerParams(vmem_limit_bytes=VMEM_LIMIT))(*shards)
    return list(outs)


def _exchange(slabs, *, name):
    n = len(slabs)

    def body(*refs):
        in_refs, out_refs = refs[:n], refs[n:2 * n]
        send_sems, recv_sems, local_sems = refs[2 * n:]
        x, y, c = _place()
        me_slot = 4 * x + 2 * y + c
        rel = [(r >> 2 & 1, r >> 1 & 1, r & 1) for r in range(1, N_DEV)]

        def peer(r):
            rx, ry, rc = r
            return (lax.rem(x + rx, 2), lax.rem(y + ry, 2), lax.rem(c + rc, 2))

        mine = [pltpu.make_async_copy(in_refs[a].at[me_slot], out_refs[a].at[me_slot], local_sems.at[a])
                for a in range(n)]
        for cp in mine:
            cp.start()
        sends = []
        for k, r in enumerate(rel):
            px, py, pc = peer(r)
            for a in range(n):
                sends.append(pltpu.make_async_remote_copy(
                    src_ref=in_refs[a].at[4 * px + 2 * py + pc], dst_ref=out_refs[a].at[me_slot],
                    send_sem=send_sems.at[a, k], recv_sem=recv_sems.at[a, k],
                    device_id=(px, py, pc), device_id_type=_MESH))
        for cp in sends:
            cp.start()
        for k, r in enumerate(rel):
            px, py, pc = peer(r)
            for a in range(n):
                slot = out_refs[a].at[4 * px + 2 * py + pc]
                pltpu.make_async_remote_copy(
                    src_ref=slot, dst_ref=slot, send_sem=send_sems.at[a, k], recv_sem=recv_sems.at[a, k],
                    device_id=(px, py, pc), device_id_type=_MESH).wait_recv()
        for cp in sends:
            cp.wait_send()
        for cp in mine:
            cp.wait()

    outs = pl.pallas_call(
        body, in_specs=[_ANY] * n, out_specs=[_ANY] * n,
        out_shape=[jax.ShapeDtypeStruct(s.shape, s.dtype) for s in slabs],
        scratch_shapes=[pltpu.SemaphoreType.DMA((n, 7)), pltpu.SemaphoreType.DMA((n, 7)),
                        pltpu.SemaphoreType.DMA((n,))],
        name=name, compiler_params=pltpu.CompilerParams(vmem_limit_bytes=VMEM_LIMIT))(*slabs)
    return list(outs)


def _sum8(a, *, name):
    _, R, Cc = a.shape
    br = _pick_rows(R, 128)

    def body(a_ref, o_ref):
        s = a_ref[0].astype(F32)
        for k in range(1, N_DEV):
            s = s + a_ref[k].astype(F32)
        o_ref[...] = s

    return pl.pallas_call(
        body, grid=(R // br,), in_specs=[pl.BlockSpec((N_DEV, br, Cc), lambda i: (0, i, 0))],
        out_specs=pl.BlockSpec((br, Cc), lambda i: (i, 0)), out_shape=jax.ShapeDtypeStruct((R, Cc), F32),
        name=name, compiler_params=_params(("parallel",)))(a)


def _pick_rows(R, cap):
    if R <= cap:
        return R
    for d in range(cap, 7, -8):
        if R % d == 0:
            return d
    return R


def _adamw(w, g, m, v, *, name):
    shape = w.shape
    as2d = (lambda t: t.reshape(1, -1)) if w.ndim == 1 else (lambda t: t)
    w2, g2, m2, v2 = as2d(w), as2d(g), as2d(m), as2d(v)
    R, Cc = w2.shape
    br = _pick_rows(R, 256)
    c1 = 1.0 - ADAM_B1 ** ADAM_STEP
    c2 = 1.0 - ADAM_B2 ** ADAM_STEP

    def body(w_ref, g_ref, m_ref, v_ref, d_ref, nm_ref, nv_ref):
        gv = g_ref[...]
        nm = ADAM_B1 * m_ref[...] + (1.0 - ADAM_B1) * gv
        nv = ADAM_B2 * v_ref[...] + (1.0 - ADAM_B2) * (gv * gv)
        nm_ref[...] = nm
        nv_ref[...] = nv
        d_ref[...] = -ADAM_LR * ((nm / c1) / (jnp.sqrt(nv / c2) + ADAM_EPS) + ADAM_WD * w_ref[...])

    blk = pl.BlockSpec((br, Cc), lambda i: (i, 0))
    outs = pl.pallas_call(
        body, grid=(R // br,), in_specs=[blk] * 4, out_specs=[blk] * 3,
        out_shape=[jax.ShapeDtypeStruct((R, Cc), F32)] * 3, name=name,
        compiler_params=_params(("parallel",)))(w2, g2, m2, v2)
    return tuple(o.reshape(shape) for o in outs)


_BIG = ("w_in", "w_out", "wq_mem", "wk_mem", "wv_mem", "wo_mem", "w_up", "w_down")
_COL_SHARDED = ("w_in", "w_up")
_WEIGHTS = ("norm1_w", "w_in", "gdn_conv_w", "gdn_a_log", "gdn_dt_bias", "gdn_norm_w", "ssm_conv_w", "ssm_conv_b",
            "ssm_a_log", "ssm_dt_bias", "ssm_d", "ssm_norm_w", "w_out", "norm2_w", "mem_norm_w", "wq_mem", "wk_mem",
            "wv_mem", "wo_mem", "norm3_w", "w_up", "w_down", "final_norm_w")
_IN_PAD = 112


def _full_from_slots(name, g):
    if name in _COL_SHARDED:
        return jnp.transpose(g, (1, 0, 2)).reshape(g.shape[1], N_DEV * g.shape[2])
    return g.reshape(N_DEV * g.shape[1], g.shape[2])


def _slots_from_full(name, f):
    if name in _COL_SHARDED:
        return jnp.transpose(f.reshape(f.shape[0], N_DEV, f.shape[1] // N_DEV), (1, 0, 2))
    return f.reshape(N_DEV, f.shape[0] // N_DEV, f.shape[1])


def _pad_w_in(w):
    z = jnp.zeros((w.shape[0], _IN_PAD), w.dtype)
    return jnp.concatenate([w[:, :4096], w[:, 4112:6672], w[:, 4096:4112], z, w[:, 6672:6688], z], axis=1)


def _unpad_w_in(gp):
    return jnp.concatenate([gp[:, :4096], gp[:, C_GATE:C_GATE + 16], gp[:, 4096:C_GATE], gp[:, C_DT:C_DT + 16]],
                           axis=1)


def _pack_rows(vals):
    rows, offs, r = [], [], 0
    for vflat in vals:
        nrow = 8 * -(-vflat.shape[0] // 1024)
        rows.append(jnp.pad(vflat, (0, nrow * 128 - vflat.shape[0])).reshape(nrow, 128))
        offs.append((r, vflat.shape[0]))
        r += nrow
    return jnp.concatenate(rows, axis=0), offs


def _unpack_rows(packed, offs, shapes):
    out = []
    for (r, nel), shp in zip(offs, shapes):
        nrow = -(-nel // 128)
        out.append(packed[r:r + nrow].reshape(-1)[:nel].reshape(shp))
    return out


def kernel(x, mem, norm1_w, w_in, gdn_conv_w, gdn_a_log, gdn_dt_bias, gdn_norm_w, ssm_conv_w, ssm_conv_b, ssm_a_log, ssm_dt_bias, ssm_d, ssm_norm_w, w_out, norm2_w, mem_norm_w, wq_mem, wk_mem, wv_mem, wo_mem, norm3_w, w_up, w_down, final_norm_w, loss_target, m_norm1_w, m_w_in, m_gdn_conv_w, m_gdn_a_log, m_gdn_dt_bias, m_gdn_norm_w, m_ssm_conv_w, m_ssm_conv_b, m_ssm_a_log, m_ssm_dt_bias, m_ssm_d, m_ssm_norm_w, m_w_out, m_norm2_w, m_mem_norm_w, m_wq_mem, m_wk_mem, m_wv_mem, m_wo_mem, m_norm3_w, m_w_up, m_w_down, m_final_norm_w, v_norm1_w, v_w_in, v_gdn_conv_w, v_gdn_a_log, v_gdn_dt_bias, v_gdn_norm_w, v_ssm_conv_w, v_ssm_conv_b, v_ssm_a_log, v_ssm_dt_bias, v_ssm_d, v_ssm_norm_w, v_w_out, v_norm2_w, v_mem_norm_w, v_wq_mem, v_wk_mem, v_wv_mem, v_wo_mem, v_norm3_w, v_w_up, v_w_down, v_final_norm_w):
    args = dict(locals())
    w_loc = {n: args[n] for n in _WEIGHTS}
    me = 4 * lax.axis_index("x") + 2 * lax.axis_index("y") + lax.axis_index("c")

    big = _all_gather([w_loc[n] for n in _BIG], BF16, name="gather_weights")
    full = {n: _full_from_slots(n, g) for n, g in zip(_BIG, big)}
    conv_pack, conv_offs = _pack_rows([gdn_conv_w.reshape(-1), ssm_conv_w.reshape(-1)])
    conv_all = _all_gather([conv_pack], F32, name="gather_conv")[0]
    gdn_cw, ssm_cw = [], []
    for k in range(N_DEV):
        a, b = _unpack_rows(conv_all[k], conv_offs, [gdn_conv_w.shape, ssm_conv_w.shape])
        gdn_cw.append(a)
        ssm_cw.append(b)
    W = {
        "w_in_pad": _pad_w_in(full["w_in"]), "w_out": full["w_out"], "wq_mem": full["wq_mem"],
        "wk_mem": full["wk_mem"], "wv_mem": full["wv_mem"], "wo_mem": full["wo_mem"], "w_up": full["w_up"],
        "w_down": full["w_down"],
        "norm1_w": norm1_w, "norm2_w": norm2_w, "norm3_w": norm3_w, "mem_norm_w": mem_norm_w,
        "final_norm_w": final_norm_w, "ssm_norm_w": ssm_norm_w, "ssm_conv_b": ssm_conv_b,
        "gdn_conv_w": jnp.concatenate(gdn_cw, axis=1), "ssm_conv_w": jnp.concatenate(ssm_cw, axis=1),
        "gdn_alog_row": jnp.pad(gdn_a_log, (GDN_H, 128 - 2 * GDN_H)).reshape(1, 128),
        "gdn_dtb_row": jnp.pad(gdn_dt_bias, (GDN_H, 128 - 2 * GDN_H)).reshape(1, 128),
        "gdn_norm_x": jnp.tile(gdn_norm_w, GDN_H).reshape(1, D),
        "ssm_dtb_row": jnp.pad(ssm_dt_bias, (0, 128 - SSM_H)).reshape(1, 128),
        "ssm_alog_x": jnp.repeat(ssm_a_log, SSM_P).reshape(1, D),
        "ssm_d_x": jnp.repeat(ssm_d, SSM_P).reshape(1, D),
    }

    loss_part, grad_x, G = _local_step(x[0], mem[0], loss_target[0], W)

    gfull = {n: (G[n] if n != "w_in" else _unpad_w_in(G["w_in_pad"])) for n in _BIG}
    got = _exchange([_slots_from_full(n, gfull[n]) for n in _BIG], name="exchange_grads")
    grads = {n: _sum8(g, name="sum_" + n) for n, g in zip(_BIG, got)}

    small = {
        "norm1_w": G["norm1_w"], "gdn_conv_w": G["gdn_conv_w"], "gdn_a_log": G["gdn_alog_row"][0, GDN_H:2 * GDN_H],
        "gdn_dt_bias": G["gdn_dtb_row"][0, GDN_H:2 * GDN_H], "gdn_norm_w": G["gdn_norm_x"].reshape(GDN_H, 128).sum(0),
        "ssm_conv_w": G["ssm_conv_w"], "ssm_conv_b": G["ssm_conv_b"],
        "ssm_a_log": G["ssm_alog_x"].reshape(SSM_H, SSM_P).sum(1), "ssm_dt_bias": G["ssm_dtb_row"][0, :SSM_H],
        "ssm_d": G["ssm_d_x"].reshape(SSM_H, SSM_P).sum(1), "ssm_norm_w": G["ssm_norm_w"].reshape(D),
        "norm2_w": G["norm2_w"], "mem_norm_w": G["mem_norm_w"], "norm3_w": G["norm3_w"],
        "final_norm_w": G["final_norm_w"], "loss": loss_part[0, :1],
    }
    names = list(small)
    pack, offs = _pack_rows([small[n].reshape(-1) for n in names])
    tot = _sum8(_all_gather([pack], F32, name="gather_small")[0], name="sum_small")
    summed = dict(zip(names, _unpack_rows(tot, offs, [small[n].shape for n in names])))
    loss = summed.pop("loss")[0]
    for n in ("gdn_conv_w", "ssm_conv_w"):
        width = w_loc[n].shape[1]
        summed[n] = lax.dynamic_slice_in_dim(summed[n], me * width, width, axis=1)
    grads.update(summed)

    upd = {n: _adamw(w_loc[n], grads[n], args["m_" + n], args["v_" + n], name="adamw_" + n) for n in _WEIGHTS}
    return (loss, grad_x[None], *[grads[n] for n in _WEIGHTS], *[upd[n][0] for n in _WEIGHTS],
            *[upd[n][1] for n in _WEIGHTS], *[upd[n][2] for n in _WEIGHTS])
```

```python
import functools
import math

import jax
import jax.numpy as jnp
from jax import lax
from jax.experimental import pallas as pl
from jax.experimental.pallas import tpu as pltpu

F32 = jnp.float32
BF16 = jnp.bfloat16
_MXU = BF16

D = 1024
EPS = 1e-6
CONV_K = 4
GDN_H, GDN_DK, GDN_C = 8, 128, 64
SSM_H, SSM_P, SSM_L, SSM_N = 16, 64, 128, 128
MEM_H, MEM_HD = 4, 256
D_FF = 4096
N_DEV = 8

C_QKV, C_ZG, C_ZS, C_XBC, C_GATE, C_DT, C_TOT = 0, 3072, 4096, 5120, 6656, 6784, 6912

ADAM_LR, ADAM_B1, ADAM_B2, ADAM_EPS, ADAM_WD, ADAM_STEP = 0.001, 0.9, 0.999, 1e-08, 0.01, 10

VMEM_LIMIT = 56 * 1024 * 1024

_NN = (((1,), (0,)), ((), ()))
_NT = (((1,), (1,)), ((), ()))
_TN = (((0,), (0,)), ((), ()))


def _dot(a, b, dims=_NN):
    return lax.dot_general(a.astype(_MXU), b.astype(_MXU), dims, preferred_element_type=F32)


def _split3(a):
    a1 = a.astype(BF16)
    r1 = a - a1.astype(F32)
    a2 = r1.astype(BF16)
    return a1, a2, (r1 - a2.astype(F32)).astype(BF16)


def _dot_sel(a, e):
    eb = e.astype(BF16)
    return sum(lax.dot_general(p, eb, _NN, preferred_element_type=F32) for p in _split3(a))


def _sel_dot(e, a):
    eb = e.astype(BF16)
    return sum(lax.dot_general(eb, p, _NN, preferred_element_type=F32) for p in _split3(a))


def _chunk_cumsum(a, tri, chunk):
    return jnp.concatenate([_sel_dot(tri, a[r:r + chunk]) for r in range(0, a.shape[0], chunk)], axis=0)


def _params(sem):
    return pltpu.CompilerParams(dimension_semantics=sem, vmem_limit_bytes=VMEM_LIMIT)


def _pick(n, cap):
    for d in range(min(cap, n), 0, -128):
        if n % d == 0 and d % 128 == 0:
            return d
    return n


def _sigmoid(x):
    return 0.5 * jnp.tanh(0.5 * x) + 0.5


def _silu(x):
    return x * _sigmoid(x)


def _dsilu(x):
    s = _sigmoid(x)
    return s * (1.0 + x * (1.0 - s))


def _softplus(x):
    return jnp.maximum(x, 0.0) + jnp.log(1.0 + jnp.exp(-jnp.abs(x)))


def _iota2(shape, axis):
    return lax.broadcasted_iota(jnp.int32, shape, axis)


def _sum_all(x):
    return jnp.sum(jnp.sum(x, axis=1, keepdims=True), axis=0, keepdims=True)


def _mm(a, b, *, dims="nn", epi="none", extra=None, out_dtype=F32, name, bm=1024, bn_cap=1024, bk_cap=2048):
    if dims == "nn":
        (M, K), (K2, N) = a.shape, b.shape
    elif dims == "nt":
        (M, K), (N, K2) = a.shape, b.shape
    else:
        (K, M), (K2, N) = a.shape, b.shape
    assert K == K2, (a.shape, b.shape, dims)
    bm = _pick(M, bm)
    bn = _pick(N, bn_cap)
    bk = _pick(K, bk_cap)
    nk = K // bk
    dn = {"nn": _NN, "nt": _NT, "tn": _TN}[dims]
    a_spec = (pl.BlockSpec((bk, bm), lambda i, j, k: (k, i)) if dims == "tn"
              else pl.BlockSpec((bm, bk), lambda i, j, k: (i, k)))
    b_spec = (pl.BlockSpec((bn, bk), lambda i, j, k: (j, k)) if dims == "nt"
              else pl.BlockSpec((bk, bn), lambda i, j, k: (k, j)))
    o_spec = pl.BlockSpec((bm, bn), lambda i, j, k: (i, j))
    n_extra = 0 if extra is None else 1
    n_out = 2 if epi == "relu2" else 1

    def body(a_ref, b_ref, *rest):
        extra_ref = rest[0] if n_extra else None
        outs = rest[n_extra:n_extra + n_out]

        def finish(r):
            if epi == "res":
                outs[0][...] = (r + extra_ref[...].astype(F32)).astype(outs[0].dtype)
            elif epi == "mul2":
                outs[0][...] = (2.0 * r * extra_ref[...].astype(F32)).astype(outs[0].dtype)
            elif epi == "relu2":
                u = jnp.maximum(r, 0.0)
                outs[0][...] = u.astype(outs[0].dtype)
                outs[1][...] = (u * u).astype(outs[1].dtype)
            else:
                outs[0][...] = r.astype(outs[0].dtype)

        part = _dot(a_ref[...], b_ref[...], dn)
        if nk == 1:
            finish(part)
        else:
            acc = rest[-1]
            k = pl.program_id(2)

            @pl.when(k == 0)
            def _():
                acc[...] = part

            @pl.when((k > 0) & (k < nk - 1))
            def _():
                acc[...] += part

            @pl.when(k == nk - 1)
            def _():
                finish(acc[...] + part)

    ins = [a, b] + ([extra] if n_extra else [])
    in_specs = [a_spec, b_spec] + ([o_spec] if n_extra else [])
    out_shape = [jax.ShapeDtypeStruct((M, N), out_dtype) for _ in range(n_out)]
    res = pl.pallas_call(
        body, grid=(M // bm, N // bn, nk), in_specs=in_specs, out_specs=[o_spec] * n_out,
        out_shape=out_shape, scratch_shapes=[pltpu.VMEM((bm, bn), F32)] if nk > 1 else [], name=name,
        compiler_params=_params(("parallel", "parallel", "arbitrary")))(*ins)
    return res if n_out > 1 else res[0]


def _rmsnorm_fwd(x, w, *, name, bt=256):
    T, Dm = x.shape
    bt = min(bt, T)

    def body(x_ref, w_ref, h_ref):
        xv = x_ref[...]
        r = lax.rsqrt(jnp.mean(xv * xv, axis=1, keepdims=True) + EPS)
        h_ref[...] = (xv * r * w_ref[...]).astype(h_ref.dtype)

    return pl.pallas_call(
        body, grid=(T // bt,),
        in_specs=[pl.BlockSpec((bt, Dm), lambda i: (i, 0)), pl.BlockSpec((1, Dm), lambda i: (0, 0))],
        out_specs=pl.BlockSpec((bt, Dm), lambda i: (i, 0)),
        out_shape=jax.ShapeDtypeStruct((T, Dm), BF16), name=name,
        compiler_params=_params(("parallel",)))(x, w.reshape(1, Dm))


def _rmsnorm_bwd(x, w, dh, dres, *, name, bt=256):
    T, Dm = x.shape
    bt = min(bt, T)
    has_res = dres is not None

    def body(x_ref, w_ref, dh_ref, *rest):
        dres_ref = rest[0] if has_res else None
        dx_ref, dw_ref = rest[-2], rest[-1]
        i = pl.program_id(0)
        xv = x_ref[...]
        r = lax.rsqrt(jnp.mean(xv * xv, axis=1, keepdims=True) + EPS)
        xh = xv * r
        dhv = dh_ref[...].astype(F32)
        dxh = dhv * w_ref[...]
        dx = r * (dxh - xh * jnp.mean(dxh * xh, axis=1, keepdims=True))
        if has_res:
            dx = dx + dres_ref[...]
        dx_ref[...] = dx

        @pl.when(i == 0)
        def _():
            dw_ref[...] = jnp.zeros_like(dw_ref)

        dw_ref[...] += jnp.sum(dhv * xh, axis=0, keepdims=True)

    row = pl.BlockSpec((bt, Dm), lambda i: (i, 0))
    vec = pl.BlockSpec((1, Dm), lambda i: (0, 0))
    ins = [x, w.reshape(1, Dm), dh] + ([dres] if has_res else [])
    dx, dw = pl.pallas_call(
        body, grid=(T // bt,), in_specs=[row, vec, row] + ([row] if has_res else []),
        out_specs=[row, vec],
        out_shape=[jax.ShapeDtypeStruct((T, Dm), F32), jax.ShapeDtypeStruct((1, Dm), F32)],
        name=name, compiler_params=_params(("arbitrary",)))(*ins)
    return dx, dw.reshape(Dm)


def _final_loss(x, w, tgt, *, bt=256):
    T, Dm = x.shape
    bt = min(bt, T)

    def body(x_ref, w_ref, t_ref, loss_ref, dx_ref, dw_ref):
        i = pl.program_id(0)
        xv = x_ref[...]
        wv = w_ref[...]
        r = lax.rsqrt(jnp.mean(xv * xv, axis=1, keepdims=True) + EPS)
        xh = xv * r
        err = xh * wv - t_ref[...]
        part = 0.5 * jnp.sum(jnp.mean(err * err, axis=1, keepdims=True), axis=0, keepdims=True)
        dy = err * (1.0 / Dm)
        dxh = dy * wv
        dx_ref[...] = r * (dxh - xh * jnp.mean(dxh * xh, axis=1, keepdims=True))

        @pl.when(i == 0)
        def _():
            dw_ref[...] = jnp.zeros_like(dw_ref)
            loss_ref[...] = jnp.zeros_like(loss_ref)

        dw_ref[...] += jnp.sum(dy * xh, axis=0, keepdims=True)
        loss_ref[...] += jnp.broadcast_to(part, loss_ref.shape)

    row = pl.BlockSpec((bt, Dm), lambda i: (i, 0))
    vec = pl.BlockSpec((1, Dm), lambda i: (0, 0))
    loss, dx, dw = pl.pallas_call(
        body, grid=(T // bt,), in_specs=[row, vec, row],
        out_specs=[pl.BlockSpec((1, 128), lambda i: (0, 0)), row, vec],
        out_shape=[jax.ShapeDtypeStruct((1, 128), F32), jax.ShapeDtypeStruct((T, Dm), F32),
                   jax.ShapeDtypeStruct((1, Dm), F32)],
        name="final_loss", compiler_params=_params(("arbitrary",)))(x, w.reshape(1, Dm), tgt)
    return loss, dx, dw.reshape(Dm)


def _attn_fwd(q, km, vm, *, bt=256):
    T = q.shape[0]
    M = km.shape[0]
    bt = min(bt, T)
    scale = MEM_HD ** -0.5

    def body(q_ref, k_ref, v_ref, o_ref):
        for h in range(MEM_H):
            sl = slice(h * MEM_HD, (h + 1) * MEM_HD)
            s = _dot(q_ref[:, sl], k_ref[:, sl], _NT) * scale
            s = s - jnp.max(s, axis=1, keepdims=True)
            e = jnp.exp(s)
            p = e / jnp.sum(e, axis=1, keepdims=True)
            o_ref[:, sl] = _dot(p, v_ref[:, sl]).astype(o_ref.dtype)

    row = pl.BlockSpec((bt, D), lambda i: (i, 0))
    mem = pl.BlockSpec((M, D), lambda i: (0, 0))
    return pl.pallas_call(
        body, grid=(T // bt,), in_specs=[row, mem, mem], out_specs=row,
        out_shape=jax.ShapeDtypeStruct((T, D), BF16), name="attn_fwd",
        compiler_params=_params(("parallel",)))(q, km, vm)


def _attn_bwd(q, km, vm, do, *, bt=256):
    T = q.shape[0]
    M = km.shape[0]
    bt = min(bt, T)
    scale = MEM_HD ** -0.5

    def body(q_ref, k_ref, v_ref, do_ref, dq_ref, dk_ref, dv_ref):
        i = pl.program_id(0)

        @pl.when(i == 0)
        def _():
            dk_ref[...] = jnp.zeros_like(dk_ref)
            dv_ref[...] = jnp.zeros_like(dv_ref)

        for h in range(MEM_H):
            sl = slice(h * MEM_HD, (h + 1) * MEM_HD)
            qh, kh, vh, doh = q_ref[:, sl], k_ref[:, sl], v_ref[:, sl], do_ref[:, sl]
            s = _dot(qh, kh, _NT) * scale
            s = s - jnp.max(s, axis=1, keepdims=True)
            e = jnp.exp(s)
            p = e / jnp.sum(e, axis=1, keepdims=True)
            dp = _dot(doh, vh, _NT)
            ds = p * (dp - jnp.sum(dp * p, axis=1, keepdims=True)) * scale
            dq_ref[:, sl] = _dot(ds, kh)
            dk_ref[:, sl] += _dot(ds, qh, _TN)
            dv_ref[:, sl] += _dot(p, doh, _TN)

    row = pl.BlockSpec((bt, D), lambda i: (i, 0))
    mem = pl.BlockSpec((M, D), lambda i: (0, 0))
    return pl.pallas_call(
        body, grid=(T // bt,), in_specs=[row, mem, mem, row], out_specs=[row, mem, mem],
        out_shape=[jax.ShapeDtypeStruct((T, D), F32), jax.ShapeDtypeStruct((M, D), F32),
                   jax.ShapeDtypeStruct((M, D), F32)],
        name="attn_bwd", compiler_params=_params(("arbitrary",)))(q, km, vm, do)


def _conv_apply(halo, x, w_ref, b_ref):
    bt = x.shape[0]
    cat = jnp.concatenate([halo, x], axis=0)
    y = x * w_ref[3:4, :]
    for k in range(CONV_K - 1):
        y = y + pltpu.roll(cat, CONV_K - 1 - k, 0)[8:8 + bt] * w_ref[k:k + 1, :]
    if b_ref is not None:
        y = y + b_ref[...]
    return y


def _l2_parts(act, bc):
    out = []
    for s in range(bc // 128):
        a = act[:, s * 128:(s + 1) * 128]
        r = lax.rsqrt(jnp.sum(a * a, axis=1, keepdims=True) + EPS)
        out.append((a, r))
    return out


def _conv_fwd(p, col0, C, w, b, *, l2, name, bt=256, bc=512):
    T = p.shape[0]
    bt = min(bt, T)
    c0, hb = col0 // bc, bt // 8
    has_b = b is not None

    def body(x_ref, halo_ref, w_ref, *rest):
        b_ref = rest[0] if has_b else None
        o_ref = rest[-1]
        i, j = pl.program_id(0), pl.program_id(1)
        x = x_ref[...]
        halo = jnp.where(i > 0, halo_ref[...], 0.0)
        act = _silu(_conv_apply(halo, x, w_ref, b_ref))
        if l2:
            @pl.when(j < 2048 // bc)
            def _():
                sc = jnp.where(j < 1024 // bc, GDN_DK ** -0.5, 1.0)
                o_ref[...] = jnp.concatenate([a * (r * sc) for a, r in _l2_parts(act, bc)], axis=1)

            @pl.when(j >= 2048 // bc)
            def _():
                o_ref[...] = act
        else:
            o_ref[...] = act

    in_specs = [pl.BlockSpec((bt, bc), lambda i, j: (i, c0 + j)),
                pl.BlockSpec((8, bc), lambda i, j: (jnp.maximum(i * hb - 1, 0), c0 + j)),
                pl.BlockSpec((CONV_K, bc), lambda i, j: (0, j))]
    ins = [p, p, w]
    if has_b:
        in_specs.append(pl.BlockSpec((1, bc), lambda i, j: (0, j)))
        ins.append(b.reshape(1, C))
    return pl.pallas_call(
        body, grid=(T // bt, C // bc), in_specs=in_specs,
        out_specs=pl.BlockSpec((bt, bc), lambda i, j: (i, j)),
        out_shape=jax.ShapeDtypeStruct((T, C), F32), name=name,
        compiler_params=_params(("parallel", "parallel")))(*ins)


def _conv_bwd_act(p, col0, C, w, b, dact, *, l2, name, bt=256, bc=512):
    T = p.shape[0]
    bt = min(bt, T)
    c0, hb = col0 // bc, bt // 8
    has_b = b is not None

    def body(x_ref, halo_ref, w_ref, *rest):
        b_ref = rest[0] if has_b else None
        dact_ref, dy_ref, dw_ref, db_ref = rest[-4:]
        j, i = pl.program_id(0), pl.program_id(1)
        x = x_ref[...]
        halo = jnp.where(i > 0, halo_ref[...], 0.0)
        y = _conv_apply(halo, x, w_ref, b_ref)
        dact = dact_ref[...]
        if l2:
            def l2_bwd(dact):
                sc = jnp.where(j < 1024 // bc, GDN_DK ** -0.5, 1.0)
                parts = []
                for s, (a, r) in enumerate(_l2_parts(_silu(y), bc)):
                    n = a * r
                    dn = dact[:, s * 128:(s + 1) * 128]
                    parts.append((r * sc) * (dn - n * jnp.sum(dn * n, axis=1, keepdims=True)))
                return jnp.concatenate(parts, axis=1)

            dact = lax.cond(j < 2048 // bc, l2_bwd, lambda d: d, dact)
        dy = dact * _dsilu(y)
        dy_ref[...] = dy

        @pl.when(i == 0)
        def _():
            dw_ref[...] = jnp.zeros_like(dw_ref)
            db_ref[...] = jnp.zeros_like(db_ref)

        db_ref[...] += jnp.sum(dy, axis=0, keepdims=True)
        cat = jnp.concatenate([halo, x], axis=0)
        dw_ref[3:4, :] += jnp.sum(dy * x, axis=0, keepdims=True)
        for k in range(CONV_K - 1):
            xs = pltpu.roll(cat, CONV_K - 1 - k, 0)[8:8 + bt]
            dw_ref[k:k + 1, :] += jnp.sum(dy * xs, axis=0, keepdims=True)

    in_specs = [pl.BlockSpec((bt, bc), lambda j, i: (i, c0 + j)),
                pl.BlockSpec((8, bc), lambda j, i: (jnp.maximum(i * hb - 1, 0), c0 + j)),
                pl.BlockSpec((CONV_K, bc), lambda j, i: (0, j))]
    ins = [p, p, w]
    if has_b:
        in_specs.append(pl.BlockSpec((1, bc), lambda j, i: (0, j)))
        ins.append(b.reshape(1, C))
    in_specs.append(pl.BlockSpec((bt, bc), lambda j, i: (i, j)))
    ins.append(dact)
    dy, dw, db = pl.pallas_call(
        body, grid=(C // bc, T // bt), in_specs=in_specs,
        out_specs=[pl.BlockSpec((bt, bc), lambda j, i: (i, j)),
                   pl.BlockSpec((CONV_K, bc), lambda j, i: (0, j)),
                   pl.BlockSpec((1, bc), lambda j, i: (0, j))],
        out_shape=[jax.ShapeDtypeStruct((T, C), F32), jax.ShapeDtypeStruct((CONV_K, C), F32),
                   jax.ShapeDtypeStruct((1, C), F32)],
        name=name, compiler_params=_params(("parallel", "arbitrary")))(*ins)
    return dy, dw, db.reshape(C)


def _conv_bwd_in(dy, w, dp_in, col0, T, *, name, bt=256, bc=512):
    C = dy.shape[1]
    bt = min(bt, T)
    c0, hb, nb = col0 // bc, bt // 8, T // bt

    def body(dy_ref, nxt_ref, w_ref, *rest):
        o_ref = rest[-1]
        i = pl.program_id(0)
        dy_v = dy_ref[...]
        nxt = jnp.where(i < nb - 1, nxt_ref[...], 0.0)
        cat = jnp.concatenate([dy_v, nxt], axis=0)
        dx = dy_v * w_ref[3:4, :]
        for k in range(CONV_K - 1):
            s = CONV_K - 1 - k
            dx = dx + pltpu.roll(cat, bt + 8 - s, 0)[0:bt] * w_ref[k:k + 1, :]
        o_ref[...] = dx

    in_specs = [pl.BlockSpec((bt, bc), lambda i, j: (i, j)),
                pl.BlockSpec((8, bc), lambda i, j: (jnp.minimum((i + 1) * hb, T // 8 - 1), j)),
                pl.BlockSpec((CONV_K, bc), lambda i, j: (0, j))]
    ins = [dy, dy, w]
    alias = {}
    if dp_in is not None:
        in_specs.append(pl.BlockSpec(memory_space=pl.ANY))
        ins.append(dp_in)
        alias = {3: 0}
    return pl.pallas_call(
        body, grid=(nb, C // bc), in_specs=in_specs,
        out_specs=pl.BlockSpec((bt, bc), lambda i, j: (i, c0 + j)),
        out_shape=jax.ShapeDtypeStruct((T, C_TOT), F32), input_output_aliases=alias, name=name,
        compiler_params=_params(("parallel", "parallel")))(*ins)


def _expand_mats(shift, row0):
    e = (_iota2((128, D), 0) - row0 == (_iota2((128, D), 1) >> shift)).astype(F32)
    et = ((_iota2((D, 128), 0) >> shift) == _iota2((D, 128), 1) - row0).astype(F32)
    return e, et


def _cum_mats(chunk):
    ri, ci = _iota2((chunk, chunk), 0), _iota2((chunk, chunk), 1)
    return (ri >= ci).astype(F32), (ri <= ci).astype(F32)


def _gdn_gates_fwd(p, alog_row, dtb_row, *, bt=256):
    T = p.shape[0]
    bt = min(bt, T)

    def body(g_ref, al_ref, db_ref, beta_ref, gam_ref):
        gt = g_ref[...]
        eb, _ = _expand_mats(7, 0)
        eg, _ = _expand_mats(7, GDN_H)
        lc, _ = _cum_mats(GDN_C)
        beta_l = _sigmoid(gt)
        g_l = -jnp.exp(al_ref[...]) * _softplus(gt + db_ref[...])
        beta_ref[...] = _dot_sel(beta_l, eb)
        gam_ref[...] = _chunk_cumsum(_dot_sel(g_l, eg), lc, GDN_C)

    vec = pl.BlockSpec((1, 128), lambda i: (0, 0))
    row = pl.BlockSpec((bt, D), lambda i: (i, 0))
    return pl.pallas_call(
        body, grid=(T // bt,),
        in_specs=[pl.BlockSpec((bt, 128), lambda i: (i, C_GATE // 128)), vec, vec],
        out_specs=[row, row],
        out_shape=[jax.ShapeDtypeStruct((T, D), F32)] * 2, name="gdn_gates_fwd",
        compiler_params=_params(("parallel",)))(p, alog_row, dtb_row)


def _gdn_gates_bwd(p, alog_row, dtb_row, dbeta_x, dgam_x, dp_in, *, bt=256):
    T = p.shape[0]
    bt = min(bt, T)

    def body(g_ref, al_ref, db_ref, dbeta_ref, dgam_ref, dpin_ref, dg_out, dal_ref, ddb_ref):
        i = pl.program_id(0)
        gt = g_ref[...]
        _, ebt = _expand_mats(7, 0)
        _, egt = _expand_mats(7, GDN_H)
        _, uc = _cum_mats(GDN_C)
        ea = jnp.exp(al_ref[...])
        zz = gt + db_ref[...]
        g_l = -ea * _softplus(zz)
        beta_l = _sigmoid(gt)
        dg_l = _dot_sel(_chunk_cumsum(dgam_ref[...], uc, GDN_C), egt)
        dbeta_l = _dot_sel(dbeta_ref[...], ebt)
        da = dg_l * (-ea) * _sigmoid(zz)
        dg_out[...] = da + dbeta_l * beta_l * (1.0 - beta_l)

        @pl.when(i == 0)
        def _():
            dal_ref[...] = jnp.zeros_like(dal_ref)
            ddb_ref[...] = jnp.zeros_like(ddb_ref)

        dal_ref[...] += jnp.sum(dg_l * g_l, axis=0, keepdims=True)
        ddb_ref[...] += jnp.sum(da, axis=0, keepdims=True)

    vec = pl.BlockSpec((1, 128), lambda i: (0, 0))
    row = pl.BlockSpec((bt, D), lambda i: (i, 0))
    gate = pl.BlockSpec((bt, 128), lambda i: (i, C_GATE // 128))
    return pl.pallas_call(
        body, grid=(T // bt,),
        in_specs=[gate, vec, vec, row, row, pl.BlockSpec(memory_space=pl.ANY)],
        out_specs=[gate, vec, vec],
        out_shape=[jax.ShapeDtypeStruct((T, C_TOT), F32), jax.ShapeDtypeStruct((1, 128), F32),
                   jax.ShapeDtypeStruct((1, 128), F32)],
        input_output_aliases={5: 0}, name="gdn_gates_bwd",
        compiler_params=_params(("arbitrary",)))(p, alog_row, dtb_row, dbeta_x, dgam_x, dp_in)


def _ssd_dt_fwd(p, dtb_row, alog_x, *, bt=256):
    T = p.shape[0]
    bt = min(bt, T)

    def body(d_ref, db_ref, al_ref, dt_ref, alpha_ref):
        ed, _ = _expand_mats(6, 0)
        lc, _ = _cum_mats(SSM_L)
        dt_x = _dot_sel(_softplus(d_ref[...] + db_ref[...]), ed)
        dt_ref[...] = dt_x
        alpha_ref[...] = _chunk_cumsum(dt_x * (-jnp.exp(al_ref[...])), lc, SSM_L)

    row = pl.BlockSpec((bt, D), lambda i: (i, 0))
    return pl.pallas_call(
        body, grid=(T // bt,),
        in_specs=[pl.BlockSpec((bt, 128), lambda i: (i, C_DT // 128)),
                  pl.BlockSpec((1, 128), lambda i: (0, 0)), pl.BlockSpec((1, D), lambda i: (0, 0))],
        out_specs=[row, row], out_shape=[jax.ShapeDtypeStruct((T, D), F32)] * 2,
        name="ssd_dt_fwd", compiler_params=_params(("parallel",)))(p, dtb_row, alog_x)


def _ssd_dt_bwd(p, dtb_row, alog_x, ddt_x, dalpha_x, dp_in, *, bt=256):
    T = p.shape[0]
    bt = min(bt, T)

    def body(d_ref, db_ref, al_ref, ddt_ref, dal_ref, dpin_ref, dd_out, ddb_ref, dalog_ref):
        i = pl.program_id(0)
        ed, edt = _expand_mats(6, 0)
        _, uc = _cum_mats(SSM_L)
        zz = d_ref[...] + db_ref[...]
        dt_x = _dot_sel(_softplus(zz), ed)
        a_x = -jnp.exp(al_ref[...])
        da_x = _chunk_cumsum(dal_ref[...], uc, SSM_L)
        ddt_l = _dot_sel(ddt_ref[...] + da_x * a_x, edt)
        draw = ddt_l * _sigmoid(zz)
        dd_out[...] = draw

        @pl.when(i == 0)
        def _():
            ddb_ref[...] = jnp.zeros_like(ddb_ref)
            dalog_ref[...] = jnp.zeros_like(dalog_ref)

        ddb_ref[...] += jnp.sum(draw, axis=0, keepdims=True)
        dalog_ref[...] += jnp.sum(da_x * dt_x, axis=0, keepdims=True) * a_x

    row = pl.BlockSpec((bt, D), lambda i: (i, 0))
    seg = pl.BlockSpec((bt, 128), lambda i: (i, C_DT // 128))
    v128 = pl.BlockSpec((1, 128), lambda i: (0, 0))
    vD = pl.BlockSpec((1, D), lambda i: (0, 0))
    return pl.pallas_call(
        body, grid=(T // bt,),
        in_specs=[seg, v128, vD, row, row, pl.BlockSpec(memory_space=pl.ANY)],
        out_specs=[seg, v128, vD],
        out_shape=[jax.ShapeDtypeStruct((T, C_TOT), F32), jax.ShapeDtypeStruct((1, 128), F32),
                   jax.ShapeDtypeStruct((1, D), F32)],
        input_output_aliases={5: 0}, name="ssd_dt_bwd",
        compiler_params=_params(("arbitrary",)))(p, dtb_row, alog_x, ddt_x, dalpha_x, dp_in)


def _gdn_post_fwd(o, p, w_x, *, bt=256):
    T = o.shape[0]
    bt = min(bt, T)

    def body(o_ref, z_ref, w_ref, out_ref):
        for h in range(GDN_H):
            sl = slice(h * 128, (h + 1) * 128)
            oh = o_ref[:, sl]
            r = lax.rsqrt(jnp.mean(oh * oh, axis=1, keepdims=True) + EPS)
            out_ref[:, sl] = (oh * r * w_ref[:, sl] * _silu(z_ref[:, sl])).astype(out_ref.dtype)

    row = pl.BlockSpec((bt, D), lambda i: (i, 0))
    return pl.pallas_call(
        body, grid=(T // bt,),
        in_specs=[row, pl.BlockSpec((bt, D), lambda i: (i, C_ZG // D)), pl.BlockSpec((1, D), lambda i: (0, 0))],
        out_specs=row, out_shape=jax.ShapeDtypeStruct((T, D), BF16), name="gdn_post_fwd",
        compiler_params=_params(("parallel",)))(o, p, w_x)


def _gdn_post_bwd(dmix, o, p, w_x, *, bt=256):
    T = o.shape[0]
    bt = min(bt, T)

    def body(dm_ref, o_ref, z_ref, w_ref, do_ref, dz_ref, dw_ref):
        i = pl.program_id(0)

        @pl.when(i == 0)
        def _():
            dw_ref[...] = jnp.zeros_like(dw_ref)

        for h in range(GDN_H):
            sl = slice(h * 128, (h + 1) * 128)
            oh, zh, wh, dm = o_ref[:, sl], z_ref[:, sl], w_ref[:, sl], dm_ref[:, sl]
            r = lax.rsqrt(jnp.mean(oh * oh, axis=1, keepdims=True) + EPS)
            ohat = oh * r
            dy = dm * _silu(zh)
            dz_ref[:, sl] = dm * ohat * wh * _dsilu(zh)
            dohat = dy * wh
            do_ref[:, sl] = r * (dohat - ohat * jnp.mean(dohat * ohat, axis=1, keepdims=True))
            dw_ref[:, sl] += jnp.sum(dy * ohat, axis=0, keepdims=True)

    row = pl.BlockSpec((bt, D), lambda i: (i, 0))
    zcol = pl.BlockSpec((bt, D), lambda i: (i, C_ZG // D))
    vec = pl.BlockSpec((1, D), lambda i: (0, 0))
    return pl.pallas_call(
        body, grid=(T // bt,), in_specs=[row, row, zcol, vec], out_specs=[row, zcol, vec],
        out_shape=[jax.ShapeDtypeStruct((T, D), F32), jax.ShapeDtypeStruct((T, C_TOT), F32),
                   jax.ShapeDtypeStruct((1, D), F32)],
        name="gdn_post_bwd", compiler_params=_params(("arbitrary",)))(dmix, o, p, w_x)


def _ssd_post_fwd(y, xs, p, d_x, w, *, bt=256):
    T = y.shape[0]
    bt = min(bt, T)

    def body(y_ref, x_ref, z_ref, d_ref, w_ref, out_ref):
        yg = (y_ref[...] + x_ref[...] * d_ref[...]) * _silu(z_ref[...])
        for g in range(2):
            sl = slice(g * 512, (g + 1) * 512)
            a = yg[:, sl]
            r = lax.rsqrt(jnp.mean(a * a, axis=1, keepdims=True) + EPS)
            out_ref[:, sl] = (a * r * w_ref[:, sl]).astype(out_ref.dtype)

    row = pl.BlockSpec((bt, D), lambda i: (i, 0))
    vec = pl.BlockSpec((1, D), lambda i: (0, 0))
    return pl.pallas_call(
        body, grid=(T // bt,),
        in_specs=[row, row, pl.BlockSpec((bt, D), lambda i: (i, C_ZS // D)), vec, vec],
        out_specs=row, out_shape=jax.ShapeDtypeStruct((T, D), BF16), name="ssd_post_fwd",
        compiler_params=_params(("parallel",)))(y, xs, p, d_x, w)


def _ssd_post_bwd(dmix, y, xs, p, d_x, w, dp_in, *, bt=256):
    T = y.shape[0]
    bt = min(bt, T)

    def body(dm_ref, y_ref, x_ref, z_ref, d_ref, w_ref, dpin_ref, dyy_ref, dz_ref, dd_ref, dw_ref):
        i = pl.program_id(0)

        @pl.when(i == 0)
        def _():
            dd_ref[...] = jnp.zeros_like(dd_ref)
            dw_ref[...] = jnp.zeros_like(dw_ref)

        xv, zv = x_ref[...], z_ref[...]
        yy = y_ref[...] + xv * d_ref[...]
        sz = _silu(zv)
        yg = yy * sz
        parts = []
        for g in range(2):
            sl = slice(g * 512, (g + 1) * 512)
            a = yg[:, sl]
            r = lax.rsqrt(jnp.mean(a * a, axis=1, keepdims=True) + EPS)
            ah = a * r
            dout = dm_ref[:, sl]
            dah = dout * w_ref[:, sl]
            dw_ref[:, sl] += jnp.sum(dout * ah, axis=0, keepdims=True)
            parts.append(r * (dah - ah * jnp.mean(dah * ah, axis=1, keepdims=True)))
        dyg = jnp.concatenate(parts, axis=1)
        dyy = dyg * sz
        dyy_ref[...] = dyy
        dz_ref[...] = dyg * yy * _dsilu(zv)
        dd_ref[...] += jnp.sum(dyy * xv, axis=0, keepdims=True)

    row = pl.BlockSpec((bt, D), lambda i: (i, 0))
    zcol = pl.BlockSpec((bt, D), lambda i: (i, C_ZS // D))
    vec = pl.BlockSpec((1, D), lambda i: (0, 0))
    return pl.pallas_call(
        body, grid=(T // bt,),
        in_specs=[row, row, row, zcol, vec, vec, pl.BlockSpec(memory_space=pl.ANY)],
        out_specs=[row, zcol, vec, vec],
        out_shape=[jax.ShapeDtypeStruct((T, D), F32), jax.ShapeDtypeStruct((T, C_TOT), F32),
                   jax.ShapeDtypeStruct((1, D), F32), jax.ShapeDtypeStruct((1, D), F32)],
        input_output_aliases={6: 1}, name="ssd_post_bwd",
        compiler_params=_params(("arbitrary",)))(dmix, y, xs, p, d_x, w, dp_in)


_NEG = -1e30


def _gdn_terms(q, k, v, bx, gam_c):
    C = GDN_C
    ri, ci = _iota2((C, C), 0), _iota2((C, C), 1)
    eye, low, strict = ri == ci, ri >= ci, ri > ci
    gam_r = jnp.sum(jnp.where(eye, gam_c, 0.0), axis=0, keepdims=True)
    G = jnp.exp(jnp.where(low, gam_c - gam_r, _NEG))
    glast = jnp.sum(jnp.where(_iota2((C, 1), 0) == C - 1, gam_c, 0.0), axis=0, keepdims=True)
    eg, egl, eL = jnp.exp(gam_c), jnp.exp(glast - gam_c), jnp.exp(glast)
    kb, vb = k * bx, v * bx
    M = _dot(kb, k, _NT)
    return dict(eye=eye, low=low, strict=strict, G=G, eg=eg, egl=egl, eL=eL, kb=kb, vb=vb, M=M,
                kbg=kb * eg, qd=q * eg, kd=k * egl, q=q, k=k, v=v, bx=bx)


def _split(a):
    hi = a.astype(_MXU)
    return hi, (a - hi.astype(F32)).astype(_MXU)


def _dot3s(a, b):
    d = lambda p, q: lax.dot_general(p, q, _NN, preferred_element_type=F32)
    return d(a[0], b[0]) + d(a[0], b[1]) + d(a[1], b[0])


def _tri_inv_many(Ls, eye):
    eyef = jnp.where(eye, 1.0, 0.0)
    Ts = [eyef - L for L in Ls]
    Ps = [-L for L in Ls]
    for _ in range(5):
        sp = [_split(p) for p in Ps]
        Ps = [_dot3s(s, s) for s in sp]
        sp = [_split(p) for p in Ps]
        st = [_split(t) for t in Ts]
        Ts = [t + _dot3s(a, b) for t, a, b in zip(Ts, st, sp)]
    return Ts


def _gdn_heads(q_ref, k_ref, v_ref, bx_ref, gx_ref):
    out = []
    for h in range(GDN_H):
        sl = slice(h * 128, (h + 1) * 128)
        gam_c = jnp.max(gx_ref[:, sl], axis=1, keepdims=True)
        out.append(_gdn_terms(q_ref[:, sl], k_ref[:, sl], v_ref[:, sl], bx_ref[:, sl], gam_c))
    return out


def _gdn_prep(qkvn, bx, gx):
    T = qkvn.shape[0]
    N = T // GDN_C
    C = GDN_C

    def body(q_ref, k_ref, v_ref, bx_ref, gx_ref, u_ref, w_ref, qd_ref, kd_ref, p_ref, t_ref):
        ts = _gdn_heads(q_ref, k_ref, v_ref, bx_ref, gx_ref)
        Ts = _tri_inv_many([jnp.where(t["strict"], t["M"] * t["G"], 0.0) for t in ts], ts[0]["eye"])
        for h, (t, Tm) in enumerate(zip(ts, Ts)):
            sl = slice(h * 128, (h + 1) * 128)
            rows = slice(h * C, (h + 1) * C)
            u_ref[:, sl] = _dot(Tm, t["vb"])
            w_ref[:, sl] = _dot(Tm, t["kbg"]).astype(w_ref.dtype)
            qd_ref[:, sl] = t["qd"].astype(qd_ref.dtype)
            kd_ref[:, sl] = t["kd"].astype(kd_ref.dtype)
            p_ref[0, rows, :] = _dot(t["q"], t["k"], _NT) * t["G"]
            t_ref[0, rows, :] = Tm

    blk = lambda c: pl.BlockSpec((C, D), lambda n: (n, c))
    sq = pl.BlockSpec((1, GDN_H * C, C), lambda n: (n, 0, 0))
    return pl.pallas_call(
        body, grid=(N,), in_specs=[blk(0), blk(1), blk(2), blk(0), blk(0)],
        out_specs=[blk(0), blk(0), blk(0), blk(0), sq, sq],
        out_shape=[jax.ShapeDtypeStruct((T, D), F32), jax.ShapeDtypeStruct((T, D), BF16),
                   jax.ShapeDtypeStruct((T, D), BF16), jax.ShapeDtypeStruct((T, D), BF16),
                   jax.ShapeDtypeStruct((N, GDN_H * C, C), F32), jax.ShapeDtypeStruct((N, GDN_H * C, C), F32)],
        name="gdn_prep", compiler_params=_params(("parallel",)))(qkvn, qkvn, qkvn, bx, gx)


def _gdn_scan_fwd(u, w, qd, kd, pm, gx):
    T = u.shape[0]
    N = T // GDN_C
    C = GDN_C

    def body(u_ref, w_ref, qd_ref, kd_ref, p_ref, gx_ref, o_ref, vn_ref, ss_ref, S_scr):
        n = pl.program_id(0)

        @pl.when(n == 0)
        def _():
            S_scr[...] = jnp.zeros_like(S_scr)

        sls = [slice(h * 128, (h + 1) * 128) for h in range(GDN_H)]
        Ss = [S_scr[:, sl] for sl in sls]
        vns = [u_ref[:, sl] - _dot(w_ref[:, sl], S) for sl, S in zip(sls, Ss)]
        for h, (sl, S, vn) in enumerate(zip(sls, Ss, vns)):
            ss_ref[0, :, sl] = S
            vn_ref[:, sl] = vn.astype(vn_ref.dtype)
            o_ref[:, sl] = _dot(qd_ref[:, sl], S) + _dot(p_ref[0, h * C:(h + 1) * C, :], vn)
            S_scr[:, sl] = S * jnp.exp(gx_ref[C - 1:C, sl]) + _dot(kd_ref[:, sl], vn, _TN)

    blk = pl.BlockSpec((C, D), lambda n: (n, 0))
    return pl.pallas_call(
        body, grid=(N,),
        in_specs=[blk, blk, blk, blk, pl.BlockSpec((1, GDN_H * C, C), lambda n: (n, 0, 0)), blk],
        out_specs=[blk, blk, pl.BlockSpec((1, GDN_DK, D), lambda n: (n, 0, 0))],
        out_shape=[jax.ShapeDtypeStruct((T, D), F32), jax.ShapeDtypeStruct((T, D), BF16),
                   jax.ShapeDtypeStruct((N, GDN_DK, D), F32)],
        scratch_shapes=[pltpu.VMEM((GDN_DK, D), F32)], name="gdn_scan_fwd",
        compiler_params=_params(("arbitrary",)))(u, w, qd, kd, pm, gx)


def _gdn_scan_bwd(w, qd, kd, pm, gx, do):
    T = w.shape[0]
    N = T // GDN_C
    C = GDN_C

    def body(w_ref, qd_ref, kd_ref, p_ref, gx_ref, do_ref, dvn_ref, ds_ref, dS_scr):
        n = pl.program_id(0)

        @pl.when(n == 0)
        def _():
            dS_scr[...] = jnp.zeros_like(dS_scr)

        sls = [slice(h * 128, (h + 1) * 128) for h in range(GDN_H)]
        dSs = [dS_scr[:, sl] for sl in sls]
        dvns = [_dot(p_ref[0, h * C:(h + 1) * C, :], do_ref[:, sl], _TN) + _dot(kd_ref[:, sl], dS2)
                for h, (sl, dS2) in enumerate(zip(sls, dSs))]
        for sl, dS2, dvn in zip(sls, dSs, dvns):
            ds_ref[0, :, sl] = dS2
            dvn_ref[:, sl] = dvn.astype(dvn_ref.dtype)
            dS_scr[:, sl] = (dS2 * jnp.exp(gx_ref[C - 1:C, sl]) + _dot(qd_ref[:, sl], do_ref[:, sl], _TN)
                             - _dot(w_ref[:, sl], dvn, _TN))

    blk = pl.BlockSpec((C, D), lambda n: (N - 1 - n, 0))
    return pl.pallas_call(
        body, grid=(N,),
        in_specs=[blk, blk, blk, pl.BlockSpec((1, GDN_H * C, C), lambda n: (N - 1 - n, 0, 0)), blk, blk],
        out_specs=[blk, pl.BlockSpec((1, GDN_DK, D), lambda n: (N - 1 - n, 0, 0))],
        out_shape=[jax.ShapeDtypeStruct((T, D), BF16), jax.ShapeDtypeStruct((N, GDN_DK, D), F32)],
        scratch_shapes=[pltpu.VMEM((GDN_DK, D), F32)], name="gdn_scan_bwd",
        compiler_params=_params(("arbitrary",)))(w, qd, kd, pm, gx, do)


def _gdn_rest_bwd(qkvn, bx, gx, s_save, t_save, vn, dvn, ds_save, do):
    T = qkvn.shape[0]
    N = T // GDN_C
    C = GDN_C

    def body(q_ref, k_ref, v_ref, bx_ref, gx_ref, ss_ref, ts_ref, vn_ref, dvn_ref, ds_ref, do_ref,
             dqkv_ref, dbx_ref, dgx_ref):
        H = range(GDN_H)
        sls = [slice(h * 128, (h + 1) * 128) for h in H]
        ts = _gdn_heads(q_ref, k_ref, v_ref, bx_ref, gx_ref)
        Ss = [ss_ref[0, :, sl] for sl in sls]
        Tms = [ts_ref[0, h * C:(h + 1) * C, :] for h in H]
        dS2s = [ds_ref[0, :, sl] for sl in sls]
        dos = [do_ref[:, sl] for sl in sls]
        vns = [vn_ref[:, sl] for sl in sls]
        dvns = [dvn_ref[:, sl] for sl in sls]
        Qs = [_dot(t["q"], t["k"], _NT) for t in ts]
        dws = [-_dot(dvn, S, _NT) for dvn, S in zip(dvns, Ss)]
        dqds = [_dot(do, S, _NT) for do, S in zip(dos, Ss)]
        dPs = [jnp.where(t["low"], _dot(do, vn, _NT), 0.0) for t, do, vn in zip(ts, dos, vns)]
        dkds = [_dot(vn, dS2, _NT) for vn, dS2 in zip(vns, dS2s)]
        dTs = [_dot(dvn, t["vb"], _NT) + _dot(dw, t["kbg"], _NT) for t, dvn, dw in zip(ts, dvns, dws)]
        dvbs = [_dot(Tm, dvn, _TN) for Tm, dvn in zip(Tms, dvns)]
        dkbgs = [_dot(Tm, dw, _TN) for Tm, dw in zip(Tms, dws)]
        TdTs = [_dot(Tm, dT, _TN) for Tm, dT in zip(Tms, dTs)]
        dLs = [jnp.where(t["strict"], -_dot(TdT, Tm, _NT), 0.0) for t, TdT, Tm in zip(ts, TdTs, Tms)]
        dMs = [dL * t["G"] for t, dL in zip(ts, dLs)]
        dQs = [dP * t["G"] for t, dP in zip(ts, dPs)]
        dkbs = [_dot(dM, t["k"]) + dkbg * t["eg"] for t, dM, dkbg in zip(ts, dMs, dkbgs)]
        rs = lambda a: jnp.sum(a, axis=1, keepdims=True)
        lane0 = _iota2((C, 128), 1) == 0
        last = _iota2((C, 1), 0) == C - 1
        for h in H:
            t, sl = ts[h], sls[h]
            E = (dLs[h] * t["M"] + dPs[h] * Qs[h]) * t["G"]
            dqkv_ref[:, sl] = _dot(dQs[h], t["k"]) + dqds[h] * t["eg"]
            dqkv_ref[:, D + h * 128:D + (h + 1) * 128] = (
                _dot(dQs[h], t["q"], _TN) + _dot(dMs[h], t["kb"], _TN) + dkds[h] * t["egl"] + dkbs[h] * t["bx"])
            dqkv_ref[:, 2 * D + h * 128:2 * D + (h + 1) * 128] = dvbs[h] * t["bx"]
            dbx_ref[:, sl] = dkbs[h] * t["k"] + dvbs[h] * t["v"]
            dkd_kd = dkds[h] * t["kd"]
            dgam_c = rs(dqds[h] * t["qd"]) + rs(dkbgs[h] * t["kbg"]) - rs(dkd_kd) + rs(E)
            dgam_r = -jnp.sum(E, axis=0, keepdims=True)
            dgam_c = dgam_c + jnp.sum(jnp.where(t["eye"], dgam_r, 0.0), axis=1, keepdims=True)
            dlast = _sum_all(dkd_kd) + t["eL"] * _sum_all(Ss[h] * dS2s[h])
            dgx_ref[:, sl] = jnp.where(lane0, dgam_c + jnp.where(last, dlast, 0.0), 0.0)

    blk = lambda c: pl.BlockSpec((C, D), lambda n: (n, c))
    st = pl.BlockSpec((1, GDN_DK, D), lambda n: (n, 0, 0))
    return pl.pallas_call(
        body, grid=(N,),
        in_specs=[blk(0), blk(1), blk(2), blk(0), blk(0), st,
                  pl.BlockSpec((1, GDN_H * C, C), lambda n: (n, 0, 0)), blk(0), blk(0), st, blk(0)],
        out_specs=[pl.BlockSpec((C, 3 * D), lambda n: (n, 0)), blk(0), blk(0)],
        out_shape=[jax.ShapeDtypeStruct((T, 3 * D), F32), jax.ShapeDtypeStruct((T, D), F32),
                   jax.ShapeDtypeStruct((T, D), F32)],
        name="gdn_rest_bwd", compiler_params=_params(("parallel",)))(
            qkvn, qkvn, qkvn, bx, gx, s_save, t_save, vn, dvn, ds_save, do)


def _ssd_seg(al_pair, half, s):
    L = SSM_L
    ri, ci = _iota2((L, L), 0), _iota2((L, L), 1)
    ac = jnp.max(jnp.where(half == s, al_pair, _NEG), axis=1, keepdims=True)
    ar = jnp.sum(jnp.where(ri == ci, ac, 0.0), axis=0, keepdims=True)
    return jnp.exp(jnp.where(ri >= ci, ac - ar, _NEG))


def _last_row(a):
    return jnp.sum(jnp.where(_iota2((a.shape[0], 1), 0) == a.shape[0] - 1, a, 0.0), axis=0, keepdims=True)


def _ssd_core_fwd(xbc, dtx, alx):
    T = xbc.shape[0]
    L = SSM_L
    Nc = T // L

    def body(x_ref, bc_ref, dt_ref, al_ref, y_ref, hs_ref, H_scr):
        c = pl.program_id(0)

        @pl.when(c == 0)
        def _():
            H_scr[...] = jnp.zeros_like(H_scr)

        half = _iota2((L, 128), 1) >> 6
        for g in range(2):
            gs = slice(g * 512, (g + 1) * 512)
            Bg = bc_ref[:, g * 128:(g + 1) * 128]
            Cg = bc_ref[:, 256 + g * 128:256 + (g + 1) * 128]
            alg = al_ref[:, gs]
            alast = _last_row(alg)
            xdt = x_ref[:, gs] * dt_ref[:, gs]
            Hg = H_scr[:, gs]
            hs_ref[0, :, gs] = Hg
            CB = _dot(Cg, Bg, _NT)
            y_ref[:, gs] = jnp.exp(alg) * _dot(Cg, Hg)
            H_scr[:, gs] = Hg * jnp.exp(alast) + _dot(Bg, jnp.exp(alast - alg) * xdt, _TN)
            for j in range(4):
                ps = slice(g * 512 + j * 128, g * 512 + (j + 1) * 128)
                al_pair = al_ref[:, ps]
                xp = x_ref[:, ps] * dt_ref[:, ps]
                ys = [_dot(_ssd_seg(al_pair, half, s) * CB, xp) for s in range(2)]
                y_ref[:, ps] += jnp.where(half == 0, ys[0], ys[1])

    row = pl.BlockSpec((L, D), lambda c: (c, 0))
    return pl.pallas_call(
        body, grid=(Nc,), in_specs=[row, pl.BlockSpec((L, 512), lambda c: (c, 2)), row, row],
        out_specs=[row, pl.BlockSpec((1, SSM_N, D), lambda c: (c, 0, 0))],
        out_shape=[jax.ShapeDtypeStruct((T, D), F32), jax.ShapeDtypeStruct((Nc, SSM_N, D), F32)],
        scratch_shapes=[pltpu.VMEM((SSM_N, D), F32)], name="ssd_core_fwd",
        compiler_params=_params(("arbitrary",)))(xbc, xbc, dtx, alx)


def _ssd_core_bwd(xbc, dtx, alx, h_save, dyy, d_x):
    T = xbc.shape[0]
    L = SSM_L
    Nc = T // L

    def body(x_ref, bc_ref, dt_ref, al_ref, hs_ref, dy_ref, d_ref, dx_ref, ddt_ref, dal_ref, dH_scr):
        c = pl.program_id(0)

        @pl.when(c == 0)
        def _():
            dH_scr[...] = jnp.zeros_like(dH_scr)

        lane = _iota2((L, 128), 1)
        half = lane >> 6
        rowi = _iota2((L, 1), 0)
        ri, ci = _iota2((L, L), 0), _iota2((L, L), 1)
        for g in range(2):
            gs = slice(g * 512, (g + 1) * 512)
            Bg = bc_ref[:, g * 128:(g + 1) * 128]
            Cg = bc_ref[:, 256 + g * 128:256 + (g + 1) * 128]
            alg = al_ref[:, gs]
            alast = _last_row(alg)
            eal, edec, eL = jnp.exp(alg), jnp.exp(alast - alg), jnp.exp(alast)
            xg, dtg, dYg = x_ref[:, gs], dt_ref[:, gs], dy_ref[:, gs]
            xdt = xg * dtg
            Hg = hs_ref[0, :, gs]
            dH2 = dH_scr[:, gs]
            CB = _dot(Cg, Bg, _NT)
            dYe = eal * dYg
            dH_scr[:, gs] = dH2 * eL + _dot(Cg, dYe, _TN)
            dC = _dot(dYe, Hg, _NT)
            zg = edec * xdt
            dz = _dot(Bg, dH2)
            dB = _dot(zg, dH2, _NT)
            tz = dz * zg
            dal = dYe * _dot(Cg, Hg) - tz
            dalast = jnp.sum(tz, axis=0, keepdims=True) + eL * jnp.sum(Hg * dH2, axis=0, keepdims=True)
            dal = dal + jnp.where(rowi == L - 1, dalast, 0.0)
            dxdt_g = edec * dz
            dx_ref[:, gs] = dxdt_g * dtg + dYg * d_ref[:, gs]
            ddt_ref[:, gs] = dxdt_g * xg
            dal_ref[:, gs] = dal
            dCB = jnp.zeros((L, L), F32)
            for j in range(4):
                ps = slice(g * 512 + j * 128, g * 512 + (j + 1) * 128)
                al_pair = al_ref[:, ps]
                xp = x_ref[:, ps] * dt_ref[:, ps]
                dYp = dy_ref[:, ps]
                dxp = []
                dal_p = jnp.zeros((L, 128), F32)
                for s in range(2):
                    seg = _ssd_seg(al_pair, half, s)
                    W = seg * CB
                    dW = jnp.where(ri >= ci, _dot(jnp.where(half == s, dYp, 0.0), xp, _NT), 0.0)
                    dxp.append(_dot(W, dYp, _TN))
                    dCB = dCB + dW * seg
                    Es = dW * W
                    dac = jnp.sum(Es, axis=1, keepdims=True) - jnp.sum(
                        jnp.where(ri == ci, jnp.sum(Es, axis=0, keepdims=True), 0.0), axis=1, keepdims=True)
                    dal_p = dal_p + jnp.where(lane == 64 * s, dac, 0.0)
                dxdt_p = jnp.where(half == 0, dxp[0], dxp[1])
                dx_ref[:, ps] += dxdt_p * dt_ref[:, ps]
                ddt_ref[:, ps] += dxdt_p * x_ref[:, ps]
                dal_ref[:, ps] += dal_p
            dx_ref[:, D + g * 128:D + (g + 1) * 128] = dB + _dot(dCB, Cg, _TN)
            dx_ref[:, D + 256 + g * 128:D + 256 + (g + 1) * 128] = dC + _dot(dCB, Bg)

    row = pl.BlockSpec((L, D), lambda c: (Nc - 1 - c, 0))
    bcs = pl.BlockSpec((L, 512), lambda c: (Nc - 1 - c, 2))
    return pl.pallas_call(
        body, grid=(Nc,),
        in_specs=[row, bcs, row, row, pl.BlockSpec((1, SSM_N, D), lambda c: (Nc - 1 - c, 0, 0)), row,
                  pl.BlockSpec((1, D), lambda c: (0, 0))],
        out_specs=[pl.BlockSpec((L, D + 512), lambda c: (Nc - 1 - c, 0)), row, row],
        out_shape=[jax.ShapeDtypeStruct((T, D + 512), F32),
                   jax.ShapeDtypeStruct((T, D), F32), jax.ShapeDtypeStruct((T, D), F32)],
        scratch_shapes=[pltpu.VMEM((SSM_N, D), F32)], name="ssd_core_bwd",
        compiler_params=_params(("arbitrary",)))(xbc, xbc, dtx, alx, h_save, dyy, d_x)


def _local_step(x, mem, tgt, W):
    T = x.shape[0]
    w_out_a, w_out_b = W["w_out"][:D], W["w_out"][D:]
    h1 = _rmsnorm_fwd(x, W["norm1_w"], name="norm1_fwd")
    p = _mm(h1, W["w_in_pad"], name="in_proj")
    qkvn = _conv_fwd(p, C_QKV, 3 * D, W["gdn_conv_w"], None, l2=True, name="gdn_conv_fwd")
    bx, gx = _gdn_gates_fwd(p, W["gdn_alog_row"], W["gdn_dtb_row"])
    u_g, w_g, qd_g, kd_g, p_g, t_save = _gdn_prep(qkvn, bx, gx)
    o_g, vn_g, s_save = _gdn_scan_fwd(u_g, w_g, qd_g, kd_g, p_g, gx)
    mixa = _gdn_post_fwd(o_g, p, W["gdn_norm_x"])
    xbc = _conv_fwd(p, C_XBC, D + 512, W["ssm_conv_w"], W["ssm_conv_b"], l2=False, name="ssm_conv_fwd")
    dtx, alx = _ssd_dt_fwd(p, W["ssm_dtb_row"], W["ssm_alog_x"])
    y_s, h_save = _ssd_core_fwd(xbc, dtx, alx)
    mixb = _ssd_post_fwd(y_s, xbc, p, W["ssm_d_x"], W["ssm_norm_w"].reshape(1, D))
    x1 = _mm(mixa, w_out_a, epi="res", extra=x, name="out_proj_a")
    x1 = _mm(mixb, w_out_b, epi="res", extra=x1, name="out_proj_b")
    h2 = _rmsnorm_fwd(x1, W["norm2_w"], name="norm2_fwd")
    qm = _mm(h2, W["wq_mem"], name="q_proj")
    m = _rmsnorm_fwd(mem, W["mem_norm_w"], name="mem_norm_fwd")
    km = _mm(m, W["wk_mem"], name="k_proj")
    vm = _mm(m, W["wv_mem"], name="v_proj")
    oa = _attn_fwd(qm, km, vm)
    x2 = _mm(oa, W["wo_mem"], epi="res", extra=x1, name="o_proj")
    h3 = _rmsnorm_fwd(x2, W["norm3_w"], name="norm3_fwd")
    u, act = _mm(h3, W["w_up"], epi="relu2", out_dtype=BF16, name="mlp_up")
    x3 = _mm(act, W["w_down"], epi="res", extra=x2, name="mlp_down")
    loss, dx3, g_final = _final_loss(x3, W["final_norm_w"], tgt)
    G = {"final_norm_w": g_final}
    dpre = _mm(dx3, W["w_down"], dims="nt", epi="mul2", extra=u, out_dtype=BF16, name="mlp_down_dx")
    G["w_down"] = _mm(act, dx3, dims="tn", out_dtype=BF16, name="mlp_down_dw")
    G["w_up"] = _mm(h3, dpre, dims="tn", out_dtype=BF16, name="mlp_up_dw")
    dh3 = _mm(dpre, W["w_up"], dims="nt", name="mlp_up_dx")
    dx2, G["norm3_w"] = _rmsnorm_bwd(x2, W["norm3_w"], dh3, dx3, name="norm3_bwd")
    do_a = _mm(dx2, W["wo_mem"], dims="nt", name="o_proj_dx")
    G["wo_mem"] = _mm(oa, dx2, dims="tn", out_dtype=BF16, name="o_proj_dw")
    dq, dk, dv = _attn_bwd(qm, km, vm, do_a)
    G["wq_mem"] = _mm(h2, dq, dims="tn", out_dtype=BF16, name="q_proj_dw")
    dh2 = _mm(dq, W["wq_mem"], dims="nt", name="q_proj_dx")
    dx1, G["norm2_w"] = _rmsnorm_bwd(x1, W["norm2_w"], dh2, dx2, name="norm2_bwd")
    G["wk_mem"] = _mm(m, dk, dims="tn", out_dtype=BF16, name="k_proj_dw")
    G["wv_mem"] = _mm(m, dv, dims="tn", out_dtype=BF16, name="v_proj_dw")
    dm = _mm(dk, W["wk_mem"], dims="nt", name="k_proj_dx")
    dm = _mm(dv, W["wv_mem"], dims="nt", epi="res", extra=dm, name="v_proj_dx")
    _, G["mem_norm_w"] = _rmsnorm_bwd(mem, W["mem_norm_w"], dm, None, name="mem_norm_bwd")
    dmixa = _mm(dx1, w_out_a, dims="nt", name="out_proj_a_dx")
    dmixb = _mm(dx1, w_out_b, dims="nt", name="out_proj_b_dx")
    G["w_out"] = jnp.concatenate([_mm(mixa, dx1, dims="tn", out_dtype=BF16, name="out_proj_a_dw"),
                                  _mm(mixb, dx1, dims="tn", out_dtype=BF16, name="out_proj_b_dw")], axis=0)
    do_g, dp, G["gdn_norm_x"] = _gdn_post_bwd(dmixa, o_g, p, W["gdn_norm_x"])
    dvn_g, ds_save = _gdn_scan_bwd(w_g, qd_g, kd_g, p_g, gx, do_g)
    dqkvn, dbx, dgx = _gdn_rest_bwd(qkvn, bx, gx, s_save, t_save, vn_g, dvn_g, ds_save, do_g)
    dy_g, G["gdn_conv_w"], _ = _conv_bwd_act(p, C_QKV, 3 * D, W["gdn_conv_w"], None, dqkvn, l2=True,
                                             name="gdn_conv_bwd_act")
    dp = _conv_bwd_in(dy_g, W["gdn_conv_w"], dp, C_QKV, T, name="gdn_conv_bwd_in")
    dp, G["gdn_alog_row"], G["gdn_dtb_row"] = _gdn_gates_bwd(p, W["gdn_alog_row"], W["gdn_dtb_row"], dbx, dgx, dp)
    dyy, dp, G["ssm_d_x"], G["ssm_norm_w"] = _ssd_post_bwd(dmixb, y_s, xbc, p, W["ssm_d_x"],
                                                          W["ssm_norm_w"].reshape(1, D), dp)
    dxbc, ddtx, dalx = _ssd_core_bwd(xbc, dtx, alx, h_save, dyy, W["ssm_d_x"])
    dy_s, G["ssm_conv_w"], G["ssm_conv_b"] = _conv_bwd_act(p, C_XBC, D + 512, W["ssm_conv_w"], W["ssm_conv_b"],
                                                           dxbc, l2=False, name="ssm_conv_bwd_act")
    dp = _conv_bwd_in(dy_s, W["ssm_conv_w"], dp, C_XBC, T, name="ssm_conv_bwd_in")
    dp, G["ssm_dtb_row"], G["ssm_alog_x"] = _ssd_dt_bwd(p, W["ssm_dtb_row"], W["ssm_alog_x"], ddtx, dalx, dp)
    dh1 = _mm(dp, W["w_in_pad"], dims="nt", name="in_proj_dx")
    G["w_in_pad"] = _mm(h1, dp, dims="tn", out_dtype=BF16, name="in_proj_dw")
    dx, G["norm1_w"] = _rmsnorm_bwd(x, W["norm1_w"], dh1, dx1, name="norm1_bwd")
    return loss, dx, G


_MESH = pl.DeviceIdType.MESH
_ANY = pl.BlockSpec(memory_space=pl.ANY)
_VM = pl.BlockSpec(memory_space=pltpu.VMEM)


def _place():
    return lax.axis_index("x"), lax.axis_index("y"), lax.axis_index("c")


def _all_gather(shards, out_dtype, *, name):
    n = len(shards)

    def body(*refs):
        x_refs, out_refs, stage = refs[:n], refs[n:2 * n], refs[2 * n:3 * n]
        send_sems, recv_sems, local_sems = refs[3 * n:]
        x, y, c = _place()
        me, sibling = (x, y, c), (x, y, 1 - c)
        chips = [(1 - x, y), (x, 1 - y), (1 - x, 1 - y)]

        def slot(px, py, pc):
            return 4 * px + 2 * py + pc

        def copy(a, k, block, to, src=None):
            dst = out_refs[a].at[slot(*block)]
            return pltpu.make_async_remote_copy(
                src_ref=dst if src is None else src, dst_ref=dst, send_sem=send_sems.at[a, k],
                recv_sem=recv_sems.at[a, k], device_id=to, device_id_type=_MESH)

        for a in range(n):
            stage[a][...] = x_refs[a][...].astype(out_dtype)
        mine = [pltpu.make_async_copy(stage[a], out_refs[a].at[slot(*me)], local_sems.at[a]) for a in range(n)]
        for cp in mine:
            cp.start()
        first = []
        for a in range(n):
            first.append(copy(a, 0, me, sibling, src=stage[a]))
            first += [copy(a, 1 + j, me, (*chip, c), src=stage[a]) for j, chip in enumerate(chips)]
        for cp in first:
            cp.start()
        passed = [[copy(a, 4 + j, (*chip, c), sibling) for j, chip in enumerate(chips)] for a in range(n)]
        for j, chip in enumerate(chips):
            for a in range(n):
                copy(a, 1 + j, (*chip, c), me).wait_recv()
                passed[a][j].start()
        for a in range(n):
            copy(a, 0, sibling, me).wait_recv()
            for j, chip in enumerate(chips):
                copy(a, 4 + j, (*chip, 1 - c), me).wait_recv()
        for cp in first + [cp for row in passed for cp in row]:
            cp.wait_send()
        for cp in mine:
            cp.wait()

    outs = pl.pallas_call(
        body, in_specs=[_VM] * n, out_specs=[_ANY] * n,
        out_shape=[jax.ShapeDtypeStruct((N_DEV,) + s.shape, out_dtype) for s in shards],
        scratch_shapes=[pltpu.VMEM(s.shape, out_dtype) for s in shards]
        + [pltpu.SemaphoreType.DMA((n, 7)), pltpu.SemaphoreType.DMA((n, 7)), pltpu.SemaphoreType.DMA((n,))],
        name=name, compiler_params=pltpu.CompilerParams(vmem_limit_bytes=VMEM_LIMIT))(*shards)
    return list(outs)


def _exchange(slabs, *, name):
    n = len(slabs)

    def body(*refs):
        in_refs, out_refs = refs[:n], refs[n:2 * n]
        send_sems, recv_sems, local_sems = refs[2 * n:]
        x, y, c = _place()
        me_slot = 4 * x + 2 * y + c
        rel = [(r >> 2 & 1, r >> 1 & 1, r & 1) for r in range(1, N_DEV)]

        def peer(r):
            rx, ry, rc = r
            return (lax.rem(x + rx, 2), lax.rem(y + ry, 2), lax.rem(c + rc, 2))

        mine = [pltpu.make_async_copy(in_refs[a].at[me_slot], out_refs[a].at[me_slot], local_sems.at[a])
                for a in range(n)]
        for cp in mine:
            cp.start()
        sends = []
        for k, r in enumerate(rel):
            px, py, pc = peer(r)
            for a in range(n):
                sends.append(pltpu.make_async_remote_copy(
                    src_ref=in_refs[a].at[4 * px + 2 * py + pc], dst_ref=out_refs[a].at[me_slot],
                    send_sem=send_sems.at[a, k], recv_sem=recv_sems.at[a, k],
                    device_id=(px, py, pc), device_id_type=_MESH))
        for cp in sends:
            cp.start()
        for k, r in enumerate(rel):
            px, py, pc = peer(r)
            for a in range(n):
                slot = out_refs[a].at[4 * px + 2 * py + pc]
                pltpu.make_async_remote_copy(
                    src_ref=slot, dst_ref=slot, send_sem=send_sems.at[a, k], recv_sem=recv_sems.at[a, k],
                    device_id=(px, py, pc), device_id_type=_MESH).wait_recv()
        for cp in sends:
            cp.wait_send()
        for cp in mine:
            cp.wait()

    outs = pl.pallas_call(
        body, in_specs=[_ANY] * n, out_specs=[_ANY] * n,
        out_shape=[jax.ShapeDtypeStruct(s.shape, s.dtype) for s in slabs],
        scratch_shapes=[pltpu.SemaphoreType.DMA((n, 7)), pltpu.SemaphoreType.DMA((n, 7)),
                        pltpu.SemaphoreType.DMA((n,))],
        name=name, compiler_params=pltpu.CompilerParams(vmem_limit_bytes=VMEM_LIMIT))(*slabs)
    return list(outs)


def _sum8(a, *, name):
    _, R, Cc = a.shape
    br = _pick_rows(R, 128)

    def body(a_ref, o_ref):
        s = a_ref[0].astype(F32)
        for k in range(1, N_DEV):
            s = s + a_ref[k].astype(F32)
        o_ref[...] = s

    return pl.pallas_call(
        body, grid=(R // br,), in_specs=[pl.BlockSpec((N_DEV, br, Cc), lambda i: (0, i, 0))],
        out_specs=pl.BlockSpec((br, Cc), lambda i: (i, 0)), out_shape=jax.ShapeDtypeStruct((R, Cc), F32),
        name=name, compiler_params=_params(("parallel",)))(a)


def _pick_rows(R, cap):
    if R <= cap:
        return R
    for d in range(cap, 7, -8):
        if R % d == 0:
            return d
    return R


def _adamw(w, g, m, v, *, name):
    shape = w.shape
    as2d = (lambda t: t.reshape(1, -1)) if w.ndim == 1 else (lambda t: t)
    w2, g2, m2, v2 = as2d(w), as2d(g), as2d(m), as2d(v)
    R, Cc = w2.shape
    br = _pick_rows(R, 256)
    c1 = 1.0 - ADAM_B1 ** ADAM_STEP
    c2 = 1.0 - ADAM_B2 ** ADAM_STEP

    def body(w_ref, g_ref, m_ref, v_ref, d_ref, nm_ref, nv_ref):
        gv = g_ref[...]
        nm = ADAM_B1 * m_ref[...] + (1.0 - ADAM_B1) * gv
        nv = ADAM_B2 * v_ref[...] + (1.0 - ADAM_B2) * (gv * gv)
        nm_ref[...] = nm
        nv_ref[...] = nv
        d_ref[...] = -ADAM_LR * ((nm / c1) / (jnp.sqrt(nv / c2) + ADAM_EPS) + ADAM_WD * w_ref[...])

    blk = pl.BlockSpec((br, Cc), lambda i: (i, 0))
    outs = pl.pallas_call(
        body, grid=(R // br,), in_specs=[blk] * 4, out_specs=[blk] * 3,
        out_shape=[jax.ShapeDtypeStruct((R, Cc), F32)] * 3, name=name,
        compiler_params=_params(("parallel",)))(w2, g2, m2, v2)
    return tuple(o.reshape(shape) for o in outs)


_BIG = ("w_in", "w_out", "wq_mem", "wk_mem", "wv_mem", "wo_mem", "w_up", "w_down")
_COL_SHARDED = ("w_in", "w_up")
_WEIGHTS = ("norm1_w", "w_in", "gdn_conv_w", "gdn_a_log", "gdn_dt_bias", "gdn_norm_w", "ssm_conv_w", "ssm_conv_b",
            "ssm_a_log", "ssm_dt_bias", "ssm_d", "ssm_norm_w", "w_out", "norm2_w", "mem_norm_w", "wq_mem", "wk_mem",
            "wv_mem", "wo_mem", "norm3_w", "w_up", "w_down", "final_norm_w")
_IN_PAD = 112


def _full_from_slots(name, g):
    if name in _COL_SHARDED:
        return jnp.transpose(g, (1, 0, 2)).reshape(g.shape[1], N_DEV * g.shape[2])
    return g.reshape(N_DEV * g.shape[1], g.shape[2])


def _slots_from_full(name, f):
    if name in _COL_SHARDED:
        return jnp.transpose(f.reshape(f.shape[0], N_DEV, f.shape[1] // N_DEV), (1, 0, 2))
    return f.reshape(N_DEV, f.shape[0] // N_DEV, f.shape[1])


def _pad_w_in(w):
    z = jnp.zeros((w.shape[0], _IN_PAD), w.dtype)
    return jnp.concatenate([w[:, :4096], w[:, 4112:6672], w[:, 4096:4112], z, w[:, 6672:6688], z], axis=1)


def _unpad_w_in(gp):
    return jnp.concatenate([gp[:, :4096], gp[:, C_GATE:C_GATE + 16], gp[:, 4096:C_GATE], gp[:, C_DT:C_DT + 16]],
                           axis=1)


def _pack_rows(vals):
    rows, offs, r = [], [], 0
    for vflat in vals:
        nrow = 8 * -(-vflat.shape[0] // 1024)
        rows.append(jnp.pad(vflat, (0, nrow * 128 - vflat.shape[0])).reshape(nrow, 128))
        offs.append((r, vflat.shape[0]))
        r += nrow
    return jnp.concatenate(rows, axis=0), offs


def _unpack_rows(packed, offs, shapes):
    out = []
    for (r, nel), shp in zip(offs, shapes):
        nrow = -(-nel // 128)
        out.append(packed[r:r + nrow].reshape(-1)[:nel].reshape(shp))
    return out


def kernel(x, mem, norm1_w, w_in, gdn_conv_w, gdn_a_log, gdn_dt_bias, gdn_norm_w, ssm_conv_w, ssm_conv_b, ssm_a_log, ssm_dt_bias, ssm_d, ssm_norm_w, w_out, norm2_w, mem_norm_w, wq_mem, wk_mem, wv_mem, wo_mem, norm3_w, w_up, w_down, final_norm_w, loss_target, m_norm1_w, m_w_in, m_gdn_conv_w, m_gdn_a_log, m_gdn_dt_bias, m_gdn_norm_w, m_ssm_conv_w, m_ssm_conv_b, m_ssm_a_log, m_ssm_dt_bias, m_ssm_d, m_ssm_norm_w, m_w_out, m_norm2_w, m_mem_norm_w, m_wq_mem, m_wk_mem, m_wv_mem, m_wo_mem, m_norm3_w, m_w_up, m_w_down, m_final_norm_w, v_norm1_w, v_w_in, v_gdn_conv_w, v_gdn_a_log, v_gdn_dt_bias, v_gdn_norm_w, v_ssm_conv_w, v_ssm_conv_b, v_ssm_a_log, v_ssm_dt_bias, v_ssm_d, v_ssm_norm_w, v_w_out, v_norm2_w, v_mem_norm_w, v_wq_mem, v_wk_mem, v_wv_mem, v_wo_mem, v_norm3_w, v_w_up, v_w_down, v_final_norm_w):
    args = dict(locals())
    w_loc = {n: args[n] for n in _WEIGHTS}
    me = 4 * lax.axis_index("x") + 2 * lax.axis_index("y") + lax.axis_index("c")

    big = _all_gather([w_loc[n] for n in _BIG], BF16, name="gather_weights")
    full = {n: _full_from_slots(n, g) for n, g in zip(_BIG, big)}
    conv_pack, conv_offs = _pack_rows([gdn_conv_w.reshape(-1), ssm_conv_w.reshape(-1)])
    conv_all = _all_gather([conv_pack], F32, name="gather_conv")[0]
    gdn_cw, ssm_cw = [], []
    for k in range(N_DEV):
        a, b = _unpack_rows(conv_all[k], conv_offs, [gdn_conv_w.shape, ssm_conv_w.shape])
        gdn_cw.append(a)
        ssm_cw.append(b)
    W = {
        "w_in_pad": _pad_w_in(full["w_in"]), "w_out": full["w_out"], "wq_mem": full["wq_mem"],
        "wk_mem": full["wk_mem"], "wv_mem": full["wv_mem"], "wo_mem": full["wo_mem"], "w_up": full["w_up"],
        "w_down": full["w_down"],
        "norm1_w": norm1_w, "norm2_w": norm2_w, "norm3_w": norm3_w, "mem_norm_w": mem_norm_w,
        "final_norm_w": final_norm_w, "ssm_norm_w": ssm_norm_w, "ssm_conv_b": ssm_conv_b,
        "gdn_conv_w": jnp.concatenate(gdn_cw, axis=1), "ssm_conv_w": jnp.concatenate(ssm_cw, axis=1),
        "gdn_alog_row": jnp.pad(gdn_a_log, (GDN_H, 128 - 2 * GDN_H)).reshape(1, 128),
        "gdn_dtb_row": jnp.pad(gdn_dt_bias, (GDN_H, 128 - 2 * GDN_H)).reshape(1, 128),
        "gdn_norm_x": jnp.tile(gdn_norm_w, GDN_H).reshape(1, D),
        "ssm_dtb_row": jnp.pad(ssm_dt_bias, (0, 128 - SSM_H)).reshape(1, 128),
        "ssm_alog_x": jnp.repeat(ssm_a_log, SSM_P).reshape(1, D),
        "ssm_d_x": jnp.repeat(ssm_d, SSM_P).reshape(1, D),
    }

    loss_part, grad_x, G = _local_step(x[0], mem[0], loss_target[0], W)

    gfull = {n: (G[n] if n != "w_in" else _unpad_w_in(G["w_in_pad"])) for n in _BIG}
    got = _exchange([_slots_from_full(n, gfull[n]) for n in _BIG], name="exchange_grads")
    grads = {n: _sum8(g, name="sum_" + n) for n, g in zip(_BIG, got)}

    small = {
        "norm1_w": G["norm1_w"], "gdn_conv_w": G["gdn_conv_w"], "gdn_a_log": G["gdn_alog_row"][0, GDN_H:2 * GDN_H],
        "gdn_dt_bias": G["gdn_dtb_row"][0, GDN_H:2 * GDN_H], "gdn_norm_w": G["gdn_norm_x"].reshape(GDN_H, 128).sum(0),
        "ssm_conv_w": G["ssm_conv_w"], "ssm_conv_b": G["ssm_conv_b"],
        "ssm_a_log": G["ssm_alog_x"].reshape(SSM_H, SSM_P).sum(1), "ssm_dt_bias": G["ssm_dtb_row"][0, :SSM_H],
        "ssm_d": G["ssm_d_x"].reshape(SSM_H, SSM_P).sum(1), "ssm_norm_w": G["ssm_norm_w"].reshape(D),
        "norm2_w": G["norm2_w"], "mem_norm_w": G["mem_norm_w"], "norm3_w": G["norm3_w"],
        "final_norm_w": G["final_norm_w"], "loss": loss_part[0, :1],
    }
    names = list(small)
    pack, offs = _pack_rows([small[n].reshape(-1) for n in names])
    tot = _sum8(_all_gather([pack], F32, name="gather_small")[0], name="sum_small")
    summed = dict(zip(names, _unpack_rows(tot, offs, [small[n].shape for n in names])))
    loss = summed.pop("loss")[0]
    for n in ("gdn_conv_w", "ssm_conv_w"):
        width = w_loc[n].shape[1]
        summed[n] = lax.dynamic_slice_in_dim(summed[n], me * width, width, axis=1)
    grads.update(summed)

    upd = {n: _adamw(w_loc[n], grads[n], args["m_" + n], args["v_" + n], name="adamw_" + n) for n in _WEIGHTS}
    return (loss, grad_x[None], *[grads[n] for n in _WEIGHTS], *[upd[n][0] for n in _WEIGHTS],
            *[upd[n][1] for n in _WEIGHTS], *[upd[n][2] for n in _WEIGHTS])
```

```python
import functools
import math

import jax
import jax.numpy as jnp
from jax import lax
from jax.experimental import pallas as pl
from jax.experimental.pallas import tpu as pltpu

F32 = jnp.float32
BF16 = jnp.bfloat16
_MXU = BF16

D = 1024
EPS = 1e-6
CONV_K = 4
GDN_H, GDN_DK, GDN_C = 8, 128, 64
SSM_H, SSM_P, SSM_L, SSM_N = 16, 64, 128, 128
MEM_H, MEM_HD = 4, 256
D_FF = 4096
N_DEV = 8

C_QKV, C_ZG, C_ZS, C_XBC, C_GATE, C_DT, C_TOT = 0, 3072, 4096, 5120, 6656, 6784, 6912

ADAM_LR, ADAM_B1, ADAM_B2, ADAM_EPS, ADAM_WD, ADAM_STEP = 0.001, 0.9, 0.999, 1e-08, 0.01, 10

VMEM_LIMIT = 56 * 1024 * 1024

_NN = (((1,), (0,)), ((), ()))
_NT = (((1,), (1,)), ((), ()))
_TN = (((0,), (0,)), ((), ()))


def _dot(a, b, dims=_NN):
    return lax.dot_general(a.astype(_MXU), b.astype(_MXU), dims, preferred_element_type=F32)


def _split3(a):
    a1 = a.astype(BF16)
    r1 = a - a1.astype(F32)
    a2 = r1.astype(BF16)
    return a1, a2, (r1 - a2.astype(F32)).astype(BF16)


def _dot_sel(a, e):
    eb = e.astype(BF16)
    return sum(lax.dot_general(p, eb, _NN, preferred_element_type=F32) for p in _split3(a))


def _sel_dot(e, a):
    eb = e.astype(BF16)
    return sum(lax.dot_general(eb, p, _NN, preferred_element_type=F32) for p in _split3(a))


def _chunk_cumsum(a, tri, chunk):
    return jnp.concatenate([_sel_dot(tri, a[r:r + chunk]) for r in range(0, a.shape[0], chunk)], axis=0)


def _params(sem):
    return pltpu.CompilerParams(dimension_semantics=sem, vmem_limit_bytes=VMEM_LIMIT)


def _pick(n, cap):
    for d in range(min(cap, n), 0, -128):
        if n % d == 0 and d % 128 == 0:
            return d
    return n


def _sigmoid(x):
    return 0.5 * jnp.tanh(0.5 * x) + 0.5


def _silu(x):
    return x * _sigmoid(x)


def _dsilu(x):
    s = _sigmoid(x)
    return s * (1.0 + x * (1.0 - s))


def _softplus(x):
    return jnp.maximum(x, 0.0) + jnp.log(1.0 + jnp.exp(-jnp.abs(x)))


def _iota2(shape, axis):
    return lax.broadcasted_iota(jnp.int32, shape, axis)


def _sum_all(x):
    return jnp.sum(jnp.sum(x, axis=1, keepdims=True), axis=0, keepdims=True)


_MESH = pl.DeviceIdType.MESH
_ANY = pl.BlockSpec(memory_space=pl.ANY)
_VM = pl.BlockSpec(memory_space=pltpu.VMEM)
_REL = [(r >> 2 & 1, r >> 1 & 1, r & 1) for r in range(1, N_DEV)]


def _place():
    return lax.axis_index("x"), lax.axis_index("y"), lax.axis_index("c")


class _Ride:
    def __init__(self, srcs, shard):
        self.srcs, self.shard, self.n = list(srcs), shard, len(srcs)
        self.out_shape = [jax.ShapeDtypeStruct(((N_DEV,) + s.shape) if shard else s.shape, s.dtype)
                          for s in self.srcs]
        self.specs = [_ANY] * self.n
        self.scratch = [pltpu.SemaphoreType.DMA((self.n, N_DEV - 1)), pltpu.SemaphoreType.DMA((self.n, N_DEV - 1)),
                        pltpu.SemaphoreType.DMA((self.n,))]

    def _copies(self, in_refs, out_refs, sems):
        send, recv, loc = sems
        x, y, c = _place()
        me = 4 * x + 2 * y + c
        local, remote, arrive = [], [], []
        for a in range(self.n):
            src = in_refs[a] if self.shard else in_refs[a].at[me]
            local.append(pltpu.make_async_copy(src, out_refs[a].at[me], loc.at[a]))
        for k, (rx, ry, rc) in enumerate(_REL):
            peer = (lax.rem(x + rx, 2), lax.rem(y + ry, 2), lax.rem(c + rc, 2))
            ps = 4 * peer[0] + 2 * peer[1] + peer[2]
            for a in range(self.n):
                src = in_refs[a] if self.shard else in_refs[a].at[ps]
                remote.append(pltpu.make_async_remote_copy(
                    src_ref=src, dst_ref=out_refs[a].at[me], send_sem=send.at[a, k], recv_sem=recv.at[a, k],
                    device_id=peer, device_id_type=_MESH))
                slot = out_refs[a].at[ps]
                arrive.append(pltpu.make_async_remote_copy(
                    src_ref=slot, dst_ref=slot, send_sem=send.at[a, k], recv_sem=recv.at[a, k],
                    device_id=peer, device_id_type=_MESH))
        return local, remote, arrive

    def start(self, in_refs, out_refs, sems):
        local, remote, _ = self._copies(in_refs, out_refs, sems)
        for cp in local + remote:
            cp.start()

    def wait(self, in_refs, out_refs, sems):
        local, remote, arrive = self._copies(in_refs, out_refs, sems)
        for cp in arrive:
            cp.wait_recv()
        for cp in remote:
            cp.wait_send()
        for cp in local:
            cp.wait()


def _mm(a, b, *, dims="nn", epi="none", extra=None, out_dtype=F32, name, bm=1024, bn_cap=1024, bk_cap=2048,
        ride=None):
    if dims == "nn":
        (M, K), (K2, N) = a.shape, b.shape
    elif dims == "nt":
        (M, K), (N, K2) = a.shape, b.shape
    else:
        (K, M), (K2, N) = a.shape, b.shape
    assert K == K2, (a.shape, b.shape, dims)
    bm = _pick(M, bm)
    bn = _pick(N, bn_cap)
    bk = _pick(K, bk_cap)
    nk = K // bk
    dn = {"nn": _NN, "nt": _NT, "tn": _TN}[dims]
    a_spec = (pl.BlockSpec((bk, bm), lambda i, j, k: (k, i)) if dims == "tn"
              else pl.BlockSpec((bm, bk), lambda i, j, k: (i, k)))
    b_spec = (pl.BlockSpec((bn, bk), lambda i, j, k: (j, k)) if dims == "nt"
              else pl.BlockSpec((bk, bn), lambda i, j, k: (k, j)))
    o_spec = pl.BlockSpec((bm, bn), lambda i, j, k: (i, j))
    n_extra = 0 if extra is None else 1
    n_out = 2 if epi == "relu2" else 1
    n_ride = ride.n if ride else 0
    gi, gj = M // bm, N // bn

    def body(a_ref, b_ref, *rest):
        extra_ref = rest[0] if n_extra else None
        ride_in = rest[n_extra:n_extra + n_ride]
        outs = rest[n_extra + n_ride:n_extra + n_ride + n_out]
        ride_out = rest[n_extra + n_ride + n_out:n_extra + 2 * n_ride + n_out]
        if ride:
            at = lambda i, j, k: ((pl.program_id(0) == i) & (pl.program_id(1) == j) & (pl.program_id(2) == k))

            @pl.when(at(0, 0, 0))
            def _():
                ride.start(ride_in, ride_out, rest[-3:])

        def finish(r):
            if epi == "res":
                outs[0][...] = (r + extra_ref[...].astype(F32)).astype(outs[0].dtype)
            elif epi == "mul2":
                outs[0][...] = (2.0 * r * extra_ref[...].astype(F32)).astype(outs[0].dtype)
            elif epi == "relu2":
                u = jnp.maximum(r, 0.0)
                outs[0][...] = u.astype(outs[0].dtype)
                outs[1][...] = (u * u).astype(outs[1].dtype)
            else:
                outs[0][...] = r.astype(outs[0].dtype)

        part = _dot(a_ref[...], b_ref[...], dn)
        if nk == 1:
            finish(part)
        else:
            acc = rest[n_extra + 2 * n_ride + n_out]
            k = pl.program_id(2)

            @pl.when(k == 0)
            def _():
                acc[...] = part

            @pl.when((k > 0) & (k < nk - 1))
            def _():
                acc[...] += part

            @pl.when(k == nk - 1)
            def _():
                finish(acc[...] + part)

        if ride:
            @pl.when(at(gi - 1, gj - 1, nk - 1))
            def _():
                ride.wait(ride_in, ride_out, rest[-3:])

    ins = [a, b] + ([extra] if n_extra else [])
    in_specs = [a_spec, b_spec] + ([o_spec] if n_extra else [])
    out_shape = [jax.ShapeDtypeStruct((M, N), out_dtype) for _ in range(n_out)]
    out_specs = [o_spec] * n_out
    scratch = [pltpu.VMEM((bm, bn), F32)] if nk > 1 else []
    sem = ("parallel", "parallel", "arbitrary")
    if ride:
        ins, in_specs = ins + ride.srcs, in_specs + ride.specs
        out_shape, out_specs = out_shape + ride.out_shape, out_specs + ride.specs
        scratch, sem = scratch + ride.scratch, ("arbitrary",) * 3
    res = pl.pallas_call(
        body, grid=(gi, gj, nk), in_specs=in_specs, out_specs=out_specs, out_shape=out_shape,
        scratch_shapes=scratch, name=name, compiler_params=_params(sem))(*ins)
    main = res[:n_out] if n_out > 1 else res[0]
    return (main, list(res[n_out:])) if ride else main


def _rmsnorm_fwd(x, w, *, name, bt=256):
    T, Dm = x.shape
    bt = min(bt, T)

    def body(x_ref, w_ref, h_ref):
        xv = x_ref[...]
        r = lax.rsqrt(jnp.mean(xv * xv, axis=1, keepdims=True) + EPS)
        h_ref[...] = (xv * r * w_ref[...]).astype(h_ref.dtype)

    return pl.pallas_call(
        body, grid=(T // bt,),
        in_specs=[pl.BlockSpec((bt, Dm), lambda i: (i, 0)), pl.BlockSpec((1, Dm), lambda i: (0, 0))],
        out_specs=pl.BlockSpec((bt, Dm), lambda i: (i, 0)),
        out_shape=jax.ShapeDtypeStruct((T, Dm), BF16), name=name,
        compiler_params=_params(("parallel",)))(x, w.reshape(1, Dm))


def _rmsnorm_bwd(x, w, dh, dres, *, name, bt=256):
    T, Dm = x.shape
    bt = min(bt, T)
    has_res = dres is not None

    def body(x_ref, w_ref, dh_ref, *rest):
        dres_ref = rest[0] if has_res else None
        dx_ref, dw_ref = rest[-2], rest[-1]
        i = pl.program_id(0)
        xv = x_ref[...]
        r = lax.rsqrt(jnp.mean(xv * xv, axis=1, keepdims=True) + EPS)
        xh = xv * r
        dhv = dh_ref[...].astype(F32)
        dxh = dhv * w_ref[...]
        dx = r * (dxh - xh * jnp.mean(dxh * xh, axis=1, keepdims=True))
        if has_res:
            dx = dx + dres_ref[...]
        dx_ref[...] = dx

        @pl.when(i == 0)
        def _():
            dw_ref[...] = jnp.zeros_like(dw_ref)

        dw_ref[...] += jnp.sum(dhv * xh, axis=0, keepdims=True)

    row = pl.BlockSpec((bt, Dm), lambda i: (i, 0))
    vec = pl.BlockSpec((1, Dm), lambda i: (0, 0))
    ins = [x, w.reshape(1, Dm), dh] + ([dres] if has_res else [])
    dx, dw = pl.pallas_call(
        body, grid=(T // bt,), in_specs=[row, vec, row] + ([row] if has_res else []),
        out_specs=[row, vec],
        out_shape=[jax.ShapeDtypeStruct((T, Dm), F32), jax.ShapeDtypeStruct((1, Dm), F32)],
        name=name, compiler_params=_params(("arbitrary",)))(*ins)
    return dx, dw.reshape(Dm)


def _final_loss(x, w, tgt, *, bt=256):
    T, Dm = x.shape
    bt = min(bt, T)

    def body(x_ref, w_ref, t_ref, loss_ref, dx_ref, dw_ref):
        i = pl.program_id(0)
        xv = x_ref[...]
        wv = w_ref[...]
        r = lax.rsqrt(jnp.mean(xv * xv, axis=1, keepdims=True) + EPS)
        xh = xv * r
        err = xh * wv - t_ref[...]
        part = 0.5 * jnp.sum(jnp.mean(err * err, axis=1, keepdims=True), axis=0, keepdims=True)
        dy = err * (1.0 / Dm)
        dxh = dy * wv
        dx_ref[...] = r * (dxh - xh * jnp.mean(dxh * xh, axis=1, keepdims=True))

        @pl.when(i == 0)
        def _():
            dw_ref[...] = jnp.zeros_like(dw_ref)
            loss_ref[...] = jnp.zeros_like(loss_ref)

        dw_ref[...] += jnp.sum(dy * xh, axis=0, keepdims=True)
        loss_ref[...] += jnp.broadcast_to(part, loss_ref.shape)

    row = pl.BlockSpec((bt, Dm), lambda i: (i, 0))
    vec = pl.BlockSpec((1, Dm), lambda i: (0, 0))
    loss, dx, dw = pl.pallas_call(
        body, grid=(T // bt,), in_specs=[row, vec, row],
        out_specs=[pl.BlockSpec((1, 128), lambda i: (0, 0)), row, vec],
        out_shape=[jax.ShapeDtypeStruct((1, 128), F32), jax.ShapeDtypeStruct((T, Dm), F32),
                   jax.ShapeDtypeStruct((1, Dm), F32)],
        name="final_loss", compiler_params=_params(("arbitrary",)))(x, w.reshape(1, Dm), tgt)
    return loss, dx, dw.reshape(Dm)


def _attn_fwd(q, km, vm, *, bt=256):
    T = q.shape[0]
    M = km.shape[0]
    bt = min(bt, T)
    scale = MEM_HD ** -0.5

    def body(q_ref, k_ref, v_ref, o_ref):
        for h in range(MEM_H):
            sl = slice(h * MEM_HD, (h + 1) * MEM_HD)
            s = _dot(q_ref[:, sl], k_ref[:, sl], _NT) * scale
            s = s - jnp.max(s, axis=1, keepdims=True)
            e = jnp.exp(s)
            p = e / jnp.sum(e, axis=1, keepdims=True)
            o_ref[:, sl] = _dot(p, v_ref[:, sl]).astype(o_ref.dtype)

    row = pl.BlockSpec((bt, D), lambda i: (i, 0))
    mem = pl.BlockSpec((M, D), lambda i: (0, 0))
    return pl.pallas_call(
        body, grid=(T // bt,), in_specs=[row, mem, mem], out_specs=row,
        out_shape=jax.ShapeDtypeStruct((T, D), BF16), name="attn_fwd",
        compiler_params=_params(("parallel",)))(q, km, vm)


def _attn_bwd(q, km, vm, do, *, bt=256):
    T = q.shape[0]
    M = km.shape[0]
    bt = min(bt, T)
    scale = MEM_HD ** -0.5

    def body(q_ref, k_ref, v_ref, do_ref, dq_ref, dk_ref, dv_ref):
        i = pl.program_id(0)

        @pl.when(i == 0)
        def _():
            dk_ref[...] = jnp.zeros_like(dk_ref)
            dv_ref[...] = jnp.zeros_like(dv_ref)

        for h in range(MEM_H):
            sl = slice(h * MEM_HD, (h + 1) * MEM_HD)
            qh, kh, vh, doh = q_ref[:, sl], k_ref[:, sl], v_ref[:, sl], do_ref[:, sl]
            s = _dot(qh, kh, _NT) * scale
            s = s - jnp.max(s, axis=1, keepdims=True)
            e = jnp.exp(s)
            p = e / jnp.sum(e, axis=1, keepdims=True)
            dp = _dot(doh, vh, _NT)
            ds = p * (dp - jnp.sum(dp * p, axis=1, keepdims=True)) * scale
            dq_ref[:, sl] = _dot(ds, kh)
            dk_ref[:, sl] += _dot(ds, qh, _TN)
            dv_ref[:, sl] += _dot(p, doh, _TN)

    row = pl.BlockSpec((bt, D), lambda i: (i, 0))
    mem = pl.BlockSpec((M, D), lambda i: (0, 0))
    return pl.pallas_call(
        body, grid=(T // bt,), in_specs=[row, mem, mem, row], out_specs=[row, mem, mem],
        out_shape=[jax.ShapeDtypeStruct((T, D), F32), jax.ShapeDtypeStruct((M, D), F32),
                   jax.ShapeDtypeStruct((M, D), F32)],
        name="attn_bwd", compiler_params=_params(("arbitrary",)))(q, km, vm, do)


def _conv_apply(halo, x, w_ref, b_ref):
    bt = x.shape[0]
    cat = jnp.concatenate([halo, x], axis=0)
    y = x * w_ref[3:4, :]
    for k in range(CONV_K - 1):
        y = y + pltpu.roll(cat, CONV_K - 1 - k, 0)[8:8 + bt] * w_ref[k:k + 1, :]
    if b_ref is not None:
        y = y + b_ref[...]
    return y


def _l2_parts(act, bc):
    out = []
    for s in range(bc // 128):
        a = act[:, s * 128:(s + 1) * 128]
        r = lax.rsqrt(jnp.sum(a * a, axis=1, keepdims=True) + EPS)
        out.append((a, r))
    return out


def _conv_fwd(p, col0, C, w, b, *, l2, name, bt=512, bc=1024):
    T = p.shape[0]
    bt = min(bt, T)
    c0, hb = col0 // bc, bt // 8
    has_b = b is not None
    assert not l2 or (bc == D and C == 2 * D)

    def body(x_ref, halo_ref, w_ref, *rest):
        b_ref = rest[0] if has_b else None
        o_ref = rest[-1]
        i, j = pl.program_id(0), pl.program_id(1)
        x = x_ref[...]
        halo = jnp.where(i > 0, halo_ref[...], 0.0)
        act = _silu(_conv_apply(halo, x, w_ref, b_ref))
        if l2:
            sc = jnp.where(j == 0, GDN_DK ** -0.5, 1.0)
            o_ref[...] = jnp.concatenate([a * (r * sc) for a, r in _l2_parts(act, bc)], axis=1)
        else:
            o_ref[...] = act

    in_specs = [pl.BlockSpec((bt, bc), lambda i, j: (i, c0 + j)),
                pl.BlockSpec((8, bc), lambda i, j: (jnp.maximum(i * hb - 1, 0), c0 + j)),
                pl.BlockSpec((CONV_K, bc), lambda i, j: (0, j))]
    ins = [p, p, w]
    if has_b:
        in_specs.append(pl.BlockSpec((1, bc), lambda i, j: (0, j)))
        ins.append(b.reshape(1, C))
    return pl.pallas_call(
        body, grid=(T // bt, C // bc), in_specs=in_specs,
        out_specs=pl.BlockSpec((bt, bc), lambda i, j: (i, j)),
        out_shape=jax.ShapeDtypeStruct((T, C), F32), name=name,
        compiler_params=_params(("parallel", "parallel")))(*ins)


def _conv_bwd_act(p, col0, C, w, b, dact, dcol0, *, l2, name, bt=512, bc=1024):
    T = p.shape[0]
    bt = min(bt, T)
    c0, d0, hb = col0 // bc, dcol0 // bc, bt // 8
    has_b = b is not None
    assert not l2 or (bc == D and C == 2 * D)

    def body(x_ref, halo_ref, w_ref, *rest):
        b_ref = rest[0] if has_b else None
        dact_ref, dy_ref, dw_ref, db_ref = rest[-4:]
        j, i = pl.program_id(0), pl.program_id(1)
        x = x_ref[...]
        halo = jnp.where(i > 0, halo_ref[...], 0.0)
        y = _conv_apply(halo, x, w_ref, b_ref)
        dact = dact_ref[...]
        if l2:
            sc = jnp.where(j == 0, GDN_DK ** -0.5, 1.0)
            parts = []
            for s, (a, r) in enumerate(_l2_parts(_silu(y), bc)):
                n = a * r
                dn = dact[:, s * 128:(s + 1) * 128]
                parts.append((r * sc) * (dn - n * jnp.sum(dn * n, axis=1, keepdims=True)))
            dact = jnp.concatenate(parts, axis=1)
        dy = dact * _dsilu(y)
        dy_ref[...] = dy

        @pl.when(i == 0)
        def _():
            dw_ref[...] = jnp.zeros_like(dw_ref)
            db_ref[...] = jnp.zeros_like(db_ref)

        db_ref[...] += jnp.sum(dy, axis=0, keepdims=True)
        cat = jnp.concatenate([halo, x], axis=0)
        dw_ref[3:4, :] += jnp.sum(dy * x, axis=0, keepdims=True)
        for k in range(CONV_K - 1):
            xs = pltpu.roll(cat, CONV_K - 1 - k, 0)[8:8 + bt]
            dw_ref[k:k + 1, :] += jnp.sum(dy * xs, axis=0, keepdims=True)

    in_specs = [pl.BlockSpec((bt, bc), lambda j, i: (i, c0 + j)),
                pl.BlockSpec((8, bc), lambda j, i: (jnp.maximum(i * hb - 1, 0), c0 + j)),
                pl.BlockSpec((CONV_K, bc), lambda j, i: (0, j))]
    ins = [p, p, w]
    if has_b:
        in_specs.append(pl.BlockSpec((1, bc), lambda j, i: (0, j)))
        ins.append(b.reshape(1, C))
    in_specs.append(pl.BlockSpec((bt, bc), lambda j, i: (i, d0 + j)))
    ins.append(dact)
    dy, dw, db = pl.pallas_call(
        body, grid=(C // bc, T // bt), in_specs=in_specs,
        out_specs=[pl.BlockSpec((bt, bc), lambda j, i: (i, j)),
                   pl.BlockSpec((CONV_K, bc), lambda j, i: (0, j)),
                   pl.BlockSpec((1, bc), lambda j, i: (0, j))],
        out_shape=[jax.ShapeDtypeStruct((T, C), F32), jax.ShapeDtypeStruct((CONV_K, C), F32),
                   jax.ShapeDtypeStruct((1, C), F32)],
        name=name, compiler_params=_params(("parallel", "arbitrary")))(*ins)
    return dy, dw, db.reshape(C)


def _conv_bwd_in(dy, w, dp_in, col0, T, *, name, bt=512, bc=1024):
    C = dy.shape[1]
    bt = min(bt, T)
    c0, hb, nb = col0 // bc, bt // 8, T // bt

    def body(dy_ref, nxt_ref, w_ref, *rest):
        o_ref = rest[-1]
        i = pl.program_id(0)
        dy_v = dy_ref[...]
        nxt = jnp.where(i < nb - 1, nxt_ref[...], 0.0)
        cat = jnp.concatenate([dy_v, nxt], axis=0)
        dx = dy_v * w_ref[3:4, :]
        for k in range(CONV_K - 1):
            s = CONV_K - 1 - k
            dx = dx + pltpu.roll(cat, bt + 8 - s, 0)[0:bt] * w_ref[k:k + 1, :]
        o_ref[...] = dx

    in_specs = [pl.BlockSpec((bt, bc), lambda i, j: (i, j)),
                pl.BlockSpec((8, bc), lambda i, j: (jnp.minimum((i + 1) * hb, T // 8 - 1), j)),
                pl.BlockSpec((CONV_K, bc), lambda i, j: (0, j))]
    ins = [dy, dy, w]
    alias = {}
    if dp_in is not None:
        in_specs.append(pl.BlockSpec(memory_space=pl.ANY))
        ins.append(dp_in)
        alias = {3: 0}
    return pl.pallas_call(
        body, grid=(nb, C // bc), in_specs=in_specs,
        out_specs=pl.BlockSpec((bt, bc), lambda i, j: (i, c0 + j)),
        out_shape=jax.ShapeDtypeStruct((T, C_TOT), F32), input_output_aliases=alias, name=name,
        compiler_params=_params(("parallel", "parallel")))(*ins)


def _expand_mats(shift, row0):
    e = (_iota2((128, D), 0) - row0 == (_iota2((128, D), 1) >> shift)).astype(F32)
    et = ((_iota2((D, 128), 0) >> shift) == _iota2((D, 128), 1) - row0).astype(F32)
    return e, et


def _cum_mats(chunk):
    ri, ci = _iota2((chunk, chunk), 0), _iota2((chunk, chunk), 1)
    return (ri >= ci).astype(F32), (ri <= ci).astype(F32)


def _gdn_gates_fwd(p, alog_row, dtb_row, *, bt=256):
    T = p.shape[0]
    bt = min(bt, T)

    def body(g_ref, al_ref, db_ref, beta_ref, gam_ref):
        gt = g_ref[...]
        eb, _ = _expand_mats(7, 0)
        eg, _ = _expand_mats(7, GDN_H)
        lc, _ = _cum_mats(GDN_C)
        beta_l = _sigmoid(gt)
        g_l = -jnp.exp(al_ref[...]) * _softplus(gt + db_ref[...])
        beta_ref[...] = _dot_sel(beta_l, eb)
        gam_ref[...] = _chunk_cumsum(_dot_sel(g_l, eg), lc, GDN_C)

    vec = pl.BlockSpec((1, 128), lambda i: (0, 0))
    row = pl.BlockSpec((bt, D), lambda i: (i, 0))
    return pl.pallas_call(
        body, grid=(T // bt,),
        in_specs=[pl.BlockSpec((bt, 128), lambda i: (i, C_GATE // 128)), vec, vec],
        out_specs=[row, row],
        out_shape=[jax.ShapeDtypeStruct((T, D), F32)] * 2, name="gdn_gates_fwd",
        compiler_params=_params(("parallel",)))(p, alog_row, dtb_row)


def _gdn_gates_bwd(p, alog_row, dtb_row, dbeta_x, dgam_x, dp_in, *, bt=256):
    T = p.shape[0]
    bt = min(bt, T)

    def body(g_ref, al_ref, db_ref, dbeta_ref, dgam_ref, dpin_ref, dg_out, dal_ref, ddb_ref):
        i = pl.program_id(0)
        gt = g_ref[...]
        _, ebt = _expand_mats(7, 0)
        _, egt = _expand_mats(7, GDN_H)
        _, uc = _cum_mats(GDN_C)
        ea = jnp.exp(al_ref[...])
        zz = gt + db_ref[...]
        g_l = -ea * _softplus(zz)
        beta_l = _sigmoid(gt)
        dg_l = _dot_sel(_chunk_cumsum(dgam_ref[...], uc, GDN_C), egt)
        dbeta_l = _dot_sel(dbeta_ref[...], ebt)
        da = dg_l * (-ea) * _sigmoid(zz)
        dg_out[...] = da + dbeta_l * beta_l * (1.0 - beta_l)

        @pl.when(i == 0)
        def _():
            dal_ref[...] = jnp.zeros_like(dal_ref)
            ddb_ref[...] = jnp.zeros_like(ddb_ref)

        dal_ref[...] += jnp.sum(dg_l * g_l, axis=0, keepdims=True)
        ddb_ref[...] += jnp.sum(da, axis=0, keepdims=True)

    vec = pl.BlockSpec((1, 128), lambda i: (0, 0))
    row = pl.BlockSpec((bt, D), lambda i: (i, 0))
    gate = pl.BlockSpec((bt, 128), lambda i: (i, C_GATE // 128))
    return pl.pallas_call(
        body, grid=(T // bt,),
        in_specs=[gate, vec, vec, row, row, pl.BlockSpec(memory_space=pl.ANY)],
        out_specs=[gate, vec, vec],
        out_shape=[jax.ShapeDtypeStruct((T, C_TOT), F32), jax.ShapeDtypeStruct((1, 128), F32),
                   jax.ShapeDtypeStruct((1, 128), F32)],
        input_output_aliases={5: 0}, name="gdn_gates_bwd",
        compiler_params=_params(("arbitrary",)))(p, alog_row, dtb_row, dbeta_x, dgam_x, dp_in)


def _ssd_dt_fwd(p, dtb_row, alog_x, *, bt=256):
    T = p.shape[0]
    bt = min(bt, T)

    def body(d_ref, db_ref, al_ref, dt_ref, alpha_ref):
        ed, _ = _expand_mats(6, 0)
        lc, _ = _cum_mats(SSM_L)
        dt_x = _dot_sel(_softplus(d_ref[...] + db_ref[...]), ed)
        dt_ref[...] = dt_x
        alpha_ref[...] = _chunk_cumsum(dt_x * (-jnp.exp(al_ref[...])), lc, SSM_L)

    row = pl.BlockSpec((bt, D), lambda i: (i, 0))
    return pl.pallas_call(
        body, grid=(T // bt,),
        in_specs=[pl.BlockSpec((bt, 128), lambda i: (i, C_DT // 128)),
                  pl.BlockSpec((1, 128), lambda i: (0, 0)), pl.BlockSpec((1, D), lambda i: (0, 0))],
        out_specs=[row, row], out_shape=[jax.ShapeDtypeStruct((T, D), F32)] * 2,
        name="ssd_dt_fwd", compiler_params=_params(("parallel",)))(p, dtb_row, alog_x)


def _ssd_dt_bwd(p, dtb_row, alog_x, ddt_x, dalpha_x, dp_in, *, bt=256):
    T = p.shape[0]
    bt = min(bt, T)

    def body(d_ref, db_ref, al_ref, ddt_ref, dal_ref, dpin_ref, dd_out, ddb_ref, dalog_ref):
        i = pl.program_id(0)
        ed, edt = _expand_mats(6, 0)
        _, uc = _cum_mats(SSM_L)
        zz = d_ref[...] + db_ref[...]
        dt_x = _dot_sel(_softplus(zz), ed)
        a_x = -jnp.exp(al_ref[...])
        da_x = _chunk_cumsum(dal_ref[...], uc, SSM_L)
        ddt_l = _dot_sel(ddt_ref[...] + da_x * a_x, edt)
        draw = ddt_l * _sigmoid(zz)
        dd_out[...] = draw

        @pl.when(i == 0)
        def _():
            ddb_ref[...] = jnp.zeros_like(ddb_ref)
            dalog_ref[...] = jnp.zeros_like(dalog_ref)

        ddb_ref[...] += jnp.sum(draw, axis=0, keepdims=True)
        dalog_ref[...] += jnp.sum(da_x * dt_x, axis=0, keepdims=True) * a_x

    row = pl.BlockSpec((bt, D), lambda i: (i, 0))
    seg = pl.BlockSpec((bt, 128), lambda i: (i, C_DT // 128))
    v128 = pl.BlockSpec((1, 128), lambda i: (0, 0))
    vD = pl.BlockSpec((1, D), lambda i: (0, 0))
    return pl.pallas_call(
        body, grid=(T // bt,),
        in_specs=[seg, v128, vD, row, row, pl.BlockSpec(memory_space=pl.ANY)],
        out_specs=[seg, v128, vD],
        out_shape=[jax.ShapeDtypeStruct((T, C_TOT), F32), jax.ShapeDtypeStruct((1, 128), F32),
                   jax.ShapeDtypeStruct((1, D), F32)],
        input_output_aliases={5: 0}, name="ssd_dt_bwd",
        compiler_params=_params(("arbitrary",)))(p, dtb_row, alog_x, ddt_x, dalpha_x, dp_in)


def _gdn_post_fwd(o, p, w_x, *, bt=256):
    T = o.shape[0]
    bt = min(bt, T)

    def body(o_ref, z_ref, w_ref, out_ref):
        for h in range(GDN_H):
            sl = slice(h * 128, (h + 1) * 128)
            oh = o_ref[:, sl]
            r = lax.rsqrt(jnp.mean(oh * oh, axis=1, keepdims=True) + EPS)
            out_ref[:, sl] = (oh * r * w_ref[:, sl] * _silu(z_ref[:, sl])).astype(out_ref.dtype)

    row = pl.BlockSpec((bt, D), lambda i: (i, 0))
    return pl.pallas_call(
        body, grid=(T // bt,),
        in_specs=[row, pl.BlockSpec((bt, D), lambda i: (i, C_ZG // D)), pl.BlockSpec((1, D), lambda i: (0, 0))],
        out_specs=row, out_shape=jax.ShapeDtypeStruct((T, D), BF16), name="gdn_post_fwd",
        compiler_params=_params(("parallel",)))(o, p, w_x)


def _gdn_post_bwd(dmix, o, p, w_x, *, bt=256):
    T = o.shape[0]
    bt = min(bt, T)

    def body(dm_ref, o_ref, z_ref, w_ref, do_ref, dz_ref, dw_ref):
        i = pl.program_id(0)

        @pl.when(i == 0)
        def _():
            dw_ref[...] = jnp.zeros_like(dw_ref)

        for h in range(GDN_H):
            sl = slice(h * 128, (h + 1) * 128)
            oh, zh, wh, dm = o_ref[:, sl], z_ref[:, sl], w_ref[:, sl], dm_ref[:, sl]
            r = lax.rsqrt(jnp.mean(oh * oh, axis=1, keepdims=True) + EPS)
            ohat = oh * r
            dy = dm * _silu(zh)
            dz_ref[:, sl] = dm * ohat * wh * _dsilu(zh)
            dohat = dy * wh
            do_ref[:, sl] = r * (dohat - ohat * jnp.mean(dohat * ohat, axis=1, keepdims=True))
            dw_ref[:, sl] += jnp.sum(dy * ohat, axis=0, keepdims=True)

    row = pl.BlockSpec((bt, D), lambda i: (i, 0))
    zcol = pl.BlockSpec((bt, D), lambda i: (i, C_ZG // D))
    vec = pl.BlockSpec((1, D), lambda i: (0, 0))
    return pl.pallas_call(
        body, grid=(T // bt,), in_specs=[row, row, zcol, vec], out_specs=[row, zcol, vec],
        out_shape=[jax.ShapeDtypeStruct((T, D), F32), jax.ShapeDtypeStruct((T, C_TOT), F32),
                   jax.ShapeDtypeStruct((1, D), F32)],
        name="gdn_post_bwd", compiler_params=_params(("arbitrary",)))(dmix, o, p, w_x)


def _ssd_post_fwd(y, xs, p, d_x, w, *, bt=256):
    T = y.shape[0]
    bt = min(bt, T)

    def body(y_ref, x_ref, z_ref, d_ref, w_ref, out_ref):
        yg = (y_ref[...] + x_ref[...] * d_ref[...]) * _silu(z_ref[...])
        for g in range(2):
            sl = slice(g * 512, (g + 1) * 512)
            a = yg[:, sl]
            r = lax.rsqrt(jnp.mean(a * a, axis=1, keepdims=True) + EPS)
            out_ref[:, sl] = (a * r * w_ref[:, sl]).astype(out_ref.dtype)

    row = pl.BlockSpec((bt, D), lambda i: (i, 0))
    vec = pl.BlockSpec((1, D), lambda i: (0, 0))
    return pl.pallas_call(
        body, grid=(T // bt,),
        in_specs=[row, row, pl.BlockSpec((bt, D), lambda i: (i, C_ZS // D)), vec, vec],
        out_specs=row, out_shape=jax.ShapeDtypeStruct((T, D), BF16), name="ssd_post_fwd",
        compiler_params=_params(("parallel",)))(y, xs, p, d_x, w)


def _ssd_post_bwd(dmix, y, xs, p, d_x, w, dp_in, *, bt=256):
    T = y.shape[0]
    bt = min(bt, T)

    def body(dm_ref, y_ref, x_ref, z_ref, d_ref, w_ref, dpin_ref, dyy_ref, dz_ref, dd_ref, dw_ref):
        i = pl.program_id(0)

        @pl.when(i == 0)
        def _():
            dd_ref[...] = jnp.zeros_like(dd_ref)
            dw_ref[...] = jnp.zeros_like(dw_ref)

        xv, zv = x_ref[...], z_ref[...]
        yy = y_ref[...] + xv * d_ref[...]
        sz = _silu(zv)
        yg = yy * sz
        parts = []
        for g in range(2):
            sl = slice(g * 512, (g + 1) * 512)
            a = yg[:, sl]
            r = lax.rsqrt(jnp.mean(a * a, axis=1, keepdims=True) + EPS)
            ah = a * r
            dout = dm_ref[:, sl]
            dah = dout * w_ref[:, sl]
            dw_ref[:, sl] += jnp.sum(dout * ah, axis=0, keepdims=True)
            parts.append(r * (dah - ah * jnp.mean(dah * ah, axis=1, keepdims=True)))
        dyg = jnp.concatenate(parts, axis=1)
        dyy = dyg * sz
        dyy_ref[...] = dyy
        dz_ref[...] = dyg * yy * _dsilu(zv)
        dd_ref[...] += jnp.sum(dyy * xv, axis=0, keepdims=True)

    row = pl.BlockSpec((bt, D), lambda i: (i, 0))
    zcol = pl.BlockSpec((bt, D), lambda i: (i, C_ZS // D))
    vec = pl.BlockSpec((1, D), lambda i: (0, 0))
    return pl.pallas_call(
        body, grid=(T // bt,),
        in_specs=[row, row, row, zcol, vec, vec, pl.BlockSpec(memory_space=pl.ANY)],
        out_specs=[row, zcol, vec, vec],
        out_shape=[jax.ShapeDtypeStruct((T, D), F32), jax.ShapeDtypeStruct((T, C_TOT), F32),
                   jax.ShapeDtypeStruct((1, D), F32), jax.ShapeDtypeStruct((1, D), F32)],
        input_output_aliases={6: 1}, name="ssd_post_bwd",
        compiler_params=_params(("arbitrary",)))(dmix, y, xs, p, d_x, w, dp_in)


_NEG = -1e30


def _gdn_terms(q, k, v, bx, gam_c):
    C = GDN_C
    ri, ci = _iota2((C, C), 0), _iota2((C, C), 1)
    eye, low, strict = ri == ci, ri >= ci, ri > ci
    gam_r = jnp.sum(jnp.where(eye, gam_c, 0.0), axis=0, keepdims=True)
    G = jnp.exp(jnp.where(low, gam_c - gam_r, _NEG))
    glast = jnp.sum(jnp.where(_iota2((C, 1), 0) == C - 1, gam_c, 0.0), axis=0, keepdims=True)
    eg, egl, eL = jnp.exp(gam_c), jnp.exp(glast - gam_c), jnp.exp(glast)
    kb, vb = k * bx, v * bx
    M = _dot(kb, k, _NT)
    return dict(eye=eye, low=low, strict=strict, G=G, eg=eg, egl=egl, eL=eL, kb=kb, vb=vb, M=M,
                kbg=kb * eg, qd=q * eg, kd=k * egl, q=q, k=k, v=v, bx=bx)


def _split(a):
    hi = a.astype(_MXU)
    return hi, (a - hi.astype(F32)).astype(_MXU)


def _dot3s(a, b):
    d = lambda p, q: lax.dot_general(p, q, _NN, preferred_element_type=F32)
    return d(a[0], b[0]) + d(a[0], b[1]) + d(a[1], b[0])


def _tri_inv_many(Ls, eye):
    eyef = jnp.where(eye, 1.0, 0.0)
    Ts = [eyef - L for L in Ls]
    Ps = [-L for L in Ls]
    for _ in range(5):
        sp = [_split(p) for p in Ps]
        Ps = [_dot3s(s, s) for s in sp]
        sp = [_split(p) for p in Ps]
        st = [_split(t) for t in Ts]
        Ts = [t + _dot3s(a, b) for t, a, b in zip(Ts, st, sp)]
    return Ts


def _gdn_heads(q_ref, k_ref, v_ref, bx_ref, gx_ref):
    out = []
    for h in range(GDN_H):
        sl = slice(h * 128, (h + 1) * 128)
        gam_c = jnp.max(gx_ref[:, sl], axis=1, keepdims=True)
        out.append(_gdn_terms(q_ref[:, sl], k_ref[:, sl], v_ref[:, sl], bx_ref[:, sl], gam_c))
    return out


def _gdn_prep(qk, v, bx, gx, ride=None):
    T = qk.shape[0]
    N = T // GDN_C
    C = GDN_C
    n_ride = ride.n if ride else 0

    def body(q_ref, k_ref, v_ref, bx_ref, gx_ref, *rest):
        ride_in = rest[:n_ride]
        u_ref, w_ref, qd_ref, kd_ref, p_ref, t_ref = rest[n_ride:n_ride + 6]
        ride_out = rest[n_ride + 6:2 * n_ride + 6]
        if ride:
            @pl.when(pl.program_id(0) == 0)
            def _():
                ride.start(ride_in, ride_out, rest[-3:])

            @pl.when(pl.program_id(0) == N - 1)
            def _():
                ride.wait(ride_in, ride_out, rest[-3:])

        ts = _gdn_heads(q_ref, k_ref, v_ref, bx_ref, gx_ref)
        Ts = _tri_inv_many([jnp.where(t["strict"], t["M"] * t["G"], 0.0) for t in ts], ts[0]["eye"])
        for h, (t, Tm) in enumerate(zip(ts, Ts)):
            sl = slice(h * 128, (h + 1) * 128)
            rows = slice(h * C, (h + 1) * C)
            u_ref[:, sl] = _dot(Tm, t["vb"])
            w_ref[:, sl] = _dot(Tm, t["kbg"]).astype(w_ref.dtype)
            qd_ref[:, sl] = t["qd"].astype(qd_ref.dtype)
            kd_ref[:, sl] = t["kd"].astype(kd_ref.dtype)
            p_ref[0, rows, :] = _dot(t["q"], t["k"], _NT) * t["G"]
            t_ref[0, rows, :] = Tm

    blk = lambda c: pl.BlockSpec((C, D), lambda n: (n, c))
    sq = pl.BlockSpec((1, GDN_H * C, C), lambda n: (n, 0, 0))
    in_specs = [blk(0), blk(1), blk(0), blk(0), blk(0)]
    out_specs = [blk(0), blk(0), blk(0), blk(0), sq, sq]
    out_shape = [jax.ShapeDtypeStruct((T, D), F32), jax.ShapeDtypeStruct((T, D), BF16),
                 jax.ShapeDtypeStruct((T, D), BF16), jax.ShapeDtypeStruct((T, D), BF16),
                 jax.ShapeDtypeStruct((N, GDN_H * C, C), F32), jax.ShapeDtypeStruct((N, GDN_H * C, C), F32)]
    ins = [qk, qk, v, bx, gx]
    if ride:
        ins, in_specs = ins + ride.srcs, in_specs + ride.specs
        out_shape, out_specs = out_shape + ride.out_shape, out_specs + ride.specs
    res = pl.pallas_call(
        body, grid=(N,), in_specs=in_specs, out_specs=out_specs, out_shape=out_shape,
        scratch_shapes=ride.scratch if ride else [], name="gdn_prep",
        compiler_params=_params(("arbitrary",) if ride else ("parallel",)))(*ins)
    return (list(res[:6]), list(res[6:])) if ride else list(res)


def _gdn_scan_fwd(u, w, qd, kd, pm, gx):
    T = u.shape[0]
    N = T // GDN_C
    C = GDN_C

    def body(u_ref, w_ref, qd_ref, kd_ref, p_ref, gx_ref, o_ref, vn_ref, ss_ref, S_scr):
        n = pl.program_id(0)

        @pl.when(n == 0)
        def _():
            S_scr[...] = jnp.zeros_like(S_scr)

        sls = [slice(h * 128, (h + 1) * 128) for h in range(GDN_H)]
        Ss = [S_scr[:, sl] for sl in sls]
        vns = [u_ref[:, sl] - _dot(w_ref[:, sl], S) for sl, S in zip(sls, Ss)]
        for h, (sl, S, vn) in enumerate(zip(sls, Ss, vns)):
            ss_ref[0, :, sl] = S
            vn_ref[:, sl] = vn.astype(vn_ref.dtype)
            o_ref[:, sl] = _dot(qd_ref[:, sl], S) + _dot(p_ref[0, h * C:(h + 1) * C, :], vn)
            S_scr[:, sl] = S * jnp.exp(gx_ref[C - 1:C, sl]) + _dot(kd_ref[:, sl], vn, _TN)

    blk = pl.BlockSpec((C, D), lambda n: (n, 0))
    return pl.pallas_call(
        body, grid=(N,),
        in_specs=[blk, blk, blk, blk, pl.BlockSpec((1, GDN_H * C, C), lambda n: (n, 0, 0)), blk],
        out_specs=[blk, blk, pl.BlockSpec((1, GDN_DK, D), lambda n: (n, 0, 0))],
        out_shape=[jax.ShapeDtypeStruct((T, D), F32), jax.ShapeDtypeStruct((T, D), BF16),
                   jax.ShapeDtypeStruct((N, GDN_DK, D), F32)],
        scratch_shapes=[pltpu.VMEM((GDN_DK, D), F32)], name="gdn_scan_fwd",
        compiler_params=_params(("arbitrary",)))(u, w, qd, kd, pm, gx)


def _gdn_scan_bwd(w, qd, kd, pm, gx, do):
    T = w.shape[0]
    N = T // GDN_C
    C = GDN_C

    def body(w_ref, qd_ref, kd_ref, p_ref, gx_ref, do_ref, dvn_ref, ds_ref, dS_scr):
        n = pl.program_id(0)

        @pl.when(n == 0)
        def _():
            dS_scr[...] = jnp.zeros_like(dS_scr)

        sls = [slice(h * 128, (h + 1) * 128) for h in range(GDN_H)]
        dSs = [dS_scr[:, sl] for sl in sls]
        dvns = [_dot(p_ref[0, h * C:(h + 1) * C, :], do_ref[:, sl], _TN) + _dot(kd_ref[:, sl], dS2)
                for h, (sl, dS2) in enumerate(zip(sls, dSs))]
        for sl, dS2, dvn in zip(sls, dSs, dvns):
            ds_ref[0, :, sl] = dS2
            dvn_ref[:, sl] = dvn.astype(dvn_ref.dtype)
            dS_scr[:, sl] = (dS2 * jnp.exp(gx_ref[C - 1:C, sl]) + _dot(qd_ref[:, sl], do_ref[:, sl], _TN)
                             - _dot(w_ref[:, sl], dvn, _TN))

    blk = pl.BlockSpec((C, D), lambda n: (N - 1 - n, 0))
    return pl.pallas_call(
        body, grid=(N,),
        in_specs=[blk, blk, blk, pl.BlockSpec((1, GDN_H * C, C), lambda n: (N - 1 - n, 0, 0)), blk, blk],
        out_specs=[blk, pl.BlockSpec((1, GDN_DK, D), lambda n: (N - 1 - n, 0, 0))],
        out_shape=[jax.ShapeDtypeStruct((T, D), BF16), jax.ShapeDtypeStruct((N, GDN_DK, D), F32)],
        scratch_shapes=[pltpu.VMEM((GDN_DK, D), F32)], name="gdn_scan_bwd",
        compiler_params=_params(("arbitrary",)))(w, qd, kd, pm, gx, do)


def _gdn_rest_bwd(qk, v, bx, gx, s_save, t_save, vn, dvn, ds_save, do, ride=None):
    T = qk.shape[0]
    N = T // GDN_C
    C = GDN_C
    n_ride = ride.n if ride else 0

    def body(q_ref, k_ref, v_ref, bx_ref, gx_ref, ss_ref, ts_ref, vn_ref, dvn_ref, ds_ref, do_ref, *rest):
        ride_in = rest[:n_ride]
        dqkv_ref, dbx_ref, dgx_ref = rest[n_ride:n_ride + 3]
        ride_out = rest[n_ride + 3:2 * n_ride + 3]
        if ride:
            @pl.when(pl.program_id(0) == 0)
            def _():
                ride.start(ride_in, ride_out, rest[-3:])

            @pl.when(pl.program_id(0) == N - 1)
            def _():
                ride.wait(ride_in, ride_out, rest[-3:])

        H = range(GDN_H)
        sls = [slice(h * 128, (h + 1) * 128) for h in H]
        ts = _gdn_heads(q_ref, k_ref, v_ref, bx_ref, gx_ref)
        Ss = [ss_ref[0, :, sl] for sl in sls]
        Tms = [ts_ref[0, h * C:(h + 1) * C, :] for h in H]
        dS2s = [ds_ref[0, :, sl] for sl in sls]
        dos = [do_ref[:, sl] for sl in sls]
        vns = [vn_ref[:, sl] for sl in sls]
        dvns = [dvn_ref[:, sl] for sl in sls]
        Qs = [_dot(t["q"], t["k"], _NT) for t in ts]
        dws = [-_dot(dvn, S, _NT) for dvn, S in zip(dvns, Ss)]
        dqds = [_dot(do, S, _NT) for do, S in zip(dos, Ss)]
        dPs = [jnp.where(t["low"], _dot(do, vn, _NT), 0.0) for t, do, vn in zip(ts, dos, vns)]
        dkds = [_dot(vn, dS2, _NT) for vn, dS2 in zip(vns, dS2s)]
        dTs = [_dot(dvn, t["vb"], _NT) + _dot(dw, t["kbg"], _NT) for t, dvn, dw in zip(ts, dvns, dws)]
        dvbs = [_dot(Tm, dvn, _TN) for Tm, dvn in zip(Tms, dvns)]
        dkbgs = [_dot(Tm, dw, _TN) for Tm, dw in zip(Tms, dws)]
        TdTs = [_dot(Tm, dT, _TN) for Tm, dT in zip(Tms, dTs)]
        dLs = [jnp.where(t["strict"], -_dot(TdT, Tm, _NT), 0.0) for t, TdT, Tm in zip(ts, TdTs, Tms)]
        dMs = [dL * t["G"] for t, dL in zip(ts, dLs)]
        dQs = [dP * t["G"] for t, dP in zip(ts, dPs)]
        dkbs = [_dot(dM, t["k"]) + dkbg * t["eg"] for t, dM, dkbg in zip(ts, dMs, dkbgs)]
        rs = lambda a: jnp.sum(a, axis=1, keepdims=True)
        lane0 = _iota2((C, 128), 1) == 0
        last = _iota2((C, 1), 0) == C - 1
        for h in H:
            t, sl = ts[h], sls[h]
            E = (dLs[h] * t["M"] + dPs[h] * Qs[h]) * t["G"]
            dqkv_ref[:, sl] = _dot(dQs[h], t["k"]) + dqds[h] * t["eg"]
            dqkv_ref[:, D + h * 128:D + (h + 1) * 128] = (
                _dot(dQs[h], t["q"], _TN) + _dot(dMs[h], t["kb"], _TN) + dkds[h] * t["egl"] + dkbs[h] * t["bx"])
            dqkv_ref[:, 2 * D + h * 128:2 * D + (h + 1) * 128] = dvbs[h] * t["bx"]
            dbx_ref[:, sl] = dkbs[h] * t["k"] + dvbs[h] * t["v"]
            dkd_kd = dkds[h] * t["kd"]
            dgam_c = rs(dqds[h] * t["qd"]) + rs(dkbgs[h] * t["kbg"]) - rs(dkd_kd) + rs(E)
            dgam_r = -jnp.sum(E, axis=0, keepdims=True)
            dgam_c = dgam_c + jnp.sum(jnp.where(t["eye"], dgam_r, 0.0), axis=1, keepdims=True)
            dlast = _sum_all(dkd_kd) + t["eL"] * _sum_all(Ss[h] * dS2s[h])
            dgx_ref[:, sl] = jnp.where(lane0, dgam_c + jnp.where(last, dlast, 0.0), 0.0)

    blk = lambda c: pl.BlockSpec((C, D), lambda n: (n, c))
    st = pl.BlockSpec((1, GDN_DK, D), lambda n: (n, 0, 0))
    in_specs = [blk(0), blk(1), blk(0), blk(0), blk(0), st,
                pl.BlockSpec((1, GDN_H * C, C), lambda n: (n, 0, 0)), blk(0), blk(0), st, blk(0)]
    out_specs = [pl.BlockSpec((C, 3 * D), lambda n: (n, 0)), blk(0), blk(0)]
    out_shape = [jax.ShapeDtypeStruct((T, 3 * D), F32), jax.ShapeDtypeStruct((T, D), F32),
                 jax.ShapeDtypeStruct((T, D), F32)]
    ins = [qk, qk, v, bx, gx, s_save, t_save, vn, dvn, ds_save, do]
    if ride:
        ins, in_specs = ins + ride.srcs, in_specs + ride.specs
        out_shape, out_specs = out_shape + ride.out_shape, out_specs + ride.specs
    res = pl.pallas_call(
        body, grid=(N,), in_specs=in_specs, out_specs=out_specs, out_shape=out_shape,
        scratch_shapes=ride.scratch if ride else [], name="gdn_rest_bwd",
        compiler_params=_params(("arbitrary",) if ride else ("parallel",)))(*ins)
    return (list(res[:3]), list(res[3:])) if ride else list(res)


def _ssd_seg(al_pair, half, s):
    L = SSM_L
    ri, ci = _iota2((L, L), 0), _iota2((L, L), 1)
    ac = jnp.max(jnp.where(half == s, al_pair, _NEG), axis=1, keepdims=True)
    ar = jnp.sum(jnp.where(ri == ci, ac, 0.0), axis=0, keepdims=True)
    return jnp.exp(jnp.where(ri >= ci, ac - ar, _NEG))


def _last_row(a):
    return jnp.sum(jnp.where(_iota2((a.shape[0], 1), 0) == a.shape[0] - 1, a, 0.0), axis=0, keepdims=True)


def _ssd_core_fwd(xbc, dtx, alx):
    T = xbc.shape[0]
    L = SSM_L
    Nc = T // L

    def body(x_ref, bc_ref, dt_ref, al_ref, y_ref, hs_ref, H_scr):
        c = pl.program_id(0)

        @pl.when(c == 0)
        def _():
            H_scr[...] = jnp.zeros_like(H_scr)

        half = _iota2((L, 128), 1) >> 6
        for g in range(2):
            gs = slice(g * 512, (g + 1) * 512)
            Bg = bc_ref[:, g * 128:(g + 1) * 128]
            Cg = bc_ref[:, 256 + g * 128:256 + (g + 1) * 128]
            alg = al_ref[:, gs]
            alast = _last_row(alg)
            xdt = x_ref[:, gs] * dt_ref[:, gs]
            Hg = H_scr[:, gs]
            hs_ref[0, :, gs] = Hg
            CB = _dot(Cg, Bg, _NT)
            y_ref[:, gs] = jnp.exp(alg) * _dot(Cg, Hg)
            H_scr[:, gs] = Hg * jnp.exp(alast) + _dot(Bg, jnp.exp(alast - alg) * xdt, _TN)
            for j in range(4):
                ps = slice(g * 512 + j * 128, g * 512 + (j + 1) * 128)
                al_pair = al_ref[:, ps]
                xp = x_ref[:, ps] * dt_ref[:, ps]
                ys = [_dot(_ssd_seg(al_pair, half, s) * CB, xp) for s in range(2)]
                y_ref[:, ps] += jnp.where(half == 0, ys[0], ys[1])

    row = pl.BlockSpec((L, D), lambda c: (c, 0))
    return pl.pallas_call(
        body, grid=(Nc,), in_specs=[row, pl.BlockSpec((L, 512), lambda c: (c, 2)), row, row],
        out_specs=[row, pl.BlockSpec((1, SSM_N, D), lambda c: (c, 0, 0))],
        out_shape=[jax.ShapeDtypeStruct((T, D), F32), jax.ShapeDtypeStruct((Nc, SSM_N, D), F32)],
        scratch_shapes=[pltpu.VMEM((SSM_N, D), F32)], name="ssd_core_fwd",
        compiler_params=_params(("arbitrary",)))(xbc, xbc, dtx, alx)


def _ssd_core_bwd(xbc, dtx, alx, h_save, dyy, d_x):
    T = xbc.shape[0]
    L = SSM_L
    Nc = T // L

    def body(x_ref, bc_ref, dt_ref, al_ref, hs_ref, dy_ref, d_ref, dx_ref, ddt_ref, dal_ref, dH_scr):
        c = pl.program_id(0)

        @pl.when(c == 0)
        def _():
            dH_scr[...] = jnp.zeros_like(dH_scr)

        lane = _iota2((L, 128), 1)
        half = lane >> 6
        rowi = _iota2((L, 1), 0)
        ri, ci = _iota2((L, L), 0), _iota2((L, L), 1)
        for g in range(2):
            gs = slice(g * 512, (g + 1) * 512)
            Bg = bc_ref[:, g * 128:(g + 1) * 128]
            Cg = bc_ref[:, 256 + g * 128:256 + (g + 1) * 128]
            alg = al_ref[:, gs]
            alast = _last_row(alg)
            eal, edec, eL = jnp.exp(alg), jnp.exp(alast - alg), jnp.exp(alast)
            xg, dtg, dYg = x_ref[:, gs], dt_ref[:, gs], dy_ref[:, gs]
            xdt = xg * dtg
            Hg = hs_ref[0, :, gs]
            dH2 = dH_scr[:, gs]
            CB = _dot(Cg, Bg, _NT)
            dYe = eal * dYg
            dH_scr[:, gs] = dH2 * eL + _dot(Cg, dYe, _TN)
            dC = _dot(dYe, Hg, _NT)
            zg = edec * xdt
            dz = _dot(Bg, dH2)
            dB = _dot(zg, dH2, _NT)
            tz = dz * zg
            dal = dYe * _dot(Cg, Hg) - tz
            dalast = jnp.sum(tz, axis=0, keepdims=True) + eL * jnp.sum(Hg * dH2, axis=0, keepdims=True)
            dal = dal + jnp.where(rowi == L - 1, dalast, 0.0)
            dxdt_g = edec * dz
            dx_ref[:, gs] = dxdt_g * dtg + dYg * d_ref[:, gs]
            ddt_ref[:, gs] = dxdt_g * xg
            dal_ref[:, gs] = dal
            dCB = jnp.zeros((L, L), F32)
            for j in range(4):
                ps = slice(g * 512 + j * 128, g * 512 + (j + 1) * 128)
                al_pair = al_ref[:, ps]
                xp = x_ref[:, ps] * dt_ref[:, ps]
                dYp = dy_ref[:, ps]
                dxp = []
                dal_p = jnp.zeros((L, 128), F32)
                for s in range(2):
                    seg = _ssd_seg(al_pair, half, s)
                    W = seg * CB
                    dW = jnp.where(ri >= ci, _dot(jnp.where(half == s, dYp, 0.0), xp, _NT), 0.0)
                    dxp.append(_dot(W, dYp, _TN))
                    dCB = dCB + dW * seg
                    Es = dW * W
                    dac = jnp.sum(Es, axis=1, keepdims=True) - jnp.sum(
                        jnp.where(ri == ci, jnp.sum(Es, axis=0, keepdims=True), 0.0), axis=1, keepdims=True)
                    dal_p = dal_p + jnp.where(lane == 64 * s, dac, 0.0)
                dxdt_p = jnp.where(half == 0, dxp[0], dxp[1])
                dx_ref[:, ps] += dxdt_p * dt_ref[:, ps]
                ddt_ref[:, ps] += dxdt_p * x_ref[:, ps]
                dal_ref[:, ps] += dal_p
            dx_ref[:, D + g * 128:D + (g + 1) * 128] = dB + _dot(dCB, Cg, _TN)
            dx_ref[:, D + 256 + g * 128:D + 256 + (g + 1) * 128] = dC + _dot(dCB, Bg)

    row = pl.BlockSpec((L, D), lambda c: (Nc - 1 - c, 0))
    bcs = pl.BlockSpec((L, 512), lambda c: (Nc - 1 - c, 2))
    return pl.pallas_call(
        body, grid=(Nc,),
        in_specs=[row, bcs, row, row, pl.BlockSpec((1, SSM_N, D), lambda c: (Nc - 1 - c, 0, 0)), row,
                  pl.BlockSpec((1, D), lambda c: (0, 0))],
        out_specs=[pl.BlockSpec((L, D + 512), lambda c: (Nc - 1 - c, 0)), row, row],
        out_shape=[jax.ShapeDtypeStruct((T, D + 512), F32),
                   jax.ShapeDtypeStruct((T, D), F32), jax.ShapeDtypeStruct((T, D), F32)],
        scratch_shapes=[pltpu.VMEM((SSM_N, D), F32)], name="ssd_core_bwd",
        compiler_params=_params(("arbitrary",)))(xbc, xbc, dtx, alx, h_save, dyy, d_x)


_EARLY = ("w_out", "wq_mem", "wk_mem", "wv_mem", "wo_mem")
_LATE = ("w_up", "w_down")
_GRADS_MLP = ("w_down", "w_up")
_GRADS_MID = ("wo_mem", "wq_mem", "wk_mem", "wv_mem", "w_out")


def _gather_ride(shards, names):
    return None if shards is None else _Ride([shards[n] for n in names], shard=True)


def _grad_ride(shards, G, names):
    return None if shards is None else _Ride([_slots_from_full(n, G[n]) for n in names], shard=False)


def _local_step(x, mem, tgt, W, shards=None):
    T = x.shape[0]
    W = dict(W)
    cw_qk, cw_v = W["gdn_conv_w"][:, :2 * D], W["gdn_conv_w"][:, 2 * D:]
    h1 = _rmsnorm_fwd(x, W["norm1_w"], name="norm1_fwd")
    ride = _gather_ride(shards, _EARLY)
    p = _mm(h1, W["w_in_pad"], name="in_proj", ride=ride)
    if ride:
        p, got = p
        W.update({n: _full_from_slots(n, g) for n, g in zip(_EARLY, got)})
    w_out_a, w_out_b = W["w_out"][:D], W["w_out"][D:]
    qk = _conv_fwd(p, C_QKV, 2 * D, cw_qk, None, l2=True, name="gdn_conv_qk_fwd")
    v_g = _conv_fwd(p, C_QKV + 2 * D, D, cw_v, None, l2=False, name="gdn_conv_v_fwd")
    bx, gx = _gdn_gates_fwd(p, W["gdn_alog_row"], W["gdn_dtb_row"])
    ride = _gather_ride(shards, _LATE)
    prep = _gdn_prep(qk, v_g, bx, gx, ride)
    if ride:
        prep, got = prep
        W.update({n: _full_from_slots(n, g) for n, g in zip(_LATE, got)})
    u_g, w_g, qd_g, kd_g, p_g, t_save = prep
    o_g, vn_g, s_save = _gdn_scan_fwd(u_g, w_g, qd_g, kd_g, p_g, gx)
    mixa = _gdn_post_fwd(o_g, p, W["gdn_norm_x"])
    xbc = _conv_fwd(p, C_XBC, D + 512, W["ssm_conv_w"], W["ssm_conv_b"], l2=False, name="ssm_conv_fwd", bc=512)
    dtx, alx = _ssd_dt_fwd(p, W["ssm_dtb_row"], W["ssm_alog_x"])
    y_s, h_save = _ssd_core_fwd(xbc, dtx, alx)
    mixb = _ssd_post_fwd(y_s, xbc, p, W["ssm_d_x"], W["ssm_norm_w"].reshape(1, D))
    x1 = _mm(mixa, w_out_a, epi="res", extra=x, name="out_proj_a")
    x1 = _mm(mixb, w_out_b, epi="res", extra=x1, name="out_proj_b")
    h2 = _rmsnorm_fwd(x1, W["norm2_w"], name="norm2_fwd")
    qm = _mm(h2, W["wq_mem"], name="q_proj")
    m = _rmsnorm_fwd(mem, W["mem_norm_w"], name="mem_norm_fwd")
    km = _mm(m, W["wk_mem"], name="k_proj")
    vm = _mm(m, W["wv_mem"], name="v_proj")
    oa = _attn_fwd(qm, km, vm)
    x2 = _mm(oa, W["wo_mem"], epi="res", extra=x1, name="o_proj")
    h3 = _rmsnorm_fwd(x2, W["norm3_w"], name="norm3_fwd")
    u, act = _mm(h3, W["w_up"], epi="relu2", out_dtype=BF16, name="mlp_up")
    x3 = _mm(act, W["w_down"], epi="res", extra=x2, name="mlp_down")
    loss, dx3, g_final = _final_loss(x3, W["final_norm_w"], tgt)
    G = {"final_norm_w": g_final}
    dpre = _mm(dx3, W["w_down"], dims="nt", epi="mul2", extra=u, out_dtype=BF16, name="mlp_down_dx")
    G["w_down"] = _mm(act, dx3, dims="tn", out_dtype=BF16, name="mlp_down_dw")
    G["w_up"] = _mm(h3, dpre, dims="tn", out_dtype=BF16, name="mlp_up_dw")
    dh3 = _mm(dpre, W["w_up"], dims="nt", name="mlp_up_dx")
    dx2, G["norm3_w"] = _rmsnorm_bwd(x2, W["norm3_w"], dh3, dx3, name="norm3_bwd")
    do_a = _mm(dx2, W["wo_mem"], dims="nt", name="o_proj_dx")
    G["wo_mem"] = _mm(oa, dx2, dims="tn", out_dtype=BF16, name="o_proj_dw")
    dq, dk, dv = _attn_bwd(qm, km, vm, do_a)
    G["wq_mem"] = _mm(h2, dq, dims="tn", out_dtype=BF16, name="q_proj_dw")
    dh2 = _mm(dq, W["wq_mem"], dims="nt", name="q_proj_dx")
    dx1, G["norm2_w"] = _rmsnorm_bwd(x1, W["norm2_w"], dh2, dx2, name="norm2_bwd")
    G["wk_mem"] = _mm(m, dk, dims="tn", out_dtype=BF16, name="k_proj_dw")
    G["wv_mem"] = _mm(m, dv, dims="tn", out_dtype=BF16, name="v_proj_dw")
    dm = _mm(dk, W["wk_mem"], dims="nt", name="k_proj_dx")
    dm = _mm(dv, W["wv_mem"], dims="nt", epi="res", extra=dm, name="v_proj_dx")
    _, G["mem_norm_w"] = _rmsnorm_bwd(mem, W["mem_norm_w"], dm, None, name="mem_norm_bwd")
    dmixa = _mm(dx1, w_out_a, dims="nt", name="out_proj_a_dx")
    dmixb = _mm(dx1, w_out_b, dims="nt", name="out_proj_b_dx")
    G["w_out"] = jnp.concatenate([_mm(mixa, dx1, dims="tn", out_dtype=BF16, name="out_proj_a_dw"),
                                  _mm(mixb, dx1, dims="tn", out_dtype=BF16, name="out_proj_b_dw")], axis=0)
    do_g, dp, G["gdn_norm_x"] = _gdn_post_bwd(dmixa, o_g, p, W["gdn_norm_x"])
    dvn_g, ds_save = _gdn_scan_bwd(w_g, qd_g, kd_g, p_g, gx, do_g)
    ride = _grad_ride(shards, G, _GRADS_MLP)
    rest = _gdn_rest_bwd(qk, v_g, bx, gx, s_save, t_save, vn_g, dvn_g, ds_save, do_g, ride)
    if ride:
        rest, got = rest
        G.update(zip(_GRADS_MLP, got))
    dqkvn, dbx, dgx = rest
    dy_qk, gcw_qk, _ = _conv_bwd_act(p, C_QKV, 2 * D, cw_qk, None, dqkvn, 0, l2=True, name="gdn_conv_qk_bwd_act")
    dy_v, gcw_v, _ = _conv_bwd_act(p, C_QKV + 2 * D, D, cw_v, None, dqkvn, 2 * D, l2=False,
                                   name="gdn_conv_v_bwd_act")
    G["gdn_conv_w"] = jnp.concatenate([gcw_qk, gcw_v], axis=1)
    dp = _conv_bwd_in(dy_qk, cw_qk, dp, C_QKV, T, name="gdn_conv_qk_bwd_in")
    dp = _conv_bwd_in(dy_v, cw_v, dp, C_QKV + 2 * D, T, name="gdn_conv_v_bwd_in")
    dp, G["gdn_alog_row"], G["gdn_dtb_row"] = _gdn_gates_bwd(p, W["gdn_alog_row"], W["gdn_dtb_row"], dbx, dgx, dp)
    dyy, dp, G["ssm_d_x"], G["ssm_norm_w"] = _ssd_post_bwd(dmixb, y_s, xbc, p, W["ssm_d_x"],
                                                          W["ssm_norm_w"].reshape(1, D), dp)
    dxbc, ddtx, dalx = _ssd_core_bwd(xbc, dtx, alx, h_save, dyy, W["ssm_d_x"])
    dy_s, G["ssm_conv_w"], G["ssm_conv_b"] = _conv_bwd_act(p, C_XBC, D + 512, W["ssm_conv_w"], W["ssm_conv_b"],
                                                           dxbc, 0, l2=False, name="ssm_conv_bwd_act", bc=512)
    dp = _conv_bwd_in(dy_s, W["ssm_conv_w"], dp, C_XBC, T, name="ssm_conv_bwd_in", bc=512)
    dp, G["ssm_dtb_row"], G["ssm_alog_x"] = _ssd_dt_bwd(p, W["ssm_dtb_row"], W["ssm_alog_x"], ddtx, dalx, dp)
    ride = _grad_ride(shards, G, _GRADS_MID)
    g_in = _mm(h1, dp, dims="tn", out_dtype=BF16, name="in_proj_dw", ride=ride)
    if ride:
        g_in, got = g_in
        G.update(zip(_GRADS_MID, got))
    G["w_in"] = _unpad_w_in(g_in)
    ride = _grad_ride(shards, G, ("w_in",))
    dh1 = _mm(dp, W["w_in_pad"], dims="nt", name="in_proj_dx", ride=ride)
    if ride:
        dh1, got = dh1
        G["w_in"] = got[0]
    dx, G["norm1_w"] = _rmsnorm_bwd(x, W["norm1_w"], dh1, dx1, name="norm1_bwd")
    return loss, dx, G


def _all_gather(shards, out_dtype, *, name):
    n = len(shards)

    def body(*refs):
        x_refs, out_refs, stage = refs[:n], refs[n:2 * n], refs[2 * n:3 * n]
        send_sems, recv_sems, local_sems = refs[3 * n:]
        x, y, c = _place()
        me, sibling = (x, y, c), (x, y, 1 - c)
        chips = [(1 - x, y), (x, 1 - y), (1 - x, 1 - y)]

        def slot(px, py, pc):
            return 4 * px + 2 * py + pc

        def copy(a, k, block, to, src=None):
            dst = out_refs[a].at[slot(*block)]
            return pltpu.make_async_remote_copy(
                src_ref=dst if src is None else src, dst_ref=dst, send_sem=send_sems.at[a, k],
                recv_sem=recv_sems.at[a, k], device_id=to, device_id_type=_MESH)

        for a in range(n):
            stage[a][...] = x_refs[a][...].astype(out_dtype)
        mine = [pltpu.make_async_copy(stage[a], out_refs[a].at[slot(*me)], local_sems.at[a]) for a in range(n)]
        for cp in mine:
            cp.start()
        first = []
        for a in range(n):
            first.append(copy(a, 0, me, sibling, src=stage[a]))
            first += [copy(a, 1 + j, me, (*chip, c), src=stage[a]) for j, chip in enumerate(chips)]
        for cp in first:
            cp.start()
        passed = [[copy(a, 4 + j, (*chip, c), sibling) for j, chip in enumerate(chips)] for a in range(n)]
        for j, chip in enumerate(chips):
            for a in range(n):
                copy(a, 1 + j, (*chip, c), me).wait_recv()
                passed[a][j].start()
        for a in range(n):
            copy(a, 0, sibling, me).wait_recv()
            for j, chip in enumerate(chips):
                copy(a, 4 + j, (*chip, 1 - c), me).wait_recv()
        for cp in first + [cp for row in passed for cp in row]:
            cp.wait_send()
        for cp in mine:
            cp.wait()

    outs = pl.pallas_call(
        body, in_specs=[_VM] * n, out_specs=[_ANY] * n,
        out_shape=[jax.ShapeDtypeStruct((N_DEV,) + s.shape, out_dtype) for s in shards],
        scratch_shapes=[pltpu.VMEM(s.shape, out_dtype) for s in shards]
        + [pltpu.SemaphoreType.DMA((n, 7)), pltpu.SemaphoreType.DMA((n, 7)), pltpu.SemaphoreType.DMA((n,))],
        name=name, compiler_params=pltpu.CompilerParams(vmem_limit_bytes=VMEM_LIMIT))(*shards)
    return list(outs)


def _cast_bf16(arrs, *, name):
    n = len(arrs)

    def body(*refs):
        for a in range(n):
            refs[n + a][...] = refs[a][...].astype(BF16)

    return list(pl.pallas_call(
        body, in_specs=[_VM] * n, out_specs=[_VM] * n,
        out_shape=[jax.ShapeDtypeStruct(s.shape, BF16) for s in arrs], name=name,
        compiler_params=pltpu.CompilerParams(vmem_limit_bytes=VMEM_LIMIT))(*arrs))


def _sum8(a, *, name):
    _, R, Cc = a.shape
    br = _pick_rows(R, 128)

    def body(a_ref, o_ref):
        s = a_ref[0].astype(F32)
        for k in range(1, N_DEV):
            s = s + a_ref[k].astype(F32)
        o_ref[...] = s

    return pl.pallas_call(
        body, grid=(R // br,), in_specs=[pl.BlockSpec((N_DEV, br, Cc), lambda i: (0, i, 0))],
        out_specs=pl.BlockSpec((br, Cc), lambda i: (i, 0)), out_shape=jax.ShapeDtypeStruct((R, Cc), F32),
        name=name, compiler_params=_params(("parallel",)))(a)


def _pick_rows(R, cap):
    if R <= cap:
        return R
    for d in range(cap, 7, -8):
        if R % d == 0:
            return d
    return R


def _adamw(w, g, m, v, *, name):
    shape = w.shape
    as2d = (lambda t: t.reshape(1, -1)) if w.ndim == 1 else (lambda t: t)
    w2, g2, m2, v2 = as2d(w), as2d(g), as2d(m), as2d(v)
    R, Cc = w2.shape
    br = _pick_rows(R, 256)
    c1 = 1.0 - ADAM_B1 ** ADAM_STEP
    c2 = 1.0 - ADAM_B2 ** ADAM_STEP

    def body(w_ref, g_ref, m_ref, v_ref, d_ref, nm_ref, nv_ref):
        gv = g_ref[...]
        nm = ADAM_B1 * m_ref[...] + (1.0 - ADAM_B1) * gv
        nv = ADAM_B2 * v_ref[...] + (1.0 - ADAM_B2) * (gv * gv)
        nm_ref[...] = nm
        nv_ref[...] = nv
        d_ref[...] = -ADAM_LR * ((nm / c1) / (jnp.sqrt(nv / c2) + ADAM_EPS) + ADAM_WD * w_ref[...])

    blk = pl.BlockSpec((br, Cc), lambda i: (i, 0))
    outs = pl.pallas_call(
        body, grid=(R // br,), in_specs=[blk] * 4, out_specs=[blk] * 3,
        out_shape=[jax.ShapeDtypeStruct((R, Cc), F32)] * 3, name=name,
        compiler_params=_params(("parallel",)))(w2, g2, m2, v2)
    return tuple(o.reshape(shape) for o in outs)


_BIG = ("w_in", "w_out", "wq_mem", "wk_mem", "wv_mem", "wo_mem", "w_up", "w_down")
_COL_SHARDED = ("w_in", "w_up")
_WEIGHTS = ("norm1_w", "w_in", "gdn_conv_w", "gdn_a_log", "gdn_dt_bias", "gdn_norm_w", "ssm_conv_w", "ssm_conv_b",
            "ssm_a_log", "ssm_dt_bias", "ssm_d", "ssm_norm_w", "w_out", "norm2_w", "mem_norm_w", "wq_mem", "wk_mem",
            "wv_mem", "wo_mem", "norm3_w", "w_up", "w_down", "final_norm_w")
_IN_PAD = 112


def _full_from_slots(name, g):
    if name in _COL_SHARDED:
        return jnp.transpose(g, (1, 0, 2)).reshape(g.shape[1], N_DEV * g.shape[2])
    return g.reshape(N_DEV * g.shape[1], g.shape[2])


def _slots_from_full(name, f):
    if name in _COL_SHARDED:
        return jnp.transpose(f.reshape(f.shape[0], N_DEV, f.shape[1] // N_DEV), (1, 0, 2))
    return f.reshape(N_DEV, f.shape[0] // N_DEV, f.shape[1])


def _pad_w_in(w):
    z = jnp.zeros((w.shape[0], _IN_PAD), w.dtype)
    return jnp.concatenate([w[:, :4096], w[:, 4112:6672], w[:, 4096:4112], z, w[:, 6672:6688], z], axis=1)


def _unpad_w_in(gp):
    return jnp.concatenate([gp[:, :4096], gp[:, C_GATE:C_GATE + 16], gp[:, 4096:C_GATE], gp[:, C_DT:C_DT + 16]],
                           axis=1)


def _pack_rows(vals):
    rows, offs, r = [], [], 0
    for vflat in vals:
        nrow = 8 * -(-vflat.shape[0] // 1024)
        rows.append(jnp.pad(vflat, (0, nrow * 128 - vflat.shape[0])).reshape(nrow, 128))
        offs.append((r, vflat.shape[0]))
        r += nrow
    return jnp.concatenate(rows, axis=0), offs


def _unpack_rows(packed, offs, shapes):
    out = []
    for (r, nel), shp in zip(offs, shapes):
        nrow = -(-nel // 128)
        out.append(packed[r:r + nrow].reshape(-1)[:nel].reshape(shp))
    return out


def kernel(x, mem, norm1_w, w_in, gdn_conv_w, gdn_a_log, gdn_dt_bias, gdn_norm_w, ssm_conv_w, ssm_conv_b, ssm_a_log, ssm_dt_bias, ssm_d, ssm_norm_w, w_out, norm2_w, mem_norm_w, wq_mem, wk_mem, wv_mem, wo_mem, norm3_w, w_up, w_down, final_norm_w, loss_target, m_norm1_w, m_w_in, m_gdn_conv_w, m_gdn_a_log, m_gdn_dt_bias, m_gdn_norm_w, m_ssm_conv_w, m_ssm_conv_b, m_ssm_a_log, m_ssm_dt_bias, m_ssm_d, m_ssm_norm_w, m_w_out, m_norm2_w, m_mem_norm_w, m_wq_mem, m_wk_mem, m_wv_mem, m_wo_mem, m_norm3_w, m_w_up, m_w_down, m_final_norm_w, v_norm1_w, v_w_in, v_gdn_conv_w, v_gdn_a_log, v_gdn_dt_bias, v_gdn_norm_w, v_ssm_conv_w, v_ssm_conv_b, v_ssm_a_log, v_ssm_dt_bias, v_ssm_d, v_ssm_norm_w, v_w_out, v_norm2_w, v_mem_norm_w, v_wq_mem, v_wk_mem, v_wv_mem, v_wo_mem, v_norm3_w, v_w_up, v_w_down, v_final_norm_w):
    args = dict(locals())
    w_loc = {n: args[n] for n in _WEIGHTS}
    me = 4 * lax.axis_index("x") + 2 * lax.axis_index("y") + lax.axis_index("c")

    w_in_full = _full_from_slots("w_in", _all_gather([w_in], BF16, name="gather_w_in")[0])
    later = _EARLY + _LATE
    shards = dict(zip(later, _cast_bf16([w_loc[n] for n in later], name="cast_shards")))
    conv_pack, conv_offs = _pack_rows([gdn_conv_w.reshape(-1), ssm_conv_w.reshape(-1)])
    conv_all = _all_gather([conv_pack], F32, name="gather_conv")[0]
    gdn_cw, ssm_cw = [], []
    for k in range(N_DEV):
        a, b = _unpack_rows(conv_all[k], conv_offs, [gdn_conv_w.shape, ssm_conv_w.shape])
        gdn_cw.append(a)
        ssm_cw.append(b)
    W = {
        "w_in_pad": _pad_w_in(w_in_full),
        "norm1_w": norm1_w, "norm2_w": norm2_w, "norm3_w": norm3_w, "mem_norm_w": mem_norm_w,
        "final_norm_w": final_norm_w, "ssm_norm_w": ssm_norm_w, "ssm_conv_b": ssm_conv_b,
        "gdn_conv_w": jnp.concatenate(gdn_cw, axis=1), "ssm_conv_w": jnp.concatenate(ssm_cw, axis=1),
        "gdn_alog_row": jnp.pad(gdn_a_log, (GDN_H, 128 - 2 * GDN_H)).reshape(1, 128),
        "gdn_dtb_row": jnp.pad(gdn_dt_bias, (GDN_H, 128 - 2 * GDN_H)).reshape(1, 128),
        "gdn_norm_x": jnp.tile(gdn_norm_w, GDN_H).reshape(1, D),
        "ssm_dtb_row": jnp.pad(ssm_dt_bias, (0, 128 - SSM_H)).reshape(1, 128),
        "ssm_alog_x": jnp.repeat(ssm_a_log, SSM_P).reshape(1, D),
        "ssm_d_x": jnp.repeat(ssm_d, SSM_P).reshape(1, D),
    }

    loss_part, grad_x, G = _local_step(x[0], mem[0], loss_target[0], W, shards)

    grads = {n: _sum8(G[n], name="sum_" + n) for n in _BIG}

    small = {
        "norm1_w": G["norm1_w"], "gdn_conv_w": G["gdn_conv_w"], "gdn_a_log": G["gdn_alog_row"][0, GDN_H:2 * GDN_H],
        "gdn_dt_bias": G["gdn_dtb_row"][0, GDN_H:2 * GDN_H], "gdn_norm_w": G["gdn_norm_x"].reshape(GDN_H, 128).sum(0),
        "ssm_conv_w": G["ssm_conv_w"], "ssm_conv_b": G["ssm_conv_b"],
        "ssm_a_log": G["ssm_alog_x"].reshape(SSM_H, SSM_P).sum(1), "ssm_dt_bias": G["ssm_dtb_row"][0, :SSM_H],
        "ssm_d": G["ssm_d_x"].reshape(SSM_H, SSM_P).sum(1), "ssm_norm_w": G["ssm_norm_w"].reshape(D),
        "norm2_w": G["norm2_w"], "mem_norm_w": G["mem_norm_w"], "norm3_w": G["norm3_w"],
        "final_norm_w": G["final_norm_w"], "loss": loss_part[0, :1],
    }
    names = list(small)
    pack, offs = _pack_rows([small[n].reshape(-1) for n in names])
    tot = _sum8(_all_gather([pack], F32, name="gather_small")[0], name="sum_small")
    summed = dict(zip(names, _unpack_rows(tot, offs, [small[n].shape for n in names])))
    loss = summed.pop("loss")[0]
    for n in ("gdn_conv_w", "ssm_conv_w"):
        width = w_loc[n].shape[1]
        summed[n] = lax.dynamic_slice_in_dim(summed[n], me * width, width, axis=1)
    grads.update(summed)

    upd = {n: _adamw(w_loc[n], grads[n], args["m_" + n], args["v_" + n], name="adamw_" + n) for n in _WEIGHTS}
    return (loss, grad_x[None], *[grads[n] for n in _WEIGHTS], *[upd[n][0] for n in _WEIGHTS],
            *[upd[n][1] for n in _WEIGHTS], *[upd[n][2] for n in _WEIGHTS])
```

```python
import functools
import math

import jax
import jax.numpy as jnp
from jax import lax
from jax.experimental import pallas as pl
from jax.experimental.pallas import tpu as pltpu

F32 = jnp.float32
BF16 = jnp.bfloat16
_MXU = BF16

D = 1024
EPS = 1e-6
CONV_K = 4
GDN_H, GDN_DK, GDN_C = 8, 128, 64
GDN_SCAN_CHUNKS = 4
SSM_H, SSM_P, SSM_L, SSM_N = 16, 64, 128, 128
MEM_H, MEM_HD = 4, 256
D_FF = 4096
N_DEV = 8

C_QKV, C_ZG, C_ZS, C_XBC, C_GATE, C_DT, C_TOT = 0, 3072, 4096, 5120, 6656, 6784, 6912

ADAM_LR, ADAM_B1, ADAM_B2, ADAM_EPS, ADAM_WD, ADAM_STEP = 0.001, 0.9, 0.999, 1e-08, 0.01, 10

VMEM_LIMIT = 56 * 1024 * 1024

_NN = (((1,), (0,)), ((), ()))
_NT = (((1,), (1,)), ((), ()))
_TN = (((0,), (0,)), ((), ()))


def _dot(a, b, dims=_NN):
    return lax.dot_general(a.astype(_MXU), b.astype(_MXU), dims, preferred_element_type=F32)


def _split3(a):
    a1 = a.astype(BF16)
    r1 = a - a1.astype(F32)
    a2 = r1.astype(BF16)
    return a1, a2, (r1 - a2.astype(F32)).astype(BF16)


def _dot_sel(a, e):
    eb = e.astype(BF16)
    return sum(lax.dot_general(p, eb, _NN, preferred_element_type=F32) for p in _split3(a))


def _sel_dot(e, a):
    eb = e.astype(BF16)
    return sum(lax.dot_general(eb, p, _NN, preferred_element_type=F32) for p in _split3(a))


def _chunk_cumsum(a, tri, chunk):
    return jnp.concatenate([_sel_dot(tri, a[r:r + chunk]) for r in range(0, a.shape[0], chunk)], axis=0)


def _params(sem):
    return pltpu.CompilerParams(dimension_semantics=sem, vmem_limit_bytes=VMEM_LIMIT)


def _pick(n, cap):
    for d in range(min(cap, n), 0, -128):
        if n % d == 0 and d % 128 == 0:
            return d
    return n


def _sigmoid(x):
    return 0.5 * jnp.tanh(0.5 * x) + 0.5


def _silu(x):
    return x * _sigmoid(x)


def _dsilu(x):
    s = _sigmoid(x)
    return s * (1.0 + x * (1.0 - s))


def _softplus(x):
    return jnp.maximum(x, 0.0) + jnp.log(1.0 + jnp.exp(-jnp.abs(x)))


def _iota2(shape, axis):
    return lax.broadcasted_iota(jnp.int32, shape, axis)


def _sum_all(x):
    return jnp.sum(jnp.sum(x, axis=1, keepdims=True), axis=0, keepdims=True)


_MESH = pl.DeviceIdType.MESH
_ANY = pl.BlockSpec(memory_space=pl.ANY)
_VM = pl.BlockSpec(memory_space=pltpu.VMEM)
_REL = [(r >> 2 & 1, r >> 1 & 1, r & 1) for r in range(1, N_DEV)]


def _place():
    return lax.axis_index("x"), lax.axis_index("y"), lax.axis_index("c")


class _Ride:
    def __init__(self, srcs, shard):
        self.srcs, self.shard, self.n = list(srcs), shard, len(srcs)
        self.out_shape = [jax.ShapeDtypeStruct(((N_DEV,) + s.shape) if shard else s.shape, s.dtype)
                          for s in self.srcs]
        self.specs = [_ANY] * self.n
        self.scratch = [pltpu.SemaphoreType.DMA((self.n, N_DEV - 1)), pltpu.SemaphoreType.DMA((self.n, N_DEV - 1)),
                        pltpu.SemaphoreType.DMA((self.n,))]

    def _copies(self, in_refs, out_refs, sems):
        send, recv, loc = sems
        x, y, c = _place()
        me = 4 * x + 2 * y + c
        local, remote, arrive = [], [], []
        for a in range(self.n):
            src = in_refs[a] if self.shard else in_refs[a].at[me]
            local.append(pltpu.make_async_copy(src, out_refs[a].at[me], loc.at[a]))
        for k, (rx, ry, rc) in enumerate(_REL):
            peer = (lax.rem(x + rx, 2), lax.rem(y + ry, 2), lax.rem(c + rc, 2))
            ps = 4 * peer[0] + 2 * peer[1] + peer[2]
            for a in range(self.n):
                src = in_refs[a] if self.shard else in_refs[a].at[ps]
                remote.append(pltpu.make_async_remote_copy(
                    src_ref=src, dst_ref=out_refs[a].at[me], send_sem=send.at[a, k], recv_sem=recv.at[a, k],
                    device_id=peer, device_id_type=_MESH))
                slot = out_refs[a].at[ps]
                arrive.append(pltpu.make_async_remote_copy(
                    src_ref=slot, dst_ref=slot, send_sem=send.at[a, k], recv_sem=recv.at[a, k],
                    device_id=peer, device_id_type=_MESH))
        return local, remote, arrive

    def start(self, in_refs, out_refs, sems):
        local, remote, _ = self._copies(in_refs, out_refs, sems)
        for cp in local + remote:
            cp.start()

    def wait(self, in_refs, out_refs, sems):
        local, remote, arrive = self._copies(in_refs, out_refs, sems)
        for cp in arrive:
            cp.wait_recv()
        for cp in remote:
            cp.wait_send()
        for cp in local:
            cp.wait()


_EPI = {
    "none": ((), ("tile",)),
    "res": (("tile",), ("tile",)),
    "mul2": (("tile",), ("tile",)),
    "relu2": ((), ("tile", "tile")),
    "res_norm": (("tile", "row"), ("tile", "tile")),
    "norm_bwd": (("tile", "tile", "row"), ("tile", "row")),
}


def _mm(a, b, *, dims="nn", epi="none", extra=(), out_dtype=F32, name, bm=1024, bn_cap=1024, bk_cap=2048,
        ride=None):
    if dims == "nn":
        (M, K), (K2, N) = a.shape, b.shape
    elif dims == "nt":
        (M, K), (N, K2) = a.shape, b.shape
    else:
        (K, M), (K2, N) = a.shape, b.shape
    assert K == K2, (a.shape, b.shape, dims)
    bm = _pick(M, bm)
    bn = _pick(N, bn_cap)
    bk = _pick(K, bk_cap)
    nk = K // bk
    dn = {"nn": _NN, "nt": _NT, "tn": _TN}[dims]
    a_spec = (pl.BlockSpec((bk, bm), lambda i, j, k: (k, i)) if dims == "tn"
              else pl.BlockSpec((bm, bk), lambda i, j, k: (i, k)))
    b_spec = (pl.BlockSpec((bn, bk), lambda i, j, k: (j, k)) if dims == "nt"
              else pl.BlockSpec((bk, bn), lambda i, j, k: (k, j)))
    o_spec = pl.BlockSpec((bm, bn), lambda i, j, k: (i, j))
    r_spec = pl.BlockSpec((1, bn), lambda i, j, k: (0, j))
    extra = list(extra) if isinstance(extra, (tuple, list)) else [extra]
    ekinds, okinds = _EPI[epi]
    assert len(extra) == len(ekinds) and (epi not in ("res_norm", "norm_bwd") or bn == N)
    n_extra, n_out = len(ekinds), len(okinds)
    n_ride = ride.n if ride else 0
    gi, gj = M // bm, N // bn

    def body(a_ref, b_ref, *rest):
        ex = rest[:n_extra]
        first = pl.program_id(0) == 0
        ride_in = rest[n_extra:n_extra + n_ride]
        outs = rest[n_extra + n_ride:n_extra + n_ride + n_out]
        ride_out = rest[n_extra + n_ride + n_out:n_extra + 2 * n_ride + n_out]
        if ride:
            at = lambda i, j, k: ((pl.program_id(0) == i) & (pl.program_id(1) == j) & (pl.program_id(2) == k))

            @pl.when(at(0, 0, 0))
            def _():
                ride.start(ride_in, ride_out, rest[-3:])

        def finish(r):
            if epi == "res":
                outs[0][...] = (r + ex[0][...].astype(F32)).astype(outs[0].dtype)
            elif epi == "mul2":
                outs[0][...] = (2.0 * r * ex[0][...].astype(F32)).astype(outs[0].dtype)
            elif epi == "relu2":
                u = jnp.maximum(r, 0.0)
                outs[0][...] = u.astype(outs[0].dtype)
                outs[1][...] = (u * u).astype(outs[1].dtype)
            elif epi == "res_norm":
                y = r + ex[0][...]
                outs[0][...] = y
                rstd = lax.rsqrt(jnp.mean(y * y, axis=1, keepdims=True) + EPS)
                outs[1][...] = (y * rstd * ex[1][...]).astype(outs[1].dtype)
            elif epi == "norm_bwd":
                xv = ex[0][...]
                rstd = lax.rsqrt(jnp.mean(xv * xv, axis=1, keepdims=True) + EPS)
                xh = xv * rstd
                dxh = r * ex[2][...]
                outs[0][...] = ex[1][...] + rstd * (dxh - xh * jnp.mean(dxh * xh, axis=1, keepdims=True))
                dw = jnp.sum(r * xh, axis=0, keepdims=True)

                @pl.when(first)
                def _():
                    outs[1][...] = dw

                @pl.when(jnp.logical_not(first))
                def _():
                    outs[1][...] += dw
            else:
                outs[0][...] = r.astype(outs[0].dtype)

        part = _dot(a_ref[...], b_ref[...], dn)
        if nk == 1:
            finish(part)
        else:
            acc = rest[n_extra + 2 * n_ride + n_out]
            k = pl.program_id(2)

            @pl.when(k == 0)
            def _():
                acc[...] = part

            @pl.when((k > 0) & (k < nk - 1))
            def _():
                acc[...] += part

            @pl.when(k == nk - 1)
            def _():
                finish(acc[...] + part)

        if ride:
            @pl.when(at(gi - 1, gj - 1, nk - 1))
            def _():
                ride.wait(ride_in, ride_out, rest[-3:])

    kind_spec = {"tile": o_spec, "row": r_spec}
    ins = [a, b] + [e.reshape(1, N) if k == "row" else e for e, k in zip(extra, ekinds)]
    in_specs = [a_spec, b_spec] + [kind_spec[k] for k in ekinds]
    out_dtypes = {"res_norm": (F32, BF16), "norm_bwd": (F32, F32)}.get(epi, (out_dtype,) * n_out)
    out_shape = [jax.ShapeDtypeStruct((M, N) if k == "tile" else (1, N), dt) for k, dt in zip(okinds, out_dtypes)]
    out_specs = [kind_spec[k] for k in okinds]
    scratch = [pltpu.VMEM((bm, bn), F32)] if nk > 1 else []
    sem = ("arbitrary" if epi == "norm_bwd" else "parallel", "parallel", "arbitrary")
    if ride:
        ins, in_specs = ins + ride.srcs, in_specs + ride.specs
        out_shape, out_specs = out_shape + ride.out_shape, out_specs + ride.specs
        scratch, sem = scratch + ride.scratch, ("arbitrary",) * 3
    res = pl.pallas_call(
        body, grid=(gi, gj, nk), in_specs=in_specs, out_specs=out_specs, out_shape=out_shape,
        scratch_shapes=scratch, name=name, compiler_params=_params(sem))(*ins)
    main = res[:n_out] if n_out > 1 else res[0]
    return (main, list(res[n_out:])) if ride else main


def _rmsnorm_fwd(x, w, *, name, bt=256):
    T, Dm = x.shape
    bt = min(bt, T)

    def body(x_ref, w_ref, h_ref):
        xv = x_ref[...]
        r = lax.rsqrt(jnp.mean(xv * xv, axis=1, keepdims=True) + EPS)
        h_ref[...] = (xv * r * w_ref[...]).astype(h_ref.dtype)

    return pl.pallas_call(
        body, grid=(T // bt,),
        in_specs=[pl.BlockSpec((bt, Dm), lambda i: (i, 0)), pl.BlockSpec((1, Dm), lambda i: (0, 0))],
        out_specs=pl.BlockSpec((bt, Dm), lambda i: (i, 0)),
        out_shape=jax.ShapeDtypeStruct((T, Dm), BF16), name=name,
        compiler_params=_params(("parallel",)))(x, w.reshape(1, Dm))


def _rmsnorm_bwd(x, w, dh, dres, *, name, bt=256):
    T, Dm = x.shape
    bt = min(bt, T)
    has_res = dres is not None

    def body(x_ref, w_ref, dh_ref, *rest):
        dres_ref = rest[0] if has_res else None
        dx_ref, dw_ref = rest[-2], rest[-1]
        i = pl.program_id(0)
        xv = x_ref[...]
        r = lax.rsqrt(jnp.mean(xv * xv, axis=1, keepdims=True) + EPS)
        xh = xv * r
        dhv = dh_ref[...].astype(F32)
        dxh = dhv * w_ref[...]
        dx = r * (dxh - xh * jnp.mean(dxh * xh, axis=1, keepdims=True))
        if has_res:
            dx = dx + dres_ref[...]
        dx_ref[...] = dx

        @pl.when(i == 0)
        def _():
            dw_ref[...] = jnp.zeros_like(dw_ref)

        dw_ref[...] += jnp.sum(dhv * xh, axis=0, keepdims=True)

    row = pl.BlockSpec((bt, Dm), lambda i: (i, 0))
    vec = pl.BlockSpec((1, Dm), lambda i: (0, 0))
    ins = [x, w.reshape(1, Dm), dh] + ([dres] if has_res else [])
    dx, dw = pl.pallas_call(
        body, grid=(T // bt,), in_specs=[row, vec, row] + ([row] if has_res else []),
        out_specs=[row, vec],
        out_shape=[jax.ShapeDtypeStruct((T, Dm), F32), jax.ShapeDtypeStruct((1, Dm), F32)],
        name=name, compiler_params=_params(("arbitrary",)))(*ins)
    return dx, dw.reshape(Dm)


def _final_loss(x, w, tgt, *, bt=256):
    T, Dm = x.shape
    bt = min(bt, T)

    def body(x_ref, w_ref, t_ref, loss_ref, dx_ref, dw_ref):
        i = pl.program_id(0)
        xv = x_ref[...]
        wv = w_ref[...]
        r = lax.rsqrt(jnp.mean(xv * xv, axis=1, keepdims=True) + EPS)
        xh = xv * r
        err = xh * wv - t_ref[...]
        part = 0.5 * jnp.sum(jnp.mean(err * err, axis=1, keepdims=True), axis=0, keepdims=True)
        dy = err * (1.0 / Dm)
        dxh = dy * wv
        dx_ref[...] = r * (dxh - xh * jnp.mean(dxh * xh, axis=1, keepdims=True))

        @pl.when(i == 0)
        def _():
            dw_ref[...] = jnp.zeros_like(dw_ref)
            loss_ref[...] = jnp.zeros_like(loss_ref)

        dw_ref[...] += jnp.sum(dy * xh, axis=0, keepdims=True)
        loss_ref[...] += jnp.broadcast_to(part, loss_ref.shape)

    row = pl.BlockSpec((bt, Dm), lambda i: (i, 0))
    vec = pl.BlockSpec((1, Dm), lambda i: (0, 0))
    loss, dx, dw = pl.pallas_call(
        body, grid=(T // bt,), in_specs=[row, vec, row],
        out_specs=[pl.BlockSpec((1, 128), lambda i: (0, 0)), row, vec],
        out_shape=[jax.ShapeDtypeStruct((1, 128), F32), jax.ShapeDtypeStruct((T, Dm), F32),
                   jax.ShapeDtypeStruct((1, Dm), F32)],
        name="final_loss", compiler_params=_params(("arbitrary",)))(x, w.reshape(1, Dm), tgt)
    return loss, dx, dw.reshape(Dm)


def _attn_fwd(q, km, vm, *, bt=256):
    T = q.shape[0]
    M = km.shape[0]
    bt = min(bt, T)
    scale = MEM_HD ** -0.5

    def body(q_ref, k_ref, v_ref, o_ref):
        for h in range(MEM_H):
            sl = slice(h * MEM_HD, (h + 1) * MEM_HD)
            s = _dot(q_ref[:, sl], k_ref[:, sl], _NT) * scale
            s = s - jnp.max(s, axis=1, keepdims=True)
            e = jnp.exp(s)
            p = e / jnp.sum(e, axis=1, keepdims=True)
            o_ref[:, sl] = _dot(p, v_ref[:, sl]).astype(o_ref.dtype)

    row = pl.BlockSpec((bt, D), lambda i: (i, 0))
    mem = pl.BlockSpec((M, D), lambda i: (0, 0))
    return pl.pallas_call(
        body, grid=(T // bt,), in_specs=[row, mem, mem], out_specs=row,
        out_shape=jax.ShapeDtypeStruct((T, D), BF16), name="attn_fwd",
        compiler_params=_params(("parallel",)))(q, km, vm)


def _attn_bwd(q, km, vm, do, *, bt=256):
    T = q.shape[0]
    M = km.shape[0]
    bt = min(bt, T)
    scale = MEM_HD ** -0.5

    def body(q_ref, k_ref, v_ref, do_ref, dq_ref, dk_ref, dv_ref):
        i = pl.program_id(0)

        @pl.when(i == 0)
        def _():
            dk_ref[...] = jnp.zeros_like(dk_ref)
            dv_ref[...] = jnp.zeros_like(dv_ref)

        for h in range(MEM_H):
            sl = slice(h * MEM_HD, (h + 1) * MEM_HD)
            qh, kh, vh, doh = q_ref[:, sl], k_ref[:, sl], v_ref[:, sl], do_ref[:, sl]
            s = _dot(qh, kh, _NT) * scale
            s = s - jnp.max(s, axis=1, keepdims=True)
            e = jnp.exp(s)
            p = e / jnp.sum(e, axis=1, keepdims=True)
            dp = _dot(doh, vh, _NT)
            ds = p * (dp - jnp.sum(dp * p, axis=1, keepdims=True)) * scale
            dq_ref[:, sl] = _dot(ds, kh).astype(dq_ref.dtype)
            dk_ref[:, sl] += _dot(ds, qh, _TN)
            dv_ref[:, sl] += _dot(p, doh, _TN)

    row = pl.BlockSpec((bt, D), lambda i: (i, 0))
    mem = pl.BlockSpec((M, D), lambda i: (0, 0))
    return pl.pallas_call(
        body, grid=(T // bt,), in_specs=[row, mem, mem, row], out_specs=[row, mem, mem],
        out_shape=[jax.ShapeDtypeStruct((T, D), BF16), jax.ShapeDtypeStruct((M, D), F32),
                   jax.ShapeDtypeStruct((M, D), F32)],
        name="attn_bwd", compiler_params=_params(("arbitrary",)))(q, km, vm, do)


def _conv_apply(halo, x, w_ref, b_ref):
    bt = x.shape[0]
    cat = jnp.concatenate([halo, x], axis=0)
    y = x * w_ref[3:4, :]
    for k in range(CONV_K - 1):
        y = y + pltpu.roll(cat, CONV_K - 1 - k, 0)[8:8 + bt] * w_ref[k:k + 1, :]
    if b_ref is not None:
        y = y + b_ref[...]
    return y


def _l2_parts(act, bc):
    out = []
    for s in range(bc // 128):
        a = act[:, s * 128:(s + 1) * 128]
        r = lax.rsqrt(jnp.sum(a * a, axis=1, keepdims=True) + EPS)
        out.append((a, r))
    return out


def _conv_fwd(p, col0, C, w, b, *, l2, name, bt=512, bc=1024):
    T = p.shape[0]
    bt = min(bt, T)
    c0, hb = col0 // bc, bt // 8
    has_b = b is not None
    assert not l2 or (bc == D and C == 2 * D)

    def body(x_ref, halo_ref, w_ref, *rest):
        b_ref = rest[0] if has_b else None
        o_ref = rest[-1]
        i, j = pl.program_id(0), pl.program_id(1)
        x = x_ref[...]
        halo = jnp.where(i > 0, halo_ref[...], 0.0)
        act = _silu(_conv_apply(halo, x, w_ref, b_ref))
        if l2:
            sc = jnp.where(j == 0, GDN_DK ** -0.5, 1.0)
            o_ref[...] = jnp.concatenate([a * (r * sc) for a, r in _l2_parts(act, bc)], axis=1)
        else:
            o_ref[...] = act

    in_specs = [pl.BlockSpec((bt, bc), lambda i, j: (i, c0 + j)),
                pl.BlockSpec((8, bc), lambda i, j: (jnp.maximum(i * hb - 1, 0), c0 + j)),
                pl.BlockSpec((CONV_K, bc), lambda i, j: (0, j))]
    ins = [p, p, w]
    if has_b:
        in_specs.append(pl.BlockSpec((1, bc), lambda i, j: (0, j)))
        ins.append(b.reshape(1, C))
    return pl.pallas_call(
        body, grid=(T // bt, C // bc), in_specs=in_specs,
        out_specs=pl.BlockSpec((bt, bc), lambda i, j: (i, j)),
        out_shape=jax.ShapeDtypeStruct((T, C), F32), name=name,
        compiler_params=_params(("parallel", "parallel")))(*ins)


def _conv_bwd_act(p, col0, C, w, b, dact, dcol0, *, l2, name, bt=512, bc=1024):
    T = p.shape[0]
    bt = min(bt, T)
    c0, d0, hb = col0 // bc, dcol0 // bc, bt // 8
    has_b = b is not None
    assert not l2 or (bc == D and C == 2 * D)

    def body(x_ref, halo_ref, w_ref, *rest):
        b_ref = rest[0] if has_b else None
        dact_ref, dy_ref, dw_ref, db_ref = rest[-4:]
        j, i = pl.program_id(0), pl.program_id(1)
        x = x_ref[...]
        halo = jnp.where(i > 0, halo_ref[...], 0.0)
        y = _conv_apply(halo, x, w_ref, b_ref)
        dact = dact_ref[...]
        if l2:
            sc = jnp.where(j == 0, GDN_DK ** -0.5, 1.0)
            parts = []
            for s, (a, r) in enumerate(_l2_parts(_silu(y), bc)):
                n = a * r
                dn = dact[:, s * 128:(s + 1) * 128]
                parts.append((r * sc) * (dn - n * jnp.sum(dn * n, axis=1, keepdims=True)))
            dact = jnp.concatenate(parts, axis=1)
        dy = dact * _dsilu(y)
        dy_ref[...] = dy

        @pl.when(i == 0)
        def _():
            dw_ref[...] = jnp.zeros_like(dw_ref)
            db_ref[...] = jnp.zeros_like(db_ref)

        db_ref[...] += jnp.sum(dy, axis=0, keepdims=True)
        cat = jnp.concatenate([halo, x], axis=0)
        dw_ref[3:4, :] += jnp.sum(dy * x, axis=0, keepdims=True)
        for k in range(CONV_K - 1):
            xs = pltpu.roll(cat, CONV_K - 1 - k, 0)[8:8 + bt]
            dw_ref[k:k + 1, :] += jnp.sum(dy * xs, axis=0, keepdims=True)

    in_specs = [pl.BlockSpec((bt, bc), lambda j, i: (i, c0 + j)),
                pl.BlockSpec((8, bc), lambda j, i: (jnp.maximum(i * hb - 1, 0), c0 + j)),
                pl.BlockSpec((CONV_K, bc), lambda j, i: (0, j))]
    ins = [p, p, w]
    if has_b:
        in_specs.append(pl.BlockSpec((1, bc), lambda j, i: (0, j)))
        ins.append(b.reshape(1, C))
    in_specs.append(pl.BlockSpec((bt, bc), lambda j, i: (i, d0 + j)))
    ins.append(dact)
    dy, dw, db = pl.pallas_call(
        body, grid=(C // bc, T // bt), in_specs=in_specs,
        out_specs=[pl.BlockSpec((bt, bc), lambda j, i: (i, j)),
                   pl.BlockSpec((CONV_K, bc), lambda j, i: (0, j)),
                   pl.BlockSpec((1, bc), lambda j, i: (0, j))],
        out_shape=[jax.ShapeDtypeStruct((T, C), F32), jax.ShapeDtypeStruct((CONV_K, C), F32),
                   jax.ShapeDtypeStruct((1, C), F32)],
        name=name, compiler_params=_params(("parallel", "arbitrary")))(*ins)
    return dy, dw, db.reshape(C)


def _conv_bwd_in(dy, w, dp_in, col0, T, *, name, bt=512, bc=1024):
    C = dy.shape[1]
    bt = min(bt, T)
    c0, hb, nb = col0 // bc, bt // 8, T // bt

    def body(dy_ref, nxt_ref, w_ref, *rest):
        o_ref = rest[-1]
        i = pl.program_id(0)
        dy_v = dy_ref[...]
        nxt = jnp.where(i < nb - 1, nxt_ref[...], 0.0)
        cat = jnp.concatenate([dy_v, nxt], axis=0)
        dx = dy_v * w_ref[3:4, :]
        for k in range(CONV_K - 1):
            s = CONV_K - 1 - k
            dx = dx + pltpu.roll(cat, bt + 8 - s, 0)[0:bt] * w_ref[k:k + 1, :]
        o_ref[...] = dx.astype(o_ref.dtype)

    in_specs = [pl.BlockSpec((bt, bc), lambda i, j: (i, j)),
                pl.BlockSpec((8, bc), lambda i, j: (jnp.minimum((i + 1) * hb, T // 8 - 1), j)),
                pl.BlockSpec((CONV_K, bc), lambda i, j: (0, j))]
    ins = [dy, dy, w]
    alias = {}
    if dp_in is not None:
        in_specs.append(pl.BlockSpec(memory_space=pl.ANY))
        ins.append(dp_in)
        alias = {3: 0}
    return pl.pallas_call(
        body, grid=(nb, C // bc), in_specs=in_specs,
        out_specs=pl.BlockSpec((bt, bc), lambda i, j: (i, c0 + j)),
        out_shape=jax.ShapeDtypeStruct((T, C_TOT), BF16), input_output_aliases=alias, name=name,
        compiler_params=_params(("parallel", "parallel")))(*ins)


def _expand_mats(shift, row0):
    e = (_iota2((128, D), 0) - row0 == (_iota2((128, D), 1) >> shift)).astype(F32)
    et = ((_iota2((D, 128), 0) >> shift) == _iota2((D, 128), 1) - row0).astype(F32)
    return e, et


def _cum_mats(chunk):
    ri, ci = _iota2((chunk, chunk), 0), _iota2((chunk, chunk), 1)
    return (ri >= ci).astype(F32), (ri <= ci).astype(F32)


def _gdn_gates_fwd(p, alog_row, dtb_row, *, bt=256):
    T = p.shape[0]
    bt = min(bt, T)

    def body(g_ref, al_ref, db_ref, beta_ref, gam_ref):
        gt = g_ref[...]
        eb, _ = _expand_mats(7, 0)
        eg, _ = _expand_mats(7, GDN_H)
        lc, _ = _cum_mats(GDN_C)
        beta_l = _sigmoid(gt)
        g_l = -jnp.exp(al_ref[...]) * _softplus(gt + db_ref[...])
        beta_ref[...] = _dot_sel(beta_l, eb)
        gam_ref[...] = _chunk_cumsum(_dot_sel(g_l, eg), lc, GDN_C)

    vec = pl.BlockSpec((1, 128), lambda i: (0, 0))
    row = pl.BlockSpec((bt, D), lambda i: (i, 0))
    return pl.pallas_call(
        body, grid=(T // bt,),
        in_specs=[pl.BlockSpec((bt, 128), lambda i: (i, C_GATE // 128)), vec, vec],
        out_specs=[row, row],
        out_shape=[jax.ShapeDtypeStruct((T, D), F32)] * 2, name="gdn_gates_fwd",
        compiler_params=_params(("parallel",)))(p, alog_row, dtb_row)


def _gdn_gates_bwd(p, alog_row, dtb_row, dbeta_x, dgam_x, dp_in, *, bt=256):
    T = p.shape[0]
    bt = min(bt, T)

    def body(g_ref, al_ref, db_ref, dbeta_ref, dgam_ref, dpin_ref, dg_out, dal_ref, ddb_ref):
        i = pl.program_id(0)
        gt = g_ref[...]
        _, ebt = _expand_mats(7, 0)
        _, egt = _expand_mats(7, GDN_H)
        _, uc = _cum_mats(GDN_C)
        ea = jnp.exp(al_ref[...])
        zz = gt + db_ref[...]
        g_l = -ea * _softplus(zz)
        beta_l = _sigmoid(gt)
        dg_l = _dot_sel(_chunk_cumsum(dgam_ref[...], uc, GDN_C), egt)
        dbeta_l = _dot_sel(dbeta_ref[...], ebt)
        da = dg_l * (-ea) * _sigmoid(zz)
        dg_out[...] = (da + dbeta_l * beta_l * (1.0 - beta_l)).astype(dg_out.dtype)

        @pl.when(i == 0)
        def _():
            dal_ref[...] = jnp.zeros_like(dal_ref)
            ddb_ref[...] = jnp.zeros_like(ddb_ref)

        dal_ref[...] += jnp.sum(dg_l * g_l, axis=0, keepdims=True)
        ddb_ref[...] += jnp.sum(da, axis=0, keepdims=True)

    vec = pl.BlockSpec((1, 128), lambda i: (0, 0))
    row = pl.BlockSpec((bt, D), lambda i: (i, 0))
    gate = pl.BlockSpec((bt, 128), lambda i: (i, C_GATE // 128))
    return pl.pallas_call(
        body, grid=(T // bt,),
        in_specs=[gate, vec, vec, row, row, pl.BlockSpec(memory_space=pl.ANY)],
        out_specs=[gate, vec, vec],
        out_shape=[jax.ShapeDtypeStruct((T, C_TOT), BF16), jax.ShapeDtypeStruct((1, 128), F32),
                   jax.ShapeDtypeStruct((1, 128), F32)],
        input_output_aliases={5: 0}, name="gdn_gates_bwd",
        compiler_params=_params(("arbitrary",)))(p, alog_row, dtb_row, dbeta_x, dgam_x, dp_in)


def _ssd_dt_fwd(p, dtb_row, alog_x, *, bt=256):
    T = p.shape[0]
    bt = min(bt, T)

    def body(d_ref, db_ref, al_ref, dt_ref, alpha_ref):
        ed, _ = _expand_mats(6, 0)
        lc, _ = _cum_mats(SSM_L)
        dt_x = _dot_sel(_softplus(d_ref[...] + db_ref[...]), ed)
        dt_ref[...] = dt_x
        alpha_ref[...] = _chunk_cumsum(dt_x * (-jnp.exp(al_ref[...])), lc, SSM_L)

    row = pl.BlockSpec((bt, D), lambda i: (i, 0))
    return pl.pallas_call(
        body, grid=(T // bt,),
        in_specs=[pl.BlockSpec((bt, 128), lambda i: (i, C_DT // 128)),
                  pl.BlockSpec((1, 128), lambda i: (0, 0)), pl.BlockSpec((1, D), lambda i: (0, 0))],
        out_specs=[row, row], out_shape=[jax.ShapeDtypeStruct((T, D), F32)] * 2,
        name="ssd_dt_fwd", compiler_params=_params(("parallel",)))(p, dtb_row, alog_x)


def _ssd_dt_bwd(p, dtb_row, alog_x, ddt_x, dalpha_x, dp_in, *, bt=256):
    T = p.shape[0]
    bt = min(bt, T)

    def body(d_ref, db_ref, al_ref, ddt_ref, dal_ref, dpin_ref, dd_out, ddb_ref, dalog_ref):
        i = pl.program_id(0)
        ed, edt = _expand_mats(6, 0)
        _, uc = _cum_mats(SSM_L)
        zz = d_ref[...] + db_ref[...]
        dt_x = _dot_sel(_softplus(zz), ed)
        a_x = -jnp.exp(al_ref[...])
        da_x = _chunk_cumsum(dal_ref[...], uc, SSM_L)
        ddt_l = _dot_sel(ddt_ref[...] + da_x * a_x, edt)
        draw = ddt_l * _sigmoid(zz)
        dd_out[...] = draw.astype(dd_out.dtype)

        @pl.when(i == 0)
        def _():
            ddb_ref[...] = jnp.zeros_like(ddb_ref)
            dalog_ref[...] = jnp.zeros_like(dalog_ref)

        ddb_ref[...] += jnp.sum(draw, axis=0, keepdims=True)
        dalog_ref[...] += jnp.sum(da_x * dt_x, axis=0, keepdims=True) * a_x

    row = pl.BlockSpec((bt, D), lambda i: (i, 0))
    seg = pl.BlockSpec((bt, 128), lambda i: (i, C_DT // 128))
    v128 = pl.BlockSpec((1, 128), lambda i: (0, 0))
    vD = pl.BlockSpec((1, D), lambda i: (0, 0))
    return pl.pallas_call(
        body, grid=(T // bt,),
        in_specs=[seg, v128, vD, row, row, pl.BlockSpec(memory_space=pl.ANY)],
        out_specs=[seg, v128, vD],
        out_shape=[jax.ShapeDtypeStruct((T, C_TOT), BF16), jax.ShapeDtypeStruct((1, 128), F32),
                   jax.ShapeDtypeStruct((1, D), F32)],
        input_output_aliases={5: 0}, name="ssd_dt_bwd",
        compiler_params=_params(("arbitrary",)))(p, dtb_row, alog_x, ddt_x, dalpha_x, dp_in)


def _gdn_post_fwd(o, p, w_x, *, bt=256):
    T = o.shape[0]
    bt = min(bt, T)

    def body(o_ref, z_ref, w_ref, out_ref):
        for h in range(GDN_H):
            sl = slice(h * 128, (h + 1) * 128)
            oh = o_ref[:, sl]
            r = lax.rsqrt(jnp.mean(oh * oh, axis=1, keepdims=True) + EPS)
            out_ref[:, sl] = (oh * r * w_ref[:, sl] * _silu(z_ref[:, sl])).astype(out_ref.dtype)

    row = pl.BlockSpec((bt, D), lambda i: (i, 0))
    return pl.pallas_call(
        body, grid=(T // bt,),
        in_specs=[row, pl.BlockSpec((bt, D), lambda i: (i, C_ZG // D)), pl.BlockSpec((1, D), lambda i: (0, 0))],
        out_specs=row, out_shape=jax.ShapeDtypeStruct((T, 2 * D), BF16), name="gdn_post_fwd",
        compiler_params=_params(("parallel",)))(o, p, w_x)


def _gdn_post_bwd(dmix, o, p, w_x, *, bt=256):
    T = o.shape[0]
    bt = min(bt, T)

    def body(dm_ref, o_ref, z_ref, w_ref, do_ref, dz_ref, dw_ref):
        i = pl.program_id(0)

        @pl.when(i == 0)
        def _():
            dw_ref[...] = jnp.zeros_like(dw_ref)

        for h in range(GDN_H):
            sl = slice(h * 128, (h + 1) * 128)
            oh, zh, wh, dm = o_ref[:, sl], z_ref[:, sl], w_ref[:, sl], dm_ref[:, sl]
            r = lax.rsqrt(jnp.mean(oh * oh, axis=1, keepdims=True) + EPS)
            ohat = oh * r
            dy = dm * _silu(zh)
            dz_ref[:, sl] = (dm * ohat * wh * _dsilu(zh)).astype(dz_ref.dtype)
            dohat = dy * wh
            do_ref[:, sl] = r * (dohat - ohat * jnp.mean(dohat * ohat, axis=1, keepdims=True))
            dw_ref[:, sl] += jnp.sum(dy * ohat, axis=0, keepdims=True)

    row = pl.BlockSpec((bt, D), lambda i: (i, 0))
    zcol = pl.BlockSpec((bt, D), lambda i: (i, C_ZG // D))
    vec = pl.BlockSpec((1, D), lambda i: (0, 0))
    return pl.pallas_call(
        body, grid=(T // bt,), in_specs=[row, row, zcol, vec], out_specs=[row, zcol, vec],
        out_shape=[jax.ShapeDtypeStruct((T, D), F32), jax.ShapeDtypeStruct((T, C_TOT), BF16),
                   jax.ShapeDtypeStruct((1, D), F32)],
        name="gdn_post_bwd", compiler_params=_params(("arbitrary",)))(dmix, o, p, w_x)


def _ssd_post_fwd(y, xs, p, d_x, w, mix_in, *, bt=256):
    T = y.shape[0]
    bt = min(bt, T)

    def body(y_ref, x_ref, z_ref, d_ref, w_ref, mix_ref, out_ref):
        yg = (y_ref[...] + x_ref[...] * d_ref[...]) * _silu(z_ref[...])
        for g in range(2):
            sl = slice(g * 512, (g + 1) * 512)
            a = yg[:, sl]
            r = lax.rsqrt(jnp.mean(a * a, axis=1, keepdims=True) + EPS)
            out_ref[:, sl] = (a * r * w_ref[:, sl]).astype(out_ref.dtype)

    row = pl.BlockSpec((bt, D), lambda i: (i, 0))
    vec = pl.BlockSpec((1, D), lambda i: (0, 0))
    return pl.pallas_call(
        body, grid=(T // bt,),
        in_specs=[row, row, pl.BlockSpec((bt, D), lambda i: (i, C_ZS // D)), vec, vec, _ANY],
        out_specs=pl.BlockSpec((bt, D), lambda i: (i, 1)), out_shape=jax.ShapeDtypeStruct((T, 2 * D), BF16),
        input_output_aliases={5: 0}, name="ssd_post_fwd",
        compiler_params=_params(("parallel",)))(y, xs, p, d_x, w, mix_in)


def _ssd_post_bwd(dmix, y, xs, p, d_x, w, dp_in, *, bt=256):
    T = y.shape[0]
    bt = min(bt, T)

    def body(dm_ref, y_ref, x_ref, z_ref, d_ref, w_ref, dpin_ref, dyy_ref, dz_ref, dd_ref, dw_ref):
        i = pl.program_id(0)

        @pl.when(i == 0)
        def _():
            dd_ref[...] = jnp.zeros_like(dd_ref)
            dw_ref[...] = jnp.zeros_like(dw_ref)

        xv, zv = x_ref[...], z_ref[...]
        yy = y_ref[...] + xv * d_ref[...]
        sz = _silu(zv)
        yg = yy * sz
        parts = []
        for g in range(2):
            sl = slice(g * 512, (g + 1) * 512)
            a = yg[:, sl]
            r = lax.rsqrt(jnp.mean(a * a, axis=1, keepdims=True) + EPS)
            ah = a * r
            dout = dm_ref[:, sl]
            dah = dout * w_ref[:, sl]
            dw_ref[:, sl] += jnp.sum(dout * ah, axis=0, keepdims=True)
            parts.append(r * (dah - ah * jnp.mean(dah * ah, axis=1, keepdims=True)))
        dyg = jnp.concatenate(parts, axis=1)
        dyy = dyg * sz
        dyy_ref[...] = dyy
        dz_ref[...] = (dyg * yy * _dsilu(zv)).astype(dz_ref.dtype)
        dd_ref[...] += jnp.sum(dyy * xv, axis=0, keepdims=True)

    row = pl.BlockSpec((bt, D), lambda i: (i, 0))
    zcol = pl.BlockSpec((bt, D), lambda i: (i, C_ZS // D))
    vec = pl.BlockSpec((1, D), lambda i: (0, 0))
    return pl.pallas_call(
        body, grid=(T // bt,),
        in_specs=[pl.BlockSpec((bt, D), lambda i: (i, 1)), row, row, zcol, vec, vec, _ANY],
        out_specs=[row, zcol, vec, vec],
        out_shape=[jax.ShapeDtypeStruct((T, D), F32), jax.ShapeDtypeStruct((T, C_TOT), BF16),
                   jax.ShapeDtypeStruct((1, D), F32), jax.ShapeDtypeStruct((1, D), F32)],
        input_output_aliases={6: 1}, name="ssd_post_bwd",
        compiler_params=_params(("arbitrary",)))(dmix, y, xs, p, d_x, w, dp_in)


_NEG = -1e30


def _gdn_terms(q, k, v, bx, gam_c):
    C = GDN_C
    ri, ci = _iota2((C, C), 0), _iota2((C, C), 1)
    eye, low, strict = ri == ci, ri >= ci, ri > ci
    gam_r = jnp.sum(jnp.where(eye, gam_c, 0.0), axis=0, keepdims=True)
    G = jnp.exp(jnp.where(low, gam_c - gam_r, _NEG))
    glast = jnp.sum(jnp.where(_iota2((C, 1), 0) == C - 1, gam_c, 0.0), axis=0, keepdims=True)
    eg, egl, eL = jnp.exp(gam_c), jnp.exp(glast - gam_c), jnp.exp(glast)
    kb, vb = k * bx, v * bx
    M = _dot(kb, k, _NT)
    return dict(eye=eye, low=low, strict=strict, G=G, eg=eg, egl=egl, eL=eL, kb=kb, vb=vb, M=M,
                kbg=kb * eg, qd=q * eg, kd=k * egl, q=q, k=k, v=v, bx=bx)


def _split(a):
    hi = a.astype(_MXU)
    return hi, (a - hi.astype(F32)).astype(_MXU)


def _dot3s(a, b):
    d = lambda p, q: lax.dot_general(p, q, _NN, preferred_element_type=F32)
    return d(a[0], b[0]) + d(a[0], b[1]) + d(a[1], b[0])


def _tri_inv_many(Ls, eye):
    eyef = jnp.where(eye, 1.0, 0.0)
    Ts = [eyef - L for L in Ls]
    Ps = [-L for L in Ls]
    for _ in range(5):
        sp = [_split(p) for p in Ps]
        Ps = [_dot3s(s, s) for s in sp]
        sp = [_split(p) for p in Ps]
        st = [_split(t) for t in Ts]
        Ts = [t + _dot3s(a, b) for t, a, b in zip(Ts, st, sp)]
    return Ts


def _gdn_heads(q_ref, k_ref, v_ref, bx_ref, gx_ref):
    out = []
    for h in range(GDN_H):
        sl = slice(h * 128, (h + 1) * 128)
        gam_c = jnp.max(gx_ref[:, sl], axis=1, keepdims=True)
        out.append(_gdn_terms(q_ref[:, sl], k_ref[:, sl], v_ref[:, sl], bx_ref[:, sl], gam_c))
    return out


def _gdn_prep(qk, v, bx, gx, ride=None):
    T = qk.shape[0]
    N = T // GDN_C
    C = GDN_C
    n_ride = ride.n if ride else 0

    def body(q_ref, k_ref, v_ref, bx_ref, gx_ref, *rest):
        ride_in = rest[:n_ride]
        u_ref, w_ref, qd_ref, kd_ref, p_ref, t_ref = rest[n_ride:n_ride + 6]
        ride_out = rest[n_ride + 6:2 * n_ride + 6]
        if ride:
            @pl.when(pl.program_id(0) == 0)
            def _():
                ride.start(ride_in, ride_out, rest[-3:])

            @pl.when(pl.program_id(0) == N - 1)
            def _():
                ride.wait(ride_in, ride_out, rest[-3:])

        ts = _gdn_heads(q_ref, k_ref, v_ref, bx_ref, gx_ref)
        Ts = _tri_inv_many([jnp.where(t["strict"], t["M"] * t["G"], 0.0) for t in ts], ts[0]["eye"])
        for h, (t, Tm) in enumerate(zip(ts, Ts)):
            sl = slice(h * 128, (h + 1) * 128)
            rows = slice(h * C, (h + 1) * C)
            u_ref[:, sl] = _dot(Tm, t["vb"])
            w_ref[:, sl] = _dot(Tm, t["kbg"]).astype(w_ref.dtype)
            qd_ref[:, sl] = t["qd"].astype(qd_ref.dtype)
            kd_ref[:, sl] = t["kd"].astype(kd_ref.dtype)
            p_ref[0, rows, :] = _dot(t["q"], t["k"], _NT) * t["G"]
            t_ref[0, rows, :] = Tm

    blk = lambda c: pl.BlockSpec((C, D), lambda n: (n, c))
    sq = pl.BlockSpec((1, GDN_H * C, C), lambda n: (n, 0, 0))
    in_specs = [blk(0), blk(1), blk(0), blk(0), blk(0)]
    out_specs = [blk(0), blk(0), blk(0), blk(0), sq, sq]
    out_shape = [jax.ShapeDtypeStruct((T, D), F32), jax.ShapeDtypeStruct((T, D), BF16),
                 jax.ShapeDtypeStruct((T, D), BF16), jax.ShapeDtypeStruct((T, D), BF16),
                 jax.ShapeDtypeStruct((N, GDN_H * C, C), F32), jax.ShapeDtypeStruct((N, GDN_H * C, C), F32)]
    ins = [qk, qk, v, bx, gx]
    if ride:
        ins, in_specs = ins + ride.srcs, in_specs + ride.specs
        out_shape, out_specs = out_shape + ride.out_shape, out_specs + ride.specs
    res = pl.pallas_call(
        body, grid=(N,), in_specs=in_specs, out_specs=out_specs, out_shape=out_shape,
        scratch_shapes=ride.scratch if ride else [], name="gdn_prep",
        compiler_params=_params(("arbitrary",) if ride else ("parallel",)))(*ins)
    return (list(res[:6]), list(res[6:])) if ride else list(res)


def _gdn_scan_fwd(u, w, qd, kd, pm, gx):
    T = u.shape[0]
    N = T // GDN_C
    C, CS = GDN_C, GDN_SCAN_CHUNKS

    def body(u_ref, w_ref, qd_ref, kd_ref, p_ref, gx_ref, o_ref, vn_ref, ss_ref, S_scr):
        n = pl.program_id(0)

        @pl.when(n == 0)
        def _():
            S_scr[...] = jnp.zeros_like(S_scr)

        sls = [slice(h * 128, (h + 1) * 128) for h in range(GDN_H)]
        for c in range(CS):
            rows = slice(c * C, (c + 1) * C)
            Ss = [S_scr[:, sl] for sl in sls]
            vns = [u_ref[rows, sl] - _dot(w_ref[rows, sl], S) for sl, S in zip(sls, Ss)]
            for h, (sl, S, vn) in enumerate(zip(sls, Ss, vns)):
                ss_ref[c, :, sl] = S
                vn_ref[rows, sl] = vn.astype(vn_ref.dtype)
                o_ref[rows, sl] = _dot(qd_ref[rows, sl], S) + _dot(p_ref[c, h * C:(h + 1) * C, :], vn)
                S_scr[:, sl] = (S * jnp.exp(gx_ref[(c + 1) * C - 1:(c + 1) * C, sl])
                                + _dot(kd_ref[rows, sl], vn, _TN))

    blk = pl.BlockSpec((CS * C, D), lambda n: (n, 0))
    return pl.pallas_call(
        body, grid=(N // CS,),
        in_specs=[blk, blk, blk, blk, pl.BlockSpec((CS, GDN_H * C, C), lambda n: (n, 0, 0)), blk],
        out_specs=[blk, blk, pl.BlockSpec((CS, GDN_DK, D), lambda n: (n, 0, 0))],
        out_shape=[jax.ShapeDtypeStruct((T, D), F32), jax.ShapeDtypeStruct((T, D), BF16),
                   jax.ShapeDtypeStruct((N, GDN_DK, D), F32)],
        scratch_shapes=[pltpu.VMEM((GDN_DK, D), F32)], name="gdn_scan_fwd",
        compiler_params=_params(("arbitrary",)))(u, w, qd, kd, pm, gx)


def _gdn_scan_bwd(w, qd, kd, pm, gx, do):
    T = w.shape[0]
    N = T // GDN_C
    C, CS = GDN_C, GDN_SCAN_CHUNKS
    NB = N // CS

    def body(w_ref, qd_ref, kd_ref, p_ref, gx_ref, do_ref, dvn_ref, ds_ref, dS_scr):
        n = pl.program_id(0)

        @pl.when(n == 0)
        def _():
            dS_scr[...] = jnp.zeros_like(dS_scr)

        sls = [slice(h * 128, (h + 1) * 128) for h in range(GDN_H)]
        for c in reversed(range(CS)):
            rows = slice(c * C, (c + 1) * C)
            dSs = [dS_scr[:, sl] for sl in sls]
            dvns = [_dot(p_ref[c, h * C:(h + 1) * C, :], do_ref[rows, sl], _TN) + _dot(kd_ref[rows, sl], dS2)
                    for h, (sl, dS2) in enumerate(zip(sls, dSs))]
            for sl, dS2, dvn in zip(sls, dSs, dvns):
                ds_ref[c, :, sl] = dS2
                dvn_ref[rows, sl] = dvn.astype(dvn_ref.dtype)
                dS_scr[:, sl] = (dS2 * jnp.exp(gx_ref[(c + 1) * C - 1:(c + 1) * C, sl])
                                 + _dot(qd_ref[rows, sl], do_ref[rows, sl], _TN) - _dot(w_ref[rows, sl], dvn, _TN))

    blk = pl.BlockSpec((CS * C, D), lambda n: (NB - 1 - n, 0))
    return pl.pallas_call(
        body, grid=(NB,),
        in_specs=[blk, blk, blk, pl.BlockSpec((CS, GDN_H * C, C), lambda n: (NB - 1 - n, 0, 0)), blk, blk],
        out_specs=[blk, pl.BlockSpec((CS, GDN_DK, D), lambda n: (NB - 1 - n, 0, 0))],
        out_shape=[jax.ShapeDtypeStruct((T, D), BF16), jax.ShapeDtypeStruct((N, GDN_DK, D), F32)],
        scratch_shapes=[pltpu.VMEM((GDN_DK, D), F32)], name="gdn_scan_bwd",
        compiler_params=_params(("arbitrary",)))(w, qd, kd, pm, gx, do)


def _gdn_rest_bwd(qk, v, bx, gx, s_save, t_save, vn, dvn, ds_save, do, ride=None):
    T = qk.shape[0]
    N = T // GDN_C
    C = GDN_C
    n_ride = ride.n if ride else 0

    def body(q_ref, k_ref, v_ref, bx_ref, gx_ref, ss_ref, ts_ref, vn_ref, dvn_ref, ds_ref, do_ref, *rest):
        ride_in = rest[:n_ride]
        dqkv_ref, dbx_ref, dgx_ref = rest[n_ride:n_ride + 3]
        ride_out = rest[n_ride + 3:2 * n_ride + 3]
        if ride:
            @pl.when(pl.program_id(0) == 0)
            def _():
                ride.start(ride_in, ride_out, rest[-3:])

            @pl.when(pl.program_id(0) == N - 1)
            def _():
                ride.wait(ride_in, ride_out, rest[-3:])

        H = range(GDN_H)
        sls = [slice(h * 128, (h + 1) * 128) for h in H]
        ts = _gdn_heads(q_ref, k_ref, v_ref, bx_ref, gx_ref)
        Ss = [ss_ref[0, :, sl] for sl in sls]
        Tms = [ts_ref[0, h * C:(h + 1) * C, :] for h in H]
        dS2s = [ds_ref[0, :, sl] for sl in sls]
        dos = [do_ref[:, sl] for sl in sls]
        vns = [vn_ref[:, sl] for sl in sls]
        dvns = [dvn_ref[:, sl] for sl in sls]
        Qs = [_dot(t["q"], t["k"], _NT) for t in ts]
        dws = [-_dot(dvn, S, _NT) for dvn, S in zip(dvns, Ss)]
        dqds = [_dot(do, S, _NT) for do, S in zip(dos, Ss)]
        dPs = [jnp.where(t["low"], _dot(do, vn, _NT), 0.0) for t, do, vn in zip(ts, dos, vns)]
        dkds = [_dot(vn, dS2, _NT) for vn, dS2 in zip(vns, dS2s)]
        dTs = [_dot(dvn, t["vb"], _NT) + _dot(dw, t["kbg"], _NT) for t, dvn, dw in zip(ts, dvns, dws)]
        dvbs = [_dot(Tm, dvn, _TN) for Tm, dvn in zip(Tms, dvns)]
        dkbgs = [_dot(Tm, dw, _TN) for Tm, dw in zip(Tms, dws)]
        TdTs = [_dot(Tm, dT, _TN) for Tm, dT in zip(Tms, dTs)]
        dLs = [jnp.where(t["strict"], -_dot(TdT, Tm, _NT), 0.0) for t, TdT, Tm in zip(ts, TdTs, Tms)]
        dMs = [dL * t["G"] for t, dL in zip(ts, dLs)]
        dQs = [dP * t["G"] for t, dP in zip(ts, dPs)]
        dkbs = [_dot(dM, t["k"]) + dkbg * t["eg"] for t, dM, dkbg in zip(ts, dMs, dkbgs)]
        rs = lambda a: jnp.sum(a, axis=1, keepdims=True)
        lane0 = _iota2((C, 128), 1) == 0
        last = _iota2((C, 1), 0) == C - 1
        for h in H:
            t, sl = ts[h], sls[h]
            E = (dLs[h] * t["M"] + dPs[h] * Qs[h]) * t["G"]
            dqkv_ref[:, sl] = _dot(dQs[h], t["k"]) + dqds[h] * t["eg"]
            dqkv_ref[:, D + h * 128:D + (h + 1) * 128] = (
                _dot(dQs[h], t["q"], _TN) + _dot(dMs[h], t["kb"], _TN) + dkds[h] * t["egl"] + dkbs[h] * t["bx"])
            dqkv_ref[:, 2 * D + h * 128:2 * D + (h + 1) * 128] = dvbs[h] * t["bx"]
            dbx_ref[:, sl] = dkbs[h] * t["k"] + dvbs[h] * t["v"]
            dkd_kd = dkds[h] * t["kd"]
            dgam_c = rs(dqds[h] * t["qd"]) + rs(dkbgs[h] * t["kbg"]) - rs(dkd_kd) + rs(E)
            dgam_r = -jnp.sum(E, axis=0, keepdims=True)
            dgam_c = dgam_c + jnp.sum(jnp.where(t["eye"], dgam_r, 0.0), axis=1, keepdims=True)
            dlast = _sum_all(dkd_kd) + t["eL"] * _sum_all(Ss[h] * dS2s[h])
            dgx_ref[:, sl] = jnp.where(lane0, dgam_c + jnp.where(last, dlast, 0.0), 0.0)

    blk = lambda c: pl.BlockSpec((C, D), lambda n: (n, c))
    st = pl.BlockSpec((1, GDN_DK, D), lambda n: (n, 0, 0))
    in_specs = [blk(0), blk(1), blk(0), blk(0), blk(0), st,
                pl.BlockSpec((1, GDN_H * C, C), lambda n: (n, 0, 0)), blk(0), blk(0), st, blk(0)]
    out_specs = [pl.BlockSpec((C, 3 * D), lambda n: (n, 0)), blk(0), blk(0)]
    out_shape = [jax.ShapeDtypeStruct((T, 3 * D), F32), jax.ShapeDtypeStruct((T, D), F32),
                 jax.ShapeDtypeStruct((T, D), F32)]
    ins = [qk, qk, v, bx, gx, s_save, t_save, vn, dvn, ds_save, do]
    if ride:
        ins, in_specs = ins + ride.srcs, in_specs + ride.specs
        out_shape, out_specs = out_shape + ride.out_shape, out_specs + ride.specs
    res = pl.pallas_call(
        body, grid=(N,), in_specs=in_specs, out_specs=out_specs, out_shape=out_shape,
        scratch_shapes=ride.scratch if ride else [], name="gdn_rest_bwd",
        compiler_params=_params(("arbitrary",) if ride else ("parallel",)))(*ins)
    return (list(res[:3]), list(res[3:])) if ride else list(res)


def _ssd_seg(al_pair, half, s):
    L = SSM_L
    ri, ci = _iota2((L, L), 0), _iota2((L, L), 1)
    ac = jnp.max(jnp.where(half == s, al_pair, _NEG), axis=1, keepdims=True)
    ar = jnp.sum(jnp.where(ri == ci, ac, 0.0), axis=0, keepdims=True)
    return jnp.exp(jnp.where(ri >= ci, ac - ar, _NEG))


def _last_row(a):
    return jnp.sum(jnp.where(_iota2((a.shape[0], 1), 0) == a.shape[0] - 1, a, 0.0), axis=0, keepdims=True)


def _ssd_core_fwd(xbc, dtx, alx):
    T = xbc.shape[0]
    L = SSM_L
    Nc = T // L

    def body(x_ref, bc_ref, dt_ref, al_ref, y_ref, hs_ref, H_scr):
        c = pl.program_id(0)

        @pl.when(c == 0)
        def _():
            H_scr[...] = jnp.zeros_like(H_scr)

        half = _iota2((L, 128), 1) >> 6
        for g in range(2):
            gs = slice(g * 512, (g + 1) * 512)
            Bg = bc_ref[:, g * 128:(g + 1) * 128]
            Cg = bc_ref[:, 256 + g * 128:256 + (g + 1) * 128]
            alg = al_ref[:, gs]
            alast = _last_row(alg)
            xdt = x_ref[:, gs] * dt_ref[:, gs]
            Hg = H_scr[:, gs]
            hs_ref[0, :, gs] = Hg
            CB = _dot(Cg, Bg, _NT)
            y_ref[:, gs] = jnp.exp(alg) * _dot(Cg, Hg)
            H_scr[:, gs] = Hg * jnp.exp(alast) + _dot(Bg, jnp.exp(alast - alg) * xdt, _TN)
            for j in range(4):
                ps = slice(g * 512 + j * 128, g * 512 + (j + 1) * 128)
                al_pair = al_ref[:, ps]
                xp = x_ref[:, ps] * dt_ref[:, ps]
                ys = [_dot(_ssd_seg(al_pair, half, s) * CB, xp) for s in range(2)]
                y_ref[:, ps] += jnp.where(half == 0, ys[0], ys[1])

    row = pl.BlockSpec((L, D), lambda c: (c, 0))
    return pl.pallas_call(
        body, grid=(Nc,), in_specs=[row, pl.BlockSpec((L, 512), lambda c: (c, 2)), row, row],
        out_specs=[row, pl.BlockSpec((1, SSM_N, D), lambda c: (c, 0, 0))],
        out_shape=[jax.ShapeDtypeStruct((T, D), F32), jax.ShapeDtypeStruct((Nc, SSM_N, D), F32)],
        scratch_shapes=[pltpu.VMEM((SSM_N, D), F32)], name="ssd_core_fwd",
        compiler_params=_params(("arbitrary",)))(xbc, xbc, dtx, alx)


def _ssd_core_bwd(xbc, dtx, alx, h_save, dyy, d_x):
    T = xbc.shape[0]
    L = SSM_L
    Nc = T // L

    def body(x_ref, bc_ref, dt_ref, al_ref, hs_ref, dy_ref, d_ref, dx_ref, ddt_ref, dal_ref, dH_scr):
        c = pl.program_id(0)

        @pl.when(c == 0)
        def _():
            dH_scr[...] = jnp.zeros_like(dH_scr)

        lane = _iota2((L, 128), 1)
        half = lane >> 6
        rowi = _iota2((L, 1), 0)
        ri, ci = _iota2((L, L), 0), _iota2((L, L), 1)
        for g in range(2):
            gs = slice(g * 512, (g + 1) * 512)
            Bg = bc_ref[:, g * 128:(g + 1) * 128]
            Cg = bc_ref[:, 256 + g * 128:256 + (g + 1) * 128]
            alg = al_ref[:, gs]
            alast = _last_row(alg)
            eal, edec, eL = jnp.exp(alg), jnp.exp(alast - alg), jnp.exp(alast)
            xg, dtg, dYg = x_ref[:, gs], dt_ref[:, gs], dy_ref[:, gs]
            xdt = xg * dtg
            Hg = hs_ref[0, :, gs]
            dH2 = dH_scr[:, gs]
            CB = _dot(Cg, Bg, _NT)
            dYe = eal * dYg
            dH_scr[:, gs] = dH2 * eL + _dot(Cg, dYe, _TN)
            dC = _dot(dYe, Hg, _NT)
            zg = edec * xdt
            dz = _dot(Bg, dH2)
            dB = _dot(zg, dH2, _NT)
            tz = dz * zg
            dal = dYe * _dot(Cg, Hg) - tz
            dalast = jnp.sum(tz, axis=0, keepdims=True) + eL * jnp.sum(Hg * dH2, axis=0, keepdims=True)
            dal = dal + jnp.where(rowi == L - 1, dalast, 0.0)
            dxdt_g = edec * dz
            dx_ref[:, gs] = dxdt_g * dtg + dYg * d_ref[:, gs]
            ddt_ref[:, gs] = dxdt_g * xg
            dal_ref[:, gs] = dal
            dCB = jnp.zeros((L, L), F32)
            for j in range(4):
                ps = slice(g * 512 + j * 128, g * 512 + (j + 1) * 128)
                al_pair = al_ref[:, ps]
                xp = x_ref[:, ps] * dt_ref[:, ps]
                dYp = dy_ref[:, ps]
                dxp = []
                dal_p = jnp.zeros((L, 128), F32)
                for s in range(2):
                    seg = _ssd_seg(al_pair, half, s)
                    W = seg * CB
                    dW = jnp.where(ri >= ci, _dot(jnp.where(half == s, dYp, 0.0), xp, _NT), 0.0)
                    dxp.append(_dot(W, dYp, _TN))
                    dCB = dCB + dW * seg
                    Es = dW * W
                    dac = jnp.sum(Es, axis=1, keepdims=True) - jnp.sum(
                        jnp.where(ri == ci, jnp.sum(Es, axis=0, keepdims=True), 0.0), axis=1, keepdims=True)
                    dal_p = dal_p + jnp.where(lane == 64 * s, dac, 0.0)
                dxdt_p = jnp.where(half == 0, dxp[0], dxp[1])
                dx_ref[:, ps] += dxdt_p * dt_ref[:, ps]
                ddt_ref[:, ps] += dxdt_p * x_ref[:, ps]
                dal_ref[:, ps] += dal_p
            dx_ref[:, D + g * 128:D + (g + 1) * 128] = dB + _dot(dCB, Cg, _TN)
            dx_ref[:, D + 256 + g * 128:D + 256 + (g + 1) * 128] = dC + _dot(dCB, Bg)

    row = pl.BlockSpec((L, D), lambda c: (Nc - 1 - c, 0))
    bcs = pl.BlockSpec((L, 512), lambda c: (Nc - 1 - c, 2))
    return pl.pallas_call(
        body, grid=(Nc,),
        in_specs=[row, bcs, row, row, pl.BlockSpec((1, SSM_N, D), lambda c: (Nc - 1 - c, 0, 0)), row,
                  pl.BlockSpec((1, D), lambda c: (0, 0))],
        out_specs=[pl.BlockSpec((L, D + 512), lambda c: (Nc - 1 - c, 0)), row, row],
        out_shape=[jax.ShapeDtypeStruct((T, D + 512), F32),
                   jax.ShapeDtypeStruct((T, D), F32), jax.ShapeDtypeStruct((T, D), F32)],
        scratch_shapes=[pltpu.VMEM((SSM_N, D), F32)], name="ssd_core_bwd",
        compiler_params=_params(("arbitrary",)))(xbc, xbc, dtx, alx, h_save, dyy, d_x)


_EARLY = ("w_out", "wq_mem", "wk_mem", "wv_mem", "wo_mem")
_LATE = ("w_up", "w_down")
_GRADS_MLP = ("w_down", "w_up")
_GRADS_MID = ("wo_mem", "wq_mem", "wk_mem", "wv_mem", "w_out")


def _gather_ride(shards, names):
    return None if shards is None else _Ride([shards[n] for n in names], shard=True)


def _grad_ride(shards, G, names):
    return None if shards is None else _Ride([_slots_from_full(n, G[n]) for n in names], shard=False)


def _local_step(x, mem, tgt, W, shards=None):
    T = x.shape[0]
    W = dict(W)
    cw_qk, cw_v = W["gdn_conv_w"][:, :2 * D], W["gdn_conv_w"][:, 2 * D:]
    h1 = _rmsnorm_fwd(x, W["norm1_w"], name="norm1_fwd")
    ride = _gather_ride(shards, _EARLY)
    p = _mm(h1, W["w_in_pad"], name="in_proj", ride=ride)
    if ride:
        p, got = p
        W.update({n: _full_from_slots(n, g) for n, g in zip(_EARLY, got)})
    qk = _conv_fwd(p, C_QKV, 2 * D, cw_qk, None, l2=True, name="gdn_conv_qk_fwd")
    v_g = _conv_fwd(p, C_QKV + 2 * D, D, cw_v, None, l2=False, name="gdn_conv_v_fwd")
    bx, gx = _gdn_gates_fwd(p, W["gdn_alog_row"], W["gdn_dtb_row"])
    ride = _gather_ride(shards, _LATE)
    prep = _gdn_prep(qk, v_g, bx, gx, ride)
    if ride:
        prep, got = prep
        W.update({n: _full_from_slots(n, g) for n, g in zip(_LATE, got)})
    u_g, w_g, qd_g, kd_g, p_g, t_save = prep
    o_g, vn_g, s_save = _gdn_scan_fwd(u_g, w_g, qd_g, kd_g, p_g, gx)
    mix = _gdn_post_fwd(o_g, p, W["gdn_norm_x"])
    xbc = _conv_fwd(p, C_XBC, D + 512, W["ssm_conv_w"], W["ssm_conv_b"], l2=False, name="ssm_conv_fwd", bc=512)
    dtx, alx = _ssd_dt_fwd(p, W["ssm_dtb_row"], W["ssm_alog_x"])
    y_s, h_save = _ssd_core_fwd(xbc, dtx, alx)
    mix = _ssd_post_fwd(y_s, xbc, p, W["ssm_d_x"], W["ssm_norm_w"].reshape(1, D), mix)
    x1, h2 = _mm(mix, W["w_out"], epi="res_norm", extra=(x, W["norm2_w"]), bm=512, name="out_proj")
    qm = _mm(h2, W["wq_mem"], out_dtype=BF16, name="q_proj")
    m = _rmsnorm_fwd(mem, W["mem_norm_w"], name="mem_norm_fwd")
    km = _mm(m, W["wk_mem"], name="k_proj")
    vm = _mm(m, W["wv_mem"], name="v_proj")
    oa = _attn_fwd(qm, km, vm)
    x2, h3 = _mm(oa, W["wo_mem"], epi="res_norm", extra=(x1, W["norm3_w"]), bm=512, name="o_proj")
    u, act = _mm(h3, W["w_up"], epi="relu2", out_dtype=BF16, name="mlp_up")
    x3 = _mm(act, W["w_down"], epi="res", extra=x2, name="mlp_down")
    loss, dx3, g_final = _final_loss(x3, W["final_norm_w"], tgt)
    G = {"final_norm_w": g_final}
    dpre = _mm(dx3, W["w_down"], dims="nt", epi="mul2", extra=u, out_dtype=BF16, name="mlp_down_dx")
    G["w_down"] = _mm(act, dx3, dims="tn", out_dtype=BF16, name="mlp_down_dw")
    G["w_up"] = _mm(h3, dpre, dims="tn", out_dtype=BF16, name="mlp_up_dw")
    dx2, gw = _mm(dpre, W["w_up"], dims="nt", epi="norm_bwd", extra=(x2, dx3, W["norm3_w"]), bm=512,
                  name="mlp_up_dx")
    G["norm3_w"] = gw.reshape(D)
    do_a = _mm(dx2, W["wo_mem"], dims="nt", out_dtype=BF16, name="o_proj_dx")
    G["wo_mem"] = _mm(oa, dx2, dims="tn", out_dtype=BF16, name="o_proj_dw")
    dq, dk, dv = _attn_bwd(qm, km, vm, do_a)
    G["wq_mem"] = _mm(h2, dq, dims="tn", out_dtype=BF16, name="q_proj_dw")
    dx1, gw = _mm(dq, W["wq_mem"], dims="nt", epi="norm_bwd", extra=(x1, dx2, W["norm2_w"]), bm=512,
                  name="q_proj_dx")
    G["norm2_w"] = gw.reshape(D)
    G["wk_mem"] = _mm(m, dk, dims="tn", out_dtype=BF16, name="k_proj_dw")
    G["wv_mem"] = _mm(m, dv, dims="tn", out_dtype=BF16, name="v_proj_dw")
    dm = _mm(dk, W["wk_mem"], dims="nt", name="k_proj_dx")
    dm = _mm(dv, W["wv_mem"], dims="nt", epi="res", extra=dm, name="v_proj_dx")
    _, G["mem_norm_w"] = _rmsnorm_bwd(mem, W["mem_norm_w"], dm, None, name="mem_norm_bwd")
    dmix = _mm(dx1, W["w_out"], dims="nt", name="out_proj_dx")
    G["w_out"] = _mm(mix, dx1, dims="tn", out_dtype=BF16, name="out_proj_dw")
    do_g, dp, G["gdn_norm_x"] = _gdn_post_bwd(dmix, o_g, p, W["gdn_norm_x"])
    dvn_g, ds_save = _gdn_scan_bwd(w_g, qd_g, kd_g, p_g, gx, do_g)
    ride = _grad_ride(shards, G, _GRADS_MLP)
    rest = _gdn_rest_bwd(qk, v_g, bx, gx, s_save, t_save, vn_g, dvn_g, ds_save, do_g, ride)
    if ride:
        rest, got = rest
        G.update(zip(_GRADS_MLP, got))
    dqkvn, dbx, dgx = rest
    dy_qk, gcw_qk, _ = _conv_bwd_act(p, C_QKV, 2 * D, cw_qk, None, dqkvn, 0, l2=True, name="gdn_conv_qk_bwd_act")
    dy_v, gcw_v, _ = _conv_bwd_act(p, C_QKV + 2 * D, D, cw_v, None, dqkvn, 2 * D, l2=False,
                                   name="gdn_conv_v_bwd_act")
    G["gdn_conv_w"] = jnp.concatenate([gcw_qk, gcw_v], axis=1)
    dp = _conv_bwd_in(dy_qk, cw_qk, dp, C_QKV, T, name="gdn_conv_qk_bwd_in")
    dp = _conv_bwd_in(dy_v, cw_v, dp, C_QKV + 2 * D, T, name="gdn_conv_v_bwd_in")
    dp, G["gdn_alog_row"], G["gdn_dtb_row"] = _gdn_gates_bwd(p, W["gdn_alog_row"], W["gdn_dtb_row"], dbx, dgx, dp)
    dyy, dp, G["ssm_d_x"], G["ssm_norm_w"] = _ssd_post_bwd(dmix, y_s, xbc, p, W["ssm_d_x"],
                                                          W["ssm_norm_w"].reshape(1, D), dp)
    dxbc, ddtx, dalx = _ssd_core_bwd(xbc, dtx, alx, h_save, dyy, W["ssm_d_x"])
    dy_s, G["ssm_conv_w"], G["ssm_conv_b"] = _conv_bwd_act(p, C_XBC, D + 512, W["ssm_conv_w"], W["ssm_conv_b"],
                                                           dxbc, 0, l2=False, name="ssm_conv_bwd_act", bc=512)
    dp = _conv_bwd_in(dy_s, W["ssm_conv_w"], dp, C_XBC, T, name="ssm_conv_bwd_in", bc=512)
    dp, G["ssm_dtb_row"], G["ssm_alog_x"] = _ssd_dt_bwd(p, W["ssm_dtb_row"], W["ssm_alog_x"], ddtx, dalx, dp)
    ride = _grad_ride(shards, G, _GRADS_MID)
    g_in = _mm(h1, dp, dims="tn", out_dtype=BF16, name="in_proj_dw", ride=ride)
    if ride:
        g_in, got = g_in
        G.update(zip(_GRADS_MID, got))
    G["w_in"] = _unpad_w_in(g_in)
    ride = _grad_ride(shards, G, ("w_in",))
    res = _mm(dp, W["w_in_pad"], dims="nt", epi="norm_bwd", extra=(x, dx1, W["norm1_w"]), bm=512,
              name="in_proj_dx", ride=ride)
    if ride:
        res, got = res
        G["w_in"] = got[0]
    dx, gw = res
    G["norm1_w"] = gw.reshape(D)
    return loss, dx, G


def _all_gather(shards, out_dtype, *, name):
    n = len(shards)

    def body(*refs):
        x_refs, out_refs, stage = refs[:n], refs[n:2 * n], refs[2 * n:3 * n]
        send_sems, recv_sems, local_sems = refs[3 * n:]
        x, y, c = _place()
        me, sibling = (x, y, c), (x, y, 1 - c)
        chips = [(1 - x, y), (x, 1 - y), (1 - x, 1 - y)]

        def slot(px, py, pc):
            return 4 * px + 2 * py + pc

        def copy(a, k, block, to, src=None):
            dst = out_refs[a].at[slot(*block)]
            return pltpu.make_async_remote_copy(
                src_ref=dst if src is None else src, dst_ref=dst, send_sem=send_sems.at[a, k],
                recv_sem=recv_sems.at[a, k], device_id=to, device_id_type=_MESH)

        for a in range(n):
            stage[a][...] = x_refs[a][...].astype(out_dtype)
        mine = [pltpu.make_async_copy(stage[a], out_refs[a].at[slot(*me)], local_sems.at[a]) for a in range(n)]
        for cp in mine:
            cp.start()
        first = []
        for a in range(n):
            first.append(copy(a, 0, me, sibling, src=stage[a]))
            first += [copy(a, 1 + j, me, (*chip, c), src=stage[a]) for j, chip in enumerate(chips)]
        for cp in first:
            cp.start()
        passed = [[copy(a, 4 + j, (*chip, c), sibling) for j, chip in enumerate(chips)] for a in range(n)]
        for j, chip in enumerate(chips):
            for a in range(n):
                copy(a, 1 + j, (*chip, c), me).wait_recv()
                passed[a][j].start()
        for a in range(n):
            copy(a, 0, sibling, me).wait_recv()
            for j, chip in enumerate(chips):
                copy(a, 4 + j, (*chip, 1 - c), me).wait_recv()
        for cp in first + [cp for row in passed for cp in row]:
            cp.wait_send()
        for cp in mine:
            cp.wait()

    outs = pl.pallas_call(
        body, in_specs=[_VM] * n, out_specs=[_ANY] * n,
        out_shape=[jax.ShapeDtypeStruct((N_DEV,) + s.shape, out_dtype) for s in shards],
        scratch_shapes=[pltpu.VMEM(s.shape, out_dtype) for s in shards]
        + [pltpu.SemaphoreType.DMA((n, 7)), pltpu.SemaphoreType.DMA((n, 7)), pltpu.SemaphoreType.DMA((n,))],
        name=name, compiler_params=pltpu.CompilerParams(vmem_limit_bytes=VMEM_LIMIT))(*shards)
    return list(outs)


def _cast_bf16(arrs, *, name):
    n = len(arrs)

    def body(*refs):
        for a in range(n):
            refs[n + a][...] = refs[a][...].astype(BF16)

    return list(pl.pallas_call(
        body, in_specs=[_VM] * n, out_specs=[_VM] * n,
        out_shape=[jax.ShapeDtypeStruct(s.shape, BF16) for s in arrs], name=name,
        compiler_params=pltpu.CompilerParams(vmem_limit_bytes=VMEM_LIMIT))(*arrs))


def _sum8(a, *, name):
    _, R, Cc = a.shape
    br = _pick_rows(R, 128)

    def body(a_ref, o_ref):
        s = a_ref[0].astype(F32)
        for k in range(1, N_DEV):
            s = s + a_ref[k].astype(F32)
        o_ref[...] = s

    return pl.pallas_call(
        body, grid=(R // br,), in_specs=[pl.BlockSpec((N_DEV, br, Cc), lambda i: (0, i, 0))],
        out_specs=pl.BlockSpec((br, Cc), lambda i: (i, 0)), out_shape=jax.ShapeDtypeStruct((R, Cc), F32),
        name=name, compiler_params=_params(("parallel",)))(a)


def _pick_rows(R, cap):
    if R <= cap:
        return R
    for d in range(cap, 7, -8):
        if R % d == 0:
            return d
    return R


def _adamw(w, g, m, v, *, name):
    shape = w.shape
    as2d = (lambda t: t.reshape(1, -1)) if w.ndim == 1 else (lambda t: t)
    w2, g2, m2, v2 = as2d(w), as2d(g), as2d(m), as2d(v)
    R, Cc = w2.shape
    br = _pick_rows(R, 256)
    c1 = 1.0 - ADAM_B1 ** ADAM_STEP
    c2 = 1.0 - ADAM_B2 ** ADAM_STEP

    def body(w_ref, g_ref, m_ref, v_ref, d_ref, nm_ref, nv_ref):
        gv = g_ref[...]
        nm = ADAM_B1 * m_ref[...] + (1.0 - ADAM_B1) * gv
        nv = ADAM_B2 * v_ref[...] + (1.0 - ADAM_B2) * (gv * gv)
        nm_ref[...] = nm
        nv_ref[...] = nv
        d_ref[...] = -ADAM_LR * ((nm / c1) / (jnp.sqrt(nv / c2) + ADAM_EPS) + ADAM_WD * w_ref[...])

    blk = pl.BlockSpec((br, Cc), lambda i: (i, 0))
    outs = pl.pallas_call(
        body, grid=(R // br,), in_specs=[blk] * 4, out_specs=[blk] * 3,
        out_shape=[jax.ShapeDtypeStruct((R, Cc), F32)] * 3, name=name,
        compiler_params=_params(("parallel",)))(w2, g2, m2, v2)
    return tuple(o.reshape(shape) for o in outs)


_BIG = ("w_in", "w_out", "wq_mem", "wk_mem", "wv_mem", "wo_mem", "w_up", "w_down")
_COL_SHARDED = ("w_in", "w_up")
_WEIGHTS = ("norm1_w", "w_in", "gdn_conv_w", "gdn_a_log", "gdn_dt_bias", "gdn_norm_w", "ssm_conv_w", "ssm_conv_b",
            "ssm_a_log", "ssm_dt_bias", "ssm_d", "ssm_norm_w", "w_out", "norm2_w", "mem_norm_w", "wq_mem", "wk_mem",
            "wv_mem", "wo_mem", "norm3_w", "w_up", "w_down", "final_norm_w")
_IN_PAD = 112


def _full_from_slots(name, g):
    if name in _COL_SHARDED:
        return jnp.transpose(g, (1, 0, 2)).reshape(g.shape[1], N_DEV * g.shape[2])
    return g.reshape(N_DEV * g.shape[1], g.shape[2])


def _slots_from_full(name, f):
    if name in _COL_SHARDED:
        return jnp.transpose(f.reshape(f.shape[0], N_DEV, f.shape[1] // N_DEV), (1, 0, 2))
    return f.reshape(N_DEV, f.shape[0] // N_DEV, f.shape[1])


def _pad_w_in(w):
    z = jnp.zeros((w.shape[0], _IN_PAD), w.dtype)
    return jnp.concatenate([w[:, :4096], w[:, 4112:6672], w[:, 4096:4112], z, w[:, 6672:6688], z], axis=1)


def _unpad_w_in(gp):
    return jnp.concatenate([gp[:, :4096], gp[:, C_GATE:C_GATE + 16], gp[:, 4096:C_GATE], gp[:, C_DT:C_DT + 16]],
                           axis=1)


def _pack_rows(vals):
    rows, offs, r = [], [], 0
    for vflat in vals:
        nrow = 8 * -(-vflat.shape[0] // 1024)
        rows.append(jnp.pad(vflat, (0, nrow * 128 - vflat.shape[0])).reshape(nrow, 128))
        offs.append((r, vflat.shape[0]))
        r += nrow
    return jnp.concatenate(rows, axis=0), offs


def _unpack_rows(packed, offs, shapes):
    out = []
    for (r, nel), shp in zip(offs, shapes):
        nrow = -(-nel // 128)
        out.append(packed[r:r + nrow].reshape(-1)[:nel].reshape(shp))
    return out


def kernel(x, mem, norm1_w, w_in, gdn_conv_w, gdn_a_log, gdn_dt_bias, gdn_norm_w, ssm_conv_w, ssm_conv_b, ssm_a_log, ssm_dt_bias, ssm_d, ssm_norm_w, w_out, norm2_w, mem_norm_w, wq_mem, wk_mem, wv_mem, wo_mem, norm3_w, w_up, w_down, final_norm_w, loss_target, m_norm1_w, m_w_in, m_gdn_conv_w, m_gdn_a_log, m_gdn_dt_bias, m_gdn_norm_w, m_ssm_conv_w, m_ssm_conv_b, m_ssm_a_log, m_ssm_dt_bias, m_ssm_d, m_ssm_norm_w, m_w_out, m_norm2_w, m_mem_norm_w, m_wq_mem, m_wk_mem, m_wv_mem, m_wo_mem, m_norm3_w, m_w_up, m_w_down, m_final_norm_w, v_norm1_w, v_w_in, v_gdn_conv_w, v_gdn_a_log, v_gdn_dt_bias, v_gdn_norm_w, v_ssm_conv_w, v_ssm_conv_b, v_ssm_a_log, v_ssm_dt_bias, v_ssm_d, v_ssm_norm_w, v_w_out, v_norm2_w, v_mem_norm_w, v_wq_mem, v_wk_mem, v_wv_mem, v_wo_mem, v_norm3_w, v_w_up, v_w_down, v_final_norm_w):
    args = dict(locals())
    w_loc = {n: args[n] for n in _WEIGHTS}
    me = 4 * lax.axis_index("x") + 2 * lax.axis_index("y") + lax.axis_index("c")

    w_in_full = _full_from_slots("w_in", _all_gather([w_in], BF16, name="gather_w_in")[0])
    later = _EARLY + _LATE
    shards = dict(zip(later, _cast_bf16([w_loc[n] for n in later], name="cast_shards")))
    conv_pack, conv_offs = _pack_rows([gdn_conv_w.reshape(-1), ssm_conv_w.reshape(-1)])
    conv_all = _all_gather([conv_pack], F32, name="gather_conv")[0]
    gdn_cw, ssm_cw = [], []
    for k in range(N_DEV):
        a, b = _unpack_rows(conv_all[k], conv_offs, [gdn_conv_w.shape, ssm_conv_w.shape])
        gdn_cw.append(a)
        ssm_cw.append(b)
    W = {
        "w_in_pad": _pad_w_in(w_in_full),
        "norm1_w": norm1_w, "norm2_w": norm2_w, "norm3_w": norm3_w, "mem_norm_w": mem_norm_w,
        "final_norm_w": final_norm_w, "ssm_norm_w": ssm_norm_w, "ssm_conv_b": ssm_conv_b,
        "gdn_conv_w": jnp.concatenate(gdn_cw, axis=1), "ssm_conv_w": jnp.concatenate(ssm_cw, axis=1),
        "gdn_alog_row": jnp.pad(gdn_a_log, (GDN_H, 128 - 2 * GDN_H)).reshape(1, 128),
        "gdn_dtb_row": jnp.pad(gdn_dt_bias, (GDN_H, 128 - 2 * GDN_H)).reshape(1, 128),
        "gdn_norm_x": jnp.tile(gdn_norm_w, GDN_H).reshape(1, D),
        "ssm_dtb_row": jnp.pad(ssm_dt_bias, (0, 128 - SSM_H)).reshape(1, 128),
        "ssm_alog_x": jnp.repeat(ssm_a_log, SSM_P).reshape(1, D),
        "ssm_d_x": jnp.repeat(ssm_d, SSM_P).reshape(1, D),
    }

    loss_part, grad_x, G = _local_step(x[0], mem[0], loss_target[0], W, shards)

    grads = {n: _sum8(G[n], name="sum_" + n) for n in _BIG}

    small = {
        "norm1_w": G["norm1_w"], "gdn_conv_w": G["gdn_conv_w"], "gdn_a_log": G["gdn_alog_row"][0, GDN_H:2 * GDN_H],
        "gdn_dt_bias": G["gdn_dtb_row"][0, GDN_H:2 * GDN_H], "gdn_norm_w": G["gdn_norm_x"].reshape(GDN_H, 128).sum(0),
        "ssm_conv_w": G["ssm_conv_w"], "ssm_conv_b": G["ssm_conv_b"],
        "ssm_a_log": G["ssm_alog_x"].reshape(SSM_H, SSM_P).sum(1), "ssm_dt_bias": G["ssm_dtb_row"][0, :SSM_H],
        "ssm_d": G["ssm_d_x"].reshape(SSM_H, SSM_P).sum(1), "ssm_norm_w": G["ssm_norm_w"].reshape(D),
        "norm2_w": G["norm2_w"], "mem_norm_w": G["mem_norm_w"], "norm3_w": G["norm3_w"],
        "final_norm_w": G["final_norm_w"], "loss": loss_part[0, :1],
    }
    names = list(small)
    pack, offs = _pack_rows([small[n].reshape(-1) for n in names])
    tot = _sum8(_all_gather([pack], F32, name="gather_small")[0], name="sum_small")
    summed = dict(zip(names, _unpack_rows(tot, offs, [small[n].shape for n in names])))
    loss = summed.pop("loss")[0]
    for n in ("gdn_conv_w", "ssm_conv_w"):
        width = w_loc[n].shape[1]
        summed[n] = lax.dynamic_slice_in_dim(summed[n], me * width, width, axis=1)
    grads.update(summed)

    upd = {n: _adamw(w_loc[n], grads[n], args["m_" + n], args["v_" + n], name="adamw_" + n) for n in _WEIGHTS}
    return (loss, grad_x[None], *[grads[n] for n in _WEIGHTS], *[upd[n][0] for n in _WEIGHTS],
            *[upd[n][1] for n in _WEIGHTS], *[upd[n][2] for n in _WEIGHTS])
```

```python
import functools
import math

import jax
import jax.numpy as jnp
from jax import lax
from jax.experimental import pallas as pl
from jax.experimental.pallas import tpu as pltpu

F32 = jnp.float32
BF16 = jnp.bfloat16
_MXU = BF16

D = 1024
EPS = 1e-6
CONV_K = 4
GDN_H, GDN_DK, GDN_C = 8, 128, 64
GDN_SCAN_CHUNKS = 4
SSM_H, SSM_P, SSM_L, SSM_N = 16, 64, 128, 128
SSM_SCAN_CHUNKS = 2
MEM_H, MEM_HD = 4, 256
D_FF = 4096
N_DEV = 8

C_QKV, C_ZG, C_ZS, C_XBC, C_GATE, C_DT, C_TOT = 0, 3072, 4096, 5120, 6656, 6784, 6912

ADAM_LR, ADAM_B1, ADAM_B2, ADAM_EPS, ADAM_WD, ADAM_STEP = 0.001, 0.9, 0.999, 1e-08, 0.01, 10

VMEM_LIMIT = 56 * 1024 * 1024

_NN = (((1,), (0,)), ((), ()))
_NT = (((1,), (1,)), ((), ()))
_TN = (((0,), (0,)), ((), ()))


def _dot(a, b, dims=_NN):
    return lax.dot_general(a.astype(_MXU), b.astype(_MXU), dims, preferred_element_type=F32)


def _split3(a):
    a1 = a.astype(BF16)
    r1 = a - a1.astype(F32)
    a2 = r1.astype(BF16)
    return a1, a2, (r1 - a2.astype(F32)).astype(BF16)


def _dot_sel(a, e):
    eb = e.astype(BF16)
    return sum(lax.dot_general(p, eb, _NN, preferred_element_type=F32) for p in _split3(a))


def _sel_dot(e, a):
    eb = e.astype(BF16)
    return sum(lax.dot_general(eb, p, _NN, preferred_element_type=F32) for p in _split3(a))


def _chunk_cumsum(a, tri, chunk):
    return jnp.concatenate([_sel_dot(tri, a[r:r + chunk]) for r in range(0, a.shape[0], chunk)], axis=0)


def _params(sem):
    return pltpu.CompilerParams(dimension_semantics=sem, vmem_limit_bytes=VMEM_LIMIT)


def _pick(n, cap):
    for d in range(min(cap, n), 0, -128):
        if n % d == 0 and d % 128 == 0:
            return d
    return n


def _sigmoid(x):
    return 0.5 * jnp.tanh(0.5 * x) + 0.5


def _silu(x):
    return x * _sigmoid(x)


def _dsilu(x):
    s = _sigmoid(x)
    return s * (1.0 + x * (1.0 - s))


def _softplus(x):
    return jnp.maximum(x, 0.0) + jnp.log(1.0 + jnp.exp(-jnp.abs(x)))


def _iota2(shape, axis):
    return lax.broadcasted_iota(jnp.int32, shape, axis)


def _sum_all(x):
    return jnp.sum(jnp.sum(x, axis=1, keepdims=True), axis=0, keepdims=True)


_MESH = pl.DeviceIdType.MESH
_ANY = pl.BlockSpec(memory_space=pl.ANY)
_VM = pl.BlockSpec(memory_space=pltpu.VMEM)
_REL = [(r >> 2 & 1, r >> 1 & 1, r & 1) for r in range(1, N_DEV)]


def _place():
    return lax.axis_index("x"), lax.axis_index("y"), lax.axis_index("c")


class _Ride:
    def __init__(self, srcs, shard):
        self.srcs, self.shard, self.n = list(srcs), shard, len(srcs)
        self.out_shape = [jax.ShapeDtypeStruct(((N_DEV,) + s.shape) if shard else s.shape, s.dtype)
                          for s in self.srcs]
        self.specs = [_ANY] * self.n
        self.scratch = [pltpu.SemaphoreType.DMA((self.n, N_DEV - 1)), pltpu.SemaphoreType.DMA((self.n, N_DEV - 1)),
                        pltpu.SemaphoreType.DMA((self.n,))]

    def _copies(self, in_refs, out_refs, sems):
        send, recv, loc = sems
        x, y, c = _place()
        me = 4 * x + 2 * y + c
        local, remote, arrive = [], [], []
        for a in range(self.n):
            src = in_refs[a] if self.shard else in_refs[a].at[me]
            local.append(pltpu.make_async_copy(src, out_refs[a].at[me], loc.at[a]))
        for k, (rx, ry, rc) in enumerate(_REL):
            peer = (lax.rem(x + rx, 2), lax.rem(y + ry, 2), lax.rem(c + rc, 2))
            ps = 4 * peer[0] + 2 * peer[1] + peer[2]
            for a in range(self.n):
                src = in_refs[a] if self.shard else in_refs[a].at[ps]
                remote.append(pltpu.make_async_remote_copy(
                    src_ref=src, dst_ref=out_refs[a].at[me], send_sem=send.at[a, k], recv_sem=recv.at[a, k],
                    device_id=peer, device_id_type=_MESH))
                slot = out_refs[a].at[ps]
                arrive.append(pltpu.make_async_remote_copy(
                    src_ref=slot, dst_ref=slot, send_sem=send.at[a, k], recv_sem=recv.at[a, k],
                    device_id=peer, device_id_type=_MESH))
        return local, remote, arrive

    def start(self, in_refs, out_refs, sems):
        local, remote, _ = self._copies(in_refs, out_refs, sems)
        for cp in local + remote:
            cp.start()

    def wait(self, in_refs, out_refs, sems):
        local, remote, arrive = self._copies(in_refs, out_refs, sems)
        for cp in arrive:
            cp.wait_recv()
        for cp in remote:
            cp.wait_send()
        for cp in local:
            cp.wait()


_EPI = {
    "none": ((), ("tile",)),
    "res": (("tile",), ("tile",)),
    "mul2": (("tile",), ("tile",)),
    "relu2": ((), ("tile", "tile")),
    "res_norm": (("tile", "row"), ("tile", "tile")),
    "norm_bwd": (("tile", "tile", "row"), ("tile", "row")),
    "res_loss": (("tile", "tile", "row"), ("tile", "row", "row")),
}


def _mm(a, b, *, dims="nn", epi="none", extra=(), out_dtype=F32, name, bm=1024, bn_cap=1024, bk_cap=2048,
        ride=None):
    if dims == "nn":
        (M, K), (K2, N) = a.shape, b.shape
    elif dims == "nt":
        (M, K), (N, K2) = a.shape, b.shape
    else:
        (K, M), (K2, N) = a.shape, b.shape
    assert K == K2, (a.shape, b.shape, dims)
    bm = _pick(M, bm)
    bn = _pick(N, bn_cap)
    bk = _pick(K, bk_cap)
    nk = K // bk
    dn = {"nn": _NN, "nt": _NT, "tn": _TN}[dims]
    a_spec = (pl.BlockSpec((bk, bm), lambda i, j, k: (k, i)) if dims == "tn"
              else pl.BlockSpec((bm, bk), lambda i, j, k: (i, k)))
    b_spec = (pl.BlockSpec((bn, bk), lambda i, j, k: (j, k)) if dims == "nt"
              else pl.BlockSpec((bk, bn), lambda i, j, k: (k, j)))
    o_spec = pl.BlockSpec((bm, bn), lambda i, j, k: (i, j))
    r_spec = pl.BlockSpec((1, bn), lambda i, j, k: (0, j))
    extra = list(extra) if isinstance(extra, (tuple, list)) else [extra]
    ekinds, okinds = _EPI[epi]
    assert len(extra) == len(ekinds) and (epi not in ("res_norm", "norm_bwd", "res_loss") or bn == N)
    n_extra, n_out = len(ekinds), len(okinds)
    n_ride = ride.n if ride else 0
    gi, gj = M // bm, N // bn

    def body(a_ref, b_ref, *rest):
        ex = rest[:n_extra]
        first = pl.program_id(0) == 0
        ride_in = rest[n_extra:n_extra + n_ride]
        outs = rest[n_extra + n_ride:n_extra + n_ride + n_out]
        ride_out = rest[n_extra + n_ride + n_out:n_extra + 2 * n_ride + n_out]
        if ride:
            at = lambda i, j, k: ((pl.program_id(0) == i) & (pl.program_id(1) == j) & (pl.program_id(2) == k))

            @pl.when(at(0, 0, 0))
            def _():
                ride.start(ride_in, ride_out, rest[-3:])

        def finish(r):
            if epi == "res":
                outs[0][...] = (r + ex[0][...].astype(F32)).astype(outs[0].dtype)
            elif epi == "mul2":
                outs[0][...] = (2.0 * r * ex[0][...].astype(F32)).astype(outs[0].dtype)
            elif epi == "relu2":
                u = jnp.maximum(r, 0.0)
                outs[0][...] = u.astype(outs[0].dtype)
                outs[1][...] = (u * u).astype(outs[1].dtype)
            elif epi == "res_norm":
                y = r + ex[0][...]
                outs[0][...] = y
                rstd = lax.rsqrt(jnp.mean(y * y, axis=1, keepdims=True) + EPS)
                outs[1][...] = (y * rstd * ex[1][...]).astype(outs[1].dtype)
            elif epi == "norm_bwd":
                xv = ex[0][...]
                rstd = lax.rsqrt(jnp.mean(xv * xv, axis=1, keepdims=True) + EPS)
                xh = xv * rstd
                dxh = r * ex[2][...]
                outs[0][...] = ex[1][...] + rstd * (dxh - xh * jnp.mean(dxh * xh, axis=1, keepdims=True))
                dw = jnp.sum(r * xh, axis=0, keepdims=True)

                @pl.when(first)
                def _():
                    outs[1][...] = dw

                @pl.when(jnp.logical_not(first))
                def _():
                    outs[1][...] += dw
            elif epi == "res_loss":
                y = r + ex[0][...]
                wv = ex[2][...]
                rstd = lax.rsqrt(jnp.mean(y * y, axis=1, keepdims=True) + EPS)
                yh = y * rstd
                err = yh * wv - ex[1][...]
                part_loss = 0.5 * jnp.sum(jnp.mean(err * err, axis=1, keepdims=True), axis=0, keepdims=True)
                dyn = err * (1.0 / N)
                dyh = dyn * wv
                outs[0][...] = rstd * (dyh - yh * jnp.mean(dyh * yh, axis=1, keepdims=True))
                dw = jnp.sum(dyn * yh, axis=0, keepdims=True)
                lrow = jnp.broadcast_to(part_loss, (1, N))

                @pl.when(first)
                def _():
                    outs[1][...] = dw
                    outs[2][...] = lrow

                @pl.when(jnp.logical_not(first))
                def _():
                    outs[1][...] += dw
                    outs[2][...] += lrow
            else:
                outs[0][...] = r.astype(outs[0].dtype)

        part = _dot(a_ref[...], b_ref[...], dn)
        if nk == 1:
            finish(part)
        else:
            acc = rest[n_extra + 2 * n_ride + n_out]
            k = pl.program_id(2)

            @pl.when(k == 0)
            def _():
                acc[...] = part

            @pl.when((k > 0) & (k < nk - 1))
            def _():
                acc[...] += part

            @pl.when(k == nk - 1)
            def _():
                finish(acc[...] + part)

        if ride:
            @pl.when(at(gi - 1, gj - 1, nk - 1))
            def _():
                ride.wait(ride_in, ride_out, rest[-3:])

    kind_spec = {"tile": o_spec, "row": r_spec}
    ins = [a, b] + [e.reshape(1, N) if k == "row" else e for e, k in zip(extra, ekinds)]
    in_specs = [a_spec, b_spec] + [kind_spec[k] for k in ekinds]
    out_dtypes = {"res_norm": (F32, BF16), "norm_bwd": (F32, F32), "res_loss": (F32, F32, F32)}.get(
        epi, (out_dtype,) * n_out)
    out_shape = [jax.ShapeDtypeStruct((M, N) if k == "tile" else (1, N), dt) for k, dt in zip(okinds, out_dtypes)]
    out_specs = [kind_spec[k] for k in okinds]
    scratch = [pltpu.VMEM((bm, bn), F32)] if nk > 1 else []
    sem = ("arbitrary" if epi in ("norm_bwd", "res_loss") else "parallel", "parallel", "arbitrary")
    if ride:
        ins, in_specs = ins + ride.srcs, in_specs + ride.specs
        out_shape, out_specs = out_shape + ride.out_shape, out_specs + ride.specs
        scratch, sem = scratch + ride.scratch, ("arbitrary",) * 3
    res = pl.pallas_call(
        body, grid=(gi, gj, nk), in_specs=in_specs, out_specs=out_specs, out_shape=out_shape,
        scratch_shapes=scratch, name=name, compiler_params=_params(sem))(*ins)
    main = res[:n_out] if n_out > 1 else res[0]
    return (main, list(res[n_out:])) if ride else main


def _rmsnorm_fwd(x, w, *, name, bt=256):
    T, Dm = x.shape
    bt = min(bt, T)

    def body(x_ref, w_ref, h_ref):
        xv = x_ref[...]
        r = lax.rsqrt(jnp.mean(xv * xv, axis=1, keepdims=True) + EPS)
        h_ref[...] = (xv * r * w_ref[...]).astype(h_ref.dtype)

    return pl.pallas_call(
        body, grid=(T // bt,),
        in_specs=[pl.BlockSpec((bt, Dm), lambda i: (i, 0)), pl.BlockSpec((1, Dm), lambda i: (0, 0))],
        out_specs=pl.BlockSpec((bt, Dm), lambda i: (i, 0)),
        out_shape=jax.ShapeDtypeStruct((T, Dm), BF16), name=name,
        compiler_params=_params(("parallel",)))(x, w.reshape(1, Dm))


def _rmsnorm_bwd(x, w, dh, dres, *, name, bt=256):
    T, Dm = x.shape
    bt = min(bt, T)
    has_res = dres is not None

    def body(x_ref, w_ref, dh_ref, *rest):
        dres_ref = rest[0] if has_res else None
        dx_ref, dw_ref = rest[-2], rest[-1]
        i = pl.program_id(0)
        xv = x_ref[...]
        r = lax.rsqrt(jnp.mean(xv * xv, axis=1, keepdims=True) + EPS)
        xh = xv * r
        dhv = dh_ref[...].astype(F32)
        dxh = dhv * w_ref[...]
        dx = r * (dxh - xh * jnp.mean(dxh * xh, axis=1, keepdims=True))
        if has_res:
            dx = dx + dres_ref[...]
        dx_ref[...] = dx

        @pl.when(i == 0)
        def _():
            dw_ref[...] = jnp.zeros_like(dw_ref)

        dw_ref[...] += jnp.sum(dhv * xh, axis=0, keepdims=True)

    row = pl.BlockSpec((bt, Dm), lambda i: (i, 0))
    vec = pl.BlockSpec((1, Dm), lambda i: (0, 0))
    ins = [x, w.reshape(1, Dm), dh] + ([dres] if has_res else [])
    dx, dw = pl.pallas_call(
        body, grid=(T // bt,), in_specs=[row, vec, row] + ([row] if has_res else []),
        out_specs=[row, vec],
        out_shape=[jax.ShapeDtypeStruct((T, Dm), F32), jax.ShapeDtypeStruct((1, Dm), F32)],
        name=name, compiler_params=_params(("arbitrary",)))(*ins)
    return dx, dw.reshape(Dm)


def _attn_fwd(q, km, vm, *, bt=256):
    T = q.shape[0]
    M = km.shape[0]
    bt = min(bt, T)
    scale = MEM_HD ** -0.5

    def body(q_ref, k_ref, v_ref, o_ref):
        for h in range(MEM_H):
            sl = slice(h * MEM_HD, (h + 1) * MEM_HD)
            s = _dot(q_ref[:, sl], k_ref[:, sl], _NT) * scale
            s = s - jnp.max(s, axis=1, keepdims=True)
            e = jnp.exp(s)
            p = e / jnp.sum(e, axis=1, keepdims=True)
            o_ref[:, sl] = _dot(p, v_ref[:, sl]).astype(o_ref.dtype)

    row = pl.BlockSpec((bt, D), lambda i: (i, 0))
    mem = pl.BlockSpec((M, D), lambda i: (0, 0))
    return pl.pallas_call(
        body, grid=(T // bt,), in_specs=[row, mem, mem], out_specs=row,
        out_shape=jax.ShapeDtypeStruct((T, D), BF16), name="attn_fwd",
        compiler_params=_params(("parallel",)))(q, km, vm)


def _attn_bwd(q, km, vm, do, *, bt=256):
    T = q.shape[0]
    M = km.shape[0]
    bt = min(bt, T)
    scale = MEM_HD ** -0.5

    def body(q_ref, k_ref, v_ref, do_ref, dq_ref, dk_ref, dv_ref):
        i = pl.program_id(0)

        @pl.when(i == 0)
        def _():
            dk_ref[...] = jnp.zeros_like(dk_ref)
            dv_ref[...] = jnp.zeros_like(dv_ref)

        sls = [slice(h * MEM_HD, (h + 1) * MEM_HD) for h in range(MEM_H)]
        ss = [_dot(q_ref[:, sl], k_ref[:, sl], _NT) * scale for sl in sls]
        dps = [_dot(do_ref[:, sl], v_ref[:, sl], _NT) for sl in sls]
        es = [jnp.exp(s - jnp.max(s, axis=1, keepdims=True)) for s in ss]
        ps = [e / jnp.sum(e, axis=1, keepdims=True) for e in es]
        dss = [p * (dp - jnp.sum(dp * p, axis=1, keepdims=True)) * scale for p, dp in zip(ps, dps)]
        for sl, p, ds in zip(sls, ps, dss):
            dq_ref[:, sl] = _dot(ds, k_ref[:, sl]).astype(dq_ref.dtype)
            dk_ref[:, sl] += _dot(ds, q_ref[:, sl], _TN)
            dv_ref[:, sl] += _dot(p, do_ref[:, sl], _TN)

    row = pl.BlockSpec((bt, D), lambda i: (i, 0))
    mem = pl.BlockSpec((M, D), lambda i: (0, 0))
    return pl.pallas_call(
        body, grid=(T // bt,), in_specs=[row, mem, mem, row], out_specs=[row, mem, mem],
        out_shape=[jax.ShapeDtypeStruct((T, D), BF16), jax.ShapeDtypeStruct((M, D), F32),
                   jax.ShapeDtypeStruct((M, D), F32)],
        name="attn_bwd", compiler_params=_params(("arbitrary",)))(q, km, vm, do)


def _conv_apply(halo, x, w_ref, b_ref):
    bt = x.shape[0]
    cat = jnp.concatenate([halo, x], axis=0)
    y = x * w_ref[3:4, :]
    for k in range(CONV_K - 1):
        y = y + pltpu.roll(cat, CONV_K - 1 - k, 0)[8:8 + bt] * w_ref[k:k + 1, :]
    if b_ref is not None:
        y = y + b_ref[...]
    return y


def _l2_parts(act, bc):
    out = []
    for s in range(bc // 128):
        a = act[:, s * 128:(s + 1) * 128]
        r = lax.rsqrt(jnp.sum(a * a, axis=1, keepdims=True) + EPS)
        out.append((a, r))
    return out


def _conv_fwd(p, col0, C, w, b, *, l2, name, bt=512, bc=1024):
    T = p.shape[0]
    bt = min(bt, T)
    c0, hb = col0 // bc, bt // 8
    has_b = b is not None
    assert not l2 or (bc == D and C == 2 * D)

    def body(x_ref, halo_ref, w_ref, *rest):
        b_ref = rest[0] if has_b else None
        o_ref = rest[-1]
        i, j = pl.program_id(0), pl.program_id(1)
        x = x_ref[...]
        halo = jnp.where(i > 0, halo_ref[...], 0.0)
        act = _silu(_conv_apply(halo, x, w_ref, b_ref))
        if l2:
            sc = jnp.where(j == 0, GDN_DK ** -0.5, 1.0)
            o_ref[...] = jnp.concatenate([a * (r * sc) for a, r in _l2_parts(act, bc)], axis=1)
        else:
            o_ref[...] = act

    in_specs = [pl.BlockSpec((bt, bc), lambda i, j: (i, c0 + j)),
                pl.BlockSpec((8, bc), lambda i, j: (jnp.maximum(i * hb - 1, 0), c0 + j)),
                pl.BlockSpec((CONV_K, bc), lambda i, j: (0, j))]
    ins = [p, p, w]
    if has_b:
        in_specs.append(pl.BlockSpec((1, bc), lambda i, j: (0, j)))
        ins.append(b.reshape(1, C))
    return pl.pallas_call(
        body, grid=(T // bt, C // bc), in_specs=in_specs,
        out_specs=pl.BlockSpec((bt, bc), lambda i, j: (i, j)),
        out_shape=jax.ShapeDtypeStruct((T, C), F32), name=name,
        compiler_params=_params(("parallel", "parallel")))(*ins)


def _conv_bwd_act(p, col0, C, w, b, dact, dcol0, *, l2, name, bt=512, bc=1024):
    T = p.shape[0]
    bt = min(bt, T)
    c0, d0, hb = col0 // bc, dcol0 // bc, bt // 8
    has_b = b is not None
    assert not l2 or (bc == D and C == 2 * D)

    def body(x_ref, halo_ref, w_ref, *rest):
        b_ref = rest[0] if has_b else None
        dact_ref, dy_ref, dw_ref, db_ref = rest[-4:]
        j, i = pl.program_id(0), pl.program_id(1)
        x = x_ref[...]
        halo = jnp.where(i > 0, halo_ref[...], 0.0)
        y = _conv_apply(halo, x, w_ref, b_ref)
        dact = dact_ref[...]
        sg = _sigmoid(y)
        if l2:
            sc = jnp.where(j == 0, GDN_DK ** -0.5, 1.0)
            parts = []
            for s, (a, r) in enumerate(_l2_parts(y * sg, bc)):
                n = a * r
                dn = dact[:, s * 128:(s + 1) * 128]
                parts.append((r * sc) * (dn - n * jnp.sum(dn * n, axis=1, keepdims=True)))
            dact = jnp.concatenate(parts, axis=1)
        dy = dact * (sg * (1.0 + y * (1.0 - sg)))
        dy_ref[...] = dy

        @pl.when(i == 0)
        def _():
            dw_ref[...] = jnp.zeros_like(dw_ref)
            db_ref[...] = jnp.zeros_like(db_ref)

        db_ref[...] += jnp.sum(dy, axis=0, keepdims=True)
        cat = jnp.concatenate([halo, x], axis=0)
        dw_ref[3:4, :] += jnp.sum(dy * x, axis=0, keepdims=True)
        for k in range(CONV_K - 1):
            xs = pltpu.roll(cat, CONV_K - 1 - k, 0)[8:8 + bt]
            dw_ref[k:k + 1, :] += jnp.sum(dy * xs, axis=0, keepdims=True)

    in_specs = [pl.BlockSpec((bt, bc), lambda j, i: (i, c0 + j)),
                pl.BlockSpec((8, bc), lambda j, i: (jnp.maximum(i * hb - 1, 0), c0 + j)),
                pl.BlockSpec((CONV_K, bc), lambda j, i: (0, j))]
    ins = [p, p, w]
    if has_b:
        in_specs.append(pl.BlockSpec((1, bc), lambda j, i: (0, j)))
        ins.append(b.reshape(1, C))
    in_specs.append(pl.BlockSpec((bt, bc), lambda j, i: (i, d0 + j)))
    ins.append(dact)
    dy, dw, db = pl.pallas_call(
        body, grid=(C // bc, T // bt), in_specs=in_specs,
        out_specs=[pl.BlockSpec((bt, bc), lambda j, i: (i, j)),
                   pl.BlockSpec((CONV_K, bc), lambda j, i: (0, j)),
                   pl.BlockSpec((1, bc), lambda j, i: (0, j))],
        out_shape=[jax.ShapeDtypeStruct((T, C), F32), jax.ShapeDtypeStruct((CONV_K, C), F32),
                   jax.ShapeDtypeStruct((1, C), F32)],
        name=name, compiler_params=_params(("parallel", "arbitrary")))(*ins)
    return dy, dw, db.reshape(C)


def _conv_bwd_in(dy, w, dp_in, col0, T, *, name, bt=512, bc=1024):
    C = dy.shape[1]
    bt = min(bt, T)
    c0, hb, nb = col0 // bc, bt // 8, T // bt

    def body(dy_ref, nxt_ref, w_ref, *rest):
        o_ref = rest[-1]
        i = pl.program_id(0)
        dy_v = dy_ref[...]
        nxt = jnp.where(i < nb - 1, nxt_ref[...], 0.0)
        cat = jnp.concatenate([dy_v, nxt], axis=0)
        dx = dy_v * w_ref[3:4, :]
        for k in range(CONV_K - 1):
            s = CONV_K - 1 - k
            dx = dx + pltpu.roll(cat, bt + 8 - s, 0)[0:bt] * w_ref[k:k + 1, :]
        o_ref[...] = dx.astype(o_ref.dtype)

    in_specs = [pl.BlockSpec((bt, bc), lambda i, j: (i, j)),
                pl.BlockSpec((8, bc), lambda i, j: (jnp.minimum((i + 1) * hb, T // 8 - 1), j)),
                pl.BlockSpec((CONV_K, bc), lambda i, j: (0, j))]
    ins = [dy, dy, w]
    alias = {}
    if dp_in is not None:
        in_specs.append(pl.BlockSpec(memory_space=pl.ANY))
        ins.append(dp_in)
        alias = {3: 0}
    return pl.pallas_call(
        body, grid=(nb, C // bc), in_specs=in_specs,
        out_specs=pl.BlockSpec((bt, bc), lambda i, j: (i, c0 + j)),
        out_shape=jax.ShapeDtypeStruct((T, C_TOT), BF16), input_output_aliases=alias, name=name,
        compiler_params=_params(("parallel", "parallel")))(*ins)


def _expand_mats(shift, row0):
    e = (_iota2((128, D), 0) - row0 == (_iota2((128, D), 1) >> shift)).astype(F32)
    et = ((_iota2((D, 128), 0) >> shift) == _iota2((D, 128), 1) - row0).astype(F32)
    return e, et


def _cum_mats(chunk):
    ri, ci = _iota2((chunk, chunk), 0), _iota2((chunk, chunk), 1)
    return (ri >= ci).astype(F32), (ri <= ci).astype(F32)


def _gdn_gates_fwd(p, alog_row, dtb_row, *, bt=256):
    T = p.shape[0]
    bt = min(bt, T)

    def body(g_ref, al_ref, db_ref, beta_ref, gam_ref):
        gt = g_ref[...]
        eb, _ = _expand_mats(7, 0)
        eg, _ = _expand_mats(7, GDN_H)
        lc, _ = _cum_mats(GDN_C)
        beta_l = _sigmoid(gt)
        g_l = -jnp.exp(al_ref[...]) * _softplus(gt + db_ref[...])
        beta_ref[...] = _dot_sel(beta_l, eb)
        gam_ref[...] = _chunk_cumsum(_dot_sel(g_l, eg), lc, GDN_C)

    vec = pl.BlockSpec((1, 128), lambda i: (0, 0))
    row = pl.BlockSpec((bt, D), lambda i: (i, 0))
    return pl.pallas_call(
        body, grid=(T // bt,),
        in_specs=[pl.BlockSpec((bt, 128), lambda i: (i, C_GATE // 128)), vec, vec],
        out_specs=[row, row],
        out_shape=[jax.ShapeDtypeStruct((T, D), F32)] * 2, name="gdn_gates_fwd",
        compiler_params=_params(("parallel",)))(p, alog_row, dtb_row)


def _gdn_gates_bwd(p, alog_row, dtb_row, dbeta_x, dgam_x, dp_in, *, bt=256):
    T = p.shape[0]
    bt = min(bt, T)

    def body(g_ref, al_ref, db_ref, dbeta_ref, dgam_ref, dpin_ref, dg_out, dal_ref, ddb_ref):
        i = pl.program_id(0)
        gt = g_ref[...]
        _, ebt = _expand_mats(7, 0)
        _, egt = _expand_mats(7, GDN_H)
        _, uc = _cum_mats(GDN_C)
        ea = jnp.exp(al_ref[...])
        zz = gt + db_ref[...]
        g_l = -ea * _softplus(zz)
        beta_l = _sigmoid(gt)
        dg_l = _dot_sel(_chunk_cumsum(dgam_ref[...], uc, GDN_C), egt)
        dbeta_l = _dot_sel(dbeta_ref[...], ebt)
        da = dg_l * (-ea) * _sigmoid(zz)
        dg_out[...] = (da + dbeta_l * beta_l * (1.0 - beta_l)).astype(dg_out.dtype)

        @pl.when(i == 0)
        def _():
            dal_ref[...] = jnp.zeros_like(dal_ref)
            ddb_ref[...] = jnp.zeros_like(ddb_ref)

        dal_ref[...] += jnp.sum(dg_l * g_l, axis=0, keepdims=True)
        ddb_ref[...] += jnp.sum(da, axis=0, keepdims=True)

    vec = pl.BlockSpec((1, 128), lambda i: (0, 0))
    row = pl.BlockSpec((bt, D), lambda i: (i, 0))
    gate = pl.BlockSpec((bt, 128), lambda i: (i, C_GATE // 128))
    return pl.pallas_call(
        body, grid=(T // bt,),
        in_specs=[gate, vec, vec, row, row, pl.BlockSpec(memory_space=pl.ANY)],
        out_specs=[gate, vec, vec],
        out_shape=[jax.ShapeDtypeStruct((T, C_TOT), BF16), jax.ShapeDtypeStruct((1, 128), F32),
                   jax.ShapeDtypeStruct((1, 128), F32)],
        input_output_aliases={5: 0}, name="gdn_gates_bwd",
        compiler_params=_params(("arbitrary",)))(p, alog_row, dtb_row, dbeta_x, dgam_x, dp_in)


def _ssd_dt_fwd(p, dtb_row, alog_x, *, bt=256):
    T = p.shape[0]
    bt = min(bt, T)

    def body(d_ref, db_ref, al_ref, dt_ref, alpha_ref):
        ed, _ = _expand_mats(6, 0)
        lc, _ = _cum_mats(SSM_L)
        dt_x = _dot_sel(_softplus(d_ref[...] + db_ref[...]), ed)
        dt_ref[...] = dt_x
        alpha_ref[...] = _chunk_cumsum(dt_x * (-jnp.exp(al_ref[...])), lc, SSM_L)

    row = pl.BlockSpec((bt, D), lambda i: (i, 0))
    return pl.pallas_call(
        body, grid=(T // bt,),
        in_specs=[pl.BlockSpec((bt, 128), lambda i: (i, C_DT // 128)),
                  pl.BlockSpec((1, 128), lambda i: (0, 0)), pl.BlockSpec((1, D), lambda i: (0, 0))],
        out_specs=[row, row], out_shape=[jax.ShapeDtypeStruct((T, D), F32)] * 2,
        name="ssd_dt_fwd", compiler_params=_params(("parallel",)))(p, dtb_row, alog_x)


def _ssd_dt_bwd(p, dtb_row, alog_x, ddt_x, dalpha_x, dp_in, *, bt=256):
    T = p.shape[0]
    bt = min(bt, T)

    def body(d_ref, db_ref, al_ref, ddt_ref, dal_ref, dpin_ref, dd_out, ddb_ref, dalog_ref):
        i = pl.program_id(0)
        ed, edt = _expand_mats(6, 0)
        _, uc = _cum_mats(SSM_L)
        zz = d_ref[...] + db_ref[...]
        dt_x = _dot_sel(_softplus(zz), ed)
        a_x = -jnp.exp(al_ref[...])
        da_x = _chunk_cumsum(dal_ref[...], uc, SSM_L)
        ddt_l = _dot_sel(ddt_ref[...] + da_x * a_x, edt)
        draw = ddt_l * _sigmoid(zz)
        dd_out[...] = draw.astype(dd_out.dtype)

        @pl.when(i == 0)
        def _():
            ddb_ref[...] = jnp.zeros_like(ddb_ref)
            dalog_ref[...] = jnp.zeros_like(dalog_ref)

        ddb_ref[...] += jnp.sum(draw, axis=0, keepdims=True)
        dalog_ref[...] += jnp.sum(da_x * dt_x, axis=0, keepdims=True) * a_x

    row = pl.BlockSpec((bt, D), lambda i: (i, 0))
    seg = pl.BlockSpec((bt, 128), lambda i: (i, C_DT // 128))
    v128 = pl.BlockSpec((1, 128), lambda i: (0, 0))
    vD = pl.BlockSpec((1, D), lambda i: (0, 0))
    return pl.pallas_call(
        body, grid=(T // bt,),
        in_specs=[seg, v128, vD, row, row, pl.BlockSpec(memory_space=pl.ANY)],
        out_specs=[seg, v128, vD],
        out_shape=[jax.ShapeDtypeStruct((T, C_TOT), BF16), jax.ShapeDtypeStruct((1, 128), F32),
                   jax.ShapeDtypeStruct((1, D), F32)],
        input_output_aliases={5: 0}, name="ssd_dt_bwd",
        compiler_params=_params(("arbitrary",)))(p, dtb_row, alog_x, ddt_x, dalpha_x, dp_in)


def _gdn_post_fwd(o, p, w_x, *, bt=256):
    T = o.shape[0]
    bt = min(bt, T)

    def body(o_ref, z_ref, w_ref, out_ref):
        for h in range(GDN_H):
            sl = slice(h * 128, (h + 1) * 128)
            oh = o_ref[:, sl]
            r = lax.rsqrt(jnp.mean(oh * oh, axis=1, keepdims=True) + EPS)
            out_ref[:, sl] = (oh * r * w_ref[:, sl] * _silu(z_ref[:, sl])).astype(out_ref.dtype)

    row = pl.BlockSpec((bt, D), lambda i: (i, 0))
    return pl.pallas_call(
        body, grid=(T // bt,),
        in_specs=[row, pl.BlockSpec((bt, D), lambda i: (i, C_ZG // D)), pl.BlockSpec((1, D), lambda i: (0, 0))],
        out_specs=row, out_shape=jax.ShapeDtypeStruct((T, 2 * D), BF16), name="gdn_post_fwd",
        compiler_params=_params(("parallel",)))(o, p, w_x)


def _gdn_post_bwd(dmix, o, p, w_x, *, bt=256):
    T = o.shape[0]
    bt = min(bt, T)

    def body(dm_ref, o_ref, z_ref, w_ref, do_ref, dz_ref, dw_ref):
        i = pl.program_id(0)

        @pl.when(i == 0)
        def _():
            dw_ref[...] = jnp.zeros_like(dw_ref)

        for h in range(GDN_H):
            sl = slice(h * 128, (h + 1) * 128)
            oh, zh, wh, dm = o_ref[:, sl], z_ref[:, sl], w_ref[:, sl], dm_ref[:, sl]
            r = lax.rsqrt(jnp.mean(oh * oh, axis=1, keepdims=True) + EPS)
            ohat = oh * r
            dy = dm * _silu(zh)
            dz_ref[:, sl] = (dm * ohat * wh * _dsilu(zh)).astype(dz_ref.dtype)
            dohat = dy * wh
            do_ref[:, sl] = r * (dohat - ohat * jnp.mean(dohat * ohat, axis=1, keepdims=True))
            dw_ref[:, sl] += jnp.sum(dy * ohat, axis=0, keepdims=True)

    row = pl.BlockSpec((bt, D), lambda i: (i, 0))
    zcol = pl.BlockSpec((bt, D), lambda i: (i, C_ZG // D))
    vec = pl.BlockSpec((1, D), lambda i: (0, 0))
    return pl.pallas_call(
        body, grid=(T // bt,), in_specs=[row, row, zcol, vec], out_specs=[row, zcol, vec],
        out_shape=[jax.ShapeDtypeStruct((T, D), F32), jax.ShapeDtypeStruct((T, C_TOT), BF16),
                   jax.ShapeDtypeStruct((1, D), F32)],
        name="gdn_post_bwd", compiler_params=_params(("arbitrary",)))(dmix, o, p, w_x)


def _ssd_post_fwd(y, xs, p, d_x, w, mix_in, *, bt=256):
    T = y.shape[0]
    bt = min(bt, T)

    def body(y_ref, x_ref, z_ref, d_ref, w_ref, mix_ref, out_ref):
        yg = (y_ref[...] + x_ref[...] * d_ref[...]) * _silu(z_ref[...])
        for g in range(2):
            sl = slice(g * 512, (g + 1) * 512)
            a = yg[:, sl]
            r = lax.rsqrt(jnp.mean(a * a, axis=1, keepdims=True) + EPS)
            out_ref[:, sl] = (a * r * w_ref[:, sl]).astype(out_ref.dtype)

    row = pl.BlockSpec((bt, D), lambda i: (i, 0))
    vec = pl.BlockSpec((1, D), lambda i: (0, 0))
    return pl.pallas_call(
        body, grid=(T // bt,),
        in_specs=[row, row, pl.BlockSpec((bt, D), lambda i: (i, C_ZS // D)), vec, vec, _ANY],
        out_specs=pl.BlockSpec((bt, D), lambda i: (i, 1)), out_shape=jax.ShapeDtypeStruct((T, 2 * D), BF16),
        input_output_aliases={5: 0}, name="ssd_post_fwd",
        compiler_params=_params(("parallel",)))(y, xs, p, d_x, w, mix_in)


def _ssd_post_bwd(dmix, y, xs, p, d_x, w, dp_in, *, bt=256):
    T = y.shape[0]
    bt = min(bt, T)

    def body(dm_ref, y_ref, x_ref, z_ref, d_ref, w_ref, dpin_ref, dyy_ref, dz_ref, dd_ref, dw_ref):
        i = pl.program_id(0)

        @pl.when(i == 0)
        def _():
            dd_ref[...] = jnp.zeros_like(dd_ref)
            dw_ref[...] = jnp.zeros_like(dw_ref)

        xv, zv = x_ref[...], z_ref[...]
        yy = y_ref[...] + xv * d_ref[...]
        sz = _silu(zv)
        yg = yy * sz
        parts = []
        for g in range(2):
            sl = slice(g * 512, (g + 1) * 512)
            a = yg[:, sl]
            r = lax.rsqrt(jnp.mean(a * a, axis=1, keepdims=True) + EPS)
            ah = a * r
            dout = dm_ref[:, sl]
            dah = dout * w_ref[:, sl]
            dw_ref[:, sl] += jnp.sum(dout * ah, axis=0, keepdims=True)
            parts.append(r * (dah - ah * jnp.mean(dah * ah, axis=1, keepdims=True)))
        dyg = jnp.concatenate(parts, axis=1)
        dyy = dyg * sz
        dyy_ref[...] = dyy
        dz_ref[...] = (dyg * yy * _dsilu(zv)).astype(dz_ref.dtype)
        dd_ref[...] += jnp.sum(dyy * xv, axis=0, keepdims=True)

    row = pl.BlockSpec((bt, D), lambda i: (i, 0))
    zcol = pl.BlockSpec((bt, D), lambda i: (i, C_ZS // D))
    vec = pl.BlockSpec((1, D), lambda i: (0, 0))
    return pl.pallas_call(
        body, grid=(T // bt,),
        in_specs=[pl.BlockSpec((bt, D), lambda i: (i, 1)), row, row, zcol, vec, vec, _ANY],
        out_specs=[row, zcol, vec, vec],
        out_shape=[jax.ShapeDtypeStruct((T, D), F32), jax.ShapeDtypeStruct((T, C_TOT), BF16),
                   jax.ShapeDtypeStruct((1, D), F32), jax.ShapeDtypeStruct((1, D), F32)],
        input_output_aliases={6: 1}, name="ssd_post_bwd",
        compiler_params=_params(("arbitrary",)))(dmix, y, xs, p, d_x, w, dp_in)


_NEG = -1e30


def _gdn_terms(q, k, v, bx, gam_c):
    C = GDN_C
    ri, ci = _iota2((C, C), 0), _iota2((C, C), 1)
    eye, low, strict = ri == ci, ri >= ci, ri > ci
    gam_r = jnp.sum(jnp.where(eye, gam_c, 0.0), axis=0, keepdims=True)
    G = jnp.exp(jnp.where(low, gam_c - gam_r, _NEG))
    glast = jnp.sum(jnp.where(_iota2((C, 1), 0) == C - 1, gam_c, 0.0), axis=0, keepdims=True)
    eg, egl, eL = jnp.exp(gam_c), jnp.exp(glast - gam_c), jnp.exp(glast)
    kb, vb = k * bx, v * bx
    M = _dot(kb, k, _NT)
    return dict(eye=eye, low=low, strict=strict, G=G, eg=eg, egl=egl, eL=eL, kb=kb, vb=vb, M=M,
                kbg=kb * eg, qd=q * eg, kd=k * egl, q=q, k=k, v=v, bx=bx)


def _split(a):
    hi = a.astype(_MXU)
    return hi, (a - hi.astype(F32)).astype(_MXU)


def _dot3s(a, b):
    d = lambda p, q: lax.dot_general(p, q, _NN, preferred_element_type=F32)
    return d(a[0], b[0]) + d(a[0], b[1]) + d(a[1], b[0])


def _tri_inv_many(Ls, eye):
    eyef = jnp.where(eye, 1.0, 0.0)
    Ts = [eyef - L for L in Ls]
    Ps = [-L for L in Ls]
    for _ in range(5):
        sp = [_split(p) for p in Ps]
        Ps = [_dot3s(s, s) for s in sp]
        sp = [_split(p) for p in Ps]
        st = [_split(t) for t in Ts]
        Ts = [t + _dot3s(a, b) for t, a, b in zip(Ts, st, sp)]
    return Ts


def _gdn_heads(q_ref, k_ref, v_ref, bx_ref, gx_ref):
    out = []
    for h in range(GDN_H):
        sl = slice(h * 128, (h + 1) * 128)
        gam_c = jnp.max(gx_ref[:, sl], axis=1, keepdims=True)
        out.append(_gdn_terms(q_ref[:, sl], k_ref[:, sl], v_ref[:, sl], bx_ref[:, sl], gam_c))
    return out


def _gdn_prep(qk, v, bx, gx, ride=None):
    T = qk.shape[0]
    N = T // GDN_C
    C = GDN_C
    n_ride = ride.n if ride else 0

    def body(q_ref, k_ref, v_ref, bx_ref, gx_ref, *rest):
        ride_in = rest[:n_ride]
        u_ref, w_ref, qd_ref, kd_ref, p_ref, t_ref = rest[n_ride:n_ride + 6]
        ride_out = rest[n_ride + 6:2 * n_ride + 6]
        if ride:
            @pl.when(pl.program_id(0) == 0)
            def _():
                ride.start(ride_in, ride_out, rest[-3:])

            @pl.when(pl.program_id(0) == N - 1)
            def _():
                ride.wait(ride_in, ride_out, rest[-3:])

        ts = _gdn_heads(q_ref, k_ref, v_ref, bx_ref, gx_ref)
        Ts = _tri_inv_many([jnp.where(t["strict"], t["M"] * t["G"], 0.0) for t in ts], ts[0]["eye"])
        for h, (t, Tm) in enumerate(zip(ts, Ts)):
            sl = slice(h * 128, (h + 1) * 128)
            rows = slice(h * C, (h + 1) * C)
            u_ref[:, sl] = _dot(Tm, t["vb"])
            w_ref[:, sl] = _dot(Tm, t["kbg"]).astype(w_ref.dtype)
            qd_ref[:, sl] = t["qd"].astype(qd_ref.dtype)
            kd_ref[:, sl] = t["kd"].astype(kd_ref.dtype)
            p_ref[0, rows, :] = _dot(t["q"], t["k"], _NT) * t["G"]
            t_ref[0, rows, :] = Tm

    blk = lambda c: pl.BlockSpec((C, D), lambda n: (n, c))
    sq = pl.BlockSpec((1, GDN_H * C, C), lambda n: (n, 0, 0))
    in_specs = [blk(0), blk(1), blk(0), blk(0), blk(0)]
    out_specs = [blk(0), blk(0), blk(0), blk(0), sq, sq]
    out_shape = [jax.ShapeDtypeStruct((T, D), F32), jax.ShapeDtypeStruct((T, D), BF16),
                 jax.ShapeDtypeStruct((T, D), BF16), jax.ShapeDtypeStruct((T, D), BF16),
                 jax.ShapeDtypeStruct((N, GDN_H * C, C), F32), jax.ShapeDtypeStruct((N, GDN_H * C, C), F32)]
    ins = [qk, qk, v, bx, gx]
    if ride:
        ins, in_specs = ins + ride.srcs, in_specs + ride.specs
        out_shape, out_specs = out_shape + ride.out_shape, out_specs + ride.specs
    res = pl.pallas_call(
        body, grid=(N,), in_specs=in_specs, out_specs=out_specs, out_shape=out_shape,
        scratch_shapes=ride.scratch if ride else [], name="gdn_prep",
        compiler_params=_params(("arbitrary",) if ride else ("parallel",)))(*ins)
    return (list(res[:6]), list(res[6:])) if ride else list(res)


def _gdn_scan_fwd(u, w, qd, kd, pm, gx):
    T = u.shape[0]
    N = T // GDN_C
    C, CS = GDN_C, GDN_SCAN_CHUNKS

    def body(u_ref, w_ref, qd_ref, kd_ref, p_ref, gx_ref, o_ref, vn_ref, ss_ref, S_scr):
        n = pl.program_id(0)

        @pl.when(n == 0)
        def _():
            S_scr[...] = jnp.zeros_like(S_scr)

        sls = [slice(h * 128, (h + 1) * 128) for h in range(GDN_H)]
        for c in range(CS):
            rows = slice(c * C, (c + 1) * C)
            Ss = [S_scr[:, sl] for sl in sls]
            vns = [u_ref[rows, sl] - _dot(w_ref[rows, sl], S) for sl, S in zip(sls, Ss)]
            for h, (sl, S, vn) in enumerate(zip(sls, Ss, vns)):
                ss_ref[c, :, sl] = S
                vn_ref[rows, sl] = vn.astype(vn_ref.dtype)
                o_ref[rows, sl] = _dot(qd_ref[rows, sl], S) + _dot(p_ref[c, h * C:(h + 1) * C, :], vn)
                S_scr[:, sl] = (S * jnp.exp(gx_ref[(c + 1) * C - 1:(c + 1) * C, sl])
                                + _dot(kd_ref[rows, sl], vn, _TN))

    blk = pl.BlockSpec((CS * C, D), lambda n: (n, 0))
    return pl.pallas_call(
        body, grid=(N // CS,),
        in_specs=[blk, blk, blk, blk, pl.BlockSpec((CS, GDN_H * C, C), lambda n: (n, 0, 0)), blk],
        out_specs=[blk, blk, pl.BlockSpec((CS, GDN_DK, D), lambda n: (n, 0, 0))],
        out_shape=[jax.ShapeDtypeStruct((T, D), F32), jax.ShapeDtypeStruct((T, D), BF16),
                   jax.ShapeDtypeStruct((N, GDN_DK, D), F32)],
        scratch_shapes=[pltpu.VMEM((GDN_DK, D), F32)], name="gdn_scan_fwd",
        compiler_params=_params(("arbitrary",)))(u, w, qd, kd, pm, gx)


def _gdn_scan_bwd(w, qd, kd, pm, gx, do):
    T = w.shape[0]
    N = T // GDN_C
    C, CS = GDN_C, GDN_SCAN_CHUNKS
    NB = N // CS

    def body(w_ref, qd_ref, kd_ref, p_ref, gx_ref, do_ref, dvn_ref, ds_ref, dS_scr):
        n = pl.program_id(0)

        @pl.when(n == 0)
        def _():
            dS_scr[...] = jnp.zeros_like(dS_scr)

        sls = [slice(h * 128, (h + 1) * 128) for h in range(GDN_H)]
        for c in reversed(range(CS)):
            rows = slice(c * C, (c + 1) * C)
            dSs = [dS_scr[:, sl] for sl in sls]
            dvns = [_dot(p_ref[c, h * C:(h + 1) * C, :], do_ref[rows, sl], _TN) + _dot(kd_ref[rows, sl], dS2)
                    for h, (sl, dS2) in enumerate(zip(sls, dSs))]
            for sl, dS2, dvn in zip(sls, dSs, dvns):
                ds_ref[c, :, sl] = dS2
                dvn_ref[rows, sl] = dvn.astype(dvn_ref.dtype)
                dS_scr[:, sl] = (dS2 * jnp.exp(gx_ref[(c + 1) * C - 1:(c + 1) * C, sl])
                                 + _dot(qd_ref[rows, sl], do_ref[rows, sl], _TN) - _dot(w_ref[rows, sl], dvn, _TN))

    blk = pl.BlockSpec((CS * C, D), lambda n: (NB - 1 - n, 0))
    return pl.pallas_call(
        body, grid=(NB,),
        in_specs=[blk, blk, blk, pl.BlockSpec((CS, GDN_H * C, C), lambda n: (NB - 1 - n, 0, 0)), blk, blk],
        out_specs=[blk, pl.BlockSpec((CS, GDN_DK, D), lambda n: (NB - 1 - n, 0, 0))],
        out_shape=[jax.ShapeDtypeStruct((T, D), BF16), jax.ShapeDtypeStruct((N, GDN_DK, D), F32)],
        scratch_shapes=[pltpu.VMEM((GDN_DK, D), F32)], name="gdn_scan_bwd",
        compiler_params=_params(("arbitrary",)))(w, qd, kd, pm, gx, do)


def _gdn_rest_bwd(qk, v, bx, gx, s_save, t_save, vn, dvn, ds_save, do, ride=None):
    T = qk.shape[0]
    N = T // GDN_C
    C = GDN_C
    n_ride = ride.n if ride else 0

    def body(q_ref, k_ref, v_ref, bx_ref, gx_ref, ss_ref, ts_ref, vn_ref, dvn_ref, ds_ref, do_ref, *rest):
        ride_in = rest[:n_ride]
        dqkv_ref, dbx_ref, dgx_ref = rest[n_ride:n_ride + 3]
        ride_out = rest[n_ride + 3:2 * n_ride + 3]
        if ride:
            @pl.when(pl.program_id(0) == 0)
            def _():
                ride.start(ride_in, ride_out, rest[-3:])

            @pl.when(pl.program_id(0) == N - 1)
            def _():
                ride.wait(ride_in, ride_out, rest[-3:])

        H = range(GDN_H)
        sls = [slice(h * 128, (h + 1) * 128) for h in H]
        ts = _gdn_heads(q_ref, k_ref, v_ref, bx_ref, gx_ref)
        Ss = [ss_ref[0, :, sl] for sl in sls]
        Tms = [ts_ref[0, h * C:(h + 1) * C, :] for h in H]
        dS2s = [ds_ref[0, :, sl] for sl in sls]
        dos = [do_ref[:, sl] for sl in sls]
        vns = [vn_ref[:, sl] for sl in sls]
        dvns = [dvn_ref[:, sl] for sl in sls]
        Qs = [_dot(t["q"], t["k"], _NT) for t in ts]
        dws = [-_dot(dvn, S, _NT) for dvn, S in zip(dvns, Ss)]
        dqds = [_dot(do, S, _NT) for do, S in zip(dos, Ss)]
        dPs = [jnp.where(t["low"], _dot(do, vn, _NT), 0.0) for t, do, vn in zip(ts, dos, vns)]
        dkds = [_dot(vn, dS2, _NT) for vn, dS2 in zip(vns, dS2s)]
        dTs = [_dot(dvn, t["vb"], _NT) + _dot(dw, t["kbg"], _NT) for t, dvn, dw in zip(ts, dvns, dws)]
        dvbs = [_dot(Tm, dvn, _TN) for Tm, dvn in zip(Tms, dvns)]
        dkbgs = [_dot(Tm, dw, _TN) for Tm, dw in zip(Tms, dws)]
        TdTs = [_dot(Tm, dT, _TN) for Tm, dT in zip(Tms, dTs)]
        dLs = [jnp.where(t["strict"], -_dot(TdT, Tm, _NT), 0.0) for t, TdT, Tm in zip(ts, TdTs, Tms)]
        dMs = [dL * t["G"] for t, dL in zip(ts, dLs)]
        dQs = [dP * t["G"] for t, dP in zip(ts, dPs)]
        dkbs = [_dot(dM, t["k"]) + dkbg * t["eg"] for t, dM, dkbg in zip(ts, dMs, dkbgs)]
        rs = lambda a: jnp.sum(a, axis=1, keepdims=True)
        lane0 = _iota2((C, 128), 1) == 0
        last = _iota2((C, 1), 0) == C - 1
        for h in H:
            t, sl = ts[h], sls[h]
            E = (dLs[h] * t["M"] + dPs[h] * Qs[h]) * t["G"]
            dqkv_ref[:, sl] = _dot(dQs[h], t["k"]) + dqds[h] * t["eg"]
            dqkv_ref[:, D + h * 128:D + (h + 1) * 128] = (
                _dot(dQs[h], t["q"], _TN) + _dot(dMs[h], t["kb"], _TN) + dkds[h] * t["egl"] + dkbs[h] * t["bx"])
            dqkv_ref[:, 2 * D + h * 128:2 * D + (h + 1) * 128] = dvbs[h] * t["bx"]
            dbx_ref[:, sl] = dkbs[h] * t["k"] + dvbs[h] * t["v"]
            dkd_kd = dkds[h] * t["kd"]
            dgam_c = rs(dqds[h] * t["qd"]) + rs(dkbgs[h] * t["kbg"]) - rs(dkd_kd) + rs(E)
            dgam_r = -jnp.sum(E, axis=0, keepdims=True)
            dgam_c = dgam_c + jnp.sum(jnp.where(t["eye"], dgam_r, 0.0), axis=1, keepdims=True)
            dlast = _sum_all(dkd_kd) + t["eL"] * _sum_all(Ss[h] * dS2s[h])
            dgx_ref[:, sl] = jnp.where(lane0, dgam_c + jnp.where(last, dlast, 0.0), 0.0)

    blk = lambda c: pl.BlockSpec((C, D), lambda n: (n, c))
    st = pl.BlockSpec((1, GDN_DK, D), lambda n: (n, 0, 0))
    in_specs = [blk(0), blk(1), blk(0), blk(0), blk(0), st,
                pl.BlockSpec((1, GDN_H * C, C), lambda n: (n, 0, 0)), blk(0), blk(0), st, blk(0)]
    out_specs = [pl.BlockSpec((C, 3 * D), lambda n: (n, 0)), blk(0), blk(0)]
    out_shape = [jax.ShapeDtypeStruct((T, 3 * D), F32), jax.ShapeDtypeStruct((T, D), F32),
                 jax.ShapeDtypeStruct((T, D), F32)]
    ins = [qk, qk, v, bx, gx, s_save, t_save, vn, dvn, ds_save, do]
    if ride:
        ins, in_specs = ins + ride.srcs, in_specs + ride.specs
        out_shape, out_specs = out_shape + ride.out_shape, out_specs + ride.specs
    res = pl.pallas_call(
        body, grid=(N,), in_specs=in_specs, out_specs=out_specs, out_shape=out_shape,
        scratch_shapes=ride.scratch if ride else [], name="gdn_rest_bwd",
        compiler_params=_params(("arbitrary",) if ride else ("parallel",)))(*ins)
    return (list(res[:3]), list(res[3:])) if ride else list(res)


def _ssd_seg(al_pair, half, s):
    L = SSM_L
    ri, ci = _iota2((L, L), 0), _iota2((L, L), 1)
    ac = jnp.max(jnp.where(half == s, al_pair, _NEG), axis=1, keepdims=True)
    ar = jnp.sum(jnp.where(ri == ci, ac, 0.0), axis=0, keepdims=True)
    return jnp.exp(jnp.where(ri >= ci, ac - ar, _NEG))


def _last_row(a):
    return jnp.sum(jnp.where(_iota2((a.shape[0], 1), 0) == a.shape[0] - 1, a, 0.0), axis=0, keepdims=True)


def _ssd_core_fwd(xbc, dtx, alx):
    T = xbc.shape[0]
    L, CS = SSM_L, SSM_SCAN_CHUNKS
    Nc = T // L

    def body(x_all, bc_all, dt_all, al_all, y_all, hs_all, H_scr):
        @pl.when(pl.program_id(0) == 0)
        def _():
            H_scr[...] = jnp.zeros_like(H_scr)

        for cc in range(CS):
            rows = pl.ds(cc * L, L)
            chunk(x_all.at[rows], bc_all.at[rows], dt_all.at[rows], al_all.at[rows], y_all.at[rows], hs_all.at[cc],
                  H_scr)

    def chunk(x_ref, bc_ref, dt_ref, al_ref, y_ref, hs_ref, H_scr):
        half = _iota2((L, 128), 1) >> 6
        for g in range(2):
            gs = slice(g * 512, (g + 1) * 512)
            Bg = bc_ref[:, g * 128:(g + 1) * 128]
            Cg = bc_ref[:, 256 + g * 128:256 + (g + 1) * 128]
            alg = al_ref[:, gs]
            alast = _last_row(alg)
            xdt = x_ref[:, gs] * dt_ref[:, gs]
            Hg = H_scr[:, gs]
            hs_ref[:, gs] = Hg
            CB = _dot(Cg, Bg, _NT)
            y_ref[:, gs] = jnp.exp(alg) * _dot(Cg, Hg)
            H_scr[:, gs] = Hg * jnp.exp(alast) + _dot(Bg, jnp.exp(alast - alg) * xdt, _TN)
            for j in range(4):
                ps = slice(g * 512 + j * 128, g * 512 + (j + 1) * 128)
                al_pair = al_ref[:, ps]
                xp = x_ref[:, ps] * dt_ref[:, ps]
                ys = [_dot(_ssd_seg(al_pair, half, s) * CB, xp) for s in range(2)]
                y_ref[:, ps] += jnp.where(half == 0, ys[0], ys[1])

    row = pl.BlockSpec((CS * L, D), lambda c: (c, 0))
    return pl.pallas_call(
        body, grid=(Nc // CS,), in_specs=[row, pl.BlockSpec((CS * L, 512), lambda c: (c, 2)), row, row],
        out_specs=[row, pl.BlockSpec((CS, SSM_N, D), lambda c: (c, 0, 0))],
        out_shape=[jax.ShapeDtypeStruct((T, D), F32), jax.ShapeDtypeStruct((Nc, SSM_N, D), F32)],
        scratch_shapes=[pltpu.VMEM((SSM_N, D), F32)], name="ssd_core_fwd",
        compiler_params=_params(("arbitrary",)))(xbc, xbc, dtx, alx)


def _ssd_core_bwd(xbc, dtx, alx, h_save, dyy, d_x):
    T = xbc.shape[0]
    L, CS = SSM_L, SSM_SCAN_CHUNKS
    Nc = T // L
    NB = Nc // CS

    def body(x_all, bc_all, dt_all, al_all, hs_all, dy_all, d_ref, dx_all, ddt_all, dal_all, dH_scr):
        @pl.when(pl.program_id(0) == 0)
        def _():
            dH_scr[...] = jnp.zeros_like(dH_scr)

        for cc in reversed(range(CS)):
            rows = pl.ds(cc * L, L)
            chunk(x_all.at[rows], bc_all.at[rows], dt_all.at[rows], al_all.at[rows], hs_all.at[cc], dy_all.at[rows],
                  d_ref, dx_all.at[rows], ddt_all.at[rows], dal_all.at[rows], dH_scr)

    def chunk(x_ref, bc_ref, dt_ref, al_ref, hs_ref, dy_ref, d_ref, dx_ref, ddt_ref, dal_ref, dH_scr):
        lane = _iota2((L, 128), 1)
        half = lane >> 6
        rowi = _iota2((L, 1), 0)
        ri, ci = _iota2((L, L), 0), _iota2((L, L), 1)
        for g in range(2):
            gs = slice(g * 512, (g + 1) * 512)
            Bg = bc_ref[:, g * 128:(g + 1) * 128]
            Cg = bc_ref[:, 256 + g * 128:256 + (g + 1) * 128]
            alg = al_ref[:, gs]
            alast = _last_row(alg)
            eal, edec, eL = jnp.exp(alg), jnp.exp(alast - alg), jnp.exp(alast)
            xg, dtg, dYg = x_ref[:, gs], dt_ref[:, gs], dy_ref[:, gs]
            xdt = xg * dtg
            Hg = hs_ref[:, gs]
            dH2 = dH_scr[:, gs]
            CB = _dot(Cg, Bg, _NT)
            dYe = eal * dYg
            dH_scr[:, gs] = dH2 * eL + _dot(Cg, dYe, _TN)
            dC = _dot(dYe, Hg, _NT)
            zg = edec * xdt
            dz = _dot(Bg, dH2)
            dB = _dot(zg, dH2, _NT)
            tz = dz * zg
            dal = dYe * _dot(Cg, Hg) - tz
            dalast = jnp.sum(tz, axis=0, keepdims=True) + eL * jnp.sum(Hg * dH2, axis=0, keepdims=True)
            dal = dal + jnp.where(rowi == L - 1, dalast, 0.0)
            dxdt_g = edec * dz
            dx_ref[:, gs] = dxdt_g * dtg + dYg * d_ref[:, gs]
            ddt_ref[:, gs] = dxdt_g * xg
            dal_ref[:, gs] = dal
            dCB = jnp.zeros((L, L), F32)
            for j in range(4):
                ps = slice(g * 512 + j * 128, g * 512 + (j + 1) * 128)
                al_pair = al_ref[:, ps]
                xp = x_ref[:, ps] * dt_ref[:, ps]
                dYp = dy_ref[:, ps]
                dxp = []
                dal_p = jnp.zeros((L, 128), F32)
                for s in range(2):
                    seg = _ssd_seg(al_pair, half, s)
                    W = seg * CB
                    dW = _dot(jnp.where(half == s, dYp, 0.0), xp, _NT)
                    dxp.append(_dot(W, dYp, _TN))
                    dCB = dCB + dW * seg
                    Es = dW * W
                    dac = jnp.sum(Es, axis=1, keepdims=True) - jnp.sum(
                        jnp.where(ri == ci, jnp.sum(Es, axis=0, keepdims=True), 0.0), axis=1, keepdims=True)
                    dal_p = dal_p + jnp.where(lane == 64 * s, dac, 0.0)
                dxdt_p = jnp.where(half == 0, dxp[0], dxp[1])
                dx_ref[:, ps] += dxdt_p * dt_ref[:, ps]
                ddt_ref[:, ps] += dxdt_p * x_ref[:, ps]
                dal_ref[:, ps] += dal_p
            dx_ref[:, D + g * 128:D + (g + 1) * 128] = dB + _dot(dCB, Cg, _TN)
            dx_ref[:, D + 256 + g * 128:D + 256 + (g + 1) * 128] = dC + _dot(dCB, Bg)

    row = pl.BlockSpec((CS * L, D), lambda c: (NB - 1 - c, 0))
    bcs = pl.BlockSpec((CS * L, 512), lambda c: (NB - 1 - c, 2))
    return pl.pallas_call(
        body, grid=(NB,),
        in_specs=[row, bcs, row, row, pl.BlockSpec((CS, SSM_N, D), lambda c: (NB - 1 - c, 0, 0)), row,
                  pl.BlockSpec((1, D), lambda c: (0, 0))],
        out_specs=[pl.BlockSpec((CS * L, D + 512), lambda c: (NB - 1 - c, 0)), row, row],
        out_shape=[jax.ShapeDtypeStruct((T, D + 512), F32),
                   jax.ShapeDtypeStruct((T, D), F32), jax.ShapeDtypeStruct((T, D), F32)],
        scratch_shapes=[pltpu.VMEM((SSM_N, D), F32)], name="ssd_core_bwd",
        compiler_params=_params(("arbitrary",)))(xbc, xbc, dtx, alx, h_save, dyy, d_x)


_EARLY = ("w_out", "wq_mem", "wk_mem", "wv_mem", "wo_mem")
_LATE = ("w_up", "w_down")
_GRADS_MLP = ("w_down", "w_up")
_GRADS_MID = ("wo_mem", "wq_mem", "wk_mem", "wv_mem", "w_out")


def _gather_ride(shards, names):
    return None if shards is None else _Ride([shards[n] for n in names], shard=True)


def _grad_ride(shards, G, names):
    return None if shards is None else _Ride([_slots_from_full(n, G[n]) for n in names], shard=False)


def _local_step(x, mem, tgt, W, shards=None):
    T = x.shape[0]
    W = dict(W)
    cw_qk, cw_v = W["gdn_conv_w"][:, :2 * D], W["gdn_conv_w"][:, 2 * D:]
    h1 = _rmsnorm_fwd(x, W["norm1_w"], name="norm1_fwd")
    ride = _gather_ride(shards, _EARLY)
    p = _mm(h1, W["w_in_pad"], name="in_proj", ride=ride)
    if ride:
        p, got = p
        W.update({n: _full_from_slots(n, g) for n, g in zip(_EARLY, got)})
    qk = _conv_fwd(p, C_QKV, 2 * D, cw_qk, None, l2=True, name="gdn_conv_qk_fwd")
    v_g = _conv_fwd(p, C_QKV + 2 * D, D, cw_v, None, l2=False, name="gdn_conv_v_fwd")
    bx, gx = _gdn_gates_fwd(p, W["gdn_alog_row"], W["gdn_dtb_row"])
    ride = _gather_ride(shards, _LATE)
    prep = _gdn_prep(qk, v_g, bx, gx, ride)
    if ride:
        prep, got = prep
        W.update({n: _full_from_slots(n, g) for n, g in zip(_LATE, got)})
    u_g, w_g, qd_g, kd_g, p_g, t_save = prep
    o_g, vn_g, s_save = _gdn_scan_fwd(u_g, w_g, qd_g, kd_g, p_g, gx)
    mix = _gdn_post_fwd(o_g, p, W["gdn_norm_x"])
    xbc = _conv_fwd(p, C_XBC, D + 512, W["ssm_conv_w"], W["ssm_conv_b"], l2=False, name="ssm_conv_fwd", bc=512)
    dtx, alx = _ssd_dt_fwd(p, W["ssm_dtb_row"], W["ssm_alog_x"])
    y_s, h_save = _ssd_core_fwd(xbc, dtx, alx)
    mix = _ssd_post_fwd(y_s, xbc, p, W["ssm_d_x"], W["ssm_norm_w"].reshape(1, D), mix)
    x1, h2 = _mm(mix, W["w_out"], epi="res_norm", extra=(x, W["norm2_w"]), bm=512, name="out_proj")
    qm = _mm(h2, W["wq_mem"], out_dtype=BF16, name="q_proj")
    m = _rmsnorm_fwd(mem, W["mem_norm_w"], name="mem_norm_fwd")
    km = _mm(m, W["wk_mem"], name="k_proj")
    vm = _mm(m, W["wv_mem"], name="v_proj")
    oa = _attn_fwd(qm, km, vm)
    x2, h3 = _mm(oa, W["wo_mem"], epi="res_norm", extra=(x1, W["norm3_w"]), bm=512, name="o_proj")
    u, act = _mm(h3, W["w_up"], epi="relu2", out_dtype=BF16, name="mlp_up")
    dx3, g_final, loss = _mm(act, W["w_down"], epi="res_loss", extra=(x2, tgt, W["final_norm_w"]), bm=512,
                             name="mlp_down_loss")
    G = {"final_norm_w": g_final.reshape(D)}
    dpre = _mm(dx3, W["w_down"], dims="nt", epi="mul2", extra=u, out_dtype=BF16, name="mlp_down_dx")
    G["w_down"] = _mm(act, dx3, dims="tn", out_dtype=BF16, name="mlp_down_dw")
    G["w_up"] = _mm(h3, dpre, dims="tn", out_dtype=BF16, name="mlp_up_dw")
    dx2, gw = _mm(dpre, W["w_up"], dims="nt", epi="norm_bwd", extra=(x2, dx3, W["norm3_w"]), bm=512,
                  name="mlp_up_dx")
    G["norm3_w"] = gw.reshape(D)
    do_a = _mm(dx2, W["wo_mem"], dims="nt", out_dtype=BF16, name="o_proj_dx")
    G["wo_mem"] = _mm(oa, dx2, dims="tn", out_dtype=BF16, name="o_proj_dw")
    dq, dk, dv = _attn_bwd(qm, km, vm, do_a)
    G["wq_mem"] = _mm(h2, dq, dims="tn", out_dtype=BF16, name="q_proj_dw")
    dx1, gw = _mm(dq, W["wq_mem"], dims="nt", epi="norm_bwd", extra=(x1, dx2, W["norm2_w"]), bm=512,
                  name="q_proj_dx")
    G["norm2_w"] = gw.reshape(D)
    G["wk_mem"] = _mm(m, dk, dims="tn", out_dtype=BF16, name="k_proj_dw")
    G["wv_mem"] = _mm(m, dv, dims="tn", out_dtype=BF16, name="v_proj_dw")
    dm = _mm(dk, W["wk_mem"], dims="nt", name="k_proj_dx")
    dm = _mm(dv, W["wv_mem"], dims="nt", epi="res", extra=dm, name="v_proj_dx")
    _, G["mem_norm_w"] = _rmsnorm_bwd(mem, W["mem_norm_w"], dm, None, name="mem_norm_bwd")
    dmix = _mm(dx1, W["w_out"], dims="nt", name="out_proj_dx")
    G["w_out"] = _mm(mix, dx1, dims="tn", out_dtype=BF16, name="out_proj_dw")
    do_g, dp, G["gdn_norm_x"] = _gdn_post_bwd(dmix, o_g, p, W["gdn_norm_x"])
    dvn_g, ds_save = _gdn_scan_bwd(w_g, qd_g, kd_g, p_g, gx, do_g)
    ride = _grad_ride(shards, G, _GRADS_MLP)
    rest = _gdn_rest_bwd(qk, v_g, bx, gx, s_save, t_save, vn_g, dvn_g, ds_save, do_g, ride)
    if ride:
        rest, got = rest
        G.update(zip(_GRADS_MLP, got))
    dqkvn, dbx, dgx = rest
    dy_qk, gcw_qk, _ = _conv_bwd_act(p, C_QKV, 2 * D, cw_qk, None, dqkvn, 0, l2=True, name="gdn_conv_qk_bwd_act")
    dy_v, gcw_v, _ = _conv_bwd_act(p, C_QKV + 2 * D, D, cw_v, None, dqkvn, 2 * D, l2=False,
                                   name="gdn_conv_v_bwd_act")
    G["gdn_conv_w"] = jnp.concatenate([gcw_qk, gcw_v], axis=1)
    dp = _conv_bwd_in(dy_qk, cw_qk, dp, C_QKV, T, name="gdn_conv_qk_bwd_in")
    dp = _conv_bwd_in(dy_v, cw_v, dp, C_QKV + 2 * D, T, name="gdn_conv_v_bwd_in")
    dp, G["gdn_alog_row"], G["gdn_dtb_row"] = _gdn_gates_bwd(p, W["gdn_alog_row"], W["gdn_dtb_row"], dbx, dgx, dp)
    dyy, dp, G["ssm_d_x"], G["ssm_norm_w"] = _ssd_post_bwd(dmix, y_s, xbc, p, W["ssm_d_x"],
                                                          W["ssm_norm_w"].reshape(1, D), dp)
    dxbc, ddtx, dalx = _ssd_core_bwd(xbc, dtx, alx, h_save, dyy, W["ssm_d_x"])
    dy_s, G["ssm_conv_w"], G["ssm_conv_b"] = _conv_bwd_act(p, C_XBC, D + 512, W["ssm_conv_w"], W["ssm_conv_b"],
                                                           dxbc, 0, l2=False, name="ssm_conv_bwd_act", bc=512)
    dp = _conv_bwd_in(dy_s, W["ssm_conv_w"], dp, C_XBC, T, name="ssm_conv_bwd_in", bc=512)
    dp, G["ssm_dtb_row"], G["ssm_alog_x"] = _ssd_dt_bwd(p, W["ssm_dtb_row"], W["ssm_alog_x"], ddtx, dalx, dp)
    ride = _grad_ride(shards, G, _GRADS_MID)
    g_in = _mm(h1, dp, dims="tn", out_dtype=BF16, name="in_proj_dw", ride=ride)
    if ride:
        g_in, got = g_in
        G.update(zip(_GRADS_MID, got))
    G["w_in"] = _unpad_w_in(g_in)
    ride = _grad_ride(shards, G, ("w_in",))
    res = _mm(dp, W["w_in_pad"], dims="nt", epi="norm_bwd", extra=(x, dx1, W["norm1_w"]), bm=512,
              name="in_proj_dx", ride=ride)
    if ride:
        res, got = res
        G["w_in"] = got[0]
    dx, gw = res
    G["norm1_w"] = gw.reshape(D)
    return loss, dx, G


def _all_gather(shards, out_dtype, *, name):
    n = len(shards)

    def body(*refs):
        x_refs, out_refs, stage = refs[:n], refs[n:2 * n], refs[2 * n:3 * n]
        send_sems, recv_sems, local_sems = refs[3 * n:]
        x, y, c = _place()
        me, sibling = (x, y, c), (x, y, 1 - c)
        chips = [(1 - x, y), (x, 1 - y), (1 - x, 1 - y)]

        def slot(px, py, pc):
            return 4 * px + 2 * py + pc

        def copy(a, k, block, to, src=None):
            dst = out_refs[a].at[slot(*block)]
            return pltpu.make_async_remote_copy(
                src_ref=dst if src is None else src, dst_ref=dst, send_sem=send_sems.at[a, k],
                recv_sem=recv_sems.at[a, k], device_id=to, device_id_type=_MESH)

        for a in range(n):
            stage[a][...] = x_refs[a][...].astype(out_dtype)
        mine = [pltpu.make_async_copy(stage[a], out_refs[a].at[slot(*me)], local_sems.at[a]) for a in range(n)]
        for cp in mine:
            cp.start()
        first = []
        for a in range(n):
            first.append(copy(a, 0, me, sibling, src=stage[a]))
            first += [copy(a, 1 + j, me, (*chip, c), src=stage[a]) for j, chip in enumerate(chips)]
        for cp in first:
            cp.start()
        passed = [[copy(a, 4 + j, (*chip, c), sibling) for j, chip in enumerate(chips)] for a in range(n)]
        for j, chip in enumerate(chips):
            for a in range(n):
                copy(a, 1 + j, (*chip, c), me).wait_recv()
                passed[a][j].start()
        for a in range(n):
            copy(a, 0, sibling, me).wait_recv()
            for j, chip in enumerate(chips):
                copy(a, 4 + j, (*chip, 1 - c), me).wait_recv()
        for cp in first + [cp for row in passed for cp in row]:
            cp.wait_send()
        for cp in mine:
            cp.wait()

    outs = pl.pallas_call(
        body, in_specs=[_VM] * n, out_specs=[_ANY] * n,
        out_shape=[jax.ShapeDtypeStruct((N_DEV,) + s.shape, out_dtype) for s in shards],
        scratch_shapes=[pltpu.VMEM(s.shape, out_dtype) for s in shards]
        + [pltpu.SemaphoreType.DMA((n, 7)), pltpu.SemaphoreType.DMA((n, 7)), pltpu.SemaphoreType.DMA((n,))],
        name=name, compiler_params=pltpu.CompilerParams(vmem_limit_bytes=VMEM_LIMIT))(*shards)
    return list(outs)


def _cast_bf16(arrs, *, name):
    n = len(arrs)

    def body(*refs):
        for a in range(n):
            refs[n + a][...] = refs[a][...].astype(BF16)

    return list(pl.pallas_call(
        body, in_specs=[_VM] * n, out_specs=[_VM] * n,
        out_shape=[jax.ShapeDtypeStruct(s.shape, BF16) for s in arrs], name=name,
        compiler_params=pltpu.CompilerParams(vmem_limit_bytes=VMEM_LIMIT))(*arrs))


def _sum8(a, *, name):
    _, R, Cc = a.shape
    br = _pick_rows(R, 128)

    def body(a_ref, o_ref):
        s = a_ref[0].astype(F32)
        for k in range(1, N_DEV):
            s = s + a_ref[k].astype(F32)
        o_ref[...] = s

    return pl.pallas_call(
        body, grid=(R // br,), in_specs=[pl.BlockSpec((N_DEV, br, Cc), lambda i: (0, i, 0))],
        out_specs=pl.BlockSpec((br, Cc), lambda i: (i, 0)), out_shape=jax.ShapeDtypeStruct((R, Cc), F32),
        name=name, compiler_params=_params(("parallel",)))(a)


def _pick_rows(R, cap):
    if R <= cap:
        return R
    for d in range(cap, 7, -8):
        if R % d == 0:
            return d
    return R


def _adamw(w, g, m, v, *, name):
    shape = w.shape
    as2d = (lambda t: t.reshape(1, -1)) if w.ndim == 1 else (lambda t: t)
    w2, g2, m2, v2 = as2d(w), as2d(g), as2d(m), as2d(v)
    R, Cc = w2.shape
    br = _pick_rows(R, 256)
    c1 = 1.0 - ADAM_B1 ** ADAM_STEP
    c2 = 1.0 - ADAM_B2 ** ADAM_STEP

    def body(w_ref, g_ref, m_ref, v_ref, d_ref, nm_ref, nv_ref):
        gv = g_ref[...]
        nm = ADAM_B1 * m_ref[...] + (1.0 - ADAM_B1) * gv
        nv = ADAM_B2 * v_ref[...] + (1.0 - ADAM_B2) * (gv * gv)
        nm_ref[...] = nm
        nv_ref[...] = nv
        d_ref[...] = -ADAM_LR * ((nm / c1) / (jnp.sqrt(nv / c2) + ADAM_EPS) + ADAM_WD * w_ref[...])

    blk = pl.BlockSpec((br, Cc), lambda i: (i, 0))
    outs = pl.pallas_call(
        body, grid=(R // br,), in_specs=[blk] * 4, out_specs=[blk] * 3,
        out_shape=[jax.ShapeDtypeStruct((R, Cc), F32)] * 3, name=name,
        compiler_params=_params(("parallel",)))(w2, g2, m2, v2)
    return tuple(o.reshape(shape) for o in outs)


_BIG = ("w_in", "w_out", "wq_mem", "wk_mem", "wv_mem", "wo_mem", "w_up", "w_down")
_COL_SHARDED = ("w_in", "w_up")
_WEIGHTS = ("norm1_w", "w_in", "gdn_conv_w", "gdn_a_log", "gdn_dt_bias", "gdn_norm_w", "ssm_conv_w", "ssm_conv_b",
            "ssm_a_log", "ssm_dt_bias", "ssm_d", "ssm_norm_w", "w_out", "norm2_w", "mem_norm_w", "wq_mem", "wk_mem",
            "wv_mem", "wo_mem", "norm3_w", "w_up", "w_down", "final_norm_w")
_IN_PAD = 112


def _full_from_slots(name, g):
    if name in _COL_SHARDED:
        return jnp.transpose(g, (1, 0, 2)).reshape(g.shape[1], N_DEV * g.shape[2])
    return g.reshape(N_DEV * g.shape[1], g.shape[2])


def _slots_from_full(name, f):
    if name in _COL_SHARDED:
        return jnp.transpose(f.reshape(f.shape[0], N_DEV, f.shape[1] // N_DEV), (1, 0, 2))
    return f.reshape(N_DEV, f.shape[0] // N_DEV, f.shape[1])


def _pad_w_in(w):
    z = jnp.zeros((w.shape[0], _IN_PAD), w.dtype)
    return jnp.concatenate([w[:, :4096], w[:, 4112:6672], w[:, 4096:4112], z, w[:, 6672:6688], z], axis=1)


def _unpad_w_in(gp):
    return jnp.concatenate([gp[:, :4096], gp[:, C_GATE:C_GATE + 16], gp[:, 4096:C_GATE], gp[:, C_DT:C_DT + 16]],
                           axis=1)


def _pack_rows(vals):
    rows, offs, r = [], [], 0
    for vflat in vals:
        nrow = 8 * -(-vflat.shape[0] // 1024)
        rows.append(jnp.pad(vflat, (0, nrow * 128 - vflat.shape[0])).reshape(nrow, 128))
        offs.append((r, vflat.shape[0]))
        r += nrow
    return jnp.concatenate(rows, axis=0), offs


def _unpack_rows(packed, offs, shapes):
    out = []
    for (r, nel), shp in zip(offs, shapes):
        nrow = -(-nel // 128)
        out.append(packed[r:r + nrow].reshape(-1)[:nel].reshape(shp))
    return out


def kernel(x, mem, norm1_w, w_in, gdn_conv_w, gdn_a_log, gdn_dt_bias, gdn_norm_w, ssm_conv_w, ssm_conv_b, ssm_a_log, ssm_dt_bias, ssm_d, ssm_norm_w, w_out, norm2_w, mem_norm_w, wq_mem, wk_mem, wv_mem, wo_mem, norm3_w, w_up, w_down, final_norm_w, loss_target, m_norm1_w, m_w_in, m_gdn_conv_w, m_gdn_a_log, m_gdn_dt_bias, m_gdn_norm_w, m_ssm_conv_w, m_ssm_conv_b, m_ssm_a_log, m_ssm_dt_bias, m_ssm_d, m_ssm_norm_w, m_w_out, m_norm2_w, m_mem_norm_w, m_wq_mem, m_wk_mem, m_wv_mem, m_wo_mem, m_norm3_w, m_w_up, m_w_down, m_final_norm_w, v_norm1_w, v_w_in, v_gdn_conv_w, v_gdn_a_log, v_gdn_dt_bias, v_gdn_norm_w, v_ssm_conv_w, v_ssm_conv_b, v_ssm_a_log, v_ssm_dt_bias, v_ssm_d, v_ssm_norm_w, v_w_out, v_norm2_w, v_mem_norm_w, v_wq_mem, v_wk_mem, v_wv_mem, v_wo_mem, v_norm3_w, v_w_up, v_w_down, v_final_norm_w):
    args = dict(locals())
    w_loc = {n: args[n] for n in _WEIGHTS}
    me = 4 * lax.axis_index("x") + 2 * lax.axis_index("y") + lax.axis_index("c")

    w_in_full = _full_from_slots("w_in", _all_gather([w_in], BF16, name="gather_w_in")[0])
    later = _EARLY + _LATE
    shards = dict(zip(later, _cast_bf16([w_loc[n] for n in later], name="cast_shards")))
    conv_pack, conv_offs = _pack_rows([gdn_conv_w.reshape(-1), ssm_conv_w.reshape(-1)])
    conv_all = _all_gather([conv_pack], F32, name="gather_conv")[0]
    gdn_cw, ssm_cw = [], []
    for k in range(N_DEV):
        a, b = _unpack_rows(conv_all[k], conv_offs, [gdn_conv_w.shape, ssm_conv_w.shape])
        gdn_cw.append(a)
        ssm_cw.append(b)
    W = {
        "w_in_pad": _pad_w_in(w_in_full),
        "norm1_w": norm1_w, "norm2_w": norm2_w, "norm3_w": norm3_w, "mem_norm_w": mem_norm_w,
        "final_norm_w": final_norm_w, "ssm_norm_w": ssm_norm_w, "ssm_conv_b": ssm_conv_b,
        "gdn_conv_w": jnp.concatenate(gdn_cw, axis=1), "ssm_conv_w": jnp.concatenate(ssm_cw, axis=1),
        "gdn_alog_row": jnp.pad(gdn_a_log, (GDN_H, 128 - 2 * GDN_H)).reshape(1, 128),
        "gdn_dtb_row": jnp.pad(gdn_dt_bias, (GDN_H, 128 - 2 * GDN_H)).reshape(1, 128),
        "gdn_norm_x": jnp.tile(gdn_norm_w, GDN_H).reshape(1, D),
        "ssm_dtb_row": jnp.pad(ssm_dt_bias, (0, 128 - SSM_H)).reshape(1, 128),
        "ssm_alog_x": jnp.repeat(ssm_a_log, SSM_P).reshape(1, D),
        "ssm_d_x": jnp.repeat(ssm_d, SSM_P).reshape(1, D),
    }

    loss_part, grad_x, G = _local_step(x[0], mem[0], loss_target[0], W, shards)

    grads = {n: _sum8(G[n], name="sum_" + n) for n in _BIG}

    small = {
        "norm1_w": G["norm1_w"], "gdn_conv_w": G["gdn_conv_w"], "gdn_a_log": G["gdn_alog_row"][0, GDN_H:2 * GDN_H],
        "gdn_dt_bias": G["gdn_dtb_row"][0, GDN_H:2 * GDN_H], "gdn_norm_w": G["gdn_norm_x"].reshape(GDN_H, 128).sum(0),
        "ssm_conv_w": G["ssm_conv_w"], "ssm_conv_b": G["ssm_conv_b"],
        "ssm_a_log": G["ssm_alog_x"].reshape(SSM_H, SSM_P).sum(1), "ssm_dt_bias": G["ssm_dtb_row"][0, :SSM_H],
        "ssm_d": G["ssm_d_x"].reshape(SSM_H, SSM_P).sum(1), "ssm_norm_w": G["ssm_norm_w"].reshape(D),
        "norm2_w": G["norm2_w"], "mem_norm_w": G["mem_norm_w"], "norm3_w": G["norm3_w"],
        "final_norm_w": G["final_norm_w"], "loss": loss_part[0, :1],
    }
    names = list(small)
    pack, offs = _pack_rows([small[n].reshape(-1) for n in names])
    tot = _sum8(_all_gather([pack], F32, name="gather_small")[0], name="sum_small")
    summed = dict(zip(names, _unpack_rows(tot, offs, [small[n].shape for n in names])))
    loss = summed.pop("loss")[0]
    for n in ("gdn_conv_w", "ssm_conv_w"):
        width = w_loc[n].shape[1]
        summed[n] = lax.dynamic_slice_in_dim(summed[n], me * width, width, axis=1)
    grads.update(summed)

    upd = {n: _adamw(w_loc[n], grads[n], args["m_" + n], args["v_" + n], name="adamw_" + n) for n in _WEIGHTS}
    return (loss, grad_x[None], *[grads[n] for n in _WEIGHTS], *[upd[n][0] for n in _WEIGHTS],
            *[upd[n][1] for n in _WEIGHTS], *[upd[n][2] for n in _WEIGHTS])
```

```python
import functools
import math

import jax
import jax.numpy as jnp
from jax import lax
from jax.experimental import pallas as pl
from jax.experimental.pallas import tpu as pltpu

F32 = jnp.float32
BF16 = jnp.bfloat16
_MXU = BF16

D = 1024
EPS = 1e-6
CONV_K = 4
GDN_H, GDN_DK, GDN_C = 8, 128, 64
GDN_SCAN_CHUNKS = 4
_GDN_GROUPS = (range(0, 8),)
SSM_H, SSM_P, SSM_L, SSM_N = 16, 64, 128, 128
SSM_SCAN_CHUNKS = 2
MEM_H, MEM_HD = 4, 256
D_FF = 4096
N_DEV = 8

C_QKV, C_ZG, C_ZS, C_XBC, C_GATE, C_DT, C_TOT = 0, 3072, 4096, 5120, 6656, 6784, 6912
P_HALO = 16

ADAM_LR, ADAM_B1, ADAM_B2, ADAM_EPS, ADAM_WD, ADAM_STEP = 0.001, 0.9, 0.999, 1e-08, 0.01, 10

VMEM_LIMIT = 56 * 1024 * 1024

_NN = (((1,), (0,)), ((), ()))
_NT = (((1,), (1,)), ((), ()))
_TN = (((0,), (0,)), ((), ()))


def _dot(a, b, dims=_NN):
    return lax.dot_general(a.astype(_MXU), b.astype(_MXU), dims, preferred_element_type=F32)


def _split3(a):
    a1 = a.astype(BF16)
    r1 = a - a1.astype(F32)
    a2 = r1.astype(BF16)
    return a1, a2, (r1 - a2.astype(F32)).astype(BF16)


def _dot_sel(a, e):
    eb = e.astype(BF16)
    return sum(lax.dot_general(p, eb, _NN, preferred_element_type=F32) for p in _split3(a))


def _sel_dot(e, a):
    eb = e.astype(BF16)
    return sum(lax.dot_general(eb, p, _NN, preferred_element_type=F32) for p in _split3(a))


def _chunk_cumsum(a, tri, chunk):
    return jnp.concatenate([_sel_dot(tri, a[r:r + chunk]) for r in range(0, a.shape[0], chunk)], axis=0)


def _params(sem):
    return pltpu.CompilerParams(dimension_semantics=sem, vmem_limit_bytes=VMEM_LIMIT)


def _pick(n, cap):
    for d in range(min(cap, n), 0, -128):
        if n % d == 0 and d % 128 == 0:
            return d
    return n


def _sigmoid(x):
    return 0.5 * jnp.tanh(0.5 * x) + 0.5


def _silu(x):
    return x * _sigmoid(x)


def _dsilu(x):
    s = _sigmoid(x)
    return s * (1.0 + x * (1.0 - s))


def _softplus(x):
    return jnp.maximum(x, 0.0) + jnp.log(1.0 + jnp.exp(-jnp.abs(x)))


def _iota2(shape, axis):
    return lax.broadcasted_iota(jnp.int32, shape, axis)


def _sum_all(x):
    return jnp.sum(jnp.sum(x, axis=1, keepdims=True), axis=0, keepdims=True)


_MESH = pl.DeviceIdType.MESH
_ANY = pl.BlockSpec(memory_space=pl.ANY)
_VM = pl.BlockSpec(memory_space=pltpu.VMEM)
_REL = [(r >> 2 & 1, r >> 1 & 1, r & 1) for r in range(1, N_DEV)]


def _place():
    return lax.axis_index("x"), lax.axis_index("y"), lax.axis_index("c")


class _Ride:
    def __init__(self, srcs, shard):
        self.srcs, self.shard, self.n = list(srcs), shard, len(srcs)
        self.out_shape = [jax.ShapeDtypeStruct(((N_DEV,) + s.shape) if shard else s.shape, s.dtype)
                          for s in self.srcs]
        self.specs = [_ANY] * self.n
        self.scratch = [pltpu.SemaphoreType.DMA((self.n, N_DEV - 1)), pltpu.SemaphoreType.DMA((self.n, N_DEV - 1)),
                        pltpu.SemaphoreType.DMA((self.n,))]

    def _copies(self, in_refs, out_refs, sems):
        send, recv, loc = sems
        x, y, c = _place()
        me = 4 * x + 2 * y + c
        local, remote, arrive = [], [], []
        for a in range(self.n):
            src = in_refs[a] if self.shard else in_refs[a].at[me]
            local.append(pltpu.make_async_copy(src, out_refs[a].at[me], loc.at[a]))
        for k, (rx, ry, rc) in enumerate(_REL):
            peer = (lax.rem(x + rx, 2), lax.rem(y + ry, 2), lax.rem(c + rc, 2))
            ps = 4 * peer[0] + 2 * peer[1] + peer[2]
            for a in range(self.n):
                src = in_refs[a] if self.shard else in_refs[a].at[ps]
                remote.append(pltpu.make_async_remote_copy(
                    src_ref=src, dst_ref=out_refs[a].at[me], send_sem=send.at[a, k], recv_sem=recv.at[a, k],
                    device_id=peer, device_id_type=_MESH))
                slot = out_refs[a].at[ps]
                arrive.append(pltpu.make_async_remote_copy(
                    src_ref=slot, dst_ref=slot, send_sem=send.at[a, k], recv_sem=recv.at[a, k],
                    device_id=peer, device_id_type=_MESH))
        return local, remote, arrive

    def start(self, in_refs, out_refs, sems):
        local, remote, _ = self._copies(in_refs, out_refs, sems)
        for cp in local + remote:
            cp.start()

    def wait(self, in_refs, out_refs, sems):
        local, remote, arrive = self._copies(in_refs, out_refs, sems)
        for cp in arrive:
            cp.wait_recv()
        for cp in remote:
            cp.wait_send()
        for cp in local:
            cp.wait()


_EPI = {
    "none": ((), ("tile",)),
    "res": (("tile",), ("tile",)),
    "mul2": (("tile",), ("tile",)),
    "relu2": ((), ("tile", "tile")),
    "res_norm": (("tile", "row"), ("tile", "tile")),
    "norm_bwd": (("tile", "tile", "row"), ("tile", "row")),
    "res_loss": (("tile", "tile", "row"), ("tile", "row", "row")),
}


def _mm(a, b, *, dims="nn", epi="none", extra=(), out_dtype=F32, name, bm=1024, bn_cap=1024, bk_cap=2048,
        ride=None):
    if dims == "nn":
        (M, K), (K2, N) = a.shape, b.shape
    elif dims == "nt":
        (M, K), (N, K2) = a.shape, b.shape
    else:
        (K, M), (K2, N) = a.shape, b.shape
    assert K == K2, (a.shape, b.shape, dims)
    bm = _pick(M, bm)
    bn = _pick(N, bn_cap)
    bk = _pick(K, bk_cap)
    nk = K // bk
    dn = {"nn": _NN, "nt": _NT, "tn": _TN}[dims]
    a_spec = (pl.BlockSpec((bk, bm), lambda i, j, k: (k, i)) if dims == "tn"
              else pl.BlockSpec((bm, bk), lambda i, j, k: (i, k)))
    b_spec = (pl.BlockSpec((bn, bk), lambda i, j, k: (j, k)) if dims == "nt"
              else pl.BlockSpec((bk, bn), lambda i, j, k: (k, j)))
    o_spec = pl.BlockSpec((bm, bn), lambda i, j, k: (i, j))
    r_spec = pl.BlockSpec((1, bn), lambda i, j, k: (0, j))
    extra = list(extra) if isinstance(extra, (tuple, list)) else [extra]
    ekinds, okinds = _EPI[epi]
    assert len(extra) == len(ekinds) and (epi not in ("res_norm", "norm_bwd", "res_loss") or bn == N)
    n_extra, n_out = len(ekinds), len(okinds)
    n_ride = ride.n if ride else 0
    gi, gj = M // bm, N // bn

    def body(a_ref, b_ref, *rest):
        ex = rest[:n_extra]
        first = pl.program_id(0) == 0
        ride_in = rest[n_extra:n_extra + n_ride]
        outs = rest[n_extra + n_ride:n_extra + n_ride + n_out]
        ride_out = rest[n_extra + n_ride + n_out:n_extra + 2 * n_ride + n_out]
        if ride:
            at = lambda i, j, k: ((pl.program_id(0) == i) & (pl.program_id(1) == j) & (pl.program_id(2) == k))

            @pl.when(at(0, 0, 0))
            def _():
                ride.start(ride_in, ride_out, rest[-3:])

        def finish(r):
            if epi == "res":
                outs[0][...] = (r + ex[0][...].astype(F32)).astype(outs[0].dtype)
            elif epi == "mul2":
                outs[0][...] = (2.0 * r * ex[0][...].astype(F32)).astype(outs[0].dtype)
            elif epi == "relu2":
                u = jnp.maximum(r, 0.0)
                outs[0][...] = u.astype(outs[0].dtype)
                outs[1][...] = (u * u).astype(outs[1].dtype)
            elif epi == "res_norm":
                y = r + ex[0][...]
                outs[0][...] = y
                rstd = lax.rsqrt(jnp.mean(y * y, axis=1, keepdims=True) + EPS)
                outs[1][...] = (y * rstd * ex[1][...]).astype(outs[1].dtype)
            elif epi == "norm_bwd":
                xv = ex[0][...]
                rstd = lax.rsqrt(jnp.mean(xv * xv, axis=1, keepdims=True) + EPS)
                xh = xv * rstd
                dxh = r * ex[2][...]
                outs[0][...] = ex[1][...] + rstd * (dxh - xh * jnp.mean(dxh * xh, axis=1, keepdims=True))
                dw = jnp.sum(r * xh, axis=0, keepdims=True)

                @pl.when(first)
                def _():
                    outs[1][...] = dw

                @pl.when(jnp.logical_not(first))
                def _():
                    outs[1][...] += dw
            elif epi == "res_loss":
                y = r + ex[0][...]
                wv = ex[2][...]
                rstd = lax.rsqrt(jnp.mean(y * y, axis=1, keepdims=True) + EPS)
                yh = y * rstd
                err = yh * wv - ex[1][...]
                part_loss = 0.5 * jnp.sum(jnp.mean(err * err, axis=1, keepdims=True), axis=0, keepdims=True)
                dyn = err * (1.0 / N)
                dyh = dyn * wv
                outs[0][...] = rstd * (dyh - yh * jnp.mean(dyh * yh, axis=1, keepdims=True))
                dw = jnp.sum(dyn * yh, axis=0, keepdims=True)
                lrow = jnp.broadcast_to(part_loss, (1, N))

                @pl.when(first)
                def _():
                    outs[1][...] = dw
                    outs[2][...] = lrow

                @pl.when(jnp.logical_not(first))
                def _():
                    outs[1][...] += dw
                    outs[2][...] += lrow
            else:
                outs[0][...] = r.astype(outs[0].dtype)

        part = _dot(a_ref[...], b_ref[...], dn)
        if nk == 1:
            finish(part)
        else:
            acc = rest[n_extra + 2 * n_ride + n_out]
            k = pl.program_id(2)

            @pl.when(k == 0)
            def _():
                acc[...] = part

            @pl.when((k > 0) & (k < nk - 1))
            def _():
                acc[...] += part

            @pl.when(k == nk - 1)
            def _():
                finish(acc[...] + part)

        if ride:
            @pl.when(at(gi - 1, gj - 1, nk - 1))
            def _():
                ride.wait(ride_in, ride_out, rest[-3:])

    kind_spec = {"tile": o_spec, "row": r_spec}
    ins = [a, b] + [e.reshape(1, N) if k == "row" else e for e, k in zip(extra, ekinds)]
    in_specs = [a_spec, b_spec] + [kind_spec[k] for k in ekinds]
    out_dtypes = {"res_norm": (F32, BF16), "norm_bwd": (F32, F32), "res_loss": (F32, F32, F32)}.get(
        epi, (out_dtype,) * n_out)
    out_shape = [jax.ShapeDtypeStruct((M, N) if k == "tile" else (1, N), dt) for k, dt in zip(okinds, out_dtypes)]
    out_specs = [kind_spec[k] for k in okinds]
    scratch = [pltpu.VMEM((bm, bn), F32)] if nk > 1 else []
    sem = ("arbitrary" if epi in ("norm_bwd", "res_loss") else "parallel", "parallel", "arbitrary")
    if ride:
        ins, in_specs = ins + ride.srcs, in_specs + ride.specs
        out_shape, out_specs = out_shape + ride.out_shape, out_specs + ride.specs
        scratch, sem = scratch + ride.scratch, ("arbitrary",) * 3
    res = pl.pallas_call(
        body, grid=(gi, gj, nk), in_specs=in_specs, out_specs=out_specs, out_shape=out_shape,
        scratch_shapes=scratch, name=name, compiler_params=_params(sem))(*ins)
    main = res[:n_out] if n_out > 1 else res[0]
    return (main, list(res[n_out:])) if ride else main


def _rmsnorm_fwd(x, w, *, name, bt=256):
    T, Dm = x.shape
    bt = min(bt, T)

    def body(x_ref, w_ref, h_ref):
        xv = x_ref[...]
        r = lax.rsqrt(jnp.mean(xv * xv, axis=1, keepdims=True) + EPS)
        h_ref[...] = (xv * r * w_ref[...]).astype(h_ref.dtype)

    return pl.pallas_call(
        body, grid=(T // bt,),
        in_specs=[pl.BlockSpec((bt, Dm), lambda i: (i, 0)), pl.BlockSpec((1, Dm), lambda i: (0, 0))],
        out_specs=pl.BlockSpec((bt, Dm), lambda i: (i, 0)),
        out_shape=jax.ShapeDtypeStruct((T, Dm), BF16), name=name,
        compiler_params=_params(("parallel",)))(x, w.reshape(1, Dm))


def _rmsnorm_bwd(x, w, dh, dres, *, name, bt=256):
    T, Dm = x.shape
    bt = min(bt, T)
    has_res = dres is not None

    def body(x_ref, w_ref, dh_ref, *rest):
        dres_ref = rest[0] if has_res else None
        dx_ref, dw_ref = rest[-2], rest[-1]
        i = pl.program_id(0)
        xv = x_ref[...]
        r = lax.rsqrt(jnp.mean(xv * xv, axis=1, keepdims=True) + EPS)
        xh = xv * r
        dhv = dh_ref[...].astype(F32)
        dxh = dhv * w_ref[...]
        dx = r * (dxh - xh * jnp.mean(dxh * xh, axis=1, keepdims=True))
        if has_res:
            dx = dx + dres_ref[...]
        dx_ref[...] = dx

        @pl.when(i == 0)
        def _():
            dw_ref[...] = jnp.zeros_like(dw_ref)

        dw_ref[...] += jnp.sum(dhv * xh, axis=0, keepdims=True)

    row = pl.BlockSpec((bt, Dm), lambda i: (i, 0))
    vec = pl.BlockSpec((1, Dm), lambda i: (0, 0))
    ins = [x, w.reshape(1, Dm), dh] + ([dres] if has_res else [])
    dx, dw = pl.pallas_call(
        body, grid=(T // bt,), in_specs=[row, vec, row] + ([row] if has_res else []),
        out_specs=[row, vec],
        out_shape=[jax.ShapeDtypeStruct((T, Dm), F32), jax.ShapeDtypeStruct((1, Dm), F32)],
        name=name, compiler_params=_params(("arbitrary",)))(*ins)
    return dx, dw.reshape(Dm)


def _attn_fwd(q, km, vm, *, bt=256):
    T = q.shape[0]
    M = km.shape[0]
    bt = min(bt, T)
    scale = MEM_HD ** -0.5

    def body(q_ref, k_ref, v_ref, o_ref):
        for h in range(MEM_H):
            sl = slice(h * MEM_HD, (h + 1) * MEM_HD)
            s = _dot(q_ref[:, sl], k_ref[:, sl], _NT) * scale
            s = s - jnp.max(s, axis=1, keepdims=True)
            e = jnp.exp(s)
            p = e / jnp.sum(e, axis=1, keepdims=True)
            o_ref[:, sl] = _dot(p, v_ref[:, sl]).astype(o_ref.dtype)

    row = pl.BlockSpec((bt, D), lambda i: (i, 0))
    mem = pl.BlockSpec((M, D), lambda i: (0, 0))
    return pl.pallas_call(
        body, grid=(T // bt,), in_specs=[row, mem, mem], out_specs=row,
        out_shape=jax.ShapeDtypeStruct((T, D), BF16), name="attn_fwd",
        compiler_params=_params(("parallel",)))(q, km, vm)


def _attn_bwd(q, km, vm, do, *, bt=256):
    T = q.shape[0]
    M = km.shape[0]
    bt = min(bt, T)
    scale = MEM_HD ** -0.5

    def body(q_ref, k_ref, v_ref, do_ref, dq_ref, dk_ref, dv_ref):
        i = pl.program_id(0)

        @pl.when(i == 0)
        def _():
            dk_ref[...] = jnp.zeros_like(dk_ref)
            dv_ref[...] = jnp.zeros_like(dv_ref)

        sls = [slice(h * MEM_HD, (h + 1) * MEM_HD) for h in range(MEM_H)]
        ss = [_dot(q_ref[:, sl], k_ref[:, sl], _NT) * scale for sl in sls]
        dps = [_dot(do_ref[:, sl], v_ref[:, sl], _NT) for sl in sls]
        es = [jnp.exp(s - jnp.max(s, axis=1, keepdims=True)) for s in ss]
        ps = [e / jnp.sum(e, axis=1, keepdims=True) for e in es]
        dss = [p * (dp - jnp.sum(dp * p, axis=1, keepdims=True)) * scale for p, dp in zip(ps, dps)]
        for sl, p, ds in zip(sls, ps, dss):
            dq_ref[:, sl] = _dot(ds, k_ref[:, sl]).astype(dq_ref.dtype)
            dk_ref[:, sl] += _dot(ds, q_ref[:, sl], _TN)
            dv_ref[:, sl] += _dot(p, do_ref[:, sl], _TN)

    row = pl.BlockSpec((bt, D), lambda i: (i, 0))
    mem = pl.BlockSpec((M, D), lambda i: (0, 0))
    return pl.pallas_call(
        body, grid=(T // bt,), in_specs=[row, mem, mem, row], out_specs=[row, mem, mem],
        out_shape=[jax.ShapeDtypeStruct((T, D), BF16), jax.ShapeDtypeStruct((M, D), F32),
                   jax.ShapeDtypeStruct((M, D), F32)],
        name="attn_bwd", compiler_params=_params(("arbitrary",)))(q, km, vm, do)


def _conv_apply(halo, x, w_ref, b_ref):
    bt, hr = x.shape[0], halo.shape[0]
    cat = jnp.concatenate([halo, x], axis=0)
    y = x * w_ref[3:4, :]
    for k in range(CONV_K - 1):
        y = y + pltpu.roll(cat, CONV_K - 1 - k, 0)[hr:hr + bt] * w_ref[k:k + 1, :]
    if b_ref is not None:
        y = y + b_ref[...]
    return y


def _l2_parts(act, bc):
    out = []
    for s in range(bc // 128):
        a = act[:, s * 128:(s + 1) * 128]
        r = lax.rsqrt(jnp.sum(a * a, axis=1, keepdims=True) + EPS)
        out.append((a, r))
    return out


def _conv_fwd(p, col0, C, w, b, *, l2, name, bt=512, bc=1024):
    T = p.shape[0]
    bt = min(bt, T)
    c0, hb = col0 // bc, bt // P_HALO
    has_b = b is not None
    assert not l2 or (bc == D and C == 2 * D)

    def body(x_ref, halo_ref, w_ref, *rest):
        b_ref = rest[0] if has_b else None
        o_ref = rest[-1]
        i, j = pl.program_id(0), pl.program_id(1)
        x = x_ref[...].astype(F32)
        halo = jnp.where(i > 0, halo_ref[...].astype(F32), 0.0)
        act = _silu(_conv_apply(halo, x, w_ref, b_ref))
        if l2:
            sc = jnp.where(j == 0, GDN_DK ** -0.5, 1.0)
            o_ref[...] = jnp.concatenate([a * (r * sc) for a, r in _l2_parts(act, bc)], axis=1)
        else:
            o_ref[...] = act

    in_specs = [pl.BlockSpec((bt, bc), lambda i, j: (i, c0 + j)),
                pl.BlockSpec((P_HALO, bc), lambda i, j: (jnp.maximum(i * hb - 1, 0), c0 + j)),
                pl.BlockSpec((CONV_K, bc), lambda i, j: (0, j))]
    ins = [p, p, w]
    if has_b:
        in_specs.append(pl.BlockSpec((1, bc), lambda i, j: (0, j)))
        ins.append(b.reshape(1, C))
    return pl.pallas_call(
        body, grid=(T // bt, C // bc), in_specs=in_specs,
        out_specs=pl.BlockSpec((bt, bc), lambda i, j: (i, j)),
        out_shape=jax.ShapeDtypeStruct((T, C), F32), name=name,
        compiler_params=_params(("parallel", "parallel")))(*ins)


def _conv_bwd_act(p, col0, C, w, b, dact, dcol0, *, l2, name, bt=512, bc=1024):
    T = p.shape[0]
    bt = min(bt, T)
    c0, d0, hb = col0 // bc, dcol0 // bc, bt // P_HALO
    has_b = b is not None
    assert not l2 or (bc == D and C == 2 * D)

    def body(x_ref, halo_ref, w_ref, *rest):
        b_ref = rest[0] if has_b else None
        dact_ref, dy_ref, dw_ref, db_ref = rest[-4:]
        j, i = pl.program_id(0), pl.program_id(1)
        x = x_ref[...].astype(F32)
        halo = jnp.where(i > 0, halo_ref[...].astype(F32), 0.0)
        y = _conv_apply(halo, x, w_ref, b_ref)
        dact = dact_ref[...]
        sg = _sigmoid(y)
        if l2:
            sc = jnp.where(j == 0, GDN_DK ** -0.5, 1.0)
            parts = []
            for s, (a, r) in enumerate(_l2_parts(y * sg, bc)):
                n = a * r
                dn = dact[:, s * 128:(s + 1) * 128]
                parts.append((r * sc) * (dn - n * jnp.sum(dn * n, axis=1, keepdims=True)))
            dact = jnp.concatenate(parts, axis=1)
        dy = dact * (sg * (1.0 + y * (1.0 - sg)))
        dy_ref[...] = dy

        @pl.when(i == 0)
        def _():
            dw_ref[...] = jnp.zeros_like(dw_ref)
            db_ref[...] = jnp.zeros_like(db_ref)

        db_ref[...] += jnp.sum(dy, axis=0, keepdims=True)
        cat = jnp.concatenate([halo, x], axis=0)
        dw_ref[3:4, :] += jnp.sum(dy * x, axis=0, keepdims=True)
        for k in range(CONV_K - 1):
            xs = pltpu.roll(cat, CONV_K - 1 - k, 0)[P_HALO:P_HALO + bt]
            dw_ref[k:k + 1, :] += jnp.sum(dy * xs, axis=0, keepdims=True)

    in_specs = [pl.BlockSpec((bt, bc), lambda j, i: (i, c0 + j)),
                pl.BlockSpec((P_HALO, bc), lambda j, i: (jnp.maximum(i * hb - 1, 0), c0 + j)),
                pl.BlockSpec((CONV_K, bc), lambda j, i: (0, j))]
    ins = [p, p, w]
    if has_b:
        in_specs.append(pl.BlockSpec((1, bc), lambda j, i: (0, j)))
        ins.append(b.reshape(1, C))
    in_specs.append(pl.BlockSpec((bt, bc), lambda j, i: (i, d0 + j)))
    ins.append(dact)
    dy, dw, db = pl.pallas_call(
        body, grid=(C // bc, T // bt), in_specs=in_specs,
        out_specs=[pl.BlockSpec((bt, bc), lambda j, i: (i, j)),
                   pl.BlockSpec((CONV_K, bc), lambda j, i: (0, j)),
                   pl.BlockSpec((1, bc), lambda j, i: (0, j))],
        out_shape=[jax.ShapeDtypeStruct((T, C), F32), jax.ShapeDtypeStruct((CONV_K, C), F32),
                   jax.ShapeDtypeStruct((1, C), F32)],
        name=name, compiler_params=_params(("parallel", "arbitrary")))(*ins)
    return dy, dw, db.reshape(C)


def _conv_bwd_in(dy, w, dp_in, col0, T, *, name, bt=512, bc=1024):
    C = dy.shape[1]
    bt = min(bt, T)
    c0, hb, nb = col0 // bc, bt // 8, T // bt

    def body(dy_ref, nxt_ref, w_ref, *rest):
        o_ref = rest[-1]
        i = pl.program_id(0)
        dy_v = dy_ref[...]
        nxt = jnp.where(i < nb - 1, nxt_ref[...], 0.0)
        cat = jnp.concatenate([dy_v, nxt], axis=0)
        dx = dy_v * w_ref[3:4, :]
        for k in range(CONV_K - 1):
            s = CONV_K - 1 - k
            dx = dx + pltpu.roll(cat, bt + 8 - s, 0)[0:bt] * w_ref[k:k + 1, :]
        o_ref[...] = dx.astype(o_ref.dtype)

    in_specs = [pl.BlockSpec((bt, bc), lambda i, j: (i, j)),
                pl.BlockSpec((8, bc), lambda i, j: (jnp.minimum((i + 1) * hb, T // 8 - 1), j)),
                pl.BlockSpec((CONV_K, bc), lambda i, j: (0, j))]
    ins = [dy, dy, w]
    alias = {}
    if dp_in is not None:
        in_specs.append(pl.BlockSpec(memory_space=pl.ANY))
        ins.append(dp_in)
        alias = {3: 0}
    return pl.pallas_call(
        body, grid=(nb, C // bc), in_specs=in_specs,
        out_specs=pl.BlockSpec((bt, bc), lambda i, j: (i, c0 + j)),
        out_shape=jax.ShapeDtypeStruct((T, C_TOT), BF16), input_output_aliases=alias, name=name,
        compiler_params=_params(("parallel", "parallel")))(*ins)


def _expand_mats(shift, row0):
    e = (_iota2((128, D), 0) - row0 == (_iota2((128, D), 1) >> shift)).astype(F32)
    et = ((_iota2((D, 128), 0) >> shift) == _iota2((D, 128), 1) - row0).astype(F32)
    return e, et


def _cum_mats(chunk):
    ri, ci = _iota2((chunk, chunk), 0), _iota2((chunk, chunk), 1)
    return (ri >= ci).astype(F32), (ri <= ci).astype(F32)


def _gdn_gates_fwd(p, alog_row, dtb_row, *, bt=256):
    T = p.shape[0]
    bt = min(bt, T)

    def body(g_ref, al_ref, db_ref, beta_ref, gam_ref):
        gt = g_ref[...]
        eb, _ = _expand_mats(7, 0)
        eg, _ = _expand_mats(7, GDN_H)
        lc, _ = _cum_mats(GDN_C)
        beta_l = _sigmoid(gt)
        g_l = -jnp.exp(al_ref[...]) * _softplus(gt + db_ref[...])
        beta_ref[...] = _dot_sel(beta_l, eb)
        gam_ref[...] = _chunk_cumsum(_dot_sel(g_l, eg), lc, GDN_C)

    vec = pl.BlockSpec((1, 128), lambda i: (0, 0))
    row = pl.BlockSpec((bt, D), lambda i: (i, 0))
    return pl.pallas_call(
        body, grid=(T // bt,),
        in_specs=[pl.BlockSpec((bt, 128), lambda i: (i, 0)), vec, vec],
        out_specs=[row, row],
        out_shape=[jax.ShapeDtypeStruct((T, D), F32)] * 2, name="gdn_gates_fwd",
        compiler_params=_params(("parallel",)))(p, alog_row, dtb_row)


def _gdn_gates_bwd(p, alog_row, dtb_row, dbeta_x, dgam_x, dp_in, *, bt=256):
    T = p.shape[0]
    bt = min(bt, T)

    def body(g_ref, al_ref, db_ref, dbeta_ref, dgam_ref, dpin_ref, dg_out, dal_ref, ddb_ref):
        i = pl.program_id(0)
        gt = g_ref[...]
        _, ebt = _expand_mats(7, 0)
        _, egt = _expand_mats(7, GDN_H)
        _, uc = _cum_mats(GDN_C)
        ea = jnp.exp(al_ref[...])
        zz = gt + db_ref[...]
        g_l = -ea * _softplus(zz)
        beta_l = _sigmoid(gt)
        dg_l = _dot_sel(_chunk_cumsum(dgam_ref[...], uc, GDN_C), egt)
        dbeta_l = _dot_sel(dbeta_ref[...], ebt)
        da = dg_l * (-ea) * _sigmoid(zz)
        dg_out[...] = (da + dbeta_l * beta_l * (1.0 - beta_l)).astype(dg_out.dtype)

        @pl.when(i == 0)
        def _():
            dal_ref[...] = jnp.zeros_like(dal_ref)
            ddb_ref[...] = jnp.zeros_like(ddb_ref)

        dal_ref[...] += jnp.sum(dg_l * g_l, axis=0, keepdims=True)
        ddb_ref[...] += jnp.sum(da, axis=0, keepdims=True)

    vec = pl.BlockSpec((1, 128), lambda i: (0, 0))
    row = pl.BlockSpec((bt, D), lambda i: (i, 0))
    gate = pl.BlockSpec((bt, 128), lambda i: (i, C_GATE // 128))
    return pl.pallas_call(
        body, grid=(T // bt,),
        in_specs=[pl.BlockSpec((bt, 128), lambda i: (i, 0)), vec, vec, row, row, _ANY],
        out_specs=[gate, vec, vec],
        out_shape=[jax.ShapeDtypeStruct((T, C_TOT), BF16), jax.ShapeDtypeStruct((1, 128), F32),
                   jax.ShapeDtypeStruct((1, 128), F32)],
        input_output_aliases={5: 0}, name="gdn_gates_bwd",
        compiler_params=_params(("arbitrary",)))(p, alog_row, dtb_row, dbeta_x, dgam_x, dp_in)


def _ssd_dt_fwd(p, dtb_row, alog_x, *, bt=256):
    T = p.shape[0]
    bt = min(bt, T)

    def body(d_ref, db_ref, al_ref, dt_ref, alpha_ref):
        ed, _ = _expand_mats(6, 0)
        lc, _ = _cum_mats(SSM_L)
        dt_x = _dot_sel(_softplus(d_ref[...] + db_ref[...]), ed)
        dt_ref[...] = dt_x
        alpha_ref[...] = _chunk_cumsum(dt_x * (-jnp.exp(al_ref[...])), lc, SSM_L)

    row = pl.BlockSpec((bt, D), lambda i: (i, 0))
    return pl.pallas_call(
        body, grid=(T // bt,),
        in_specs=[pl.BlockSpec((bt, 128), lambda i: (i, 1)),
                  pl.BlockSpec((1, 128), lambda i: (0, 0)), pl.BlockSpec((1, D), lambda i: (0, 0))],
        out_specs=[row, row], out_shape=[jax.ShapeDtypeStruct((T, D), F32)] * 2,
        name="ssd_dt_fwd", compiler_params=_params(("parallel",)))(p, dtb_row, alog_x)


def _ssd_dt_bwd(p, dtb_row, alog_x, ddt_x, dalpha_x, dp_in, *, bt=256):
    T = p.shape[0]
    bt = min(bt, T)

    def body(d_ref, db_ref, al_ref, ddt_ref, dal_ref, dpin_ref, dd_out, ddb_ref, dalog_ref):
        i = pl.program_id(0)
        ed, edt = _expand_mats(6, 0)
        _, uc = _cum_mats(SSM_L)
        zz = d_ref[...] + db_ref[...]
        dt_x = _dot_sel(_softplus(zz), ed)
        a_x = -jnp.exp(al_ref[...])
        da_x = _chunk_cumsum(dal_ref[...], uc, SSM_L)
        ddt_l = _dot_sel(ddt_ref[...] + da_x * a_x, edt)
        draw = ddt_l * _sigmoid(zz)
        dd_out[...] = draw.astype(dd_out.dtype)

        @pl.when(i == 0)
        def _():
            ddb_ref[...] = jnp.zeros_like(ddb_ref)
            dalog_ref[...] = jnp.zeros_like(dalog_ref)

        ddb_ref[...] += jnp.sum(draw, axis=0, keepdims=True)
        dalog_ref[...] += jnp.sum(da_x * dt_x, axis=0, keepdims=True) * a_x

    row = pl.BlockSpec((bt, D), lambda i: (i, 0))
    seg = pl.BlockSpec((bt, 128), lambda i: (i, C_DT // 128))
    v128 = pl.BlockSpec((1, 128), lambda i: (0, 0))
    vD = pl.BlockSpec((1, D), lambda i: (0, 0))
    return pl.pallas_call(
        body, grid=(T // bt,),
        in_specs=[pl.BlockSpec((bt, 128), lambda i: (i, 1)), v128, vD, row, row, _ANY],
        out_specs=[seg, v128, vD],
        out_shape=[jax.ShapeDtypeStruct((T, C_TOT), BF16), jax.ShapeDtypeStruct((1, 128), F32),
                   jax.ShapeDtypeStruct((1, D), F32)],
        input_output_aliases={5: 0}, name="ssd_dt_bwd",
        compiler_params=_params(("arbitrary",)))(p, dtb_row, alog_x, ddt_x, dalpha_x, dp_in)


def _gdn_post_fwd(o, p, w_x, *, bt=256):
    T = o.shape[0]
    bt = min(bt, T)

    def body(o_ref, z_ref, w_ref, out_ref):
        for h in range(GDN_H):
            sl = slice(h * 128, (h + 1) * 128)
            oh = o_ref[:, sl]
            r = lax.rsqrt(jnp.mean(oh * oh, axis=1, keepdims=True) + EPS)
            out_ref[:, sl] = (oh * r * w_ref[:, sl] * _silu(z_ref[:, sl].astype(F32))).astype(out_ref.dtype)

    row = pl.BlockSpec((bt, D), lambda i: (i, 0))
    return pl.pallas_call(
        body, grid=(T // bt,),
        in_specs=[row, pl.BlockSpec((bt, D), lambda i: (i, C_ZG // D)), pl.BlockSpec((1, D), lambda i: (0, 0))],
        out_specs=row, out_shape=jax.ShapeDtypeStruct((T, 2 * D), BF16), name="gdn_post_fwd",
        compiler_params=_params(("parallel",)))(o, p, w_x)


def _gdn_post_bwd(dmix, o, p, w_x, *, bt=256):
    T = o.shape[0]
    bt = min(bt, T)

    def body(dm_ref, o_ref, z_ref, w_ref, do_ref, dz_ref, dw_ref):
        i = pl.program_id(0)

        @pl.when(i == 0)
        def _():
            dw_ref[...] = jnp.zeros_like(dw_ref)

        for h in range(GDN_H):
            sl = slice(h * 128, (h + 1) * 128)
            oh, zh, wh, dm = o_ref[:, sl], z_ref[:, sl].astype(F32), w_ref[:, sl], dm_ref[:, sl]
            r = lax.rsqrt(jnp.mean(oh * oh, axis=1, keepdims=True) + EPS)
            ohat = oh * r
            dy = dm * _silu(zh)
            dz_ref[:, sl] = (dm * ohat * wh * _dsilu(zh)).astype(dz_ref.dtype)
            dohat = dy * wh
            do_ref[:, sl] = r * (dohat - ohat * jnp.mean(dohat * ohat, axis=1, keepdims=True))
            dw_ref[:, sl] += jnp.sum(dy * ohat, axis=0, keepdims=True)

    row = pl.BlockSpec((bt, D), lambda i: (i, 0))
    zcol = pl.BlockSpec((bt, D), lambda i: (i, C_ZG // D))
    vec = pl.BlockSpec((1, D), lambda i: (0, 0))
    return pl.pallas_call(
        body, grid=(T // bt,), in_specs=[row, row, zcol, vec], out_specs=[row, zcol, vec],
        out_shape=[jax.ShapeDtypeStruct((T, D), F32), jax.ShapeDtypeStruct((T, C_TOT), BF16),
                   jax.ShapeDtypeStruct((1, D), F32)],
        name="gdn_post_bwd", compiler_params=_params(("arbitrary",)))(dmix, o, p, w_x)


def _ssd_post_fwd(y, xs, p, d_x, w, mix_in, *, bt=256):
    T = y.shape[0]
    bt = min(bt, T)

    def body(y_ref, x_ref, z_ref, d_ref, w_ref, mix_ref, out_ref):
        yg = (y_ref[...] + x_ref[...] * d_ref[...]) * _silu(z_ref[...].astype(F32))
        for g in range(2):
            sl = slice(g * 512, (g + 1) * 512)
            a = yg[:, sl]
            r = lax.rsqrt(jnp.mean(a * a, axis=1, keepdims=True) + EPS)
            out_ref[:, sl] = (a * r * w_ref[:, sl]).astype(out_ref.dtype)

    row = pl.BlockSpec((bt, D), lambda i: (i, 0))
    vec = pl.BlockSpec((1, D), lambda i: (0, 0))
    return pl.pallas_call(
        body, grid=(T // bt,),
        in_specs=[row, row, pl.BlockSpec((bt, D), lambda i: (i, C_ZS // D)), vec, vec, _ANY],
        out_specs=pl.BlockSpec((bt, D), lambda i: (i, 1)), out_shape=jax.ShapeDtypeStruct((T, 2 * D), BF16),
        input_output_aliases={5: 0}, name="ssd_post_fwd",
        compiler_params=_params(("parallel",)))(y, xs, p, d_x, w, mix_in)


def _ssd_post_bwd(dmix, y, xs, p, d_x, w, dp_in, *, bt=256):
    T = y.shape[0]
    bt = min(bt, T)

    def body(dm_ref, y_ref, x_ref, z_ref, d_ref, w_ref, dpin_ref, dyy_ref, dz_ref, dd_ref, dw_ref):
        i = pl.program_id(0)

        @pl.when(i == 0)
        def _():
            dd_ref[...] = jnp.zeros_like(dd_ref)
            dw_ref[...] = jnp.zeros_like(dw_ref)

        xv, zv = x_ref[...], z_ref[...].astype(F32)
        yy = y_ref[...] + xv * d_ref[...]
        sz = _silu(zv)
        yg = yy * sz
        parts = []
        for g in range(2):
            sl = slice(g * 512, (g + 1) * 512)
            a = yg[:, sl]
            r = lax.rsqrt(jnp.mean(a * a, axis=1, keepdims=True) + EPS)
            ah = a * r
            dout = dm_ref[:, sl]
            dah = dout * w_ref[:, sl]
            dw_ref[:, sl] += jnp.sum(dout * ah, axis=0, keepdims=True)
            parts.append(r * (dah - ah * jnp.mean(dah * ah, axis=1, keepdims=True)))
        dyg = jnp.concatenate(parts, axis=1)
        dyy = dyg * sz
        dyy_ref[...] = dyy
        dz_ref[...] = (dyg * yy * _dsilu(zv)).astype(dz_ref.dtype)
        dd_ref[...] += jnp.sum(dyy * xv, axis=0, keepdims=True)

    row = pl.BlockSpec((bt, D), lambda i: (i, 0))
    zcol = pl.BlockSpec((bt, D), lambda i: (i, C_ZS // D))
    vec = pl.BlockSpec((1, D), lambda i: (0, 0))
    return pl.pallas_call(
        body, grid=(T // bt,),
        in_specs=[pl.BlockSpec((bt, D), lambda i: (i, 1)), row, row, zcol, vec, vec, _ANY],
        out_specs=[row, zcol, vec, vec],
        out_shape=[jax.ShapeDtypeStruct((T, D), F32), jax.ShapeDtypeStruct((T, C_TOT), BF16),
                   jax.ShapeDtypeStruct((1, D), F32), jax.ShapeDtypeStruct((1, D), F32)],
        input_output_aliases={6: 1}, name="ssd_post_bwd",
        compiler_params=_params(("arbitrary",)))(dmix, y, xs, p, d_x, w, dp_in)


_NEG = -1e30


def _gdn_terms(q, k, v, bx, gam_c):
    C = GDN_C
    ri, ci = _iota2((C, C), 0), _iota2((C, C), 1)
    eye, low, strict = ri == ci, ri >= ci, ri > ci
    gam_r = jnp.sum(jnp.where(eye, gam_c, 0.0), axis=0, keepdims=True)
    G = jnp.exp(jnp.where(low, gam_c - gam_r, _NEG))
    glast = jnp.sum(jnp.where(_iota2((C, 1), 0) == C - 1, gam_c, 0.0), axis=0, keepdims=True)
    eg, egl, eL = jnp.exp(gam_c), jnp.exp(glast - gam_c), jnp.exp(glast)
    kb, vb = k * bx, v * bx
    M = _dot(kb, k, _NT)
    return dict(eye=eye, low=low, strict=strict, G=G, eg=eg, egl=egl, eL=eL, kb=kb, vb=vb, M=M,
                kbg=kb * eg, qd=q * eg, kd=k * egl, q=q, k=k, v=v, bx=bx)


def _split(a):
    hi = a.astype(_MXU)
    return hi, (a - hi.astype(F32)).astype(_MXU)


def _dot3s(a, b):
    d = lambda p, q: lax.dot_general(p, q, _NN, preferred_element_type=F32)
    return d(a[0], b[0]) + d(a[0], b[1]) + d(a[1], b[0])


def _tri_inv_many(Ls, eye):
    eyef = jnp.where(eye, 1.0, 0.0)
    Ts = [eyef - L for L in Ls]
    Ps = [-L for L in Ls]
    for _ in range(5):
        sp = [_split(p) for p in Ps]
        Ps = [_dot3s(s, s) for s in sp]
        sp = [_split(p) for p in Ps]
        st = [_split(t) for t in Ts]
        Ts = [t + _dot3s(a, b) for t, a, b in zip(Ts, st, sp)]
    return Ts


def _gdn_heads(q_ref, k_ref, v_ref, bx_ref, gx_ref, heads):
    out = []
    for h in heads:
        sl = slice(h * 128, (h + 1) * 128)
        gam_c = jnp.max(gx_ref[:, sl], axis=1, keepdims=True)
        out.append(_gdn_terms(q_ref[:, sl], k_ref[:, sl], v_ref[:, sl], bx_ref[:, sl], gam_c))
    return out


def _gdn_prep(qk, v, bx, gx, ride=None):
    T = qk.shape[0]
    N = T // GDN_C
    C = GDN_C
    n_ride = ride.n if ride else 0

    def body(q_ref, k_ref, v_ref, bx_ref, gx_ref, *rest):
        ride_in = rest[:n_ride]
        u_ref, w_ref, qd_ref, kd_ref, p_ref, t_ref = rest[n_ride:n_ride + 6]
        ride_out = rest[n_ride + 6:2 * n_ride + 6]
        if ride:
            @pl.when(pl.program_id(0) == 0)
            def _():
                ride.start(ride_in, ride_out, rest[-3:])

            @pl.when(pl.program_id(0) == N - 1)
            def _():
                ride.wait(ride_in, ride_out, rest[-3:])

        for heads in _GDN_GROUPS:
            ts = _gdn_heads(q_ref, k_ref, v_ref, bx_ref, gx_ref, heads)
            Ts = _tri_inv_many([jnp.where(t["strict"], t["M"] * t["G"], 0.0) for t in ts], ts[0]["eye"])
            for h, t, Tm in zip(heads, ts, Ts):
                sl = slice(h * 128, (h + 1) * 128)
                rows = slice(h * C, (h + 1) * C)
                u_ref[:, sl] = _dot(Tm, t["vb"])
                w_ref[:, sl] = _dot(Tm, t["kbg"]).astype(w_ref.dtype)
                qd_ref[:, sl] = t["qd"].astype(qd_ref.dtype)
                kd_ref[:, sl] = t["kd"].astype(kd_ref.dtype)
                p_ref[0, rows, :] = _dot(t["q"], t["k"], _NT) * t["G"]
                t_ref[0, rows, :] = Tm

    blk = lambda c: pl.BlockSpec((C, D), lambda n: (n, c))
    sq = pl.BlockSpec((1, GDN_H * C, C), lambda n: (n, 0, 0))
    in_specs = [blk(0), blk(1), blk(0), blk(0), blk(0)]
    out_specs = [blk(0), blk(0), blk(0), blk(0), sq, sq]
    out_shape = [jax.ShapeDtypeStruct((T, D), F32), jax.ShapeDtypeStruct((T, D), BF16),
                 jax.ShapeDtypeStruct((T, D), BF16), jax.ShapeDtypeStruct((T, D), BF16),
                 jax.ShapeDtypeStruct((N, GDN_H * C, C), F32), jax.ShapeDtypeStruct((N, GDN_H * C, C), F32)]
    ins = [qk, qk, v, bx, gx]
    if ride:
        ins, in_specs = ins + ride.srcs, in_specs + ride.specs
        out_shape, out_specs = out_shape + ride.out_shape, out_specs + ride.specs
    res = pl.pallas_call(
        body, grid=(N,), in_specs=in_specs, out_specs=out_specs, out_shape=out_shape,
        scratch_shapes=ride.scratch if ride else [], name="gdn_prep",
        compiler_params=_params(("arbitrary",) if ride else ("parallel",)))(*ins)
    return (list(res[:6]), list(res[6:])) if ride else list(res)


def _gdn_scan_fwd(u, w, qd, kd, pm, gx):
    T = u.shape[0]
    N = T // GDN_C
    C, CS = GDN_C, GDN_SCAN_CHUNKS

    def body(u_ref, w_ref, qd_ref, kd_ref, p_ref, gx_ref, o_ref, vn_ref, ss_ref, S_scr):
        n = pl.program_id(0)

        @pl.when(n == 0)
        def _():
            S_scr[...] = jnp.zeros_like(S_scr)

        sls = [slice(h * 128, (h + 1) * 128) for h in range(GDN_H)]
        for c in range(CS):
            rows = slice(c * C, (c + 1) * C)
            Ss = [S_scr[:, sl] for sl in sls]
            vns = [u_ref[rows, sl] - _dot(w_ref[rows, sl], S) for sl, S in zip(sls, Ss)]
            for h, (sl, S, vn) in enumerate(zip(sls, Ss, vns)):
                ss_ref[c, :, sl] = S
                vn_ref[rows, sl] = vn.astype(vn_ref.dtype)
                o_ref[rows, sl] = _dot(qd_ref[rows, sl], S) + _dot(p_ref[c, h * C:(h + 1) * C, :], vn)
                S_scr[:, sl] = (S * jnp.exp(gx_ref[(c + 1) * C - 1:(c + 1) * C, sl])
                                + _dot(kd_ref[rows, sl], vn, _TN))

    blk = pl.BlockSpec((CS * C, D), lambda n: (n, 0))
    return pl.pallas_call(
        body, grid=(N // CS,),
        in_specs=[blk, blk, blk, blk, pl.BlockSpec((CS, GDN_H * C, C), lambda n: (n, 0, 0)), blk],
        out_specs=[blk, blk, pl.BlockSpec((CS, GDN_DK, D), lambda n: (n, 0, 0))],
        out_shape=[jax.ShapeDtypeStruct((T, D), F32), jax.ShapeDtypeStruct((T, D), BF16),
                   jax.ShapeDtypeStruct((N, GDN_DK, D), F32)],
        scratch_shapes=[pltpu.VMEM((GDN_DK, D), F32)], name="gdn_scan_fwd",
        compiler_params=_params(("arbitrary",)))(u, w, qd, kd, pm, gx)


def _gdn_scan_bwd(w, qd, kd, pm, gx, do):
    T = w.shape[0]
    N = T // GDN_C
    C, CS = GDN_C, GDN_SCAN_CHUNKS
    NB = N // CS

    def body(w_ref, qd_ref, kd_ref, p_ref, gx_ref, do_ref, dvn_ref, ds_ref, dS_scr):
        n = pl.program_id(0)

        @pl.when(n == 0)
        def _():
            dS_scr[...] = jnp.zeros_like(dS_scr)

        sls = [slice(h * 128, (h + 1) * 128) for h in range(GDN_H)]
        for c in reversed(range(CS)):
            rows = slice(c * C, (c + 1) * C)
            dSs = [dS_scr[:, sl] for sl in sls]
            dvns = [_dot(p_ref[c, h * C:(h + 1) * C, :], do_ref[rows, sl], _TN) + _dot(kd_ref[rows, sl], dS2)
                    for h, (sl, dS2) in enumerate(zip(sls, dSs))]
            for sl, dS2, dvn in zip(sls, dSs, dvns):
                ds_ref[c, :, sl] = dS2
                dvn_ref[rows, sl] = dvn.astype(dvn_ref.dtype)
                dS_scr[:, sl] = (dS2 * jnp.exp(gx_ref[(c + 1) * C - 1:(c + 1) * C, sl])
                                 + _dot(qd_ref[rows, sl], do_ref[rows, sl], _TN) - _dot(w_ref[rows, sl], dvn, _TN))

    blk = pl.BlockSpec((CS * C, D), lambda n: (NB - 1 - n, 0))
    return pl.pallas_call(
        body, grid=(NB,),
        in_specs=[blk, blk, blk, pl.BlockSpec((CS, GDN_H * C, C), lambda n: (NB - 1 - n, 0, 0)), blk, blk],
        out_specs=[blk, pl.BlockSpec((CS, GDN_DK, D), lambda n: (NB - 1 - n, 0, 0))],
        out_shape=[jax.ShapeDtypeStruct((T, D), BF16), jax.ShapeDtypeStruct((N, GDN_DK, D), F32)],
        scratch_shapes=[pltpu.VMEM((GDN_DK, D), F32)], name="gdn_scan_bwd",
        compiler_params=_params(("arbitrary",)))(w, qd, kd, pm, gx, do)


def _gdn_rest_bwd(qk, v, bx, gx, s_save, t_save, vn, dvn, ds_save, do, ride=None):
    T = qk.shape[0]
    N = T // GDN_C
    C = GDN_C
    n_ride = ride.n if ride else 0

    def body(q_ref, k_ref, v_ref, bx_ref, gx_ref, ss_ref, ts_ref, vn_ref, dvn_ref, ds_ref, do_ref, *rest):
        ride_in = rest[:n_ride]
        dqkv_ref, dbx_ref, dgx_ref = rest[n_ride:n_ride + 3]
        ride_out = rest[n_ride + 3:2 * n_ride + 3]
        if ride:
            @pl.when(pl.program_id(0) == 0)
            def _():
                ride.start(ride_in, ride_out, rest[-3:])

            @pl.when(pl.program_id(0) == N - 1)
            def _():
                ride.wait(ride_in, ride_out, rest[-3:])

        for heads in _GDN_GROUPS:
            group(heads, q_ref, k_ref, v_ref, bx_ref, gx_ref, ss_ref, ts_ref, vn_ref, dvn_ref, ds_ref, do_ref,
                  dqkv_ref, dbx_ref, dgx_ref)

    def group(H, q_ref, k_ref, v_ref, bx_ref, gx_ref, ss_ref, ts_ref, vn_ref, dvn_ref, ds_ref, do_ref,
              dqkv_ref, dbx_ref, dgx_ref):
        sls = [slice(h * 128, (h + 1) * 128) for h in H]
        ts = _gdn_heads(q_ref, k_ref, v_ref, bx_ref, gx_ref, H)
        Ss = [ss_ref[0, :, sl] for sl in sls]
        Tms = [ts_ref[0, h * C:(h + 1) * C, :] for h in H]
        dS2s = [ds_ref[0, :, sl] for sl in sls]
        dos = [do_ref[:, sl] for sl in sls]
        vns = [vn_ref[:, sl] for sl in sls]
        dvns = [dvn_ref[:, sl] for sl in sls]
        Qs = [_dot(t["q"], t["k"], _NT) for t in ts]
        dws = [-_dot(dvn, S, _NT) for dvn, S in zip(dvns, Ss)]
        dqds = [_dot(do, S, _NT) for do, S in zip(dos, Ss)]
        dPs = [jnp.where(t["low"], _dot(do, vn, _NT), 0.0) for t, do, vn in zip(ts, dos, vns)]
        dkds = [_dot(vn, dS2, _NT) for vn, dS2 in zip(vns, dS2s)]
        dTs = [_dot(dvn, t["vb"], _NT) + _dot(dw, t["kbg"], _NT) for t, dvn, dw in zip(ts, dvns, dws)]
        dvbs = [_dot(Tm, dvn, _TN) for Tm, dvn in zip(Tms, dvns)]
        dkbgs = [_dot(Tm, dw, _TN) for Tm, dw in zip(Tms, dws)]
        TdTs = [_dot(Tm, dT, _TN) for Tm, dT in zip(Tms, dTs)]
        dLs = [jnp.where(t["strict"], -_dot(TdT, Tm, _NT), 0.0) for t, TdT, Tm in zip(ts, TdTs, Tms)]
        dMs = [dL * t["G"] for t, dL in zip(ts, dLs)]
        dQs = [dP * t["G"] for t, dP in zip(ts, dPs)]
        dkbs = [_dot(dM, t["k"]) + dkbg * t["eg"] for t, dM, dkbg in zip(ts, dMs, dkbgs)]
        rs = lambda a: jnp.sum(a, axis=1, keepdims=True)
        lane0 = _iota2((C, 128), 1) == 0
        last = _iota2((C, 1), 0) == C - 1
        for i, h in enumerate(H):
            t, sl = ts[i], sls[i]
            E = (dLs[i] * t["M"] + dPs[i] * Qs[i]) * t["G"]
            dqkv_ref[:, sl] = _dot(dQs[i], t["k"]) + dqds[i] * t["eg"]
            dqkv_ref[:, D + h * 128:D + (h + 1) * 128] = (
                _dot(dQs[i], t["q"], _TN) + _dot(dMs[i], t["kb"], _TN) + dkds[i] * t["egl"] + dkbs[i] * t["bx"])
            dqkv_ref[:, 2 * D + h * 128:2 * D + (h + 1) * 128] = dvbs[i] * t["bx"]
            dbx_ref[:, sl] = dkbs[i] * t["k"] + dvbs[i] * t["v"]
            dkd_kd = dkds[i] * t["kd"]
            dgam_c = rs(dqds[i] * t["qd"]) + rs(dkbgs[i] * t["kbg"]) - rs(dkd_kd) + rs(E)
            dgam_r = -jnp.sum(E, axis=0, keepdims=True)
            dgam_c = dgam_c + jnp.sum(jnp.where(t["eye"], dgam_r, 0.0), axis=1, keepdims=True)
            dlast = _sum_all(dkd_kd) + t["eL"] * _sum_all(Ss[i] * dS2s[i])
            dgx_ref[:, sl] = jnp.where(lane0, dgam_c + jnp.where(last, dlast, 0.0), 0.0)

    blk = lambda c: pl.BlockSpec((C, D), lambda n: (n, c))
    st = pl.BlockSpec((1, GDN_DK, D), lambda n: (n, 0, 0))
    in_specs = [blk(0), blk(1), blk(0), blk(0), blk(0), st,
                pl.BlockSpec((1, GDN_H * C, C), lambda n: (n, 0, 0)), blk(0), blk(0), st, blk(0)]
    out_specs = [pl.BlockSpec((C, 3 * D), lambda n: (n, 0)), blk(0), blk(0)]
    out_shape = [jax.ShapeDtypeStruct((T, 3 * D), F32), jax.ShapeDtypeStruct((T, D), F32),
                 jax.ShapeDtypeStruct((T, D), F32)]
    ins = [qk, qk, v, bx, gx, s_save, t_save, vn, dvn, ds_save, do]
    if ride:
        ins, in_specs = ins + ride.srcs, in_specs + ride.specs
        out_shape, out_specs = out_shape + ride.out_shape, out_specs + ride.specs
    res = pl.pallas_call(
        body, grid=(N,), in_specs=in_specs, out_specs=out_specs, out_shape=out_shape,
        scratch_shapes=ride.scratch if ride else [], name="gdn_rest_bwd",
        compiler_params=_params(("arbitrary",) if ride else ("parallel",)))(*ins)
    return (list(res[:3]), list(res[3:])) if ride else list(res)


def _ssd_seg(al_pair, half, s):
    L = SSM_L
    ri, ci = _iota2((L, L), 0), _iota2((L, L), 1)
    ac = jnp.max(jnp.where(half == s, al_pair, _NEG), axis=1, keepdims=True)
    ar = jnp.sum(jnp.where(ri == ci, ac, 0.0), axis=0, keepdims=True)
    return jnp.exp(jnp.where(ri >= ci, ac - ar, _NEG))


def _last_row(a):
    return jnp.sum(jnp.where(_iota2((a.shape[0], 1), 0) == a.shape[0] - 1, a, 0.0), axis=0, keepdims=True)


def _ssd_core_fwd(xbc, dtx, alx):
    T = xbc.shape[0]
    L, CS = SSM_L, SSM_SCAN_CHUNKS
    Nc = T // L

    def body(x_all, bc_all, dt_all, al_all, y_all, hs_all, H_scr):
        @pl.when(pl.program_id(0) == 0)
        def _():
            H_scr[...] = jnp.zeros_like(H_scr)

        for cc in range(CS):
            rows = pl.ds(cc * L, L)
            chunk(x_all.at[rows], bc_all.at[rows], dt_all.at[rows], al_all.at[rows], y_all.at[rows], hs_all.at[cc],
                  H_scr)

    def chunk(x_ref, bc_ref, dt_ref, al_ref, y_ref, hs_ref, H_scr):
        half = _iota2((L, 128), 1) >> 6
        for g in range(2):
            gs = slice(g * 512, (g + 1) * 512)
            Bg = bc_ref[:, g * 128:(g + 1) * 128]
            Cg = bc_ref[:, 256 + g * 128:256 + (g + 1) * 128]
            alg = al_ref[:, gs]
            alast = _last_row(alg)
            xdt = x_ref[:, gs] * dt_ref[:, gs]
            Hg = H_scr[:, gs]
            hs_ref[:, gs] = Hg
            CB = _dot(Cg, Bg, _NT)
            y_ref[:, gs] = jnp.exp(alg) * _dot(Cg, Hg)
            H_scr[:, gs] = Hg * jnp.exp(alast) + _dot(Bg, jnp.exp(alast - alg) * xdt, _TN)
            for j in range(4):
                ps = slice(g * 512 + j * 128, g * 512 + (j + 1) * 128)
                al_pair = al_ref[:, ps]
                xp = x_ref[:, ps] * dt_ref[:, ps]
                ys = [_dot(_ssd_seg(al_pair, half, s) * CB, xp) for s in range(2)]
                y_ref[:, ps] += jnp.where(half == 0, ys[0], ys[1])

    row = pl.BlockSpec((CS * L, D), lambda c: (c, 0))
    return pl.pallas_call(
        body, grid=(Nc // CS,), in_specs=[row, pl.BlockSpec((CS * L, 512), lambda c: (c, 2)), row, row],
        out_specs=[row, pl.BlockSpec((CS, SSM_N, D), lambda c: (c, 0, 0))],
        out_shape=[jax.ShapeDtypeStruct((T, D), F32), jax.ShapeDtypeStruct((Nc, SSM_N, D), F32)],
        scratch_shapes=[pltpu.VMEM((SSM_N, D), F32)], name="ssd_core_fwd",
        compiler_params=_params(("arbitrary",)))(xbc, xbc, dtx, alx)


def _ssd_core_bwd(xbc, dtx, alx, h_save, dyy, d_x):
    T = xbc.shape[0]
    L, CS = SSM_L, SSM_SCAN_CHUNKS
    Nc = T // L
    NB = Nc // CS

    def body(x_all, bc_all, dt_all, al_all, hs_all, dy_all, d_ref, dx_all, ddt_all, dal_all, dH_scr):
        @pl.when(pl.program_id(0) == 0)
        def _():
            dH_scr[...] = jnp.zeros_like(dH_scr)

        for cc in reversed(range(CS)):
            rows = pl.ds(cc * L, L)
            chunk(x_all.at[rows], bc_all.at[rows], dt_all.at[rows], al_all.at[rows], hs_all.at[cc], dy_all.at[rows],
                  d_ref, dx_all.at[rows], ddt_all.at[rows], dal_all.at[rows], dH_scr)

    def chunk(x_ref, bc_ref, dt_ref, al_ref, hs_ref, dy_ref, d_ref, dx_ref, ddt_ref, dal_ref, dH_scr):
        lane = _iota2((L, 128), 1)
        half = lane >> 6
        rowi = _iota2((L, 1), 0)
        ri, ci = _iota2((L, L), 0), _iota2((L, L), 1)
        for g in range(2):
            gs = slice(g * 512, (g + 1) * 512)
            Bg = bc_ref[:, g * 128:(g + 1) * 128]
            Cg = bc_ref[:, 256 + g * 128:256 + (g + 1) * 128]
            alg = al_ref[:, gs]
            alast = _last_row(alg)
            eal, edec, eL = jnp.exp(alg), jnp.exp(alast - alg), jnp.exp(alast)
            xg, dtg, dYg = x_ref[:, gs], dt_ref[:, gs], dy_ref[:, gs]
            xdt = xg * dtg
            Hg = hs_ref[:, gs]
            dH2 = dH_scr[:, gs]
            CB = _dot(Cg, Bg, _NT)
            dYe = eal * dYg
            dH_scr[:, gs] = dH2 * eL + _dot(Cg, dYe, _TN)
            dC = _dot(dYe, Hg, _NT)
            zg = edec * xdt
            dz = _dot(Bg, dH2)
            dB = _dot(zg, dH2, _NT)
            tz = dz * zg
            dal = dYe * _dot(Cg, Hg) - tz
            dalast = jnp.sum(tz, axis=0, keepdims=True) + eL * jnp.sum(Hg * dH2, axis=0, keepdims=True)
            dal = dal + jnp.where(rowi == L - 1, dalast, 0.0)
            dxdt_g = edec * dz
            dx_ref[:, gs] = dxdt_g * dtg + dYg * d_ref[:, gs]
            ddt_ref[:, gs] = dxdt_g * xg
            dal_ref[:, gs] = dal
            dCB = jnp.zeros((L, L), F32)
            for j in range(4):
                ps = slice(g * 512 + j * 128, g * 512 + (j + 1) * 128)
                al_pair = al_ref[:, ps]
                xp = x_ref[:, ps] * dt_ref[:, ps]
                dYp = dy_ref[:, ps]
                dxp = []
                dal_p = jnp.zeros((L, 128), F32)
                for s in range(2):
                    seg = _ssd_seg(al_pair, half, s)
                    W = seg * CB
                    dW = _dot(jnp.where(half == s, dYp, 0.0), xp, _NT)
                    dxp.append(_dot(W, dYp, _TN))
                    dCB = dCB + dW * seg
                    Es = dW * W
                    dac = jnp.sum(Es, axis=1, keepdims=True) - jnp.sum(
                        jnp.where(ri == ci, jnp.sum(Es, axis=0, keepdims=True), 0.0), axis=1, keepdims=True)
                    dal_p = dal_p + jnp.where(lane == 64 * s, dac, 0.0)
                dxdt_p = jnp.where(half == 0, dxp[0], dxp[1])
                dx_ref[:, ps] += dxdt_p * dt_ref[:, ps]
                ddt_ref[:, ps] += dxdt_p * x_ref[:, ps]
                dal_ref[:, ps] += dal_p
            dx_ref[:, D + g * 128:D + (g + 1) * 128] = dB + _dot(dCB, Cg, _TN)
            dx_ref[:, D + 256 + g * 128:D + 256 + (g + 1) * 128] = dC + _dot(dCB, Bg)

    row = pl.BlockSpec((CS * L, D), lambda c: (NB - 1 - c, 0))
    bcs = pl.BlockSpec((CS * L, 512), lambda c: (NB - 1 - c, 2))
    return pl.pallas_call(
        body, grid=(NB,),
        in_specs=[row, bcs, row, row, pl.BlockSpec((CS, SSM_N, D), lambda c: (NB - 1 - c, 0, 0)), row,
                  pl.BlockSpec((1, D), lambda c: (0, 0))],
        out_specs=[pl.BlockSpec((CS * L, D + 512), lambda c: (NB - 1 - c, 0)), row, row],
        out_shape=[jax.ShapeDtypeStruct((T, D + 512), F32),
                   jax.ShapeDtypeStruct((T, D), F32), jax.ShapeDtypeStruct((T, D), F32)],
        scratch_shapes=[pltpu.VMEM((SSM_N, D), F32)], name="ssd_core_bwd",
        compiler_params=_params(("arbitrary",)))(xbc, xbc, dtx, alx, h_save, dyy, d_x)


_EARLY = ("w_out", "wq_mem", "wk_mem", "wv_mem", "wo_mem")
_LATE = ("w_up", "w_down")
_GRADS_MLP = ("w_down", "w_up")
_GRADS_MID = ("wo_mem", "wq_mem", "wk_mem", "wv_mem", "w_out")


def _gather_ride(shards, names):
    return None if shards is None else _Ride([shards[n] for n in names], shard=True)


def _grad_ride(shards, G, names):
    return None if shards is None else _Ride([_slots_from_full(n, G[n]) for n in names], shard=False)


def _local_step(x, mem, tgt, W, shards=None):
    T = x.shape[0]
    W = dict(W)
    cw_qk, cw_v = W["gdn_conv_w"][:, :2 * D], W["gdn_conv_w"][:, 2 * D:]
    h1 = _rmsnorm_fwd(x, W["norm1_w"], name="norm1_fwd")
    ride = _gather_ride(shards, _EARLY)
    pg = _mm(h1, W["w_in_pad"][:, C_GATE:], name="in_proj_gates")
    p = _mm(h1, W["w_in_pad"][:, :C_GATE], out_dtype=BF16, bn_cap=1664, name="in_proj", ride=ride)
    if ride:
        p, got = p
        W.update({n: _full_from_slots(n, g) for n, g in zip(_EARLY, got)})
    qk = _conv_fwd(p, C_QKV, 2 * D, cw_qk, None, l2=True, name="gdn_conv_qk_fwd")
    v_g = _conv_fwd(p, C_QKV + 2 * D, D, cw_v, None, l2=False, name="gdn_conv_v_fwd")
    bx, gx = _gdn_gates_fwd(pg, W["gdn_alog_row"], W["gdn_dtb_row"])
    ride = _gather_ride(shards, _LATE)
    prep = _gdn_prep(qk, v_g, bx, gx, ride)
    if ride:
        prep, got = prep
        W.update({n: _full_from_slots(n, g) for n, g in zip(_LATE, got)})
    u_g, w_g, qd_g, kd_g, p_g, t_save = prep
    o_g, vn_g, s_save = _gdn_scan_fwd(u_g, w_g, qd_g, kd_g, p_g, gx)
    mix = _gdn_post_fwd(o_g, p, W["gdn_norm_x"])
    xbc = _conv_fwd(p, C_XBC, D + 512, W["ssm_conv_w"], W["ssm_conv_b"], l2=False, name="ssm_conv_fwd", bc=512)
    dtx, alx = _ssd_dt_fwd(pg, W["ssm_dtb_row"], W["ssm_alog_x"])
    y_s, h_save = _ssd_core_fwd(xbc, dtx, alx)
    mix = _ssd_post_fwd(y_s, xbc, p, W["ssm_d_x"], W["ssm_norm_w"].reshape(1, D), mix)
    x1, h2 = _mm(mix, W["w_out"], epi="res_norm", extra=(x, W["norm2_w"]), bm=512, name="out_proj")
    qm = _mm(h2, W["wq_mem"], out_dtype=BF16, name="q_proj")
    m = _rmsnorm_fwd(mem, W["mem_norm_w"], name="mem_norm_fwd")
    km = _mm(m, W["wk_mem"], name="k_proj")
    vm = _mm(m, W["wv_mem"], name="v_proj")
    oa = _attn_fwd(qm, km, vm)
    x2, h3 = _mm(oa, W["wo_mem"], epi="res_norm", extra=(x1, W["norm3_w"]), bm=512, name="o_proj")
    u, act = _mm(h3, W["w_up"], epi="relu2", out_dtype=BF16, name="mlp_up")
    dx3, g_final, loss = _mm(act, W["w_down"], epi="res_loss", extra=(x2, tgt, W["final_norm_w"]), bk_cap=1024,
                             name="mlp_down_loss")
    G = {"final_norm_w": g_final.reshape(D)}
    dpre = _mm(dx3, W["w_down"], dims="nt", epi="mul2", extra=u, out_dtype=BF16, name="mlp_down_dx")
    G["w_down"] = _mm(act, dx3, dims="tn", out_dtype=BF16, name="mlp_down_dw")
    G["w_up"] = _mm(h3, dpre, dims="tn", out_dtype=BF16, name="mlp_up_dw")
    dx2, gw = _mm(dpre, W["w_up"], dims="nt", epi="norm_bwd", extra=(x2, dx3, W["norm3_w"]), bk_cap=1024,
                  name="mlp_up_dx")
    G["norm3_w"] = gw.reshape(D)
    do_a = _mm(dx2, W["wo_mem"], dims="nt", out_dtype=BF16, name="o_proj_dx")
    G["wo_mem"] = _mm(oa, dx2, dims="tn", out_dtype=BF16, name="o_proj_dw")
    dq, dk, dv = _attn_bwd(qm, km, vm, do_a)
    G["wq_mem"] = _mm(h2, dq, dims="tn", out_dtype=BF16, name="q_proj_dw")
    dx1, gw = _mm(dq, W["wq_mem"], dims="nt", epi="norm_bwd", extra=(x1, dx2, W["norm2_w"]), bm=512,
                  name="q_proj_dx")
    G["norm2_w"] = gw.reshape(D)
    G["wk_mem"] = _mm(m, dk, dims="tn", out_dtype=BF16, name="k_proj_dw")
    G["wv_mem"] = _mm(m, dv, dims="tn", out_dtype=BF16, name="v_proj_dw")
    dm = _mm(dk, W["wk_mem"], dims="nt", name="k_proj_dx")
    dm = _mm(dv, W["wv_mem"], dims="nt", epi="res", extra=dm, name="v_proj_dx")
    _, G["mem_norm_w"] = _rmsnorm_bwd(mem, W["mem_norm_w"], dm, None, name="mem_norm_bwd")
    dmix = _mm(dx1, W["w_out"], dims="nt", name="out_proj_dx")
    G["w_out"] = _mm(mix, dx1, dims="tn", out_dtype=BF16, name="out_proj_dw")
    do_g, dp, G["gdn_norm_x"] = _gdn_post_bwd(dmix, o_g, p, W["gdn_norm_x"])
    dvn_g, ds_save = _gdn_scan_bwd(w_g, qd_g, kd_g, p_g, gx, do_g)
    ride = _grad_ride(shards, G, _GRADS_MLP)
    rest = _gdn_rest_bwd(qk, v_g, bx, gx, s_save, t_save, vn_g, dvn_g, ds_save, do_g, ride)
    if ride:
        rest, got = rest
        G.update(zip(_GRADS_MLP, got))
    dqkvn, dbx, dgx = rest
    dy_qk, gcw_qk, _ = _conv_bwd_act(p, C_QKV, 2 * D, cw_qk, None, dqkvn, 0, l2=True, name="gdn_conv_qk_bwd_act")
    dy_v, gcw_v, _ = _conv_bwd_act(p, C_QKV + 2 * D, D, cw_v, None, dqkvn, 2 * D, l2=False,
                                   name="gdn_conv_v_bwd_act")
    G["gdn_conv_w"] = jnp.concatenate([gcw_qk, gcw_v], axis=1)
    dp = _conv_bwd_in(dy_qk, cw_qk, dp, C_QKV, T, name="gdn_conv_qk_bwd_in")
    dp = _conv_bwd_in(dy_v, cw_v, dp, C_QKV + 2 * D, T, name="gdn_conv_v_bwd_in")
    dp, G["gdn_alog_row"], G["gdn_dtb_row"] = _gdn_gates_bwd(pg, W["gdn_alog_row"], W["gdn_dtb_row"], dbx, dgx, dp)
    dyy, dp, G["ssm_d_x"], G["ssm_norm_w"] = _ssd_post_bwd(dmix, y_s, xbc, p, W["ssm_d_x"],
                                                          W["ssm_norm_w"].reshape(1, D), dp)
    dxbc, ddtx, dalx = _ssd_core_bwd(xbc, dtx, alx, h_save, dyy, W["ssm_d_x"])
    dy_s, G["ssm_conv_w"], G["ssm_conv_b"] = _conv_bwd_act(p, C_XBC, D + 512, W["ssm_conv_w"], W["ssm_conv_b"],
                                                           dxbc, 0, l2=False, name="ssm_conv_bwd_act", bc=512)
    dp = _conv_bwd_in(dy_s, W["ssm_conv_w"], dp, C_XBC, T, name="ssm_conv_bwd_in", bc=512)
    dp, G["ssm_dtb_row"], G["ssm_alog_x"] = _ssd_dt_bwd(pg, W["ssm_dtb_row"], W["ssm_alog_x"], ddtx, dalx, dp)
    ride = _grad_ride(shards, G, _GRADS_MID)
    g_in = _mm(h1, dp, dims="tn", out_dtype=BF16, bn_cap=1152, name="in_proj_dw", ride=ride)
    if ride:
        g_in, got = g_in
        G.update(zip(_GRADS_MID, got))
    G["w_in"] = _unpad_w_in(g_in)
    ride = _grad_ride(shards, G, ("w_in",))
    res = _mm(dp, W["w_in_pad"], dims="nt", epi="norm_bwd", extra=(x, dx1, W["norm1_w"]),
              name="in_proj_dx", ride=ride)
    if ride:
        res, got = res
        G["w_in"] = got[0]
    dx, gw = res
    G["norm1_w"] = gw.reshape(D)
    return loss, dx, G


def _all_gather(shards, out_dtype, *, name):
    n = len(shards)

    def body(*refs):
        x_refs, out_refs, stage = refs[:n], refs[n:2 * n], refs[2 * n:3 * n]
        send_sems, recv_sems, local_sems = refs[3 * n:]
        x, y, c = _place()
        me, sibling = (x, y, c), (x, y, 1 - c)
        chips = [(1 - x, y), (x, 1 - y), (1 - x, 1 - y)]

        def slot(px, py, pc):
            return 4 * px + 2 * py + pc

        def copy(a, k, block, to, src=None):
            dst = out_refs[a].at[slot(*block)]
            return pltpu.make_async_remote_copy(
                src_ref=dst if src is None else src, dst_ref=dst, send_sem=send_sems.at[a, k],
                recv_sem=recv_sems.at[a, k], device_id=to, device_id_type=_MESH)

        for a in range(n):
            stage[a][...] = x_refs[a][...].astype(out_dtype)
        mine = [pltpu.make_async_copy(stage[a], out_refs[a].at[slot(*me)], local_sems.at[a]) for a in range(n)]
        for cp in mine:
            cp.start()
        first = []
        for a in range(n):
            first.append(copy(a, 0, me, sibling, src=stage[a]))
            first += [copy(a, 1 + j, me, (*chip, c), src=stage[a]) for j, chip in enumerate(chips)]
        for cp in first:
            cp.start()
        passed = [[copy(a, 4 + j, (*chip, c), sibling) for j, chip in enumerate(chips)] for a in range(n)]
        for j, chip in enumerate(chips):
            for a in range(n):
                copy(a, 1 + j, (*chip, c), me).wait_recv()
                passed[a][j].start()
        for a in range(n):
            copy(a, 0, sibling, me).wait_recv()
            for j, chip in enumerate(chips):
                copy(a, 4 + j, (*chip, 1 - c), me).wait_recv()
        for cp in first + [cp for row in passed for cp in row]:
            cp.wait_send()
        for cp in mine:
            cp.wait()

    outs = pl.pallas_call(
        body, in_specs=[_VM] * n, out_specs=[_ANY] * n,
        out_shape=[jax.ShapeDtypeStruct((N_DEV,) + s.shape, out_dtype) for s in shards],
        scratch_shapes=[pltpu.VMEM(s.shape, out_dtype) for s in shards]
        + [pltpu.SemaphoreType.DMA((n, 7)), pltpu.SemaphoreType.DMA((n, 7)), pltpu.SemaphoreType.DMA((n,))],
        name=name, compiler_params=pltpu.CompilerParams(vmem_limit_bytes=VMEM_LIMIT))(*shards)
    return list(outs)


def _cast_bf16(arrs, *, name):
    n = len(arrs)

    def body(*refs):
        for a in range(n):
            refs[n + a][...] = refs[a][...].astype(BF16)

    return list(pl.pallas_call(
        body, in_specs=[_VM] * n, out_specs=[_VM] * n,
        out_shape=[jax.ShapeDtypeStruct(s.shape, BF16) for s in arrs], name=name,
        compiler_params=pltpu.CompilerParams(vmem_limit_bytes=VMEM_LIMIT))(*arrs))


def _sum8(a, *, name):
    _, R, Cc = a.shape
    br = _pick_rows(R, 128)

    def body(a_ref, o_ref):
        s = a_ref[0].astype(F32)
        for k in range(1, N_DEV):
            s = s + a_ref[k].astype(F32)
        o_ref[...] = s

    return pl.pallas_call(
        body, grid=(R // br,), in_specs=[pl.BlockSpec((N_DEV, br, Cc), lambda i: (0, i, 0))],
        out_specs=pl.BlockSpec((br, Cc), lambda i: (i, 0)), out_shape=jax.ShapeDtypeStruct((R, Cc), F32),
        name=name, compiler_params=_params(("parallel",)))(a)


def _pick_rows(R, cap):
    if R <= cap:
        return R
    for d in range(cap, 7, -8):
        if R % d == 0:
            return d
    return R


def _adamw(w, g, m, v, *, name):
    shape = w.shape
    as2d = (lambda t: t.reshape(1, -1)) if w.ndim == 1 else (lambda t: t)
    w2, g2, m2, v2 = as2d(w), as2d(g), as2d(m), as2d(v)
    R, Cc = w2.shape
    br = _pick_rows(R, 256)
    c1 = 1.0 - ADAM_B1 ** ADAM_STEP
    c2 = 1.0 - ADAM_B2 ** ADAM_STEP

    def body(w_ref, g_ref, m_ref, v_ref, d_ref, nm_ref, nv_ref):
        gv = g_ref[...]
        nm = ADAM_B1 * m_ref[...] + (1.0 - ADAM_B1) * gv
        nv = ADAM_B2 * v_ref[...] + (1.0 - ADAM_B2) * (gv * gv)
        nm_ref[...] = nm
        nv_ref[...] = nv
        d_ref[...] = -ADAM_LR * ((nm / c1) / (jnp.sqrt(nv / c2) + ADAM_EPS) + ADAM_WD * w_ref[...])

    blk = pl.BlockSpec((br, Cc), lambda i: (i, 0))
    outs = pl.pallas_call(
        body, grid=(R // br,), in_specs=[blk] * 4, out_specs=[blk] * 3,
        out_shape=[jax.ShapeDtypeStruct((R, Cc), F32)] * 3, name=name,
        compiler_params=_params(("parallel",)))(w2, g2, m2, v2)
    return tuple(o.reshape(shape) for o in outs)


_BIG = ("w_in", "w_out", "wq_mem", "wk_mem", "wv_mem", "wo_mem", "w_up", "w_down")
_COL_SHARDED = ("w_in", "w_up")
_WEIGHTS = ("norm1_w", "w_in", "gdn_conv_w", "gdn_a_log", "gdn_dt_bias", "gdn_norm_w", "ssm_conv_w", "ssm_conv_b",
            "ssm_a_log", "ssm_dt_bias", "ssm_d", "ssm_norm_w", "w_out", "norm2_w", "mem_norm_w", "wq_mem", "wk_mem",
            "wv_mem", "wo_mem", "norm3_w", "w_up", "w_down", "final_norm_w")
_IN_PAD = 112


def _full_from_slots(name, g):
    if name in _COL_SHARDED:
        return jnp.transpose(g, (1, 0, 2)).reshape(g.shape[1], N_DEV * g.shape[2])
    return g.reshape(N_DEV * g.shape[1], g.shape[2])


def _slots_from_full(name, f):
    if name in _COL_SHARDED:
        return jnp.transpose(f.reshape(f.shape[0], N_DEV, f.shape[1] // N_DEV), (1, 0, 2))
    return f.reshape(N_DEV, f.shape[0] // N_DEV, f.shape[1])


def _pad_w_in(w):
    z = jnp.zeros((w.shape[0], _IN_PAD), w.dtype)
    return jnp.concatenate([w[:, :4096], w[:, 4112:6672], w[:, 4096:4112], z, w[:, 6672:6688], z], axis=1)


def _unpad_w_in(gp):
    return jnp.concatenate([gp[:, :4096], gp[:, C_GATE:C_GATE + 16], gp[:, 4096:C_GATE], gp[:, C_DT:C_DT + 16]],
                           axis=1)


def _pack_rows(vals):
    rows, offs, r = [], [], 0
    for vflat in vals:
        nrow = 8 * -(-vflat.shape[0] // 1024)
        rows.append(jnp.pad(vflat, (0, nrow * 128 - vflat.shape[0])).reshape(nrow, 128))
        offs.append((r, vflat.shape[0]))
        r += nrow
    return jnp.concatenate(rows, axis=0), offs


def _unpack_rows(packed, offs, shapes):
    out = []
    for (r, nel), shp in zip(offs, shapes):
        nrow = -(-nel // 128)
        out.append(packed[r:r + nrow].reshape(-1)[:nel].reshape(shp))
    return out


def kernel(x, mem, norm1_w, w_in, gdn_conv_w, gdn_a_log, gdn_dt_bias, gdn_norm_w, ssm_conv_w, ssm_conv_b, ssm_a_log, ssm_dt_bias, ssm_d, ssm_norm_w, w_out, norm2_w, mem_norm_w, wq_mem, wk_mem, wv_mem, wo_mem, norm3_w, w_up, w_down, final_norm_w, loss_target, m_norm1_w, m_w_in, m_gdn_conv_w, m_gdn_a_log, m_gdn_dt_bias, m_gdn_norm_w, m_ssm_conv_w, m_ssm_conv_b, m_ssm_a_log, m_ssm_dt_bias, m_ssm_d, m_ssm_norm_w, m_w_out, m_norm2_w, m_mem_norm_w, m_wq_mem, m_wk_mem, m_wv_mem, m_wo_mem, m_norm3_w, m_w_up, m_w_down, m_final_norm_w, v_norm1_w, v_w_in, v_gdn_conv_w, v_gdn_a_log, v_gdn_dt_bias, v_gdn_norm_w, v_ssm_conv_w, v_ssm_conv_b, v_ssm_a_log, v_ssm_dt_bias, v_ssm_d, v_ssm_norm_w, v_w_out, v_norm2_w, v_mem_norm_w, v_wq_mem, v_wk_mem, v_wv_mem, v_wo_mem, v_norm3_w, v_w_up, v_w_down, v_final_norm_w):
    args = dict(locals())
    w_loc = {n: args[n] for n in _WEIGHTS}
    me = 4 * lax.axis_index("x") + 2 * lax.axis_index("y") + lax.axis_index("c")

    w_in_full = _full_from_slots("w_in", _all_gather([w_in], BF16, name="gather_w_in")[0])
    later = _EARLY + _LATE
    shards = dict(zip(later, _cast_bf16([w_loc[n] for n in later], name="cast_shards")))
    conv_pack, conv_offs = _pack_rows([gdn_conv_w.reshape(-1), ssm_conv_w.reshape(-1)])
    conv_all = _all_gather([conv_pack], F32, name="gather_conv")[0]
    gdn_cw, ssm_cw = [], []
    for k in range(N_DEV):
        a, b = _unpack_rows(conv_all[k], conv_offs, [gdn_conv_w.shape, ssm_conv_w.shape])
        gdn_cw.append(a)
        ssm_cw.append(b)
    W = {
        "w_in_pad": _pad_w_in(w_in_full),
        "norm1_w": norm1_w, "norm2_w": norm2_w, "norm3_w": norm3_w, "mem_norm_w": mem_norm_w,
        "final_norm_w": final_norm_w, "ssm_norm_w": ssm_norm_w, "ssm_conv_b": ssm_conv_b,
        "gdn_conv_w": jnp.concatenate(gdn_cw, axis=1), "ssm_conv_w": jnp.concatenate(ssm_cw, axis=1),
        "gdn_alog_row": jnp.pad(gdn_a_log, (GDN_H, 128 - 2 * GDN_H)).reshape(1, 128),
        "gdn_dtb_row": jnp.pad(gdn_dt_bias, (GDN_H, 128 - 2 * GDN_H)).reshape(1, 128),
        "gdn_norm_x": jnp.tile(gdn_norm_w, GDN_H).reshape(1, D),
        "ssm_dtb_row": jnp.pad(ssm_dt_bias, (0, 128 - SSM_H)).reshape(1, 128),
        "ssm_alog_x": jnp.repeat(ssm_a_log, SSM_P).reshape(1, D),
        "ssm_d_x": jnp.repeat(ssm_d, SSM_P).reshape(1, D),
    }

    loss_part, grad_x, G = _local_step(x[0], mem[0], loss_target[0], W, shards)

    grads = {n: _sum8(G[n], name="sum_" + n) for n in _BIG}

    small = {
        "norm1_w": G["norm1_w"], "gdn_conv_w": G["gdn_conv_w"], "gdn_a_log": G["gdn_alog_row"][0, GDN_H:2 * GDN_H],
        "gdn_dt_bias": G["gdn_dtb_row"][0, GDN_H:2 * GDN_H], "gdn_norm_w": G["gdn_norm_x"].reshape(GDN_H, 128).sum(0),
        "ssm_conv_w": G["ssm_conv_w"], "ssm_conv_b": G["ssm_conv_b"],
        "ssm_a_log": G["ssm_alog_x"].reshape(SSM_H, SSM_P).sum(1), "ssm_dt_bias": G["ssm_dtb_row"][0, :SSM_H],
        "ssm_d": G["ssm_d_x"].reshape(SSM_H, SSM_P).sum(1), "ssm_norm_w": G["ssm_norm_w"].reshape(D),
        "norm2_w": G["norm2_w"], "mem_norm_w": G["mem_norm_w"], "norm3_w": G["norm3_w"],
        "final_norm_w": G["final_norm_w"], "loss": loss_part[0, :1],
    }
    names = list(small)
    pack, offs = _pack_rows([small[n].reshape(-1) for n in names])
    tot = _sum8(_all_gather([pack], F32, name="gather_small")[0], name="sum_small")
    summed = dict(zip(names, _unpack_rows(tot, offs, [small[n].shape for n in names])))
    loss = summed.pop("loss")[0]
    for n in ("gdn_conv_w", "ssm_conv_w"):
        width = w_loc[n].shape[1]
        summed[n] = lax.dynamic_slice_in_dim(summed[n], me * width, width, axis=1)
    grads.update(summed)

    upd = {n: _adamw(w_loc[n], grads[n], args["m_" + n], args["v_" + n], name="adamw_" + n) for n in _WEIGHTS}
    return (loss, grad_x[None], *[grads[n] for n in _WEIGHTS], *[upd[n][0] for n in _WEIGHTS],
            *[upd[n][1] for n in _WEIGHTS], *[upd[n][2] for n in _WEIGHTS])
```

```python
import functools
import math

import jax
import jax.numpy as jnp
from jax import lax
from jax.experimental import pallas as pl
from jax.experimental.pallas import tpu as pltpu

F32 = jnp.float32
BF16 = jnp.bfloat16
_MXU = BF16

D = 1024
EPS = 1e-6
CONV_K = 4
GDN_H, GDN_DK, GDN_C = 8, 128, 64
GDN_SCAN_CHUNKS = 4
GDN_LOCAL_CHUNKS = 4
GDN_REST_CHUNKS = 4
SSM_H, SSM_P, SSM_L, SSM_N = 16, 64, 128, 128
SSM_SCAN_CHUNKS = 2
MEM_H, MEM_HD = 4, 256
D_FF = 4096
N_DEV = 8

C_QKV, C_ZG, C_ZS, C_XBC, C_GATE, C_DT, C_TOT = 0, 3072, 4096, 5120, 6656, 6784, 6912
P_HALO = 16

ADAM_LR, ADAM_B1, ADAM_B2, ADAM_EPS, ADAM_WD, ADAM_STEP = 0.001, 0.9, 0.999, 1e-08, 0.01, 10

VMEM_LIMIT = 56 * 1024 * 1024

_NN = (((1,), (0,)), ((), ()))
_NT = (((1,), (1,)), ((), ()))
_TN = (((0,), (0,)), ((), ()))


def _dot(a, b, dims=_NN):
    return lax.dot_general(a.astype(_MXU), b.astype(_MXU), dims, preferred_element_type=F32)


def _split3(a):
    a1 = a.astype(BF16)
    r1 = a - a1.astype(F32)
    a2 = r1.astype(BF16)
    return a1, a2, (r1 - a2.astype(F32)).astype(BF16)


def _dot_sel(a, e):
    eb = e.astype(BF16)
    return sum(lax.dot_general(p, eb, _NN, preferred_element_type=F32) for p in _split3(a))


def _sel_dot(e, a):
    eb = e.astype(BF16)
    return sum(lax.dot_general(eb, p, _NN, preferred_element_type=F32) for p in _split3(a))


def _chunk_cumsum(a, tri, chunk):
    return jnp.concatenate([_sel_dot(tri, a[r:r + chunk]) for r in range(0, a.shape[0], chunk)], axis=0)


def _params(sem):
    return pltpu.CompilerParams(dimension_semantics=sem, vmem_limit_bytes=VMEM_LIMIT)


def _pick(n, cap):
    for d in range(min(cap, n), 0, -128):
        if n % d == 0 and d % 128 == 0:
            return d
    return n


def _sigmoid(x):
    return 0.5 * jnp.tanh(0.5 * x) + 0.5


def _silu(x):
    return x * _sigmoid(x)


def _dsilu(x):
    s = _sigmoid(x)
    return s * (1.0 + x * (1.0 - s))


def _softplus(x):
    return jnp.maximum(x, 0.0) + jnp.log(1.0 + jnp.exp(-jnp.abs(x)))


def _iota2(shape, axis):
    return lax.broadcasted_iota(jnp.int32, shape, axis)


def _sum_all(x):
    return jnp.sum(jnp.sum(x, axis=1, keepdims=True), axis=0, keepdims=True)


_MESH = pl.DeviceIdType.MESH
_ANY = pl.BlockSpec(memory_space=pl.ANY)
_VM = pl.BlockSpec(memory_space=pltpu.VMEM)
_REL = [(r >> 2 & 1, r >> 1 & 1, r & 1) for r in range(1, N_DEV)]


def _place():
    return lax.axis_index("x"), lax.axis_index("y"), lax.axis_index("c")


class _Ride:
    def __init__(self, srcs, shard):
        self.srcs, self.shard, self.n = list(srcs), shard, len(srcs)
        self.out_shape = [jax.ShapeDtypeStruct(((N_DEV,) + s.shape) if shard else s.shape, s.dtype)
                          for s in self.srcs]
        self.specs = [_ANY] * self.n
        self.scratch = [pltpu.SemaphoreType.DMA((self.n, N_DEV - 1)), pltpu.SemaphoreType.DMA((self.n, N_DEV - 1)),
                        pltpu.SemaphoreType.DMA((self.n,))]

    def _copies(self, in_refs, out_refs, sems):
        send, recv, loc = sems
        x, y, c = _place()
        me = 4 * x + 2 * y + c
        local, remote, arrive = [], [], []
        for a in range(self.n):
            src = in_refs[a] if self.shard else in_refs[a].at[me]
            local.append(pltpu.make_async_copy(src, out_refs[a].at[me], loc.at[a]))
        for k, (rx, ry, rc) in enumerate(_REL):
            peer = (lax.rem(x + rx, 2), lax.rem(y + ry, 2), lax.rem(c + rc, 2))
            ps = 4 * peer[0] + 2 * peer[1] + peer[2]
            for a in range(self.n):
                src = in_refs[a] if self.shard else in_refs[a].at[ps]
                remote.append(pltpu.make_async_remote_copy(
                    src_ref=src, dst_ref=out_refs[a].at[me], send_sem=send.at[a, k], recv_sem=recv.at[a, k],
                    device_id=peer, device_id_type=_MESH))
                slot = out_refs[a].at[ps]
                arrive.append(pltpu.make_async_remote_copy(
                    src_ref=slot, dst_ref=slot, send_sem=send.at[a, k], recv_sem=recv.at[a, k],
                    device_id=peer, device_id_type=_MESH))
        return local, remote, arrive

    def start(self, in_refs, out_refs, sems):
        local, remote, _ = self._copies(in_refs, out_refs, sems)
        for cp in local + remote:
            cp.start()

    def wait(self, in_refs, out_refs, sems):
        local, remote, arrive = self._copies(in_refs, out_refs, sems)
        for cp in arrive:
            cp.wait_recv()
        for cp in remote:
            cp.wait_send()
        for cp in local:
            cp.wait()


_EPI = {
    "none": ((), ("tile",)),
    "res": (("tile",), ("tile",)),
    "mul2": (("tile",), ("tile",)),
    "relu2": ((), ("tile", "tile")),
    "res_norm": (("tile", "row"), ("tile", "tile")),
    "norm_bwd": (("tile", "tile", "row"), ("tile", "row")),
    "res_loss": (("tile", "tile", "row"), ("tile", "row", "row")),
}


def _mm(a, b, *, dims="nn", epi="none", extra=(), out_dtype=F32, name, bm=1024, bn_cap=1024, bk_cap=2048,
        ride=None):
    if dims == "nn":
        (M, K), (K2, N) = a.shape, b.shape
    elif dims == "nt":
        (M, K), (N, K2) = a.shape, b.shape
    else:
        (K, M), (K2, N) = a.shape, b.shape
    assert K == K2, (a.shape, b.shape, dims)
    bm = _pick(M, bm)
    bn = _pick(N, bn_cap)
    bk = _pick(K, bk_cap)
    nk = K // bk
    dn = {"nn": _NN, "nt": _NT, "tn": _TN}[dims]
    a_spec = (pl.BlockSpec((bk, bm), lambda i, j, k: (k, i)) if dims == "tn"
              else pl.BlockSpec((bm, bk), lambda i, j, k: (i, k)))
    b_spec = (pl.BlockSpec((bn, bk), lambda i, j, k: (j, k)) if dims == "nt"
              else pl.BlockSpec((bk, bn), lambda i, j, k: (k, j)))
    o_spec = pl.BlockSpec((bm, bn), lambda i, j, k: (i, j))
    r_spec = pl.BlockSpec((1, bn), lambda i, j, k: (0, j))
    extra = list(extra) if isinstance(extra, (tuple, list)) else [extra]
    ekinds, okinds = _EPI[epi]
    assert len(extra) == len(ekinds) and (epi not in ("res_norm", "norm_bwd", "res_loss") or bn == N)
    n_extra, n_out = len(ekinds), len(okinds)
    n_ride = ride.n if ride else 0
    gi, gj = M // bm, N // bn

    def body(a_ref, b_ref, *rest):
        ex = rest[:n_extra]
        first = pl.program_id(0) == 0
        ride_in = rest[n_extra:n_extra + n_ride]
        outs = rest[n_extra + n_ride:n_extra + n_ride + n_out]
        ride_out = rest[n_extra + n_ride + n_out:n_extra + 2 * n_ride + n_out]
        if ride:
            at = lambda i, j, k: ((pl.program_id(0) == i) & (pl.program_id(1) == j) & (pl.program_id(2) == k))

            @pl.when(at(0, 0, 0))
            def _():
                ride.start(ride_in, ride_out, rest[-3:])

        def finish(r):
            if epi == "res":
                outs[0][...] = (r + ex[0][...].astype(F32)).astype(outs[0].dtype)
            elif epi == "mul2":
                outs[0][...] = (2.0 * r * ex[0][...].astype(F32)).astype(outs[0].dtype)
            elif epi == "relu2":
                u = jnp.maximum(r, 0.0)
                outs[0][...] = u.astype(outs[0].dtype)
                outs[1][...] = (u * u).astype(outs[1].dtype)
            elif epi == "res_norm":
                y = r + ex[0][...]
                outs[0][...] = y
                rstd = lax.rsqrt(jnp.mean(y * y, axis=1, keepdims=True) + EPS)
                outs[1][...] = (y * rstd * ex[1][...]).astype(outs[1].dtype)
            elif epi == "norm_bwd":
                xv = ex[0][...]
                rstd = lax.rsqrt(jnp.mean(xv * xv, axis=1, keepdims=True) + EPS)
                xh = xv * rstd
                dxh = r * ex[2][...]
                outs[0][...] = ex[1][...] + rstd * (dxh - xh * jnp.mean(dxh * xh, axis=1, keepdims=True))
                dw = jnp.sum(r * xh, axis=0, keepdims=True)

                @pl.when(first)
                def _():
                    outs[1][...] = dw

                @pl.when(jnp.logical_not(first))
                def _():
                    outs[1][...] += dw
            elif epi == "res_loss":
                y = r + ex[0][...]
                wv = ex[2][...]
                rstd = lax.rsqrt(jnp.mean(y * y, axis=1, keepdims=True) + EPS)
                yh = y * rstd
                err = yh * wv - ex[1][...]
                part_loss = 0.5 * jnp.sum(jnp.mean(err * err, axis=1, keepdims=True), axis=0, keepdims=True)
                dyn = err * (1.0 / N)
                dyh = dyn * wv
                outs[0][...] = rstd * (dyh - yh * jnp.mean(dyh * yh, axis=1, keepdims=True))
                dw = jnp.sum(dyn * yh, axis=0, keepdims=True)
                lrow = jnp.broadcast_to(part_loss, (1, N))

                @pl.when(first)
                def _():
                    outs[1][...] = dw
                    outs[2][...] = lrow

                @pl.when(jnp.logical_not(first))
                def _():
                    outs[1][...] += dw
                    outs[2][...] += lrow
            else:
                outs[0][...] = r.astype(outs[0].dtype)

        part = _dot(a_ref[...], b_ref[...], dn)
        if nk == 1:
            finish(part)
        else:
            acc = rest[n_extra + 2 * n_ride + n_out]
            k = pl.program_id(2)

            @pl.when(k == 0)
            def _():
                acc[...] = part

            @pl.when((k > 0) & (k < nk - 1))
            def _():
                acc[...] += part

            @pl.when(k == nk - 1)
            def _():
                finish(acc[...] + part)

        if ride:
            @pl.when(at(gi - 1, gj - 1, nk - 1))
            def _():
                ride.wait(ride_in, ride_out, rest[-3:])

    kind_spec = {"tile": o_spec, "row": r_spec}
    ins = [a, b] + [e.reshape(1, N) if k == "row" else e for e, k in zip(extra, ekinds)]
    in_specs = [a_spec, b_spec] + [kind_spec[k] for k in ekinds]
    out_dtypes = {"res_norm": (F32, BF16), "norm_bwd": (F32, F32), "res_loss": (F32, F32, F32)}.get(
        epi, (out_dtype,) * n_out)
    out_shape = [jax.ShapeDtypeStruct((M, N) if k == "tile" else (1, N), dt) for k, dt in zip(okinds, out_dtypes)]
    out_specs = [kind_spec[k] for k in okinds]
    scratch = [pltpu.VMEM((bm, bn), F32)] if nk > 1 else []
    sem = ("arbitrary" if epi in ("norm_bwd", "res_loss") else "parallel", "parallel", "arbitrary")
    if ride:
        ins, in_specs = ins + ride.srcs, in_specs + ride.specs
        out_shape, out_specs = out_shape + ride.out_shape, out_specs + ride.specs
        scratch, sem = scratch + ride.scratch, ("arbitrary",) * 3
    res = pl.pallas_call(
        body, grid=(gi, gj, nk), in_specs=in_specs, out_specs=out_specs, out_shape=out_shape,
        scratch_shapes=scratch, name=name, compiler_params=_params(sem))(*ins)
    main = res[:n_out] if n_out > 1 else res[0]
    return (main, list(res[n_out:])) if ride else main


def _rmsnorm_fwd(x, w, *, name, bt=256):
    T, Dm = x.shape
    bt = min(bt, T)

    def body(x_ref, w_ref, h_ref):
        xv = x_ref[...]
        r = lax.rsqrt(jnp.mean(xv * xv, axis=1, keepdims=True) + EPS)
        h_ref[...] = (xv * r * w_ref[...]).astype(h_ref.dtype)

    return pl.pallas_call(
        body, grid=(T // bt,),
        in_specs=[pl.BlockSpec((bt, Dm), lambda i: (i, 0)), pl.BlockSpec((1, Dm), lambda i: (0, 0))],
        out_specs=pl.BlockSpec((bt, Dm), lambda i: (i, 0)),
        out_shape=jax.ShapeDtypeStruct((T, Dm), BF16), name=name,
        compiler_params=_params(("parallel",)))(x, w.reshape(1, Dm))


def _rmsnorm_bwd(x, w, dh, dres, *, name, bt=256):
    T, Dm = x.shape
    bt = min(bt, T)
    has_res = dres is not None

    def body(x_ref, w_ref, dh_ref, *rest):
        dres_ref = rest[0] if has_res else None
        dx_ref, dw_ref = rest[-2], rest[-1]
        i = pl.program_id(0)
        xv = x_ref[...]
        r = lax.rsqrt(jnp.mean(xv * xv, axis=1, keepdims=True) + EPS)
        xh = xv * r
        dhv = dh_ref[...].astype(F32)
        dxh = dhv * w_ref[...]
        dx = r * (dxh - xh * jnp.mean(dxh * xh, axis=1, keepdims=True))
        if has_res:
            dx = dx + dres_ref[...]
        dx_ref[...] = dx

        @pl.when(i == 0)
        def _():
            dw_ref[...] = jnp.zeros_like(dw_ref)

        dw_ref[...] += jnp.sum(dhv * xh, axis=0, keepdims=True)

    row = pl.BlockSpec((bt, Dm), lambda i: (i, 0))
    vec = pl.BlockSpec((1, Dm), lambda i: (0, 0))
    ins = [x, w.reshape(1, Dm), dh] + ([dres] if has_res else [])
    dx, dw = pl.pallas_call(
        body, grid=(T // bt,), in_specs=[row, vec, row] + ([row] if has_res else []),
        out_specs=[row, vec],
        out_shape=[jax.ShapeDtypeStruct((T, Dm), F32), jax.ShapeDtypeStruct((1, Dm), F32)],
        name=name, compiler_params=_params(("arbitrary",)))(*ins)
    return dx, dw.reshape(Dm)


def _attn_fwd(q, km, vm, *, bt=256):
    T = q.shape[0]
    M = km.shape[0]
    bt = min(bt, T)
    scale = MEM_HD ** -0.5

    def body(q_ref, k_ref, v_ref, o_ref):
        for h in range(MEM_H):
            sl = slice(h * MEM_HD, (h + 1) * MEM_HD)
            s = _dot(q_ref[:, sl], k_ref[:, sl], _NT) * scale
            s = s - jnp.max(s, axis=1, keepdims=True)
            e = jnp.exp(s)
            p = e / jnp.sum(e, axis=1, keepdims=True)
            o_ref[:, sl] = _dot(p, v_ref[:, sl]).astype(o_ref.dtype)

    row = pl.BlockSpec((bt, D), lambda i: (i, 0))
    mem = pl.BlockSpec((M, D), lambda i: (0, 0))
    return pl.pallas_call(
        body, grid=(T // bt,), in_specs=[row, mem, mem], out_specs=row,
        out_shape=jax.ShapeDtypeStruct((T, D), BF16), name="attn_fwd",
        compiler_params=_params(("parallel",)))(q, km, vm)


def _attn_bwd(q, km, vm, do, *, bt=256):
    T = q.shape[0]
    M = km.shape[0]
    bt = min(bt, T)
    scale = MEM_HD ** -0.5

    def body(q_ref, k_ref, v_ref, do_ref, dq_ref, dk_ref, dv_ref):
        i = pl.program_id(0)

        @pl.when(i == 0)
        def _():
            dk_ref[...] = jnp.zeros_like(dk_ref)
            dv_ref[...] = jnp.zeros_like(dv_ref)

        sls = [slice(h * MEM_HD, (h + 1) * MEM_HD) for h in range(MEM_H)]
        ss = [_dot(q_ref[:, sl], k_ref[:, sl], _NT) * scale for sl in sls]
        dps = [_dot(do_ref[:, sl], v_ref[:, sl], _NT) for sl in sls]
        es = [jnp.exp(s - jnp.max(s, axis=1, keepdims=True)) for s in ss]
        ps = [e / jnp.sum(e, axis=1, keepdims=True) for e in es]
        dss = [p * (dp - jnp.sum(dp * p, axis=1, keepdims=True)) * scale for p, dp in zip(ps, dps)]
        for sl, p, ds in zip(sls, ps, dss):
            dq_ref[:, sl] = _dot(ds, k_ref[:, sl]).astype(dq_ref.dtype)
            dk_ref[:, sl] += _dot(ds, q_ref[:, sl], _TN)
            dv_ref[:, sl] += _dot(p, do_ref[:, sl], _TN)

    row = pl.BlockSpec((bt, D), lambda i: (i, 0))
    mem = pl.BlockSpec((M, D), lambda i: (0, 0))
    return pl.pallas_call(
        body, grid=(T // bt,), in_specs=[row, mem, mem, row], out_specs=[row, mem, mem],
        out_shape=[jax.ShapeDtypeStruct((T, D), BF16), jax.ShapeDtypeStruct((M, D), F32),
                   jax.ShapeDtypeStruct((M, D), F32)],
        name="attn_bwd", compiler_params=_params(("arbitrary",)))(q, km, vm, do)


def _conv_apply(halo, x, w_ref, b_ref):
    bt, hr = x.shape[0], halo.shape[0]
    cat = jnp.concatenate([halo, x], axis=0)
    y = x * w_ref[3:4, :]
    for k in range(CONV_K - 1):
        y = y + pltpu.roll(cat, CONV_K - 1 - k, 0)[hr:hr + bt] * w_ref[k:k + 1, :]
    if b_ref is not None:
        y = y + b_ref[...]
    return y


def _l2_parts(act, bc):
    out = []
    for s in range(bc // 128):
        a = act[:, s * 128:(s + 1) * 128]
        r = lax.rsqrt(jnp.sum(a * a, axis=1, keepdims=True) + EPS)
        out.append((a, r))
    return out


def _conv_fwd(p, col0, C, w, b, *, l2, name, bt=512, bc=1024):
    T = p.shape[0]
    bt = min(bt, T)
    c0, hb = col0 // bc, bt // P_HALO
    has_b = b is not None
    assert not l2 or (bc == D and C == 2 * D)

    def body(x_ref, halo_ref, w_ref, *rest):
        b_ref = rest[0] if has_b else None
        o_ref = rest[-1]
        i, j = pl.program_id(0), pl.program_id(1)
        x = x_ref[...].astype(F32)
        halo = jnp.where(i > 0, halo_ref[...].astype(F32), 0.0)
        act = _silu(_conv_apply(halo, x, w_ref, b_ref))
        if l2:
            sc = jnp.where(j == 0, GDN_DK ** -0.5, 1.0)
            o_ref[...] = jnp.concatenate([a * (r * sc) for a, r in _l2_parts(act, bc)], axis=1)
        else:
            o_ref[...] = act

    in_specs = [pl.BlockSpec((bt, bc), lambda i, j: (i, c0 + j)),
                pl.BlockSpec((P_HALO, bc), lambda i, j: (jnp.maximum(i * hb - 1, 0), c0 + j)),
                pl.BlockSpec((CONV_K, bc), lambda i, j: (0, j))]
    ins = [p, p, w]
    if has_b:
        in_specs.append(pl.BlockSpec((1, bc), lambda i, j: (0, j)))
        ins.append(b.reshape(1, C))
    return pl.pallas_call(
        body, grid=(T // bt, C // bc), in_specs=in_specs,
        out_specs=pl.BlockSpec((bt, bc), lambda i, j: (i, j)),
        out_shape=jax.ShapeDtypeStruct((T, C), F32), name=name,
        compiler_params=_params(("parallel", "parallel")))(*ins)


def _conv_bwd_act(p, col0, C, w, b, dact, dcol0, *, l2, name, bt=512, bc=1024):
    T = p.shape[0]
    bt = min(bt, T)
    c0, d0, hb = col0 // bc, dcol0 // bc, bt // P_HALO
    has_b = b is not None
    assert not l2 or (bc == D and C == 2 * D)

    def body(x_ref, halo_ref, w_ref, *rest):
        b_ref = rest[0] if has_b else None
        dact_ref, dy_ref, dw_ref, db_ref = rest[-4:]
        j, i = pl.program_id(0), pl.program_id(1)
        x = x_ref[...].astype(F32)
        halo = jnp.where(i > 0, halo_ref[...].astype(F32), 0.0)
        y = _conv_apply(halo, x, w_ref, b_ref)
        dact = dact_ref[...]
        sg = _sigmoid(y)
        if l2:
            sc = jnp.where(j == 0, GDN_DK ** -0.5, 1.0)
            parts = []
            for s, (a, r) in enumerate(_l2_parts(y * sg, bc)):
                n = a * r
                dn = dact[:, s * 128:(s + 1) * 128]
                parts.append((r * sc) * (dn - n * jnp.sum(dn * n, axis=1, keepdims=True)))
            dact = jnp.concatenate(parts, axis=1)
        dy = dact * (sg * (1.0 + y * (1.0 - sg)))
        dy_ref[...] = dy

        @pl.when(i == 0)
        def _():
            dw_ref[...] = jnp.zeros_like(dw_ref)
            db_ref[...] = jnp.zeros_like(db_ref)

        db_ref[...] += jnp.sum(dy, axis=0, keepdims=True)
        cat = jnp.concatenate([halo, x], axis=0)
        dw_ref[3:4, :] += jnp.sum(dy * x, axis=0, keepdims=True)
        for k in range(CONV_K - 1):
            xs = pltpu.roll(cat, CONV_K - 1 - k, 0)[P_HALO:P_HALO + bt]
            dw_ref[k:k + 1, :] += jnp.sum(dy * xs, axis=0, keepdims=True)

    in_specs = [pl.BlockSpec((bt, bc), lambda j, i: (i, c0 + j)),
                pl.BlockSpec((P_HALO, bc), lambda j, i: (jnp.maximum(i * hb - 1, 0), c0 + j)),
                pl.BlockSpec((CONV_K, bc), lambda j, i: (0, j))]
    ins = [p, p, w]
    if has_b:
        in_specs.append(pl.BlockSpec((1, bc), lambda j, i: (0, j)))
        ins.append(b.reshape(1, C))
    in_specs.append(pl.BlockSpec((bt, bc), lambda j, i: (i, d0 + j)))
    ins.append(dact)
    dy, dw, db = pl.pallas_call(
        body, grid=(C // bc, T // bt), in_specs=in_specs,
        out_specs=[pl.BlockSpec((bt, bc), lambda j, i: (i, j)),
                   pl.BlockSpec((CONV_K, bc), lambda j, i: (0, j)),
                   pl.BlockSpec((1, bc), lambda j, i: (0, j))],
        out_shape=[jax.ShapeDtypeStruct((T, C), F32), jax.ShapeDtypeStruct((CONV_K, C), F32),
                   jax.ShapeDtypeStruct((1, C), F32)],
        name=name, compiler_params=_params(("parallel", "arbitrary")))(*ins)
    return dy, dw, db.reshape(C)


def _conv_bwd_in(dy, w, dp_in, col0, T, *, name, bt=512, bc=1024):
    C = dy.shape[1]
    bt = min(bt, T)
    c0, hb, nb = col0 // bc, bt // 8, T // bt

    def body(dy_ref, nxt_ref, w_ref, *rest):
        o_ref = rest[-1]
        i = pl.program_id(0)
        dy_v = dy_ref[...]
        nxt = jnp.where(i < nb - 1, nxt_ref[...], 0.0)
        cat = jnp.concatenate([dy_v, nxt], axis=0)
        dx = dy_v * w_ref[3:4, :]
        for k in range(CONV_K - 1):
            s = CONV_K - 1 - k
            dx = dx + pltpu.roll(cat, bt + 8 - s, 0)[0:bt] * w_ref[k:k + 1, :]
        o_ref[...] = dx.astype(o_ref.dtype)

    in_specs = [pl.BlockSpec((bt, bc), lambda i, j: (i, j)),
                pl.BlockSpec((8, bc), lambda i, j: (jnp.minimum((i + 1) * hb, T // 8 - 1), j)),
                pl.BlockSpec((CONV_K, bc), lambda i, j: (0, j))]
    ins = [dy, dy, w]
    alias = {}
    if dp_in is not None:
        in_specs.append(pl.BlockSpec(memory_space=pl.ANY))
        ins.append(dp_in)
        alias = {3: 0}
    return pl.pallas_call(
        body, grid=(nb, C // bc), in_specs=in_specs,
        out_specs=pl.BlockSpec((bt, bc), lambda i, j: (i, c0 + j)),
        out_shape=jax.ShapeDtypeStruct((T, C_TOT), BF16), input_output_aliases=alias, name=name,
        compiler_params=_params(("parallel", "parallel")))(*ins)


def _expand_mats(shift, row0):
    e = (_iota2((128, D), 0) - row0 == (_iota2((128, D), 1) >> shift)).astype(F32)
    et = ((_iota2((D, 128), 0) >> shift) == _iota2((D, 128), 1) - row0).astype(F32)
    return e, et


def _cum_mats(chunk):
    ri, ci = _iota2((chunk, chunk), 0), _iota2((chunk, chunk), 1)
    return (ri >= ci).astype(F32), (ri <= ci).astype(F32)


def _gdn_gates_fwd(p, alog_row, dtb_row, *, bt=256):
    T = p.shape[0]
    bt = min(bt, T)

    def body(g_ref, al_ref, db_ref, beta_ref, gam_ref):
        gt = g_ref[...]
        eb, _ = _expand_mats(7, 0)
        eg, _ = _expand_mats(7, GDN_H)
        lc, _ = _cum_mats(GDN_C)
        beta_l = _sigmoid(gt)
        g_l = -jnp.exp(al_ref[...]) * _softplus(gt + db_ref[...])
        beta_ref[...] = _dot_sel(beta_l, eb)
        gam_ref[...] = _chunk_cumsum(_dot_sel(g_l, eg), lc, GDN_C)

    vec = pl.BlockSpec((1, 128), lambda i: (0, 0))
    row = pl.BlockSpec((bt, D), lambda i: (i, 0))
    return pl.pallas_call(
        body, grid=(T // bt,),
        in_specs=[pl.BlockSpec((bt, 128), lambda i: (i, 0)), vec, vec],
        out_specs=[row, row],
        out_shape=[jax.ShapeDtypeStruct((T, D), F32)] * 2, name="gdn_gates_fwd",
        compiler_params=_params(("parallel",)))(p, alog_row, dtb_row)


def _gdn_gates_bwd(p, alog_row, dtb_row, dbeta_x, dgam_x, dp_in, *, bt=256):
    T = p.shape[0]
    bt = min(bt, T)

    def body(g_ref, al_ref, db_ref, dbeta_ref, dgam_ref, dpin_ref, dg_out, dal_ref, ddb_ref):
        i = pl.program_id(0)
        gt = g_ref[...]
        _, ebt = _expand_mats(7, 0)
        _, egt = _expand_mats(7, GDN_H)
        _, uc = _cum_mats(GDN_C)
        ea = jnp.exp(al_ref[...])
        zz = gt + db_ref[...]
        g_l = -ea * _softplus(zz)
        beta_l = _sigmoid(gt)
        dg_l = _dot_sel(_chunk_cumsum(dgam_ref[...], uc, GDN_C), egt)
        dbeta_l = _dot_sel(dbeta_ref[...], ebt)
        da = dg_l * (-ea) * _sigmoid(zz)
        dg_out[...] = (da + dbeta_l * beta_l * (1.0 - beta_l)).astype(dg_out.dtype)

        @pl.when(i == 0)
        def _():
            dal_ref[...] = jnp.zeros_like(dal_ref)
            ddb_ref[...] = jnp.zeros_like(ddb_ref)

        dal_ref[...] += jnp.sum(dg_l * g_l, axis=0, keepdims=True)
        ddb_ref[...] += jnp.sum(da, axis=0, keepdims=True)

    vec = pl.BlockSpec((1, 128), lambda i: (0, 0))
    row = pl.BlockSpec((bt, D), lambda i: (i, 0))
    gate = pl.BlockSpec((bt, 128), lambda i: (i, C_GATE // 128))
    return pl.pallas_call(
        body, grid=(T // bt,),
        in_specs=[pl.BlockSpec((bt, 128), lambda i: (i, 0)), vec, vec, row, row, _ANY],
        out_specs=[gate, vec, vec],
        out_shape=[jax.ShapeDtypeStruct((T, C_TOT), BF16), jax.ShapeDtypeStruct((1, 128), F32),
                   jax.ShapeDtypeStruct((1, 128), F32)],
        input_output_aliases={5: 0}, name="gdn_gates_bwd",
        compiler_params=_params(("arbitrary",)))(p, alog_row, dtb_row, dbeta_x, dgam_x, dp_in)


def _ssd_dt_fwd(p, dtb_row, alog_x, *, bt=256):
    T = p.shape[0]
    bt = min(bt, T)

    def body(d_ref, db_ref, al_ref, dt_ref, alpha_ref):
        ed, _ = _expand_mats(6, 0)
        lc, _ = _cum_mats(SSM_L)
        dt_x = _dot_sel(_softplus(d_ref[...] + db_ref[...]), ed)
        dt_ref[...] = dt_x
        alpha_ref[...] = _chunk_cumsum(dt_x * (-jnp.exp(al_ref[...])), lc, SSM_L)

    row = pl.BlockSpec((bt, D), lambda i: (i, 0))
    return pl.pallas_call(
        body, grid=(T // bt,),
        in_specs=[pl.BlockSpec((bt, 128), lambda i: (i, 1)),
                  pl.BlockSpec((1, 128), lambda i: (0, 0)), pl.BlockSpec((1, D), lambda i: (0, 0))],
        out_specs=[row, row], out_shape=[jax.ShapeDtypeStruct((T, D), F32)] * 2,
        name="ssd_dt_fwd", compiler_params=_params(("parallel",)))(p, dtb_row, alog_x)


def _ssd_dt_bwd(p, dtb_row, alog_x, ddt_x, dalpha_x, dp_in, *, bt=256):
    T = p.shape[0]
    bt = min(bt, T)

    def body(d_ref, db_ref, al_ref, ddt_ref, dal_ref, dpin_ref, dd_out, ddb_ref, dalog_ref):
        i = pl.program_id(0)
        ed, edt = _expand_mats(6, 0)
        _, uc = _cum_mats(SSM_L)
        zz = d_ref[...] + db_ref[...]
        dt_x = _dot_sel(_softplus(zz), ed)
        a_x = -jnp.exp(al_ref[...])
        da_x = _chunk_cumsum(dal_ref[...], uc, SSM_L)
        ddt_l = _dot_sel(ddt_ref[...] + da_x * a_x, edt)
        draw = ddt_l * _sigmoid(zz)
        dd_out[...] = draw.astype(dd_out.dtype)

        @pl.when(i == 0)
        def _():
            ddb_ref[...] = jnp.zeros_like(ddb_ref)
            dalog_ref[...] = jnp.zeros_like(dalog_ref)

        ddb_ref[...] += jnp.sum(draw, axis=0, keepdims=True)
        dalog_ref[...] += jnp.sum(da_x * dt_x, axis=0, keepdims=True) * a_x

    row = pl.BlockSpec((bt, D), lambda i: (i, 0))
    seg = pl.BlockSpec((bt, 128), lambda i: (i, C_DT // 128))
    v128 = pl.BlockSpec((1, 128), lambda i: (0, 0))
    vD = pl.BlockSpec((1, D), lambda i: (0, 0))
    return pl.pallas_call(
        body, grid=(T // bt,),
        in_specs=[pl.BlockSpec((bt, 128), lambda i: (i, 1)), v128, vD, row, row, _ANY],
        out_specs=[seg, v128, vD],
        out_shape=[jax.ShapeDtypeStruct((T, C_TOT), BF16), jax.ShapeDtypeStruct((1, 128), F32),
                   jax.ShapeDtypeStruct((1, D), F32)],
        input_output_aliases={5: 0}, name="ssd_dt_bwd",
        compiler_params=_params(("arbitrary",)))(p, dtb_row, alog_x, ddt_x, dalpha_x, dp_in)


def _gdn_post_fwd(o, p, w_x, *, bt=256):
    T = o.shape[0]
    bt = min(bt, T)

    def body(o_ref, z_ref, w_ref, out_ref):
        for h in range(GDN_H):
            sl = slice(h * 128, (h + 1) * 128)
            oh = o_ref[:, sl]
            r = lax.rsqrt(jnp.mean(oh * oh, axis=1, keepdims=True) + EPS)
            out_ref[:, sl] = (oh * r * w_ref[:, sl] * _silu(z_ref[:, sl].astype(F32))).astype(out_ref.dtype)

    row = pl.BlockSpec((bt, D), lambda i: (i, 0))
    return pl.pallas_call(
        body, grid=(T // bt,),
        in_specs=[row, pl.BlockSpec((bt, D), lambda i: (i, C_ZG // D)), pl.BlockSpec((1, D), lambda i: (0, 0))],
        out_specs=row, out_shape=jax.ShapeDtypeStruct((T, 2 * D), BF16), name="gdn_post_fwd",
        compiler_params=_params(("parallel",)))(o, p, w_x)


def _gdn_post_bwd(dmix, o, p, w_x, *, bt=256):
    T = o.shape[0]
    bt = min(bt, T)

    def body(dm_ref, o_ref, z_ref, w_ref, do_ref, dz_ref, dw_ref):
        i = pl.program_id(0)

        @pl.when(i == 0)
        def _():
            dw_ref[...] = jnp.zeros_like(dw_ref)

        for h in range(GDN_H):
            sl = slice(h * 128, (h + 1) * 128)
            oh, zh, wh, dm = o_ref[:, sl], z_ref[:, sl].astype(F32), w_ref[:, sl], dm_ref[:, sl]
            r = lax.rsqrt(jnp.mean(oh * oh, axis=1, keepdims=True) + EPS)
            ohat = oh * r
            dy = dm * _silu(zh)
            dz_ref[:, sl] = (dm * ohat * wh * _dsilu(zh)).astype(dz_ref.dtype)
            dohat = dy * wh
            do_ref[:, sl] = r * (dohat - ohat * jnp.mean(dohat * ohat, axis=1, keepdims=True))
            dw_ref[:, sl] += jnp.sum(dy * ohat, axis=0, keepdims=True)

    row = pl.BlockSpec((bt, D), lambda i: (i, 0))
    zcol = pl.BlockSpec((bt, D), lambda i: (i, C_ZG // D))
    vec = pl.BlockSpec((1, D), lambda i: (0, 0))
    return pl.pallas_call(
        body, grid=(T // bt,), in_specs=[row, row, zcol, vec], out_specs=[row, zcol, vec],
        out_shape=[jax.ShapeDtypeStruct((T, D), F32), jax.ShapeDtypeStruct((T, C_TOT), BF16),
                   jax.ShapeDtypeStruct((1, D), F32)],
        name="gdn_post_bwd", compiler_params=_params(("arbitrary",)))(dmix, o, p, w_x)


def _ssd_post_fwd(y, xs, p, d_x, w, mix_in, *, bt=256):
    T = y.shape[0]
    bt = min(bt, T)

    def body(y_ref, x_ref, z_ref, d_ref, w_ref, mix_ref, out_ref):
        yg = (y_ref[...] + x_ref[...] * d_ref[...]) * _silu(z_ref[...].astype(F32))
        for g in range(2):
            sl = slice(g * 512, (g + 1) * 512)
            a = yg[:, sl]
            r = lax.rsqrt(jnp.mean(a * a, axis=1, keepdims=True) + EPS)
            out_ref[:, sl] = (a * r * w_ref[:, sl]).astype(out_ref.dtype)

    row = pl.BlockSpec((bt, D), lambda i: (i, 0))
    vec = pl.BlockSpec((1, D), lambda i: (0, 0))
    return pl.pallas_call(
        body, grid=(T // bt,),
        in_specs=[row, row, pl.BlockSpec((bt, D), lambda i: (i, C_ZS // D)), vec, vec, _ANY],
        out_specs=pl.BlockSpec((bt, D), lambda i: (i, 1)), out_shape=jax.ShapeDtypeStruct((T, 2 * D), BF16),
        input_output_aliases={5: 0}, name="ssd_post_fwd",
        compiler_params=_params(("parallel",)))(y, xs, p, d_x, w, mix_in)


def _ssd_post_bwd(dmix, y, xs, p, d_x, w, dp_in, *, bt=256):
    T = y.shape[0]
    bt = min(bt, T)

    def body(dm_ref, y_ref, x_ref, z_ref, d_ref, w_ref, dpin_ref, dyy_ref, dz_ref, dd_ref, dw_ref):
        i = pl.program_id(0)

        @pl.when(i == 0)
        def _():
            dd_ref[...] = jnp.zeros_like(dd_ref)
            dw_ref[...] = jnp.zeros_like(dw_ref)

        xv, zv = x_ref[...], z_ref[...].astype(F32)
        yy = y_ref[...] + xv * d_ref[...]
        sz = _silu(zv)
        yg = yy * sz
        parts = []
        for g in range(2):
            sl = slice(g * 512, (g + 1) * 512)
            a = yg[:, sl]
            r = lax.rsqrt(jnp.mean(a * a, axis=1, keepdims=True) + EPS)
            ah = a * r
            dout = dm_ref[:, sl]
            dah = dout * w_ref[:, sl]
            dw_ref[:, sl] += jnp.sum(dout * ah, axis=0, keepdims=True)
            parts.append(r * (dah - ah * jnp.mean(dah * ah, axis=1, keepdims=True)))
        dyg = jnp.concatenate(parts, axis=1)
        dyy = dyg * sz
        dyy_ref[...] = dyy
        dz_ref[...] = (dyg * yy * _dsilu(zv)).astype(dz_ref.dtype)
        dd_ref[...] += jnp.sum(dyy * xv, axis=0, keepdims=True)

    row = pl.BlockSpec((bt, D), lambda i: (i, 0))
    zcol = pl.BlockSpec((bt, D), lambda i: (i, C_ZS // D))
    vec = pl.BlockSpec((1, D), lambda i: (0, 0))
    return pl.pallas_call(
        body, grid=(T // bt,),
        in_specs=[pl.BlockSpec((bt, D), lambda i: (i, 1)), row, row, zcol, vec, vec, _ANY],
        out_specs=[row, zcol, vec, vec],
        out_shape=[jax.ShapeDtypeStruct((T, D), F32), jax.ShapeDtypeStruct((T, C_TOT), BF16),
                   jax.ShapeDtypeStruct((1, D), F32), jax.ShapeDtypeStruct((1, D), F32)],
        input_output_aliases={6: 1}, name="ssd_post_bwd",
        compiler_params=_params(("arbitrary",)))(dmix, y, xs, p, d_x, w, dp_in)


_NEG = -1e30


def _gdn_terms(q, k, v, bx, gam_c):
    C = GDN_C
    ri, ci = _iota2((C, C), 0), _iota2((C, C), 1)
    eye, low, strict = ri == ci, ri >= ci, ri > ci
    gam_r = jnp.sum(jnp.where(eye, gam_c, 0.0), axis=0, keepdims=True)
    G = jnp.exp(jnp.where(low, gam_c - gam_r, _NEG))
    glast = jnp.sum(jnp.where(_iota2((C, 1), 0) == C - 1, gam_c, 0.0), axis=0, keepdims=True)
    eg, egl, eL = jnp.exp(gam_c), jnp.exp(glast - gam_c), jnp.exp(glast)
    kb, vb = k * bx, v * bx
    M = _dot(kb, k, _NT)
    return dict(eye=eye, low=low, strict=strict, G=G, eg=eg, egl=egl, eL=eL, kb=kb, vb=vb, M=M,
                kbg=kb * eg, qd=q * eg, kd=k * egl, q=q, k=k, v=v, bx=bx)


def _split(a):
    hi = a.astype(_MXU)
    return hi, (a - hi.astype(F32)).astype(_MXU)


def _dot3s(a, b):
    d = lambda p, q: lax.dot_general(p, q, _NN, preferred_element_type=F32)
    return d(a[0], b[0]) + d(a[0], b[1]) + d(a[1], b[0])


def _tri_inv_many(Ls, eye):
    eyef = jnp.where(eye, 1.0, 0.0)
    Ts = [eyef - L for L in Ls]
    Ps = [-L for L in Ls]
    for _ in range(5):
        sp = [_split(p) for p in Ps]
        Ps = [_dot3s(s, s) for s in sp]
        sp = [_split(p) for p in Ps]
        st = [_split(t) for t in Ts]
        Ts = [t + _dot3s(a, b) for t, a, b in zip(Ts, st, sp)]
    return Ts


def _gdn_heads(q_ref, k_ref, v_ref, bx_ref, gx_ref, heads):
    out = []
    for h in heads:
        sl = slice(h * 128, (h + 1) * 128)
        gam_c = jnp.max(gx_ref[:, sl], axis=1, keepdims=True)
        out.append(_gdn_terms(q_ref[:, sl], k_ref[:, sl], v_ref[:, sl], bx_ref[:, sl], gam_c))
    return out


def _gdn_prep(qk, v, bx, gx, ride=None):
    T = qk.shape[0]
    N = T // GDN_C
    C, CS = GDN_C, GDN_LOCAL_CHUNKS
    NB = N // CS
    n_ride = ride.n if ride else 0

    def body(q_ref, k_ref, v_ref, bx_ref, gx_ref, *rest):
        ride_in = rest[:n_ride]
        u_ref, w_ref, qd_ref, kd_ref, p_ref, t_ref = rest[n_ride:n_ride + 6]
        ride_out = rest[n_ride + 6:2 * n_ride + 6]
        if ride:
            @pl.when(pl.program_id(0) == 0)
            def _():
                ride.start(ride_in, ride_out, rest[-3:])

            @pl.when(pl.program_id(0) == NB - 1)
            def _():
                ride.wait(ride_in, ride_out, rest[-3:])

        items = [(c, h) for c in range(CS) for h in range(GDN_H)]
        views = [[r.at[pl.ds(c * C, C)] for r in (q_ref, k_ref, v_ref, bx_ref, gx_ref)] for c in range(CS)]
        ts = [_gdn_heads(*views[c], [h])[0] for c, h in items]
        Ts = _tri_inv_many([jnp.where(t["strict"], t["M"] * t["G"], 0.0) for t in ts], ts[0]["eye"])
        for (c, h), t, Tm in zip(items, ts, Ts):
            tok = slice(c * C, (c + 1) * C)
            sl = slice(h * 128, (h + 1) * 128)
            rows = slice(h * C, (h + 1) * C)
            u_ref[tok, sl] = _dot(Tm, t["vb"])
            w_ref[tok, sl] = _dot(Tm, t["kbg"]).astype(w_ref.dtype)
            qd_ref[tok, sl] = t["qd"].astype(qd_ref.dtype)
            kd_ref[tok, sl] = t["kd"].astype(kd_ref.dtype)
            p_ref[c, rows, :] = _dot(t["q"], t["k"], _NT) * t["G"]
            t_ref[c, rows, :] = Tm

    blk = lambda c: pl.BlockSpec((CS * C, D), lambda n: (n, c))
    sq = pl.BlockSpec((CS, GDN_H * C, C), lambda n: (n, 0, 0))
    in_specs = [blk(0), blk(1), blk(0), blk(0), blk(0)]
    out_specs = [blk(0), blk(0), blk(0), blk(0), sq, sq]
    out_shape = [jax.ShapeDtypeStruct((T, D), F32), jax.ShapeDtypeStruct((T, D), BF16),
                 jax.ShapeDtypeStruct((T, D), BF16), jax.ShapeDtypeStruct((T, D), BF16),
                 jax.ShapeDtypeStruct((N, GDN_H * C, C), F32), jax.ShapeDtypeStruct((N, GDN_H * C, C), F32)]
    ins = [qk, qk, v, bx, gx]
    if ride:
        ins, in_specs = ins + ride.srcs, in_specs + ride.specs
        out_shape, out_specs = out_shape + ride.out_shape, out_specs + ride.specs
    res = pl.pallas_call(
        body, grid=(NB,), in_specs=in_specs, out_specs=out_specs, out_shape=out_shape,
        scratch_shapes=ride.scratch if ride else [], name="gdn_prep",
        compiler_params=_params(("arbitrary",) if ride else ("parallel",)))(*ins)
    return (list(res[:6]), list(res[6:])) if ride else list(res)


def _gdn_scan_fwd(u, w, qd, kd, pm, gx):
    T = u.shape[0]
    N = T // GDN_C
    C, CS = GDN_C, GDN_SCAN_CHUNKS

    def body(u_ref, w_ref, qd_ref, kd_ref, p_ref, gx_ref, o_ref, vn_ref, ss_ref, S_scr):
        n = pl.program_id(0)

        @pl.when(n == 0)
        def _():
            S_scr[...] = jnp.zeros_like(S_scr)

        sls = [slice(h * 128, (h + 1) * 128) for h in range(GDN_H)]
        for c in range(CS):
            rows = slice(c * C, (c + 1) * C)
            Ss = [S_scr[:, sl] for sl in sls]
            vns = [u_ref[rows, sl] - _dot(w_ref[rows, sl], S) for sl, S in zip(sls, Ss)]
            for h, (sl, S, vn) in enumerate(zip(sls, Ss, vns)):
                ss_ref[c, :, sl] = S
                vn_ref[rows, sl] = vn.astype(vn_ref.dtype)
                o_ref[rows, sl] = _dot(qd_ref[rows, sl], S) + _dot(p_ref[c, h * C:(h + 1) * C, :], vn)
                S_scr[:, sl] = (S * jnp.exp(gx_ref[(c + 1) * C - 1:(c + 1) * C, sl])
                                + _dot(kd_ref[rows, sl], vn, _TN))

    blk = pl.BlockSpec((CS * C, D), lambda n: (n, 0))
    return pl.pallas_call(
        body, grid=(N // CS,),
        in_specs=[blk, blk, blk, blk, pl.BlockSpec((CS, GDN_H * C, C), lambda n: (n, 0, 0)), blk],
        out_specs=[blk, blk, pl.BlockSpec((CS, GDN_DK, D), lambda n: (n, 0, 0))],
        out_shape=[jax.ShapeDtypeStruct((T, D), F32), jax.ShapeDtypeStruct((T, D), BF16),
                   jax.ShapeDtypeStruct((N, GDN_DK, D), F32)],
        scratch_shapes=[pltpu.VMEM((GDN_DK, D), F32)], name="gdn_scan_fwd",
        compiler_params=_params(("arbitrary",)))(u, w, qd, kd, pm, gx)


def _gdn_scan_bwd(w, qd, kd, pm, gx, do):
    T = w.shape[0]
    N = T // GDN_C
    C, CS = GDN_C, GDN_SCAN_CHUNKS
    NB = N // CS

    def body(w_ref, qd_ref, kd_ref, p_ref, gx_ref, do_ref, dvn_ref, ds_ref, dS_scr):
        n = pl.program_id(0)

        @pl.when(n == 0)
        def _():
            dS_scr[...] = jnp.zeros_like(dS_scr)

        sls = [slice(h * 128, (h + 1) * 128) for h in range(GDN_H)]
        for c in reversed(range(CS)):
            rows = slice(c * C, (c + 1) * C)
            dSs = [dS_scr[:, sl] for sl in sls]
            dvns = [_dot(p_ref[c, h * C:(h + 1) * C, :], do_ref[rows, sl], _TN) + _dot(kd_ref[rows, sl], dS2)
                    for h, (sl, dS2) in enumerate(zip(sls, dSs))]
            for sl, dS2, dvn in zip(sls, dSs, dvns):
                ds_ref[c, :, sl] = dS2
                dvn_ref[rows, sl] = dvn.astype(dvn_ref.dtype)
                dS_scr[:, sl] = (dS2 * jnp.exp(gx_ref[(c + 1) * C - 1:(c + 1) * C, sl])
                                 + _dot(qd_ref[rows, sl], do_ref[rows, sl], _TN) - _dot(w_ref[rows, sl], dvn, _TN))

    blk = pl.BlockSpec((CS * C, D), lambda n: (NB - 1 - n, 0))
    return pl.pallas_call(
        body, grid=(NB,),
        in_specs=[blk, blk, blk, pl.BlockSpec((CS, GDN_H * C, C), lambda n: (NB - 1 - n, 0, 0)), blk, blk],
        out_specs=[blk, pl.BlockSpec((CS, GDN_DK, D), lambda n: (NB - 1 - n, 0, 0))],
        out_shape=[jax.ShapeDtypeStruct((T, D), BF16), jax.ShapeDtypeStruct((N, GDN_DK, D), F32)],
        scratch_shapes=[pltpu.VMEM((GDN_DK, D), F32)], name="gdn_scan_bwd",
        compiler_params=_params(("arbitrary",)))(w, qd, kd, pm, gx, do)


def _gdn_rest_bwd(qk, v, bx, gx, s_save, t_save, vn, dvn, ds_save, do, ride=None):
    T = qk.shape[0]
    N = T // GDN_C
    C, CS = GDN_C, GDN_REST_CHUNKS
    NB = N // CS
    n_ride = ride.n if ride else 0

    def body(q_ref, k_ref, v_ref, bx_ref, gx_ref, ss_ref, ts_ref, vn_ref, dvn_ref, ds_ref, do_ref, *rest):
        ride_in = rest[:n_ride]
        dqkv_ref, dbx_ref, dgx_ref = rest[n_ride:n_ride + 3]
        ride_out = rest[n_ride + 3:2 * n_ride + 3]
        if ride:
            @pl.when(pl.program_id(0) == 0)
            def _():
                ride.start(ride_in, ride_out, rest[-3:])

            @pl.when(pl.program_id(0) == NB - 1)
            def _():
                ride.wait(ride_in, ride_out, rest[-3:])

        items = [(c, h) for c in range(CS) for h in range(GDN_H)]
        toks = [slice(c * C, (c + 1) * C) for c, _ in items]
        sls = [slice(h * 128, (h + 1) * 128) for _, h in items]
        views = [[r.at[pl.ds(c * C, C)] for r in (q_ref, k_ref, v_ref, bx_ref, gx_ref)] for c in range(CS)]
        ts = [_gdn_heads(*views[c], [h])[0] for c, h in items]
        Ss = [ss_ref[c, :, sl] for (c, _), sl in zip(items, sls)]
        Tms = [ts_ref[c, h * C:(h + 1) * C, :] for c, h in items]
        dS2s = [ds_ref[c, :, sl] for (c, _), sl in zip(items, sls)]
        dos = [do_ref[tok, sl] for tok, sl in zip(toks, sls)]
        vns = [vn_ref[tok, sl] for tok, sl in zip(toks, sls)]
        dvns = [dvn_ref[tok, sl] for tok, sl in zip(toks, sls)]
        Qs = [_dot(t["q"], t["k"], _NT) for t in ts]
        dws = [-_dot(dvn, S, _NT) for dvn, S in zip(dvns, Ss)]
        dqds = [_dot(do, S, _NT) for do, S in zip(dos, Ss)]
        dPs = [jnp.where(t["low"], _dot(do, vn, _NT), 0.0) for t, do, vn in zip(ts, dos, vns)]
        dkds = [_dot(vn, dS2, _NT) for vn, dS2 in zip(vns, dS2s)]
        dTs = [_dot(dvn, t["vb"], _NT) + _dot(dw, t["kbg"], _NT) for t, dvn, dw in zip(ts, dvns, dws)]
        dvbs = [_dot(Tm, dvn, _TN) for Tm, dvn in zip(Tms, dvns)]
        dkbgs = [_dot(Tm, dw, _TN) for Tm, dw in zip(Tms, dws)]
        TdTs = [_dot(Tm, dT, _TN) for Tm, dT in zip(Tms, dTs)]
        dLs = [jnp.where(t["strict"], -_dot(TdT, Tm, _NT), 0.0) for t, TdT, Tm in zip(ts, TdTs, Tms)]
        dMs = [dL * t["G"] for t, dL in zip(ts, dLs)]
        dQs = [dP * t["G"] for t, dP in zip(ts, dPs)]
        dkbs = [_dot(dM, t["k"]) + dkbg * t["eg"] for t, dM, dkbg in zip(ts, dMs, dkbgs)]
        rs = lambda a: jnp.sum(a, axis=1, keepdims=True)
        lane0 = _iota2((C, 128), 1) == 0
        last = _iota2((C, 1), 0) == C - 1
        for i, (_, h) in enumerate(items):
            t, sl, tok = ts[i], sls[i], toks[i]
            E = (dLs[i] * t["M"] + dPs[i] * Qs[i]) * t["G"]
            dqkv_ref[tok, sl] = _dot(dQs[i], t["k"]) + dqds[i] * t["eg"]
            dqkv_ref[tok, D + h * 128:D + (h + 1) * 128] = (
                _dot(dQs[i], t["q"], _TN) + _dot(dMs[i], t["kb"], _TN) + dkds[i] * t["egl"] + dkbs[i] * t["bx"])
            dqkv_ref[tok, 2 * D + h * 128:2 * D + (h + 1) * 128] = dvbs[i] * t["bx"]
            dbx_ref[tok, sl] = dkbs[i] * t["k"] + dvbs[i] * t["v"]
            dkd_kd = dkds[i] * t["kd"]
            dgam_c = rs(dqds[i] * t["qd"]) + rs(dkbgs[i] * t["kbg"]) - rs(dkd_kd) + rs(E)
            dgam_r = -jnp.sum(E, axis=0, keepdims=True)
            dgam_c = dgam_c + jnp.sum(jnp.where(t["eye"], dgam_r, 0.0), axis=1, keepdims=True)
            dlast = _sum_all(dkd_kd) + t["eL"] * _sum_all(Ss[i] * dS2s[i])
            dgx_ref[tok, sl] = jnp.where(lane0, dgam_c + jnp.where(last, dlast, 0.0), 0.0)

    blk = lambda c: pl.BlockSpec((CS * C, D), lambda n: (n, c))
    st = pl.BlockSpec((CS, GDN_DK, D), lambda n: (n, 0, 0))
    in_specs = [blk(0), blk(1), blk(0), blk(0), blk(0), st,
                pl.BlockSpec((CS, GDN_H * C, C), lambda n: (n, 0, 0)), blk(0), blk(0), st, blk(0)]
    out_specs = [pl.BlockSpec((CS * C, 3 * D), lambda n: (n, 0)), blk(0), blk(0)]
    out_shape = [jax.ShapeDtypeStruct((T, 3 * D), F32), jax.ShapeDtypeStruct((T, D), F32),
                 jax.ShapeDtypeStruct((T, D), F32)]
    ins = [qk, qk, v, bx, gx, s_save, t_save, vn, dvn, ds_save, do]
    if ride:
        ins, in_specs = ins + ride.srcs, in_specs + ride.specs
        out_shape, out_specs = out_shape + ride.out_shape, out_specs + ride.specs
    res = pl.pallas_call(
        body, grid=(NB,), in_specs=in_specs, out_specs=out_specs, out_shape=out_shape,
        scratch_shapes=ride.scratch if ride else [], name="gdn_rest_bwd",
        compiler_params=_params(("arbitrary",) if ride else ("parallel",)))(*ins)
    return (list(res[:3]), list(res[3:])) if ride else list(res)


def _ssd_seg(al_pair, half, s):
    L = SSM_L
    ri, ci = _iota2((L, L), 0), _iota2((L, L), 1)
    ac = jnp.max(jnp.where(half == s, al_pair, _NEG), axis=1, keepdims=True)
    ar = jnp.sum(jnp.where(ri == ci, ac, 0.0), axis=0, keepdims=True)
    return jnp.exp(jnp.where(ri >= ci, ac - ar, _NEG))


def _last_row(a):
    return jnp.sum(jnp.where(_iota2((a.shape[0], 1), 0) == a.shape[0] - 1, a, 0.0), axis=0, keepdims=True)


def _ssd_core_fwd(xbc, dtx, alx):
    T = xbc.shape[0]
    L, CS = SSM_L, SSM_SCAN_CHUNKS
    Nc = T // L

    def body(x_all, bc_all, dt_all, al_all, y_all, hs_all, H_scr):
        @pl.when(pl.program_id(0) == 0)
        def _():
            H_scr[...] = jnp.zeros_like(H_scr)

        for cc in range(CS):
            rows = pl.ds(cc * L, L)
            chunk(x_all.at[rows], bc_all.at[rows], dt_all.at[rows], al_all.at[rows], y_all.at[rows], hs_all.at[cc],
                  H_scr)

    def chunk(x_ref, bc_ref, dt_ref, al_ref, y_ref, hs_ref, H_scr):
        half = _iota2((L, 128), 1) >> 6
        for g in range(2):
            gs = slice(g * 512, (g + 1) * 512)
            Bg = bc_ref[:, g * 128:(g + 1) * 128]
            Cg = bc_ref[:, 256 + g * 128:256 + (g + 1) * 128]
            alg = al_ref[:, gs]
            alast = _last_row(alg)
            xdt = x_ref[:, gs] * dt_ref[:, gs]
            Hg = H_scr[:, gs]
            hs_ref[:, gs] = Hg
            CB = _dot(Cg, Bg, _NT)
            y_ref[:, gs] = jnp.exp(alg) * _dot(Cg, Hg)
            H_scr[:, gs] = Hg * jnp.exp(alast) + _dot(Bg, jnp.exp(alast - alg) * xdt, _TN)
            for j in range(4):
                ps = slice(g * 512 + j * 128, g * 512 + (j + 1) * 128)
                al_pair = al_ref[:, ps]
                xp = x_ref[:, ps] * dt_ref[:, ps]
                ys = [_dot(_ssd_seg(al_pair, half, s) * CB, xp) for s in range(2)]
                y_ref[:, ps] += jnp.where(half == 0, ys[0], ys[1])

    row = pl.BlockSpec((CS * L, D), lambda c: (c, 0))
    return pl.pallas_call(
        body, grid=(Nc // CS,), in_specs=[row, pl.BlockSpec((CS * L, 512), lambda c: (c, 2)), row, row],
        out_specs=[row, pl.BlockSpec((CS, SSM_N, D), lambda c: (c, 0, 0))],
        out_shape=[jax.ShapeDtypeStruct((T, D), F32), jax.ShapeDtypeStruct((Nc, SSM_N, D), F32)],
        scratch_shapes=[pltpu.VMEM((SSM_N, D), F32)], name="ssd_core_fwd",
        compiler_params=_params(("arbitrary",)))(xbc, xbc, dtx, alx)


def _ssd_core_bwd(xbc, dtx, alx, h_save, dyy, d_x):
    T = xbc.shape[0]
    L, CS = SSM_L, SSM_SCAN_CHUNKS
    Nc = T // L
    NB = Nc // CS

    def body(x_all, bc_all, dt_all, al_all, hs_all, dy_all, d_ref, dx_all, ddt_all, dal_all, dH_scr):
        @pl.when(pl.program_id(0) == 0)
        def _():
            dH_scr[...] = jnp.zeros_like(dH_scr)

        for cc in reversed(range(CS)):
            rows = pl.ds(cc * L, L)
            chunk(x_all.at[rows], bc_all.at[rows], dt_all.at[rows], al_all.at[rows], hs_all.at[cc], dy_all.at[rows],
                  d_ref, dx_all.at[rows], ddt_all.at[rows], dal_all.at[rows], dH_scr)

    def chunk(x_ref, bc_ref, dt_ref, al_ref, hs_ref, dy_ref, d_ref, dx_ref, ddt_ref, dal_ref, dH_scr):
        lane = _iota2((L, 128), 1)
        half = lane >> 6
        rowi = _iota2((L, 1), 0)
        ri, ci = _iota2((L, L), 0), _iota2((L, L), 1)
        for g in range(2):
            gs = slice(g * 512, (g + 1) * 512)
            Bg = bc_ref[:, g * 128:(g + 1) * 128]
            Cg = bc_ref[:, 256 + g * 128:256 + (g + 1) * 128]
            alg = al_ref[:, gs]
            alast = _last_row(alg)
            eal, edec, eL = jnp.exp(alg), jnp.exp(alast - alg), jnp.exp(alast)
            xg, dtg, dYg = x_ref[:, gs], dt_ref[:, gs], dy_ref[:, gs]
            xdt = xg * dtg
            Hg = hs_ref[:, gs]
            dH2 = dH_scr[:, gs]
            CB = _dot(Cg, Bg, _NT)
            dYe = eal * dYg
            dH_scr[:, gs] = dH2 * eL + _dot(Cg, dYe, _TN)
            dC = _dot(dYe, Hg, _NT)
            zg = edec * xdt
            dz = _dot(Bg, dH2)
            dB = _dot(zg, dH2, _NT)
            tz = dz * zg
            dal = dYe * _dot(Cg, Hg) - tz
            dalast = jnp.sum(tz, axis=0, keepdims=True) + eL * jnp.sum(Hg * dH2, axis=0, keepdims=True)
            dal = dal + jnp.where(rowi == L - 1, dalast, 0.0)
            dxdt_g = edec * dz
            dx_ref[:, gs] = dxdt_g * dtg + dYg * d_ref[:, gs]
            ddt_ref[:, gs] = dxdt_g * xg
            dal_ref[:, gs] = dal
            dCB = jnp.zeros((L, L), F32)
            for j in range(4):
                ps = slice(g * 512 + j * 128, g * 512 + (j + 1) * 128)
                al_pair = al_ref[:, ps]
                xp = x_ref[:, ps] * dt_ref[:, ps]
                dYp = dy_ref[:, ps]
                dxp = []
                dal_p = jnp.zeros((L, 128), F32)
                for s in range(2):
                    seg = _ssd_seg(al_pair, half, s)
                    W = seg * CB
                    dW = _dot(jnp.where(half == s, dYp, 0.0), xp, _NT)
                    dxp.append(_dot(W, dYp, _TN))
                    dCB = dCB + dW * seg
                    Es = dW * W
                    dac = jnp.sum(Es, axis=1, keepdims=True) - jnp.sum(
                        jnp.where(ri == ci, jnp.sum(Es, axis=0, keepdims=True), 0.0), axis=1, keepdims=True)
                    dal_p = dal_p + jnp.where(lane == 64 * s, dac, 0.0)
                dxdt_p = jnp.where(half == 0, dxp[0], dxp[1])
                dx_ref[:, ps] += dxdt_p * dt_ref[:, ps]
                ddt_ref[:, ps] += dxdt_p * x_ref[:, ps]
                dal_ref[:, ps] += dal_p
            dx_ref[:, D + g * 128:D + (g + 1) * 128] = dB + _dot(dCB, Cg, _TN)
            dx_ref[:, D + 256 + g * 128:D + 256 + (g + 1) * 128] = dC + _dot(dCB, Bg)

    row = pl.BlockSpec((CS * L, D), lambda c: (NB - 1 - c, 0))
    bcs = pl.BlockSpec((CS * L, 512), lambda c: (NB - 1 - c, 2))
    return pl.pallas_call(
        body, grid=(NB,),
        in_specs=[row, bcs, row, row, pl.BlockSpec((CS, SSM_N, D), lambda c: (NB - 1 - c, 0, 0)), row,
                  pl.BlockSpec((1, D), lambda c: (0, 0))],
        out_specs=[pl.BlockSpec((CS * L, D + 512), lambda c: (NB - 1 - c, 0)), row, row],
        out_shape=[jax.ShapeDtypeStruct((T, D + 512), F32),
                   jax.ShapeDtypeStruct((T, D), F32), jax.ShapeDtypeStruct((T, D), F32)],
        scratch_shapes=[pltpu.VMEM((SSM_N, D), F32)], name="ssd_core_bwd",
        compiler_params=_params(("arbitrary",)))(xbc, xbc, dtx, alx, h_save, dyy, d_x)


_EARLY = ("w_out", "wq_mem", "wk_mem", "wv_mem", "wo_mem")
_LATE = ("w_up", "w_down")
_GRADS_MLP = ("w_down", "w_up")
_GRADS_MID = ("wo_mem", "wq_mem", "wk_mem", "wv_mem", "w_out")


def _gather_ride(shards, names):
    return None if shards is None else _Ride([shards[n] for n in names], shard=True)


def _grad_ride(shards, G, names):
    return None if shards is None else _Ride([_slots_from_full(n, G[n]) for n in names], shard=False)


def _local_step(x, mem, tgt, W, shards=None):
    T = x.shape[0]
    W = dict(W)
    cw_qk, cw_v = W["gdn_conv_w"][:, :2 * D], W["gdn_conv_w"][:, 2 * D:]
    h1 = _rmsnorm_fwd(x, W["norm1_w"], name="norm1_fwd")
    ride = _gather_ride(shards, _EARLY)
    pg = _mm(h1, W["w_in_pad"][:, C_GATE:], name="in_proj_gates")
    p = _mm(h1, W["w_in_pad"][:, :C_GATE], out_dtype=BF16, bn_cap=1664, name="in_proj", ride=ride)
    if ride:
        p, got = p
        W.update({n: _full_from_slots(n, g) for n, g in zip(_EARLY, got)})
    qk = _conv_fwd(p, C_QKV, 2 * D, cw_qk, None, l2=True, name="gdn_conv_qk_fwd")
    v_g = _conv_fwd(p, C_QKV + 2 * D, D, cw_v, None, l2=False, name="gdn_conv_v_fwd")
    bx, gx = _gdn_gates_fwd(pg, W["gdn_alog_row"], W["gdn_dtb_row"])
    ride = _gather_ride(shards, _LATE)
    prep = _gdn_prep(qk, v_g, bx, gx, ride)
    if ride:
        prep, got = prep
        W.update({n: _full_from_slots(n, g) for n, g in zip(_LATE, got)})
    u_g, w_g, qd_g, kd_g, p_g, t_save = prep
    o_g, vn_g, s_save = _gdn_scan_fwd(u_g, w_g, qd_g, kd_g, p_g, gx)
    mix = _gdn_post_fwd(o_g, p, W["gdn_norm_x"])
    xbc = _conv_fwd(p, C_XBC, D + 512, W["ssm_conv_w"], W["ssm_conv_b"], l2=False, name="ssm_conv_fwd", bc=512)
    dtx, alx = _ssd_dt_fwd(pg, W["ssm_dtb_row"], W["ssm_alog_x"])
    y_s, h_save = _ssd_core_fwd(xbc, dtx, alx)
    mix = _ssd_post_fwd(y_s, xbc, p, W["ssm_d_x"], W["ssm_norm_w"].reshape(1, D), mix)
    x1, h2 = _mm(mix, W["w_out"], epi="res_norm", extra=(x, W["norm2_w"]), bm=512, name="out_proj")
    qm = _mm(h2, W["wq_mem"], out_dtype=BF16, name="q_proj")
    m = _rmsnorm_fwd(mem, W["mem_norm_w"], name="mem_norm_fwd")
    km = _mm(m, W["wk_mem"], name="k_proj")
    vm = _mm(m, W["wv_mem"], name="v_proj")
    oa = _attn_fwd(qm, km, vm)
    x2, h3 = _mm(oa, W["wo_mem"], epi="res_norm", extra=(x1, W["norm3_w"]), bm=512, name="o_proj")
    u, act = _mm(h3, W["w_up"], epi="relu2", out_dtype=BF16, name="mlp_up")
    dx3, g_final, loss = _mm(act, W["w_down"], epi="res_loss", extra=(x2, tgt, W["final_norm_w"]), bk_cap=1024,
                             name="mlp_down_loss")
    G = {"final_norm_w": g_final.reshape(D)}
    dpre = _mm(dx3, W["w_down"], dims="nt", epi="mul2", extra=u, out_dtype=BF16, name="mlp_down_dx")
    G["w_down"] = _mm(act, dx3, dims="tn", out_dtype=BF16, name="mlp_down_dw")
    G["w_up"] = _mm(h3, dpre, dims="tn", out_dtype=BF16, name="mlp_up_dw")
    dx2, gw = _mm(dpre, W["w_up"], dims="nt", epi="norm_bwd", extra=(x2, dx3, W["norm3_w"]), bk_cap=1024,
                  name="mlp_up_dx")
    G["norm3_w"] = gw.reshape(D)
    do_a = _mm(dx2, W["wo_mem"], dims="nt", out_dtype=BF16, name="o_proj_dx")
    G["wo_mem"] = _mm(oa, dx2, dims="tn", out_dtype=BF16, name="o_proj_dw")
    dq, dk, dv = _attn_bwd(qm, km, vm, do_a)
    G["wq_mem"] = _mm(h2, dq, dims="tn", out_dtype=BF16, name="q_proj_dw")
    dx1, gw = _mm(dq, W["wq_mem"], dims="nt", epi="norm_bwd", extra=(x1, dx2, W["norm2_w"]), bm=512,
                  name="q_proj_dx")
    G["norm2_w"] = gw.reshape(D)
    G["wk_mem"] = _mm(m, dk, dims="tn", out_dtype=BF16, name="k_proj_dw")
    G["wv_mem"] = _mm(m, dv, dims="tn", out_dtype=BF16, name="v_proj_dw")
    dm = _mm(dk, W["wk_mem"], dims="nt", name="k_proj_dx")
    dm = _mm(dv, W["wv_mem"], dims="nt", epi="res", extra=dm, name="v_proj_dx")
    _, G["mem_norm_w"] = _rmsnorm_bwd(mem, W["mem_norm_w"], dm, None, name="mem_norm_bwd")
    dmix = _mm(dx1, W["w_out"], dims="nt", name="out_proj_dx")
    G["w_out"] = _mm(mix, dx1, dims="tn", out_dtype=BF16, name="out_proj_dw")
    do_g, dp, G["gdn_norm_x"] = _gdn_post_bwd(dmix, o_g, p, W["gdn_norm_x"])
    dvn_g, ds_save = _gdn_scan_bwd(w_g, qd_g, kd_g, p_g, gx, do_g)
    ride = _grad_ride(shards, G, _GRADS_MLP)
    rest = _gdn_rest_bwd(qk, v_g, bx, gx, s_save, t_save, vn_g, dvn_g, ds_save, do_g, ride)
    if ride:
        rest, got = rest
        G.update(zip(_GRADS_MLP, got))
    dqkvn, dbx, dgx = rest
    dy_qk, gcw_qk, _ = _conv_bwd_act(p, C_QKV, 2 * D, cw_qk, None, dqkvn, 0, l2=True, name="gdn_conv_qk_bwd_act")
    dy_v, gcw_v, _ = _conv_bwd_act(p, C_QKV + 2 * D, D, cw_v, None, dqkvn, 2 * D, l2=False,
                                   name="gdn_conv_v_bwd_act")
    G["gdn_conv_w"] = jnp.concatenate([gcw_qk, gcw_v], axis=1)
    dp = _conv_bwd_in(dy_qk, cw_qk, dp, C_QKV, T, name="gdn_conv_qk_bwd_in")
    dp = _conv_bwd_in(dy_v, cw_v, dp, C_QKV + 2 * D, T, name="gdn_conv_v_bwd_in")
    dp, G["gdn_alog_row"], G["gdn_dtb_row"] = _gdn_gates_bwd(pg, W["gdn_alog_row"], W["gdn_dtb_row"], dbx, dgx, dp)
    dyy, dp, G["ssm_d_x"], G["ssm_norm_w"] = _ssd_post_bwd(dmix, y_s, xbc, p, W["ssm_d_x"],
                                                          W["ssm_norm_w"].reshape(1, D), dp)
    dxbc, ddtx, dalx = _ssd_core_bwd(xbc, dtx, alx, h_save, dyy, W["ssm_d_x"])
    dy_s, G["ssm_conv_w"], G["ssm_conv_b"] = _conv_bwd_act(p, C_XBC, D + 512, W["ssm_conv_w"], W["ssm_conv_b"],
                                                           dxbc, 0, l2=False, name="ssm_conv_bwd_act", bc=512)
    dp = _conv_bwd_in(dy_s, W["ssm_conv_w"], dp, C_XBC, T, name="ssm_conv_bwd_in", bc=512)
    dp, G["ssm_dtb_row"], G["ssm_alog_x"] = _ssd_dt_bwd(pg, W["ssm_dtb_row"], W["ssm_alog_x"], ddtx, dalx, dp)
    ride = _grad_ride(shards, G, _GRADS_MID)
    g_in = _mm(h1, dp, dims="tn", out_dtype=BF16, bn_cap=1152, name="in_proj_dw", ride=ride)
    if ride:
        g_in, got = g_in
        G.update(zip(_GRADS_MID, got))
    G["w_in"] = _unpad_w_in(g_in)
    ride = _grad_ride(shards, G, ("w_in",))
    res = _mm(dp, W["w_in_pad"], dims="nt", epi="norm_bwd", extra=(x, dx1, W["norm1_w"]),
              name="in_proj_dx", ride=ride)
    if ride:
        res, got = res
        G["w_in"] = got[0]
    dx, gw = res
    G["norm1_w"] = gw.reshape(D)
    return loss, dx, G


def _all_gather(shards, out_dtype, *, name):
    n = len(shards)

    def body(*refs):
        x_refs, out_refs, stage = refs[:n], refs[n:2 * n], refs[2 * n:3 * n]
        send_sems, recv_sems, local_sems = refs[3 * n:]
        x, y, c = _place()
        me, sibling = (x, y, c), (x, y, 1 - c)
        chips = [(1 - x, y), (x, 1 - y), (1 - x, 1 - y)]

        def slot(px, py, pc):
            return 4 * px + 2 * py + pc

        def copy(a, k, block, to, src=None):
            dst = out_refs[a].at[slot(*block)]
            return pltpu.make_async_remote_copy(
                src_ref=dst if src is None else src, dst_ref=dst, send_sem=send_sems.at[a, k],
                recv_sem=recv_sems.at[a, k], device_id=to, device_id_type=_MESH)

        for a in range(n):
            stage[a][...] = x_refs[a][...].astype(out_dtype)
        mine = [pltpu.make_async_copy(stage[a], out_refs[a].at[slot(*me)], local_sems.at[a]) for a in range(n)]
        for cp in mine:
            cp.start()
        first = []
        for a in range(n):
            first.append(copy(a, 0, me, sibling, src=stage[a]))
            first += [copy(a, 1 + j, me, (*chip, c), src=stage[a]) for j, chip in enumerate(chips)]
        for cp in first:
            cp.start()
        passed = [[copy(a, 4 + j, (*chip, c), sibling) for j, chip in enumerate(chips)] for a in range(n)]
        for j, chip in enumerate(chips):
            for a in range(n):
                copy(a, 1 + j, (*chip, c), me).wait_recv()
                passed[a][j].start()
        for a in range(n):
            copy(a, 0, sibling, me).wait_recv()
            for j, chip in enumerate(chips):
                copy(a, 4 + j, (*chip, 1 - c), me).wait_recv()
        for cp in first + [cp for row in passed for cp in row]:
            cp.wait_send()
        for cp in mine:
            cp.wait()

    outs = pl.pallas_call(
        body, in_specs=[_VM] * n, out_specs=[_ANY] * n,
        out_shape=[jax.ShapeDtypeStruct((N_DEV,) + s.shape, out_dtype) for s in shards],
        scratch_shapes=[pltpu.VMEM(s.shape, out_dtype) for s in shards]
        + [pltpu.SemaphoreType.DMA((n, 7)), pltpu.SemaphoreType.DMA((n, 7)), pltpu.SemaphoreType.DMA((n,))],
        name=name, compiler_params=pltpu.CompilerParams(vmem_limit_bytes=VMEM_LIMIT))(*shards)
    return list(outs)


def _cast_bf16(arrs, *, name):
    n = len(arrs)

    def body(*refs):
        for a in range(n):
            refs[n + a][...] = refs[a][...].astype(BF16)

    return list(pl.pallas_call(
        body, in_specs=[_VM] * n, out_specs=[_VM] * n,
        out_shape=[jax.ShapeDtypeStruct(s.shape, BF16) for s in arrs], name=name,
        compiler_params=pltpu.CompilerParams(vmem_limit_bytes=VMEM_LIMIT))(*arrs))


def _sum8(a, *, name):
    _, R, Cc = a.shape
    br = _pick_rows(R, 128)

    def body(a_ref, o_ref):
        s = a_ref[0].astype(F32)
        for k in range(1, N_DEV):
            s = s + a_ref[k].astype(F32)
        o_ref[...] = s

    return pl.pallas_call(
        body, grid=(R // br,), in_specs=[pl.BlockSpec((N_DEV, br, Cc), lambda i: (0, i, 0))],
        out_specs=pl.BlockSpec((br, Cc), lambda i: (i, 0)), out_shape=jax.ShapeDtypeStruct((R, Cc), F32),
        name=name, compiler_params=_params(("parallel",)))(a)


def _pick_rows(R, cap):
    if R <= cap:
        return R
    for d in range(cap, 7, -8):
        if R % d == 0:
            return d
    return R


def _adamw(w, g, m, v, *, name):
    shape = w.shape
    as2d = (lambda t: t.reshape(1, -1)) if w.ndim == 1 else (lambda t: t)
    w2, g2, m2, v2 = as2d(w), as2d(g), as2d(m), as2d(v)
    R, Cc = w2.shape
    br = _pick_rows(R, 256)
    c1 = 1.0 - ADAM_B1 ** ADAM_STEP
    c2 = 1.0 - ADAM_B2 ** ADAM_STEP

    def body(w_ref, g_ref, m_ref, v_ref, d_ref, nm_ref, nv_ref):
        gv = g_ref[...]
        nm = ADAM_B1 * m_ref[...] + (1.0 - ADAM_B1) * gv
        nv = ADAM_B2 * v_ref[...] + (1.0 - ADAM_B2) * (gv * gv)
        nm_ref[...] = nm
        nv_ref[...] = nv
        d_ref[...] = -ADAM_LR * ((nm / c1) / (jnp.sqrt(nv / c2) + ADAM_EPS) + ADAM_WD * w_ref[...])

    blk = pl.BlockSpec((br, Cc), lambda i: (i, 0))
    outs = pl.pallas_call(
        body, grid=(R // br,), in_specs=[blk] * 4, out_specs=[blk] * 3,
        out_shape=[jax.ShapeDtypeStruct((R, Cc), F32)] * 3, name=name,
        compiler_params=_params(("parallel",)))(w2, g2, m2, v2)
    return tuple(o.reshape(shape) for o in outs)


_BIG = ("w_in", "w_out", "wq_mem", "wk_mem", "wv_mem", "wo_mem", "w_up", "w_down")
_COL_SHARDED = ("w_in", "w_up")
_WEIGHTS = ("norm1_w", "w_in", "gdn_conv_w", "gdn_a_log", "gdn_dt_bias", "gdn_norm_w", "ssm_conv_w", "ssm_conv_b",
            "ssm_a_log", "ssm_dt_bias", "ssm_d", "ssm_norm_w", "w_out", "norm2_w", "mem_norm_w", "wq_mem", "wk_mem",
            "wv_mem", "wo_mem", "norm3_w", "w_up", "w_down", "final_norm_w")
_IN_PAD = 112


def _full_from_slots(name, g):
    if name in _COL_SHARDED:
        return jnp.transpose(g, (1, 0, 2)).reshape(g.shape[1], N_DEV * g.shape[2])
    return g.reshape(N_DEV * g.shape[1], g.shape[2])


def _slots_from_full(name, f):
    if name in _COL_SHARDED:
        return jnp.transpose(f.reshape(f.shape[0], N_DEV, f.shape[1] // N_DEV), (1, 0, 2))
    return f.reshape(N_DEV, f.shape[0] // N_DEV, f.shape[1])


def _pad_w_in(w):
    z = jnp.zeros((w.shape[0], _IN_PAD), w.dtype)
    return jnp.concatenate([w[:, :4096], w[:, 4112:6672], w[:, 4096:4112], z, w[:, 6672:6688], z], axis=1)


def _unpad_w_in(gp):
    return jnp.concatenate([gp[:, :4096], gp[:, C_GATE:C_GATE + 16], gp[:, 4096:C_GATE], gp[:, C_DT:C_DT + 16]],
                           axis=1)


def _pack_rows(vals):
    rows, offs, r = [], [], 0
    for vflat in vals:
        nrow = 8 * -(-vflat.shape[0] // 1024)
        rows.append(jnp.pad(vflat, (0, nrow * 128 - vflat.shape[0])).reshape(nrow, 128))
        offs.append((r, vflat.shape[0]))
        r += nrow
    return jnp.concatenate(rows, axis=0), offs


def _unpack_rows(packed, offs, shapes):
    out = []
    for (r, nel), shp in zip(offs, shapes):
        nrow = -(-nel // 128)
        out.append(packed[r:r + nrow].reshape(-1)[:nel].reshape(shp))
    return out


def kernel(x, mem, norm1_w, w_in, gdn_conv_w, gdn_a_log, gdn_dt_bias, gdn_norm_w, ssm_conv_w, ssm_conv_b, ssm_a_log, ssm_dt_bias, ssm_d, ssm_norm_w, w_out, norm2_w, mem_norm_w, wq_mem, wk_mem, wv_mem, wo_mem, norm3_w, w_up, w_down, final_norm_w, loss_target, m_norm1_w, m_w_in, m_gdn_conv_w, m_gdn_a_log, m_gdn_dt_bias, m_gdn_norm_w, m_ssm_conv_w, m_ssm_conv_b, m_ssm_a_log, m_ssm_dt_bias, m_ssm_d, m_ssm_norm_w, m_w_out, m_norm2_w, m_mem_norm_w, m_wq_mem, m_wk_mem, m_wv_mem, m_wo_mem, m_norm3_w, m_w_up, m_w_down, m_final_norm_w, v_norm1_w, v_w_in, v_gdn_conv_w, v_gdn_a_log, v_gdn_dt_bias, v_gdn_norm_w, v_ssm_conv_w, v_ssm_conv_b, v_ssm_a_log, v_ssm_dt_bias, v_ssm_d, v_ssm_norm_w, v_w_out, v_norm2_w, v_mem_norm_w, v_wq_mem, v_wk_mem, v_wv_mem, v_wo_mem, v_norm3_w, v_w_up, v_w_down, v_final_norm_w):
    args = dict(locals())
    w_loc = {n: args[n] for n in _WEIGHTS}
    me = 4 * lax.axis_index("x") + 2 * lax.axis_index("y") + lax.axis_index("c")

    w_in_full = _full_from_slots("w_in", _all_gather([w_in], BF16, name="gather_w_in")[0])
    later = _EARLY + _LATE
    shards = dict(zip(later, _cast_bf16([w_loc[n] for n in later], name="cast_shards")))
    conv_pack, conv_offs = _pack_rows([gdn_conv_w.reshape(-1), ssm_conv_w.reshape(-1)])
    conv_all = _all_gather([conv_pack], F32, name="gather_conv")[0]
    gdn_cw, ssm_cw = [], []
    for k in range(N_DEV):
        a, b = _unpack_rows(conv_all[k], conv_offs, [gdn_conv_w.shape, ssm_conv_w.shape])
        gdn_cw.append(a)
        ssm_cw.append(b)
    W = {
        "w_in_pad": _pad_w_in(w_in_full),
        "norm1_w": norm1_w, "norm2_w": norm2_w, "norm3_w": norm3_w, "mem_norm_w": mem_norm_w,
        "final_norm_w": final_norm_w, "ssm_norm_w": ssm_norm_w, "ssm_conv_b": ssm_conv_b,
        "gdn_conv_w": jnp.concatenate(gdn_cw, axis=1), "ssm_conv_w": jnp.concatenate(ssm_cw, axis=1),
        "gdn_alog_row": jnp.pad(gdn_a_log, (GDN_H, 128 - 2 * GDN_H)).reshape(1, 128),
        "gdn_dtb_row": jnp.pad(gdn_dt_bias, (GDN_H, 128 - 2 * GDN_H)).reshape(1, 128),
        "gdn_norm_x": jnp.tile(gdn_norm_w, GDN_H).reshape(1, D),
        "ssm_dtb_row": jnp.pad(ssm_dt_bias, (0, 128 - SSM_H)).reshape(1, 128),
        "ssm_alog_x": jnp.repeat(ssm_a_log, SSM_P).reshape(1, D),
        "ssm_d_x": jnp.repeat(ssm_d, SSM_P).reshape(1, D),
    }

    loss_part, grad_x, G = _local_step(x[0], mem[0], loss_target[0], W, shards)

    grads = {n: _sum8(G[n], name="sum_" + n) for n in _BIG}

    small = {
        "norm1_w": G["norm1_w"], "gdn_conv_w": G["gdn_conv_w"], "gdn_a_log": G["gdn_alog_row"][0, GDN_H:2 * GDN_H],
        "gdn_dt_bias": G["gdn_dtb_row"][0, GDN_H:2 * GDN_H], "gdn_norm_w": G["gdn_norm_x"].reshape(GDN_H, 128).sum(0),
        "ssm_conv_w": G["ssm_conv_w"], "ssm_conv_b": G["ssm_conv_b"],
        "ssm_a_log": G["ssm_alog_x"].reshape(SSM_H, SSM_P).sum(1), "ssm_dt_bias": G["ssm_dtb_row"][0, :SSM_H],
        "ssm_d": G["ssm_d_x"].reshape(SSM_H, SSM_P).sum(1), "ssm_norm_w": G["ssm_norm_w"].reshape(D),
        "norm2_w": G["norm2_w"], "mem_norm_w": G["mem_norm_w"], "norm3_w": G["norm3_w"],
        "final_norm_w": G["final_norm_w"], "loss": loss_part[0, :1],
    }
    names = list(small)
    pack, offs = _pack_rows([small[n].reshape(-1) for n in names])
    tot = _sum8(_all_gather([pack], F32, name="gather_small")[0], name="sum_small")
    summed = dict(zip(names, _unpack_rows(tot, offs, [small[n].shape for n in names])))
    loss = summed.pop("loss")[0]
    for n in ("gdn_conv_w", "ssm_conv_w"):
        width = w_loc[n].shape[1]
        summed[n] = lax.dynamic_slice_in_dim(summed[n], me * width, width, axis=1)
    grads.update(summed)

    upd = {n: _adamw(w_loc[n], grads[n], args["m_" + n], args["v_" + n], name="adamw_" + n) for n in _WEIGHTS}
    return (loss, grad_x[None], *[grads[n] for n in _WEIGHTS], *[upd[n][0] for n in _WEIGHTS],
            *[upd[n][1] for n in _WEIGHTS], *[upd[n][2] for n in _WEIGHTS])
```

```python
import functools
import math

import jax
import jax.numpy as jnp
from jax import lax
from jax.experimental import pallas as pl
from jax.experimental.pallas import tpu as pltpu

F32 = jnp.float32
BF16 = jnp.bfloat16
_MXU = BF16

D = 1024
EPS = 1e-6
CONV_K = 4
GDN_H, GDN_DK, GDN_C = 8, 128, 64
GDN_SCAN_CHUNKS = 4
GDN_LOCAL_CHUNKS = 4
GDN_REST_CHUNKS = 4
SSM_H, SSM_P, SSM_L, SSM_N = 16, 64, 128, 128
SSM_SCAN_CHUNKS = 2
MEM_H, MEM_HD = 4, 256
D_FF = 4096
N_DEV = 8

C_QKV, C_ZG, C_ZS, C_XBC, C_GATE, C_DT, C_TOT = 0, 3072, 4096, 5120, 6656, 6784, 6912
P_HALO = 16

ADAM_LR, ADAM_B1, ADAM_B2, ADAM_EPS, ADAM_WD, ADAM_STEP = 0.001, 0.9, 0.999, 1e-08, 0.01, 10

VMEM_LIMIT = 56 * 1024 * 1024

_NN = (((1,), (0,)), ((), ()))
_NT = (((1,), (1,)), ((), ()))
_TN = (((0,), (0,)), ((), ()))


def _dot(a, b, dims=_NN):
    return lax.dot_general(a.astype(_MXU), b.astype(_MXU), dims, preferred_element_type=F32)


def _split3(a):
    a1 = a.astype(BF16)
    r1 = a - a1.astype(F32)
    a2 = r1.astype(BF16)
    return a1, a2, (r1 - a2.astype(F32)).astype(BF16)


def _dot_sel(a, e):
    eb = e.astype(BF16)
    return sum(lax.dot_general(p, eb, _NN, preferred_element_type=F32) for p in _split3(a))


def _sel_dot(e, a):
    eb = e.astype(BF16)
    return sum(lax.dot_general(eb, p, _NN, preferred_element_type=F32) for p in _split3(a))


def _chunk_cumsum(a, tri, chunk):
    return jnp.concatenate([_sel_dot(tri, a[r:r + chunk]) for r in range(0, a.shape[0], chunk)], axis=0)


def _params(sem):
    return pltpu.CompilerParams(dimension_semantics=sem, vmem_limit_bytes=VMEM_LIMIT)


def _pick(n, cap):
    for d in range(min(cap, n), 0, -128):
        if n % d == 0 and d % 128 == 0:
            return d
    return n


def _sigmoid(x):
    return 0.5 * jnp.tanh(0.5 * x) + 0.5


def _silu(x):
    return x * _sigmoid(x)


def _dsilu(x):
    s = _sigmoid(x)
    return s * (1.0 + x * (1.0 - s))


def _softplus(x):
    return jnp.maximum(x, 0.0) + jnp.log(1.0 + jnp.exp(-jnp.abs(x)))


def _iota2(shape, axis):
    return lax.broadcasted_iota(jnp.int32, shape, axis)


def _sum_all(x):
    return jnp.sum(jnp.sum(x, axis=1, keepdims=True), axis=0, keepdims=True)


_MESH = pl.DeviceIdType.MESH
_ANY = pl.BlockSpec(memory_space=pl.ANY)
_VM = pl.BlockSpec(memory_space=pltpu.VMEM)
_REL = [(r >> 2 & 1, r >> 1 & 1, r & 1) for r in range(1, N_DEV)]


def _place():
    return lax.axis_index("x"), lax.axis_index("y"), lax.axis_index("c")


class _Ride:
    def __init__(self, srcs, shard):
        self.srcs, self.shard, self.n = list(srcs), shard, len(srcs)
        self.out_shape = [jax.ShapeDtypeStruct(((N_DEV,) + s.shape) if shard else s.shape, s.dtype)
                          for s in self.srcs]
        self.specs = [_ANY] * self.n
        self.scratch = [pltpu.SemaphoreType.DMA((self.n, N_DEV - 1)), pltpu.SemaphoreType.DMA((self.n, N_DEV - 1)),
                        pltpu.SemaphoreType.DMA((self.n,))]

    def _copies(self, in_refs, out_refs, sems):
        send, recv, loc = sems
        x, y, c = _place()
        me = 4 * x + 2 * y + c
        local, remote, arrive = [], [], []
        for a in range(self.n):
            src = in_refs[a] if self.shard else in_refs[a].at[me]
            local.append(pltpu.make_async_copy(src, out_refs[a].at[me], loc.at[a]))
        for k, (rx, ry, rc) in enumerate(_REL):
            peer = (lax.rem(x + rx, 2), lax.rem(y + ry, 2), lax.rem(c + rc, 2))
            ps = 4 * peer[0] + 2 * peer[1] + peer[2]
            for a in range(self.n):
                src = in_refs[a] if self.shard else in_refs[a].at[ps]
                remote.append(pltpu.make_async_remote_copy(
                    src_ref=src, dst_ref=out_refs[a].at[me], send_sem=send.at[a, k], recv_sem=recv.at[a, k],
                    device_id=peer, device_id_type=_MESH))
                slot = out_refs[a].at[ps]
                arrive.append(pltpu.make_async_remote_copy(
                    src_ref=slot, dst_ref=slot, send_sem=send.at[a, k], recv_sem=recv.at[a, k],
                    device_id=peer, device_id_type=_MESH))
        return local, remote, arrive

    def start(self, in_refs, out_refs, sems):
        local, remote, _ = self._copies(in_refs, out_refs, sems)
        for cp in local + remote:
            cp.start()

    def wait(self, in_refs, out_refs, sems):
        local, remote, arrive = self._copies(in_refs, out_refs, sems)
        for cp in arrive:
            cp.wait_recv()
        for cp in remote:
            cp.wait_send()
        for cp in local:
            cp.wait()


_EPI = {
    "none": ((), ("tile",)),
    "res": (("tile",), ("tile",)),
    "mul2": (("tile",), ("tile",)),
    "relu2": ((), ("tile", "tile")),
    "res_norm": (("tile", "row"), ("tile", "tile")),
    "norm_bwd": (("tile", "tile", "row"), ("tile", "row")),
    "res_loss": (("tile", "tile", "row"), ("tile", "row", "row")),
}


def _mm(a, b, *, dims="nn", epi="none", extra=(), out_dtype=F32, name, bm=1024, bn_cap=1024, bk_cap=2048,
        ride=None):
    if dims == "nn":
        (M, K), (K2, N) = a.shape, b.shape
    elif dims == "nt":
        (M, K), (N, K2) = a.shape, b.shape
    else:
        (K, M), (K2, N) = a.shape, b.shape
    assert K == K2, (a.shape, b.shape, dims)
    bm = _pick(M, bm)
    bn = _pick(N, bn_cap)
    bk = _pick(K, bk_cap)
    nk = K // bk
    dn = {"nn": _NN, "nt": _NT, "tn": _TN}[dims]
    a_spec = (pl.BlockSpec((bk, bm), lambda i, j, k: (k, i)) if dims == "tn"
              else pl.BlockSpec((bm, bk), lambda i, j, k: (i, k)))
    b_spec = (pl.BlockSpec((bn, bk), lambda i, j, k: (j, k)) if dims == "nt"
              else pl.BlockSpec((bk, bn), lambda i, j, k: (k, j)))
    o_spec = pl.BlockSpec((bm, bn), lambda i, j, k: (i, j))
    r_spec = pl.BlockSpec((1, bn), lambda i, j, k: (0, j))
    extra = list(extra) if isinstance(extra, (tuple, list)) else [extra]
    ekinds, okinds = _EPI[epi]
    assert len(extra) == len(ekinds) and (epi not in ("res_norm", "norm_bwd", "res_loss") or bn == N)
    n_extra, n_out = len(ekinds), len(okinds)
    n_ride = ride.n if ride else 0
    gi, gj = M // bm, N // bn

    def body(a_ref, b_ref, *rest):
        ex = rest[:n_extra]
        first = pl.program_id(0) == 0
        ride_in = rest[n_extra:n_extra + n_ride]
        outs = rest[n_extra + n_ride:n_extra + n_ride + n_out]
        ride_out = rest[n_extra + n_ride + n_out:n_extra + 2 * n_ride + n_out]
        if ride:
            at = lambda i, j, k: ((pl.program_id(0) == i) & (pl.program_id(1) == j) & (pl.program_id(2) == k))

            @pl.when(at(0, 0, 0))
            def _():
                ride.start(ride_in, ride_out, rest[-3:])

        def finish(r):
            if epi == "res":
                outs[0][...] = (r + ex[0][...].astype(F32)).astype(outs[0].dtype)
            elif epi == "mul2":
                outs[0][...] = (2.0 * r * ex[0][...].astype(F32)).astype(outs[0].dtype)
            elif epi == "relu2":
                u = jnp.maximum(r, 0.0)
                outs[0][...] = u.astype(outs[0].dtype)
                outs[1][...] = (u * u).astype(outs[1].dtype)
            elif epi == "res_norm":
                y = r + ex[0][...]
                outs[0][...] = y
                rstd = lax.rsqrt(jnp.mean(y * y, axis=1, keepdims=True) + EPS)
                outs[1][...] = (y * rstd * ex[1][...]).astype(outs[1].dtype)
            elif epi == "norm_bwd":
                xv = ex[0][...]
                rstd = lax.rsqrt(jnp.mean(xv * xv, axis=1, keepdims=True) + EPS)
                xh = xv * rstd
                dxh = r * ex[2][...]
                outs[0][...] = ex[1][...] + rstd * (dxh - xh * jnp.mean(dxh * xh, axis=1, keepdims=True))
                dw = jnp.sum(r * xh, axis=0, keepdims=True)

                @pl.when(first)
                def _():
                    outs[1][...] = dw

                @pl.when(jnp.logical_not(first))
                def _():
                    outs[1][...] += dw
            elif epi == "res_loss":
                y = r + ex[0][...]
                wv = ex[2][...]
                rstd = lax.rsqrt(jnp.mean(y * y, axis=1, keepdims=True) + EPS)
                yh = y * rstd
                err = yh * wv - ex[1][...]
                part_loss = 0.5 * jnp.sum(jnp.mean(err * err, axis=1, keepdims=True), axis=0, keepdims=True)
                dyn = err * (1.0 / N)
                dyh = dyn * wv
                outs[0][...] = rstd * (dyh - yh * jnp.mean(dyh * yh, axis=1, keepdims=True))
                dw = jnp.sum(dyn * yh, axis=0, keepdims=True)
                lrow = jnp.broadcast_to(part_loss, (1, N))

                @pl.when(first)
                def _():
                    outs[1][...] = dw
                    outs[2][...] = lrow

                @pl.when(jnp.logical_not(first))
                def _():
                    outs[1][...] += dw
                    outs[2][...] += lrow
            else:
                outs[0][...] = r.astype(outs[0].dtype)

        part = _dot(a_ref[...], b_ref[...], dn)
        if nk == 1:
            finish(part)
        else:
            acc = rest[n_extra + 2 * n_ride + n_out]
            k = pl.program_id(2)

            @pl.when(k == 0)
            def _():
                acc[...] = part

            @pl.when((k > 0) & (k < nk - 1))
            def _():
                acc[...] += part

            @pl.when(k == nk - 1)
            def _():
                finish(acc[...] + part)

        if ride:
            @pl.when(at(gi - 1, gj - 1, nk - 1))
            def _():
                ride.wait(ride_in, ride_out, rest[-3:])

    kind_spec = {"tile": o_spec, "row": r_spec}
    ins = [a, b] + [e.reshape(1, N) if k == "row" else e for e, k in zip(extra, ekinds)]
    in_specs = [a_spec, b_spec] + [kind_spec[k] for k in ekinds]
    out_dtypes = {"res_norm": (F32, BF16), "norm_bwd": (F32, F32), "res_loss": (F32, F32, F32)}.get(
        epi, (out_dtype,) * n_out)
    out_shape = [jax.ShapeDtypeStruct((M, N) if k == "tile" else (1, N), dt) for k, dt in zip(okinds, out_dtypes)]
    out_specs = [kind_spec[k] for k in okinds]
    scratch = [pltpu.VMEM((bm, bn), F32)] if nk > 1 else []
    sem = ("arbitrary" if epi in ("norm_bwd", "res_loss") else "parallel", "parallel", "arbitrary")
    if ride:
        ins, in_specs = ins + ride.srcs, in_specs + ride.specs
        out_shape, out_specs = out_shape + ride.out_shape, out_specs + ride.specs
        scratch, sem = scratch + ride.scratch, ("arbitrary",) * 3
    res = pl.pallas_call(
        body, grid=(gi, gj, nk), in_specs=in_specs, out_specs=out_specs, out_shape=out_shape,
        scratch_shapes=scratch, name=name, compiler_params=_params(sem))(*ins)
    main = res[:n_out] if n_out > 1 else res[0]
    return (main, list(res[n_out:])) if ride else main


def _rmsnorm_fwd(x, w, *, name, bt=256):
    T, Dm = x.shape
    bt = min(bt, T)

    def body(x_ref, w_ref, h_ref):
        xv = x_ref[...]
        r = lax.rsqrt(jnp.mean(xv * xv, axis=1, keepdims=True) + EPS)
        h_ref[...] = (xv * r * w_ref[...]).astype(h_ref.dtype)

    return pl.pallas_call(
        body, grid=(T // bt,),
        in_specs=[pl.BlockSpec((bt, Dm), lambda i: (i, 0)), pl.BlockSpec((1, Dm), lambda i: (0, 0))],
        out_specs=pl.BlockSpec((bt, Dm), lambda i: (i, 0)),
        out_shape=jax.ShapeDtypeStruct((T, Dm), BF16), name=name,
        compiler_params=_params(("parallel",)))(x, w.reshape(1, Dm))


def _rmsnorm_bwd(x, w, dh, dres, *, name, bt=256):
    T, Dm = x.shape
    bt = min(bt, T)
    has_res = dres is not None

    def body(x_ref, w_ref, dh_ref, *rest):
        dres_ref = rest[0] if has_res else None
        dx_ref, dw_ref = rest[-2], rest[-1]
        i = pl.program_id(0)
        xv = x_ref[...]
        r = lax.rsqrt(jnp.mean(xv * xv, axis=1, keepdims=True) + EPS)
        xh = xv * r
        dhv = dh_ref[...].astype(F32)
        dxh = dhv * w_ref[...]
        dx = r * (dxh - xh * jnp.mean(dxh * xh, axis=1, keepdims=True))
        if has_res:
            dx = dx + dres_ref[...]
        dx_ref[...] = dx

        @pl.when(i == 0)
        def _():
            dw_ref[...] = jnp.zeros_like(dw_ref)

        dw_ref[...] += jnp.sum(dhv * xh, axis=0, keepdims=True)

    row = pl.BlockSpec((bt, Dm), lambda i: (i, 0))
    vec = pl.BlockSpec((1, Dm), lambda i: (0, 0))
    ins = [x, w.reshape(1, Dm), dh] + ([dres] if has_res else [])
    dx, dw = pl.pallas_call(
        body, grid=(T // bt,), in_specs=[row, vec, row] + ([row] if has_res else []),
        out_specs=[row, vec],
        out_shape=[jax.ShapeDtypeStruct((T, Dm), F32), jax.ShapeDtypeStruct((1, Dm), F32)],
        name=name, compiler_params=_params(("arbitrary",)))(*ins)
    return dx, dw.reshape(Dm)


def _attn_fwd(q, km, vm, *, bt=256):
    T = q.shape[0]
    M = km.shape[0]
    bt = min(bt, T)
    scale = MEM_HD ** -0.5

    def body(q_ref, k_ref, v_ref, o_ref):
        for h in range(MEM_H):
            sl = slice(h * MEM_HD, (h + 1) * MEM_HD)
            s = _dot(q_ref[:, sl], k_ref[:, sl], _NT) * scale
            s = s - jnp.max(s, axis=1, keepdims=True)
            e = jnp.exp(s)
            p = e / jnp.sum(e, axis=1, keepdims=True)
            o_ref[:, sl] = _dot(p, v_ref[:, sl]).astype(o_ref.dtype)

    row = pl.BlockSpec((bt, D), lambda i: (i, 0))
    mem = pl.BlockSpec((M, D), lambda i: (0, 0))
    return pl.pallas_call(
        body, grid=(T // bt,), in_specs=[row, mem, mem], out_specs=row,
        out_shape=jax.ShapeDtypeStruct((T, D), BF16), name="attn_fwd",
        compiler_params=_params(("parallel",)))(q, km, vm)


def _attn_bwd(q, km, vm, do, *, bt=256):
    T = q.shape[0]
    M = km.shape[0]
    bt = min(bt, T)
    scale = MEM_HD ** -0.5

    def body(q_ref, k_ref, v_ref, do_ref, dq_ref, dk_ref, dv_ref):
        i = pl.program_id(0)

        @pl.when(i == 0)
        def _():
            dk_ref[...] = jnp.zeros_like(dk_ref)
            dv_ref[...] = jnp.zeros_like(dv_ref)

        sls = [slice(h * MEM_HD, (h + 1) * MEM_HD) for h in range(MEM_H)]
        ss = [_dot(q_ref[:, sl], k_ref[:, sl], _NT) * scale for sl in sls]
        dps = [_dot(do_ref[:, sl], v_ref[:, sl], _NT) for sl in sls]
        es = [jnp.exp(s - jnp.max(s, axis=1, keepdims=True)) for s in ss]
        ps = [e / jnp.sum(e, axis=1, keepdims=True) for e in es]
        dss = [p * (dp - jnp.sum(dp * p, axis=1, keepdims=True)) * scale for p, dp in zip(ps, dps)]
        for sl, p, ds in zip(sls, ps, dss):
            dq_ref[:, sl] = _dot(ds, k_ref[:, sl]).astype(dq_ref.dtype)
            dk_ref[:, sl] += _dot(ds, q_ref[:, sl], _TN)
            dv_ref[:, sl] += _dot(p, do_ref[:, sl], _TN)

    row = pl.BlockSpec((bt, D), lambda i: (i, 0))
    mem = pl.BlockSpec((M, D), lambda i: (0, 0))
    return pl.pallas_call(
        body, grid=(T // bt,), in_specs=[row, mem, mem, row], out_specs=[row, mem, mem],
        out_shape=[jax.ShapeDtypeStruct((T, D), BF16), jax.ShapeDtypeStruct((M, D), F32),
                   jax.ShapeDtypeStruct((M, D), F32)],
        name="attn_bwd", compiler_params=_params(("arbitrary",)))(q, km, vm, do)


def _conv_apply(halo, x, w_ref, b_ref):
    bt, hr = x.shape[0], halo.shape[0]
    cat = jnp.concatenate([halo, x], axis=0)
    y = x * w_ref[3:4, :]
    for k in range(CONV_K - 1):
        y = y + pltpu.roll(cat, CONV_K - 1 - k, 0)[hr:hr + bt] * w_ref[k:k + 1, :]
    if b_ref is not None:
        y = y + b_ref[...]
    return y


def _l2_parts(act, bc):
    out = []
    for s in range(bc // 128):
        a = act[:, s * 128:(s + 1) * 128]
        r = lax.rsqrt(jnp.sum(a * a, axis=1, keepdims=True) + EPS)
        out.append((a, r))
    return out


def _conv_fwd(p, col0, C, w, b, *, l2, name, bt=512, bc=1024):
    T = p.shape[0]
    bt = min(bt, T)
    c0, hb = col0 // bc, bt // P_HALO
    has_b = b is not None
    assert not l2 or (bc == D and C == 2 * D)

    def body(x_ref, halo_ref, w_ref, *rest):
        b_ref = rest[0] if has_b else None
        o_ref = rest[-1]
        i, j = pl.program_id(0), pl.program_id(1)
        x = x_ref[...].astype(F32)
        halo = jnp.where(i > 0, halo_ref[...].astype(F32), 0.0)
        act = _silu(_conv_apply(halo, x, w_ref, b_ref))
        if l2:
            sc = jnp.where(j == 0, GDN_DK ** -0.5, 1.0)
            o_ref[...] = jnp.concatenate([a * (r * sc) for a, r in _l2_parts(act, bc)], axis=1)
        else:
            o_ref[...] = act

    in_specs = [pl.BlockSpec((bt, bc), lambda i, j: (i, c0 + j)),
                pl.BlockSpec((P_HALO, bc), lambda i, j: (jnp.maximum(i * hb - 1, 0), c0 + j)),
                pl.BlockSpec((CONV_K, bc), lambda i, j: (0, j))]
    ins = [p, p, w]
    if has_b:
        in_specs.append(pl.BlockSpec((1, bc), lambda i, j: (0, j)))
        ins.append(b.reshape(1, C))
    return pl.pallas_call(
        body, grid=(T // bt, C // bc), in_specs=in_specs,
        out_specs=pl.BlockSpec((bt, bc), lambda i, j: (i, j)),
        out_shape=jax.ShapeDtypeStruct((T, C), F32), name=name,
        compiler_params=_params(("parallel", "parallel")))(*ins)


def _conv_bwd_act(p, col0, C, w, b, dact, dcol0, *, l2, name, bt=512, bc=1024):
    T = p.shape[0]
    bt = min(bt, T)
    c0, d0, hb = col0 // bc, dcol0 // bc, bt // P_HALO
    has_b = b is not None
    assert not l2 or (bc == D and C == 2 * D)

    def body(x_ref, halo_ref, w_ref, *rest):
        b_ref = rest[0] if has_b else None
        dact_ref, dy_ref, dw_ref, db_ref = rest[-4:]
        j, i = pl.program_id(0), pl.program_id(1)
        x = x_ref[...].astype(F32)
        halo = jnp.where(i > 0, halo_ref[...].astype(F32), 0.0)
        y = _conv_apply(halo, x, w_ref, b_ref)
        dact = dact_ref[...]
        sg = _sigmoid(y)
        if l2:
            sc = jnp.where(j == 0, GDN_DK ** -0.5, 1.0)
            parts = []
            for s, (a, r) in enumerate(_l2_parts(y * sg, bc)):
                n = a * r
                dn = dact[:, s * 128:(s + 1) * 128]
                parts.append((r * sc) * (dn - n * jnp.sum(dn * n, axis=1, keepdims=True)))
            dact = jnp.concatenate(parts, axis=1)
        dy = dact * (sg * (1.0 + y * (1.0 - sg)))
        dy_ref[...] = dy

        @pl.when(i == 0)
        def _():
            dw_ref[...] = jnp.zeros_like(dw_ref)
            db_ref[...] = jnp.zeros_like(db_ref)

        db_ref[...] += jnp.sum(dy, axis=0, keepdims=True)
        cat = jnp.concatenate([halo, x], axis=0)
        dw_ref[3:4, :] += jnp.sum(dy * x, axis=0, keepdims=True)
        for k in range(CONV_K - 1):
            xs = pltpu.roll(cat, CONV_K - 1 - k, 0)[P_HALO:P_HALO + bt]
            dw_ref[k:k + 1, :] += jnp.sum(dy * xs, axis=0, keepdims=True)

    in_specs = [pl.BlockSpec((bt, bc), lambda j, i: (i, c0 + j)),
                pl.BlockSpec((P_HALO, bc), lambda j, i: (jnp.maximum(i * hb - 1, 0), c0 + j)),
                pl.BlockSpec((CONV_K, bc), lambda j, i: (0, j))]
    ins = [p, p, w]
    if has_b:
        in_specs.append(pl.BlockSpec((1, bc), lambda j, i: (0, j)))
        ins.append(b.reshape(1, C))
    in_specs.append(pl.BlockSpec((bt, bc), lambda j, i: (i, d0 + j)))
    ins.append(dact)
    dy, dw, db = pl.pallas_call(
        body, grid=(C // bc, T // bt), in_specs=in_specs,
        out_specs=[pl.BlockSpec((bt, bc), lambda j, i: (i, j)),
                   pl.BlockSpec((CONV_K, bc), lambda j, i: (0, j)),
                   pl.BlockSpec((1, bc), lambda j, i: (0, j))],
        out_shape=[jax.ShapeDtypeStruct((T, C), F32), jax.ShapeDtypeStruct((CONV_K, C), F32),
                   jax.ShapeDtypeStruct((1, C), F32)],
        name=name, compiler_params=_params(("parallel", "arbitrary")))(*ins)
    return dy, dw, db.reshape(C)


def _conv_bwd_in(dy, w, dp_in, col0, T, *, name, bt=512, bc=1024):
    C = dy.shape[1]
    bt = min(bt, T)
    c0, hb, nb = col0 // bc, bt // 8, T // bt

    def body(dy_ref, nxt_ref, w_ref, *rest):
        o_ref = rest[-1]
        i = pl.program_id(0)
        dy_v = dy_ref[...]
        nxt = jnp.where(i < nb - 1, nxt_ref[...], 0.0)
        cat = jnp.concatenate([dy_v, nxt], axis=0)
        dx = dy_v * w_ref[3:4, :]
        for k in range(CONV_K - 1):
            s = CONV_K - 1 - k
            dx = dx + pltpu.roll(cat, bt + 8 - s, 0)[0:bt] * w_ref[k:k + 1, :]
        o_ref[...] = dx.astype(o_ref.dtype)

    in_specs = [pl.BlockSpec((bt, bc), lambda i, j: (i, j)),
                pl.BlockSpec((8, bc), lambda i, j: (jnp.minimum((i + 1) * hb, T // 8 - 1), j)),
                pl.BlockSpec((CONV_K, bc), lambda i, j: (0, j))]
    ins = [dy, dy, w]
    alias = {}
    if dp_in is not None:
        in_specs.append(pl.BlockSpec(memory_space=pl.ANY))
        ins.append(dp_in)
        alias = {3: 0}
    return pl.pallas_call(
        body, grid=(nb, C // bc), in_specs=in_specs,
        out_specs=pl.BlockSpec((bt, bc), lambda i, j: (i, c0 + j)),
        out_shape=jax.ShapeDtypeStruct((T, C_TOT), BF16), input_output_aliases=alias, name=name,
        compiler_params=_params(("parallel", "parallel")))(*ins)


def _expand_mats(shift, row0):
    e = (_iota2((128, D), 0) - row0 == (_iota2((128, D), 1) >> shift)).astype(F32)
    et = ((_iota2((D, 128), 0) >> shift) == _iota2((D, 128), 1) - row0).astype(F32)
    return e, et


def _cum_mats(chunk):
    ri, ci = _iota2((chunk, chunk), 0), _iota2((chunk, chunk), 1)
    return (ri >= ci).astype(F32), (ri <= ci).astype(F32)


def _gdn_gates_fwd(p, alog_row, dtb_row, *, bt=256):
    T = p.shape[0]
    bt = min(bt, T)

    def body(g_ref, al_ref, db_ref, bg_ref):
        gt = g_ref[...]
        lc, _ = _cum_mats(GDN_C)
        g_l = -jnp.exp(al_ref[...]) * _softplus(gt + db_ref[...])
        bg_ref[...] = jnp.where(_iota2((bt, 128), 1) < GDN_H, _sigmoid(gt), _chunk_cumsum(g_l, lc, GDN_C))

    vec = pl.BlockSpec((1, 128), lambda i: (0, 0))
    seg = pl.BlockSpec((bt, 128), lambda i: (i, 0))
    return pl.pallas_call(
        body, grid=(T // bt,), in_specs=[seg, vec, vec], out_specs=seg,
        out_shape=jax.ShapeDtypeStruct((T, 128), F32), name="gdn_gates_fwd",
        compiler_params=_params(("parallel",)))(p, alog_row, dtb_row)


def _gdn_gates_bwd(p, alog_row, dtb_row, dbg, dp_in, *, bt=256):
    T = p.shape[0]
    bt = min(bt, T)

    def body(g_ref, al_ref, db_ref, dbg_ref, dpin_ref, dg_out, dal_ref, ddb_ref):
        i = pl.program_id(0)
        gt = g_ref[...]
        lane = _iota2((bt, 128), 1)
        _, uc = _cum_mats(GDN_C)
        ea = jnp.exp(al_ref[...])
        zz = gt + db_ref[...]
        g_l = -ea * _softplus(zz)
        beta_l = _sigmoid(gt)
        dbg_v = dbg_ref[...]
        dg_l = jnp.where((lane >= GDN_H) & (lane < 2 * GDN_H), _chunk_cumsum(dbg_v, uc, GDN_C), 0.0)
        dbeta_l = jnp.where(lane < GDN_H, dbg_v, 0.0)
        da = dg_l * (-ea) * _sigmoid(zz)
        dg_out[...] = (da + dbeta_l * beta_l * (1.0 - beta_l)).astype(dg_out.dtype)

        @pl.when(i == 0)
        def _():
            dal_ref[...] = jnp.zeros_like(dal_ref)
            ddb_ref[...] = jnp.zeros_like(ddb_ref)

        dal_ref[...] += jnp.sum(dg_l * g_l, axis=0, keepdims=True)
        ddb_ref[...] += jnp.sum(da, axis=0, keepdims=True)

    vec = pl.BlockSpec((1, 128), lambda i: (0, 0))
    seg = pl.BlockSpec((bt, 128), lambda i: (i, 0))
    gate = pl.BlockSpec((bt, 128), lambda i: (i, C_GATE // 128))
    return pl.pallas_call(
        body, grid=(T // bt,), in_specs=[seg, vec, vec, seg, _ANY], out_specs=[gate, vec, vec],
        out_shape=[jax.ShapeDtypeStruct((T, C_TOT), BF16), jax.ShapeDtypeStruct((1, 128), F32),
                   jax.ShapeDtypeStruct((1, 128), F32)],
        input_output_aliases={4: 0}, name="gdn_gates_bwd",
        compiler_params=_params(("arbitrary",)))(p, alog_row, dtb_row, dbg, dp_in)


def _ssd_dt_fwd(p, dtb_row, alog_x, *, bt=256):
    T = p.shape[0]
    bt = min(bt, T)

    def body(d_ref, db_ref, al_ref, dt_ref, alpha_ref):
        ed, _ = _expand_mats(6, 0)
        lc, _ = _cum_mats(SSM_L)
        dt_x = _dot_sel(_softplus(d_ref[...] + db_ref[...]), ed)
        dt_ref[...] = dt_x
        alpha_ref[...] = _chunk_cumsum(dt_x * (-jnp.exp(al_ref[...])), lc, SSM_L)

    row = pl.BlockSpec((bt, D), lambda i: (i, 0))
    return pl.pallas_call(
        body, grid=(T // bt,),
        in_specs=[pl.BlockSpec((bt, 128), lambda i: (i, 1)),
                  pl.BlockSpec((1, 128), lambda i: (0, 0)), pl.BlockSpec((1, D), lambda i: (0, 0))],
        out_specs=[row, row], out_shape=[jax.ShapeDtypeStruct((T, D), F32)] * 2,
        name="ssd_dt_fwd", compiler_params=_params(("parallel",)))(p, dtb_row, alog_x)


def _ssd_dt_bwd(p, dtb_row, alog_x, ddt_x, dalpha_x, dp_in, *, bt=256):
    T = p.shape[0]
    bt = min(bt, T)

    def body(d_ref, db_ref, al_ref, ddt_ref, dal_ref, dpin_ref, dd_out, ddb_ref, dalog_ref):
        i = pl.program_id(0)
        ed, edt = _expand_mats(6, 0)
        _, uc = _cum_mats(SSM_L)
        zz = d_ref[...] + db_ref[...]
        dt_x = _dot_sel(_softplus(zz), ed)
        a_x = -jnp.exp(al_ref[...])
        da_x = _chunk_cumsum(dal_ref[...], uc, SSM_L)
        ddt_l = _dot_sel(ddt_ref[...] + da_x * a_x, edt)
        draw = ddt_l * _sigmoid(zz)
        dd_out[...] = draw.astype(dd_out.dtype)

        @pl.when(i == 0)
        def _():
            ddb_ref[...] = jnp.zeros_like(ddb_ref)
            dalog_ref[...] = jnp.zeros_like(dalog_ref)

        ddb_ref[...] += jnp.sum(draw, axis=0, keepdims=True)
        dalog_ref[...] += jnp.sum(da_x * dt_x, axis=0, keepdims=True) * a_x

    row = pl.BlockSpec((bt, D), lambda i: (i, 0))
    seg = pl.BlockSpec((bt, 128), lambda i: (i, C_DT // 128))
    v128 = pl.BlockSpec((1, 128), lambda i: (0, 0))
    vD = pl.BlockSpec((1, D), lambda i: (0, 0))
    return pl.pallas_call(
        body, grid=(T // bt,),
        in_specs=[pl.BlockSpec((bt, 128), lambda i: (i, 1)), v128, vD, row, row, _ANY],
        out_specs=[seg, v128, vD],
        out_shape=[jax.ShapeDtypeStruct((T, C_TOT), BF16), jax.ShapeDtypeStruct((1, 128), F32),
                   jax.ShapeDtypeStruct((1, D), F32)],
        input_output_aliases={5: 0}, name="ssd_dt_bwd",
        compiler_params=_params(("arbitrary",)))(p, dtb_row, alog_x, ddt_x, dalpha_x, dp_in)


def _gdn_post_fwd(o, p, w_x, *, bt=256):
    T = o.shape[0]
    bt = min(bt, T)

    def body(o_ref, z_ref, w_ref, out_ref):
        for h in range(GDN_H):
            sl = slice(h * 128, (h + 1) * 128)
            oh = o_ref[:, sl]
            r = lax.rsqrt(jnp.mean(oh * oh, axis=1, keepdims=True) + EPS)
            out_ref[:, sl] = (oh * r * w_ref[:, sl] * _silu(z_ref[:, sl].astype(F32))).astype(out_ref.dtype)

    row = pl.BlockSpec((bt, D), lambda i: (i, 0))
    return pl.pallas_call(
        body, grid=(T // bt,),
        in_specs=[row, pl.BlockSpec((bt, D), lambda i: (i, C_ZG // D)), pl.BlockSpec((1, D), lambda i: (0, 0))],
        out_specs=row, out_shape=jax.ShapeDtypeStruct((T, 2 * D), BF16), name="gdn_post_fwd",
        compiler_params=_params(("parallel",)))(o, p, w_x)


def _gdn_post_bwd(dmix, o, p, w_x, *, bt=256):
    T = o.shape[0]
    bt = min(bt, T)

    def body(dm_ref, o_ref, z_ref, w_ref, do_ref, dz_ref, dw_ref):
        i = pl.program_id(0)

        @pl.when(i == 0)
        def _():
            dw_ref[...] = jnp.zeros_like(dw_ref)

        for h in range(GDN_H):
            sl = slice(h * 128, (h + 1) * 128)
            oh, zh, wh, dm = o_ref[:, sl], z_ref[:, sl].astype(F32), w_ref[:, sl], dm_ref[:, sl]
            r = lax.rsqrt(jnp.mean(oh * oh, axis=1, keepdims=True) + EPS)
            ohat = oh * r
            dy = dm * _silu(zh)
            dz_ref[:, sl] = (dm * ohat * wh * _dsilu(zh)).astype(dz_ref.dtype)
            dohat = dy * wh
            do_ref[:, sl] = r * (dohat - ohat * jnp.mean(dohat * ohat, axis=1, keepdims=True))
            dw_ref[:, sl] += jnp.sum(dy * ohat, axis=0, keepdims=True)

    row = pl.BlockSpec((bt, D), lambda i: (i, 0))
    zcol = pl.BlockSpec((bt, D), lambda i: (i, C_ZG // D))
    vec = pl.BlockSpec((1, D), lambda i: (0, 0))
    return pl.pallas_call(
        body, grid=(T // bt,), in_specs=[row, row, zcol, vec], out_specs=[row, zcol, vec],
        out_shape=[jax.ShapeDtypeStruct((T, D), F32), jax.ShapeDtypeStruct((T, C_TOT), BF16),
                   jax.ShapeDtypeStruct((1, D), F32)],
        name="gdn_post_bwd", compiler_params=_params(("arbitrary",)))(dmix, o, p, w_x)


def _ssd_post_fwd(y, xs, p, d_x, w, mix_in, *, bt=256):
    T = y.shape[0]
    bt = min(bt, T)

    def body(y_ref, x_ref, z_ref, d_ref, w_ref, mix_ref, out_ref):
        yg = (y_ref[...] + x_ref[...] * d_ref[...]) * _silu(z_ref[...].astype(F32))
        for g in range(2):
            sl = slice(g * 512, (g + 1) * 512)
            a = yg[:, sl]
            r = lax.rsqrt(jnp.mean(a * a, axis=1, keepdims=True) + EPS)
            out_ref[:, sl] = (a * r * w_ref[:, sl]).astype(out_ref.dtype)

    row = pl.BlockSpec((bt, D), lambda i: (i, 0))
    vec = pl.BlockSpec((1, D), lambda i: (0, 0))
    return pl.pallas_call(
        body, grid=(T // bt,),
        in_specs=[row, row, pl.BlockSpec((bt, D), lambda i: (i, C_ZS // D)), vec, vec, _ANY],
        out_specs=pl.BlockSpec((bt, D), lambda i: (i, 1)), out_shape=jax.ShapeDtypeStruct((T, 2 * D), BF16),
        input_output_aliases={5: 0}, name="ssd_post_fwd",
        compiler_params=_params(("parallel",)))(y, xs, p, d_x, w, mix_in)


def _ssd_post_bwd(dmix, y, xs, p, d_x, w, dp_in, *, bt=256):
    T = y.shape[0]
    bt = min(bt, T)

    def body(dm_ref, y_ref, x_ref, z_ref, d_ref, w_ref, dpin_ref, dyy_ref, dz_ref, dd_ref, dw_ref):
        i = pl.program_id(0)

        @pl.when(i == 0)
        def _():
            dd_ref[...] = jnp.zeros_like(dd_ref)
            dw_ref[...] = jnp.zeros_like(dw_ref)

        xv, zv = x_ref[...], z_ref[...].astype(F32)
        yy = y_ref[...] + xv * d_ref[...]
        sz = _silu(zv)
        yg = yy * sz
        parts = []
        for g in range(2):
            sl = slice(g * 512, (g + 1) * 512)
            a = yg[:, sl]
            r = lax.rsqrt(jnp.mean(a * a, axis=1, keepdims=True) + EPS)
            ah = a * r
            dout = dm_ref[:, sl]
            dah = dout * w_ref[:, sl]
            dw_ref[:, sl] += jnp.sum(dout * ah, axis=0, keepdims=True)
            parts.append(r * (dah - ah * jnp.mean(dah * ah, axis=1, keepdims=True)))
        dyg = jnp.concatenate(parts, axis=1)
        dyy = dyg * sz
        dyy_ref[...] = dyy
        dz_ref[...] = (dyg * yy * _dsilu(zv)).astype(dz_ref.dtype)
        dd_ref[...] += jnp.sum(dyy * xv, axis=0, keepdims=True)

    row = pl.BlockSpec((bt, D), lambda i: (i, 0))
    zcol = pl.BlockSpec((bt, D), lambda i: (i, C_ZS // D))
    vec = pl.BlockSpec((1, D), lambda i: (0, 0))
    return pl.pallas_call(
        body, grid=(T // bt,),
        in_specs=[pl.BlockSpec((bt, D), lambda i: (i, 1)), row, row, zcol, vec, vec, _ANY],
        out_specs=[row, zcol, vec, vec],
        out_shape=[jax.ShapeDtypeStruct((T, D), F32), jax.ShapeDtypeStruct((T, C_TOT), BF16),
                   jax.ShapeDtypeStruct((1, D), F32), jax.ShapeDtypeStruct((1, D), F32)],
        input_output_aliases={6: 1}, name="ssd_post_bwd",
        compiler_params=_params(("arbitrary",)))(dmix, y, xs, p, d_x, w, dp_in)


_NEG = -1e30


def _gdn_terms(q, k, v, bx, gam_c):
    C = GDN_C
    ri, ci = _iota2((C, C), 0), _iota2((C, C), 1)
    eye, low, strict = ri == ci, ri >= ci, ri > ci
    gam_r = jnp.sum(jnp.where(eye, gam_c, 0.0), axis=0, keepdims=True)
    G = jnp.exp(jnp.where(low, gam_c - gam_r, _NEG))
    glast = jnp.sum(jnp.where(_iota2((C, 1), 0) == C - 1, gam_c, 0.0), axis=0, keepdims=True)
    eg, egl, eL = jnp.exp(gam_c), jnp.exp(glast - gam_c), jnp.exp(glast)
    kb, vb = k * bx, v * bx
    M = _dot(kb, k, _NT)
    return dict(eye=eye, low=low, strict=strict, G=G, eg=eg, egl=egl, eL=eL, kb=kb, vb=vb, M=M,
                kbg=kb * eg, qd=q * eg, kd=k * egl, q=q, k=k, v=v, bx=bx)


def _split(a):
    hi = a.astype(_MXU)
    return hi, (a - hi.astype(F32)).astype(_MXU)


def _dot3s(a, b):
    d = lambda p, q: lax.dot_general(p, q, _NN, preferred_element_type=F32)
    return d(a[0], b[0]) + d(a[0], b[1]) + d(a[1], b[0])


def _tri_inv_many(Ls, eye):
    eyef = jnp.where(eye, 1.0, 0.0)
    Ts = [eyef - L for L in Ls]
    Ps = [-L for L in Ls]
    for _ in range(5):
        sp = [_split(p) for p in Ps]
        Ps = [_dot3s(s, s) for s in sp]
        sp = [_split(p) for p in Ps]
        st = [_split(t) for t in Ts]
        Ts = [t + _dot3s(a, b) for t, a, b in zip(Ts, st, sp)]
    return Ts


def _lane_col(tile, idx):
    return jnp.sum(jnp.where(_iota2(tile.shape, 1) == idx, tile, 0.0), axis=1, keepdims=True)


def _gdn_heads(q_ref, k_ref, v_ref, bg_ref, heads):
    out = []
    bg = bg_ref[...]
    for h in heads:
        sl = slice(h * 128, (h + 1) * 128)
        out.append(_gdn_terms(q_ref[:, sl], k_ref[:, sl], v_ref[:, sl], _lane_col(bg, h), _lane_col(bg, GDN_H + h)))
    return out


def _gdn_prep(qk, v, bg, ride=None):
    T = qk.shape[0]
    N = T // GDN_C
    C, CS = GDN_C, GDN_LOCAL_CHUNKS
    NB = N // CS
    n_ride = ride.n if ride else 0

    def body(q_ref, k_ref, v_ref, bg_ref, *rest):
        ride_in = rest[:n_ride]
        u_ref, w_ref, qd_ref, kd_ref, p_ref, t_ref = rest[n_ride:n_ride + 6]
        ride_out = rest[n_ride + 6:2 * n_ride + 6]
        if ride:
            @pl.when(pl.program_id(0) == 0)
            def _():
                ride.start(ride_in, ride_out, rest[-3:])

            @pl.when(pl.program_id(0) == NB - 1)
            def _():
                ride.wait(ride_in, ride_out, rest[-3:])

        items = [(c, h) for c in range(CS) for h in range(GDN_H)]
        views = [[r.at[pl.ds(c * C, C)] for r in (q_ref, k_ref, v_ref, bg_ref)] for c in range(CS)]
        ts = [_gdn_heads(*views[c], [h])[0] for c, h in items]
        Ts = _tri_inv_many([jnp.where(t["strict"], t["M"] * t["G"], 0.0) for t in ts], ts[0]["eye"])
        for (c, h), t, Tm in zip(items, ts, Ts):
            tok = slice(c * C, (c + 1) * C)
            sl = slice(h * 128, (h + 1) * 128)
            rows = slice(h * C, (h + 1) * C)
            u_ref[tok, sl] = _dot(Tm, t["vb"])
            w_ref[tok, sl] = _dot(Tm, t["kbg"]).astype(w_ref.dtype)
            qd_ref[tok, sl] = t["qd"].astype(qd_ref.dtype)
            kd_ref[tok, sl] = t["kd"].astype(kd_ref.dtype)
            p_ref[c, rows, :] = _dot(t["q"], t["k"], _NT) * t["G"]
            t_ref[c, rows, :] = Tm

    blk = lambda c: pl.BlockSpec((CS * C, D), lambda n: (n, c))
    sq = pl.BlockSpec((CS, GDN_H * C, C), lambda n: (n, 0, 0))
    in_specs = [blk(0), blk(1), blk(0), pl.BlockSpec((CS * C, 128), lambda n: (n, 0))]
    out_specs = [blk(0), blk(0), blk(0), blk(0), sq, sq]
    out_shape = [jax.ShapeDtypeStruct((T, D), F32), jax.ShapeDtypeStruct((T, D), BF16),
                 jax.ShapeDtypeStruct((T, D), BF16), jax.ShapeDtypeStruct((T, D), BF16),
                 jax.ShapeDtypeStruct((N, GDN_H * C, C), F32), jax.ShapeDtypeStruct((N, GDN_H * C, C), F32)]
    ins = [qk, qk, v, bg]
    if ride:
        ins, in_specs = ins + ride.srcs, in_specs + ride.specs
        out_shape, out_specs = out_shape + ride.out_shape, out_specs + ride.specs
    res = pl.pallas_call(
        body, grid=(NB,), in_specs=in_specs, out_specs=out_specs, out_shape=out_shape,
        scratch_shapes=ride.scratch if ride else [], name="gdn_prep",
        compiler_params=_params(("arbitrary",) if ride else ("parallel",)))(*ins)
    return (list(res[:6]), list(res[6:])) if ride else list(res)


def _gdn_scan_fwd(u, w, qd, kd, pm, bg):
    T = u.shape[0]
    N = T // GDN_C
    C, CS = GDN_C, GDN_SCAN_CHUNKS

    def body(u_ref, w_ref, qd_ref, kd_ref, p_ref, bg_ref, o_ref, vn_ref, ss_ref, S_scr):
        n = pl.program_id(0)

        @pl.when(n == 0)
        def _():
            S_scr[...] = jnp.zeros_like(S_scr)

        sls = [slice(h * 128, (h + 1) * 128) for h in range(GDN_H)]
        for c in range(CS):
            rows = slice(c * C, (c + 1) * C)
            glast = bg_ref[(c + 1) * C - 1:(c + 1) * C, :]
            Ss = [S_scr[:, sl] for sl in sls]
            vns = [u_ref[rows, sl] - _dot(w_ref[rows, sl], S) for sl, S in zip(sls, Ss)]
            for h, (sl, S, vn) in enumerate(zip(sls, Ss, vns)):
                ss_ref[c, :, sl] = S.astype(ss_ref.dtype)
                vn_ref[rows, sl] = vn.astype(vn_ref.dtype)
                o_ref[rows, sl] = _dot(qd_ref[rows, sl], S) + _dot(p_ref[c, h * C:(h + 1) * C, :], vn)
                S_scr[:, sl] = S * jnp.exp(_lane_col(glast, GDN_H + h)) + _dot(kd_ref[rows, sl], vn, _TN)

    blk = pl.BlockSpec((CS * C, D), lambda n: (n, 0))
    return dict(
        body=body, steps=N // CS, ins=[u, w, qd, kd, pm, bg],
        in_specs=[blk, blk, blk, blk, pl.BlockSpec((CS, GDN_H * C, C), lambda n: (n, 0, 0)),
                  pl.BlockSpec((CS * C, 128), lambda n: (n, 0))],
        out_specs=[blk, blk, pl.BlockSpec((CS, GDN_DK, D), lambda n: (n, 0, 0))],
        out_shape=[jax.ShapeDtypeStruct((T, D), F32), jax.ShapeDtypeStruct((T, D), BF16),
                   jax.ShapeDtypeStruct((N, GDN_DK, D), BF16)],
        scratch=[pltpu.VMEM((GDN_DK, D), F32)])


def _gdn_scan_bwd(w, qd, kd, pm, bg, do):
    T = w.shape[0]
    N = T // GDN_C
    C, CS = GDN_C, GDN_SCAN_CHUNKS
    NB = N // CS

    def body(w_ref, qd_ref, kd_ref, p_ref, bg_ref, do_ref, dvn_ref, ds_ref, dS_scr):
        n = pl.program_id(0)

        @pl.when(n == 0)
        def _():
            dS_scr[...] = jnp.zeros_like(dS_scr)

        sls = [slice(h * 128, (h + 1) * 128) for h in range(GDN_H)]
        for c in reversed(range(CS)):
            rows = slice(c * C, (c + 1) * C)
            glast = bg_ref[(c + 1) * C - 1:(c + 1) * C, :]
            dSs = [dS_scr[:, sl] for sl in sls]
            dvns = [_dot(p_ref[c, h * C:(h + 1) * C, :], do_ref[rows, sl], _TN) + _dot(kd_ref[rows, sl], dS2)
                    for h, (sl, dS2) in enumerate(zip(sls, dSs))]
            for h, (sl, dS2, dvn) in enumerate(zip(sls, dSs, dvns)):
                ds_ref[c, :, sl] = dS2.astype(ds_ref.dtype)
                dvn_ref[rows, sl] = dvn.astype(dvn_ref.dtype)
                dS_scr[:, sl] = (dS2 * jnp.exp(_lane_col(glast, GDN_H + h))
                                 + _dot(qd_ref[rows, sl], do_ref[rows, sl], _TN) - _dot(w_ref[rows, sl], dvn, _TN))

    blk = pl.BlockSpec((CS * C, D), lambda n: (NB - 1 - n, 0))
    return dict(
        body=body, steps=NB, ins=[w, qd, kd, pm, bg, do],
        in_specs=[blk, blk, blk, pl.BlockSpec((CS, GDN_H * C, C), lambda n: (NB - 1 - n, 0, 0)),
                  pl.BlockSpec((CS * C, 128), lambda n: (NB - 1 - n, 0)), blk],
        out_specs=[blk, pl.BlockSpec((CS, GDN_DK, D), lambda n: (NB - 1 - n, 0, 0))],
        out_shape=[jax.ShapeDtypeStruct((T, D), BF16), jax.ShapeDtypeStruct((N, GDN_DK, D), BF16)],
        scratch=[pltpu.VMEM((GDN_DK, D), F32)])


def _gdn_rest_bwd(qk, v, bg, s_save, t_save, vn, dvn, ds_save, do, ride=None):
    T = qk.shape[0]
    N = T // GDN_C
    C, CS = GDN_C, GDN_REST_CHUNKS
    NB = N // CS
    n_ride = ride.n if ride else 0

    def body(q_ref, k_ref, v_ref, bg_ref, ss_ref, ts_ref, vn_ref, dvn_ref, ds_ref, do_ref, *rest):
        ride_in = rest[:n_ride]
        dqkv_ref, dbg_ref = rest[n_ride:n_ride + 2]
        ride_out = rest[n_ride + 2:2 * n_ride + 2]
        if ride:
            @pl.when(pl.program_id(0) == 0)
            def _():
                ride.start(ride_in, ride_out, rest[-3:])

            @pl.when(pl.program_id(0) == NB - 1)
            def _():
                ride.wait(ride_in, ride_out, rest[-3:])

        items = [(c, h) for c in range(CS) for h in range(GDN_H)]
        toks = [slice(c * C, (c + 1) * C) for c, _ in items]
        sls = [slice(h * 128, (h + 1) * 128) for _, h in items]
        views = [[r.at[pl.ds(c * C, C)] for r in (q_ref, k_ref, v_ref, bg_ref)] for c in range(CS)]
        ts = [_gdn_heads(*views[c], [h])[0] for c, h in items]
        Ss = [ss_ref[c, :, sl] for (c, _), sl in zip(items, sls)]
        Tms = [ts_ref[c, h * C:(h + 1) * C, :] for c, h in items]
        dS2s = [ds_ref[c, :, sl] for (c, _), sl in zip(items, sls)]
        dos = [do_ref[tok, sl] for tok, sl in zip(toks, sls)]
        vns = [vn_ref[tok, sl] for tok, sl in zip(toks, sls)]
        dvns = [dvn_ref[tok, sl] for tok, sl in zip(toks, sls)]
        Qs = [_dot(t["q"], t["k"], _NT) for t in ts]
        dws = [-_dot(dvn, S, _NT) for dvn, S in zip(dvns, Ss)]
        dqds = [_dot(do, S, _NT) for do, S in zip(dos, Ss)]
        dPs = [jnp.where(t["low"], _dot(do, vn, _NT), 0.0) for t, do, vn in zip(ts, dos, vns)]
        dkds = [_dot(vn, dS2, _NT) for vn, dS2 in zip(vns, dS2s)]
        dTs = [_dot(dvn, t["vb"], _NT) + _dot(dw, t["kbg"], _NT) for t, dvn, dw in zip(ts, dvns, dws)]
        dvbs = [_dot(Tm, dvn, _TN) for Tm, dvn in zip(Tms, dvns)]
        dkbgs = [_dot(Tm, dw, _TN) for Tm, dw in zip(Tms, dws)]
        TdTs = [_dot(Tm, dT, _TN) for Tm, dT in zip(Tms, dTs)]
        dLs = [jnp.where(t["strict"], -_dot(TdT, Tm, _NT), 0.0) for t, TdT, Tm in zip(ts, TdTs, Tms)]
        dMs = [dL * t["G"] for t, dL in zip(ts, dLs)]
        dQs = [dP * t["G"] for t, dP in zip(ts, dPs)]
        dkbs = [_dot(dM, t["k"]) + dkbg * t["eg"] for t, dM, dkbg in zip(ts, dMs, dkbgs)]
        rs = lambda a: jnp.sum(a, axis=1, keepdims=True)
        lane = _iota2((C, 128), 1)
        last = _iota2((C, 1), 0) == C - 1
        dbg = [jnp.zeros((C, 128), F32) for _ in range(CS)]
        for i, (c, h) in enumerate(items):
            t, sl, tok = ts[i], sls[i], toks[i]
            E = (dLs[i] * t["M"] + dPs[i] * Qs[i]) * t["G"]
            dqkv_ref[tok, sl] = _dot(dQs[i], t["k"]) + dqds[i] * t["eg"]
            dqkv_ref[tok, D + h * 128:D + (h + 1) * 128] = (
                _dot(dQs[i], t["q"], _TN) + _dot(dMs[i], t["kb"], _TN) + dkds[i] * t["egl"] + dkbs[i] * t["bx"])
            dqkv_ref[tok, 2 * D + h * 128:2 * D + (h + 1) * 128] = dvbs[i] * t["bx"]
            dbeta_c = rs(dkbs[i] * t["k"] + dvbs[i] * t["v"])
            dkd_kd = dkds[i] * t["kd"]
            dgam_c = rs(dqds[i] * t["qd"]) + rs(dkbgs[i] * t["kbg"]) - rs(dkd_kd) + rs(E)
            dgam_r = -jnp.sum(E, axis=0, keepdims=True)
            dgam_c = dgam_c + jnp.sum(jnp.where(t["eye"], dgam_r, 0.0), axis=1, keepdims=True)
            dlast = _sum_all(dkd_kd) + t["eL"] * _sum_all(Ss[i].astype(F32) * dS2s[i].astype(F32))
            dgam_c = dgam_c + jnp.where(last, dlast, 0.0)
            dbg[c] = dbg[c] + jnp.where(lane == h, dbeta_c, 0.0) + jnp.where(lane == GDN_H + h, dgam_c, 0.0)
        for c in range(CS):
            dbg_ref[c * C:(c + 1) * C, :] = dbg[c]

    blk = lambda c: pl.BlockSpec((CS * C, D), lambda n: (n, c))
    st = pl.BlockSpec((CS, GDN_DK, D), lambda n: (n, 0, 0))
    seg = pl.BlockSpec((CS * C, 128), lambda n: (n, 0))
    in_specs = [blk(0), blk(1), blk(0), seg, st,
                pl.BlockSpec((CS, GDN_H * C, C), lambda n: (n, 0, 0)), blk(0), blk(0), st, blk(0)]
    out_specs = [pl.BlockSpec((CS * C, 3 * D), lambda n: (n, 0)), seg]
    out_shape = [jax.ShapeDtypeStruct((T, 3 * D), F32), jax.ShapeDtypeStruct((T, 128), F32)]
    ins = [qk, qk, v, bg, s_save, t_save, vn, dvn, ds_save, do]
    if ride:
        ins, in_specs = ins + ride.srcs, in_specs + ride.specs
        out_shape, out_specs = out_shape + ride.out_shape, out_specs + ride.specs
    res = pl.pallas_call(
        body, grid=(NB,), in_specs=in_specs, out_specs=out_specs, out_shape=out_shape,
        scratch_shapes=ride.scratch if ride else [], name="gdn_rest_bwd",
        compiler_params=_params(("arbitrary",) if ride else ("parallel",)))(*ins)
    return (list(res[:2]), list(res[2:])) if ride else list(res)


def _ssd_seg(al_pair, half, s):
    L = SSM_L
    ri, ci = _iota2((L, L), 0), _iota2((L, L), 1)
    ac = jnp.max(jnp.where(half == s, al_pair, _NEG), axis=1, keepdims=True)
    ar = jnp.sum(jnp.where(ri == ci, ac, 0.0), axis=0, keepdims=True)
    return jnp.exp(jnp.where(ri >= ci, ac - ar, _NEG))


def _last_row(a):
    return jnp.sum(jnp.where(_iota2((a.shape[0], 1), 0) == a.shape[0] - 1, a, 0.0), axis=0, keepdims=True)


def _ssd_core_fwd(xbc, dtx, alx):
    T = xbc.shape[0]
    L, CS = SSM_L, SSM_SCAN_CHUNKS
    Nc = T // L

    def body(x_all, bc_all, dt_all, al_all, y_all, hs_all, H_scr):
        @pl.when(pl.program_id(0) == 0)
        def _():
            H_scr[...] = jnp.zeros_like(H_scr)

        for cc in range(CS):
            rows = pl.ds(cc * L, L)
            chunk(x_all.at[rows], bc_all.at[rows], dt_all.at[rows], al_all.at[rows], y_all.at[rows], hs_all.at[cc],
                  H_scr)

    def chunk(x_ref, bc_ref, dt_ref, al_ref, y_ref, hs_ref, H_scr):
        half = _iota2((L, 128), 1) >> 6
        for g in range(2):
            gs = slice(g * 512, (g + 1) * 512)
            Bg = bc_ref[:, g * 128:(g + 1) * 128]
            Cg = bc_ref[:, 256 + g * 128:256 + (g + 1) * 128]
            alg = al_ref[:, gs]
            alast = _last_row(alg)
            xdt = x_ref[:, gs] * dt_ref[:, gs]
            Hg = H_scr[:, gs]
            hs_ref[:, gs] = Hg
            CB = _dot(Cg, Bg, _NT)
            y_ref[:, gs] = jnp.exp(alg) * _dot(Cg, Hg)
            H_scr[:, gs] = Hg * jnp.exp(alast) + _dot(Bg, jnp.exp(alast - alg) * xdt, _TN)
            for j in range(4):
                ps = slice(g * 512 + j * 128, g * 512 + (j + 1) * 128)
                al_pair = al_ref[:, ps]
                xp = x_ref[:, ps] * dt_ref[:, ps]
                ys = [_dot(_ssd_seg(al_pair, half, s) * CB, xp) for s in range(2)]
                y_ref[:, ps] += jnp.where(half == 0, ys[0], ys[1])

    row = pl.BlockSpec((CS * L, D), lambda c: (c, 0))
    return dict(
        body=body, steps=Nc // CS, ins=[xbc, xbc, dtx, alx],
        in_specs=[row, pl.BlockSpec((CS * L, 512), lambda c: (c, 2)), row, row],
        out_specs=[row, pl.BlockSpec((CS, SSM_N, D), lambda c: (c, 0, 0))],
        out_shape=[jax.ShapeDtypeStruct((T, D), F32), jax.ShapeDtypeStruct((Nc, SSM_N, D), F32)],
        scratch=[pltpu.VMEM((SSM_N, D), F32)])


def _ssd_core_bwd(xbc, dtx, alx, h_save, dyy, d_x):
    T = xbc.shape[0]
    L, CS = SSM_L, SSM_SCAN_CHUNKS
    Nc = T // L
    NB = Nc // CS

    def body(x_all, bc_all, dt_all, al_all, hs_all, dy_all, d_ref, dx_all, ddt_all, dal_all, dH_scr):
        @pl.when(pl.program_id(0) == 0)
        def _():
            dH_scr[...] = jnp.zeros_like(dH_scr)

        for cc in reversed(range(CS)):
            rows = pl.ds(cc * L, L)
            chunk(x_all.at[rows], bc_all.at[rows], dt_all.at[rows], al_all.at[rows], hs_all.at[cc], dy_all.at[rows],
                  d_ref, dx_all.at[rows], ddt_all.at[rows], dal_all.at[rows], dH_scr)

    def chunk(x_ref, bc_ref, dt_ref, al_ref, hs_ref, dy_ref, d_ref, dx_ref, ddt_ref, dal_ref, dH_scr):
        lane = _iota2((L, 128), 1)
        half = lane >> 6
        rowi = _iota2((L, 1), 0)
        ri, ci = _iota2((L, L), 0), _iota2((L, L), 1)
        for g in range(2):
            gs = slice(g * 512, (g + 1) * 512)
            Bg = bc_ref[:, g * 128:(g + 1) * 128]
            Cg = bc_ref[:, 256 + g * 128:256 + (g + 1) * 128]
            alg = al_ref[:, gs]
            alast = _last_row(alg)
            eal, edec, eL = jnp.exp(alg), jnp.exp(alast - alg), jnp.exp(alast)
            xg, dtg, dYg = x_ref[:, gs], dt_ref[:, gs], dy_ref[:, gs]
            xdt = xg * dtg
            Hg = hs_ref[:, gs]
            dH2 = dH_scr[:, gs]
            CB = _dot(Cg, Bg, _NT)
            dYe = eal * dYg
            dH_scr[:, gs] = dH2 * eL + _dot(Cg, dYe, _TN)
            dC = _dot(dYe, Hg, _NT)
            zg = edec * xdt
            dz = _dot(Bg, dH2)
            dB = _dot(zg, dH2, _NT)
            tz = dz * zg
            dal = dYe * _dot(Cg, Hg) - tz
            dalast = jnp.sum(tz, axis=0, keepdims=True) + eL * jnp.sum(Hg * dH2, axis=0, keepdims=True)
            dal = dal + jnp.where(rowi == L - 1, dalast, 0.0)
            dxdt_g = edec * dz
            dx_ref[:, gs] = dxdt_g * dtg + dYg * d_ref[:, gs]
            ddt_ref[:, gs] = dxdt_g * xg
            dal_ref[:, gs] = dal
            dCB = jnp.zeros((L, L), F32)
            for j in range(4):
                ps = slice(g * 512 + j * 128, g * 512 + (j + 1) * 128)
                al_pair = al_ref[:, ps]
                xp = x_ref[:, ps] * dt_ref[:, ps]
                dYp = dy_ref[:, ps]
                dxp = []
                dal_p = jnp.zeros((L, 128), F32)
                for s in range(2):
                    seg = _ssd_seg(al_pair, half, s)
                    W = seg * CB
                    dW = _dot(jnp.where(half == s, dYp, 0.0), xp, _NT)
                    dxp.append(_dot(W, dYp, _TN))
                    dCB = dCB + dW * seg
                    Es = dW * W
                    dac = jnp.sum(Es, axis=1, keepdims=True) - jnp.sum(
                        jnp.where(ri == ci, jnp.sum(Es, axis=0, keepdims=True), 0.0), axis=1, keepdims=True)
                    dal_p = dal_p + jnp.where(lane == 64 * s, dac, 0.0)
                dxdt_p = jnp.where(half == 0, dxp[0], dxp[1])
                dx_ref[:, ps] += dxdt_p * dt_ref[:, ps]
                ddt_ref[:, ps] += dxdt_p * x_ref[:, ps]
                dal_ref[:, ps] += dal_p
            dx_ref[:, D + g * 128:D + (g + 1) * 128] = dB + _dot(dCB, Cg, _TN)
            dx_ref[:, D + 256 + g * 128:D + 256 + (g + 1) * 128] = dC + _dot(dCB, Bg)

    row = pl.BlockSpec((CS * L, D), lambda c: (NB - 1 - c, 0))
    bcs = pl.BlockSpec((CS * L, 512), lambda c: (NB - 1 - c, 2))
    return dict(
        body=body, steps=NB, ins=[xbc, xbc, dtx, alx, h_save, dyy, d_x],
        in_specs=[row, bcs, row, row, pl.BlockSpec((CS, SSM_N, D), lambda c: (NB - 1 - c, 0, 0)), row,
                  pl.BlockSpec((1, D), lambda c: (0, 0))],
        out_specs=[pl.BlockSpec((CS * L, D + 512), lambda c: (NB - 1 - c, 0)), row, row],
        out_shape=[jax.ShapeDtypeStruct((T, D + 512), F32),
                   jax.ShapeDtypeStruct((T, D), F32), jax.ShapeDtypeStruct((T, D), F32)],
        scratch=[pltpu.VMEM((SSM_N, D), F32)])


def _run_scans(parts, *, name):
    steps = parts[0]["steps"]
    assert all(p["steps"] == steps for p in parts)
    cnt = lambda key: [len(p[key]) for p in parts]
    n_in, n_out, n_scr = cnt("ins"), cnt("out_shape"), cnt("scratch")

    def body(*refs):
        ins, outs, scr = refs[:sum(n_in)], refs[sum(n_in):sum(n_in) + sum(n_out)], refs[sum(n_in) + sum(n_out):]
        oi = oo = os_ = 0
        for p, a, b, c in zip(parts, n_in, n_out, n_scr):
            p["body"](*ins[oi:oi + a], *outs[oo:oo + b], *scr[os_:os_ + c])
            oi, oo, os_ = oi + a, oo + b, os_ + c

    cat = lambda key: [v for p in parts for v in p[key]]
    res = pl.pallas_call(
        body, grid=(steps,), in_specs=cat("in_specs"), out_specs=cat("out_specs"), out_shape=cat("out_shape"),
        scratch_shapes=cat("scratch"), name=name, compiler_params=_params(("arbitrary",)))(*cat("ins"))
    out, o = [], 0
    for b in n_out:
        out.append(list(res[o:o + b]))
        o += b
    return out


_EARLY = ("w_out", "wq_mem", "wk_mem", "wv_mem", "wo_mem")
_LATE = ("w_up", "w_down")
_GRADS_MLP = ("w_down", "w_up")
_GRADS_MID = ("wo_mem", "wq_mem", "wk_mem", "wv_mem", "w_out")


def _gather_ride(shards, names):
    return None if shards is None else _Ride([shards[n] for n in names], shard=True)


def _grad_ride(shards, G, names):
    return None if shards is None else _Ride([_slots_from_full(n, G[n]) for n in names], shard=False)


def _local_step(x, mem, tgt, W, shards=None):
    T = x.shape[0]
    W = dict(W)
    cw_qk, cw_v = W["gdn_conv_w"][:, :2 * D], W["gdn_conv_w"][:, 2 * D:]
    h1 = _rmsnorm_fwd(x, W["norm1_w"], name="norm1_fwd")
    ride = _gather_ride(shards, _EARLY)
    pg = _mm(h1, W["w_in_pad"][:, C_GATE:], name="in_proj_gates")
    p = _mm(h1, W["w_in_pad"][:, :C_GATE], out_dtype=BF16, bn_cap=1664, name="in_proj", ride=ride)
    if ride:
        p, got = p
        W.update({n: _full_from_slots(n, g) for n, g in zip(_EARLY, got)})
    qk = _conv_fwd(p, C_QKV, 2 * D, cw_qk, None, l2=True, name="gdn_conv_qk_fwd")
    v_g = _conv_fwd(p, C_QKV + 2 * D, D, cw_v, None, l2=False, name="gdn_conv_v_fwd")
    bg = _gdn_gates_fwd(pg, W["gdn_alog_row"], W["gdn_dtb_row"])
    ride = _gather_ride(shards, _LATE)
    prep = _gdn_prep(qk, v_g, bg, ride)
    if ride:
        prep, got = prep
        W.update({n: _full_from_slots(n, g) for n, g in zip(_LATE, got)})
    u_g, w_g, qd_g, kd_g, p_g, t_save = prep
    xbc = _conv_fwd(p, C_XBC, D + 512, W["ssm_conv_w"], W["ssm_conv_b"], l2=False, name="ssm_conv_fwd", bc=512)
    dtx, alx = _ssd_dt_fwd(pg, W["ssm_dtb_row"], W["ssm_alog_x"])
    (o_g, vn_g, s_save), (y_s, h_save) = _run_scans(
        [_gdn_scan_fwd(u_g, w_g, qd_g, kd_g, p_g, bg), _ssd_core_fwd(xbc, dtx, alx)], name="scans_fwd")
    mix = _gdn_post_fwd(o_g, p, W["gdn_norm_x"])
    mix = _ssd_post_fwd(y_s, xbc, p, W["ssm_d_x"], W["ssm_norm_w"].reshape(1, D), mix)
    x1, h2 = _mm(mix, W["w_out"], epi="res_norm", extra=(x, W["norm2_w"]), bm=512, name="out_proj")
    qm = _mm(h2, W["wq_mem"], out_dtype=BF16, name="q_proj")
    m = _rmsnorm_fwd(mem, W["mem_norm_w"], name="mem_norm_fwd")
    km = _mm(m, W["wk_mem"], name="k_proj")
    vm = _mm(m, W["wv_mem"], name="v_proj")
    oa = _attn_fwd(qm, km, vm)
    x2, h3 = _mm(oa, W["wo_mem"], epi="res_norm", extra=(x1, W["norm3_w"]), bm=512, name="o_proj")
    u, act = _mm(h3, W["w_up"], epi="relu2", out_dtype=BF16, name="mlp_up")
    dx3, g_final, loss = _mm(act, W["w_down"], epi="res_loss", extra=(x2, tgt, W["final_norm_w"]), bk_cap=1024,
                             name="mlp_down_loss")
    G = {"final_norm_w": g_final.reshape(D)}
    dpre = _mm(dx3, W["w_down"], dims="nt", epi="mul2", extra=u, out_dtype=BF16, name="mlp_down_dx")
    G["w_down"] = _mm(act, dx3, dims="tn", out_dtype=BF16, name="mlp_down_dw")
    G["w_up"] = _mm(h3, dpre, dims="tn", out_dtype=BF16, name="mlp_up_dw")
    dx2, gw = _mm(dpre, W["w_up"], dims="nt", epi="norm_bwd", extra=(x2, dx3, W["norm3_w"]), bk_cap=1024,
                  name="mlp_up_dx")
    G["norm3_w"] = gw.reshape(D)
    do_a = _mm(dx2, W["wo_mem"], dims="nt", out_dtype=BF16, name="o_proj_dx")
    G["wo_mem"] = _mm(oa, dx2, dims="tn", out_dtype=BF16, name="o_proj_dw")
    dq, dk, dv = _attn_bwd(qm, km, vm, do_a)
    G["wq_mem"] = _mm(h2, dq, dims="tn", out_dtype=BF16, name="q_proj_dw")
    dx1, gw = _mm(dq, W["wq_mem"], dims="nt", epi="norm_bwd", extra=(x1, dx2, W["norm2_w"]), bm=512,
                  name="q_proj_dx")
    G["norm2_w"] = gw.reshape(D)
    G["wk_mem"] = _mm(m, dk, dims="tn", out_dtype=BF16, name="k_proj_dw")
    G["wv_mem"] = _mm(m, dv, dims="tn", out_dtype=BF16, name="v_proj_dw")
    dm = _mm(dk, W["wk_mem"], dims="nt", name="k_proj_dx")
    dm = _mm(dv, W["wv_mem"], dims="nt", epi="res", extra=dm, name="v_proj_dx")
    _, G["mem_norm_w"] = _rmsnorm_bwd(mem, W["mem_norm_w"], dm, None, name="mem_norm_bwd")
    dmix = _mm(dx1, W["w_out"], dims="nt", name="out_proj_dx")
    G["w_out"] = _mm(mix, dx1, dims="tn", out_dtype=BF16, name="out_proj_dw")
    do_g, dp, G["gdn_norm_x"] = _gdn_post_bwd(dmix, o_g, p, W["gdn_norm_x"])
    dyy, dp, G["ssm_d_x"], G["ssm_norm_w"] = _ssd_post_bwd(dmix, y_s, xbc, p, W["ssm_d_x"],
                                                          W["ssm_norm_w"].reshape(1, D), dp)
    (dvn_g, ds_save), (dxbc, ddtx, dalx) = _run_scans(
        [_gdn_scan_bwd(w_g, qd_g, kd_g, p_g, bg, do_g), _ssd_core_bwd(xbc, dtx, alx, h_save, dyy, W["ssm_d_x"])],
        name="scans_bwd")
    ride = _grad_ride(shards, G, _GRADS_MLP)
    rest = _gdn_rest_bwd(qk, v_g, bg, s_save, t_save, vn_g, dvn_g, ds_save, do_g, ride)
    if ride:
        rest, got = rest
        G.update(zip(_GRADS_MLP, got))
    dqkvn, dbg = rest
    dy_qk, gcw_qk, _ = _conv_bwd_act(p, C_QKV, 2 * D, cw_qk, None, dqkvn, 0, l2=True, name="gdn_conv_qk_bwd_act")
    dy_v, gcw_v, _ = _conv_bwd_act(p, C_QKV + 2 * D, D, cw_v, None, dqkvn, 2 * D, l2=False,
                                   name="gdn_conv_v_bwd_act")
    G["gdn_conv_w"] = jnp.concatenate([gcw_qk, gcw_v], axis=1)
    dp = _conv_bwd_in(dy_qk, cw_qk, dp, C_QKV, T, name="gdn_conv_qk_bwd_in")
    dp = _conv_bwd_in(dy_v, cw_v, dp, C_QKV + 2 * D, T, name="gdn_conv_v_bwd_in")
    dp, G["gdn_alog_row"], G["gdn_dtb_row"] = _gdn_gates_bwd(pg, W["gdn_alog_row"], W["gdn_dtb_row"], dbg, dp)
    dy_s, G["ssm_conv_w"], G["ssm_conv_b"] = _conv_bwd_act(p, C_XBC, D + 512, W["ssm_conv_w"], W["ssm_conv_b"],
                                                           dxbc, 0, l2=False, name="ssm_conv_bwd_act", bc=512)
    dp = _conv_bwd_in(dy_s, W["ssm_conv_w"], dp, C_XBC, T, name="ssm_conv_bwd_in", bc=512)
    dp, G["ssm_dtb_row"], G["ssm_alog_x"] = _ssd_dt_bwd(pg, W["ssm_dtb_row"], W["ssm_alog_x"], ddtx, dalx, dp)
    ride = _grad_ride(shards, G, _GRADS_MID)
    g_in = _mm(h1, dp, dims="tn", out_dtype=BF16, bn_cap=1152, name="in_proj_dw", ride=ride)
    if ride:
        g_in, got = g_in
        G.update(zip(_GRADS_MID, got))
    G["w_in"] = _unpad_w_in(g_in)
    ride = _grad_ride(shards, G, ("w_in",))
    res = _mm(dp, W["w_in_pad"], dims="nt", epi="norm_bwd", extra=(x, dx1, W["norm1_w"]),
              name="in_proj_dx", ride=ride)
    if ride:
        res, got = res
        G["w_in"] = got[0]
    dx, gw = res
    G["norm1_w"] = gw.reshape(D)
    return loss, dx, G


def _all_gather(shards, out_dtype, *, name):
    n = len(shards)

    def body(*refs):
        x_refs, out_refs, stage = refs[:n], refs[n:2 * n], refs[2 * n:3 * n]
        send_sems, recv_sems, local_sems = refs[3 * n:]
        x, y, c = _place()
        me, sibling = (x, y, c), (x, y, 1 - c)
        chips = [(1 - x, y), (x, 1 - y), (1 - x, 1 - y)]

        def slot(px, py, pc):
            return 4 * px + 2 * py + pc

        def copy(a, k, block, to, src=None):
            dst = out_refs[a].at[slot(*block)]
            return pltpu.make_async_remote_copy(
                src_ref=dst if src is None else src, dst_ref=dst, send_sem=send_sems.at[a, k],
                recv_sem=recv_sems.at[a, k], device_id=to, device_id_type=_MESH)

        for a in range(n):
            stage[a][...] = x_refs[a][...].astype(out_dtype)
        mine = [pltpu.make_async_copy(stage[a], out_refs[a].at[slot(*me)], local_sems.at[a]) for a in range(n)]
        for cp in mine:
            cp.start()
        first = []
        for a in range(n):
            first.append(copy(a, 0, me, sibling, src=stage[a]))
            first += [copy(a, 1 + j, me, (*chip, c), src=stage[a]) for j, chip in enumerate(chips)]
        for cp in first:
            cp.start()
        passed = [[copy(a, 4 + j, (*chip, c), sibling) for j, chip in enumerate(chips)] for a in range(n)]
        for j, chip in enumerate(chips):
            for a in range(n):
                copy(a, 1 + j, (*chip, c), me).wait_recv()
                passed[a][j].start()
        for a in range(n):
            copy(a, 0, sibling, me).wait_recv()
            for j, chip in enumerate(chips):
                copy(a, 4 + j, (*chip, 1 - c), me).wait_recv()
        for cp in first + [cp for row in passed for cp in row]:
            cp.wait_send()
        for cp in mine:
            cp.wait()

    outs = pl.pallas_call(
        body, in_specs=[_VM] * n, out_specs=[_ANY] * n,
        out_shape=[jax.ShapeDtypeStruct((N_DEV,) + s.shape, out_dtype) for s in shards],
        scratch_shapes=[pltpu.VMEM(s.shape, out_dtype) for s in shards]
        + [pltpu.SemaphoreType.DMA((n, 7)), pltpu.SemaphoreType.DMA((n, 7)), pltpu.SemaphoreType.DMA((n,))],
        name=name, compiler_params=pltpu.CompilerParams(vmem_limit_bytes=VMEM_LIMIT))(*shards)
    return list(outs)


def _cast_bf16(arrs, *, name):
    n = len(arrs)

    def body(*refs):
        for a in range(n):
            refs[n + a][...] = refs[a][...].astype(BF16)

    return list(pl.pallas_call(
        body, in_specs=[_VM] * n, out_specs=[_VM] * n,
        out_shape=[jax.ShapeDtypeStruct(s.shape, BF16) for s in arrs], name=name,
        compiler_params=pltpu.CompilerParams(vmem_limit_bytes=VMEM_LIMIT))(*arrs))


def _sum8(a, *, name):
    _, R, Cc = a.shape
    br = _pick_rows(R, 128)

    def body(a_ref, o_ref):
        s = a_ref[0].astype(F32)
        for k in range(1, N_DEV):
            s = s + a_ref[k].astype(F32)
        o_ref[...] = s

    return pl.pallas_call(
        body, grid=(R // br,), in_specs=[pl.BlockSpec((N_DEV, br, Cc), lambda i: (0, i, 0))],
        out_specs=pl.BlockSpec((br, Cc), lambda i: (i, 0)), out_shape=jax.ShapeDtypeStruct((R, Cc), F32),
        name=name, compiler_params=_params(("parallel",)))(a)


def _pick_rows(R, cap):
    if R <= cap:
        return R
    for d in range(cap, 7, -8):
        if R % d == 0:
            return d
    return R


def _adamw(w, g, m, v, *, name):
    shape = w.shape
    as2d = (lambda t: t.reshape(1, -1)) if w.ndim == 1 else (lambda t: t)
    w2, g2, m2, v2 = as2d(w), as2d(g), as2d(m), as2d(v)
    R, Cc = w2.shape
    br = _pick_rows(R, 256)
    c1 = 1.0 - ADAM_B1 ** ADAM_STEP
    c2 = 1.0 - ADAM_B2 ** ADAM_STEP

    def body(w_ref, g_ref, m_ref, v_ref, d_ref, nm_ref, nv_ref):
        gv = g_ref[...]
        nm = ADAM_B1 * m_ref[...] + (1.0 - ADAM_B1) * gv
        nv = ADAM_B2 * v_ref[...] + (1.0 - ADAM_B2) * (gv * gv)
        nm_ref[...] = nm
        nv_ref[...] = nv
        d_ref[...] = -ADAM_LR * ((nm / c1) / (jnp.sqrt(nv / c2) + ADAM_EPS) + ADAM_WD * w_ref[...])

    blk = pl.BlockSpec((br, Cc), lambda i: (i, 0))
    outs = pl.pallas_call(
        body, grid=(R // br,), in_specs=[blk] * 4, out_specs=[blk] * 3,
        out_shape=[jax.ShapeDtypeStruct((R, Cc), F32)] * 3, name=name,
        compiler_params=_params(("parallel",)))(w2, g2, m2, v2)
    return tuple(o.reshape(shape) for o in outs)


_BIG = ("w_in", "w_out", "wq_mem", "wk_mem", "wv_mem", "wo_mem", "w_up", "w_down")
_COL_SHARDED = ("w_in", "w_up")
_WEIGHTS = ("norm1_w", "w_in", "gdn_conv_w", "gdn_a_log", "gdn_dt_bias", "gdn_norm_w", "ssm_conv_w", "ssm_conv_b",
            "ssm_a_log", "ssm_dt_bias", "ssm_d", "ssm_norm_w", "w_out", "norm2_w", "mem_norm_w", "wq_mem", "wk_mem",
            "wv_mem", "wo_mem", "norm3_w", "w_up", "w_down", "final_norm_w")
_IN_PAD = 112


def _full_from_slots(name, g):
    if name in _COL_SHARDED:
        return jnp.transpose(g, (1, 0, 2)).reshape(g.shape[1], N_DEV * g.shape[2])
    return g.reshape(N_DEV * g.shape[1], g.shape[2])


def _slots_from_full(name, f):
    if name in _COL_SHARDED:
        return jnp.transpose(f.reshape(f.shape[0], N_DEV, f.shape[1] // N_DEV), (1, 0, 2))
    return f.reshape(N_DEV, f.shape[0] // N_DEV, f.shape[1])


def _pad_w_in(w):
    z = jnp.zeros((w.shape[0], _IN_PAD), w.dtype)
    return jnp.concatenate([w[:, :4096], w[:, 4112:6672], w[:, 4096:4112], z, w[:, 6672:6688], z], axis=1)


def _unpad_w_in(gp):
    return jnp.concatenate([gp[:, :4096], gp[:, C_GATE:C_GATE + 16], gp[:, 4096:C_GATE], gp[:, C_DT:C_DT + 16]],
                           axis=1)


def _pack_rows(vals):
    rows, offs, r = [], [], 0
    for vflat in vals:
        nrow = 8 * -(-vflat.shape[0] // 1024)
        rows.append(jnp.pad(vflat, (0, nrow * 128 - vflat.shape[0])).reshape(nrow, 128))
        offs.append((r, vflat.shape[0]))
        r += nrow
    return jnp.concatenate(rows, axis=0), offs


def _unpack_rows(packed, offs, shapes):
    out = []
    for (r, nel), shp in zip(offs, shapes):
        nrow = -(-nel // 128)
        out.append(packed[r:r + nrow].reshape(-1)[:nel].reshape(shp))
    return out


def kernel(x, mem, norm1_w, w_in, gdn_conv_w, gdn_a_log, gdn_dt_bias, gdn_norm_w, ssm_conv_w, ssm_conv_b, ssm_a_log, ssm_dt_bias, ssm_d, ssm_norm_w, w_out, norm2_w, mem_norm_w, wq_mem, wk_mem, wv_mem, wo_mem, norm3_w, w_up, w_down, final_norm_w, loss_target, m_norm1_w, m_w_in, m_gdn_conv_w, m_gdn_a_log, m_gdn_dt_bias, m_gdn_norm_w, m_ssm_conv_w, m_ssm_conv_b, m_ssm_a_log, m_ssm_dt_bias, m_ssm_d, m_ssm_norm_w, m_w_out, m_norm2_w, m_mem_norm_w, m_wq_mem, m_wk_mem, m_wv_mem, m_wo_mem, m_norm3_w, m_w_up, m_w_down, m_final_norm_w, v_norm1_w, v_w_in, v_gdn_conv_w, v_gdn_a_log, v_gdn_dt_bias, v_gdn_norm_w, v_ssm_conv_w, v_ssm_conv_b, v_ssm_a_log, v_ssm_dt_bias, v_ssm_d, v_ssm_norm_w, v_w_out, v_norm2_w, v_mem_norm_w, v_wq_mem, v_wk_mem, v_wv_mem, v_wo_mem, v_norm3_w, v_w_up, v_w_down, v_final_norm_w):
    args = dict(locals())
    w_loc = {n: args[n] for n in _WEIGHTS}
    me = 4 * lax.axis_index("x") + 2 * lax.axis_index("y") + lax.axis_index("c")

    w_in_full = _full_from_slots("w_in", _all_gather([w_in], BF16, name="gather_w_in")[0])
    later = _EARLY + _LATE
    shards = dict(zip(later, _cast_bf16([w_loc[n] for n in later], name="cast_shards")))
    conv_pack, conv_offs = _pack_rows([gdn_conv_w.reshape(-1), ssm_conv_w.reshape(-1)])
    conv_all = _all_gather([conv_pack], F32, name="gather_conv")[0]
    gdn_cw, ssm_cw = [], []
    for k in range(N_DEV):
        a, b = _unpack_rows(conv_all[k], conv_offs, [gdn_conv_w.shape, ssm_conv_w.shape])
        gdn_cw.append(a)
        ssm_cw.append(b)
    W = {
        "w_in_pad": _pad_w_in(w_in_full),
        "norm1_w": norm1_w, "norm2_w": norm2_w, "norm3_w": norm3_w, "mem_norm_w": mem_norm_w,
        "final_norm_w": final_norm_w, "ssm_norm_w": ssm_norm_w, "ssm_conv_b": ssm_conv_b,
        "gdn_conv_w": jnp.concatenate(gdn_cw, axis=1), "ssm_conv_w": jnp.concatenate(ssm_cw, axis=1),
        "gdn_alog_row": jnp.pad(gdn_a_log, (GDN_H, 128 - 2 * GDN_H)).reshape(1, 128),
        "gdn_dtb_row": jnp.pad(gdn_dt_bias, (GDN_H, 128 - 2 * GDN_H)).reshape(1, 128),
        "gdn_norm_x": jnp.tile(gdn_norm_w, GDN_H).reshape(1, D),
        "ssm_dtb_row": jnp.pad(ssm_dt_bias, (0, 128 - SSM_H)).reshape(1, 128),
        "ssm_alog_x": jnp.repeat(ssm_a_log, SSM_P).reshape(1, D),
        "ssm_d_x": jnp.repeat(ssm_d, SSM_P).reshape(1, D),
    }

    loss_part, grad_x, G = _local_step(x[0], mem[0], loss_target[0], W, shards)

    grads = {n: _sum8(G[n], name="sum_" + n) for n in _BIG}

    small = {
        "norm1_w": G["norm1_w"], "gdn_conv_w": G["gdn_conv_w"], "gdn_a_log": G["gdn_alog_row"][0, GDN_H:2 * GDN_H],
        "gdn_dt_bias": G["gdn_dtb_row"][0, GDN_H:2 * GDN_H], "gdn_norm_w": G["gdn_norm_x"].reshape(GDN_H, 128).sum(0),
        "ssm_conv_w": G["ssm_conv_w"], "ssm_conv_b": G["ssm_conv_b"],
        "ssm_a_log": G["ssm_alog_x"].reshape(SSM_H, SSM_P).sum(1), "ssm_dt_bias": G["ssm_dtb_row"][0, :SSM_H],
        "ssm_d": G["ssm_d_x"].reshape(SSM_H, SSM_P).sum(1), "ssm_norm_w": G["ssm_norm_w"].reshape(D),
        "norm2_w": G["norm2_w"], "mem_norm_w": G["mem_norm_w"], "norm3_w": G["norm3_w"],
        "final_norm_w": G["final_norm_w"], "loss": loss_part[0, :1],
    }
    names = list(small)
    pack, offs = _pack_rows([small[n].reshape(-1) for n in names])
    tot = _sum8(_all_gather([pack], F32, name="gather_small")[0], name="sum_small")
    summed = dict(zip(names, _unpack_rows(tot, offs, [small[n].shape for n in names])))
    loss = summed.pop("loss")[0]
    for n in ("gdn_conv_w", "ssm_conv_w"):
        width = w_loc[n].shape[1]
        summed[n] = lax.dynamic_slice_in_dim(summed[n], me * width, width, axis=1)
    grads.update(summed)

    upd = {n: _adamw(w_loc[n], grads[n], args["m_" + n], args["v_" + n], name="adamw_" + n) for n in _WEIGHTS}
    return (loss, grad_x[None], *[grads[n] for n in _WEIGHTS], *[upd[n][0] for n in _WEIGHTS],
            *[upd[n][1] for n in _WEIGHTS], *[upd[n][2] for n in _WEIGHTS])
```

```python
import functools
import math

import jax
import jax.numpy as jnp
from jax import lax
from jax.experimental import pallas as pl
from jax.experimental.pallas import tpu as pltpu

F32 = jnp.float32
BF16 = jnp.bfloat16
_MXU = BF16

D = 1024
EPS = 1e-6
CONV_K = 4
GDN_H, GDN_DK, GDN_C = 8, 128, 64
GDN_SCAN_CHUNKS = 4
GDN_LOCAL_CHUNKS = 4
GDN_REST_CHUNKS = 4
SSM_H, SSM_P, SSM_L, SSM_N = 16, 64, 128, 128
SSM_SCAN_CHUNKS = 2
MEM_H, MEM_HD = 4, 256
D_FF = 4096
N_DEV = 8

C_QKV, C_ZG, C_ZS, C_XBC, C_GATE, C_DT, C_TOT = 0, 3072, 4096, 5120, 6656, 6784, 6912
P_HALO = 16

ADAM_LR, ADAM_B1, ADAM_B2, ADAM_EPS, ADAM_WD, ADAM_STEP = 0.001, 0.9, 0.999, 1e-08, 0.01, 10

VMEM_LIMIT = 56 * 1024 * 1024

_NN = (((1,), (0,)), ((), ()))
_NT = (((1,), (1,)), ((), ()))
_TN = (((0,), (0,)), ((), ()))


def _dot(a, b, dims=_NN):
    return lax.dot_general(a.astype(_MXU), b.astype(_MXU), dims, preferred_element_type=F32)


def _split3(a):
    a1 = a.astype(BF16)
    r1 = a - a1.astype(F32)
    a2 = r1.astype(BF16)
    return a1, a2, (r1 - a2.astype(F32)).astype(BF16)


def _dot_sel(a, e):
    eb = e.astype(BF16)
    return sum(lax.dot_general(p, eb, _NN, preferred_element_type=F32) for p in _split3(a))


def _sel_dot(e, a):
    eb = e.astype(BF16)
    return sum(lax.dot_general(eb, p, _NN, preferred_element_type=F32) for p in _split3(a))


def _chunk_cumsum(a, tri, chunk):
    return jnp.concatenate([_sel_dot(tri, a[r:r + chunk]) for r in range(0, a.shape[0], chunk)], axis=0)


def _params(sem):
    return pltpu.CompilerParams(dimension_semantics=sem, vmem_limit_bytes=VMEM_LIMIT)


def _pick(n, cap):
    for d in range(min(cap, n), 0, -128):
        if n % d == 0 and d % 128 == 0:
            return d
    return n


def _sigmoid(x):
    return 0.5 * jnp.tanh(0.5 * x) + 0.5


def _silu(x):
    return x * _sigmoid(x)


def _dsilu(x):
    s = _sigmoid(x)
    return s * (1.0 + x * (1.0 - s))


def _softplus(x):
    return jnp.maximum(x, 0.0) + jnp.log(1.0 + jnp.exp(-jnp.abs(x)))


def _iota2(shape, axis):
    return lax.broadcasted_iota(jnp.int32, shape, axis)


def _sum_all(x):
    return jnp.sum(jnp.sum(x, axis=1, keepdims=True), axis=0, keepdims=True)


_MESH = pl.DeviceIdType.MESH
_ANY = pl.BlockSpec(memory_space=pl.ANY)
_VM = pl.BlockSpec(memory_space=pltpu.VMEM)
_REL = [(r >> 2 & 1, r >> 1 & 1, r & 1) for r in range(1, N_DEV)]


def _place():
    return lax.axis_index("x"), lax.axis_index("y"), lax.axis_index("c")


class _Ride:
    def __init__(self, srcs, shard):
        self.srcs, self.shard, self.n = list(srcs), shard, len(srcs)
        self.out_shape = [jax.ShapeDtypeStruct(((N_DEV,) + s.shape) if shard else s.shape, s.dtype)
                          for s in self.srcs]
        self.specs = [_ANY] * self.n
        self.scratch = [pltpu.SemaphoreType.DMA((self.n, N_DEV - 1)), pltpu.SemaphoreType.DMA((self.n, N_DEV - 1)),
                        pltpu.SemaphoreType.DMA((self.n,))]

    def _copies(self, in_refs, out_refs, sems):
        send, recv, loc = sems
        x, y, c = _place()
        me = 4 * x + 2 * y + c
        local, remote, arrive = [], [], []
        for a in range(self.n):
            src = in_refs[a] if self.shard else in_refs[a].at[me]
            local.append(pltpu.make_async_copy(src, out_refs[a].at[me], loc.at[a]))
        for k, (rx, ry, rc) in enumerate(_REL):
            peer = (lax.rem(x + rx, 2), lax.rem(y + ry, 2), lax.rem(c + rc, 2))
            ps = 4 * peer[0] + 2 * peer[1] + peer[2]
            for a in range(self.n):
                src = in_refs[a] if self.shard else in_refs[a].at[ps]
                remote.append(pltpu.make_async_remote_copy(
                    src_ref=src, dst_ref=out_refs[a].at[me], send_sem=send.at[a, k], recv_sem=recv.at[a, k],
                    device_id=peer, device_id_type=_MESH))
                slot = out_refs[a].at[ps]
                arrive.append(pltpu.make_async_remote_copy(
                    src_ref=slot, dst_ref=slot, send_sem=send.at[a, k], recv_sem=recv.at[a, k],
                    device_id=peer, device_id_type=_MESH))
        return local, remote, arrive

    def start(self, in_refs, out_refs, sems):
        local, remote, _ = self._copies(in_refs, out_refs, sems)
        for cp in local + remote:
            cp.start()

    def wait(self, in_refs, out_refs, sems):
        local, remote, arrive = self._copies(in_refs, out_refs, sems)
        for cp in arrive:
            cp.wait_recv()
        for cp in remote:
            cp.wait_send()
        for cp in local:
            cp.wait()


_EPI = {
    "none": ((), ("tile",)),
    "res": (("tile",), ("tile",)),
    "mul2": (("tile",), ("tile",)),
    "relu2": ((), ("tile", "tile")),
    "res_norm": (("tile", "row"), ("tile", "tile")),
    "norm_bwd": (("tile", "tile", "row"), ("tile", "row")),
    "res_loss": (("tile", "tile", "row"), ("tile", "row", "row")),
}


def _mm(a, b, *, dims="nn", epi="none", extra=(), out_dtype=F32, name, bm=1024, bn_cap=1024, bk_cap=2048,
        ride=None, b_cols=None, out_slabs=0):
    slabs = b.ndim == 3
    if dims == "nn":
        (M, K) = a.shape
        K2, N = (b.shape[1], b.shape[0] * b.shape[2]) if slabs else b.shape
        if slabs:
            bn_cap = b.shape[2]
    elif dims == "nt":
        (M, K) = a.shape
        N, K2 = (b.shape[1], b.shape[0] * b.shape[2]) if slabs else b.shape
        if slabs:
            bk_cap = b.shape[2]
    else:
        (K, M), (K2, N) = a.shape, b.shape
    jb0 = 0
    if b_cols is not None:
        N = b_cols[1]
    if out_slabs:
        bn_cap = N // out_slabs
    assert K == K2, (a.shape, b.shape, dims)
    bm = _pick(M, bm)
    bn = _pick(N, bn_cap)
    bk = _pick(K, bk_cap)
    nk = K // bk
    if b_cols is not None:
        assert dims == "nn" and b_cols[0] % bn == 0
        jb0 = b_cols[0] // bn
    assert not slabs or (bn == b.shape[2] if dims == "nn" else bk == b.shape[2])
    dn = {"nn": _NN, "nt": _NT, "tn": _TN}[dims]
    a_spec = (pl.BlockSpec((bk, bm), lambda i, j, k: (k, i)) if dims == "tn"
              else pl.BlockSpec((bm, bk), lambda i, j, k: (i, k)))
    if slabs:
        b_spec = (pl.BlockSpec((None, bn, bk), lambda i, j, k: (k, j, 0)) if dims == "nt"
                  else pl.BlockSpec((None, bk, bn), lambda i, j, k: (j, k, 0)))
    else:
        b_spec = (pl.BlockSpec((bn, bk), lambda i, j, k: (j, k)) if dims == "nt"
                  else pl.BlockSpec((bk, bn), lambda i, j, k: (k, j + jb0)))
    o_spec = (pl.BlockSpec((None, bm, bn), lambda i, j, k: (j, i, 0)) if out_slabs
              else pl.BlockSpec((bm, bn), lambda i, j, k: (i, j)))
    r_spec = pl.BlockSpec((1, bn), lambda i, j, k: (0, j))
    extra = list(extra) if isinstance(extra, (tuple, list)) else [extra]
    ekinds, okinds = _EPI[epi]
    assert len(extra) == len(ekinds) and (epi not in ("res_norm", "norm_bwd", "res_loss") or bn == N)
    n_extra, n_out = len(ekinds), len(okinds)
    n_ride = ride.n if ride else 0
    gi, gj = M // bm, N // bn

    def body(a_ref, b_ref, *rest):
        ex = rest[:n_extra]
        first = pl.program_id(0) == 0
        ride_in = rest[n_extra:n_extra + n_ride]
        outs = rest[n_extra + n_ride:n_extra + n_ride + n_out]
        ride_out = rest[n_extra + n_ride + n_out:n_extra + 2 * n_ride + n_out]
        if ride:
            at = lambda i, j, k: ((pl.program_id(0) == i) & (pl.program_id(1) == j) & (pl.program_id(2) == k))

            @pl.when(at(0, 0, 0))
            def _():
                ride.start(ride_in, ride_out, rest[-3:])

        def finish(r):
            if epi == "res":
                outs[0][...] = (r + ex[0][...].astype(F32)).astype(outs[0].dtype)
            elif epi == "mul2":
                outs[0][...] = (2.0 * r * ex[0][...].astype(F32)).astype(outs[0].dtype)
            elif epi == "relu2":
                u = jnp.maximum(r, 0.0)
                outs[0][...] = u.astype(outs[0].dtype)
                outs[1][...] = (u * u).astype(outs[1].dtype)
            elif epi == "res_norm":
                y = r + ex[0][...]
                outs[0][...] = y
                rstd = lax.rsqrt(jnp.mean(y * y, axis=1, keepdims=True) + EPS)
                outs[1][...] = (y * rstd * ex[1][...]).astype(outs[1].dtype)
            elif epi == "norm_bwd":
                xv = ex[0][...]
                rstd = lax.rsqrt(jnp.mean(xv * xv, axis=1, keepdims=True) + EPS)
                xh = xv * rstd
                dxh = r * ex[2][...]
                outs[0][...] = ex[1][...] + rstd * (dxh - xh * jnp.mean(dxh * xh, axis=1, keepdims=True))
                dw = jnp.sum(r * xh, axis=0, keepdims=True)

                @pl.when(first)
                def _():
                    outs[1][...] = dw

                @pl.when(jnp.logical_not(first))
                def _():
                    outs[1][...] += dw
            elif epi == "res_loss":
                y = r + ex[0][...]
                wv = ex[2][...]
                rstd = lax.rsqrt(jnp.mean(y * y, axis=1, keepdims=True) + EPS)
                yh = y * rstd
                err = yh * wv - ex[1][...]
                part_loss = 0.5 * jnp.sum(jnp.mean(err * err, axis=1, keepdims=True), axis=0, keepdims=True)
                dyn = err * (1.0 / N)
                dyh = dyn * wv
                outs[0][...] = rstd * (dyh - yh * jnp.mean(dyh * yh, axis=1, keepdims=True))
                dw = jnp.sum(dyn * yh, axis=0, keepdims=True)
                lrow = jnp.broadcast_to(part_loss, (1, N))

                @pl.when(first)
                def _():
                    outs[1][...] = dw
                    outs[2][...] = lrow

                @pl.when(jnp.logical_not(first))
                def _():
                    outs[1][...] += dw
                    outs[2][...] += lrow
            else:
                outs[0][...] = r.astype(outs[0].dtype)

        part = _dot(a_ref[...], b_ref[...], dn)
        if nk == 1:
            finish(part)
        else:
            acc = rest[n_extra + 2 * n_ride + n_out]
            k = pl.program_id(2)

            @pl.when(k == 0)
            def _():
                acc[...] = part

            @pl.when((k > 0) & (k < nk - 1))
            def _():
                acc[...] += part

            @pl.when(k == nk - 1)
            def _():
                finish(acc[...] + part)

        if ride:
            @pl.when(at(gi - 1, gj - 1, nk - 1))
            def _():
                ride.wait(ride_in, ride_out, rest[-3:])

    kind_spec = {"tile": o_spec, "row": r_spec}
    ins = [a, b] + [e.reshape(1, N) if k == "row" else e for e, k in zip(extra, ekinds)]
    in_specs = [a_spec, b_spec] + [kind_spec[k] for k in ekinds]
    out_dtypes = {"res_norm": (F32, BF16), "norm_bwd": (F32, F32), "res_loss": (F32, F32, F32)}.get(
        epi, (out_dtype,) * n_out)
    tile_shape = (out_slabs, M, N // out_slabs) if out_slabs else (M, N)
    out_shape = [jax.ShapeDtypeStruct(tile_shape if k == "tile" else (1, N), dt) for k, dt in zip(okinds, out_dtypes)]
    out_specs = [kind_spec[k] for k in okinds]
    scratch = [pltpu.VMEM((bm, bn), F32)] if nk > 1 else []
    sem = ("arbitrary" if epi in ("norm_bwd", "res_loss") else "parallel", "parallel", "arbitrary")
    if ride:
        ins, in_specs = ins + ride.srcs, in_specs + ride.specs
        out_shape, out_specs = out_shape + ride.out_shape, out_specs + ride.specs
        scratch, sem = scratch + ride.scratch, ("arbitrary",) * 3
    res = pl.pallas_call(
        body, grid=(gi, gj, nk), in_specs=in_specs, out_specs=out_specs, out_shape=out_shape,
        scratch_shapes=scratch, name=name, compiler_params=_params(sem))(*ins)
    main = res[:n_out] if n_out > 1 else res[0]
    return (main, list(res[n_out:])) if ride else main


def _rmsnorm_fwd(x, w, *, name, bt=256):
    T, Dm = x.shape
    bt = min(bt, T)

    def body(x_ref, w_ref, h_ref):
        xv = x_ref[...]
        r = lax.rsqrt(jnp.mean(xv * xv, axis=1, keepdims=True) + EPS)
        h_ref[...] = (xv * r * w_ref[...]).astype(h_ref.dtype)

    return pl.pallas_call(
        body, grid=(T // bt,),
        in_specs=[pl.BlockSpec((bt, Dm), lambda i: (i, 0)), pl.BlockSpec((1, Dm), lambda i: (0, 0))],
        out_specs=pl.BlockSpec((bt, Dm), lambda i: (i, 0)),
        out_shape=jax.ShapeDtypeStruct((T, Dm), BF16), name=name,
        compiler_params=_params(("parallel",)))(x, w.reshape(1, Dm))


def _rmsnorm_bwd(x, w, dh, dres, *, name, bt=256):
    T, Dm = x.shape
    bt = min(bt, T)
    has_res = dres is not None

    def body(x_ref, w_ref, dh_ref, *rest):
        dres_ref = rest[0] if has_res else None
        dx_ref, dw_ref = rest[-2], rest[-1]
        i = pl.program_id(0)
        xv = x_ref[...]
        r = lax.rsqrt(jnp.mean(xv * xv, axis=1, keepdims=True) + EPS)
        xh = xv * r
        dhv = dh_ref[...].astype(F32)
        dxh = dhv * w_ref[...]
        dx = r * (dxh - xh * jnp.mean(dxh * xh, axis=1, keepdims=True))
        if has_res:
            dx = dx + dres_ref[...]
        dx_ref[...] = dx

        @pl.when(i == 0)
        def _():
            dw_ref[...] = jnp.zeros_like(dw_ref)

        dw_ref[...] += jnp.sum(dhv * xh, axis=0, keepdims=True)

    row = pl.BlockSpec((bt, Dm), lambda i: (i, 0))
    vec = pl.BlockSpec((1, Dm), lambda i: (0, 0))
    ins = [x, w.reshape(1, Dm), dh] + ([dres] if has_res else [])
    dx, dw = pl.pallas_call(
        body, grid=(T // bt,), in_specs=[row, vec, row] + ([row] if has_res else []),
        out_specs=[row, vec],
        out_shape=[jax.ShapeDtypeStruct((T, Dm), F32), jax.ShapeDtypeStruct((1, Dm), F32)],
        name=name, compiler_params=_params(("arbitrary",)))(*ins)
    return dx, dw.reshape(Dm)


def _attn_fwd(q, km, vm, *, bt=256):
    T = q.shape[0]
    M = km.shape[0]
    bt = min(bt, T)
    scale = MEM_HD ** -0.5

    def body(q_ref, k_ref, v_ref, o_ref):
        for h in range(MEM_H):
            sl = slice(h * MEM_HD, (h + 1) * MEM_HD)
            s = _dot(q_ref[:, sl], k_ref[:, sl], _NT) * scale
            s = s - jnp.max(s, axis=1, keepdims=True)
            e = jnp.exp(s)
            p = e / jnp.sum(e, axis=1, keepdims=True)
            o_ref[:, sl] = _dot(p, v_ref[:, sl]).astype(o_ref.dtype)

    row = pl.BlockSpec((bt, D), lambda i: (i, 0))
    mem = pl.BlockSpec((M, D), lambda i: (0, 0))
    return pl.pallas_call(
        body, grid=(T // bt,), in_specs=[row, mem, mem], out_specs=row,
        out_shape=jax.ShapeDtypeStruct((T, D), BF16), name="attn_fwd",
        compiler_params=_params(("parallel",)))(q, km, vm)


def _attn_bwd(q, km, vm, do, *, bt=256):
    T = q.shape[0]
    M = km.shape[0]
    bt = min(bt, T)
    scale = MEM_HD ** -0.5

    def body(q_ref, k_ref, v_ref, do_ref, dq_ref, dk_ref, dv_ref):
        i = pl.program_id(0)

        @pl.when(i == 0)
        def _():
            dk_ref[...] = jnp.zeros_like(dk_ref)
            dv_ref[...] = jnp.zeros_like(dv_ref)

        sls = [slice(h * MEM_HD, (h + 1) * MEM_HD) for h in range(MEM_H)]
        ss = [_dot(q_ref[:, sl], k_ref[:, sl], _NT) * scale for sl in sls]
        dps = [_dot(do_ref[:, sl], v_ref[:, sl], _NT) for sl in sls]
        es = [jnp.exp(s - jnp.max(s, axis=1, keepdims=True)) for s in ss]
        ps = [e / jnp.sum(e, axis=1, keepdims=True) for e in es]
        dss = [p * (dp - jnp.sum(dp * p, axis=1, keepdims=True)) * scale for p, dp in zip(ps, dps)]
        for sl, p, ds in zip(sls, ps, dss):
            dq_ref[:, sl] = _dot(ds, k_ref[:, sl]).astype(dq_ref.dtype)
            dk_ref[:, sl] += _dot(ds, q_ref[:, sl], _TN)
            dv_ref[:, sl] += _dot(p, do_ref[:, sl], _TN)

    row = pl.BlockSpec((bt, D), lambda i: (i, 0))
    mem = pl.BlockSpec((M, D), lambda i: (0, 0))
    return pl.pallas_call(
        body, grid=(T // bt,), in_specs=[row, mem, mem, row], out_specs=[row, mem, mem],
        out_shape=[jax.ShapeDtypeStruct((T, D), BF16), jax.ShapeDtypeStruct((M, D), F32),
                   jax.ShapeDtypeStruct((M, D), F32)],
        name="attn_bwd", compiler_params=_params(("arbitrary",)))(q, km, vm, do)


def _conv_apply(halo, x, w_ref, b_ref):
    bt, hr = x.shape[0], halo.shape[0]
    cat = jnp.concatenate([halo, x], axis=0)
    y = x * w_ref[3:4, :]
    for k in range(CONV_K - 1):
        y = y + pltpu.roll(cat, CONV_K - 1 - k, 0)[hr:hr + bt] * w_ref[k:k + 1, :]
    if b_ref is not None:
        y = y + b_ref[...]
    return y


def _l2_parts(act, bc):
    out = []
    for s in range(bc // 128):
        a = act[:, s * 128:(s + 1) * 128]
        r = lax.rsqrt(jnp.sum(a * a, axis=1, keepdims=True) + EPS)
        out.append((a, r))
    return out


def _conv_fwd(p, col0, C, w, b, *, l2, name, bt=512, bc=1024):
    T = p.shape[0]
    bt = min(bt, T)
    c0, hb = col0 // bc, bt // P_HALO
    has_b = b is not None
    assert not l2 or (bc == D and C == 2 * D)

    def body(x_ref, halo_ref, w_ref, *rest):
        b_ref = rest[0] if has_b else None
        o_ref = rest[-1]
        i, j = pl.program_id(0), pl.program_id(1)
        x = x_ref[...].astype(F32)
        halo = jnp.where(i > 0, halo_ref[...].astype(F32), 0.0)
        act = _silu(_conv_apply(halo, x, w_ref, b_ref))
        if l2:
            sc = jnp.where(j == 0, GDN_DK ** -0.5, 1.0)
            o_ref[...] = jnp.concatenate([a * (r * sc) for a, r in _l2_parts(act, bc)], axis=1)
        else:
            o_ref[...] = act

    in_specs = [pl.BlockSpec((bt, bc), lambda i, j: (i, c0 + j)),
                pl.BlockSpec((P_HALO, bc), lambda i, j: (jnp.maximum(i * hb - 1, 0), c0 + j)),
                pl.BlockSpec((CONV_K, bc), lambda i, j: (0, j))]
    ins = [p, p, w]
    if has_b:
        in_specs.append(pl.BlockSpec((1, bc), lambda i, j: (0, j)))
        ins.append(b.reshape(1, C))
    return pl.pallas_call(
        body, grid=(T // bt, C // bc), in_specs=in_specs,
        out_specs=pl.BlockSpec((bt, bc), lambda i, j: (i, j)),
        out_shape=jax.ShapeDtypeStruct((T, C), F32), name=name,
        compiler_params=_params(("parallel", "parallel")))(*ins)


def _conv_bwd_act(p, col0, C, w, b, dact, dcol0, *, l2, name, bt=512, bc=1024):
    T = p.shape[0]
    bt = min(bt, T)
    c0, d0, hb = col0 // bc, dcol0 // bc, bt // P_HALO
    has_b = b is not None
    assert not l2 or (bc == D and C == 2 * D)

    def body(x_ref, halo_ref, w_ref, *rest):
        b_ref = rest[0] if has_b else None
        dact_ref, dy_ref, dw_ref, db_ref = rest[-4:]
        j, i = pl.program_id(0), pl.program_id(1)
        x = x_ref[...].astype(F32)
        halo = jnp.where(i > 0, halo_ref[...].astype(F32), 0.0)
        y = _conv_apply(halo, x, w_ref, b_ref)
        dact = dact_ref[...]
        sg = _sigmoid(y)
        if l2:
            sc = jnp.where(j == 0, GDN_DK ** -0.5, 1.0)
            parts = []
            for s, (a, r) in enumerate(_l2_parts(y * sg, bc)):
                n = a * r
                dn = dact[:, s * 128:(s + 1) * 128]
                parts.append((r * sc) * (dn - n * jnp.sum(dn * n, axis=1, keepdims=True)))
            dact = jnp.concatenate(parts, axis=1)
        dy = dact * (sg * (1.0 + y * (1.0 - sg)))
        dy_ref[...] = dy

        @pl.when(i == 0)
        def _():
            dw_ref[...] = jnp.zeros_like(dw_ref)
            db_ref[...] = jnp.zeros_like(db_ref)

        db_ref[...] += jnp.sum(dy, axis=0, keepdims=True)
        cat = jnp.concatenate([halo, x], axis=0)
        dw_ref[3:4, :] += jnp.sum(dy * x, axis=0, keepdims=True)
        for k in range(CONV_K - 1):
            xs = pltpu.roll(cat, CONV_K - 1 - k, 0)[P_HALO:P_HALO + bt]
            dw_ref[k:k + 1, :] += jnp.sum(dy * xs, axis=0, keepdims=True)

    in_specs = [pl.BlockSpec((bt, bc), lambda j, i: (i, c0 + j)),
                pl.BlockSpec((P_HALO, bc), lambda j, i: (jnp.maximum(i * hb - 1, 0), c0 + j)),
                pl.BlockSpec((CONV_K, bc), lambda j, i: (0, j))]
    ins = [p, p, w]
    if has_b:
        in_specs.append(pl.BlockSpec((1, bc), lambda j, i: (0, j)))
        ins.append(b.reshape(1, C))
    in_specs.append(pl.BlockSpec((bt, bc), lambda j, i: (i, d0 + j)))
    ins.append(dact)
    dy, dw, db = pl.pallas_call(
        body, grid=(C // bc, T // bt), in_specs=in_specs,
        out_specs=[pl.BlockSpec((bt, bc), lambda j, i: (i, j)),
                   pl.BlockSpec((CONV_K, bc), lambda j, i: (0, j)),
                   pl.BlockSpec((1, bc), lambda j, i: (0, j))],
        out_shape=[jax.ShapeDtypeStruct((T, C), F32), jax.ShapeDtypeStruct((CONV_K, C), F32),
                   jax.ShapeDtypeStruct((1, C), F32)],
        name=name, compiler_params=_params(("parallel", "arbitrary")))(*ins)
    return dy, dw, db.reshape(C)


def _conv_bwd_in(dy, w, dp_in, col0, T, *, name, bt=512, bc=1024):
    C = dy.shape[1]
    bt = min(bt, T)
    c0, hb, nb = col0 // bc, bt // 8, T // bt

    def body(dy_ref, nxt_ref, w_ref, *rest):
        o_ref = rest[-1]
        i = pl.program_id(0)
        dy_v = dy_ref[...]
        nxt = jnp.where(i < nb - 1, nxt_ref[...], 0.0)
        cat = jnp.concatenate([dy_v, nxt], axis=0)
        dx = dy_v * w_ref[3:4, :]
        for k in range(CONV_K - 1):
            s = CONV_K - 1 - k
            dx = dx + pltpu.roll(cat, bt + 8 - s, 0)[0:bt] * w_ref[k:k + 1, :]
        o_ref[...] = dx.astype(o_ref.dtype)

    in_specs = [pl.BlockSpec((bt, bc), lambda i, j: (i, j)),
                pl.BlockSpec((8, bc), lambda i, j: (jnp.minimum((i + 1) * hb, T // 8 - 1), j)),
                pl.BlockSpec((CONV_K, bc), lambda i, j: (0, j))]
    ins = [dy, dy, w]
    alias = {}
    if dp_in is not None:
        in_specs.append(pl.BlockSpec(memory_space=pl.ANY))
        ins.append(dp_in)
        alias = {3: 0}
    return pl.pallas_call(
        body, grid=(nb, C // bc), in_specs=in_specs,
        out_specs=pl.BlockSpec((bt, bc), lambda i, j: (i, c0 + j)),
        out_shape=jax.ShapeDtypeStruct((T, C_TOT), BF16), input_output_aliases=alias, name=name,
        compiler_params=_params(("parallel", "parallel")))(*ins)


def _expand_mats(shift, row0):
    e = (_iota2((128, D), 0) - row0 == (_iota2((128, D), 1) >> shift)).astype(F32)
    et = ((_iota2((D, 128), 0) >> shift) == _iota2((D, 128), 1) - row0).astype(F32)
    return e, et


def _cum_mats(chunk):
    ri, ci = _iota2((chunk, chunk), 0), _iota2((chunk, chunk), 1)
    return (ri >= ci).astype(F32), (ri <= ci).astype(F32)


def _gdn_gates_fwd(p, alog_row, dtb_row, *, bt=256):
    T = p.shape[0]
    bt = min(bt, T)

    def body(g_ref, al_ref, db_ref, bg_ref):
        gt = g_ref[...]
        lc, _ = _cum_mats(GDN_C)
        g_l = -jnp.exp(al_ref[...]) * _softplus(gt + db_ref[...])
        bg_ref[...] = jnp.where(_iota2((bt, 128), 1) < GDN_H, _sigmoid(gt), _chunk_cumsum(g_l, lc, GDN_C))

    vec = pl.BlockSpec((1, 128), lambda i: (0, 0))
    seg = pl.BlockSpec((bt, 128), lambda i: (i, 0))
    return pl.pallas_call(
        body, grid=(T // bt,), in_specs=[seg, vec, vec], out_specs=seg,
        out_shape=jax.ShapeDtypeStruct((T, 128), F32), name="gdn_gates_fwd",
        compiler_params=_params(("parallel",)))(p, alog_row, dtb_row)


def _gdn_gates_bwd(p, alog_row, dtb_row, dbg, dp_in, *, bt=256):
    T = p.shape[0]
    bt = min(bt, T)

    def body(g_ref, al_ref, db_ref, dbg_ref, dpin_ref, dg_out, dal_ref, ddb_ref):
        i = pl.program_id(0)
        gt = g_ref[...]
        lane = _iota2((bt, 128), 1)
        _, uc = _cum_mats(GDN_C)
        ea = jnp.exp(al_ref[...])
        zz = gt + db_ref[...]
        g_l = -ea * _softplus(zz)
        beta_l = _sigmoid(gt)
        dbg_v = dbg_ref[...]
        dg_l = jnp.where((lane >= GDN_H) & (lane < 2 * GDN_H), _chunk_cumsum(dbg_v, uc, GDN_C), 0.0)
        dbeta_l = jnp.where(lane < GDN_H, dbg_v, 0.0)
        da = dg_l * (-ea) * _sigmoid(zz)
        dg_out[...] = (da + dbeta_l * beta_l * (1.0 - beta_l)).astype(dg_out.dtype)

        @pl.when(i == 0)
        def _():
            dal_ref[...] = jnp.zeros_like(dal_ref)
            ddb_ref[...] = jnp.zeros_like(ddb_ref)

        dal_ref[...] += jnp.sum(dg_l * g_l, axis=0, keepdims=True)
        ddb_ref[...] += jnp.sum(da, axis=0, keepdims=True)

    vec = pl.BlockSpec((1, 128), lambda i: (0, 0))
    seg = pl.BlockSpec((bt, 128), lambda i: (i, 0))
    gate = pl.BlockSpec((bt, 128), lambda i: (i, C_GATE // 128))
    return pl.pallas_call(
        body, grid=(T // bt,), in_specs=[seg, vec, vec, seg, _ANY], out_specs=[gate, vec, vec],
        out_shape=[jax.ShapeDtypeStruct((T, C_TOT), BF16), jax.ShapeDtypeStruct((1, 128), F32),
                   jax.ShapeDtypeStruct((1, 128), F32)],
        input_output_aliases={4: 0}, name="gdn_gates_bwd",
        compiler_params=_params(("arbitrary",)))(p, alog_row, dtb_row, dbg, dp_in)


def _ssd_dt_fwd(p, dtb_row, alog_x, *, bt=256):
    T = p.shape[0]
    bt = min(bt, T)

    def body(d_ref, db_ref, al_ref, dt_ref, alpha_ref):
        ed, _ = _expand_mats(6, 0)
        lc, _ = _cum_mats(SSM_L)
        dt_x = _dot_sel(_softplus(d_ref[...] + db_ref[...]), ed)
        dt_ref[...] = dt_x
        alpha_ref[...] = _chunk_cumsum(dt_x * (-jnp.exp(al_ref[...])), lc, SSM_L)

    row = pl.BlockSpec((bt, D), lambda i: (i, 0))
    return pl.pallas_call(
        body, grid=(T // bt,),
        in_specs=[pl.BlockSpec((bt, 128), lambda i: (i, 1)),
                  pl.BlockSpec((1, 128), lambda i: (0, 0)), pl.BlockSpec((1, D), lambda i: (0, 0))],
        out_specs=[row, row], out_shape=[jax.ShapeDtypeStruct((T, D), F32)] * 2,
        name="ssd_dt_fwd", compiler_params=_params(("parallel",)))(p, dtb_row, alog_x)


def _ssd_dt_bwd(p, dtb_row, alog_x, ddt_x, dalpha_x, dp_in, *, bt=256):
    T = p.shape[0]
    bt = min(bt, T)

    def body(d_ref, db_ref, al_ref, ddt_ref, dal_ref, dpin_ref, dd_out, ddb_ref, dalog_ref):
        i = pl.program_id(0)
        ed, edt = _expand_mats(6, 0)
        _, uc = _cum_mats(SSM_L)
        zz = d_ref[...] + db_ref[...]
        dt_x = _dot_sel(_softplus(zz), ed)
        a_x = -jnp.exp(al_ref[...])
        da_x = _chunk_cumsum(dal_ref[...], uc, SSM_L)
        ddt_l = _dot_sel(ddt_ref[...] + da_x * a_x, edt)
        draw = ddt_l * _sigmoid(zz)
        dd_out[...] = draw.astype(dd_out.dtype)

        @pl.when(i == 0)
        def _():
            ddb_ref[...] = jnp.zeros_like(ddb_ref)
            dalog_ref[...] = jnp.zeros_like(dalog_ref)

        ddb_ref[...] += jnp.sum(draw, axis=0, keepdims=True)
        dalog_ref[...] += jnp.sum(da_x * dt_x, axis=0, keepdims=True) * a_x

    row = pl.BlockSpec((bt, D), lambda i: (i, 0))
    seg = pl.BlockSpec((bt, 128), lambda i: (i, C_DT // 128))
    v128 = pl.BlockSpec((1, 128), lambda i: (0, 0))
    vD = pl.BlockSpec((1, D), lambda i: (0, 0))
    return pl.pallas_call(
        body, grid=(T // bt,),
        in_specs=[pl.BlockSpec((bt, 128), lambda i: (i, 1)), v128, vD, row, row, _ANY],
        out_specs=[seg, v128, vD],
        out_shape=[jax.ShapeDtypeStruct((T, C_TOT), BF16), jax.ShapeDtypeStruct((1, 128), F32),
                   jax.ShapeDtypeStruct((1, D), F32)],
        input_output_aliases={5: 0}, name="ssd_dt_bwd",
        compiler_params=_params(("arbitrary",)))(p, dtb_row, alog_x, ddt_x, dalpha_x, dp_in)


def _gdn_post_fwd(o, p, w_x, *, bt=256):
    T = o.shape[0]
    bt = min(bt, T)

    def body(o_ref, z_ref, w_ref, out_ref):
        for h in range(GDN_H):
            sl = slice(h * 128, (h + 1) * 128)
            oh = o_ref[:, sl]
            r = lax.rsqrt(jnp.mean(oh * oh, axis=1, keepdims=True) + EPS)
            out_ref[:, sl] = (oh * r * w_ref[:, sl] * _silu(z_ref[:, sl].astype(F32))).astype(out_ref.dtype)

    row = pl.BlockSpec((bt, D), lambda i: (i, 0))
    return pl.pallas_call(
        body, grid=(T // bt,),
        in_specs=[row, pl.BlockSpec((bt, D), lambda i: (i, C_ZG // D)), pl.BlockSpec((1, D), lambda i: (0, 0))],
        out_specs=row, out_shape=jax.ShapeDtypeStruct((T, 2 * D), BF16), name="gdn_post_fwd",
        compiler_params=_params(("parallel",)))(o, p, w_x)


def _gdn_post_bwd(dmix, o, p, w_x, *, bt=256):
    T = o.shape[0]
    bt = min(bt, T)

    def body(dm_ref, o_ref, z_ref, w_ref, do_ref, dz_ref, dw_ref):
        i = pl.program_id(0)

        @pl.when(i == 0)
        def _():
            dw_ref[...] = jnp.zeros_like(dw_ref)

        for h in range(GDN_H):
            sl = slice(h * 128, (h + 1) * 128)
            oh, zh, wh, dm = o_ref[:, sl], z_ref[:, sl].astype(F32), w_ref[:, sl], dm_ref[:, sl]
            r = lax.rsqrt(jnp.mean(oh * oh, axis=1, keepdims=True) + EPS)
            ohat = oh * r
            dy = dm * _silu(zh)
            dz_ref[:, sl] = (dm * ohat * wh * _dsilu(zh)).astype(dz_ref.dtype)
            dohat = dy * wh
            do_ref[:, sl] = r * (dohat - ohat * jnp.mean(dohat * ohat, axis=1, keepdims=True))
            dw_ref[:, sl] += jnp.sum(dy * ohat, axis=0, keepdims=True)

    row = pl.BlockSpec((bt, D), lambda i: (i, 0))
    zcol = pl.BlockSpec((bt, D), lambda i: (i, C_ZG // D))
    vec = pl.BlockSpec((1, D), lambda i: (0, 0))
    return pl.pallas_call(
        body, grid=(T // bt,), in_specs=[row, row, zcol, vec], out_specs=[row, zcol, vec],
        out_shape=[jax.ShapeDtypeStruct((T, D), F32), jax.ShapeDtypeStruct((T, C_TOT), BF16),
                   jax.ShapeDtypeStruct((1, D), F32)],
        name="gdn_post_bwd", compiler_params=_params(("arbitrary",)))(dmix, o, p, w_x)


def _ssd_post_fwd(y, xs, p, d_x, w, mix_in, *, bt=256):
    T = y.shape[0]
    bt = min(bt, T)

    def body(y_ref, x_ref, z_ref, d_ref, w_ref, mix_ref, out_ref):
        yg = (y_ref[...] + x_ref[...] * d_ref[...]) * _silu(z_ref[...].astype(F32))
        for g in range(2):
            sl = slice(g * 512, (g + 1) * 512)
            a = yg[:, sl]
            r = lax.rsqrt(jnp.mean(a * a, axis=1, keepdims=True) + EPS)
            out_ref[:, sl] = (a * r * w_ref[:, sl]).astype(out_ref.dtype)

    row = pl.BlockSpec((bt, D), lambda i: (i, 0))
    vec = pl.BlockSpec((1, D), lambda i: (0, 0))
    return pl.pallas_call(
        body, grid=(T // bt,),
        in_specs=[row, row, pl.BlockSpec((bt, D), lambda i: (i, C_ZS // D)), vec, vec, _ANY],
        out_specs=pl.BlockSpec((bt, D), lambda i: (i, 1)), out_shape=jax.ShapeDtypeStruct((T, 2 * D), BF16),
        input_output_aliases={5: 0}, name="ssd_post_fwd",
        compiler_params=_params(("parallel",)))(y, xs, p, d_x, w, mix_in)


def _ssd_post_bwd(dmix, y, xs, p, d_x, w, dp_in, *, bt=256):
    T = y.shape[0]
    bt = min(bt, T)

    def body(dm_ref, y_ref, x_ref, z_ref, d_ref, w_ref, dpin_ref, dyy_ref, dz_ref, dd_ref, dw_ref):
        i = pl.program_id(0)

        @pl.when(i == 0)
        def _():
            dd_ref[...] = jnp.zeros_like(dd_ref)
            dw_ref[...] = jnp.zeros_like(dw_ref)

        xv, zv = x_ref[...], z_ref[...].astype(F32)
        yy = y_ref[...] + xv * d_ref[...]
        sz = _silu(zv)
        yg = yy * sz
        parts = []
        for g in range(2):
            sl = slice(g * 512, (g + 1) * 512)
            a = yg[:, sl]
            r = lax.rsqrt(jnp.mean(a * a, axis=1, keepdims=True) + EPS)
            ah = a * r
            dout = dm_ref[:, sl]
            dah = dout * w_ref[:, sl]
            dw_ref[:, sl] += jnp.sum(dout * ah, axis=0, keepdims=True)
            parts.append(r * (dah - ah * jnp.mean(dah * ah, axis=1, keepdims=True)))
        dyg = jnp.concatenate(parts, axis=1)
        dyy = dyg * sz
        dyy_ref[...] = dyy
        dz_ref[...] = (dyg * yy * _dsilu(zv)).astype(dz_ref.dtype)
        dd_ref[...] += jnp.sum(dyy * xv, axis=0, keepdims=True)

    row = pl.BlockSpec((bt, D), lambda i: (i, 0))
    zcol = pl.BlockSpec((bt, D), lambda i: (i, C_ZS // D))
    vec = pl.BlockSpec((1, D), lambda i: (0, 0))
    return pl.pallas_call(
        body, grid=(T // bt,),
        in_specs=[pl.BlockSpec((bt, D), lambda i: (i, 1)), row, row, zcol, vec, vec, _ANY],
        out_specs=[row, zcol, vec, vec],
        out_shape=[jax.ShapeDtypeStruct((T, D), F32), jax.ShapeDtypeStruct((T, C_TOT), BF16),
                   jax.ShapeDtypeStruct((1, D), F32), jax.ShapeDtypeStruct((1, D), F32)],
        input_output_aliases={6: 1}, name="ssd_post_bwd",
        compiler_params=_params(("arbitrary",)))(dmix, y, xs, p, d_x, w, dp_in)


_NEG = -1e30


def _gdn_terms(q, k, v, bx, gam_c):
    C = GDN_C
    ri, ci = _iota2((C, C), 0), _iota2((C, C), 1)
    eye, low, strict = ri == ci, ri >= ci, ri > ci
    gam_r = jnp.sum(jnp.where(eye, gam_c, 0.0), axis=0, keepdims=True)
    G = jnp.exp(jnp.where(low, gam_c - gam_r, _NEG))
    glast = jnp.sum(jnp.where(_iota2((C, 1), 0) == C - 1, gam_c, 0.0), axis=0, keepdims=True)
    eg, egl, eL = jnp.exp(gam_c), jnp.exp(glast - gam_c), jnp.exp(glast)
    kb, vb = k * bx, v * bx
    M = _dot(kb, k, _NT)
    return dict(eye=eye, low=low, strict=strict, G=G, eg=eg, egl=egl, eL=eL, kb=kb, vb=vb, M=M,
                kbg=kb * eg, qd=q * eg, kd=k * egl, q=q, k=k, v=v, bx=bx)


def _split(a):
    hi = a.astype(_MXU)
    return hi, (a - hi.astype(F32)).astype(_MXU)


def _dot3s(a, b):
    d = lambda p, q: lax.dot_general(p, q, _NN, preferred_element_type=F32)
    return d(a[0], b[0]) + d(a[0], b[1]) + d(a[1], b[0])


def _tri_inv_many(Ls, eye):
    eyef = jnp.where(eye, 1.0, 0.0)
    Ts = [eyef - L for L in Ls]
    Ps = [-L for L in Ls]
    for _ in range(5):
        sp = [_split(p) for p in Ps]
        Ps = [_dot3s(s, s) for s in sp]
        sp = [_split(p) for p in Ps]
        st = [_split(t) for t in Ts]
        Ts = [t + _dot3s(a, b) for t, a, b in zip(Ts, st, sp)]
    return Ts


def _lane_col(tile, idx):
    return jnp.sum(jnp.where(_iota2(tile.shape, 1) == idx, tile, 0.0), axis=1, keepdims=True)


def _gdn_heads(q_ref, k_ref, v_ref, bg_ref, heads):
    out = []
    bg = bg_ref[...]
    for h in heads:
        sl = slice(h * 128, (h + 1) * 128)
        out.append(_gdn_terms(q_ref[:, sl], k_ref[:, sl], v_ref[:, sl], _lane_col(bg, h), _lane_col(bg, GDN_H + h)))
    return out


def _gdn_prep(qk, v, bg, ride=None):
    T = qk.shape[0]
    N = T // GDN_C
    C, CS = GDN_C, GDN_LOCAL_CHUNKS
    NB = N // CS
    n_ride = ride.n if ride else 0

    def body(q_ref, k_ref, v_ref, bg_ref, *rest):
        ride_in = rest[:n_ride]
        u_ref, w_ref, qd_ref, kd_ref, p_ref, t_ref = rest[n_ride:n_ride + 6]
        ride_out = rest[n_ride + 6:2 * n_ride + 6]
        if ride:
            @pl.when(pl.program_id(0) == 0)
            def _():
                ride.start(ride_in, ride_out, rest[-3:])

            @pl.when(pl.program_id(0) == NB - 1)
            def _():
                ride.wait(ride_in, ride_out, rest[-3:])

        items = [(c, h) for c in range(CS) for h in range(GDN_H)]
        views = [[r.at[pl.ds(c * C, C)] for r in (q_ref, k_ref, v_ref, bg_ref)] for c in range(CS)]
        ts = [_gdn_heads(*views[c], [h])[0] for c, h in items]
        Ts = _tri_inv_many([jnp.where(t["strict"], t["M"] * t["G"], 0.0) for t in ts], ts[0]["eye"])
        for (c, h), t, Tm in zip(items, ts, Ts):
            tok = slice(c * C, (c + 1) * C)
            sl = slice(h * 128, (h + 1) * 128)
            rows = slice(h * C, (h + 1) * C)
            u_ref[tok, sl] = _dot(Tm, t["vb"])
            w_ref[tok, sl] = _dot(Tm, t["kbg"]).astype(w_ref.dtype)
            qd_ref[tok, sl] = t["qd"].astype(qd_ref.dtype)
            kd_ref[tok, sl] = t["kd"].astype(kd_ref.dtype)
            p_ref[c, rows, :] = _dot(t["q"], t["k"], _NT) * t["G"]
            t_ref[c, rows, :] = Tm

    blk = lambda c: pl.BlockSpec((CS * C, D), lambda n: (n, c))
    sq = pl.BlockSpec((CS, GDN_H * C, C), lambda n: (n, 0, 0))
    in_specs = [blk(0), blk(1), blk(0), pl.BlockSpec((CS * C, 128), lambda n: (n, 0))]
    out_specs = [blk(0), blk(0), blk(0), blk(0), sq, sq]
    out_shape = [jax.ShapeDtypeStruct((T, D), F32), jax.ShapeDtypeStruct((T, D), BF16),
                 jax.ShapeDtypeStruct((T, D), BF16), jax.ShapeDtypeStruct((T, D), BF16),
                 jax.ShapeDtypeStruct((N, GDN_H * C, C), F32), jax.ShapeDtypeStruct((N, GDN_H * C, C), F32)]
    ins = [qk, qk, v, bg]
    if ride:
        ins, in_specs = ins + ride.srcs, in_specs + ride.specs
        out_shape, out_specs = out_shape + ride.out_shape, out_specs + ride.specs
    res = pl.pallas_call(
        body, grid=(NB,), in_specs=in_specs, out_specs=out_specs, out_shape=out_shape,
        scratch_shapes=ride.scratch if ride else [], name="gdn_prep",
        compiler_params=_params(("arbitrary",) if ride else ("parallel",)))(*ins)
    return (list(res[:6]), list(res[6:])) if ride else list(res)


def _gdn_scan_fwd(u, w, qd, kd, pm, bg):
    T = u.shape[0]
    N = T // GDN_C
    C, CS = GDN_C, GDN_SCAN_CHUNKS

    def body(u_ref, w_ref, qd_ref, kd_ref, p_ref, bg_ref, o_ref, vn_ref, ss_ref, S_scr):
        n = pl.program_id(0)

        @pl.when(n == 0)
        def _():
            S_scr[...] = jnp.zeros_like(S_scr)

        sls = [slice(h * 128, (h + 1) * 128) for h in range(GDN_H)]
        for c in range(CS):
            rows = slice(c * C, (c + 1) * C)
            glast = bg_ref[(c + 1) * C - 1:(c + 1) * C, :]
            Ss = [S_scr[:, sl] for sl in sls]
            vns = [u_ref[rows, sl] - _dot(w_ref[rows, sl], S) for sl, S in zip(sls, Ss)]
            for h, (sl, S, vn) in enumerate(zip(sls, Ss, vns)):
                ss_ref[c, :, sl] = S.astype(ss_ref.dtype)
                vn_ref[rows, sl] = vn.astype(vn_ref.dtype)
                o_ref[rows, sl] = _dot(qd_ref[rows, sl], S) + _dot(p_ref[c, h * C:(h + 1) * C, :], vn)
                S_scr[:, sl] = S * jnp.exp(_lane_col(glast, GDN_H + h)) + _dot(kd_ref[rows, sl], vn, _TN)

    blk = pl.BlockSpec((CS * C, D), lambda n: (n, 0))
    return dict(
        body=body, steps=N // CS, ins=[u, w, qd, kd, pm, bg],
        in_specs=[blk, blk, blk, blk, pl.BlockSpec((CS, GDN_H * C, C), lambda n: (n, 0, 0)),
                  pl.BlockSpec((CS * C, 128), lambda n: (n, 0))],
        out_specs=[blk, blk, pl.BlockSpec((CS, GDN_DK, D), lambda n: (n, 0, 0))],
        out_shape=[jax.ShapeDtypeStruct((T, D), F32), jax.ShapeDtypeStruct((T, D), BF16),
                   jax.ShapeDtypeStruct((N, GDN_DK, D), BF16)],
        scratch=[pltpu.VMEM((GDN_DK, D), F32)])


def _gdn_scan_bwd(w, qd, kd, pm, bg, do):
    T = w.shape[0]
    N = T // GDN_C
    C, CS = GDN_C, GDN_SCAN_CHUNKS
    NB = N // CS

    def body(w_ref, qd_ref, kd_ref, p_ref, bg_ref, do_ref, dvn_ref, ds_ref, dS_scr):
        n = pl.program_id(0)

        @pl.when(n == 0)
        def _():
            dS_scr[...] = jnp.zeros_like(dS_scr)

        sls = [slice(h * 128, (h + 1) * 128) for h in range(GDN_H)]
        for c in reversed(range(CS)):
            rows = slice(c * C, (c + 1) * C)
            glast = bg_ref[(c + 1) * C - 1:(c + 1) * C, :]
            dSs = [dS_scr[:, sl] for sl in sls]
            dvns = [_dot(p_ref[c, h * C:(h + 1) * C, :], do_ref[rows, sl], _TN) + _dot(kd_ref[rows, sl], dS2)
                    for h, (sl, dS2) in enumerate(zip(sls, dSs))]
            for h, (sl, dS2, dvn) in enumerate(zip(sls, dSs, dvns)):
                ds_ref[c, :, sl] = dS2.astype(ds_ref.dtype)
                dvn_ref[rows, sl] = dvn.astype(dvn_ref.dtype)
                dS_scr[:, sl] = (dS2 * jnp.exp(_lane_col(glast, GDN_H + h))
                                 + _dot(qd_ref[rows, sl], do_ref[rows, sl], _TN) - _dot(w_ref[rows, sl], dvn, _TN))

    blk = pl.BlockSpec((CS * C, D), lambda n: (NB - 1 - n, 0))
    return dict(
        body=body, steps=NB, ins=[w, qd, kd, pm, bg, do],
        in_specs=[blk, blk, blk, pl.BlockSpec((CS, GDN_H * C, C), lambda n: (NB - 1 - n, 0, 0)),
                  pl.BlockSpec((CS * C, 128), lambda n: (NB - 1 - n, 0)), blk],
        out_specs=[blk, pl.BlockSpec((CS, GDN_DK, D), lambda n: (NB - 1 - n, 0, 0))],
        out_shape=[jax.ShapeDtypeStruct((T, D), BF16), jax.ShapeDtypeStruct((N, GDN_DK, D), BF16)],
        scratch=[pltpu.VMEM((GDN_DK, D), F32)])


def _gdn_rest_bwd(qk, v, bg, s_save, t_save, vn, dvn, ds_save, do, ride=None):
    T = qk.shape[0]
    N = T // GDN_C
    C, CS = GDN_C, GDN_REST_CHUNKS
    NB = N // CS
    n_ride = ride.n if ride else 0

    def body(q_ref, k_ref, v_ref, bg_ref, ss_ref, ts_ref, vn_ref, dvn_ref, ds_ref, do_ref, *rest):
        ride_in = rest[:n_ride]
        dqkv_ref, dbg_ref = rest[n_ride:n_ride + 2]
        ride_out = rest[n_ride + 2:2 * n_ride + 2]
        if ride:
            @pl.when(pl.program_id(0) == 0)
            def _():
                ride.start(ride_in, ride_out, rest[-3:])

            @pl.when(pl.program_id(0) == NB - 1)
            def _():
                ride.wait(ride_in, ride_out, rest[-3:])

        items = [(c, h) for c in range(CS) for h in range(GDN_H)]
        toks = [slice(c * C, (c + 1) * C) for c, _ in items]
        sls = [slice(h * 128, (h + 1) * 128) for _, h in items]
        views = [[r.at[pl.ds(c * C, C)] for r in (q_ref, k_ref, v_ref, bg_ref)] for c in range(CS)]
        ts = [_gdn_heads(*views[c], [h])[0] for c, h in items]
        Ss = [ss_ref[c, :, sl] for (c, _), sl in zip(items, sls)]
        Tms = [ts_ref[c, h * C:(h + 1) * C, :] for c, h in items]
        dS2s = [ds_ref[c, :, sl] for (c, _), sl in zip(items, sls)]
        dos = [do_ref[tok, sl] for tok, sl in zip(toks, sls)]
        vns = [vn_ref[tok, sl] for tok, sl in zip(toks, sls)]
        dvns = [dvn_ref[tok, sl] for tok, sl in zip(toks, sls)]
        Qs = [_dot(t["q"], t["k"], _NT) for t in ts]
        dws = [-_dot(dvn, S, _NT) for dvn, S in zip(dvns, Ss)]
        dqds = [_dot(do, S, _NT) for do, S in zip(dos, Ss)]
        dPs = [jnp.where(t["low"], _dot(do, vn, _NT), 0.0) for t, do, vn in zip(ts, dos, vns)]
        dkds = [_dot(vn, dS2, _NT) for vn, dS2 in zip(vns, dS2s)]
        dTs = [_dot(dvn, t["vb"], _NT) + _dot(dw, t["kbg"], _NT) for t, dvn, dw in zip(ts, dvns, dws)]
        dvbs = [_dot(Tm, dvn, _TN) for Tm, dvn in zip(Tms, dvns)]
        dkbgs = [_dot(Tm, dw, _TN) for Tm, dw in zip(Tms, dws)]
        TdTs = [_dot(Tm, dT, _TN) for Tm, dT in zip(Tms, dTs)]
        dLs = [jnp.where(t["strict"], -_dot(TdT, Tm, _NT), 0.0) for t, TdT, Tm in zip(ts, TdTs, Tms)]
        dMs = [dL * t["G"] for t, dL in zip(ts, dLs)]
        dQs = [dP * t["G"] for t, dP in zip(ts, dPs)]
        dkbs = [_dot(dM, t["k"]) + dkbg * t["eg"] for t, dM, dkbg in zip(ts, dMs, dkbgs)]
        rs = lambda a: jnp.sum(a, axis=1, keepdims=True)
        lane = _iota2((C, 128), 1)
        last = _iota2((C, 1), 0) == C - 1
        dbg = [jnp.zeros((C, 128), F32) for _ in range(CS)]
        for i, (c, h) in enumerate(items):
            t, sl, tok = ts[i], sls[i], toks[i]
            E = (dLs[i] * t["M"] + dPs[i] * Qs[i]) * t["G"]
            dqkv_ref[tok, sl] = _dot(dQs[i], t["k"]) + dqds[i] * t["eg"]
            dqkv_ref[tok, D + h * 128:D + (h + 1) * 128] = (
                _dot(dQs[i], t["q"], _TN) + _dot(dMs[i], t["kb"], _TN) + dkds[i] * t["egl"] + dkbs[i] * t["bx"])
            dqkv_ref[tok, 2 * D + h * 128:2 * D + (h + 1) * 128] = dvbs[i] * t["bx"]
            dbeta_c = rs(dkbs[i] * t["k"] + dvbs[i] * t["v"])
            dkd_kd = dkds[i] * t["kd"]
            dgam_c = rs(dqds[i] * t["qd"]) + rs(dkbgs[i] * t["kbg"]) - rs(dkd_kd) + rs(E)
            dgam_r = -jnp.sum(E, axis=0, keepdims=True)
            dgam_c = dgam_c + jnp.sum(jnp.where(t["eye"], dgam_r, 0.0), axis=1, keepdims=True)
            dlast = _sum_all(dkd_kd) + t["eL"] * _sum_all(Ss[i].astype(F32) * dS2s[i].astype(F32))
            dgam_c = dgam_c + jnp.where(last, dlast, 0.0)
            dbg[c] = dbg[c] + jnp.where(lane == h, dbeta_c, 0.0) + jnp.where(lane == GDN_H + h, dgam_c, 0.0)
        for c in range(CS):
            dbg_ref[c * C:(c + 1) * C, :] = dbg[c]

    blk = lambda c: pl.BlockSpec((CS * C, D), lambda n: (n, c))
    st = pl.BlockSpec((CS, GDN_DK, D), lambda n: (n, 0, 0))
    seg = pl.BlockSpec((CS * C, 128), lambda n: (n, 0))
    in_specs = [blk(0), blk(1), blk(0), seg, st,
                pl.BlockSpec((CS, GDN_H * C, C), lambda n: (n, 0, 0)), blk(0), blk(0), st, blk(0)]
    out_specs = [pl.BlockSpec((CS * C, 3 * D), lambda n: (n, 0)), seg]
    out_shape = [jax.ShapeDtypeStruct((T, 3 * D), F32), jax.ShapeDtypeStruct((T, 128), F32)]
    ins = [qk, qk, v, bg, s_save, t_save, vn, dvn, ds_save, do]
    if ride:
        ins, in_specs = ins + ride.srcs, in_specs + ride.specs
        out_shape, out_specs = out_shape + ride.out_shape, out_specs + ride.specs
    res = pl.pallas_call(
        body, grid=(NB,), in_specs=in_specs, out_specs=out_specs, out_shape=out_shape,
        scratch_shapes=ride.scratch if ride else [], name="gdn_rest_bwd",
        compiler_params=_params(("arbitrary",) if ride else ("parallel",)))(*ins)
    return (list(res[:2]), list(res[2:])) if ride else list(res)


def _ssd_seg(al_pair, half, s):
    L = SSM_L
    ri, ci = _iota2((L, L), 0), _iota2((L, L), 1)
    ac = jnp.max(jnp.where(half == s, al_pair, _NEG), axis=1, keepdims=True)
    ar = jnp.sum(jnp.where(ri == ci, ac, 0.0), axis=0, keepdims=True)
    return jnp.exp(jnp.where(ri >= ci, ac - ar, _NEG))


def _last_row(a):
    return jnp.sum(jnp.where(_iota2((a.shape[0], 1), 0) == a.shape[0] - 1, a, 0.0), axis=0, keepdims=True)


def _ssd_core_fwd(xbc, dtx, alx):
    T = xbc.shape[0]
    L, CS = SSM_L, SSM_SCAN_CHUNKS
    Nc = T // L

    def body(x_all, bc_all, dt_all, al_all, y_all, hs_all, H_scr):
        @pl.when(pl.program_id(0) == 0)
        def _():
            H_scr[...] = jnp.zeros_like(H_scr)

        for cc in range(CS):
            rows = pl.ds(cc * L, L)
            chunk(x_all.at[rows], bc_all.at[rows], dt_all.at[rows], al_all.at[rows], y_all.at[rows], hs_all.at[cc],
                  H_scr)

    def chunk(x_ref, bc_ref, dt_ref, al_ref, y_ref, hs_ref, H_scr):
        half = _iota2((L, 128), 1) >> 6
        for g in range(2):
            gs = slice(g * 512, (g + 1) * 512)
            Bg = bc_ref[:, g * 128:(g + 1) * 128]
            Cg = bc_ref[:, 256 + g * 128:256 + (g + 1) * 128]
            alg = al_ref[:, gs]
            alast = _last_row(alg)
            xdt = x_ref[:, gs] * dt_ref[:, gs]
            Hg = H_scr[:, gs]
            hs_ref[:, gs] = Hg
            CB = _dot(Cg, Bg, _NT)
            y_ref[:, gs] = jnp.exp(alg) * _dot(Cg, Hg)
            H_scr[:, gs] = Hg * jnp.exp(alast) + _dot(Bg, jnp.exp(alast - alg) * xdt, _TN)
            for j in range(4):
                ps = slice(g * 512 + j * 128, g * 512 + (j + 1) * 128)
                al_pair = al_ref[:, ps]
                xp = x_ref[:, ps] * dt_ref[:, ps]
                ys = [_dot(_ssd_seg(al_pair, half, s) * CB, xp) for s in range(2)]
                y_ref[:, ps] += jnp.where(half == 0, ys[0], ys[1])

    row = pl.BlockSpec((CS * L, D), lambda c: (c, 0))
    return dict(
        body=body, steps=Nc // CS, ins=[xbc, xbc, dtx, alx],
        in_specs=[row, pl.BlockSpec((CS * L, 512), lambda c: (c, 2)), row, row],
        out_specs=[row, pl.BlockSpec((CS, SSM_N, D), lambda c: (c, 0, 0))],
        out_shape=[jax.ShapeDtypeStruct((T, D), F32), jax.ShapeDtypeStruct((Nc, SSM_N, D), F32)],
        scratch=[pltpu.VMEM((SSM_N, D), F32)])


def _ssd_core_bwd(xbc, dtx, alx, h_save, dyy, d_x):
    T = xbc.shape[0]
    L, CS = SSM_L, SSM_SCAN_CHUNKS
    Nc = T // L
    NB = Nc // CS

    def body(x_all, bc_all, dt_all, al_all, hs_all, dy_all, d_ref, dx_all, ddt_all, dal_all, dH_scr):
        @pl.when(pl.program_id(0) == 0)
        def _():
            dH_scr[...] = jnp.zeros_like(dH_scr)

        for cc in reversed(range(CS)):
            rows = pl.ds(cc * L, L)
            chunk(x_all.at[rows], bc_all.at[rows], dt_all.at[rows], al_all.at[rows], hs_all.at[cc], dy_all.at[rows],
                  d_ref, dx_all.at[rows], ddt_all.at[rows], dal_all.at[rows], dH_scr)

    def chunk(x_ref, bc_ref, dt_ref, al_ref, hs_ref, dy_ref, d_ref, dx_ref, ddt_ref, dal_ref, dH_scr):
        lane = _iota2((L, 128), 1)
        half = lane >> 6
        rowi = _iota2((L, 1), 0)
        ri, ci = _iota2((L, L), 0), _iota2((L, L), 1)
        for g in range(2):
            gs = slice(g * 512, (g + 1) * 512)
            Bg = bc_ref[:, g * 128:(g + 1) * 128]
            Cg = bc_ref[:, 256 + g * 128:256 + (g + 1) * 128]
            alg = al_ref[:, gs]
            alast = _last_row(alg)
            eal, edec, eL = jnp.exp(alg), jnp.exp(alast - alg), jnp.exp(alast)
            xg, dtg, dYg = x_ref[:, gs], dt_ref[:, gs], dy_ref[:, gs]
            xdt = xg * dtg
            Hg = hs_ref[:, gs]
            dH2 = dH_scr[:, gs]
            CB = _dot(Cg, Bg, _NT)
            dYe = eal * dYg
            dH_scr[:, gs] = dH2 * eL + _dot(Cg, dYe, _TN)
            dC = _dot(dYe, Hg, _NT)
            zg = edec * xdt
            dz = _dot(Bg, dH2)
            dB = _dot(zg, dH2, _NT)
            tz = dz * zg
            dal = dYe * _dot(Cg, Hg) - tz
            dalast = jnp.sum(tz, axis=0, keepdims=True) + eL * jnp.sum(Hg * dH2, axis=0, keepdims=True)
            dal = dal + jnp.where(rowi == L - 1, dalast, 0.0)
            dxdt_g = edec * dz
            dx_ref[:, gs] = dxdt_g * dtg + dYg * d_ref[:, gs]
            ddt_ref[:, gs] = dxdt_g * xg
            dal_ref[:, gs] = dal
            dCB = jnp.zeros((L, L), F32)
            for j in range(4):
                ps = slice(g * 512 + j * 128, g * 512 + (j + 1) * 128)
                al_pair = al_ref[:, ps]
                xp = x_ref[:, ps] * dt_ref[:, ps]
                dYp = dy_ref[:, ps]
                dxp = []
                dal_p = jnp.zeros((L, 128), F32)
                for s in range(2):
                    seg = _ssd_seg(al_pair, half, s)
                    W = seg * CB
                    dW = _dot(jnp.where(half == s, dYp, 0.0), xp, _NT)
                    dxp.append(_dot(W, dYp, _TN))
                    dCB = dCB + dW * seg
                    Es = dW * W
                    dac = jnp.sum(Es, axis=1, keepdims=True) - jnp.sum(
                        jnp.where(ri == ci, jnp.sum(Es, axis=0, keepdims=True), 0.0), axis=1, keepdims=True)
                    dal_p = dal_p + jnp.where(lane == 64 * s, dac, 0.0)
                dxdt_p = jnp.where(half == 0, dxp[0], dxp[1])
                dx_ref[:, ps] += dxdt_p * dt_ref[:, ps]
                ddt_ref[:, ps] += dxdt_p * x_ref[:, ps]
                dal_ref[:, ps] += dal_p
            dx_ref[:, D + g * 128:D + (g + 1) * 128] = dB + _dot(dCB, Cg, _TN)
            dx_ref[:, D + 256 + g * 128:D + 256 + (g + 1) * 128] = dC + _dot(dCB, Bg)

    row = pl.BlockSpec((CS * L, D), lambda c: (NB - 1 - c, 0))
    bcs = pl.BlockSpec((CS * L, 512), lambda c: (NB - 1 - c, 2))
    return dict(
        body=body, steps=NB, ins=[xbc, xbc, dtx, alx, h_save, dyy, d_x],
        in_specs=[row, bcs, row, row, pl.BlockSpec((CS, SSM_N, D), lambda c: (NB - 1 - c, 0, 0)), row,
                  pl.BlockSpec((1, D), lambda c: (0, 0))],
        out_specs=[pl.BlockSpec((CS * L, D + 512), lambda c: (NB - 1 - c, 0)), row, row],
        out_shape=[jax.ShapeDtypeStruct((T, D + 512), F32),
                   jax.ShapeDtypeStruct((T, D), F32), jax.ShapeDtypeStruct((T, D), F32)],
        scratch=[pltpu.VMEM((SSM_N, D), F32)])


def _run_scans(parts, *, name):
    steps = parts[0]["steps"]
    assert all(p["steps"] == steps for p in parts)
    cnt = lambda key: [len(p[key]) for p in parts]
    n_in, n_out, n_scr = cnt("ins"), cnt("out_shape"), cnt("scratch")

    def body(*refs):
        ins, outs, scr = refs[:sum(n_in)], refs[sum(n_in):sum(n_in) + sum(n_out)], refs[sum(n_in) + sum(n_out):]
        oi = oo = os_ = 0
        for p, a, b, c in zip(parts, n_in, n_out, n_scr):
            p["body"](*ins[oi:oi + a], *outs[oo:oo + b], *scr[os_:os_ + c])
            oi, oo, os_ = oi + a, oo + b, os_ + c

    cat = lambda key: [v for p in parts for v in p[key]]
    res = pl.pallas_call(
        body, grid=(steps,), in_specs=cat("in_specs"), out_specs=cat("out_specs"), out_shape=cat("out_shape"),
        scratch_shapes=cat("scratch"), name=name, compiler_params=_params(("arbitrary",)))(*cat("ins"))
    out, o = [], 0
    for b in n_out:
        out.append(list(res[o:o + b]))
        o += b
    return out


_EARLY = ("w_out", "wq_mem", "wk_mem", "wv_mem", "wo_mem")
_LATE = ("w_up", "w_down")
_GRADS_MLP = ("w_down", "w_up")
_GRADS_MID = ("wo_mem", "wq_mem", "wk_mem", "wv_mem", "w_out")


def _gather_ride(shards, names):
    return None if shards is None else _Ride([shards[n] for n in names], shard=True)


def _grad_ride(shards, G, names):
    if shards is None:
        return None
    return _Ride([G[n] if G[n].ndim == 3 else _slots_from_full(n, G[n]) for n in names], shard=False)


def _local_step(x, mem, tgt, W, shards=None):
    T = x.shape[0]
    W = dict(W)
    cw_qk, cw_v = W["gdn_conv_w"][:, :2 * D], W["gdn_conv_w"][:, 2 * D:]
    h1 = _rmsnorm_fwd(x, W["norm1_w"], name="norm1_fwd")
    ride = _gather_ride(shards, _EARLY)
    pg = _mm(h1, W["w_in_pad"], b_cols=(C_GATE, C_TOT - C_GATE), name="in_proj_gates")
    p = _mm(h1, W["w_in_pad"], b_cols=(0, C_GATE), out_dtype=BF16, bn_cap=1664, name="in_proj", ride=ride)
    if ride:
        p, got = p
        W.update({n: _full_from_slots(n, g) for n, g in zip(_EARLY, got)})
    qk = _conv_fwd(p, C_QKV, 2 * D, cw_qk, None, l2=True, name="gdn_conv_qk_fwd")
    v_g = _conv_fwd(p, C_QKV + 2 * D, D, cw_v, None, l2=False, name="gdn_conv_v_fwd")
    bg = _gdn_gates_fwd(pg, W["gdn_alog_row"], W["gdn_dtb_row"])
    ride = _gather_ride(shards, _LATE)
    prep = _gdn_prep(qk, v_g, bg, ride)
    if ride:
        prep, got = prep
        W.update({n: (g if n in _COL_SHARDED else _full_from_slots(n, g)) for n, g in zip(_LATE, got)})
    u_g, w_g, qd_g, kd_g, p_g, t_save = prep
    xbc = _conv_fwd(p, C_XBC, D + 512, W["ssm_conv_w"], W["ssm_conv_b"], l2=False, name="ssm_conv_fwd", bc=512)
    dtx, alx = _ssd_dt_fwd(pg, W["ssm_dtb_row"], W["ssm_alog_x"])
    (o_g, vn_g, s_save), (y_s, h_save) = _run_scans(
        [_gdn_scan_fwd(u_g, w_g, qd_g, kd_g, p_g, bg), _ssd_core_fwd(xbc, dtx, alx)], name="scans_fwd")
    mix = _gdn_post_fwd(o_g, p, W["gdn_norm_x"])
    mix = _ssd_post_fwd(y_s, xbc, p, W["ssm_d_x"], W["ssm_norm_w"].reshape(1, D), mix)
    x1, h2 = _mm(mix, W["w_out"], epi="res_norm", extra=(x, W["norm2_w"]), bm=512, name="out_proj")
    qm = _mm(h2, W["wq_mem"], out_dtype=BF16, name="q_proj")
    m = _rmsnorm_fwd(mem, W["mem_norm_w"], name="mem_norm_fwd")
    km = _mm(m, W["wk_mem"], name="k_proj")
    vm = _mm(m, W["wv_mem"], name="v_proj")
    oa = _attn_fwd(qm, km, vm)
    x2, h3 = _mm(oa, W["wo_mem"], epi="res_norm", extra=(x1, W["norm3_w"]), bm=512, name="o_proj")
    u, act = _mm(h3, W["w_up"], epi="relu2", out_dtype=BF16, name="mlp_up")
    dx3, g_final, loss = _mm(act, W["w_down"], epi="res_loss", extra=(x2, tgt, W["final_norm_w"]), bk_cap=1024,
                             name="mlp_down_loss")
    G = {"final_norm_w": g_final.reshape(D)}
    dpre = _mm(dx3, W["w_down"], dims="nt", epi="mul2", extra=u, out_dtype=BF16, name="mlp_down_dx")
    G["w_down"] = _mm(act, dx3, dims="tn", out_dtype=BF16, name="mlp_down_dw")
    G["w_up"] = _mm(h3, dpre, dims="tn", out_dtype=BF16, out_slabs=W["w_up"].shape[0] if W["w_up"].ndim == 3 else 0,
                    name="mlp_up_dw")
    dx2, gw = _mm(dpre, W["w_up"], dims="nt", epi="norm_bwd", extra=(x2, dx3, W["norm3_w"]), bk_cap=1024,
                  name="mlp_up_dx")
    G["norm3_w"] = gw.reshape(D)
    do_a = _mm(dx2, W["wo_mem"], dims="nt", out_dtype=BF16, name="o_proj_dx")
    G["wo_mem"] = _mm(oa, dx2, dims="tn", out_dtype=BF16, name="o_proj_dw")
    dq, dk, dv = _attn_bwd(qm, km, vm, do_a)
    G["wq_mem"] = _mm(h2, dq, dims="tn", out_dtype=BF16, name="q_proj_dw")
    dx1, gw = _mm(dq, W["wq_mem"], dims="nt", epi="norm_bwd", extra=(x1, dx2, W["norm2_w"]), bm=512,
                  name="q_proj_dx")
    G["norm2_w"] = gw.reshape(D)
    G["wk_mem"] = _mm(m, dk, dims="tn", out_dtype=BF16, name="k_proj_dw")
    G["wv_mem"] = _mm(m, dv, dims="tn", out_dtype=BF16, name="v_proj_dw")
    dm = _mm(dk, W["wk_mem"], dims="nt", name="k_proj_dx")
    dm = _mm(dv, W["wv_mem"], dims="nt", epi="res", extra=dm, name="v_proj_dx")
    _, G["mem_norm_w"] = _rmsnorm_bwd(mem, W["mem_norm_w"], dm, None, name="mem_norm_bwd")
    dmix = _mm(dx1, W["w_out"], dims="nt", name="out_proj_dx")
    G["w_out"] = _mm(mix, dx1, dims="tn", out_dtype=BF16, name="out_proj_dw")
    do_g, dp, G["gdn_norm_x"] = _gdn_post_bwd(dmix, o_g, p, W["gdn_norm_x"])
    dyy, dp, G["ssm_d_x"], G["ssm_norm_w"] = _ssd_post_bwd(dmix, y_s, xbc, p, W["ssm_d_x"],
                                                          W["ssm_norm_w"].reshape(1, D), dp)
    (dvn_g, ds_save), (dxbc, ddtx, dalx) = _run_scans(
        [_gdn_scan_bwd(w_g, qd_g, kd_g, p_g, bg, do_g), _ssd_core_bwd(xbc, dtx, alx, h_save, dyy, W["ssm_d_x"])],
        name="scans_bwd")
    ride = _grad_ride(shards, G, _GRADS_MLP)
    rest = _gdn_rest_bwd(qk, v_g, bg, s_save, t_save, vn_g, dvn_g, ds_save, do_g, ride)
    if ride:
        rest, got = rest
        G.update(zip(_GRADS_MLP, got))
    dqkvn, dbg = rest
    dy_qk, gcw_qk, _ = _conv_bwd_act(p, C_QKV, 2 * D, cw_qk, None, dqkvn, 0, l2=True, name="gdn_conv_qk_bwd_act")
    dy_v, gcw_v, _ = _conv_bwd_act(p, C_QKV + 2 * D, D, cw_v, None, dqkvn, 2 * D, l2=False,
                                   name="gdn_conv_v_bwd_act")
    G["gdn_conv_w"] = jnp.concatenate([gcw_qk, gcw_v], axis=1)
    dp = _conv_bwd_in(dy_qk, cw_qk, dp, C_QKV, T, name="gdn_conv_qk_bwd_in")
    dp = _conv_bwd_in(dy_v, cw_v, dp, C_QKV + 2 * D, T, name="gdn_conv_v_bwd_in")
    dp, G["gdn_alog_row"], G["gdn_dtb_row"] = _gdn_gates_bwd(pg, W["gdn_alog_row"], W["gdn_dtb_row"], dbg, dp)
    dy_s, G["ssm_conv_w"], G["ssm_conv_b"] = _conv_bwd_act(p, C_XBC, D + 512, W["ssm_conv_w"], W["ssm_conv_b"],
                                                           dxbc, 0, l2=False, name="ssm_conv_bwd_act", bc=512)
    dp = _conv_bwd_in(dy_s, W["ssm_conv_w"], dp, C_XBC, T, name="ssm_conv_bwd_in", bc=512)
    dp, G["ssm_dtb_row"], G["ssm_alog_x"] = _ssd_dt_bwd(pg, W["ssm_dtb_row"], W["ssm_alog_x"], ddtx, dalx, dp)
    ride = _grad_ride(shards, G, _GRADS_MID)
    g_in = _mm(h1, dp, dims="tn", out_dtype=BF16, bn_cap=1152, name="in_proj_dw", ride=ride)
    if ride:
        g_in, got = g_in
        G.update(zip(_GRADS_MID, got))
    G["w_in"] = _unpad_w_in(g_in)
    ride = _grad_ride(shards, G, ("w_in",))
    res = _mm(dp, W["w_in_pad"], dims="nt", epi="norm_bwd", extra=(x, dx1, W["norm1_w"]),
              name="in_proj_dx", ride=ride)
    if ride:
        res, got = res
        G["w_in"] = got[0]
    dx, gw = res
    G["norm1_w"] = gw.reshape(D)
    return loss, dx, G


def _all_gather(shards, out_dtype, *, name):
    n = len(shards)

    def body(*refs):
        x_refs, out_refs, stage = refs[:n], refs[n:2 * n], refs[2 * n:3 * n]
        send_sems, recv_sems, local_sems = refs[3 * n:]
        x, y, c = _place()
        me, sibling = (x, y, c), (x, y, 1 - c)
        chips = [(1 - x, y), (x, 1 - y), (1 - x, 1 - y)]

        def slot(px, py, pc):
            return 4 * px + 2 * py + pc

        def copy(a, k, block, to, src=None):
            dst = out_refs[a].at[slot(*block)]
            return pltpu.make_async_remote_copy(
                src_ref=dst if src is None else src, dst_ref=dst, send_sem=send_sems.at[a, k],
                recv_sem=recv_sems.at[a, k], device_id=to, device_id_type=_MESH)

        for a in range(n):
            stage[a][...] = x_refs[a][...].astype(out_dtype)
        mine = [pltpu.make_async_copy(stage[a], out_refs[a].at[slot(*me)], local_sems.at[a]) for a in range(n)]
        for cp in mine:
            cp.start()
        first = []
        for a in range(n):
            first.append(copy(a, 0, me, sibling, src=stage[a]))
            first += [copy(a, 1 + j, me, (*chip, c), src=stage[a]) for j, chip in enumerate(chips)]
        for cp in first:
            cp.start()
        passed = [[copy(a, 4 + j, (*chip, c), sibling) for j, chip in enumerate(chips)] for a in range(n)]
        for j, chip in enumerate(chips):
            for a in range(n):
                copy(a, 1 + j, (*chip, c), me).wait_recv()
                passed[a][j].start()
        for a in range(n):
            copy(a, 0, sibling, me).wait_recv()
            for j, chip in enumerate(chips):
                copy(a, 4 + j, (*chip, 1 - c), me).wait_recv()
        for cp in first + [cp for row in passed for cp in row]:
            cp.wait_send()
        for cp in mine:
            cp.wait()

    outs = pl.pallas_call(
        body, in_specs=[_VM] * n, out_specs=[_ANY] * n,
        out_shape=[jax.ShapeDtypeStruct((N_DEV,) + s.shape, out_dtype) for s in shards],
        scratch_shapes=[pltpu.VMEM(s.shape, out_dtype) for s in shards]
        + [pltpu.SemaphoreType.DMA((n, 7)), pltpu.SemaphoreType.DMA((n, 7)), pltpu.SemaphoreType.DMA((n,))],
        name=name, compiler_params=pltpu.CompilerParams(vmem_limit_bytes=VMEM_LIMIT))(*shards)
    return list(outs)


def _cast_bf16(arrs, *, name):
    n = len(arrs)

    def body(*refs):
        for a in range(n):
            refs[n + a][...] = refs[a][...].astype(BF16)

    return list(pl.pallas_call(
        body, in_specs=[_VM] * n, out_specs=[_VM] * n,
        out_shape=[jax.ShapeDtypeStruct(s.shape, BF16) for s in arrs], name=name,
        compiler_params=pltpu.CompilerParams(vmem_limit_bytes=VMEM_LIMIT))(*arrs))


def _sum8(a, *, name):
    _, R, Cc = a.shape
    br = _pick_rows(R, 128)

    def body(a_ref, o_ref):
        s = a_ref[0].astype(F32)
        for k in range(1, N_DEV):
            s = s + a_ref[k].astype(F32)
        o_ref[...] = s

    return pl.pallas_call(
        body, grid=(R // br,), in_specs=[pl.BlockSpec((N_DEV, br, Cc), lambda i: (0, i, 0))],
        out_specs=pl.BlockSpec((br, Cc), lambda i: (i, 0)), out_shape=jax.ShapeDtypeStruct((R, Cc), F32),
        name=name, compiler_params=_params(("parallel",)))(a)


def _pick_rows(R, cap):
    if R <= cap:
        return R
    for d in range(cap, 7, -8):
        if R % d == 0:
            return d
    return R


def _adamw(w, g, m, v, *, name):
    shape = w.shape
    as2d = (lambda t: t.reshape(1, -1)) if w.ndim == 1 else (lambda t: t)
    w2, g2, m2, v2 = as2d(w), as2d(g), as2d(m), as2d(v)
    R, Cc = w2.shape
    br = _pick_rows(R, 256)
    c1 = 1.0 - ADAM_B1 ** ADAM_STEP
    c2 = 1.0 - ADAM_B2 ** ADAM_STEP

    def body(w_ref, g_ref, m_ref, v_ref, d_ref, nm_ref, nv_ref):
        gv = g_ref[...]
        nm = ADAM_B1 * m_ref[...] + (1.0 - ADAM_B1) * gv
        nv = ADAM_B2 * v_ref[...] + (1.0 - ADAM_B2) * (gv * gv)
        nm_ref[...] = nm
        nv_ref[...] = nv
        d_ref[...] = -ADAM_LR * ((nm / c1) / (jnp.sqrt(nv / c2) + ADAM_EPS) + ADAM_WD * w_ref[...])

    blk = pl.BlockSpec((br, Cc), lambda i: (i, 0))
    outs = pl.pallas_call(
        body, grid=(R // br,), in_specs=[blk] * 4, out_specs=[blk] * 3,
        out_shape=[jax.ShapeDtypeStruct((R, Cc), F32)] * 3, name=name,
        compiler_params=_params(("parallel",)))(w2, g2, m2, v2)
    return tuple(o.reshape(shape) for o in outs)


_BIG = ("w_in", "w_out", "wq_mem", "wk_mem", "wv_mem", "wo_mem", "w_up", "w_down")
_COL_SHARDED = ("w_in", "w_up")
_WEIGHTS = ("norm1_w", "w_in", "gdn_conv_w", "gdn_a_log", "gdn_dt_bias", "gdn_norm_w", "ssm_conv_w", "ssm_conv_b",
            "ssm_a_log", "ssm_dt_bias", "ssm_d", "ssm_norm_w", "w_out", "norm2_w", "mem_norm_w", "wq_mem", "wk_mem",
            "wv_mem", "wo_mem", "norm3_w", "w_up", "w_down", "final_norm_w")
_IN_PAD = 112


def _full_from_slots(name, g):
    if name in _COL_SHARDED:
        return jnp.transpose(g, (1, 0, 2)).reshape(g.shape[1], N_DEV * g.shape[2])
    return g.reshape(N_DEV * g.shape[1], g.shape[2])


def _slots_from_full(name, f):
    if name in _COL_SHARDED:
        return jnp.transpose(f.reshape(f.shape[0], N_DEV, f.shape[1] // N_DEV), (1, 0, 2))
    return f.reshape(N_DEV, f.shape[0] // N_DEV, f.shape[1])


def _pad_w_in(w):
    z = jnp.zeros((w.shape[0], _IN_PAD), w.dtype)
    return jnp.concatenate([w[:, :4096], w[:, 4112:6672], w[:, 4096:4112], z, w[:, 6672:6688], z], axis=1)


def _unpad_w_in(gp):
    return jnp.concatenate([gp[:, :4096], gp[:, C_GATE:C_GATE + 16], gp[:, 4096:C_GATE], gp[:, C_DT:C_DT + 16]],
                           axis=1)


def _pack_rows(vals):
    rows, offs, r = [], [], 0
    for vflat in vals:
        nrow = 8 * -(-vflat.shape[0] // 1024)
        rows.append(jnp.pad(vflat, (0, nrow * 128 - vflat.shape[0])).reshape(nrow, 128))
        offs.append((r, vflat.shape[0]))
        r += nrow
    return jnp.concatenate(rows, axis=0), offs


def _unpack_rows(packed, offs, shapes):
    out = []
    for (r, nel), shp in zip(offs, shapes):
        nrow = -(-nel // 128)
        out.append(packed[r:r + nrow].reshape(-1)[:nel].reshape(shp))
    return out


def kernel(x, mem, norm1_w, w_in, gdn_conv_w, gdn_a_log, gdn_dt_bias, gdn_norm_w, ssm_conv_w, ssm_conv_b, ssm_a_log, ssm_dt_bias, ssm_d, ssm_norm_w, w_out, norm2_w, mem_norm_w, wq_mem, wk_mem, wv_mem, wo_mem, norm3_w, w_up, w_down, final_norm_w, loss_target, m_norm1_w, m_w_in, m_gdn_conv_w, m_gdn_a_log, m_gdn_dt_bias, m_gdn_norm_w, m_ssm_conv_w, m_ssm_conv_b, m_ssm_a_log, m_ssm_dt_bias, m_ssm_d, m_ssm_norm_w, m_w_out, m_norm2_w, m_mem_norm_w, m_wq_mem, m_wk_mem, m_wv_mem, m_wo_mem, m_norm3_w, m_w_up, m_w_down, m_final_norm_w, v_norm1_w, v_w_in, v_gdn_conv_w, v_gdn_a_log, v_gdn_dt_bias, v_gdn_norm_w, v_ssm_conv_w, v_ssm_conv_b, v_ssm_a_log, v_ssm_dt_bias, v_ssm_d, v_ssm_norm_w, v_w_out, v_norm2_w, v_mem_norm_w, v_wq_mem, v_wk_mem, v_wv_mem, v_wo_mem, v_norm3_w, v_w_up, v_w_down, v_final_norm_w):
    args = dict(locals())
    w_loc = {n: args[n] for n in _WEIGHTS}
    me = 4 * lax.axis_index("x") + 2 * lax.axis_index("y") + lax.axis_index("c")

    w_in_full = _full_from_slots("w_in", _all_gather([w_in], BF16, name="gather_w_in")[0])
    later = _EARLY + _LATE
    shards = dict(zip(later, _cast_bf16([w_loc[n] for n in later], name="cast_shards")))
    conv_pack, conv_offs = _pack_rows([gdn_conv_w.reshape(-1), ssm_conv_w.reshape(-1)])
    conv_all = _all_gather([conv_pack], F32, name="gather_conv")[0]
    gdn_cw, ssm_cw = [], []
    for k in range(N_DEV):
        a, b = _unpack_rows(conv_all[k], conv_offs, [gdn_conv_w.shape, ssm_conv_w.shape])
        gdn_cw.append(a)
        ssm_cw.append(b)
    W = {
        "w_in_pad": _pad_w_in(w_in_full),
        "norm1_w": norm1_w, "norm2_w": norm2_w, "norm3_w": norm3_w, "mem_norm_w": mem_norm_w,
        "final_norm_w": final_norm_w, "ssm_norm_w": ssm_norm_w, "ssm_conv_b": ssm_conv_b,
        "gdn_conv_w": jnp.concatenate(gdn_cw, axis=1), "ssm_conv_w": jnp.concatenate(ssm_cw, axis=1),
        "gdn_alog_row": jnp.pad(gdn_a_log, (GDN_H, 128 - 2 * GDN_H)).reshape(1, 128),
        "gdn_dtb_row": jnp.pad(gdn_dt_bias, (GDN_H, 128 - 2 * GDN_H)).reshape(1, 128),
        "gdn_norm_x": jnp.tile(gdn_norm_w, GDN_H).reshape(1, D),
        "ssm_dtb_row": jnp.pad(ssm_dt_bias, (0, 128 - SSM_H)).reshape(1, 128),
        "ssm_alog_x": jnp.repeat(ssm_a_log, SSM_P).reshape(1, D),
        "ssm_d_x": jnp.repeat(ssm_d, SSM_P).reshape(1, D),
    }

    loss_part, grad_x, G = _local_step(x[0], mem[0], loss_target[0], W, shards)

    grads = {n: _sum8(G[n], name="sum_" + n) for n in _BIG}

    small = {
        "norm1_w": G["norm1_w"], "gdn_conv_w": G["gdn_conv_w"], "gdn_a_log": G["gdn_alog_row"][0, GDN_H:2 * GDN_H],
        "gdn_dt_bias": G["gdn_dtb_row"][0, GDN_H:2 * GDN_H], "gdn_norm_w": G["gdn_norm_x"].reshape(GDN_H, 128).sum(0),
        "ssm_conv_w": G["ssm_conv_w"], "ssm_conv_b": G["ssm_conv_b"],
        "ssm_a_log": G["ssm_alog_x"].reshape(SSM_H, SSM_P).sum(1), "ssm_dt_bias": G["ssm_dtb_row"][0, :SSM_H],
        "ssm_d": G["ssm_d_x"].reshape(SSM_H, SSM_P).sum(1), "ssm_norm_w": G["ssm_norm_w"].reshape(D),
        "norm2_w": G["norm2_w"], "mem_norm_w": G["mem_norm_w"], "norm3_w": G["norm3_w"],
        "final_norm_w": G["final_norm_w"], "loss": loss_part[0, :1],
    }
    names = list(small)
    pack, offs = _pack_rows([small[n].reshape(-1) for n in names])
    tot = _sum8(_all_gather([pack], F32, name="gather_small")[0], name="sum_small")
    summed = dict(zip(names, _unpack_rows(tot, offs, [small[n].shape for n in names])))
    loss = summed.pop("loss")[0]
    for n in ("gdn_conv_w", "ssm_conv_w"):
        width = w_loc[n].shape[1]
        summed[n] = lax.dynamic_slice_in_dim(summed[n], me * width, width, axis=1)
    grads.update(summed)

    upd = {n: _adamw(w_loc[n], grads[n], args["m_" + n], args["v_" + n], name="adamw_" + n) for n in _WEIGHTS}
    return (loss, grad_x[None], *[grads[n] for n in _WEIGHTS], *[upd[n][0] for n in _WEIGHTS],
            *[upd[n][1] for n in _WEIGHTS], *[upd[n][2] for n in _WEIGHTS])
```

```python
import functools
import math

import jax
import jax.numpy as jnp
from jax import lax
from jax.experimental import pallas as pl
from jax.experimental.pallas import tpu as pltpu

F32 = jnp.float32
BF16 = jnp.bfloat16
_MXU = BF16

D = 1024
EPS = 1e-6
CONV_K = 4
GDN_H, GDN_DK, GDN_C = 8, 128, 64
GDN_SCAN_CHUNKS = 4
GDN_LOCAL_CHUNKS = 4
GDN_REST_CHUNKS = 4
SSM_H, SSM_P, SSM_L, SSM_N = 16, 64, 128, 128
SSM_SCAN_CHUNKS = 2
MEM_H, MEM_HD = 4, 256
D_FF = 4096
N_DEV = 8

C_QKV, C_ZG, C_ZS, C_XBC, C_GATE, C_DT, C_TOT = 0, 3072, 4096, 5120, 6656, 6784, 6912
P_HALO = 16

ADAM_LR, ADAM_B1, ADAM_B2, ADAM_EPS, ADAM_WD, ADAM_STEP = 0.001, 0.9, 0.999, 1e-08, 0.01, 10

VMEM_LIMIT = 56 * 1024 * 1024

_NN = (((1,), (0,)), ((), ()))
_NT = (((1,), (1,)), ((), ()))
_TN = (((0,), (0,)), ((), ()))


def _dot(a, b, dims=_NN):
    return lax.dot_general(a.astype(_MXU), b.astype(_MXU), dims, preferred_element_type=F32)


def _split3(a):
    a1 = a.astype(BF16)
    r1 = a - a1.astype(F32)
    a2 = r1.astype(BF16)
    return a1, a2, (r1 - a2.astype(F32)).astype(BF16)


def _dot_sel(a, e):
    eb = e.astype(BF16)
    return sum(lax.dot_general(p, eb, _NN, preferred_element_type=F32) for p in _split3(a))


def _sel_dot(e, a):
    eb = e.astype(BF16)
    return sum(lax.dot_general(eb, p, _NN, preferred_element_type=F32) for p in _split3(a))


def _chunk_cumsum(a, tri, chunk):
    return jnp.concatenate([_sel_dot(tri, a[r:r + chunk]) for r in range(0, a.shape[0], chunk)], axis=0)


def _params(sem):
    return pltpu.CompilerParams(dimension_semantics=sem, vmem_limit_bytes=VMEM_LIMIT)


def _pick(n, cap):
    for d in range(min(cap, n), 0, -128):
        if n % d == 0 and d % 128 == 0:
            return d
    return n


def _sigmoid(x):
    return 0.5 * jnp.tanh(0.5 * x) + 0.5


def _silu(x):
    return x * _sigmoid(x)


def _dsilu(x):
    s = _sigmoid(x)
    return s * (1.0 + x * (1.0 - s))


def _softplus(x):
    return jnp.maximum(x, 0.0) + jnp.log(1.0 + jnp.exp(-jnp.abs(x)))


def _iota2(shape, axis):
    return lax.broadcasted_iota(jnp.int32, shape, axis)


def _sum_all(x):
    return jnp.sum(jnp.sum(x, axis=1, keepdims=True), axis=0, keepdims=True)


_MESH = pl.DeviceIdType.MESH
_ANY = pl.BlockSpec(memory_space=pl.ANY)
_VM = pl.BlockSpec(memory_space=pltpu.VMEM)
_REL = [(r >> 2 & 1, r >> 1 & 1, r & 1) for r in range(1, N_DEV)]


def _place():
    return lax.axis_index("x"), lax.axis_index("y"), lax.axis_index("c")


class _Ride:
    def __init__(self, srcs, shard):
        self.srcs, self.shard, self.n = list(srcs), shard, len(srcs)
        self.out_shape = [jax.ShapeDtypeStruct(((N_DEV,) + s.shape) if shard else s.shape, s.dtype)
                          for s in self.srcs]
        self.specs = [_ANY] * self.n
        self.scratch = [pltpu.SemaphoreType.DMA((self.n, N_DEV - 1)), pltpu.SemaphoreType.DMA((self.n, N_DEV - 1)),
                        pltpu.SemaphoreType.DMA((self.n,))]

    def _copies(self, in_refs, out_refs, sems):
        send, recv, loc = sems
        x, y, c = _place()
        me = 4 * x + 2 * y + c
        local, remote, arrive = [], [], []
        for a in range(self.n):
            src = in_refs[a] if self.shard else in_refs[a].at[me]
            local.append(pltpu.make_async_copy(src, out_refs[a].at[me], loc.at[a]))
        for k, (rx, ry, rc) in enumerate(_REL):
            peer = (lax.rem(x + rx, 2), lax.rem(y + ry, 2), lax.rem(c + rc, 2))
            ps = 4 * peer[0] + 2 * peer[1] + peer[2]
            for a in range(self.n):
                src = in_refs[a] if self.shard else in_refs[a].at[ps]
                remote.append(pltpu.make_async_remote_copy(
                    src_ref=src, dst_ref=out_refs[a].at[me], send_sem=send.at[a, k], recv_sem=recv.at[a, k],
                    device_id=peer, device_id_type=_MESH))
                slot = out_refs[a].at[ps]
                arrive.append(pltpu.make_async_remote_copy(
                    src_ref=slot, dst_ref=slot, send_sem=send.at[a, k], recv_sem=recv.at[a, k],
                    device_id=peer, device_id_type=_MESH))
        return local, remote, arrive

    def start(self, in_refs, out_refs, sems):
        local, remote, _ = self._copies(in_refs, out_refs, sems)
        for cp in local + remote:
            cp.start()

    def wait(self, in_refs, out_refs, sems):
        local, remote, arrive = self._copies(in_refs, out_refs, sems)
        for cp in arrive:
            cp.wait_recv()
        for cp in remote:
            cp.wait_send()
        for cp in local:
            cp.wait()


_EPI = {
    "none": ((), ("tile",)),
    "res": (("tile",), ("tile",)),
    "mul2": (("tile",), ("tile",)),
    "relu2": ((), ("tile", "tile")),
    "res_norm": (("tile", "row"), ("tile", "tile")),
    "norm_bwd": (("tile", "tile", "row"), ("tile", "row")),
    "res_loss": (("tile", "tile", "row"), ("tile", "row", "row")),
}


def _mm(a, b, *, dims="nn", epi="none", extra=(), out_dtype=F32, name, bm=1024, bn_cap=1024, bk_cap=2048,
        ride=None, b_cols=None):
    if dims == "nn":
        (M, K), (K2, N) = a.shape, b.shape
    elif dims == "nt":
        (M, K), (N, K2) = a.shape, b.shape
    else:
        (K, M), (K2, N) = a.shape, b.shape
    jb0 = 0
    if b_cols is not None:
        N = b_cols[1]
    assert K == K2, (a.shape, b.shape, dims)
    bm = _pick(M, bm)
    bn = _pick(N, bn_cap)
    bk = _pick(K, bk_cap)
    nk = K // bk
    if b_cols is not None:
        assert dims == "nn" and b_cols[0] % bn == 0
        jb0 = b_cols[0] // bn
    dn = {"nn": _NN, "nt": _NT, "tn": _TN}[dims]
    a_spec = (pl.BlockSpec((bk, bm), lambda i, j, k: (k, i)) if dims == "tn"
              else pl.BlockSpec((bm, bk), lambda i, j, k: (i, k)))
    b_spec = (pl.BlockSpec((bn, bk), lambda i, j, k: (j, k)) if dims == "nt"
              else pl.BlockSpec((bk, bn), lambda i, j, k: (k, j + jb0)))
    o_spec = pl.BlockSpec((bm, bn), lambda i, j, k: (i, j))
    r_spec = pl.BlockSpec((1, bn), lambda i, j, k: (0, j))
    extra = list(extra) if isinstance(extra, (tuple, list)) else [extra]
    ekinds, okinds = _EPI[epi]
    assert len(extra) == len(ekinds) and (epi not in ("res_norm", "norm_bwd", "res_loss") or bn == N)
    n_extra, n_out = len(ekinds), len(okinds)
    n_ride = ride.n if ride else 0
    gi, gj = M // bm, N // bn

    def body(a_ref, b_ref, *rest):
        ex = rest[:n_extra]
        first = pl.program_id(0) == 0
        ride_in = rest[n_extra:n_extra + n_ride]
        outs = rest[n_extra + n_ride:n_extra + n_ride + n_out]
        ride_out = rest[n_extra + n_ride + n_out:n_extra + 2 * n_ride + n_out]
        if ride:
            at = lambda i, j, k: ((pl.program_id(0) == i) & (pl.program_id(1) == j) & (pl.program_id(2) == k))

            @pl.when(at(0, 0, 0))
            def _():
                ride.start(ride_in, ride_out, rest[-3:])

        def finish(r):
            if epi == "res":
                outs[0][...] = (r + ex[0][...].astype(F32)).astype(outs[0].dtype)
            elif epi == "mul2":
                outs[0][...] = (2.0 * r * ex[0][...].astype(F32)).astype(outs[0].dtype)
            elif epi == "relu2":
                u = jnp.maximum(r, 0.0)
                outs[0][...] = u.astype(outs[0].dtype)
                outs[1][...] = (u * u).astype(outs[1].dtype)
            elif epi == "res_norm":
                y = r + ex[0][...]
                outs[0][...] = y
                rstd = lax.rsqrt(jnp.mean(y * y, axis=1, keepdims=True) + EPS)
                outs[1][...] = (y * rstd * ex[1][...]).astype(outs[1].dtype)
            elif epi == "norm_bwd":
                xv = ex[0][...]
                rstd = lax.rsqrt(jnp.mean(xv * xv, axis=1, keepdims=True) + EPS)
                xh = xv * rstd
                dxh = r * ex[2][...]
                outs[0][...] = ex[1][...] + rstd * (dxh - xh * jnp.mean(dxh * xh, axis=1, keepdims=True))
                dw = jnp.sum(r * xh, axis=0, keepdims=True)

                @pl.when(first)
                def _():
                    outs[1][...] = dw

                @pl.when(jnp.logical_not(first))
                def _():
                    outs[1][...] += dw
            elif epi == "res_loss":
                y = r + ex[0][...]
                wv = ex[2][...]
                rstd = lax.rsqrt(jnp.mean(y * y, axis=1, keepdims=True) + EPS)
                yh = y * rstd
                err = yh * wv - ex[1][...]
                part_loss = 0.5 * jnp.sum(jnp.mean(err * err, axis=1, keepdims=True), axis=0, keepdims=True)
                dyn = err * (1.0 / N)
                dyh = dyn * wv
                outs[0][...] = rstd * (dyh - yh * jnp.mean(dyh * yh, axis=1, keepdims=True))
                dw = jnp.sum(dyn * yh, axis=0, keepdims=True)
                lrow = jnp.broadcast_to(part_loss, (1, N))

                @pl.when(first)
                def _():
                    outs[1][...] = dw
                    outs[2][...] = lrow

                @pl.when(jnp.logical_not(first))
                def _():
                    outs[1][...] += dw
                    outs[2][...] += lrow
            else:
                outs[0][...] = r.astype(outs[0].dtype)

        part = _dot(a_ref[...], b_ref[...], dn)
        if nk == 1:
            finish(part)
        else:
            acc = rest[n_extra + 2 * n_ride + n_out]
            k = pl.program_id(2)

            @pl.when(k == 0)
            def _():
                acc[...] = part

            @pl.when((k > 0) & (k < nk - 1))
            def _():
                acc[...] += part

            @pl.when(k == nk - 1)
            def _():
                finish(acc[...] + part)

        if ride:
            @pl.when(at(gi - 1, gj - 1, nk - 1))
            def _():
                ride.wait(ride_in, ride_out, rest[-3:])

    kind_spec = {"tile": o_spec, "row": r_spec}
    ins = [a, b] + [e.reshape(1, N) if k == "row" else e for e, k in zip(extra, ekinds)]
    in_specs = [a_spec, b_spec] + [kind_spec[k] for k in ekinds]
    out_dtypes = {"res_norm": (F32, BF16), "norm_bwd": (F32, F32), "res_loss": (F32, F32, F32)}.get(
        epi, (out_dtype,) * n_out)
    out_shape = [jax.ShapeDtypeStruct((M, N) if k == "tile" else (1, N), dt) for k, dt in zip(okinds, out_dtypes)]
    out_specs = [kind_spec[k] for k in okinds]
    scratch = [pltpu.VMEM((bm, bn), F32)] if nk > 1 else []
    sem = ("arbitrary" if epi in ("norm_bwd", "res_loss") else "parallel", "parallel", "arbitrary")
    if ride:
        ins, in_specs = ins + ride.srcs, in_specs + ride.specs
        out_shape, out_specs = out_shape + ride.out_shape, out_specs + ride.specs
        scratch, sem = scratch + ride.scratch, ("arbitrary",) * 3
    res = pl.pallas_call(
        body, grid=(gi, gj, nk), in_specs=in_specs, out_specs=out_specs, out_shape=out_shape,
        scratch_shapes=scratch, name=name, compiler_params=_params(sem))(*ins)
    main = res[:n_out] if n_out > 1 else res[0]
    return (main, list(res[n_out:])) if ride else main


def _rmsnorm_fwd(x, w, *, name, bt=256):
    T, Dm = x.shape
    bt = min(bt, T)

    def body(x_ref, w_ref, h_ref):
        xv = x_ref[...]
        r = lax.rsqrt(jnp.mean(xv * xv, axis=1, keepdims=True) + EPS)
        h_ref[...] = (xv * r * w_ref[...]).astype(h_ref.dtype)

    return pl.pallas_call(
        body, grid=(T // bt,),
        in_specs=[pl.BlockSpec((bt, Dm), lambda i: (i, 0)), pl.BlockSpec((1, Dm), lambda i: (0, 0))],
        out_specs=pl.BlockSpec((bt, Dm), lambda i: (i, 0)),
        out_shape=jax.ShapeDtypeStruct((T, Dm), BF16), name=name,
        compiler_params=_params(("parallel",)))(x, w.reshape(1, Dm))


def _rmsnorm_bwd(x, w, dh, dres, *, name, bt=256):
    T, Dm = x.shape
    bt = min(bt, T)
    has_res = dres is not None

    def body(x_ref, w_ref, dh_ref, *rest):
        dres_ref = rest[0] if has_res else None
        dx_ref, dw_ref = rest[-2], rest[-1]
        i = pl.program_id(0)
        xv = x_ref[...]
        r = lax.rsqrt(jnp.mean(xv * xv, axis=1, keepdims=True) + EPS)
        xh = xv * r
        dhv = dh_ref[...].astype(F32)
        dxh = dhv * w_ref[...]
        dx = r * (dxh - xh * jnp.mean(dxh * xh, axis=1, keepdims=True))
        if has_res:
            dx = dx + dres_ref[...]
        dx_ref[...] = dx

        @pl.when(i == 0)
        def _():
            dw_ref[...] = jnp.zeros_like(dw_ref)

        dw_ref[...] += jnp.sum(dhv * xh, axis=0, keepdims=True)

    row = pl.BlockSpec((bt, Dm), lambda i: (i, 0))
    vec = pl.BlockSpec((1, Dm), lambda i: (0, 0))
    ins = [x, w.reshape(1, Dm), dh] + ([dres] if has_res else [])
    dx, dw = pl.pallas_call(
        body, grid=(T // bt,), in_specs=[row, vec, row] + ([row] if has_res else []),
        out_specs=[row, vec],
        out_shape=[jax.ShapeDtypeStruct((T, Dm), F32), jax.ShapeDtypeStruct((1, Dm), F32)],
        name=name, compiler_params=_params(("arbitrary",)))(*ins)
    return dx, dw.reshape(Dm)


def _attn_fwd(q, km, vm, *, bt=256):
    T = q.shape[0]
    M = km.shape[0]
    bt = min(bt, T)
    scale = MEM_HD ** -0.5

    def body(q_ref, k_ref, v_ref, o_ref):
        sls = [slice(h * MEM_HD, (h + 1) * MEM_HD) for h in range(MEM_H)]
        ss = [_dot(q_ref[:, sl], k_ref[:, sl], _NT) * scale for sl in sls]
        es = [jnp.exp(s - jnp.max(s, axis=1, keepdims=True)) for s in ss]
        ps = [e / jnp.sum(e, axis=1, keepdims=True) for e in es]
        for sl, p in zip(sls, ps):
            o_ref[:, sl] = _dot(p, v_ref[:, sl]).astype(o_ref.dtype)

    row = pl.BlockSpec((bt, D), lambda i: (i, 0))
    mem = pl.BlockSpec((M, D), lambda i: (0, 0))
    return pl.pallas_call(
        body, grid=(T // bt,), in_specs=[row, mem, mem], out_specs=row,
        out_shape=jax.ShapeDtypeStruct((T, D), BF16), name="attn_fwd",
        compiler_params=_params(("parallel",)))(q, km, vm)


def _attn_bwd(q, km, vm, do, *, bt=256):
    T = q.shape[0]
    M = km.shape[0]
    bt = min(bt, T)
    scale = MEM_HD ** -0.5

    def body(q_ref, k_ref, v_ref, do_ref, dq_ref, dk_ref, dv_ref):
        i = pl.program_id(0)

        @pl.when(i == 0)
        def _():
            dk_ref[...] = jnp.zeros_like(dk_ref)
            dv_ref[...] = jnp.zeros_like(dv_ref)

        sls = [slice(h * MEM_HD, (h + 1) * MEM_HD) for h in range(MEM_H)]
        ss = [_dot(q_ref[:, sl], k_ref[:, sl], _NT) * scale for sl in sls]
        dps = [_dot(do_ref[:, sl], v_ref[:, sl], _NT) for sl in sls]
        es = [jnp.exp(s - jnp.max(s, axis=1, keepdims=True)) for s in ss]
        ps = [e / jnp.sum(e, axis=1, keepdims=True) for e in es]
        dss = [p * (dp - jnp.sum(dp * p, axis=1, keepdims=True)) * scale for p, dp in zip(ps, dps)]
        for sl, p, ds in zip(sls, ps, dss):
            dq_ref[:, sl] = _dot(ds, k_ref[:, sl]).astype(dq_ref.dtype)
            dk_ref[:, sl] += _dot(ds, q_ref[:, sl], _TN)
            dv_ref[:, sl] += _dot(p, do_ref[:, sl], _TN)

    row = pl.BlockSpec((bt, D), lambda i: (i, 0))
    mem = pl.BlockSpec((M, D), lambda i: (0, 0))
    return pl.pallas_call(
        body, grid=(T // bt,), in_specs=[row, mem, mem, row], out_specs=[row, mem, mem],
        out_shape=[jax.ShapeDtypeStruct((T, D), BF16), jax.ShapeDtypeStruct((M, D), F32),
                   jax.ShapeDtypeStruct((M, D), F32)],
        name="attn_bwd", compiler_params=_params(("arbitrary",)))(q, km, vm, do)


def _conv_apply(halo, x, w_ref, b_ref):
    bt, hr = x.shape[0], halo.shape[0]
    cat = jnp.concatenate([halo, x], axis=0)
    y = x * w_ref[3:4, :]
    for k in range(CONV_K - 1):
        y = y + pltpu.roll(cat, CONV_K - 1 - k, 0)[hr:hr + bt] * w_ref[k:k + 1, :]
    if b_ref is not None:
        y = y + b_ref[...]
    return y


def _l2_parts(act, bc):
    out = []
    for s in range(bc // 128):
        a = act[:, s * 128:(s + 1) * 128]
        r = lax.rsqrt(jnp.sum(a * a, axis=1, keepdims=True) + EPS)
        out.append((a, r))
    return out


def _conv_fwd(p, col0, C, w, b, *, l2, name, bt=512, bc=1024):
    T = p.shape[0]
    bt = min(bt, T)
    c0, hb = col0 // bc, bt // P_HALO
    has_b = b is not None
    assert not l2 or (bc == D and C == 2 * D)

    def body(x_ref, halo_ref, w_ref, *rest):
        b_ref = rest[0] if has_b else None
        o_ref = rest[-1]
        i, j = pl.program_id(0), pl.program_id(1)
        x = x_ref[...].astype(F32)
        halo = jnp.where(i > 0, halo_ref[...].astype(F32), 0.0)
        act = _silu(_conv_apply(halo, x, w_ref, b_ref))
        if l2:
            sc = jnp.where(j == 0, GDN_DK ** -0.5, 1.0)
            o_ref[...] = jnp.concatenate([a * (r * sc) for a, r in _l2_parts(act, bc)], axis=1)
        else:
            o_ref[...] = act

    in_specs = [pl.BlockSpec((bt, bc), lambda i, j: (i, c0 + j)),
                pl.BlockSpec((P_HALO, bc), lambda i, j: (jnp.maximum(i * hb - 1, 0), c0 + j)),
                pl.BlockSpec((CONV_K, bc), lambda i, j: (0, j))]
    ins = [p, p, w]
    if has_b:
        in_specs.append(pl.BlockSpec((1, bc), lambda i, j: (0, j)))
        ins.append(b.reshape(1, C))
    return pl.pallas_call(
        body, grid=(T // bt, C // bc), in_specs=in_specs,
        out_specs=pl.BlockSpec((bt, bc), lambda i, j: (i, j)),
        out_shape=jax.ShapeDtypeStruct((T, C), F32), name=name,
        compiler_params=_params(("parallel", "parallel")))(*ins)


def _conv_bwd_act(p, col0, C, w, b, dact, dcol0, *, l2, name, bt=512, bc=1024):
    T = p.shape[0]
    bt = min(bt, T)
    c0, d0, hb = col0 // bc, dcol0 // bc, bt // P_HALO
    has_b = b is not None
    assert not l2 or (bc == D and C == 2 * D)

    def body(x_ref, halo_ref, w_ref, *rest):
        b_ref = rest[0] if has_b else None
        dact_ref, dy_ref, dw_ref, db_ref = rest[-4:]
        j, i = pl.program_id(0), pl.program_id(1)
        x = x_ref[...].astype(F32)
        halo = jnp.where(i > 0, halo_ref[...].astype(F32), 0.0)
        y = _conv_apply(halo, x, w_ref, b_ref)
        dact = dact_ref[...]
        sg = _sigmoid(y)
        if l2:
            sc = jnp.where(j == 0, GDN_DK ** -0.5, 1.0)
            parts = []
            for s, (a, r) in enumerate(_l2_parts(y * sg, bc)):
                n = a * r
                dn = dact[:, s * 128:(s + 1) * 128]
                parts.append((r * sc) * (dn - n * jnp.sum(dn * n, axis=1, keepdims=True)))
            dact = jnp.concatenate(parts, axis=1)
        dy = dact * (sg * (1.0 + y * (1.0 - sg)))
        dy_ref[...] = dy

        @pl.when(i == 0)
        def _():
            dw_ref[...] = jnp.zeros_like(dw_ref)
            db_ref[...] = jnp.zeros_like(db_ref)

        db_ref[...] += jnp.sum(dy, axis=0, keepdims=True)
        cat = jnp.concatenate([halo, x], axis=0)
        dw_ref[3:4, :] += jnp.sum(dy * x, axis=0, keepdims=True)
        for k in range(CONV_K - 1):
            xs = pltpu.roll(cat, CONV_K - 1 - k, 0)[P_HALO:P_HALO + bt]
            dw_ref[k:k + 1, :] += jnp.sum(dy * xs, axis=0, keepdims=True)

    in_specs = [pl.BlockSpec((bt, bc), lambda j, i: (i, c0 + j)),
                pl.BlockSpec((P_HALO, bc), lambda j, i: (jnp.maximum(i * hb - 1, 0), c0 + j)),
                pl.BlockSpec((CONV_K, bc), lambda j, i: (0, j))]
    ins = [p, p, w]
    if has_b:
        in_specs.append(pl.BlockSpec((1, bc), lambda j, i: (0, j)))
        ins.append(b.reshape(1, C))
    in_specs.append(pl.BlockSpec((bt, bc), lambda j, i: (i, d0 + j)))
    ins.append(dact)
    dy, dw, db = pl.pallas_call(
        body, grid=(C // bc, T // bt), in_specs=in_specs,
        out_specs=[pl.BlockSpec((bt, bc), lambda j, i: (i, j)),
                   pl.BlockSpec((CONV_K, bc), lambda j, i: (0, j)),
                   pl.BlockSpec((1, bc), lambda j, i: (0, j))],
        out_shape=[jax.ShapeDtypeStruct((T, C), F32), jax.ShapeDtypeStruct((CONV_K, C), F32),
                   jax.ShapeDtypeStruct((1, C), F32)],
        name=name, compiler_params=_params(("parallel", "arbitrary")))(*ins)
    return dy, dw, db.reshape(C)


def _conv_bwd_in(dy, w, dp_in, col0, T, *, name, bt=512, bc=1024):
    C = dy.shape[1]
    bt = min(bt, T)
    c0, hb, nb = col0 // bc, bt // 8, T // bt

    def body(dy_ref, nxt_ref, w_ref, *rest):
        o_ref = rest[-1]
        i = pl.program_id(0)
        dy_v = dy_ref[...]
        nxt = jnp.where(i < nb - 1, nxt_ref[...], 0.0)
        cat = jnp.concatenate([dy_v, nxt], axis=0)
        dx = dy_v * w_ref[3:4, :]
        for k in range(CONV_K - 1):
            s = CONV_K - 1 - k
            dx = dx + pltpu.roll(cat, bt + 8 - s, 0)[0:bt] * w_ref[k:k + 1, :]
        o_ref[...] = dx.astype(o_ref.dtype)

    in_specs = [pl.BlockSpec((bt, bc), lambda i, j: (i, j)),
                pl.BlockSpec((8, bc), lambda i, j: (jnp.minimum((i + 1) * hb, T // 8 - 1), j)),
                pl.BlockSpec((CONV_K, bc), lambda i, j: (0, j))]
    ins = [dy, dy, w]
    alias = {}
    if dp_in is not None:
        in_specs.append(pl.BlockSpec(memory_space=pl.ANY))
        ins.append(dp_in)
        alias = {3: 0}
    return pl.pallas_call(
        body, grid=(nb, C // bc), in_specs=in_specs,
        out_specs=pl.BlockSpec((bt, bc), lambda i, j: (i, c0 + j)),
        out_shape=jax.ShapeDtypeStruct((T, C_TOT), BF16), input_output_aliases=alias, name=name,
        compiler_params=_params(("parallel", "parallel")))(*ins)


def _expand_mats(shift, row0):
    e = (_iota2((128, D), 0) - row0 == (_iota2((128, D), 1) >> shift)).astype(F32)
    et = ((_iota2((D, 128), 0) >> shift) == _iota2((D, 128), 1) - row0).astype(F32)
    return e, et


def _cum_mats(chunk):
    ri, ci = _iota2((chunk, chunk), 0), _iota2((chunk, chunk), 1)
    return (ri >= ci).astype(F32), (ri <= ci).astype(F32)


def _gdn_gates_fwd(p, alog_row, dtb_row, *, bt=256):
    T = p.shape[0]
    bt = min(bt, T)

    def body(g_ref, al_ref, db_ref, bg_ref):
        gt = g_ref[...]
        lc, _ = _cum_mats(GDN_C)
        g_l = -jnp.exp(al_ref[...]) * _softplus(gt + db_ref[...])
        bg_ref[...] = jnp.where(_iota2((bt, 128), 1) < GDN_H, _sigmoid(gt), _chunk_cumsum(g_l, lc, GDN_C))

    vec = pl.BlockSpec((1, 128), lambda i: (0, 0))
    seg = pl.BlockSpec((bt, 128), lambda i: (i, 0))
    return pl.pallas_call(
        body, grid=(T // bt,), in_specs=[seg, vec, vec], out_specs=seg,
        out_shape=jax.ShapeDtypeStruct((T, 128), F32), name="gdn_gates_fwd",
        compiler_params=_params(("parallel",)))(p, alog_row, dtb_row)


def _gdn_gates_bwd(p, alog_row, dtb_row, dbg, dp_in, *, bt=256):
    T = p.shape[0]
    bt = min(bt, T)

    def body(g_ref, al_ref, db_ref, dbg_ref, dpin_ref, dg_out, dal_ref, ddb_ref):
        i = pl.program_id(0)
        gt = g_ref[...]
        lane = _iota2((bt, 128), 1)
        _, uc = _cum_mats(GDN_C)
        ea = jnp.exp(al_ref[...])
        zz = gt + db_ref[...]
        g_l = -ea * _softplus(zz)
        beta_l = _sigmoid(gt)
        dbg_v = dbg_ref[...]
        dg_l = jnp.where((lane >= GDN_H) & (lane < 2 * GDN_H), _chunk_cumsum(dbg_v, uc, GDN_C), 0.0)
        dbeta_l = jnp.where(lane < GDN_H, dbg_v, 0.0)
        da = dg_l * (-ea) * _sigmoid(zz)
        dg_out[...] = (da + dbeta_l * beta_l * (1.0 - beta_l)).astype(dg_out.dtype)

        @pl.when(i == 0)
        def _():
            dal_ref[...] = jnp.zeros_like(dal_ref)
            ddb_ref[...] = jnp.zeros_like(ddb_ref)

        dal_ref[...] += jnp.sum(dg_l * g_l, axis=0, keepdims=True)
        ddb_ref[...] += jnp.sum(da, axis=0, keepdims=True)

    vec = pl.BlockSpec((1, 128), lambda i: (0, 0))
    seg = pl.BlockSpec((bt, 128), lambda i: (i, 0))
    gate = pl.BlockSpec((bt, 128), lambda i: (i, C_GATE // 128))
    return pl.pallas_call(
        body, grid=(T // bt,), in_specs=[seg, vec, vec, seg, _ANY], out_specs=[gate, vec, vec],
        out_shape=[jax.ShapeDtypeStruct((T, C_TOT), BF16), jax.ShapeDtypeStruct((1, 128), F32),
                   jax.ShapeDtypeStruct((1, 128), F32)],
        input_output_aliases={4: 0}, name="gdn_gates_bwd",
        compiler_params=_params(("arbitrary",)))(p, alog_row, dtb_row, dbg, dp_in)


def _ssd_dt_fwd(p, dtb_row, alog_x, *, bt=256):
    T = p.shape[0]
    bt = min(bt, T)

    def body(d_ref, db_ref, al_ref, dt_ref, alpha_ref):
        ed, _ = _expand_mats(6, 0)
        lc, _ = _cum_mats(SSM_L)
        dt_x = _dot_sel(_softplus(d_ref[...] + db_ref[...]), ed)
        dt_ref[...] = dt_x
        alpha_ref[...] = _chunk_cumsum(dt_x * (-jnp.exp(al_ref[...])), lc, SSM_L)

    row = pl.BlockSpec((bt, D), lambda i: (i, 0))
    return pl.pallas_call(
        body, grid=(T // bt,),
        in_specs=[pl.BlockSpec((bt, 128), lambda i: (i, 1)),
                  pl.BlockSpec((1, 128), lambda i: (0, 0)), pl.BlockSpec((1, D), lambda i: (0, 0))],
        out_specs=[row, row], out_shape=[jax.ShapeDtypeStruct((T, D), F32)] * 2,
        name="ssd_dt_fwd", compiler_params=_params(("parallel",)))(p, dtb_row, alog_x)


def _ssd_dt_bwd(p, dtb_row, alog_x, ddt_x, dalpha_x, dp_in, *, bt=256):
    T = p.shape[0]
    bt = min(bt, T)

    def body(d_ref, db_ref, al_ref, ddt_ref, dal_ref, dpin_ref, dd_out, ddb_ref, dalog_ref):
        i = pl.program_id(0)
        ed, edt = _expand_mats(6, 0)
        _, uc = _cum_mats(SSM_L)
        zz = d_ref[...] + db_ref[...]
        dt_x = _dot_sel(_softplus(zz), ed)
        a_x = -jnp.exp(al_ref[...])
        da_x = _chunk_cumsum(dal_ref[...], uc, SSM_L)
        ddt_l = _dot_sel(ddt_ref[...] + da_x * a_x, edt)
        draw = ddt_l * _sigmoid(zz)
        dd_out[...] = draw.astype(dd_out.dtype)

        @pl.when(i == 0)
        def _():
            ddb_ref[...] = jnp.zeros_like(ddb_ref)
            dalog_ref[...] = jnp.zeros_like(dalog_ref)

        ddb_ref[...] += jnp.sum(draw, axis=0, keepdims=True)
        dalog_ref[...] += jnp.sum(da_x * dt_x, axis=0, keepdims=True) * a_x

    row = pl.BlockSpec((bt, D), lambda i: (i, 0))
    seg = pl.BlockSpec((bt, 128), lambda i: (i, C_DT // 128))
    v128 = pl.BlockSpec((1, 128), lambda i: (0, 0))
    vD = pl.BlockSpec((1, D), lambda i: (0, 0))
    return pl.pallas_call(
        body, grid=(T // bt,),
        in_specs=[pl.BlockSpec((bt, 128), lambda i: (i, 1)), v128, vD, row, row, _ANY],
        out_specs=[seg, v128, vD],
        out_shape=[jax.ShapeDtypeStruct((T, C_TOT), BF16), jax.ShapeDtypeStruct((1, 128), F32),
                   jax.ShapeDtypeStruct((1, D), F32)],
        input_output_aliases={5: 0}, name="ssd_dt_bwd",
        compiler_params=_params(("arbitrary",)))(p, dtb_row, alog_x, ddt_x, dalpha_x, dp_in)


def _gdn_post_fwd(o, p, w_x, *, bt=256):
    T = o.shape[0]
    bt = min(bt, T)

    def body(o_ref, z_ref, w_ref, out_ref):
        for h in range(GDN_H):
            sl = slice(h * 128, (h + 1) * 128)
            oh = o_ref[:, sl]
            r = lax.rsqrt(jnp.mean(oh * oh, axis=1, keepdims=True) + EPS)
            out_ref[:, sl] = (oh * r * w_ref[:, sl] * _silu(z_ref[:, sl].astype(F32))).astype(out_ref.dtype)

    row = pl.BlockSpec((bt, D), lambda i: (i, 0))
    return pl.pallas_call(
        body, grid=(T // bt,),
        in_specs=[row, pl.BlockSpec((bt, D), lambda i: (i, C_ZG // D)), pl.BlockSpec((1, D), lambda i: (0, 0))],
        out_specs=row, out_shape=jax.ShapeDtypeStruct((T, 2 * D), BF16), name="gdn_post_fwd",
        compiler_params=_params(("parallel",)))(o, p, w_x)


def _gdn_post_bwd(dmix, o, p, w_x, *, bt=256):
    T = o.shape[0]
    bt = min(bt, T)

    def body(dm_ref, o_ref, z_ref, w_ref, do_ref, dz_ref, dw_ref):
        i = pl.program_id(0)

        @pl.when(i == 0)
        def _():
            dw_ref[...] = jnp.zeros_like(dw_ref)

        for h in range(GDN_H):
            sl = slice(h * 128, (h + 1) * 128)
            oh, zh, wh, dm = o_ref[:, sl], z_ref[:, sl].astype(F32), w_ref[:, sl], dm_ref[:, sl]
            r = lax.rsqrt(jnp.mean(oh * oh, axis=1, keepdims=True) + EPS)
            ohat = oh * r
            dy = dm * _silu(zh)
            dz_ref[:, sl] = (dm * ohat * wh * _dsilu(zh)).astype(dz_ref.dtype)
            dohat = dy * wh
            do_ref[:, sl] = r * (dohat - ohat * jnp.mean(dohat * ohat, axis=1, keepdims=True))
            dw_ref[:, sl] += jnp.sum(dy * ohat, axis=0, keepdims=True)

    row = pl.BlockSpec((bt, D), lambda i: (i, 0))
    zcol = pl.BlockSpec((bt, D), lambda i: (i, C_ZG // D))
    vec = pl.BlockSpec((1, D), lambda i: (0, 0))
    return pl.pallas_call(
        body, grid=(T // bt,), in_specs=[row, row, zcol, vec], out_specs=[row, zcol, vec],
        out_shape=[jax.ShapeDtypeStruct((T, D), F32), jax.ShapeDtypeStruct((T, C_TOT), BF16),
                   jax.ShapeDtypeStruct((1, D), F32)],
        name="gdn_post_bwd", compiler_params=_params(("arbitrary",)))(dmix, o, p, w_x)


def _ssd_post_fwd(y, xs, p, d_x, w, mix_in, *, bt=256):
    T = y.shape[0]
    bt = min(bt, T)

    def body(y_ref, x_ref, z_ref, d_ref, w_ref, mix_ref, out_ref):
        yg = (y_ref[...] + x_ref[...] * d_ref[...]) * _silu(z_ref[...].astype(F32))
        for g in range(2):
            sl = slice(g * 512, (g + 1) * 512)
            a = yg[:, sl]
            r = lax.rsqrt(jnp.mean(a * a, axis=1, keepdims=True) + EPS)
            out_ref[:, sl] = (a * r * w_ref[:, sl]).astype(out_ref.dtype)

    row = pl.BlockSpec((bt, D), lambda i: (i, 0))
    vec = pl.BlockSpec((1, D), lambda i: (0, 0))
    return pl.pallas_call(
        body, grid=(T // bt,),
        in_specs=[row, row, pl.BlockSpec((bt, D), lambda i: (i, C_ZS // D)), vec, vec, _ANY],
        out_specs=pl.BlockSpec((bt, D), lambda i: (i, 1)), out_shape=jax.ShapeDtypeStruct((T, 2 * D), BF16),
        input_output_aliases={5: 0}, name="ssd_post_fwd",
        compiler_params=_params(("parallel",)))(y, xs, p, d_x, w, mix_in)


def _ssd_post_bwd(dmix, y, xs, p, d_x, w, dp_in, *, bt=256):
    T = y.shape[0]
    bt = min(bt, T)

    def body(dm_ref, y_ref, x_ref, z_ref, d_ref, w_ref, dpin_ref, dyy_ref, dz_ref, dd_ref, dw_ref):
        i = pl.program_id(0)

        @pl.when(i == 0)
        def _():
            dd_ref[...] = jnp.zeros_like(dd_ref)
            dw_ref[...] = jnp.zeros_like(dw_ref)

        xv, zv = x_ref[...], z_ref[...].astype(F32)
        yy = y_ref[...] + xv * d_ref[...]
        sz = _silu(zv)
        yg = yy * sz
        parts = []
        for g in range(2):
            sl = slice(g * 512, (g + 1) * 512)
            a = yg[:, sl]
            r = lax.rsqrt(jnp.mean(a * a, axis=1, keepdims=True) + EPS)
            ah = a * r
            dout = dm_ref[:, sl]
            dah = dout * w_ref[:, sl]
            dw_ref[:, sl] += jnp.sum(dout * ah, axis=0, keepdims=True)
            parts.append(r * (dah - ah * jnp.mean(dah * ah, axis=1, keepdims=True)))
        dyg = jnp.concatenate(parts, axis=1)
        dyy = dyg * sz
        dyy_ref[...] = dyy
        dz_ref[...] = (dyg * yy * _dsilu(zv)).astype(dz_ref.dtype)
        dd_ref[...] += jnp.sum(dyy * xv, axis=0, keepdims=True)

    row = pl.BlockSpec((bt, D), lambda i: (i, 0))
    zcol = pl.BlockSpec((bt, D), lambda i: (i, C_ZS // D))
    vec = pl.BlockSpec((1, D), lambda i: (0, 0))
    return pl.pallas_call(
        body, grid=(T // bt,),
        in_specs=[pl.BlockSpec((bt, D), lambda i: (i, 1)), row, row, zcol, vec, vec, _ANY],
        out_specs=[row, zcol, vec, vec],
        out_shape=[jax.ShapeDtypeStruct((T, D), F32), jax.ShapeDtypeStruct((T, C_TOT), BF16),
                   jax.ShapeDtypeStruct((1, D), F32), jax.ShapeDtypeStruct((1, D), F32)],
        input_output_aliases={6: 1}, name="ssd_post_bwd",
        compiler_params=_params(("arbitrary",)))(dmix, y, xs, p, d_x, w, dp_in)


_NEG = -1e30


def _gdn_terms(q, k, v, bx, gam_c):
    C = GDN_C
    ri, ci = _iota2((C, C), 0), _iota2((C, C), 1)
    eye, low, strict = ri == ci, ri >= ci, ri > ci
    gam_r = jnp.sum(jnp.where(eye, gam_c, 0.0), axis=0, keepdims=True)
    G = jnp.exp(jnp.where(low, gam_c - gam_r, _NEG))
    glast = jnp.sum(jnp.where(_iota2((C, 1), 0) == C - 1, gam_c, 0.0), axis=0, keepdims=True)
    eg, egl, eL = jnp.exp(gam_c), jnp.exp(glast - gam_c), jnp.exp(glast)
    kb, vb = k * bx, v * bx
    M = _dot(kb, k, _NT)
    return dict(eye=eye, low=low, strict=strict, G=G, eg=eg, egl=egl, eL=eL, kb=kb, vb=vb, M=M,
                kbg=kb * eg, qd=q * eg, kd=k * egl, q=q, k=k, v=v, bx=bx)


def _split(a):
    hi = a.astype(_MXU)
    return hi, (a - hi.astype(F32)).astype(_MXU)


def _dot3s(a, b):
    d = lambda p, q: lax.dot_general(p, q, _NN, preferred_element_type=F32)
    return d(a[0], b[0]) + d(a[0], b[1]) + d(a[1], b[0])


def _tri_inv_many(Ls, eye):
    eyef = jnp.where(eye, 1.0, 0.0)
    Ts = [eyef - L for L in Ls]
    Ps = [-L for L in Ls]
    for _ in range(5):
        sp = [_split(p) for p in Ps]
        Ps = [_dot3s(s, s) for s in sp]
        sp = [_split(p) for p in Ps]
        st = [_split(t) for t in Ts]
        Ts = [t + _dot3s(a, b) for t, a, b in zip(Ts, st, sp)]
    return Ts


def _lane_col(tile, idx):
    return jnp.sum(jnp.where(_iota2(tile.shape, 1) == idx, tile, 0.0), axis=1, keepdims=True)


def _gdn_heads(q_ref, k_ref, v_ref, bg_ref, heads):
    out = []
    bg = bg_ref[...]
    for h in heads:
        sl = slice(h * 128, (h + 1) * 128)
        out.append(_gdn_terms(q_ref[:, sl], k_ref[:, sl], v_ref[:, sl], _lane_col(bg, h), _lane_col(bg, GDN_H + h)))
    return out


def _gdn_prep(qk, v, bg, ride=None):
    T = qk.shape[0]
    N = T // GDN_C
    C, CS = GDN_C, GDN_LOCAL_CHUNKS
    NB = N // CS
    n_ride = ride.n if ride else 0

    def body(q_ref, k_ref, v_ref, bg_ref, *rest):
        ride_in = rest[:n_ride]
        u_ref, w_ref, qd_ref, kd_ref, p_ref, t_ref = rest[n_ride:n_ride + 6]
        ride_out = rest[n_ride + 6:2 * n_ride + 6]
        if ride:
            @pl.when(pl.program_id(0) == 0)
            def _():
                ride.start(ride_in, ride_out, rest[-3:])

            @pl.when(pl.program_id(0) == NB - 1)
            def _():
                ride.wait(ride_in, ride_out, rest[-3:])

        items = [(c, h) for c in range(CS) for h in range(GDN_H)]
        views = [[r.at[pl.ds(c * C, C)] for r in (q_ref, k_ref, v_ref, bg_ref)] for c in range(CS)]
        ts = [_gdn_heads(*views[c], [h])[0] for c, h in items]
        Ts = _tri_inv_many([jnp.where(t["strict"], t["M"] * t["G"], 0.0) for t in ts], ts[0]["eye"])
        for (c, h), t, Tm in zip(items, ts, Ts):
            tok = slice(c * C, (c + 1) * C)
            sl = slice(h * 128, (h + 1) * 128)
            rows = slice(h * C, (h + 1) * C)
            u_ref[tok, sl] = _dot(Tm, t["vb"])
            w_ref[tok, sl] = _dot(Tm, t["kbg"]).astype(w_ref.dtype)
            qd_ref[tok, sl] = t["qd"].astype(qd_ref.dtype)
            kd_ref[tok, sl] = t["kd"].astype(kd_ref.dtype)
            p_ref[c, rows, :] = _dot(t["q"], t["k"], _NT) * t["G"]
            t_ref[c, rows, :] = Tm

    blk = lambda c: pl.BlockSpec((CS * C, D), lambda n: (n, c))
    sq = pl.BlockSpec((CS, GDN_H * C, C), lambda n: (n, 0, 0))
    in_specs = [blk(0), blk(1), blk(0), pl.BlockSpec((CS * C, 128), lambda n: (n, 0))]
    out_specs = [blk(0), blk(0), blk(0), blk(0), sq, sq]
    out_shape = [jax.ShapeDtypeStruct((T, D), F32), jax.ShapeDtypeStruct((T, D), BF16),
                 jax.ShapeDtypeStruct((T, D), BF16), jax.ShapeDtypeStruct((T, D), BF16),
                 jax.ShapeDtypeStruct((N, GDN_H * C, C), F32), jax.ShapeDtypeStruct((N, GDN_H * C, C), F32)]
    ins = [qk, qk, v, bg]
    if ride:
        ins, in_specs = ins + ride.srcs, in_specs + ride.specs
        out_shape, out_specs = out_shape + ride.out_shape, out_specs + ride.specs
    res = pl.pallas_call(
        body, grid=(NB,), in_specs=in_specs, out_specs=out_specs, out_shape=out_shape,
        scratch_shapes=ride.scratch if ride else [], name="gdn_prep",
        compiler_params=_params(("arbitrary",) if ride else ("parallel",)))(*ins)
    return (list(res[:6]), list(res[6:])) if ride else list(res)


def _gdn_scan_fwd(u, w, qd, kd, pm, bg):
    T = u.shape[0]
    N = T // GDN_C
    C, CS = GDN_C, GDN_SCAN_CHUNKS

    def body(u_ref, w_ref, qd_ref, kd_ref, p_ref, bg_ref, o_ref, vn_ref, ss_ref, S_scr):
        n = pl.program_id(0)

        @pl.when(n == 0)
        def _():
            S_scr[...] = jnp.zeros_like(S_scr)

        sls = [slice(h * 128, (h + 1) * 128) for h in range(GDN_H)]
        for c in range(CS):
            rows = slice(c * C, (c + 1) * C)
            glast = bg_ref[(c + 1) * C - 1:(c + 1) * C, :]
            Ss = [S_scr[:, sl] for sl in sls]
            vns = [u_ref[rows, sl] - _dot(w_ref[rows, sl], S) for sl, S in zip(sls, Ss)]
            for h, (sl, S, vn) in enumerate(zip(sls, Ss, vns)):
                ss_ref[c, :, sl] = S.astype(ss_ref.dtype)
                vn_ref[rows, sl] = vn.astype(vn_ref.dtype)
                o_ref[rows, sl] = _dot(qd_ref[rows, sl], S) + _dot(p_ref[c, h * C:(h + 1) * C, :], vn)
                S_scr[:, sl] = S * jnp.exp(_lane_col(glast, GDN_H + h)) + _dot(kd_ref[rows, sl], vn, _TN)

    blk = pl.BlockSpec((CS * C, D), lambda n: (n, 0))
    return dict(
        body=body, steps=N // CS, ins=[u, w, qd, kd, pm, bg],
        in_specs=[blk, blk, blk, blk, pl.BlockSpec((CS, GDN_H * C, C), lambda n: (n, 0, 0)),
                  pl.BlockSpec((CS * C, 128), lambda n: (n, 0))],
        out_specs=[blk, blk, pl.BlockSpec((CS, GDN_DK, D), lambda n: (n, 0, 0))],
        out_shape=[jax.ShapeDtypeStruct((T, D), F32), jax.ShapeDtypeStruct((T, D), BF16),
                   jax.ShapeDtypeStruct((N, GDN_DK, D), BF16)],
        scratch=[pltpu.VMEM((GDN_DK, D), F32)])


def _gdn_scan_bwd(w, qd, kd, pm, bg, do):
    T = w.shape[0]
    N = T // GDN_C
    C, CS = GDN_C, GDN_SCAN_CHUNKS
    NB = N // CS

    def body(w_ref, qd_ref, kd_ref, p_ref, bg_ref, do_ref, dvn_ref, ds_ref, dS_scr):
        n = pl.program_id(0)

        @pl.when(n == 0)
        def _():
            dS_scr[...] = jnp.zeros_like(dS_scr)

        sls = [slice(h * 128, (h + 1) * 128) for h in range(GDN_H)]
        for c in reversed(range(CS)):
            rows = slice(c * C, (c + 1) * C)
            glast = bg_ref[(c + 1) * C - 1:(c + 1) * C, :]
            dSs = [dS_scr[:, sl] for sl in sls]
            dvns = [_dot(p_ref[c, h * C:(h + 1) * C, :], do_ref[rows, sl], _TN) + _dot(kd_ref[rows, sl], dS2)
                    for h, (sl, dS2) in enumerate(zip(sls, dSs))]
            for h, (sl, dS2, dvn) in enumerate(zip(sls, dSs, dvns)):
                ds_ref[c, :, sl] = dS2.astype(ds_ref.dtype)
                dvn_ref[rows, sl] = dvn.astype(dvn_ref.dtype)
                dS_scr[:, sl] = (dS2 * jnp.exp(_lane_col(glast, GDN_H + h))
                                 + _dot(qd_ref[rows, sl], do_ref[rows, sl], _TN) - _dot(w_ref[rows, sl], dvn, _TN))

    blk = pl.BlockSpec((CS * C, D), lambda n: (NB - 1 - n, 0))
    return dict(
        body=body, steps=NB, ins=[w, qd, kd, pm, bg, do],
        in_specs=[blk, blk, blk, pl.BlockSpec((CS, GDN_H * C, C), lambda n: (NB - 1 - n, 0, 0)),
                  pl.BlockSpec((CS * C, 128), lambda n: (NB - 1 - n, 0)), blk],
        out_specs=[blk, pl.BlockSpec((CS, GDN_DK, D), lambda n: (NB - 1 - n, 0, 0))],
        out_shape=[jax.ShapeDtypeStruct((T, D), BF16), jax.ShapeDtypeStruct((N, GDN_DK, D), BF16)],
        scratch=[pltpu.VMEM((GDN_DK, D), F32)])


def _gdn_rest_bwd(qk, v, bg, s_save, t_save, vn, dvn, ds_save, do, ride=None):
    T = qk.shape[0]
    N = T // GDN_C
    C, CS = GDN_C, GDN_REST_CHUNKS
    NB = N // CS
    n_ride = ride.n if ride else 0

    def body(q_ref, k_ref, v_ref, bg_ref, ss_ref, ts_ref, vn_ref, dvn_ref, ds_ref, do_ref, *rest):
        ride_in = rest[:n_ride]
        dqkv_ref, dbg_ref = rest[n_ride:n_ride + 2]
        ride_out = rest[n_ride + 2:2 * n_ride + 2]
        if ride:
            @pl.when(pl.program_id(0) == 0)
            def _():
                ride.start(ride_in, ride_out, rest[-3:])

            @pl.when(pl.program_id(0) == NB - 1)
            def _():
                ride.wait(ride_in, ride_out, rest[-3:])

        items = [(c, h) for c in range(CS) for h in range(GDN_H)]
        toks = [slice(c * C, (c + 1) * C) for c, _ in items]
        sls = [slice(h * 128, (h + 1) * 128) for _, h in items]
        views = [[r.at[pl.ds(c * C, C)] for r in (q_ref, k_ref, v_ref, bg_ref)] for c in range(CS)]
        ts = [_gdn_heads(*views[c], [h])[0] for c, h in items]
        Ss = [ss_ref[c, :, sl] for (c, _), sl in zip(items, sls)]
        Tms = [ts_ref[c, h * C:(h + 1) * C, :] for c, h in items]
        dS2s = [ds_ref[c, :, sl] for (c, _), sl in zip(items, sls)]
        dos = [do_ref[tok, sl] for tok, sl in zip(toks, sls)]
        vns = [vn_ref[tok, sl] for tok, sl in zip(toks, sls)]
        dvns = [dvn_ref[tok, sl] for tok, sl in zip(toks, sls)]
        Qs = [_dot(t["q"], t["k"], _NT) for t in ts]
        dws = [-_dot(dvn, S, _NT) for dvn, S in zip(dvns, Ss)]
        dqds = [_dot(do, S, _NT) for do, S in zip(dos, Ss)]
        dPs = [jnp.where(t["low"], _dot(do, vn, _NT), 0.0) for t, do, vn in zip(ts, dos, vns)]
        dkds = [_dot(vn, dS2, _NT) for vn, dS2 in zip(vns, dS2s)]
        dTs = [_dot(dvn, t["vb"], _NT) + _dot(dw, t["kbg"], _NT) for t, dvn, dw in zip(ts, dvns, dws)]
        dvbs = [_dot(Tm, dvn, _TN) for Tm, dvn in zip(Tms, dvns)]
        dkbgs = [_dot(Tm, dw, _TN) for Tm, dw in zip(Tms, dws)]
        TdTs = [_dot(Tm, dT, _TN) for Tm, dT in zip(Tms, dTs)]
        dLs = [jnp.where(t["strict"], -_dot(TdT, Tm, _NT), 0.0) for t, TdT, Tm in zip(ts, TdTs, Tms)]
        dMs = [dL * t["G"] for t, dL in zip(ts, dLs)]
        dQs = [dP * t["G"] for t, dP in zip(ts, dPs)]
        dkbs = [_dot(dM, t["k"]) + dkbg * t["eg"] for t, dM, dkbg in zip(ts, dMs, dkbgs)]
        rs = lambda a: jnp.sum(a, axis=1, keepdims=True)
        lane = _iota2((C, 128), 1)
        last = _iota2((C, 1), 0) == C - 1
        dbg = [jnp.zeros((C, 128), F32) for _ in range(CS)]
        for i, (c, h) in enumerate(items):
            t, sl, tok = ts[i], sls[i], toks[i]
            E = (dLs[i] * t["M"] + dPs[i] * Qs[i]) * t["G"]
            dqkv_ref[tok, sl] = _dot(dQs[i], t["k"]) + dqds[i] * t["eg"]
            dqkv_ref[tok, D + h * 128:D + (h + 1) * 128] = (
                _dot(dQs[i], t["q"], _TN) + _dot(dMs[i], t["kb"], _TN) + dkds[i] * t["egl"] + dkbs[i] * t["bx"])
            dqkv_ref[tok, 2 * D + h * 128:2 * D + (h + 1) * 128] = dvbs[i] * t["bx"]
            dbeta_c = rs(dkbs[i] * t["k"] + dvbs[i] * t["v"])
            dkd_kd = dkds[i] * t["kd"]
            dgam_c = rs(dqds[i] * t["qd"]) + rs(dkbgs[i] * t["kbg"]) - rs(dkd_kd) + rs(E)
            dgam_r = -jnp.sum(E, axis=0, keepdims=True)
            dgam_c = dgam_c + jnp.sum(jnp.where(t["eye"], dgam_r, 0.0), axis=1, keepdims=True)
            dlast = _sum_all(dkd_kd) + t["eL"] * _sum_all(Ss[i].astype(F32) * dS2s[i].astype(F32))
            dgam_c = dgam_c + jnp.where(last, dlast, 0.0)
            dbg[c] = dbg[c] + jnp.where(lane == h, dbeta_c, 0.0) + jnp.where(lane == GDN_H + h, dgam_c, 0.0)
        for c in range(CS):
            dbg_ref[c * C:(c + 1) * C, :] = dbg[c]

    blk = lambda c: pl.BlockSpec((CS * C, D), lambda n: (n, c))
    st = pl.BlockSpec((CS, GDN_DK, D), lambda n: (n, 0, 0))
    seg = pl.BlockSpec((CS * C, 128), lambda n: (n, 0))
    in_specs = [blk(0), blk(1), blk(0), seg, st,
                pl.BlockSpec((CS, GDN_H * C, C), lambda n: (n, 0, 0)), blk(0), blk(0), st, blk(0)]
    out_specs = [pl.BlockSpec((CS * C, 3 * D), lambda n: (n, 0)), seg]
    out_shape = [jax.ShapeDtypeStruct((T, 3 * D), F32), jax.ShapeDtypeStruct((T, 128), F32)]
    ins = [qk, qk, v, bg, s_save, t_save, vn, dvn, ds_save, do]
    if ride:
        ins, in_specs = ins + ride.srcs, in_specs + ride.specs
        out_shape, out_specs = out_shape + ride.out_shape, out_specs + ride.specs
    res = pl.pallas_call(
        body, grid=(NB,), in_specs=in_specs, out_specs=out_specs, out_shape=out_shape,
        scratch_shapes=ride.scratch if ride else [], name="gdn_rest_bwd",
        compiler_params=_params(("arbitrary",) if ride else ("parallel",)))(*ins)
    return (list(res[:2]), list(res[2:])) if ride else list(res)


def _ssd_seg(al_pair, half, s):
    L = SSM_L
    ri, ci = _iota2((L, L), 0), _iota2((L, L), 1)
    ac = jnp.max(jnp.where(half == s, al_pair, _NEG), axis=1, keepdims=True)
    ar = jnp.sum(jnp.where(ri == ci, ac, 0.0), axis=0, keepdims=True)
    return jnp.exp(jnp.where(ri >= ci, ac - ar, _NEG))


def _last_row(a):
    return jnp.sum(jnp.where(_iota2((a.shape[0], 1), 0) == a.shape[0] - 1, a, 0.0), axis=0, keepdims=True)


def _ssd_core_fwd(xbc, dtx, alx):
    T = xbc.shape[0]
    L, CS = SSM_L, SSM_SCAN_CHUNKS
    Nc = T // L

    def body(x_all, bc_all, dt_all, al_all, y_all, hs_all, H_scr):
        @pl.when(pl.program_id(0) == 0)
        def _():
            H_scr[...] = jnp.zeros_like(H_scr)

        for cc in range(CS):
            rows = pl.ds(cc * L, L)
            chunk(x_all.at[rows], bc_all.at[rows], dt_all.at[rows], al_all.at[rows], y_all.at[rows], hs_all.at[cc],
                  H_scr)

    def chunk(x_ref, bc_ref, dt_ref, al_ref, y_ref, hs_ref, H_scr):
        half = _iota2((L, 128), 1) >> 6
        for g in range(2):
            gs = slice(g * 512, (g + 1) * 512)
            Bg = bc_ref[:, g * 128:(g + 1) * 128]
            Cg = bc_ref[:, 256 + g * 128:256 + (g + 1) * 128]
            alg = al_ref[:, gs]
            alast = _last_row(alg)
            xdt = x_ref[:, gs] * dt_ref[:, gs]
            Hg = H_scr[:, gs]
            hs_ref[:, gs] = Hg
            CB = _dot(Cg, Bg, _NT)
            y_ref[:, gs] = jnp.exp(alg) * _dot(Cg, Hg)
            H_scr[:, gs] = Hg * jnp.exp(alast) + _dot(Bg, jnp.exp(alast - alg) * xdt, _TN)
            for j in range(4):
                ps = slice(g * 512 + j * 128, g * 512 + (j + 1) * 128)
                al_pair = al_ref[:, ps]
                xp = x_ref[:, ps] * dt_ref[:, ps]
                ys = [_dot(_ssd_seg(al_pair, half, s) * CB, xp) for s in range(2)]
                y_ref[:, ps] += jnp.where(half == 0, ys[0], ys[1])

    row = pl.BlockSpec((CS * L, D), lambda c: (c, 0))
    return dict(
        body=body, steps=Nc // CS, ins=[xbc, xbc, dtx, alx],
        in_specs=[row, pl.BlockSpec((CS * L, 512), lambda c: (c, 2)), row, row],
        out_specs=[row, pl.BlockSpec((CS, SSM_N, D), lambda c: (c, 0, 0))],
        out_shape=[jax.ShapeDtypeStruct((T, D), F32), jax.ShapeDtypeStruct((Nc, SSM_N, D), F32)],
        scratch=[pltpu.VMEM((SSM_N, D), F32)])


def _ssd_core_bwd(xbc, dtx, alx, h_save, dyy, d_x):
    T = xbc.shape[0]
    L, CS = SSM_L, SSM_SCAN_CHUNKS
    Nc = T // L
    NB = Nc // CS

    def body(x_all, bc_all, dt_all, al_all, hs_all, dy_all, d_ref, dx_all, ddt_all, dal_all, dH_scr):
        @pl.when(pl.program_id(0) == 0)
        def _():
            dH_scr[...] = jnp.zeros_like(dH_scr)

        for cc in reversed(range(CS)):
            rows = pl.ds(cc * L, L)
            chunk(x_all.at[rows], bc_all.at[rows], dt_all.at[rows], al_all.at[rows], hs_all.at[cc], dy_all.at[rows],
                  d_ref, dx_all.at[rows], ddt_all.at[rows], dal_all.at[rows], dH_scr)

    def chunk(x_ref, bc_ref, dt_ref, al_ref, hs_ref, dy_ref, d_ref, dx_ref, ddt_ref, dal_ref, dH_scr):
        lane = _iota2((L, 128), 1)
        half = lane >> 6
        rowi = _iota2((L, 1), 0)
        ri, ci = _iota2((L, L), 0), _iota2((L, L), 1)
        for g in range(2):
            gs = slice(g * 512, (g + 1) * 512)
            Bg = bc_ref[:, g * 128:(g + 1) * 128]
            Cg = bc_ref[:, 256 + g * 128:256 + (g + 1) * 128]
            alg = al_ref[:, gs]
            alast = _last_row(alg)
            eal, edec, eL = jnp.exp(alg), jnp.exp(alast - alg), jnp.exp(alast)
            xg, dtg, dYg = x_ref[:, gs], dt_ref[:, gs], dy_ref[:, gs]
            xdt = xg * dtg
            Hg = hs_ref[:, gs]
            dH2 = dH_scr[:, gs]
            CB = _dot(Cg, Bg, _NT)
            dYe = eal * dYg
            dH_scr[:, gs] = dH2 * eL + _dot(Cg, dYe, _TN)
            dC = _dot(dYe, Hg, _NT)
            zg = edec * xdt
            dz = _dot(Bg, dH2)
            dB = _dot(zg, dH2, _NT)
            tz = dz * zg
            dal = dYe * _dot(Cg, Hg) - tz
            dalast = jnp.sum(tz, axis=0, keepdims=True) + eL * jnp.sum(Hg * dH2, axis=0, keepdims=True)
            dal = dal + jnp.where(rowi == L - 1, dalast, 0.0)
            dxdt_g = edec * dz
            dx_ref[:, gs] = dxdt_g * dtg + dYg * d_ref[:, gs]
            ddt_ref[:, gs] = dxdt_g * xg
            dal_ref[:, gs] = dal
            dCB = jnp.zeros((L, L), F32)
            for j in range(4):
                ps = slice(g * 512 + j * 128, g * 512 + (j + 1) * 128)
                al_pair = al_ref[:, ps]
                xp = x_ref[:, ps] * dt_ref[:, ps]
                dYp = dy_ref[:, ps]
                dxp = []
                dal_p = jnp.zeros((L, 128), F32)
                for s in range(2):
                    seg = _ssd_seg(al_pair, half, s)
                    W = seg * CB
                    dW = _dot(jnp.where(half == s, dYp, 0.0), xp, _NT)
                    dxp.append(_dot(W, dYp, _TN))
                    dCB = dCB + dW * seg
                    Es = dW * W
                    dac = jnp.sum(Es, axis=1, keepdims=True) - jnp.sum(
                        jnp.where(ri == ci, jnp.sum(Es, axis=0, keepdims=True), 0.0), axis=1, keepdims=True)
                    dal_p = dal_p + jnp.where(lane == 64 * s, dac, 0.0)
                dxdt_p = jnp.where(half == 0, dxp[0], dxp[1])
                dx_ref[:, ps] += dxdt_p * dt_ref[:, ps]
                ddt_ref[:, ps] += dxdt_p * x_ref[:, ps]
                dal_ref[:, ps] += dal_p
            dx_ref[:, D + g * 128:D + (g + 1) * 128] = dB + _dot(dCB, Cg, _TN)
            dx_ref[:, D + 256 + g * 128:D + 256 + (g + 1) * 128] = dC + _dot(dCB, Bg)

    row = pl.BlockSpec((CS * L, D), lambda c: (NB - 1 - c, 0))
    bcs = pl.BlockSpec((CS * L, 512), lambda c: (NB - 1 - c, 2))
    return dict(
        body=body, steps=NB, ins=[xbc, xbc, dtx, alx, h_save, dyy, d_x],
        in_specs=[row, bcs, row, row, pl.BlockSpec((CS, SSM_N, D), lambda c: (NB - 1 - c, 0, 0)), row,
                  pl.BlockSpec((1, D), lambda c: (0, 0))],
        out_specs=[pl.BlockSpec((CS * L, D + 512), lambda c: (NB - 1 - c, 0)), row, row],
        out_shape=[jax.ShapeDtypeStruct((T, D + 512), F32),
                   jax.ShapeDtypeStruct((T, D), F32), jax.ShapeDtypeStruct((T, D), F32)],
        scratch=[pltpu.VMEM((SSM_N, D), F32)])


def _run_scans(parts, *, name):
    steps = parts[0]["steps"]
    assert all(p["steps"] == steps for p in parts)
    cnt = lambda key: [len(p[key]) for p in parts]
    n_in, n_out, n_scr = cnt("ins"), cnt("out_shape"), cnt("scratch")

    def body(*refs):
        ins, outs, scr = refs[:sum(n_in)], refs[sum(n_in):sum(n_in) + sum(n_out)], refs[sum(n_in) + sum(n_out):]
        oi = oo = os_ = 0
        for p, a, b, c in zip(parts, n_in, n_out, n_scr):
            p["body"](*ins[oi:oi + a], *outs[oo:oo + b], *scr[os_:os_ + c])
            oi, oo, os_ = oi + a, oo + b, os_ + c

    cat = lambda key: [v for p in parts for v in p[key]]
    res = pl.pallas_call(
        body, grid=(steps,), in_specs=cat("in_specs"), out_specs=cat("out_specs"), out_shape=cat("out_shape"),
        scratch_shapes=cat("scratch"), name=name, compiler_params=_params(("arbitrary",)))(*cat("ins"))
    out, o = [], 0
    for b in n_out:
        out.append(list(res[o:o + b]))
        o += b
    return out


_EARLY = ("w_out", "wq_mem", "wk_mem", "wv_mem", "wo_mem")
_LATE = ("w_up", "w_down")
_GRADS_MLP = ("w_down", "w_up")
_GRADS_MID = ("wo_mem", "wq_mem", "wk_mem", "wv_mem", "w_out")


def _gather_ride(shards, names):
    return None if shards is None else _Ride([shards[n] for n in names], shard=True)


def _grad_ride(shards, G, names):
    return None if shards is None else _Ride([_slots_from_full(n, G[n]) for n in names], shard=False)


def _local_step(x, mem, tgt, W, shards=None):
    T = x.shape[0]
    W = dict(W)
    cw_qk, cw_v = W["gdn_conv_w"][:, :2 * D], W["gdn_conv_w"][:, 2 * D:]
    h1 = _rmsnorm_fwd(x, W["norm1_w"], name="norm1_fwd")
    ride = _gather_ride(shards, _EARLY)
    pg = _mm(h1, W["w_in_pad"], b_cols=(C_GATE, C_TOT - C_GATE), name="in_proj_gates")
    p = _mm(h1, W["w_in_pad"], b_cols=(0, C_GATE), out_dtype=BF16, bn_cap=1664, name="in_proj", ride=ride)
    if ride:
        p, got = p
        W.update({n: _full_from_slots(n, g) for n, g in zip(_EARLY, got)})
    qk = _conv_fwd(p, C_QKV, 2 * D, cw_qk, None, l2=True, name="gdn_conv_qk_fwd")
    v_g = _conv_fwd(p, C_QKV + 2 * D, D, cw_v, None, l2=False, name="gdn_conv_v_fwd")
    bg = _gdn_gates_fwd(pg, W["gdn_alog_row"], W["gdn_dtb_row"])
    ride = _gather_ride(shards, _LATE)
    prep = _gdn_prep(qk, v_g, bg, ride)
    if ride:
        prep, got = prep
        W.update({n: _full_from_slots(n, g) for n, g in zip(_LATE, got)})
    u_g, w_g, qd_g, kd_g, p_g, t_save = prep
    xbc = _conv_fwd(p, C_XBC, D + 512, W["ssm_conv_w"], W["ssm_conv_b"], l2=False, name="ssm_conv_fwd", bc=512)
    dtx, alx = _ssd_dt_fwd(pg, W["ssm_dtb_row"], W["ssm_alog_x"])
    (o_g, vn_g, s_save), (y_s, h_save) = _run_scans(
        [_gdn_scan_fwd(u_g, w_g, qd_g, kd_g, p_g, bg), _ssd_core_fwd(xbc, dtx, alx)], name="scans_fwd")
    mix = _gdn_post_fwd(o_g, p, W["gdn_norm_x"])
    mix = _ssd_post_fwd(y_s, xbc, p, W["ssm_d_x"], W["ssm_norm_w"].reshape(1, D), mix)
    x1, h2 = _mm(mix, W["w_out"], epi="res_norm", extra=(x, W["norm2_w"]), bm=512, name="out_proj")
    qm = _mm(h2, W["wq_mem"], out_dtype=BF16, name="q_proj")
    m = _rmsnorm_fwd(mem, W["mem_norm_w"], name="mem_norm_fwd")
    km = _mm(m, W["wk_mem"], name="k_proj")
    vm = _mm(m, W["wv_mem"], name="v_proj")
    oa = _attn_fwd(qm, km, vm)
    x2, h3 = _mm(oa, W["wo_mem"], epi="res_norm", extra=(x1, W["norm3_w"]), bm=512, name="o_proj")
    u, act = _mm(h3, W["w_up"], epi="relu2", out_dtype=BF16, name="mlp_up")
    dx3, g_final, loss = _mm(act, W["w_down"], epi="res_loss", extra=(x2, tgt, W["final_norm_w"]), bk_cap=1024,
                             name="mlp_down_loss")
    G = {"final_norm_w": g_final.reshape(D)}
    dpre = _mm(dx3, W["w_down"], dims="nt", epi="mul2", extra=u, out_dtype=BF16, name="mlp_down_dx")
    G["w_down"] = _mm(act, dx3, dims="tn", out_dtype=BF16, name="mlp_down_dw")
    G["w_up"] = _mm(h3, dpre, dims="tn", out_dtype=BF16, name="mlp_up_dw")
    dx2, gw = _mm(dpre, W["w_up"], dims="nt", epi="norm_bwd", extra=(x2, dx3, W["norm3_w"]), bk_cap=1024,
                  name="mlp_up_dx")
    G["norm3_w"] = gw.reshape(D)
    do_a = _mm(dx2, W["wo_mem"], dims="nt", out_dtype=BF16, name="o_proj_dx")
    G["wo_mem"] = _mm(oa, dx2, dims="tn", out_dtype=BF16, name="o_proj_dw")
    dq, dk, dv = _attn_bwd(qm, km, vm, do_a)
    G["wq_mem"] = _mm(h2, dq, dims="tn", out_dtype=BF16, name="q_proj_dw")
    dx1, gw = _mm(dq, W["wq_mem"], dims="nt", epi="norm_bwd", extra=(x1, dx2, W["norm2_w"]), bm=512,
                  name="q_proj_dx")
    G["norm2_w"] = gw.reshape(D)
    G["wk_mem"] = _mm(m, dk, dims="tn", out_dtype=BF16, name="k_proj_dw")
    G["wv_mem"] = _mm(m, dv, dims="tn", out_dtype=BF16, name="v_proj_dw")
    dm = _mm(dk, W["wk_mem"], dims="nt", name="k_proj_dx")
    dm = _mm(dv, W["wv_mem"], dims="nt", epi="res", extra=dm, name="v_proj_dx")
    _, G["mem_norm_w"] = _rmsnorm_bwd(mem, W["mem_norm_w"], dm, None, name="mem_norm_bwd")
    dmix = _mm(dx1, W["w_out"], dims="nt", name="out_proj_dx")
    G["w_out"] = _mm(mix, dx1, dims="tn", out_dtype=BF16, name="out_proj_dw")
    do_g, dp, G["gdn_norm_x"] = _gdn_post_bwd(dmix, o_g, p, W["gdn_norm_x"])
    dyy, dp, G["ssm_d_x"], G["ssm_norm_w"] = _ssd_post_bwd(dmix, y_s, xbc, p, W["ssm_d_x"],
                                                          W["ssm_norm_w"].reshape(1, D), dp)
    (dvn_g, ds_save), (dxbc, ddtx, dalx) = _run_scans(
        [_gdn_scan_bwd(w_g, qd_g, kd_g, p_g, bg, do_g), _ssd_core_bwd(xbc, dtx, alx, h_save, dyy, W["ssm_d_x"])],
        name="scans_bwd")
    ride = _grad_ride(shards, G, _GRADS_MLP)
    rest = _gdn_rest_bwd(qk, v_g, bg, s_save, t_save, vn_g, dvn_g, ds_save, do_g, ride)
    if ride:
        rest, got = rest
        G.update(zip(_GRADS_MLP, got))
    dqkvn, dbg = rest
    dy_qk, gcw_qk, _ = _conv_bwd_act(p, C_QKV, 2 * D, cw_qk, None, dqkvn, 0, l2=True, name="gdn_conv_qk_bwd_act")
    dy_v, gcw_v, _ = _conv_bwd_act(p, C_QKV + 2 * D, D, cw_v, None, dqkvn, 2 * D, l2=False,
                                   name="gdn_conv_v_bwd_act")
    G["gdn_conv_w"] = jnp.concatenate([gcw_qk, gcw_v], axis=1)
    dp = _conv_bwd_in(dy_qk, cw_qk, dp, C_QKV, T, name="gdn_conv_qk_bwd_in")
    dp = _conv_bwd_in(dy_v, cw_v, dp, C_QKV + 2 * D, T, name="gdn_conv_v_bwd_in")
    dp, G["gdn_alog_row"], G["gdn_dtb_row"] = _gdn_gates_bwd(pg, W["gdn_alog_row"], W["gdn_dtb_row"], dbg, dp)
    dy_s, G["ssm_conv_w"], G["ssm_conv_b"] = _conv_bwd_act(p, C_XBC, D + 512, W["ssm_conv_w"], W["ssm_conv_b"],
                                                           dxbc, 0, l2=False, name="ssm_conv_bwd_act", bc=512)
    dp = _conv_bwd_in(dy_s, W["ssm_conv_w"], dp, C_XBC, T, name="ssm_conv_bwd_in", bc=512)
    dp, G["ssm_dtb_row"], G["ssm_alog_x"] = _ssd_dt_bwd(pg, W["ssm_dtb_row"], W["ssm_alog_x"], ddtx, dalx, dp)
    ride = _grad_ride(shards, G, _GRADS_MID)
    g_in = _mm(h1, dp, dims="tn", out_dtype=BF16, bn_cap=1152, name="in_proj_dw", ride=ride)
    if ride:
        g_in, got = g_in
        G.update(zip(_GRADS_MID, got))
    G["w_in"] = _unpad_w_in(g_in)
    ride = _grad_ride(shards, G, ("w_in",))
    res = _mm(dp, W["w_in_pad"], dims="nt", epi="norm_bwd", extra=(x, dx1, W["norm1_w"]),
              name="in_proj_dx", ride=ride)
    if ride:
        res, got = res
        G["w_in"] = got[0]
    dx, gw = res
    G["norm1_w"] = gw.reshape(D)
    return loss, dx, G


def _all_gather(shards, out_dtype, *, name):
    n = len(shards)

    def body(*refs):
        x_refs, out_refs, stage = refs[:n], refs[n:2 * n], refs[2 * n:3 * n]
        send_sems, recv_sems, local_sems = refs[3 * n:]
        x, y, c = _place()
        me, sibling = (x, y, c), (x, y, 1 - c)
        chips = [(1 - x, y), (x, 1 - y), (1 - x, 1 - y)]

        def slot(px, py, pc):
            return 4 * px + 2 * py + pc

        def copy(a, k, block, to, src=None):
            dst = out_refs[a].at[slot(*block)]
            return pltpu.make_async_remote_copy(
                src_ref=dst if src is None else src, dst_ref=dst, send_sem=send_sems.at[a, k],
                recv_sem=recv_sems.at[a, k], device_id=to, device_id_type=_MESH)

        for a in range(n):
            stage[a][...] = x_refs[a][...].astype(out_dtype)
        mine = [pltpu.make_async_copy(stage[a], out_refs[a].at[slot(*me)], local_sems.at[a]) for a in range(n)]
        for cp in mine:
            cp.start()
        first = []
        for a in range(n):
            first.append(copy(a, 0, me, sibling, src=stage[a]))
            first += [copy(a, 1 + j, me, (*chip, c), src=stage[a]) for j, chip in enumerate(chips)]
        for cp in first:
            cp.start()
        passed = [[copy(a, 4 + j, (*chip, c), sibling) for j, chip in enumerate(chips)] for a in range(n)]
        for j, chip in enumerate(chips):
            for a in range(n):
                copy(a, 1 + j, (*chip, c), me).wait_recv()
                passed[a][j].start()
        for a in range(n):
            copy(a, 0, sibling, me).wait_recv()
            for j, chip in enumerate(chips):
                copy(a, 4 + j, (*chip, 1 - c), me).wait_recv()
        for cp in first + [cp for row in passed for cp in row]:
            cp.wait_send()
        for cp in mine:
            cp.wait()

    outs = pl.pallas_call(
        body, in_specs=[_VM] * n, out_specs=[_ANY] * n,
        out_shape=[jax.ShapeDtypeStruct((N_DEV,) + s.shape, out_dtype) for s in shards],
        scratch_shapes=[pltpu.VMEM(s.shape, out_dtype) for s in shards]
        + [pltpu.SemaphoreType.DMA((n, 7)), pltpu.SemaphoreType.DMA((n, 7)), pltpu.SemaphoreType.DMA((n,))],
        name=name, compiler_params=pltpu.CompilerParams(vmem_limit_bytes=VMEM_LIMIT))(*shards)
    return list(outs)


def _cast_bf16(arrs, *, name):
    n = len(arrs)

    def body(*refs):
        for a in range(n):
            refs[n + a][...] = refs[a][...].astype(BF16)

    return list(pl.pallas_call(
        body, in_specs=[_VM] * n, out_specs=[_VM] * n,
        out_shape=[jax.ShapeDtypeStruct(s.shape, BF16) for s in arrs], name=name,
        compiler_params=pltpu.CompilerParams(vmem_limit_bytes=VMEM_LIMIT))(*arrs))


def _sum8(a, *, name):
    _, R, Cc = a.shape
    br = _pick_rows(R, 128)

    def body(a_ref, o_ref):
        s = a_ref[0].astype(F32)
        for k in range(1, N_DEV):
            s = s + a_ref[k].astype(F32)
        o_ref[...] = s

    return pl.pallas_call(
        body, grid=(R // br,), in_specs=[pl.BlockSpec((N_DEV, br, Cc), lambda i: (0, i, 0))],
        out_specs=pl.BlockSpec((br, Cc), lambda i: (i, 0)), out_shape=jax.ShapeDtypeStruct((R, Cc), F32),
        name=name, compiler_params=_params(("parallel",)))(a)


def _pick_rows(R, cap):
    if R <= cap:
        return R
    for d in range(cap, 7, -8):
        if R % d == 0:
            return d
    return R


def _adamw(w, g, m, v, *, name):
    shape = w.shape
    as2d = (lambda t: t.reshape(1, -1)) if w.ndim == 1 else (lambda t: t)
    w2, m2, v2 = as2d(w), as2d(m), as2d(v)
    R, Cc = w2.shape
    from_slabs = g.ndim == 3
    br = _pick_rows(R, 128 if from_slabs else 256)
    c1 = 1.0 - ADAM_B1 ** ADAM_STEP
    c2 = 1.0 - ADAM_B2 ** ADAM_STEP

    def body(w_ref, g_ref, m_ref, v_ref, go_ref, d_ref, nm_ref, nv_ref):
        if from_slabs:
            gv = g_ref[0].astype(F32)
            for k in range(1, N_DEV):
                gv = gv + g_ref[k].astype(F32)
        else:
            gv = g_ref[...]
        go_ref[...] = gv
        nm = ADAM_B1 * m_ref[...] + (1.0 - ADAM_B1) * gv
        nv = ADAM_B2 * v_ref[...] + (1.0 - ADAM_B2) * (gv * gv)
        nm_ref[...] = nm
        nv_ref[...] = nv
        d_ref[...] = -ADAM_LR * ((nm / c1) / (jnp.sqrt(nv / c2) + ADAM_EPS) + ADAM_WD * w_ref[...])

    blk = pl.BlockSpec((br, Cc), lambda i: (i, 0))
    g_spec = pl.BlockSpec((N_DEV, br, Cc), lambda i: (0, i, 0)) if from_slabs else blk
    outs = pl.pallas_call(
        body, grid=(R // br,), in_specs=[blk, g_spec, blk, blk], out_specs=[blk] * 4,
        out_shape=[jax.ShapeDtypeStruct((R, Cc), F32)] * 4, name=name,
        compiler_params=_params(("parallel",)))(w2, g if from_slabs else as2d(g), m2, v2)
    return tuple(o.reshape(shape) for o in outs)


_BIG = ("w_in", "w_out", "wq_mem", "wk_mem", "wv_mem", "wo_mem", "w_up", "w_down")
_COL_SHARDED = ("w_in", "w_up")
_WEIGHTS = ("norm1_w", "w_in", "gdn_conv_w", "gdn_a_log", "gdn_dt_bias", "gdn_norm_w", "ssm_conv_w", "ssm_conv_b",
            "ssm_a_log", "ssm_dt_bias", "ssm_d", "ssm_norm_w", "w_out", "norm2_w", "mem_norm_w", "wq_mem", "wk_mem",
            "wv_mem", "wo_mem", "norm3_w", "w_up", "w_down", "final_norm_w")
_IN_PAD = 112


def _full_from_slots(name, g):
    if name in _COL_SHARDED:
        return jnp.transpose(g, (1, 0, 2)).reshape(g.shape[1], N_DEV * g.shape[2])
    return g.reshape(N_DEV * g.shape[1], g.shape[2])


def _slots_from_full(name, f):
    if name in _COL_SHARDED:
        return jnp.transpose(f.reshape(f.shape[0], N_DEV, f.shape[1] // N_DEV), (1, 0, 2))
    return f.reshape(N_DEV, f.shape[0] // N_DEV, f.shape[1])


def _pad_w_in(w):
    z = jnp.zeros((w.shape[0], _IN_PAD), w.dtype)
    return jnp.concatenate([w[:, :4096], w[:, 4112:6672], w[:, 4096:4112], z, w[:, 6672:6688], z], axis=1)


def _unpad_w_in(gp):
    return jnp.concatenate([gp[:, :4096], gp[:, C_GATE:C_GATE + 16], gp[:, 4096:C_GATE], gp[:, C_DT:C_DT + 16]],
                           axis=1)


def _pack_rows(vals):
    rows, offs, r = [], [], 0
    for vflat in vals:
        nrow = 8 * -(-vflat.shape[0] // 1024)
        rows.append(jnp.pad(vflat, (0, nrow * 128 - vflat.shape[0])).reshape(nrow, 128))
        offs.append((r, vflat.shape[0]))
        r += nrow
    return jnp.concatenate(rows, axis=0), offs


def _unpack_rows(packed, offs, shapes):
    out = []
    for (r, nel), shp in zip(offs, shapes):
        nrow = -(-nel // 128)
        out.append(packed[r:r + nrow].reshape(-1)[:nel].reshape(shp))
    return out


def kernel(x, mem, norm1_w, w_in, gdn_conv_w, gdn_a_log, gdn_dt_bias, gdn_norm_w, ssm_conv_w, ssm_conv_b, ssm_a_log, ssm_dt_bias, ssm_d, ssm_norm_w, w_out, norm2_w, mem_norm_w, wq_mem, wk_mem, wv_mem, wo_mem, norm3_w, w_up, w_down, final_norm_w, loss_target, m_norm1_w, m_w_in, m_gdn_conv_w, m_gdn_a_log, m_gdn_dt_bias, m_gdn_norm_w, m_ssm_conv_w, m_ssm_conv_b, m_ssm_a_log, m_ssm_dt_bias, m_ssm_d, m_ssm_norm_w, m_w_out, m_norm2_w, m_mem_norm_w, m_wq_mem, m_wk_mem, m_wv_mem, m_wo_mem, m_norm3_w, m_w_up, m_w_down, m_final_norm_w, v_norm1_w, v_w_in, v_gdn_conv_w, v_gdn_a_log, v_gdn_dt_bias, v_gdn_norm_w, v_ssm_conv_w, v_ssm_conv_b, v_ssm_a_log, v_ssm_dt_bias, v_ssm_d, v_ssm_norm_w, v_w_out, v_norm2_w, v_mem_norm_w, v_wq_mem, v_wk_mem, v_wv_mem, v_wo_mem, v_norm3_w, v_w_up, v_w_down, v_final_norm_w):
    args = dict(locals())
    w_loc = {n: args[n] for n in _WEIGHTS}
    me = 4 * lax.axis_index("x") + 2 * lax.axis_index("y") + lax.axis_index("c")

    w_in_full = _full_from_slots("w_in", _all_gather([w_in], BF16, name="gather_w_in")[0])
    later = _EARLY + _LATE
    shards = dict(zip(later, _cast_bf16([w_loc[n] for n in later], name="cast_shards")))
    conv_pack, conv_offs = _pack_rows([gdn_conv_w.reshape(-1), ssm_conv_w.reshape(-1)])
    conv_all = _all_gather([conv_pack], F32, name="gather_conv")[0]
    gdn_cw, ssm_cw = [], []
    for k in range(N_DEV):
        a, b = _unpack_rows(conv_all[k], conv_offs, [gdn_conv_w.shape, ssm_conv_w.shape])
        gdn_cw.append(a)
        ssm_cw.append(b)
    W = {
        "w_in_pad": _pad_w_in(w_in_full),
        "norm1_w": norm1_w, "norm2_w": norm2_w, "norm3_w": norm3_w, "mem_norm_w": mem_norm_w,
        "final_norm_w": final_norm_w, "ssm_norm_w": ssm_norm_w, "ssm_conv_b": ssm_conv_b,
        "gdn_conv_w": jnp.concatenate(gdn_cw, axis=1), "ssm_conv_w": jnp.concatenate(ssm_cw, axis=1),
        "gdn_alog_row": jnp.pad(gdn_a_log, (GDN_H, 128 - 2 * GDN_H)).reshape(1, 128),
        "gdn_dtb_row": jnp.pad(gdn_dt_bias, (GDN_H, 128 - 2 * GDN_H)).reshape(1, 128),
        "gdn_norm_x": jnp.tile(gdn_norm_w, GDN_H).reshape(1, D),
        "ssm_dtb_row": jnp.pad(ssm_dt_bias, (0, 128 - SSM_H)).reshape(1, 128),
        "ssm_alog_x": jnp.repeat(ssm_a_log, SSM_P).reshape(1, D),
        "ssm_d_x": jnp.repeat(ssm_d, SSM_P).reshape(1, D),
    }

    loss_part, grad_x, G = _local_step(x[0], mem[0], loss_target[0], W, shards)

    grads = {n: G[n] for n in _BIG}

    small = {
        "norm1_w": G["norm1_w"], "gdn_conv_w": G["gdn_conv_w"], "gdn_a_log": G["gdn_alog_row"][0, GDN_H:2 * GDN_H],
        "gdn_dt_bias": G["gdn_dtb_row"][0, GDN_H:2 * GDN_H], "gdn_norm_w": G["gdn_norm_x"].reshape(GDN_H, 128).sum(0),
        "ssm_conv_w": G["ssm_conv_w"], "ssm_conv_b": G["ssm_conv_b"],
        "ssm_a_log": G["ssm_alog_x"].reshape(SSM_H, SSM_P).sum(1), "ssm_dt_bias": G["ssm_dtb_row"][0, :SSM_H],
        "ssm_d": G["ssm_d_x"].reshape(SSM_H, SSM_P).sum(1), "ssm_norm_w": G["ssm_norm_w"].reshape(D),
        "norm2_w": G["norm2_w"], "mem_norm_w": G["mem_norm_w"], "norm3_w": G["norm3_w"],
        "final_norm_w": G["final_norm_w"], "loss": loss_part[0, :1],
    }
    names = list(small)
    pack, offs = _pack_rows([small[n].reshape(-1) for n in names])
    tot = _sum8(_all_gather([pack], F32, name="gather_small")[0], name="sum_small")
    summed = dict(zip(names, _unpack_rows(tot, offs, [small[n].shape for n in names])))
    loss = summed.pop("loss")[0]
    for n in ("gdn_conv_w", "ssm_conv_w"):
        width = w_loc[n].shape[1]
        summed[n] = lax.dynamic_slice_in_dim(summed[n], me * width, width, axis=1)
    grads.update(summed)

    upd = {n: _adamw(w_loc[n], grads[n], args["m_" + n], args["v_" + n], name="adamw_" + n) for n in _WEIGHTS}
    return (loss, grad_x[None], *[upd[n][0] for n in _WEIGHTS], *[upd[n][1] for n in _WEIGHTS],
            *[upd[n][2] for n in _WEIGHTS], *[upd[n][3] for n in _WEIGHTS])
```

```python
import functools
import math

import jax
import jax.numpy as jnp
from jax import lax
from jax.experimental import pallas as pl
from jax.experimental.pallas import tpu as pltpu

F32 = jnp.float32
BF16 = jnp.bfloat16
_MXU = BF16

D = 1024
EPS = 1e-6
CONV_K = 4
GDN_H, GDN_DK, GDN_C = 8, 128, 64
GDN_SCAN_CHUNKS = 4
GDN_LOCAL_CHUNKS = 4
GDN_REST_CHUNKS = 4
SSM_H, SSM_P, SSM_L, SSM_N = 16, 64, 128, 128
SSM_SCAN_CHUNKS = 2
MEM_H, MEM_HD = 4, 256
D_FF = 4096
N_DEV = 8

C_QKV, C_ZG, C_ZS, C_XBC, C_GATE, C_DT, C_TOT = 0, 3072, 4096, 5120, 6656, 6784, 6912
P_HALO = 16

ADAM_LR, ADAM_B1, ADAM_B2, ADAM_EPS, ADAM_WD, ADAM_STEP = 0.001, 0.9, 0.999, 1e-08, 0.01, 10

VMEM_LIMIT = 56 * 1024 * 1024

_NN = (((1,), (0,)), ((), ()))
_NT = (((1,), (1,)), ((), ()))
_TN = (((0,), (0,)), ((), ()))


def _dot(a, b, dims=_NN):
    return lax.dot_general(a.astype(_MXU), b.astype(_MXU), dims, preferred_element_type=F32)


def _split3(a):
    a1 = a.astype(BF16)
    r1 = a - a1.astype(F32)
    a2 = r1.astype(BF16)
    return a1, a2, (r1 - a2.astype(F32)).astype(BF16)


def _dot_sel(a, e):
    eb = e.astype(BF16)
    return sum(lax.dot_general(p, eb, _NN, preferred_element_type=F32) for p in _split3(a))


def _sel_dot(e, a):
    eb = e.astype(BF16)
    return sum(lax.dot_general(eb, p, _NN, preferred_element_type=F32) for p in _split3(a))


def _chunk_cumsum(a, tri, chunk):
    return jnp.concatenate([_sel_dot(tri, a[r:r + chunk]) for r in range(0, a.shape[0], chunk)], axis=0)


def _params(sem):
    return pltpu.CompilerParams(dimension_semantics=sem, vmem_limit_bytes=VMEM_LIMIT)


def _pick(n, cap):
    for d in range(min(cap, n), 0, -128):
        if n % d == 0 and d % 128 == 0:
            return d
    return n


def _sigmoid(x):
    return 0.5 * jnp.tanh(0.5 * x) + 0.5


def _silu(x):
    return x * _sigmoid(x)


def _dsilu(x):
    s = _sigmoid(x)
    return s * (1.0 + x * (1.0 - s))


def _softplus(x):
    return jnp.maximum(x, 0.0) + jnp.log(1.0 + jnp.exp(-jnp.abs(x)))


def _iota2(shape, axis):
    return lax.broadcasted_iota(jnp.int32, shape, axis)


def _sum_all(x):
    return jnp.sum(jnp.sum(x, axis=1, keepdims=True), axis=0, keepdims=True)


_MESH = pl.DeviceIdType.MESH
_ANY = pl.BlockSpec(memory_space=pl.ANY)
_VM = pl.BlockSpec(memory_space=pltpu.VMEM)
_REL = [(r >> 2 & 1, r >> 1 & 1, r & 1) for r in range(1, N_DEV)]


def _place():
    return lax.axis_index("x"), lax.axis_index("y"), lax.axis_index("c")


class _Ride:
    def __init__(self, srcs, shard):
        self.srcs, self.shard, self.n = list(srcs), shard, len(srcs)
        self.out_shape = [jax.ShapeDtypeStruct(((N_DEV,) + s.shape) if shard else s.shape, s.dtype)
                          for s in self.srcs]
        self.specs = [_ANY] * self.n
        self.scratch = [pltpu.SemaphoreType.DMA((self.n, N_DEV - 1)), pltpu.SemaphoreType.DMA((self.n, N_DEV - 1)),
                        pltpu.SemaphoreType.DMA((self.n,))]

    def _copies(self, in_refs, out_refs, sems):
        send, recv, loc = sems
        x, y, c = _place()
        me = 4 * x + 2 * y + c
        local, remote, arrive = [], [], []
        for a in range(self.n):
            src = in_refs[a] if self.shard else in_refs[a].at[me]
            local.append(pltpu.make_async_copy(src, out_refs[a].at[me], loc.at[a]))
        for k, (rx, ry, rc) in enumerate(_REL):
            peer = (lax.rem(x + rx, 2), lax.rem(y + ry, 2), lax.rem(c + rc, 2))
            ps = 4 * peer[0] + 2 * peer[1] + peer[2]
            for a in range(self.n):
                src = in_refs[a] if self.shard else in_refs[a].at[ps]
                remote.append(pltpu.make_async_remote_copy(
                    src_ref=src, dst_ref=out_refs[a].at[me], send_sem=send.at[a, k], recv_sem=recv.at[a, k],
                    device_id=peer, device_id_type=_MESH))
                slot = out_refs[a].at[ps]
                arrive.append(pltpu.make_async_remote_copy(
                    src_ref=slot, dst_ref=slot, send_sem=send.at[a, k], recv_sem=recv.at[a, k],
                    device_id=peer, device_id_type=_MESH))
        return local, remote, arrive

    def start(self, in_refs, out_refs, sems):
        local, remote, _ = self._copies(in_refs, out_refs, sems)
        for cp in local + remote:
            cp.start()

    def wait(self, in_refs, out_refs, sems):
        local, remote, arrive = self._copies(in_refs, out_refs, sems)
        for cp in arrive:
            cp.wait_recv()
        for cp in remote:
            cp.wait_send()
        for cp in local:
            cp.wait()


_EPI = {
    "none": ((), ("tile",)),
    "res": (("tile",), ("tile",)),
    "mul2": (("tile",), ("tile",)),
    "relu2": ((), ("tile", "tile")),
    "res_norm": (("tile", "row"), ("tile", "tile")),
    "norm_bwd": (("tile", "tile", "row"), ("tile", "row")),
    "res_loss": (("tile", "tile", "row"), ("tile", "row", "row")),
}


def _mm(a, b, *, dims="nn", epi="none", extra=(), out_dtype=F32, name, bm=1024, bn_cap=1024, bk_cap=2048,
        ride=None, b_cols=None):
    if dims == "nn":
        (M, K), (K2, N) = a.shape, b.shape
    elif dims == "nt":
        (M, K), (N, K2) = a.shape, b.shape
    else:
        (K, M), (K2, N) = a.shape, b.shape
    jb0 = 0
    if b_cols is not None:
        N = b_cols[1]
    assert K == K2, (a.shape, b.shape, dims)
    bm = _pick(M, bm)
    bn = _pick(N, bn_cap)
    bk = _pick(K, bk_cap)
    nk = K // bk
    if b_cols is not None:
        assert dims == "nn" and b_cols[0] % bn == 0
        jb0 = b_cols[0] // bn
    dn = {"nn": _NN, "nt": _NT, "tn": _TN}[dims]
    a_spec = (pl.BlockSpec((bk, bm), lambda i, j, k: (k, i)) if dims == "tn"
              else pl.BlockSpec((bm, bk), lambda i, j, k: (i, k)))
    b_spec = (pl.BlockSpec((bn, bk), lambda i, j, k: (j, k)) if dims == "nt"
              else pl.BlockSpec((bk, bn), lambda i, j, k: (k, j + jb0)))
    o_spec = pl.BlockSpec((bm, bn), lambda i, j, k: (i, j))
    r_spec = pl.BlockSpec((1, bn), lambda i, j, k: (0, j))
    extra = list(extra) if isinstance(extra, (tuple, list)) else [extra]
    ekinds, okinds = _EPI[epi]
    assert len(extra) == len(ekinds) and (epi not in ("res_norm", "norm_bwd", "res_loss") or bn == N)
    n_extra, n_out = len(ekinds), len(okinds)
    n_ride = ride.n if ride else 0
    gi, gj = M // bm, N // bn

    def body(a_ref, b_ref, *rest):
        ex = rest[:n_extra]
        first = pl.program_id(0) == 0
        ride_in = rest[n_extra:n_extra + n_ride]
        outs = rest[n_extra + n_ride:n_extra + n_ride + n_out]
        ride_out = rest[n_extra + n_ride + n_out:n_extra + 2 * n_ride + n_out]
        if ride:
            at = lambda i, j, k: ((pl.program_id(0) == i) & (pl.program_id(1) == j) & (pl.program_id(2) == k))

            @pl.when(at(0, 0, 0))
            def _():
                ride.start(ride_in, ride_out, rest[-3:])

        def finish(r):
            if epi == "res":
                outs[0][...] = (r + ex[0][...].astype(F32)).astype(outs[0].dtype)
            elif epi == "mul2":
                outs[0][...] = (2.0 * r * ex[0][...].astype(F32)).astype(outs[0].dtype)
            elif epi == "relu2":
                u = jnp.maximum(r, 0.0)
                outs[0][...] = u.astype(outs[0].dtype)
                outs[1][...] = (u * u).astype(outs[1].dtype)
            elif epi == "res_norm":
                y = r + ex[0][...]
                outs[0][...] = y
                rstd = lax.rsqrt(jnp.mean(y * y, axis=1, keepdims=True) + EPS)
                outs[1][...] = (y * rstd * ex[1][...]).astype(outs[1].dtype)
            elif epi == "norm_bwd":
                xv = ex[0][...]
                rstd = lax.rsqrt(jnp.mean(xv * xv, axis=1, keepdims=True) + EPS)
                xh = xv * rstd
                dxh = r * ex[2][...]
                outs[0][...] = ex[1][...] + rstd * (dxh - xh * jnp.mean(dxh * xh, axis=1, keepdims=True))
                dw = jnp.sum(r * xh, axis=0, keepdims=True)

                @pl.when(first)
                def _():
                    outs[1][...] = dw

                @pl.when(jnp.logical_not(first))
                def _():
                    outs[1][...] += dw
            elif epi == "res_loss":
                y = r + ex[0][...]
                wv = ex[2][...]
                rstd = lax.rsqrt(jnp.mean(y * y, axis=1, keepdims=True) + EPS)
                yh = y * rstd
                err = yh * wv - ex[1][...]
                part_loss = 0.5 * jnp.sum(jnp.mean(err * err, axis=1, keepdims=True), axis=0, keepdims=True)
                dyn = err * (1.0 / N)
                dyh = dyn * wv
                outs[0][...] = rstd * (dyh - yh * jnp.mean(dyh * yh, axis=1, keepdims=True))
                dw = jnp.sum(dyn * yh, axis=0, keepdims=True)
                lrow = jnp.broadcast_to(part_loss, (1, N))

                @pl.when(first)
                def _():
                    outs[1][...] = dw
                    outs[2][...] = lrow

                @pl.when(jnp.logical_not(first))
                def _():
                    outs[1][...] += dw
                    outs[2][...] += lrow
            else:
                outs[0][...] = r.astype(outs[0].dtype)

        part = _dot(a_ref[...], b_ref[...], dn)
        if nk == 1:
            finish(part)
        else:
            acc = rest[n_extra + 2 * n_ride + n_out]
            k = pl.program_id(2)

            @pl.when(k == 0)
            def _():
                acc[...] = part

            @pl.when((k > 0) & (k < nk - 1))
            def _():
                acc[...] += part

            @pl.when(k == nk - 1)
            def _():
                finish(acc[...] + part)

        if ride:
            @pl.when(at(gi - 1, gj - 1, nk - 1))
            def _():
                ride.wait(ride_in, ride_out, rest[-3:])

    kind_spec = {"tile": o_spec, "row": r_spec}
    ins = [a, b] + [e.reshape(1, N) if k == "row" else e for e, k in zip(extra, ekinds)]
    in_specs = [a_spec, b_spec] + [kind_spec[k] for k in ekinds]
    out_dtypes = {"res_norm": (F32, BF16), "norm_bwd": (F32, F32), "res_loss": (F32, F32, F32)}.get(
        epi, (out_dtype,) * n_out)
    out_shape = [jax.ShapeDtypeStruct((M, N) if k == "tile" else (1, N), dt) for k, dt in zip(okinds, out_dtypes)]
    out_specs = [kind_spec[k] for k in okinds]
    scratch = [pltpu.VMEM((bm, bn), F32)] if nk > 1 else []
    sem = ("arbitrary" if epi in ("norm_bwd", "res_loss") else "parallel", "parallel", "arbitrary")
    if ride:
        ins, in_specs = ins + ride.srcs, in_specs + ride.specs
        out_shape, out_specs = out_shape + ride.out_shape, out_specs + ride.specs
        scratch, sem = scratch + ride.scratch, ("arbitrary",) * 3
    res = pl.pallas_call(
        body, grid=(gi, gj, nk), in_specs=in_specs, out_specs=out_specs, out_shape=out_shape,
        scratch_shapes=scratch, name=name, compiler_params=_params(sem))(*ins)
    main = res[:n_out] if n_out > 1 else res[0]
    return (main, list(res[n_out:])) if ride else main


def _rmsnorm_fwd(x, w, *, name, bt=256):
    T, Dm = x.shape
    bt = min(bt, T)

    def body(x_ref, w_ref, h_ref):
        xv = x_ref[...]
        r = lax.rsqrt(jnp.mean(xv * xv, axis=1, keepdims=True) + EPS)
        h_ref[...] = (xv * r * w_ref[...]).astype(h_ref.dtype)

    return pl.pallas_call(
        body, grid=(T // bt,),
        in_specs=[pl.BlockSpec((bt, Dm), lambda i: (i, 0)), pl.BlockSpec((1, Dm), lambda i: (0, 0))],
        out_specs=pl.BlockSpec((bt, Dm), lambda i: (i, 0)),
        out_shape=jax.ShapeDtypeStruct((T, Dm), BF16), name=name,
        compiler_params=_params(("parallel",)))(x, w.reshape(1, Dm))


def _rmsnorm_bwd(x, w, dh, dres, *, name, bt=256):
    T, Dm = x.shape
    bt = min(bt, T)
    has_res = dres is not None

    def body(x_ref, w_ref, dh_ref, *rest):
        dres_ref = rest[0] if has_res else None
        dx_ref, dw_ref = rest[-2], rest[-1]
        i = pl.program_id(0)
        xv = x_ref[...]
        r = lax.rsqrt(jnp.mean(xv * xv, axis=1, keepdims=True) + EPS)
        xh = xv * r
        dhv = dh_ref[...].astype(F32)
        dxh = dhv * w_ref[...]
        dx = r * (dxh - xh * jnp.mean(dxh * xh, axis=1, keepdims=True))
        if has_res:
            dx = dx + dres_ref[...]
        dx_ref[...] = dx

        @pl.when(i == 0)
        def _():
            dw_ref[...] = jnp.zeros_like(dw_ref)

        dw_ref[...] += jnp.sum(dhv * xh, axis=0, keepdims=True)

    row = pl.BlockSpec((bt, Dm), lambda i: (i, 0))
    vec = pl.BlockSpec((1, Dm), lambda i: (0, 0))
    ins = [x, w.reshape(1, Dm), dh] + ([dres] if has_res else [])
    dx, dw = pl.pallas_call(
        body, grid=(T // bt,), in_specs=[row, vec, row] + ([row] if has_res else []),
        out_specs=[row, vec],
        out_shape=[jax.ShapeDtypeStruct((T, Dm), F32), jax.ShapeDtypeStruct((1, Dm), F32)],
        name=name, compiler_params=_params(("arbitrary",)))(*ins)
    return dx, dw.reshape(Dm)


def _attn_fwd(q, km, vm, *, bt=256):
    T = q.shape[0]
    M = km.shape[0]
    bt = min(bt, T)
    scale = MEM_HD ** -0.5

    def body(q_ref, k_ref, v_ref, o_ref):
        sls = [slice(h * MEM_HD, (h + 1) * MEM_HD) for h in range(MEM_H)]
        ss = [_dot(q_ref[:, sl], k_ref[:, sl], _NT) * scale for sl in sls]
        es = [jnp.exp(s - jnp.max(s, axis=1, keepdims=True)) for s in ss]
        ps = [e / jnp.sum(e, axis=1, keepdims=True) for e in es]
        for sl, p in zip(sls, ps):
            o_ref[:, sl] = _dot(p, v_ref[:, sl]).astype(o_ref.dtype)

    row = pl.BlockSpec((bt, D), lambda i: (i, 0))
    mem = pl.BlockSpec((M, D), lambda i: (0, 0))
    return pl.pallas_call(
        body, grid=(T // bt,), in_specs=[row, mem, mem], out_specs=row,
        out_shape=jax.ShapeDtypeStruct((T, D), BF16), name="attn_fwd",
        compiler_params=_params(("parallel",)))(q, km, vm)


def _attn_bwd(q, km, vm, do, *, bt=256):
    T = q.shape[0]
    M = km.shape[0]
    bt = min(bt, T)
    scale = MEM_HD ** -0.5

    def body(q_ref, k_ref, v_ref, do_ref, dq_ref, dk_ref, dv_ref):
        i = pl.program_id(0)

        @pl.when(i == 0)
        def _():
            dk_ref[...] = jnp.zeros_like(dk_ref)
            dv_ref[...] = jnp.zeros_like(dv_ref)

        sls = [slice(h * MEM_HD, (h + 1) * MEM_HD) for h in range(MEM_H)]
        ss = [_dot(q_ref[:, sl], k_ref[:, sl], _NT) * scale for sl in sls]
        dps = [_dot(do_ref[:, sl], v_ref[:, sl], _NT) for sl in sls]
        es = [jnp.exp(s - jnp.max(s, axis=1, keepdims=True)) for s in ss]
        ps = [e / jnp.sum(e, axis=1, keepdims=True) for e in es]
        dss = [p * (dp - jnp.sum(dp * p, axis=1, keepdims=True)) * scale for p, dp in zip(ps, dps)]
        for sl, p, ds in zip(sls, ps, dss):
            dq_ref[:, sl] = _dot(ds, k_ref[:, sl]).astype(dq_ref.dtype)
            dk_ref[:, sl] += _dot(ds, q_ref[:, sl], _TN)
            dv_ref[:, sl] += _dot(p, do_ref[:, sl], _TN)

    row = pl.BlockSpec((bt, D), lambda i: (i, 0))
    mem = pl.BlockSpec((M, D), lambda i: (0, 0))
    return pl.pallas_call(
        body, grid=(T // bt,), in_specs=[row, mem, mem, row], out_specs=[row, mem, mem],
        out_shape=[jax.ShapeDtypeStruct((T, D), BF16), jax.ShapeDtypeStruct((M, D), F32),
                   jax.ShapeDtypeStruct((M, D), F32)],
        name="attn_bwd", compiler_params=_params(("arbitrary",)))(q, km, vm, do)


def _conv_apply(halo, x, w_ref, b_ref):
    bt, hr = x.shape[0], halo.shape[0]
    cat = jnp.concatenate([halo, x], axis=0)
    y = x * w_ref[3:4, :]
    for k in range(CONV_K - 1):
        y = y + pltpu.roll(cat, CONV_K - 1 - k, 0)[hr:hr + bt] * w_ref[k:k + 1, :]
    if b_ref is not None:
        y = y + b_ref[...]
    return y


def _l2_parts(act, bc):
    out = []
    for s in range(bc // 128):
        a = act[:, s * 128:(s + 1) * 128]
        r = lax.rsqrt(jnp.sum(a * a, axis=1, keepdims=True) + EPS)
        out.append((a, r))
    return out


def _conv_fwd(p, col0, C, w, b, *, l2, name, bt=512, bc=1024):
    T = p.shape[0]
    bt = min(bt, T)
    c0, hb = col0 // bc, bt // P_HALO
    has_b = b is not None
    assert not l2 or (bc == D and C == 2 * D)

    def body(x_ref, halo_ref, w_ref, *rest):
        b_ref = rest[0] if has_b else None
        o_ref = rest[-1]
        i, j = pl.program_id(0), pl.program_id(1)
        x = x_ref[...].astype(F32)
        halo = jnp.where(i > 0, halo_ref[...].astype(F32), 0.0)
        act = _silu(_conv_apply(halo, x, w_ref, b_ref))
        if l2:
            sc = jnp.where(j == 0, GDN_DK ** -0.5, 1.0)
            o_ref[...] = jnp.concatenate([a * (r * sc) for a, r in _l2_parts(act, bc)], axis=1)
        else:
            o_ref[...] = act

    in_specs = [pl.BlockSpec((bt, bc), lambda i, j: (i, c0 + j)),
                pl.BlockSpec((P_HALO, bc), lambda i, j: (jnp.maximum(i * hb - 1, 0), c0 + j)),
                pl.BlockSpec((CONV_K, bc), lambda i, j: (0, j))]
    ins = [p, p, w]
    if has_b:
        in_specs.append(pl.BlockSpec((1, bc), lambda i, j: (0, j)))
        ins.append(b.reshape(1, C))
    return pl.pallas_call(
        body, grid=(T // bt, C // bc), in_specs=in_specs,
        out_specs=pl.BlockSpec((bt, bc), lambda i, j: (i, j)),
        out_shape=jax.ShapeDtypeStruct((T, C), F32), name=name,
        compiler_params=_params(("parallel", "parallel")))(*ins)


def _conv_bwd_act(p, col0, C, w, b, dact, dcol0, *, l2, name, bt=512, bc=1024):
    T = p.shape[0]
    bt = min(bt, T)
    c0, d0, hb = col0 // bc, dcol0 // bc, bt // P_HALO
    has_b = b is not None
    assert not l2 or (bc == D and C == 2 * D)

    def body(x_ref, halo_ref, w_ref, *rest):
        b_ref = rest[0] if has_b else None
        dact_ref, dy_ref, dw_ref, db_ref = rest[-4:]
        j, i = pl.program_id(0), pl.program_id(1)
        x = x_ref[...].astype(F32)
        halo = jnp.where(i > 0, halo_ref[...].astype(F32), 0.0)
        y = _conv_apply(halo, x, w_ref, b_ref)
        dact = dact_ref[...]
        sg = _sigmoid(y)
        if l2:
            sc = jnp.where(j == 0, GDN_DK ** -0.5, 1.0)
            parts = []
            for s, (a, r) in enumerate(_l2_parts(y * sg, bc)):
                n = a * r
                dn = dact[:, s * 128:(s + 1) * 128]
                parts.append((r * sc) * (dn - n * jnp.sum(dn * n, axis=1, keepdims=True)))
            dact = jnp.concatenate(parts, axis=1)
        dy = dact * (sg * (1.0 + y * (1.0 - sg)))
        dy_ref[...] = dy

        @pl.when(i == 0)
        def _():
            dw_ref[...] = jnp.zeros_like(dw_ref)
            db_ref[...] = jnp.zeros_like(db_ref)

        db_ref[...] += jnp.sum(dy, axis=0, keepdims=True)
        cat = jnp.concatenate([halo, x], axis=0)
        dw_ref[3:4, :] += jnp.sum(dy * x, axis=0, keepdims=True)
        for k in range(CONV_K - 1):
            xs = pltpu.roll(cat, CONV_K - 1 - k, 0)[P_HALO:P_HALO + bt]
            dw_ref[k:k + 1, :] += jnp.sum(dy * xs, axis=0, keepdims=True)

    in_specs = [pl.BlockSpec((bt, bc), lambda j, i: (i, c0 + j)),
                pl.BlockSpec((P_HALO, bc), lambda j, i: (jnp.maximum(i * hb - 1, 0), c0 + j)),
                pl.BlockSpec((CONV_K, bc), lambda j, i: (0, j))]
    ins = [p, p, w]
    if has_b:
        in_specs.append(pl.BlockSpec((1, bc), lambda j, i: (0, j)))
        ins.append(b.reshape(1, C))
    in_specs.append(pl.BlockSpec((bt, bc), lambda j, i: (i, d0 + j)))
    ins.append(dact)
    dy, dw, db = pl.pallas_call(
        body, grid=(C // bc, T // bt), in_specs=in_specs,
        out_specs=[pl.BlockSpec((bt, bc), lambda j, i: (i, j)),
                   pl.BlockSpec((CONV_K, bc), lambda j, i: (0, j)),
                   pl.BlockSpec((1, bc), lambda j, i: (0, j))],
        out_shape=[jax.ShapeDtypeStruct((T, C), F32), jax.ShapeDtypeStruct((CONV_K, C), F32),
                   jax.ShapeDtypeStruct((1, C), F32)],
        name=name, compiler_params=_params(("parallel", "arbitrary")))(*ins)
    return dy, dw, db.reshape(C)


def _conv_bwd_in(dy, w, dp_in, col0, T, *, name, bt=512, bc=1024):
    C = dy.shape[1]
    bt = min(bt, T)
    c0, hb, nb = col0 // bc, bt // 8, T // bt

    def body(dy_ref, nxt_ref, w_ref, *rest):
        o_ref = rest[-1]
        i = pl.program_id(0)
        dy_v = dy_ref[...]
        nxt = jnp.where(i < nb - 1, nxt_ref[...], 0.0)
        cat = jnp.concatenate([dy_v, nxt], axis=0)
        dx = dy_v * w_ref[3:4, :]
        for k in range(CONV_K - 1):
            s = CONV_K - 1 - k
            dx = dx + pltpu.roll(cat, bt + 8 - s, 0)[0:bt] * w_ref[k:k + 1, :]
        o_ref[...] = dx.astype(o_ref.dtype)

    in_specs = [pl.BlockSpec((bt, bc), lambda i, j: (i, j)),
                pl.BlockSpec((8, bc), lambda i, j: (jnp.minimum((i + 1) * hb, T // 8 - 1), j)),
                pl.BlockSpec((CONV_K, bc), lambda i, j: (0, j))]
    ins = [dy, dy, w]
    alias = {}
    if dp_in is not None:
        in_specs.append(pl.BlockSpec(memory_space=pl.ANY))
        ins.append(dp_in)
        alias = {3: 0}
    return pl.pallas_call(
        body, grid=(nb, C // bc), in_specs=in_specs,
        out_specs=pl.BlockSpec((bt, bc), lambda i, j: (i, c0 + j)),
        out_shape=jax.ShapeDtypeStruct((T, C_TOT), BF16), input_output_aliases=alias, name=name,
        compiler_params=_params(("parallel", "parallel")))(*ins)


def _expand_mats(shift, row0):
    e = (_iota2((128, D), 0) - row0 == (_iota2((128, D), 1) >> shift)).astype(F32)
    et = ((_iota2((D, 128), 0) >> shift) == _iota2((D, 128), 1) - row0).astype(F32)
    return e, et


def _cum_mats(chunk):
    ri, ci = _iota2((chunk, chunk), 0), _iota2((chunk, chunk), 1)
    return (ri >= ci).astype(F32), (ri <= ci).astype(F32)


def _gdn_gates_fwd(p, alog_row, dtb_row, *, bt=256):
    T = p.shape[0]
    bt = min(bt, T)

    def body(g_ref, al_ref, db_ref, bg_ref):
        gt = g_ref[...]
        lc, _ = _cum_mats(GDN_C)
        g_l = -jnp.exp(al_ref[...]) * _softplus(gt + db_ref[...])
        bg_ref[...] = jnp.where(_iota2((bt, 128), 1) < GDN_H, _sigmoid(gt), _chunk_cumsum(g_l, lc, GDN_C))

    vec = pl.BlockSpec((1, 128), lambda i: (0, 0))
    seg = pl.BlockSpec((bt, 128), lambda i: (i, 0))
    return pl.pallas_call(
        body, grid=(T // bt,), in_specs=[seg, vec, vec], out_specs=seg,
        out_shape=jax.ShapeDtypeStruct((T, 128), F32), name="gdn_gates_fwd",
        compiler_params=_params(("parallel",)))(p, alog_row, dtb_row)


def _gdn_gates_bwd(p, alog_row, dtb_row, dbg, dp_in, *, bt=256):
    T = p.shape[0]
    bt = min(bt, T)

    def body(g_ref, al_ref, db_ref, dbg_ref, dpin_ref, dg_out, dal_ref, ddb_ref):
        i = pl.program_id(0)
        gt = g_ref[...]
        lane = _iota2((bt, 128), 1)
        _, uc = _cum_mats(GDN_C)
        ea = jnp.exp(al_ref[...])
        zz = gt + db_ref[...]
        g_l = -ea * _softplus(zz)
        beta_l = _sigmoid(gt)
        dbg_v = dbg_ref[...]
        dg_l = jnp.where((lane >= GDN_H) & (lane < 2 * GDN_H), _chunk_cumsum(dbg_v, uc, GDN_C), 0.0)
        dbeta_l = jnp.where(lane < GDN_H, dbg_v, 0.0)
        da = dg_l * (-ea) * _sigmoid(zz)
        dg_out[...] = (da + dbeta_l * beta_l * (1.0 - beta_l)).astype(dg_out.dtype)

        @pl.when(i == 0)
        def _():
            dal_ref[...] = jnp.zeros_like(dal_ref)
            ddb_ref[...] = jnp.zeros_like(ddb_ref)

        dal_ref[...] += jnp.sum(dg_l * g_l, axis=0, keepdims=True)
        ddb_ref[...] += jnp.sum(da, axis=0, keepdims=True)

    vec = pl.BlockSpec((1, 128), lambda i: (0, 0))
    seg = pl.BlockSpec((bt, 128), lambda i: (i, 0))
    gate = pl.BlockSpec((bt, 128), lambda i: (i, C_GATE // 128))
    return pl.pallas_call(
        body, grid=(T // bt,), in_specs=[seg, vec, vec, seg, _ANY], out_specs=[gate, vec, vec],
        out_shape=[jax.ShapeDtypeStruct((T, C_TOT), BF16), jax.ShapeDtypeStruct((1, 128), F32),
                   jax.ShapeDtypeStruct((1, 128), F32)],
        input_output_aliases={4: 0}, name="gdn_gates_bwd",
        compiler_params=_params(("arbitrary",)))(p, alog_row, dtb_row, dbg, dp_in)


def _ssd_dt_fwd(p, dtb_row, alog_row, *, bt=256):
    T = p.shape[0]
    bt = min(bt, T)

    def body(d_ref, db_ref, al_ref, da_ref):
        lc, _ = _cum_mats(SSM_L)
        dt_l = _softplus(d_ref[...] + db_ref[...])
        alpha_l = _chunk_cumsum(dt_l * (-jnp.exp(al_ref[...])), lc, SSM_L)
        da_ref[...] = jnp.where(_iota2((bt, 128), 1) < SSM_H, dt_l, pltpu.roll(alpha_l, SSM_H, 1))

    v128 = pl.BlockSpec((1, 128), lambda i: (0, 0))
    return pl.pallas_call(
        body, grid=(T // bt,), in_specs=[pl.BlockSpec((bt, 128), lambda i: (i, 1)), v128, v128],
        out_specs=pl.BlockSpec((bt, 128), lambda i: (i, 0)), out_shape=jax.ShapeDtypeStruct((T, 128), F32),
        name="ssd_dt_fwd", compiler_params=_params(("parallel",)))(p, dtb_row, alog_row)


def _ssd_dt_bwd(p, dtb_row, alog_row, dda, dp_in, *, bt=256):
    T = p.shape[0]
    bt = min(bt, T)

    def body(d_ref, db_ref, al_ref, dda_ref, dpin_ref, dd_out, ddb_ref, dalog_ref):
        i = pl.program_id(0)
        heads = _iota2((bt, 128), 1) < SSM_H
        _, uc = _cum_mats(SSM_L)
        zz = d_ref[...] + db_ref[...]
        dt_l = _softplus(zz)
        a_row = -jnp.exp(al_ref[...])
        dda_v = dda_ref[...]
        da_l = _chunk_cumsum(jnp.where(heads, pltpu.roll(dda_v, 128 - SSM_H, 1), 0.0), uc, SSM_L)
        draw = jnp.where(heads, (dda_v + da_l * a_row) * _sigmoid(zz), 0.0)
        dd_out[...] = draw.astype(dd_out.dtype)

        @pl.when(i == 0)
        def _():
            ddb_ref[...] = jnp.zeros_like(ddb_ref)
            dalog_ref[...] = jnp.zeros_like(dalog_ref)

        ddb_ref[...] += jnp.sum(draw, axis=0, keepdims=True)
        dalog_ref[...] += jnp.sum(da_l * dt_l, axis=0, keepdims=True) * a_row

    seg = pl.BlockSpec((bt, 128), lambda i: (i, C_DT // 128))
    v128 = pl.BlockSpec((1, 128), lambda i: (0, 0))
    return pl.pallas_call(
        body, grid=(T // bt,),
        in_specs=[pl.BlockSpec((bt, 128), lambda i: (i, 1)), v128, v128, pl.BlockSpec((bt, 128), lambda i: (i, 0)), _ANY],
        out_specs=[seg, v128, v128],
        out_shape=[jax.ShapeDtypeStruct((T, C_TOT), BF16), jax.ShapeDtypeStruct((1, 128), F32),
                   jax.ShapeDtypeStruct((1, 128), F32)],
        input_output_aliases={4: 0}, name="ssd_dt_bwd",
        compiler_params=_params(("arbitrary",)))(p, dtb_row, alog_row, dda, dp_in)


def _gdn_post_fwd(o, p, w_x, *, bt=256):
    T = o.shape[0]
    bt = min(bt, T)

    def body(o_ref, z_ref, w_ref, out_ref):
        for h in range(GDN_H):
            sl = slice(h * 128, (h + 1) * 128)
            oh = o_ref[:, sl]
            r = lax.rsqrt(jnp.mean(oh * oh, axis=1, keepdims=True) + EPS)
            out_ref[:, sl] = (oh * r * w_ref[:, sl] * _silu(z_ref[:, sl].astype(F32))).astype(out_ref.dtype)

    row = pl.BlockSpec((bt, D), lambda i: (i, 0))
    return pl.pallas_call(
        body, grid=(T // bt,),
        in_specs=[row, pl.BlockSpec((bt, D), lambda i: (i, C_ZG // D)), pl.BlockSpec((1, D), lambda i: (0, 0))],
        out_specs=row, out_shape=jax.ShapeDtypeStruct((T, 2 * D), BF16), name="gdn_post_fwd",
        compiler_params=_params(("parallel",)))(o, p, w_x)


def _gdn_post_bwd(dmix, o, p, w_x, *, bt=256):
    T = o.shape[0]
    bt = min(bt, T)

    def body(dm_ref, o_ref, z_ref, w_ref, do_ref, dz_ref, dw_ref):
        i = pl.program_id(0)

        @pl.when(i == 0)
        def _():
            dw_ref[...] = jnp.zeros_like(dw_ref)

        for h in range(GDN_H):
            sl = slice(h * 128, (h + 1) * 128)
            oh, zh, wh, dm = o_ref[:, sl], z_ref[:, sl].astype(F32), w_ref[:, sl], dm_ref[:, sl]
            r = lax.rsqrt(jnp.mean(oh * oh, axis=1, keepdims=True) + EPS)
            ohat = oh * r
            dy = dm * _silu(zh)
            dz_ref[:, sl] = (dm * ohat * wh * _dsilu(zh)).astype(dz_ref.dtype)
            dohat = dy * wh
            do_ref[:, sl] = (r * (dohat - ohat * jnp.mean(dohat * ohat, axis=1, keepdims=True))).astype(do_ref.dtype)
            dw_ref[:, sl] += jnp.sum(dy * ohat, axis=0, keepdims=True)

    row = pl.BlockSpec((bt, D), lambda i: (i, 0))
    zcol = pl.BlockSpec((bt, D), lambda i: (i, C_ZG // D))
    vec = pl.BlockSpec((1, D), lambda i: (0, 0))
    return pl.pallas_call(
        body, grid=(T // bt,), in_specs=[row, row, zcol, vec], out_specs=[row, zcol, vec],
        out_shape=[jax.ShapeDtypeStruct((T, D), BF16), jax.ShapeDtypeStruct((T, C_TOT), BF16),
                   jax.ShapeDtypeStruct((1, D), F32)],
        name="gdn_post_bwd", compiler_params=_params(("arbitrary",)))(dmix, o, p, w_x)


def _ssd_post_fwd(y, xs, p, d_x, w, mix_in, *, bt=256):
    T = y.shape[0]
    bt = min(bt, T)

    def body(y_ref, x_ref, z_ref, d_ref, w_ref, mix_ref, out_ref):
        yg = (y_ref[...] + x_ref[...] * d_ref[...]) * _silu(z_ref[...].astype(F32))
        for g in range(2):
            sl = slice(g * 512, (g + 1) * 512)
            a = yg[:, sl]
            r = lax.rsqrt(jnp.mean(a * a, axis=1, keepdims=True) + EPS)
            out_ref[:, sl] = (a * r * w_ref[:, sl]).astype(out_ref.dtype)

    row = pl.BlockSpec((bt, D), lambda i: (i, 0))
    vec = pl.BlockSpec((1, D), lambda i: (0, 0))
    return pl.pallas_call(
        body, grid=(T // bt,),
        in_specs=[row, row, pl.BlockSpec((bt, D), lambda i: (i, C_ZS // D)), vec, vec, _ANY],
        out_specs=pl.BlockSpec((bt, D), lambda i: (i, 1)), out_shape=jax.ShapeDtypeStruct((T, 2 * D), BF16),
        input_output_aliases={5: 0}, name="ssd_post_fwd",
        compiler_params=_params(("parallel",)))(y, xs, p, d_x, w, mix_in)


def _ssd_post_bwd(dmix, y, xs, p, d_x, w, dp_in, *, bt=256):
    T = y.shape[0]
    bt = min(bt, T)

    def body(dm_ref, y_ref, x_ref, z_ref, d_ref, w_ref, dpin_ref, dyy_ref, dz_ref, dd_ref, dw_ref):
        i = pl.program_id(0)

        @pl.when(i == 0)
        def _():
            dd_ref[...] = jnp.zeros_like(dd_ref)
            dw_ref[...] = jnp.zeros_like(dw_ref)

        xv, zv = x_ref[...], z_ref[...].astype(F32)
        yy = y_ref[...] + xv * d_ref[...]
        sz = _silu(zv)
        yg = yy * sz
        parts = []
        for g in range(2):
            sl = slice(g * 512, (g + 1) * 512)
            a = yg[:, sl]
            r = lax.rsqrt(jnp.mean(a * a, axis=1, keepdims=True) + EPS)
            ah = a * r
            dout = dm_ref[:, sl]
            dah = dout * w_ref[:, sl]
            dw_ref[:, sl] += jnp.sum(dout * ah, axis=0, keepdims=True)
            parts.append(r * (dah - ah * jnp.mean(dah * ah, axis=1, keepdims=True)))
        dyg = jnp.concatenate(parts, axis=1)
        dyy = dyg * sz
        dyy_ref[...] = dyy
        dz_ref[...] = (dyg * yy * _dsilu(zv)).astype(dz_ref.dtype)
        dd_ref[...] += jnp.sum(dyy * xv, axis=0, keepdims=True)

    row = pl.BlockSpec((bt, D), lambda i: (i, 0))
    zcol = pl.BlockSpec((bt, D), lambda i: (i, C_ZS // D))
    vec = pl.BlockSpec((1, D), lambda i: (0, 0))
    return pl.pallas_call(
        body, grid=(T // bt,),
        in_specs=[pl.BlockSpec((bt, D), lambda i: (i, 1)), row, row, zcol, vec, vec, _ANY],
        out_specs=[row, zcol, vec, vec],
        out_shape=[jax.ShapeDtypeStruct((T, D), F32), jax.ShapeDtypeStruct((T, C_TOT), BF16),
                   jax.ShapeDtypeStruct((1, D), F32), jax.ShapeDtypeStruct((1, D), F32)],
        input_output_aliases={6: 1}, name="ssd_post_bwd",
        compiler_params=_params(("arbitrary",)))(dmix, y, xs, p, d_x, w, dp_in)


_NEG = -1e30


def _gdn_terms(q, k, v, bx, gam_c):
    C = GDN_C
    ri, ci = _iota2((C, C), 0), _iota2((C, C), 1)
    eye, low, strict = ri == ci, ri >= ci, ri > ci
    gam_r = jnp.sum(jnp.where(eye, gam_c, 0.0), axis=0, keepdims=True)
    G = jnp.exp(jnp.where(low, gam_c - gam_r, _NEG))
    glast = jnp.sum(jnp.where(_iota2((C, 1), 0) == C - 1, gam_c, 0.0), axis=0, keepdims=True)
    eg, egl, eL = jnp.exp(gam_c), jnp.exp(glast - gam_c), jnp.exp(glast)
    kb, vb = k * bx, v * bx
    M = _dot(kb, k, _NT)
    return dict(eye=eye, low=low, strict=strict, G=G, eg=eg, egl=egl, eL=eL, kb=kb, vb=vb, M=M,
                kbg=kb * eg, qd=q * eg, kd=k * egl, q=q, k=k, v=v, bx=bx)


def _split(a):
    hi = a.astype(_MXU)
    return hi, (a - hi.astype(F32)).astype(_MXU)


def _dot3s(a, b):
    d = lambda p, q: lax.dot_general(p, q, _NN, preferred_element_type=F32)
    return d(a[0], b[0]) + d(a[0], b[1]) + d(a[1], b[0])


def _tri_inv_many(Ls, eye):
    eyef = jnp.where(eye, 1.0, 0.0)
    Ts = [eyef - L for L in Ls]
    Ps = [-L for L in Ls]
    for _ in range(5):
        sp = [_split(p) for p in Ps]
        Ps = [_dot3s(s, s) for s in sp]
        sp = [_split(p) for p in Ps]
        st = [_split(t) for t in Ts]
        Ts = [t + _dot3s(a, b) for t, a, b in zip(Ts, st, sp)]
    return Ts


def _lane_col(tile, idx):
    return jnp.sum(jnp.where(_iota2(tile.shape, 1) == idx, tile, 0.0), axis=1, keepdims=True)


def _gdn_heads(q_ref, k_ref, v_ref, bg_ref, heads):
    out = []
    bg = bg_ref[...]
    for h in heads:
        sl = slice(h * 128, (h + 1) * 128)
        out.append(_gdn_terms(q_ref[:, sl], k_ref[:, sl], v_ref[:, sl], _lane_col(bg, h), _lane_col(bg, GDN_H + h)))
    return out


def _gdn_prep(qk, v, bg, ride=None):
    T = qk.shape[0]
    N = T // GDN_C
    C, CS = GDN_C, GDN_LOCAL_CHUNKS
    NB = N // CS
    n_ride = ride.n if ride else 0

    def body(q_ref, k_ref, v_ref, bg_ref, *rest):
        ride_in = rest[:n_ride]
        u_ref, w_ref, qd_ref, kd_ref, p_ref, t_ref = rest[n_ride:n_ride + 6]
        ride_out = rest[n_ride + 6:2 * n_ride + 6]
        if ride:
            @pl.when(pl.program_id(0) == 0)
            def _():
                ride.start(ride_in, ride_out, rest[-3:])

            @pl.when(pl.program_id(0) == NB - 1)
            def _():
                ride.wait(ride_in, ride_out, rest[-3:])

        items = [(c, h) for c in range(CS) for h in range(GDN_H)]
        views = [[r.at[pl.ds(c * C, C)] for r in (q_ref, k_ref, v_ref, bg_ref)] for c in range(CS)]
        ts = [_gdn_heads(*views[c], [h])[0] for c, h in items]
        Ts = _tri_inv_many([jnp.where(t["strict"], t["M"] * t["G"], 0.0) for t in ts], ts[0]["eye"])
        for (c, h), t, Tm in zip(items, ts, Ts):
            tok = slice(c * C, (c + 1) * C)
            sl = slice(h * 128, (h + 1) * 128)
            rows = slice(h * C, (h + 1) * C)
            u_ref[tok, sl] = _dot(Tm, t["vb"])
            w_ref[tok, sl] = _dot(Tm, t["kbg"]).astype(w_ref.dtype)
            qd_ref[tok, sl] = t["qd"].astype(qd_ref.dtype)
            kd_ref[tok, sl] = t["kd"].astype(kd_ref.dtype)
            p_ref[c, rows, :] = _dot(t["q"], t["k"], _NT) * t["G"]
            t_ref[c, rows, :] = Tm

    blk = lambda c: pl.BlockSpec((CS * C, D), lambda n: (n, c))
    sq = pl.BlockSpec((CS, GDN_H * C, C), lambda n: (n, 0, 0))
    in_specs = [blk(0), blk(1), blk(0), pl.BlockSpec((CS * C, 128), lambda n: (n, 0))]
    out_specs = [blk(0), blk(0), blk(0), blk(0), sq, sq]
    out_shape = [jax.ShapeDtypeStruct((T, D), F32), jax.ShapeDtypeStruct((T, D), BF16),
                 jax.ShapeDtypeStruct((T, D), BF16), jax.ShapeDtypeStruct((T, D), BF16),
                 jax.ShapeDtypeStruct((N, GDN_H * C, C), F32), jax.ShapeDtypeStruct((N, GDN_H * C, C), F32)]
    ins = [qk, qk, v, bg]
    if ride:
        ins, in_specs = ins + ride.srcs, in_specs + ride.specs
        out_shape, out_specs = out_shape + ride.out_shape, out_specs + ride.specs
    res = pl.pallas_call(
        body, grid=(NB,), in_specs=in_specs, out_specs=out_specs, out_shape=out_shape,
        scratch_shapes=ride.scratch if ride else [], name="gdn_prep",
        compiler_params=_params(("arbitrary",) if ride else ("parallel",)))(*ins)
    return (list(res[:6]), list(res[6:])) if ride else list(res)


def _gdn_scan_fwd(u, w, qd, kd, pm, bg):
    T = u.shape[0]
    N = T // GDN_C
    C, CS = GDN_C, GDN_SCAN_CHUNKS

    def body(u_ref, w_ref, qd_ref, kd_ref, p_ref, bg_ref, o_ref, vn_ref, ss_ref, S_scr):
        n = pl.program_id(0)

        @pl.when(n == 0)
        def _():
            S_scr[...] = jnp.zeros_like(S_scr)

        sls = [slice(h * 128, (h + 1) * 128) for h in range(GDN_H)]
        for c in range(CS):
            rows = slice(c * C, (c + 1) * C)
            glast = bg_ref[(c + 1) * C - 1:(c + 1) * C, :]
            Ss = [S_scr[:, sl] for sl in sls]
            vns = [u_ref[rows, sl] - _dot(w_ref[rows, sl], S) for sl, S in zip(sls, Ss)]
            for h, (sl, S, vn) in enumerate(zip(sls, Ss, vns)):
                ss_ref[c, :, sl] = S.astype(ss_ref.dtype)
                vn_ref[rows, sl] = vn.astype(vn_ref.dtype)
                o_ref[rows, sl] = _dot(qd_ref[rows, sl], S) + _dot(p_ref[c, h * C:(h + 1) * C, :], vn)
                S_scr[:, sl] = S * jnp.exp(_lane_col(glast, GDN_H + h)) + _dot(kd_ref[rows, sl], vn, _TN)

    blk = pl.BlockSpec((CS * C, D), lambda n: (n, 0))
    return dict(
        body=body, steps=N // CS, ins=[u, w, qd, kd, pm, bg],
        in_specs=[blk, blk, blk, blk, pl.BlockSpec((CS, GDN_H * C, C), lambda n: (n, 0, 0)),
                  pl.BlockSpec((CS * C, 128), lambda n: (n, 0))],
        out_specs=[blk, blk, pl.BlockSpec((CS, GDN_DK, D), lambda n: (n, 0, 0))],
        out_shape=[jax.ShapeDtypeStruct((T, D), F32), jax.ShapeDtypeStruct((T, D), BF16),
                   jax.ShapeDtypeStruct((N, GDN_DK, D), BF16)],
        scratch=[pltpu.VMEM((GDN_DK, D), F32)])


def _gdn_scan_bwd(w, qd, kd, pm, bg, do):
    T = w.shape[0]
    N = T // GDN_C
    C, CS = GDN_C, GDN_SCAN_CHUNKS
    NB = N // CS

    def body(w_ref, qd_ref, kd_ref, p_ref, bg_ref, do_ref, dvn_ref, ds_ref, dS_scr):
        n = pl.program_id(0)

        @pl.when(n == 0)
        def _():
            dS_scr[...] = jnp.zeros_like(dS_scr)

        sls = [slice(h * 128, (h + 1) * 128) for h in range(GDN_H)]
        for c in reversed(range(CS)):
            rows = slice(c * C, (c + 1) * C)
            glast = bg_ref[(c + 1) * C - 1:(c + 1) * C, :]
            dSs = [dS_scr[:, sl] for sl in sls]
            dvns = [_dot(p_ref[c, h * C:(h + 1) * C, :], do_ref[rows, sl], _TN) + _dot(kd_ref[rows, sl], dS2)
                    for h, (sl, dS2) in enumerate(zip(sls, dSs))]
            for h, (sl, dS2, dvn) in enumerate(zip(sls, dSs, dvns)):
                ds_ref[c, :, sl] = dS2.astype(ds_ref.dtype)
                dvn_ref[rows, sl] = dvn.astype(dvn_ref.dtype)
                dS_scr[:, sl] = (dS2 * jnp.exp(_lane_col(glast, GDN_H + h))
                                 + _dot(qd_ref[rows, sl], do_ref[rows, sl], _TN) - _dot(w_ref[rows, sl], dvn, _TN))

    blk = pl.BlockSpec((CS * C, D), lambda n: (NB - 1 - n, 0))
    return dict(
        body=body, steps=NB, ins=[w, qd, kd, pm, bg, do],
        in_specs=[blk, blk, blk, pl.BlockSpec((CS, GDN_H * C, C), lambda n: (NB - 1 - n, 0, 0)),
                  pl.BlockSpec((CS * C, 128), lambda n: (NB - 1 - n, 0)), blk],
        out_specs=[blk, pl.BlockSpec((CS, GDN_DK, D), lambda n: (NB - 1 - n, 0, 0))],
        out_shape=[jax.ShapeDtypeStruct((T, D), BF16), jax.ShapeDtypeStruct((N, GDN_DK, D), BF16)],
        scratch=[pltpu.VMEM((GDN_DK, D), F32)])


def _gdn_rest_bwd(qk, v, bg, s_save, t_save, vn, dvn, ds_save, do, ride=None):
    T = qk.shape[0]
    N = T // GDN_C
    C, CS = GDN_C, GDN_REST_CHUNKS
    NB = N // CS
    n_ride = ride.n if ride else 0

    def body(q_ref, k_ref, v_ref, bg_ref, ss_ref, ts_ref, vn_ref, dvn_ref, ds_ref, do_ref, *rest):
        ride_in = rest[:n_ride]
        dqkv_ref, dbg_ref = rest[n_ride:n_ride + 2]
        ride_out = rest[n_ride + 2:2 * n_ride + 2]
        if ride:
            @pl.when(pl.program_id(0) == 0)
            def _():
                ride.start(ride_in, ride_out, rest[-3:])

            @pl.when(pl.program_id(0) == NB - 1)
            def _():
                ride.wait(ride_in, ride_out, rest[-3:])

        items = [(c, h) for c in range(CS) for h in range(GDN_H)]
        toks = [slice(c * C, (c + 1) * C) for c, _ in items]
        sls = [slice(h * 128, (h + 1) * 128) for _, h in items]
        views = [[r.at[pl.ds(c * C, C)] for r in (q_ref, k_ref, v_ref, bg_ref)] for c in range(CS)]
        ts = [_gdn_heads(*views[c], [h])[0] for c, h in items]
        Ss = [ss_ref[c, :, sl] for (c, _), sl in zip(items, sls)]
        Tms = [ts_ref[c, h * C:(h + 1) * C, :] for c, h in items]
        dS2s = [ds_ref[c, :, sl] for (c, _), sl in zip(items, sls)]
        dos = [do_ref[tok, sl] for tok, sl in zip(toks, sls)]
        vns = [vn_ref[tok, sl] for tok, sl in zip(toks, sls)]
        dvns = [dvn_ref[tok, sl] for tok, sl in zip(toks, sls)]
        Qs = [_dot(t["q"], t["k"], _NT) for t in ts]
        dws = [-_dot(dvn, S, _NT) for dvn, S in zip(dvns, Ss)]
        dqds = [_dot(do, S, _NT) for do, S in zip(dos, Ss)]
        dPs = [jnp.where(t["low"], _dot(do, vn, _NT), 0.0) for t, do, vn in zip(ts, dos, vns)]
        dkds = [_dot(vn, dS2, _NT) for vn, dS2 in zip(vns, dS2s)]
        dTs = [_dot(dvn, t["vb"], _NT) + _dot(dw, t["kbg"], _NT) for t, dvn, dw in zip(ts, dvns, dws)]
        dvbs = [_dot(Tm, dvn, _TN) for Tm, dvn in zip(Tms, dvns)]
        dkbgs = [_dot(Tm, dw, _TN) for Tm, dw in zip(Tms, dws)]
        TdTs = [_dot(Tm, dT, _TN) for Tm, dT in zip(Tms, dTs)]
        dLs = [jnp.where(t["strict"], -_dot(TdT, Tm, _NT), 0.0) for t, TdT, Tm in zip(ts, TdTs, Tms)]
        dMs = [dL * t["G"] for t, dL in zip(ts, dLs)]
        dQs = [dP * t["G"] for t, dP in zip(ts, dPs)]
        dkbs = [_dot(dM, t["k"]) + dkbg * t["eg"] for t, dM, dkbg in zip(ts, dMs, dkbgs)]
        rs = lambda a: jnp.sum(a, axis=1, keepdims=True)
        lane = _iota2((C, 128), 1)
        last = _iota2((C, 1), 0) == C - 1
        dbg = [jnp.zeros((C, 128), F32) for _ in range(CS)]
        for i, (c, h) in enumerate(items):
            t, sl, tok = ts[i], sls[i], toks[i]
            E = (dLs[i] * t["M"] + dPs[i] * Qs[i]) * t["G"]
            dqkv_ref[tok, sl] = _dot(dQs[i], t["k"]) + dqds[i] * t["eg"]
            dqkv_ref[tok, D + h * 128:D + (h + 1) * 128] = (
                _dot(dQs[i], t["q"], _TN) + _dot(dMs[i], t["kb"], _TN) + dkds[i] * t["egl"] + dkbs[i] * t["bx"])
            dqkv_ref[tok, 2 * D + h * 128:2 * D + (h + 1) * 128] = dvbs[i] * t["bx"]
            dbeta_c = rs(dkbs[i] * t["k"] + dvbs[i] * t["v"])
            dkd_kd = dkds[i] * t["kd"]
            dgam_c = rs(dqds[i] * t["qd"]) + rs(dkbgs[i] * t["kbg"]) - rs(dkd_kd) + rs(E)
            dgam_r = -jnp.sum(E, axis=0, keepdims=True)
            dgam_c = dgam_c + jnp.sum(jnp.where(t["eye"], dgam_r, 0.0), axis=1, keepdims=True)
            dlast = _sum_all(dkd_kd) + t["eL"] * _sum_all(Ss[i].astype(F32) * dS2s[i].astype(F32))
            dgam_c = dgam_c + jnp.where(last, dlast, 0.0)
            dbg[c] = dbg[c] + jnp.where(lane == h, dbeta_c, 0.0) + jnp.where(lane == GDN_H + h, dgam_c, 0.0)
        for c in range(CS):
            dbg_ref[c * C:(c + 1) * C, :] = dbg[c]

    blk = lambda c: pl.BlockSpec((CS * C, D), lambda n: (n, c))
    st = pl.BlockSpec((CS, GDN_DK, D), lambda n: (n, 0, 0))
    seg = pl.BlockSpec((CS * C, 128), lambda n: (n, 0))
    in_specs = [blk(0), blk(1), blk(0), seg, st,
                pl.BlockSpec((CS, GDN_H * C, C), lambda n: (n, 0, 0)), blk(0), blk(0), st, blk(0)]
    out_specs = [pl.BlockSpec((CS * C, 3 * D), lambda n: (n, 0)), seg]
    out_shape = [jax.ShapeDtypeStruct((T, 3 * D), F32), jax.ShapeDtypeStruct((T, 128), F32)]
    ins = [qk, qk, v, bg, s_save, t_save, vn, dvn, ds_save, do]
    if ride:
        ins, in_specs = ins + ride.srcs, in_specs + ride.specs
        out_shape, out_specs = out_shape + ride.out_shape, out_specs + ride.specs
    res = pl.pallas_call(
        body, grid=(NB,), in_specs=in_specs, out_specs=out_specs, out_shape=out_shape,
        scratch_shapes=ride.scratch if ride else [], name="gdn_rest_bwd",
        compiler_params=_params(("arbitrary",) if ride else ("parallel",)))(*ins)
    return (list(res[:2]), list(res[2:])) if ride else list(res)


def _ssd_seg(al_pair, half, s):
    L = SSM_L
    ri, ci = _iota2((L, L), 0), _iota2((L, L), 1)
    ac = jnp.max(jnp.where(half == s, al_pair, _NEG), axis=1, keepdims=True)
    ar = jnp.sum(jnp.where(ri == ci, ac, 0.0), axis=0, keepdims=True)
    return jnp.exp(jnp.where(ri >= ci, ac - ar, _NEG))


def _last_row(a):
    return jnp.sum(jnp.where(_iota2((a.shape[0], 1), 0) == a.shape[0] - 1, a, 0.0), axis=0, keepdims=True)


def _ssd_expand(da_ref):
    da = da_ref[...]
    return _dot_sel(da, _expand_mats(6, 0)[0]), _dot_sel(da, _expand_mats(6, SSM_H)[0])


def _ssd_core_fwd(xbc, da):
    T = xbc.shape[0]
    L, CS = SSM_L, SSM_SCAN_CHUNKS
    Nc = T // L

    def body(x_all, bc_all, da_all, y_all, hs_all, H_scr):
        @pl.when(pl.program_id(0) == 0)
        def _():
            H_scr[...] = jnp.zeros_like(H_scr)

        for cc in range(CS):
            rows = pl.ds(cc * L, L)
            chunk(x_all.at[rows], bc_all.at[rows], da_all.at[rows], y_all.at[rows], hs_all.at[cc], H_scr)

    def chunk(x_ref, bc_ref, da_ref, y_ref, hs_ref, H_scr):
        dt_ref, al_ref = _ssd_expand(da_ref)
        half = _iota2((L, 128), 1) >> 6
        for g in range(2):
            gs = slice(g * 512, (g + 1) * 512)
            Bg = bc_ref[:, g * 128:(g + 1) * 128]
            Cg = bc_ref[:, 256 + g * 128:256 + (g + 1) * 128]
            alg = al_ref[:, gs]
            alast = _last_row(alg)
            xdt = x_ref[:, gs] * dt_ref[:, gs]
            Hg = H_scr[:, gs]
            hs_ref[:, gs] = Hg
            CB = _dot(Cg, Bg, _NT)
            y_ref[:, gs] = jnp.exp(alg) * _dot(Cg, Hg)
            H_scr[:, gs] = Hg * jnp.exp(alast) + _dot(Bg, jnp.exp(alast - alg) * xdt, _TN)
            for j in range(4):
                ps = slice(g * 512 + j * 128, g * 512 + (j + 1) * 128)
                al_pair = al_ref[:, ps]
                xp = x_ref[:, ps] * dt_ref[:, ps]
                ys = [_dot(_ssd_seg(al_pair, half, s) * CB, xp) for s in range(2)]
                y_ref[:, ps] += jnp.where(half == 0, ys[0], ys[1])

    row = pl.BlockSpec((CS * L, D), lambda c: (c, 0))
    return dict(
        body=body, steps=Nc // CS, ins=[xbc, xbc, da],
        in_specs=[row, pl.BlockSpec((CS * L, 512), lambda c: (c, 2)), pl.BlockSpec((CS * L, 128), lambda c: (c, 0))],
        out_specs=[row, pl.BlockSpec((CS, SSM_N, D), lambda c: (c, 0, 0))],
        out_shape=[jax.ShapeDtypeStruct((T, D), F32), jax.ShapeDtypeStruct((Nc, SSM_N, D), F32)],
        scratch=[pltpu.VMEM((SSM_N, D), F32)])


def _ssd_core_bwd(xbc, da, h_save, dyy, d_x):
    T = xbc.shape[0]
    L, CS = SSM_L, SSM_SCAN_CHUNKS
    Nc = T // L
    NB = Nc // CS

    def body(x_all, bc_all, da_all, hs_all, dy_all, d_ref, dx_all, dda_all, dH_scr, ddt_ref, dal_ref):
        @pl.when(pl.program_id(0) == 0)
        def _():
            dH_scr[...] = jnp.zeros_like(dH_scr)

        for cc in reversed(range(CS)):
            rows = pl.ds(cc * L, L)
            chunk(x_all.at[rows], bc_all.at[rows], da_all.at[rows], hs_all.at[cc], dy_all.at[rows],
                  d_ref, dx_all.at[rows], ddt_ref, dal_ref, dH_scr)
            dda_all[rows, :] = (_dot_sel(ddt_ref[...], _expand_mats(6, 0)[1])
                                + _dot_sel(dal_ref[...], _expand_mats(6, SSM_H)[1]))

    def chunk(x_ref, bc_ref, da_ref, hs_ref, dy_ref, d_ref, dx_ref, ddt_ref, dal_ref, dH_scr):
        dt_ref, al_ref = _ssd_expand(da_ref)
        lane = _iota2((L, 128), 1)
        half = lane >> 6
        rowi = _iota2((L, 1), 0)
        ri, ci = _iota2((L, L), 0), _iota2((L, L), 1)
        for g in range(2):
            gs = slice(g * 512, (g + 1) * 512)
            Bg = bc_ref[:, g * 128:(g + 1) * 128]
            Cg = bc_ref[:, 256 + g * 128:256 + (g + 1) * 128]
            alg = al_ref[:, gs]
            alast = _last_row(alg)
            eal, edec, eL = jnp.exp(alg), jnp.exp(alast - alg), jnp.exp(alast)
            xg, dtg, dYg = x_ref[:, gs], dt_ref[:, gs], dy_ref[:, gs]
            xdt = xg * dtg
            Hg = hs_ref[:, gs]
            dH2 = dH_scr[:, gs]
            CB = _dot(Cg, Bg, _NT)
            dYe = eal * dYg
            dH_scr[:, gs] = dH2 * eL + _dot(Cg, dYe, _TN)
            dC = _dot(dYe, Hg, _NT)
            zg = edec * xdt
            dz = _dot(Bg, dH2)
            dB = _dot(zg, dH2, _NT)
            tz = dz * zg
            dal = dYe * _dot(Cg, Hg) - tz
            dalast = jnp.sum(tz, axis=0, keepdims=True) + eL * jnp.sum(Hg * dH2, axis=0, keepdims=True)
            dal = dal + jnp.where(rowi == L - 1, dalast, 0.0)
            dxdt_g = edec * dz
            dx_ref[:, gs] = dxdt_g * dtg + dYg * d_ref[:, gs]
            ddt_ref[:, gs] = dxdt_g * xg
            dal_ref[:, gs] = dal
            dCB = jnp.zeros((L, L), F32)
            for j in range(4):
                ps = slice(g * 512 + j * 128, g * 512 + (j + 1) * 128)
                al_pair = al_ref[:, ps]
                xp = x_ref[:, ps] * dt_ref[:, ps]
                dYp = dy_ref[:, ps]
                dxp = []
                dal_p = jnp.zeros((L, 128), F32)
                for s in range(2):
                    seg = _ssd_seg(al_pair, half, s)
                    W = seg * CB
                    dW = _dot(jnp.where(half == s, dYp, 0.0), xp, _NT)
                    dxp.append(_dot(W, dYp, _TN))
                    dCB = dCB + dW * seg
                    Es = dW * W
                    dac = jnp.sum(Es, axis=1, keepdims=True) - jnp.sum(
                        jnp.where(ri == ci, jnp.sum(Es, axis=0, keepdims=True), 0.0), axis=1, keepdims=True)
                    dal_p = dal_p + jnp.where(lane == 64 * s, dac, 0.0)
                dxdt_p = jnp.where(half == 0, dxp[0], dxp[1])
                dx_ref[:, ps] += dxdt_p * dt_ref[:, ps]
                ddt_ref[:, ps] += dxdt_p * x_ref[:, ps]
                dal_ref[:, ps] += dal_p
            dx_ref[:, D + g * 128:D + (g + 1) * 128] = dB + _dot(dCB, Cg, _TN)
            dx_ref[:, D + 256 + g * 128:D + 256 + (g + 1) * 128] = dC + _dot(dCB, Bg)

    row = pl.BlockSpec((CS * L, D), lambda c: (NB - 1 - c, 0))
    bcs = pl.BlockSpec((CS * L, 512), lambda c: (NB - 1 - c, 2))
    seg = pl.BlockSpec((CS * L, 128), lambda c: (NB - 1 - c, 0))
    return dict(
        body=body, steps=NB, ins=[xbc, xbc, da, h_save, dyy, d_x],
        in_specs=[row, bcs, seg, pl.BlockSpec((CS, SSM_N, D), lambda c: (NB - 1 - c, 0, 0)), row,
                  pl.BlockSpec((1, D), lambda c: (0, 0))],
        out_specs=[pl.BlockSpec((CS * L, D + 512), lambda c: (NB - 1 - c, 0)), seg],
        out_shape=[jax.ShapeDtypeStruct((T, D + 512), F32), jax.ShapeDtypeStruct((T, 128), F32)],
        scratch=[pltpu.VMEM((SSM_N, D), F32), pltpu.VMEM((L, D), F32), pltpu.VMEM((L, D), F32)])


def _run_scans(parts, *, name):
    steps = parts[0]["steps"]
    assert all(p["steps"] == steps for p in parts)
    cnt = lambda key: [len(p[key]) for p in parts]
    n_in, n_out, n_scr = cnt("ins"), cnt("out_shape"), cnt("scratch")

    def body(*refs):
        ins, outs, scr = refs[:sum(n_in)], refs[sum(n_in):sum(n_in) + sum(n_out)], refs[sum(n_in) + sum(n_out):]
        oi = oo = os_ = 0
        for p, a, b, c in zip(parts, n_in, n_out, n_scr):
            p["body"](*ins[oi:oi + a], *outs[oo:oo + b], *scr[os_:os_ + c])
            oi, oo, os_ = oi + a, oo + b, os_ + c

    cat = lambda key: [v for p in parts for v in p[key]]
    res = pl.pallas_call(
        body, grid=(steps,), in_specs=cat("in_specs"), out_specs=cat("out_specs"), out_shape=cat("out_shape"),
        scratch_shapes=cat("scratch"), name=name, compiler_params=_params(("arbitrary",)))(*cat("ins"))
    out, o = [], 0
    for b in n_out:
        out.append(list(res[o:o + b]))
        o += b
    return out


_EARLY = ("w_out", "wq_mem", "wk_mem", "wv_mem", "wo_mem")
_LATE = ("w_up", "w_down")
_GRADS_MLP = ("w_down", "w_up")
_GRADS_MID = ("wo_mem", "wq_mem", "wk_mem", "wv_mem", "w_out")


def _gather_ride(shards, names):
    return None if shards is None else _Ride([shards[n] for n in names], shard=True)


def _grad_ride(shards, G, names):
    return None if shards is None else _Ride([_slots_from_full(n, G[n]) for n in names], shard=False)


def _local_step(x, mem, tgt, W, shards=None):
    T = x.shape[0]
    W = dict(W)
    cw_qk, cw_v = W["gdn_conv_w"][:, :2 * D], W["gdn_conv_w"][:, 2 * D:]
    h1 = _rmsnorm_fwd(x, W["norm1_w"], name="norm1_fwd")
    ride = _gather_ride(shards, _EARLY)
    pg = _mm(h1, W["w_in_pad"], b_cols=(C_GATE, C_TOT - C_GATE), name="in_proj_gates")
    p = _mm(h1, W["w_in_pad"], b_cols=(0, C_GATE), out_dtype=BF16, bn_cap=1664, name="in_proj", ride=ride)
    if ride:
        p, got = p
        W.update({n: _full_from_slots(n, g) for n, g in zip(_EARLY, got)})
    qk = _conv_fwd(p, C_QKV, 2 * D, cw_qk, None, l2=True, name="gdn_conv_qk_fwd")
    v_g = _conv_fwd(p, C_QKV + 2 * D, D, cw_v, None, l2=False, name="gdn_conv_v_fwd")
    bg = _gdn_gates_fwd(pg, W["gdn_alog_row"], W["gdn_dtb_row"])
    ride = _gather_ride(shards, _LATE)
    prep = _gdn_prep(qk, v_g, bg, ride)
    if ride:
        prep, got = prep
        W.update({n: _full_from_slots(n, g) for n, g in zip(_LATE, got)})
    u_g, w_g, qd_g, kd_g, p_g, t_save = prep
    xbc = _conv_fwd(p, C_XBC, D + 512, W["ssm_conv_w"], W["ssm_conv_b"], l2=False, name="ssm_conv_fwd", bc=512)
    da_s = _ssd_dt_fwd(pg, W["ssm_dtb_row"], W["ssm_alog_row"])
    (o_g, vn_g, s_save), (y_s, h_save) = _run_scans(
        [_gdn_scan_fwd(u_g, w_g, qd_g, kd_g, p_g, bg), _ssd_core_fwd(xbc, da_s)], name="scans_fwd")
    mix = _gdn_post_fwd(o_g, p, W["gdn_norm_x"])
    mix = _ssd_post_fwd(y_s, xbc, p, W["ssm_d_x"], W["ssm_norm_w"].reshape(1, D), mix)
    x1, h2 = _mm(mix, W["w_out"], epi="res_norm", extra=(x, W["norm2_w"]), bm=512, name="out_proj")
    qm = _mm(h2, W["wq_mem"], out_dtype=BF16, name="q_proj")
    m = _rmsnorm_fwd(mem, W["mem_norm_w"], name="mem_norm_fwd")
    km = _mm(m, W["wk_mem"], name="k_proj")
    vm = _mm(m, W["wv_mem"], name="v_proj")
    oa = _attn_fwd(qm, km, vm)
    x2, h3 = _mm(oa, W["wo_mem"], epi="res_norm", extra=(x1, W["norm3_w"]), bm=512, name="o_proj")
    u, act = _mm(h3, W["w_up"], epi="relu2", out_dtype=BF16, name="mlp_up")
    dx3, g_final, loss = _mm(act, W["w_down"], epi="res_loss", extra=(x2, tgt, W["final_norm_w"]), bk_cap=1024,
                             name="mlp_down_loss")
    G = {"final_norm_w": g_final.reshape(D)}
    dpre = _mm(dx3, W["w_down"], dims="nt", epi="mul2", extra=u, out_dtype=BF16, name="mlp_down_dx")
    G["w_down"] = _mm(act, dx3, dims="tn", out_dtype=BF16, name="mlp_down_dw")
    G["w_up"] = _mm(h3, dpre, dims="tn", out_dtype=BF16, name="mlp_up_dw")
    dx2, gw = _mm(dpre, W["w_up"], dims="nt", epi="norm_bwd", extra=(x2, dx3, W["norm3_w"]), bk_cap=1024,
                  name="mlp_up_dx")
    G["norm3_w"] = gw.reshape(D)
    do_a = _mm(dx2, W["wo_mem"], dims="nt", out_dtype=BF16, name="o_proj_dx")
    G["wo_mem"] = _mm(oa, dx2, dims="tn", out_dtype=BF16, name="o_proj_dw")
    dq, dk, dv = _attn_bwd(qm, km, vm, do_a)
    G["wq_mem"] = _mm(h2, dq, dims="tn", out_dtype=BF16, name="q_proj_dw")
    dx1, gw = _mm(dq, W["wq_mem"], dims="nt", epi="norm_bwd", extra=(x1, dx2, W["norm2_w"]), bm=512,
                  name="q_proj_dx")
    G["norm2_w"] = gw.reshape(D)
    G["wk_mem"] = _mm(m, dk, dims="tn", out_dtype=BF16, name="k_proj_dw")
    G["wv_mem"] = _mm(m, dv, dims="tn", out_dtype=BF16, name="v_proj_dw")
    dm = _mm(dk, W["wk_mem"], dims="nt", name="k_proj_dx")
    dm = _mm(dv, W["wv_mem"], dims="nt", epi="res", extra=dm, name="v_proj_dx")
    _, G["mem_norm_w"] = _rmsnorm_bwd(mem, W["mem_norm_w"], dm, None, name="mem_norm_bwd")
    dmix = _mm(dx1, W["w_out"], dims="nt", name="out_proj_dx")
    G["w_out"] = _mm(mix, dx1, dims="tn", out_dtype=BF16, name="out_proj_dw")
    do_g, dp, G["gdn_norm_x"] = _gdn_post_bwd(dmix, o_g, p, W["gdn_norm_x"])
    dyy, dp, G["ssm_d_x"], G["ssm_norm_w"] = _ssd_post_bwd(dmix, y_s, xbc, p, W["ssm_d_x"],
                                                          W["ssm_norm_w"].reshape(1, D), dp)
    (dvn_g, ds_save), (dxbc, dda_s) = _run_scans(
        [_gdn_scan_bwd(w_g, qd_g, kd_g, p_g, bg, do_g), _ssd_core_bwd(xbc, da_s, h_save, dyy, W["ssm_d_x"])],
        name="scans_bwd")
    ride = _grad_ride(shards, G, _GRADS_MLP)
    rest = _gdn_rest_bwd(qk, v_g, bg, s_save, t_save, vn_g, dvn_g, ds_save, do_g, ride)
    if ride:
        rest, got = rest
        G.update(zip(_GRADS_MLP, got))
    dqkvn, dbg = rest
    dy_qk, gcw_qk, _ = _conv_bwd_act(p, C_QKV, 2 * D, cw_qk, None, dqkvn, 0, l2=True, name="gdn_conv_qk_bwd_act")
    dy_v, gcw_v, _ = _conv_bwd_act(p, C_QKV + 2 * D, D, cw_v, None, dqkvn, 2 * D, l2=False,
                                   name="gdn_conv_v_bwd_act")
    G["gdn_conv_w"] = jnp.concatenate([gcw_qk, gcw_v], axis=1)
    dp = _conv_bwd_in(dy_qk, cw_qk, dp, C_QKV, T, name="gdn_conv_qk_bwd_in")
    dp = _conv_bwd_in(dy_v, cw_v, dp, C_QKV + 2 * D, T, name="gdn_conv_v_bwd_in")
    dp, G["gdn_alog_row"], G["gdn_dtb_row"] = _gdn_gates_bwd(pg, W["gdn_alog_row"], W["gdn_dtb_row"], dbg, dp)
    dy_s, G["ssm_conv_w"], G["ssm_conv_b"] = _conv_bwd_act(p, C_XBC, D + 512, W["ssm_conv_w"], W["ssm_conv_b"],
                                                           dxbc, 0, l2=False, name="ssm_conv_bwd_act", bc=512)
    dp = _conv_bwd_in(dy_s, W["ssm_conv_w"], dp, C_XBC, T, name="ssm_conv_bwd_in", bc=512)
    dp, G["ssm_dtb_row"], G["ssm_alog_row"] = _ssd_dt_bwd(pg, W["ssm_dtb_row"], W["ssm_alog_row"], dda_s, dp)
    ride = _grad_ride(shards, G, _GRADS_MID)
    g_in = _mm(h1, dp, dims="tn", out_dtype=BF16, bn_cap=1152, name="in_proj_dw", ride=ride)
    if ride:
        g_in, got = g_in
        G.update(zip(_GRADS_MID, got))
    G["w_in"] = _unpad_w_in(g_in)
    ride = _grad_ride(shards, G, ("w_in",))
    res = _mm(dp, W["w_in_pad"], dims="nt", epi="norm_bwd", extra=(x, dx1, W["norm1_w"]),
              name="in_proj_dx", ride=ride)
    if ride:
        res, got = res
        G["w_in"] = got[0]
    dx, gw = res
    G["norm1_w"] = gw.reshape(D)
    return loss, dx, G


def _all_gather(shards, out_dtype, *, name):
    n = len(shards)

    def body(*refs):
        x_refs, out_refs, stage = refs[:n], refs[n:2 * n], refs[2 * n:3 * n]
        send_sems, recv_sems, local_sems = refs[3 * n:]
        x, y, c = _place()
        me, sibling = (x, y, c), (x, y, 1 - c)
        chips = [(1 - x, y), (x, 1 - y), (1 - x, 1 - y)]

        def slot(px, py, pc):
            return 4 * px + 2 * py + pc

        def copy(a, k, block, to, src=None):
            dst = out_refs[a].at[slot(*block)]
            return pltpu.make_async_remote_copy(
                src_ref=dst if src is None else src, dst_ref=dst, send_sem=send_sems.at[a, k],
                recv_sem=recv_sems.at[a, k], device_id=to, device_id_type=_MESH)

        for a in range(n):
            stage[a][...] = x_refs[a][...].astype(out_dtype)
        mine = [pltpu.make_async_copy(stage[a], out_refs[a].at[slot(*me)], local_sems.at[a]) for a in range(n)]
        for cp in mine:
            cp.start()
        first = []
        for a in range(n):
            first.append(copy(a, 0, me, sibling, src=stage[a]))
            first += [copy(a, 1 + j, me, (*chip, c), src=stage[a]) for j, chip in enumerate(chips)]
        for cp in first:
            cp.start()
        passed = [[copy(a, 4 + j, (*chip, c), sibling) for j, chip in enumerate(chips)] for a in range(n)]
        for j, chip in enumerate(chips):
            for a in range(n):
                copy(a, 1 + j, (*chip, c), me).wait_recv()
                passed[a][j].start()
        for a in range(n):
            copy(a, 0, sibling, me).wait_recv()
            for j, chip in enumerate(chips):
                copy(a, 4 + j, (*chip, 1 - c), me).wait_recv()
        for cp in first + [cp for row in passed for cp in row]:
            cp.wait_send()
        for cp in mine:
            cp.wait()

    outs = pl.pallas_call(
        body, in_specs=[_VM] * n, out_specs=[_ANY] * n,
        out_shape=[jax.ShapeDtypeStruct((N_DEV,) + s.shape, out_dtype) for s in shards],
        scratch_shapes=[pltpu.VMEM(s.shape, out_dtype) for s in shards]
        + [pltpu.SemaphoreType.DMA((n, 7)), pltpu.SemaphoreType.DMA((n, 7)), pltpu.SemaphoreType.DMA((n,))],
        name=name, compiler_params=pltpu.CompilerParams(vmem_limit_bytes=VMEM_LIMIT))(*shards)
    return list(outs)


def _cast_bf16(arrs, *, name):
    n = len(arrs)

    def body(*refs):
        for a in range(n):
            refs[n + a][...] = refs[a][...].astype(BF16)

    return list(pl.pallas_call(
        body, in_specs=[_VM] * n, out_specs=[_VM] * n,
        out_shape=[jax.ShapeDtypeStruct(s.shape, BF16) for s in arrs], name=name,
        compiler_params=pltpu.CompilerParams(vmem_limit_bytes=VMEM_LIMIT))(*arrs))


def _sum8(a, *, name):
    _, R, Cc = a.shape
    br = _pick_rows(R, 128)

    def body(a_ref, o_ref):
        s = a_ref[0].astype(F32)
        for k in range(1, N_DEV):
            s = s + a_ref[k].astype(F32)
        o_ref[...] = s

    return pl.pallas_call(
        body, grid=(R // br,), in_specs=[pl.BlockSpec((N_DEV, br, Cc), lambda i: (0, i, 0))],
        out_specs=pl.BlockSpec((br, Cc), lambda i: (i, 0)), out_shape=jax.ShapeDtypeStruct((R, Cc), F32),
        name=name, compiler_params=_params(("parallel",)))(a)


def _pick_rows(R, cap):
    if R <= cap:
        return R
    for d in range(cap, 7, -8):
        if R % d == 0:
            return d
    return R


def _adamw(w, g, m, v, *, name):
    shape = w.shape
    as2d = (lambda t: t.reshape(1, -1)) if w.ndim == 1 else (lambda t: t)
    w2, m2, v2 = as2d(w), as2d(m), as2d(v)
    R, Cc = w2.shape
    from_slabs = g.ndim == 3
    br = _pick_rows(R, 128 if from_slabs else 256)
    c1 = 1.0 - ADAM_B1 ** ADAM_STEP
    c2 = 1.0 - ADAM_B2 ** ADAM_STEP

    def body(w_ref, g_ref, m_ref, v_ref, go_ref, d_ref, nm_ref, nv_ref):
        if from_slabs:
            gv = g_ref[0].astype(F32)
            for k in range(1, N_DEV):
                gv = gv + g_ref[k].astype(F32)
        else:
            gv = g_ref[...]
        go_ref[...] = gv
        nm = ADAM_B1 * m_ref[...] + (1.0 - ADAM_B1) * gv
        nv = ADAM_B2 * v_ref[...] + (1.0 - ADAM_B2) * (gv * gv)
        nm_ref[...] = nm
        nv_ref[...] = nv
        d_ref[...] = -ADAM_LR * ((nm / c1) / (jnp.sqrt(nv / c2) + ADAM_EPS) + ADAM_WD * w_ref[...])

    blk = pl.BlockSpec((br, Cc), lambda i: (i, 0))
    g_spec = pl.BlockSpec((N_DEV, br, Cc), lambda i: (0, i, 0)) if from_slabs else blk
    outs = pl.pallas_call(
        body, grid=(R // br,), in_specs=[blk, g_spec, blk, blk], out_specs=[blk] * 4,
        out_shape=[jax.ShapeDtypeStruct((R, Cc), F32)] * 4, name=name,
        compiler_params=_params(("parallel",)))(w2, g if from_slabs else as2d(g), m2, v2)
    return tuple(o.reshape(shape) for o in outs)


_BIG = ("w_in", "w_out", "wq_mem", "wk_mem", "wv_mem", "wo_mem", "w_up", "w_down")
_COL_SHARDED = ("w_in", "w_up")
_WEIGHTS = ("norm1_w", "w_in", "gdn_conv_w", "gdn_a_log", "gdn_dt_bias", "gdn_norm_w", "ssm_conv_w", "ssm_conv_b",
            "ssm_a_log", "ssm_dt_bias", "ssm_d", "ssm_norm_w", "w_out", "norm2_w", "mem_norm_w", "wq_mem", "wk_mem",
            "wv_mem", "wo_mem", "norm3_w", "w_up", "w_down", "final_norm_w")
_IN_PAD = 112


def _full_from_slots(name, g):
    if name in _COL_SHARDED:
        return jnp.transpose(g, (1, 0, 2)).reshape(g.shape[1], N_DEV * g.shape[2])
    return g.reshape(N_DEV * g.shape[1], g.shape[2])


def _slots_from_full(name, f):
    if name in _COL_SHARDED:
        return jnp.transpose(f.reshape(f.shape[0], N_DEV, f.shape[1] // N_DEV), (1, 0, 2))
    return f.reshape(N_DEV, f.shape[0] // N_DEV, f.shape[1])


def _pad_w_in(w):
    z = jnp.zeros((w.shape[0], _IN_PAD), w.dtype)
    return jnp.concatenate([w[:, :4096], w[:, 4112:6672], w[:, 4096:4112], z, w[:, 6672:6688], z], axis=1)


def _unpad_w_in(gp):
    return jnp.concatenate([gp[:, :4096], gp[:, C_GATE:C_GATE + 16], gp[:, 4096:C_GATE], gp[:, C_DT:C_DT + 16]],
                           axis=1)


def _pack_rows(vals):
    rows, offs, r = [], [], 0
    for vflat in vals:
        nrow = 8 * -(-vflat.shape[0] // 1024)
        rows.append(jnp.pad(vflat, (0, nrow * 128 - vflat.shape[0])).reshape(nrow, 128))
        offs.append((r, vflat.shape[0]))
        r += nrow
    return jnp.concatenate(rows, axis=0), offs


def _unpack_rows(packed, offs, shapes):
    out = []
    for (r, nel), shp in zip(offs, shapes):
        nrow = -(-nel // 128)
        out.append(packed[r:r + nrow].reshape(-1)[:nel].reshape(shp))
    return out


def kernel(x, mem, norm1_w, w_in, gdn_conv_w, gdn_a_log, gdn_dt_bias, gdn_norm_w, ssm_conv_w, ssm_conv_b, ssm_a_log, ssm_dt_bias, ssm_d, ssm_norm_w, w_out, norm2_w, mem_norm_w, wq_mem, wk_mem, wv_mem, wo_mem, norm3_w, w_up, w_down, final_norm_w, loss_target, m_norm1_w, m_w_in, m_gdn_conv_w, m_gdn_a_log, m_gdn_dt_bias, m_gdn_norm_w, m_ssm_conv_w, m_ssm_conv_b, m_ssm_a_log, m_ssm_dt_bias, m_ssm_d, m_ssm_norm_w, m_w_out, m_norm2_w, m_mem_norm_w, m_wq_mem, m_wk_mem, m_wv_mem, m_wo_mem, m_norm3_w, m_w_up, m_w_down, m_final_norm_w, v_norm1_w, v_w_in, v_gdn_conv_w, v_gdn_a_log, v_gdn_dt_bias, v_gdn_norm_w, v_ssm_conv_w, v_ssm_conv_b, v_ssm_a_log, v_ssm_dt_bias, v_ssm_d, v_ssm_norm_w, v_w_out, v_norm2_w, v_mem_norm_w, v_wq_mem, v_wk_mem, v_wv_mem, v_wo_mem, v_norm3_w, v_w_up, v_w_down, v_final_norm_w):
    args = dict(locals())
    w_loc = {n: args[n] for n in _WEIGHTS}
    me = 4 * lax.axis_index("x") + 2 * lax.axis_index("y") + lax.axis_index("c")

    w_in_full = _full_from_slots("w_in", _all_gather([w_in], BF16, name="gather_w_in")[0])
    later = _EARLY + _LATE
    shards = dict(zip(later, _cast_bf16([w_loc[n] for n in later], name="cast_shards")))
    conv_pack, conv_offs = _pack_rows([gdn_conv_w.reshape(-1), ssm_conv_w.reshape(-1)])
    conv_all = _all_gather([conv_pack], F32, name="gather_conv")[0]
    gdn_cw, ssm_cw = [], []
    for k in range(N_DEV):
        a, b = _unpack_rows(conv_all[k], conv_offs, [gdn_conv_w.shape, ssm_conv_w.shape])
        gdn_cw.append(a)
        ssm_cw.append(b)
    W = {
        "w_in_pad": _pad_w_in(w_in_full),
        "norm1_w": norm1_w, "norm2_w": norm2_w, "norm3_w": norm3_w, "mem_norm_w": mem_norm_w,
        "final_norm_w": final_norm_w, "ssm_norm_w": ssm_norm_w, "ssm_conv_b": ssm_conv_b,
        "gdn_conv_w": jnp.concatenate(gdn_cw, axis=1), "ssm_conv_w": jnp.concatenate(ssm_cw, axis=1),
        "gdn_alog_row": jnp.pad(gdn_a_log, (GDN_H, 128 - 2 * GDN_H)).reshape(1, 128),
        "gdn_dtb_row": jnp.pad(gdn_dt_bias, (GDN_H, 128 - 2 * GDN_H)).reshape(1, 128),
        "gdn_norm_x": jnp.tile(gdn_norm_w, GDN_H).reshape(1, D),
        "ssm_dtb_row": jnp.pad(ssm_dt_bias, (0, 128 - SSM_H)).reshape(1, 128),
        "ssm_alog_row": jnp.pad(ssm_a_log, (0, 128 - SSM_H)).reshape(1, 128),
        "ssm_d_x": jnp.repeat(ssm_d, SSM_P).reshape(1, D),
    }

    loss_part, grad_x, G = _local_step(x[0], mem[0], loss_target[0], W, shards)

    grads = {n: G[n] for n in _BIG}

    small = {
        "norm1_w": G["norm1_w"], "gdn_conv_w": G["gdn_conv_w"], "gdn_a_log": G["gdn_alog_row"][0, GDN_H:2 * GDN_H],
        "gdn_dt_bias": G["gdn_dtb_row"][0, GDN_H:2 * GDN_H], "gdn_norm_w": G["gdn_norm_x"].reshape(GDN_H, 128).sum(0),
        "ssm_conv_w": G["ssm_conv_w"], "ssm_conv_b": G["ssm_conv_b"],
        "ssm_a_log": G["ssm_alog_row"][0, :SSM_H], "ssm_dt_bias": G["ssm_dtb_row"][0, :SSM_H],
        "ssm_d": G["ssm_d_x"].reshape(SSM_H, SSM_P).sum(1), "ssm_norm_w": G["ssm_norm_w"].reshape(D),
        "norm2_w": G["norm2_w"], "mem_norm_w": G["mem_norm_w"], "norm3_w": G["norm3_w"],
        "final_norm_w": G["final_norm_w"], "loss": loss_part[0, :1],
    }
    names = list(small)
    pack, offs = _pack_rows([small[n].reshape(-1) for n in names])
    tot = _sum8(_all_gather([pack], F32, name="gather_small")[0], name="sum_small")
    summed = dict(zip(names, _unpack_rows(tot, offs, [small[n].shape for n in names])))
    loss = summed.pop("loss")[0]
    for n in ("gdn_conv_w", "ssm_conv_w"):
        width = w_loc[n].shape[1]
        summed[n] = lax.dynamic_slice_in_dim(summed[n], me * width, width, axis=1)
    grads.update(summed)

    upd = {n: _adamw(w_loc[n], grads[n], args["m_" + n], args["v_" + n], name="adamw_" + n) for n in _WEIGHTS}
    return (loss, grad_x[None], *[upd[n][0] for n in _WEIGHTS], *[upd[n][1] for n in _WEIGHTS],
            *[upd[n][2] for n in _WEIGHTS], *[upd[n][3] for n in _WEIGHTS])
```

```python
import jax
import jax.numpy as jnp
from jax import lax
from jax.experimental import pallas as pl
from jax.experimental.pallas import tpu as pltpu

F32 = jnp.float32
BF16 = jnp.bfloat16
_MXU = BF16

D = 1024
EPS = 1e-6
CONV_K = 4
GDN_H, GDN_DK, GDN_C = 8, 128, 64
GDN_SCAN_CHUNKS = 4
GDN_LOCAL_CHUNKS = 4
GDN_REST_CHUNKS = 4
SSM_H, SSM_P, SSM_L, SSM_N = 16, 64, 128, 128
SSM_SCAN_CHUNKS = 2
MEM_H, MEM_HD = 4, 256
D_FF = 4096
N_DEV = 8

C_QKV, C_ZG, C_ZS, C_XBC, C_GATE, C_DT, C_TOT = 0, 3072, 4096, 5120, 6656, 6784, 6912
P_HALO = 16

ADAM_LR, ADAM_B1, ADAM_B2, ADAM_EPS, ADAM_WD, ADAM_STEP = 0.001, 0.9, 0.999, 1e-08, 0.01, 10

VMEM_LIMIT = 56 * 1024 * 1024

_NN = (((1,), (0,)), ((), ()))
_NT = (((1,), (1,)), ((), ()))
_TN = (((0,), (0,)), ((), ()))


def _dot(a, b, dims=_NN):
    return lax.dot_general(a.astype(_MXU), b.astype(_MXU), dims, preferred_element_type=F32)


def _split3(a):
    a1 = a.astype(BF16)
    r1 = a - a1.astype(F32)
    a2 = r1.astype(BF16)
    return a1, a2, (r1 - a2.astype(F32)).astype(BF16)


def _dot_sel(a, e):
    eb = e.astype(BF16)
    return sum(lax.dot_general(p, eb, _NN, preferred_element_type=F32) for p in _split3(a))


def _sel_dot(e, a):
    eb = e.astype(BF16)
    return sum(lax.dot_general(eb, p, _NN, preferred_element_type=F32) for p in _split3(a))


def _chunk_cumsum(a, tri, chunk):
    return jnp.concatenate([_sel_dot(tri, a[r:r + chunk]) for r in range(0, a.shape[0], chunk)], axis=0)


def _params(sem):
    return pltpu.CompilerParams(dimension_semantics=sem, vmem_limit_bytes=VMEM_LIMIT)


def _pick(n, cap):
    for d in range(min(cap, n), 0, -128):
        if n % d == 0 and d % 128 == 0:
            return d
    return n


def _sigmoid(x):
    return 0.5 * jnp.tanh(0.5 * x) + 0.5


def _silu(x):
    return x * _sigmoid(x)


def _dsilu(x):
    s = _sigmoid(x)
    return s * (1.0 + x * (1.0 - s))


def _softplus(x):
    return jnp.maximum(x, 0.0) + jnp.log(1.0 + jnp.exp(-jnp.abs(x)))


def _iota2(shape, axis):
    return lax.broadcasted_iota(jnp.int32, shape, axis)


def _sum_all(x):
    return jnp.sum(jnp.sum(x, axis=1, keepdims=True), axis=0, keepdims=True)


_MESH = pl.DeviceIdType.MESH
_ANY = pl.BlockSpec(memory_space=pl.ANY)
_VM = pl.BlockSpec(memory_space=pltpu.VMEM)
_REL = [(r >> 2 & 1, r >> 1 & 1, r & 1) for r in range(1, N_DEV)]


def _place():
    return lax.axis_index("x"), lax.axis_index("y"), lax.axis_index("c")


class _Ride:
    def __init__(self, srcs, shard):
        self.srcs, self.shard, self.n = list(srcs), shard, len(srcs)
        self.out_shape = [jax.ShapeDtypeStruct(((N_DEV,) + s.shape) if shard else s.shape, s.dtype)
                          for s in self.srcs]
        self.specs = [_ANY] * self.n
        self.scratch = [pltpu.SemaphoreType.DMA((self.n, N_DEV - 1)), pltpu.SemaphoreType.DMA((self.n, N_DEV - 1)),
                        pltpu.SemaphoreType.DMA((self.n,))]

    def _copies(self, in_refs, out_refs, sems):
        send, recv, loc = sems
        x, y, c = _place()
        me = 4 * x + 2 * y + c
        local, remote, arrive = [], [], []
        for a in range(self.n):
            src = in_refs[a] if self.shard else in_refs[a].at[me]
            local.append(pltpu.make_async_copy(src, out_refs[a].at[me], loc.at[a]))
        for k, (rx, ry, rc) in enumerate(_REL):
            peer = (lax.rem(x + rx, 2), lax.rem(y + ry, 2), lax.rem(c + rc, 2))
            ps = 4 * peer[0] + 2 * peer[1] + peer[2]
            for a in range(self.n):
                src = in_refs[a] if self.shard else in_refs[a].at[ps]
                remote.append(pltpu.make_async_remote_copy(
                    src_ref=src, dst_ref=out_refs[a].at[me], send_sem=send.at[a, k], recv_sem=recv.at[a, k],
                    device_id=peer, device_id_type=_MESH))
                slot = out_refs[a].at[ps]
                arrive.append(pltpu.make_async_remote_copy(
                    src_ref=slot, dst_ref=slot, send_sem=send.at[a, k], recv_sem=recv.at[a, k],
                    device_id=peer, device_id_type=_MESH))
        return local, remote, arrive

    def start(self, in_refs, out_refs, sems):
        local, remote, _ = self._copies(in_refs, out_refs, sems)
        for cp in local + remote:
            cp.start()

    def wait(self, in_refs, out_refs, sems):
        local, remote, arrive = self._copies(in_refs, out_refs, sems)
        for cp in arrive:
            cp.wait_recv()
        for cp in remote:
            cp.wait_send()
        for cp in local:
            cp.wait()


_EPI = {
    "none": ((), ("tile",)),
    "res": (("tile",), ("tile",)),
    "mul2": (("tile",), ("tile",)),
    "relu2": ((), ("tile", "tile")),
    "res_norm": (("tile", "row"), ("tile", "tile")),
    "norm_bwd": (("tile", "tile", "row"), ("tile", "row")),
    "res_loss": (("tile", "tile", "row"), ("tile", "row", "row")),
}


def _mm(a, b, *, dims="nn", epi="none", extra=(), out_dtype=F32, name, bm=1024, bn_cap=1024, bk_cap=2048,
        ride=None, b_cols=None):
    if dims == "nn":
        (M, K), (K2, N) = a.shape, b.shape
    elif dims == "nt":
        (M, K), (N, K2) = a.shape, b.shape
    else:
        (K, M), (K2, N) = a.shape, b.shape
    jb0 = 0
    if b_cols is not None:
        N = b_cols[1]
    assert K == K2, (a.shape, b.shape, dims)
    bm = _pick(M, bm)
    bn = _pick(N, bn_cap)
    bk = _pick(K, bk_cap)
    nk = K // bk
    if b_cols is not None:
        assert dims == "nn" and b_cols[0] % bn == 0
        jb0 = b_cols[0] // bn
    dn = {"nn": _NN, "nt": _NT, "tn": _TN}[dims]
    a_spec = (pl.BlockSpec((bk, bm), lambda i, j, k: (k, i)) if dims == "tn"
              else pl.BlockSpec((bm, bk), lambda i, j, k: (i, k)))
    b_spec = (pl.BlockSpec((bn, bk), lambda i, j, k: (j, k)) if dims == "nt"
              else pl.BlockSpec((bk, bn), lambda i, j, k: (k, j + jb0)))
    o_spec = pl.BlockSpec((bm, bn), lambda i, j, k: (i, j))
    r_spec = pl.BlockSpec((1, bn), lambda i, j, k: (0, j))
    extra = list(extra) if isinstance(extra, (tuple, list)) else [extra]
    ekinds, okinds = _EPI[epi]
    assert len(extra) == len(ekinds) and (epi not in ("res_norm", "norm_bwd", "res_loss") or bn == N)
    n_extra, n_out = len(ekinds), len(okinds)
    n_ride = ride.n if ride else 0
    gi, gj = M // bm, N // bn

    def body(a_ref, b_ref, *rest):
        ex = rest[:n_extra]
        first = pl.program_id(0) == 0
        ride_in = rest[n_extra:n_extra + n_ride]
        outs = rest[n_extra + n_ride:n_extra + n_ride + n_out]
        ride_out = rest[n_extra + n_ride + n_out:n_extra + 2 * n_ride + n_out]
        if ride:
            at = lambda i, j, k: ((pl.program_id(0) == i) & (pl.program_id(1) == j) & (pl.program_id(2) == k))

            @pl.when(at(0, 0, 0))
            def _():
                ride.start(ride_in, ride_out, rest[-3:])

        def finish(r):
            if epi == "res":
                outs[0][...] = (r + ex[0][...].astype(F32)).astype(outs[0].dtype)
            elif epi == "mul2":
                outs[0][...] = (2.0 * r * ex[0][...].astype(F32)).astype(outs[0].dtype)
            elif epi == "relu2":
                u = jnp.maximum(r, 0.0)
                outs[0][...] = u.astype(outs[0].dtype)
                outs[1][...] = (u * u).astype(outs[1].dtype)
            elif epi == "res_norm":
                y = r + ex[0][...]
                outs[0][...] = y
                rstd = lax.rsqrt(jnp.mean(y * y, axis=1, keepdims=True) + EPS)
                outs[1][...] = (y * rstd * ex[1][...]).astype(outs[1].dtype)
            elif epi == "norm_bwd":
                xv = ex[0][...]
                rstd = lax.rsqrt(jnp.mean(xv * xv, axis=1, keepdims=True) + EPS)
                xh = xv * rstd
                dxh = r * ex[2][...]
                outs[0][...] = ex[1][...] + rstd * (dxh - xh * jnp.mean(dxh * xh, axis=1, keepdims=True))
                dw = jnp.sum(r * xh, axis=0, keepdims=True)

                @pl.when(first)
                def _():
                    outs[1][...] = dw

                @pl.when(jnp.logical_not(first))
                def _():
                    outs[1][...] += dw
            elif epi == "res_loss":
                y = r + ex[0][...]
                wv = ex[2][...]
                rstd = lax.rsqrt(jnp.mean(y * y, axis=1, keepdims=True) + EPS)
                yh = y * rstd
                err = yh * wv - ex[1][...]
                part_loss = 0.5 * jnp.sum(jnp.mean(err * err, axis=1, keepdims=True), axis=0, keepdims=True)
                dyn = err * (1.0 / N)
                dyh = dyn * wv
                outs[0][...] = rstd * (dyh - yh * jnp.mean(dyh * yh, axis=1, keepdims=True))
                dw = jnp.sum(dyn * yh, axis=0, keepdims=True)
                lrow = jnp.broadcast_to(part_loss, (1, N))

                @pl.when(first)
                def _():
                    outs[1][...] = dw
                    outs[2][...] = lrow

                @pl.when(jnp.logical_not(first))
                def _():
                    outs[1][...] += dw
                    outs[2][...] += lrow
            else:
                outs[0][...] = r.astype(outs[0].dtype)

        part = _dot(a_ref[...], b_ref[...], dn)
        if nk == 1:
            finish(part)
        else:
            acc = rest[n_extra + 2 * n_ride + n_out]
            k = pl.program_id(2)

            @pl.when(k == 0)
            def _():
                acc[...] = part

            @pl.when((k > 0) & (k < nk - 1))
            def _():
                acc[...] += part

            @pl.when(k == nk - 1)
            def _():
                finish(acc[...] + part)

        if ride:
            @pl.when(at(gi - 1, gj - 1, nk - 1))
            def _():
                ride.wait(ride_in, ride_out, rest[-3:])

    kind_spec = {"tile": o_spec, "row": r_spec}
    ins = [a, b] + [e.reshape(1, N) if k == "row" else e for e, k in zip(extra, ekinds)]
    in_specs = [a_spec, b_spec] + [kind_spec[k] for k in ekinds]
    out_dtypes = {"res_norm": (F32, BF16), "norm_bwd": (F32, F32), "res_loss": (F32, F32, F32)}.get(
        epi, (out_dtype,) * n_out)
    out_shape = [jax.ShapeDtypeStruct((M, N) if k == "tile" else (1, N), dt) for k, dt in zip(okinds, out_dtypes)]
    out_specs = [kind_spec[k] for k in okinds]
    scratch = [pltpu.VMEM((bm, bn), F32)] if nk > 1 else []
    sem = ("arbitrary" if epi in ("norm_bwd", "res_loss") else "parallel", "parallel", "arbitrary")
    if ride:
        ins, in_specs = ins + ride.srcs, in_specs + ride.specs
        out_shape, out_specs = out_shape + ride.out_shape, out_specs + ride.specs
        scratch, sem = scratch + ride.scratch, ("arbitrary",) * 3
    res = pl.pallas_call(
        body, grid=(gi, gj, nk), in_specs=in_specs, out_specs=out_specs, out_shape=out_shape,
        scratch_shapes=scratch, name=name, compiler_params=_params(sem))(*ins)
    main = res[:n_out] if n_out > 1 else res[0]
    return (main, list(res[n_out:])) if ride else main


def _rmsnorm_fwd(x, w, *, name, bt=256):
    T, Dm = x.shape
    bt = min(bt, T)

    def body(x_ref, w_ref, h_ref):
        xv = x_ref[...]
        r = lax.rsqrt(jnp.mean(xv * xv, axis=1, keepdims=True) + EPS)
        h_ref[...] = (xv * r * w_ref[...]).astype(h_ref.dtype)

    return pl.pallas_call(
        body, grid=(T // bt,),
        in_specs=[pl.BlockSpec((bt, Dm), lambda i: (i, 0)), pl.BlockSpec((1, Dm), lambda i: (0, 0))],
        out_specs=pl.BlockSpec((bt, Dm), lambda i: (i, 0)),
        out_shape=jax.ShapeDtypeStruct((T, Dm), BF16), name=name,
        compiler_params=_params(("parallel",)))(x, w.reshape(1, Dm))


def _rmsnorm_bwd(x, w, dh, dres, *, name, bt=256):
    T, Dm = x.shape
    bt = min(bt, T)
    has_res = dres is not None

    def body(x_ref, w_ref, dh_ref, *rest):
        dres_ref = rest[0] if has_res else None
        dx_ref, dw_ref = rest[-2], rest[-1]
        i = pl.program_id(0)
        xv = x_ref[...]
        r = lax.rsqrt(jnp.mean(xv * xv, axis=1, keepdims=True) + EPS)
        xh = xv * r
        dhv = dh_ref[...].astype(F32)
        dxh = dhv * w_ref[...]
        dx = r * (dxh - xh * jnp.mean(dxh * xh, axis=1, keepdims=True))
        if has_res:
            dx = dx + dres_ref[...]
        dx_ref[...] = dx

        @pl.when(i == 0)
        def _():
            dw_ref[...] = jnp.zeros_like(dw_ref)

        dw_ref[...] += jnp.sum(dhv * xh, axis=0, keepdims=True)

    row = pl.BlockSpec((bt, Dm), lambda i: (i, 0))
    vec = pl.BlockSpec((1, Dm), lambda i: (0, 0))
    ins = [x, w.reshape(1, Dm), dh] + ([dres] if has_res else [])
    dx, dw = pl.pallas_call(
        body, grid=(T // bt,), in_specs=[row, vec, row] + ([row] if has_res else []),
        out_specs=[row, vec],
        out_shape=[jax.ShapeDtypeStruct((T, Dm), F32), jax.ShapeDtypeStruct((1, Dm), F32)],
        name=name, compiler_params=_params(("arbitrary",)))(*ins)
    return dx, dw.reshape(Dm)


def _attn_fwd(q, km, vm, *, bt=256):
    T = q.shape[0]
    M = km.shape[0]
    bt = min(bt, T)
    scale = MEM_HD ** -0.5

    def body(q_ref, k_ref, v_ref, o_ref):
        sls = [slice(h * MEM_HD, (h + 1) * MEM_HD) for h in range(MEM_H)]
        ss = [_dot(q_ref[:, sl], k_ref[:, sl], _NT) * scale for sl in sls]
        es = [jnp.exp(s - jnp.max(s, axis=1, keepdims=True)) for s in ss]
        ps = [e / jnp.sum(e, axis=1, keepdims=True) for e in es]
        for sl, p in zip(sls, ps):
            o_ref[:, sl] = _dot(p, v_ref[:, sl]).astype(o_ref.dtype)

    row = pl.BlockSpec((bt, D), lambda i: (i, 0))
    mem = pl.BlockSpec((M, D), lambda i: (0, 0))
    return pl.pallas_call(
        body, grid=(T // bt,), in_specs=[row, mem, mem], out_specs=row,
        out_shape=jax.ShapeDtypeStruct((T, D), BF16), name="attn_fwd",
        compiler_params=_params(("parallel",)))(q, km, vm)


def _attn_bwd(q, km, vm, do, *, bt=256):
    T = q.shape[0]
    M = km.shape[0]
    bt = min(bt, T)
    scale = MEM_HD ** -0.5

    def body(q_ref, k_ref, v_ref, do_ref, dq_ref, dk_ref, dv_ref):
        i = pl.program_id(0)

        @pl.when(i == 0)
        def _():
            dk_ref[...] = jnp.zeros_like(dk_ref)
            dv_ref[...] = jnp.zeros_like(dv_ref)

        sls = [slice(h * MEM_HD, (h + 1) * MEM_HD) for h in range(MEM_H)]
        ss = [_dot(q_ref[:, sl], k_ref[:, sl], _NT) * scale for sl in sls]
        dps = [_dot(do_ref[:, sl], v_ref[:, sl], _NT) for sl in sls]
        es = [jnp.exp(s - jnp.max(s, axis=1, keepdims=True)) for s in ss]
        ps = [e / jnp.sum(e, axis=1, keepdims=True) for e in es]
        dss = [p * (dp - jnp.sum(dp * p, axis=1, keepdims=True)) * scale for p, dp in zip(ps, dps)]
        for sl, p, ds in zip(sls, ps, dss):
            dq_ref[:, sl] = _dot(ds, k_ref[:, sl]).astype(dq_ref.dtype)
            dk_ref[:, sl] += _dot(ds, q_ref[:, sl], _TN)
            dv_ref[:, sl] += _dot(p, do_ref[:, sl], _TN)

    row = pl.BlockSpec((bt, D), lambda i: (i, 0))
    mem = pl.BlockSpec((M, D), lambda i: (0, 0))
    return pl.pallas_call(
        body, grid=(T // bt,), in_specs=[row, mem, mem, row], out_specs=[row, mem, mem],
        out_shape=[jax.ShapeDtypeStruct((T, D), BF16), jax.ShapeDtypeStruct((M, D), F32),
                   jax.ShapeDtypeStruct((M, D), F32)],
        name="attn_bwd", compiler_params=_params(("arbitrary",)))(q, km, vm, do)


def _conv_apply(halo, x, w_ref, b_ref):
    bt, hr = x.shape[0], halo.shape[0]
    cat = jnp.concatenate([halo, x], axis=0)
    y = x * w_ref[3:4, :]
    for k in range(CONV_K - 1):
        y = y + pltpu.roll(cat, CONV_K - 1 - k, 0)[hr:hr + bt] * w_ref[k:k + 1, :]
    if b_ref is not None:
        y = y + b_ref[...]
    return y


def _l2_parts(act, bc):
    out = []
    for s in range(bc // 128):
        a = act[:, s * 128:(s + 1) * 128]
        r = lax.rsqrt(jnp.sum(a * a, axis=1, keepdims=True) + EPS)
        out.append((a, r))
    return out


def _conv_fwd(p, col0, C, w, b, *, l2, name, bt=512, bc=1024):
    T = p.shape[0]
    bt = min(bt, T)
    c0, hb = col0 // bc, bt // P_HALO
    has_b = b is not None
    assert not l2 or (bc == D and C == 2 * D)

    def body(x_ref, halo_ref, w_ref, *rest):
        b_ref = rest[0] if has_b else None
        o_ref = rest[-1]
        i, j = pl.program_id(0), pl.program_id(1)
        x = x_ref[...].astype(F32)
        halo = jnp.where(i > 0, halo_ref[...].astype(F32), 0.0)
        act = _silu(_conv_apply(halo, x, w_ref, b_ref))
        if l2:
            sc = jnp.where(j == 0, GDN_DK ** -0.5, 1.0)
            o_ref[...] = jnp.concatenate([a * (r * sc) for a, r in _l2_parts(act, bc)], axis=1)
        else:
            o_ref[...] = act

    in_specs = [pl.BlockSpec((bt, bc), lambda i, j: (i, c0 + j)),
                pl.BlockSpec((P_HALO, bc), lambda i, j: (jnp.maximum(i * hb - 1, 0), c0 + j)),
                pl.BlockSpec((CONV_K, bc), lambda i, j: (0, j))]
    ins = [p, p, w]
    if has_b:
        in_specs.append(pl.BlockSpec((1, bc), lambda i, j: (0, j)))
        ins.append(b.reshape(1, C))
    return pl.pallas_call(
        body, grid=(T // bt, C // bc), in_specs=in_specs,
        out_specs=pl.BlockSpec((bt, bc), lambda i, j: (i, j)),
        out_shape=jax.ShapeDtypeStruct((T, C), F32), name=name,
        compiler_params=_params(("parallel", "parallel")))(*ins)


def _conv_bwd_act(p, col0, C, w, b, dact, dcol0, *, l2, name, bt=512, bc=1024):
    T = p.shape[0]
    bt = min(bt, T)
    c0, d0, hb = col0 // bc, dcol0 // bc, bt // P_HALO
    has_b = b is not None
    assert not l2 or (bc == D and C == 2 * D)

    def body(x_ref, halo_ref, w_ref, *rest):
        b_ref = rest[0] if has_b else None
        dact_ref, dy_ref, dw_ref, db_ref = rest[-4:]
        j, i = pl.program_id(0), pl.program_id(1)
        x = x_ref[...].astype(F32)
        halo = jnp.where(i > 0, halo_ref[...].astype(F32), 0.0)
        y = _conv_apply(halo, x, w_ref, b_ref)
        dact = dact_ref[...]
        sg = _sigmoid(y)
        if l2:
            sc = jnp.where(j == 0, GDN_DK ** -0.5, 1.0)
            parts = []
            for s, (a, r) in enumerate(_l2_parts(y * sg, bc)):
                n = a * r
                dn = dact[:, s * 128:(s + 1) * 128]
                parts.append((r * sc) * (dn - n * jnp.sum(dn * n, axis=1, keepdims=True)))
            dact = jnp.concatenate(parts, axis=1)
        dy = dact * (sg * (1.0 + y * (1.0 - sg)))
        dy_ref[...] = dy

        @pl.when(i == 0)
        def _():
            dw_ref[...] = jnp.zeros_like(dw_ref)
            db_ref[...] = jnp.zeros_like(db_ref)

        db_ref[...] += jnp.sum(dy, axis=0, keepdims=True)
        cat = jnp.concatenate([halo, x], axis=0)
        dw_ref[3:4, :] += jnp.sum(dy * x, axis=0, keepdims=True)
        for k in range(CONV_K - 1):
            xs = pltpu.roll(cat, CONV_K - 1 - k, 0)[P_HALO:P_HALO + bt]
            dw_ref[k:k + 1, :] += jnp.sum(dy * xs, axis=0, keepdims=True)

    in_specs = [pl.BlockSpec((bt, bc), lambda j, i: (i, c0 + j)),
                pl.BlockSpec((P_HALO, bc), lambda j, i: (jnp.maximum(i * hb - 1, 0), c0 + j)),
                pl.BlockSpec((CONV_K, bc), lambda j, i: (0, j))]
    ins = [p, p, w]
    if has_b:
        in_specs.append(pl.BlockSpec((1, bc), lambda j, i: (0, j)))
        ins.append(b.reshape(1, C))
    in_specs.append(pl.BlockSpec((bt, bc), lambda j, i: (i, d0 + j)))
    ins.append(dact)
    dy, dw, db = pl.pallas_call(
        body, grid=(C // bc, T // bt), in_specs=in_specs,
        out_specs=[pl.BlockSpec((bt, bc), lambda j, i: (i, j)),
                   pl.BlockSpec((CONV_K, bc), lambda j, i: (0, j)),
                   pl.BlockSpec((1, bc), lambda j, i: (0, j))],
        out_shape=[jax.ShapeDtypeStruct((T, C), F32), jax.ShapeDtypeStruct((CONV_K, C), F32),
                   jax.ShapeDtypeStruct((1, C), F32)],
        name=name, compiler_params=_params(("parallel", "arbitrary")))(*ins)
    return dy, dw, db.reshape(C)


def _conv_bwd_in(dy, w, dp_in, col0, T, *, name, bt=512, bc=1024):
    C = dy.shape[1]
    bt = min(bt, T)
    c0, hb, nb = col0 // bc, bt // 8, T // bt

    def body(dy_ref, nxt_ref, w_ref, *rest):
        o_ref = rest[-1]
        i = pl.program_id(0)
        dy_v = dy_ref[...]
        nxt = jnp.where(i < nb - 1, nxt_ref[...], 0.0)
        cat = jnp.concatenate([dy_v, nxt], axis=0)
        dx = dy_v * w_ref[3:4, :]
        for k in range(CONV_K - 1):
            s = CONV_K - 1 - k
            dx = dx + pltpu.roll(cat, bt + 8 - s, 0)[0:bt] * w_ref[k:k + 1, :]
        o_ref[...] = dx.astype(o_ref.dtype)

    in_specs = [pl.BlockSpec((bt, bc), lambda i, j: (i, j)),
                pl.BlockSpec((8, bc), lambda i, j: (jnp.minimum((i + 1) * hb, T // 8 - 1), j)),
                pl.BlockSpec((CONV_K, bc), lambda i, j: (0, j))]
    ins = [dy, dy, w]
    alias = {}
    if dp_in is not None:
        in_specs.append(pl.BlockSpec(memory_space=pl.ANY))
        ins.append(dp_in)
        alias = {3: 0}
    return pl.pallas_call(
        body, grid=(nb, C // bc), in_specs=in_specs,
        out_specs=pl.BlockSpec((bt, bc), lambda i, j: (i, c0 + j)),
        out_shape=jax.ShapeDtypeStruct((T, C_TOT), BF16), input_output_aliases=alias, name=name,
        compiler_params=_params(("parallel", "parallel")))(*ins)


def _expand_mats(shift, row0):
    e = (_iota2((128, D), 0) - row0 == (_iota2((128, D), 1) >> shift)).astype(F32)
    et = ((_iota2((D, 128), 0) >> shift) == _iota2((D, 128), 1) - row0).astype(F32)
    return e, et


def _cum_mats(chunk):
    ri, ci = _iota2((chunk, chunk), 0), _iota2((chunk, chunk), 1)
    return (ri >= ci).astype(F32), (ri <= ci).astype(F32)


def _gdn_gates_fwd(p, alog_row, dtb_row, *, bt=256):
    T = p.shape[0]
    bt = min(bt, T)

    def body(g_ref, al_ref, db_ref, bg_ref):
        gt = g_ref[...]
        lc, _ = _cum_mats(GDN_C)
        g_l = -jnp.exp(al_ref[...]) * _softplus(gt + db_ref[...])
        bg_ref[...] = jnp.where(_iota2((bt, 128), 1) < GDN_H, _sigmoid(gt), _chunk_cumsum(g_l, lc, GDN_C))

    vec = pl.BlockSpec((1, 128), lambda i: (0, 0))
    seg = pl.BlockSpec((bt, 128), lambda i: (i, 0))
    return pl.pallas_call(
        body, grid=(T // bt,), in_specs=[seg, vec, vec], out_specs=seg,
        out_shape=jax.ShapeDtypeStruct((T, 128), F32), name="gdn_gates_fwd",
        compiler_params=_params(("parallel",)))(p, alog_row, dtb_row)


def _gdn_gates_bwd(p, alog_row, dtb_row, dbg, dp_in, *, bt=256):
    T = p.shape[0]
    bt = min(bt, T)

    def body(g_ref, al_ref, db_ref, dbg_ref, dpin_ref, dg_out, dal_ref, ddb_ref):
        i = pl.program_id(0)
        gt = g_ref[...]
        lane = _iota2((bt, 128), 1)
        _, uc = _cum_mats(GDN_C)
        ea = jnp.exp(al_ref[...])
        zz = gt + db_ref[...]
        g_l = -ea * _softplus(zz)
        beta_l = _sigmoid(gt)
        dbg_v = dbg_ref[...]
        dg_l = jnp.where((lane >= GDN_H) & (lane < 2 * GDN_H), _chunk_cumsum(dbg_v, uc, GDN_C), 0.0)
        dbeta_l = jnp.where(lane < GDN_H, dbg_v, 0.0)
        da = dg_l * (-ea) * _sigmoid(zz)
        dg_out[...] = (da + dbeta_l * beta_l * (1.0 - beta_l)).astype(dg_out.dtype)

        @pl.when(i == 0)
        def _():
            dal_ref[...] = jnp.zeros_like(dal_ref)
            ddb_ref[...] = jnp.zeros_like(ddb_ref)

        dal_ref[...] += jnp.sum(dg_l * g_l, axis=0, keepdims=True)
        ddb_ref[...] += jnp.sum(da, axis=0, keepdims=True)

    vec = pl.BlockSpec((1, 128), lambda i: (0, 0))
    seg = pl.BlockSpec((bt, 128), lambda i: (i, 0))
    gate = pl.BlockSpec((bt, 128), lambda i: (i, C_GATE // 128))
    return pl.pallas_call(
        body, grid=(T // bt,), in_specs=[seg, vec, vec, seg, _ANY], out_specs=[gate, vec, vec],
        out_shape=[jax.ShapeDtypeStruct((T, C_TOT), BF16), jax.ShapeDtypeStruct((1, 128), F32),
                   jax.ShapeDtypeStruct((1, 128), F32)],
        input_output_aliases={4: 0}, name="gdn_gates_bwd",
        compiler_params=_params(("arbitrary",)))(p, alog_row, dtb_row, dbg, dp_in)


def _ssd_dt_fwd(p, dtb_row, alog_row, *, bt=256):
    T = p.shape[0]
    bt = min(bt, T)

    def body(d_ref, db_ref, al_ref, da_ref):
        lc, _ = _cum_mats(SSM_L)
        dt_l = _softplus(d_ref[...] + db_ref[...])
        alpha_l = _chunk_cumsum(dt_l * (-jnp.exp(al_ref[...])), lc, SSM_L)
        da_ref[...] = jnp.where(_iota2((bt, 128), 1) < SSM_H, dt_l, pltpu.roll(alpha_l, SSM_H, 1))

    v128 = pl.BlockSpec((1, 128), lambda i: (0, 0))
    return pl.pallas_call(
        body, grid=(T // bt,), in_specs=[pl.BlockSpec((bt, 128), lambda i: (i, 1)), v128, v128],
        out_specs=pl.BlockSpec((bt, 128), lambda i: (i, 0)), out_shape=jax.ShapeDtypeStruct((T, 128), F32),
        name="ssd_dt_fwd", compiler_params=_params(("parallel",)))(p, dtb_row, alog_row)


def _ssd_dt_bwd(p, dtb_row, alog_row, dda, dp_in, *, bt=256):
    T = p.shape[0]
    bt = min(bt, T)

    def body(d_ref, db_ref, al_ref, dda_ref, dpin_ref, dd_out, ddb_ref, dalog_ref):
        i = pl.program_id(0)
        heads = _iota2((bt, 128), 1) < SSM_H
        _, uc = _cum_mats(SSM_L)
        zz = d_ref[...] + db_ref[...]
        dt_l = _softplus(zz)
        a_row = -jnp.exp(al_ref[...])
        dda_v = dda_ref[...]
        da_l = _chunk_cumsum(jnp.where(heads, pltpu.roll(dda_v, 128 - SSM_H, 1), 0.0), uc, SSM_L)
        draw = jnp.where(heads, (dda_v + da_l * a_row) * _sigmoid(zz), 0.0)
        dd_out[...] = draw.astype(dd_out.dtype)

        @pl.when(i == 0)
        def _():
            ddb_ref[...] = jnp.zeros_like(ddb_ref)
            dalog_ref[...] = jnp.zeros_like(dalog_ref)

        ddb_ref[...] += jnp.sum(draw, axis=0, keepdims=True)
        dalog_ref[...] += jnp.sum(da_l * dt_l, axis=0, keepdims=True) * a_row

    seg = pl.BlockSpec((bt, 128), lambda i: (i, C_DT // 128))
    v128 = pl.BlockSpec((1, 128), lambda i: (0, 0))
    return pl.pallas_call(
        body, grid=(T // bt,),
        in_specs=[pl.BlockSpec((bt, 128), lambda i: (i, 1)), v128, v128, pl.BlockSpec((bt, 128), lambda i: (i, 0)), _ANY],
        out_specs=[seg, v128, v128],
        out_shape=[jax.ShapeDtypeStruct((T, C_TOT), BF16), jax.ShapeDtypeStruct((1, 128), F32),
                   jax.ShapeDtypeStruct((1, 128), F32)],
        input_output_aliases={4: 0}, name="ssd_dt_bwd",
        compiler_params=_params(("arbitrary",)))(p, dtb_row, alog_row, dda, dp_in)


def _gdn_post_fwd(o, p, w_x, *, bt=256):
    T = o.shape[0]
    bt = min(bt, T)

    def body(o_ref, z_ref, w_ref, out_ref):
        for h in range(GDN_H):
            sl = slice(h * 128, (h + 1) * 128)
            oh = o_ref[:, sl].astype(F32)
            r = lax.rsqrt(jnp.mean(oh * oh, axis=1, keepdims=True) + EPS)
            out_ref[:, sl] = (oh * r * w_ref[:, sl] * _silu(z_ref[:, sl].astype(F32))).astype(out_ref.dtype)

    row = pl.BlockSpec((bt, D), lambda i: (i, 0))
    return pl.pallas_call(
        body, grid=(T // bt,),
        in_specs=[row, pl.BlockSpec((bt, D), lambda i: (i, C_ZG // D)), pl.BlockSpec((1, D), lambda i: (0, 0))],
        out_specs=row, out_shape=jax.ShapeDtypeStruct((T, 2 * D), BF16), name="gdn_post_fwd",
        compiler_params=_params(("parallel",)))(o, p, w_x)


def _gdn_post_bwd(dmix, o, p, w_x, *, bt=256):
    T = o.shape[0]
    bt = min(bt, T)

    def body(dm_ref, o_ref, z_ref, w_ref, do_ref, dz_ref, dw_ref):
        i = pl.program_id(0)

        @pl.when(i == 0)
        def _():
            dw_ref[...] = jnp.zeros_like(dw_ref)

        for h in range(GDN_H):
            sl = slice(h * 128, (h + 1) * 128)
            oh, zh, wh = o_ref[:, sl].astype(F32), z_ref[:, sl].astype(F32), w_ref[:, sl]
            dm = dm_ref[:, sl].astype(F32)
            r = lax.rsqrt(jnp.mean(oh * oh, axis=1, keepdims=True) + EPS)
            ohat = oh * r
            dy = dm * _silu(zh)
            dz_ref[:, sl] = (dm * ohat * wh * _dsilu(zh)).astype(dz_ref.dtype)
            dohat = dy * wh
            do_ref[:, sl] = (r * (dohat - ohat * jnp.mean(dohat * ohat, axis=1, keepdims=True))).astype(do_ref.dtype)
            dw_ref[:, sl] += jnp.sum(dy * ohat, axis=0, keepdims=True)

    row = pl.BlockSpec((bt, D), lambda i: (i, 0))
    zcol = pl.BlockSpec((bt, D), lambda i: (i, C_ZG // D))
    vec = pl.BlockSpec((1, D), lambda i: (0, 0))
    return pl.pallas_call(
        body, grid=(T // bt,), in_specs=[row, row, zcol, vec], out_specs=[row, zcol, vec],
        out_shape=[jax.ShapeDtypeStruct((T, D), BF16), jax.ShapeDtypeStruct((T, C_TOT), BF16),
                   jax.ShapeDtypeStruct((1, D), F32)],
        name="gdn_post_bwd", compiler_params=_params(("arbitrary",)))(dmix, o, p, w_x)


def _ssd_post_fwd(y, xs, p, d_x, w, mix_in, *, bt=256):
    T = y.shape[0]
    bt = min(bt, T)

    def body(y_ref, x_ref, z_ref, d_ref, w_ref, mix_ref, out_ref):
        yg = (y_ref[...].astype(F32) + x_ref[...] * d_ref[...]) * _silu(z_ref[...].astype(F32))
        for g in range(2):
            sl = slice(g * 512, (g + 1) * 512)
            a = yg[:, sl]
            r = lax.rsqrt(jnp.mean(a * a, axis=1, keepdims=True) + EPS)
            out_ref[:, sl] = (a * r * w_ref[:, sl]).astype(out_ref.dtype)

    row = pl.BlockSpec((bt, D), lambda i: (i, 0))
    vec = pl.BlockSpec((1, D), lambda i: (0, 0))
    return pl.pallas_call(
        body, grid=(T // bt,),
        in_specs=[row, row, pl.BlockSpec((bt, D), lambda i: (i, C_ZS // D)), vec, vec, _ANY],
        out_specs=pl.BlockSpec((bt, D), lambda i: (i, 1)), out_shape=jax.ShapeDtypeStruct((T, 2 * D), BF16),
        input_output_aliases={5: 0}, name="ssd_post_fwd",
        compiler_params=_params(("parallel",)))(y, xs, p, d_x, w, mix_in)


def _ssd_post_bwd(dmix, y, xs, p, d_x, w, dp_in, *, bt=256):
    T = y.shape[0]
    bt = min(bt, T)

    def body(dm_ref, y_ref, x_ref, z_ref, d_ref, w_ref, dpin_ref, dyy_ref, dz_ref, dd_ref, dw_ref):
        i = pl.program_id(0)

        @pl.when(i == 0)
        def _():
            dd_ref[...] = jnp.zeros_like(dd_ref)
            dw_ref[...] = jnp.zeros_like(dw_ref)

        xv, zv = x_ref[...], z_ref[...].astype(F32)
        yy = y_ref[...].astype(F32) + xv * d_ref[...]
        sz = _silu(zv)
        yg = yy * sz
        parts = []
        for g in range(2):
            sl = slice(g * 512, (g + 1) * 512)
            a = yg[:, sl]
            r = lax.rsqrt(jnp.mean(a * a, axis=1, keepdims=True) + EPS)
            ah = a * r
            dout = dm_ref[:, sl].astype(F32)
            dah = dout * w_ref[:, sl]
            dw_ref[:, sl] += jnp.sum(dout * ah, axis=0, keepdims=True)
            parts.append(r * (dah - ah * jnp.mean(dah * ah, axis=1, keepdims=True)))
        dyg = jnp.concatenate(parts, axis=1)
        dyy = dyg * sz
        dyy_ref[...] = dyy
        dz_ref[...] = (dyg * yy * _dsilu(zv)).astype(dz_ref.dtype)
        dd_ref[...] += jnp.sum(dyy * xv, axis=0, keepdims=True)

    row = pl.BlockSpec((bt, D), lambda i: (i, 0))
    zcol = pl.BlockSpec((bt, D), lambda i: (i, C_ZS // D))
    vec = pl.BlockSpec((1, D), lambda i: (0, 0))
    return pl.pallas_call(
        body, grid=(T // bt,),
        in_specs=[pl.BlockSpec((bt, D), lambda i: (i, 1)), row, row, zcol, vec, vec, _ANY],
        out_specs=[row, zcol, vec, vec],
        out_shape=[jax.ShapeDtypeStruct((T, D), F32), jax.ShapeDtypeStruct((T, C_TOT), BF16),
                   jax.ShapeDtypeStruct((1, D), F32), jax.ShapeDtypeStruct((1, D), F32)],
        input_output_aliases={6: 1}, name="ssd_post_bwd",
        compiler_params=_params(("arbitrary",)))(dmix, y, xs, p, d_x, w, dp_in)


_NEG = -1e30


def _gdn_terms(q, k, v, bx, gam_c):
    C = GDN_C
    ri, ci = _iota2((C, C), 0), _iota2((C, C), 1)
    eye, low, strict = ri == ci, ri >= ci, ri > ci
    gam_r = jnp.sum(jnp.where(eye, gam_c, 0.0), axis=0, keepdims=True)
    G = jnp.exp(jnp.where(low, gam_c - gam_r, _NEG))
    glast = jnp.sum(jnp.where(_iota2((C, 1), 0) == C - 1, gam_c, 0.0), axis=0, keepdims=True)
    eg, egl, eL = jnp.exp(gam_c), jnp.exp(glast - gam_c), jnp.exp(glast)
    kb, vb = k * bx, v * bx
    M = _dot(kb, k, _NT)
    return dict(eye=eye, low=low, strict=strict, G=G, eg=eg, egl=egl, eL=eL, kb=kb, vb=vb, M=M,
                kbg=kb * eg, qd=q * eg, kd=k * egl, q=q, k=k, v=v, bx=bx)


def _split(a):
    hi = a.astype(_MXU)
    return hi, (a - hi.astype(F32)).astype(_MXU)


def _dot3s(a, b):
    d = lambda p, q: lax.dot_general(p, q, _NN, preferred_element_type=F32)
    return d(a[0], b[0]) + d(a[0], b[1]) + d(a[1], b[0])


def _tri_inv_many(Ls, eye):
    eyef = jnp.where(eye, 1.0, 0.0)
    Ts = [eyef - L for L in Ls]
    Ps = [-L for L in Ls]
    for _ in range(5):
        sp = [_split(p) for p in Ps]
        Ps = [_dot3s(s, s) for s in sp]
        sp = [_split(p) for p in Ps]
        st = [_split(t) for t in Ts]
        Ts = [t + _dot3s(a, b) for t, a, b in zip(Ts, st, sp)]
    return Ts


def _lane_col(tile, idx):
    return jnp.sum(jnp.where(_iota2(tile.shape, 1) == idx, tile, 0.0), axis=1, keepdims=True)


def _gdn_heads(q_ref, k_ref, v_ref, bg_ref, heads):
    out = []
    bg = bg_ref[...]
    for h in heads:
        sl = slice(h * 128, (h + 1) * 128)
        out.append(_gdn_terms(q_ref[:, sl], k_ref[:, sl], v_ref[:, sl], _lane_col(bg, h), _lane_col(bg, GDN_H + h)))
    return out


def _gdn_prep(qk, v, bg, ride=None):
    T = qk.shape[0]
    N = T // GDN_C
    C, CS = GDN_C, GDN_LOCAL_CHUNKS
    NB = N // CS
    n_ride = ride.n if ride else 0

    def body(q_ref, k_ref, v_ref, bg_ref, *rest):
        ride_in = rest[:n_ride]
        u_ref, w_ref, qd_ref, kd_ref, p_ref, t_ref = rest[n_ride:n_ride + 6]
        ride_out = rest[n_ride + 6:2 * n_ride + 6]
        if ride:
            @pl.when(pl.program_id(0) == 0)
            def _():
                ride.start(ride_in, ride_out, rest[-3:])

            @pl.when(pl.program_id(0) == NB - 1)
            def _():
                ride.wait(ride_in, ride_out, rest[-3:])

        items = [(c, h) for c in range(CS) for h in range(GDN_H)]
        views = [[r.at[pl.ds(c * C, C)] for r in (q_ref, k_ref, v_ref, bg_ref)] for c in range(CS)]
        ts = [_gdn_heads(*views[c], [h])[0] for c, h in items]
        Ts = _tri_inv_many([jnp.where(t["strict"], t["M"] * t["G"], 0.0) for t in ts], ts[0]["eye"])
        for (c, h), t, Tm in zip(items, ts, Ts):
            tok = slice(c * C, (c + 1) * C)
            sl = slice(h * 128, (h + 1) * 128)
            rows = slice(h * C, (h + 1) * C)
            u_ref[tok, sl] = _dot(Tm, t["vb"])
            w_ref[tok, sl] = _dot(Tm, t["kbg"]).astype(w_ref.dtype)
            qd_ref[tok, sl] = t["qd"].astype(qd_ref.dtype)
            kd_ref[tok, sl] = t["kd"].astype(kd_ref.dtype)
            p_ref[c, rows, :] = _dot(t["q"], t["k"], _NT) * t["G"]
            t_ref[c, rows, :] = Tm

    blk = lambda c: pl.BlockSpec((CS * C, D), lambda n: (n, c))
    sq = pl.BlockSpec((CS, GDN_H * C, C), lambda n: (n, 0, 0))
    in_specs = [blk(0), blk(1), blk(0), pl.BlockSpec((CS * C, 128), lambda n: (n, 0))]
    out_specs = [blk(0), blk(0), blk(0), blk(0), sq, sq]
    out_shape = [jax.ShapeDtypeStruct((T, D), F32), jax.ShapeDtypeStruct((T, D), BF16),
                 jax.ShapeDtypeStruct((T, D), BF16), jax.ShapeDtypeStruct((T, D), BF16),
                 jax.ShapeDtypeStruct((N, GDN_H * C, C), F32), jax.ShapeDtypeStruct((N, GDN_H * C, C), F32)]
    ins = [qk, qk, v, bg]
    if ride:
        ins, in_specs = ins + ride.srcs, in_specs + ride.specs
        out_shape, out_specs = out_shape + ride.out_shape, out_specs + ride.specs
    res = pl.pallas_call(
        body, grid=(NB,), in_specs=in_specs, out_specs=out_specs, out_shape=out_shape,
        scratch_shapes=ride.scratch if ride else [], name="gdn_prep",
        compiler_params=_params(("arbitrary",) if ride else ("parallel",)))(*ins)
    return (list(res[:6]), list(res[6:])) if ride else list(res)


def _gdn_scan_fwd(u, w, qd, kd, pm, bg):
    T = u.shape[0]
    N = T // GDN_C
    C, CS = GDN_C, GDN_SCAN_CHUNKS

    def body(u_ref, w_ref, qd_ref, kd_ref, p_ref, bg_ref, o_ref, vn_ref, ss_ref, S_scr):
        n = pl.program_id(0)

        @pl.when(n == 0)
        def _():
            S_scr[...] = jnp.zeros_like(S_scr)

        sls = [slice(h * 128, (h + 1) * 128) for h in range(GDN_H)]
        for c in range(CS):
            rows = slice(c * C, (c + 1) * C)
            glast = bg_ref[(c + 1) * C - 1:(c + 1) * C, :]
            Ss = [S_scr[:, sl] for sl in sls]
            vns = [u_ref[rows, sl] - _dot(w_ref[rows, sl], S) for sl, S in zip(sls, Ss)]
            for h, (sl, S, vn) in enumerate(zip(sls, Ss, vns)):
                ss_ref[c, :, sl] = S.astype(ss_ref.dtype)
                vn_ref[rows, sl] = vn.astype(vn_ref.dtype)
                o_ref[rows, sl] = (_dot(qd_ref[rows, sl], S)
                                   + _dot(p_ref[c, h * C:(h + 1) * C, :], vn)).astype(o_ref.dtype)
                S_scr[:, sl] = S * jnp.exp(_lane_col(glast, GDN_H + h)) + _dot(kd_ref[rows, sl], vn, _TN)

    blk = pl.BlockSpec((CS * C, D), lambda n: (n, 0))
    return dict(
        body=body, steps=N // CS, ins=[u, w, qd, kd, pm, bg],
        in_specs=[blk, blk, blk, blk, pl.BlockSpec((CS, GDN_H * C, C), lambda n: (n, 0, 0)),
                  pl.BlockSpec((CS * C, 128), lambda n: (n, 0))],
        out_specs=[blk, blk, pl.BlockSpec((CS, GDN_DK, D), lambda n: (n, 0, 0))],
        out_shape=[jax.ShapeDtypeStruct((T, D), BF16), jax.ShapeDtypeStruct((T, D), BF16),
                   jax.ShapeDtypeStruct((N, GDN_DK, D), BF16)],
        scratch=[pltpu.VMEM((GDN_DK, D), F32)])


def _gdn_scan_bwd(w, qd, kd, pm, bg, do):
    T = w.shape[0]
    N = T // GDN_C
    C, CS = GDN_C, GDN_SCAN_CHUNKS
    NB = N // CS

    def body(w_ref, qd_ref, kd_ref, p_ref, bg_ref, do_ref, dvn_ref, ds_ref, dS_scr):
        n = pl.program_id(0)

        @pl.when(n == 0)
        def _():
            dS_scr[...] = jnp.zeros_like(dS_scr)

        sls = [slice(h * 128, (h + 1) * 128) for h in range(GDN_H)]
        for c in reversed(range(CS)):
            rows = slice(c * C, (c + 1) * C)
            glast = bg_ref[(c + 1) * C - 1:(c + 1) * C, :]
            dSs = [dS_scr[:, sl] for sl in sls]
            dvns = [_dot(p_ref[c, h * C:(h + 1) * C, :], do_ref[rows, sl], _TN) + _dot(kd_ref[rows, sl], dS2)
                    for h, (sl, dS2) in enumerate(zip(sls, dSs))]
            for h, (sl, dS2, dvn) in enumerate(zip(sls, dSs, dvns)):
                ds_ref[c, :, sl] = dS2.astype(ds_ref.dtype)
                dvn_ref[rows, sl] = dvn.astype(dvn_ref.dtype)
                dS_scr[:, sl] = (dS2 * jnp.exp(_lane_col(glast, GDN_H + h))
                                 + _dot(qd_ref[rows, sl], do_ref[rows, sl], _TN) - _dot(w_ref[rows, sl], dvn, _TN))

    blk = pl.BlockSpec((CS * C, D), lambda n: (NB - 1 - n, 0))
    return dict(
        body=body, steps=NB, ins=[w, qd, kd, pm, bg, do],
        in_specs=[blk, blk, blk, pl.BlockSpec((CS, GDN_H * C, C), lambda n: (NB - 1 - n, 0, 0)),
                  pl.BlockSpec((CS * C, 128), lambda n: (NB - 1 - n, 0)), blk],
        out_specs=[blk, pl.BlockSpec((CS, GDN_DK, D), lambda n: (NB - 1 - n, 0, 0))],
        out_shape=[jax.ShapeDtypeStruct((T, D), BF16), jax.ShapeDtypeStruct((N, GDN_DK, D), BF16)],
        scratch=[pltpu.VMEM((GDN_DK, D), F32)])


def _gdn_rest_bwd(qk, v, bg, s_save, t_save, vn, dvn, ds_save, do, ride=None):
    T = qk.shape[0]
    N = T // GDN_C
    C, CS = GDN_C, GDN_REST_CHUNKS
    NB = N // CS
    n_ride = ride.n if ride else 0

    def body(q_ref, k_ref, v_ref, bg_ref, ss_ref, ts_ref, vn_ref, dvn_ref, ds_ref, do_ref, *rest):
        ride_in = rest[:n_ride]
        dqkv_ref, dbg_ref = rest[n_ride:n_ride + 2]
        ride_out = rest[n_ride + 2:2 * n_ride + 2]
        if ride:
            @pl.when(pl.program_id(0) == 0)
            def _():
                ride.start(ride_in, ride_out, rest[-3:])

            @pl.when(pl.program_id(0) == NB - 1)
            def _():
                ride.wait(ride_in, ride_out, rest[-3:])

        items = [(c, h) for c in range(CS) for h in range(GDN_H)]
        toks = [slice(c * C, (c + 1) * C) for c, _ in items]
        sls = [slice(h * 128, (h + 1) * 128) for _, h in items]
        views = [[r.at[pl.ds(c * C, C)] for r in (q_ref, k_ref, v_ref, bg_ref)] for c in range(CS)]
        ts = [_gdn_heads(*views[c], [h])[0] for c, h in items]
        Ss = [ss_ref[c, :, sl] for (c, _), sl in zip(items, sls)]
        Tms = [ts_ref[c, h * C:(h + 1) * C, :] for c, h in items]
        dS2s = [ds_ref[c, :, sl] for (c, _), sl in zip(items, sls)]
        dos = [do_ref[tok, sl] for tok, sl in zip(toks, sls)]
        vns = [vn_ref[tok, sl] for tok, sl in zip(toks, sls)]
        dvns = [dvn_ref[tok, sl] for tok, sl in zip(toks, sls)]
        Qs = [_dot(t["q"], t["k"], _NT) for t in ts]
        dws = [-_dot(dvn, S, _NT) for dvn, S in zip(dvns, Ss)]
        dqds = [_dot(do, S, _NT) for do, S in zip(dos, Ss)]
        dPs = [jnp.where(t["low"], _dot(do, vn, _NT), 0.0) for t, do, vn in zip(ts, dos, vns)]
        dkds = [_dot(vn, dS2, _NT) for vn, dS2 in zip(vns, dS2s)]
        dTs = [_dot(dvn, t["vb"], _NT) + _dot(dw, t["kbg"], _NT) for t, dvn, dw in zip(ts, dvns, dws)]
        dvbs = [_dot(Tm, dvn, _TN) for Tm, dvn in zip(Tms, dvns)]
        dkbgs = [_dot(Tm, dw, _TN) for Tm, dw in zip(Tms, dws)]
        TdTs = [_dot(Tm, dT, _TN) for Tm, dT in zip(Tms, dTs)]
        dLs = [jnp.where(t["strict"], -_dot(TdT, Tm, _NT), 0.0) for t, TdT, Tm in zip(ts, TdTs, Tms)]
        dMs = [dL * t["G"] for t, dL in zip(ts, dLs)]
        dQs = [dP * t["G"] for t, dP in zip(ts, dPs)]
        dkbs = [_dot(dM, t["k"]) + dkbg * t["eg"] for t, dM, dkbg in zip(ts, dMs, dkbgs)]
        rs = lambda a: jnp.sum(a, axis=1, keepdims=True)
        lane = _iota2((C, 128), 1)
        last = _iota2((C, 1), 0) == C - 1
        dbg = [jnp.zeros((C, 128), F32) for _ in range(CS)]
        for i, (c, h) in enumerate(items):
            t, sl, tok = ts[i], sls[i], toks[i]
            E = (dLs[i] * t["M"] + dPs[i] * Qs[i]) * t["G"]
            dqkv_ref[tok, sl] = _dot(dQs[i], t["k"]) + dqds[i] * t["eg"]
            dqkv_ref[tok, D + h * 128:D + (h + 1) * 128] = (
                _dot(dQs[i], t["q"], _TN) + _dot(dMs[i], t["kb"], _TN) + dkds[i] * t["egl"] + dkbs[i] * t["bx"])
            dqkv_ref[tok, 2 * D + h * 128:2 * D + (h + 1) * 128] = dvbs[i] * t["bx"]
            dbeta_c = rs(dkbs[i] * t["k"] + dvbs[i] * t["v"])
            dkd_kd = dkds[i] * t["kd"]
            dgam_c = rs(dqds[i] * t["qd"]) + rs(dkbgs[i] * t["kbg"]) - rs(dkd_kd) + rs(E)
            dgam_r = -jnp.sum(E, axis=0, keepdims=True)
            dgam_c = dgam_c + jnp.sum(jnp.where(t["eye"], dgam_r, 0.0), axis=1, keepdims=True)
            dlast = _sum_all(dkd_kd) + t["eL"] * _sum_all(Ss[i].astype(F32) * dS2s[i].astype(F32))
            dgam_c = dgam_c + jnp.where(last, dlast, 0.0)
            dbg[c] = dbg[c] + jnp.where(lane == h, dbeta_c, 0.0) + jnp.where(lane == GDN_H + h, dgam_c, 0.0)
        for c in range(CS):
            dbg_ref[c * C:(c + 1) * C, :] = dbg[c]

    blk = lambda c: pl.BlockSpec((CS * C, D), lambda n: (n, c))
    st = pl.BlockSpec((CS, GDN_DK, D), lambda n: (n, 0, 0))
    seg = pl.BlockSpec((CS * C, 128), lambda n: (n, 0))
    in_specs = [blk(0), blk(1), blk(0), seg, st,
                pl.BlockSpec((CS, GDN_H * C, C), lambda n: (n, 0, 0)), blk(0), blk(0), st, blk(0)]
    out_specs = [pl.BlockSpec((CS * C, 3 * D), lambda n: (n, 0)), seg]
    out_shape = [jax.ShapeDtypeStruct((T, 3 * D), F32), jax.ShapeDtypeStruct((T, 128), F32)]
    ins = [qk, qk, v, bg, s_save, t_save, vn, dvn, ds_save, do]
    if ride:
        ins, in_specs = ins + ride.srcs, in_specs + ride.specs
        out_shape, out_specs = out_shape + ride.out_shape, out_specs + ride.specs
    res = pl.pallas_call(
        body, grid=(NB,), in_specs=in_specs, out_specs=out_specs, out_shape=out_shape,
        scratch_shapes=ride.scratch if ride else [], name="gdn_rest_bwd",
        compiler_params=_params(("arbitrary",) if ride else ("parallel",)))(*ins)
    return (list(res[:2]), list(res[2:])) if ride else list(res)


def _ssd_seg(al_pair, half, s):
    L = SSM_L
    ri, ci = _iota2((L, L), 0), _iota2((L, L), 1)
    ac = jnp.max(jnp.where(half == s, al_pair, _NEG), axis=1, keepdims=True)
    ar = jnp.sum(jnp.where(ri == ci, ac, 0.0), axis=0, keepdims=True)
    return jnp.exp(jnp.where(ri >= ci, ac - ar, _NEG))


def _last_row(a):
    return jnp.sum(jnp.where(_iota2((a.shape[0], 1), 0) == a.shape[0] - 1, a, 0.0), axis=0, keepdims=True)


def _ssd_expand(da_ref):
    da = da_ref[...]
    return _dot_sel(da, _expand_mats(6, 0)[0]), _dot_sel(da, _expand_mats(6, SSM_H)[0])


def _ssd_core_fwd(xbc, da):
    T = xbc.shape[0]
    L, CS = SSM_L, SSM_SCAN_CHUNKS
    Nc = T // L

    def body(x_all, bc_all, da_all, y_all, hs_all, H_scr):
        @pl.when(pl.program_id(0) == 0)
        def _():
            H_scr[...] = jnp.zeros_like(H_scr)

        for cc in range(CS):
            rows = pl.ds(cc * L, L)
            chunk(x_all.at[rows], bc_all.at[rows], da_all.at[rows], y_all.at[rows], hs_all.at[cc], H_scr)

    def chunk(x_ref, bc_ref, da_ref, y_ref, hs_ref, H_scr):
        dt_ref, al_ref = _ssd_expand(da_ref)
        half = _iota2((L, 128), 1) >> 6
        for g in range(2):
            gs = slice(g * 512, (g + 1) * 512)
            Bg = bc_ref[:, g * 128:(g + 1) * 128]
            Cg = bc_ref[:, 256 + g * 128:256 + (g + 1) * 128]
            alg = al_ref[:, gs]
            alast = _last_row(alg)
            xdt = x_ref[:, gs] * dt_ref[:, gs]
            Hg = H_scr[:, gs]
            hs_ref[:, gs] = Hg
            CB = _dot(Cg, Bg, _NT)
            y_off = jnp.exp(alg) * _dot(Cg, Hg)
            H_scr[:, gs] = Hg * jnp.exp(alast) + _dot(Bg, jnp.exp(alast - alg) * xdt, _TN)
            for j in range(4):
                ps = slice(g * 512 + j * 128, g * 512 + (j + 1) * 128)
                al_pair = al_ref[:, ps]
                xp = x_ref[:, ps] * dt_ref[:, ps]
                ys = [_dot(_ssd_seg(al_pair, half, s) * CB, xp) for s in range(2)]
                y_ref[:, ps] = (y_off[:, j * 128:(j + 1) * 128]
                                + jnp.where(half == 0, ys[0], ys[1])).astype(y_ref.dtype)

    row = pl.BlockSpec((CS * L, D), lambda c: (c, 0))
    return dict(
        body=body, steps=Nc // CS, ins=[xbc, xbc, da],
        in_specs=[row, pl.BlockSpec((CS * L, 512), lambda c: (c, 2)), pl.BlockSpec((CS * L, 128), lambda c: (c, 0))],
        out_specs=[row, pl.BlockSpec((CS, SSM_N, D), lambda c: (c, 0, 0))],
        out_shape=[jax.ShapeDtypeStruct((T, D), BF16), jax.ShapeDtypeStruct((Nc, SSM_N, D), F32)],
        scratch=[pltpu.VMEM((SSM_N, D), F32)])


def _ssd_core_bwd(xbc, da, h_save, dyy, d_x):
    T = xbc.shape[0]
    L, CS = SSM_L, SSM_SCAN_CHUNKS
    Nc = T // L
    NB = Nc // CS

    def body(x_all, bc_all, da_all, hs_all, dy_all, d_ref, dx_all, dda_all, dH_scr, ddt_ref, dal_ref):
        @pl.when(pl.program_id(0) == 0)
        def _():
            dH_scr[...] = jnp.zeros_like(dH_scr)

        for cc in reversed(range(CS)):
            rows = pl.ds(cc * L, L)
            chunk(x_all.at[rows], bc_all.at[rows], da_all.at[rows], hs_all.at[cc], dy_all.at[rows],
                  d_ref, dx_all.at[rows], ddt_ref, dal_ref, dH_scr)
            dda_all[rows, :] = (_dot_sel(ddt_ref[...], _expand_mats(6, 0)[1])
                                + _dot_sel(dal_ref[...], _expand_mats(6, SSM_H)[1]))

    def chunk(x_ref, bc_ref, da_ref, hs_ref, dy_ref, d_ref, dx_ref, ddt_ref, dal_ref, dH_scr):
        dt_ref, al_ref = _ssd_expand(da_ref)
        lane = _iota2((L, 128), 1)
        half = lane >> 6
        rowi = _iota2((L, 1), 0)
        ri, ci = _iota2((L, L), 0), _iota2((L, L), 1)
        for g in range(2):
            gs = slice(g * 512, (g + 1) * 512)
            Bg = bc_ref[:, g * 128:(g + 1) * 128]
            Cg = bc_ref[:, 256 + g * 128:256 + (g + 1) * 128]
            alg = al_ref[:, gs]
            alast = _last_row(alg)
            eal, edec, eL = jnp.exp(alg), jnp.exp(alast - alg), jnp.exp(alast)
            xg, dtg, dYg = x_ref[:, gs], dt_ref[:, gs], dy_ref[:, gs]
            xdt = xg * dtg
            Hg = hs_ref[:, gs]
            dH2 = dH_scr[:, gs]
            CB = _dot(Cg, Bg, _NT)
            dYe = eal * dYg
            dH_scr[:, gs] = dH2 * eL + _dot(Cg, dYe, _TN)
            dC = _dot(dYe, Hg, _NT)
            zg = edec * xdt
            dz = _dot(Bg, dH2)
            dB = _dot(zg, dH2, _NT)
            tz = dz * zg
            dal = dYe * _dot(Cg, Hg) - tz
            dalast = jnp.sum(tz, axis=0, keepdims=True) + eL * jnp.sum(Hg * dH2, axis=0, keepdims=True)
            dal = dal + jnp.where(rowi == L - 1, dalast, 0.0)
            dxdt_g = edec * dz
            dx_ref[:, gs] = dxdt_g * dtg + dYg * d_ref[:, gs]
            ddt_ref[:, gs] = dxdt_g * xg
            dal_ref[:, gs] = dal
            dCB = jnp.zeros((L, L), F32)
            for j in range(4):
                ps = slice(g * 512 + j * 128, g * 512 + (j + 1) * 128)
                al_pair = al_ref[:, ps]
                xp = x_ref[:, ps] * dt_ref[:, ps]
                dYp = dy_ref[:, ps]
                dxp = []
                dal_p = jnp.zeros((L, 128), F32)
                for s in range(2):
                    seg = _ssd_seg(al_pair, half, s)
                    W = seg * CB
                    dW = _dot(jnp.where(half == s, dYp, 0.0), xp, _NT)
                    dxp.append(_dot(W, dYp, _TN))
                    dCB = dCB + dW * seg
                    Es = dW * W
                    dac = jnp.sum(Es, axis=1, keepdims=True) - jnp.sum(
                        jnp.where(ri == ci, jnp.sum(Es, axis=0, keepdims=True), 0.0), axis=1, keepdims=True)
                    dal_p = dal_p + jnp.where(lane == 64 * s, dac, 0.0)
                dxdt_p = jnp.where(half == 0, dxp[0], dxp[1])
                dx_ref[:, ps] += dxdt_p * dt_ref[:, ps]
                ddt_ref[:, ps] += dxdt_p * x_ref[:, ps]
                dal_ref[:, ps] += dal_p
            dx_ref[:, D + g * 128:D + (g + 1) * 128] = dB + _dot(dCB, Cg, _TN)
            dx_ref[:, D + 256 + g * 128:D + 256 + (g + 1) * 128] = dC + _dot(dCB, Bg)

    row = pl.BlockSpec((CS * L, D), lambda c: (NB - 1 - c, 0))
    bcs = pl.BlockSpec((CS * L, 512), lambda c: (NB - 1 - c, 2))
    seg = pl.BlockSpec((CS * L, 128), lambda c: (NB - 1 - c, 0))
    return dict(
        body=body, steps=NB, ins=[xbc, xbc, da, h_save, dyy, d_x],
        in_specs=[row, bcs, seg, pl.BlockSpec((CS, SSM_N, D), lambda c: (NB - 1 - c, 0, 0)), row,
                  pl.BlockSpec((1, D), lambda c: (0, 0))],
        out_specs=[pl.BlockSpec((CS * L, D + 512), lambda c: (NB - 1 - c, 0)), seg],
        out_shape=[jax.ShapeDtypeStruct((T, D + 512), F32), jax.ShapeDtypeStruct((T, 128), F32)],
        scratch=[pltpu.VMEM((SSM_N, D), F32), pltpu.VMEM((L, D), F32), pltpu.VMEM((L, D), F32)])


def _run_scans(parts, *, name):
    steps = parts[0]["steps"]
    assert all(p["steps"] == steps for p in parts)
    cnt = lambda key: [len(p[key]) for p in parts]
    n_in, n_out, n_scr = cnt("ins"), cnt("out_shape"), cnt("scratch")

    def body(*refs):
        ins, outs, scr = refs[:sum(n_in)], refs[sum(n_in):sum(n_in) + sum(n_out)], refs[sum(n_in) + sum(n_out):]
        oi = oo = os_ = 0
        for p, a, b, c in zip(parts, n_in, n_out, n_scr):
            p["body"](*ins[oi:oi + a], *outs[oo:oo + b], *scr[os_:os_ + c])
            oi, oo, os_ = oi + a, oo + b, os_ + c

    cat = lambda key: [v for p in parts for v in p[key]]
    res = pl.pallas_call(
        body, grid=(steps,), in_specs=cat("in_specs"), out_specs=cat("out_specs"), out_shape=cat("out_shape"),
        scratch_shapes=cat("scratch"), name=name, compiler_params=_params(("arbitrary",)))(*cat("ins"))
    out, o = [], 0
    for b in n_out:
        out.append(list(res[o:o + b]))
        o += b
    return out


_EARLY = ("w_out", "wq_mem", "wk_mem", "wv_mem", "wo_mem")
_LATE = ("w_up", "w_down")
_GRADS_MLP = ("w_down", "w_up")
_GRADS_MID = ("wo_mem", "wq_mem", "wk_mem", "wv_mem", "w_out")


def _gather_ride(shards, names):
    return None if shards is None else _Ride([shards[n] for n in names], shard=True)


def _grad_ride(shards, G, names):
    return None if shards is None else _Ride([_slots_from_full(n, G[n]) for n in names], shard=False)


def _local_step(x, mem, tgt, W, shards=None):
    T = x.shape[0]
    W = dict(W)
    cw_qk, cw_v = W["gdn_conv_w"][:, :2 * D], W["gdn_conv_w"][:, 2 * D:]
    h1 = _rmsnorm_fwd(x, W["norm1_w"], name="norm1_fwd")
    ride = _gather_ride(shards, _EARLY)
    pg = _mm(h1, W["w_in_pad"], b_cols=(C_GATE, C_TOT - C_GATE), name="in_proj_gates")
    p = _mm(h1, W["w_in_pad"], b_cols=(0, C_GATE), out_dtype=BF16, bn_cap=1664, name="in_proj", ride=ride)
    if ride:
        p, got = p
        W.update({n: _full_from_slots(n, g) for n, g in zip(_EARLY, got)})
    qk = _conv_fwd(p, C_QKV, 2 * D, cw_qk, None, l2=True, name="gdn_conv_qk_fwd")
    v_g = _conv_fwd(p, C_QKV + 2 * D, D, cw_v, None, l2=False, name="gdn_conv_v_fwd")
    bg = _gdn_gates_fwd(pg, W["gdn_alog_row"], W["gdn_dtb_row"])
    ride = _gather_ride(shards, _LATE)
    prep = _gdn_prep(qk, v_g, bg, ride)
    if ride:
        prep, got = prep
        W.update({n: _full_from_slots(n, g) for n, g in zip(_LATE, got)})
    u_g, w_g, qd_g, kd_g, p_g, t_save = prep
    xbc = _conv_fwd(p, C_XBC, D + 512, W["ssm_conv_w"], W["ssm_conv_b"], l2=False, name="ssm_conv_fwd", bc=512)
    da_s = _ssd_dt_fwd(pg, W["ssm_dtb_row"], W["ssm_alog_row"])
    (o_g, vn_g, s_save), (y_s, h_save) = _run_scans(
        [_gdn_scan_fwd(u_g, w_g, qd_g, kd_g, p_g, bg), _ssd_core_fwd(xbc, da_s)], name="scans_fwd")
    mix = _gdn_post_fwd(o_g, p, W["gdn_norm_x"])
    mix = _ssd_post_fwd(y_s, xbc, p, W["ssm_d_x"], W["ssm_norm_w"].reshape(1, D), mix)
    x1, h2 = _mm(mix, W["w_out"], epi="res_norm", extra=(x, W["norm2_w"]), bm=512, name="out_proj")
    qm = _mm(h2, W["wq_mem"], out_dtype=BF16, name="q_proj")
    m = _rmsnorm_fwd(mem, W["mem_norm_w"], name="mem_norm_fwd")
    km = _mm(m, W["wk_mem"], name="k_proj")
    vm = _mm(m, W["wv_mem"], name="v_proj")
    oa = _attn_fwd(qm, km, vm)
    x2, h3 = _mm(oa, W["wo_mem"], epi="res_norm", extra=(x1, W["norm3_w"]), bm=512, name="o_proj")
    u, act = _mm(h3, W["w_up"], epi="relu2", out_dtype=BF16, name="mlp_up")
    dx3, g_final, loss = _mm(act, W["w_down"], epi="res_loss", extra=(x2, tgt, W["final_norm_w"]), bk_cap=1024,
                             name="mlp_down_loss")
    G = {"final_norm_w": g_final.reshape(D)}
    dpre = _mm(dx3, W["w_down"], dims="nt", epi="mul2", extra=u, out_dtype=BF16, name="mlp_down_dx")
    G["w_down"] = _mm(act, dx3, dims="tn", out_dtype=BF16, name="mlp_down_dw")
    G["w_up"] = _mm(h3, dpre, dims="tn", out_dtype=BF16, name="mlp_up_dw")
    dx2, gw = _mm(dpre, W["w_up"], dims="nt", epi="norm_bwd", extra=(x2, dx3, W["norm3_w"]), bk_cap=1024,
                  name="mlp_up_dx")
    G["norm3_w"] = gw.reshape(D)
    do_a = _mm(dx2, W["wo_mem"], dims="nt", out_dtype=BF16, name="o_proj_dx")
    G["wo_mem"] = _mm(oa, dx2, dims="tn", out_dtype=BF16, name="o_proj_dw")
    dq, dk, dv = _attn_bwd(qm, km, vm, do_a)
    G["wq_mem"] = _mm(h2, dq, dims="tn", out_dtype=BF16, name="q_proj_dw")
    dx1, gw = _mm(dq, W["wq_mem"], dims="nt", epi="norm_bwd", extra=(x1, dx2, W["norm2_w"]), bm=512,
                  name="q_proj_dx")
    G["norm2_w"] = gw.reshape(D)
    G["wk_mem"] = _mm(m, dk, dims="tn", out_dtype=BF16, name="k_proj_dw")
    G["wv_mem"] = _mm(m, dv, dims="tn", out_dtype=BF16, name="v_proj_dw")
    dm = _mm(dk, W["wk_mem"], dims="nt", name="k_proj_dx")
    dm = _mm(dv, W["wv_mem"], dims="nt", epi="res", extra=dm, name="v_proj_dx")
    _, G["mem_norm_w"] = _rmsnorm_bwd(mem, W["mem_norm_w"], dm, None, name="mem_norm_bwd")
    dmix = _mm(dx1, W["w_out"], dims="nt", out_dtype=BF16, name="out_proj_dx")
    G["w_out"] = _mm(mix, dx1, dims="tn", out_dtype=BF16, name="out_proj_dw")
    do_g, dp, G["gdn_norm_x"] = _gdn_post_bwd(dmix, o_g, p, W["gdn_norm_x"])
    dyy, dp, G["ssm_d_x"], G["ssm_norm_w"] = _ssd_post_bwd(dmix, y_s, xbc, p, W["ssm_d_x"],
                                                          W["ssm_norm_w"].reshape(1, D), dp)
    (dvn_g, ds_save), (dxbc, dda_s) = _run_scans(
        [_gdn_scan_bwd(w_g, qd_g, kd_g, p_g, bg, do_g), _ssd_core_bwd(xbc, da_s, h_save, dyy, W["ssm_d_x"])],
        name="scans_bwd")
    ride = _grad_ride(shards, G, _GRADS_MLP)
    rest = _gdn_rest_bwd(qk, v_g, bg, s_save, t_save, vn_g, dvn_g, ds_save, do_g, ride)
    if ride:
        rest, got = rest
        G.update(zip(_GRADS_MLP, got))
    dqkvn, dbg = rest
    dy_qk, gcw_qk, _ = _conv_bwd_act(p, C_QKV, 2 * D, cw_qk, None, dqkvn, 0, l2=True, name="gdn_conv_qk_bwd_act")
    dy_v, gcw_v, _ = _conv_bwd_act(p, C_QKV + 2 * D, D, cw_v, None, dqkvn, 2 * D, l2=False,
                                   name="gdn_conv_v_bwd_act")
    G["gdn_conv_w"] = jnp.concatenate([gcw_qk, gcw_v], axis=1)
    dp = _conv_bwd_in(dy_qk, cw_qk, dp, C_QKV, T, name="gdn_conv_qk_bwd_in")
    dp = _conv_bwd_in(dy_v, cw_v, dp, C_QKV + 2 * D, T, name="gdn_conv_v_bwd_in")
    dp, G["gdn_alog_row"], G["gdn_dtb_row"] = _gdn_gates_bwd(pg, W["gdn_alog_row"], W["gdn_dtb_row"], dbg, dp)
    dy_s, G["ssm_conv_w"], G["ssm_conv_b"] = _conv_bwd_act(p, C_XBC, D + 512, W["ssm_conv_w"], W["ssm_conv_b"],
                                                           dxbc, 0, l2=False, name="ssm_conv_bwd_act", bc=512)
    dp = _conv_bwd_in(dy_s, W["ssm_conv_w"], dp, C_XBC, T, name="ssm_conv_bwd_in", bc=512)
    dp, G["ssm_dtb_row"], G["ssm_alog_row"] = _ssd_dt_bwd(pg, W["ssm_dtb_row"], W["ssm_alog_row"], dda_s, dp)
    ride = _grad_ride(shards, G, _GRADS_MID)
    g_in = _mm(h1, dp, dims="tn", out_dtype=BF16, bn_cap=1152, name="in_proj_dw", ride=ride)
    if ride:
        g_in, got = g_in
        G.update(zip(_GRADS_MID, got))
    G["w_in"] = _unpad_w_in(g_in)
    ride = _grad_ride(shards, G, ("w_in",))
    res = _mm(dp, W["w_in_pad"], dims="nt", epi="norm_bwd", extra=(x, dx1, W["norm1_w"]),
              name="in_proj_dx", ride=ride)
    if ride:
        res, got = res
        G["w_in"] = got[0]
    dx, gw = res
    G["norm1_w"] = gw.reshape(D)
    return loss, dx, G


def _all_gather(shards, out_dtype, *, name):
    n = len(shards)

    def body(*refs):
        x_refs, out_refs, stage = refs[:n], refs[n:2 * n], refs[2 * n:3 * n]
        send_sems, recv_sems, local_sems = refs[3 * n:]
        x, y, c = _place()
        me, sibling = (x, y, c), (x, y, 1 - c)
        chips = [(1 - x, y), (x, 1 - y), (1 - x, 1 - y)]

        def slot(px, py, pc):
            return 4 * px + 2 * py + pc

        def copy(a, k, block, to, src=None):
            dst = out_refs[a].at[slot(*block)]
            return pltpu.make_async_remote_copy(
                src_ref=dst if src is None else src, dst_ref=dst, send_sem=send_sems.at[a, k],
                recv_sem=recv_sems.at[a, k], device_id=to, device_id_type=_MESH)

        for a in range(n):
            stage[a][...] = x_refs[a][...].astype(out_dtype)
        mine = [pltpu.make_async_copy(stage[a], out_refs[a].at[slot(*me)], local_sems.at[a]) for a in range(n)]
        for cp in mine:
            cp.start()
        first = []
        for a in range(n):
            first.append(copy(a, 0, me, sibling, src=stage[a]))
            first += [copy(a, 1 + j, me, (*chip, c), src=stage[a]) for j, chip in enumerate(chips)]
        for cp in first:
            cp.start()
        passed = [[copy(a, 4 + j, (*chip, c), sibling) for j, chip in enumerate(chips)] for a in range(n)]
        for j, chip in enumerate(chips):
            for a in range(n):
                copy(a, 1 + j, (*chip, c), me).wait_recv()
                passed[a][j].start()
        for a in range(n):
            copy(a, 0, sibling, me).wait_recv()
            for j, chip in enumerate(chips):
                copy(a, 4 + j, (*chip, 1 - c), me).wait_recv()
        for cp in first + [cp for row in passed for cp in row]:
            cp.wait_send()
        for cp in mine:
            cp.wait()

    outs = pl.pallas_call(
        body, in_specs=[_VM] * n, out_specs=[_ANY] * n,
        out_shape=[jax.ShapeDtypeStruct((N_DEV,) + s.shape, out_dtype) for s in shards],
        scratch_shapes=[pltpu.VMEM(s.shape, out_dtype) for s in shards]
        + [pltpu.SemaphoreType.DMA((n, 7)), pltpu.SemaphoreType.DMA((n, 7)), pltpu.SemaphoreType.DMA((n,))],
        name=name, compiler_params=pltpu.CompilerParams(vmem_limit_bytes=VMEM_LIMIT))(*shards)
    return list(outs)


def _cast_bf16(arrs, *, name):
    n = len(arrs)

    def body(*refs):
        for a in range(n):
            refs[n + a][...] = refs[a][...].astype(BF16)

    return list(pl.pallas_call(
        body, in_specs=[_VM] * n, out_specs=[_VM] * n,
        out_shape=[jax.ShapeDtypeStruct(s.shape, BF16) for s in arrs], name=name,
        compiler_params=pltpu.CompilerParams(vmem_limit_bytes=VMEM_LIMIT))(*arrs))


def _sum8(a, *, name):
    _, R, Cc = a.shape
    br = _pick_rows(R, 128)

    def body(a_ref, o_ref):
        s = a_ref[0].astype(F32)
        for k in range(1, N_DEV):
            s = s + a_ref[k].astype(F32)
        o_ref[...] = s

    return pl.pallas_call(
        body, grid=(R // br,), in_specs=[pl.BlockSpec((N_DEV, br, Cc), lambda i: (0, i, 0))],
        out_specs=pl.BlockSpec((br, Cc), lambda i: (i, 0)), out_shape=jax.ShapeDtypeStruct((R, Cc), F32),
        name=name, compiler_params=_params(("parallel",)))(a)


def _pick_rows(R, cap):
    if R <= cap:
        return R
    for d in range(cap, 7, -8):
        if R % d == 0:
            return d
    return R


def _adamw(w, g, m, v, *, name):
    shape = w.shape
    as2d = (lambda t: t.reshape(1, -1)) if w.ndim == 1 else (lambda t: t)
    w2, m2, v2 = as2d(w), as2d(m), as2d(v)
    R, Cc = w2.shape
    from_slabs = g.ndim == 3
    br = _pick_rows(R, 128 if from_slabs else 256)
    c1 = 1.0 - ADAM_B1 ** ADAM_STEP
    c2 = 1.0 - ADAM_B2 ** ADAM_STEP

    def body(w_ref, g_ref, m_ref, v_ref, go_ref, d_ref, nm_ref, nv_ref):
        if from_slabs:
            gv = g_ref[0].astype(F32)
            for k in range(1, N_DEV):
                gv = gv + g_ref[k].astype(F32)
        else:
            gv = g_ref[...]
        go_ref[...] = gv
        nm = ADAM_B1 * m_ref[...] + (1.0 - ADAM_B1) * gv
        nv = ADAM_B2 * v_ref[...] + (1.0 - ADAM_B2) * (gv * gv)
        nm_ref[...] = nm
        nv_ref[...] = nv
        d_ref[...] = -ADAM_LR * ((nm / c1) / (jnp.sqrt(nv / c2) + ADAM_EPS) + ADAM_WD * w_ref[...])

    blk = pl.BlockSpec((br, Cc), lambda i: (i, 0))
    g_spec = pl.BlockSpec((N_DEV, br, Cc), lambda i: (0, i, 0)) if from_slabs else blk
    outs = pl.pallas_call(
        body, grid=(R // br,), in_specs=[blk, g_spec, blk, blk], out_specs=[blk] * 4,
        out_shape=[jax.ShapeDtypeStruct((R, Cc), F32)] * 4, name=name,
        compiler_params=_params(("parallel",)))(w2, g if from_slabs else as2d(g), m2, v2)
    return tuple(o.reshape(shape) for o in outs)


_BIG = ("w_in", "w_out", "wq_mem", "wk_mem", "wv_mem", "wo_mem", "w_up", "w_down")
_COL_SHARDED = ("w_in", "w_up")
_WEIGHTS = ("norm1_w", "w_in", "gdn_conv_w", "gdn_a_log", "gdn_dt_bias", "gdn_norm_w", "ssm_conv_w", "ssm_conv_b",
            "ssm_a_log", "ssm_dt_bias", "ssm_d", "ssm_norm_w", "w_out", "norm2_w", "mem_norm_w", "wq_mem", "wk_mem",
            "wv_mem", "wo_mem", "norm3_w", "w_up", "w_down", "final_norm_w")
_IN_PAD = 112


def _full_from_slots(name, g):
    if name in _COL_SHARDED:
        return jnp.transpose(g, (1, 0, 2)).reshape(g.shape[1], N_DEV * g.shape[2])
    return g.reshape(N_DEV * g.shape[1], g.shape[2])


def _slots_from_full(name, f):
    if name in _COL_SHARDED:
        return jnp.transpose(f.reshape(f.shape[0], N_DEV, f.shape[1] // N_DEV), (1, 0, 2))
    return f.reshape(N_DEV, f.shape[0] // N_DEV, f.shape[1])


def _pad_w_in(w):
    z = jnp.zeros((w.shape[0], _IN_PAD), w.dtype)
    return jnp.concatenate([w[:, :4096], w[:, 4112:6672], w[:, 4096:4112], z, w[:, 6672:6688], z], axis=1)


def _unpad_w_in(gp):
    return jnp.concatenate([gp[:, :4096], gp[:, C_GATE:C_GATE + 16], gp[:, 4096:C_GATE], gp[:, C_DT:C_DT + 16]],
                           axis=1)


def _pack_rows(vals):
    rows, offs, r = [], [], 0
    for vflat in vals:
        nrow = 8 * -(-vflat.shape[0] // 1024)
        rows.append(jnp.pad(vflat, (0, nrow * 128 - vflat.shape[0])).reshape(nrow, 128))
        offs.append((r, vflat.shape[0]))
        r += nrow
    return jnp.concatenate(rows, axis=0), offs


def _unpack_rows(packed, offs, shapes):
    out = []
    for (r, nel), shp in zip(offs, shapes):
        nrow = -(-nel // 128)
        out.append(packed[r:r + nrow].reshape(-1)[:nel].reshape(shp))
    return out


def kernel(x, mem, norm1_w, w_in, gdn_conv_w, gdn_a_log, gdn_dt_bias, gdn_norm_w, ssm_conv_w, ssm_conv_b, ssm_a_log, ssm_dt_bias, ssm_d, ssm_norm_w, w_out, norm2_w, mem_norm_w, wq_mem, wk_mem, wv_mem, wo_mem, norm3_w, w_up, w_down, final_norm_w, loss_target, m_norm1_w, m_w_in, m_gdn_conv_w, m_gdn_a_log, m_gdn_dt_bias, m_gdn_norm_w, m_ssm_conv_w, m_ssm_conv_b, m_ssm_a_log, m_ssm_dt_bias, m_ssm_d, m_ssm_norm_w, m_w_out, m_norm2_w, m_mem_norm_w, m_wq_mem, m_wk_mem, m_wv_mem, m_wo_mem, m_norm3_w, m_w_up, m_w_down, m_final_norm_w, v_norm1_w, v_w_in, v_gdn_conv_w, v_gdn_a_log, v_gdn_dt_bias, v_gdn_norm_w, v_ssm_conv_w, v_ssm_conv_b, v_ssm_a_log, v_ssm_dt_bias, v_ssm_d, v_ssm_norm_w, v_w_out, v_norm2_w, v_mem_norm_w, v_wq_mem, v_wk_mem, v_wv_mem, v_wo_mem, v_norm3_w, v_w_up, v_w_down, v_final_norm_w):
    args = dict(locals())
    w_loc = {n: args[n] for n in _WEIGHTS}
    me = 4 * lax.axis_index("x") + 2 * lax.axis_index("y") + lax.axis_index("c")

    w_in_full = _full_from_slots("w_in", _all_gather([w_in], BF16, name="gather_w_in")[0])
    later = _EARLY + _LATE
    shards = dict(zip(later, _cast_bf16([w_loc[n] for n in later], name="cast_shards")))
    conv_pack, conv_offs = _pack_rows([gdn_conv_w.reshape(-1), ssm_conv_w.reshape(-1)])
    conv_all = _all_gather([conv_pack], F32, name="gather_conv")[0]
    gdn_cw, ssm_cw = [], []
    for k in range(N_DEV):
        a, b = _unpack_rows(conv_all[k], conv_offs, [gdn_conv_w.shape, ssm_conv_w.shape])
        gdn_cw.append(a)
        ssm_cw.append(b)
    W = {
        "w_in_pad": _pad_w_in(w_in_full),
        "norm1_w": norm1_w, "norm2_w": norm2_w, "norm3_w": norm3_w, "mem_norm_w": mem_norm_w,
        "final_norm_w": final_norm_w, "ssm_norm_w": ssm_norm_w, "ssm_conv_b": ssm_conv_b,
        "gdn_conv_w": jnp.concatenate(gdn_cw, axis=1), "ssm_conv_w": jnp.concatenate(ssm_cw, axis=1),
        "gdn_alog_row": jnp.pad(gdn_a_log, (GDN_H, 128 - 2 * GDN_H)).reshape(1, 128),
        "gdn_dtb_row": jnp.pad(gdn_dt_bias, (GDN_H, 128 - 2 * GDN_H)).reshape(1, 128),
        "gdn_norm_x": jnp.tile(gdn_norm_w, GDN_H).reshape(1, D),
        "ssm_dtb_row": jnp.pad(ssm_dt_bias, (0, 128 - SSM_H)).reshape(1, 128),
        "ssm_alog_row": jnp.pad(ssm_a_log, (0, 128 - SSM_H)).reshape(1, 128),
        "ssm_d_x": jnp.repeat(ssm_d, SSM_P).reshape(1, D),
    }

    loss_part, grad_x, G = _local_step(x[0], mem[0], loss_target[0], W, shards)

    grads = {n: G[n] for n in _BIG}

    small = {
        "norm1_w": G["norm1_w"], "gdn_conv_w": G["gdn_conv_w"], "gdn_a_log": G["gdn_alog_row"][0, GDN_H:2 * GDN_H],
        "gdn_dt_bias": G["gdn_dtb_row"][0, GDN_H:2 * GDN_H], "gdn_norm_w": G["gdn_norm_x"].reshape(GDN_H, 128).sum(0),
        "ssm_conv_w": G["ssm_conv_w"], "ssm_conv_b": G["ssm_conv_b"],
        "ssm_a_log": G["ssm_alog_row"][0, :SSM_H], "ssm_dt_bias": G["ssm_dtb_row"][0, :SSM_H],
        "ssm_d": G["ssm_d_x"].reshape(SSM_H, SSM_P).sum(1), "ssm_norm_w": G["ssm_norm_w"].reshape(D),
        "norm2_w": G["norm2_w"], "mem_norm_w": G["mem_norm_w"], "norm3_w": G["norm3_w"],
        "final_norm_w": G["final_norm_w"], "loss": loss_part[0, :1],
    }
    names = list(small)
    pack, offs = _pack_rows([small[n].reshape(-1) for n in names])
    tot = _sum8(_all_gather([pack], F32, name="gather_small")[0], name="sum_small")
    summed = dict(zip(names, _unpack_rows(tot, offs, [small[n].shape for n in names])))
    loss = summed.pop("loss")[0]
    for n in ("gdn_conv_w", "ssm_conv_w"):
        width = w_loc[n].shape[1]
        summed[n] = lax.dynamic_slice_in_dim(summed[n], me * width, width, axis=1)
    grads.update(summed)

    upd = {n: _adamw(w_loc[n], grads[n], args["m_" + n], args["v_" + n], name="adamw_" + n) for n in _WEIGHTS}
    return (loss, grad_x[None], *[upd[n][0] for n in _WEIGHTS], *[upd[n][1] for n in _WEIGHTS],
            *[upd[n][2] for n in _WEIGHTS], *[upd[n][3] for n in _WEIGHTS])
```

```python
import jax
import jax.numpy as jnp
from jax import lax
from jax.experimental import pallas as pl
from jax.experimental.pallas import tpu as pltpu

F32 = jnp.float32
BF16 = jnp.bfloat16
_MXU = BF16

D = 1024
EPS = 1e-6
CONV_K = 4
GDN_H, GDN_DK, GDN_C = 8, 128, 64
GDN_SCAN_CHUNKS = 4
GDN_LOCAL_CHUNKS = 4
GDN_REST_CHUNKS = 4
SSM_H, SSM_P, SSM_L, SSM_N = 16, 64, 128, 128
SSM_SCAN_CHUNKS = 2
MEM_H, MEM_HD = 4, 256
D_FF = 4096
N_DEV = 8

C_QKV, C_ZG, C_ZS, C_XBC, C_GATE, C_DT, C_TOT = 0, 3072, 4096, 5120, 6656, 6784, 6912
P_HALO = 16

ADAM_LR, ADAM_B1, ADAM_B2, ADAM_EPS, ADAM_WD, ADAM_STEP = 0.001, 0.9, 0.999, 1e-08, 0.01, 10

VMEM_LIMIT = 56 * 1024 * 1024

_NN = (((1,), (0,)), ((), ()))
_NT = (((1,), (1,)), ((), ()))
_TN = (((0,), (0,)), ((), ()))


def _dot(a, b, dims=_NN):
    return lax.dot_general(a.astype(_MXU), b.astype(_MXU), dims, preferred_element_type=F32)


def _split3(a):
    a1 = a.astype(BF16)
    r1 = a - a1.astype(F32)
    a2 = r1.astype(BF16)
    return a1, a2, (r1 - a2.astype(F32)).astype(BF16)


def _dot_sel(a, e):
    eb = e.astype(BF16)
    return sum(lax.dot_general(p, eb, _NN, preferred_element_type=F32) for p in _split3(a))


def _sel_dot(e, a):
    eb = e.astype(BF16)
    return sum(lax.dot_general(eb, p, _NN, preferred_element_type=F32) for p in _split3(a))


def _chunk_cumsum(a, tri, chunk):
    return jnp.concatenate([_sel_dot(tri, a[r:r + chunk]) for r in range(0, a.shape[0], chunk)], axis=0)


def _params(sem):
    return pltpu.CompilerParams(dimension_semantics=sem, vmem_limit_bytes=VMEM_LIMIT)


def _pick(n, cap):
    for d in range(min(cap, n), 0, -128):
        if n % d == 0 and d % 128 == 0:
            return d
    return n


def _sigmoid(x):
    return 0.5 * jnp.tanh(0.5 * x) + 0.5


def _silu(x):
    return x * _sigmoid(x)


def _dsilu(x):
    s = _sigmoid(x)
    return s * (1.0 + x * (1.0 - s))


def _softplus(x):
    return jnp.maximum(x, 0.0) + jnp.log(1.0 + jnp.exp(-jnp.abs(x)))


def _iota2(shape, axis):
    return lax.broadcasted_iota(jnp.int32, shape, axis)


def _sum_all(x):
    return jnp.sum(jnp.sum(x, axis=1, keepdims=True), axis=0, keepdims=True)


_MESH = pl.DeviceIdType.MESH
_ANY = pl.BlockSpec(memory_space=pl.ANY)
_VM = pl.BlockSpec(memory_space=pltpu.VMEM)
_REL = [(r >> 2 & 1, r >> 1 & 1, r & 1) for r in range(1, N_DEV)]


def _place():
    return lax.axis_index("x"), lax.axis_index("y"), lax.axis_index("c")


class _Ride:
    def __init__(self, srcs, shard):
        self.srcs, self.shard, self.n = list(srcs), shard, len(srcs)
        self.out_shape = [jax.ShapeDtypeStruct(((N_DEV,) + s.shape) if shard else s.shape, s.dtype)
                          for s in self.srcs]
        self.specs = [_ANY] * self.n
        self.scratch = [pltpu.SemaphoreType.DMA((self.n, N_DEV - 1)), pltpu.SemaphoreType.DMA((self.n, N_DEV - 1)),
                        pltpu.SemaphoreType.DMA((self.n,))]

    def _copies(self, in_refs, out_refs, sems):
        send, recv, loc = sems
        x, y, c = _place()
        me = 4 * x + 2 * y + c
        local, remote, arrive = [], [], []
        for a in range(self.n):
            src = in_refs[a] if self.shard else in_refs[a].at[me]
            local.append(pltpu.make_async_copy(src, out_refs[a].at[me], loc.at[a]))
        for k, (rx, ry, rc) in enumerate(_REL):
            peer = (lax.rem(x + rx, 2), lax.rem(y + ry, 2), lax.rem(c + rc, 2))
            ps = 4 * peer[0] + 2 * peer[1] + peer[2]
            for a in range(self.n):
                src = in_refs[a] if self.shard else in_refs[a].at[ps]
                remote.append(pltpu.make_async_remote_copy(
                    src_ref=src, dst_ref=out_refs[a].at[me], send_sem=send.at[a, k], recv_sem=recv.at[a, k],
                    device_id=peer, device_id_type=_MESH))
                slot = out_refs[a].at[ps]
                arrive.append(pltpu.make_async_remote_copy(
                    src_ref=slot, dst_ref=slot, send_sem=send.at[a, k], recv_sem=recv.at[a, k],
                    device_id=peer, device_id_type=_MESH))
        return local, remote, arrive

    def start(self, in_refs, out_refs, sems):
        local, remote, _ = self._copies(in_refs, out_refs, sems)
        for cp in local + remote:
            cp.start()

    def wait(self, in_refs, out_refs, sems):
        local, remote, arrive = self._copies(in_refs, out_refs, sems)
        for cp in arrive:
            cp.wait_recv()
        for cp in remote:
            cp.wait_send()
        for cp in local:
            cp.wait()


_EPI = {
    "none": ((), ("tile",)),
    "res": (("tile",), ("tile",)),
    "mul2": (("tile",), ("tile",)),
    "relu2": ((), ("tile", "tile")),
    "res_norm": (("tile", "row"), ("tile", "tile")),
    "norm_bwd": (("tile", "tile", "row"), ("tile", "row")),
    "res_loss": (("tile", "tile", "row"), ("tile", "row", "row")),
}


def _mm(a, b, *, dims="nn", epi="none", extra=(), out_dtype=F32, name, bm=1024, bn_cap=1024, bk_cap=2048,
        ride=None, b_cols=None):
    if dims == "nn":
        (M, K), (K2, N) = a.shape, b.shape
    elif dims == "nt":
        (M, K), (N, K2) = a.shape, b.shape
    else:
        (K, M), (K2, N) = a.shape, b.shape
    jb0 = 0
    if b_cols is not None:
        N = b_cols[1]
    assert K == K2, (a.shape, b.shape, dims)
    bm = _pick(M, bm)
    bn = _pick(N, bn_cap)
    bk = _pick(K, bk_cap)
    nk = K // bk
    if b_cols is not None:
        assert dims == "nn" and b_cols[0] % bn == 0
        jb0 = b_cols[0] // bn
    dn = {"nn": _NN, "nt": _NT, "tn": _TN}[dims]
    a_spec = (pl.BlockSpec((bk, bm), lambda i, j, k: (k, i)) if dims == "tn"
              else pl.BlockSpec((bm, bk), lambda i, j, k: (i, k)))
    b_spec = (pl.BlockSpec((bn, bk), lambda i, j, k: (j, k)) if dims == "nt"
              else pl.BlockSpec((bk, bn), lambda i, j, k: (k, j + jb0)))
    o_spec = pl.BlockSpec((bm, bn), lambda i, j, k: (i, j))
    r_spec = pl.BlockSpec((1, bn), lambda i, j, k: (0, j))
    extra = list(extra) if isinstance(extra, (tuple, list)) else [extra]
    ekinds, okinds = _EPI[epi]
    assert len(extra) == len(ekinds) and (epi not in ("res_norm", "norm_bwd", "res_loss") or bn == N)
    n_extra, n_out = len(ekinds), len(okinds)
    n_ride = ride.n if ride else 0
    gi, gj = M // bm, N // bn

    def body(a_ref, b_ref, *rest):
        ex = rest[:n_extra]
        first = pl.program_id(0) == 0
        ride_in = rest[n_extra:n_extra + n_ride]
        outs = rest[n_extra + n_ride:n_extra + n_ride + n_out]
        ride_out = rest[n_extra + n_ride + n_out:n_extra + 2 * n_ride + n_out]
        if ride:
            at = lambda i, j, k: ((pl.program_id(0) == i) & (pl.program_id(1) == j) & (pl.program_id(2) == k))

            @pl.when(at(0, 0, 0))
            def _():
                ride.start(ride_in, ride_out, rest[-3:])

        def finish(r):
            if epi == "res":
                outs[0][...] = (r + ex[0][...].astype(F32)).astype(outs[0].dtype)
            elif epi == "mul2":
                outs[0][...] = (2.0 * r * ex[0][...].astype(F32)).astype(outs[0].dtype)
            elif epi == "relu2":
                u = jnp.maximum(r, 0.0)
                outs[0][...] = u.astype(outs[0].dtype)
                outs[1][...] = (u * u).astype(outs[1].dtype)
            elif epi == "res_norm":
                y = r + ex[0][...]
                outs[0][...] = y
                rstd = lax.rsqrt(jnp.mean(y * y, axis=1, keepdims=True) + EPS)
                outs[1][...] = (y * rstd * ex[1][...]).astype(outs[1].dtype)
            elif epi == "norm_bwd":
                xv = ex[0][...]
                rstd = lax.rsqrt(jnp.mean(xv * xv, axis=1, keepdims=True) + EPS)
                xh = xv * rstd
                dxh = r * ex[2][...]
                outs[0][...] = ex[1][...] + rstd * (dxh - xh * jnp.mean(dxh * xh, axis=1, keepdims=True))
                dw = jnp.sum(r * xh, axis=0, keepdims=True)

                @pl.when(first)
                def _():
                    outs[1][...] = dw

                @pl.when(jnp.logical_not(first))
                def _():
                    outs[1][...] += dw
            elif epi == "res_loss":
                y = r + ex[0][...]
                wv = ex[2][...]
                rstd = lax.rsqrt(jnp.mean(y * y, axis=1, keepdims=True) + EPS)
                yh = y * rstd
                err = yh * wv - ex[1][...]
                part_loss = 0.5 * jnp.sum(jnp.mean(err * err, axis=1, keepdims=True), axis=0, keepdims=True)
                dyn = err * (1.0 / N)
                dyh = dyn * wv
                outs[0][...] = rstd * (dyh - yh * jnp.mean(dyh * yh, axis=1, keepdims=True))
                dw = jnp.sum(dyn * yh, axis=0, keepdims=True)
                lrow = jnp.broadcast_to(part_loss, (1, N))

                @pl.when(first)
                def _():
                    outs[1][...] = dw
                    outs[2][...] = lrow

                @pl.when(jnp.logical_not(first))
                def _():
                    outs[1][...] += dw
                    outs[2][...] += lrow
            else:
                outs[0][...] = r.astype(outs[0].dtype)

        part = _dot(a_ref[...], b_ref[...], dn)
        if nk == 1:
            finish(part)
        else:
            acc = rest[n_extra + 2 * n_ride + n_out]
            k = pl.program_id(2)

            @pl.when(k == 0)
            def _():
                acc[...] = part

            @pl.when((k > 0) & (k < nk - 1))
            def _():
                acc[...] += part

            @pl.when(k == nk - 1)
            def _():
                finish(acc[...] + part)

        if ride:
            @pl.when(at(gi - 1, gj - 1, nk - 1))
            def _():
                ride.wait(ride_in, ride_out, rest[-3:])

    kind_spec = {"tile": o_spec, "row": r_spec}
    ins = [a, b] + [e.reshape(1, N) if k == "row" else e for e, k in zip(extra, ekinds)]
    in_specs = [a_spec, b_spec] + [kind_spec[k] for k in ekinds]
    out_dtypes = {"res_norm": (F32, BF16), "norm_bwd": (F32, F32), "res_loss": (F32, F32, F32)}.get(
        epi, (out_dtype,) * n_out)
    out_shape = [jax.ShapeDtypeStruct((M, N) if k == "tile" else (1, N), dt) for k, dt in zip(okinds, out_dtypes)]
    out_specs = [kind_spec[k] for k in okinds]
    scratch = [pltpu.VMEM((bm, bn), F32)] if nk > 1 else []
    sem = ("arbitrary" if epi in ("norm_bwd", "res_loss") else "parallel", "parallel", "arbitrary")
    if ride:
        ins, in_specs = ins + ride.srcs, in_specs + ride.specs
        out_shape, out_specs = out_shape + ride.out_shape, out_specs + ride.specs
        scratch, sem = scratch + ride.scratch, ("arbitrary",) * 3
    res = pl.pallas_call(
        body, grid=(gi, gj, nk), in_specs=in_specs, out_specs=out_specs, out_shape=out_shape,
        scratch_shapes=scratch, name=name, compiler_params=_params(sem))(*ins)
    main = res[:n_out] if n_out > 1 else res[0]
    return (main, list(res[n_out:])) if ride else main


def _rmsnorm_fwd(x, w, *, name, bt=256):
    T, Dm = x.shape
    bt = min(bt, T)

    def body(x_ref, w_ref, h_ref):
        xv = x_ref[...]
        r = lax.rsqrt(jnp.mean(xv * xv, axis=1, keepdims=True) + EPS)
        h_ref[...] = (xv * r * w_ref[...]).astype(h_ref.dtype)

    return pl.pallas_call(
        body, grid=(T // bt,),
        in_specs=[pl.BlockSpec((bt, Dm), lambda i: (i, 0)), pl.BlockSpec((1, Dm), lambda i: (0, 0))],
        out_specs=pl.BlockSpec((bt, Dm), lambda i: (i, 0)),
        out_shape=jax.ShapeDtypeStruct((T, Dm), BF16), name=name,
        compiler_params=_params(("parallel",)))(x, w.reshape(1, Dm))


def _rmsnorm_bwd(x, w, dh, dres, *, name, bt=256):
    T, Dm = x.shape
    bt = min(bt, T)
    has_res = dres is not None

    def body(x_ref, w_ref, dh_ref, *rest):
        dres_ref = rest[0] if has_res else None
        dx_ref, dw_ref = rest[-2], rest[-1]
        i = pl.program_id(0)
        xv = x_ref[...]
        r = lax.rsqrt(jnp.mean(xv * xv, axis=1, keepdims=True) + EPS)
        xh = xv * r
        dhv = dh_ref[...].astype(F32)
        dxh = dhv * w_ref[...]
        dx = r * (dxh - xh * jnp.mean(dxh * xh, axis=1, keepdims=True))
        if has_res:
            dx = dx + dres_ref[...]
        dx_ref[...] = dx

        @pl.when(i == 0)
        def _():
            dw_ref[...] = jnp.zeros_like(dw_ref)

        dw_ref[...] += jnp.sum(dhv * xh, axis=0, keepdims=True)

    row = pl.BlockSpec((bt, Dm), lambda i: (i, 0))
    vec = pl.BlockSpec((1, Dm), lambda i: (0, 0))
    ins = [x, w.reshape(1, Dm), dh] + ([dres] if has_res else [])
    dx, dw = pl.pallas_call(
        body, grid=(T // bt,), in_specs=[row, vec, row] + ([row] if has_res else []),
        out_specs=[row, vec],
        out_shape=[jax.ShapeDtypeStruct((T, Dm), F32), jax.ShapeDtypeStruct((1, Dm), F32)],
        name=name, compiler_params=_params(("arbitrary",)))(*ins)
    return dx, dw.reshape(Dm)


def _attn_fwd(q, km, vm, *, bt=256):
    T = q.shape[0]
    M = km.shape[0]
    bt = min(bt, T)
    scale = MEM_HD ** -0.5

    def body(q_ref, k_ref, v_ref, o_ref):
        sls = [slice(h * MEM_HD, (h + 1) * MEM_HD) for h in range(MEM_H)]
        ss = [_dot(q_ref[:, sl], k_ref[:, sl], _NT) * scale for sl in sls]
        es = [jnp.exp(s - jnp.max(s, axis=1, keepdims=True)) for s in ss]
        ps = [e / jnp.sum(e, axis=1, keepdims=True) for e in es]
        for sl, p in zip(sls, ps):
            o_ref[:, sl] = _dot(p, v_ref[:, sl]).astype(o_ref.dtype)

    row = pl.BlockSpec((bt, D), lambda i: (i, 0))
    mem = pl.BlockSpec((M, D), lambda i: (0, 0))
    return pl.pallas_call(
        body, grid=(T // bt,), in_specs=[row, mem, mem], out_specs=row,
        out_shape=jax.ShapeDtypeStruct((T, D), BF16), name="attn_fwd",
        compiler_params=_params(("parallel",)))(q, km, vm)


def _attn_bwd(q, km, vm, do, *, bt=256):
    T = q.shape[0]
    M = km.shape[0]
    bt = min(bt, T)
    scale = MEM_HD ** -0.5

    def body(q_ref, k_ref, v_ref, do_ref, dq_ref, dk_ref, dv_ref):
        i = pl.program_id(0)

        @pl.when(i == 0)
        def _():
            dk_ref[...] = jnp.zeros_like(dk_ref)
            dv_ref[...] = jnp.zeros_like(dv_ref)

        sls = [slice(h * MEM_HD, (h + 1) * MEM_HD) for h in range(MEM_H)]
        ss = [_dot(q_ref[:, sl], k_ref[:, sl], _NT) * scale for sl in sls]
        dps = [_dot(do_ref[:, sl], v_ref[:, sl], _NT) for sl in sls]
        es = [jnp.exp(s - jnp.max(s, axis=1, keepdims=True)) for s in ss]
        ps = [e / jnp.sum(e, axis=1, keepdims=True) for e in es]
        dss = [p * (dp - jnp.sum(dp * p, axis=1, keepdims=True)) * scale for p, dp in zip(ps, dps)]
        for sl, p, ds in zip(sls, ps, dss):
            dq_ref[:, sl] = _dot(ds, k_ref[:, sl]).astype(dq_ref.dtype)
            dk_ref[:, sl] += _dot(ds, q_ref[:, sl], _TN)
            dv_ref[:, sl] += _dot(p, do_ref[:, sl], _TN)

    row = pl.BlockSpec((bt, D), lambda i: (i, 0))
    mem = pl.BlockSpec((M, D), lambda i: (0, 0))
    return pl.pallas_call(
        body, grid=(T // bt,), in_specs=[row, mem, mem, row], out_specs=[row, mem, mem],
        out_shape=[jax.ShapeDtypeStruct((T, D), BF16), jax.ShapeDtypeStruct((M, D), F32),
                   jax.ShapeDtypeStruct((M, D), F32)],
        name="attn_bwd", compiler_params=_params(("arbitrary",)))(q, km, vm, do)


def _conv_apply(halo, x, w_ref, b_ref):
    bt, hr = x.shape[0], halo.shape[0]
    cat = jnp.concatenate([halo, x], axis=0)
    y = x * w_ref[3:4, :]
    for k in range(CONV_K - 1):
        y = y + pltpu.roll(cat, CONV_K - 1 - k, 0)[hr:hr + bt] * w_ref[k:k + 1, :]
    if b_ref is not None:
        y = y + b_ref[...]
    return y


def _l2_parts(act, bc):
    out = []
    for s in range(bc // 128):
        a = act[:, s * 128:(s + 1) * 128]
        r = lax.rsqrt(jnp.sum(a * a, axis=1, keepdims=True) + EPS)
        out.append((a, r))
    return out


def _conv_fwd(p, col0, C, w, b, *, l2, name, bt=512, bc=1024):
    T = p.shape[0]
    bt = min(bt, T)
    c0, hb = col0 // bc, bt // P_HALO
    has_b = b is not None
    assert not l2 or (bc == D and C == 2 * D)

    def body(x_ref, halo_ref, w_ref, *rest):
        b_ref = rest[0] if has_b else None
        o_ref = rest[-1]
        i, j = pl.program_id(0), pl.program_id(1)
        x = x_ref[...].astype(F32)
        halo = jnp.where(i > 0, halo_ref[...].astype(F32), 0.0)
        act = _silu(_conv_apply(halo, x, w_ref, b_ref))
        if l2:
            sc = jnp.where(j == 0, GDN_DK ** -0.5, 1.0)
            o_ref[...] = jnp.concatenate([a * (r * sc) for a, r in _l2_parts(act, bc)], axis=1)
        else:
            o_ref[...] = act

    in_specs = [pl.BlockSpec((bt, bc), lambda i, j: (i, c0 + j)),
                pl.BlockSpec((P_HALO, bc), lambda i, j: (jnp.maximum(i * hb - 1, 0), c0 + j)),
                pl.BlockSpec((CONV_K, bc), lambda i, j: (0, j))]
    ins = [p, p, w]
    if has_b:
        in_specs.append(pl.BlockSpec((1, bc), lambda i, j: (0, j)))
        ins.append(b.reshape(1, C))
    return pl.pallas_call(
        body, grid=(T // bt, C // bc), in_specs=in_specs,
        out_specs=pl.BlockSpec((bt, bc), lambda i, j: (i, j)),
        out_shape=jax.ShapeDtypeStruct((T, C), F32), name=name,
        compiler_params=_params(("parallel", "parallel")))(*ins)


def _conv_bwd_act(p, col0, C, w, b, dact, dcol0, *, l2, name, bt=512, bc=1024):
    T = p.shape[0]
    bt = min(bt, T)
    c0, d0, hb = col0 // bc, dcol0 // bc, bt // P_HALO
    has_b = b is not None
    assert not l2 or (bc == D and C == 2 * D)

    def body(x_ref, halo_ref, w_ref, *rest):
        b_ref = rest[0] if has_b else None
        dact_ref, dy_ref, dw_ref, db_ref = rest[-4:]
        j, i = pl.program_id(0), pl.program_id(1)
        x = x_ref[...].astype(F32)
        halo = jnp.where(i > 0, halo_ref[...].astype(F32), 0.0)
        y = _conv_apply(halo, x, w_ref, b_ref)
        dact = dact_ref[...]
        sg = _sigmoid(y)
        if l2:
            sc = jnp.where(j == 0, GDN_DK ** -0.5, 1.0)
            parts = []
            for s, (a, r) in enumerate(_l2_parts(y * sg, bc)):
                n = a * r
                dn = dact[:, s * 128:(s + 1) * 128]
                parts.append((r * sc) * (dn - n * jnp.sum(dn * n, axis=1, keepdims=True)))
            dact = jnp.concatenate(parts, axis=1)
        dy = dact * (sg * (1.0 + y * (1.0 - sg)))
        dy_ref[...] = dy

        @pl.when(i == 0)
        def _():
            dw_ref[...] = jnp.zeros_like(dw_ref)
            db_ref[...] = jnp.zeros_like(db_ref)

        db_ref[...] += jnp.sum(dy, axis=0, keepdims=True)
        cat = jnp.concatenate([halo, x], axis=0)
        dw_ref[3:4, :] += jnp.sum(dy * x, axis=0, keepdims=True)
        for k in range(CONV_K - 1):
            xs = pltpu.roll(cat, CONV_K - 1 - k, 0)[P_HALO:P_HALO + bt]
            dw_ref[k:k + 1, :] += jnp.sum(dy * xs, axis=0, keepdims=True)

    in_specs = [pl.BlockSpec((bt, bc), lambda j, i: (i, c0 + j)),
                pl.BlockSpec((P_HALO, bc), lambda j, i: (jnp.maximum(i * hb - 1, 0), c0 + j)),
                pl.BlockSpec((CONV_K, bc), lambda j, i: (0, j))]
    ins = [p, p, w]
    if has_b:
        in_specs.append(pl.BlockSpec((1, bc), lambda j, i: (0, j)))
        ins.append(b.reshape(1, C))
    in_specs.append(pl.BlockSpec((bt, bc), lambda j, i: (i, d0 + j)))
    ins.append(dact)
    dy, dw, db = pl.pallas_call(
        body, grid=(C // bc, T // bt), in_specs=in_specs,
        out_specs=[pl.BlockSpec((bt, bc), lambda j, i: (i, j)),
                   pl.BlockSpec((CONV_K, bc), lambda j, i: (0, j)),
                   pl.BlockSpec((1, bc), lambda j, i: (0, j))],
        out_shape=[jax.ShapeDtypeStruct((T, C), F32), jax.ShapeDtypeStruct((CONV_K, C), F32),
                   jax.ShapeDtypeStruct((1, C), F32)],
        name=name, compiler_params=_params(("parallel", "arbitrary")))(*ins)
    return dy, dw, db.reshape(C)


def _conv_bwd_in(dy, w, dp_in, col0, T, *, name, bt=512, bc=1024):
    C = dy.shape[1]
    bt = min(bt, T)
    c0, hb, nb = col0 // bc, bt // 8, T // bt

    def body(dy_ref, nxt_ref, w_ref, *rest):
        o_ref = rest[-1]
        i = pl.program_id(0)
        dy_v = dy_ref[...]
        nxt = jnp.where(i < nb - 1, nxt_ref[...], 0.0)
        cat = jnp.concatenate([dy_v, nxt], axis=0)
        dx = dy_v * w_ref[3:4, :]
        for k in range(CONV_K - 1):
            s = CONV_K - 1 - k
            dx = dx + pltpu.roll(cat, bt + 8 - s, 0)[0:bt] * w_ref[k:k + 1, :]
        o_ref[...] = dx.astype(o_ref.dtype)

    in_specs = [pl.BlockSpec((bt, bc), lambda i, j: (i, j)),
                pl.BlockSpec((8, bc), lambda i, j: (jnp.minimum((i + 1) * hb, T // 8 - 1), j)),
                pl.BlockSpec((CONV_K, bc), lambda i, j: (0, j))]
    ins = [dy, dy, w]
    alias = {}
    if dp_in is not None:
        in_specs.append(pl.BlockSpec(memory_space=pl.ANY))
        ins.append(dp_in)
        alias = {3: 0}
    return pl.pallas_call(
        body, grid=(nb, C // bc), in_specs=in_specs,
        out_specs=pl.BlockSpec((bt, bc), lambda i, j: (i, c0 + j)),
        out_shape=jax.ShapeDtypeStruct((T, C_TOT), BF16), input_output_aliases=alias, name=name,
        compiler_params=_params(("parallel", "parallel")))(*ins)


def _expand_mats(shift, row0):
    e = (_iota2((128, D), 0) - row0 == (_iota2((128, D), 1) >> shift)).astype(F32)
    et = ((_iota2((D, 128), 0) >> shift) == _iota2((D, 128), 1) - row0).astype(F32)
    return e, et


def _cum_mats(chunk):
    ri, ci = _iota2((chunk, chunk), 0), _iota2((chunk, chunk), 1)
    return (ri >= ci).astype(F32), (ri <= ci).astype(F32)


def _gdn_gates_fwd(p, alog_row, dtb_row, *, bt=256):
    T = p.shape[0]
    bt = min(bt, T)

    def body(g_ref, al_ref, db_ref, bg_ref):
        gt = g_ref[...]
        lc, _ = _cum_mats(GDN_C)
        g_l = -jnp.exp(al_ref[...]) * _softplus(gt + db_ref[...])
        bg_ref[...] = jnp.where(_iota2((bt, 128), 1) < GDN_H, _sigmoid(gt), _chunk_cumsum(g_l, lc, GDN_C))

    vec = pl.BlockSpec((1, 128), lambda i: (0, 0))
    seg = pl.BlockSpec((bt, 128), lambda i: (i, 0))
    return pl.pallas_call(
        body, grid=(T // bt,), in_specs=[seg, vec, vec], out_specs=seg,
        out_shape=jax.ShapeDtypeStruct((T, 128), F32), name="gdn_gates_fwd",
        compiler_params=_params(("parallel",)))(p, alog_row, dtb_row)


def _gdn_gates_bwd(p, alog_row, dtb_row, dbg, dp_in, *, bt=256):
    T = p.shape[0]
    bt = min(bt, T)

    def body(g_ref, al_ref, db_ref, dbg_ref, dpin_ref, dg_out, dal_ref, ddb_ref):
        i = pl.program_id(0)
        gt = g_ref[...]
        lane = _iota2((bt, 128), 1)
        _, uc = _cum_mats(GDN_C)
        ea = jnp.exp(al_ref[...])
        zz = gt + db_ref[...]
        g_l = -ea * _softplus(zz)
        beta_l = _sigmoid(gt)
        dbg_v = dbg_ref[...]
        dg_l = jnp.where((lane >= GDN_H) & (lane < 2 * GDN_H), _chunk_cumsum(dbg_v, uc, GDN_C), 0.0)
        dbeta_l = jnp.where(lane < GDN_H, dbg_v, 0.0)
        da = dg_l * (-ea) * _sigmoid(zz)
        dg_out[...] = (da + dbeta_l * beta_l * (1.0 - beta_l)).astype(dg_out.dtype)

        @pl.when(i == 0)
        def _():
            dal_ref[...] = jnp.zeros_like(dal_ref)
            ddb_ref[...] = jnp.zeros_like(ddb_ref)

        dal_ref[...] += jnp.sum(dg_l * g_l, axis=0, keepdims=True)
        ddb_ref[...] += jnp.sum(da, axis=0, keepdims=True)

    vec = pl.BlockSpec((1, 128), lambda i: (0, 0))
    seg = pl.BlockSpec((bt, 128), lambda i: (i, 0))
    gate = pl.BlockSpec((bt, 128), lambda i: (i, C_GATE // 128))
    return pl.pallas_call(
        body, grid=(T // bt,), in_specs=[seg, vec, vec, seg, _ANY], out_specs=[gate, vec, vec],
        out_shape=[jax.ShapeDtypeStruct((T, C_TOT), BF16), jax.ShapeDtypeStruct((1, 128), F32),
                   jax.ShapeDtypeStruct((1, 128), F32)],
        input_output_aliases={4: 0}, name="gdn_gates_bwd",
        compiler_params=_params(("arbitrary",)))(p, alog_row, dtb_row, dbg, dp_in)


def _ssd_dt_fwd(p, dtb_row, alog_row, *, bt=256):
    T = p.shape[0]
    bt = min(bt, T)

    def body(d_ref, db_ref, al_ref, da_ref):
        lc, _ = _cum_mats(SSM_L)
        dt_l = _softplus(d_ref[...] + db_ref[...])
        alpha_l = _chunk_cumsum(dt_l * (-jnp.exp(al_ref[...])), lc, SSM_L)
        da_ref[...] = jnp.where(_iota2((bt, 128), 1) < SSM_H, dt_l, pltpu.roll(alpha_l, SSM_H, 1))

    v128 = pl.BlockSpec((1, 128), lambda i: (0, 0))
    return pl.pallas_call(
        body, grid=(T // bt,), in_specs=[pl.BlockSpec((bt, 128), lambda i: (i, 1)), v128, v128],
        out_specs=pl.BlockSpec((bt, 128), lambda i: (i, 0)), out_shape=jax.ShapeDtypeStruct((T, 128), F32),
        name="ssd_dt_fwd", compiler_params=_params(("parallel",)))(p, dtb_row, alog_row)


def _ssd_dt_bwd(p, dtb_row, alog_row, dda, dp_in, *, bt=256):
    T = p.shape[0]
    bt = min(bt, T)

    def body(d_ref, db_ref, al_ref, dda_ref, dpin_ref, dd_out, ddb_ref, dalog_ref):
        i = pl.program_id(0)
        heads = _iota2((bt, 128), 1) < SSM_H
        _, uc = _cum_mats(SSM_L)
        zz = d_ref[...] + db_ref[...]
        dt_l = _softplus(zz)
        a_row = -jnp.exp(al_ref[...])
        dda_v = dda_ref[...]
        da_l = _chunk_cumsum(jnp.where(heads, pltpu.roll(dda_v, 128 - SSM_H, 1), 0.0), uc, SSM_L)
        draw = jnp.where(heads, (dda_v + da_l * a_row) * _sigmoid(zz), 0.0)
        dd_out[...] = draw.astype(dd_out.dtype)

        @pl.when(i == 0)
        def _():
            ddb_ref[...] = jnp.zeros_like(ddb_ref)
            dalog_ref[...] = jnp.zeros_like(dalog_ref)

        ddb_ref[...] += jnp.sum(draw, axis=0, keepdims=True)
        dalog_ref[...] += jnp.sum(da_l * dt_l, axis=0, keepdims=True) * a_row

    seg = pl.BlockSpec((bt, 128), lambda i: (i, C_DT // 128))
    v128 = pl.BlockSpec((1, 128), lambda i: (0, 0))
    return pl.pallas_call(
        body, grid=(T // bt,),
        in_specs=[pl.BlockSpec((bt, 128), lambda i: (i, 1)), v128, v128, pl.BlockSpec((bt, 128), lambda i: (i, 0)), _ANY],
        out_specs=[seg, v128, v128],
        out_shape=[jax.ShapeDtypeStruct((T, C_TOT), BF16), jax.ShapeDtypeStruct((1, 128), F32),
                   jax.ShapeDtypeStruct((1, 128), F32)],
        input_output_aliases={4: 0}, name="ssd_dt_bwd",
        compiler_params=_params(("arbitrary",)))(p, dtb_row, alog_row, dda, dp_in)


def _gdn_post_fwd(o, p, w_x, *, bt=256):
    T = o.shape[0]
    bt = min(bt, T)

    def body(o_ref, z_ref, w_ref, out_ref):
        for h in range(GDN_H):
            sl = slice(h * 128, (h + 1) * 128)
            oh = o_ref[:, sl].astype(F32)
            r = lax.rsqrt(jnp.mean(oh * oh, axis=1, keepdims=True) + EPS)
            out_ref[:, sl] = (oh * r * w_ref[:, sl] * _silu(z_ref[:, sl].astype(F32))).astype(out_ref.dtype)

    row = pl.BlockSpec((bt, D), lambda i: (i, 0))
    return pl.pallas_call(
        body, grid=(T // bt,),
        in_specs=[row, pl.BlockSpec((bt, D), lambda i: (i, C_ZG // D)), pl.BlockSpec((1, D), lambda i: (0, 0))],
        out_specs=row, out_shape=jax.ShapeDtypeStruct((T, 2 * D), BF16), name="gdn_post_fwd",
        compiler_params=_params(("parallel",)))(o, p, w_x)


def _gdn_post_bwd(dmix, o, p, w_x, *, bt=256):
    T = o.shape[0]
    bt = min(bt, T)

    def body(dm_ref, o_ref, z_ref, w_ref, do_ref, dz_ref, dw_ref):
        i = pl.program_id(0)

        @pl.when(i == 0)
        def _():
            dw_ref[...] = jnp.zeros_like(dw_ref)

        for h in range(GDN_H):
            sl = slice(h * 128, (h + 1) * 128)
            oh, zh, wh = o_ref[:, sl].astype(F32), z_ref[:, sl].astype(F32), w_ref[:, sl]
            dm = dm_ref[:, sl].astype(F32)
            r = lax.rsqrt(jnp.mean(oh * oh, axis=1, keepdims=True) + EPS)
            ohat = oh * r
            dy = dm * _silu(zh)
            dz_ref[:, sl] = (dm * ohat * wh * _dsilu(zh)).astype(dz_ref.dtype)
            dohat = dy * wh
            do_ref[:, sl] = (r * (dohat - ohat * jnp.mean(dohat * ohat, axis=1, keepdims=True))).astype(do_ref.dtype)
            dw_ref[:, sl] += jnp.sum(dy * ohat, axis=0, keepdims=True)

    row = pl.BlockSpec((bt, D), lambda i: (i, 0))
    zcol = pl.BlockSpec((bt, D), lambda i: (i, C_ZG // D))
    vec = pl.BlockSpec((1, D), lambda i: (0, 0))
    return pl.pallas_call(
        body, grid=(T // bt,), in_specs=[row, row, zcol, vec], out_specs=[row, zcol, vec],
        out_shape=[jax.ShapeDtypeStruct((T, D), BF16), jax.ShapeDtypeStruct((T, C_TOT), BF16),
                   jax.ShapeDtypeStruct((1, D), F32)],
        name="gdn_post_bwd", compiler_params=_params(("arbitrary",)))(dmix, o, p, w_x)


def _ssd_post_fwd(y, xs, p, d_x, w, mix_in, *, bt=256):
    T = y.shape[0]
    bt = min(bt, T)

    def body(y_ref, x_ref, z_ref, d_ref, w_ref, mix_ref, out_ref):
        yg = (y_ref[...].astype(F32) + x_ref[...] * d_ref[...]) * _silu(z_ref[...].astype(F32))
        for g in range(2):
            sl = slice(g * 512, (g + 1) * 512)
            a = yg[:, sl]
            r = lax.rsqrt(jnp.mean(a * a, axis=1, keepdims=True) + EPS)
            out_ref[:, sl] = (a * r * w_ref[:, sl]).astype(out_ref.dtype)

    row = pl.BlockSpec((bt, D), lambda i: (i, 0))
    vec = pl.BlockSpec((1, D), lambda i: (0, 0))
    return pl.pallas_call(
        body, grid=(T // bt,),
        in_specs=[row, row, pl.BlockSpec((bt, D), lambda i: (i, C_ZS // D)), vec, vec, _ANY],
        out_specs=pl.BlockSpec((bt, D), lambda i: (i, 1)), out_shape=jax.ShapeDtypeStruct((T, 2 * D), BF16),
        input_output_aliases={5: 0}, name="ssd_post_fwd",
        compiler_params=_params(("parallel",)))(y, xs, p, d_x, w, mix_in)


def _ssd_post_bwd(dmix, y, xs, p, d_x, w, dp_in, *, bt=256):
    T = y.shape[0]
    bt = min(bt, T)

    def body(dm_ref, y_ref, x_ref, z_ref, d_ref, w_ref, dpin_ref, dyy_ref, dz_ref, dd_ref, dw_ref):
        i = pl.program_id(0)

        @pl.when(i == 0)
        def _():
            dd_ref[...] = jnp.zeros_like(dd_ref)
            dw_ref[...] = jnp.zeros_like(dw_ref)

        xv, zv = x_ref[...], z_ref[...].astype(F32)
        yy = y_ref[...].astype(F32) + xv * d_ref[...]
        sz = _silu(zv)
        yg = yy * sz
        parts = []
        for g in range(2):
            sl = slice(g * 512, (g + 1) * 512)
            a = yg[:, sl]
            r = lax.rsqrt(jnp.mean(a * a, axis=1, keepdims=True) + EPS)
            ah = a * r
            dout = dm_ref[:, sl].astype(F32)
            dah = dout * w_ref[:, sl]
            dw_ref[:, sl] += jnp.sum(dout * ah, axis=0, keepdims=True)
            parts.append(r * (dah - ah * jnp.mean(dah * ah, axis=1, keepdims=True)))
        dyg = jnp.concatenate(parts, axis=1)
        dyy = dyg * sz
        dyy_ref[...] = dyy
        dz_ref[...] = (dyg * yy * _dsilu(zv)).astype(dz_ref.dtype)
        dd_ref[...] += jnp.sum(dyy * xv, axis=0, keepdims=True)

    row = pl.BlockSpec((bt, D), lambda i: (i, 0))
    zcol = pl.BlockSpec((bt, D), lambda i: (i, C_ZS // D))
    vec = pl.BlockSpec((1, D), lambda i: (0, 0))
    return pl.pallas_call(
        body, grid=(T // bt,),
        in_specs=[pl.BlockSpec((bt, D), lambda i: (i, 1)), row, row, zcol, vec, vec, _ANY],
        out_specs=[row, zcol, vec, vec],
        out_shape=[jax.ShapeDtypeStruct((T, D), F32), jax.ShapeDtypeStruct((T, C_TOT), BF16),
                   jax.ShapeDtypeStruct((1, D), F32), jax.ShapeDtypeStruct((1, D), F32)],
        input_output_aliases={6: 1}, name="ssd_post_bwd",
        compiler_params=_params(("arbitrary",)))(dmix, y, xs, p, d_x, w, dp_in)


_NEG = -1e30


def _gdn_terms(q, k, v, bx, gam_c):
    C = GDN_C
    ri, ci = _iota2((C, C), 0), _iota2((C, C), 1)
    eye, low, strict = ri == ci, ri >= ci, ri > ci
    gam_r = jnp.sum(jnp.where(eye, gam_c, 0.0), axis=0, keepdims=True)
    G = jnp.exp(jnp.where(low, gam_c - gam_r, _NEG))
    glast = jnp.sum(jnp.where(_iota2((C, 1), 0) == C - 1, gam_c, 0.0), axis=0, keepdims=True)
    eg, egl, eL = jnp.exp(gam_c), jnp.exp(glast - gam_c), jnp.exp(glast)
    kb, vb = k * bx, v * bx
    M = _dot(kb, k, _NT)
    return dict(eye=eye, low=low, strict=strict, G=G, eg=eg, egl=egl, eL=eL, kb=kb, vb=vb, M=M,
                kbg=kb * eg, qd=q * eg, kd=k * egl, q=q, k=k, v=v, bx=bx)


def _split(a):
    hi = a.astype(_MXU)
    return hi, (a - hi.astype(F32)).astype(_MXU)


def _dot3s(a, b):
    d = lambda p, q: lax.dot_general(p, q, _NN, preferred_element_type=F32)
    return d(a[0], b[0]) + d(a[0], b[1]) + d(a[1], b[0])


def _tri_inv_many(Ls, eye):
    eyef = jnp.where(eye, 1.0, 0.0)
    Ts = [eyef - L for L in Ls]
    Ps = [-L for L in Ls]
    for _ in range(5):
        sp = [_split(p) for p in Ps]
        Ps = [_dot3s(s, s) for s in sp]
        sp = [_split(p) for p in Ps]
        st = [_split(t) for t in Ts]
        Ts = [t + _dot3s(a, b) for t, a, b in zip(Ts, st, sp)]
    return Ts


def _lane_col(tile, idx):
    return jnp.sum(jnp.where(_iota2(tile.shape, 1) == idx, tile, 0.0), axis=1, keepdims=True)


def _gdn_heads(q_ref, k_ref, v_ref, bg_ref, heads):
    out = []
    bg = bg_ref[...]
    for h in heads:
        sl = slice(h * 128, (h + 1) * 128)
        out.append(_gdn_terms(q_ref[:, sl], k_ref[:, sl], v_ref[:, sl], _lane_col(bg, h), _lane_col(bg, GDN_H + h)))
    return out


def _gdn_prep(qk, v, bg, ride=None):
    T = qk.shape[0]
    N = T // GDN_C
    C, CS = GDN_C, GDN_LOCAL_CHUNKS
    NB = N // CS
    n_ride = ride.n if ride else 0

    def body(q_ref, k_ref, v_ref, bg_ref, *rest):
        ride_in = rest[:n_ride]
        u_ref, w_ref, qd_ref, kd_ref, p_ref, t_ref = rest[n_ride:n_ride + 6]
        ride_out = rest[n_ride + 6:2 * n_ride + 6]
        if ride:
            @pl.when(pl.program_id(0) == 0)
            def _():
                ride.start(ride_in, ride_out, rest[-3:])

            @pl.when(pl.program_id(0) == NB - 1)
            def _():
                ride.wait(ride_in, ride_out, rest[-3:])

        items = [(c, h) for c in range(CS) for h in range(GDN_H)]
        views = [[r.at[pl.ds(c * C, C)] for r in (q_ref, k_ref, v_ref, bg_ref)] for c in range(CS)]
        ts = [_gdn_heads(*views[c], [h])[0] for c, h in items]
        Ts = _tri_inv_many([jnp.where(t["strict"], t["M"] * t["G"], 0.0) for t in ts], ts[0]["eye"])
        for (c, h), t, Tm in zip(items, ts, Ts):
            tok = slice(c * C, (c + 1) * C)
            sl = slice(h * 128, (h + 1) * 128)
            rows = slice(h * C, (h + 1) * C)
            u_ref[tok, sl] = _dot(Tm, t["vb"])
            w_ref[tok, sl] = _dot(Tm, t["kbg"]).astype(w_ref.dtype)
            qd_ref[tok, sl] = t["qd"].astype(qd_ref.dtype)
            kd_ref[tok, sl] = t["kd"].astype(kd_ref.dtype)
            p_ref[c, rows, :] = _dot(t["q"], t["k"], _NT) * t["G"]
            t_ref[c, rows, :] = Tm

    blk = lambda c: pl.BlockSpec((CS * C, D), lambda n: (n, c))
    sq = pl.BlockSpec((CS, GDN_H * C, C), lambda n: (n, 0, 0))
    in_specs = [blk(0), blk(1), blk(0), pl.BlockSpec((CS * C, 128), lambda n: (n, 0))]
    out_specs = [blk(0), blk(0), blk(0), blk(0), sq, sq]
    out_shape = [jax.ShapeDtypeStruct((T, D), F32), jax.ShapeDtypeStruct((T, D), BF16),
                 jax.ShapeDtypeStruct((T, D), BF16), jax.ShapeDtypeStruct((T, D), BF16),
                 jax.ShapeDtypeStruct((N, GDN_H * C, C), F32), jax.ShapeDtypeStruct((N, GDN_H * C, C), F32)]
    ins = [qk, qk, v, bg]
    if ride:
        ins, in_specs = ins + ride.srcs, in_specs + ride.specs
        out_shape, out_specs = out_shape + ride.out_shape, out_specs + ride.specs
    res = pl.pallas_call(
        body, grid=(NB,), in_specs=in_specs, out_specs=out_specs, out_shape=out_shape,
        scratch_shapes=ride.scratch if ride else [], name="gdn_prep",
        compiler_params=_params(("arbitrary",) if ride else ("parallel",)))(*ins)
    return (list(res[:6]), list(res[6:])) if ride else list(res)


def _gdn_scan_fwd(u, w, qd, kd, pm, bg):
    T = u.shape[0]
    N = T // GDN_C
    C, CS = GDN_C, GDN_SCAN_CHUNKS

    def body(u_ref, w_ref, qd_ref, kd_ref, p_ref, bg_ref, o_ref, vn_ref, ss_ref, S_scr):
        n = pl.program_id(0)

        @pl.when(n == 0)
        def _():
            S_scr[...] = jnp.zeros_like(S_scr)

        sls = [slice(h * 128, (h + 1) * 128) for h in range(GDN_H)]
        for c in range(CS):
            rows = slice(c * C, (c + 1) * C)
            glast = bg_ref[(c + 1) * C - 1:(c + 1) * C, :]
            Ss = [S_scr[:, sl] for sl in sls]
            vns = [u_ref[rows, sl] - _dot(w_ref[rows, sl], S) for sl, S in zip(sls, Ss)]
            for h, (sl, S, vn) in enumerate(zip(sls, Ss, vns)):
                ss_ref[c, :, sl] = S.astype(ss_ref.dtype)
                vn_ref[rows, sl] = vn.astype(vn_ref.dtype)
                o_ref[rows, sl] = (_dot(qd_ref[rows, sl], S)
                                   + _dot(p_ref[c, h * C:(h + 1) * C, :], vn)).astype(o_ref.dtype)
                S_scr[:, sl] = S * jnp.exp(_lane_col(glast, GDN_H + h)) + _dot(kd_ref[rows, sl], vn, _TN)

    blk = pl.BlockSpec((CS * C, D), lambda n: (n, 0))
    return dict(
        body=body, steps=N // CS, ins=[u, w, qd, kd, pm, bg],
        in_specs=[blk, blk, blk, blk, pl.BlockSpec((CS, GDN_H * C, C), lambda n: (n, 0, 0)),
                  pl.BlockSpec((CS * C, 128), lambda n: (n, 0))],
        out_specs=[blk, blk, pl.BlockSpec((CS, GDN_DK, D), lambda n: (n, 0, 0))],
        out_shape=[jax.ShapeDtypeStruct((T, D), BF16), jax.ShapeDtypeStruct((T, D), BF16),
                   jax.ShapeDtypeStruct((N, GDN_DK, D), BF16)],
        scratch=[pltpu.VMEM((GDN_DK, D), F32)])


def _gdn_scan_bwd(w, qd, kd, pm, bg, do):
    T = w.shape[0]
    N = T // GDN_C
    C, CS = GDN_C, GDN_SCAN_CHUNKS
    NB = N // CS

    def body(w_ref, qd_ref, kd_ref, p_ref, bg_ref, do_ref, dvn_ref, ds_ref, dS_scr):
        n = pl.program_id(0)

        @pl.when(n == 0)
        def _():
            dS_scr[...] = jnp.zeros_like(dS_scr)

        sls = [slice(h * 128, (h + 1) * 128) for h in range(GDN_H)]
        for c in reversed(range(CS)):
            rows = slice(c * C, (c + 1) * C)
            glast = bg_ref[(c + 1) * C - 1:(c + 1) * C, :]
            dSs = [dS_scr[:, sl] for sl in sls]
            dvns = [_dot(p_ref[c, h * C:(h + 1) * C, :], do_ref[rows, sl], _TN) + _dot(kd_ref[rows, sl], dS2)
                    for h, (sl, dS2) in enumerate(zip(sls, dSs))]
            for h, (sl, dS2, dvn) in enumerate(zip(sls, dSs, dvns)):
                ds_ref[c, :, sl] = dS2.astype(ds_ref.dtype)
                dvn_ref[rows, sl] = dvn.astype(dvn_ref.dtype)
                dS_scr[:, sl] = (dS2 * jnp.exp(_lane_col(glast, GDN_H + h))
                                 + _dot(qd_ref[rows, sl], do_ref[rows, sl], _TN) - _dot(w_ref[rows, sl], dvn, _TN))

    blk = pl.BlockSpec((CS * C, D), lambda n: (NB - 1 - n, 0))
    return dict(
        body=body, steps=NB, ins=[w, qd, kd, pm, bg, do],
        in_specs=[blk, blk, blk, pl.BlockSpec((CS, GDN_H * C, C), lambda n: (NB - 1 - n, 0, 0)),
                  pl.BlockSpec((CS * C, 128), lambda n: (NB - 1 - n, 0)), blk],
        out_specs=[blk, pl.BlockSpec((CS, GDN_DK, D), lambda n: (NB - 1 - n, 0, 0))],
        out_shape=[jax.ShapeDtypeStruct((T, D), BF16), jax.ShapeDtypeStruct((N, GDN_DK, D), BF16)],
        scratch=[pltpu.VMEM((GDN_DK, D), F32)])


def _gdn_rest_bwd(qk, v, bg, s_save, t_save, vn, dvn, ds_save, do, ride=None):
    T = qk.shape[0]
    N = T // GDN_C
    C, CS = GDN_C, GDN_REST_CHUNKS
    NB = N // CS
    n_ride = ride.n if ride else 0

    def body(q_ref, k_ref, v_ref, bg_ref, ss_ref, ts_ref, vn_ref, dvn_ref, ds_ref, do_ref, *rest):
        ride_in = rest[:n_ride]
        dqkv_ref, dbg_ref = rest[n_ride:n_ride + 2]
        ride_out = rest[n_ride + 2:2 * n_ride + 2]
        if ride:
            @pl.when(pl.program_id(0) == 0)
            def _():
                ride.start(ride_in, ride_out, rest[-3:])

            @pl.when(pl.program_id(0) == NB - 1)
            def _():
                ride.wait(ride_in, ride_out, rest[-3:])

        items = [(c, h) for c in range(CS) for h in range(GDN_H)]
        toks = [slice(c * C, (c + 1) * C) for c, _ in items]
        sls = [slice(h * 128, (h + 1) * 128) for _, h in items]
        views = [[r.at[pl.ds(c * C, C)] for r in (q_ref, k_ref, v_ref, bg_ref)] for c in range(CS)]
        ts = [_gdn_heads(*views[c], [h])[0] for c, h in items]
        Ss = [ss_ref[c, :, sl] for (c, _), sl in zip(items, sls)]
        Tms = [ts_ref[c, h * C:(h + 1) * C, :] for c, h in items]
        dS2s = [ds_ref[c, :, sl] for (c, _), sl in zip(items, sls)]
        dos = [do_ref[tok, sl] for tok, sl in zip(toks, sls)]
        vns = [vn_ref[tok, sl] for tok, sl in zip(toks, sls)]
        dvns = [dvn_ref[tok, sl] for tok, sl in zip(toks, sls)]
        Qs = [_dot(t["q"], t["k"], _NT) for t in ts]
        dws = [-_dot(dvn, S, _NT) for dvn, S in zip(dvns, Ss)]
        dqds = [_dot(do, S, _NT) for do, S in zip(dos, Ss)]
        dPs = [jnp.where(t["low"], _dot(do, vn, _NT), 0.0) for t, do, vn in zip(ts, dos, vns)]
        dkds = [_dot(vn, dS2, _NT) for vn, dS2 in zip(vns, dS2s)]
        dTs = [_dot(dvn, t["vb"], _NT) + _dot(dw, t["kbg"], _NT) for t, dvn, dw in zip(ts, dvns, dws)]
        dvbs = [_dot(Tm, dvn, _TN) for Tm, dvn in zip(Tms, dvns)]
        dkbgs = [_dot(Tm, dw, _TN) for Tm, dw in zip(Tms, dws)]
        TdTs = [_dot(Tm, dT, _TN) for Tm, dT in zip(Tms, dTs)]
        dLs = [jnp.where(t["strict"], -_dot(TdT, Tm, _NT), 0.0) for t, TdT, Tm in zip(ts, TdTs, Tms)]
        dMs = [dL * t["G"] for t, dL in zip(ts, dLs)]
        dQs = [dP * t["G"] for t, dP in zip(ts, dPs)]
        dkbs = [_dot(dM, t["k"]) + dkbg * t["eg"] for t, dM, dkbg in zip(ts, dMs, dkbgs)]
        rs = lambda a: jnp.sum(a, axis=1, keepdims=True)
        lane = _iota2((C, 128), 1)
        last = _iota2((C, 1), 0) == C - 1
        dbg = [jnp.zeros((C, 128), F32) for _ in range(CS)]
        for i, (c, h) in enumerate(items):
            t, sl, tok = ts[i], sls[i], toks[i]
            E = (dLs[i] * t["M"] + dPs[i] * Qs[i]) * t["G"]
            dqkv_ref[tok, sl] = _dot(dQs[i], t["k"]) + dqds[i] * t["eg"]
            dqkv_ref[tok, D + h * 128:D + (h + 1) * 128] = (
                _dot(dQs[i], t["q"], _TN) + _dot(dMs[i], t["kb"], _TN) + dkds[i] * t["egl"] + dkbs[i] * t["bx"])
            dqkv_ref[tok, 2 * D + h * 128:2 * D + (h + 1) * 128] = dvbs[i] * t["bx"]
            dbeta_c = rs(dkbs[i] * t["k"] + dvbs[i] * t["v"])
            dkd_kd = dkds[i] * t["kd"]
            dgam_c = rs(dqds[i] * t["qd"]) + rs(dkbgs[i] * t["kbg"]) - rs(dkd_kd) + rs(E)
            dgam_r = -jnp.sum(E, axis=0, keepdims=True)
            dgam_c = dgam_c + jnp.sum(jnp.where(t["eye"], dgam_r, 0.0), axis=1, keepdims=True)
            dlast = _sum_all(dkd_kd) + t["eL"] * _sum_all(Ss[i].astype(F32) * dS2s[i].astype(F32))
            dgam_c = dgam_c + jnp.where(last, dlast, 0.0)
            dbg[c] = dbg[c] + jnp.where(lane == h, dbeta_c, 0.0) + jnp.where(lane == GDN_H + h, dgam_c, 0.0)
        for c in range(CS):
            dbg_ref[c * C:(c + 1) * C, :] = dbg[c]

    blk = lambda c: pl.BlockSpec((CS * C, D), lambda n: (n, c))
    st = pl.BlockSpec((CS, GDN_DK, D), lambda n: (n, 0, 0))
    seg = pl.BlockSpec((CS * C, 128), lambda n: (n, 0))
    in_specs = [blk(0), blk(1), blk(0), seg, st,
                pl.BlockSpec((CS, GDN_H * C, C), lambda n: (n, 0, 0)), blk(0), blk(0), st, blk(0)]
    out_specs = [pl.BlockSpec((CS * C, 3 * D), lambda n: (n, 0)), seg]
    out_shape = [jax.ShapeDtypeStruct((T, 3 * D), F32), jax.ShapeDtypeStruct((T, 128), F32)]
    ins = [qk, qk, v, bg, s_save, t_save, vn, dvn, ds_save, do]
    if ride:
        ins, in_specs = ins + ride.srcs, in_specs + ride.specs
        out_shape, out_specs = out_shape + ride.out_shape, out_specs + ride.specs
    res = pl.pallas_call(
        body, grid=(NB,), in_specs=in_specs, out_specs=out_specs, out_shape=out_shape,
        scratch_shapes=ride.scratch if ride else [], name="gdn_rest_bwd",
        compiler_params=_params(("arbitrary",) if ride else ("parallel",)))(*ins)
    return (list(res[:2]), list(res[2:])) if ride else list(res)


def _ssd_seg(al_pair, half, s):
    L = SSM_L
    ri, ci = _iota2((L, L), 0), _iota2((L, L), 1)
    ac = jnp.max(jnp.where(half == s, al_pair, _NEG), axis=1, keepdims=True)
    ar = jnp.sum(jnp.where(ri == ci, ac, 0.0), axis=0, keepdims=True)
    return jnp.exp(jnp.where(ri >= ci, ac - ar, _NEG))


def _last_row(a):
    return jnp.sum(jnp.where(_iota2((a.shape[0], 1), 0) == a.shape[0] - 1, a, 0.0), axis=0, keepdims=True)


def _ssd_expand(da_ref):
    da = da_ref[...]
    return _dot_sel(da, _expand_mats(6, 0)[0]), _dot_sel(da, _expand_mats(6, SSM_H)[0])


def _ssd_core_fwd(xbc, da):
    T = xbc.shape[0]
    L, CS = SSM_L, SSM_SCAN_CHUNKS
    Nc = T // L

    def body(x_all, bc_all, da_all, y_all, hs_all, H_scr):
        @pl.when(pl.program_id(0) == 0)
        def _():
            H_scr[...] = jnp.zeros_like(H_scr)

        for cc in range(CS):
            rows = pl.ds(cc * L, L)
            chunk(x_all.at[rows], bc_all.at[rows], da_all.at[rows], y_all.at[rows], hs_all.at[cc], H_scr)

    def chunk(x_ref, bc_ref, da_ref, y_ref, hs_ref, H_scr):
        dt_ref, al_ref = _ssd_expand(da_ref)
        half = _iota2((L, 128), 1) >> 6
        for g in range(2):
            gs = slice(g * 512, (g + 1) * 512)
            Bg = bc_ref[:, g * 128:(g + 1) * 128]
            Cg = bc_ref[:, 256 + g * 128:256 + (g + 1) * 128]
            alg = al_ref[:, gs]
            alast = _last_row(alg)
            xdt = x_ref[:, gs] * dt_ref[:, gs]
            Hg = H_scr[:, gs]
            hs_ref[:, gs] = Hg
            CB = _dot(Cg, Bg, _NT)
            y_off = jnp.exp(alg) * _dot(Cg, Hg)
            H_scr[:, gs] = Hg * jnp.exp(alast) + _dot(Bg, jnp.exp(alast - alg) * xdt, _TN)
            for j in range(4):
                ps = slice(g * 512 + j * 128, g * 512 + (j + 1) * 128)
                al_pair = al_ref[:, ps]
                xp = x_ref[:, ps] * dt_ref[:, ps]
                ys = [_dot(_ssd_seg(al_pair, half, s) * CB, xp) for s in range(2)]
                y_ref[:, ps] = (y_off[:, j * 128:(j + 1) * 128]
                                + jnp.where(half == 0, ys[0], ys[1])).astype(y_ref.dtype)

    row = pl.BlockSpec((CS * L, D), lambda c: (c, 0))
    return dict(
        body=body, steps=Nc // CS, ins=[xbc, xbc, da],
        in_specs=[row, pl.BlockSpec((CS * L, 512), lambda c: (c, 2)), pl.BlockSpec((CS * L, 128), lambda c: (c, 0))],
        out_specs=[row, pl.BlockSpec((CS, SSM_N, D), lambda c: (c, 0, 0))],
        out_shape=[jax.ShapeDtypeStruct((T, D), BF16), jax.ShapeDtypeStruct((Nc, SSM_N, D), F32)],
        scratch=[pltpu.VMEM((SSM_N, D), F32)])


def _ssd_core_bwd(xbc, da, h_save, dyy, d_x):
    T = xbc.shape[0]
    L, CS = SSM_L, SSM_SCAN_CHUNKS
    Nc = T // L
    NB = Nc // CS

    def body(x_all, bc_all, da_all, hs_all, dy_all, d_ref, dx_all, dda_all, dH_scr, ddt_ref, dal_ref):
        @pl.when(pl.program_id(0) == 0)
        def _():
            dH_scr[...] = jnp.zeros_like(dH_scr)

        for cc in reversed(range(CS)):
            rows = pl.ds(cc * L, L)
            chunk(x_all.at[rows], bc_all.at[rows], da_all.at[rows], hs_all.at[cc], dy_all.at[rows],
                  d_ref, dx_all.at[rows], ddt_ref, dal_ref, dH_scr)
            dda_all[rows, :] = (_dot_sel(ddt_ref[...], _expand_mats(6, 0)[1])
                                + _dot_sel(dal_ref[...], _expand_mats(6, SSM_H)[1]))

    def chunk(x_ref, bc_ref, da_ref, hs_ref, dy_ref, d_ref, dx_ref, ddt_ref, dal_ref, dH_scr):
        dt_ref, al_ref = _ssd_expand(da_ref)
        lane = _iota2((L, 128), 1)
        half = lane >> 6
        rowi = _iota2((L, 1), 0)
        ri, ci = _iota2((L, L), 0), _iota2((L, L), 1)
        for g in range(2):
            gs = slice(g * 512, (g + 1) * 512)
            Bg = bc_ref[:, g * 128:(g + 1) * 128]
            Cg = bc_ref[:, 256 + g * 128:256 + (g + 1) * 128]
            alg = al_ref[:, gs]
            alast = _last_row(alg)
            eal, edec, eL = jnp.exp(alg), jnp.exp(alast - alg), jnp.exp(alast)
            xg, dtg, dYg = x_ref[:, gs], dt_ref[:, gs], dy_ref[:, gs]
            xdt = xg * dtg
            Hg = hs_ref[:, gs]
            dH2 = dH_scr[:, gs]
            CB = _dot(Cg, Bg, _NT)
            dYe = eal * dYg
            dH_scr[:, gs] = dH2 * eL + _dot(Cg, dYe, _TN)
            dC = _dot(dYe, Hg, _NT)
            zg = edec * xdt
            dz = _dot(Bg, dH2)
            dB = _dot(zg, dH2, _NT)
            tz = dz * zg
            dal = dYe * _dot(Cg, Hg) - tz
            dalast = jnp.sum(tz, axis=0, keepdims=True) + eL * jnp.sum(Hg * dH2, axis=0, keepdims=True)
            dal = dal + jnp.where(rowi == L - 1, dalast, 0.0)
            dxdt_g = edec * dz
            dx_ref[:, gs] = dxdt_g * dtg + dYg * d_ref[:, gs]
            ddt_ref[:, gs] = dxdt_g * xg
            dal_ref[:, gs] = dal
            dCB = jnp.zeros((L, L), F32)
            for j in range(4):
                ps = slice(g * 512 + j * 128, g * 512 + (j + 1) * 128)
                al_pair = al_ref[:, ps]
                xp = x_ref[:, ps] * dt_ref[:, ps]
                dYp = dy_ref[:, ps]
                dxp = []
                dal_p = jnp.zeros((L, 128), F32)
                for s in range(2):
                    seg = _ssd_seg(al_pair, half, s)
                    W = seg * CB
                    dW = _dot(jnp.where(half == s, dYp, 0.0), xp, _NT)
                    dxp.append(_dot(W, dYp, _TN))
                    dCB = dCB + dW * seg
                    Es = dW * W
                    dac = jnp.sum(Es, axis=1, keepdims=True) - jnp.sum(
                        jnp.where(ri == ci, jnp.sum(Es, axis=0, keepdims=True), 0.0), axis=1, keepdims=True)
                    dal_p = dal_p + jnp.where(lane == 64 * s, dac, 0.0)
                dxdt_p = jnp.where(half == 0, dxp[0], dxp[1])
                dx_ref[:, ps] += dxdt_p * dt_ref[:, ps]
                ddt_ref[:, ps] += dxdt_p * x_ref[:, ps]
                dal_ref[:, ps] += dal_p
            dx_ref[:, D + g * 128:D + (g + 1) * 128] = dB + _dot(dCB, Cg, _TN)
            dx_ref[:, D + 256 + g * 128:D + 256 + (g + 1) * 128] = dC + _dot(dCB, Bg)

    row = pl.BlockSpec((CS * L, D), lambda c: (NB - 1 - c, 0))
    bcs = pl.BlockSpec((CS * L, 512), lambda c: (NB - 1 - c, 2))
    seg = pl.BlockSpec((CS * L, 128), lambda c: (NB - 1 - c, 0))
    return dict(
        body=body, steps=NB, ins=[xbc, xbc, da, h_save, dyy, d_x],
        in_specs=[row, bcs, seg, pl.BlockSpec((CS, SSM_N, D), lambda c: (NB - 1 - c, 0, 0)), row,
                  pl.BlockSpec((1, D), lambda c: (0, 0))],
        out_specs=[pl.BlockSpec((CS * L, D + 512), lambda c: (NB - 1 - c, 0)), seg],
        out_shape=[jax.ShapeDtypeStruct((T, D + 512), F32), jax.ShapeDtypeStruct((T, 128), F32)],
        scratch=[pltpu.VMEM((SSM_N, D), F32), pltpu.VMEM((L, D), F32), pltpu.VMEM((L, D), F32)])


def _run_scans(parts, *, name):
    steps = parts[0]["steps"]
    assert all(p["steps"] == steps for p in parts)
    cnt = lambda key: [len(p[key]) for p in parts]
    n_in, n_out, n_scr = cnt("ins"), cnt("out_shape"), cnt("scratch")

    def body(*refs):
        ins, outs, scr = refs[:sum(n_in)], refs[sum(n_in):sum(n_in) + sum(n_out)], refs[sum(n_in) + sum(n_out):]
        oi = oo = os_ = 0
        for p, a, b, c in zip(parts, n_in, n_out, n_scr):
            p["body"](*ins[oi:oi + a], *outs[oo:oo + b], *scr[os_:os_ + c])
            oi, oo, os_ = oi + a, oo + b, os_ + c

    cat = lambda key: [v for p in parts for v in p[key]]
    res = pl.pallas_call(
        body, grid=(steps,), in_specs=cat("in_specs"), out_specs=cat("out_specs"), out_shape=cat("out_shape"),
        scratch_shapes=cat("scratch"), name=name, compiler_params=_params(("arbitrary",)))(*cat("ins"))
    out, o = [], 0
    for b in n_out:
        out.append(list(res[o:o + b]))
        o += b
    return out


_EARLY = ("w_out", "wq_mem", "wk_mem", "wv_mem", "wo_mem")
_LATE = ("w_up", "w_down")
_GRADS_MLP = ("w_down", "w_up")
_GRADS_MID = ("wo_mem", "wq_mem", "wk_mem", "wv_mem", "w_out")


def _gather_ride(shards, names):
    return None if shards is None else _Ride([shards[n] for n in names], shard=True)


def _grad_ride(shards, G, names):
    return None if shards is None else _Ride([_slots_from_full(n, G[n]) for n in names], shard=False)


def _local_step(x, mem, tgt, W, shards=None):
    T = x.shape[0]
    W = dict(W)
    cw_qk, cw_v = W["gdn_conv_w"][:, :2 * D], W["gdn_conv_w"][:, 2 * D:]
    h1 = _rmsnorm_fwd(x, W["norm1_w"], name="norm1_fwd")
    ride = _gather_ride(shards, _EARLY)
    pg = _mm(h1, W["w_in_pad"], b_cols=(C_GATE, C_TOT - C_GATE), name="in_proj_gates")
    p = _mm(h1, W["w_in_pad"], b_cols=(0, C_GATE), out_dtype=BF16, bn_cap=1664, name="in_proj", ride=ride)
    if ride:
        p, got = p
        W.update({n: _full_from_slots(n, g) for n, g in zip(_EARLY, got)})
    qk = _conv_fwd(p, C_QKV, 2 * D, cw_qk, None, l2=True, name="gdn_conv_qk_fwd")
    v_g = _conv_fwd(p, C_QKV + 2 * D, D, cw_v, None, l2=False, name="gdn_conv_v_fwd")
    bg = _gdn_gates_fwd(pg, W["gdn_alog_row"], W["gdn_dtb_row"])
    ride = _gather_ride(shards, _LATE)
    prep = _gdn_prep(qk, v_g, bg, ride)
    if ride:
        prep, got = prep
        W.update({n: _full_from_slots(n, g) for n, g in zip(_LATE, got)})
    u_g, w_g, qd_g, kd_g, p_g, t_save = prep
    xbc = _conv_fwd(p, C_XBC, D + 512, W["ssm_conv_w"], W["ssm_conv_b"], l2=False, name="ssm_conv_fwd", bc=512)
    da_s = _ssd_dt_fwd(pg, W["ssm_dtb_row"], W["ssm_alog_row"])
    (o_g, vn_g, s_save), (y_s, h_save) = _run_scans(
        [_gdn_scan_fwd(u_g, w_g, qd_g, kd_g, p_g, bg), _ssd_core_fwd(xbc, da_s)], name="scans_fwd")
    mix = _gdn_post_fwd(o_g, p, W["gdn_norm_x"])
    mix = _ssd_post_fwd(y_s, xbc, p, W["ssm_d_x"], W["ssm_norm_w"].reshape(1, D), mix)
    x1, h2 = _mm(mix, W["w_out"], epi="res_norm", extra=(x, W["norm2_w"]), bm=512, name="out_proj")
    qm = _mm(h2, W["wq_mem"], out_dtype=BF16, name="q_proj")
    m = _rmsnorm_fwd(mem, W["mem_norm_w"], name="mem_norm_fwd")
    km = _mm(m, W["wk_mem"], name="k_proj")
    vm = _mm(m, W["wv_mem"], name="v_proj")
    oa = _attn_fwd(qm, km, vm)
    x2, h3 = _mm(oa, W["wo_mem"], epi="res_norm", extra=(x1, W["norm3_w"]), bm=512, name="o_proj")
    u, act = _mm(h3, W["w_up"], epi="relu2", out_dtype=BF16, name="mlp_up")
    dx3, g_final, loss = _mm(act, W["w_down"], epi="res_loss", extra=(x2, tgt, W["final_norm_w"]), bk_cap=1024,
                             name="mlp_down_loss")
    G = {"final_norm_w": g_final.reshape(D)}
    dpre = _mm(dx3, W["w_down"], dims="nt", epi="mul2", extra=u, out_dtype=BF16, name="mlp_down_dx")
    G["w_down"] = _mm(act, dx3, dims="tn", out_dtype=BF16, name="mlp_down_dw")
    G["w_up"] = _mm(h3, dpre, dims="tn", out_dtype=BF16, name="mlp_up_dw")
    dx2, gw = _mm(dpre, W["w_up"], dims="nt", epi="norm_bwd", extra=(x2, dx3, W["norm3_w"]), bk_cap=1024,
                  name="mlp_up_dx")
    G["norm3_w"] = gw.reshape(D)
    do_a = _mm(dx2, W["wo_mem"], dims="nt", out_dtype=BF16, name="o_proj_dx")
    G["wo_mem"] = _mm(oa, dx2, dims="tn", out_dtype=BF16, name="o_proj_dw")
    dq, dk, dv = _attn_bwd(qm, km, vm, do_a)
    G["wq_mem"] = _mm(h2, dq, dims="tn", out_dtype=BF16, name="q_proj_dw")
    dx1, gw = _mm(dq, W["wq_mem"], dims="nt", epi="norm_bwd", extra=(x1, dx2, W["norm2_w"]), bm=512,
                  name="q_proj_dx")
    G["norm2_w"] = gw.reshape(D)
    G["wk_mem"] = _mm(m, dk, dims="tn", out_dtype=BF16, name="k_proj_dw")
    G["wv_mem"] = _mm(m, dv, dims="tn", out_dtype=BF16, name="v_proj_dw")
    dm = _mm(dk, W["wk_mem"], dims="nt", name="k_proj_dx")
    dm = _mm(dv, W["wv_mem"], dims="nt", epi="res", extra=dm, name="v_proj_dx")
    _, G["mem_norm_w"] = _rmsnorm_bwd(mem, W["mem_norm_w"], dm, None, name="mem_norm_bwd")
    dmix = _mm(dx1, W["w_out"], dims="nt", out_dtype=BF16, name="out_proj_dx")
    G["w_out"] = _mm(mix, dx1, dims="tn", out_dtype=BF16, name="out_proj_dw")
    do_g, dp, G["gdn_norm_x"] = _gdn_post_bwd(dmix, o_g, p, W["gdn_norm_x"])
    dyy, dp, G["ssm_d_x"], G["ssm_norm_w"] = _ssd_post_bwd(dmix, y_s, xbc, p, W["ssm_d_x"],
                                                          W["ssm_norm_w"].reshape(1, D), dp)
    (dvn_g, ds_save), (dxbc, dda_s) = _run_scans(
        [_gdn_scan_bwd(w_g, qd_g, kd_g, p_g, bg, do_g), _ssd_core_bwd(xbc, da_s, h_save, dyy, W["ssm_d_x"])],
        name="scans_bwd")
    ride = _grad_ride(shards, G, _GRADS_MLP)
    rest = _gdn_rest_bwd(qk, v_g, bg, s_save, t_save, vn_g, dvn_g, ds_save, do_g, ride)
    if ride:
        rest, got = rest
        G.update(zip(_GRADS_MLP, got))
    dqkvn, dbg = rest
    dy_qk, gcw_qk, _ = _conv_bwd_act(p, C_QKV, 2 * D, cw_qk, None, dqkvn, 0, l2=True, name="gdn_conv_qk_bwd_act")
    dy_v, gcw_v, _ = _conv_bwd_act(p, C_QKV + 2 * D, D, cw_v, None, dqkvn, 2 * D, l2=False,
                                   name="gdn_conv_v_bwd_act")
    G["gdn_conv_w"] = jnp.concatenate([gcw_qk, gcw_v], axis=1)
    dp = _conv_bwd_in(dy_qk, cw_qk, dp, C_QKV, T, name="gdn_conv_qk_bwd_in")
    dp = _conv_bwd_in(dy_v, cw_v, dp, C_QKV + 2 * D, T, name="gdn_conv_v_bwd_in")
    dp, G["gdn_alog_row"], G["gdn_dtb_row"] = _gdn_gates_bwd(pg, W["gdn_alog_row"], W["gdn_dtb_row"], dbg, dp)
    dy_s, G["ssm_conv_w"], G["ssm_conv_b"] = _conv_bwd_act(p, C_XBC, D + 512, W["ssm_conv_w"], W["ssm_conv_b"],
                                                           dxbc, 0, l2=False, name="ssm_conv_bwd_act", bc=512)
    dp = _conv_bwd_in(dy_s, W["ssm_conv_w"], dp, C_XBC, T, name="ssm_conv_bwd_in", bc=512)
    dp, G["ssm_dtb_row"], G["ssm_alog_row"] = _ssd_dt_bwd(pg, W["ssm_dtb_row"], W["ssm_alog_row"], dda_s, dp)
    ride = _grad_ride(shards, G, _GRADS_MID)
    g_in = _mm(h1, dp, dims="tn", out_dtype=BF16, bn_cap=1152, name="in_proj_dw", ride=ride)
    if ride:
        g_in, got = g_in
        G.update(zip(_GRADS_MID, got))
    G["w_in"] = _unpad_w_in(g_in)
    ride = _grad_ride(shards, G, ("w_in",))
    res = _mm(dp, W["w_in_pad"], dims="nt", epi="norm_bwd", extra=(x, dx1, W["norm1_w"]),
              name="in_proj_dx", ride=ride)
    if ride:
        res, got = res
        G["w_in"] = got[0]
    dx, gw = res
    G["norm1_w"] = gw.reshape(D)
    return loss, dx, G


def _all_gather(shards, out_dtype, *, name):
    n = len(shards)

    def body(*refs):
        x_refs, out_refs, stage = refs[:n], refs[n:2 * n], refs[2 * n:3 * n]
        send_sems, recv_sems, local_sems = refs[3 * n:]
        x, y, c = _place()
        me, sibling = (x, y, c), (x, y, 1 - c)
        chips = [(1 - x, y), (x, 1 - y), (1 - x, 1 - y)]

        def slot(px, py, pc):
            return 4 * px + 2 * py + pc

        def copy(a, k, block, to, src=None):
            dst = out_refs[a].at[slot(*block)]
            return pltpu.make_async_remote_copy(
                src_ref=dst if src is None else src, dst_ref=dst, send_sem=send_sems.at[a, k],
                recv_sem=recv_sems.at[a, k], device_id=to, device_id_type=_MESH)

        for a in range(n):
            stage[a][...] = x_refs[a][...].astype(out_dtype)
        mine = [pltpu.make_async_copy(stage[a], out_refs[a].at[slot(*me)], local_sems.at[a]) for a in range(n)]
        for cp in mine:
            cp.start()
        first = []
        for a in range(n):
            first.append(copy(a, 0, me, sibling, src=stage[a]))
            first += [copy(a, 1 + j, me, (*chip, c), src=stage[a]) for j, chip in enumerate(chips)]
        for cp in first:
            cp.start()
        passed = [[copy(a, 4 + j, (*chip, c), sibling) for j, chip in enumerate(chips)] for a in range(n)]
        for j, chip in enumerate(chips):
            for a in range(n):
                copy(a, 1 + j, (*chip, c), me).wait_recv()
                passed[a][j].start()
        for a in range(n):
            copy(a, 0, sibling, me).wait_recv()
            for j, chip in enumerate(chips):
                copy(a, 4 + j, (*chip, 1 - c), me).wait_recv()
        for cp in first + [cp for row in passed for cp in row]:
            cp.wait_send()
        for cp in mine:
            cp.wait()

    outs = pl.pallas_call(
        body, in_specs=[_VM] * n, out_specs=[_ANY] * n,
        out_shape=[jax.ShapeDtypeStruct((N_DEV,) + s.shape, out_dtype) for s in shards],
        scratch_shapes=[pltpu.VMEM(s.shape, out_dtype) for s in shards]
        + [pltpu.SemaphoreType.DMA((n, 7)), pltpu.SemaphoreType.DMA((n, 7)), pltpu.SemaphoreType.DMA((n,))],
        name=name, compiler_params=pltpu.CompilerParams(vmem_limit_bytes=VMEM_LIMIT))(*shards)
    return list(outs)


def _cast_bf16(arrs, *, name):
    n = len(arrs)

    def body(*refs):
        for a in range(n):
            refs[n + a][...] = refs[a][...].astype(BF16)

    return list(pl.pallas_call(
        body, in_specs=[_VM] * n, out_specs=[_VM] * n,
        out_shape=[jax.ShapeDtypeStruct(s.shape, BF16) for s in arrs], name=name,
        compiler_params=pltpu.CompilerParams(vmem_limit_bytes=VMEM_LIMIT))(*arrs))


def _sum8(a, *, name):
    _, R, Cc = a.shape
    br = _pick_rows(R, 128)

    def body(a_ref, o_ref):
        s = a_ref[0].astype(F32)
        for k in range(1, N_DEV):
            s = s + a_ref[k].astype(F32)
        o_ref[...] = s

    return pl.pallas_call(
        body, grid=(R // br,), in_specs=[pl.BlockSpec((N_DEV, br, Cc), lambda i: (0, i, 0))],
        out_specs=pl.BlockSpec((br, Cc), lambda i: (i, 0)), out_shape=jax.ShapeDtypeStruct((R, Cc), F32),
        name=name, compiler_params=_params(("parallel",)))(a)


def _pick_rows(R, cap):
    if R <= cap:
        return R
    for d in range(cap, 7, -8):
        if R % d == 0:
            return d
    return R


def _adamw(w, g, m, v, *, name):
    shape = w.shape
    as2d = (lambda t: t.reshape(1, -1)) if w.ndim == 1 else (lambda t: t)
    w2, m2, v2 = as2d(w), as2d(m), as2d(v)
    R, Cc = w2.shape
    from_slabs = g.ndim == 3
    br = _pick_rows(R, 128 if from_slabs else 256)
    c1 = 1.0 - ADAM_B1 ** ADAM_STEP
    c2 = 1.0 - ADAM_B2 ** ADAM_STEP

    def body(w_ref, g_ref, m_ref, v_ref, go_ref, d_ref, nm_ref, nv_ref):
        if from_slabs:
            gv = g_ref[0].astype(F32)
            for k in range(1, N_DEV):
                gv = gv + g_ref[k].astype(F32)
        else:
            gv = g_ref[...]
        go_ref[...] = gv
        nm = ADAM_B1 * m_ref[...] + (1.0 - ADAM_B1) * gv
        nv = ADAM_B2 * v_ref[...] + (1.0 - ADAM_B2) * (gv * gv)
        nm_ref[...] = nm
        nv_ref[...] = nv
        d_ref[...] = -ADAM_LR * ((nm / c1) / (jnp.sqrt(nv / c2) + ADAM_EPS) + ADAM_WD * w_ref[...])

    blk = pl.BlockSpec((br, Cc), lambda i: (i, 0))
    g_spec = pl.BlockSpec((N_DEV, br, Cc), lambda i: (0, i, 0)) if from_slabs else blk
    outs = pl.pallas_call(
        body, grid=(R // br,), in_specs=[blk, g_spec, blk, blk], out_specs=[blk] * 4,
        out_shape=[jax.ShapeDtypeStruct((R, Cc), F32)] * 4, name=name,
        compiler_params=_params(("parallel",)))(w2, g if from_slabs else as2d(g), m2, v2)
    return tuple(o.reshape(shape) for o in outs)


_BIG = ("w_in", "w_out", "wq_mem", "wk_mem", "wv_mem", "wo_mem", "w_up", "w_down")
_COL_SHARDED = ("w_in", "w_up")
_WEIGHTS = ("norm1_w", "w_in", "gdn_conv_w", "gdn_a_log", "gdn_dt_bias", "gdn_norm_w", "ssm_conv_w", "ssm_conv_b",
            "ssm_a_log", "ssm_dt_bias", "ssm_d", "ssm_norm_w", "w_out", "norm2_w", "mem_norm_w", "wq_mem", "wk_mem",
            "wv_mem", "wo_mem", "norm3_w", "w_up", "w_down", "final_norm_w")
_IN_PAD = 112


def _move_col_slabs(a, to_slabs, *, name):
    n, R, c = (N_DEV, a.shape[0], a.shape[1] // N_DEV) if to_slabs else a.shape
    slab = pl.BlockSpec((None, R, c), lambda j: (j, 0, 0))
    cols = pl.BlockSpec((R, c), lambda j: (0, j))

    def body(a_ref, o_ref):
        o_ref[...] = a_ref[...]

    return pl.pallas_call(
        body, grid=(n,), in_specs=[cols if to_slabs else slab], out_specs=slab if to_slabs else cols,
        out_shape=jax.ShapeDtypeStruct((n, R, c) if to_slabs else (R, n * c), a.dtype), name=name,
        compiler_params=_params(("parallel",)))(a)


def _full_from_slots(name, g):
    if name in _COL_SHARDED:
        if g.shape[2] % 128 == 0:
            return _move_col_slabs(g, False, name="cols_" + name)
        return jnp.transpose(g, (1, 0, 2)).reshape(g.shape[1], N_DEV * g.shape[2])
    return g.reshape(N_DEV * g.shape[1], g.shape[2])


def _slots_from_full(name, f):
    if name in _COL_SHARDED:
        if (f.shape[1] // N_DEV) % 128 == 0:
            return _move_col_slabs(f, True, name="slabs_" + name)
        return jnp.transpose(f.reshape(f.shape[0], N_DEV, f.shape[1] // N_DEV), (1, 0, 2))
    return f.reshape(N_DEV, f.shape[0] // N_DEV, f.shape[1])


def _pad_w_in(w):
    z = jnp.zeros((w.shape[0], _IN_PAD), w.dtype)
    return jnp.concatenate([w[:, :4096], w[:, 4112:6672], w[:, 4096:4112], z, w[:, 6672:6688], z], axis=1)


def _unpad_w_in(gp):
    return jnp.concatenate([gp[:, :4096], gp[:, C_GATE:C_GATE + 16], gp[:, 4096:C_GATE], gp[:, C_DT:C_DT + 16]],
                           axis=1)


def _pack_rows(vals):
    rows, offs, r = [], [], 0
    for vflat in vals:
        nrow = 8 * -(-vflat.shape[0] // 1024)
        rows.append(jnp.pad(vflat, (0, nrow * 128 - vflat.shape[0])).reshape(nrow, 128))
        offs.append((r, vflat.shape[0]))
        r += nrow
    return jnp.concatenate(rows, axis=0), offs


def _unpack_rows(packed, offs, shapes):
    out = []
    for (r, nel), shp in zip(offs, shapes):
        nrow = -(-nel // 128)
        out.append(packed[r:r + nrow].reshape(-1)[:nel].reshape(shp))
    return out


def kernel(x, mem, norm1_w, w_in, gdn_conv_w, gdn_a_log, gdn_dt_bias, gdn_norm_w, ssm_conv_w, ssm_conv_b, ssm_a_log, ssm_dt_bias, ssm_d, ssm_norm_w, w_out, norm2_w, mem_norm_w, wq_mem, wk_mem, wv_mem, wo_mem, norm3_w, w_up, w_down, final_norm_w, loss_target, m_norm1_w, m_w_in, m_gdn_conv_w, m_gdn_a_log, m_gdn_dt_bias, m_gdn_norm_w, m_ssm_conv_w, m_ssm_conv_b, m_ssm_a_log, m_ssm_dt_bias, m_ssm_d, m_ssm_norm_w, m_w_out, m_norm2_w, m_mem_norm_w, m_wq_mem, m_wk_mem, m_wv_mem, m_wo_mem, m_norm3_w, m_w_up, m_w_down, m_final_norm_w, v_norm1_w, v_w_in, v_gdn_conv_w, v_gdn_a_log, v_gdn_dt_bias, v_gdn_norm_w, v_ssm_conv_w, v_ssm_conv_b, v_ssm_a_log, v_ssm_dt_bias, v_ssm_d, v_ssm_norm_w, v_w_out, v_norm2_w, v_mem_norm_w, v_wq_mem, v_wk_mem, v_wv_mem, v_wo_mem, v_norm3_w, v_w_up, v_w_down, v_final_norm_w):
    args = dict(locals())
    w_loc = {n: args[n] for n in _WEIGHTS}
    me = 4 * lax.axis_index("x") + 2 * lax.axis_index("y") + lax.axis_index("c")

    w_in_full = _full_from_slots("w_in", _all_gather([w_in], BF16, name="gather_w_in")[0])
    later = _EARLY + _LATE
    shards = dict(zip(later, _cast_bf16([w_loc[n] for n in later], name="cast_shards")))
    conv_pack, conv_offs = _pack_rows([gdn_conv_w.reshape(-1), ssm_conv_w.reshape(-1)])
    conv_all = _all_gather([conv_pack], F32, name="gather_conv")[0]
    gdn_cw, ssm_cw = [], []
    for k in range(N_DEV):
        a, b = _unpack_rows(conv_all[k], conv_offs, [gdn_conv_w.shape, ssm_conv_w.shape])
        gdn_cw.append(a)
        ssm_cw.append(b)
    W = {
        "w_in_pad": _pad_w_in(w_in_full),
        "norm1_w": norm1_w, "norm2_w": norm2_w, "norm3_w": norm3_w, "mem_norm_w": mem_norm_w,
        "final_norm_w": final_norm_w, "ssm_norm_w": ssm_norm_w, "ssm_conv_b": ssm_conv_b,
        "gdn_conv_w": jnp.concatenate(gdn_cw, axis=1), "ssm_conv_w": jnp.concatenate(ssm_cw, axis=1),
        "gdn_alog_row": jnp.pad(gdn_a_log, (GDN_H, 128 - 2 * GDN_H)).reshape(1, 128),
        "gdn_dtb_row": jnp.pad(gdn_dt_bias, (GDN_H, 128 - 2 * GDN_H)).reshape(1, 128),
        "gdn_norm_x": jnp.tile(gdn_norm_w, GDN_H).reshape(1, D),
        "ssm_dtb_row": jnp.pad(ssm_dt_bias, (0, 128 - SSM_H)).reshape(1, 128),
        "ssm_alog_row": jnp.pad(ssm_a_log, (0, 128 - SSM_H)).reshape(1, 128),
        "ssm_d_x": jnp.repeat(ssm_d, SSM_P).reshape(1, D),
    }

    loss_part, grad_x, G = _local_step(x[0], mem[0], loss_target[0], W, shards)

    grads = {n: G[n] for n in _BIG}

    small = {
        "norm1_w": G["norm1_w"], "gdn_conv_w": G["gdn_conv_w"], "gdn_a_log": G["gdn_alog_row"][0, GDN_H:2 * GDN_H],
        "gdn_dt_bias": G["gdn_dtb_row"][0, GDN_H:2 * GDN_H], "gdn_norm_w": G["gdn_norm_x"].reshape(GDN_H, 128).sum(0),
        "ssm_conv_w": G["ssm_conv_w"], "ssm_conv_b": G["ssm_conv_b"],
        "ssm_a_log": G["ssm_alog_row"][0, :SSM_H], "ssm_dt_bias": G["ssm_dtb_row"][0, :SSM_H],
        "ssm_d": G["ssm_d_x"].reshape(SSM_H, SSM_P).sum(1), "ssm_norm_w": G["ssm_norm_w"].reshape(D),
        "norm2_w": G["norm2_w"], "mem_norm_w": G["mem_norm_w"], "norm3_w": G["norm3_w"],
        "final_norm_w": G["final_norm_w"], "loss": loss_part[0, :1],
    }
    names = list(small)
    pack, offs = _pack_rows([small[n].reshape(-1) for n in names])
    tot = _sum8(_all_gather([pack], F32, name="gather_small")[0], name="sum_small")
    summed = dict(zip(names, _unpack_rows(tot, offs, [small[n].shape for n in names])))
    loss = summed.pop("loss")[0]
    for n in ("gdn_conv_w", "ssm_conv_w"):
        width = w_loc[n].shape[1]
        summed[n] = lax.dynamic_slice_in_dim(summed[n], me * width, width, axis=1)
    grads.update(summed)

    upd = {n: _adamw(w_loc[n], grads[n], args["m_" + n], args["v_" + n], name="adamw_" + n) for n in _WEIGHTS}
    return (loss, grad_x[None], *[upd[n][0] for n in _WEIGHTS], *[upd[n][1] for n in _WEIGHTS],
            *[upd[n][2] for n in _WEIGHTS], *[upd[n][3] for n in _WEIGHTS])
```

```python
import jax
import jax.numpy as jnp
from jax import lax
from jax.experimental import pallas as pl
from jax.experimental.pallas import tpu as pltpu

F32 = jnp.float32
BF16 = jnp.bfloat16
_MXU = BF16

D = 1024
EPS = 1e-6
CONV_K = 4
GDN_H, GDN_DK, GDN_C = 8, 128, 64
GDN_SCAN_CHUNKS = 4
GDN_LOCAL_CHUNKS = 4
GDN_REST_CHUNKS = 4
SSM_H, SSM_P, SSM_L, SSM_N = 16, 64, 128, 128
SSM_SCAN_CHUNKS = 2
MEM_H, MEM_HD = 4, 256
D_FF = 4096
N_DEV = 8

C_QKV, C_ZG, C_ZS, C_XBC, C_GATE, C_DT, C_TOT = 0, 3072, 4096, 5120, 6656, 6784, 6912
P_HALO = 16

ADAM_LR, ADAM_B1, ADAM_B2, ADAM_EPS, ADAM_WD, ADAM_STEP = 0.001, 0.9, 0.999, 1e-08, 0.01, 10

VMEM_LIMIT = 56 * 1024 * 1024

_NN = (((1,), (0,)), ((), ()))
_NT = (((1,), (1,)), ((), ()))
_TN = (((0,), (0,)), ((), ()))


def _dot(a, b, dims=_NN):
    return lax.dot_general(a.astype(_MXU), b.astype(_MXU), dims, preferred_element_type=F32)


def _split3(a):
    a1 = a.astype(BF16)
    r1 = a - a1.astype(F32)
    a2 = r1.astype(BF16)
    return a1, a2, (r1 - a2.astype(F32)).astype(BF16)


def _dot_sel(a, e):
    eb = e.astype(BF16)
    return sum(lax.dot_general(p, eb, _NN, preferred_element_type=F32) for p in _split3(a))


def _sel_dot(e, a):
    eb = e.astype(BF16)
    return sum(lax.dot_general(eb, p, _NN, preferred_element_type=F32) for p in _split3(a))


def _chunk_cumsum(a, tri, chunk):
    return jnp.concatenate([_sel_dot(tri, a[r:r + chunk]) for r in range(0, a.shape[0], chunk)], axis=0)


def _params(sem):
    return pltpu.CompilerParams(dimension_semantics=sem, vmem_limit_bytes=VMEM_LIMIT)


def _pick(n, cap):
    for d in range(min(cap, n), 0, -128):
        if n % d == 0 and d % 128 == 0:
            return d
    return n


def _sigmoid(x):
    return 0.5 * jnp.tanh(0.5 * x) + 0.5


def _silu(x):
    return x * _sigmoid(x)


def _dsilu(x):
    s = _sigmoid(x)
    return s * (1.0 + x * (1.0 - s))


def _softplus(x):
    return jnp.maximum(x, 0.0) + jnp.log(1.0 + jnp.exp(-jnp.abs(x)))


def _iota2(shape, axis):
    return lax.broadcasted_iota(jnp.int32, shape, axis)


def _sum_all(x):
    return jnp.sum(jnp.sum(x, axis=1, keepdims=True), axis=0, keepdims=True)


_MESH = pl.DeviceIdType.MESH
_ANY = pl.BlockSpec(memory_space=pl.ANY)
_VM = pl.BlockSpec(memory_space=pltpu.VMEM)
_REL = [(r >> 2 & 1, r >> 1 & 1, r & 1) for r in range(1, N_DEV)]


def _place():
    return lax.axis_index("x"), lax.axis_index("y"), lax.axis_index("c")


class _Ride:
    def __init__(self, srcs, shard):
        self.srcs, self.shard, self.n = list(srcs), shard, len(srcs)
        self.out_shape = [jax.ShapeDtypeStruct(((N_DEV,) + s.shape) if shard else s.shape, s.dtype)
                          for s in self.srcs]
        self.specs = [_ANY] * self.n
        self.scratch = [pltpu.SemaphoreType.DMA((self.n, N_DEV - 1)), pltpu.SemaphoreType.DMA((self.n, N_DEV - 1)),
                        pltpu.SemaphoreType.DMA((self.n,))]

    def _copies(self, in_refs, out_refs, sems):
        send, recv, loc = sems
        x, y, c = _place()
        me = 4 * x + 2 * y + c
        local, remote, arrive = [], [], []
        for a in range(self.n):
            src = in_refs[a] if self.shard else in_refs[a].at[me]
            local.append(pltpu.make_async_copy(src, out_refs[a].at[me], loc.at[a]))
        for k, (rx, ry, rc) in enumerate(_REL):
            peer = (lax.rem(x + rx, 2), lax.rem(y + ry, 2), lax.rem(c + rc, 2))
            ps = 4 * peer[0] + 2 * peer[1] + peer[2]
            for a in range(self.n):
                src = in_refs[a] if self.shard else in_refs[a].at[ps]
                remote.append(pltpu.make_async_remote_copy(
                    src_ref=src, dst_ref=out_refs[a].at[me], send_sem=send.at[a, k], recv_sem=recv.at[a, k],
                    device_id=peer, device_id_type=_MESH))
                slot = out_refs[a].at[ps]
                arrive.append(pltpu.make_async_remote_copy(
                    src_ref=slot, dst_ref=slot, send_sem=send.at[a, k], recv_sem=recv.at[a, k],
                    device_id=peer, device_id_type=_MESH))
        return local, remote, arrive

    def start(self, in_refs, out_refs, sems):
        local, remote, _ = self._copies(in_refs, out_refs, sems)
        for cp in local + remote:
            cp.start()

    def wait(self, in_refs, out_refs, sems):
        local, remote, arrive = self._copies(in_refs, out_refs, sems)
        for cp in arrive:
            cp.wait_recv()
        for cp in remote:
            cp.wait_send()
        for cp in local:
            cp.wait()


_EPI = {
    "none": ((), ("tile",)),
    "res": (("tile",), ("tile",)),
    "mul2": (("tile",), ("tile",)),
    "relu2": ((), ("tile", "tile")),
    "res_norm": (("tile", "row"), ("tile", "tile")),
    "norm_bwd": (("tile", "tile", "row"), ("tile", "row")),
    "res_loss": (("tile", "tile", "row"), ("tile", "row", "row")),
}


def _mm(a, b, *, dims="nn", epi="none", extra=(), out_dtype=F32, name, bm=1024, bn_cap=1024, bk_cap=2048,
        ride=None, b_cols=None):
    if dims == "nn":
        (M, K), (K2, N) = a.shape, b.shape
    elif dims == "nt":
        (M, K), (N, K2) = a.shape, b.shape
    else:
        (K, M), (K2, N) = a.shape, b.shape
    jb0 = 0
    if b_cols is not None:
        N = b_cols[1]
    assert K == K2, (a.shape, b.shape, dims)
    bm = _pick(M, bm)
    bn = _pick(N, bn_cap)
    bk = _pick(K, bk_cap)
    nk = K // bk
    if b_cols is not None:
        assert dims == "nn" and b_cols[0] % bn == 0
        jb0 = b_cols[0] // bn
    dn = {"nn": _NN, "nt": _NT, "tn": _TN}[dims]
    a_spec = (pl.BlockSpec((bk, bm), lambda i, j, k: (k, i)) if dims == "tn"
              else pl.BlockSpec((bm, bk), lambda i, j, k: (i, k)))
    b_spec = (pl.BlockSpec((bn, bk), lambda i, j, k: (j, k)) if dims == "nt"
              else pl.BlockSpec((bk, bn), lambda i, j, k: (k, j + jb0)))
    o_spec = pl.BlockSpec((bm, bn), lambda i, j, k: (i, j))
    r_spec = pl.BlockSpec((1, bn), lambda i, j, k: (0, j))
    extra = list(extra) if isinstance(extra, (tuple, list)) else [extra]
    ekinds, okinds = _EPI[epi]
    assert len(extra) == len(ekinds) and (epi not in ("res_norm", "norm_bwd", "res_loss") or bn == N)
    n_extra, n_out = len(ekinds), len(okinds)
    n_ride = ride.n if ride else 0
    gi, gj = M // bm, N // bn

    def body(a_ref, b_ref, *rest):
        ex = rest[:n_extra]
        first = pl.program_id(0) == 0
        ride_in = rest[n_extra:n_extra + n_ride]
        outs = rest[n_extra + n_ride:n_extra + n_ride + n_out]
        ride_out = rest[n_extra + n_ride + n_out:n_extra + 2 * n_ride + n_out]
        if ride:
            at = lambda i, j, k: ((pl.program_id(0) == i) & (pl.program_id(1) == j) & (pl.program_id(2) == k))

            @pl.when(at(0, 0, 0))
            def _():
                ride.start(ride_in, ride_out, rest[-3:])

        def finish(r):
            if epi == "res":
                outs[0][...] = (r + ex[0][...].astype(F32)).astype(outs[0].dtype)
            elif epi == "mul2":
                outs[0][...] = (2.0 * r * ex[0][...].astype(F32)).astype(outs[0].dtype)
            elif epi == "relu2":
                u = jnp.maximum(r, 0.0)
                outs[0][...] = u.astype(outs[0].dtype)
                outs[1][...] = (u * u).astype(outs[1].dtype)
            elif epi == "res_norm":
                y = r + ex[0][...]
                outs[0][...] = y
                rstd = lax.rsqrt(jnp.mean(y * y, axis=1, keepdims=True) + EPS)
                outs[1][...] = (y * rstd * ex[1][...]).astype(outs[1].dtype)
            elif epi == "norm_bwd":
                xv = ex[0][...]
                rstd = lax.rsqrt(jnp.mean(xv * xv, axis=1, keepdims=True) + EPS)
                xh = xv * rstd
                dxh = r * ex[2][...]
                outs[0][...] = ex[1][...] + rstd * (dxh - xh * jnp.mean(dxh * xh, axis=1, keepdims=True))
                dw = jnp.sum(r * xh, axis=0, keepdims=True)

                @pl.when(first)
                def _():
                    outs[1][...] = dw

                @pl.when(jnp.logical_not(first))
                def _():
                    outs[1][...] += dw
            elif epi == "res_loss":
                y = r + ex[0][...]
                wv = ex[2][...]
                rstd = lax.rsqrt(jnp.mean(y * y, axis=1, keepdims=True) + EPS)
                yh = y * rstd
                err = yh * wv - ex[1][...]
                part_loss = 0.5 * jnp.sum(jnp.mean(err * err, axis=1, keepdims=True), axis=0, keepdims=True)
                dyn = err * (1.0 / N)
                dyh = dyn * wv
                outs[0][...] = rstd * (dyh - yh * jnp.mean(dyh * yh, axis=1, keepdims=True))
                dw = jnp.sum(dyn * yh, axis=0, keepdims=True)
                lrow = jnp.broadcast_to(part_loss, (1, N))

                @pl.when(first)
                def _():
                    outs[1][...] = dw
                    outs[2][...] = lrow

                @pl.when(jnp.logical_not(first))
                def _():
                    outs[1][...] += dw
                    outs[2][...] += lrow
            else:
                outs[0][...] = r.astype(outs[0].dtype)

        part = _dot(a_ref[...], b_ref[...], dn)
        if nk == 1:
            finish(part)
        else:
            acc = rest[n_extra + 2 * n_ride + n_out]
            k = pl.program_id(2)

            @pl.when(k == 0)
            def _():
                acc[...] = part

            @pl.when((k > 0) & (k < nk - 1))
            def _():
                acc[...] += part

            @pl.when(k == nk - 1)
            def _():
                finish(acc[...] + part)

        if ride:
            @pl.when(at(gi - 1, gj - 1, nk - 1))
            def _():
                ride.wait(ride_in, ride_out, rest[-3:])

    kind_spec = {"tile": o_spec, "row": r_spec}
    ins = [a, b] + [e.reshape(1, N) if k == "row" else e for e, k in zip(extra, ekinds)]
    in_specs = [a_spec, b_spec] + [kind_spec[k] for k in ekinds]
    out_dtypes = {"res_norm": (F32, BF16), "norm_bwd": (F32, F32), "res_loss": (F32, F32, F32)}.get(
        epi, (out_dtype,) * n_out)
    out_shape = [jax.ShapeDtypeStruct((M, N) if k == "tile" else (1, N), dt) for k, dt in zip(okinds, out_dtypes)]
    out_specs = [kind_spec[k] for k in okinds]
    scratch = [pltpu.VMEM((bm, bn), F32)] if nk > 1 else []
    sem = ("arbitrary" if epi in ("norm_bwd", "res_loss") else "parallel", "parallel", "arbitrary")
    if ride:
        ins, in_specs = ins + ride.srcs, in_specs + ride.specs
        out_shape, out_specs = out_shape + ride.out_shape, out_specs + ride.specs
        scratch, sem = scratch + ride.scratch, ("arbitrary",) * 3
    res = pl.pallas_call(
        body, grid=(gi, gj, nk), in_specs=in_specs, out_specs=out_specs, out_shape=out_shape,
        scratch_shapes=scratch, name=name, compiler_params=_params(sem))(*ins)
    main = res[:n_out] if n_out > 1 else res[0]
    return (main, list(res[n_out:])) if ride else main


def _rmsnorm_fwd(x, w, *, name, bt=256):
    T, Dm = x.shape
    bt = min(bt, T)

    def body(x_ref, w_ref, h_ref):
        xv = x_ref[...]
        r = lax.rsqrt(jnp.mean(xv * xv, axis=1, keepdims=True) + EPS)
        h_ref[...] = (xv * r * w_ref[...]).astype(h_ref.dtype)

    return pl.pallas_call(
        body, grid=(T // bt,),
        in_specs=[pl.BlockSpec((bt, Dm), lambda i: (i, 0)), pl.BlockSpec((1, Dm), lambda i: (0, 0))],
        out_specs=pl.BlockSpec((bt, Dm), lambda i: (i, 0)),
        out_shape=jax.ShapeDtypeStruct((T, Dm), BF16), name=name,
        compiler_params=_params(("parallel",)))(x, w.reshape(1, Dm))


def _rmsnorm_bwd(x, w, dh, dres, *, name, bt=256):
    T, Dm = x.shape
    bt = min(bt, T)
    has_res = dres is not None

    def body(x_ref, w_ref, dh_ref, *rest):
        dres_ref = rest[0] if has_res else None
        dx_ref, dw_ref = rest[-2], rest[-1]
        i = pl.program_id(0)
        xv = x_ref[...]
        r = lax.rsqrt(jnp.mean(xv * xv, axis=1, keepdims=True) + EPS)
        xh = xv * r
        dhv = dh_ref[...].astype(F32)
        dxh = dhv * w_ref[...]
        dx = r * (dxh - xh * jnp.mean(dxh * xh, axis=1, keepdims=True))
        if has_res:
            dx = dx + dres_ref[...]
        dx_ref[...] = dx

        @pl.when(i == 0)
        def _():
            dw_ref[...] = jnp.zeros_like(dw_ref)

        dw_ref[...] += jnp.sum(dhv * xh, axis=0, keepdims=True)

    row = pl.BlockSpec((bt, Dm), lambda i: (i, 0))
    vec = pl.BlockSpec((1, Dm), lambda i: (0, 0))
    ins = [x, w.reshape(1, Dm), dh] + ([dres] if has_res else [])
    dx, dw = pl.pallas_call(
        body, grid=(T // bt,), in_specs=[row, vec, row] + ([row] if has_res else []),
        out_specs=[row, vec],
        out_shape=[jax.ShapeDtypeStruct((T, Dm), F32), jax.ShapeDtypeStruct((1, Dm), F32)],
        name=name, compiler_params=_params(("arbitrary",)))(*ins)
    return dx, dw.reshape(Dm)


def _attn_fwd(q, km, vm, *, bt=256):
    T = q.shape[0]
    M = km.shape[0]
    bt = min(bt, T)
    scale = MEM_HD ** -0.5

    def body(q_ref, k_ref, v_ref, o_ref):
        sls = [slice(h * MEM_HD, (h + 1) * MEM_HD) for h in range(MEM_H)]
        ss = [_dot(q_ref[:, sl], k_ref[:, sl], _NT) * scale for sl in sls]
        es = [jnp.exp(s - jnp.max(s, axis=1, keepdims=True)) for s in ss]
        ps = [e / jnp.sum(e, axis=1, keepdims=True) for e in es]
        for sl, p in zip(sls, ps):
            o_ref[:, sl] = _dot(p, v_ref[:, sl]).astype(o_ref.dtype)

    row = pl.BlockSpec((bt, D), lambda i: (i, 0))
    mem = pl.BlockSpec((M, D), lambda i: (0, 0))
    return pl.pallas_call(
        body, grid=(T // bt,), in_specs=[row, mem, mem], out_specs=row,
        out_shape=jax.ShapeDtypeStruct((T, D), BF16), name="attn_fwd",
        compiler_params=_params(("parallel",)))(q, km, vm)


def _attn_bwd(q, km, vm, do, *, bt=256):
    T = q.shape[0]
    M = km.shape[0]
    bt = min(bt, T)
    scale = MEM_HD ** -0.5

    def body(q_ref, k_ref, v_ref, do_ref, dq_ref, dk_ref, dv_ref):
        i = pl.program_id(0)

        @pl.when(i == 0)
        def _():
            dk_ref[...] = jnp.zeros_like(dk_ref)
            dv_ref[...] = jnp.zeros_like(dv_ref)

        sls = [slice(h * MEM_HD, (h + 1) * MEM_HD) for h in range(MEM_H)]
        ss = [_dot(q_ref[:, sl], k_ref[:, sl], _NT) * scale for sl in sls]
        dps = [_dot(do_ref[:, sl], v_ref[:, sl], _NT) for sl in sls]
        es = [jnp.exp(s - jnp.max(s, axis=1, keepdims=True)) for s in ss]
        ps = [e / jnp.sum(e, axis=1, keepdims=True) for e in es]
        dss = [p * (dp - jnp.sum(dp * p, axis=1, keepdims=True)) * scale for p, dp in zip(ps, dps)]
        for sl, p, ds in zip(sls, ps, dss):
            dq_ref[:, sl] = _dot(ds, k_ref[:, sl]).astype(dq_ref.dtype)
            dk_ref[:, sl] += _dot(ds, q_ref[:, sl], _TN)
            dv_ref[:, sl] += _dot(p, do_ref[:, sl], _TN)

    row = pl.BlockSpec((bt, D), lambda i: (i, 0))
    mem = pl.BlockSpec((M, D), lambda i: (0, 0))
    return pl.pallas_call(
        body, grid=(T // bt,), in_specs=[row, mem, mem, row], out_specs=[row, mem, mem],
        out_shape=[jax.ShapeDtypeStruct((T, D), BF16), jax.ShapeDtypeStruct((M, D), F32),
                   jax.ShapeDtypeStruct((M, D), F32)],
        name="attn_bwd", compiler_params=_params(("arbitrary",)))(q, km, vm, do)


def _conv_apply(halo, x, w_ref, b_ref):
    bt, hr = x.shape[0], halo.shape[0]
    cat = jnp.concatenate([halo, x], axis=0)
    y = x * w_ref[3:4, :]
    for k in range(CONV_K - 1):
        y = y + pltpu.roll(cat, CONV_K - 1 - k, 0)[hr:hr + bt] * w_ref[k:k + 1, :]
    if b_ref is not None:
        y = y + b_ref[...]
    return y


def _l2_parts(act, bc):
    out = []
    for s in range(bc // 128):
        a = act[:, s * 128:(s + 1) * 128]
        r = lax.rsqrt(jnp.sum(a * a, axis=1, keepdims=True) + EPS)
        out.append((a, r))
    return out


def _conv_fwd(p, col0, C, w, b, *, l2, name, bt=512, bc=1024):
    T = p.shape[0]
    bt = min(bt, T)
    c0, hb = col0 // bc, bt // P_HALO
    has_b = b is not None
    assert not l2 or (bc == D and C == 2 * D)

    def body(x_ref, halo_ref, w_ref, *rest):
        b_ref = rest[0] if has_b else None
        o_ref = rest[-1]
        i, j = pl.program_id(0), pl.program_id(1)
        x = x_ref[...].astype(F32)
        halo = jnp.where(i > 0, halo_ref[...].astype(F32), 0.0)
        act = _silu(_conv_apply(halo, x, w_ref, b_ref))
        if l2:
            sc = jnp.where(j == 0, GDN_DK ** -0.5, 1.0)
            o_ref[...] = jnp.concatenate([a * (r * sc) for a, r in _l2_parts(act, bc)], axis=1)
        else:
            o_ref[...] = act

    in_specs = [pl.BlockSpec((bt, bc), lambda i, j: (i, c0 + j)),
                pl.BlockSpec((P_HALO, bc), lambda i, j: (jnp.maximum(i * hb - 1, 0), c0 + j)),
                pl.BlockSpec((CONV_K, bc), lambda i, j: (0, j))]
    ins = [p, p, w]
    if has_b:
        in_specs.append(pl.BlockSpec((1, bc), lambda i, j: (0, j)))
        ins.append(b.reshape(1, C))
    return pl.pallas_call(
        body, grid=(T // bt, C // bc), in_specs=in_specs,
        out_specs=pl.BlockSpec((bt, bc), lambda i, j: (i, j)),
        out_shape=jax.ShapeDtypeStruct((T, C), F32), name=name,
        compiler_params=_params(("parallel", "parallel")))(*ins)


def _conv_bwd_act(p, col0, C, w, b, dact, dcol0, *, l2, name, bt=512, bc=1024):
    T = p.shape[0]
    bt = min(bt, T)
    c0, d0, hb = col0 // bc, dcol0 // bc, bt // P_HALO
    has_b = b is not None
    assert not l2 or (bc == D and C == 2 * D)

    def body(x_ref, halo_ref, w_ref, *rest):
        b_ref = rest[0] if has_b else None
        dact_ref, dy_ref, dw_ref, db_ref = rest[-4:]
        j, i = pl.program_id(0), pl.program_id(1)
        x = x_ref[...].astype(F32)
        halo = jnp.where(i > 0, halo_ref[...].astype(F32), 0.0)
        y = _conv_apply(halo, x, w_ref, b_ref)
        dact = dact_ref[...]
        sg = _sigmoid(y)
        if l2:
            sc = jnp.where(j == 0, GDN_DK ** -0.5, 1.0)
            parts = []
            for s, (a, r) in enumerate(_l2_parts(y * sg, bc)):
                n = a * r
                dn = dact[:, s * 128:(s + 1) * 128]
                parts.append((r * sc) * (dn - n * jnp.sum(dn * n, axis=1, keepdims=True)))
            dact = jnp.concatenate(parts, axis=1)
        dy = dact * (sg * (1.0 + y * (1.0 - sg)))
        dy_ref[...] = dy

        @pl.when(i == 0)
        def _():
            dw_ref[...] = jnp.zeros_like(dw_ref)
            db_ref[...] = jnp.zeros_like(db_ref)

        db_ref[...] += jnp.sum(dy, axis=0, keepdims=True)
        cat = jnp.concatenate([halo, x], axis=0)
        dw_ref[3:4, :] += jnp.sum(dy * x, axis=0, keepdims=True)
        for k in range(CONV_K - 1):
            xs = pltpu.roll(cat, CONV_K - 1 - k, 0)[P_HALO:P_HALO + bt]
            dw_ref[k:k + 1, :] += jnp.sum(dy * xs, axis=0, keepdims=True)

    in_specs = [pl.BlockSpec((bt, bc), lambda j, i: (i, c0 + j)),
                pl.BlockSpec((P_HALO, bc), lambda j, i: (jnp.maximum(i * hb - 1, 0), c0 + j)),
                pl.BlockSpec((CONV_K, bc), lambda j, i: (0, j))]
    ins = [p, p, w]
    if has_b:
        in_specs.append(pl.BlockSpec((1, bc), lambda j, i: (0, j)))
        ins.append(b.reshape(1, C))
    in_specs.append(pl.BlockSpec((bt, bc), lambda j, i: (i, d0 + j)))
    ins.append(dact)
    dy, dw, db = pl.pallas_call(
        body, grid=(C // bc, T // bt), in_specs=in_specs,
        out_specs=[pl.BlockSpec((bt, bc), lambda j, i: (i, j)),
                   pl.BlockSpec((CONV_K, bc), lambda j, i: (0, j)),
                   pl.BlockSpec((1, bc), lambda j, i: (0, j))],
        out_shape=[jax.ShapeDtypeStruct((T, C), F32), jax.ShapeDtypeStruct((CONV_K, C), F32),
                   jax.ShapeDtypeStruct((1, C), F32)],
        name=name, compiler_params=_params(("parallel", "arbitrary")))(*ins)
    return dy, dw, db.reshape(C)


def _conv_bwd_in(dy, w, dp_in, col0, T, *, name, bt=512, bc=1024):
    C = dy.shape[1]
    bt = min(bt, T)
    c0, hb, nb = col0 // bc, bt // 8, T // bt

    def body(dy_ref, nxt_ref, w_ref, *rest):
        o_ref = rest[-1]
        i = pl.program_id(0)
        dy_v = dy_ref[...]
        nxt = jnp.where(i < nb - 1, nxt_ref[...], 0.0)
        cat = jnp.concatenate([dy_v, nxt], axis=0)
        dx = dy_v * w_ref[3:4, :]
        for k in range(CONV_K - 1):
            s = CONV_K - 1 - k
            dx = dx + pltpu.roll(cat, bt + 8 - s, 0)[0:bt] * w_ref[k:k + 1, :]
        o_ref[...] = dx.astype(o_ref.dtype)

    in_specs = [pl.BlockSpec((bt, bc), lambda i, j: (i, j)),
                pl.BlockSpec((8, bc), lambda i, j: (jnp.minimum((i + 1) * hb, T // 8 - 1), j)),
                pl.BlockSpec((CONV_K, bc), lambda i, j: (0, j))]
    ins = [dy, dy, w]
    alias = {}
    if dp_in is not None:
        in_specs.append(pl.BlockSpec(memory_space=pl.ANY))
        ins.append(dp_in)
        alias = {3: 0}
    return pl.pallas_call(
        body, grid=(nb, C // bc), in_specs=in_specs,
        out_specs=pl.BlockSpec((bt, bc), lambda i, j: (i, c0 + j)),
        out_shape=jax.ShapeDtypeStruct((T, C_TOT), BF16), input_output_aliases=alias, name=name,
        compiler_params=_params(("parallel", "parallel")))(*ins)


def _expand_mats(shift, row0):
    e = (_iota2((128, D), 0) - row0 == (_iota2((128, D), 1) >> shift)).astype(F32)
    et = ((_iota2((D, 128), 0) >> shift) == _iota2((D, 128), 1) - row0).astype(F32)
    return e, et


def _cum_mats(chunk):
    ri, ci = _iota2((chunk, chunk), 0), _iota2((chunk, chunk), 1)
    return (ri >= ci).astype(F32), (ri <= ci).astype(F32)


def _gdn_gates_fwd(p, alog_row, dtb_row, *, bt=256):
    T = p.shape[0]
    bt = min(bt, T)

    def body(g_ref, al_ref, db_ref, bg_ref):
        gt = g_ref[...]
        lc, _ = _cum_mats(GDN_C)
        g_l = -jnp.exp(al_ref[...]) * _softplus(gt + db_ref[...])
        bg_ref[...] = jnp.where(_iota2((bt, 128), 1) < GDN_H, _sigmoid(gt), _chunk_cumsum(g_l, lc, GDN_C))

    vec = pl.BlockSpec((1, 128), lambda i: (0, 0))
    seg = pl.BlockSpec((bt, 128), lambda i: (i, 0))
    return pl.pallas_call(
        body, grid=(T // bt,), in_specs=[seg, vec, vec], out_specs=seg,
        out_shape=jax.ShapeDtypeStruct((T, 128), F32), name="gdn_gates_fwd",
        compiler_params=_params(("parallel",)))(p, alog_row, dtb_row)


def _gdn_gates_bwd(p, alog_row, dtb_row, dbg, dp_in, *, bt=256):
    T = p.shape[0]
    bt = min(bt, T)

    def body(g_ref, al_ref, db_ref, dbg_ref, dpin_ref, dg_out, dal_ref, ddb_ref):
        i = pl.program_id(0)
        gt = g_ref[...]
        lane = _iota2((bt, 128), 1)
        _, uc = _cum_mats(GDN_C)
        ea = jnp.exp(al_ref[...])
        zz = gt + db_ref[...]
        g_l = -ea * _softplus(zz)
        beta_l = _sigmoid(gt)
        dbg_v = dbg_ref[...]
        dg_l = jnp.where((lane >= GDN_H) & (lane < 2 * GDN_H), _chunk_cumsum(dbg_v, uc, GDN_C), 0.0)
        dbeta_l = jnp.where(lane < GDN_H, dbg_v, 0.0)
        da = dg_l * (-ea) * _sigmoid(zz)
        dg_out[...] = (da + dbeta_l * beta_l * (1.0 - beta_l)).astype(dg_out.dtype)

        @pl.when(i == 0)
        def _():
            dal_ref[...] = jnp.zeros_like(dal_ref)
            ddb_ref[...] = jnp.zeros_like(ddb_ref)

        dal_ref[...] += jnp.sum(dg_l * g_l, axis=0, keepdims=True)
        ddb_ref[...] += jnp.sum(da, axis=0, keepdims=True)

    vec = pl.BlockSpec((1, 128), lambda i: (0, 0))
    seg = pl.BlockSpec((bt, 128), lambda i: (i, 0))
    gate = pl.BlockSpec((bt, 128), lambda i: (i, C_GATE // 128))
    return pl.pallas_call(
        body, grid=(T // bt,), in_specs=[seg, vec, vec, seg, _ANY], out_specs=[gate, vec, vec],
        out_shape=[jax.ShapeDtypeStruct((T, C_TOT), BF16), jax.ShapeDtypeStruct((1, 128), F32),
                   jax.ShapeDtypeStruct((1, 128), F32)],
        input_output_aliases={4: 0}, name="gdn_gates_bwd",
        compiler_params=_params(("arbitrary",)))(p, alog_row, dtb_row, dbg, dp_in)


def _ssd_dt_fwd(p, dtb_row, alog_row, *, bt=256):
    T = p.shape[0]
    bt = min(bt, T)

    def body(d_ref, db_ref, al_ref, da_ref):
        lc, _ = _cum_mats(SSM_L)
        dt_l = _softplus(d_ref[...] + db_ref[...])
        alpha_l = _chunk_cumsum(dt_l * (-jnp.exp(al_ref[...])), lc, SSM_L)
        da_ref[...] = jnp.where(_iota2((bt, 128), 1) < SSM_H, dt_l, pltpu.roll(alpha_l, SSM_H, 1))

    v128 = pl.BlockSpec((1, 128), lambda i: (0, 0))
    return pl.pallas_call(
        body, grid=(T // bt,), in_specs=[pl.BlockSpec((bt, 128), lambda i: (i, 1)), v128, v128],
        out_specs=pl.BlockSpec((bt, 128), lambda i: (i, 0)), out_shape=jax.ShapeDtypeStruct((T, 128), F32),
        name="ssd_dt_fwd", compiler_params=_params(("parallel",)))(p, dtb_row, alog_row)


def _ssd_dt_bwd(p, dtb_row, alog_row, dda, dp_in, *, bt=256):
    T = p.shape[0]
    bt = min(bt, T)

    def body(d_ref, db_ref, al_ref, dda_ref, dpin_ref, dd_out, ddb_ref, dalog_ref):
        i = pl.program_id(0)
        heads = _iota2((bt, 128), 1) < SSM_H
        _, uc = _cum_mats(SSM_L)
        zz = d_ref[...] + db_ref[...]
        dt_l = _softplus(zz)
        a_row = -jnp.exp(al_ref[...])
        dda_v = dda_ref[...]
        da_l = _chunk_cumsum(jnp.where(heads, pltpu.roll(dda_v, 128 - SSM_H, 1), 0.0), uc, SSM_L)
        draw = jnp.where(heads, (dda_v + da_l * a_row) * _sigmoid(zz), 0.0)
        dd_out[...] = draw.astype(dd_out.dtype)

        @pl.when(i == 0)
        def _():
            ddb_ref[...] = jnp.zeros_like(ddb_ref)
            dalog_ref[...] = jnp.zeros_like(dalog_ref)

        ddb_ref[...] += jnp.sum(draw, axis=0, keepdims=True)
        dalog_ref[...] += jnp.sum(da_l * dt_l, axis=0, keepdims=True) * a_row

    seg = pl.BlockSpec((bt, 128), lambda i: (i, C_DT // 128))
    v128 = pl.BlockSpec((1, 128), lambda i: (0, 0))
    return pl.pallas_call(
        body, grid=(T // bt,),
        in_specs=[pl.BlockSpec((bt, 128), lambda i: (i, 1)), v128, v128, pl.BlockSpec((bt, 128), lambda i: (i, 0)), _ANY],
        out_specs=[seg, v128, v128],
        out_shape=[jax.ShapeDtypeStruct((T, C_TOT), BF16), jax.ShapeDtypeStruct((1, 128), F32),
                   jax.ShapeDtypeStruct((1, 128), F32)],
        input_output_aliases={4: 0}, name="ssd_dt_bwd",
        compiler_params=_params(("arbitrary",)))(p, dtb_row, alog_row, dda, dp_in)


def _gdn_post_fwd(o, p, w_x, *, bt=256):
    T = o.shape[0]
    bt = min(bt, T)

    def body(o_ref, z_ref, w_ref, out_ref):
        for h in range(GDN_H):
            sl = slice(h * 128, (h + 1) * 128)
            oh = o_ref[:, sl].astype(F32)
            r = lax.rsqrt(jnp.mean(oh * oh, axis=1, keepdims=True) + EPS)
            out_ref[:, sl] = (oh * r * w_ref[:, sl] * _silu(z_ref[:, sl].astype(F32))).astype(out_ref.dtype)

    row = pl.BlockSpec((bt, D), lambda i: (i, 0))
    return pl.pallas_call(
        body, grid=(T // bt,),
        in_specs=[row, pl.BlockSpec((bt, D), lambda i: (i, C_ZG // D)), pl.BlockSpec((1, D), lambda i: (0, 0))],
        out_specs=row, out_shape=jax.ShapeDtypeStruct((T, 2 * D), BF16), name="gdn_post_fwd",
        compiler_params=_params(("parallel",)))(o, p, w_x)


def _gdn_post_bwd(dx1, w_out, o, p, w_x, *, bt=512):
    T = o.shape[0]
    bt = min(bt, T)

    def body(dx_ref, wo_ref, o_ref, z_ref, w_ref, do_ref, dz_ref, dw_ref):
        i = pl.program_id(0)

        @pl.when(i == 0)
        def _():
            dw_ref[...] = jnp.zeros_like(dw_ref)

        dmix = _dot(dx_ref[...], wo_ref[...], _NT)
        for h in range(GDN_H):
            sl = slice(h * 128, (h + 1) * 128)
            oh, zh, wh = o_ref[:, sl].astype(F32), z_ref[:, sl].astype(F32), w_ref[:, sl]
            dm = dmix[:, sl]
            r = lax.rsqrt(jnp.mean(oh * oh, axis=1, keepdims=True) + EPS)
            ohat = oh * r
            dy = dm * _silu(zh)
            dz_ref[:, sl] = (dm * ohat * wh * _dsilu(zh)).astype(dz_ref.dtype)
            dohat = dy * wh
            do_ref[:, sl] = (r * (dohat - ohat * jnp.mean(dohat * ohat, axis=1, keepdims=True))).astype(do_ref.dtype)
            dw_ref[:, sl] += jnp.sum(dy * ohat, axis=0, keepdims=True)

    row = pl.BlockSpec((bt, D), lambda i: (i, 0))
    zcol = pl.BlockSpec((bt, D), lambda i: (i, C_ZG // D))
    vec = pl.BlockSpec((1, D), lambda i: (0, 0))
    return pl.pallas_call(
        body, grid=(T // bt,), in_specs=[row, pl.BlockSpec((D, D), lambda i: (0, 0)), row, zcol, vec],
        out_specs=[row, zcol, vec],
        out_shape=[jax.ShapeDtypeStruct((T, D), BF16), jax.ShapeDtypeStruct((T, C_TOT), BF16),
                   jax.ShapeDtypeStruct((1, D), F32)],
        name="gdn_post_bwd", compiler_params=_params(("arbitrary",)))(dx1, w_out, o, p, w_x)


def _ssd_post_fwd(y, xs, p, d_x, w, mix_in, *, bt=256):
    T = y.shape[0]
    bt = min(bt, T)

    def body(y_ref, x_ref, z_ref, d_ref, w_ref, mix_ref, out_ref):
        yg = (y_ref[...].astype(F32) + x_ref[...] * d_ref[...]) * _silu(z_ref[...].astype(F32))
        for g in range(2):
            sl = slice(g * 512, (g + 1) * 512)
            a = yg[:, sl]
            r = lax.rsqrt(jnp.mean(a * a, axis=1, keepdims=True) + EPS)
            out_ref[:, sl] = (a * r * w_ref[:, sl]).astype(out_ref.dtype)

    row = pl.BlockSpec((bt, D), lambda i: (i, 0))
    vec = pl.BlockSpec((1, D), lambda i: (0, 0))
    return pl.pallas_call(
        body, grid=(T // bt,),
        in_specs=[row, row, pl.BlockSpec((bt, D), lambda i: (i, C_ZS // D)), vec, vec, _ANY],
        out_specs=pl.BlockSpec((bt, D), lambda i: (i, 1)), out_shape=jax.ShapeDtypeStruct((T, 2 * D), BF16),
        input_output_aliases={5: 0}, name="ssd_post_fwd",
        compiler_params=_params(("parallel",)))(y, xs, p, d_x, w, mix_in)


def _ssd_post_bwd(dx1, w_out, y, xs, p, d_x, w, dp_in, *, bt=512):
    T = y.shape[0]
    bt = min(bt, T)

    def body(dx_ref, wo_ref, y_ref, x_ref, z_ref, d_ref, w_ref, dpin_ref, dyy_ref, dz_ref, dd_ref, dw_ref):
        i = pl.program_id(0)

        @pl.when(i == 0)
        def _():
            dd_ref[...] = jnp.zeros_like(dd_ref)
            dw_ref[...] = jnp.zeros_like(dw_ref)

        dmix = _dot(dx_ref[...], wo_ref[...], _NT)
        xv, zv = x_ref[...], z_ref[...].astype(F32)
        yy = y_ref[...].astype(F32) + xv * d_ref[...]
        sz = _silu(zv)
        yg = yy * sz
        parts = []
        for g in range(2):
            sl = slice(g * 512, (g + 1) * 512)
            a = yg[:, sl]
            r = lax.rsqrt(jnp.mean(a * a, axis=1, keepdims=True) + EPS)
            ah = a * r
            dout = dmix[:, sl]
            dah = dout * w_ref[:, sl]
            dw_ref[:, sl] += jnp.sum(dout * ah, axis=0, keepdims=True)
            parts.append(r * (dah - ah * jnp.mean(dah * ah, axis=1, keepdims=True)))
        dyg = jnp.concatenate(parts, axis=1)
        dyy = dyg * sz
        dyy_ref[...] = dyy
        dz_ref[...] = (dyg * yy * _dsilu(zv)).astype(dz_ref.dtype)
        dd_ref[...] += jnp.sum(dyy * xv, axis=0, keepdims=True)

    row = pl.BlockSpec((bt, D), lambda i: (i, 0))
    zcol = pl.BlockSpec((bt, D), lambda i: (i, C_ZS // D))
    vec = pl.BlockSpec((1, D), lambda i: (0, 0))
    return pl.pallas_call(
        body, grid=(T // bt,),
        in_specs=[row, pl.BlockSpec((D, D), lambda i: (1, 0)), row, row, zcol, vec, vec, _ANY],
        out_specs=[row, zcol, vec, vec],
        out_shape=[jax.ShapeDtypeStruct((T, D), F32), jax.ShapeDtypeStruct((T, C_TOT), BF16),
                   jax.ShapeDtypeStruct((1, D), F32), jax.ShapeDtypeStruct((1, D), F32)],
        input_output_aliases={7: 1}, name="ssd_post_bwd",
        compiler_params=_params(("arbitrary",)))(dx1, w_out, y, xs, p, d_x, w, dp_in)


_NEG = -1e30


def _gdn_terms(q, k, v, bx, gam_c):
    C = GDN_C
    ri, ci = _iota2((C, C), 0), _iota2((C, C), 1)
    eye, low, strict = ri == ci, ri >= ci, ri > ci
    gam_r = jnp.sum(jnp.where(eye, gam_c, 0.0), axis=0, keepdims=True)
    G = jnp.exp(jnp.where(low, gam_c - gam_r, _NEG))
    glast = jnp.sum(jnp.where(_iota2((C, 1), 0) == C - 1, gam_c, 0.0), axis=0, keepdims=True)
    eg, egl, eL = jnp.exp(gam_c), jnp.exp(glast - gam_c), jnp.exp(glast)
    kb, vb = k * bx, v * bx
    M = _dot(kb, k, _NT)
    return dict(eye=eye, low=low, strict=strict, G=G, eg=eg, egl=egl, eL=eL, kb=kb, vb=vb, M=M,
                kbg=kb * eg, qd=q * eg, kd=k * egl, q=q, k=k, v=v, bx=bx)


def _split(a):
    hi = a.astype(_MXU)
    return hi, (a - hi.astype(F32)).astype(_MXU)


def _dot3s(a, b):
    d = lambda p, q: lax.dot_general(p, q, _NN, preferred_element_type=F32)
    return d(a[0], b[0]) + d(a[0], b[1]) + d(a[1], b[0])


def _tri_inv_many(Ls, eye):
    eyef = jnp.where(eye, 1.0, 0.0)
    Ts = [eyef - L for L in Ls]
    Ps = [-L for L in Ls]
    for _ in range(5):
        sp = [_split(p) for p in Ps]
        Ps = [_dot3s(s, s) for s in sp]
        sp = [_split(p) for p in Ps]
        st = [_split(t) for t in Ts]
        Ts = [t + _dot3s(a, b) for t, a, b in zip(Ts, st, sp)]
    return Ts


def _lane_col(tile, idx):
    return jnp.sum(jnp.where(_iota2(tile.shape, 1) == idx, tile, 0.0), axis=1, keepdims=True)


def _gdn_heads(q_ref, k_ref, v_ref, bg_ref, heads):
    out = []
    bg = bg_ref[...]
    for h in heads:
        sl = slice(h * 128, (h + 1) * 128)
        out.append(_gdn_terms(q_ref[:, sl], k_ref[:, sl], v_ref[:, sl], _lane_col(bg, h), _lane_col(bg, GDN_H + h)))
    return out


def _gdn_prep(qk, v, bg, ride=None):
    T = qk.shape[0]
    N = T // GDN_C
    C, CS = GDN_C, GDN_LOCAL_CHUNKS
    NB = N // CS
    n_ride = ride.n if ride else 0

    def body(q_ref, k_ref, v_ref, bg_ref, *rest):
        ride_in = rest[:n_ride]
        u_ref, w_ref, qd_ref, kd_ref, p_ref, t_ref = rest[n_ride:n_ride + 6]
        ride_out = rest[n_ride + 6:2 * n_ride + 6]
        if ride:
            @pl.when(pl.program_id(0) == 0)
            def _():
                ride.start(ride_in, ride_out, rest[-3:])

            @pl.when(pl.program_id(0) == NB - 1)
            def _():
                ride.wait(ride_in, ride_out, rest[-3:])

        items = [(c, h) for c in range(CS) for h in range(GDN_H)]
        views = [[r.at[pl.ds(c * C, C)] for r in (q_ref, k_ref, v_ref, bg_ref)] for c in range(CS)]
        ts = [_gdn_heads(*views[c], [h])[0] for c, h in items]
        Ts = _tri_inv_many([jnp.where(t["strict"], t["M"] * t["G"], 0.0) for t in ts], ts[0]["eye"])
        for (c, h), t, Tm in zip(items, ts, Ts):
            tok = slice(c * C, (c + 1) * C)
            sl = slice(h * 128, (h + 1) * 128)
            rows = slice(h * C, (h + 1) * C)
            u_ref[tok, sl] = _dot(Tm, t["vb"])
            w_ref[tok, sl] = _dot(Tm, t["kbg"]).astype(w_ref.dtype)
            qd_ref[tok, sl] = t["qd"].astype(qd_ref.dtype)
            kd_ref[tok, sl] = t["kd"].astype(kd_ref.dtype)
            p_ref[c, rows, :] = _dot(t["q"], t["k"], _NT) * t["G"]
            t_ref[c, rows, :] = Tm

    blk = lambda c: pl.BlockSpec((CS * C, D), lambda n: (n, c))
    sq = pl.BlockSpec((CS, GDN_H * C, C), lambda n: (n, 0, 0))
    in_specs = [blk(0), blk(1), blk(0), pl.BlockSpec((CS * C, 128), lambda n: (n, 0))]
    out_specs = [blk(0), blk(0), blk(0), blk(0), sq, sq]
    out_shape = [jax.ShapeDtypeStruct((T, D), F32), jax.ShapeDtypeStruct((T, D), BF16),
                 jax.ShapeDtypeStruct((T, D), BF16), jax.ShapeDtypeStruct((T, D), BF16),
                 jax.ShapeDtypeStruct((N, GDN_H * C, C), F32), jax.ShapeDtypeStruct((N, GDN_H * C, C), F32)]
    ins = [qk, qk, v, bg]
    if ride:
        ins, in_specs = ins + ride.srcs, in_specs + ride.specs
        out_shape, out_specs = out_shape + ride.out_shape, out_specs + ride.specs
    res = pl.pallas_call(
        body, grid=(NB,), in_specs=in_specs, out_specs=out_specs, out_shape=out_shape,
        scratch_shapes=ride.scratch if ride else [], name="gdn_prep",
        compiler_params=_params(("arbitrary",) if ride else ("parallel",)))(*ins)
    return (list(res[:6]), list(res[6:])) if ride else list(res)


def _gdn_scan_fwd(u, w, qd, kd, pm, bg):
    T = u.shape[0]
    N = T // GDN_C
    C, CS = GDN_C, GDN_SCAN_CHUNKS

    def body(u_ref, w_ref, qd_ref, kd_ref, p_ref, bg_ref, o_ref, vn_ref, ss_ref, S_scr):
        n = pl.program_id(0)

        @pl.when(n == 0)
        def _():
            S_scr[...] = jnp.zeros_like(S_scr)

        sls = [slice(h * 128, (h + 1) * 128) for h in range(GDN_H)]
        for c in range(CS):
            rows = slice(c * C, (c + 1) * C)
            glast = bg_ref[(c + 1) * C - 1:(c + 1) * C, :]
            Ss = [S_scr[:, sl] for sl in sls]
            vns = [u_ref[rows, sl] - _dot(w_ref[rows, sl], S) for sl, S in zip(sls, Ss)]
            for h, (sl, S, vn) in enumerate(zip(sls, Ss, vns)):
                ss_ref[c, :, sl] = S.astype(ss_ref.dtype)
                vn_ref[rows, sl] = vn.astype(vn_ref.dtype)
                o_ref[rows, sl] = (_dot(qd_ref[rows, sl], S)
                                   + _dot(p_ref[c, h * C:(h + 1) * C, :], vn)).astype(o_ref.dtype)
                S_scr[:, sl] = S * jnp.exp(_lane_col(glast, GDN_H + h)) + _dot(kd_ref[rows, sl], vn, _TN)

    blk = pl.BlockSpec((CS * C, D), lambda n: (n, 0))
    return dict(
        body=body, steps=N // CS, ins=[u, w, qd, kd, pm, bg],
        in_specs=[blk, blk, blk, blk, pl.BlockSpec((CS, GDN_H * C, C), lambda n: (n, 0, 0)),
                  pl.BlockSpec((CS * C, 128), lambda n: (n, 0))],
        out_specs=[blk, blk, pl.BlockSpec((CS, GDN_DK, D), lambda n: (n, 0, 0))],
        out_shape=[jax.ShapeDtypeStruct((T, D), BF16), jax.ShapeDtypeStruct((T, D), BF16),
                   jax.ShapeDtypeStruct((N, GDN_DK, D), BF16)],
        scratch=[pltpu.VMEM((GDN_DK, D), F32)])


def _gdn_scan_bwd(w, qd, kd, pm, bg, do):
    T = w.shape[0]
    N = T // GDN_C
    C, CS = GDN_C, GDN_SCAN_CHUNKS
    NB = N // CS

    def body(w_ref, qd_ref, kd_ref, p_ref, bg_ref, do_ref, dvn_ref, ds_ref, dS_scr):
        n = pl.program_id(0)

        @pl.when(n == 0)
        def _():
            dS_scr[...] = jnp.zeros_like(dS_scr)

        sls = [slice(h * 128, (h + 1) * 128) for h in range(GDN_H)]
        for c in reversed(range(CS)):
            rows = slice(c * C, (c + 1) * C)
            glast = bg_ref[(c + 1) * C - 1:(c + 1) * C, :]
            dSs = [dS_scr[:, sl] for sl in sls]
            dvns = [_dot(p_ref[c, h * C:(h + 1) * C, :], do_ref[rows, sl], _TN) + _dot(kd_ref[rows, sl], dS2)
                    for h, (sl, dS2) in enumerate(zip(sls, dSs))]
            for h, (sl, dS2, dvn) in enumerate(zip(sls, dSs, dvns)):
                ds_ref[c, :, sl] = dS2.astype(ds_ref.dtype)
                dvn_ref[rows, sl] = dvn.astype(dvn_ref.dtype)
                dS_scr[:, sl] = (dS2 * jnp.exp(_lane_col(glast, GDN_H + h))
                                 + _dot(qd_ref[rows, sl], do_ref[rows, sl], _TN) - _dot(w_ref[rows, sl], dvn, _TN))

    blk = pl.BlockSpec((CS * C, D), lambda n: (NB - 1 - n, 0))
    return dict(
        body=body, steps=NB, ins=[w, qd, kd, pm, bg, do],
        in_specs=[blk, blk, blk, pl.BlockSpec((CS, GDN_H * C, C), lambda n: (NB - 1 - n, 0, 0)),
                  pl.BlockSpec((CS * C, 128), lambda n: (NB - 1 - n, 0)), blk],
        out_specs=[blk, pl.BlockSpec((CS, GDN_DK, D), lambda n: (NB - 1 - n, 0, 0))],
        out_shape=[jax.ShapeDtypeStruct((T, D), BF16), jax.ShapeDtypeStruct((N, GDN_DK, D), BF16)],
        scratch=[pltpu.VMEM((GDN_DK, D), F32)])


def _gdn_rest_bwd(qk, v, bg, s_save, t_save, vn, dvn, ds_save, do, ride=None):
    T = qk.shape[0]
    N = T // GDN_C
    C, CS = GDN_C, GDN_REST_CHUNKS
    NB = N // CS
    n_ride = ride.n if ride else 0

    def body(q_ref, k_ref, v_ref, bg_ref, ss_ref, ts_ref, vn_ref, dvn_ref, ds_ref, do_ref, *rest):
        ride_in = rest[:n_ride]
        dqkv_ref, dbg_ref = rest[n_ride:n_ride + 2]
        ride_out = rest[n_ride + 2:2 * n_ride + 2]
        if ride:
            @pl.when(pl.program_id(0) == 0)
            def _():
                ride.start(ride_in, ride_out, rest[-3:])

            @pl.when(pl.program_id(0) == NB - 1)
            def _():
                ride.wait(ride_in, ride_out, rest[-3:])

        items = [(c, h) for c in range(CS) for h in range(GDN_H)]
        toks = [slice(c * C, (c + 1) * C) for c, _ in items]
        sls = [slice(h * 128, (h + 1) * 128) for _, h in items]
        views = [[r.at[pl.ds(c * C, C)] for r in (q_ref, k_ref, v_ref, bg_ref)] for c in range(CS)]
        ts = [_gdn_heads(*views[c], [h])[0] for c, h in items]
        Ss = [ss_ref[c, :, sl] for (c, _), sl in zip(items, sls)]
        Tms = [ts_ref[c, h * C:(h + 1) * C, :] for c, h in items]
        dS2s = [ds_ref[c, :, sl] for (c, _), sl in zip(items, sls)]
        dos = [do_ref[tok, sl] for tok, sl in zip(toks, sls)]
        vns = [vn_ref[tok, sl] for tok, sl in zip(toks, sls)]
        dvns = [dvn_ref[tok, sl] for tok, sl in zip(toks, sls)]
        Qs = [_dot(t["q"], t["k"], _NT) for t in ts]
        dws = [-_dot(dvn, S, _NT) for dvn, S in zip(dvns, Ss)]
        dqds = [_dot(do, S, _NT) for do, S in zip(dos, Ss)]
        dPs = [jnp.where(t["low"], _dot(do, vn, _NT), 0.0) for t, do, vn in zip(ts, dos, vns)]
        dkds = [_dot(vn, dS2, _NT) for vn, dS2 in zip(vns, dS2s)]
        dTs = [_dot(dvn, t["vb"], _NT) + _dot(dw, t["kbg"], _NT) for t, dvn, dw in zip(ts, dvns, dws)]
        dvbs = [_dot(Tm, dvn, _TN) for Tm, dvn in zip(Tms, dvns)]
        dkbgs = [_dot(Tm, dw, _TN) for Tm, dw in zip(Tms, dws)]
        TdTs = [_dot(Tm, dT, _TN) for Tm, dT in zip(Tms, dTs)]
        dLs = [jnp.where(t["strict"], -_dot(TdT, Tm, _NT), 0.0) for t, TdT, Tm in zip(ts, TdTs, Tms)]
        dMs = [dL * t["G"] for t, dL in zip(ts, dLs)]
        dQs = [dP * t["G"] for t, dP in zip(ts, dPs)]
        dkbs = [_dot(dM, t["k"]) + dkbg * t["eg"] for t, dM, dkbg in zip(ts, dMs, dkbgs)]
        rs = lambda a: jnp.sum(a, axis=1, keepdims=True)
        lane = _iota2((C, 128), 1)
        last = _iota2((C, 1), 0) == C - 1
        dbg = [jnp.zeros((C, 128), F32) for _ in range(CS)]
        for i, (c, h) in enumerate(items):
            t, sl, tok = ts[i], sls[i], toks[i]
            E = (dLs[i] * t["M"] + dPs[i] * Qs[i]) * t["G"]
            dqkv_ref[tok, sl] = _dot(dQs[i], t["k"]) + dqds[i] * t["eg"]
            dqkv_ref[tok, D + h * 128:D + (h + 1) * 128] = (
                _dot(dQs[i], t["q"], _TN) + _dot(dMs[i], t["kb"], _TN) + dkds[i] * t["egl"] + dkbs[i] * t["bx"])
            dqkv_ref[tok, 2 * D + h * 128:2 * D + (h + 1) * 128] = dvbs[i] * t["bx"]
            dbeta_c = rs(dkbs[i] * t["k"] + dvbs[i] * t["v"])
            dkd_kd = dkds[i] * t["kd"]
            dgam_c = rs(dqds[i] * t["qd"]) + rs(dkbgs[i] * t["kbg"]) - rs(dkd_kd) + rs(E)
            dgam_r = -jnp.sum(E, axis=0, keepdims=True)
            dgam_c = dgam_c + jnp.sum(jnp.where(t["eye"], dgam_r, 0.0), axis=1, keepdims=True)
            dlast = _sum_all(dkd_kd) + t["eL"] * _sum_all(Ss[i].astype(F32) * dS2s[i].astype(F32))
            dgam_c = dgam_c + jnp.where(last, dlast, 0.0)
            dbg[c] = dbg[c] + jnp.where(lane == h, dbeta_c, 0.0) + jnp.where(lane == GDN_H + h, dgam_c, 0.0)
        for c in range(CS):
            dbg_ref[c * C:(c + 1) * C, :] = dbg[c]

    blk = lambda c: pl.BlockSpec((CS * C, D), lambda n: (n, c))
    st = pl.BlockSpec((CS, GDN_DK, D), lambda n: (n, 0, 0))
    seg = pl.BlockSpec((CS * C, 128), lambda n: (n, 0))
    in_specs = [blk(0), blk(1), blk(0), seg, st,
                pl.BlockSpec((CS, GDN_H * C, C), lambda n: (n, 0, 0)), blk(0), blk(0), st, blk(0)]
    out_specs = [pl.BlockSpec((CS * C, 3 * D), lambda n: (n, 0)), seg]
    out_shape = [jax.ShapeDtypeStruct((T, 3 * D), F32), jax.ShapeDtypeStruct((T, 128), F32)]
    ins = [qk, qk, v, bg, s_save, t_save, vn, dvn, ds_save, do]
    if ride:
        ins, in_specs = ins + ride.srcs, in_specs + ride.specs
        out_shape, out_specs = out_shape + ride.out_shape, out_specs + ride.specs
    res = pl.pallas_call(
        body, grid=(NB,), in_specs=in_specs, out_specs=out_specs, out_shape=out_shape,
        scratch_shapes=ride.scratch if ride else [], name="gdn_rest_bwd",
        compiler_params=_params(("arbitrary",) if ride else ("parallel",)))(*ins)
    return (list(res[:2]), list(res[2:])) if ride else list(res)


def _ssd_seg(al_pair, half, s):
    L = SSM_L
    ri, ci = _iota2((L, L), 0), _iota2((L, L), 1)
    ac = jnp.max(jnp.where(half == s, al_pair, _NEG), axis=1, keepdims=True)
    ar = jnp.sum(jnp.where(ri == ci, ac, 0.0), axis=0, keepdims=True)
    return jnp.exp(jnp.where(ri >= ci, ac - ar, _NEG))


def _last_row(a):
    return jnp.sum(jnp.where(_iota2((a.shape[0], 1), 0) == a.shape[0] - 1, a, 0.0), axis=0, keepdims=True)


def _ssd_expand(da_ref):
    da = da_ref[...]
    return _dot_sel(da, _expand_mats(6, 0)[0]), _dot_sel(da, _expand_mats(6, SSM_H)[0])


def _ssd_core_fwd(xbc, da):
    T = xbc.shape[0]
    L, CS = SSM_L, SSM_SCAN_CHUNKS
    Nc = T // L

    def body(x_all, bc_all, da_all, y_all, hs_all, H_scr):
        @pl.when(pl.program_id(0) == 0)
        def _():
            H_scr[...] = jnp.zeros_like(H_scr)

        for cc in range(CS):
            rows = pl.ds(cc * L, L)
            chunk(x_all.at[rows], bc_all.at[rows], da_all.at[rows], y_all.at[rows], hs_all.at[cc], H_scr)

    def chunk(x_ref, bc_ref, da_ref, y_ref, hs_ref, H_scr):
        dt_ref, al_ref = _ssd_expand(da_ref)
        half = _iota2((L, 128), 1) >> 6
        for g in range(2):
            gs = slice(g * 512, (g + 1) * 512)
            Bg = bc_ref[:, g * 128:(g + 1) * 128]
            Cg = bc_ref[:, 256 + g * 128:256 + (g + 1) * 128]
            alg = al_ref[:, gs]
            alast = _last_row(alg)
            xdt = x_ref[:, gs] * dt_ref[:, gs]
            Hg = H_scr[:, gs]
            hs_ref[:, gs] = Hg
            CB = _dot(Cg, Bg, _NT)
            y_off = jnp.exp(alg) * _dot(Cg, Hg)
            H_scr[:, gs] = Hg * jnp.exp(alast) + _dot(Bg, jnp.exp(alast - alg) * xdt, _TN)
            for j in range(4):
                ps = slice(g * 512 + j * 128, g * 512 + (j + 1) * 128)
                al_pair = al_ref[:, ps]
                xp = x_ref[:, ps] * dt_ref[:, ps]
                ys = [_dot(_ssd_seg(al_pair, half, s) * CB, xp) for s in range(2)]
                y_ref[:, ps] = (y_off[:, j * 128:(j + 1) * 128]
                                + jnp.where(half == 0, ys[0], ys[1])).astype(y_ref.dtype)

    row = pl.BlockSpec((CS * L, D), lambda c: (c, 0))
    return dict(
        body=body, steps=Nc // CS, ins=[xbc, xbc, da],
        in_specs=[row, pl.BlockSpec((CS * L, 512), lambda c: (c, 2)), pl.BlockSpec((CS * L, 128), lambda c: (c, 0))],
        out_specs=[row, pl.BlockSpec((CS, SSM_N, D), lambda c: (c, 0, 0))],
        out_shape=[jax.ShapeDtypeStruct((T, D), BF16), jax.ShapeDtypeStruct((Nc, SSM_N, D), F32)],
        scratch=[pltpu.VMEM((SSM_N, D), F32)])


def _ssd_core_bwd(xbc, da, h_save, dyy, d_x):
    T = xbc.shape[0]
    L, CS = SSM_L, SSM_SCAN_CHUNKS
    Nc = T // L
    NB = Nc // CS

    def body(x_all, bc_all, da_all, hs_all, dy_all, d_ref, dx_all, dda_all, dH_scr, ddt_ref, dal_ref):
        @pl.when(pl.program_id(0) == 0)
        def _():
            dH_scr[...] = jnp.zeros_like(dH_scr)

        for cc in reversed(range(CS)):
            rows = pl.ds(cc * L, L)
            chunk(x_all.at[rows], bc_all.at[rows], da_all.at[rows], hs_all.at[cc], dy_all.at[rows],
                  d_ref, dx_all.at[rows], ddt_ref, dal_ref, dH_scr)
            dda_all[rows, :] = (_dot_sel(ddt_ref[...], _expand_mats(6, 0)[1])
                                + _dot_sel(dal_ref[...], _expand_mats(6, SSM_H)[1]))

    def chunk(x_ref, bc_ref, da_ref, hs_ref, dy_ref, d_ref, dx_ref, ddt_ref, dal_ref, dH_scr):
        dt_ref, al_ref = _ssd_expand(da_ref)
        lane = _iota2((L, 128), 1)
        half = lane >> 6
        rowi = _iota2((L, 1), 0)
        ri, ci = _iota2((L, L), 0), _iota2((L, L), 1)
        for g in range(2):
            gs = slice(g * 512, (g + 1) * 512)
            Bg = bc_ref[:, g * 128:(g + 1) * 128]
            Cg = bc_ref[:, 256 + g * 128:256 + (g + 1) * 128]
            alg = al_ref[:, gs]
            alast = _last_row(alg)
            eal, edec, eL = jnp.exp(alg), jnp.exp(alast - alg), jnp.exp(alast)
            xg, dtg, dYg = x_ref[:, gs], dt_ref[:, gs], dy_ref[:, gs]
            xdt = xg * dtg
            Hg = hs_ref[:, gs]
            dH2 = dH_scr[:, gs]
            CB = _dot(Cg, Bg, _NT)
            dYe = eal * dYg
            dH_scr[:, gs] = dH2 * eL + _dot(Cg, dYe, _TN)
            dC = _dot(dYe, Hg, _NT)
            zg = edec * xdt
            dz = _dot(Bg, dH2)
            dB = _dot(zg, dH2, _NT)
            tz = dz * zg
            dal = dYe * _dot(Cg, Hg) - tz
            dalast = jnp.sum(tz, axis=0, keepdims=True) + eL * jnp.sum(Hg * dH2, axis=0, keepdims=True)
            dal = dal + jnp.where(rowi == L - 1, dalast, 0.0)
            dxdt_g = edec * dz
            dx_ref[:, gs] = dxdt_g * dtg + dYg * d_ref[:, gs]
            ddt_ref[:, gs] = dxdt_g * xg
            dal_ref[:, gs] = dal
            dCB = jnp.zeros((L, L), F32)
            for j in range(4):
                ps = slice(g * 512 + j * 128, g * 512 + (j + 1) * 128)
                al_pair = al_ref[:, ps]
                xp = x_ref[:, ps] * dt_ref[:, ps]
                dYp = dy_ref[:, ps]
                dxp = []
                dal_p = jnp.zeros((L, 128), F32)
                for s in range(2):
                    seg = _ssd_seg(al_pair, half, s)
                    W = seg * CB
                    dW = _dot(jnp.where(half == s, dYp, 0.0), xp, _NT)
                    dxp.append(_dot(W, dYp, _TN))
                    dCB = dCB + dW * seg
                    Es = dW * W
                    dac = jnp.sum(Es, axis=1, keepdims=True) - jnp.sum(
                        jnp.where(ri == ci, jnp.sum(Es, axis=0, keepdims=True), 0.0), axis=1, keepdims=True)
                    dal_p = dal_p + jnp.where(lane == 64 * s, dac, 0.0)
                dxdt_p = jnp.where(half == 0, dxp[0], dxp[1])
                dx_ref[:, ps] += dxdt_p * dt_ref[:, ps]
                ddt_ref[:, ps] += dxdt_p * x_ref[:, ps]
                dal_ref[:, ps] += dal_p
            dx_ref[:, D + g * 128:D + (g + 1) * 128] = dB + _dot(dCB, Cg, _TN)
            dx_ref[:, D + 256 + g * 128:D + 256 + (g + 1) * 128] = dC + _dot(dCB, Bg)

    row = pl.BlockSpec((CS * L, D), lambda c: (NB - 1 - c, 0))
    bcs = pl.BlockSpec((CS * L, 512), lambda c: (NB - 1 - c, 2))
    seg = pl.BlockSpec((CS * L, 128), lambda c: (NB - 1 - c, 0))
    return dict(
        body=body, steps=NB, ins=[xbc, xbc, da, h_save, dyy, d_x],
        in_specs=[row, bcs, seg, pl.BlockSpec((CS, SSM_N, D), lambda c: (NB - 1 - c, 0, 0)), row,
                  pl.BlockSpec((1, D), lambda c: (0, 0))],
        out_specs=[pl.BlockSpec((CS * L, D + 512), lambda c: (NB - 1 - c, 0)), seg],
        out_shape=[jax.ShapeDtypeStruct((T, D + 512), F32), jax.ShapeDtypeStruct((T, 128), F32)],
        scratch=[pltpu.VMEM((SSM_N, D), F32), pltpu.VMEM((L, D), F32), pltpu.VMEM((L, D), F32)])


def _run_scans(parts, *, name):
    steps = parts[0]["steps"]
    assert all(p["steps"] == steps for p in parts)
    cnt = lambda key: [len(p[key]) for p in parts]
    n_in, n_out, n_scr = cnt("ins"), cnt("out_shape"), cnt("scratch")

    def body(*refs):
        ins, outs, scr = refs[:sum(n_in)], refs[sum(n_in):sum(n_in) + sum(n_out)], refs[sum(n_in) + sum(n_out):]
        oi = oo = os_ = 0
        for p, a, b, c in zip(parts, n_in, n_out, n_scr):
            p["body"](*ins[oi:oi + a], *outs[oo:oo + b], *scr[os_:os_ + c])
            oi, oo, os_ = oi + a, oo + b, os_ + c

    cat = lambda key: [v for p in parts for v in p[key]]
    res = pl.pallas_call(
        body, grid=(steps,), in_specs=cat("in_specs"), out_specs=cat("out_specs"), out_shape=cat("out_shape"),
        scratch_shapes=cat("scratch"), name=name, compiler_params=_params(("arbitrary",)))(*cat("ins"))
    out, o = [], 0
    for b in n_out:
        out.append(list(res[o:o + b]))
        o += b
    return out


_EARLY = ("w_out", "wq_mem", "wk_mem", "wv_mem", "wo_mem")
_LATE = ("w_up", "w_down")
_GRADS_MLP = ("w_down", "w_up")
_GRADS_MID = ("wo_mem", "wq_mem", "wk_mem", "wv_mem", "w_out")


def _gather_ride(shards, names):
    return None if shards is None else _Ride([shards[n] for n in names], shard=True)


def _grad_ride(shards, G, names):
    return None if shards is None else _Ride([_slots_from_full(n, G[n]) for n in names], shard=False)


def _local_step(x, mem, tgt, W, shards=None):
    T = x.shape[0]
    W = dict(W)
    cw_qk, cw_v = W["gdn_conv_w"][:, :2 * D], W["gdn_conv_w"][:, 2 * D:]
    h1 = _rmsnorm_fwd(x, W["norm1_w"], name="norm1_fwd")
    ride = _gather_ride(shards, _EARLY)
    pg = _mm(h1, W["w_in_pad"], b_cols=(C_GATE, C_TOT - C_GATE), name="in_proj_gates")
    p = _mm(h1, W["w_in_pad"], b_cols=(0, C_GATE), out_dtype=BF16, bn_cap=1664, name="in_proj", ride=ride)
    if ride:
        p, got = p
        W.update({n: _full_from_slots(n, g) for n, g in zip(_EARLY, got)})
    qk = _conv_fwd(p, C_QKV, 2 * D, cw_qk, None, l2=True, name="gdn_conv_qk_fwd")
    v_g = _conv_fwd(p, C_QKV + 2 * D, D, cw_v, None, l2=False, name="gdn_conv_v_fwd")
    bg = _gdn_gates_fwd(pg, W["gdn_alog_row"], W["gdn_dtb_row"])
    ride = _gather_ride(shards, _LATE)
    prep = _gdn_prep(qk, v_g, bg, ride)
    if ride:
        prep, got = prep
        W.update({n: _full_from_slots(n, g) for n, g in zip(_LATE, got)})
    u_g, w_g, qd_g, kd_g, p_g, t_save = prep
    xbc = _conv_fwd(p, C_XBC, D + 512, W["ssm_conv_w"], W["ssm_conv_b"], l2=False, name="ssm_conv_fwd", bc=512)
    da_s = _ssd_dt_fwd(pg, W["ssm_dtb_row"], W["ssm_alog_row"])
    (o_g, vn_g, s_save), (y_s, h_save) = _run_scans(
        [_gdn_scan_fwd(u_g, w_g, qd_g, kd_g, p_g, bg), _ssd_core_fwd(xbc, da_s)], name="scans_fwd")
    mix = _gdn_post_fwd(o_g, p, W["gdn_norm_x"])
    mix = _ssd_post_fwd(y_s, xbc, p, W["ssm_d_x"], W["ssm_norm_w"].reshape(1, D), mix)
    x1, h2 = _mm(mix, W["w_out"], epi="res_norm", extra=(x, W["norm2_w"]), bm=512, name="out_proj")
    qm = _mm(h2, W["wq_mem"], out_dtype=BF16, name="q_proj")
    m = _rmsnorm_fwd(mem, W["mem_norm_w"], name="mem_norm_fwd")
    km = _mm(m, W["wk_mem"], name="k_proj")
    vm = _mm(m, W["wv_mem"], name="v_proj")
    oa = _attn_fwd(qm, km, vm)
    x2, h3 = _mm(oa, W["wo_mem"], epi="res_norm", extra=(x1, W["norm3_w"]), bm=512, name="o_proj")
    u, act = _mm(h3, W["w_up"], epi="relu2", out_dtype=BF16, name="mlp_up")
    dx3, g_final, loss = _mm(act, W["w_down"], epi="res_loss", extra=(x2, tgt, W["final_norm_w"]), bk_cap=1024,
                             name="mlp_down_loss")
    G = {"final_norm_w": g_final.reshape(D)}
    dpre = _mm(dx3, W["w_down"], dims="nt", epi="mul2", extra=u, out_dtype=BF16, name="mlp_down_dx")
    G["w_down"] = _mm(act, dx3, dims="tn", out_dtype=BF16, name="mlp_down_dw")
    G["w_up"] = _mm(h3, dpre, dims="tn", out_dtype=BF16, name="mlp_up_dw")
    dx2, gw = _mm(dpre, W["w_up"], dims="nt", epi="norm_bwd", extra=(x2, dx3, W["norm3_w"]), bk_cap=1024,
                  name="mlp_up_dx")
    G["norm3_w"] = gw.reshape(D)
    do_a = _mm(dx2, W["wo_mem"], dims="nt", out_dtype=BF16, name="o_proj_dx")
    G["wo_mem"] = _mm(oa, dx2, dims="tn", out_dtype=BF16, name="o_proj_dw")
    dq, dk, dv = _attn_bwd(qm, km, vm, do_a)
    G["wq_mem"] = _mm(h2, dq, dims="tn", out_dtype=BF16, name="q_proj_dw")
    dx1, gw = _mm(dq, W["wq_mem"], dims="nt", epi="norm_bwd", extra=(x1, dx2, W["norm2_w"]), bm=512,
                  name="q_proj_dx")
    G["norm2_w"] = gw.reshape(D)
    G["wk_mem"] = _mm(m, dk, dims="tn", out_dtype=BF16, name="k_proj_dw")
    G["wv_mem"] = _mm(m, dv, dims="tn", out_dtype=BF16, name="v_proj_dw")
    dm = _mm(dk, W["wk_mem"], dims="nt", name="k_proj_dx")
    dm = _mm(dv, W["wv_mem"], dims="nt", epi="res", extra=dm, name="v_proj_dx")
    _, G["mem_norm_w"] = _rmsnorm_bwd(mem, W["mem_norm_w"], dm, None, name="mem_norm_bwd")
    G["w_out"] = _mm(mix, dx1, dims="tn", out_dtype=BF16, name="out_proj_dw")
    do_g, dp, G["gdn_norm_x"] = _gdn_post_bwd(dx1, W["w_out"], o_g, p, W["gdn_norm_x"])
    dyy, dp, G["ssm_d_x"], G["ssm_norm_w"] = _ssd_post_bwd(dx1, W["w_out"], y_s, xbc, p, W["ssm_d_x"],
                                                          W["ssm_norm_w"].reshape(1, D), dp)
    (dvn_g, ds_save), (dxbc, dda_s) = _run_scans(
        [_gdn_scan_bwd(w_g, qd_g, kd_g, p_g, bg, do_g), _ssd_core_bwd(xbc, da_s, h_save, dyy, W["ssm_d_x"])],
        name="scans_bwd")
    ride = _grad_ride(shards, G, _GRADS_MLP)
    rest = _gdn_rest_bwd(qk, v_g, bg, s_save, t_save, vn_g, dvn_g, ds_save, do_g, ride)
    if ride:
        rest, got = rest
        G.update(zip(_GRADS_MLP, got))
    dqkvn, dbg = rest
    dy_qk, gcw_qk, _ = _conv_bwd_act(p, C_QKV, 2 * D, cw_qk, None, dqkvn, 0, l2=True, name="gdn_conv_qk_bwd_act")
    dy_v, gcw_v, _ = _conv_bwd_act(p, C_QKV + 2 * D, D, cw_v, None, dqkvn, 2 * D, l2=False,
                                   name="gdn_conv_v_bwd_act")
    G["gdn_conv_w"] = jnp.concatenate([gcw_qk, gcw_v], axis=1)
    dp = _conv_bwd_in(dy_qk, cw_qk, dp, C_QKV, T, name="gdn_conv_qk_bwd_in")
    dp = _conv_bwd_in(dy_v, cw_v, dp, C_QKV + 2 * D, T, name="gdn_conv_v_bwd_in")
    dp, G["gdn_alog_row"], G["gdn_dtb_row"] = _gdn_gates_bwd(pg, W["gdn_alog_row"], W["gdn_dtb_row"], dbg, dp)
    dy_s, G["ssm_conv_w"], G["ssm_conv_b"] = _conv_bwd_act(p, C_XBC, D + 512, W["ssm_conv_w"], W["ssm_conv_b"],
                                                           dxbc, 0, l2=False, name="ssm_conv_bwd_act", bc=512)
    dp = _conv_bwd_in(dy_s, W["ssm_conv_w"], dp, C_XBC, T, name="ssm_conv_bwd_in", bc=512)
    dp, G["ssm_dtb_row"], G["ssm_alog_row"] = _ssd_dt_bwd(pg, W["ssm_dtb_row"], W["ssm_alog_row"], dda_s, dp)
    ride = _grad_ride(shards, G, _GRADS_MID)
    g_in = _mm(h1, dp, dims="tn", out_dtype=BF16, bn_cap=1152, name="in_proj_dw", ride=ride)
    if ride:
        g_in, got = g_in
        G.update(zip(_GRADS_MID, got))
    G["w_in"] = _unpad_w_in(g_in)
    ride = _grad_ride(shards, G, ("w_in",))
    res = _mm(dp, W["w_in_pad"], dims="nt", epi="norm_bwd", extra=(x, dx1, W["norm1_w"]),
              name="in_proj_dx", ride=ride)
    if ride:
        res, got = res
        G["w_in"] = got[0]
    dx, gw = res
    G["norm1_w"] = gw.reshape(D)
    return loss, dx, G


def _all_gather(shards, out_dtype, *, name):
    n = len(shards)

    def body(*refs):
        x_refs, out_refs, stage = refs[:n], refs[n:2 * n], refs[2 * n:3 * n]
        send_sems, recv_sems, local_sems = refs[3 * n:]
        x, y, c = _place()
        me, sibling = (x, y, c), (x, y, 1 - c)
        chips = [(1 - x, y), (x, 1 - y), (1 - x, 1 - y)]

        def slot(px, py, pc):
            return 4 * px + 2 * py + pc

        def copy(a, k, block, to, src=None):
            dst = out_refs[a].at[slot(*block)]
            return pltpu.make_async_remote_copy(
                src_ref=dst if src is None else src, dst_ref=dst, send_sem=send_sems.at[a, k],
                recv_sem=recv_sems.at[a, k], device_id=to, device_id_type=_MESH)

        for a in range(n):
            stage[a][...] = x_refs[a][...].astype(out_dtype)
        mine = [pltpu.make_async_copy(stage[a], out_refs[a].at[slot(*me)], local_sems.at[a]) for a in range(n)]
        for cp in mine:
            cp.start()
        first = []
        for a in range(n):
            first.append(copy(a, 0, me, sibling, src=stage[a]))
            first += [copy(a, 1 + j, me, (*chip, c), src=stage[a]) for j, chip in enumerate(chips)]
        for cp in first:
            cp.start()
        passed = [[copy(a, 4 + j, (*chip, c), sibling) for j, chip in enumerate(chips)] for a in range(n)]
        for j, chip in enumerate(chips):
            for a in range(n):
                copy(a, 1 + j, (*chip, c), me).wait_recv()
                passed[a][j].start()
        for a in range(n):
            copy(a, 0, sibling, me).wait_recv()
            for j, chip in enumerate(chips):
                copy(a, 4 + j, (*chip, 1 - c), me).wait_recv()
        for cp in first + [cp for row in passed for cp in row]:
            cp.wait_send()
        for cp in mine:
            cp.wait()

    outs = pl.pallas_call(
        body, in_specs=[_VM] * n, out_specs=[_ANY] * n,
        out_shape=[jax.ShapeDtypeStruct((N_DEV,) + s.shape, out_dtype) for s in shards],
        scratch_shapes=[pltpu.VMEM(s.shape, out_dtype) for s in shards]
        + [pltpu.SemaphoreType.DMA((n, 7)), pltpu.SemaphoreType.DMA((n, 7)), pltpu.SemaphoreType.DMA((n,))],
        name=name, compiler_params=pltpu.CompilerParams(vmem_limit_bytes=VMEM_LIMIT))(*shards)
    return list(outs)


def _cast_bf16(arrs, *, name):
    n = len(arrs)

    def body(*refs):
        for a in range(n):
            refs[n + a][...] = refs[a][...].astype(BF16)

    return list(pl.pallas_call(
        body, in_specs=[_VM] * n, out_specs=[_VM] * n,
        out_shape=[jax.ShapeDtypeStruct(s.shape, BF16) for s in arrs], name=name,
        compiler_params=pltpu.CompilerParams(vmem_limit_bytes=VMEM_LIMIT))(*arrs))


def _sum8(a, *, name):
    _, R, Cc = a.shape
    br = _pick_rows(R, 128)

    def body(a_ref, o_ref):
        s = a_ref[0].astype(F32)
        for k in range(1, N_DEV):
            s = s + a_ref[k].astype(F32)
        o_ref[...] = s

    return pl.pallas_call(
        body, grid=(R // br,), in_specs=[pl.BlockSpec((N_DEV, br, Cc), lambda i: (0, i, 0))],
        out_specs=pl.BlockSpec((br, Cc), lambda i: (i, 0)), out_shape=jax.ShapeDtypeStruct((R, Cc), F32),
        name=name, compiler_params=_params(("parallel",)))(a)


def _pick_rows(R, cap):
    if R <= cap:
        return R
    for d in range(cap, 7, -8):
        if R % d == 0:
            return d
    return R


def _adamw(w, g, m, v, *, name):
    shape = w.shape
    as2d = (lambda t: t.reshape(1, -1)) if w.ndim == 1 else (lambda t: t)
    w2, m2, v2 = as2d(w), as2d(m), as2d(v)
    R, Cc = w2.shape
    from_slabs = g.ndim == 3
    br = _pick_rows(R, 128 if from_slabs else 256)
    c1 = 1.0 - ADAM_B1 ** ADAM_STEP
    c2 = 1.0 - ADAM_B2 ** ADAM_STEP

    def body(w_ref, g_ref, m_ref, v_ref, go_ref, d_ref, nm_ref, nv_ref):
        if from_slabs:
            gv = g_ref[0].astype(F32)
            for k in range(1, N_DEV):
                gv = gv + g_ref[k].astype(F32)
        else:
            gv = g_ref[...]
        go_ref[...] = gv
        nm = ADAM_B1 * m_ref[...] + (1.0 - ADAM_B1) * gv
        nv = ADAM_B2 * v_ref[...] + (1.0 - ADAM_B2) * (gv * gv)
        nm_ref[...] = nm
        nv_ref[...] = nv
        d_ref[...] = -ADAM_LR * ((nm / c1) / (jnp.sqrt(nv / c2) + ADAM_EPS) + ADAM_WD * w_ref[...])

    blk = pl.BlockSpec((br, Cc), lambda i: (i, 0))
    g_spec = pl.BlockSpec((N_DEV, br, Cc), lambda i: (0, i, 0)) if from_slabs else blk
    outs = pl.pallas_call(
        body, grid=(R // br,), in_specs=[blk, g_spec, blk, blk], out_specs=[blk] * 4,
        out_shape=[jax.ShapeDtypeStruct((R, Cc), F32)] * 4, name=name,
        compiler_params=_params(("parallel",)))(w2, g if from_slabs else as2d(g), m2, v2)
    return tuple(o.reshape(shape) for o in outs)


_BIG = ("w_in", "w_out", "wq_mem", "wk_mem", "wv_mem", "wo_mem", "w_up", "w_down")
_COL_SHARDED = ("w_in", "w_up")
_WEIGHTS = ("norm1_w", "w_in", "gdn_conv_w", "gdn_a_log", "gdn_dt_bias", "gdn_norm_w", "ssm_conv_w", "ssm_conv_b",
            "ssm_a_log", "ssm_dt_bias", "ssm_d", "ssm_norm_w", "w_out", "norm2_w", "mem_norm_w", "wq_mem", "wk_mem",
            "wv_mem", "wo_mem", "norm3_w", "w_up", "w_down", "final_norm_w")
_IN_PAD = 112


def _move_col_slabs(a, to_slabs, *, name):
    n, R, c = (N_DEV, a.shape[0], a.shape[1] // N_DEV) if to_slabs else a.shape
    slab = pl.BlockSpec((None, R, c), lambda j: (j, 0, 0))
    cols = pl.BlockSpec((R, c), lambda j: (0, j))

    def body(a_ref, o_ref):
        o_ref[...] = a_ref[...]

    return pl.pallas_call(
        body, grid=(n,), in_specs=[cols if to_slabs else slab], out_specs=slab if to_slabs else cols,
        out_shape=jax.ShapeDtypeStruct((n, R, c) if to_slabs else (R, n * c), a.dtype), name=name,
        compiler_params=_params(("parallel",)))(a)


def _full_from_slots(name, g):
    if name in _COL_SHARDED:
        if g.shape[2] % 128 == 0:
            return _move_col_slabs(g, False, name="cols_" + name)
        return jnp.transpose(g, (1, 0, 2)).reshape(g.shape[1], N_DEV * g.shape[2])
    return g.reshape(N_DEV * g.shape[1], g.shape[2])


def _slots_from_full(name, f):
    if name in _COL_SHARDED:
        if (f.shape[1] // N_DEV) % 128 == 0:
            return _move_col_slabs(f, True, name="slabs_" + name)
        return jnp.transpose(f.reshape(f.shape[0], N_DEV, f.shape[1] // N_DEV), (1, 0, 2))
    return f.reshape(N_DEV, f.shape[0] // N_DEV, f.shape[1])


def _pad_w_in(w):
    z = jnp.zeros((w.shape[0], _IN_PAD), w.dtype)
    return jnp.concatenate([w[:, :4096], w[:, 4112:6672], w[:, 4096:4112], z, w[:, 6672:6688], z], axis=1)


def _unpad_w_in(gp):
    return jnp.concatenate([gp[:, :4096], gp[:, C_GATE:C_GATE + 16], gp[:, 4096:C_GATE], gp[:, C_DT:C_DT + 16]],
                           axis=1)


def _pack_rows(vals):
    rows, offs, r = [], [], 0
    for vflat in vals:
        nrow = 8 * -(-vflat.shape[0] // 1024)
        rows.append(jnp.pad(vflat, (0, nrow * 128 - vflat.shape[0])).reshape(nrow, 128))
        offs.append((r, vflat.shape[0]))
        r += nrow
    return jnp.concatenate(rows, axis=0), offs


def _unpack_rows(packed, offs, shapes):
    out = []
    for (r, nel), shp in zip(offs, shapes):
        nrow = -(-nel // 128)
        out.append(packed[r:r + nrow].reshape(-1)[:nel].reshape(shp))
    return out


def kernel(x, mem, norm1_w, w_in, gdn_conv_w, gdn_a_log, gdn_dt_bias, gdn_norm_w, ssm_conv_w, ssm_conv_b, ssm_a_log, ssm_dt_bias, ssm_d, ssm_norm_w, w_out, norm2_w, mem_norm_w, wq_mem, wk_mem, wv_mem, wo_mem, norm3_w, w_up, w_down, final_norm_w, loss_target, m_norm1_w, m_w_in, m_gdn_conv_w, m_gdn_a_log, m_gdn_dt_bias, m_gdn_norm_w, m_ssm_conv_w, m_ssm_conv_b, m_ssm_a_log, m_ssm_dt_bias, m_ssm_d, m_ssm_norm_w, m_w_out, m_norm2_w, m_mem_norm_w, m_wq_mem, m_wk_mem, m_wv_mem, m_wo_mem, m_norm3_w, m_w_up, m_w_down, m_final_norm_w, v_norm1_w, v_w_in, v_gdn_conv_w, v_gdn_a_log, v_gdn_dt_bias, v_gdn_norm_w, v_ssm_conv_w, v_ssm_conv_b, v_ssm_a_log, v_ssm_dt_bias, v_ssm_d, v_ssm_norm_w, v_w_out, v_norm2_w, v_mem_norm_w, v_wq_mem, v_wk_mem, v_wv_mem, v_wo_mem, v_norm3_w, v_w_up, v_w_down, v_final_norm_w):
    args = dict(locals())
    w_loc = {n: args[n] for n in _WEIGHTS}
    me = 4 * lax.axis_index("x") + 2 * lax.axis_index("y") + lax.axis_index("c")

    w_in_full = _full_from_slots("w_in", _all_gather([w_in], BF16, name="gather_w_in")[0])
    later = _EARLY + _LATE
    shards = dict(zip(later, _cast_bf16([w_loc[n] for n in later], name="cast_shards")))
    conv_pack, conv_offs = _pack_rows([gdn_conv_w.reshape(-1), ssm_conv_w.reshape(-1)])
    conv_all = _all_gather([conv_pack], F32, name="gather_conv")[0]
    gdn_cw, ssm_cw = [], []
    for k in range(N_DEV):
        a, b = _unpack_rows(conv_all[k], conv_offs, [gdn_conv_w.shape, ssm_conv_w.shape])
        gdn_cw.append(a)
        ssm_cw.append(b)
    W = {
        "w_in_pad": _pad_w_in(w_in_full),
        "norm1_w": norm1_w, "norm2_w": norm2_w, "norm3_w": norm3_w, "mem_norm_w": mem_norm_w,
        "final_norm_w": final_norm_w, "ssm_norm_w": ssm_norm_w, "ssm_conv_b": ssm_conv_b,
        "gdn_conv_w": jnp.concatenate(gdn_cw, axis=1), "ssm_conv_w": jnp.concatenate(ssm_cw, axis=1),
        "gdn_alog_row": jnp.pad(gdn_a_log, (GDN_H, 128 - 2 * GDN_H)).reshape(1, 128),
        "gdn_dtb_row": jnp.pad(gdn_dt_bias, (GDN_H, 128 - 2 * GDN_H)).reshape(1, 128),
        "gdn_norm_x": jnp.tile(gdn_norm_w, GDN_H).reshape(1, D),
        "ssm_dtb_row": jnp.pad(ssm_dt_bias, (0, 128 - SSM_H)).reshape(1, 128),
        "ssm_alog_row": jnp.pad(ssm_a_log, (0, 128 - SSM_H)).reshape(1, 128),
        "ssm_d_x": jnp.repeat(ssm_d, SSM_P).reshape(1, D),
    }

    loss_part, grad_x, G = _local_step(x[0], mem[0], loss_target[0], W, shards)

    grads = {n: G[n] for n in _BIG}

    small = {
        "norm1_w": G["norm1_w"], "gdn_conv_w": G["gdn_conv_w"], "gdn_a_log": G["gdn_alog_row"][0, GDN_H:2 * GDN_H],
        "gdn_dt_bias": G["gdn_dtb_row"][0, GDN_H:2 * GDN_H], "gdn_norm_w": G["gdn_norm_x"].reshape(GDN_H, 128).sum(0),
        "ssm_conv_w": G["ssm_conv_w"], "ssm_conv_b": G["ssm_conv_b"],
        "ssm_a_log": G["ssm_alog_row"][0, :SSM_H], "ssm_dt_bias": G["ssm_dtb_row"][0, :SSM_H],
        "ssm_d": G["ssm_d_x"].reshape(SSM_H, SSM_P).sum(1), "ssm_norm_w": G["ssm_norm_w"].reshape(D),
        "norm2_w": G["norm2_w"], "mem_norm_w": G["mem_norm_w"], "norm3_w": G["norm3_w"],
        "final_norm_w": G["final_norm_w"], "loss": loss_part[0, :1],
    }
    names = list(small)
    pack, offs = _pack_rows([small[n].reshape(-1) for n in names])
    tot = _sum8(_all_gather([pack], F32, name="gather_small")[0], name="sum_small")
    summed = dict(zip(names, _unpack_rows(tot, offs, [small[n].shape for n in names])))
    loss = summed.pop("loss")[0]
    for n in ("gdn_conv_w", "ssm_conv_w"):
        width = w_loc[n].shape[1]
        summed[n] = lax.dynamic_slice_in_dim(summed[n], me * width, width, axis=1)
    grads.update(summed)

    upd = {n: _adamw(w_loc[n], grads[n], args["m_" + n], args["v_" + n], name="adamw_" + n) for n in _WEIGHTS}
    return (loss, grad_x[None], *[upd[n][0] for n in _WEIGHTS], *[upd[n][1] for n in _WEIGHTS],
            *[upd[n][2] for n in _WEIGHTS], *[upd[n][3] for n in _WEIGHTS])
```

```python
import jax
import jax.numpy as jnp
from jax import lax
from jax.experimental import pallas as pl
from jax.experimental.pallas import tpu as pltpu

F32 = jnp.float32
BF16 = jnp.bfloat16
_MXU = BF16

D = 1024
EPS = 1e-6
CONV_K = 4
GDN_H, GDN_DK, GDN_C = 8, 128, 64
GDN_SCAN_CHUNKS = 4
GDN_LOCAL_CHUNKS = 4
GDN_REST_CHUNKS = 4
SSM_H, SSM_P, SSM_L, SSM_N = 16, 64, 128, 128
SSM_SCAN_CHUNKS = 2
MEM_H, MEM_HD = 4, 256
D_FF = 4096
N_DEV = 8

C_QKV, C_ZG, C_ZS, C_XBC, C_GATE, C_DT, C_TOT = 0, 3072, 4096, 5120, 6656, 6784, 6912
P_HALO = 16

ADAM_LR, ADAM_B1, ADAM_B2, ADAM_EPS, ADAM_WD, ADAM_STEP = 0.001, 0.9, 0.999, 1e-08, 0.01, 10

VMEM_LIMIT = 56 * 1024 * 1024

_NN = (((1,), (0,)), ((), ()))
_NT = (((1,), (1,)), ((), ()))
_TN = (((0,), (0,)), ((), ()))


def _dot(a, b, dims=_NN):
    return lax.dot_general(a.astype(_MXU), b.astype(_MXU), dims, preferred_element_type=F32)


def _split3(a):
    a1 = a.astype(BF16)
    r1 = a - a1.astype(F32)
    a2 = r1.astype(BF16)
    return a1, a2, (r1 - a2.astype(F32)).astype(BF16)


def _dot_sel(a, e):
    eb = e.astype(BF16)
    return sum(lax.dot_general(p, eb, _NN, preferred_element_type=F32) for p in _split3(a))


def _sel_dot(e, a):
    eb = e.astype(BF16)
    return sum(lax.dot_general(eb, p, _NN, preferred_element_type=F32) for p in _split3(a))


def _chunk_cumsum(a, tri, chunk):
    return jnp.concatenate([_sel_dot(tri, a[r:r + chunk]) for r in range(0, a.shape[0], chunk)], axis=0)


def _params(sem):
    return pltpu.CompilerParams(dimension_semantics=sem, vmem_limit_bytes=VMEM_LIMIT)


def _pick(n, cap):
    for d in range(min(cap, n), 0, -128):
        if n % d == 0 and d % 128 == 0:
            return d
    return n


def _sigmoid(x):
    return 0.5 * jnp.tanh(0.5 * x) + 0.5


def _silu(x):
    return x * _sigmoid(x)


def _dsilu(x):
    s = _sigmoid(x)
    return s * (1.0 + x * (1.0 - s))


def _softplus(x):
    return jnp.maximum(x, 0.0) + jnp.log(1.0 + jnp.exp(-jnp.abs(x)))


def _iota2(shape, axis):
    return lax.broadcasted_iota(jnp.int32, shape, axis)


def _sum_all(x):
    return jnp.sum(jnp.sum(x, axis=1, keepdims=True), axis=0, keepdims=True)


_MESH = pl.DeviceIdType.MESH
_ANY = pl.BlockSpec(memory_space=pl.ANY)
_VM = pl.BlockSpec(memory_space=pltpu.VMEM)
_REL = [(r >> 2 & 1, r >> 1 & 1, r & 1) for r in range(1, N_DEV)]


def _place():
    return lax.axis_index("x"), lax.axis_index("y"), lax.axis_index("c")


class _Ride:
    def __init__(self, srcs, shard):
        self.srcs, self.shard, self.n = list(srcs), shard, len(srcs)
        self.out_shape = [jax.ShapeDtypeStruct(((N_DEV,) + s.shape) if shard else s.shape, s.dtype)
                          for s in self.srcs]
        self.specs = [_ANY] * self.n
        self.scratch = [pltpu.SemaphoreType.DMA((self.n, N_DEV - 1)), pltpu.SemaphoreType.DMA((self.n, N_DEV - 1)),
                        pltpu.SemaphoreType.DMA((self.n,))]

    def _copies(self, in_refs, out_refs, sems):
        send, recv, loc = sems
        x, y, c = _place()
        me = 4 * x + 2 * y + c
        local, remote, arrive = [], [], []
        for a in range(self.n):
            src = in_refs[a] if self.shard else in_refs[a].at[me]
            local.append(pltpu.make_async_copy(src, out_refs[a].at[me], loc.at[a]))
        for k, (rx, ry, rc) in enumerate(_REL):
            peer = (lax.rem(x + rx, 2), lax.rem(y + ry, 2), lax.rem(c + rc, 2))
            ps = 4 * peer[0] + 2 * peer[1] + peer[2]
            for a in range(self.n):
                src = in_refs[a] if self.shard else in_refs[a].at[ps]
                remote.append(pltpu.make_async_remote_copy(
                    src_ref=src, dst_ref=out_refs[a].at[me], send_sem=send.at[a, k], recv_sem=recv.at[a, k],
                    device_id=peer, device_id_type=_MESH))
                slot = out_refs[a].at[ps]
                arrive.append(pltpu.make_async_remote_copy(
                    src_ref=slot, dst_ref=slot, send_sem=send.at[a, k], recv_sem=recv.at[a, k],
                    device_id=peer, device_id_type=_MESH))
        return local, remote, arrive

    def start(self, in_refs, out_refs, sems):
        local, remote, _ = self._copies(in_refs, out_refs, sems)
        for cp in local + remote:
            cp.start()

    def wait(self, in_refs, out_refs, sems):
        local, remote, arrive = self._copies(in_refs, out_refs, sems)
        for cp in arrive:
            cp.wait_recv()
        for cp in remote:
            cp.wait_send()
        for cp in local:
            cp.wait()


_EPI = {
    "none": ((), ("tile",)),
    "res": (("tile",), ("tile",)),
    "mul2": (("tile",), ("tile",)),
    "relu2": ((), ("tile", "tile")),
    "res_norm": (("tile", "row"), ("tile", "tile")),
    "norm_bwd": (("tile", "tile", "row"), ("tile", "row")),
    "res_loss": (("tile", "tile", "row"), ("tile", "row", "row")),
}


def _mm(a, b, *, dims="nn", epi="none", extra=(), out_dtype=F32, name, bm=1024, bn_cap=1024, bk_cap=2048,
        ride=None, b_cols=None, b_resident=False):
    if dims == "nn":
        (M, K), (K2, N) = a.shape, b.shape
    elif dims == "nt":
        (M, K), (N, K2) = a.shape, b.shape
    else:
        (K, M), (K2, N) = a.shape, b.shape
    jb0 = 0
    if b_cols is not None:
        N = b_cols[1]
    assert K == K2, (a.shape, b.shape, dims)
    bm = _pick(M, bm)
    bn = _pick(N, bn_cap)
    bk = _pick(K, bk_cap)
    nk = K // bk
    if b_cols is not None:
        assert dims == "nn" and b_cols[0] % bn == 0
        jb0 = b_cols[0] // bn
    dn = {"nn": _NN, "nt": _NT, "tn": _TN}[dims]
    a_spec = (pl.BlockSpec((bk, bm), lambda i, j, k: (k, i)) if dims == "tn"
              else pl.BlockSpec((bm, bk), lambda i, j, k: (i, k)))
    if b_resident:
        b_spec = pl.BlockSpec(b.shape, lambda i, j, k: (0, 0), pipeline_mode=pl.Buffered(1))
    else:
        b_spec = (pl.BlockSpec((bn, bk), lambda i, j, k: (j, k)) if dims == "nt"
                  else pl.BlockSpec((bk, bn), lambda i, j, k: (k, j + jb0)))
    o_spec = pl.BlockSpec((bm, bn), lambda i, j, k: (i, j))
    r_spec = pl.BlockSpec((1, bn), lambda i, j, k: (0, j))
    extra = list(extra) if isinstance(extra, (tuple, list)) else [extra]
    ekinds, okinds = _EPI[epi]
    assert len(extra) == len(ekinds) and (epi not in ("res_norm", "norm_bwd", "res_loss") or bn == N)
    n_extra, n_out = len(ekinds), len(okinds)
    n_ride = ride.n if ride else 0
    gi, gj = M // bm, N // bn

    def body(a_ref, b_ref, *rest):
        ex = rest[:n_extra]
        first = pl.program_id(0) == 0
        ride_in = rest[n_extra:n_extra + n_ride]
        outs = rest[n_extra + n_ride:n_extra + n_ride + n_out]
        ride_out = rest[n_extra + n_ride + n_out:n_extra + 2 * n_ride + n_out]
        if ride:
            at = lambda i, j, k: ((pl.program_id(0) == i) & (pl.program_id(1) == j) & (pl.program_id(2) == k))

            @pl.when(at(0, 0, 0))
            def _():
                ride.start(ride_in, ride_out, rest[-3:])

        def finish(r):
            if epi == "res":
                outs[0][...] = (r + ex[0][...].astype(F32)).astype(outs[0].dtype)
            elif epi == "mul2":
                outs[0][...] = (2.0 * r * ex[0][...].astype(F32)).astype(outs[0].dtype)
            elif epi == "relu2":
                u = jnp.maximum(r, 0.0)
                outs[0][...] = u.astype(outs[0].dtype)
                outs[1][...] = (u * u).astype(outs[1].dtype)
            elif epi == "res_norm":
                y = r + ex[0][...]
                outs[0][...] = y
                rstd = lax.rsqrt(jnp.mean(y * y, axis=1, keepdims=True) + EPS)
                outs[1][...] = (y * rstd * ex[1][...]).astype(outs[1].dtype)
            elif epi == "norm_bwd":
                xv = ex[0][...]
                rstd = lax.rsqrt(jnp.mean(xv * xv, axis=1, keepdims=True) + EPS)
                xh = xv * rstd
                dxh = r * ex[2][...]
                outs[0][...] = ex[1][...] + rstd * (dxh - xh * jnp.mean(dxh * xh, axis=1, keepdims=True))
                dw = jnp.sum(r * xh, axis=0, keepdims=True)

                @pl.when(first)
                def _():
                    outs[1][...] = dw

                @pl.when(jnp.logical_not(first))
                def _():
                    outs[1][...] += dw
            elif epi == "res_loss":
                y = r + ex[0][...]
                wv = ex[2][...]
                rstd = lax.rsqrt(jnp.mean(y * y, axis=1, keepdims=True) + EPS)
                yh = y * rstd
                err = yh * wv - ex[1][...]
                part_loss = 0.5 * jnp.sum(jnp.mean(err * err, axis=1, keepdims=True), axis=0, keepdims=True)
                dyn = err * (1.0 / N)
                dyh = dyn * wv
                outs[0][...] = rstd * (dyh - yh * jnp.mean(dyh * yh, axis=1, keepdims=True))
                dw = jnp.sum(dyn * yh, axis=0, keepdims=True)
                lrow = jnp.broadcast_to(part_loss, (1, N))

                @pl.when(first)
                def _():
                    outs[1][...] = dw
                    outs[2][...] = lrow

                @pl.when(jnp.logical_not(first))
                def _():
                    outs[1][...] += dw
                    outs[2][...] += lrow
            else:
                outs[0][...] = r.astype(outs[0].dtype)

        if b_resident:
            jo = pl.multiple_of((pl.program_id(1) + jb0) * bn, bn)
            ko = pl.multiple_of(pl.program_id(2) * bk, bk)
            b_blk = b_ref[pl.ds(jo, bn), pl.ds(ko, bk)] if dims == "nt" else b_ref[pl.ds(ko, bk), pl.ds(jo, bn)]
        else:
            b_blk = b_ref[...]
        part = _dot(a_ref[...], b_blk, dn)
        if nk == 1:
            finish(part)
        else:
            acc = rest[n_extra + 2 * n_ride + n_out]
            k = pl.program_id(2)

            @pl.when(k == 0)
            def _():
                acc[...] = part

            @pl.when((k > 0) & (k < nk - 1))
            def _():
                acc[...] += part

            @pl.when(k == nk - 1)
            def _():
                finish(acc[...] + part)

        if ride:
            @pl.when(at(gi - 1, gj - 1, nk - 1))
            def _():
                ride.wait(ride_in, ride_out, rest[-3:])

    kind_spec = {"tile": o_spec, "row": r_spec}
    ins = [a, b] + [e.reshape(1, N) if k == "row" else e for e, k in zip(extra, ekinds)]
    in_specs = [a_spec, b_spec] + [kind_spec[k] for k in ekinds]
    out_dtypes = {"res_norm": (F32, BF16), "norm_bwd": (F32, F32), "res_loss": (F32, F32, F32)}.get(
        epi, (out_dtype,) * n_out)
    out_shape = [jax.ShapeDtypeStruct((M, N) if k == "tile" else (1, N), dt) for k, dt in zip(okinds, out_dtypes)]
    out_specs = [kind_spec[k] for k in okinds]
    scratch = [pltpu.VMEM((bm, bn), F32)] if nk > 1 else []
    sem = ("arbitrary" if epi in ("norm_bwd", "res_loss") else "parallel", "parallel", "arbitrary")
    if ride:
        ins, in_specs = ins + ride.srcs, in_specs + ride.specs
        out_shape, out_specs = out_shape + ride.out_shape, out_specs + ride.specs
        scratch, sem = scratch + ride.scratch, ("arbitrary",) * 3
    res = pl.pallas_call(
        body, grid=(gi, gj, nk), in_specs=in_specs, out_specs=out_specs, out_shape=out_shape,
        scratch_shapes=scratch, name=name, compiler_params=_params(sem))(*ins)
    main = res[:n_out] if n_out > 1 else res[0]
    return (main, list(res[n_out:])) if ride else main


def _rmsnorm_fwd(x, w, *, name, bt=256):
    T, Dm = x.shape
    bt = min(bt, T)

    def body(x_ref, w_ref, h_ref):
        xv = x_ref[...]
        r = lax.rsqrt(jnp.mean(xv * xv, axis=1, keepdims=True) + EPS)
        h_ref[...] = (xv * r * w_ref[...]).astype(h_ref.dtype)

    return pl.pallas_call(
        body, grid=(T // bt,),
        in_specs=[pl.BlockSpec((bt, Dm), lambda i: (i, 0)), pl.BlockSpec((1, Dm), lambda i: (0, 0))],
        out_specs=pl.BlockSpec((bt, Dm), lambda i: (i, 0)),
        out_shape=jax.ShapeDtypeStruct((T, Dm), BF16), name=name,
        compiler_params=_params(("parallel",)))(x, w.reshape(1, Dm))


def _rmsnorm_bwd(x, w, dh, dres, *, name, bt=256):
    T, Dm = x.shape
    bt = min(bt, T)
    has_res = dres is not None

    def body(x_ref, w_ref, dh_ref, *rest):
        dres_ref = rest[0] if has_res else None
        dx_ref, dw_ref = rest[-2], rest[-1]
        i = pl.program_id(0)
        xv = x_ref[...]
        r = lax.rsqrt(jnp.mean(xv * xv, axis=1, keepdims=True) + EPS)
        xh = xv * r
        dhv = dh_ref[...].astype(F32)
        dxh = dhv * w_ref[...]
        dx = r * (dxh - xh * jnp.mean(dxh * xh, axis=1, keepdims=True))
        if has_res:
            dx = dx + dres_ref[...]
        dx_ref[...] = dx

        @pl.when(i == 0)
        def _():
            dw_ref[...] = jnp.zeros_like(dw_ref)

        dw_ref[...] += jnp.sum(dhv * xh, axis=0, keepdims=True)

    row = pl.BlockSpec((bt, Dm), lambda i: (i, 0))
    vec = pl.BlockSpec((1, Dm), lambda i: (0, 0))
    ins = [x, w.reshape(1, Dm), dh] + ([dres] if has_res else [])
    dx, dw = pl.pallas_call(
        body, grid=(T // bt,), in_specs=[row, vec, row] + ([row] if has_res else []),
        out_specs=[row, vec],
        out_shape=[jax.ShapeDtypeStruct((T, Dm), F32), jax.ShapeDtypeStruct((1, Dm), F32)],
        name=name, compiler_params=_params(("arbitrary",)))(*ins)
    return dx, dw.reshape(Dm)


def _attn_fwd(q, km, vm, *, bt=256):
    T = q.shape[0]
    M = km.shape[0]
    bt = min(bt, T)
    scale = MEM_HD ** -0.5

    def body(q_ref, k_ref, v_ref, o_ref):
        sls = [slice(h * MEM_HD, (h + 1) * MEM_HD) for h in range(MEM_H)]
        ss = [_dot(q_ref[:, sl], k_ref[:, sl], _NT) * scale for sl in sls]
        es = [jnp.exp(s - jnp.max(s, axis=1, keepdims=True)) for s in ss]
        ps = [e / jnp.sum(e, axis=1, keepdims=True) for e in es]
        for sl, p in zip(sls, ps):
            o_ref[:, sl] = _dot(p, v_ref[:, sl]).astype(o_ref.dtype)

    row = pl.BlockSpec((bt, D), lambda i: (i, 0))
    mem = pl.BlockSpec((M, D), lambda i: (0, 0))
    return pl.pallas_call(
        body, grid=(T // bt,), in_specs=[row, mem, mem], out_specs=row,
        out_shape=jax.ShapeDtypeStruct((T, D), BF16), name="attn_fwd",
        compiler_params=_params(("parallel",)))(q, km, vm)


def _attn_bwd(q, km, vm, do, *, bt=256):
    T = q.shape[0]
    M = km.shape[0]
    bt = min(bt, T)
    scale = MEM_HD ** -0.5

    def body(q_ref, k_ref, v_ref, do_ref, dq_ref, dk_ref, dv_ref):
        i = pl.program_id(0)

        @pl.when(i == 0)
        def _():
            dk_ref[...] = jnp.zeros_like(dk_ref)
            dv_ref[...] = jnp.zeros_like(dv_ref)

        sls = [slice(h * MEM_HD, (h + 1) * MEM_HD) for h in range(MEM_H)]
        ss = [_dot(q_ref[:, sl], k_ref[:, sl], _NT) * scale for sl in sls]
        dps = [_dot(do_ref[:, sl], v_ref[:, sl], _NT) for sl in sls]
        es = [jnp.exp(s - jnp.max(s, axis=1, keepdims=True)) for s in ss]
        ps = [e / jnp.sum(e, axis=1, keepdims=True) for e in es]
        dss = [p * (dp - jnp.sum(dp * p, axis=1, keepdims=True)) * scale for p, dp in zip(ps, dps)]
        for sl, p, ds in zip(sls, ps, dss):
            dq_ref[:, sl] = _dot(ds, k_ref[:, sl]).astype(dq_ref.dtype)
            dk_ref[:, sl] += _dot(ds, q_ref[:, sl], _TN)
            dv_ref[:, sl] += _dot(p, do_ref[:, sl], _TN)

    row = pl.BlockSpec((bt, D), lambda i: (i, 0))
    mem = pl.BlockSpec((M, D), lambda i: (0, 0))
    return pl.pallas_call(
        body, grid=(T // bt,), in_specs=[row, mem, mem, row], out_specs=[row, mem, mem],
        out_shape=[jax.ShapeDtypeStruct((T, D), BF16), jax.ShapeDtypeStruct((M, D), F32),
                   jax.ShapeDtypeStruct((M, D), F32)],
        name="attn_bwd", compiler_params=_params(("arbitrary",)))(q, km, vm, do)


def _conv_apply(halo, x, w_ref, b_ref):
    bt, hr = x.shape[0], halo.shape[0]
    cat = jnp.concatenate([halo, x], axis=0)
    y = x * w_ref[3:4, :]
    for k in range(CONV_K - 1):
        y = y + pltpu.roll(cat, CONV_K - 1 - k, 0)[hr:hr + bt] * w_ref[k:k + 1, :]
    if b_ref is not None:
        y = y + b_ref[...]
    return y


def _l2_parts(act, bc):
    out = []
    for s in range(bc // 128):
        a = act[:, s * 128:(s + 1) * 128]
        r = lax.rsqrt(jnp.sum(a * a, axis=1, keepdims=True) + EPS)
        out.append((a, r))
    return out


def _conv_fwd(p, col0, C, w, b, *, l2, name, bt=512, bc=1024):
    T = p.shape[0]
    bt = min(bt, T)
    c0, hb = col0 // bc, bt // P_HALO
    has_b = b is not None
    assert not l2 or (bc == D and C == 2 * D)

    def body(x_ref, halo_ref, w_ref, *rest):
        b_ref = rest[0] if has_b else None
        o_ref = rest[-1]
        i, j = pl.program_id(0), pl.program_id(1)
        x = x_ref[...].astype(F32)
        halo = jnp.where(i > 0, halo_ref[...].astype(F32), 0.0)
        act = _silu(_conv_apply(halo, x, w_ref, b_ref))
        if l2:
            sc = jnp.where(j == 0, GDN_DK ** -0.5, 1.0)
            o_ref[...] = jnp.concatenate([a * (r * sc) for a, r in _l2_parts(act, bc)], axis=1)
        else:
            o_ref[...] = act

    in_specs = [pl.BlockSpec((bt, bc), lambda i, j: (i, c0 + j)),
                pl.BlockSpec((P_HALO, bc), lambda i, j: (jnp.maximum(i * hb - 1, 0), c0 + j)),
                pl.BlockSpec((CONV_K, bc), lambda i, j: (0, j))]
    ins = [p, p, w]
    if has_b:
        in_specs.append(pl.BlockSpec((1, bc), lambda i, j: (0, j)))
        ins.append(b.reshape(1, C))
    return pl.pallas_call(
        body, grid=(T // bt, C // bc), in_specs=in_specs,
        out_specs=pl.BlockSpec((bt, bc), lambda i, j: (i, j)),
        out_shape=jax.ShapeDtypeStruct((T, C), F32), name=name,
        compiler_params=_params(("parallel", "parallel")))(*ins)


def _conv_bwd_act(p, col0, C, w, b, dact, dcol0, *, l2, name, bt=512, bc=1024):
    T = p.shape[0]
    bt = min(bt, T)
    c0, d0, hb = col0 // bc, dcol0 // bc, bt // P_HALO
    has_b = b is not None
    assert not l2 or (bc == D and C == 2 * D)

    def body(x_ref, halo_ref, w_ref, *rest):
        b_ref = rest[0] if has_b else None
        dact_ref, dy_ref, dw_ref, db_ref = rest[-4:]
        j, i = pl.program_id(0), pl.program_id(1)
        x = x_ref[...].astype(F32)
        halo = jnp.where(i > 0, halo_ref[...].astype(F32), 0.0)
        y = _conv_apply(halo, x, w_ref, b_ref)
        dact = dact_ref[...]
        sg = _sigmoid(y)
        if l2:
            sc = jnp.where(j == 0, GDN_DK ** -0.5, 1.0)
            parts = []
            for s, (a, r) in enumerate(_l2_parts(y * sg, bc)):
                n = a * r
                dn = dact[:, s * 128:(s + 1) * 128]
                parts.append((r * sc) * (dn - n * jnp.sum(dn * n, axis=1, keepdims=True)))
            dact = jnp.concatenate(parts, axis=1)
        dy = dact * (sg * (1.0 + y * (1.0 - sg)))
        dy_ref[...] = dy

        @pl.when(i == 0)
        def _():
            dw_ref[...] = jnp.zeros_like(dw_ref)
            db_ref[...] = jnp.zeros_like(db_ref)

        db_ref[...] += jnp.sum(dy, axis=0, keepdims=True)
        cat = jnp.concatenate([halo, x], axis=0)
        dw_ref[3:4, :] += jnp.sum(dy * x, axis=0, keepdims=True)
        for k in range(CONV_K - 1):
            xs = pltpu.roll(cat, CONV_K - 1 - k, 0)[P_HALO:P_HALO + bt]
            dw_ref[k:k + 1, :] += jnp.sum(dy * xs, axis=0, keepdims=True)

    in_specs = [pl.BlockSpec((bt, bc), lambda j, i: (i, c0 + j)),
                pl.BlockSpec((P_HALO, bc), lambda j, i: (jnp.maximum(i * hb - 1, 0), c0 + j)),
                pl.BlockSpec((CONV_K, bc), lambda j, i: (0, j))]
    ins = [p, p, w]
    if has_b:
        in_specs.append(pl.BlockSpec((1, bc), lambda j, i: (0, j)))
        ins.append(b.reshape(1, C))
    in_specs.append(pl.BlockSpec((bt, bc), lambda j, i: (i, d0 + j)))
    ins.append(dact)
    dy, dw, db = pl.pallas_call(
        body, grid=(C // bc, T // bt), in_specs=in_specs,
        out_specs=[pl.BlockSpec((bt, bc), lambda j, i: (i, j)),
                   pl.BlockSpec((CONV_K, bc), lambda j, i: (0, j)),
                   pl.BlockSpec((1, bc), lambda j, i: (0, j))],
        out_shape=[jax.ShapeDtypeStruct((T, C), F32), jax.ShapeDtypeStruct((CONV_K, C), F32),
                   jax.ShapeDtypeStruct((1, C), F32)],
        name=name, compiler_params=_params(("parallel", "arbitrary")))(*ins)
    return dy, dw, db.reshape(C)


def _conv_bwd_in(dy, w, dp_in, col0, T, *, name, bt=512, bc=1024):
    C = dy.shape[1]
    bt = min(bt, T)
    c0, hb, nb = col0 // bc, bt // 8, T // bt

    def body(dy_ref, nxt_ref, w_ref, *rest):
        o_ref = rest[-1]
        i = pl.program_id(0)
        dy_v = dy_ref[...]
        nxt = jnp.where(i < nb - 1, nxt_ref[...], 0.0)
        cat = jnp.concatenate([dy_v, nxt], axis=0)
        dx = dy_v * w_ref[3:4, :]
        for k in range(CONV_K - 1):
            s = CONV_K - 1 - k
            dx = dx + pltpu.roll(cat, bt + 8 - s, 0)[0:bt] * w_ref[k:k + 1, :]
        o_ref[...] = dx.astype(o_ref.dtype)

    in_specs = [pl.BlockSpec((bt, bc), lambda i, j: (i, j)),
                pl.BlockSpec((8, bc), lambda i, j: (jnp.minimum((i + 1) * hb, T // 8 - 1), j)),
                pl.BlockSpec((CONV_K, bc), lambda i, j: (0, j))]
    ins = [dy, dy, w]
    alias = {}
    if dp_in is not None:
        in_specs.append(pl.BlockSpec(memory_space=pl.ANY))
        ins.append(dp_in)
        alias = {3: 0}
    return pl.pallas_call(
        body, grid=(nb, C // bc), in_specs=in_specs,
        out_specs=pl.BlockSpec((bt, bc), lambda i, j: (i, c0 + j)),
        out_shape=jax.ShapeDtypeStruct((T, C_TOT), BF16), input_output_aliases=alias, name=name,
        compiler_params=_params(("parallel", "parallel")))(*ins)


def _expand_mats(shift, row0):
    e = (_iota2((128, D), 0) - row0 == (_iota2((128, D), 1) >> shift)).astype(F32)
    et = ((_iota2((D, 128), 0) >> shift) == _iota2((D, 128), 1) - row0).astype(F32)
    return e, et


def _cum_mats(chunk):
    ri, ci = _iota2((chunk, chunk), 0), _iota2((chunk, chunk), 1)
    return (ri >= ci).astype(F32), (ri <= ci).astype(F32)


def _gdn_gates_fwd(p, alog_row, dtb_row, *, bt=256):
    T = p.shape[0]
    bt = min(bt, T)

    def body(g_ref, al_ref, db_ref, bg_ref):
        gt = g_ref[...]
        lc, _ = _cum_mats(GDN_C)
        g_l = -jnp.exp(al_ref[...]) * _softplus(gt + db_ref[...])
        bg_ref[...] = jnp.where(_iota2((bt, 128), 1) < GDN_H, _sigmoid(gt), _chunk_cumsum(g_l, lc, GDN_C))

    vec = pl.BlockSpec((1, 128), lambda i: (0, 0))
    seg = pl.BlockSpec((bt, 128), lambda i: (i, 0))
    return pl.pallas_call(
        body, grid=(T // bt,), in_specs=[seg, vec, vec], out_specs=seg,
        out_shape=jax.ShapeDtypeStruct((T, 128), F32), name="gdn_gates_fwd",
        compiler_params=_params(("parallel",)))(p, alog_row, dtb_row)


def _gdn_gates_bwd(p, alog_row, dtb_row, dbg, dp_in, *, bt=256):
    T = p.shape[0]
    bt = min(bt, T)

    def body(g_ref, al_ref, db_ref, dbg_ref, dpin_ref, dg_out, dal_ref, ddb_ref):
        i = pl.program_id(0)
        gt = g_ref[...]
        lane = _iota2((bt, 128), 1)
        _, uc = _cum_mats(GDN_C)
        ea = jnp.exp(al_ref[...])
        zz = gt + db_ref[...]
        g_l = -ea * _softplus(zz)
        beta_l = _sigmoid(gt)
        dbg_v = dbg_ref[...]
        dg_l = jnp.where((lane >= GDN_H) & (lane < 2 * GDN_H), _chunk_cumsum(dbg_v, uc, GDN_C), 0.0)
        dbeta_l = jnp.where(lane < GDN_H, dbg_v, 0.0)
        da = dg_l * (-ea) * _sigmoid(zz)
        dg_out[...] = (da + dbeta_l * beta_l * (1.0 - beta_l)).astype(dg_out.dtype)

        @pl.when(i == 0)
        def _():
            dal_ref[...] = jnp.zeros_like(dal_ref)
            ddb_ref[...] = jnp.zeros_like(ddb_ref)

        dal_ref[...] += jnp.sum(dg_l * g_l, axis=0, keepdims=True)
        ddb_ref[...] += jnp.sum(da, axis=0, keepdims=True)

    vec = pl.BlockSpec((1, 128), lambda i: (0, 0))
    seg = pl.BlockSpec((bt, 128), lambda i: (i, 0))
    gate = pl.BlockSpec((bt, 128), lambda i: (i, C_GATE // 128))
    return pl.pallas_call(
        body, grid=(T // bt,), in_specs=[seg, vec, vec, seg, _ANY], out_specs=[gate, vec, vec],
        out_shape=[jax.ShapeDtypeStruct((T, C_TOT), BF16), jax.ShapeDtypeStruct((1, 128), F32),
                   jax.ShapeDtypeStruct((1, 128), F32)],
        input_output_aliases={4: 0}, name="gdn_gates_bwd",
        compiler_params=_params(("arbitrary",)))(p, alog_row, dtb_row, dbg, dp_in)


def _ssd_dt_fwd(p, dtb_row, alog_row, *, bt=256):
    T = p.shape[0]
    bt = min(bt, T)

    def body(d_ref, db_ref, al_ref, da_ref):
        lc, _ = _cum_mats(SSM_L)
        dt_l = _softplus(d_ref[...] + db_ref[...])
        alpha_l = _chunk_cumsum(dt_l * (-jnp.exp(al_ref[...])), lc, SSM_L)
        da_ref[...] = jnp.where(_iota2((bt, 128), 1) < SSM_H, dt_l, pltpu.roll(alpha_l, SSM_H, 1))

    v128 = pl.BlockSpec((1, 128), lambda i: (0, 0))
    return pl.pallas_call(
        body, grid=(T // bt,), in_specs=[pl.BlockSpec((bt, 128), lambda i: (i, 1)), v128, v128],
        out_specs=pl.BlockSpec((bt, 128), lambda i: (i, 0)), out_shape=jax.ShapeDtypeStruct((T, 128), F32),
        name="ssd_dt_fwd", compiler_params=_params(("parallel",)))(p, dtb_row, alog_row)


def _ssd_dt_bwd(p, dtb_row, alog_row, dda, dp_in, *, bt=256):
    T = p.shape[0]
    bt = min(bt, T)

    def body(d_ref, db_ref, al_ref, dda_ref, dpin_ref, dd_out, ddb_ref, dalog_ref):
        i = pl.program_id(0)
        heads = _iota2((bt, 128), 1) < SSM_H
        _, uc = _cum_mats(SSM_L)
        zz = d_ref[...] + db_ref[...]
        dt_l = _softplus(zz)
        a_row = -jnp.exp(al_ref[...])
        dda_v = dda_ref[...]
        da_l = _chunk_cumsum(jnp.where(heads, pltpu.roll(dda_v, 128 - SSM_H, 1), 0.0), uc, SSM_L)
        draw = jnp.where(heads, (dda_v + da_l * a_row) * _sigmoid(zz), 0.0)
        dd_out[...] = draw.astype(dd_out.dtype)

        @pl.when(i == 0)
        def _():
            ddb_ref[...] = jnp.zeros_like(ddb_ref)
            dalog_ref[...] = jnp.zeros_like(dalog_ref)

        ddb_ref[...] += jnp.sum(draw, axis=0, keepdims=True)
        dalog_ref[...] += jnp.sum(da_l * dt_l, axis=0, keepdims=True) * a_row

    seg = pl.BlockSpec((bt, 128), lambda i: (i, C_DT // 128))
    v128 = pl.BlockSpec((1, 128), lambda i: (0, 0))
    return pl.pallas_call(
        body, grid=(T // bt,),
        in_specs=[pl.BlockSpec((bt, 128), lambda i: (i, 1)), v128, v128, pl.BlockSpec((bt, 128), lambda i: (i, 0)), _ANY],
        out_specs=[seg, v128, v128],
        out_shape=[jax.ShapeDtypeStruct((T, C_TOT), BF16), jax.ShapeDtypeStruct((1, 128), F32),
                   jax.ShapeDtypeStruct((1, 128), F32)],
        input_output_aliases={4: 0}, name="ssd_dt_bwd",
        compiler_params=_params(("arbitrary",)))(p, dtb_row, alog_row, dda, dp_in)


def _gdn_post_fwd(o, p, w_x, *, bt=256):
    T = o.shape[0]
    bt = min(bt, T)

    def body(o_ref, z_ref, w_ref, out_ref):
        for h in range(GDN_H):
            sl = slice(h * 128, (h + 1) * 128)
            oh = o_ref[:, sl].astype(F32)
            r = lax.rsqrt(jnp.mean(oh * oh, axis=1, keepdims=True) + EPS)
            out_ref[:, sl] = (oh * r * w_ref[:, sl] * _silu(z_ref[:, sl].astype(F32))).astype(out_ref.dtype)

    row = pl.BlockSpec((bt, D), lambda i: (i, 0))
    return pl.pallas_call(
        body, grid=(T // bt,),
        in_specs=[row, pl.BlockSpec((bt, D), lambda i: (i, C_ZG // D)), pl.BlockSpec((1, D), lambda i: (0, 0))],
        out_specs=row, out_shape=jax.ShapeDtypeStruct((T, 2 * D), BF16), name="gdn_post_fwd",
        compiler_params=_params(("parallel",)))(o, p, w_x)


def _gdn_post_bwd(dx1, w_out, o, p, w_x, *, bt=512):
    T = o.shape[0]
    bt = min(bt, T)

    def body(dx_ref, wo_ref, o_ref, z_ref, w_ref, do_ref, dz_ref, dw_ref):
        i = pl.program_id(0)

        @pl.when(i == 0)
        def _():
            dw_ref[...] = jnp.zeros_like(dw_ref)

        dmix = _dot(dx_ref[...], wo_ref[...], _NT)
        for h in range(GDN_H):
            sl = slice(h * 128, (h + 1) * 128)
            oh, zh, wh = o_ref[:, sl].astype(F32), z_ref[:, sl].astype(F32), w_ref[:, sl]
            dm = dmix[:, sl]
            r = lax.rsqrt(jnp.mean(oh * oh, axis=1, keepdims=True) + EPS)
            ohat = oh * r
            dy = dm * _silu(zh)
            dz_ref[:, sl] = (dm * ohat * wh * _dsilu(zh)).astype(dz_ref.dtype)
            dohat = dy * wh
            do_ref[:, sl] = (r * (dohat - ohat * jnp.mean(dohat * ohat, axis=1, keepdims=True))).astype(do_ref.dtype)
            dw_ref[:, sl] += jnp.sum(dy * ohat, axis=0, keepdims=True)

    row = pl.BlockSpec((bt, D), lambda i: (i, 0))
    zcol = pl.BlockSpec((bt, D), lambda i: (i, C_ZG // D))
    vec = pl.BlockSpec((1, D), lambda i: (0, 0))
    return pl.pallas_call(
        body, grid=(T // bt,), in_specs=[row, pl.BlockSpec((D, D), lambda i: (0, 0)), row, zcol, vec],
        out_specs=[row, zcol, vec],
        out_shape=[jax.ShapeDtypeStruct((T, D), BF16), jax.ShapeDtypeStruct((T, C_TOT), BF16),
                   jax.ShapeDtypeStruct((1, D), F32)],
        name="gdn_post_bwd", compiler_params=_params(("arbitrary",)))(dx1, w_out, o, p, w_x)


def _ssd_post_fwd(y, xs, p, d_x, w, mix_in, *, bt=256):
    T = y.shape[0]
    bt = min(bt, T)

    def body(y_ref, x_ref, z_ref, d_ref, w_ref, mix_ref, out_ref):
        yg = (y_ref[...].astype(F32) + x_ref[...] * d_ref[...]) * _silu(z_ref[...].astype(F32))
        for g in range(2):
            sl = slice(g * 512, (g + 1) * 512)
            a = yg[:, sl]
            r = lax.rsqrt(jnp.mean(a * a, axis=1, keepdims=True) + EPS)
            out_ref[:, sl] = (a * r * w_ref[:, sl]).astype(out_ref.dtype)

    row = pl.BlockSpec((bt, D), lambda i: (i, 0))
    vec = pl.BlockSpec((1, D), lambda i: (0, 0))
    return pl.pallas_call(
        body, grid=(T // bt,),
        in_specs=[row, row, pl.BlockSpec((bt, D), lambda i: (i, C_ZS // D)), vec, vec, _ANY],
        out_specs=pl.BlockSpec((bt, D), lambda i: (i, 1)), out_shape=jax.ShapeDtypeStruct((T, 2 * D), BF16),
        input_output_aliases={5: 0}, name="ssd_post_fwd",
        compiler_params=_params(("parallel",)))(y, xs, p, d_x, w, mix_in)


def _ssd_post_bwd(dx1, w_out, y, xs, p, d_x, w, dp_in, *, bt=512):
    T = y.shape[0]
    bt = min(bt, T)

    def body(dx_ref, wo_ref, y_ref, x_ref, z_ref, d_ref, w_ref, dpin_ref, dyy_ref, dz_ref, dd_ref, dw_ref):
        i = pl.program_id(0)

        @pl.when(i == 0)
        def _():
            dd_ref[...] = jnp.zeros_like(dd_ref)
            dw_ref[...] = jnp.zeros_like(dw_ref)

        dmix = _dot(dx_ref[...], wo_ref[...], _NT)
        xv, zv = x_ref[...], z_ref[...].astype(F32)
        yy = y_ref[...].astype(F32) + xv * d_ref[...]
        sz = _silu(zv)
        yg = yy * sz
        parts = []
        for g in range(2):
            sl = slice(g * 512, (g + 1) * 512)
            a = yg[:, sl]
            r = lax.rsqrt(jnp.mean(a * a, axis=1, keepdims=True) + EPS)
            ah = a * r
            dout = dmix[:, sl]
            dah = dout * w_ref[:, sl]
            dw_ref[:, sl] += jnp.sum(dout * ah, axis=0, keepdims=True)
            parts.append(r * (dah - ah * jnp.mean(dah * ah, axis=1, keepdims=True)))
        dyg = jnp.concatenate(parts, axis=1)
        dyy = dyg * sz
        dyy_ref[...] = dyy
        dz_ref[...] = (dyg * yy * _dsilu(zv)).astype(dz_ref.dtype)
        dd_ref[...] += jnp.sum(dyy * xv, axis=0, keepdims=True)

    row = pl.BlockSpec((bt, D), lambda i: (i, 0))
    zcol = pl.BlockSpec((bt, D), lambda i: (i, C_ZS // D))
    vec = pl.BlockSpec((1, D), lambda i: (0, 0))
    return pl.pallas_call(
        body, grid=(T // bt,),
        in_specs=[row, pl.BlockSpec((D, D), lambda i: (1, 0)), row, row, zcol, vec, vec, _ANY],
        out_specs=[row, zcol, vec, vec],
        out_shape=[jax.ShapeDtypeStruct((T, D), F32), jax.ShapeDtypeStruct((T, C_TOT), BF16),
                   jax.ShapeDtypeStruct((1, D), F32), jax.ShapeDtypeStruct((1, D), F32)],
        input_output_aliases={7: 1}, name="ssd_post_bwd",
        compiler_params=_params(("arbitrary",)))(dx1, w_out, y, xs, p, d_x, w, dp_in)


_NEG = -1e30


def _gdn_terms(q, k, v, bx, gam_c):
    C = GDN_C
    ri, ci = _iota2((C, C), 0), _iota2((C, C), 1)
    eye, low, strict = ri == ci, ri >= ci, ri > ci
    gam_r = jnp.sum(jnp.where(eye, gam_c, 0.0), axis=0, keepdims=True)
    G = jnp.exp(jnp.where(low, gam_c - gam_r, _NEG))
    glast = jnp.sum(jnp.where(_iota2((C, 1), 0) == C - 1, gam_c, 0.0), axis=0, keepdims=True)
    eg, egl, eL = jnp.exp(gam_c), jnp.exp(glast - gam_c), jnp.exp(glast)
    kb, vb = k * bx, v * bx
    M = _dot(kb, k, _NT)
    return dict(eye=eye, low=low, strict=strict, G=G, eg=eg, egl=egl, eL=eL, kb=kb, vb=vb, M=M,
                kbg=kb * eg, qd=q * eg, kd=k * egl, q=q, k=k, v=v, bx=bx)


def _split(a):
    hi = a.astype(_MXU)
    return hi, (a - hi.astype(F32)).astype(_MXU)


def _dot3s(a, b):
    d = lambda p, q: lax.dot_general(p, q, _NN, preferred_element_type=F32)
    return d(a[0], b[0]) + d(a[0], b[1]) + d(a[1], b[0])


def _tri_inv_many(Ls, eye):
    eyef = jnp.where(eye, 1.0, 0.0)
    Ts = [eyef - L for L in Ls]
    Ps = [-L for L in Ls]
    for _ in range(5):
        sp = [_split(p) for p in Ps]
        Ps = [_dot3s(s, s) for s in sp]
        sp = [_split(p) for p in Ps]
        st = [_split(t) for t in Ts]
        Ts = [t + _dot3s(a, b) for t, a, b in zip(Ts, st, sp)]
    return Ts


def _lane_col(tile, idx):
    return jnp.sum(jnp.where(_iota2(tile.shape, 1) == idx, tile, 0.0), axis=1, keepdims=True)


def _gdn_heads(q_ref, k_ref, v_ref, bg_ref, heads):
    out = []
    bg = bg_ref[...]
    for h in heads:
        sl = slice(h * 128, (h + 1) * 128)
        out.append(_gdn_terms(q_ref[:, sl], k_ref[:, sl], v_ref[:, sl], _lane_col(bg, h), _lane_col(bg, GDN_H + h)))
    return out


def _gdn_prep(qk, v, bg, ride=None):
    T = qk.shape[0]
    N = T // GDN_C
    C, CS = GDN_C, GDN_LOCAL_CHUNKS
    NB = N // CS
    n_ride = ride.n if ride else 0

    def body(q_ref, k_ref, v_ref, bg_ref, *rest):
        ride_in = rest[:n_ride]
        u_ref, w_ref, qd_ref, kd_ref, p_ref, t_ref = rest[n_ride:n_ride + 6]
        ride_out = rest[n_ride + 6:2 * n_ride + 6]
        if ride:
            @pl.when(pl.program_id(0) == 0)
            def _():
                ride.start(ride_in, ride_out, rest[-3:])

            @pl.when(pl.program_id(0) == NB - 1)
            def _():
                ride.wait(ride_in, ride_out, rest[-3:])

        items = [(c, h) for c in range(CS) for h in range(GDN_H)]
        views = [[r.at[pl.ds(c * C, C)] for r in (q_ref, k_ref, v_ref, bg_ref)] for c in range(CS)]
        ts = [_gdn_heads(*views[c], [h])[0] for c, h in items]
        Ts = _tri_inv_many([jnp.where(t["strict"], t["M"] * t["G"], 0.0) for t in ts], ts[0]["eye"])
        for (c, h), t, Tm in zip(items, ts, Ts):
            tok = slice(c * C, (c + 1) * C)
            sl = slice(h * 128, (h + 1) * 128)
            rows = slice(h * C, (h + 1) * C)
            u_ref[tok, sl] = _dot(Tm, t["vb"])
            w_ref[tok, sl] = _dot(Tm, t["kbg"]).astype(w_ref.dtype)
            qd_ref[tok, sl] = t["qd"].astype(qd_ref.dtype)
            kd_ref[tok, sl] = t["kd"].astype(kd_ref.dtype)
            p_ref[c, rows, :] = _dot(t["q"], t["k"], _NT) * t["G"]
            t_ref[c, rows, :] = Tm

    blk = lambda c: pl.BlockSpec((CS * C, D), lambda n: (n, c))
    sq = pl.BlockSpec((CS, GDN_H * C, C), lambda n: (n, 0, 0))
    in_specs = [blk(0), blk(1), blk(0), pl.BlockSpec((CS * C, 128), lambda n: (n, 0))]
    out_specs = [blk(0), blk(0), blk(0), blk(0), sq, sq]
    out_shape = [jax.ShapeDtypeStruct((T, D), F32), jax.ShapeDtypeStruct((T, D), BF16),
                 jax.ShapeDtypeStruct((T, D), BF16), jax.ShapeDtypeStruct((T, D), BF16),
                 jax.ShapeDtypeStruct((N, GDN_H * C, C), F32), jax.ShapeDtypeStruct((N, GDN_H * C, C), F32)]
    ins = [qk, qk, v, bg]
    if ride:
        ins, in_specs = ins + ride.srcs, in_specs + ride.specs
        out_shape, out_specs = out_shape + ride.out_shape, out_specs + ride.specs
    res = pl.pallas_call(
        body, grid=(NB,), in_specs=in_specs, out_specs=out_specs, out_shape=out_shape,
        scratch_shapes=ride.scratch if ride else [], name="gdn_prep",
        compiler_params=_params(("arbitrary",) if ride else ("parallel",)))(*ins)
    return (list(res[:6]), list(res[6:])) if ride else list(res)


def _gdn_scan_fwd(u, w, qd, kd, pm, bg):
    T = u.shape[0]
    N = T // GDN_C
    C, CS = GDN_C, GDN_SCAN_CHUNKS

    def body(u_ref, w_ref, qd_ref, kd_ref, p_ref, bg_ref, o_ref, vn_ref, ss_ref, S_scr):
        n = pl.program_id(0)

        @pl.when(n == 0)
        def _():
            S_scr[...] = jnp.zeros_like(S_scr)

        sls = [slice(h * 128, (h + 1) * 128) for h in range(GDN_H)]
        for c in range(CS):
            rows = slice(c * C, (c + 1) * C)
            glast = bg_ref[(c + 1) * C - 1:(c + 1) * C, :]
            Ss = [S_scr[:, sl] for sl in sls]
            vns = [u_ref[rows, sl] - _dot(w_ref[rows, sl], S) for sl, S in zip(sls, Ss)]
            for h, (sl, S, vn) in enumerate(zip(sls, Ss, vns)):
                ss_ref[c, :, sl] = S.astype(ss_ref.dtype)
                vn_ref[rows, sl] = vn.astype(vn_ref.dtype)
                o_ref[rows, sl] = (_dot(qd_ref[rows, sl], S)
                                   + _dot(p_ref[c, h * C:(h + 1) * C, :], vn)).astype(o_ref.dtype)
                S_scr[:, sl] = S * jnp.exp(_lane_col(glast, GDN_H + h)) + _dot(kd_ref[rows, sl], vn, _TN)

    blk = pl.BlockSpec((CS * C, D), lambda n: (n, 0))
    return dict(
        body=body, steps=N // CS, ins=[u, w, qd, kd, pm, bg],
        in_specs=[blk, blk, blk, blk, pl.BlockSpec((CS, GDN_H * C, C), lambda n: (n, 0, 0)),
                  pl.BlockSpec((CS * C, 128), lambda n: (n, 0))],
        out_specs=[blk, blk, pl.BlockSpec((CS, GDN_DK, D), lambda n: (n, 0, 0))],
        out_shape=[jax.ShapeDtypeStruct((T, D), BF16), jax.ShapeDtypeStruct((T, D), BF16),
                   jax.ShapeDtypeStruct((N, GDN_DK, D), BF16)],
        scratch=[pltpu.VMEM((GDN_DK, D), F32)])


def _gdn_scan_bwd(w, qd, kd, pm, bg, do):
    T = w.shape[0]
    N = T // GDN_C
    C, CS = GDN_C, GDN_SCAN_CHUNKS
    NB = N // CS

    def body(w_ref, qd_ref, kd_ref, p_ref, bg_ref, do_ref, dvn_ref, ds_ref, dS_scr):
        n = pl.program_id(0)

        @pl.when(n == 0)
        def _():
            dS_scr[...] = jnp.zeros_like(dS_scr)

        sls = [slice(h * 128, (h + 1) * 128) for h in range(GDN_H)]
        for c in reversed(range(CS)):
            rows = slice(c * C, (c + 1) * C)
            glast = bg_ref[(c + 1) * C - 1:(c + 1) * C, :]
            dSs = [dS_scr[:, sl] for sl in sls]
            dvns = [_dot(p_ref[c, h * C:(h + 1) * C, :], do_ref[rows, sl], _TN) + _dot(kd_ref[rows, sl], dS2)
                    for h, (sl, dS2) in enumerate(zip(sls, dSs))]
            for h, (sl, dS2, dvn) in enumerate(zip(sls, dSs, dvns)):
                ds_ref[c, :, sl] = dS2.astype(ds_ref.dtype)
                dvn_ref[rows, sl] = dvn.astype(dvn_ref.dtype)
                dS_scr[:, sl] = (dS2 * jnp.exp(_lane_col(glast, GDN_H + h))
                                 + _dot(qd_ref[rows, sl], do_ref[rows, sl], _TN) - _dot(w_ref[rows, sl], dvn, _TN))

    blk = pl.BlockSpec((CS * C, D), lambda n: (NB - 1 - n, 0))
    return dict(
        body=body, steps=NB, ins=[w, qd, kd, pm, bg, do],
        in_specs=[blk, blk, blk, pl.BlockSpec((CS, GDN_H * C, C), lambda n: (NB - 1 - n, 0, 0)),
                  pl.BlockSpec((CS * C, 128), lambda n: (NB - 1 - n, 0)), blk],
        out_specs=[blk, pl.BlockSpec((CS, GDN_DK, D), lambda n: (NB - 1 - n, 0, 0))],
        out_shape=[jax.ShapeDtypeStruct((T, D), BF16), jax.ShapeDtypeStruct((N, GDN_DK, D), BF16)],
        scratch=[pltpu.VMEM((GDN_DK, D), F32)])


def _gdn_rest_bwd(qk, v, bg, s_save, t_save, vn, dvn, ds_save, do, ride=None):
    T = qk.shape[0]
    N = T // GDN_C
    C, CS = GDN_C, GDN_REST_CHUNKS
    NB = N // CS
    n_ride = ride.n if ride else 0

    def body(q_ref, k_ref, v_ref, bg_ref, ss_ref, ts_ref, vn_ref, dvn_ref, ds_ref, do_ref, *rest):
        ride_in = rest[:n_ride]
        dqkv_ref, dbg_ref = rest[n_ride:n_ride + 2]
        ride_out = rest[n_ride + 2:2 * n_ride + 2]
        if ride:
            @pl.when(pl.program_id(0) == 0)
            def _():
                ride.start(ride_in, ride_out, rest[-3:])

            @pl.when(pl.program_id(0) == NB - 1)
            def _():
                ride.wait(ride_in, ride_out, rest[-3:])

        items = [(c, h) for c in range(CS) for h in range(GDN_H)]
        toks = [slice(c * C, (c + 1) * C) for c, _ in items]
        sls = [slice(h * 128, (h + 1) * 128) for _, h in items]
        views = [[r.at[pl.ds(c * C, C)] for r in (q_ref, k_ref, v_ref, bg_ref)] for c in range(CS)]
        ts = [_gdn_heads(*views[c], [h])[0] for c, h in items]
        Ss = [ss_ref[c, :, sl] for (c, _), sl in zip(items, sls)]
        Tms = [ts_ref[c, h * C:(h + 1) * C, :] for c, h in items]
        dS2s = [ds_ref[c, :, sl] for (c, _), sl in zip(items, sls)]
        dos = [do_ref[tok, sl] for tok, sl in zip(toks, sls)]
        vns = [vn_ref[tok, sl] for tok, sl in zip(toks, sls)]
        dvns = [dvn_ref[tok, sl] for tok, sl in zip(toks, sls)]
        Qs = [_dot(t["q"], t["k"], _NT) for t in ts]
        dws = [-_dot(dvn, S, _NT) for dvn, S in zip(dvns, Ss)]
        dqds = [_dot(do, S, _NT) for do, S in zip(dos, Ss)]
        dPs = [jnp.where(t["low"], _dot(do, vn, _NT), 0.0) for t, do, vn in zip(ts, dos, vns)]
        dkds = [_dot(vn, dS2, _NT) for vn, dS2 in zip(vns, dS2s)]
        dTs = [_dot(dvn, t["vb"], _NT) + _dot(dw, t["kbg"], _NT) for t, dvn, dw in zip(ts, dvns, dws)]
        dvbs = [_dot(Tm, dvn, _TN) for Tm, dvn in zip(Tms, dvns)]
        dkbgs = [_dot(Tm, dw, _TN) for Tm, dw in zip(Tms, dws)]
        TdTs = [_dot(Tm, dT, _TN) for Tm, dT in zip(Tms, dTs)]
        dLs = [jnp.where(t["strict"], -_dot(TdT, Tm, _NT), 0.0) for t, TdT, Tm in zip(ts, TdTs, Tms)]
        dMs = [dL * t["G"] for t, dL in zip(ts, dLs)]
        dQs = [dP * t["G"] for t, dP in zip(ts, dPs)]
        dkbs = [_dot(dM, t["k"]) + dkbg * t["eg"] for t, dM, dkbg in zip(ts, dMs, dkbgs)]
        rs = lambda a: jnp.sum(a, axis=1, keepdims=True)
        lane = _iota2((C, 128), 1)
        last = _iota2((C, 1), 0) == C - 1
        dbg = [jnp.zeros((C, 128), F32) for _ in range(CS)]
        for i, (c, h) in enumerate(items):
            t, sl, tok = ts[i], sls[i], toks[i]
            E = (dLs[i] * t["M"] + dPs[i] * Qs[i]) * t["G"]
            dqkv_ref[tok, sl] = _dot(dQs[i], t["k"]) + dqds[i] * t["eg"]
            dqkv_ref[tok, D + h * 128:D + (h + 1) * 128] = (
                _dot(dQs[i], t["q"], _TN) + _dot(dMs[i], t["kb"], _TN) + dkds[i] * t["egl"] + dkbs[i] * t["bx"])
            dqkv_ref[tok, 2 * D + h * 128:2 * D + (h + 1) * 128] = dvbs[i] * t["bx"]
            dbeta_c = rs(dkbs[i] * t["k"] + dvbs[i] * t["v"])
            dkd_kd = dkds[i] * t["kd"]
            dgam_c = rs(dqds[i] * t["qd"]) + rs(dkbgs[i] * t["kbg"]) - rs(dkd_kd) + rs(E)
            dgam_r = -jnp.sum(E, axis=0, keepdims=True)
            dgam_c = dgam_c + jnp.sum(jnp.where(t["eye"], dgam_r, 0.0), axis=1, keepdims=True)
            dlast = _sum_all(dkd_kd) + t["eL"] * _sum_all(Ss[i].astype(F32) * dS2s[i].astype(F32))
            dgam_c = dgam_c + jnp.where(last, dlast, 0.0)
            dbg[c] = dbg[c] + jnp.where(lane == h, dbeta_c, 0.0) + jnp.where(lane == GDN_H + h, dgam_c, 0.0)
        for c in range(CS):
            dbg_ref[c * C:(c + 1) * C, :] = dbg[c]

    blk = lambda c: pl.BlockSpec((CS * C, D), lambda n: (n, c))
    st = pl.BlockSpec((CS, GDN_DK, D), lambda n: (n, 0, 0))
    seg = pl.BlockSpec((CS * C, 128), lambda n: (n, 0))
    in_specs = [blk(0), blk(1), blk(0), seg, st,
                pl.BlockSpec((CS, GDN_H * C, C), lambda n: (n, 0, 0)), blk(0), blk(0), st, blk(0)]
    out_specs = [pl.BlockSpec((CS * C, 3 * D), lambda n: (n, 0)), seg]
    out_shape = [jax.ShapeDtypeStruct((T, 3 * D), F32), jax.ShapeDtypeStruct((T, 128), F32)]
    ins = [qk, qk, v, bg, s_save, t_save, vn, dvn, ds_save, do]
    if ride:
        ins, in_specs = ins + ride.srcs, in_specs + ride.specs
        out_shape, out_specs = out_shape + ride.out_shape, out_specs + ride.specs
    res = pl.pallas_call(
        body, grid=(NB,), in_specs=in_specs, out_specs=out_specs, out_shape=out_shape,
        scratch_shapes=ride.scratch if ride else [], name="gdn_rest_bwd",
        compiler_params=_params(("arbitrary",) if ride else ("parallel",)))(*ins)
    return (list(res[:2]), list(res[2:])) if ride else list(res)


def _ssd_seg(al_pair, half, s):
    L = SSM_L
    ri, ci = _iota2((L, L), 0), _iota2((L, L), 1)
    ac = jnp.max(jnp.where(half == s, al_pair, _NEG), axis=1, keepdims=True)
    ar = jnp.sum(jnp.where(ri == ci, ac, 0.0), axis=0, keepdims=True)
    return jnp.exp(jnp.where(ri >= ci, ac - ar, _NEG))


def _last_row(a):
    return jnp.sum(jnp.where(_iota2((a.shape[0], 1), 0) == a.shape[0] - 1, a, 0.0), axis=0, keepdims=True)


def _ssd_expand(da_ref):
    da = da_ref[...]
    return _dot_sel(da, _expand_mats(6, 0)[0]), _dot_sel(da, _expand_mats(6, SSM_H)[0])


def _ssd_core_fwd(xbc, da):
    T = xbc.shape[0]
    L, CS = SSM_L, SSM_SCAN_CHUNKS
    Nc = T // L

    def body(x_all, bc_all, da_all, y_all, hs_all, H_scr):
        @pl.when(pl.program_id(0) == 0)
        def _():
            H_scr[...] = jnp.zeros_like(H_scr)

        for cc in range(CS):
            rows = pl.ds(cc * L, L)
            chunk(x_all.at[rows], bc_all.at[rows], da_all.at[rows], y_all.at[rows], hs_all.at[cc], H_scr)

    def chunk(x_ref, bc_ref, da_ref, y_ref, hs_ref, H_scr):
        dt_ref, al_ref = _ssd_expand(da_ref)
        half = _iota2((L, 128), 1) >> 6
        for g in range(2):
            gs = slice(g * 512, (g + 1) * 512)
            Bg = bc_ref[:, g * 128:(g + 1) * 128]
            Cg = bc_ref[:, 256 + g * 128:256 + (g + 1) * 128]
            alg = al_ref[:, gs]
            alast = _last_row(alg)
            xdt = x_ref[:, gs] * dt_ref[:, gs]
            Hg = H_scr[:, gs]
            hs_ref[:, gs] = Hg
            CB = _dot(Cg, Bg, _NT)
            y_off = jnp.exp(alg) * _dot(Cg, Hg)
            H_scr[:, gs] = Hg * jnp.exp(alast) + _dot(Bg, jnp.exp(alast - alg) * xdt, _TN)
            for j in range(4):
                ps = slice(g * 512 + j * 128, g * 512 + (j + 1) * 128)
                al_pair = al_ref[:, ps]
                xp = x_ref[:, ps] * dt_ref[:, ps]
                ys = [_dot(_ssd_seg(al_pair, half, s) * CB, xp) for s in range(2)]
                y_ref[:, ps] = (y_off[:, j * 128:(j + 1) * 128]
                                + jnp.where(half == 0, ys[0], ys[1])).astype(y_ref.dtype)

    row = pl.BlockSpec((CS * L, D), lambda c: (c, 0))
    return dict(
        body=body, steps=Nc // CS, ins=[xbc, xbc, da],
        in_specs=[row, pl.BlockSpec((CS * L, 512), lambda c: (c, 2)), pl.BlockSpec((CS * L, 128), lambda c: (c, 0))],
        out_specs=[row, pl.BlockSpec((CS, SSM_N, D), lambda c: (c, 0, 0))],
        out_shape=[jax.ShapeDtypeStruct((T, D), BF16), jax.ShapeDtypeStruct((Nc, SSM_N, D), F32)],
        scratch=[pltpu.VMEM((SSM_N, D), F32)])


def _ssd_core_bwd(xbc, da, h_save, dyy, d_x):
    T = xbc.shape[0]
    L, CS = SSM_L, SSM_SCAN_CHUNKS
    Nc = T // L
    NB = Nc // CS

    def body(x_all, bc_all, da_all, hs_all, dy_all, d_ref, dx_all, dda_all, dH_scr, ddt_ref, dal_ref):
        @pl.when(pl.program_id(0) == 0)
        def _():
            dH_scr[...] = jnp.zeros_like(dH_scr)

        for cc in reversed(range(CS)):
            rows = pl.ds(cc * L, L)
            chunk(x_all.at[rows], bc_all.at[rows], da_all.at[rows], hs_all.at[cc], dy_all.at[rows],
                  d_ref, dx_all.at[rows], ddt_ref, dal_ref, dH_scr)
            dda_all[rows, :] = (_dot_sel(ddt_ref[...], _expand_mats(6, 0)[1])
                                + _dot_sel(dal_ref[...], _expand_mats(6, SSM_H)[1]))

    def chunk(x_ref, bc_ref, da_ref, hs_ref, dy_ref, d_ref, dx_ref, ddt_ref, dal_ref, dH_scr):
        dt_ref, al_ref = _ssd_expand(da_ref)
        lane = _iota2((L, 128), 1)
        half = lane >> 6
        rowi = _iota2((L, 1), 0)
        ri, ci = _iota2((L, L), 0), _iota2((L, L), 1)
        for g in range(2):
            gs = slice(g * 512, (g + 1) * 512)
            Bg = bc_ref[:, g * 128:(g + 1) * 128]
            Cg = bc_ref[:, 256 + g * 128:256 + (g + 1) * 128]
            alg = al_ref[:, gs]
            alast = _last_row(alg)
            eal, edec, eL = jnp.exp(alg), jnp.exp(alast - alg), jnp.exp(alast)
            xg, dtg, dYg = x_ref[:, gs], dt_ref[:, gs], dy_ref[:, gs]
            xdt = xg * dtg
            Hg = hs_ref[:, gs]
            dH2 = dH_scr[:, gs]
            CB = _dot(Cg, Bg, _NT)
            dYe = eal * dYg
            dH_scr[:, gs] = dH2 * eL + _dot(Cg, dYe, _TN)
            dC = _dot(dYe, Hg, _NT)
            zg = edec * xdt
            dz = _dot(Bg, dH2)
            dB = _dot(zg, dH2, _NT)
            tz = dz * zg
            dal = dYe * _dot(Cg, Hg) - tz
            dalast = jnp.sum(tz, axis=0, keepdims=True) + eL * jnp.sum(Hg * dH2, axis=0, keepdims=True)
            dal = dal + jnp.where(rowi == L - 1, dalast, 0.0)
            dxdt_g = edec * dz
            dx_ref[:, gs] = dxdt_g * dtg + dYg * d_ref[:, gs]
            ddt_ref[:, gs] = dxdt_g * xg
            dal_ref[:, gs] = dal
            dCB = jnp.zeros((L, L), F32)
            for j in range(4):
                ps = slice(g * 512 + j * 128, g * 512 + (j + 1) * 128)
                al_pair = al_ref[:, ps]
                xp = x_ref[:, ps] * dt_ref[:, ps]
                dYp = dy_ref[:, ps]
                dxp = []
                dal_p = jnp.zeros((L, 128), F32)
                for s in range(2):
                    seg = _ssd_seg(al_pair, half, s)
                    W = seg * CB
                    dW = _dot(jnp.where(half == s, dYp, 0.0), xp, _NT)
                    dxp.append(_dot(W, dYp, _TN))
                    dCB = dCB + dW * seg
                    Es = dW * W
                    dac = jnp.sum(Es, axis=1, keepdims=True) - jnp.sum(
                        jnp.where(ri == ci, jnp.sum(Es, axis=0, keepdims=True), 0.0), axis=1, keepdims=True)
                    dal_p = dal_p + jnp.where(lane == 64 * s, dac, 0.0)
                dxdt_p = jnp.where(half == 0, dxp[0], dxp[1])
                dx_ref[:, ps] += dxdt_p * dt_ref[:, ps]
                ddt_ref[:, ps] += dxdt_p * x_ref[:, ps]
                dal_ref[:, ps] += dal_p
            dx_ref[:, D + g * 128:D + (g + 1) * 128] = dB + _dot(dCB, Cg, _TN)
            dx_ref[:, D + 256 + g * 128:D + 256 + (g + 1) * 128] = dC + _dot(dCB, Bg)

    row = pl.BlockSpec((CS * L, D), lambda c: (NB - 1 - c, 0))
    bcs = pl.BlockSpec((CS * L, 512), lambda c: (NB - 1 - c, 2))
    seg = pl.BlockSpec((CS * L, 128), lambda c: (NB - 1 - c, 0))
    return dict(
        body=body, steps=NB, ins=[xbc, xbc, da, h_save, dyy, d_x],
        in_specs=[row, bcs, seg, pl.BlockSpec((CS, SSM_N, D), lambda c: (NB - 1 - c, 0, 0)), row,
                  pl.BlockSpec((1, D), lambda c: (0, 0))],
        out_specs=[pl.BlockSpec((CS * L, D + 512), lambda c: (NB - 1 - c, 0)), seg],
        out_shape=[jax.ShapeDtypeStruct((T, D + 512), F32), jax.ShapeDtypeStruct((T, 128), F32)],
        scratch=[pltpu.VMEM((SSM_N, D), F32), pltpu.VMEM((L, D), F32), pltpu.VMEM((L, D), F32)])


def _run_scans(parts, *, name):
    steps = parts[0]["steps"]
    assert all(p["steps"] == steps for p in parts)
    cnt = lambda key: [len(p[key]) for p in parts]
    n_in, n_out, n_scr = cnt("ins"), cnt("out_shape"), cnt("scratch")

    def body(*refs):
        ins, outs, scr = refs[:sum(n_in)], refs[sum(n_in):sum(n_in) + sum(n_out)], refs[sum(n_in) + sum(n_out):]
        oi = oo = os_ = 0
        for p, a, b, c in zip(parts, n_in, n_out, n_scr):
            p["body"](*ins[oi:oi + a], *outs[oo:oo + b], *scr[os_:os_ + c])
            oi, oo, os_ = oi + a, oo + b, os_ + c

    cat = lambda key: [v for p in parts for v in p[key]]
    res = pl.pallas_call(
        body, grid=(steps,), in_specs=cat("in_specs"), out_specs=cat("out_specs"), out_shape=cat("out_shape"),
        scratch_shapes=cat("scratch"), name=name, compiler_params=_params(("arbitrary",)))(*cat("ins"))
    out, o = [], 0
    for b in n_out:
        out.append(list(res[o:o + b]))
        o += b
    return out


_EARLY = ("w_out", "wq_mem", "wk_mem", "wv_mem", "wo_mem")
_LATE = ("w_up", "w_down")
_GRADS_MLP = ("w_down", "w_up")
_GRADS_MID = ("wo_mem", "wq_mem", "wk_mem", "wv_mem", "w_out")


def _gather_ride(shards, names):
    return None if shards is None else _Ride([shards[n] for n in names], shard=True)


def _grad_ride(shards, G, names):
    return None if shards is None else _Ride([_slots_from_full(n, G[n]) for n in names], shard=False)


def _local_step(x, mem, tgt, W, shards=None):
    T = x.shape[0]
    W = dict(W)
    cw_qk, cw_v = W["gdn_conv_w"][:, :2 * D], W["gdn_conv_w"][:, 2 * D:]
    h1 = _rmsnorm_fwd(x, W["norm1_w"], name="norm1_fwd")
    ride = _gather_ride(shards, _EARLY)
    pg = _mm(h1, W["w_in_pad"], b_cols=(C_GATE, C_TOT - C_GATE), name="in_proj_gates")
    p = _mm(h1, W["w_in_pad"], b_cols=(0, C_GATE), out_dtype=BF16, bn_cap=1664, b_resident=True, name="in_proj",
            ride=ride)
    if ride:
        p, got = p
        W.update({n: _full_from_slots(n, g) for n, g in zip(_EARLY, got)})
    qk = _conv_fwd(p, C_QKV, 2 * D, cw_qk, None, l2=True, name="gdn_conv_qk_fwd")
    v_g = _conv_fwd(p, C_QKV + 2 * D, D, cw_v, None, l2=False, name="gdn_conv_v_fwd")
    bg = _gdn_gates_fwd(pg, W["gdn_alog_row"], W["gdn_dtb_row"])
    ride = _gather_ride(shards, _LATE)
    prep = _gdn_prep(qk, v_g, bg, ride)
    if ride:
        prep, got = prep
        W.update({n: _full_from_slots(n, g) for n, g in zip(_LATE, got)})
    u_g, w_g, qd_g, kd_g, p_g, t_save = prep
    xbc = _conv_fwd(p, C_XBC, D + 512, W["ssm_conv_w"], W["ssm_conv_b"], l2=False, name="ssm_conv_fwd", bc=512)
    da_s = _ssd_dt_fwd(pg, W["ssm_dtb_row"], W["ssm_alog_row"])
    (o_g, vn_g, s_save), (y_s, h_save) = _run_scans(
        [_gdn_scan_fwd(u_g, w_g, qd_g, kd_g, p_g, bg), _ssd_core_fwd(xbc, da_s)], name="scans_fwd")
    mix = _gdn_post_fwd(o_g, p, W["gdn_norm_x"])
    mix = _ssd_post_fwd(y_s, xbc, p, W["ssm_d_x"], W["ssm_norm_w"].reshape(1, D), mix)
    x1, h2 = _mm(mix, W["w_out"], epi="res_norm", extra=(x, W["norm2_w"]), bm=512, name="out_proj")
    qm = _mm(h2, W["wq_mem"], out_dtype=BF16, name="q_proj")
    m = _rmsnorm_fwd(mem, W["mem_norm_w"], name="mem_norm_fwd")
    km = _mm(m, W["wk_mem"], name="k_proj")
    vm = _mm(m, W["wv_mem"], name="v_proj")
    oa = _attn_fwd(qm, km, vm)
    x2, h3 = _mm(oa, W["wo_mem"], epi="res_norm", extra=(x1, W["norm3_w"]), bm=512, name="o_proj")
    u, act = _mm(h3, W["w_up"], epi="relu2", out_dtype=BF16, b_resident=True, name="mlp_up")
    dx3, g_final, loss = _mm(act, W["w_down"], epi="res_loss", extra=(x2, tgt, W["final_norm_w"]), bk_cap=1024,
                             b_resident=True, name="mlp_down_loss")
    G = {"final_norm_w": g_final.reshape(D)}
    dpre = _mm(dx3, W["w_down"], dims="nt", epi="mul2", extra=u, out_dtype=BF16, b_resident=True, name="mlp_down_dx")
    G["w_down"] = _mm(act, dx3, dims="tn", out_dtype=BF16, name="mlp_down_dw")
    G["w_up"] = _mm(h3, dpre, dims="tn", out_dtype=BF16, name="mlp_up_dw")
    dx2, gw = _mm(dpre, W["w_up"], dims="nt", epi="norm_bwd", extra=(x2, dx3, W["norm3_w"]), bk_cap=1024,
                  b_resident=True, name="mlp_up_dx")
    G["norm3_w"] = gw.reshape(D)
    do_a = _mm(dx2, W["wo_mem"], dims="nt", out_dtype=BF16, name="o_proj_dx")
    G["wo_mem"] = _mm(oa, dx2, dims="tn", out_dtype=BF16, name="o_proj_dw")
    dq, dk, dv = _attn_bwd(qm, km, vm, do_a)
    G["wq_mem"] = _mm(h2, dq, dims="tn", out_dtype=BF16, name="q_proj_dw")
    dx1, gw = _mm(dq, W["wq_mem"], dims="nt", epi="norm_bwd", extra=(x1, dx2, W["norm2_w"]), bm=512,
                  name="q_proj_dx")
    G["norm2_w"] = gw.reshape(D)
    G["wk_mem"] = _mm(m, dk, dims="tn", out_dtype=BF16, name="k_proj_dw")
    G["wv_mem"] = _mm(m, dv, dims="tn", out_dtype=BF16, name="v_proj_dw")
    dm = _mm(dk, W["wk_mem"], dims="nt", name="k_proj_dx")
    dm = _mm(dv, W["wv_mem"], dims="nt", epi="res", extra=dm, name="v_proj_dx")
    _, G["mem_norm_w"] = _rmsnorm_bwd(mem, W["mem_norm_w"], dm, None, name="mem_norm_bwd")
    G["w_out"] = _mm(mix, dx1, dims="tn", out_dtype=BF16, name="out_proj_dw")
    do_g, dp, G["gdn_norm_x"] = _gdn_post_bwd(dx1, W["w_out"], o_g, p, W["gdn_norm_x"])
    dyy, dp, G["ssm_d_x"], G["ssm_norm_w"] = _ssd_post_bwd(dx1, W["w_out"], y_s, xbc, p, W["ssm_d_x"],
                                                          W["ssm_norm_w"].reshape(1, D), dp)
    (dvn_g, ds_save), (dxbc, dda_s) = _run_scans(
        [_gdn_scan_bwd(w_g, qd_g, kd_g, p_g, bg, do_g), _ssd_core_bwd(xbc, da_s, h_save, dyy, W["ssm_d_x"])],
        name="scans_bwd")
    ride = _grad_ride(shards, G, _GRADS_MLP)
    rest = _gdn_rest_bwd(qk, v_g, bg, s_save, t_save, vn_g, dvn_g, ds_save, do_g, ride)
    if ride:
        rest, got = rest
        G.update(zip(_GRADS_MLP, got))
    dqkvn, dbg = rest
    dy_qk, gcw_qk, _ = _conv_bwd_act(p, C_QKV, 2 * D, cw_qk, None, dqkvn, 0, l2=True, name="gdn_conv_qk_bwd_act")
    dy_v, gcw_v, _ = _conv_bwd_act(p, C_QKV + 2 * D, D, cw_v, None, dqkvn, 2 * D, l2=False,
                                   name="gdn_conv_v_bwd_act")
    G["gdn_conv_w"] = jnp.concatenate([gcw_qk, gcw_v], axis=1)
    dp = _conv_bwd_in(dy_qk, cw_qk, dp, C_QKV, T, name="gdn_conv_qk_bwd_in")
    dp = _conv_bwd_in(dy_v, cw_v, dp, C_QKV + 2 * D, T, name="gdn_conv_v_bwd_in")
    dp, G["gdn_alog_row"], G["gdn_dtb_row"] = _gdn_gates_bwd(pg, W["gdn_alog_row"], W["gdn_dtb_row"], dbg, dp)
    dy_s, G["ssm_conv_w"], G["ssm_conv_b"] = _conv_bwd_act(p, C_XBC, D + 512, W["ssm_conv_w"], W["ssm_conv_b"],
                                                           dxbc, 0, l2=False, name="ssm_conv_bwd_act", bc=512)
    dp = _conv_bwd_in(dy_s, W["ssm_conv_w"], dp, C_XBC, T, name="ssm_conv_bwd_in", bc=512)
    dp, G["ssm_dtb_row"], G["ssm_alog_row"] = _ssd_dt_bwd(pg, W["ssm_dtb_row"], W["ssm_alog_row"], dda_s, dp)
    ride = _grad_ride(shards, G, _GRADS_MID)
    g_in = _mm(h1, dp, dims="tn", out_dtype=BF16, bn_cap=1152, name="in_proj_dw", ride=ride)
    if ride:
        g_in, got = g_in
        G.update(zip(_GRADS_MID, got))
    G["w_in"] = _unpad_w_in(g_in)
    ride = _grad_ride(shards, G, ("w_in",))
    res = _mm(dp, W["w_in_pad"], dims="nt", epi="norm_bwd", extra=(x, dx1, W["norm1_w"]), b_resident=True,
              name="in_proj_dx", ride=ride)
    if ride:
        res, got = res
        G["w_in"] = got[0]
    dx, gw = res
    G["norm1_w"] = gw.reshape(D)
    return loss, dx, G


def _all_gather(shards, out_dtype, *, name):
    n = len(shards)

    def body(*refs):
        x_refs, out_refs, stage = refs[:n], refs[n:2 * n], refs[2 * n:3 * n]
        send_sems, recv_sems, local_sems = refs[3 * n:]
        x, y, c = _place()
        me, sibling = (x, y, c), (x, y, 1 - c)
        chips = [(1 - x, y), (x, 1 - y), (1 - x, 1 - y)]

        def slot(px, py, pc):
            return 4 * px + 2 * py + pc

        def copy(a, k, block, to, src=None):
            dst = out_refs[a].at[slot(*block)]
            return pltpu.make_async_remote_copy(
                src_ref=dst if src is None else src, dst_ref=dst, send_sem=send_sems.at[a, k],
                recv_sem=recv_sems.at[a, k], device_id=to, device_id_type=_MESH)

        for a in range(n):
            stage[a][...] = x_refs[a][...].astype(out_dtype)
        mine = [pltpu.make_async_copy(stage[a], out_refs[a].at[slot(*me)], local_sems.at[a]) for a in range(n)]
        for cp in mine:
            cp.start()
        first = []
        for a in range(n):
            first.append(copy(a, 0, me, sibling, src=stage[a]))
            first += [copy(a, 1 + j, me, (*chip, c), src=stage[a]) for j, chip in enumerate(chips)]
        for cp in first:
            cp.start()
        passed = [[copy(a, 4 + j, (*chip, c), sibling) for j, chip in enumerate(chips)] for a in range(n)]
        for j, chip in enumerate(chips):
            for a in range(n):
                copy(a, 1 + j, (*chip, c), me).wait_recv()
                passed[a][j].start()
        for a in range(n):
            copy(a, 0, sibling, me).wait_recv()
            for j, chip in enumerate(chips):
                copy(a, 4 + j, (*chip, 1 - c), me).wait_recv()
        for cp in first + [cp for row in passed for cp in row]:
            cp.wait_send()
        for cp in mine:
            cp.wait()

    outs = pl.pallas_call(
        body, in_specs=[_VM] * n, out_specs=[_ANY] * n,
        out_shape=[jax.ShapeDtypeStruct((N_DEV,) + s.shape, out_dtype) for s in shards],
        scratch_shapes=[pltpu.VMEM(s.shape, out_dtype) for s in shards]
        + [pltpu.SemaphoreType.DMA((n, 7)), pltpu.SemaphoreType.DMA((n, 7)), pltpu.SemaphoreType.DMA((n,))],
        name=name, compiler_params=pltpu.CompilerParams(vmem_limit_bytes=VMEM_LIMIT))(*shards)
    return list(outs)


def _cast_bf16(arrs, *, name):
    n = len(arrs)

    def body(*refs):
        for a in range(n):
            refs[n + a][...] = refs[a][...].astype(BF16)

    return list(pl.pallas_call(
        body, in_specs=[_VM] * n, out_specs=[_VM] * n,
        out_shape=[jax.ShapeDtypeStruct(s.shape, BF16) for s in arrs], name=name,
        compiler_params=pltpu.CompilerParams(vmem_limit_bytes=VMEM_LIMIT))(*arrs))


def _sum8(a, *, name):
    _, R, Cc = a.shape
    br = _pick_rows(R, 128)

    def body(a_ref, o_ref):
        s = a_ref[0].astype(F32)
        for k in range(1, N_DEV):
            s = s + a_ref[k].astype(F32)
        o_ref[...] = s

    return pl.pallas_call(
        body, grid=(R // br,), in_specs=[pl.BlockSpec((N_DEV, br, Cc), lambda i: (0, i, 0))],
        out_specs=pl.BlockSpec((br, Cc), lambda i: (i, 0)), out_shape=jax.ShapeDtypeStruct((R, Cc), F32),
        name=name, compiler_params=_params(("parallel",)))(a)


def _pick_rows(R, cap):
    if R <= cap:
        return R
    for d in range(cap, 7, -8):
        if R % d == 0:
            return d
    return R


def _adamw(w, g, m, v, *, name):
    shape = w.shape
    as2d = (lambda t: t.reshape(1, -1)) if w.ndim == 1 else (lambda t: t)
    w2, m2, v2 = as2d(w), as2d(m), as2d(v)
    R, Cc = w2.shape
    from_slabs = g.ndim == 3
    br = _pick_rows(R, 128 if from_slabs else 256)
    c1 = 1.0 - ADAM_B1 ** ADAM_STEP
    c2 = 1.0 - ADAM_B2 ** ADAM_STEP

    def body(w_ref, g_ref, m_ref, v_ref, go_ref, d_ref, nm_ref, nv_ref):
        if from_slabs:
            gv = g_ref[0].astype(F32)
            for k in range(1, N_DEV):
                gv = gv + g_ref[k].astype(F32)
        else:
            gv = g_ref[...]
        go_ref[...] = gv
        nm = ADAM_B1 * m_ref[...] + (1.0 - ADAM_B1) * gv
        nv = ADAM_B2 * v_ref[...] + (1.0 - ADAM_B2) * (gv * gv)
        nm_ref[...] = nm
        nv_ref[...] = nv
        d_ref[...] = -ADAM_LR * ((nm / c1) / (jnp.sqrt(nv / c2) + ADAM_EPS) + ADAM_WD * w_ref[...])

    blk = pl.BlockSpec((br, Cc), lambda i: (i, 0))
    g_spec = pl.BlockSpec((N_DEV, br, Cc), lambda i: (0, i, 0)) if from_slabs else blk
    outs = pl.pallas_call(
        body, grid=(R // br,), in_specs=[blk, g_spec, blk, blk], out_specs=[blk] * 4,
        out_shape=[jax.ShapeDtypeStruct((R, Cc), F32)] * 4, name=name,
        compiler_params=_params(("parallel",)))(w2, g if from_slabs else as2d(g), m2, v2)
    return tuple(o.reshape(shape) for o in outs)


_BIG = ("w_in", "w_out", "wq_mem", "wk_mem", "wv_mem", "wo_mem", "w_up", "w_down")
_COL_SHARDED = ("w_in", "w_up")
_WEIGHTS = ("norm1_w", "w_in", "gdn_conv_w", "gdn_a_log", "gdn_dt_bias", "gdn_norm_w", "ssm_conv_w", "ssm_conv_b",
            "ssm_a_log", "ssm_dt_bias", "ssm_d", "ssm_norm_w", "w_out", "norm2_w", "mem_norm_w", "wq_mem", "wk_mem",
            "wv_mem", "wo_mem", "norm3_w", "w_up", "w_down", "final_norm_w")
_IN_PAD = 112


def _move_col_slabs(a, to_slabs, *, name):
    n, R, c = (N_DEV, a.shape[0], a.shape[1] // N_DEV) if to_slabs else a.shape
    slab = pl.BlockSpec((None, R, c), lambda j: (j, 0, 0))
    cols = pl.BlockSpec((R, c), lambda j: (0, j))

    def body(a_ref, o_ref):
        o_ref[...] = a_ref[...]

    return pl.pallas_call(
        body, grid=(n,), in_specs=[cols if to_slabs else slab], out_specs=slab if to_slabs else cols,
        out_shape=jax.ShapeDtypeStruct((n, R, c) if to_slabs else (R, n * c), a.dtype), name=name,
        compiler_params=_params(("parallel",)))(a)


def _full_from_slots(name, g):
    if name in _COL_SHARDED:
        if g.shape[2] % 128 == 0:
            return _move_col_slabs(g, False, name="cols_" + name)
        return jnp.transpose(g, (1, 0, 2)).reshape(g.shape[1], N_DEV * g.shape[2])
    return g.reshape(N_DEV * g.shape[1], g.shape[2])


def _slots_from_full(name, f):
    if name in _COL_SHARDED:
        if (f.shape[1] // N_DEV) % 128 == 0:
            return _move_col_slabs(f, True, name="slabs_" + name)
        return jnp.transpose(f.reshape(f.shape[0], N_DEV, f.shape[1] // N_DEV), (1, 0, 2))
    return f.reshape(N_DEV, f.shape[0] // N_DEV, f.shape[1])


def _pad_w_in(w):
    z = jnp.zeros((w.shape[0], _IN_PAD), w.dtype)
    return jnp.concatenate([w[:, :4096], w[:, 4112:6672], w[:, 4096:4112], z, w[:, 6672:6688], z], axis=1)


def _unpad_w_in(gp):
    return jnp.concatenate([gp[:, :4096], gp[:, C_GATE:C_GATE + 16], gp[:, 4096:C_GATE], gp[:, C_DT:C_DT + 16]],
                           axis=1)


def _pack_rows(vals):
    rows, offs, r = [], [], 0
    for vflat in vals:
        nrow = 8 * -(-vflat.shape[0] // 1024)
        rows.append(jnp.pad(vflat, (0, nrow * 128 - vflat.shape[0])).reshape(nrow, 128))
        offs.append((r, vflat.shape[0]))
        r += nrow
    return jnp.concatenate(rows, axis=0), offs


def _unpack_rows(packed, offs, shapes):
    out = []
    for (r, nel), shp in zip(offs, shapes):
        nrow = -(-nel // 128)
        out.append(packed[r:r + nrow].reshape(-1)[:nel].reshape(shp))
    return out


def kernel(x, mem, norm1_w, w_in, gdn_conv_w, gdn_a_log, gdn_dt_bias, gdn_norm_w, ssm_conv_w, ssm_conv_b, ssm_a_log, ssm_dt_bias, ssm_d, ssm_norm_w, w_out, norm2_w, mem_norm_w, wq_mem, wk_mem, wv_mem, wo_mem, norm3_w, w_up, w_down, final_norm_w, loss_target, m_norm1_w, m_w_in, m_gdn_conv_w, m_gdn_a_log, m_gdn_dt_bias, m_gdn_norm_w, m_ssm_conv_w, m_ssm_conv_b, m_ssm_a_log, m_ssm_dt_bias, m_ssm_d, m_ssm_norm_w, m_w_out, m_norm2_w, m_mem_norm_w, m_wq_mem, m_wk_mem, m_wv_mem, m_wo_mem, m_norm3_w, m_w_up, m_w_down, m_final_norm_w, v_norm1_w, v_w_in, v_gdn_conv_w, v_gdn_a_log, v_gdn_dt_bias, v_gdn_norm_w, v_ssm_conv_w, v_ssm_conv_b, v_ssm_a_log, v_ssm_dt_bias, v_ssm_d, v_ssm_norm_w, v_w_out, v_norm2_w, v_mem_norm_w, v_wq_mem, v_wk_mem, v_wv_mem, v_wo_mem, v_norm3_w, v_w_up, v_w_down, v_final_norm_w):
    args = dict(locals())
    w_loc = {n: args[n] for n in _WEIGHTS}
    me = 4 * lax.axis_index("x") + 2 * lax.axis_index("y") + lax.axis_index("c")

    w_in_full = _full_from_slots("w_in", _all_gather([w_in], BF16, name="gather_w_in")[0])
    later = _EARLY + _LATE
    shards = dict(zip(later, _cast_bf16([w_loc[n] for n in later], name="cast_shards")))
    conv_pack, conv_offs = _pack_rows([gdn_conv_w.reshape(-1), ssm_conv_w.reshape(-1)])
    conv_all = _all_gather([conv_pack], F32, name="gather_conv")[0]
    gdn_cw, ssm_cw = [], []
    for k in range(N_DEV):
        a, b = _unpack_rows(conv_all[k], conv_offs, [gdn_conv_w.shape, ssm_conv_w.shape])
        gdn_cw.append(a)
        ssm_cw.append(b)
    W = {
        "w_in_pad": _pad_w_in(w_in_full),
        "norm1_w": norm1_w, "norm2_w": norm2_w, "norm3_w": norm3_w, "mem_norm_w": mem_norm_w,
        "final_norm_w": final_norm_w, "ssm_norm_w": ssm_norm_w, "ssm_conv_b": ssm_conv_b,
        "gdn_conv_w": jnp.concatenate(gdn_cw, axis=1), "ssm_conv_w": jnp.concatenate(ssm_cw, axis=1),
        "gdn_alog_row": jnp.pad(gdn_a_log, (GDN_H, 128 - 2 * GDN_H)).reshape(1, 128),
        "gdn_dtb_row": jnp.pad(gdn_dt_bias, (GDN_H, 128 - 2 * GDN_H)).reshape(1, 128),
        "gdn_norm_x": jnp.tile(gdn_norm_w, GDN_H).reshape(1, D),
        "ssm_dtb_row": jnp.pad(ssm_dt_bias, (0, 128 - SSM_H)).reshape(1, 128),
        "ssm_alog_row": jnp.pad(ssm_a_log, (0, 128 - SSM_H)).reshape(1, 128),
        "ssm_d_x": jnp.repeat(ssm_d, SSM_P).reshape(1, D),
    }

    loss_part, grad_x, G = _local_step(x[0], mem[0], loss_target[0], W, shards)

    grads = {n: G[n] for n in _BIG}

    small = {
        "norm1_w": G["norm1_w"], "gdn_conv_w": G["gdn_conv_w"], "gdn_a_log": G["gdn_alog_row"][0, GDN_H:2 * GDN_H],
        "gdn_dt_bias": G["gdn_dtb_row"][0, GDN_H:2 * GDN_H], "gdn_norm_w": G["gdn_norm_x"].reshape(GDN_H, 128).sum(0),
        "ssm_conv_w": G["ssm_conv_w"], "ssm_conv_b": G["ssm_conv_b"],
        "ssm_a_log": G["ssm_alog_row"][0, :SSM_H], "ssm_dt_bias": G["ssm_dtb_row"][0, :SSM_H],
        "ssm_d": G["ssm_d_x"].reshape(SSM_H, SSM_P).sum(1), "ssm_norm_w": G["ssm_norm_w"].reshape(D),
        "norm2_w": G["norm2_w"], "mem_norm_w": G["mem_norm_w"], "norm3_w": G["norm3_w"],
        "final_norm_w": G["final_norm_w"], "loss": loss_part[0, :1],
    }
    names = list(small)
    pack, offs = _pack_rows([small[n].reshape(-1) for n in names])
    tot = _sum8(_all_gather([pack], F32, name="gather_small")[0], name="sum_small")
    summed = dict(zip(names, _unpack_rows(tot, offs, [small[n].shape for n in names])))
    loss = summed.pop("loss")[0]
    for n in ("gdn_conv_w", "ssm_conv_w"):
        width = w_loc[n].shape[1]
        summed[n] = lax.dynamic_slice_in_dim(summed[n], me * width, width, axis=1)
    grads.update(summed)

    upd = {n: _adamw(w_loc[n], grads[n], args["m_" + n], args["v_" + n], name="adamw_" + n) for n in _WEIGHTS}
    return (loss, grad_x[None], *[upd[n][0] for n in _WEIGHTS], *[upd[n][1] for n in _WEIGHTS],
            *[upd[n][2] for n in _WEIGHTS], *[upd[n][3] for n in _WEIGHTS])
```

```python
import jax
import jax.numpy as jnp
from jax import lax
from jax.experimental import pallas as pl
from jax.experimental.pallas import tpu as pltpu

F32 = jnp.float32
BF16 = jnp.bfloat16
_MXU = BF16

D = 1024
EPS = 1e-6
CONV_K = 4
GDN_H, GDN_DK, GDN_C = 8, 128, 64
GDN_SCAN_CHUNKS = 8
GDN_LOCAL_CHUNKS = 4
GDN_REST_CHUNKS = 4
SSM_H, SSM_P, SSM_L, SSM_N = 16, 64, 128, 128
SSM_SCAN_CHUNKS = 4
MEM_H, MEM_HD = 4, 256
D_FF = 4096
N_DEV = 8

C_QKV, C_ZG, C_ZS, C_XBC, C_GATE, C_DT, C_TOT = 0, 3072, 4096, 5120, 6656, 6784, 6912
P_HALO = 16

ADAM_LR, ADAM_B1, ADAM_B2, ADAM_EPS, ADAM_WD, ADAM_STEP = 0.001, 0.9, 0.999, 1e-08, 0.01, 10

VMEM_LIMIT = 56 * 1024 * 1024

_NN = (((1,), (0,)), ((), ()))
_NT = (((1,), (1,)), ((), ()))
_TN = (((0,), (0,)), ((), ()))


def _dot(a, b, dims=_NN):
    return lax.dot_general(a.astype(_MXU), b.astype(_MXU), dims, preferred_element_type=F32)


def _split3(a):
    a1 = a.astype(BF16)
    r1 = a - a1.astype(F32)
    a2 = r1.astype(BF16)
    return a1, a2, (r1 - a2.astype(F32)).astype(BF16)


def _dot_sel(a, e):
    eb = e.astype(BF16)
    return sum(lax.dot_general(p, eb, _NN, preferred_element_type=F32) for p in _split3(a))


def _sel_dot(e, a):
    eb = e.astype(BF16)
    return sum(lax.dot_general(eb, p, _NN, preferred_element_type=F32) for p in _split3(a))


def _chunk_cumsum(a, tri, chunk):
    return jnp.concatenate([_sel_dot(tri, a[r:r + chunk]) for r in range(0, a.shape[0], chunk)], axis=0)


def _params(sem):
    return pltpu.CompilerParams(dimension_semantics=sem, vmem_limit_bytes=VMEM_LIMIT)


def _pick(n, cap):
    for d in range(min(cap, n), 0, -128):
        if n % d == 0 and d % 128 == 0:
            return d
    return n


def _sigmoid(x):
    return 0.5 * jnp.tanh(0.5 * x) + 0.5


def _silu(x):
    return x * _sigmoid(x)


def _dsilu(x):
    s = _sigmoid(x)
    return s * (1.0 + x * (1.0 - s))


def _softplus(x):
    return jnp.maximum(x, 0.0) + jnp.log(1.0 + jnp.exp(-jnp.abs(x)))


def _iota2(shape, axis):
    return lax.broadcasted_iota(jnp.int32, shape, axis)


def _sum_all(x):
    return jnp.sum(jnp.sum(x, axis=1, keepdims=True), axis=0, keepdims=True)


_MESH = pl.DeviceIdType.MESH
_ANY = pl.BlockSpec(memory_space=pl.ANY)
_VM = pl.BlockSpec(memory_space=pltpu.VMEM)
_REL = [(r >> 2 & 1, r >> 1 & 1, r & 1) for r in range(1, N_DEV)]


def _place():
    return lax.axis_index("x"), lax.axis_index("y"), lax.axis_index("c")


class _Ride:
    def __init__(self, srcs, shard):
        self.srcs, self.shard, self.n = list(srcs), shard, len(srcs)
        self.out_shape = [jax.ShapeDtypeStruct(((N_DEV,) + s.shape) if shard else s.shape, s.dtype)
                          for s in self.srcs]
        self.specs = [_ANY] * self.n
        self.scratch = [pltpu.SemaphoreType.DMA((self.n, N_DEV - 1)), pltpu.SemaphoreType.DMA((self.n, N_DEV - 1)),
                        pltpu.SemaphoreType.DMA((self.n,))]

    def _copies(self, in_refs, out_refs, sems):
        send, recv, loc = sems
        x, y, c = _place()
        me = 4 * x + 2 * y + c
        local, remote, arrive = [], [], []
        for a in range(self.n):
            src = in_refs[a] if self.shard else in_refs[a].at[me]
            local.append(pltpu.make_async_copy(src, out_refs[a].at[me], loc.at[a]))
        for k, (rx, ry, rc) in enumerate(_REL):
            peer = (lax.rem(x + rx, 2), lax.rem(y + ry, 2), lax.rem(c + rc, 2))
            ps = 4 * peer[0] + 2 * peer[1] + peer[2]
            for a in range(self.n):
                src = in_refs[a] if self.shard else in_refs[a].at[ps]
                remote.append(pltpu.make_async_remote_copy(
                    src_ref=src, dst_ref=out_refs[a].at[me], send_sem=send.at[a, k], recv_sem=recv.at[a, k],
                    device_id=peer, device_id_type=_MESH))
                slot = out_refs[a].at[ps]
                arrive.append(pltpu.make_async_remote_copy(
                    src_ref=slot, dst_ref=slot, send_sem=send.at[a, k], recv_sem=recv.at[a, k],
                    device_id=peer, device_id_type=_MESH))
        return local, remote, arrive

    def start(self, in_refs, out_refs, sems):
        local, remote, _ = self._copies(in_refs, out_refs, sems)
        for cp in local + remote:
            cp.start()

    def wait(self, in_refs, out_refs, sems):
        local, remote, arrive = self._copies(in_refs, out_refs, sems)
        for cp in arrive:
            cp.wait_recv()
        for cp in remote:
            cp.wait_send()
        for cp in local:
            cp.wait()


_EPI = {
    "none": ((), ("tile",)),
    "res": (("tile",), ("tile",)),
    "mul2": (("tile",), ("tile",)),
    "relu2": ((), ("tile", "tile")),
    "res_norm": (("tile", "row"), ("tile", "tile")),
    "norm_bwd": (("tile", "tile", "row"), ("tile", "row")),
    "res_loss": (("tile", "tile", "row"), ("tile", "row", "row")),
}


def _mm(a, b, *, dims="nn", epi="none", extra=(), out_dtype=F32, name, bm=1024, bn_cap=1024, bk_cap=2048,
        ride=None, b_cols=None, b_resident=False):
    if dims == "nn":
        (M, K), (K2, N) = a.shape, b.shape
    elif dims == "nt":
        (M, K), (N, K2) = a.shape, b.shape
    else:
        (K, M), (K2, N) = a.shape, b.shape
    jb0 = 0
    if b_cols is not None:
        N = b_cols[1]
    assert K == K2, (a.shape, b.shape, dims)
    bm = _pick(M, bm)
    bn = _pick(N, bn_cap)
    bk = _pick(K, bk_cap)
    nk = K // bk
    if b_cols is not None:
        assert dims == "nn" and b_cols[0] % bn == 0
        jb0 = b_cols[0] // bn
    dn = {"nn": _NN, "nt": _NT, "tn": _TN}[dims]
    a_spec = (pl.BlockSpec((bk, bm), lambda i, j, k: (k, i)) if dims == "tn"
              else pl.BlockSpec((bm, bk), lambda i, j, k: (i, k)))
    if b_resident:
        b_spec = pl.BlockSpec(b.shape, lambda i, j, k: (0, 0), pipeline_mode=pl.Buffered(1))
    else:
        b_spec = (pl.BlockSpec((bn, bk), lambda i, j, k: (j, k)) if dims == "nt"
                  else pl.BlockSpec((bk, bn), lambda i, j, k: (k, j + jb0)))
    o_spec = pl.BlockSpec((bm, bn), lambda i, j, k: (i, j))
    r_spec = pl.BlockSpec((1, bn), lambda i, j, k: (0, j))
    extra = list(extra) if isinstance(extra, (tuple, list)) else [extra]
    ekinds, okinds = _EPI[epi]
    assert len(extra) == len(ekinds) and (epi not in ("res_norm", "norm_bwd", "res_loss") or bn == N)
    n_extra, n_out = len(ekinds), len(okinds)
    n_ride = ride.n if ride else 0
    gi, gj = M // bm, N // bn

    def body(a_ref, b_ref, *rest):
        ex = rest[:n_extra]
        first = pl.program_id(0) == 0
        ride_in = rest[n_extra:n_extra + n_ride]
        outs = rest[n_extra + n_ride:n_extra + n_ride + n_out]
        ride_out = rest[n_extra + n_ride + n_out:n_extra + 2 * n_ride + n_out]
        if ride:
            at = lambda i, j, k: ((pl.program_id(0) == i) & (pl.program_id(1) == j) & (pl.program_id(2) == k))

            @pl.when(at(0, 0, 0))
            def _():
                ride.start(ride_in, ride_out, rest[-3:])

        def finish(r):
            if epi == "res":
                outs[0][...] = (r + ex[0][...].astype(F32)).astype(outs[0].dtype)
            elif epi == "mul2":
                outs[0][...] = (2.0 * r * ex[0][...].astype(F32)).astype(outs[0].dtype)
            elif epi == "relu2":
                u = jnp.maximum(r, 0.0)
                outs[0][...] = u.astype(outs[0].dtype)
                outs[1][...] = (u * u).astype(outs[1].dtype)
            elif epi == "res_norm":
                y = r + ex[0][...]
                outs[0][...] = y
                rstd = lax.rsqrt(jnp.mean(y * y, axis=1, keepdims=True) + EPS)
                outs[1][...] = (y * rstd * ex[1][...]).astype(outs[1].dtype)
            elif epi == "norm_bwd":
                xv = ex[0][...]
                rstd = lax.rsqrt(jnp.mean(xv * xv, axis=1, keepdims=True) + EPS)
                xh = xv * rstd
                dxh = r * ex[2][...]
                outs[0][...] = ex[1][...] + rstd * (dxh - xh * jnp.mean(dxh * xh, axis=1, keepdims=True))
                dw = jnp.sum(r * xh, axis=0, keepdims=True)

                @pl.when(first)
                def _():
                    outs[1][...] = dw

                @pl.when(jnp.logical_not(first))
                def _():
                    outs[1][...] += dw
            elif epi == "res_loss":
                y = r + ex[0][...]
                wv = ex[2][...]
                rstd = lax.rsqrt(jnp.mean(y * y, axis=1, keepdims=True) + EPS)
                yh = y * rstd
                err = yh * wv - ex[1][...]
                part_loss = 0.5 * jnp.sum(jnp.mean(err * err, axis=1, keepdims=True), axis=0, keepdims=True)
                dyn = err * (1.0 / N)
                dyh = dyn * wv
                outs[0][...] = rstd * (dyh - yh * jnp.mean(dyh * yh, axis=1, keepdims=True))
                dw = jnp.sum(dyn * yh, axis=0, keepdims=True)
                lrow = jnp.broadcast_to(part_loss, (1, N))

                @pl.when(first)
                def _():
                    outs[1][...] = dw
                    outs[2][...] = lrow

                @pl.when(jnp.logical_not(first))
                def _():
                    outs[1][...] += dw
                    outs[2][...] += lrow
            else:
                outs[0][...] = r.astype(outs[0].dtype)

        if b_resident:
            jo = pl.multiple_of((pl.program_id(1) + jb0) * bn, bn)
            ko = pl.multiple_of(pl.program_id(2) * bk, bk)
            b_blk = b_ref[pl.ds(jo, bn), pl.ds(ko, bk)] if dims == "nt" else b_ref[pl.ds(ko, bk), pl.ds(jo, bn)]
        else:
            b_blk = b_ref[...]
        part = _dot(a_ref[...], b_blk, dn)
        if nk == 1:
            finish(part)
        else:
            acc = rest[n_extra + 2 * n_ride + n_out]
            k = pl.program_id(2)

            @pl.when(k == 0)
            def _():
                acc[...] = part

            @pl.when((k > 0) & (k < nk - 1))
            def _():
                acc[...] += part

            @pl.when(k == nk - 1)
            def _():
                finish(acc[...] + part)

        if ride:
            @pl.when(at(gi - 1, gj - 1, nk - 1))
            def _():
                ride.wait(ride_in, ride_out, rest[-3:])

    kind_spec = {"tile": o_spec, "row": r_spec}
    ins = [a, b] + [e.reshape(1, N) if k == "row" else e for e, k in zip(extra, ekinds)]
    in_specs = [a_spec, b_spec] + [kind_spec[k] for k in ekinds]
    out_dtypes = {"res_norm": (F32, BF16), "norm_bwd": (F32, F32), "res_loss": (F32, F32, F32)}.get(
        epi, (out_dtype,) * n_out)
    out_shape = [jax.ShapeDtypeStruct((M, N) if k == "tile" else (1, N), dt) for k, dt in zip(okinds, out_dtypes)]
    out_specs = [kind_spec[k] for k in okinds]
    scratch = [pltpu.VMEM((bm, bn), F32)] if nk > 1 else []
    sem = ("arbitrary" if epi in ("norm_bwd", "res_loss") else "parallel", "parallel", "arbitrary")
    if ride:
        ins, in_specs = ins + ride.srcs, in_specs + ride.specs
        out_shape, out_specs = out_shape + ride.out_shape, out_specs + ride.specs
        scratch, sem = scratch + ride.scratch, ("arbitrary",) * 3
    res = pl.pallas_call(
        body, grid=(gi, gj, nk), in_specs=in_specs, out_specs=out_specs, out_shape=out_shape,
        scratch_shapes=scratch, name=name, compiler_params=_params(sem))(*ins)
    main = res[:n_out] if n_out > 1 else res[0]
    return (main, list(res[n_out:])) if ride else main


def _rmsnorm_fwd(x, w, *, name, bt=256):
    T, Dm = x.shape
    bt = min(bt, T)

    def body(x_ref, w_ref, h_ref):
        xv = x_ref[...]
        r = lax.rsqrt(jnp.mean(xv * xv, axis=1, keepdims=True) + EPS)
        h_ref[...] = (xv * r * w_ref[...]).astype(h_ref.dtype)

    return pl.pallas_call(
        body, grid=(T // bt,),
        in_specs=[pl.BlockSpec((bt, Dm), lambda i: (i, 0)), pl.BlockSpec((1, Dm), lambda i: (0, 0))],
        out_specs=pl.BlockSpec((bt, Dm), lambda i: (i, 0)),
        out_shape=jax.ShapeDtypeStruct((T, Dm), BF16), name=name,
        compiler_params=_params(("parallel",)))(x, w.reshape(1, Dm))


def _rmsnorm_bwd(x, w, dh, dres, *, name, bt=256):
    T, Dm = x.shape
    bt = min(bt, T)
    has_res = dres is not None

    def body(x_ref, w_ref, dh_ref, *rest):
        dres_ref = rest[0] if has_res else None
        dx_ref, dw_ref = rest[-2], rest[-1]
        i = pl.program_id(0)
        xv = x_ref[...]
        r = lax.rsqrt(jnp.mean(xv * xv, axis=1, keepdims=True) + EPS)
        xh = xv * r
        dhv = dh_ref[...].astype(F32)
        dxh = dhv * w_ref[...]
        dx = r * (dxh - xh * jnp.mean(dxh * xh, axis=1, keepdims=True))
        if has_res:
            dx = dx + dres_ref[...]
        dx_ref[...] = dx

        @pl.when(i == 0)
        def _():
            dw_ref[...] = jnp.zeros_like(dw_ref)

        dw_ref[...] += jnp.sum(dhv * xh, axis=0, keepdims=True)

    row = pl.BlockSpec((bt, Dm), lambda i: (i, 0))
    vec = pl.BlockSpec((1, Dm), lambda i: (0, 0))
    ins = [x, w.reshape(1, Dm), dh] + ([dres] if has_res else [])
    dx, dw = pl.pallas_call(
        body, grid=(T // bt,), in_specs=[row, vec, row] + ([row] if has_res else []),
        out_specs=[row, vec],
        out_shape=[jax.ShapeDtypeStruct((T, Dm), F32), jax.ShapeDtypeStruct((1, Dm), F32)],
        name=name, compiler_params=_params(("arbitrary",)))(*ins)
    return dx, dw.reshape(Dm)


def _attn_fwd(q, km, vm, *, bt=256):
    T = q.shape[0]
    M = km.shape[0]
    bt = min(bt, T)
    scale = MEM_HD ** -0.5

    def body(q_ref, k_ref, v_ref, o_ref):
        sls = [slice(h * MEM_HD, (h + 1) * MEM_HD) for h in range(MEM_H)]
        ss = [_dot(q_ref[:, sl], k_ref[:, sl], _NT) * scale for sl in sls]
        es = [jnp.exp(s - jnp.max(s, axis=1, keepdims=True)) for s in ss]
        ps = [e / jnp.sum(e, axis=1, keepdims=True) for e in es]
        for sl, p in zip(sls, ps):
            o_ref[:, sl] = _dot(p, v_ref[:, sl]).astype(o_ref.dtype)

    row = pl.BlockSpec((bt, D), lambda i: (i, 0))
    mem = pl.BlockSpec((M, D), lambda i: (0, 0))
    return pl.pallas_call(
        body, grid=(T // bt,), in_specs=[row, mem, mem], out_specs=row,
        out_shape=jax.ShapeDtypeStruct((T, D), BF16), name="attn_fwd",
        compiler_params=_params(("parallel",)))(q, km, vm)


def _attn_bwd(q, km, vm, do, *, bt=256):
    T = q.shape[0]
    M = km.shape[0]
    bt = min(bt, T)
    scale = MEM_HD ** -0.5

    def body(q_ref, k_ref, v_ref, do_ref, dq_ref, dk_ref, dv_ref):
        i = pl.program_id(0)

        @pl.when(i == 0)
        def _():
            dk_ref[...] = jnp.zeros_like(dk_ref)
            dv_ref[...] = jnp.zeros_like(dv_ref)

        sls = [slice(h * MEM_HD, (h + 1) * MEM_HD) for h in range(MEM_H)]
        ss = [_dot(q_ref[:, sl], k_ref[:, sl], _NT) * scale for sl in sls]
        dps = [_dot(do_ref[:, sl], v_ref[:, sl], _NT) for sl in sls]
        es = [jnp.exp(s - jnp.max(s, axis=1, keepdims=True)) for s in ss]
        ps = [e / jnp.sum(e, axis=1, keepdims=True) for e in es]
        dss = [p * (dp - jnp.sum(dp * p, axis=1, keepdims=True)) * scale for p, dp in zip(ps, dps)]
        for sl, p, ds in zip(sls, ps, dss):
            dq_ref[:, sl] = _dot(ds, k_ref[:, sl]).astype(dq_ref.dtype)
            dk_ref[:, sl] += _dot(ds, q_ref[:, sl], _TN)
            dv_ref[:, sl] += _dot(p, do_ref[:, sl], _TN)

    row = pl.BlockSpec((bt, D), lambda i: (i, 0))
    mem = pl.BlockSpec((M, D), lambda i: (0, 0))
    return pl.pallas_call(
        body, grid=(T // bt,), in_specs=[row, mem, mem, row], out_specs=[row, mem, mem],
        out_shape=[jax.ShapeDtypeStruct((T, D), BF16), jax.ShapeDtypeStruct((M, D), F32),
                   jax.ShapeDtypeStruct((M, D), F32)],
        name="attn_bwd", compiler_params=_params(("arbitrary",)))(q, km, vm, do)


def _conv_apply(halo, x, w_ref, b_ref):
    bt, hr = x.shape[0], halo.shape[0]
    cat = jnp.concatenate([halo, x], axis=0)
    y = x * w_ref[3:4, :]
    for k in range(CONV_K - 1):
        y = y + pltpu.roll(cat, CONV_K - 1 - k, 0)[hr:hr + bt] * w_ref[k:k + 1, :]
    if b_ref is not None:
        y = y + b_ref[...]
    return y


def _l2_parts(act, bc):
    out = []
    for s in range(bc // 128):
        a = act[:, s * 128:(s + 1) * 128]
        r = lax.rsqrt(jnp.sum(a * a, axis=1, keepdims=True) + EPS)
        out.append((a, r))
    return out


def _conv_fwd(p, col0, C, w, b, *, l2, name, bt=512, bc=1024):
    T = p.shape[0]
    bt = min(bt, T)
    c0, hb = col0 // bc, bt // P_HALO
    has_b = b is not None
    assert not l2 or (bc == D and C == 2 * D)

    def body(x_ref, halo_ref, w_ref, *rest):
        b_ref = rest[0] if has_b else None
        o_ref = rest[-1]
        i, j = pl.program_id(0), pl.program_id(1)
        x = x_ref[...].astype(F32)
        halo = jnp.where(i > 0, halo_ref[...].astype(F32), 0.0)
        act = _silu(_conv_apply(halo, x, w_ref, b_ref))
        if l2:
            sc = jnp.where(j == 0, GDN_DK ** -0.5, 1.0)
            o_ref[...] = jnp.concatenate([a * (r * sc) for a, r in _l2_parts(act, bc)], axis=1)
        else:
            o_ref[...] = act

    in_specs = [pl.BlockSpec((bt, bc), lambda i, j: (i, c0 + j)),
                pl.BlockSpec((P_HALO, bc), lambda i, j: (jnp.maximum(i * hb - 1, 0), c0 + j)),
                pl.BlockSpec((CONV_K, bc), lambda i, j: (0, j))]
    ins = [p, p, w]
    if has_b:
        in_specs.append(pl.BlockSpec((1, bc), lambda i, j: (0, j)))
        ins.append(b.reshape(1, C))
    return pl.pallas_call(
        body, grid=(T // bt, C // bc), in_specs=in_specs,
        out_specs=pl.BlockSpec((bt, bc), lambda i, j: (i, j)),
        out_shape=jax.ShapeDtypeStruct((T, C), F32), name=name,
        compiler_params=_params(("parallel", "parallel")))(*ins)


def _conv_bwd_act(p, col0, C, w, b, dact, dcol0, *, l2, name, bt=512, bc=1024):
    T = p.shape[0]
    bt = min(bt, T)
    c0, d0, hb = col0 // bc, dcol0 // bc, bt // P_HALO
    has_b = b is not None
    assert not l2 or (bc == D and C == 2 * D)

    def body(x_ref, halo_ref, w_ref, *rest):
        b_ref = rest[0] if has_b else None
        dact_ref, dy_ref, dw_ref, db_ref = rest[-4:]
        j, i = pl.program_id(0), pl.program_id(1)
        x = x_ref[...].astype(F32)
        halo = jnp.where(i > 0, halo_ref[...].astype(F32), 0.0)
        y = _conv_apply(halo, x, w_ref, b_ref)
        dact = dact_ref[...]
        sg = _sigmoid(y)
        if l2:
            sc = jnp.where(j == 0, GDN_DK ** -0.5, 1.0)
            parts = []
            for s, (a, r) in enumerate(_l2_parts(y * sg, bc)):
                n = a * r
                dn = dact[:, s * 128:(s + 1) * 128]
                parts.append((r * sc) * (dn - n * jnp.sum(dn * n, axis=1, keepdims=True)))
            dact = jnp.concatenate(parts, axis=1)
        dy = dact * (sg * (1.0 + y * (1.0 - sg)))
        dy_ref[...] = dy

        @pl.when(i == 0)
        def _():
            dw_ref[...] = jnp.zeros_like(dw_ref)
            db_ref[...] = jnp.zeros_like(db_ref)

        db_ref[...] += jnp.sum(dy, axis=0, keepdims=True)
        cat = jnp.concatenate([halo, x], axis=0)
        dw_ref[3:4, :] += jnp.sum(dy * x, axis=0, keepdims=True)
        for k in range(CONV_K - 1):
            xs = pltpu.roll(cat, CONV_K - 1 - k, 0)[P_HALO:P_HALO + bt]
            dw_ref[k:k + 1, :] += jnp.sum(dy * xs, axis=0, keepdims=True)

    in_specs = [pl.BlockSpec((bt, bc), lambda j, i: (i, c0 + j)),
                pl.BlockSpec((P_HALO, bc), lambda j, i: (jnp.maximum(i * hb - 1, 0), c0 + j)),
                pl.BlockSpec((CONV_K, bc), lambda j, i: (0, j))]
    ins = [p, p, w]
    if has_b:
        in_specs.append(pl.BlockSpec((1, bc), lambda j, i: (0, j)))
        ins.append(b.reshape(1, C))
    in_specs.append(pl.BlockSpec((bt, bc), lambda j, i: (i, d0 + j)))
    ins.append(dact)
    dy, dw, db = pl.pallas_call(
        body, grid=(C // bc, T // bt), in_specs=in_specs,
        out_specs=[pl.BlockSpec((bt, bc), lambda j, i: (i, j)),
                   pl.BlockSpec((CONV_K, bc), lambda j, i: (0, j)),
                   pl.BlockSpec((1, bc), lambda j, i: (0, j))],
        out_shape=[jax.ShapeDtypeStruct((T, C), F32), jax.ShapeDtypeStruct((CONV_K, C), F32),
                   jax.ShapeDtypeStruct((1, C), F32)],
        name=name, compiler_params=_params(("parallel", "arbitrary")))(*ins)
    return dy, dw, db.reshape(C)


def _conv_bwd_in(dy, w, dp_in, col0, T, *, name, bt=512, bc=1024):
    C = dy.shape[1]
    bt = min(bt, T)
    c0, hb, nb = col0 // bc, bt // 8, T // bt

    def body(dy_ref, nxt_ref, w_ref, *rest):
        o_ref = rest[-1]
        i = pl.program_id(0)
        dy_v = dy_ref[...]
        nxt = jnp.where(i < nb - 1, nxt_ref[...], 0.0)
        cat = jnp.concatenate([dy_v, nxt], axis=0)
        dx = dy_v * w_ref[3:4, :]
        for k in range(CONV_K - 1):
            s = CONV_K - 1 - k
            dx = dx + pltpu.roll(cat, bt + 8 - s, 0)[0:bt] * w_ref[k:k + 1, :]
        o_ref[...] = dx.astype(o_ref.dtype)

    in_specs = [pl.BlockSpec((bt, bc), lambda i, j: (i, j)),
                pl.BlockSpec((8, bc), lambda i, j: (jnp.minimum((i + 1) * hb, T // 8 - 1), j)),
                pl.BlockSpec((CONV_K, bc), lambda i, j: (0, j))]
    ins = [dy, dy, w]
    alias = {}
    if dp_in is not None:
        in_specs.append(pl.BlockSpec(memory_space=pl.ANY))
        ins.append(dp_in)
        alias = {3: 0}
    return pl.pallas_call(
        body, grid=(nb, C // bc), in_specs=in_specs,
        out_specs=pl.BlockSpec((bt, bc), lambda i, j: (i, c0 + j)),
        out_shape=jax.ShapeDtypeStruct((T, C_TOT), BF16), input_output_aliases=alias, name=name,
        compiler_params=_params(("parallel", "parallel")))(*ins)


def _expand_mats(shift, row0):
    e = (_iota2((128, D), 0) - row0 == (_iota2((128, D), 1) >> shift)).astype(F32)
    et = ((_iota2((D, 128), 0) >> shift) == _iota2((D, 128), 1) - row0).astype(F32)
    return e, et


def _cum_mats(chunk):
    ri, ci = _iota2((chunk, chunk), 0), _iota2((chunk, chunk), 1)
    return (ri >= ci).astype(F32), (ri <= ci).astype(F32)


def _gdn_gates_fwd(p, alog_row, dtb_row, *, bt=256):
    T = p.shape[0]
    bt = min(bt, T)

    def body(g_ref, al_ref, db_ref, bg_ref):
        gt = g_ref[...]
        lc, _ = _cum_mats(GDN_C)
        g_l = -jnp.exp(al_ref[...]) * _softplus(gt + db_ref[...])
        bg_ref[...] = jnp.where(_iota2((bt, 128), 1) < GDN_H, _sigmoid(gt), _chunk_cumsum(g_l, lc, GDN_C))

    vec = pl.BlockSpec((1, 128), lambda i: (0, 0))
    seg = pl.BlockSpec((bt, 128), lambda i: (i, 0))
    return pl.pallas_call(
        body, grid=(T // bt,), in_specs=[seg, vec, vec], out_specs=seg,
        out_shape=jax.ShapeDtypeStruct((T, 128), F32), name="gdn_gates_fwd",
        compiler_params=_params(("parallel",)))(p, alog_row, dtb_row)


def _gdn_gates_bwd(p, alog_row, dtb_row, dbg, dp_in, *, bt=256):
    T = p.shape[0]
    bt = min(bt, T)

    def body(g_ref, al_ref, db_ref, dbg_ref, dpin_ref, dg_out, dal_ref, ddb_ref):
        i = pl.program_id(0)
        gt = g_ref[...]
        lane = _iota2((bt, 128), 1)
        _, uc = _cum_mats(GDN_C)
        ea = jnp.exp(al_ref[...])
        zz = gt + db_ref[...]
        g_l = -ea * _softplus(zz)
        beta_l = _sigmoid(gt)
        dbg_v = dbg_ref[...]
        dg_l = jnp.where((lane >= GDN_H) & (lane < 2 * GDN_H), _chunk_cumsum(dbg_v, uc, GDN_C), 0.0)
        dbeta_l = jnp.where(lane < GDN_H, dbg_v, 0.0)
        da = dg_l * (-ea) * _sigmoid(zz)
        dg_out[...] = (da + dbeta_l * beta_l * (1.0 - beta_l)).astype(dg_out.dtype)

        @pl.when(i == 0)
        def _():
            dal_ref[...] = jnp.zeros_like(dal_ref)
            ddb_ref[...] = jnp.zeros_like(ddb_ref)

        dal_ref[...] += jnp.sum(dg_l * g_l, axis=0, keepdims=True)
        ddb_ref[...] += jnp.sum(da, axis=0, keepdims=True)

    vec = pl.BlockSpec((1, 128), lambda i: (0, 0))
    seg = pl.BlockSpec((bt, 128), lambda i: (i, 0))
    gate = pl.BlockSpec((bt, 128), lambda i: (i, C_GATE // 128))
    return pl.pallas_call(
        body, grid=(T // bt,), in_specs=[seg, vec, vec, seg, _ANY], out_specs=[gate, vec, vec],
        out_shape=[jax.ShapeDtypeStruct((T, C_TOT), BF16), jax.ShapeDtypeStruct((1, 128), F32),
                   jax.ShapeDtypeStruct((1, 128), F32)],
        input_output_aliases={4: 0}, name="gdn_gates_bwd",
        compiler_params=_params(("arbitrary",)))(p, alog_row, dtb_row, dbg, dp_in)


def _ssd_dt_fwd(p, dtb_row, alog_row, *, bt=256):
    T = p.shape[0]
    bt = min(bt, T)

    def body(d_ref, db_ref, al_ref, da_ref):
        lc, _ = _cum_mats(SSM_L)
        dt_l = _softplus(d_ref[...] + db_ref[...])
        alpha_l = _chunk_cumsum(dt_l * (-jnp.exp(al_ref[...])), lc, SSM_L)
        da_ref[...] = jnp.where(_iota2((bt, 128), 1) < SSM_H, dt_l, pltpu.roll(alpha_l, SSM_H, 1))

    v128 = pl.BlockSpec((1, 128), lambda i: (0, 0))
    return pl.pallas_call(
        body, grid=(T // bt,), in_specs=[pl.BlockSpec((bt, 128), lambda i: (i, 1)), v128, v128],
        out_specs=pl.BlockSpec((bt, 128), lambda i: (i, 0)), out_shape=jax.ShapeDtypeStruct((T, 128), F32),
        name="ssd_dt_fwd", compiler_params=_params(("parallel",)))(p, dtb_row, alog_row)


def _ssd_dt_bwd(p, dtb_row, alog_row, dda, dp_in, *, bt=256):
    T = p.shape[0]
    bt = min(bt, T)

    def body(d_ref, db_ref, al_ref, dda_ref, dpin_ref, dd_out, ddb_ref, dalog_ref):
        i = pl.program_id(0)
        heads = _iota2((bt, 128), 1) < SSM_H
        _, uc = _cum_mats(SSM_L)
        zz = d_ref[...] + db_ref[...]
        dt_l = _softplus(zz)
        a_row = -jnp.exp(al_ref[...])
        dda_v = dda_ref[...]
        da_l = _chunk_cumsum(jnp.where(heads, pltpu.roll(dda_v, 128 - SSM_H, 1), 0.0), uc, SSM_L)
        draw = jnp.where(heads, (dda_v + da_l * a_row) * _sigmoid(zz), 0.0)
        dd_out[...] = draw.astype(dd_out.dtype)

        @pl.when(i == 0)
        def _():
            ddb_ref[...] = jnp.zeros_like(ddb_ref)
            dalog_ref[...] = jnp.zeros_like(dalog_ref)

        ddb_ref[...] += jnp.sum(draw, axis=0, keepdims=True)
        dalog_ref[...] += jnp.sum(da_l * dt_l, axis=0, keepdims=True) * a_row

    seg = pl.BlockSpec((bt, 128), lambda i: (i, C_DT // 128))
    v128 = pl.BlockSpec((1, 128), lambda i: (0, 0))
    return pl.pallas_call(
        body, grid=(T // bt,),
        in_specs=[pl.BlockSpec((bt, 128), lambda i: (i, 1)), v128, v128, pl.BlockSpec((bt, 128), lambda i: (i, 0)), _ANY],
        out_specs=[seg, v128, v128],
        out_shape=[jax.ShapeDtypeStruct((T, C_TOT), BF16), jax.ShapeDtypeStruct((1, 128), F32),
                   jax.ShapeDtypeStruct((1, 128), F32)],
        input_output_aliases={4: 0}, name="ssd_dt_bwd",
        compiler_params=_params(("arbitrary",)))(p, dtb_row, alog_row, dda, dp_in)


def _gdn_post_fwd(o, p, w_x, *, bt=256):
    T = o.shape[0]
    bt = min(bt, T)

    def body(o_ref, z_ref, w_ref, out_ref):
        for h in range(GDN_H):
            sl = slice(h * 128, (h + 1) * 128)
            oh = o_ref[:, sl].astype(F32)
            r = lax.rsqrt(jnp.mean(oh * oh, axis=1, keepdims=True) + EPS)
            out_ref[:, sl] = (oh * r * w_ref[:, sl] * _silu(z_ref[:, sl].astype(F32))).astype(out_ref.dtype)

    row = pl.BlockSpec((bt, D), lambda i: (i, 0))
    return pl.pallas_call(
        body, grid=(T // bt,),
        in_specs=[row, pl.BlockSpec((bt, D), lambda i: (i, C_ZG // D)), pl.BlockSpec((1, D), lambda i: (0, 0))],
        out_specs=row, out_shape=jax.ShapeDtypeStruct((T, 2 * D), BF16), name="gdn_post_fwd",
        compiler_params=_params(("parallel",)))(o, p, w_x)


def _gdn_post_bwd(dx1, w_out, o, p, w_x, *, bt=512):
    T = o.shape[0]
    bt = min(bt, T)

    def body(dx_ref, wo_ref, o_ref, z_ref, w_ref, do_ref, dz_ref, dw_ref):
        i = pl.program_id(0)

        @pl.when(i == 0)
        def _():
            dw_ref[...] = jnp.zeros_like(dw_ref)

        dmix = _dot(dx_ref[...], wo_ref[...], _NT)
        for h in range(GDN_H):
            sl = slice(h * 128, (h + 1) * 128)
            oh, zh, wh = o_ref[:, sl].astype(F32), z_ref[:, sl].astype(F32), w_ref[:, sl]
            dm = dmix[:, sl]
            r = lax.rsqrt(jnp.mean(oh * oh, axis=1, keepdims=True) + EPS)
            ohat = oh * r
            dy = dm * _silu(zh)
            dz_ref[:, sl] = (dm * ohat * wh * _dsilu(zh)).astype(dz_ref.dtype)
            dohat = dy * wh
            do_ref[:, sl] = (r * (dohat - ohat * jnp.mean(dohat * ohat, axis=1, keepdims=True))).astype(do_ref.dtype)
            dw_ref[:, sl] += jnp.sum(dy * ohat, axis=0, keepdims=True)

    row = pl.BlockSpec((bt, D), lambda i: (i, 0))
    zcol = pl.BlockSpec((bt, D), lambda i: (i, C_ZG // D))
    vec = pl.BlockSpec((1, D), lambda i: (0, 0))
    return pl.pallas_call(
        body, grid=(T // bt,), in_specs=[row, pl.BlockSpec((D, D), lambda i: (0, 0)), row, zcol, vec],
        out_specs=[row, zcol, vec],
        out_shape=[jax.ShapeDtypeStruct((T, D), BF16), jax.ShapeDtypeStruct((T, C_TOT), BF16),
                   jax.ShapeDtypeStruct((1, D), F32)],
        name="gdn_post_bwd", compiler_params=_params(("arbitrary",)))(dx1, w_out, o, p, w_x)


def _ssd_post_fwd(y, xs, p, d_x, w, mix_in, *, bt=256):
    T = y.shape[0]
    bt = min(bt, T)

    def body(y_ref, x_ref, z_ref, d_ref, w_ref, mix_ref, out_ref):
        yg = (y_ref[...].astype(F32) + x_ref[...] * d_ref[...]) * _silu(z_ref[...].astype(F32))
        for g in range(2):
            sl = slice(g * 512, (g + 1) * 512)
            a = yg[:, sl]
            r = lax.rsqrt(jnp.mean(a * a, axis=1, keepdims=True) + EPS)
            out_ref[:, sl] = (a * r * w_ref[:, sl]).astype(out_ref.dtype)

    row = pl.BlockSpec((bt, D), lambda i: (i, 0))
    vec = pl.BlockSpec((1, D), lambda i: (0, 0))
    return pl.pallas_call(
        body, grid=(T // bt,),
        in_specs=[row, row, pl.BlockSpec((bt, D), lambda i: (i, C_ZS // D)), vec, vec, _ANY],
        out_specs=pl.BlockSpec((bt, D), lambda i: (i, 1)), out_shape=jax.ShapeDtypeStruct((T, 2 * D), BF16),
        input_output_aliases={5: 0}, name="ssd_post_fwd",
        compiler_params=_params(("parallel",)))(y, xs, p, d_x, w, mix_in)


def _ssd_post_bwd(dx1, w_out, y, xs, p, d_x, w, dp_in, *, bt=512):
    T = y.shape[0]
    bt = min(bt, T)

    def body(dx_ref, wo_ref, y_ref, x_ref, z_ref, d_ref, w_ref, dpin_ref, dyy_ref, dz_ref, dd_ref, dw_ref):
        i = pl.program_id(0)

        @pl.when(i == 0)
        def _():
            dd_ref[...] = jnp.zeros_like(dd_ref)
            dw_ref[...] = jnp.zeros_like(dw_ref)

        dmix = _dot(dx_ref[...], wo_ref[...], _NT)
        xv, zv = x_ref[...], z_ref[...].astype(F32)
        yy = y_ref[...].astype(F32) + xv * d_ref[...]
        sz = _silu(zv)
        yg = yy * sz
        parts = []
        for g in range(2):
            sl = slice(g * 512, (g + 1) * 512)
            a = yg[:, sl]
            r = lax.rsqrt(jnp.mean(a * a, axis=1, keepdims=True) + EPS)
            ah = a * r
            dout = dmix[:, sl]
            dah = dout * w_ref[:, sl]
            dw_ref[:, sl] += jnp.sum(dout * ah, axis=0, keepdims=True)
            parts.append(r * (dah - ah * jnp.mean(dah * ah, axis=1, keepdims=True)))
        dyg = jnp.concatenate(parts, axis=1)
        dyy = dyg * sz
        dyy_ref[...] = dyy
        dz_ref[...] = (dyg * yy * _dsilu(zv)).astype(dz_ref.dtype)
        dd_ref[...] += jnp.sum(dyy * xv, axis=0, keepdims=True)

    row = pl.BlockSpec((bt, D), lambda i: (i, 0))
    zcol = pl.BlockSpec((bt, D), lambda i: (i, C_ZS // D))
    vec = pl.BlockSpec((1, D), lambda i: (0, 0))
    return pl.pallas_call(
        body, grid=(T // bt,),
        in_specs=[row, pl.BlockSpec((D, D), lambda i: (1, 0)), row, row, zcol, vec, vec, _ANY],
        out_specs=[row, zcol, vec, vec],
        out_shape=[jax.ShapeDtypeStruct((T, D), F32), jax.ShapeDtypeStruct((T, C_TOT), BF16),
                   jax.ShapeDtypeStruct((1, D), F32), jax.ShapeDtypeStruct((1, D), F32)],
        input_output_aliases={7: 1}, name="ssd_post_bwd",
        compiler_params=_params(("arbitrary",)))(dx1, w_out, y, xs, p, d_x, w, dp_in)


_NEG = -1e30


def _gdn_terms(q, k, v, bx, gam_c):
    C = GDN_C
    ri, ci = _iota2((C, C), 0), _iota2((C, C), 1)
    eye, low, strict = ri == ci, ri >= ci, ri > ci
    gam_r = jnp.sum(jnp.where(eye, gam_c, 0.0), axis=0, keepdims=True)
    G = jnp.exp(jnp.where(low, gam_c - gam_r, _NEG))
    glast = jnp.sum(jnp.where(_iota2((C, 1), 0) == C - 1, gam_c, 0.0), axis=0, keepdims=True)
    eg, egl, eL = jnp.exp(gam_c), jnp.exp(glast - gam_c), jnp.exp(glast)
    kb, vb = k * bx, v * bx
    M = _dot(kb, k, _NT)
    return dict(eye=eye, low=low, strict=strict, G=G, eg=eg, egl=egl, eL=eL, kb=kb, vb=vb, M=M,
                kbg=kb * eg, qd=q * eg, kd=k * egl, q=q, k=k, v=v, bx=bx)


def _split(a):
    hi = a.astype(_MXU)
    return hi, (a - hi.astype(F32)).astype(_MXU)


def _dot3s(a, b):
    d = lambda p, q: lax.dot_general(p, q, _NN, preferred_element_type=F32)
    return d(a[0], b[0]) + d(a[0], b[1]) + d(a[1], b[0])


def _tri_inv_many(Ls, eye):
    eyef = jnp.where(eye, 1.0, 0.0)
    Ts = [eyef - L for L in Ls]
    Ps = [-L for L in Ls]
    for _ in range(5):
        sp = [_split(p) for p in Ps]
        Ps = [_dot3s(s, s) for s in sp]
        sp = [_split(p) for p in Ps]
        st = [_split(t) for t in Ts]
        Ts = [t + _dot3s(a, b) for t, a, b in zip(Ts, st, sp)]
    return Ts


def _lane_col(tile, idx):
    return jnp.sum(jnp.where(_iota2(tile.shape, 1) == idx, tile, 0.0), axis=1, keepdims=True)


def _gdn_heads(q_ref, k_ref, v_ref, bg_ref, heads):
    out = []
    bg = bg_ref[...]
    for h in heads:
        sl = slice(h * 128, (h + 1) * 128)
        out.append(_gdn_terms(q_ref[:, sl], k_ref[:, sl], v_ref[:, sl], _lane_col(bg, h), _lane_col(bg, GDN_H + h)))
    return out


def _gdn_prep(qk, v, bg, ride=None):
    T = qk.shape[0]
    N = T // GDN_C
    C, CS = GDN_C, GDN_LOCAL_CHUNKS
    NB = N // CS
    n_ride = ride.n if ride else 0

    def body(q_ref, k_ref, v_ref, bg_ref, *rest):
        ride_in = rest[:n_ride]
        u_ref, w_ref, qd_ref, kd_ref, p_ref, t_ref = rest[n_ride:n_ride + 6]
        ride_out = rest[n_ride + 6:2 * n_ride + 6]
        if ride:
            @pl.when(pl.program_id(0) == 0)
            def _():
                ride.start(ride_in, ride_out, rest[-3:])

            @pl.when(pl.program_id(0) == NB - 1)
            def _():
                ride.wait(ride_in, ride_out, rest[-3:])

        items = [(c, h) for c in range(CS) for h in range(GDN_H)]
        views = [[r.at[pl.ds(c * C, C)] for r in (q_ref, k_ref, v_ref, bg_ref)] for c in range(CS)]
        ts = [_gdn_heads(*views[c], [h])[0] for c, h in items]
        Ts = _tri_inv_many([jnp.where(t["strict"], t["M"] * t["G"], 0.0) for t in ts], ts[0]["eye"])
        for (c, h), t, Tm in zip(items, ts, Ts):
            tok = slice(c * C, (c + 1) * C)
            sl = slice(h * 128, (h + 1) * 128)
            rows = slice(h * C, (h + 1) * C)
            u_ref[tok, sl] = _dot(Tm, t["vb"])
            w_ref[tok, sl] = _dot(Tm, t["kbg"]).astype(w_ref.dtype)
            qd_ref[tok, sl] = t["qd"].astype(qd_ref.dtype)
            kd_ref[tok, sl] = t["kd"].astype(kd_ref.dtype)
            p_ref[c, rows, :] = _dot(t["q"], t["k"], _NT) * t["G"]
            t_ref[c, rows, :] = Tm

    blk = lambda c: pl.BlockSpec((CS * C, D), lambda n: (n, c))
    sq = pl.BlockSpec((CS, GDN_H * C, C), lambda n: (n, 0, 0))
    in_specs = [blk(0), blk(1), blk(0), pl.BlockSpec((CS * C, 128), lambda n: (n, 0))]
    out_specs = [blk(0), blk(0), blk(0), blk(0), sq, sq]
    out_shape = [jax.ShapeDtypeStruct((T, D), F32), jax.ShapeDtypeStruct((T, D), BF16),
                 jax.ShapeDtypeStruct((T, D), BF16), jax.ShapeDtypeStruct((T, D), BF16),
                 jax.ShapeDtypeStruct((N, GDN_H * C, C), F32), jax.ShapeDtypeStruct((N, GDN_H * C, C), F32)]
    ins = [qk, qk, v, bg]
    if ride:
        ins, in_specs = ins + ride.srcs, in_specs + ride.specs
        out_shape, out_specs = out_shape + ride.out_shape, out_specs + ride.specs
    res = pl.pallas_call(
        body, grid=(NB,), in_specs=in_specs, out_specs=out_specs, out_shape=out_shape,
        scratch_shapes=ride.scratch if ride else [], name="gdn_prep",
        compiler_params=_params(("arbitrary",) if ride else ("parallel",)))(*ins)
    return (list(res[:6]), list(res[6:])) if ride else list(res)


def _gdn_scan_fwd(u, w, qd, kd, pm, bg):
    T = u.shape[0]
    N = T // GDN_C
    C, CS = GDN_C, GDN_SCAN_CHUNKS

    def body(u_ref, w_ref, qd_ref, kd_ref, p_ref, bg_ref, o_ref, vn_ref, ss_ref, S_scr):
        n = pl.program_id(0)

        @pl.when(n == 0)
        def _():
            S_scr[...] = jnp.zeros_like(S_scr)

        sls = [slice(h * 128, (h + 1) * 128) for h in range(GDN_H)]
        for c in range(CS):
            rows = slice(c * C, (c + 1) * C)
            glast = bg_ref[(c + 1) * C - 1:(c + 1) * C, :]
            Ss = [S_scr[:, sl] for sl in sls]
            vns = [u_ref[rows, sl] - _dot(w_ref[rows, sl], S) for sl, S in zip(sls, Ss)]
            for h, (sl, S, vn) in enumerate(zip(sls, Ss, vns)):
                ss_ref[c, :, sl] = S.astype(ss_ref.dtype)
                vn_ref[rows, sl] = vn.astype(vn_ref.dtype)
                o_ref[rows, sl] = (_dot(qd_ref[rows, sl], S)
                                   + _dot(p_ref[c, h * C:(h + 1) * C, :], vn)).astype(o_ref.dtype)
                S_scr[:, sl] = S * jnp.exp(_lane_col(glast, GDN_H + h)) + _dot(kd_ref[rows, sl], vn, _TN)

    blk = pl.BlockSpec((CS * C, D), lambda n: (n, 0))
    return dict(
        body=body, steps=N // CS, ins=[u, w, qd, kd, pm, bg],
        in_specs=[blk, blk, blk, blk, pl.BlockSpec((CS, GDN_H * C, C), lambda n: (n, 0, 0)),
                  pl.BlockSpec((CS * C, 128), lambda n: (n, 0))],
        out_specs=[blk, blk, pl.BlockSpec((CS, GDN_DK, D), lambda n: (n, 0, 0))],
        out_shape=[jax.ShapeDtypeStruct((T, D), BF16), jax.ShapeDtypeStruct((T, D), BF16),
                   jax.ShapeDtypeStruct((N, GDN_DK, D), BF16)],
        scratch=[pltpu.VMEM((GDN_DK, D), F32)])


def _gdn_scan_bwd(w, qd, kd, pm, bg, do):
    T = w.shape[0]
    N = T // GDN_C
    C, CS = GDN_C, GDN_SCAN_CHUNKS
    NB = N // CS

    def body(w_ref, qd_ref, kd_ref, p_ref, bg_ref, do_ref, dvn_ref, ds_ref, dS_scr):
        n = pl.program_id(0)

        @pl.when(n == 0)
        def _():
            dS_scr[...] = jnp.zeros_like(dS_scr)

        sls = [slice(h * 128, (h + 1) * 128) for h in range(GDN_H)]
        for c in reversed(range(CS)):
            rows = slice(c * C, (c + 1) * C)
            glast = bg_ref[(c + 1) * C - 1:(c + 1) * C, :]
            dSs = [dS_scr[:, sl] for sl in sls]
            dvns = [_dot(p_ref[c, h * C:(h + 1) * C, :], do_ref[rows, sl], _TN) + _dot(kd_ref[rows, sl], dS2)
                    for h, (sl, dS2) in enumerate(zip(sls, dSs))]
            for h, (sl, dS2, dvn) in enumerate(zip(sls, dSs, dvns)):
                ds_ref[c, :, sl] = dS2.astype(ds_ref.dtype)
                dvn_ref[rows, sl] = dvn.astype(dvn_ref.dtype)
                dS_scr[:, sl] = (dS2 * jnp.exp(_lane_col(glast, GDN_H + h))
                                 + _dot(qd_ref[rows, sl], do_ref[rows, sl], _TN) - _dot(w_ref[rows, sl], dvn, _TN))

    blk = pl.BlockSpec((CS * C, D), lambda n: (NB - 1 - n, 0))
    return dict(
        body=body, steps=NB, ins=[w, qd, kd, pm, bg, do],
        in_specs=[blk, blk, blk, pl.BlockSpec((CS, GDN_H * C, C), lambda n: (NB - 1 - n, 0, 0)),
                  pl.BlockSpec((CS * C, 128), lambda n: (NB - 1 - n, 0)), blk],
        out_specs=[blk, pl.BlockSpec((CS, GDN_DK, D), lambda n: (NB - 1 - n, 0, 0))],
        out_shape=[jax.ShapeDtypeStruct((T, D), BF16), jax.ShapeDtypeStruct((N, GDN_DK, D), BF16)],
        scratch=[pltpu.VMEM((GDN_DK, D), F32)])


def _gdn_rest_bwd(qk, v, bg, s_save, t_save, vn, dvn, ds_save, do, ride=None):
    T = qk.shape[0]
    N = T // GDN_C
    C, CS = GDN_C, GDN_REST_CHUNKS
    NB = N // CS
    n_ride = ride.n if ride else 0

    def body(q_ref, k_ref, v_ref, bg_ref, ss_ref, ts_ref, vn_ref, dvn_ref, ds_ref, do_ref, *rest):
        ride_in = rest[:n_ride]
        dqkv_ref, dbg_ref = rest[n_ride:n_ride + 2]
        ride_out = rest[n_ride + 2:2 * n_ride + 2]
        if ride:
            @pl.when(pl.program_id(0) == 0)
            def _():
                ride.start(ride_in, ride_out, rest[-3:])

            @pl.when(pl.program_id(0) == NB - 1)
            def _():
                ride.wait(ride_in, ride_out, rest[-3:])

        items = [(c, h) for c in range(CS) for h in range(GDN_H)]
        toks = [slice(c * C, (c + 1) * C) for c, _ in items]
        sls = [slice(h * 128, (h + 1) * 128) for _, h in items]
        views = [[r.at[pl.ds(c * C, C)] for r in (q_ref, k_ref, v_ref, bg_ref)] for c in range(CS)]
        ts = [_gdn_heads(*views[c], [h])[0] for c, h in items]
        Ss = [ss_ref[c, :, sl] for (c, _), sl in zip(items, sls)]
        Tms = [ts_ref[c, h * C:(h + 1) * C, :] for c, h in items]
        dS2s = [ds_ref[c, :, sl] for (c, _), sl in zip(items, sls)]
        dos = [do_ref[tok, sl] for tok, sl in zip(toks, sls)]
        vns = [vn_ref[tok, sl] for tok, sl in zip(toks, sls)]
        dvns = [dvn_ref[tok, sl] for tok, sl in zip(toks, sls)]
        Qs = [_dot(t["q"], t["k"], _NT) for t in ts]
        dws = [-_dot(dvn, S, _NT) for dvn, S in zip(dvns, Ss)]
        dqds = [_dot(do, S, _NT) for do, S in zip(dos, Ss)]
        dPs = [jnp.where(t["low"], _dot(do, vn, _NT), 0.0) for t, do, vn in zip(ts, dos, vns)]
        dkds = [_dot(vn, dS2, _NT) for vn, dS2 in zip(vns, dS2s)]
        dTs = [_dot(dvn, t["vb"], _NT) + _dot(dw, t["kbg"], _NT) for t, dvn, dw in zip(ts, dvns, dws)]
        dvbs = [_dot(Tm, dvn, _TN) for Tm, dvn in zip(Tms, dvns)]
        dkbgs = [_dot(Tm, dw, _TN) for Tm, dw in zip(Tms, dws)]
        TdTs = [_dot(Tm, dT, _TN) for Tm, dT in zip(Tms, dTs)]
        dLs = [jnp.where(t["strict"], -_dot(TdT, Tm, _NT), 0.0) for t, TdT, Tm in zip(ts, TdTs, Tms)]
        dMs = [dL * t["G"] for t, dL in zip(ts, dLs)]
        dQs = [dP * t["G"] for t, dP in zip(ts, dPs)]
        dkbs = [_dot(dM, t["k"]) + dkbg * t["eg"] for t, dM, dkbg in zip(ts, dMs, dkbgs)]
        rs = lambda a: jnp.sum(a, axis=1, keepdims=True)
        lane = _iota2((C, 128), 1)
        last = _iota2((C, 1), 0) == C - 1
        dbg = [jnp.zeros((C, 128), F32) for _ in range(CS)]
        for i, (c, h) in enumerate(items):
            t, sl, tok = ts[i], sls[i], toks[i]
            E = (dLs[i] * t["M"] + dPs[i] * Qs[i]) * t["G"]
            dqkv_ref[tok, sl] = _dot(dQs[i], t["k"]) + dqds[i] * t["eg"]
            dqkv_ref[tok, D + h * 128:D + (h + 1) * 128] = (
                _dot(dQs[i], t["q"], _TN) + _dot(dMs[i], t["kb"], _TN) + dkds[i] * t["egl"] + dkbs[i] * t["bx"])
            dqkv_ref[tok, 2 * D + h * 128:2 * D + (h + 1) * 128] = dvbs[i] * t["bx"]
            dbeta_c = rs(dkbs[i] * t["k"] + dvbs[i] * t["v"])
            dkd_kd = dkds[i] * t["kd"]
            dgam_c = rs(dqds[i] * t["qd"]) + rs(dkbgs[i] * t["kbg"]) - rs(dkd_kd) + rs(E)
            dgam_r = -jnp.sum(E, axis=0, keepdims=True)
            dgam_c = dgam_c + jnp.sum(jnp.where(t["eye"], dgam_r, 0.0), axis=1, keepdims=True)
            dlast = _sum_all(dkd_kd) + t["eL"] * _sum_all(Ss[i].astype(F32) * dS2s[i].astype(F32))
            dgam_c = dgam_c + jnp.where(last, dlast, 0.0)
            dbg[c] = dbg[c] + jnp.where(lane == h, dbeta_c, 0.0) + jnp.where(lane == GDN_H + h, dgam_c, 0.0)
        for c in range(CS):
            dbg_ref[c * C:(c + 1) * C, :] = dbg[c]

    blk = lambda c: pl.BlockSpec((CS * C, D), lambda n: (n, c))
    st = pl.BlockSpec((CS, GDN_DK, D), lambda n: (n, 0, 0))
    seg = pl.BlockSpec((CS * C, 128), lambda n: (n, 0))
    in_specs = [blk(0), blk(1), blk(0), seg, st,
                pl.BlockSpec((CS, GDN_H * C, C), lambda n: (n, 0, 0)), blk(0), blk(0), st, blk(0)]
    out_specs = [pl.BlockSpec((CS * C, 3 * D), lambda n: (n, 0)), seg]
    out_shape = [jax.ShapeDtypeStruct((T, 3 * D), F32), jax.ShapeDtypeStruct((T, 128), F32)]
    ins = [qk, qk, v, bg, s_save, t_save, vn, dvn, ds_save, do]
    if ride:
        ins, in_specs = ins + ride.srcs, in_specs + ride.specs
        out_shape, out_specs = out_shape + ride.out_shape, out_specs + ride.specs
    res = pl.pallas_call(
        body, grid=(NB,), in_specs=in_specs, out_specs=out_specs, out_shape=out_shape,
        scratch_shapes=ride.scratch if ride else [], name="gdn_rest_bwd",
        compiler_params=_params(("arbitrary",) if ride else ("parallel",)))(*ins)
    return (list(res[:2]), list(res[2:])) if ride else list(res)


def _ssd_seg(al_pair, half, s):
    L = SSM_L
    ri, ci = _iota2((L, L), 0), _iota2((L, L), 1)
    ac = jnp.max(jnp.where(half == s, al_pair, _NEG), axis=1, keepdims=True)
    ar = jnp.sum(jnp.where(ri == ci, ac, 0.0), axis=0, keepdims=True)
    return jnp.exp(jnp.where(ri >= ci, ac - ar, _NEG))


def _last_row(a):
    return jnp.sum(jnp.where(_iota2((a.shape[0], 1), 0) == a.shape[0] - 1, a, 0.0), axis=0, keepdims=True)


def _ssd_expand(da_ref):
    da = da_ref[...]
    return _dot_sel(da, _expand_mats(6, 0)[0]), _dot_sel(da, _expand_mats(6, SSM_H)[0])


def _ssd_core_fwd(xbc, da):
    T = xbc.shape[0]
    L, CS = SSM_L, SSM_SCAN_CHUNKS
    Nc = T // L

    def body(x_all, bc_all, da_all, y_all, hs_all, H_scr):
        @pl.when(pl.program_id(0) == 0)
        def _():
            H_scr[...] = jnp.zeros_like(H_scr)

        for cc in range(CS):
            rows = pl.ds(cc * L, L)
            chunk(x_all.at[rows], bc_all.at[rows], da_all.at[rows], y_all.at[rows], hs_all.at[cc], H_scr)

    def chunk(x_ref, bc_ref, da_ref, y_ref, hs_ref, H_scr):
        dt_ref, al_ref = _ssd_expand(da_ref)
        half = _iota2((L, 128), 1) >> 6
        for g in range(2):
            gs = slice(g * 512, (g + 1) * 512)
            Bg = bc_ref[:, g * 128:(g + 1) * 128]
            Cg = bc_ref[:, 256 + g * 128:256 + (g + 1) * 128]
            alg = al_ref[:, gs]
            alast = _last_row(alg)
            xdt = x_ref[:, gs] * dt_ref[:, gs]
            Hg = H_scr[:, gs]
            hs_ref[:, gs] = Hg
            CB = _dot(Cg, Bg, _NT)
            y_off = jnp.exp(alg) * _dot(Cg, Hg)
            H_scr[:, gs] = Hg * jnp.exp(alast) + _dot(Bg, jnp.exp(alast - alg) * xdt, _TN)
            for j in range(4):
                ps = slice(g * 512 + j * 128, g * 512 + (j + 1) * 128)
                al_pair = al_ref[:, ps]
                xp = x_ref[:, ps] * dt_ref[:, ps]
                ys = [_dot(_ssd_seg(al_pair, half, s) * CB, xp) for s in range(2)]
                y_ref[:, ps] = (y_off[:, j * 128:(j + 1) * 128]
                                + jnp.where(half == 0, ys[0], ys[1])).astype(y_ref.dtype)

    row = pl.BlockSpec((CS * L, D), lambda c: (c, 0))
    return dict(
        body=body, steps=Nc // CS, ins=[xbc, xbc, da],
        in_specs=[row, pl.BlockSpec((CS * L, 512), lambda c: (c, 2)), pl.BlockSpec((CS * L, 128), lambda c: (c, 0))],
        out_specs=[row, pl.BlockSpec((CS, SSM_N, D), lambda c: (c, 0, 0))],
        out_shape=[jax.ShapeDtypeStruct((T, D), BF16), jax.ShapeDtypeStruct((Nc, SSM_N, D), F32)],
        scratch=[pltpu.VMEM((SSM_N, D), F32)])


def _ssd_core_bwd(xbc, da, h_save, dyy, d_x):
    T = xbc.shape[0]
    L, CS = SSM_L, SSM_SCAN_CHUNKS
    Nc = T // L
    NB = Nc // CS

    def body(x_all, bc_all, da_all, hs_all, dy_all, d_ref, dx_all, dda_all, dH_scr, ddt_ref, dal_ref):
        @pl.when(pl.program_id(0) == 0)
        def _():
            dH_scr[...] = jnp.zeros_like(dH_scr)

        for cc in reversed(range(CS)):
            rows = pl.ds(cc * L, L)
            chunk(x_all.at[rows], bc_all.at[rows], da_all.at[rows], hs_all.at[cc], dy_all.at[rows],
                  d_ref, dx_all.at[rows], ddt_ref, dal_ref, dH_scr)
            dda_all[rows, :] = (_dot_sel(ddt_ref[...], _expand_mats(6, 0)[1])
                                + _dot_sel(dal_ref[...], _expand_mats(6, SSM_H)[1]))

    def chunk(x_ref, bc_ref, da_ref, hs_ref, dy_ref, d_ref, dx_ref, ddt_ref, dal_ref, dH_scr):
        dt_ref, al_ref = _ssd_expand(da_ref)
        lane = _iota2((L, 128), 1)
        half = lane >> 6
        rowi = _iota2((L, 1), 0)
        ri, ci = _iota2((L, L), 0), _iota2((L, L), 1)
        for g in range(2):
            gs = slice(g * 512, (g + 1) * 512)
            Bg = bc_ref[:, g * 128:(g + 1) * 128]
            Cg = bc_ref[:, 256 + g * 128:256 + (g + 1) * 128]
            alg = al_ref[:, gs]
            alast = _last_row(alg)
            eal, edec, eL = jnp.exp(alg), jnp.exp(alast - alg), jnp.exp(alast)
            xg, dtg, dYg = x_ref[:, gs], dt_ref[:, gs], dy_ref[:, gs]
            xdt = xg * dtg
            Hg = hs_ref[:, gs]
            dH2 = dH_scr[:, gs]
            CB = _dot(Cg, Bg, _NT)
            dYe = eal * dYg
            dH_scr[:, gs] = dH2 * eL + _dot(Cg, dYe, _TN)
            dC = _dot(dYe, Hg, _NT)
            zg = edec * xdt
            dz = _dot(Bg, dH2)
            dB = _dot(zg, dH2, _NT)
            tz = dz * zg
            dal = dYe * _dot(Cg, Hg) - tz
            dalast = jnp.sum(tz, axis=0, keepdims=True) + eL * jnp.sum(Hg * dH2, axis=0, keepdims=True)
            dal = dal + jnp.where(rowi == L - 1, dalast, 0.0)
            dxdt_g = edec * dz
            dx_ref[:, gs] = dxdt_g * dtg + dYg * d_ref[:, gs]
            ddt_ref[:, gs] = dxdt_g * xg
            dal_ref[:, gs] = dal
            dCB = jnp.zeros((L, L), F32)
            for j in range(4):
                ps = slice(g * 512 + j * 128, g * 512 + (j + 1) * 128)
                al_pair = al_ref[:, ps]
                xp = x_ref[:, ps] * dt_ref[:, ps]
                dYp = dy_ref[:, ps]
                dxp = []
                dal_p = jnp.zeros((L, 128), F32)
                for s in range(2):
                    seg = _ssd_seg(al_pair, half, s)
                    W = seg * CB
                    dW = _dot(jnp.where(half == s, dYp, 0.0), xp, _NT)
                    dxp.append(_dot(W, dYp, _TN))
                    dCB = dCB + dW * seg
                    Es = dW * W
                    dac = jnp.sum(Es, axis=1, keepdims=True) - jnp.sum(
                        jnp.where(ri == ci, jnp.sum(Es, axis=0, keepdims=True), 0.0), axis=1, keepdims=True)
                    dal_p = dal_p + jnp.where(lane == 64 * s, dac, 0.0)
                dxdt_p = jnp.where(half == 0, dxp[0], dxp[1])
                dx_ref[:, ps] += dxdt_p * dt_ref[:, ps]
                ddt_ref[:, ps] += dxdt_p * x_ref[:, ps]
                dal_ref[:, ps] += dal_p
            dx_ref[:, D + g * 128:D + (g + 1) * 128] = dB + _dot(dCB, Cg, _TN)
            dx_ref[:, D + 256 + g * 128:D + 256 + (g + 1) * 128] = dC + _dot(dCB, Bg)

    row = pl.BlockSpec((CS * L, D), lambda c: (NB - 1 - c, 0))
    bcs = pl.BlockSpec((CS * L, 512), lambda c: (NB - 1 - c, 2))
    seg = pl.BlockSpec((CS * L, 128), lambda c: (NB - 1 - c, 0))
    return dict(
        body=body, steps=NB, ins=[xbc, xbc, da, h_save, dyy, d_x],
        in_specs=[row, bcs, seg, pl.BlockSpec((CS, SSM_N, D), lambda c: (NB - 1 - c, 0, 0)), row,
                  pl.BlockSpec((1, D), lambda c: (0, 0))],
        out_specs=[pl.BlockSpec((CS * L, D + 512), lambda c: (NB - 1 - c, 0)), seg],
        out_shape=[jax.ShapeDtypeStruct((T, D + 512), F32), jax.ShapeDtypeStruct((T, 128), F32)],
        scratch=[pltpu.VMEM((SSM_N, D), F32), pltpu.VMEM((L, D), F32), pltpu.VMEM((L, D), F32)])


def _run_scans(parts, *, name):
    steps = parts[0]["steps"]
    assert all(p["steps"] == steps for p in parts)
    cnt = lambda key: [len(p[key]) for p in parts]
    n_in, n_out, n_scr = cnt("ins"), cnt("out_shape"), cnt("scratch")

    def body(*refs):
        ins, outs, scr = refs[:sum(n_in)], refs[sum(n_in):sum(n_in) + sum(n_out)], refs[sum(n_in) + sum(n_out):]
        oi = oo = os_ = 0
        for p, a, b, c in zip(parts, n_in, n_out, n_scr):
            p["body"](*ins[oi:oi + a], *outs[oo:oo + b], *scr[os_:os_ + c])
            oi, oo, os_ = oi + a, oo + b, os_ + c

    cat = lambda key: [v for p in parts for v in p[key]]
    res = pl.pallas_call(
        body, grid=(steps,), in_specs=cat("in_specs"), out_specs=cat("out_specs"), out_shape=cat("out_shape"),
        scratch_shapes=cat("scratch"), name=name, compiler_params=_params(("arbitrary",)))(*cat("ins"))
    out, o = [], 0
    for b in n_out:
        out.append(list(res[o:o + b]))
        o += b
    return out


_EARLY = ("w_out", "wq_mem", "wk_mem", "wv_mem", "wo_mem")
_LATE = ("w_up", "w_down")
_GRADS_MLP = ("w_down", "w_up")
_GRADS_MID = ("wo_mem", "wq_mem", "wk_mem", "wv_mem", "w_out")


def _gather_ride(shards, names):
    return None if shards is None else _Ride([shards[n] for n in names], shard=True)


def _grad_ride(shards, G, names):
    return None if shards is None else _Ride([_slots_from_full(n, G[n]) for n in names], shard=False)


def _local_step(x, mem, tgt, W, shards=None):
    T = x.shape[0]
    W = dict(W)
    cw_qk, cw_v = W["gdn_conv_w"][:, :2 * D], W["gdn_conv_w"][:, 2 * D:]
    h1 = _rmsnorm_fwd(x, W["norm1_w"], name="norm1_fwd")
    ride = _gather_ride(shards, _EARLY)
    pg = _mm(h1, W["w_in_pad"], b_cols=(C_GATE, C_TOT - C_GATE), name="in_proj_gates")
    p = _mm(h1, W["w_in_pad"], b_cols=(0, C_GATE), out_dtype=BF16, bn_cap=1664, name="in_proj", ride=ride)
    if ride:
        p, got = p
        W.update({n: _full_from_slots(n, g) for n, g in zip(_EARLY, got)})
    qk = _conv_fwd(p, C_QKV, 2 * D, cw_qk, None, l2=True, name="gdn_conv_qk_fwd")
    v_g = _conv_fwd(p, C_QKV + 2 * D, D, cw_v, None, l2=False, name="gdn_conv_v_fwd")
    bg = _gdn_gates_fwd(pg, W["gdn_alog_row"], W["gdn_dtb_row"])
    ride = _gather_ride(shards, _LATE)
    prep = _gdn_prep(qk, v_g, bg, ride)
    if ride:
        prep, got = prep
        W.update({n: _full_from_slots(n, g) for n, g in zip(_LATE, got)})
    u_g, w_g, qd_g, kd_g, p_g, t_save = prep
    xbc = _conv_fwd(p, C_XBC, D + 512, W["ssm_conv_w"], W["ssm_conv_b"], l2=False, name="ssm_conv_fwd", bc=512)
    da_s = _ssd_dt_fwd(pg, W["ssm_dtb_row"], W["ssm_alog_row"])
    (o_g, vn_g, s_save), (y_s, h_save) = _run_scans(
        [_gdn_scan_fwd(u_g, w_g, qd_g, kd_g, p_g, bg), _ssd_core_fwd(xbc, da_s)], name="scans_fwd")
    mix = _gdn_post_fwd(o_g, p, W["gdn_norm_x"])
    mix = _ssd_post_fwd(y_s, xbc, p, W["ssm_d_x"], W["ssm_norm_w"].reshape(1, D), mix)
    x1, h2 = _mm(mix, W["w_out"], epi="res_norm", extra=(x, W["norm2_w"]), bm=512, name="out_proj")
    qm = _mm(h2, W["wq_mem"], out_dtype=BF16, name="q_proj")
    m = _rmsnorm_fwd(mem, W["mem_norm_w"], name="mem_norm_fwd")
    km = _mm(m, W["wk_mem"], name="k_proj")
    vm = _mm(m, W["wv_mem"], name="v_proj")
    oa = _attn_fwd(qm, km, vm)
    x2, h3 = _mm(oa, W["wo_mem"], epi="res_norm", extra=(x1, W["norm3_w"]), bm=512, name="o_proj")
    u, act = _mm(h3, W["w_up"], epi="relu2", out_dtype=BF16, b_resident=True, name="mlp_up")
    dx3, g_final, loss = _mm(act, W["w_down"], epi="res_loss", extra=(x2, tgt, W["final_norm_w"]), bk_cap=1024,
                             b_resident=True, name="mlp_down_loss")
    G = {"final_norm_w": g_final.reshape(D)}
    dpre = _mm(dx3, W["w_down"], dims="nt", epi="mul2", extra=u, out_dtype=BF16, b_resident=True, name="mlp_down_dx")
    G["w_down"] = _mm(act, dx3, dims="tn", out_dtype=BF16, name="mlp_down_dw")
    G["w_up"] = _mm(h3, dpre, dims="tn", out_dtype=BF16, name="mlp_up_dw")
    dx2, gw = _mm(dpre, W["w_up"], dims="nt", epi="norm_bwd", extra=(x2, dx3, W["norm3_w"]), bk_cap=1024,
                  b_resident=True, name="mlp_up_dx")
    G["norm3_w"] = gw.reshape(D)
    do_a = _mm(dx2, W["wo_mem"], dims="nt", out_dtype=BF16, name="o_proj_dx")
    G["wo_mem"] = _mm(oa, dx2, dims="tn", out_dtype=BF16, name="o_proj_dw")
    dq, dk, dv = _attn_bwd(qm, km, vm, do_a)
    G["wq_mem"] = _mm(h2, dq, dims="tn", out_dtype=BF16, name="q_proj_dw")
    dx1, gw = _mm(dq, W["wq_mem"], dims="nt", epi="norm_bwd", extra=(x1, dx2, W["norm2_w"]), bm=512,
                  name="q_proj_dx")
    G["norm2_w"] = gw.reshape(D)
    G["wk_mem"] = _mm(m, dk, dims="tn", out_dtype=BF16, name="k_proj_dw")
    G["wv_mem"] = _mm(m, dv, dims="tn", out_dtype=BF16, name="v_proj_dw")
    dm = _mm(dk, W["wk_mem"], dims="nt", name="k_proj_dx")
    dm = _mm(dv, W["wv_mem"], dims="nt", epi="res", extra=dm, name="v_proj_dx")
    _, G["mem_norm_w"] = _rmsnorm_bwd(mem, W["mem_norm_w"], dm, None, name="mem_norm_bwd")
    G["w_out"] = _mm(mix, dx1, dims="tn", out_dtype=BF16, name="out_proj_dw")
    do_g, dp, G["gdn_norm_x"] = _gdn_post_bwd(dx1, W["w_out"], o_g, p, W["gdn_norm_x"])
    dyy, dp, G["ssm_d_x"], G["ssm_norm_w"] = _ssd_post_bwd(dx1, W["w_out"], y_s, xbc, p, W["ssm_d_x"],
                                                          W["ssm_norm_w"].reshape(1, D), dp)
    (dvn_g, ds_save), (dxbc, dda_s) = _run_scans(
        [_gdn_scan_bwd(w_g, qd_g, kd_g, p_g, bg, do_g), _ssd_core_bwd(xbc, da_s, h_save, dyy, W["ssm_d_x"])],
        name="scans_bwd")
    ride = _grad_ride(shards, G, _GRADS_MLP)
    rest = _gdn_rest_bwd(qk, v_g, bg, s_save, t_save, vn_g, dvn_g, ds_save, do_g, ride)
    if ride:
        rest, got = rest
        G.update(zip(_GRADS_MLP, got))
    dqkvn, dbg = rest
    dy_qk, gcw_qk, _ = _conv_bwd_act(p, C_QKV, 2 * D, cw_qk, None, dqkvn, 0, l2=True, name="gdn_conv_qk_bwd_act")
    dy_v, gcw_v, _ = _conv_bwd_act(p, C_QKV + 2 * D, D, cw_v, None, dqkvn, 2 * D, l2=False,
                                   name="gdn_conv_v_bwd_act")
    G["gdn_conv_w"] = jnp.concatenate([gcw_qk, gcw_v], axis=1)
    dp = _conv_bwd_in(dy_qk, cw_qk, dp, C_QKV, T, name="gdn_conv_qk_bwd_in")
    dp = _conv_bwd_in(dy_v, cw_v, dp, C_QKV + 2 * D, T, name="gdn_conv_v_bwd_in")
    dp, G["gdn_alog_row"], G["gdn_dtb_row"] = _gdn_gates_bwd(pg, W["gdn_alog_row"], W["gdn_dtb_row"], dbg, dp)
    dy_s, G["ssm_conv_w"], G["ssm_conv_b"] = _conv_bwd_act(p, C_XBC, D + 512, W["ssm_conv_w"], W["ssm_conv_b"],
                                                           dxbc, 0, l2=False, name="ssm_conv_bwd_act", bc=512)
    dp = _conv_bwd_in(dy_s, W["ssm_conv_w"], dp, C_XBC, T, name="ssm_conv_bwd_in", bc=512)
    dp, G["ssm_dtb_row"], G["ssm_alog_row"] = _ssd_dt_bwd(pg, W["ssm_dtb_row"], W["ssm_alog_row"], dda_s, dp)
    ride = _grad_ride(shards, G, _GRADS_MID)
    g_in = _mm(h1, dp, dims="tn", out_dtype=BF16, bn_cap=1152, name="in_proj_dw", ride=ride)
    if ride:
        g_in, got = g_in
        G.update(zip(_GRADS_MID, got))
    G["w_in"] = _unpad_w_in(g_in)
    ride = _grad_ride(shards, G, ("w_in",))
    res = _mm(dp, W["w_in_pad"], dims="nt", epi="norm_bwd", extra=(x, dx1, W["norm1_w"]), b_resident=True,
              name="in_proj_dx", ride=ride)
    if ride:
        res, got = res
        G["w_in"] = got[0]
    dx, gw = res
    G["norm1_w"] = gw.reshape(D)
    return loss, dx, G


def _all_gather(shards, out_dtype, *, name):
    n = len(shards)

    def body(*refs):
        x_refs, out_refs, stage = refs[:n], refs[n:2 * n], refs[2 * n:3 * n]
        send_sems, recv_sems, local_sems = refs[3 * n:]
        x, y, c = _place()
        me, sibling = (x, y, c), (x, y, 1 - c)
        chips = [(1 - x, y), (x, 1 - y), (1 - x, 1 - y)]

        def slot(px, py, pc):
            return 4 * px + 2 * py + pc

        def copy(a, k, block, to, src=None):
            dst = out_refs[a].at[slot(*block)]
            return pltpu.make_async_remote_copy(
                src_ref=dst if src is None else src, dst_ref=dst, send_sem=send_sems.at[a, k],
                recv_sem=recv_sems.at[a, k], device_id=to, device_id_type=_MESH)

        for a in range(n):
            stage[a][...] = x_refs[a][...].astype(out_dtype)
        mine = [pltpu.make_async_copy(stage[a], out_refs[a].at[slot(*me)], local_sems.at[a]) for a in range(n)]
        for cp in mine:
            cp.start()
        first = []
        for a in range(n):
            first.append(copy(a, 0, me, sibling, src=stage[a]))
            first += [copy(a, 1 + j, me, (*chip, c), src=stage[a]) for j, chip in enumerate(chips)]
        for cp in first:
            cp.start()
        passed = [[copy(a, 4 + j, (*chip, c), sibling) for j, chip in enumerate(chips)] for a in range(n)]
        for j, chip in enumerate(chips):
            for a in range(n):
                copy(a, 1 + j, (*chip, c), me).wait_recv()
                passed[a][j].start()
        for a in range(n):
            copy(a, 0, sibling, me).wait_recv()
            for j, chip in enumerate(chips):
                copy(a, 4 + j, (*chip, 1 - c), me).wait_recv()
        for cp in first + [cp for row in passed for cp in row]:
            cp.wait_send()
        for cp in mine:
            cp.wait()

    outs = pl.pallas_call(
        body, in_specs=[_VM] * n, out_specs=[_ANY] * n,
        out_shape=[jax.ShapeDtypeStruct((N_DEV,) + s.shape, out_dtype) for s in shards],
        scratch_shapes=[pltpu.VMEM(s.shape, out_dtype) for s in shards]
        + [pltpu.SemaphoreType.DMA((n, 7)), pltpu.SemaphoreType.DMA((n, 7)), pltpu.SemaphoreType.DMA((n,))],
        name=name, compiler_params=pltpu.CompilerParams(vmem_limit_bytes=VMEM_LIMIT))(*shards)
    return list(outs)


def _cast_bf16(arrs, *, name):
    n = len(arrs)

    def body(*refs):
        for a in range(n):
            refs[n + a][...] = refs[a][...].astype(BF16)

    return list(pl.pallas_call(
        body, in_specs=[_VM] * n, out_specs=[_VM] * n,
        out_shape=[jax.ShapeDtypeStruct(s.shape, BF16) for s in arrs], name=name,
        compiler_params=pltpu.CompilerParams(vmem_limit_bytes=VMEM_LIMIT))(*arrs))


def _sum8(a, *, name):
    _, R, Cc = a.shape
    br = _pick_rows(R, 128)

    def body(a_ref, o_ref):
        s = a_ref[0].astype(F32)
        for k in range(1, N_DEV):
            s = s + a_ref[k].astype(F32)
        o_ref[...] = s

    return pl.pallas_call(
        body, grid=(R // br,), in_specs=[pl.BlockSpec((N_DEV, br, Cc), lambda i: (0, i, 0))],
        out_specs=pl.BlockSpec((br, Cc), lambda i: (i, 0)), out_shape=jax.ShapeDtypeStruct((R, Cc), F32),
        name=name, compiler_params=_params(("parallel",)))(a)


def _pick_rows(R, cap):
    if R <= cap:
        return R
    for d in range(cap, 7, -8):
        if R % d == 0:
            return d
    return R


def _adamw(w, g, m, v, *, name):
    shape = w.shape
    as2d = (lambda t: t.reshape(1, -1)) if w.ndim == 1 else (lambda t: t)
    w2, m2, v2 = as2d(w), as2d(m), as2d(v)
    R, Cc = w2.shape
    from_slabs = g.ndim == 3
    br = _pick_rows(R, 128 if from_slabs else 256)
    c1 = 1.0 - ADAM_B1 ** ADAM_STEP
    c2 = 1.0 - ADAM_B2 ** ADAM_STEP

    def body(w_ref, g_ref, m_ref, v_ref, go_ref, d_ref, nm_ref, nv_ref):
        if from_slabs:
            gv = g_ref[0].astype(F32)
            for k in range(1, N_DEV):
                gv = gv + g_ref[k].astype(F32)
        else:
            gv = g_ref[...]
        go_ref[...] = gv
        nm = ADAM_B1 * m_ref[...] + (1.0 - ADAM_B1) * gv
        nv = ADAM_B2 * v_ref[...] + (1.0 - ADAM_B2) * (gv * gv)
        nm_ref[...] = nm
        nv_ref[...] = nv
        d_ref[...] = -ADAM_LR * ((nm / c1) / (jnp.sqrt(nv / c2) + ADAM_EPS) + ADAM_WD * w_ref[...])

    blk = pl.BlockSpec((br, Cc), lambda i: (i, 0))
    g_spec = pl.BlockSpec((N_DEV, br, Cc), lambda i: (0, i, 0)) if from_slabs else blk
    outs = pl.pallas_call(
        body, grid=(R // br,), in_specs=[blk, g_spec, blk, blk], out_specs=[blk] * 4,
        out_shape=[jax.ShapeDtypeStruct((R, Cc), F32)] * 4, name=name,
        compiler_params=_params(("parallel",)))(w2, g if from_slabs else as2d(g), m2, v2)
    return tuple(o.reshape(shape) for o in outs)


_BIG = ("w_in", "w_out", "wq_mem", "wk_mem", "wv_mem", "wo_mem", "w_up", "w_down")
_COL_SHARDED = ("w_in", "w_up")
_WEIGHTS = ("norm1_w", "w_in", "gdn_conv_w", "gdn_a_log", "gdn_dt_bias", "gdn_norm_w", "ssm_conv_w", "ssm_conv_b",
            "ssm_a_log", "ssm_dt_bias", "ssm_d", "ssm_norm_w", "w_out", "norm2_w", "mem_norm_w", "wq_mem", "wk_mem",
            "wv_mem", "wo_mem", "norm3_w", "w_up", "w_down", "final_norm_w")
_IN_PAD = 112


def _move_col_slabs(a, to_slabs, *, name):
    n, R, c = (N_DEV, a.shape[0], a.shape[1] // N_DEV) if to_slabs else a.shape
    slab = pl.BlockSpec((None, R, c), lambda j: (j, 0, 0))
    cols = pl.BlockSpec((R, c), lambda j: (0, j))

    def body(a_ref, o_ref):
        o_ref[...] = a_ref[...]

    return pl.pallas_call(
        body, grid=(n,), in_specs=[cols if to_slabs else slab], out_specs=slab if to_slabs else cols,
        out_shape=jax.ShapeDtypeStruct((n, R, c) if to_slabs else (R, n * c), a.dtype), name=name,
        compiler_params=_params(("parallel",)))(a)


def _full_from_slots(name, g):
    if name in _COL_SHARDED:
        if g.shape[2] % 128 == 0:
            return _move_col_slabs(g, False, name="cols_" + name)
        return jnp.transpose(g, (1, 0, 2)).reshape(g.shape[1], N_DEV * g.shape[2])
    return g.reshape(N_DEV * g.shape[1], g.shape[2])


def _slots_from_full(name, f):
    if name in _COL_SHARDED:
        if (f.shape[1] // N_DEV) % 128 == 0:
            return _move_col_slabs(f, True, name="slabs_" + name)
        return jnp.transpose(f.reshape(f.shape[0], N_DEV, f.shape[1] // N_DEV), (1, 0, 2))
    return f.reshape(N_DEV, f.shape[0] // N_DEV, f.shape[1])


def _pad_w_in(w):
    z = jnp.zeros((w.shape[0], _IN_PAD), w.dtype)
    return jnp.concatenate([w[:, :4096], w[:, 4112:6672], w[:, 4096:4112], z, w[:, 6672:6688], z], axis=1)


def _unpad_w_in(gp):
    return jnp.concatenate([gp[:, :4096], gp[:, C_GATE:C_GATE + 16], gp[:, 4096:C_GATE], gp[:, C_DT:C_DT + 16]],
                           axis=1)


def _pack_rows(vals):
    rows, offs, r = [], [], 0
    for vflat in vals:
        nrow = 8 * -(-vflat.shape[0] // 1024)
        rows.append(jnp.pad(vflat, (0, nrow * 128 - vflat.shape[0])).reshape(nrow, 128))
        offs.append((r, vflat.shape[0]))
        r += nrow
    return jnp.concatenate(rows, axis=0), offs


def _unpack_rows(packed, offs, shapes):
    out = []
    for (r, nel), shp in zip(offs, shapes):
        nrow = -(-nel // 128)
        out.append(packed[r:r + nrow].reshape(-1)[:nel].reshape(shp))
    return out


def kernel(x, mem, norm1_w, w_in, gdn_conv_w, gdn_a_log, gdn_dt_bias, gdn_norm_w, ssm_conv_w, ssm_conv_b, ssm_a_log, ssm_dt_bias, ssm_d, ssm_norm_w, w_out, norm2_w, mem_norm_w, wq_mem, wk_mem, wv_mem, wo_mem, norm3_w, w_up, w_down, final_norm_w, loss_target, m_norm1_w, m_w_in, m_gdn_conv_w, m_gdn_a_log, m_gdn_dt_bias, m_gdn_norm_w, m_ssm_conv_w, m_ssm_conv_b, m_ssm_a_log, m_ssm_dt_bias, m_ssm_d, m_ssm_norm_w, m_w_out, m_norm2_w, m_mem_norm_w, m_wq_mem, m_wk_mem, m_wv_mem, m_wo_mem, m_norm3_w, m_w_up, m_w_down, m_final_norm_w, v_norm1_w, v_w_in, v_gdn_conv_w, v_gdn_a_log, v_gdn_dt_bias, v_gdn_norm_w, v_ssm_conv_w, v_ssm_conv_b, v_ssm_a_log, v_ssm_dt_bias, v_ssm_d, v_ssm_norm_w, v_w_out, v_norm2_w, v_mem_norm_w, v_wq_mem, v_wk_mem, v_wv_mem, v_wo_mem, v_norm3_w, v_w_up, v_w_down, v_final_norm_w):
    args = dict(locals())
    w_loc = {n: args[n] for n in _WEIGHTS}
    me = 4 * lax.axis_index("x") + 2 * lax.axis_index("y") + lax.axis_index("c")

    w_in_full = _full_from_slots("w_in", _all_gather([w_in], BF16, name="gather_w_in")[0])
    later = _EARLY + _LATE
    shards = dict(zip(later, _cast_bf16([w_loc[n] for n in later], name="cast_shards")))
    conv_pack, conv_offs = _pack_rows([gdn_conv_w.reshape(-1), ssm_conv_w.reshape(-1)])
    conv_all = _all_gather([conv_pack], F32, name="gather_conv")[0]
    gdn_cw, ssm_cw = [], []
    for k in range(N_DEV):
        a, b = _unpack_rows(conv_all[k], conv_offs, [gdn_conv_w.shape, ssm_conv_w.shape])
        gdn_cw.append(a)
        ssm_cw.append(b)
    W = {
        "w_in_pad": _pad_w_in(w_in_full),
        "norm1_w": norm1_w, "norm2_w": norm2_w, "norm3_w": norm3_w, "mem_norm_w": mem_norm_w,
        "final_norm_w": final_norm_w, "ssm_norm_w": ssm_norm_w, "ssm_conv_b": ssm_conv_b,
        "gdn_conv_w": jnp.concatenate(gdn_cw, axis=1), "ssm_conv_w": jnp.concatenate(ssm_cw, axis=1),
        "gdn_alog_row": jnp.pad(gdn_a_log, (GDN_H, 128 - 2 * GDN_H)).reshape(1, 128),
        "gdn_dtb_row": jnp.pad(gdn_dt_bias, (GDN_H, 128 - 2 * GDN_H)).reshape(1, 128),
        "gdn_norm_x": jnp.tile(gdn_norm_w, GDN_H).reshape(1, D),
        "ssm_dtb_row": jnp.pad(ssm_dt_bias, (0, 128 - SSM_H)).reshape(1, 128),
        "ssm_alog_row": jnp.pad(ssm_a_log, (0, 128 - SSM_H)).reshape(1, 128),
        "ssm_d_x": jnp.repeat(ssm_d, SSM_P).reshape(1, D),
    }

    loss_part, grad_x, G = _local_step(x[0], mem[0], loss_target[0], W, shards)

    grads = {n: G[n] for n in _BIG}

    small = {
        "norm1_w": G["norm1_w"], "gdn_conv_w": G["gdn_conv_w"], "gdn_a_log": G["gdn_alog_row"][0, GDN_H:2 * GDN_H],
        "gdn_dt_bias": G["gdn_dtb_row"][0, GDN_H:2 * GDN_H], "gdn_norm_w": G["gdn_norm_x"].reshape(GDN_H, 128).sum(0),
        "ssm_conv_w": G["ssm_conv_w"], "ssm_conv_b": G["ssm_conv_b"],
        "ssm_a_log": G["ssm_alog_row"][0, :SSM_H], "ssm_dt_bias": G["ssm_dtb_row"][0, :SSM_H],
        "ssm_d": G["ssm_d_x"].reshape(SSM_H, SSM_P).sum(1), "ssm_norm_w": G["ssm_norm_w"].reshape(D),
        "norm2_w": G["norm2_w"], "mem_norm_w": G["mem_norm_w"], "norm3_w": G["norm3_w"],
        "final_norm_w": G["final_norm_w"], "loss": loss_part[0, :1],
    }
    names = list(small)
    pack, offs = _pack_rows([small[n].reshape(-1) for n in names])
    tot = _sum8(_all_gather([pack], F32, name="gather_small")[0], name="sum_small")
    summed = dict(zip(names, _unpack_rows(tot, offs, [small[n].shape for n in names])))
    loss = summed.pop("loss")[0]
    for n in ("gdn_conv_w", "ssm_conv_w"):
        width = w_loc[n].shape[1]
        summed[n] = lax.dynamic_slice_in_dim(summed[n], me * width, width, axis=1)
    grads.update(summed)

    upd = {n: _adamw(w_loc[n], grads[n], args["m_" + n], args["v_" + n], name="adamw_" + n) for n in _WEIGHTS}
    return (loss, grad_x[None], *[upd[n][0] for n in _WEIGHTS], *[upd[n][1] for n in _WEIGHTS],
            *[upd[n][2] for n in _WEIGHTS], *[upd[n][3] for n in _WEIGHTS])
```

```python
import jax
import jax.numpy as jnp
from jax import lax
from jax.experimental import pallas as pl
from jax.experimental.pallas import tpu as pltpu

F32 = jnp.float32
BF16 = jnp.bfloat16
_MXU = BF16

D = 1024
EPS = 1e-6
CONV_K = 4
GDN_H, GDN_DK, GDN_C = 8, 128, 64
GDN_SCAN_CHUNKS = 8
GDN_LOCAL_CHUNKS = 4
GDN_REST_CHUNKS = 4
SSM_H, SSM_P, SSM_L, SSM_N = 16, 64, 128, 128
SSM_SCAN_CHUNKS = 4
MEM_H, MEM_HD = 4, 256
D_FF = 4096
N_DEV = 8

C_QKV, C_ZG, C_ZS, C_XBC, C_GATE, C_DT, C_TOT = 0, 3072, 4096, 5120, 6656, 6784, 6912
P_HALO = 16

ADAM_LR, ADAM_B1, ADAM_B2, ADAM_EPS, ADAM_WD, ADAM_STEP = 0.001, 0.9, 0.999, 1e-08, 0.01, 10

VMEM_LIMIT = 56 * 1024 * 1024

_NN = (((1,), (0,)), ((), ()))
_NT = (((1,), (1,)), ((), ()))
_TN = (((0,), (0,)), ((), ()))


def _dot(a, b, dims=_NN):
    return lax.dot_general(a.astype(_MXU), b.astype(_MXU), dims, preferred_element_type=F32)


def _split3(a):
    a1 = a.astype(BF16)
    r1 = a - a1.astype(F32)
    a2 = r1.astype(BF16)
    return a1, a2, (r1 - a2.astype(F32)).astype(BF16)


def _dot_sel(a, e):
    eb = e.astype(BF16)
    return sum(lax.dot_general(p, eb, _NN, preferred_element_type=F32) for p in _split3(a))


def _sel_dot(e, a):
    eb = e.astype(BF16)
    return sum(lax.dot_general(eb, p, _NN, preferred_element_type=F32) for p in _split3(a))


def _chunk_cumsum(a, tri, chunk):
    return jnp.concatenate([_sel_dot(tri, a[r:r + chunk]) for r in range(0, a.shape[0], chunk)], axis=0)


def _params(sem):
    return pltpu.CompilerParams(dimension_semantics=sem, vmem_limit_bytes=VMEM_LIMIT)


def _pick(n, cap):
    for d in range(min(cap, n), 0, -128):
        if n % d == 0 and d % 128 == 0:
            return d
    return n


def _sigmoid(x):
    return 0.5 * jnp.tanh(0.5 * x) + 0.5


def _silu(x):
    return x * _sigmoid(x)


def _dsilu(x):
    s = _sigmoid(x)
    return s * (1.0 + x * (1.0 - s))


def _softplus(x):
    return jnp.maximum(x, 0.0) + jnp.log(1.0 + jnp.exp(-jnp.abs(x)))


def _iota2(shape, axis):
    return lax.broadcasted_iota(jnp.int32, shape, axis)


def _sum_all(x):
    return jnp.sum(jnp.sum(x, axis=1, keepdims=True), axis=0, keepdims=True)


_MESH = pl.DeviceIdType.MESH
_ANY = pl.BlockSpec(memory_space=pl.ANY)
_VM = pl.BlockSpec(memory_space=pltpu.VMEM)
_REL = [(r >> 2 & 1, r >> 1 & 1, r & 1) for r in range(1, N_DEV)]


def _place():
    return lax.axis_index("x"), lax.axis_index("y"), lax.axis_index("c")


class _Ride:
    def __init__(self, srcs, shard):
        self.srcs, self.shard, self.n = list(srcs), shard, len(srcs)
        self.out_shape = [jax.ShapeDtypeStruct(((N_DEV,) + s.shape) if shard else s.shape, s.dtype)
                          for s in self.srcs]
        self.specs = [_ANY] * self.n
        self.scratch = [pltpu.SemaphoreType.DMA((self.n, N_DEV - 1)), pltpu.SemaphoreType.DMA((self.n, N_DEV - 1)),
                        pltpu.SemaphoreType.DMA((self.n,))]

    def _copies(self, in_refs, out_refs, sems):
        send, recv, loc = sems
        x, y, c = _place()
        me = 4 * x + 2 * y + c
        local, remote, arrive = [], [], []
        for a in range(self.n):
            src = in_refs[a] if self.shard else in_refs[a].at[me]
            local.append(pltpu.make_async_copy(src, out_refs[a].at[me], loc.at[a]))
        for k, (rx, ry, rc) in enumerate(_REL):
            peer = (lax.rem(x + rx, 2), lax.rem(y + ry, 2), lax.rem(c + rc, 2))
            ps = 4 * peer[0] + 2 * peer[1] + peer[2]
            for a in range(self.n):
                src = in_refs[a] if self.shard else in_refs[a].at[ps]
                remote.append(pltpu.make_async_remote_copy(
                    src_ref=src, dst_ref=out_refs[a].at[me], send_sem=send.at[a, k], recv_sem=recv.at[a, k],
                    device_id=peer, device_id_type=_MESH))
                slot = out_refs[a].at[ps]
                arrive.append(pltpu.make_async_remote_copy(
                    src_ref=slot, dst_ref=slot, send_sem=send.at[a, k], recv_sem=recv.at[a, k],
                    device_id=peer, device_id_type=_MESH))
        return local, remote, arrive

    def start(self, in_refs, out_refs, sems):
        local, remote, _ = self._copies(in_refs, out_refs, sems)
        for cp in local + remote:
            cp.start()

    def wait(self, in_refs, out_refs, sems):
        local, remote, arrive = self._copies(in_refs, out_refs, sems)
        for cp in arrive:
            cp.wait_recv()
        for cp in remote:
            cp.wait_send()
        for cp in local:
            cp.wait()


_EPI = {
    "none": ((), ("tile",)),
    "res": (("tile",), ("tile",)),
    "mul2": (("tile",), ("tile",)),
    "relu2": ((), ("tile", "tile")),
    "res_norm": (("tile", "row"), ("tile", "tile")),
    "norm_bwd": (("tile", "tile", "row"), ("tile", "row")),
    "res_loss": (("tile", "tile", "row"), ("tile", "row", "row")),
}


def _mm(a, b, *, dims="nn", epi="none", extra=(), out_dtype=F32, name, bm=1024, bn_cap=1024, bk_cap=2048,
        ride=None, b_cols=None, b_resident=False):
    if dims == "nn":
        (M, K), (K2, N) = a.shape, b.shape
    elif dims == "nt":
        (M, K), (N, K2) = a.shape, b.shape
    else:
        (K, M), (K2, N) = a.shape, b.shape
    jb0 = 0
    if b_cols is not None:
        N = b_cols[1]
    assert K == K2, (a.shape, b.shape, dims)
    bm = _pick(M, bm)
    bn = _pick(N, bn_cap)
    bk = _pick(K, bk_cap)
    nk = K // bk
    if b_cols is not None:
        assert dims == "nn" and b_cols[0] % bn == 0
        jb0 = b_cols[0] // bn
    dn = {"nn": _NN, "nt": _NT, "tn": _TN}[dims]
    a_spec = (pl.BlockSpec((bk, bm), lambda i, j, k: (k, i)) if dims == "tn"
              else pl.BlockSpec((bm, bk), lambda i, j, k: (i, k)))
    if b_resident:
        b_spec = pl.BlockSpec(b.shape, lambda i, j, k: (0, 0), pipeline_mode=pl.Buffered(1))
    else:
        b_spec = (pl.BlockSpec((bn, bk), lambda i, j, k: (j, k)) if dims == "nt"
                  else pl.BlockSpec((bk, bn), lambda i, j, k: (k, j + jb0)))
    o_spec = pl.BlockSpec((bm, bn), lambda i, j, k: (i, j))
    r_spec = pl.BlockSpec((1, bn), lambda i, j, k: (0, j))
    extra = list(extra) if isinstance(extra, (tuple, list)) else [extra]
    ekinds, okinds = _EPI[epi]
    assert len(extra) == len(ekinds) and (epi not in ("res_norm", "norm_bwd", "res_loss") or bn == N)
    n_extra, n_out = len(ekinds), len(okinds)
    n_ride = ride.n if ride else 0
    gi, gj = M // bm, N // bn

    def body(a_ref, b_ref, *rest):
        ex = rest[:n_extra]
        first = pl.program_id(0) == 0
        ride_in = rest[n_extra:n_extra + n_ride]
        outs = rest[n_extra + n_ride:n_extra + n_ride + n_out]
        ride_out = rest[n_extra + n_ride + n_out:n_extra + 2 * n_ride + n_out]
        if ride:
            at = lambda i, j, k: ((pl.program_id(0) == i) & (pl.program_id(1) == j) & (pl.program_id(2) == k))

            @pl.when(at(0, 0, 0))
            def _():
                ride.start(ride_in, ride_out, rest[-3:])

        def finish(r):
            if epi == "res":
                outs[0][...] = (r + ex[0][...].astype(F32)).astype(outs[0].dtype)
            elif epi == "mul2":
                outs[0][...] = (2.0 * r * ex[0][...].astype(F32)).astype(outs[0].dtype)
            elif epi == "relu2":
                u = jnp.maximum(r, 0.0)
                outs[0][...] = u.astype(outs[0].dtype)
                outs[1][...] = (u * u).astype(outs[1].dtype)
            elif epi == "res_norm":
                y = r + ex[0][...]
                outs[0][...] = y
                rstd = lax.rsqrt(jnp.mean(y * y, axis=1, keepdims=True) + EPS)
                outs[1][...] = (y * rstd * ex[1][...]).astype(outs[1].dtype)
            elif epi == "norm_bwd":
                xv = ex[0][...]
                rstd = lax.rsqrt(jnp.mean(xv * xv, axis=1, keepdims=True) + EPS)
                xh = xv * rstd
                dxh = r * ex[2][...]
                outs[0][...] = ex[1][...] + rstd * (dxh - xh * jnp.mean(dxh * xh, axis=1, keepdims=True))
                dw = jnp.sum(r * xh, axis=0, keepdims=True)

                @pl.when(first)
                def _():
                    outs[1][...] = dw

                @pl.when(jnp.logical_not(first))
                def _():
                    outs[1][...] += dw
            elif epi == "res_loss":
                y = r + ex[0][...]
                wv = ex[2][...]
                rstd = lax.rsqrt(jnp.mean(y * y, axis=1, keepdims=True) + EPS)
                yh = y * rstd
                err = yh * wv - ex[1][...]
                part_loss = 0.5 * jnp.sum(jnp.mean(err * err, axis=1, keepdims=True), axis=0, keepdims=True)
                dyn = err * (1.0 / N)
                dyh = dyn * wv
                outs[0][...] = rstd * (dyh - yh * jnp.mean(dyh * yh, axis=1, keepdims=True))
                dw = jnp.sum(dyn * yh, axis=0, keepdims=True)
                lrow = jnp.broadcast_to(part_loss, (1, N))

                @pl.when(first)
                def _():
                    outs[1][...] = dw
                    outs[2][...] = lrow

                @pl.when(jnp.logical_not(first))
                def _():
                    outs[1][...] += dw
                    outs[2][...] += lrow
            else:
                outs[0][...] = r.astype(outs[0].dtype)

        if b_resident:
            jo = pl.multiple_of((pl.program_id(1) + jb0) * bn, bn)
            ko = pl.multiple_of(pl.program_id(2) * bk, bk)
            b_blk = b_ref[pl.ds(jo, bn), pl.ds(ko, bk)] if dims == "nt" else b_ref[pl.ds(ko, bk), pl.ds(jo, bn)]
        else:
            b_blk = b_ref[...]
        part = _dot(a_ref[...], b_blk, dn)
        if nk == 1:
            finish(part)
        else:
            acc = rest[n_extra + 2 * n_ride + n_out]
            k = pl.program_id(2)

            @pl.when(k == 0)
            def _():
                acc[...] = part

            @pl.when((k > 0) & (k < nk - 1))
            def _():
                acc[...] += part

            @pl.when(k == nk - 1)
            def _():
                finish(acc[...] + part)

        if ride:
            @pl.when(at(gi - 1, gj - 1, nk - 1))
            def _():
                ride.wait(ride_in, ride_out, rest[-3:])

    kind_spec = {"tile": o_spec, "row": r_spec}
    ins = [a, b] + [e.reshape(1, N) if k == "row" else e for e, k in zip(extra, ekinds)]
    in_specs = [a_spec, b_spec] + [kind_spec[k] for k in ekinds]
    out_dtypes = {"res_norm": (F32, BF16), "norm_bwd": (F32, F32), "res_loss": (F32, F32, F32)}.get(
        epi, (out_dtype,) * n_out)
    out_shape = [jax.ShapeDtypeStruct((M, N) if k == "tile" else (1, N), dt) for k, dt in zip(okinds, out_dtypes)]
    out_specs = [kind_spec[k] for k in okinds]
    scratch = [pltpu.VMEM((bm, bn), F32)] if nk > 1 else []
    sem = ("arbitrary" if epi in ("norm_bwd", "res_loss") else "parallel", "parallel", "arbitrary")
    if ride:
        ins, in_specs = ins + ride.srcs, in_specs + ride.specs
        out_shape, out_specs = out_shape + ride.out_shape, out_specs + ride.specs
        scratch, sem = scratch + ride.scratch, ("arbitrary",) * 3
    res = pl.pallas_call(
        body, grid=(gi, gj, nk), in_specs=in_specs, out_specs=out_specs, out_shape=out_shape,
        scratch_shapes=scratch, name=name, compiler_params=_params(sem))(*ins)
    main = res[:n_out] if n_out > 1 else res[0]
    return (main, list(res[n_out:])) if ride else main


def _rmsnorm_fwd(x, w, *, name, bt=256):
    T, Dm = x.shape
    bt = min(bt, T)

    def body(x_ref, w_ref, h_ref):
        xv = x_ref[...]
        r = lax.rsqrt(jnp.mean(xv * xv, axis=1, keepdims=True) + EPS)
        h_ref[...] = (xv * r * w_ref[...]).astype(h_ref.dtype)

    return pl.pallas_call(
        body, grid=(T // bt,),
        in_specs=[pl.BlockSpec((bt, Dm), lambda i: (i, 0)), pl.BlockSpec((1, Dm), lambda i: (0, 0))],
        out_specs=pl.BlockSpec((bt, Dm), lambda i: (i, 0)),
        out_shape=jax.ShapeDtypeStruct((T, Dm), BF16), name=name,
        compiler_params=_params(("parallel",)))(x, w.reshape(1, Dm))


def _rmsnorm_bwd(x, w, dh, dres, *, name, bt=256):
    T, Dm = x.shape
    bt = min(bt, T)
    has_res = dres is not None

    def body(x_ref, w_ref, dh_ref, *rest):
        dres_ref = rest[0] if has_res else None
        dx_ref, dw_ref = rest[-2], rest[-1]
        i = pl.program_id(0)
        xv = x_ref[...]
        r = lax.rsqrt(jnp.mean(xv * xv, axis=1, keepdims=True) + EPS)
        xh = xv * r
        dhv = dh_ref[...].astype(F32)
        dxh = dhv * w_ref[...]
        dx = r * (dxh - xh * jnp.mean(dxh * xh, axis=1, keepdims=True))
        if has_res:
            dx = dx + dres_ref[...]
        dx_ref[...] = dx

        @pl.when(i == 0)
        def _():
            dw_ref[...] = jnp.zeros_like(dw_ref)

        dw_ref[...] += jnp.sum(dhv * xh, axis=0, keepdims=True)

    row = pl.BlockSpec((bt, Dm), lambda i: (i, 0))
    vec = pl.BlockSpec((1, Dm), lambda i: (0, 0))
    ins = [x, w.reshape(1, Dm), dh] + ([dres] if has_res else [])
    dx, dw = pl.pallas_call(
        body, grid=(T // bt,), in_specs=[row, vec, row] + ([row] if has_res else []),
        out_specs=[row, vec],
        out_shape=[jax.ShapeDtypeStruct((T, Dm), F32), jax.ShapeDtypeStruct((1, Dm), F32)],
        name=name, compiler_params=_params(("arbitrary",)))(*ins)
    return dx, dw.reshape(Dm)


def _attn_fwd(q, km, vm, *, bt=256):
    T = q.shape[0]
    M = km.shape[0]
    bt = min(bt, T)
    scale = MEM_HD ** -0.5

    def body(q_ref, k_ref, v_ref, o_ref):
        sls = [slice(h * MEM_HD, (h + 1) * MEM_HD) for h in range(MEM_H)]
        ss = [_dot(q_ref[:, sl], k_ref[:, sl], _NT) * scale for sl in sls]
        es = [jnp.exp(s - jnp.max(s, axis=1, keepdims=True)) for s in ss]
        ps = [e / jnp.sum(e, axis=1, keepdims=True) for e in es]
        for sl, p in zip(sls, ps):
            o_ref[:, sl] = _dot(p, v_ref[:, sl]).astype(o_ref.dtype)

    row = pl.BlockSpec((bt, D), lambda i: (i, 0))
    mem = pl.BlockSpec((M, D), lambda i: (0, 0))
    return pl.pallas_call(
        body, grid=(T // bt,), in_specs=[row, mem, mem], out_specs=row,
        out_shape=jax.ShapeDtypeStruct((T, D), BF16), name="attn_fwd",
        compiler_params=_params(("parallel",)))(q, km, vm)


def _attn_bwd(q, km, vm, do, *, bt=256):
    T = q.shape[0]
    M = km.shape[0]
    bt = min(bt, T)
    scale = MEM_HD ** -0.5

    def body(q_ref, k_ref, v_ref, do_ref, dq_ref, dk_ref, dv_ref):
        i = pl.program_id(0)

        @pl.when(i == 0)
        def _():
            dk_ref[...] = jnp.zeros_like(dk_ref)
            dv_ref[...] = jnp.zeros_like(dv_ref)

        sls = [slice(h * MEM_HD, (h + 1) * MEM_HD) for h in range(MEM_H)]
        ss = [_dot(q_ref[:, sl], k_ref[:, sl], _NT) * scale for sl in sls]
        dps = [_dot(do_ref[:, sl], v_ref[:, sl], _NT) for sl in sls]
        es = [jnp.exp(s - jnp.max(s, axis=1, keepdims=True)) for s in ss]
        ps = [e / jnp.sum(e, axis=1, keepdims=True) for e in es]
        dss = [p * (dp - jnp.sum(dp * p, axis=1, keepdims=True)) * scale for p, dp in zip(ps, dps)]
        for sl, p, ds in zip(sls, ps, dss):
            dq_ref[:, sl] = _dot(ds, k_ref[:, sl]).astype(dq_ref.dtype)
            dk_ref[:, sl] += _dot(ds, q_ref[:, sl], _TN)
            dv_ref[:, sl] += _dot(p, do_ref[:, sl], _TN)

    row = pl.BlockSpec((bt, D), lambda i: (i, 0))
    mem = pl.BlockSpec((M, D), lambda i: (0, 0))
    return pl.pallas_call(
        body, grid=(T // bt,), in_specs=[row, mem, mem, row], out_specs=[row, mem, mem],
        out_shape=[jax.ShapeDtypeStruct((T, D), BF16), jax.ShapeDtypeStruct((M, D), F32),
                   jax.ShapeDtypeStruct((M, D), F32)],
        name="attn_bwd", compiler_params=_params(("arbitrary",)))(q, km, vm, do)


def _conv_apply(halo, x, w_ref, b_ref):
    bt, hr = x.shape[0], halo.shape[0]
    cat = jnp.concatenate([halo, x], axis=0)
    y = x * w_ref[3:4, :]
    for k in range(CONV_K - 1):
        y = y + pltpu.roll(cat, CONV_K - 1 - k, 0)[hr:hr + bt] * w_ref[k:k + 1, :]
    if b_ref is not None:
        y = y + b_ref[...]
    return y


def _l2_parts(act, bc):
    out = []
    for s in range(bc // 128):
        a = act[:, s * 128:(s + 1) * 128]
        r = lax.rsqrt(jnp.sum(a * a, axis=1, keepdims=True) + EPS)
        out.append((a, r))
    return out


def _conv_fwd(p, col0, C, w, b, *, l2, name, bt=512, bc=1024):
    T = p.shape[0]
    bt = min(bt, T)
    c0, hb = col0 // bc, bt // P_HALO
    has_b = b is not None
    assert not l2 or (bc == D and C == 2 * D)

    def body(x_ref, halo_ref, w_ref, *rest):
        b_ref = rest[0] if has_b else None
        o_ref = rest[-1]
        i, j = pl.program_id(0), pl.program_id(1)
        x = x_ref[...].astype(F32)
        halo = jnp.where(i > 0, halo_ref[...].astype(F32), 0.0)
        act = _silu(_conv_apply(halo, x, w_ref, b_ref))
        if l2:
            sc = jnp.where(j == 0, GDN_DK ** -0.5, 1.0)
            o_ref[...] = jnp.concatenate([a * (r * sc) for a, r in _l2_parts(act, bc)], axis=1)
        else:
            o_ref[...] = act

    in_specs = [pl.BlockSpec((bt, bc), lambda i, j: (i, c0 + j)),
                pl.BlockSpec((P_HALO, bc), lambda i, j: (jnp.maximum(i * hb - 1, 0), c0 + j)),
                pl.BlockSpec((CONV_K, bc), lambda i, j: (0, j))]
    ins = [p, p, w]
    if has_b:
        in_specs.append(pl.BlockSpec((1, bc), lambda i, j: (0, j)))
        ins.append(b.reshape(1, C))
    return pl.pallas_call(
        body, grid=(T // bt, C // bc), in_specs=in_specs,
        out_specs=pl.BlockSpec((bt, bc), lambda i, j: (i, j)),
        out_shape=jax.ShapeDtypeStruct((T, C), F32), name=name,
        compiler_params=_params(("parallel", "parallel")))(*ins)


def _conv_bwd_act(p, col0, C, w, b, dact, dcol0, *, l2, name, bt=512, bc=1024):
    T = p.shape[0]
    bt = min(bt, T)
    c0, d0, hb = col0 // bc, dcol0 // bc, bt // P_HALO
    has_b = b is not None
    assert not l2 or (bc == D and C == 2 * D)

    def body(x_ref, halo_ref, w_ref, *rest):
        b_ref = rest[0] if has_b else None
        dact_ref, dy_ref, dw_ref, db_ref = rest[-4:]
        j, i = pl.program_id(0), pl.program_id(1)
        x = x_ref[...].astype(F32)
        halo = jnp.where(i > 0, halo_ref[...].astype(F32), 0.0)
        y = _conv_apply(halo, x, w_ref, b_ref)
        dact = dact_ref[...]
        sg = _sigmoid(y)
        if l2:
            sc = jnp.where(j == 0, GDN_DK ** -0.5, 1.0)
            parts = []
            for s, (a, r) in enumerate(_l2_parts(y * sg, bc)):
                n = a * r
                dn = dact[:, s * 128:(s + 1) * 128]
                parts.append((r * sc) * (dn - n * jnp.sum(dn * n, axis=1, keepdims=True)))
            dact = jnp.concatenate(parts, axis=1)
        dy = dact * (sg * (1.0 + y * (1.0 - sg)))
        dy_ref[...] = dy

        @pl.when(i == 0)
        def _():
            dw_ref[...] = jnp.zeros_like(dw_ref)
            db_ref[...] = jnp.zeros_like(db_ref)

        db_ref[...] += jnp.sum(dy, axis=0, keepdims=True)
        cat = jnp.concatenate([halo, x], axis=0)
        dw_ref[3:4, :] += jnp.sum(dy * x, axis=0, keepdims=True)
        for k in range(CONV_K - 1):
            xs = pltpu.roll(cat, CONV_K - 1 - k, 0)[P_HALO:P_HALO + bt]
            dw_ref[k:k + 1, :] += jnp.sum(dy * xs, axis=0, keepdims=True)

    in_specs = [pl.BlockSpec((bt, bc), lambda j, i: (i, c0 + j)),
                pl.BlockSpec((P_HALO, bc), lambda j, i: (jnp.maximum(i * hb - 1, 0), c0 + j)),
                pl.BlockSpec((CONV_K, bc), lambda j, i: (0, j))]
    ins = [p, p, w]
    if has_b:
        in_specs.append(pl.BlockSpec((1, bc), lambda j, i: (0, j)))
        ins.append(b.reshape(1, C))
    in_specs.append(pl.BlockSpec((bt, bc), lambda j, i: (i, d0 + j)))
    ins.append(dact)
    dy, dw, db = pl.pallas_call(
        body, grid=(C // bc, T // bt), in_specs=in_specs,
        out_specs=[pl.BlockSpec((bt, bc), lambda j, i: (i, j)),
                   pl.BlockSpec((CONV_K, bc), lambda j, i: (0, j)),
                   pl.BlockSpec((1, bc), lambda j, i: (0, j))],
        out_shape=[jax.ShapeDtypeStruct((T, C), F32), jax.ShapeDtypeStruct((CONV_K, C), F32),
                   jax.ShapeDtypeStruct((1, C), F32)],
        name=name, compiler_params=_params(("parallel", "arbitrary")))(*ins)
    return dy, dw, db.reshape(C)


def _conv_bwd_in(dy, w, dp_in, col0, T, *, name, bt=512, bc=1024):
    C = dy.shape[1]
    bt = min(bt, T)
    c0, hb, nb = col0 // bc, bt // 8, T // bt

    def body(dy_ref, nxt_ref, w_ref, *rest):
        o_ref = rest[-1]
        i = pl.program_id(0)
        dy_v = dy_ref[...]
        nxt = jnp.where(i < nb - 1, nxt_ref[...], 0.0)
        cat = jnp.concatenate([dy_v, nxt], axis=0)
        dx = dy_v * w_ref[3:4, :]
        for k in range(CONV_K - 1):
            s = CONV_K - 1 - k
            dx = dx + pltpu.roll(cat, bt + 8 - s, 0)[0:bt] * w_ref[k:k + 1, :]
        o_ref[...] = dx.astype(o_ref.dtype)

    in_specs = [pl.BlockSpec((bt, bc), lambda i, j: (i, j)),
                pl.BlockSpec((8, bc), lambda i, j: (jnp.minimum((i + 1) * hb, T // 8 - 1), j)),
                pl.BlockSpec((CONV_K, bc), lambda i, j: (0, j))]
    ins = [dy, dy, w]
    alias = {}
    if dp_in is not None:
        in_specs.append(pl.BlockSpec(memory_space=pl.ANY))
        ins.append(dp_in)
        alias = {3: 0}
    return pl.pallas_call(
        body, grid=(nb, C // bc), in_specs=in_specs,
        out_specs=pl.BlockSpec((bt, bc), lambda i, j: (i, c0 + j)),
        out_shape=jax.ShapeDtypeStruct((T, C_TOT), BF16), input_output_aliases=alias, name=name,
        compiler_params=_params(("parallel", "parallel")))(*ins)


def _expand_mats(shift, row0):
    e = (_iota2((128, D), 0) - row0 == (_iota2((128, D), 1) >> shift)).astype(F32)
    et = ((_iota2((D, 128), 0) >> shift) == _iota2((D, 128), 1) - row0).astype(F32)
    return e, et


def _cum_mats(chunk):
    ri, ci = _iota2((chunk, chunk), 0), _iota2((chunk, chunk), 1)
    return (ri >= ci).astype(F32), (ri <= ci).astype(F32)


def _gdn_gates_fwd(p, alog_row, dtb_row, *, bt=256):
    T = p.shape[0]
    bt = min(bt, T)

    def body(g_ref, al_ref, db_ref, bg_ref):
        gt = g_ref[...]
        lc, _ = _cum_mats(GDN_C)
        g_l = -jnp.exp(al_ref[...]) * _softplus(gt + db_ref[...])
        bg_ref[...] = jnp.where(_iota2((bt, 128), 1) < GDN_H, _sigmoid(gt), _chunk_cumsum(g_l, lc, GDN_C))

    vec = pl.BlockSpec((1, 128), lambda i: (0, 0))
    seg = pl.BlockSpec((bt, 128), lambda i: (i, 0))
    return pl.pallas_call(
        body, grid=(T // bt,), in_specs=[seg, vec, vec], out_specs=seg,
        out_shape=jax.ShapeDtypeStruct((T, 128), F32), name="gdn_gates_fwd",
        compiler_params=_params(("parallel",)))(p, alog_row, dtb_row)


def _gdn_gates_bwd(p, alog_row, dtb_row, dbg, dp_in, *, bt=256):
    T = p.shape[0]
    bt = min(bt, T)

    def body(g_ref, al_ref, db_ref, dbg_ref, dpin_ref, dg_out, dal_ref, ddb_ref):
        i = pl.program_id(0)
        gt = g_ref[...]
        lane = _iota2((bt, 128), 1)
        _, uc = _cum_mats(GDN_C)
        ea = jnp.exp(al_ref[...])
        zz = gt + db_ref[...]
        g_l = -ea * _softplus(zz)
        beta_l = _sigmoid(gt)
        dbg_v = dbg_ref[...]
        dg_l = jnp.where((lane >= GDN_H) & (lane < 2 * GDN_H), _chunk_cumsum(dbg_v, uc, GDN_C), 0.0)
        dbeta_l = jnp.where(lane < GDN_H, dbg_v, 0.0)
        da = dg_l * (-ea) * _sigmoid(zz)
        dg_out[...] = (da + dbeta_l * beta_l * (1.0 - beta_l)).astype(dg_out.dtype)

        @pl.when(i == 0)
        def _():
            dal_ref[...] = jnp.zeros_like(dal_ref)
            ddb_ref[...] = jnp.zeros_like(ddb_ref)

        dal_ref[...] += jnp.sum(dg_l * g_l, axis=0, keepdims=True)
        ddb_ref[...] += jnp.sum(da, axis=0, keepdims=True)

    vec = pl.BlockSpec((1, 128), lambda i: (0, 0))
    seg = pl.BlockSpec((bt, 128), lambda i: (i, 0))
    gate = pl.BlockSpec((bt, 128), lambda i: (i, C_GATE // 128))
    return pl.pallas_call(
        body, grid=(T // bt,), in_specs=[seg, vec, vec, seg, _ANY], out_specs=[gate, vec, vec],
        out_shape=[jax.ShapeDtypeStruct((T, C_TOT), BF16), jax.ShapeDtypeStruct((1, 128), F32),
                   jax.ShapeDtypeStruct((1, 128), F32)],
        input_output_aliases={4: 0}, name="gdn_gates_bwd",
        compiler_params=_params(("arbitrary",)))(p, alog_row, dtb_row, dbg, dp_in)


def _ssd_dt_fwd(p, dtb_row, alog_row, *, bt=256):
    T = p.shape[0]
    bt = min(bt, T)

    def body(d_ref, db_ref, al_ref, da_ref):
        lc, _ = _cum_mats(SSM_L)
        dt_l = _softplus(d_ref[...] + db_ref[...])
        alpha_l = _chunk_cumsum(dt_l * (-jnp.exp(al_ref[...])), lc, SSM_L)
        da_ref[...] = jnp.where(_iota2((bt, 128), 1) < SSM_H, dt_l, pltpu.roll(alpha_l, SSM_H, 1))

    v128 = pl.BlockSpec((1, 128), lambda i: (0, 0))
    return pl.pallas_call(
        body, grid=(T // bt,), in_specs=[pl.BlockSpec((bt, 128), lambda i: (i, 1)), v128, v128],
        out_specs=pl.BlockSpec((bt, 128), lambda i: (i, 0)), out_shape=jax.ShapeDtypeStruct((T, 128), F32),
        name="ssd_dt_fwd", compiler_params=_params(("parallel",)))(p, dtb_row, alog_row)


def _ssd_dt_bwd(p, dtb_row, alog_row, dda, dp_in, *, bt=256):
    T = p.shape[0]
    bt = min(bt, T)

    def body(d_ref, db_ref, al_ref, dda_ref, dpin_ref, dd_out, ddb_ref, dalog_ref):
        i = pl.program_id(0)
        heads = _iota2((bt, 128), 1) < SSM_H
        _, uc = _cum_mats(SSM_L)
        zz = d_ref[...] + db_ref[...]
        dt_l = _softplus(zz)
        a_row = -jnp.exp(al_ref[...])
        dda_v = dda_ref[...]
        da_l = _chunk_cumsum(jnp.where(heads, pltpu.roll(dda_v, 128 - SSM_H, 1), 0.0), uc, SSM_L)
        draw = jnp.where(heads, (dda_v + da_l * a_row) * _sigmoid(zz), 0.0)
        dd_out[...] = draw.astype(dd_out.dtype)

        @pl.when(i == 0)
        def _():
            ddb_ref[...] = jnp.zeros_like(ddb_ref)
            dalog_ref[...] = jnp.zeros_like(dalog_ref)

        ddb_ref[...] += jnp.sum(draw, axis=0, keepdims=True)
        dalog_ref[...] += jnp.sum(da_l * dt_l, axis=0, keepdims=True) * a_row

    seg = pl.BlockSpec((bt, 128), lambda i: (i, C_DT // 128))
    v128 = pl.BlockSpec((1, 128), lambda i: (0, 0))
    return pl.pallas_call(
        body, grid=(T // bt,),
        in_specs=[pl.BlockSpec((bt, 128), lambda i: (i, 1)), v128, v128, pl.BlockSpec((bt, 128), lambda i: (i, 0)), _ANY],
        out_specs=[seg, v128, v128],
        out_shape=[jax.ShapeDtypeStruct((T, C_TOT), BF16), jax.ShapeDtypeStruct((1, 128), F32),
                   jax.ShapeDtypeStruct((1, 128), F32)],
        input_output_aliases={4: 0}, name="ssd_dt_bwd",
        compiler_params=_params(("arbitrary",)))(p, dtb_row, alog_row, dda, dp_in)


def _gdn_post_fwd(o, p, w_x, *, bt=256):
    T = o.shape[0]
    bt = min(bt, T)

    def body(o_ref, z_ref, w_ref, out_ref):
        for h in range(GDN_H):
            sl = slice(h * 128, (h + 1) * 128)
            oh = o_ref[:, sl].astype(F32)
            r = lax.rsqrt(jnp.mean(oh * oh, axis=1, keepdims=True) + EPS)
            out_ref[:, sl] = (oh * r * w_ref[:, sl] * _silu(z_ref[:, sl].astype(F32))).astype(out_ref.dtype)

    row = pl.BlockSpec((bt, D), lambda i: (i, 0))
    return pl.pallas_call(
        body, grid=(T // bt,),
        in_specs=[row, pl.BlockSpec((bt, D), lambda i: (i, C_ZG // D)), pl.BlockSpec((1, D), lambda i: (0, 0))],
        out_specs=row, out_shape=jax.ShapeDtypeStruct((T, 2 * D), BF16), name="gdn_post_fwd",
        compiler_params=_params(("parallel",)))(o, p, w_x)


def _gdn_post_bwd(dx1, w_out, o, p, w_x, *, bt=512):
    T = o.shape[0]
    bt = min(bt, T)

    def body(dx_ref, wo_ref, o_ref, z_ref, w_ref, do_ref, dz_ref, dw_ref):
        i = pl.program_id(0)

        @pl.when(i == 0)
        def _():
            dw_ref[...] = jnp.zeros_like(dw_ref)

        dmix = _dot(dx_ref[...], wo_ref[...], _NT)
        for h in range(GDN_H):
            sl = slice(h * 128, (h + 1) * 128)
            oh, zh, wh = o_ref[:, sl].astype(F32), z_ref[:, sl].astype(F32), w_ref[:, sl]
            dm = dmix[:, sl]
            r = lax.rsqrt(jnp.mean(oh * oh, axis=1, keepdims=True) + EPS)
            ohat = oh * r
            dy = dm * _silu(zh)
            dz_ref[:, sl] = (dm * ohat * wh * _dsilu(zh)).astype(dz_ref.dtype)
            dohat = dy * wh
            do_ref[:, sl] = (r * (dohat - ohat * jnp.mean(dohat * ohat, axis=1, keepdims=True))).astype(do_ref.dtype)
            dw_ref[:, sl] += jnp.sum(dy * ohat, axis=0, keepdims=True)

    row = pl.BlockSpec((bt, D), lambda i: (i, 0))
    zcol = pl.BlockSpec((bt, D), lambda i: (i, C_ZG // D))
    vec = pl.BlockSpec((1, D), lambda i: (0, 0))
    return pl.pallas_call(
        body, grid=(T // bt,), in_specs=[row, pl.BlockSpec((D, D), lambda i: (0, 0)), row, zcol, vec],
        out_specs=[row, zcol, vec],
        out_shape=[jax.ShapeDtypeStruct((T, D), BF16), jax.ShapeDtypeStruct((T, C_TOT), BF16),
                   jax.ShapeDtypeStruct((1, D), F32)],
        name="gdn_post_bwd", compiler_params=_params(("arbitrary",)))(dx1, w_out, o, p, w_x)


def _ssd_post_fwd(y, xs, p, d_x, w, mix_in, *, bt=256):
    T = y.shape[0]
    bt = min(bt, T)

    def body(y_ref, x_ref, z_ref, d_ref, w_ref, mix_ref, out_ref):
        yg = (y_ref[...].astype(F32) + x_ref[...] * d_ref[...]) * _silu(z_ref[...].astype(F32))
        for g in range(2):
            sl = slice(g * 512, (g + 1) * 512)
            a = yg[:, sl]
            r = lax.rsqrt(jnp.mean(a * a, axis=1, keepdims=True) + EPS)
            out_ref[:, sl] = (a * r * w_ref[:, sl]).astype(out_ref.dtype)

    row = pl.BlockSpec((bt, D), lambda i: (i, 0))
    vec = pl.BlockSpec((1, D), lambda i: (0, 0))
    return pl.pallas_call(
        body, grid=(T // bt,),
        in_specs=[row, row, pl.BlockSpec((bt, D), lambda i: (i, C_ZS // D)), vec, vec, _ANY],
        out_specs=pl.BlockSpec((bt, D), lambda i: (i, 1)), out_shape=jax.ShapeDtypeStruct((T, 2 * D), BF16),
        input_output_aliases={5: 0}, name="ssd_post_fwd",
        compiler_params=_params(("parallel",)))(y, xs, p, d_x, w, mix_in)


def _ssd_post_bwd(dx1, w_out, y, xs, p, d_x, w, dp_in, *, bt=512):
    T = y.shape[0]
    bt = min(bt, T)

    def body(dx_ref, wo_ref, y_ref, x_ref, z_ref, d_ref, w_ref, dpin_ref, dyy_ref, dz_ref, dd_ref, dw_ref):
        i = pl.program_id(0)

        @pl.when(i == 0)
        def _():
            dd_ref[...] = jnp.zeros_like(dd_ref)
            dw_ref[...] = jnp.zeros_like(dw_ref)

        dmix = _dot(dx_ref[...], wo_ref[...], _NT)
        xv, zv = x_ref[...], z_ref[...].astype(F32)
        yy = y_ref[...].astype(F32) + xv * d_ref[...]
        sz = _silu(zv)
        yg = yy * sz
        parts = []
        for g in range(2):
            sl = slice(g * 512, (g + 1) * 512)
            a = yg[:, sl]
            r = lax.rsqrt(jnp.mean(a * a, axis=1, keepdims=True) + EPS)
            ah = a * r
            dout = dmix[:, sl]
            dah = dout * w_ref[:, sl]
            dw_ref[:, sl] += jnp.sum(dout * ah, axis=0, keepdims=True)
            parts.append(r * (dah - ah * jnp.mean(dah * ah, axis=1, keepdims=True)))
        dyg = jnp.concatenate(parts, axis=1)
        dyy = dyg * sz
        dyy_ref[...] = dyy
        dz_ref[...] = (dyg * yy * _dsilu(zv)).astype(dz_ref.dtype)
        dd_ref[...] += jnp.sum(dyy * xv, axis=0, keepdims=True)

    row = pl.BlockSpec((bt, D), lambda i: (i, 0))
    zcol = pl.BlockSpec((bt, D), lambda i: (i, C_ZS // D))
    vec = pl.BlockSpec((1, D), lambda i: (0, 0))
    return pl.pallas_call(
        body, grid=(T // bt,),
        in_specs=[row, pl.BlockSpec((D, D), lambda i: (1, 0)), row, row, zcol, vec, vec, _ANY],
        out_specs=[row, zcol, vec, vec],
        out_shape=[jax.ShapeDtypeStruct((T, D), F32), jax.ShapeDtypeStruct((T, C_TOT), BF16),
                   jax.ShapeDtypeStruct((1, D), F32), jax.ShapeDtypeStruct((1, D), F32)],
        input_output_aliases={7: 1}, name="ssd_post_bwd",
        compiler_params=_params(("arbitrary",)))(dx1, w_out, y, xs, p, d_x, w, dp_in)


_NEG = -1e30


def _gdn_terms(q, k, v, bx, gam_c):
    C = GDN_C
    ri, ci = _iota2((C, C), 0), _iota2((C, C), 1)
    eye, low, strict = ri == ci, ri >= ci, ri > ci
    gam_r = jnp.sum(jnp.where(eye, gam_c, 0.0), axis=0, keepdims=True)
    G = jnp.exp(jnp.where(low, gam_c - gam_r, _NEG))
    glast = jnp.sum(jnp.where(_iota2((C, 1), 0) == C - 1, gam_c, 0.0), axis=0, keepdims=True)
    eg, egl, eL = jnp.exp(gam_c), jnp.exp(glast - gam_c), jnp.exp(glast)
    kb, vb = k * bx, v * bx
    M = _dot(kb, k, _NT)
    return dict(eye=eye, low=low, strict=strict, G=G, eg=eg, egl=egl, eL=eL, kb=kb, vb=vb, M=M,
                kbg=kb * eg, qd=q * eg, kd=k * egl, q=q, k=k, v=v, bx=bx)


def _split(a):
    hi = a.astype(_MXU)
    return hi, (a - hi.astype(F32)).astype(_MXU)


def _dot3s(a, b):
    d = lambda p, q: lax.dot_general(p, q, _NN, preferred_element_type=F32)
    return d(a[0], b[0]) + d(a[0], b[1]) + d(a[1], b[0])


def _tri_inv_many(Ls, eye):
    eyef = jnp.where(eye, 1.0, 0.0)
    Ts = [eyef - L for L in Ls]
    Ps = [-L for L in Ls]
    for _ in range(5):
        sp = [_split(p) for p in Ps]
        Ps = [_dot3s(s, s) for s in sp]
        sp = [_split(p) for p in Ps]
        st = [_split(t) for t in Ts]
        Ts = [t + _dot3s(a, b) for t, a, b in zip(Ts, st, sp)]
    return Ts


def _lane_col(tile, idx):
    return jnp.sum(jnp.where(_iota2(tile.shape, 1) == idx, tile, 0.0), axis=1, keepdims=True)


def _gdn_heads(q_ref, k_ref, v_ref, bg_ref, heads):
    out = []
    bg = bg_ref[...]
    for h in heads:
        sl = slice(h * 128, (h + 1) * 128)
        out.append(_gdn_terms(q_ref[:, sl], k_ref[:, sl], v_ref[:, sl], _lane_col(bg, h), _lane_col(bg, GDN_H + h)))
    return out


def _gdn_prep(qk, v, bg, ride=None):
    T = qk.shape[0]
    N = T // GDN_C
    C, CS = GDN_C, GDN_LOCAL_CHUNKS
    NB = N // CS
    n_ride = ride.n if ride else 0

    def body(q_ref, k_ref, v_ref, bg_ref, *rest):
        ride_in = rest[:n_ride]
        u_ref, w_ref, qd_ref, kd_ref, p_ref, t_ref = rest[n_ride:n_ride + 6]
        ride_out = rest[n_ride + 6:2 * n_ride + 6]
        if ride:
            @pl.when(pl.program_id(0) == 0)
            def _():
                ride.start(ride_in, ride_out, rest[-3:])

            @pl.when(pl.program_id(0) == NB - 1)
            def _():
                ride.wait(ride_in, ride_out, rest[-3:])

        items = [(c, h) for c in range(CS) for h in range(GDN_H)]
        views = [[r.at[pl.ds(c * C, C)] for r in (q_ref, k_ref, v_ref, bg_ref)] for c in range(CS)]
        ts = [_gdn_heads(*views[c], [h])[0] for c, h in items]
        Ts = _tri_inv_many([jnp.where(t["strict"], t["M"] * t["G"], 0.0) for t in ts], ts[0]["eye"])
        for (c, h), t, Tm in zip(items, ts, Ts):
            tok = slice(c * C, (c + 1) * C)
            sl = slice(h * 128, (h + 1) * 128)
            rows = slice(h * C, (h + 1) * C)
            u_ref[tok, sl] = _dot(Tm, t["vb"])
            w_ref[tok, sl] = _dot(Tm, t["kbg"]).astype(w_ref.dtype)
            qd_ref[tok, sl] = t["qd"].astype(qd_ref.dtype)
            kd_ref[tok, sl] = t["kd"].astype(kd_ref.dtype)
            p_ref[c, rows, :] = _dot(t["q"], t["k"], _NT) * t["G"]
            t_ref[c, rows, :] = Tm

    blk = lambda c: pl.BlockSpec((CS * C, D), lambda n: (n, c))
    sq = pl.BlockSpec((CS, GDN_H * C, C), lambda n: (n, 0, 0))
    in_specs = [blk(0), blk(1), blk(0), pl.BlockSpec((CS * C, 128), lambda n: (n, 0))]
    out_specs = [blk(0), blk(0), blk(0), blk(0), sq, sq]
    out_shape = [jax.ShapeDtypeStruct((T, D), F32), jax.ShapeDtypeStruct((T, D), BF16),
                 jax.ShapeDtypeStruct((T, D), BF16), jax.ShapeDtypeStruct((T, D), BF16),
                 jax.ShapeDtypeStruct((N, GDN_H * C, C), F32), jax.ShapeDtypeStruct((N, GDN_H * C, C), F32)]
    ins = [qk, qk, v, bg]
    if ride:
        ins, in_specs = ins + ride.srcs, in_specs + ride.specs
        out_shape, out_specs = out_shape + ride.out_shape, out_specs + ride.specs
    res = pl.pallas_call(
        body, grid=(NB,), in_specs=in_specs, out_specs=out_specs, out_shape=out_shape,
        scratch_shapes=ride.scratch if ride else [], name="gdn_prep",
        compiler_params=_params(("arbitrary",) if ride else ("parallel",)))(*ins)
    return (list(res[:6]), list(res[6:])) if ride else list(res)


def _gdn_scan_fwd(u, w, qd, kd, pm, bg):
    T = u.shape[0]
    N = T // GDN_C
    C, CS = GDN_C, GDN_SCAN_CHUNKS

    def body(u_ref, w_ref, qd_ref, kd_ref, p_ref, bg_ref, o_ref, vn_ref, ss_ref, S_scr):
        n = pl.program_id(0)

        @pl.when(n == 0)
        def _():
            S_scr[...] = jnp.zeros_like(S_scr)

        sls = [slice(h * 128, (h + 1) * 128) for h in range(GDN_H)]
        for c in range(CS):
            rows = slice(c * C, (c + 1) * C)
            glast = bg_ref[(c + 1) * C - 1:(c + 1) * C, :]
            Ss = [S_scr[:, sl] for sl in sls]
            vns = [u_ref[rows, sl] - _dot(w_ref[rows, sl], S) for sl, S in zip(sls, Ss)]
            for h, (sl, S, vn) in enumerate(zip(sls, Ss, vns)):
                ss_ref[c, :, sl] = S.astype(ss_ref.dtype)
                vn_ref[rows, sl] = vn.astype(vn_ref.dtype)
                o_ref[rows, sl] = (_dot(qd_ref[rows, sl], S)
                                   + _dot(p_ref[c, h * C:(h + 1) * C, :], vn)).astype(o_ref.dtype)
                S_scr[:, sl] = S * jnp.exp(_lane_col(glast, GDN_H + h)) + _dot(kd_ref[rows, sl], vn, _TN)

    blk = pl.BlockSpec((CS * C, D), lambda n: (n, 0))
    return dict(
        body=body, steps=N // CS, ins=[u, w, qd, kd, pm, bg],
        in_specs=[blk, blk, blk, blk, pl.BlockSpec((CS, GDN_H * C, C), lambda n: (n, 0, 0)),
                  pl.BlockSpec((CS * C, 128), lambda n: (n, 0))],
        out_specs=[blk, blk, pl.BlockSpec((CS, GDN_DK, D), lambda n: (n, 0, 0))],
        out_shape=[jax.ShapeDtypeStruct((T, D), BF16), jax.ShapeDtypeStruct((T, D), BF16),
                   jax.ShapeDtypeStruct((N, GDN_DK, D), BF16)],
        scratch=[pltpu.VMEM((GDN_DK, D), F32)])


def _gdn_scan_bwd(w, qd, kd, pm, bg, do):
    T = w.shape[0]
    N = T // GDN_C
    C, CS = GDN_C, GDN_SCAN_CHUNKS
    NB = N // CS

    def body(w_ref, qd_ref, kd_ref, p_ref, bg_ref, do_ref, dvn_ref, ds_ref, dS_scr):
        n = pl.program_id(0)

        @pl.when(n == 0)
        def _():
            dS_scr[...] = jnp.zeros_like(dS_scr)

        sls = [slice(h * 128, (h + 1) * 128) for h in range(GDN_H)]
        for c in reversed(range(CS)):
            rows = slice(c * C, (c + 1) * C)
            glast = bg_ref[(c + 1) * C - 1:(c + 1) * C, :]
            dSs = [dS_scr[:, sl] for sl in sls]
            dvns = [_dot(p_ref[c, h * C:(h + 1) * C, :], do_ref[rows, sl], _TN) + _dot(kd_ref[rows, sl], dS2)
                    for h, (sl, dS2) in enumerate(zip(sls, dSs))]
            for h, (sl, dS2, dvn) in enumerate(zip(sls, dSs, dvns)):
                ds_ref[c, :, sl] = dS2.astype(ds_ref.dtype)
                dvn_ref[rows, sl] = dvn.astype(dvn_ref.dtype)
                dS_scr[:, sl] = (dS2 * jnp.exp(_lane_col(glast, GDN_H + h))
                                 + _dot(qd_ref[rows, sl], do_ref[rows, sl], _TN) - _dot(w_ref[rows, sl], dvn, _TN))

    blk = pl.BlockSpec((CS * C, D), lambda n: (NB - 1 - n, 0))
    return dict(
        body=body, steps=NB, ins=[w, qd, kd, pm, bg, do],
        in_specs=[blk, blk, blk, pl.BlockSpec((CS, GDN_H * C, C), lambda n: (NB - 1 - n, 0, 0)),
                  pl.BlockSpec((CS * C, 128), lambda n: (NB - 1 - n, 0)), blk],
        out_specs=[blk, pl.BlockSpec((CS, GDN_DK, D), lambda n: (NB - 1 - n, 0, 0))],
        out_shape=[jax.ShapeDtypeStruct((T, D), BF16), jax.ShapeDtypeStruct((N, GDN_DK, D), BF16)],
        scratch=[pltpu.VMEM((GDN_DK, D), F32)])


def _gdn_rest_bwd(qk, v, bg, s_save, t_save, vn, dvn, ds_save, do, ride=None):
    T = qk.shape[0]
    N = T // GDN_C
    C, CS = GDN_C, GDN_REST_CHUNKS
    NB = N // CS
    n_ride = ride.n if ride else 0

    def body(q_ref, k_ref, v_ref, bg_ref, ss_ref, ts_ref, vn_ref, dvn_ref, ds_ref, do_ref, *rest):
        ride_in = rest[:n_ride]
        dqkv_ref, dbg_ref = rest[n_ride:n_ride + 2]
        ride_out = rest[n_ride + 2:2 * n_ride + 2]
        if ride:
            @pl.when(pl.program_id(0) == 0)
            def _():
                ride.start(ride_in, ride_out, rest[-3:])

            @pl.when(pl.program_id(0) == NB - 1)
            def _():
                ride.wait(ride_in, ride_out, rest[-3:])

        items = [(c, h) for c in range(CS) for h in range(GDN_H)]
        toks = [slice(c * C, (c + 1) * C) for c, _ in items]
        sls = [slice(h * 128, (h + 1) * 128) for _, h in items]
        views = [[r.at[pl.ds(c * C, C)] for r in (q_ref, k_ref, v_ref, bg_ref)] for c in range(CS)]
        ts = [_gdn_heads(*views[c], [h])[0] for c, h in items]
        Ss = [ss_ref[c, :, sl] for (c, _), sl in zip(items, sls)]
        Tms = [ts_ref[c, h * C:(h + 1) * C, :] for c, h in items]
        dS2s = [ds_ref[c, :, sl] for (c, _), sl in zip(items, sls)]
        dos = [do_ref[tok, sl] for tok, sl in zip(toks, sls)]
        vns = [vn_ref[tok, sl] for tok, sl in zip(toks, sls)]
        dvns = [dvn_ref[tok, sl] for tok, sl in zip(toks, sls)]
        Qs = [_dot(t["q"], t["k"], _NT) for t in ts]
        dws = [-_dot(dvn, S, _NT) for dvn, S in zip(dvns, Ss)]
        dqds = [_dot(do, S, _NT) for do, S in zip(dos, Ss)]
        dPs = [jnp.where(t["low"], _dot(do, vn, _NT), 0.0) for t, do, vn in zip(ts, dos, vns)]
        dkds = [_dot(vn, dS2, _NT) for vn, dS2 in zip(vns, dS2s)]
        dTs = [_dot(dvn, t["vb"], _NT) + _dot(dw, t["kbg"], _NT) for t, dvn, dw in zip(ts, dvns, dws)]
        dvbs = [_dot(Tm, dvn, _TN) for Tm, dvn in zip(Tms, dvns)]
        dkbgs = [_dot(Tm, dw, _TN) for Tm, dw in zip(Tms, dws)]
        TdTs = [_dot(Tm, dT, _TN) for Tm, dT in zip(Tms, dTs)]
        dLs = [jnp.where(t["strict"], -_dot(TdT, Tm, _NT), 0.0) for t, TdT, Tm in zip(ts, TdTs, Tms)]
        dMs = [dL * t["G"] for t, dL in zip(ts, dLs)]
        dQs = [dP * t["G"] for t, dP in zip(ts, dPs)]
        dkbs = [_dot(dM, t["k"]) + dkbg * t["eg"] for t, dM, dkbg in zip(ts, dMs, dkbgs)]
        rs = lambda a: jnp.sum(a, axis=1, keepdims=True)
        lane = _iota2((C, 128), 1)
        last = _iota2((C, 1), 0) == C - 1
        dbg = [jnp.zeros((C, 128), F32) for _ in range(CS)]
        for i, (c, h) in enumerate(items):
            t, sl, tok = ts[i], sls[i], toks[i]
            E = (dLs[i] * t["M"] + dPs[i] * Qs[i]) * t["G"]
            dqkv_ref[tok, sl] = _dot(dQs[i], t["k"]) + dqds[i] * t["eg"]
            dqkv_ref[tok, D + h * 128:D + (h + 1) * 128] = (
                _dot(dQs[i], t["q"], _TN) + _dot(dMs[i], t["kb"], _TN) + dkds[i] * t["egl"] + dkbs[i] * t["bx"])
            dqkv_ref[tok, 2 * D + h * 128:2 * D + (h + 1) * 128] = dvbs[i] * t["bx"]
            dbeta_c = rs(dkbs[i] * t["k"] + dvbs[i] * t["v"])
            dkd_kd = dkds[i] * t["kd"]
            dgam_c = rs(dqds[i] * t["qd"]) + rs(dkbgs[i] * t["kbg"]) - rs(dkd_kd) + rs(E)
            dgam_r = -jnp.sum(E, axis=0, keepdims=True)
            dgam_c = dgam_c + jnp.sum(jnp.where(t["eye"], dgam_r, 0.0), axis=1, keepdims=True)
            dlast = _sum_all(dkd_kd) + t["eL"] * _sum_all(Ss[i].astype(F32) * dS2s[i].astype(F32))
            dgam_c = dgam_c + jnp.where(last, dlast, 0.0)
            dbg[c] = dbg[c] + jnp.where(lane == h, dbeta_c, 0.0) + jnp.where(lane == GDN_H + h, dgam_c, 0.0)
        for c in range(CS):
            dbg_ref[c * C:(c + 1) * C, :] = dbg[c]

    blk = lambda c: pl.BlockSpec((CS * C, D), lambda n: (n, c))
    st = pl.BlockSpec((CS, GDN_DK, D), lambda n: (n, 0, 0))
    seg = pl.BlockSpec((CS * C, 128), lambda n: (n, 0))
    in_specs = [blk(0), blk(1), blk(0), seg, st,
                pl.BlockSpec((CS, GDN_H * C, C), lambda n: (n, 0, 0)), blk(0), blk(0), st, blk(0)]
    out_specs = [pl.BlockSpec((CS * C, 3 * D), lambda n: (n, 0)), seg]
    out_shape = [jax.ShapeDtypeStruct((T, 3 * D), F32), jax.ShapeDtypeStruct((T, 128), F32)]
    ins = [qk, qk, v, bg, s_save, t_save, vn, dvn, ds_save, do]
    if ride:
        ins, in_specs = ins + ride.srcs, in_specs + ride.specs
        out_shape, out_specs = out_shape + ride.out_shape, out_specs + ride.specs
    res = pl.pallas_call(
        body, grid=(NB,), in_specs=in_specs, out_specs=out_specs, out_shape=out_shape,
        scratch_shapes=ride.scratch if ride else [], name="gdn_rest_bwd",
        compiler_params=_params(("arbitrary",) if ride else ("parallel",)))(*ins)
    return (list(res[:2]), list(res[2:])) if ride else list(res)


def _ssd_seg(al_pair, half, s):
    L = SSM_L
    ri, ci = _iota2((L, L), 0), _iota2((L, L), 1)
    ac = jnp.max(jnp.where(half == s, al_pair, _NEG), axis=1, keepdims=True)
    ar = jnp.sum(jnp.where(ri == ci, ac, 0.0), axis=0, keepdims=True)
    return jnp.exp(jnp.where(ri >= ci, ac - ar, _NEG))


def _last_row(a):
    return jnp.sum(jnp.where(_iota2((a.shape[0], 1), 0) == a.shape[0] - 1, a, 0.0), axis=0, keepdims=True)


def _ssd_expand(da_ref):
    da = da_ref[...]
    return _dot_sel(da, _expand_mats(6, 0)[0]), _dot_sel(da, _expand_mats(6, SSM_H)[0])


def _ssd_core_fwd(xbc, da):
    T = xbc.shape[0]
    L, CS = SSM_L, SSM_SCAN_CHUNKS
    Nc = T // L

    def body(x_all, bc_all, da_all, y_all, hs_all, H_scr):
        @pl.when(pl.program_id(0) == 0)
        def _():
            H_scr[...] = jnp.zeros_like(H_scr)

        for cc in range(CS):
            rows = pl.ds(cc * L, L)
            chunk(x_all.at[rows], bc_all.at[rows], da_all.at[rows], y_all.at[rows], hs_all.at[cc], H_scr)

    def chunk(x_ref, bc_ref, da_ref, y_ref, hs_ref, H_scr):
        dt_ref, al_ref = _ssd_expand(da_ref)
        half = _iota2((L, 128), 1) >> 6
        for g in range(2):
            gs = slice(g * 512, (g + 1) * 512)
            Bg = bc_ref[:, g * 128:(g + 1) * 128]
            Cg = bc_ref[:, 256 + g * 128:256 + (g + 1) * 128]
            alg = al_ref[:, gs]
            alast = _last_row(alg)
            xdt = x_ref[:, gs] * dt_ref[:, gs]
            Hg = H_scr[:, gs]
            hs_ref[:, gs] = Hg
            CB = _dot(Cg, Bg, _NT)
            y_off = jnp.exp(alg) * _dot(Cg, Hg)
            H_scr[:, gs] = Hg * jnp.exp(alast) + _dot(Bg, jnp.exp(alast - alg) * xdt, _TN)
            for j in range(4):
                ps = slice(g * 512 + j * 128, g * 512 + (j + 1) * 128)
                al_pair = al_ref[:, ps]
                xp = x_ref[:, ps] * dt_ref[:, ps]
                ys = [_dot(_ssd_seg(al_pair, half, s) * CB, xp) for s in range(2)]
                y_ref[:, ps] = (y_off[:, j * 128:(j + 1) * 128]
                                + jnp.where(half == 0, ys[0], ys[1])).astype(y_ref.dtype)

    row = pl.BlockSpec((CS * L, D), lambda c: (c, 0))
    return dict(
        body=body, steps=Nc // CS, ins=[xbc, xbc, da],
        in_specs=[row, pl.BlockSpec((CS * L, 512), lambda c: (c, 2)), pl.BlockSpec((CS * L, 128), lambda c: (c, 0))],
        out_specs=[row, pl.BlockSpec((CS, SSM_N, D), lambda c: (c, 0, 0))],
        out_shape=[jax.ShapeDtypeStruct((T, D), BF16), jax.ShapeDtypeStruct((Nc, SSM_N, D), F32)],
        scratch=[pltpu.VMEM((SSM_N, D), F32)])


def _ssd_core_bwd(xbc, da, h_save, dyy, d_x):
    T = xbc.shape[0]
    L, CS = SSM_L, SSM_SCAN_CHUNKS
    Nc = T // L
    NB = Nc // CS

    def body(x_all, bc_all, da_all, hs_all, dy_all, d_ref, dx_all, dda_all, dH_scr, ddt_ref, dal_ref):
        @pl.when(pl.program_id(0) == 0)
        def _():
            dH_scr[...] = jnp.zeros_like(dH_scr)

        for cc in reversed(range(CS)):
            rows = pl.ds(cc * L, L)
            chunk(x_all.at[rows], bc_all.at[rows], da_all.at[rows], hs_all.at[cc], dy_all.at[rows],
                  d_ref, dx_all.at[rows], ddt_ref, dal_ref, dH_scr)
            dda_all[rows, :] = (_dot_sel(ddt_ref[...], _expand_mats(6, 0)[1])
                                + _dot_sel(dal_ref[...], _expand_mats(6, SSM_H)[1]))

    def chunk(x_ref, bc_ref, da_ref, hs_ref, dy_ref, d_ref, dx_ref, ddt_ref, dal_ref, dH_scr):
        dt_ref, al_ref = _ssd_expand(da_ref)
        lane = _iota2((L, 128), 1)
        half = lane >> 6
        rowi = _iota2((L, 1), 0)
        ri, ci = _iota2((L, L), 0), _iota2((L, L), 1)
        for g in range(2):
            gs = slice(g * 512, (g + 1) * 512)
            Bg = bc_ref[:, g * 128:(g + 1) * 128]
            Cg = bc_ref[:, 256 + g * 128:256 + (g + 1) * 128]
            alg = al_ref[:, gs]
            alast = _last_row(alg)
            eal, edec, eL = jnp.exp(alg), jnp.exp(alast - alg), jnp.exp(alast)
            xg, dtg, dYg = x_ref[:, gs], dt_ref[:, gs], dy_ref[:, gs]
            xdt = xg * dtg
            Hg = hs_ref[:, gs]
            dH2 = dH_scr[:, gs]
            CB = _dot(Cg, Bg, _NT)
            dYe = eal * dYg
            dH_scr[:, gs] = dH2 * eL + _dot(Cg, dYe, _TN)
            dC = _dot(dYe, Hg, _NT)
            zg = edec * xdt
            dz = _dot(Bg, dH2)
            dB = _dot(zg, dH2, _NT)
            tz = dz * zg
            dal = dYe * _dot(Cg, Hg) - tz
            dalast = jnp.sum(tz, axis=0, keepdims=True) + eL * jnp.sum(Hg * dH2, axis=0, keepdims=True)
            dal = dal + jnp.where(rowi == L - 1, dalast, 0.0)
            dxdt_g = edec * dz
            dx_ref[:, gs] = dxdt_g * dtg + dYg * d_ref[:, gs]
            ddt_ref[:, gs] = dxdt_g * xg
            dal_ref[:, gs] = dal
            dCB = jnp.zeros((L, L), F32)
            for j in range(4):
                ps = slice(g * 512 + j * 128, g * 512 + (j + 1) * 128)
                al_pair = al_ref[:, ps]
                xp = x_ref[:, ps] * dt_ref[:, ps]
                dYp = dy_ref[:, ps]
                dxp = []
                dal_p = jnp.zeros((L, 128), F32)
                for s in range(2):
                    seg = _ssd_seg(al_pair, half, s)
                    W = seg * CB
                    dW = _dot(jnp.where(half == s, dYp, 0.0), xp, _NT)
                    dxp.append(_dot(W, dYp, _TN))
                    dCB = dCB + dW * seg
                    Es = dW * W
                    dac = jnp.sum(Es, axis=1, keepdims=True) - jnp.sum(
                        jnp.where(ri == ci, jnp.sum(Es, axis=0, keepdims=True), 0.0), axis=1, keepdims=True)
                    dal_p = dal_p + jnp.where(lane == 64 * s, dac, 0.0)
                dxdt_p = jnp.where(half == 0, dxp[0], dxp[1])
                dx_ref[:, ps] += dxdt_p * dt_ref[:, ps]
                ddt_ref[:, ps] += dxdt_p * x_ref[:, ps]
                dal_ref[:, ps] += dal_p
            dx_ref[:, D + g * 128:D + (g + 1) * 128] = dB + _dot(dCB, Cg, _TN)
            dx_ref[:, D + 256 + g * 128:D + 256 + (g + 1) * 128] = dC + _dot(dCB, Bg)

    row = pl.BlockSpec((CS * L, D), lambda c: (NB - 1 - c, 0))
    bcs = pl.BlockSpec((CS * L, 512), lambda c: (NB - 1 - c, 2))
    seg = pl.BlockSpec((CS * L, 128), lambda c: (NB - 1 - c, 0))
    return dict(
        body=body, steps=NB, ins=[xbc, xbc, da, h_save, dyy, d_x],
        in_specs=[row, bcs, seg, pl.BlockSpec((CS, SSM_N, D), lambda c: (NB - 1 - c, 0, 0)), row,
                  pl.BlockSpec((1, D), lambda c: (0, 0))],
        out_specs=[pl.BlockSpec((CS * L, D + 512), lambda c: (NB - 1 - c, 0)), seg],
        out_shape=[jax.ShapeDtypeStruct((T, D + 512), F32), jax.ShapeDtypeStruct((T, 128), F32)],
        scratch=[pltpu.VMEM((SSM_N, D), F32), pltpu.VMEM((L, D), F32), pltpu.VMEM((L, D), F32)])


def _run_scans(parts, *, name):
    steps = parts[0]["steps"]
    assert all(p["steps"] == steps for p in parts)
    cnt = lambda key: [len(p[key]) for p in parts]
    n_in, n_out, n_scr = cnt("ins"), cnt("out_shape"), cnt("scratch")

    def body(*refs):
        ins, outs, scr = refs[:sum(n_in)], refs[sum(n_in):sum(n_in) + sum(n_out)], refs[sum(n_in) + sum(n_out):]
        oi = oo = os_ = 0
        for p, a, b, c in zip(parts, n_in, n_out, n_scr):
            p["body"](*ins[oi:oi + a], *outs[oo:oo + b], *scr[os_:os_ + c])
            oi, oo, os_ = oi + a, oo + b, os_ + c

    cat = lambda key: [v for p in parts for v in p[key]]
    res = pl.pallas_call(
        body, grid=(steps,), in_specs=cat("in_specs"), out_specs=cat("out_specs"), out_shape=cat("out_shape"),
        scratch_shapes=cat("scratch"), name=name, compiler_params=_params(("arbitrary",)))(*cat("ins"))
    out, o = [], 0
    for b in n_out:
        out.append(list(res[o:o + b]))
        o += b
    return out


_EARLY = ("w_out", "wq_mem", "wk_mem", "wv_mem", "wo_mem")
_LATE = ("w_up", "w_down")
_GRADS_MLP = ("w_down", "w_up")
_GRADS_MID = ("wo_mem", "wq_mem", "wk_mem", "wv_mem", "w_out")


def _gather_ride(shards, names):
    return None if shards is None else _Ride([shards[n] for n in names], shard=True)


def _grad_ride(shards, G, names):
    return None if shards is None else _Ride([_slots_from_full(n, G[n]) for n in names], shard=False)


def _local_step(x, mem, tgt, W, shards=None):
    T = x.shape[0]
    W = dict(W)
    cw_qk, cw_v = W["gdn_conv_w"][:, :2 * D], W["gdn_conv_w"][:, 2 * D:]
    h1 = _rmsnorm_fwd(x, W["norm1_w"], name="norm1_fwd")
    ride = _gather_ride(shards, _EARLY)
    pg = _mm(h1, W["w_in_pad"], b_cols=(C_GATE, C_TOT - C_GATE), name="in_proj_gates")
    p = _mm(h1, W["w_in_pad"], b_cols=(0, C_GATE), out_dtype=BF16, bn_cap=1664, name="in_proj", ride=ride)
    if ride:
        p, got = p
        W.update({n: _full_from_slots(n, g) for n, g in zip(_EARLY, got)})
    qk = _conv_fwd(p, C_QKV, 2 * D, cw_qk, None, l2=True, name="gdn_conv_qk_fwd")
    v_g = _conv_fwd(p, C_QKV + 2 * D, D, cw_v, None, l2=False, name="gdn_conv_v_fwd")
    bg = _gdn_gates_fwd(pg, W["gdn_alog_row"], W["gdn_dtb_row"])
    ride = _gather_ride(shards, _LATE)
    prep = _gdn_prep(qk, v_g, bg, ride)
    if ride:
        prep, got = prep
        W.update({n: _full_from_slots(n, g) for n, g in zip(_LATE, got)})
    u_g, w_g, qd_g, kd_g, p_g, t_save = prep
    xbc = _conv_fwd(p, C_XBC, D + 512, W["ssm_conv_w"], W["ssm_conv_b"], l2=False, name="ssm_conv_fwd", bc=512)
    da_s = _ssd_dt_fwd(pg, W["ssm_dtb_row"], W["ssm_alog_row"])
    (o_g, vn_g, s_save), (y_s, h_save) = _run_scans(
        [_gdn_scan_fwd(u_g, w_g, qd_g, kd_g, p_g, bg), _ssd_core_fwd(xbc, da_s)], name="scans_fwd")
    mix = _gdn_post_fwd(o_g, p, W["gdn_norm_x"])
    mix = _ssd_post_fwd(y_s, xbc, p, W["ssm_d_x"], W["ssm_norm_w"].reshape(1, D), mix)
    x1, h2 = _mm(mix, W["w_out"], epi="res_norm", extra=(x, W["norm2_w"]), bm=512, name="out_proj")
    qm = _mm(h2, W["wq_mem"], out_dtype=BF16, name="q_proj")
    m = _rmsnorm_fwd(mem, W["mem_norm_w"], name="mem_norm_fwd")
    km = _mm(m, W["wk_mem"], name="k_proj")
    vm = _mm(m, W["wv_mem"], name="v_proj")
    oa = _attn_fwd(qm, km, vm)
    x2, h3 = _mm(oa, W["wo_mem"], epi="res_norm", extra=(x1, W["norm3_w"]), bm=512, name="o_proj")
    u, act = _mm(h3, W["w_up"], epi="relu2", out_dtype=BF16, b_resident=True, name="mlp_up")
    dx3, g_final, loss = _mm(act, W["w_down"], epi="res_loss", extra=(x2, tgt, W["final_norm_w"]), bk_cap=1024,
                             b_resident=True, name="mlp_down_loss")
    G = {"final_norm_w": g_final.reshape(D)}
    dpre = _mm(dx3, W["w_down"], dims="nt", epi="mul2", extra=u, out_dtype=BF16, b_resident=True, name="mlp_down_dx")
    G["w_down"] = _mm(act, dx3, dims="tn", out_dtype=BF16, name="mlp_down_dw")
    G["w_up"] = _mm(h3, dpre, dims="tn", out_dtype=BF16, bk_cap=4096, name="mlp_up_dw")
    dx2, gw = _mm(dpre, W["w_up"], dims="nt", epi="norm_bwd", extra=(x2, dx3, W["norm3_w"]), bk_cap=1024,
                  b_resident=True, name="mlp_up_dx")
    G["norm3_w"] = gw.reshape(D)
    do_a = _mm(dx2, W["wo_mem"], dims="nt", out_dtype=BF16, name="o_proj_dx")
    G["wo_mem"] = _mm(oa, dx2, dims="tn", out_dtype=BF16, name="o_proj_dw")
    dq, dk, dv = _attn_bwd(qm, km, vm, do_a)
    G["wq_mem"] = _mm(h2, dq, dims="tn", out_dtype=BF16, bk_cap=4096, name="q_proj_dw")
    dx1, gw = _mm(dq, W["wq_mem"], dims="nt", epi="norm_bwd", extra=(x1, dx2, W["norm2_w"]), bm=512,
                  name="q_proj_dx")
    G["norm2_w"] = gw.reshape(D)
    G["wk_mem"] = _mm(m, dk, dims="tn", out_dtype=BF16, name="k_proj_dw")
    G["wv_mem"] = _mm(m, dv, dims="tn", out_dtype=BF16, name="v_proj_dw")
    dm = _mm(dk, W["wk_mem"], dims="nt", name="k_proj_dx")
    dm = _mm(dv, W["wv_mem"], dims="nt", epi="res", extra=dm, name="v_proj_dx")
    _, G["mem_norm_w"] = _rmsnorm_bwd(mem, W["mem_norm_w"], dm, None, name="mem_norm_bwd")
    G["w_out"] = _mm(mix, dx1, dims="tn", out_dtype=BF16, name="out_proj_dw")
    do_g, dp, G["gdn_norm_x"] = _gdn_post_bwd(dx1, W["w_out"], o_g, p, W["gdn_norm_x"])
    dyy, dp, G["ssm_d_x"], G["ssm_norm_w"] = _ssd_post_bwd(dx1, W["w_out"], y_s, xbc, p, W["ssm_d_x"],
                                                          W["ssm_norm_w"].reshape(1, D), dp)
    (dvn_g, ds_save), (dxbc, dda_s) = _run_scans(
        [_gdn_scan_bwd(w_g, qd_g, kd_g, p_g, bg, do_g), _ssd_core_bwd(xbc, da_s, h_save, dyy, W["ssm_d_x"])],
        name="scans_bwd")
    ride = _grad_ride(shards, G, _GRADS_MLP)
    rest = _gdn_rest_bwd(qk, v_g, bg, s_save, t_save, vn_g, dvn_g, ds_save, do_g, ride)
    if ride:
        rest, got = rest
        G.update(zip(_GRADS_MLP, got))
    dqkvn, dbg = rest
    dy_qk, gcw_qk, _ = _conv_bwd_act(p, C_QKV, 2 * D, cw_qk, None, dqkvn, 0, l2=True, name="gdn_conv_qk_bwd_act")
    dy_v, gcw_v, _ = _conv_bwd_act(p, C_QKV + 2 * D, D, cw_v, None, dqkvn, 2 * D, l2=False,
                                   name="gdn_conv_v_bwd_act")
    G["gdn_conv_w"] = jnp.concatenate([gcw_qk, gcw_v], axis=1)
    dp = _conv_bwd_in(dy_qk, cw_qk, dp, C_QKV, T, name="gdn_conv_qk_bwd_in")
    dp = _conv_bwd_in(dy_v, cw_v, dp, C_QKV + 2 * D, T, name="gdn_conv_v_bwd_in")
    dp, G["gdn_alog_row"], G["gdn_dtb_row"] = _gdn_gates_bwd(pg, W["gdn_alog_row"], W["gdn_dtb_row"], dbg, dp)
    dy_s, G["ssm_conv_w"], G["ssm_conv_b"] = _conv_bwd_act(p, C_XBC, D + 512, W["ssm_conv_w"], W["ssm_conv_b"],
                                                           dxbc, 0, l2=False, name="ssm_conv_bwd_act", bc=512)
    dp = _conv_bwd_in(dy_s, W["ssm_conv_w"], dp, C_XBC, T, name="ssm_conv_bwd_in", bc=512)
    dp, G["ssm_dtb_row"], G["ssm_alog_row"] = _ssd_dt_bwd(pg, W["ssm_dtb_row"], W["ssm_alog_row"], dda_s, dp)
    ride = _grad_ride(shards, G, _GRADS_MID)
    g_in = _mm(h1, dp, dims="tn", out_dtype=BF16, bn_cap=1152, bk_cap=4096, name="in_proj_dw", ride=ride)
    if ride:
        g_in, got = g_in
        G.update(zip(_GRADS_MID, got))
    G["w_in"] = _unpad_w_in(g_in)
    ride = _grad_ride(shards, G, ("w_in",))
    res = _mm(dp, W["w_in_pad"], dims="nt", epi="norm_bwd", extra=(x, dx1, W["norm1_w"]), b_resident=True,
              name="in_proj_dx", ride=ride)
    if ride:
        res, got = res
        G["w_in"] = got[0]
    dx, gw = res
    G["norm1_w"] = gw.reshape(D)
    return loss, dx, G


def _all_gather(shards, out_dtype, *, name):
    n = len(shards)

    def body(*refs):
        x_refs, out_refs, stage = refs[:n], refs[n:2 * n], refs[2 * n:3 * n]
        send_sems, recv_sems, local_sems = refs[3 * n:]
        x, y, c = _place()
        me, sibling = (x, y, c), (x, y, 1 - c)
        chips = [(1 - x, y), (x, 1 - y), (1 - x, 1 - y)]

        def slot(px, py, pc):
            return 4 * px + 2 * py + pc

        def copy(a, k, block, to, src=None):
            dst = out_refs[a].at[slot(*block)]
            return pltpu.make_async_remote_copy(
                src_ref=dst if src is None else src, dst_ref=dst, send_sem=send_sems.at[a, k],
                recv_sem=recv_sems.at[a, k], device_id=to, device_id_type=_MESH)

        for a in range(n):
            stage[a][...] = x_refs[a][...].astype(out_dtype)
        mine = [pltpu.make_async_copy(stage[a], out_refs[a].at[slot(*me)], local_sems.at[a]) for a in range(n)]
        for cp in mine:
            cp.start()
        first = []
        for a in range(n):
            first.append(copy(a, 0, me, sibling, src=stage[a]))
            first += [copy(a, 1 + j, me, (*chip, c), src=stage[a]) for j, chip in enumerate(chips)]
        for cp in first:
            cp.start()
        passed = [[copy(a, 4 + j, (*chip, c), sibling) for j, chip in enumerate(chips)] for a in range(n)]
        for j, chip in enumerate(chips):
            for a in range(n):
                copy(a, 1 + j, (*chip, c), me).wait_recv()
                passed[a][j].start()
        for a in range(n):
            copy(a, 0, sibling, me).wait_recv()
            for j, chip in enumerate(chips):
                copy(a, 4 + j, (*chip, 1 - c), me).wait_recv()
        for cp in first + [cp for row in passed for cp in row]:
            cp.wait_send()
        for cp in mine:
            cp.wait()

    outs = pl.pallas_call(
        body, in_specs=[_VM] * n, out_specs=[_ANY] * n,
        out_shape=[jax.ShapeDtypeStruct((N_DEV,) + s.shape, out_dtype) for s in shards],
        scratch_shapes=[pltpu.VMEM(s.shape, out_dtype) for s in shards]
        + [pltpu.SemaphoreType.DMA((n, 7)), pltpu.SemaphoreType.DMA((n, 7)), pltpu.SemaphoreType.DMA((n,))],
        name=name, compiler_params=pltpu.CompilerParams(vmem_limit_bytes=VMEM_LIMIT))(*shards)
    return list(outs)


def _cast_bf16(arrs, *, name):
    n = len(arrs)

    def body(*refs):
        for a in range(n):
            refs[n + a][...] = refs[a][...].astype(BF16)

    return list(pl.pallas_call(
        body, in_specs=[_VM] * n, out_specs=[_VM] * n,
        out_shape=[jax.ShapeDtypeStruct(s.shape, BF16) for s in arrs], name=name,
        compiler_params=pltpu.CompilerParams(vmem_limit_bytes=VMEM_LIMIT))(*arrs))


def _sum8(a, *, name):
    _, R, Cc = a.shape
    br = _pick_rows(R, 128)

    def body(a_ref, o_ref):
        s = a_ref[0].astype(F32)
        for k in range(1, N_DEV):
            s = s + a_ref[k].astype(F32)
        o_ref[...] = s

    return pl.pallas_call(
        body, grid=(R // br,), in_specs=[pl.BlockSpec((N_DEV, br, Cc), lambda i: (0, i, 0))],
        out_specs=pl.BlockSpec((br, Cc), lambda i: (i, 0)), out_shape=jax.ShapeDtypeStruct((R, Cc), F32),
        name=name, compiler_params=_params(("parallel",)))(a)


def _pick_rows(R, cap):
    if R <= cap:
        return R
    for d in range(cap, 7, -8):
        if R % d == 0:
            return d
    return R


def _adamw(w, g, m, v, *, name):
    shape = w.shape
    as2d = (lambda t: t.reshape(1, -1)) if w.ndim == 1 else (lambda t: t)
    w2, m2, v2 = as2d(w), as2d(m), as2d(v)
    R, Cc = w2.shape
    from_slabs = g.ndim == 3
    br = _pick_rows(R, 128 if from_slabs else 256)
    c1 = 1.0 - ADAM_B1 ** ADAM_STEP
    c2 = 1.0 - ADAM_B2 ** ADAM_STEP

    def body(w_ref, g_ref, m_ref, v_ref, go_ref, d_ref, nm_ref, nv_ref):
        if from_slabs:
            gv = g_ref[0].astype(F32)
            for k in range(1, N_DEV):
                gv = gv + g_ref[k].astype(F32)
        else:
            gv = g_ref[...]
        go_ref[...] = gv
        nm = ADAM_B1 * m_ref[...] + (1.0 - ADAM_B1) * gv
        nv = ADAM_B2 * v_ref[...] + (1.0 - ADAM_B2) * (gv * gv)
        nm_ref[...] = nm
        nv_ref[...] = nv
        d_ref[...] = -ADAM_LR * ((nm / c1) / (jnp.sqrt(nv / c2) + ADAM_EPS) + ADAM_WD * w_ref[...])

    blk = pl.BlockSpec((br, Cc), lambda i: (i, 0))
    g_spec = pl.BlockSpec((N_DEV, br, Cc), lambda i: (0, i, 0)) if from_slabs else blk
    outs = pl.pallas_call(
        body, grid=(R // br,), in_specs=[blk, g_spec, blk, blk], out_specs=[blk] * 4,
        out_shape=[jax.ShapeDtypeStruct((R, Cc), F32)] * 4, name=name,
        compiler_params=_params(("parallel",)))(w2, g if from_slabs else as2d(g), m2, v2)
    return tuple(o.reshape(shape) for o in outs)


_BIG = ("w_in", "w_out", "wq_mem", "wk_mem", "wv_mem", "wo_mem", "w_up", "w_down")
_COL_SHARDED = ("w_in", "w_up")
_WEIGHTS = ("norm1_w", "w_in", "gdn_conv_w", "gdn_a_log", "gdn_dt_bias", "gdn_norm_w", "ssm_conv_w", "ssm_conv_b",
            "ssm_a_log", "ssm_dt_bias", "ssm_d", "ssm_norm_w", "w_out", "norm2_w", "mem_norm_w", "wq_mem", "wk_mem",
            "wv_mem", "wo_mem", "norm3_w", "w_up", "w_down", "final_norm_w")
_IN_PAD = 112


def _move_col_slabs(a, to_slabs, *, name):
    n, R, c = (N_DEV, a.shape[0], a.shape[1] // N_DEV) if to_slabs else a.shape
    slab = pl.BlockSpec((None, R, c), lambda j: (j, 0, 0))
    cols = pl.BlockSpec((R, c), lambda j: (0, j))

    def body(a_ref, o_ref):
        o_ref[...] = a_ref[...]

    return pl.pallas_call(
        body, grid=(n,), in_specs=[cols if to_slabs else slab], out_specs=slab if to_slabs else cols,
        out_shape=jax.ShapeDtypeStruct((n, R, c) if to_slabs else (R, n * c), a.dtype), name=name,
        compiler_params=_params(("parallel",)))(a)


def _full_from_slots(name, g):
    if name in _COL_SHARDED:
        if g.shape[2] % 128 == 0:
            return _move_col_slabs(g, False, name="cols_" + name)
        return jnp.transpose(g, (1, 0, 2)).reshape(g.shape[1], N_DEV * g.shape[2])
    return g.reshape(N_DEV * g.shape[1], g.shape[2])


def _slots_from_full(name, f):
    if name in _COL_SHARDED:
        if (f.shape[1] // N_DEV) % 128 == 0:
            return _move_col_slabs(f, True, name="slabs_" + name)
        return jnp.transpose(f.reshape(f.shape[0], N_DEV, f.shape[1] // N_DEV), (1, 0, 2))
    return f.reshape(N_DEV, f.shape[0] // N_DEV, f.shape[1])


def _pad_w_in(w):
    z = jnp.zeros((w.shape[0], _IN_PAD), w.dtype)
    return jnp.concatenate([w[:, :4096], w[:, 4112:6672], w[:, 4096:4112], z, w[:, 6672:6688], z], axis=1)


def _unpad_w_in(gp):
    return jnp.concatenate([gp[:, :4096], gp[:, C_GATE:C_GATE + 16], gp[:, 4096:C_GATE], gp[:, C_DT:C_DT + 16]],
                           axis=1)


def _pack_rows(vals):
    rows, offs, r = [], [], 0
    for vflat in vals:
        nrow = 8 * -(-vflat.shape[0] // 1024)
        rows.append(jnp.pad(vflat, (0, nrow * 128 - vflat.shape[0])).reshape(nrow, 128))
        offs.append((r, vflat.shape[0]))
        r += nrow
    return jnp.concatenate(rows, axis=0), offs


def _unpack_rows(packed, offs, shapes):
    out = []
    for (r, nel), shp in zip(offs, shapes):
        nrow = -(-nel // 128)
        out.append(packed[r:r + nrow].reshape(-1)[:nel].reshape(shp))
    return out


def kernel(x, mem, norm1_w, w_in, gdn_conv_w, gdn_a_log, gdn_dt_bias, gdn_norm_w, ssm_conv_w, ssm_conv_b, ssm_a_log, ssm_dt_bias, ssm_d, ssm_norm_w, w_out, norm2_w, mem_norm_w, wq_mem, wk_mem, wv_mem, wo_mem, norm3_w, w_up, w_down, final_norm_w, loss_target, m_norm1_w, m_w_in, m_gdn_conv_w, m_gdn_a_log, m_gdn_dt_bias, m_gdn_norm_w, m_ssm_conv_w, m_ssm_conv_b, m_ssm_a_log, m_ssm_dt_bias, m_ssm_d, m_ssm_norm_w, m_w_out, m_norm2_w, m_mem_norm_w, m_wq_mem, m_wk_mem, m_wv_mem, m_wo_mem, m_norm3_w, m_w_up, m_w_down, m_final_norm_w, v_norm1_w, v_w_in, v_gdn_conv_w, v_gdn_a_log, v_gdn_dt_bias, v_gdn_norm_w, v_ssm_conv_w, v_ssm_conv_b, v_ssm_a_log, v_ssm_dt_bias, v_ssm_d, v_ssm_norm_w, v_w_out, v_norm2_w, v_mem_norm_w, v_wq_mem, v_wk_mem, v_wv_mem, v_wo_mem, v_norm3_w, v_w_up, v_w_down, v_final_norm_w):
    args = dict(locals())
    w_loc = {n: args[n] for n in _WEIGHTS}
    me = 4 * lax.axis_index("x") + 2 * lax.axis_index("y") + lax.axis_index("c")

    w_in_full = _full_from_slots("w_in", _all_gather([w_in], BF16, name="gather_w_in")[0])
    later = _EARLY + _LATE
    shards = dict(zip(later, _cast_bf16([w_loc[n] for n in later], name="cast_shards")))
    conv_pack, conv_offs = _pack_rows([gdn_conv_w.reshape(-1), ssm_conv_w.reshape(-1)])
    conv_all = _all_gather([conv_pack], F32, name="gather_conv")[0]
    gdn_cw, ssm_cw = [], []
    for k in range(N_DEV):
        a, b = _unpack_rows(conv_all[k], conv_offs, [gdn_conv_w.shape, ssm_conv_w.shape])
        gdn_cw.append(a)
        ssm_cw.append(b)
    W = {
        "w_in_pad": _pad_w_in(w_in_full),
        "norm1_w": norm1_w, "norm2_w": norm2_w, "norm3_w": norm3_w, "mem_norm_w": mem_norm_w,
        "final_norm_w": final_norm_w, "ssm_norm_w": ssm_norm_w, "ssm_conv_b": ssm_conv_b,
        "gdn_conv_w": jnp.concatenate(gdn_cw, axis=1), "ssm_conv_w": jnp.concatenate(ssm_cw, axis=1),
        "gdn_alog_row": jnp.pad(gdn_a_log, (GDN_H, 128 - 2 * GDN_H)).reshape(1, 128),
        "gdn_dtb_row": jnp.pad(gdn_dt_bias, (GDN_H, 128 - 2 * GDN_H)).reshape(1, 128),
        "gdn_norm_x": jnp.tile(gdn_norm_w, GDN_H).reshape(1, D),
        "ssm_dtb_row": jnp.pad(ssm_dt_bias, (0, 128 - SSM_H)).reshape(1, 128),
        "ssm_alog_row": jnp.pad(ssm_a_log, (0, 128 - SSM_H)).reshape(1, 128),
        "ssm_d_x": jnp.repeat(ssm_d, SSM_P).reshape(1, D),
    }

    loss_part, grad_x, G = _local_step(x[0], mem[0], loss_target[0], W, shards)

    grads = {n: G[n] for n in _BIG}

    small = {
        "norm1_w": G["norm1_w"], "gdn_conv_w": G["gdn_conv_w"], "gdn_a_log": G["gdn_alog_row"][0, GDN_H:2 * GDN_H],
        "gdn_dt_bias": G["gdn_dtb_row"][0, GDN_H:2 * GDN_H], "gdn_norm_w": G["gdn_norm_x"].reshape(GDN_H, 128).sum(0),
        "ssm_conv_w": G["ssm_conv_w"], "ssm_conv_b": G["ssm_conv_b"],
        "ssm_a_log": G["ssm_alog_row"][0, :SSM_H], "ssm_dt_bias": G["ssm_dtb_row"][0, :SSM_H],
        "ssm_d": G["ssm_d_x"].reshape(SSM_H, SSM_P).sum(1), "ssm_norm_w": G["ssm_norm_w"].reshape(D),
        "norm2_w": G["norm2_w"], "mem_norm_w": G["mem_norm_w"], "norm3_w": G["norm3_w"],
        "final_norm_w": G["final_norm_w"], "loss": loss_part[0, :1],
    }
    names = list(small)
    pack, offs = _pack_rows([small[n].reshape(-1) for n in names])
    tot = _sum8(_all_gather([pack], F32, name="gather_small")[0], name="sum_small")
    summed = dict(zip(names, _unpack_rows(tot, offs, [small[n].shape for n in names])))
    loss = summed.pop("loss")[0]
    for n in ("gdn_conv_w", "ssm_conv_w"):
        width = w_loc[n].shape[1]
        summed[n] = lax.dynamic_slice_in_dim(summed[n], me * width, width, axis=1)
    grads.update(summed)

    upd = {n: _adamw(w_loc[n], grads[n], args["m_" + n], args["v_" + n], name="adamw_" + n) for n in _WEIGHTS}
    return (loss, grad_x[None], *[upd[n][0] for n in _WEIGHTS], *[upd[n][1] for n in _WEIGHTS],
            *[upd[n][2] for n in _WEIGHTS], *[upd[n][3] for n in _WEIGHTS])
```

```python
import jax
import jax.numpy as jnp
from jax import lax
from jax.experimental import pallas as pl
from jax.experimental.pallas import tpu as pltpu

F32 = jnp.float32
BF16 = jnp.bfloat16
_MXU = BF16

D = 1024
EPS = 1e-6
CONV_K = 4
GDN_H, GDN_DK, GDN_C = 8, 128, 64
GDN_SCAN_CHUNKS = 8
GDN_LOCAL_CHUNKS = 4
GDN_REST_CHUNKS = 4
SSM_H, SSM_P, SSM_L, SSM_N = 16, 64, 128, 128
SSM_SCAN_CHUNKS = 4
MEM_H, MEM_HD = 4, 256
D_FF = 4096
N_DEV = 8

C_QKV, C_ZG, C_ZS, C_XBC, C_GATE, C_DT, C_TOT = 0, 3072, 4096, 5120, 6656, 6784, 6912
P_HALO = 16

ADAM_LR, ADAM_B1, ADAM_B2, ADAM_EPS, ADAM_WD, ADAM_STEP = 0.001, 0.9, 0.999, 1e-08, 0.01, 10

VMEM_LIMIT = 56 * 1024 * 1024

_NN = (((1,), (0,)), ((), ()))
_NT = (((1,), (1,)), ((), ()))
_TN = (((0,), (0,)), ((), ()))


def _dot(a, b, dims=_NN):
    return lax.dot_general(a.astype(_MXU), b.astype(_MXU), dims, preferred_element_type=F32)


def _split3(a):
    a1 = a.astype(BF16)
    r1 = a - a1.astype(F32)
    a2 = r1.astype(BF16)
    return a1, a2, (r1 - a2.astype(F32)).astype(BF16)


def _dot_sel(a, e):
    eb = e.astype(BF16)
    return sum(lax.dot_general(p, eb, _NN, preferred_element_type=F32) for p in _split3(a))


def _sel_dot(e, a):
    eb = e.astype(BF16)
    return sum(lax.dot_general(eb, p, _NN, preferred_element_type=F32) for p in _split3(a))


def _chunk_cumsum(a, tri, chunk):
    return jnp.concatenate([_sel_dot(tri, a[r:r + chunk]) for r in range(0, a.shape[0], chunk)], axis=0)


def _params(sem):
    return pltpu.CompilerParams(dimension_semantics=sem, vmem_limit_bytes=VMEM_LIMIT)


def _pick(n, cap):
    for d in range(min(cap, n), 0, -128):
        if n % d == 0 and d % 128 == 0:
            return d
    return n


def _sigmoid(x):
    return 0.5 * jnp.tanh(0.5 * x) + 0.5


def _silu(x):
    return x * _sigmoid(x)


def _dsilu(x):
    s = _sigmoid(x)
    return s * (1.0 + x * (1.0 - s))


def _softplus(x):
    return jnp.maximum(x, 0.0) + jnp.log(1.0 + jnp.exp(-jnp.abs(x)))


def _iota2(shape, axis):
    return lax.broadcasted_iota(jnp.int32, shape, axis)


def _sum_all(x):
    return jnp.sum(jnp.sum(x, axis=1, keepdims=True), axis=0, keepdims=True)


_MESH = pl.DeviceIdType.MESH
_ANY = pl.BlockSpec(memory_space=pl.ANY)
_VM = pl.BlockSpec(memory_space=pltpu.VMEM)
_REL = [(r >> 2 & 1, r >> 1 & 1, r & 1) for r in range(1, N_DEV)]


def _place():
    return lax.axis_index("x"), lax.axis_index("y"), lax.axis_index("c")


class _Ride:
    def __init__(self, srcs, shard):
        self.srcs, self.shard, self.n = list(srcs), shard, len(srcs)
        self.out_shape = [jax.ShapeDtypeStruct(((N_DEV,) + s.shape) if shard else s.shape, s.dtype)
                          for s in self.srcs]
        self.specs = [_ANY] * self.n
        self.scratch = [pltpu.SemaphoreType.DMA((self.n, N_DEV - 1)), pltpu.SemaphoreType.DMA((self.n, N_DEV - 1)),
                        pltpu.SemaphoreType.DMA((self.n,))]

    def _copies(self, in_refs, out_refs, sems):
        send, recv, loc = sems
        x, y, c = _place()
        me = 4 * x + 2 * y + c
        local, remote, arrive = [], [], []
        for a in range(self.n):
            src = in_refs[a] if self.shard else in_refs[a].at[me]
            local.append(pltpu.make_async_copy(src, out_refs[a].at[me], loc.at[a]))
        for k, (rx, ry, rc) in enumerate(_REL):
            peer = (lax.rem(x + rx, 2), lax.rem(y + ry, 2), lax.rem(c + rc, 2))
            ps = 4 * peer[0] + 2 * peer[1] + peer[2]
            for a in range(self.n):
                src = in_refs[a] if self.shard else in_refs[a].at[ps]
                remote.append(pltpu.make_async_remote_copy(
                    src_ref=src, dst_ref=out_refs[a].at[me], send_sem=send.at[a, k], recv_sem=recv.at[a, k],
                    device_id=peer, device_id_type=_MESH))
                slot = out_refs[a].at[ps]
                arrive.append(pltpu.make_async_remote_copy(
                    src_ref=slot, dst_ref=slot, send_sem=send.at[a, k], recv_sem=recv.at[a, k],
                    device_id=peer, device_id_type=_MESH))
        return local, remote, arrive

    def start(self, in_refs, out_refs, sems):
        local, remote, _ = self._copies(in_refs, out_refs, sems)
        for cp in local + remote:
            cp.start()

    def wait(self, in_refs, out_refs, sems):
        local, remote, arrive = self._copies(in_refs, out_refs, sems)
        for cp in arrive:
            cp.wait_recv()
        for cp in remote:
            cp.wait_send()
        for cp in local:
            cp.wait()


_EPI = {
    "none": ((), ("tile",)),
    "res": (("tile",), ("tile",)),
    "mul2": (("tile",), ("tile",)),
    "relu2": ((), ("tile", "tile")),
    "res_norm": (("tile", "row"), ("tile", "tile")),
    "norm_bwd": (("tile", "tile", "row"), ("tile", "row")),
    "res_loss": (("tile", "tile", "row"), ("tile", "row", "row")),
}


def _mm(a, b, *, dims="nn", epi="none", extra=(), out_dtype=F32, name, bm=1024, bn_cap=1024, bk_cap=2048,
        ride=None, b_cols=None, b_resident=False):
    if dims == "nn":
        (M, K), (K2, N) = a.shape, b.shape
    elif dims == "nt":
        (M, K), (N, K2) = a.shape, b.shape
    else:
        (K, M), (K2, N) = a.shape, b.shape
    jb0 = 0
    if b_cols is not None:
        N = b_cols[1]
    assert K == K2, (a.shape, b.shape, dims)
    bm = _pick(M, bm)
    bn = _pick(N, bn_cap)
    bk = _pick(K, bk_cap)
    nk = K // bk
    if b_cols is not None:
        assert dims == "nn" and b_cols[0] % bn == 0
        jb0 = b_cols[0] // bn
    dn = {"nn": _NN, "nt": _NT, "tn": _TN}[dims]
    a_spec = (pl.BlockSpec((bk, bm), lambda i, j, k: (k, i)) if dims == "tn"
              else pl.BlockSpec((bm, bk), lambda i, j, k: (i, k)))
    if b_resident:
        b_spec = pl.BlockSpec(b.shape, lambda i, j, k: (0, 0), pipeline_mode=pl.Buffered(1))
    else:
        b_spec = (pl.BlockSpec((bn, bk), lambda i, j, k: (j, k)) if dims == "nt"
                  else pl.BlockSpec((bk, bn), lambda i, j, k: (k, j + jb0)))
    o_spec = pl.BlockSpec((bm, bn), lambda i, j, k: (i, j))
    r_spec = pl.BlockSpec((1, bn), lambda i, j, k: (0, j))
    extra = list(extra) if isinstance(extra, (tuple, list)) else [extra]
    ekinds, okinds = _EPI[epi]
    assert len(extra) == len(ekinds) and (epi not in ("res_norm", "norm_bwd", "res_loss") or bn == N)
    n_extra, n_out = len(ekinds), len(okinds)
    n_ride = ride.n if ride else 0
    gi, gj = M // bm, N // bn

    def body(a_ref, b_ref, *rest):
        ex = rest[:n_extra]
        first = pl.program_id(0) == 0
        ride_in = rest[n_extra:n_extra + n_ride]
        outs = rest[n_extra + n_ride:n_extra + n_ride + n_out]
        ride_out = rest[n_extra + n_ride + n_out:n_extra + 2 * n_ride + n_out]
        if ride:
            at = lambda i, j, k: ((pl.program_id(0) == i) & (pl.program_id(1) == j) & (pl.program_id(2) == k))

            @pl.when(at(0, 0, 0))
            def _():
                ride.start(ride_in, ride_out, rest[-3:])

        def finish(r):
            if epi == "res":
                outs[0][...] = (r + ex[0][...].astype(F32)).astype(outs[0].dtype)
            elif epi == "mul2":
                outs[0][...] = (2.0 * r * ex[0][...].astype(F32)).astype(outs[0].dtype)
            elif epi == "relu2":
                u = jnp.maximum(r, 0.0)
                outs[0][...] = u.astype(outs[0].dtype)
                outs[1][...] = (u * u).astype(outs[1].dtype)
            elif epi == "res_norm":
                y = r + ex[0][...]
                outs[0][...] = y
                rstd = lax.rsqrt(jnp.mean(y * y, axis=1, keepdims=True) + EPS)
                outs[1][...] = (y * rstd * ex[1][...]).astype(outs[1].dtype)
            elif epi == "norm_bwd":
                xv = ex[0][...]
                rstd = lax.rsqrt(jnp.mean(xv * xv, axis=1, keepdims=True) + EPS)
                xh = xv * rstd
                dxh = r * ex[2][...]
                outs[0][...] = ex[1][...] + rstd * (dxh - xh * jnp.mean(dxh * xh, axis=1, keepdims=True))
                dw = jnp.sum(r * xh, axis=0, keepdims=True)

                @pl.when(first)
                def _():
                    outs[1][...] = dw

                @pl.when(jnp.logical_not(first))
                def _():
                    outs[1][...] += dw
            elif epi == "res_loss":
                y = r + ex[0][...]
                wv = ex[2][...]
                rstd = lax.rsqrt(jnp.mean(y * y, axis=1, keepdims=True) + EPS)
                yh = y * rstd
                err = yh * wv - ex[1][...]
                part_loss = 0.5 * jnp.sum(jnp.mean(err * err, axis=1, keepdims=True), axis=0, keepdims=True)
                dyn = err * (1.0 / N)
                dyh = dyn * wv
                outs[0][...] = rstd * (dyh - yh * jnp.mean(dyh * yh, axis=1, keepdims=True))
                dw = jnp.sum(dyn * yh, axis=0, keepdims=True)
                lrow = jnp.broadcast_to(part_loss, (1, N))

                @pl.when(first)
                def _():
                    outs[1][...] = dw
                    outs[2][...] = lrow

                @pl.when(jnp.logical_not(first))
                def _():
                    outs[1][...] += dw
                    outs[2][...] += lrow
            else:
                outs[0][...] = r.astype(outs[0].dtype)

        if b_resident:
            jo = pl.multiple_of((pl.program_id(1) + jb0) * bn, bn)
            ko = pl.multiple_of(pl.program_id(2) * bk, bk)
            b_blk = b_ref[pl.ds(jo, bn), pl.ds(ko, bk)] if dims == "nt" else b_ref[pl.ds(ko, bk), pl.ds(jo, bn)]
        else:
            b_blk = b_ref[...]
        part = _dot(a_ref[...], b_blk, dn)
        if nk == 1:
            finish(part)
        else:
            acc = rest[n_extra + 2 * n_ride + n_out]
            k = pl.program_id(2)

            @pl.when(k == 0)
            def _():
                acc[...] = part

            @pl.when((k > 0) & (k < nk - 1))
            def _():
                acc[...] += part

            @pl.when(k == nk - 1)
            def _():
                finish(acc[...] + part)

        if ride:
            @pl.when(at(gi - 1, gj - 1, nk - 1))
            def _():
                ride.wait(ride_in, ride_out, rest[-3:])

    kind_spec = {"tile": o_spec, "row": r_spec}
    ins = [a, b] + [e.reshape(1, N) if k == "row" else e for e, k in zip(extra, ekinds)]
    in_specs = [a_spec, b_spec] + [kind_spec[k] for k in ekinds]
    out_dtypes = {"res_norm": (F32, BF16), "norm_bwd": (F32, F32), "res_loss": (F32, F32, F32)}.get(
        epi, (out_dtype,) * n_out)
    out_shape = [jax.ShapeDtypeStruct((M, N) if k == "tile" else (1, N), dt) for k, dt in zip(okinds, out_dtypes)]
    out_specs = [kind_spec[k] for k in okinds]
    scratch = [pltpu.VMEM((bm, bn), F32)] if nk > 1 else []
    sem = ("arbitrary" if epi in ("norm_bwd", "res_loss") else "parallel", "parallel", "arbitrary")
    if ride:
        ins, in_specs = ins + ride.srcs, in_specs + ride.specs
        out_shape, out_specs = out_shape + ride.out_shape, out_specs + ride.specs
        scratch, sem = scratch + ride.scratch, ("arbitrary",) * 3
    res = pl.pallas_call(
        body, grid=(gi, gj, nk), in_specs=in_specs, out_specs=out_specs, out_shape=out_shape,
        scratch_shapes=scratch, name=name, compiler_params=_params(sem))(*ins)
    main = res[:n_out] if n_out > 1 else res[0]
    return (main, list(res[n_out:])) if ride else main


def _rmsnorm_fwd(x, w, *, name, bt=512):
    T, Dm = x.shape
    bt = min(bt, T)

    def body(x_ref, w_ref, h_ref):
        xv = x_ref[...]
        r = lax.rsqrt(jnp.mean(xv * xv, axis=1, keepdims=True) + EPS)
        h_ref[...] = (xv * r * w_ref[...]).astype(h_ref.dtype)

    return pl.pallas_call(
        body, grid=(T // bt,),
        in_specs=[pl.BlockSpec((bt, Dm), lambda i: (i, 0)), pl.BlockSpec((1, Dm), lambda i: (0, 0))],
        out_specs=pl.BlockSpec((bt, Dm), lambda i: (i, 0)),
        out_shape=jax.ShapeDtypeStruct((T, Dm), BF16), name=name,
        compiler_params=_params(("parallel",)))(x, w.reshape(1, Dm))


def _rmsnorm_bwd(x, w, dh, dres, *, name, bt=256):
    T, Dm = x.shape
    bt = min(bt, T)
    has_res = dres is not None

    def body(x_ref, w_ref, dh_ref, *rest):
        dres_ref = rest[0] if has_res else None
        dx_ref, dw_ref = rest[-2], rest[-1]
        i = pl.program_id(0)
        xv = x_ref[...]
        r = lax.rsqrt(jnp.mean(xv * xv, axis=1, keepdims=True) + EPS)
        xh = xv * r
        dhv = dh_ref[...].astype(F32)
        dxh = dhv * w_ref[...]
        dx = r * (dxh - xh * jnp.mean(dxh * xh, axis=1, keepdims=True))
        if has_res:
            dx = dx + dres_ref[...]
        dx_ref[...] = dx

        @pl.when(i == 0)
        def _():
            dw_ref[...] = jnp.zeros_like(dw_ref)

        dw_ref[...] += jnp.sum(dhv * xh, axis=0, keepdims=True)

    row = pl.BlockSpec((bt, Dm), lambda i: (i, 0))
    vec = pl.BlockSpec((1, Dm), lambda i: (0, 0))
    ins = [x, w.reshape(1, Dm), dh] + ([dres] if has_res else [])
    dx, dw = pl.pallas_call(
        body, grid=(T // bt,), in_specs=[row, vec, row] + ([row] if has_res else []),
        out_specs=[row, vec],
        out_shape=[jax.ShapeDtypeStruct((T, Dm), F32), jax.ShapeDtypeStruct((1, Dm), F32)],
        name=name, compiler_params=_params(("arbitrary",)))(*ins)
    return dx, dw.reshape(Dm)


def _attn_fwd(q, km, vm, *, bt=512):
    T = q.shape[0]
    M = km.shape[0]
    bt = min(bt, T)
    scale = MEM_HD ** -0.5

    def body(q_ref, k_ref, v_ref, o_ref):
        sls = [slice(h * MEM_HD, (h + 1) * MEM_HD) for h in range(MEM_H)]
        ss = [_dot(q_ref[:, sl], k_ref[:, sl], _NT) * scale for sl in sls]
        es = [jnp.exp(s - jnp.max(s, axis=1, keepdims=True)) for s in ss]
        ps = [e / jnp.sum(e, axis=1, keepdims=True) for e in es]
        for sl, p in zip(sls, ps):
            o_ref[:, sl] = _dot(p, v_ref[:, sl]).astype(o_ref.dtype)

    row = pl.BlockSpec((bt, D), lambda i: (i, 0))
    mem = pl.BlockSpec((M, D), lambda i: (0, 0))
    return pl.pallas_call(
        body, grid=(T // bt,), in_specs=[row, mem, mem], out_specs=row,
        out_shape=jax.ShapeDtypeStruct((T, D), BF16), name="attn_fwd",
        compiler_params=_params(("parallel",)))(q, km, vm)


def _attn_bwd(q, km, vm, do, *, bt=256):
    T = q.shape[0]
    M = km.shape[0]
    bt = min(bt, T)
    scale = MEM_HD ** -0.5

    def body(q_ref, k_ref, v_ref, do_ref, dq_ref, dk_ref, dv_ref):
        i = pl.program_id(0)

        @pl.when(i == 0)
        def _():
            dk_ref[...] = jnp.zeros_like(dk_ref)
            dv_ref[...] = jnp.zeros_like(dv_ref)

        sls = [slice(h * MEM_HD, (h + 1) * MEM_HD) for h in range(MEM_H)]
        ss = [_dot(q_ref[:, sl], k_ref[:, sl], _NT) * scale for sl in sls]
        dps = [_dot(do_ref[:, sl], v_ref[:, sl], _NT) for sl in sls]
        es = [jnp.exp(s - jnp.max(s, axis=1, keepdims=True)) for s in ss]
        ps = [e / jnp.sum(e, axis=1, keepdims=True) for e in es]
        dss = [p * (dp - jnp.sum(dp * p, axis=1, keepdims=True)) * scale for p, dp in zip(ps, dps)]
        for sl, p, ds in zip(sls, ps, dss):
            dq_ref[:, sl] = _dot(ds, k_ref[:, sl]).astype(dq_ref.dtype)
            dk_ref[:, sl] += _dot(ds, q_ref[:, sl], _TN)
            dv_ref[:, sl] += _dot(p, do_ref[:, sl], _TN)

    row = pl.BlockSpec((bt, D), lambda i: (i, 0))
    mem = pl.BlockSpec((M, D), lambda i: (0, 0))
    return pl.pallas_call(
        body, grid=(T // bt,), in_specs=[row, mem, mem, row], out_specs=[row, mem, mem],
        out_shape=[jax.ShapeDtypeStruct((T, D), BF16), jax.ShapeDtypeStruct((M, D), F32),
                   jax.ShapeDtypeStruct((M, D), F32)],
        name="attn_bwd", compiler_params=_params(("arbitrary",)))(q, km, vm, do)


def _conv_apply(halo, x, w_ref, b_ref):
    bt, hr = x.shape[0], halo.shape[0]
    cat = jnp.concatenate([halo, x], axis=0)
    y = x * w_ref[3:4, :]
    for k in range(CONV_K - 1):
        y = y + pltpu.roll(cat, CONV_K - 1 - k, 0)[hr:hr + bt] * w_ref[k:k + 1, :]
    if b_ref is not None:
        y = y + b_ref[...]
    return y


def _l2_parts(act, bc):
    out = []
    for s in range(bc // 128):
        a = act[:, s * 128:(s + 1) * 128]
        r = lax.rsqrt(jnp.sum(a * a, axis=1, keepdims=True) + EPS)
        out.append((a, r))
    return out


def _conv_fwd(p, col0, C, w, b, *, l2, name, bt=512, bc=1024):
    T = p.shape[0]
    bt = min(bt, T)
    c0, hb = col0 // bc, bt // P_HALO
    has_b = b is not None
    assert not l2 or (bc == D and C == 2 * D)

    def body(x_ref, halo_ref, w_ref, *rest):
        b_ref = rest[0] if has_b else None
        o_ref = rest[-1]
        i, j = pl.program_id(0), pl.program_id(1)
        x = x_ref[...].astype(F32)
        halo = jnp.where(i > 0, halo_ref[...].astype(F32), 0.0)
        act = _silu(_conv_apply(halo, x, w_ref, b_ref))
        if l2:
            sc = jnp.where(j == 0, GDN_DK ** -0.5, 1.0)
            o_ref[...] = jnp.concatenate([a * (r * sc) for a, r in _l2_parts(act, bc)], axis=1)
        else:
            o_ref[...] = act

    in_specs = [pl.BlockSpec((bt, bc), lambda i, j: (i, c0 + j)),
                pl.BlockSpec((P_HALO, bc), lambda i, j: (jnp.maximum(i * hb - 1, 0), c0 + j)),
                pl.BlockSpec((CONV_K, bc), lambda i, j: (0, j))]
    ins = [p, p, w]
    if has_b:
        in_specs.append(pl.BlockSpec((1, bc), lambda i, j: (0, j)))
        ins.append(b.reshape(1, C))
    return pl.pallas_call(
        body, grid=(T // bt, C // bc), in_specs=in_specs,
        out_specs=pl.BlockSpec((bt, bc), lambda i, j: (i, j)),
        out_shape=jax.ShapeDtypeStruct((T, C), F32), name=name,
        compiler_params=_params(("parallel", "parallel")))(*ins)


def _conv_bwd_act(p, col0, C, w, b, dact, dcol0, *, l2, name, bt=512, bc=1024):
    T = p.shape[0]
    bt = min(bt, T)
    c0, d0, hb = col0 // bc, dcol0 // bc, bt // P_HALO
    has_b = b is not None
    assert not l2 or (bc == D and C == 2 * D)

    def body(x_ref, halo_ref, w_ref, *rest):
        b_ref = rest[0] if has_b else None
        dact_ref, dy_ref, dw_ref, db_ref = rest[-4:]
        j, i = pl.program_id(0), pl.program_id(1)
        x = x_ref[...].astype(F32)
        halo = jnp.where(i > 0, halo_ref[...].astype(F32), 0.0)
        y = _conv_apply(halo, x, w_ref, b_ref)
        dact = dact_ref[...]
        sg = _sigmoid(y)
        if l2:
            sc = jnp.where(j == 0, GDN_DK ** -0.5, 1.0)
            parts = []
            for s, (a, r) in enumerate(_l2_parts(y * sg, bc)):
                n = a * r
                dn = dact[:, s * 128:(s + 1) * 128]
                parts.append((r * sc) * (dn - n * jnp.sum(dn * n, axis=1, keepdims=True)))
            dact = jnp.concatenate(parts, axis=1)
        dy = dact * (sg * (1.0 + y * (1.0 - sg)))
        dy_ref[...] = dy

        @pl.when(i == 0)
        def _():
            dw_ref[...] = jnp.zeros_like(dw_ref)
            db_ref[...] = jnp.zeros_like(db_ref)

        db_ref[...] += jnp.sum(dy, axis=0, keepdims=True)
        cat = jnp.concatenate([halo, x], axis=0)
        dw_ref[3:4, :] += jnp.sum(dy * x, axis=0, keepdims=True)
        for k in range(CONV_K - 1):
            xs = pltpu.roll(cat, CONV_K - 1 - k, 0)[P_HALO:P_HALO + bt]
            dw_ref[k:k + 1, :] += jnp.sum(dy * xs, axis=0, keepdims=True)

    in_specs = [pl.BlockSpec((bt, bc), lambda j, i: (i, c0 + j)),
                pl.BlockSpec((P_HALO, bc), lambda j, i: (jnp.maximum(i * hb - 1, 0), c0 + j)),
                pl.BlockSpec((CONV_K, bc), lambda j, i: (0, j))]
    ins = [p, p, w]
    if has_b:
        in_specs.append(pl.BlockSpec((1, bc), lambda j, i: (0, j)))
        ins.append(b.reshape(1, C))
    in_specs.append(pl.BlockSpec((bt, bc), lambda j, i: (i, d0 + j)))
    ins.append(dact)
    dy, dw, db = pl.pallas_call(
        body, grid=(C // bc, T // bt), in_specs=in_specs,
        out_specs=[pl.BlockSpec((bt, bc), lambda j, i: (i, j)),
                   pl.BlockSpec((CONV_K, bc), lambda j, i: (0, j)),
                   pl.BlockSpec((1, bc), lambda j, i: (0, j))],
        out_shape=[jax.ShapeDtypeStruct((T, C), F32), jax.ShapeDtypeStruct((CONV_K, C), F32),
                   jax.ShapeDtypeStruct((1, C), F32)],
        name=name, compiler_params=_params(("parallel", "arbitrary")))(*ins)
    return dy, dw, db.reshape(C)


def _conv_bwd_in(dy, w, dp_in, col0, T, *, name, bt=512, bc=1024):
    C = dy.shape[1]
    bt = min(bt, T)
    c0, hb, nb = col0 // bc, bt // 8, T // bt

    def body(dy_ref, nxt_ref, w_ref, *rest):
        o_ref = rest[-1]
        i = pl.program_id(0)
        dy_v = dy_ref[...]
        nxt = jnp.where(i < nb - 1, nxt_ref[...], 0.0)
        cat = jnp.concatenate([dy_v, nxt], axis=0)
        dx = dy_v * w_ref[3:4, :]
        for k in range(CONV_K - 1):
            s = CONV_K - 1 - k
            dx = dx + pltpu.roll(cat, bt + 8 - s, 0)[0:bt] * w_ref[k:k + 1, :]
        o_ref[...] = dx.astype(o_ref.dtype)

    in_specs = [pl.BlockSpec((bt, bc), lambda i, j: (i, j)),
                pl.BlockSpec((8, bc), lambda i, j: (jnp.minimum((i + 1) * hb, T // 8 - 1), j)),
                pl.BlockSpec((CONV_K, bc), lambda i, j: (0, j))]
    ins = [dy, dy, w]
    alias = {}
    if dp_in is not None:
        in_specs.append(pl.BlockSpec(memory_space=pl.ANY))
        ins.append(dp_in)
        alias = {3: 0}
    return pl.pallas_call(
        body, grid=(nb, C // bc), in_specs=in_specs,
        out_specs=pl.BlockSpec((bt, bc), lambda i, j: (i, c0 + j)),
        out_shape=jax.ShapeDtypeStruct((T, C_TOT), BF16), input_output_aliases=alias, name=name,
        compiler_params=_params(("parallel", "parallel")))(*ins)


def _expand_mats(shift, row0):
    e = (_iota2((128, D), 0) - row0 == (_iota2((128, D), 1) >> shift)).astype(F32)
    et = ((_iota2((D, 128), 0) >> shift) == _iota2((D, 128), 1) - row0).astype(F32)
    return e, et


def _cum_mats(chunk):
    ri, ci = _iota2((chunk, chunk), 0), _iota2((chunk, chunk), 1)
    return (ri >= ci).astype(F32), (ri <= ci).astype(F32)


def _gdn_gates_fwd(p, alog_row, dtb_row, *, bt=256):
    T = p.shape[0]
    bt = min(bt, T)

    def body(g_ref, al_ref, db_ref, bg_ref):
        gt = g_ref[...]
        lc, _ = _cum_mats(GDN_C)
        g_l = -jnp.exp(al_ref[...]) * _softplus(gt + db_ref[...])
        bg_ref[...] = jnp.where(_iota2((bt, 128), 1) < GDN_H, _sigmoid(gt), _chunk_cumsum(g_l, lc, GDN_C))

    vec = pl.BlockSpec((1, 128), lambda i: (0, 0))
    seg = pl.BlockSpec((bt, 128), lambda i: (i, 0))
    return pl.pallas_call(
        body, grid=(T // bt,), in_specs=[seg, vec, vec], out_specs=seg,
        out_shape=jax.ShapeDtypeStruct((T, 128), F32), name="gdn_gates_fwd",
        compiler_params=_params(("parallel",)))(p, alog_row, dtb_row)


def _gdn_gates_bwd(p, alog_row, dtb_row, dbg, dp_in, *, bt=256):
    T = p.shape[0]
    bt = min(bt, T)

    def body(g_ref, al_ref, db_ref, dbg_ref, dpin_ref, dg_out, dal_ref, ddb_ref):
        i = pl.program_id(0)
        gt = g_ref[...]
        lane = _iota2((bt, 128), 1)
        _, uc = _cum_mats(GDN_C)
        ea = jnp.exp(al_ref[...])
        zz = gt + db_ref[...]
        g_l = -ea * _softplus(zz)
        beta_l = _sigmoid(gt)
        dbg_v = dbg_ref[...]
        dg_l = jnp.where((lane >= GDN_H) & (lane < 2 * GDN_H), _chunk_cumsum(dbg_v, uc, GDN_C), 0.0)
        dbeta_l = jnp.where(lane < GDN_H, dbg_v, 0.0)
        da = dg_l * (-ea) * _sigmoid(zz)
        dg_out[...] = (da + dbeta_l * beta_l * (1.0 - beta_l)).astype(dg_out.dtype)

        @pl.when(i == 0)
        def _():
            dal_ref[...] = jnp.zeros_like(dal_ref)
            ddb_ref[...] = jnp.zeros_like(ddb_ref)

        dal_ref[...] += jnp.sum(dg_l * g_l, axis=0, keepdims=True)
        ddb_ref[...] += jnp.sum(da, axis=0, keepdims=True)

    vec = pl.BlockSpec((1, 128), lambda i: (0, 0))
    seg = pl.BlockSpec((bt, 128), lambda i: (i, 0))
    gate = pl.BlockSpec((bt, 128), lambda i: (i, C_GATE // 128))
    return pl.pallas_call(
        body, grid=(T // bt,), in_specs=[seg, vec, vec, seg, _ANY], out_specs=[gate, vec, vec],
        out_shape=[jax.ShapeDtypeStruct((T, C_TOT), BF16), jax.ShapeDtypeStruct((1, 128), F32),
                   jax.ShapeDtypeStruct((1, 128), F32)],
        input_output_aliases={4: 0}, name="gdn_gates_bwd",
        compiler_params=_params(("arbitrary",)))(p, alog_row, dtb_row, dbg, dp_in)


def _ssd_dt_fwd(p, dtb_row, alog_row, *, bt=256):
    T = p.shape[0]
    bt = min(bt, T)

    def body(d_ref, db_ref, al_ref, da_ref):
        lc, _ = _cum_mats(SSM_L)
        dt_l = _softplus(d_ref[...] + db_ref[...])
        alpha_l = _chunk_cumsum(dt_l * (-jnp.exp(al_ref[...])), lc, SSM_L)
        da_ref[...] = jnp.where(_iota2((bt, 128), 1) < SSM_H, dt_l, pltpu.roll(alpha_l, SSM_H, 1))

    v128 = pl.BlockSpec((1, 128), lambda i: (0, 0))
    return pl.pallas_call(
        body, grid=(T // bt,), in_specs=[pl.BlockSpec((bt, 128), lambda i: (i, 1)), v128, v128],
        out_specs=pl.BlockSpec((bt, 128), lambda i: (i, 0)), out_shape=jax.ShapeDtypeStruct((T, 128), F32),
        name="ssd_dt_fwd", compiler_params=_params(("parallel",)))(p, dtb_row, alog_row)


def _ssd_dt_bwd(p, dtb_row, alog_row, dda, dp_in, *, bt=256):
    T = p.shape[0]
    bt = min(bt, T)

    def body(d_ref, db_ref, al_ref, dda_ref, dpin_ref, dd_out, ddb_ref, dalog_ref):
        i = pl.program_id(0)
        heads = _iota2((bt, 128), 1) < SSM_H
        _, uc = _cum_mats(SSM_L)
        zz = d_ref[...] + db_ref[...]
        dt_l = _softplus(zz)
        a_row = -jnp.exp(al_ref[...])
        dda_v = dda_ref[...]
        da_l = _chunk_cumsum(jnp.where(heads, pltpu.roll(dda_v, 128 - SSM_H, 1), 0.0), uc, SSM_L)
        draw = jnp.where(heads, (dda_v + da_l * a_row) * _sigmoid(zz), 0.0)
        dd_out[...] = draw.astype(dd_out.dtype)

        @pl.when(i == 0)
        def _():
            ddb_ref[...] = jnp.zeros_like(ddb_ref)
            dalog_ref[...] = jnp.zeros_like(dalog_ref)

        ddb_ref[...] += jnp.sum(draw, axis=0, keepdims=True)
        dalog_ref[...] += jnp.sum(da_l * dt_l, axis=0, keepdims=True) * a_row

    seg = pl.BlockSpec((bt, 128), lambda i: (i, C_DT // 128))
    v128 = pl.BlockSpec((1, 128), lambda i: (0, 0))
    return pl.pallas_call(
        body, grid=(T // bt,),
        in_specs=[pl.BlockSpec((bt, 128), lambda i: (i, 1)), v128, v128, pl.BlockSpec((bt, 128), lambda i: (i, 0)), _ANY],
        out_specs=[seg, v128, v128],
        out_shape=[jax.ShapeDtypeStruct((T, C_TOT), BF16), jax.ShapeDtypeStruct((1, 128), F32),
                   jax.ShapeDtypeStruct((1, 128), F32)],
        input_output_aliases={4: 0}, name="ssd_dt_bwd",
        compiler_params=_params(("arbitrary",)))(p, dtb_row, alog_row, dda, dp_in)


def _gdn_post_fwd(o, p, w_x, *, bt=512):
    T = o.shape[0]
    bt = min(bt, T)

    def body(o_ref, z_ref, w_ref, out_ref):
        for h in range(GDN_H):
            sl = slice(h * 128, (h + 1) * 128)
            oh = o_ref[:, sl].astype(F32)
            r = lax.rsqrt(jnp.mean(oh * oh, axis=1, keepdims=True) + EPS)
            out_ref[:, sl] = (oh * r * w_ref[:, sl] * _silu(z_ref[:, sl].astype(F32))).astype(out_ref.dtype)

    row = pl.BlockSpec((bt, D), lambda i: (i, 0))
    return pl.pallas_call(
        body, grid=(T // bt,),
        in_specs=[row, pl.BlockSpec((bt, D), lambda i: (i, C_ZG // D)), pl.BlockSpec((1, D), lambda i: (0, 0))],
        out_specs=row, out_shape=jax.ShapeDtypeStruct((T, 2 * D), BF16), name="gdn_post_fwd",
        compiler_params=_params(("parallel",)))(o, p, w_x)


def _gdn_post_bwd(dx1, w_out, o, p, w_x, *, bt=512):
    T = o.shape[0]
    bt = min(bt, T)

    def body(dx_ref, wo_ref, o_ref, z_ref, w_ref, do_ref, dz_ref, dw_ref):
        i = pl.program_id(0)

        @pl.when(i == 0)
        def _():
            dw_ref[...] = jnp.zeros_like(dw_ref)

        dmix = _dot(dx_ref[...], wo_ref[...], _NT)
        for h in range(GDN_H):
            sl = slice(h * 128, (h + 1) * 128)
            oh, zh, wh = o_ref[:, sl].astype(F32), z_ref[:, sl].astype(F32), w_ref[:, sl]
            dm = dmix[:, sl]
            r = lax.rsqrt(jnp.mean(oh * oh, axis=1, keepdims=True) + EPS)
            ohat = oh * r
            dy = dm * _silu(zh)
            dz_ref[:, sl] = (dm * ohat * wh * _dsilu(zh)).astype(dz_ref.dtype)
            dohat = dy * wh
            do_ref[:, sl] = (r * (dohat - ohat * jnp.mean(dohat * ohat, axis=1, keepdims=True))).astype(do_ref.dtype)
            dw_ref[:, sl] += jnp.sum(dy * ohat, axis=0, keepdims=True)

    row = pl.BlockSpec((bt, D), lambda i: (i, 0))
    zcol = pl.BlockSpec((bt, D), lambda i: (i, C_ZG // D))
    vec = pl.BlockSpec((1, D), lambda i: (0, 0))
    return pl.pallas_call(
        body, grid=(T // bt,), in_specs=[row, pl.BlockSpec((D, D), lambda i: (0, 0)), row, zcol, vec],
        out_specs=[row, zcol, vec],
        out_shape=[jax.ShapeDtypeStruct((T, D), BF16), jax.ShapeDtypeStruct((T, C_TOT), BF16),
                   jax.ShapeDtypeStruct((1, D), F32)],
        name="gdn_post_bwd", compiler_params=_params(("arbitrary",)))(dx1, w_out, o, p, w_x)


def _ssd_post_fwd(y, xs, p, d_x, w, mix_in, *, bt=512):
    T = y.shape[0]
    bt = min(bt, T)

    def body(y_ref, x_ref, z_ref, d_ref, w_ref, mix_ref, out_ref):
        yg = (y_ref[...].astype(F32) + x_ref[...] * d_ref[...]) * _silu(z_ref[...].astype(F32))
        for g in range(2):
            sl = slice(g * 512, (g + 1) * 512)
            a = yg[:, sl]
            r = lax.rsqrt(jnp.mean(a * a, axis=1, keepdims=True) + EPS)
            out_ref[:, sl] = (a * r * w_ref[:, sl]).astype(out_ref.dtype)

    row = pl.BlockSpec((bt, D), lambda i: (i, 0))
    vec = pl.BlockSpec((1, D), lambda i: (0, 0))
    return pl.pallas_call(
        body, grid=(T // bt,),
        in_specs=[row, row, pl.BlockSpec((bt, D), lambda i: (i, C_ZS // D)), vec, vec, _ANY],
        out_specs=pl.BlockSpec((bt, D), lambda i: (i, 1)), out_shape=jax.ShapeDtypeStruct((T, 2 * D), BF16),
        input_output_aliases={5: 0}, name="ssd_post_fwd",
        compiler_params=_params(("parallel",)))(y, xs, p, d_x, w, mix_in)


def _ssd_post_bwd(dx1, w_out, y, xs, p, d_x, w, dp_in, *, bt=512):
    T = y.shape[0]
    bt = min(bt, T)

    def body(dx_ref, wo_ref, y_ref, x_ref, z_ref, d_ref, w_ref, dpin_ref, dyy_ref, dz_ref, dd_ref, dw_ref):
        i = pl.program_id(0)

        @pl.when(i == 0)
        def _():
            dd_ref[...] = jnp.zeros_like(dd_ref)
            dw_ref[...] = jnp.zeros_like(dw_ref)

        dmix = _dot(dx_ref[...], wo_ref[...], _NT)
        xv, zv = x_ref[...], z_ref[...].astype(F32)
        yy = y_ref[...].astype(F32) + xv * d_ref[...]
        sz = _silu(zv)
        yg = yy * sz
        parts = []
        for g in range(2):
            sl = slice(g * 512, (g + 1) * 512)
            a = yg[:, sl]
            r = lax.rsqrt(jnp.mean(a * a, axis=1, keepdims=True) + EPS)
            ah = a * r
            dout = dmix[:, sl]
            dah = dout * w_ref[:, sl]
            dw_ref[:, sl] += jnp.sum(dout * ah, axis=0, keepdims=True)
            parts.append(r * (dah - ah * jnp.mean(dah * ah, axis=1, keepdims=True)))
        dyg = jnp.concatenate(parts, axis=1)
        dyy = dyg * sz
        dyy_ref[...] = dyy
        dz_ref[...] = (dyg * yy * _dsilu(zv)).astype(dz_ref.dtype)
        dd_ref[...] += jnp.sum(dyy * xv, axis=0, keepdims=True)

    row = pl.BlockSpec((bt, D), lambda i: (i, 0))
    zcol = pl.BlockSpec((bt, D), lambda i: (i, C_ZS // D))
    vec = pl.BlockSpec((1, D), lambda i: (0, 0))
    return pl.pallas_call(
        body, grid=(T // bt,),
        in_specs=[row, pl.BlockSpec((D, D), lambda i: (1, 0)), row, row, zcol, vec, vec, _ANY],
        out_specs=[row, zcol, vec, vec],
        out_shape=[jax.ShapeDtypeStruct((T, D), F32), jax.ShapeDtypeStruct((T, C_TOT), BF16),
                   jax.ShapeDtypeStruct((1, D), F32), jax.ShapeDtypeStruct((1, D), F32)],
        input_output_aliases={7: 1}, name="ssd_post_bwd",
        compiler_params=_params(("arbitrary",)))(dx1, w_out, y, xs, p, d_x, w, dp_in)


_NEG = -1e30


def _gdn_terms(q, k, v, bx, gam_c):
    C = GDN_C
    ri, ci = _iota2((C, C), 0), _iota2((C, C), 1)
    eye, low, strict = ri == ci, ri >= ci, ri > ci
    gam_r = jnp.sum(jnp.where(eye, gam_c, 0.0), axis=0, keepdims=True)
    G = jnp.exp(jnp.where(low, gam_c - gam_r, _NEG))
    glast = jnp.sum(jnp.where(_iota2((C, 1), 0) == C - 1, gam_c, 0.0), axis=0, keepdims=True)
    eg, egl, eL = jnp.exp(gam_c), jnp.exp(glast - gam_c), jnp.exp(glast)
    kb, vb = k * bx, v * bx
    M = _dot(kb, k, _NT)
    return dict(eye=eye, low=low, strict=strict, G=G, eg=eg, egl=egl, eL=eL, kb=kb, vb=vb, M=M,
                kbg=kb * eg, qd=q * eg, kd=k * egl, q=q, k=k, v=v, bx=bx)


def _split(a):
    hi = a.astype(_MXU)
    return hi, (a - hi.astype(F32)).astype(_MXU)


def _dot3s(a, b):
    d = lambda p, q: lax.dot_general(p, q, _NN, preferred_element_type=F32)
    return d(a[0], b[0]) + d(a[0], b[1]) + d(a[1], b[0])


def _tri_inv_many(Ls, eye):
    eyef = jnp.where(eye, 1.0, 0.0)
    Ts = [eyef - L for L in Ls]
    Ps = [-L for L in Ls]
    for _ in range(5):
        sp = [_split(p) for p in Ps]
        Ps = [_dot3s(s, s) for s in sp]
        sp = [_split(p) for p in Ps]
        st = [_split(t) for t in Ts]
        Ts = [t + _dot3s(a, b) for t, a, b in zip(Ts, st, sp)]
    return Ts


def _lane_col(tile, idx):
    return jnp.sum(jnp.where(_iota2(tile.shape, 1) == idx, tile, 0.0), axis=1, keepdims=True)


def _gdn_heads(q_ref, k_ref, v_ref, bg_ref, heads):
    out = []
    bg = bg_ref[...]
    for h in heads:
        sl = slice(h * 128, (h + 1) * 128)
        out.append(_gdn_terms(q_ref[:, sl], k_ref[:, sl], v_ref[:, sl], _lane_col(bg, h), _lane_col(bg, GDN_H + h)))
    return out


def _gdn_prep(qk, v, bg, ride=None):
    T = qk.shape[0]
    N = T // GDN_C
    C, CS = GDN_C, GDN_LOCAL_CHUNKS
    NB = N // CS
    n_ride = ride.n if ride else 0

    def body(q_ref, k_ref, v_ref, bg_ref, *rest):
        ride_in = rest[:n_ride]
        u_ref, w_ref, qd_ref, kd_ref, p_ref, t_ref = rest[n_ride:n_ride + 6]
        ride_out = rest[n_ride + 6:2 * n_ride + 6]
        if ride:
            @pl.when(pl.program_id(0) == 0)
            def _():
                ride.start(ride_in, ride_out, rest[-3:])

            @pl.when(pl.program_id(0) == NB - 1)
            def _():
                ride.wait(ride_in, ride_out, rest[-3:])

        items = [(c, h) for c in range(CS) for h in range(GDN_H)]
        views = [[r.at[pl.ds(c * C, C)] for r in (q_ref, k_ref, v_ref, bg_ref)] for c in range(CS)]
        ts = [_gdn_heads(*views[c], [h])[0] for c, h in items]
        Ts = _tri_inv_many([jnp.where(t["strict"], t["M"] * t["G"], 0.0) for t in ts], ts[0]["eye"])
        for (c, h), t, Tm in zip(items, ts, Ts):
            tok = slice(c * C, (c + 1) * C)
            sl = slice(h * 128, (h + 1) * 128)
            rows = slice(h * C, (h + 1) * C)
            u_ref[tok, sl] = _dot(Tm, t["vb"])
            w_ref[tok, sl] = _dot(Tm, t["kbg"]).astype(w_ref.dtype)
            qd_ref[tok, sl] = t["qd"].astype(qd_ref.dtype)
            kd_ref[tok, sl] = t["kd"].astype(kd_ref.dtype)
            p_ref[c, rows, :] = _dot(t["q"], t["k"], _NT) * t["G"]
            t_ref[c, rows, :] = Tm

    blk = lambda c: pl.BlockSpec((CS * C, D), lambda n: (n, c))
    sq = pl.BlockSpec((CS, GDN_H * C, C), lambda n: (n, 0, 0))
    in_specs = [blk(0), blk(1), blk(0), pl.BlockSpec((CS * C, 128), lambda n: (n, 0))]
    out_specs = [blk(0), blk(0), blk(0), blk(0), sq, sq]
    out_shape = [jax.ShapeDtypeStruct((T, D), F32), jax.ShapeDtypeStruct((T, D), BF16),
                 jax.ShapeDtypeStruct((T, D), BF16), jax.ShapeDtypeStruct((T, D), BF16),
                 jax.ShapeDtypeStruct((N, GDN_H * C, C), F32), jax.ShapeDtypeStruct((N, GDN_H * C, C), F32)]
    ins = [qk, qk, v, bg]
    if ride:
        ins, in_specs = ins + ride.srcs, in_specs + ride.specs
        out_shape, out_specs = out_shape + ride.out_shape, out_specs + ride.specs
    res = pl.pallas_call(
        body, grid=(NB,), in_specs=in_specs, out_specs=out_specs, out_shape=out_shape,
        scratch_shapes=ride.scratch if ride else [], name="gdn_prep",
        compiler_params=_params(("arbitrary",) if ride else ("parallel",)))(*ins)
    return (list(res[:6]), list(res[6:])) if ride else list(res)


def _gdn_scan_fwd(u, w, qd, kd, pm, bg):
    T = u.shape[0]
    N = T // GDN_C
    C, CS = GDN_C, GDN_SCAN_CHUNKS

    def body(u_ref, w_ref, qd_ref, kd_ref, p_ref, bg_ref, o_ref, vn_ref, ss_ref, S_scr):
        n = pl.program_id(0)

        @pl.when(n == 0)
        def _():
            S_scr[...] = jnp.zeros_like(S_scr)

        sls = [slice(h * 128, (h + 1) * 128) for h in range(GDN_H)]
        for c in range(CS):
            rows = slice(c * C, (c + 1) * C)
            glast = bg_ref[(c + 1) * C - 1:(c + 1) * C, :]
            Ss = [S_scr[:, sl] for sl in sls]
            vns = [u_ref[rows, sl] - _dot(w_ref[rows, sl], S) for sl, S in zip(sls, Ss)]
            for h, (sl, S, vn) in enumerate(zip(sls, Ss, vns)):
                ss_ref[c, :, sl] = S.astype(ss_ref.dtype)
                vn_ref[rows, sl] = vn.astype(vn_ref.dtype)
                o_ref[rows, sl] = (_dot(qd_ref[rows, sl], S)
                                   + _dot(p_ref[c, h * C:(h + 1) * C, :], vn)).astype(o_ref.dtype)
                S_scr[:, sl] = S * jnp.exp(_lane_col(glast, GDN_H + h)) + _dot(kd_ref[rows, sl], vn, _TN)

    blk = pl.BlockSpec((CS * C, D), lambda n: (n, 0))
    return dict(
        body=body, steps=N // CS, ins=[u, w, qd, kd, pm, bg],
        in_specs=[blk, blk, blk, blk, pl.BlockSpec((CS, GDN_H * C, C), lambda n: (n, 0, 0)),
                  pl.BlockSpec((CS * C, 128), lambda n: (n, 0))],
        out_specs=[blk, blk, pl.BlockSpec((CS, GDN_DK, D), lambda n: (n, 0, 0))],
        out_shape=[jax.ShapeDtypeStruct((T, D), BF16), jax.ShapeDtypeStruct((T, D), BF16),
                   jax.ShapeDtypeStruct((N, GDN_DK, D), BF16)],
        scratch=[pltpu.VMEM((GDN_DK, D), F32)])


def _gdn_scan_bwd(w, qd, kd, pm, bg, do):
    T = w.shape[0]
    N = T // GDN_C
    C, CS = GDN_C, GDN_SCAN_CHUNKS
    NB = N // CS

    def body(w_ref, qd_ref, kd_ref, p_ref, bg_ref, do_ref, dvn_ref, ds_ref, dS_scr):
        n = pl.program_id(0)

        @pl.when(n == 0)
        def _():
            dS_scr[...] = jnp.zeros_like(dS_scr)

        sls = [slice(h * 128, (h + 1) * 128) for h in range(GDN_H)]
        for c in reversed(range(CS)):
            rows = slice(c * C, (c + 1) * C)
            glast = bg_ref[(c + 1) * C - 1:(c + 1) * C, :]
            dSs = [dS_scr[:, sl] for sl in sls]
            dvns = [_dot(p_ref[c, h * C:(h + 1) * C, :], do_ref[rows, sl], _TN) + _dot(kd_ref[rows, sl], dS2)
                    for h, (sl, dS2) in enumerate(zip(sls, dSs))]
            for h, (sl, dS2, dvn) in enumerate(zip(sls, dSs, dvns)):
                ds_ref[c, :, sl] = dS2.astype(ds_ref.dtype)
                dvn_ref[rows, sl] = dvn.astype(dvn_ref.dtype)
                dS_scr[:, sl] = (dS2 * jnp.exp(_lane_col(glast, GDN_H + h))
                                 + _dot(qd_ref[rows, sl], do_ref[rows, sl], _TN) - _dot(w_ref[rows, sl], dvn, _TN))

    blk = pl.BlockSpec((CS * C, D), lambda n: (NB - 1 - n, 0))
    return dict(
        body=body, steps=NB, ins=[w, qd, kd, pm, bg, do],
        in_specs=[blk, blk, blk, pl.BlockSpec((CS, GDN_H * C, C), lambda n: (NB - 1 - n, 0, 0)),
                  pl.BlockSpec((CS * C, 128), lambda n: (NB - 1 - n, 0)), blk],
        out_specs=[blk, pl.BlockSpec((CS, GDN_DK, D), lambda n: (NB - 1 - n, 0, 0))],
        out_shape=[jax.ShapeDtypeStruct((T, D), BF16), jax.ShapeDtypeStruct((N, GDN_DK, D), BF16)],
        scratch=[pltpu.VMEM((GDN_DK, D), F32)])


def _gdn_rest_bwd(qk, v, bg, s_save, t_save, vn, dvn, ds_save, do, ride=None):
    T = qk.shape[0]
    N = T // GDN_C
    C, CS = GDN_C, GDN_REST_CHUNKS
    NB = N // CS
    n_ride = ride.n if ride else 0

    def body(q_ref, k_ref, v_ref, bg_ref, ss_ref, ts_ref, vn_ref, dvn_ref, ds_ref, do_ref, *rest):
        ride_in = rest[:n_ride]
        dqkv_ref, dbg_ref = rest[n_ride:n_ride + 2]
        ride_out = rest[n_ride + 2:2 * n_ride + 2]
        if ride:
            @pl.when(pl.program_id(0) == 0)
            def _():
                ride.start(ride_in, ride_out, rest[-3:])

            @pl.when(pl.program_id(0) == NB - 1)
            def _():
                ride.wait(ride_in, ride_out, rest[-3:])

        items = [(c, h) for c in range(CS) for h in range(GDN_H)]
        toks = [slice(c * C, (c + 1) * C) for c, _ in items]
        sls = [slice(h * 128, (h + 1) * 128) for _, h in items]
        views = [[r.at[pl.ds(c * C, C)] for r in (q_ref, k_ref, v_ref, bg_ref)] for c in range(CS)]
        ts = [_gdn_heads(*views[c], [h])[0] for c, h in items]
        Ss = [ss_ref[c, :, sl] for (c, _), sl in zip(items, sls)]
        Tms = [ts_ref[c, h * C:(h + 1) * C, :] for c, h in items]
        dS2s = [ds_ref[c, :, sl] for (c, _), sl in zip(items, sls)]
        dos = [do_ref[tok, sl] for tok, sl in zip(toks, sls)]
        vns = [vn_ref[tok, sl] for tok, sl in zip(toks, sls)]
        dvns = [dvn_ref[tok, sl] for tok, sl in zip(toks, sls)]
        Qs = [_dot(t["q"], t["k"], _NT) for t in ts]
        dws = [-_dot(dvn, S, _NT) for dvn, S in zip(dvns, Ss)]
        dqds = [_dot(do, S, _NT) for do, S in zip(dos, Ss)]
        dPs = [jnp.where(t["low"], _dot(do, vn, _NT), 0.0) for t, do, vn in zip(ts, dos, vns)]
        dkds = [_dot(vn, dS2, _NT) for vn, dS2 in zip(vns, dS2s)]
        dTs = [_dot(dvn, t["vb"], _NT) + _dot(dw, t["kbg"], _NT) for t, dvn, dw in zip(ts, dvns, dws)]
        dvbs = [_dot(Tm, dvn, _TN) for Tm, dvn in zip(Tms, dvns)]
        dkbgs = [_dot(Tm, dw, _TN) for Tm, dw in zip(Tms, dws)]
        TdTs = [_dot(Tm, dT, _TN) for Tm, dT in zip(Tms, dTs)]
        dLs = [jnp.where(t["strict"], -_dot(TdT, Tm, _NT), 0.0) for t, TdT, Tm in zip(ts, TdTs, Tms)]
        dMs = [dL * t["G"] for t, dL in zip(ts, dLs)]
        dQs = [dP * t["G"] for t, dP in zip(ts, dPs)]
        dkbs = [_dot(dM, t["k"]) + dkbg * t["eg"] for t, dM, dkbg in zip(ts, dMs, dkbgs)]
        rs = lambda a: jnp.sum(a, axis=1, keepdims=True)
        lane = _iota2((C, 128), 1)
        last = _iota2((C, 1), 0) == C - 1
        dbg = [jnp.zeros((C, 128), F32) for _ in range(CS)]
        for i, (c, h) in enumerate(items):
            t, sl, tok = ts[i], sls[i], toks[i]
            E = (dLs[i] * t["M"] + dPs[i] * Qs[i]) * t["G"]
            dqkv_ref[tok, sl] = _dot(dQs[i], t["k"]) + dqds[i] * t["eg"]
            dqkv_ref[tok, D + h * 128:D + (h + 1) * 128] = (
                _dot(dQs[i], t["q"], _TN) + _dot(dMs[i], t["kb"], _TN) + dkds[i] * t["egl"] + dkbs[i] * t["bx"])
            dqkv_ref[tok, 2 * D + h * 128:2 * D + (h + 1) * 128] = dvbs[i] * t["bx"]
            dbeta_c = rs(dkbs[i] * t["k"] + dvbs[i] * t["v"])
            dkd_kd = dkds[i] * t["kd"]
            dgam_c = rs(dqds[i] * t["qd"]) + rs(dkbgs[i] * t["kbg"]) - rs(dkd_kd) + rs(E)
            dgam_r = -jnp.sum(E, axis=0, keepdims=True)
            dgam_c = dgam_c + jnp.sum(jnp.where(t["eye"], dgam_r, 0.0), axis=1, keepdims=True)
            dlast = _sum_all(dkd_kd) + t["eL"] * _sum_all(Ss[i].astype(F32) * dS2s[i].astype(F32))
            dgam_c = dgam_c + jnp.where(last, dlast, 0.0)
            dbg[c] = dbg[c] + jnp.where(lane == h, dbeta_c, 0.0) + jnp.where(lane == GDN_H + h, dgam_c, 0.0)
        for c in range(CS):
            dbg_ref[c * C:(c + 1) * C, :] = dbg[c]

    blk = lambda c: pl.BlockSpec((CS * C, D), lambda n: (n, c))
    st = pl.BlockSpec((CS, GDN_DK, D), lambda n: (n, 0, 0))
    seg = pl.BlockSpec((CS * C, 128), lambda n: (n, 0))
    in_specs = [blk(0), blk(1), blk(0), seg, st,
                pl.BlockSpec((CS, GDN_H * C, C), lambda n: (n, 0, 0)), blk(0), blk(0), st, blk(0)]
    out_specs = [pl.BlockSpec((CS * C, 3 * D), lambda n: (n, 0)), seg]
    out_shape = [jax.ShapeDtypeStruct((T, 3 * D), F32), jax.ShapeDtypeStruct((T, 128), F32)]
    ins = [qk, qk, v, bg, s_save, t_save, vn, dvn, ds_save, do]
    if ride:
        ins, in_specs = ins + ride.srcs, in_specs + ride.specs
        out_shape, out_specs = out_shape + ride.out_shape, out_specs + ride.specs
    res = pl.pallas_call(
        body, grid=(NB,), in_specs=in_specs, out_specs=out_specs, out_shape=out_shape,
        scratch_shapes=ride.scratch if ride else [], name="gdn_rest_bwd",
        compiler_params=_params(("arbitrary",) if ride else ("parallel",)))(*ins)
    return (list(res[:2]), list(res[2:])) if ride else list(res)


def _ssd_seg(al_pair, half, s):
    L = SSM_L
    ri, ci = _iota2((L, L), 0), _iota2((L, L), 1)
    ac = jnp.max(jnp.where(half == s, al_pair, _NEG), axis=1, keepdims=True)
    ar = jnp.sum(jnp.where(ri == ci, ac, 0.0), axis=0, keepdims=True)
    return jnp.exp(jnp.where(ri >= ci, ac - ar, _NEG))


def _last_row(a):
    return jnp.sum(jnp.where(_iota2((a.shape[0], 1), 0) == a.shape[0] - 1, a, 0.0), axis=0, keepdims=True)


def _ssd_expand(da_ref):
    da = da_ref[...]
    return _dot_sel(da, _expand_mats(6, 0)[0]), _dot_sel(da, _expand_mats(6, SSM_H)[0])


def _ssd_core_fwd(xbc, da):
    T = xbc.shape[0]
    L, CS = SSM_L, SSM_SCAN_CHUNKS
    Nc = T // L

    def body(x_all, bc_all, da_all, y_all, hs_all, H_scr):
        @pl.when(pl.program_id(0) == 0)
        def _():
            H_scr[...] = jnp.zeros_like(H_scr)

        for cc in range(CS):
            rows = pl.ds(cc * L, L)
            chunk(x_all.at[rows], bc_all.at[rows], da_all.at[rows], y_all.at[rows], hs_all.at[cc], H_scr)

    def chunk(x_ref, bc_ref, da_ref, y_ref, hs_ref, H_scr):
        dt_ref, al_ref = _ssd_expand(da_ref)
        half = _iota2((L, 128), 1) >> 6
        for g in range(2):
            gs = slice(g * 512, (g + 1) * 512)
            Bg = bc_ref[:, g * 128:(g + 1) * 128]
            Cg = bc_ref[:, 256 + g * 128:256 + (g + 1) * 128]
            alg = al_ref[:, gs]
            alast = _last_row(alg)
            xdt = x_ref[:, gs] * dt_ref[:, gs]
            Hg = H_scr[:, gs]
            hs_ref[:, gs] = Hg
            CB = _dot(Cg, Bg, _NT)
            y_off = jnp.exp(alg) * _dot(Cg, Hg)
            H_scr[:, gs] = Hg * jnp.exp(alast) + _dot(Bg, jnp.exp(alast - alg) * xdt, _TN)
            for j in range(4):
                ps = slice(g * 512 + j * 128, g * 512 + (j + 1) * 128)
                al_pair = al_ref[:, ps]
                xp = x_ref[:, ps] * dt_ref[:, ps]
                ys = [_dot(_ssd_seg(al_pair, half, s) * CB, xp) for s in range(2)]
                y_ref[:, ps] = (y_off[:, j * 128:(j + 1) * 128]
                                + jnp.where(half == 0, ys[0], ys[1])).astype(y_ref.dtype)

    row = pl.BlockSpec((CS * L, D), lambda c: (c, 0))
    return dict(
        body=body, steps=Nc // CS, ins=[xbc, xbc, da],
        in_specs=[row, pl.BlockSpec((CS * L, 512), lambda c: (c, 2)), pl.BlockSpec((CS * L, 128), lambda c: (c, 0))],
        out_specs=[row, pl.BlockSpec((CS, SSM_N, D), lambda c: (c, 0, 0))],
        out_shape=[jax.ShapeDtypeStruct((T, D), BF16), jax.ShapeDtypeStruct((Nc, SSM_N, D), F32)],
        scratch=[pltpu.VMEM((SSM_N, D), F32)])


def _ssd_core_bwd(xbc, da, h_save, dyy, d_x):
    T = xbc.shape[0]
    L, CS = SSM_L, SSM_SCAN_CHUNKS
    Nc = T // L
    NB = Nc // CS

    def body(x_all, bc_all, da_all, hs_all, dy_all, d_ref, dx_all, dda_all, dH_scr, ddt_ref, dal_ref):
        @pl.when(pl.program_id(0) == 0)
        def _():
            dH_scr[...] = jnp.zeros_like(dH_scr)

        for cc in reversed(range(CS)):
            rows = pl.ds(cc * L, L)
            chunk(x_all.at[rows], bc_all.at[rows], da_all.at[rows], hs_all.at[cc], dy_all.at[rows],
                  d_ref, dx_all.at[rows], ddt_ref, dal_ref, dH_scr)
            dda_all[rows, :] = (_dot_sel(ddt_ref[...], _expand_mats(6, 0)[1])
                                + _dot_sel(dal_ref[...], _expand_mats(6, SSM_H)[1]))

    def chunk(x_ref, bc_ref, da_ref, hs_ref, dy_ref, d_ref, dx_ref, ddt_ref, dal_ref, dH_scr):
        dt_ref, al_ref = _ssd_expand(da_ref)
        lane = _iota2((L, 128), 1)
        half = lane >> 6
        rowi = _iota2((L, 1), 0)
        ri, ci = _iota2((L, L), 0), _iota2((L, L), 1)
        for g in range(2):
            gs = slice(g * 512, (g + 1) * 512)
            Bg = bc_ref[:, g * 128:(g + 1) * 128]
            Cg = bc_ref[:, 256 + g * 128:256 + (g + 1) * 128]
            alg = al_ref[:, gs]
            alast = _last_row(alg)
            eal, edec, eL = jnp.exp(alg), jnp.exp(alast - alg), jnp.exp(alast)
            xg, dtg, dYg = x_ref[:, gs], dt_ref[:, gs], dy_ref[:, gs]
            xdt = xg * dtg
            Hg = hs_ref[:, gs]
            dH2 = dH_scr[:, gs]
            CB = _dot(Cg, Bg, _NT)
            dYe = eal * dYg
            dH_scr[:, gs] = dH2 * eL + _dot(Cg, dYe, _TN)
            dC = _dot(dYe, Hg, _NT)
            zg = edec * xdt
            dz = _dot(Bg, dH2)
            dB = _dot(zg, dH2, _NT)
            tz = dz * zg
            dal = dYe * _dot(Cg, Hg) - tz
            dalast = jnp.sum(tz, axis=0, keepdims=True) + eL * jnp.sum(Hg * dH2, axis=0, keepdims=True)
            dal = dal + jnp.where(rowi == L - 1, dalast, 0.0)
            dxdt_g = edec * dz
            dx_ref[:, gs] = dxdt_g * dtg + dYg * d_ref[:, gs]
            ddt_ref[:, gs] = dxdt_g * xg
            dal_ref[:, gs] = dal
            dCB = jnp.zeros((L, L), F32)
            for j in range(4):
                ps = slice(g * 512 + j * 128, g * 512 + (j + 1) * 128)
                al_pair = al_ref[:, ps]
                xp = x_ref[:, ps] * dt_ref[:, ps]
                dYp = dy_ref[:, ps]
                dxp = []
                dal_p = jnp.zeros((L, 128), F32)
                for s in range(2):
                    seg = _ssd_seg(al_pair, half, s)
                    W = seg * CB
                    dW = _dot(jnp.where(half == s, dYp, 0.0), xp, _NT)
                    dxp.append(_dot(W, dYp, _TN))
                    dCB = dCB + dW * seg
                    Es = dW * W
                    dac = jnp.sum(Es, axis=1, keepdims=True) - jnp.sum(
                        jnp.where(ri == ci, jnp.sum(Es, axis=0, keepdims=True), 0.0), axis=1, keepdims=True)
                    dal_p = dal_p + jnp.where(lane == 64 * s, dac, 0.0)
                dxdt_p = jnp.where(half == 0, dxp[0], dxp[1])
                dx_ref[:, ps] += dxdt_p * dt_ref[:, ps]
                ddt_ref[:, ps] += dxdt_p * x_ref[:, ps]
                dal_ref[:, ps] += dal_p
            dx_ref[:, D + g * 128:D + (g + 1) * 128] = dB + _dot(dCB, Cg, _TN)
            dx_ref[:, D + 256 + g * 128:D + 256 + (g + 1) * 128] = dC + _dot(dCB, Bg)

    row = pl.BlockSpec((CS * L, D), lambda c: (NB - 1 - c, 0))
    bcs = pl.BlockSpec((CS * L, 512), lambda c: (NB - 1 - c, 2))
    seg = pl.BlockSpec((CS * L, 128), lambda c: (NB - 1 - c, 0))
    return dict(
        body=body, steps=NB, ins=[xbc, xbc, da, h_save, dyy, d_x],
        in_specs=[row, bcs, seg, pl.BlockSpec((CS, SSM_N, D), lambda c: (NB - 1 - c, 0, 0)), row,
                  pl.BlockSpec((1, D), lambda c: (0, 0))],
        out_specs=[pl.BlockSpec((CS * L, D + 512), lambda c: (NB - 1 - c, 0)), seg],
        out_shape=[jax.ShapeDtypeStruct((T, D + 512), F32), jax.ShapeDtypeStruct((T, 128), F32)],
        scratch=[pltpu.VMEM((SSM_N, D), F32), pltpu.VMEM((L, D), F32), pltpu.VMEM((L, D), F32)])


def _run_scans(parts, *, name):
    steps = parts[0]["steps"]
    assert all(p["steps"] == steps for p in parts)
    cnt = lambda key: [len(p[key]) for p in parts]
    n_in, n_out, n_scr = cnt("ins"), cnt("out_shape"), cnt("scratch")

    def body(*refs):
        ins, outs, scr = refs[:sum(n_in)], refs[sum(n_in):sum(n_in) + sum(n_out)], refs[sum(n_in) + sum(n_out):]
        oi = oo = os_ = 0
        for p, a, b, c in zip(parts, n_in, n_out, n_scr):
            p["body"](*ins[oi:oi + a], *outs[oo:oo + b], *scr[os_:os_ + c])
            oi, oo, os_ = oi + a, oo + b, os_ + c

    cat = lambda key: [v for p in parts for v in p[key]]
    res = pl.pallas_call(
        body, grid=(steps,), in_specs=cat("in_specs"), out_specs=cat("out_specs"), out_shape=cat("out_shape"),
        scratch_shapes=cat("scratch"), name=name, compiler_params=_params(("arbitrary",)))(*cat("ins"))
    out, o = [], 0
    for b in n_out:
        out.append(list(res[o:o + b]))
        o += b
    return out


_EARLY = ("w_out", "wq_mem", "wk_mem", "wv_mem", "wo_mem")
_LATE = ("w_up", "w_down")
_GRADS_MLP = ("w_down", "w_up")
_GRADS_MID = ("wo_mem", "wq_mem", "wk_mem", "wv_mem", "w_out")


def _gather_ride(shards, names):
    return None if shards is None else _Ride([shards[n] for n in names], shard=True)


def _grad_ride(shards, G, names):
    return None if shards is None else _Ride([_slots_from_full(n, G[n]) for n in names], shard=False)


def _local_step(x, mem, tgt, W, shards=None):
    T = x.shape[0]
    W = dict(W)
    cw_qk, cw_v = W["gdn_conv_w"][:, :2 * D], W["gdn_conv_w"][:, 2 * D:]
    h1 = _rmsnorm_fwd(x, W["norm1_w"], name="norm1_fwd")
    ride = _gather_ride(shards, _EARLY)
    pg = _mm(h1, W["w_in_pad"], b_cols=(C_GATE, C_TOT - C_GATE), name="in_proj_gates")
    p = _mm(h1, W["w_in_pad"], b_cols=(0, C_GATE), out_dtype=BF16, bn_cap=1664, name="in_proj", ride=ride)
    if ride:
        p, got = p
        W.update({n: _full_from_slots(n, g) for n, g in zip(_EARLY, got)})
    qk = _conv_fwd(p, C_QKV, 2 * D, cw_qk, None, l2=True, name="gdn_conv_qk_fwd")
    v_g = _conv_fwd(p, C_QKV + 2 * D, D, cw_v, None, l2=False, name="gdn_conv_v_fwd")
    bg = _gdn_gates_fwd(pg, W["gdn_alog_row"], W["gdn_dtb_row"])
    ride = _gather_ride(shards, _LATE)
    prep = _gdn_prep(qk, v_g, bg, ride)
    if ride:
        prep, got = prep
        W.update({n: _full_from_slots(n, g) for n, g in zip(_LATE, got)})
    u_g, w_g, qd_g, kd_g, p_g, t_save = prep
    xbc = _conv_fwd(p, C_XBC, D + 512, W["ssm_conv_w"], W["ssm_conv_b"], l2=False, name="ssm_conv_fwd", bc=512)
    da_s = _ssd_dt_fwd(pg, W["ssm_dtb_row"], W["ssm_alog_row"])
    (o_g, vn_g, s_save), (y_s, h_save) = _run_scans(
        [_gdn_scan_fwd(u_g, w_g, qd_g, kd_g, p_g, bg), _ssd_core_fwd(xbc, da_s)], name="scans_fwd")
    mix = _gdn_post_fwd(o_g, p, W["gdn_norm_x"])
    mix = _ssd_post_fwd(y_s, xbc, p, W["ssm_d_x"], W["ssm_norm_w"].reshape(1, D), mix)
    x1, h2 = _mm(mix, W["w_out"], epi="res_norm", extra=(x, W["norm2_w"]), bm=512, name="out_proj")
    qm = _mm(h2, W["wq_mem"], out_dtype=BF16, name="q_proj")
    m = _rmsnorm_fwd(mem, W["mem_norm_w"], name="mem_norm_fwd")
    km = _mm(m, W["wk_mem"], name="k_proj")
    vm = _mm(m, W["wv_mem"], name="v_proj")
    oa = _attn_fwd(qm, km, vm)
    x2, h3 = _mm(oa, W["wo_mem"], epi="res_norm", extra=(x1, W["norm3_w"]), bm=512, name="o_proj")
    u, act = _mm(h3, W["w_up"], epi="relu2", out_dtype=BF16, bn_cap=2048, b_resident=True, name="mlp_up")
    dx3, g_final, loss = _mm(act, W["w_down"], epi="res_loss", extra=(x2, tgt, W["final_norm_w"]), bk_cap=1024,
                             b_resident=True, name="mlp_down_loss")
    G = {"final_norm_w": g_final.reshape(D)}
    dpre = _mm(dx3, W["w_down"], dims="nt", epi="mul2", extra=u, out_dtype=BF16, b_resident=True, name="mlp_down_dx")
    G["w_down"] = _mm(act, dx3, dims="tn", out_dtype=BF16, name="mlp_down_dw")
    G["w_up"] = _mm(h3, dpre, dims="tn", out_dtype=BF16, bk_cap=4096, name="mlp_up_dw")
    dx2, gw = _mm(dpre, W["w_up"], dims="nt", epi="norm_bwd", extra=(x2, dx3, W["norm3_w"]), bk_cap=1024,
                  b_resident=True, name="mlp_up_dx")
    G["norm3_w"] = gw.reshape(D)
    do_a = _mm(dx2, W["wo_mem"], dims="nt", out_dtype=BF16, name="o_proj_dx")
    G["wo_mem"] = _mm(oa, dx2, dims="tn", out_dtype=BF16, name="o_proj_dw")
    dq, dk, dv = _attn_bwd(qm, km, vm, do_a)
    G["wq_mem"] = _mm(h2, dq, dims="tn", out_dtype=BF16, name="q_proj_dw")
    dx1, gw = _mm(dq, W["wq_mem"], dims="nt", epi="norm_bwd", extra=(x1, dx2, W["norm2_w"]), bm=512,
                  name="q_proj_dx")
    G["norm2_w"] = gw.reshape(D)
    G["wk_mem"] = _mm(m, dk, dims="tn", out_dtype=BF16, name="k_proj_dw")
    G["wv_mem"] = _mm(m, dv, dims="tn", out_dtype=BF16, name="v_proj_dw")
    dm = _mm(dk, W["wk_mem"], dims="nt", name="k_proj_dx")
    dm = _mm(dv, W["wv_mem"], dims="nt", epi="res", extra=dm, name="v_proj_dx")
    _, G["mem_norm_w"] = _rmsnorm_bwd(mem, W["mem_norm_w"], dm, None, name="mem_norm_bwd")
    G["w_out"] = _mm(mix, dx1, dims="tn", out_dtype=BF16, name="out_proj_dw")
    do_g, dp, G["gdn_norm_x"] = _gdn_post_bwd(dx1, W["w_out"], o_g, p, W["gdn_norm_x"])
    dyy, dp, G["ssm_d_x"], G["ssm_norm_w"] = _ssd_post_bwd(dx1, W["w_out"], y_s, xbc, p, W["ssm_d_x"],
                                                          W["ssm_norm_w"].reshape(1, D), dp)
    (dvn_g, ds_save), (dxbc, dda_s) = _run_scans(
        [_gdn_scan_bwd(w_g, qd_g, kd_g, p_g, bg, do_g), _ssd_core_bwd(xbc, da_s, h_save, dyy, W["ssm_d_x"])],
        name="scans_bwd")
    ride = _grad_ride(shards, G, _GRADS_MLP)
    rest = _gdn_rest_bwd(qk, v_g, bg, s_save, t_save, vn_g, dvn_g, ds_save, do_g, ride)
    if ride:
        rest, got = rest
        G.update(zip(_GRADS_MLP, got))
    dqkvn, dbg = rest
    dy_qk, gcw_qk, _ = _conv_bwd_act(p, C_QKV, 2 * D, cw_qk, None, dqkvn, 0, l2=True, name="gdn_conv_qk_bwd_act")
    dy_v, gcw_v, _ = _conv_bwd_act(p, C_QKV + 2 * D, D, cw_v, None, dqkvn, 2 * D, l2=False,
                                   name="gdn_conv_v_bwd_act")
    G["gdn_conv_w"] = jnp.concatenate([gcw_qk, gcw_v], axis=1)
    dp = _conv_bwd_in(dy_qk, cw_qk, dp, C_QKV, T, name="gdn_conv_qk_bwd_in")
    dp = _conv_bwd_in(dy_v, cw_v, dp, C_QKV + 2 * D, T, name="gdn_conv_v_bwd_in")
    dp, G["gdn_alog_row"], G["gdn_dtb_row"] = _gdn_gates_bwd(pg, W["gdn_alog_row"], W["gdn_dtb_row"], dbg, dp)
    dy_s, G["ssm_conv_w"], G["ssm_conv_b"] = _conv_bwd_act(p, C_XBC, D + 512, W["ssm_conv_w"], W["ssm_conv_b"],
                                                           dxbc, 0, l2=False, name="ssm_conv_bwd_act", bc=512)
    dp = _conv_bwd_in(dy_s, W["ssm_conv_w"], dp, C_XBC, T, name="ssm_conv_bwd_in", bc=512)
    dp, G["ssm_dtb_row"], G["ssm_alog_row"] = _ssd_dt_bwd(pg, W["ssm_dtb_row"], W["ssm_alog_row"], dda_s, dp)
    ride = _grad_ride(shards, G, _GRADS_MID)
    g_in = _mm(h1, dp, dims="tn", out_dtype=BF16, bn_cap=1152, bk_cap=4096, name="in_proj_dw", ride=ride)
    if ride:
        g_in, got = g_in
        G.update(zip(_GRADS_MID, got))
    G["w_in"] = _unpad_w_in(g_in)
    ride = _grad_ride(shards, G, ("w_in",))
    res = _mm(dp, W["w_in_pad"], dims="nt", epi="norm_bwd", extra=(x, dx1, W["norm1_w"]), b_resident=True,
              name="in_proj_dx", ride=ride)
    if ride:
        res, got = res
        G["w_in"] = got[0]
    dx, gw = res
    G["norm1_w"] = gw.reshape(D)
    return loss, dx, G


def _all_gather(shards, out_dtype, *, name):
    n = len(shards)

    def body(*refs):
        x_refs, out_refs, stage = refs[:n], refs[n:2 * n], refs[2 * n:3 * n]
        send_sems, recv_sems, local_sems = refs[3 * n:]
        x, y, c = _place()
        me, sibling = (x, y, c), (x, y, 1 - c)
        chips = [(1 - x, y), (x, 1 - y), (1 - x, 1 - y)]

        def slot(px, py, pc):
            return 4 * px + 2 * py + pc

        def copy(a, k, block, to, src=None):
            dst = out_refs[a].at[slot(*block)]
            return pltpu.make_async_remote_copy(
                src_ref=dst if src is None else src, dst_ref=dst, send_sem=send_sems.at[a, k],
                recv_sem=recv_sems.at[a, k], device_id=to, device_id_type=_MESH)

        for a in range(n):
            stage[a][...] = x_refs[a][...].astype(out_dtype)
        mine = [pltpu.make_async_copy(stage[a], out_refs[a].at[slot(*me)], local_sems.at[a]) for a in range(n)]
        for cp in mine:
            cp.start()
        first = []
        for a in range(n):
            first.append(copy(a, 0, me, sibling, src=stage[a]))
            first += [copy(a, 1 + j, me, (*chip, c), src=stage[a]) for j, chip in enumerate(chips)]
        for cp in first:
            cp.start()
        passed = [[copy(a, 4 + j, (*chip, c), sibling) for j, chip in enumerate(chips)] for a in range(n)]
        for j, chip in enumerate(chips):
            for a in range(n):
                copy(a, 1 + j, (*chip, c), me).wait_recv()
                passed[a][j].start()
        for a in range(n):
            copy(a, 0, sibling, me).wait_recv()
            for j, chip in enumerate(chips):
                copy(a, 4 + j, (*chip, 1 - c), me).wait_recv()
        for cp in first + [cp for row in passed for cp in row]:
            cp.wait_send()
        for cp in mine:
            cp.wait()

    outs = pl.pallas_call(
        body, in_specs=[_VM] * n, out_specs=[_ANY] * n,
        out_shape=[jax.ShapeDtypeStruct((N_DEV,) + s.shape, out_dtype) for s in shards],
        scratch_shapes=[pltpu.VMEM(s.shape, out_dtype) for s in shards]
        + [pltpu.SemaphoreType.DMA((n, 7)), pltpu.SemaphoreType.DMA((n, 7)), pltpu.SemaphoreType.DMA((n,))],
        name=name, compiler_params=pltpu.CompilerParams(vmem_limit_bytes=VMEM_LIMIT))(*shards)
    return list(outs)


def _cast_bf16(arrs, *, name):
    n = len(arrs)

    def body(*refs):
        for a in range(n):
            refs[n + a][...] = refs[a][...].astype(BF16)

    return list(pl.pallas_call(
        body, in_specs=[_VM] * n, out_specs=[_VM] * n,
        out_shape=[jax.ShapeDtypeStruct(s.shape, BF16) for s in arrs], name=name,
        compiler_params=pltpu.CompilerParams(vmem_limit_bytes=VMEM_LIMIT))(*arrs))


def _sum8(a, *, name):
    _, R, Cc = a.shape
    br = _pick_rows(R, 128)

    def body(a_ref, o_ref):
        s = a_ref[0].astype(F32)
        for k in range(1, N_DEV):
            s = s + a_ref[k].astype(F32)
        o_ref[...] = s

    return pl.pallas_call(
        body, grid=(R // br,), in_specs=[pl.BlockSpec((N_DEV, br, Cc), lambda i: (0, i, 0))],
        out_specs=pl.BlockSpec((br, Cc), lambda i: (i, 0)), out_shape=jax.ShapeDtypeStruct((R, Cc), F32),
        name=name, compiler_params=_params(("parallel",)))(a)


def _pick_rows(R, cap):
    if R <= cap:
        return R
    for d in range(cap, 7, -8):
        if R % d == 0:
            return d
    return R


def _adamw(w, g, m, v, *, name):
    shape = w.shape
    as2d = (lambda t: t.reshape(1, -1)) if w.ndim == 1 else (lambda t: t)
    w2, m2, v2 = as2d(w), as2d(m), as2d(v)
    R, Cc = w2.shape
    from_slabs = g.ndim == 3
    br = _pick_rows(R, 128 if from_slabs else 256)
    c1 = 1.0 - ADAM_B1 ** ADAM_STEP
    c2 = 1.0 - ADAM_B2 ** ADAM_STEP

    def body(w_ref, g_ref, m_ref, v_ref, go_ref, d_ref, nm_ref, nv_ref):
        if from_slabs:
            gv = g_ref[0].astype(F32)
            for k in range(1, N_DEV):
                gv = gv + g_ref[k].astype(F32)
        else:
            gv = g_ref[...]
        go_ref[...] = gv
        nm = ADAM_B1 * m_ref[...] + (1.0 - ADAM_B1) * gv
        nv = ADAM_B2 * v_ref[...] + (1.0 - ADAM_B2) * (gv * gv)
        nm_ref[...] = nm
        nv_ref[...] = nv
        d_ref[...] = -ADAM_LR * ((nm / c1) / (jnp.sqrt(nv / c2) + ADAM_EPS) + ADAM_WD * w_ref[...])

    blk = pl.BlockSpec((br, Cc), lambda i: (i, 0))
    g_spec = pl.BlockSpec((N_DEV, br, Cc), lambda i: (0, i, 0)) if from_slabs else blk
    outs = pl.pallas_call(
        body, grid=(R // br,), in_specs=[blk, g_spec, blk, blk], out_specs=[blk] * 4,
        out_shape=[jax.ShapeDtypeStruct((R, Cc), F32)] * 4, name=name,
        compiler_params=_params(("parallel",)))(w2, g if from_slabs else as2d(g), m2, v2)
    return tuple(o.reshape(shape) for o in outs)


_BIG = ("w_in", "w_out", "wq_mem", "wk_mem", "wv_mem", "wo_mem", "w_up", "w_down")
_COL_SHARDED = ("w_in", "w_up")
_WEIGHTS = ("norm1_w", "w_in", "gdn_conv_w", "gdn_a_log", "gdn_dt_bias", "gdn_norm_w", "ssm_conv_w", "ssm_conv_b",
            "ssm_a_log", "ssm_dt_bias", "ssm_d", "ssm_norm_w", "w_out", "norm2_w", "mem_norm_w", "wq_mem", "wk_mem",
            "wv_mem", "wo_mem", "norm3_w", "w_up", "w_down", "final_norm_w")
_IN_PAD = 112


def _move_col_slabs(a, to_slabs, *, name):
    n, R, c = (N_DEV, a.shape[0], a.shape[1] // N_DEV) if to_slabs else a.shape
    slab = pl.BlockSpec((None, R, c), lambda j: (j, 0, 0))
    cols = pl.BlockSpec((R, c), lambda j: (0, j))

    def body(a_ref, o_ref):
        o_ref[...] = a_ref[...]

    return pl.pallas_call(
        body, grid=(n,), in_specs=[cols if to_slabs else slab], out_specs=slab if to_slabs else cols,
        out_shape=jax.ShapeDtypeStruct((n, R, c) if to_slabs else (R, n * c), a.dtype), name=name,
        compiler_params=_params(("parallel",)))(a)


def _full_from_slots(name, g):
    if name in _COL_SHARDED:
        if g.shape[2] % 128 == 0:
            return _move_col_slabs(g, False, name="cols_" + name)
        return jnp.transpose(g, (1, 0, 2)).reshape(g.shape[1], N_DEV * g.shape[2])
    return g.reshape(N_DEV * g.shape[1], g.shape[2])


def _slots_from_full(name, f):
    if name in _COL_SHARDED:
        if (f.shape[1] // N_DEV) % 128 == 0:
            return _move_col_slabs(f, True, name="slabs_" + name)
        return jnp.transpose(f.reshape(f.shape[0], N_DEV, f.shape[1] // N_DEV), (1, 0, 2))
    return f.reshape(N_DEV, f.shape[0] // N_DEV, f.shape[1])


def _pad_w_in(w):
    z = jnp.zeros((w.shape[0], _IN_PAD), w.dtype)
    return jnp.concatenate([w[:, :4096], w[:, 4112:6672], w[:, 4096:4112], z, w[:, 6672:6688], z], axis=1)


def _unpad_w_in(gp):
    return jnp.concatenate([gp[:, :4096], gp[:, C_GATE:C_GATE + 16], gp[:, 4096:C_GATE], gp[:, C_DT:C_DT + 16]],
                           axis=1)


def _pack_rows(vals):
    rows, offs, r = [], [], 0
    for vflat in vals:
        nrow = 8 * -(-vflat.shape[0] // 1024)
        rows.append(jnp.pad(vflat, (0, nrow * 128 - vflat.shape[0])).reshape(nrow, 128))
        offs.append((r, vflat.shape[0]))
        r += nrow
    return jnp.concatenate(rows, axis=0), offs


def _unpack_rows(packed, offs, shapes):
    out = []
    for (r, nel), shp in zip(offs, shapes):
        nrow = -(-nel // 128)
        out.append(packed[r:r + nrow].reshape(-1)[:nel].reshape(shp))
    return out


def kernel(x, mem, norm1_w, w_in, gdn_conv_w, gdn_a_log, gdn_dt_bias, gdn_norm_w, ssm_conv_w, ssm_conv_b, ssm_a_log, ssm_dt_bias, ssm_d, ssm_norm_w, w_out, norm2_w, mem_norm_w, wq_mem, wk_mem, wv_mem, wo_mem, norm3_w, w_up, w_down, final_norm_w, loss_target, m_norm1_w, m_w_in, m_gdn_conv_w, m_gdn_a_log, m_gdn_dt_bias, m_gdn_norm_w, m_ssm_conv_w, m_ssm_conv_b, m_ssm_a_log, m_ssm_dt_bias, m_ssm_d, m_ssm_norm_w, m_w_out, m_norm2_w, m_mem_norm_w, m_wq_mem, m_wk_mem, m_wv_mem, m_wo_mem, m_norm3_w, m_w_up, m_w_down, m_final_norm_w, v_norm1_w, v_w_in, v_gdn_conv_w, v_gdn_a_log, v_gdn_dt_bias, v_gdn_norm_w, v_ssm_conv_w, v_ssm_conv_b, v_ssm_a_log, v_ssm_dt_bias, v_ssm_d, v_ssm_norm_w, v_w_out, v_norm2_w, v_mem_norm_w, v_wq_mem, v_wk_mem, v_wv_mem, v_wo_mem, v_norm3_w, v_w_up, v_w_down, v_final_norm_w):
    args = dict(locals())
    w_loc = {n: args[n] for n in _WEIGHTS}
    me = 4 * lax.axis_index("x") + 2 * lax.axis_index("y") + lax.axis_index("c")

    w_in_full = _full_from_slots("w_in", _all_gather([w_in], BF16, name="gather_w_in")[0])
    later = _EARLY + _LATE
    shards = dict(zip(later, _cast_bf16([w_loc[n] for n in later], name="cast_shards")))
    conv_pack, conv_offs = _pack_rows([gdn_conv_w.reshape(-1), ssm_conv_w.reshape(-1)])
    conv_all = _all_gather([conv_pack], F32, name="gather_conv")[0]
    gdn_cw, ssm_cw = [], []
    for k in range(N_DEV):
        a, b = _unpack_rows(conv_all[k], conv_offs, [gdn_conv_w.shape, ssm_conv_w.shape])
        gdn_cw.append(a)
        ssm_cw.append(b)
    W = {
        "w_in_pad": _pad_w_in(w_in_full),
        "norm1_w": norm1_w, "norm2_w": norm2_w, "norm3_w": norm3_w, "mem_norm_w": mem_norm_w,
        "final_norm_w": final_norm_w, "ssm_norm_w": ssm_norm_w, "ssm_conv_b": ssm_conv_b,
        "gdn_conv_w": jnp.concatenate(gdn_cw, axis=1), "ssm_conv_w": jnp.concatenate(ssm_cw, axis=1),
        "gdn_alog_row": jnp.pad(gdn_a_log, (GDN_H, 128 - 2 * GDN_H)).reshape(1, 128),
        "gdn_dtb_row": jnp.pad(gdn_dt_bias, (GDN_H, 128 - 2 * GDN_H)).reshape(1, 128),
        "gdn_norm_x": jnp.tile(gdn_norm_w, GDN_H).reshape(1, D),
        "ssm_dtb_row": jnp.pad(ssm_dt_bias, (0, 128 - SSM_H)).reshape(1, 128),
        "ssm_alog_row": jnp.pad(ssm_a_log, (0, 128 - SSM_H)).reshape(1, 128),
        "ssm_d_x": jnp.repeat(ssm_d, SSM_P).reshape(1, D),
    }

    loss_part, grad_x, G = _local_step(x[0], mem[0], loss_target[0], W, shards)

    grads = {n: G[n] for n in _BIG}

    small = {
        "norm1_w": G["norm1_w"], "gdn_conv_w": G["gdn_conv_w"], "gdn_a_log": G["gdn_alog_row"][0, GDN_H:2 * GDN_H],
        "gdn_dt_bias": G["gdn_dtb_row"][0, GDN_H:2 * GDN_H], "gdn_norm_w": G["gdn_norm_x"].reshape(GDN_H, 128).sum(0),
        "ssm_conv_w": G["ssm_conv_w"], "ssm_conv_b": G["ssm_conv_b"],
        "ssm_a_log": G["ssm_alog_row"][0, :SSM_H], "ssm_dt_bias": G["ssm_dtb_row"][0, :SSM_H],
        "ssm_d": G["ssm_d_x"].reshape(SSM_H, SSM_P).sum(1), "ssm_norm_w": G["ssm_norm_w"].reshape(D),
        "norm2_w": G["norm2_w"], "mem_norm_w": G["mem_norm_w"], "norm3_w": G["norm3_w"],
        "final_norm_w": G["final_norm_w"], "loss": loss_part[0, :1],
    }
    names = list(small)
    pack, offs = _pack_rows([small[n].reshape(-1) for n in names])
    tot = _sum8(_all_gather([pack], F32, name="gather_small")[0], name="sum_small")
    summed = dict(zip(names, _unpack_rows(tot, offs, [small[n].shape for n in names])))
    loss = summed.pop("loss")[0]
    for n in ("gdn_conv_w", "ssm_conv_w"):
        width = w_loc[n].shape[1]
        summed[n] = lax.dynamic_slice_in_dim(summed[n], me * width, width, axis=1)
    grads.update(summed)

    upd = {n: _adamw(w_loc[n], grads[n], args["m_" + n], args["v_" + n], name="adamw_" + n) for n in _WEIGHTS}
    return (loss, grad_x[None], *[upd[n][0] for n in _WEIGHTS], *[upd[n][1] for n in _WEIGHTS],
            *[upd[n][2] for n in _WEIGHTS], *[upd[n][3] for n in _WEIGHTS])
```

```python
import jax
import jax.numpy as jnp
from jax import lax
from jax.experimental import pallas as pl
from jax.experimental.pallas import tpu as pltpu

F32 = jnp.float32
BF16 = jnp.bfloat16
_MXU = BF16

D = 1024
EPS = 1e-6
CONV_K = 4
GDN_H, GDN_DK, GDN_C = 8, 128, 64
GDN_SCAN_CHUNKS = 8
GDN_LOCAL_CHUNKS = 4
GDN_REST_CHUNKS = 4
SSM_H, SSM_P, SSM_L, SSM_N = 16, 64, 128, 128
SSM_SCAN_CHUNKS = 4
MEM_H, MEM_HD = 4, 256
D_FF = 4096
N_DEV = 8

C_QKV, C_ZG, C_ZS, C_XBC, C_GATE, C_DT, C_TOT = 0, 3072, 4096, 5120, 6656, 6784, 6912
P_HALO = 16

ADAM_LR, ADAM_B1, ADAM_B2, ADAM_EPS, ADAM_WD, ADAM_STEP = 0.001, 0.9, 0.999, 1e-08, 0.01, 10

VMEM_LIMIT = 56 * 1024 * 1024

_NN = (((1,), (0,)), ((), ()))
_NT = (((1,), (1,)), ((), ()))
_TN = (((0,), (0,)), ((), ()))


def _dot(a, b, dims=_NN):
    return lax.dot_general(a.astype(_MXU), b.astype(_MXU), dims, preferred_element_type=F32)


def _split3(a):
    a1 = a.astype(BF16)
    r1 = a - a1.astype(F32)
    a2 = r1.astype(BF16)
    return a1, a2, (r1 - a2.astype(F32)).astype(BF16)


def _dot_sel(a, e):
    eb = e.astype(BF16)
    return sum(lax.dot_general(p, eb, _NN, preferred_element_type=F32) for p in _split3(a))


def _sel_dot(e, a):
    eb = e.astype(BF16)
    return sum(lax.dot_general(eb, p, _NN, preferred_element_type=F32) for p in _split3(a))


def _chunk_cumsum(a, tri, chunk):
    return jnp.concatenate([_sel_dot(tri, a[r:r + chunk]) for r in range(0, a.shape[0], chunk)], axis=0)


def _params(sem):
    return pltpu.CompilerParams(dimension_semantics=sem, vmem_limit_bytes=VMEM_LIMIT)


def _pick(n, cap):
    for d in range(min(cap, n), 0, -128):
        if n % d == 0 and d % 128 == 0:
            return d
    return n


def _sigmoid(x):
    return 0.5 * jnp.tanh(0.5 * x) + 0.5


def _silu(x):
    return x * _sigmoid(x)


def _dsilu(x):
    s = _sigmoid(x)
    return s * (1.0 + x * (1.0 - s))


def _softplus(x):
    return jnp.maximum(x, 0.0) + jnp.log(1.0 + jnp.exp(-jnp.abs(x)))


def _iota2(shape, axis):
    return lax.broadcasted_iota(jnp.int32, shape, axis)


def _sum_all(x):
    return jnp.sum(jnp.sum(x, axis=1, keepdims=True), axis=0, keepdims=True)


_MESH = pl.DeviceIdType.MESH
_ANY = pl.BlockSpec(memory_space=pl.ANY)
_VM = pl.BlockSpec(memory_space=pltpu.VMEM)
_REL = [(r >> 2 & 1, r >> 1 & 1, r & 1) for r in range(1, N_DEV)]


def _place():
    return lax.axis_index("x"), lax.axis_index("y"), lax.axis_index("c")


class _Ride:
    def __init__(self, srcs, shard):
        self.srcs, self.shard, self.n = list(srcs), shard, len(srcs)
        self.out_shape = [jax.ShapeDtypeStruct(((N_DEV,) + s.shape) if shard else s.shape, s.dtype)
                          for s in self.srcs]
        self.specs = [_ANY] * self.n
        self.scratch = [pltpu.SemaphoreType.DMA((self.n, N_DEV - 1)), pltpu.SemaphoreType.DMA((self.n, N_DEV - 1)),
                        pltpu.SemaphoreType.DMA((self.n,))]

    def _copies(self, in_refs, out_refs, sems):
        send, recv, loc = sems
        x, y, c = _place()
        me = 4 * x + 2 * y + c
        local, remote, arrive = [], [], []
        for a in range(self.n):
            src = in_refs[a] if self.shard else in_refs[a].at[me]
            local.append(pltpu.make_async_copy(src, out_refs[a].at[me], loc.at[a]))
        for k, (rx, ry, rc) in enumerate(_REL):
            peer = (lax.rem(x + rx, 2), lax.rem(y + ry, 2), lax.rem(c + rc, 2))
            ps = 4 * peer[0] + 2 * peer[1] + peer[2]
            for a in range(self.n):
                src = in_refs[a] if self.shard else in_refs[a].at[ps]
                remote.append(pltpu.make_async_remote_copy(
                    src_ref=src, dst_ref=out_refs[a].at[me], send_sem=send.at[a, k], recv_sem=recv.at[a, k],
                    device_id=peer, device_id_type=_MESH))
                slot = out_refs[a].at[ps]
                arrive.append(pltpu.make_async_remote_copy(
                    src_ref=slot, dst_ref=slot, send_sem=send.at[a, k], recv_sem=recv.at[a, k],
                    device_id=peer, device_id_type=_MESH))
        return local, remote, arrive

    def start(self, in_refs, out_refs, sems):
        local, remote, _ = self._copies(in_refs, out_refs, sems)
        for cp in local + remote:
            cp.start()

    def wait(self, in_refs, out_refs, sems):
        local, remote, arrive = self._copies(in_refs, out_refs, sems)
        for cp in arrive:
            cp.wait_recv()
        for cp in remote:
            cp.wait_send()
        for cp in local:
            cp.wait()


_EPI = {
    "none": ((), ("tile",)),
    "res": (("tile",), ("tile",)),
    "mul2": (("tile",), ("tile",)),
    "relu2": ((), ("tile", "tile")),
    "res_norm": (("tile", "row"), ("tile", "tile")),
    "norm_bwd": (("tile", "tile", "row"), ("tile", "row")),
    "res_loss": (("tile", "tile", "row"), ("tile", "row", "row")),
}


def _mm(a, b, *, dims="nn", epi="none", extra=(), out_dtype=F32, name, bm=1024, bn_cap=1024, bk_cap=2048,
        ride=None, b_cols=None, b_resident=False):
    if dims == "nn":
        (M, K), (K2, N) = a.shape, b.shape
    elif dims == "nt":
        (M, K), (N, K2) = a.shape, b.shape
    else:
        (K, M), (K2, N) = a.shape, b.shape
    jb0 = 0
    if b_cols is not None:
        N = b_cols[1]
    assert K == K2, (a.shape, b.shape, dims)
    bm = _pick(M, bm)
    bn = _pick(N, bn_cap)
    bk = _pick(K, bk_cap)
    nk = K // bk
    if b_cols is not None:
        assert dims == "nn" and b_cols[0] % bn == 0
        jb0 = b_cols[0] // bn
    dn = {"nn": _NN, "nt": _NT, "tn": _TN}[dims]
    a_spec = (pl.BlockSpec((bk, bm), lambda i, j, k: (k, i)) if dims == "tn"
              else pl.BlockSpec((bm, bk), lambda i, j, k: (i, k)))
    if b_resident:
        b_spec = pl.BlockSpec(b.shape, lambda i, j, k: (0, 0), pipeline_mode=pl.Buffered(1))
    else:
        b_spec = (pl.BlockSpec((bn, bk), lambda i, j, k: (j, k)) if dims == "nt"
                  else pl.BlockSpec((bk, bn), lambda i, j, k: (k, j + jb0)))
    o_spec = pl.BlockSpec((bm, bn), lambda i, j, k: (i, j))
    r_spec = pl.BlockSpec((1, bn), lambda i, j, k: (0, j))
    extra = list(extra) if isinstance(extra, (tuple, list)) else [extra]
    ekinds, okinds = _EPI[epi]
    assert len(extra) == len(ekinds) and (epi not in ("res_norm", "norm_bwd", "res_loss") or bn == N)
    n_extra, n_out = len(ekinds), len(okinds)
    n_ride = ride.n if ride else 0
    gi, gj = M // bm, N // bn

    def body(a_ref, b_ref, *rest):
        ex = rest[:n_extra]
        first = pl.program_id(0) == 0
        ride_in = rest[n_extra:n_extra + n_ride]
        outs = rest[n_extra + n_ride:n_extra + n_ride + n_out]
        ride_out = rest[n_extra + n_ride + n_out:n_extra + 2 * n_ride + n_out]
        if ride:
            at = lambda i, j, k: ((pl.program_id(0) == i) & (pl.program_id(1) == j) & (pl.program_id(2) == k))

            @pl.when(at(0, 0, 0))
            def _():
                ride.start(ride_in, ride_out, rest[-3:])

        def finish(r):
            if epi == "res":
                outs[0][...] = (r + ex[0][...].astype(F32)).astype(outs[0].dtype)
            elif epi == "mul2":
                outs[0][...] = (2.0 * r * ex[0][...].astype(F32)).astype(outs[0].dtype)
            elif epi == "relu2":
                u = jnp.maximum(r, 0.0)
                outs[0][...] = u.astype(outs[0].dtype)
                outs[1][...] = (u * u).astype(outs[1].dtype)
            elif epi == "res_norm":
                y = r + ex[0][...]
                outs[0][...] = y
                rstd = lax.rsqrt(jnp.mean(y * y, axis=1, keepdims=True) + EPS)
                outs[1][...] = (y * rstd * ex[1][...]).astype(outs[1].dtype)
            elif epi == "norm_bwd":
                xv = ex[0][...]
                rstd = lax.rsqrt(jnp.mean(xv * xv, axis=1, keepdims=True) + EPS)
                xh = xv * rstd
                dxh = r * ex[2][...]
                outs[0][...] = ex[1][...] + rstd * (dxh - xh * jnp.mean(dxh * xh, axis=1, keepdims=True))
                dw = jnp.sum(r * xh, axis=0, keepdims=True)

                @pl.when(first)
                def _():
                    outs[1][...] = dw

                @pl.when(jnp.logical_not(first))
                def _():
                    outs[1][...] += dw
            elif epi == "res_loss":
                y = r + ex[0][...]
                wv = ex[2][...]
                rstd = lax.rsqrt(jnp.mean(y * y, axis=1, keepdims=True) + EPS)
                yh = y * rstd
                err = yh * wv - ex[1][...]
                part_loss = 0.5 * jnp.sum(jnp.mean(err * err, axis=1, keepdims=True), axis=0, keepdims=True)
                dyn = err * (1.0 / N)
                dyh = dyn * wv
                outs[0][...] = rstd * (dyh - yh * jnp.mean(dyh * yh, axis=1, keepdims=True))
                dw = jnp.sum(dyn * yh, axis=0, keepdims=True)
                lrow = jnp.broadcast_to(part_loss, (1, N))

                @pl.when(first)
                def _():
                    outs[1][...] = dw
                    outs[2][...] = lrow

                @pl.when(jnp.logical_not(first))
                def _():
                    outs[1][...] += dw
                    outs[2][...] += lrow
            else:
                outs[0][...] = r.astype(outs[0].dtype)

        if b_resident:
            jo = pl.multiple_of((pl.program_id(1) + jb0) * bn, bn)
            ko = pl.multiple_of(pl.program_id(2) * bk, bk)
            b_blk = b_ref[pl.ds(jo, bn), pl.ds(ko, bk)] if dims == "nt" else b_ref[pl.ds(ko, bk), pl.ds(jo, bn)]
        else:
            b_blk = b_ref[...]
        part = _dot(a_ref[...], b_blk, dn)
        if nk == 1:
            finish(part)
        else:
            acc = rest[n_extra + 2 * n_ride + n_out]
            k = pl.program_id(2)

            @pl.when(k == 0)
            def _():
                acc[...] = part

            @pl.when((k > 0) & (k < nk - 1))
            def _():
                acc[...] += part

            @pl.when(k == nk - 1)
            def _():
                finish(acc[...] + part)

        if ride:
            @pl.when(at(gi - 1, gj - 1, nk - 1))
            def _():
                ride.wait(ride_in, ride_out, rest[-3:])

    kind_spec = {"tile": o_spec, "row": r_spec}
    ins = [a, b] + [e.reshape(1, N) if k == "row" else e for e, k in zip(extra, ekinds)]
    in_specs = [a_spec, b_spec] + [kind_spec[k] for k in ekinds]
    out_dtypes = {"res_norm": (F32, BF16), "norm_bwd": (F32, F32), "res_loss": (F32, F32, F32)}.get(
        epi, (out_dtype,) * n_out)
    out_shape = [jax.ShapeDtypeStruct((M, N) if k == "tile" else (1, N), dt) for k, dt in zip(okinds, out_dtypes)]
    out_specs = [kind_spec[k] for k in okinds]
    scratch = [pltpu.VMEM((bm, bn), F32)] if nk > 1 else []
    sem = ("arbitrary" if epi in ("norm_bwd", "res_loss") else "parallel", "parallel", "arbitrary")
    if ride:
        ins, in_specs = ins + ride.srcs, in_specs + ride.specs
        out_shape, out_specs = out_shape + ride.out_shape, out_specs + ride.specs
        scratch, sem = scratch + ride.scratch, ("arbitrary",) * 3
    res = pl.pallas_call(
        body, grid=(gi, gj, nk), in_specs=in_specs, out_specs=out_specs, out_shape=out_shape,
        scratch_shapes=scratch, name=name, compiler_params=_params(sem))(*ins)
    main = res[:n_out] if n_out > 1 else res[0]
    return (main, list(res[n_out:])) if ride else main


def _rmsnorm_fwd(x, w, *, name, bt=1024):
    T, Dm = x.shape
    bt = min(bt, T)

    def body(x_ref, w_ref, h_ref):
        xv = x_ref[...]
        r = lax.rsqrt(jnp.mean(xv * xv, axis=1, keepdims=True) + EPS)
        h_ref[...] = (xv * r * w_ref[...]).astype(h_ref.dtype)

    return pl.pallas_call(
        body, grid=(T // bt,),
        in_specs=[pl.BlockSpec((bt, Dm), lambda i: (i, 0)), pl.BlockSpec((1, Dm), lambda i: (0, 0))],
        out_specs=pl.BlockSpec((bt, Dm), lambda i: (i, 0)),
        out_shape=jax.ShapeDtypeStruct((T, Dm), BF16), name=name,
        compiler_params=_params(("parallel",)))(x, w.reshape(1, Dm))


def _rmsnorm_bwd(x, w, dh, dres, *, name, bt=256):
    T, Dm = x.shape
    bt = min(bt, T)
    has_res = dres is not None

    def body(x_ref, w_ref, dh_ref, *rest):
        dres_ref = rest[0] if has_res else None
        dx_ref, dw_ref = rest[-2], rest[-1]
        i = pl.program_id(0)
        xv = x_ref[...]
        r = lax.rsqrt(jnp.mean(xv * xv, axis=1, keepdims=True) + EPS)
        xh = xv * r
        dhv = dh_ref[...].astype(F32)
        dxh = dhv * w_ref[...]
        dx = r * (dxh - xh * jnp.mean(dxh * xh, axis=1, keepdims=True))
        if has_res:
            dx = dx + dres_ref[...]
        dx_ref[...] = dx

        @pl.when(i == 0)
        def _():
            dw_ref[...] = jnp.zeros_like(dw_ref)

        dw_ref[...] += jnp.sum(dhv * xh, axis=0, keepdims=True)

    row = pl.BlockSpec((bt, Dm), lambda i: (i, 0))
    vec = pl.BlockSpec((1, Dm), lambda i: (0, 0))
    ins = [x, w.reshape(1, Dm), dh] + ([dres] if has_res else [])
    dx, dw = pl.pallas_call(
        body, grid=(T // bt,), in_specs=[row, vec, row] + ([row] if has_res else []),
        out_specs=[row, vec],
        out_shape=[jax.ShapeDtypeStruct((T, Dm), F32), jax.ShapeDtypeStruct((1, Dm), F32)],
        name=name, compiler_params=_params(("arbitrary",)))(*ins)
    return dx, dw.reshape(Dm)


def _attn_fwd(q, km, vm, *, bt=1024):
    T = q.shape[0]
    M = km.shape[0]
    bt = min(bt, T)
    scale = MEM_HD ** -0.5

    def body(q_ref, k_ref, v_ref, o_ref):
        sls = [slice(h * MEM_HD, (h + 1) * MEM_HD) for h in range(MEM_H)]
        ss = [_dot(q_ref[:, sl], k_ref[:, sl], _NT) * scale for sl in sls]
        es = [jnp.exp(s - jnp.max(s, axis=1, keepdims=True)) for s in ss]
        ps = [e / jnp.sum(e, axis=1, keepdims=True) for e in es]
        for sl, p in zip(sls, ps):
            o_ref[:, sl] = _dot(p, v_ref[:, sl]).astype(o_ref.dtype)

    row = pl.BlockSpec((bt, D), lambda i: (i, 0))
    mem = pl.BlockSpec((M, D), lambda i: (0, 0))
    return pl.pallas_call(
        body, grid=(T // bt,), in_specs=[row, mem, mem], out_specs=row,
        out_shape=jax.ShapeDtypeStruct((T, D), BF16), name="attn_fwd",
        compiler_params=_params(("parallel",)))(q, km, vm)


def _attn_bwd(q, km, vm, do, *, bt=512):
    T = q.shape[0]
    M = km.shape[0]
    bt = min(bt, T)
    scale = MEM_HD ** -0.5

    def body(q_ref, k_ref, v_ref, do_ref, dq_ref, dk_ref, dv_ref):
        i = pl.program_id(0)

        @pl.when(i == 0)
        def _():
            dk_ref[...] = jnp.zeros_like(dk_ref)
            dv_ref[...] = jnp.zeros_like(dv_ref)

        sls = [slice(h * MEM_HD, (h + 1) * MEM_HD) for h in range(MEM_H)]
        ss = [_dot(q_ref[:, sl], k_ref[:, sl], _NT) * scale for sl in sls]
        dps = [_dot(do_ref[:, sl], v_ref[:, sl], _NT) for sl in sls]
        es = [jnp.exp(s - jnp.max(s, axis=1, keepdims=True)) for s in ss]
        ps = [e / jnp.sum(e, axis=1, keepdims=True) for e in es]
        dss = [p * (dp - jnp.sum(dp * p, axis=1, keepdims=True)) * scale for p, dp in zip(ps, dps)]
        for sl, p, ds in zip(sls, ps, dss):
            dq_ref[:, sl] = _dot(ds, k_ref[:, sl]).astype(dq_ref.dtype)
            dk_ref[:, sl] += _dot(ds, q_ref[:, sl], _TN)
            dv_ref[:, sl] += _dot(p, do_ref[:, sl], _TN)

    row = pl.BlockSpec((bt, D), lambda i: (i, 0))
    mem = pl.BlockSpec((M, D), lambda i: (0, 0))
    return pl.pallas_call(
        body, grid=(T // bt,), in_specs=[row, mem, mem, row], out_specs=[row, mem, mem],
        out_shape=[jax.ShapeDtypeStruct((T, D), BF16), jax.ShapeDtypeStruct((M, D), F32),
                   jax.ShapeDtypeStruct((M, D), F32)],
        name="attn_bwd", compiler_params=_params(("arbitrary",)))(q, km, vm, do)


def _conv_apply(halo, x, w_ref, b_ref):
    bt, hr = x.shape[0], halo.shape[0]
    cat = jnp.concatenate([halo, x], axis=0)
    y = x * w_ref[3:4, :]
    for k in range(CONV_K - 1):
        y = y + pltpu.roll(cat, CONV_K - 1 - k, 0)[hr:hr + bt] * w_ref[k:k + 1, :]
    if b_ref is not None:
        y = y + b_ref[...]
    return y


def _l2_parts(act, bc):
    out = []
    for s in range(bc // 128):
        a = act[:, s * 128:(s + 1) * 128]
        r = lax.rsqrt(jnp.sum(a * a, axis=1, keepdims=True) + EPS)
        out.append((a, r))
    return out


def _conv_fwd(p, col0, C, w, b, *, l2, name, bt=512, bc=1024):
    T = p.shape[0]
    bt = min(bt, T)
    c0, hb = col0 // bc, bt // P_HALO
    has_b = b is not None
    assert not l2 or (bc == D and C == 2 * D)

    def body(x_ref, halo_ref, w_ref, *rest):
        b_ref = rest[0] if has_b else None
        o_ref = rest[-1]
        i, j = pl.program_id(0), pl.program_id(1)
        x = x_ref[...].astype(F32)
        halo = jnp.where(i > 0, halo_ref[...].astype(F32), 0.0)
        act = _silu(_conv_apply(halo, x, w_ref, b_ref))
        if l2:
            sc = jnp.where(j == 0, GDN_DK ** -0.5, 1.0)
            o_ref[...] = jnp.concatenate([a * (r * sc) for a, r in _l2_parts(act, bc)], axis=1)
        else:
            o_ref[...] = act

    in_specs = [pl.BlockSpec((bt, bc), lambda i, j: (i, c0 + j)),
                pl.BlockSpec((P_HALO, bc), lambda i, j: (jnp.maximum(i * hb - 1, 0), c0 + j)),
                pl.BlockSpec((CONV_K, bc), lambda i, j: (0, j))]
    ins = [p, p, w]
    if has_b:
        in_specs.append(pl.BlockSpec((1, bc), lambda i, j: (0, j)))
        ins.append(b.reshape(1, C))
    return pl.pallas_call(
        body, grid=(T // bt, C // bc), in_specs=in_specs,
        out_specs=pl.BlockSpec((bt, bc), lambda i, j: (i, j)),
        out_shape=jax.ShapeDtypeStruct((T, C), F32), name=name,
        compiler_params=_params(("parallel", "parallel")))(*ins)


def _conv_bwd_act(p, col0, C, w, b, dact, dcol0, *, l2, name, bt=512, bc=1024):
    T = p.shape[0]
    bt = min(bt, T)
    c0, d0, hb = col0 // bc, dcol0 // bc, bt // P_HALO
    has_b = b is not None
    assert not l2 or (bc == D and C == 2 * D)

    def body(x_ref, halo_ref, w_ref, *rest):
        b_ref = rest[0] if has_b else None
        dact_ref, dy_ref, dw_ref, db_ref = rest[-4:]
        j, i = pl.program_id(0), pl.program_id(1)
        x = x_ref[...].astype(F32)
        halo = jnp.where(i > 0, halo_ref[...].astype(F32), 0.0)
        y = _conv_apply(halo, x, w_ref, b_ref)
        dact = dact_ref[...]
        sg = _sigmoid(y)
        if l2:
            sc = jnp.where(j == 0, GDN_DK ** -0.5, 1.0)
            parts = []
            for s, (a, r) in enumerate(_l2_parts(y * sg, bc)):
                n = a * r
                dn = dact[:, s * 128:(s + 1) * 128]
                parts.append((r * sc) * (dn - n * jnp.sum(dn * n, axis=1, keepdims=True)))
            dact = jnp.concatenate(parts, axis=1)
        dy = dact * (sg * (1.0 + y * (1.0 - sg)))
        dy_ref[...] = dy

        @pl.when(i == 0)
        def _():
            dw_ref[...] = jnp.zeros_like(dw_ref)
            db_ref[...] = jnp.zeros_like(db_ref)

        db_ref[...] += jnp.sum(dy, axis=0, keepdims=True)
        cat = jnp.concatenate([halo, x], axis=0)
        dw_ref[3:4, :] += jnp.sum(dy * x, axis=0, keepdims=True)
        for k in range(CONV_K - 1):
            xs = pltpu.roll(cat, CONV_K - 1 - k, 0)[P_HALO:P_HALO + bt]
            dw_ref[k:k + 1, :] += jnp.sum(dy * xs, axis=0, keepdims=True)

    in_specs = [pl.BlockSpec((bt, bc), lambda j, i: (i, c0 + j)),
                pl.BlockSpec((P_HALO, bc), lambda j, i: (jnp.maximum(i * hb - 1, 0), c0 + j)),
                pl.BlockSpec((CONV_K, bc), lambda j, i: (0, j))]
    ins = [p, p, w]
    if has_b:
        in_specs.append(pl.BlockSpec((1, bc), lambda j, i: (0, j)))
        ins.append(b.reshape(1, C))
    in_specs.append(pl.BlockSpec((bt, bc), lambda j, i: (i, d0 + j)))
    ins.append(dact)
    dy, dw, db = pl.pallas_call(
        body, grid=(C // bc, T // bt), in_specs=in_specs,
        out_specs=[pl.BlockSpec((bt, bc), lambda j, i: (i, j)),
                   pl.BlockSpec((CONV_K, bc), lambda j, i: (0, j)),
                   pl.BlockSpec((1, bc), lambda j, i: (0, j))],
        out_shape=[jax.ShapeDtypeStruct((T, C), F32), jax.ShapeDtypeStruct((CONV_K, C), F32),
                   jax.ShapeDtypeStruct((1, C), F32)],
        name=name, compiler_params=_params(("parallel", "arbitrary")))(*ins)
    return dy, dw, db.reshape(C)


def _conv_bwd_in(dy, w, dp_in, col0, T, *, name, bt=512, bc=1024):
    C = dy.shape[1]
    bt = min(bt, T)
    c0, hb, nb = col0 // bc, bt // 8, T // bt

    def body(dy_ref, nxt_ref, w_ref, *rest):
        o_ref = rest[-1]
        i = pl.program_id(0)
        dy_v = dy_ref[...]
        nxt = jnp.where(i < nb - 1, nxt_ref[...], 0.0)
        cat = jnp.concatenate([dy_v, nxt], axis=0)
        dx = dy_v * w_ref[3:4, :]
        for k in range(CONV_K - 1):
            s = CONV_K - 1 - k
            dx = dx + pltpu.roll(cat, bt + 8 - s, 0)[0:bt] * w_ref[k:k + 1, :]
        o_ref[...] = dx.astype(o_ref.dtype)

    in_specs = [pl.BlockSpec((bt, bc), lambda i, j: (i, j)),
                pl.BlockSpec((8, bc), lambda i, j: (jnp.minimum((i + 1) * hb, T // 8 - 1), j)),
                pl.BlockSpec((CONV_K, bc), lambda i, j: (0, j))]
    ins = [dy, dy, w]
    alias = {}
    if dp_in is not None:
        in_specs.append(pl.BlockSpec(memory_space=pl.ANY))
        ins.append(dp_in)
        alias = {3: 0}
    return pl.pallas_call(
        body, grid=(nb, C // bc), in_specs=in_specs,
        out_specs=pl.BlockSpec((bt, bc), lambda i, j: (i, c0 + j)),
        out_shape=jax.ShapeDtypeStruct((T, C_TOT), BF16), input_output_aliases=alias, name=name,
        compiler_params=_params(("parallel", "parallel")))(*ins)


def _expand_mats(shift, row0):
    e = (_iota2((128, D), 0) - row0 == (_iota2((128, D), 1) >> shift)).astype(F32)
    et = ((_iota2((D, 128), 0) >> shift) == _iota2((D, 128), 1) - row0).astype(F32)
    return e, et


def _cum_mats(chunk):
    ri, ci = _iota2((chunk, chunk), 0), _iota2((chunk, chunk), 1)
    return (ri >= ci).astype(F32), (ri <= ci).astype(F32)


def _gdn_gates_fwd(p, alog_row, dtb_row, *, bt=512):
    T = p.shape[0]
    bt = min(bt, T)

    def body(g_ref, al_ref, db_ref, bg_ref):
        gt = g_ref[...]
        lc, _ = _cum_mats(GDN_C)
        g_l = -jnp.exp(al_ref[...]) * _softplus(gt + db_ref[...])
        bg_ref[...] = jnp.where(_iota2((bt, 128), 1) < GDN_H, _sigmoid(gt), _chunk_cumsum(g_l, lc, GDN_C))

    vec = pl.BlockSpec((1, 128), lambda i: (0, 0))
    seg = pl.BlockSpec((bt, 128), lambda i: (i, 0))
    return pl.pallas_call(
        body, grid=(T // bt,), in_specs=[seg, vec, vec], out_specs=seg,
        out_shape=jax.ShapeDtypeStruct((T, 128), F32), name="gdn_gates_fwd",
        compiler_params=_params(("parallel",)))(p, alog_row, dtb_row)


def _gdn_gates_bwd(p, alog_row, dtb_row, dbg, dp_in, *, bt=512):
    T = p.shape[0]
    bt = min(bt, T)

    def body(g_ref, al_ref, db_ref, dbg_ref, dpin_ref, dg_out, dal_ref, ddb_ref):
        i = pl.program_id(0)
        gt = g_ref[...]
        lane = _iota2((bt, 128), 1)
        _, uc = _cum_mats(GDN_C)
        ea = jnp.exp(al_ref[...])
        zz = gt + db_ref[...]
        g_l = -ea * _softplus(zz)
        beta_l = _sigmoid(gt)
        dbg_v = dbg_ref[...]
        dg_l = jnp.where((lane >= GDN_H) & (lane < 2 * GDN_H), _chunk_cumsum(dbg_v, uc, GDN_C), 0.0)
        dbeta_l = jnp.where(lane < GDN_H, dbg_v, 0.0)
        da = dg_l * (-ea) * _sigmoid(zz)
        dg_out[...] = (da + dbeta_l * beta_l * (1.0 - beta_l)).astype(dg_out.dtype)

        @pl.when(i == 0)
        def _():
            dal_ref[...] = jnp.zeros_like(dal_ref)
            ddb_ref[...] = jnp.zeros_like(ddb_ref)

        dal_ref[...] += jnp.sum(dg_l * g_l, axis=0, keepdims=True)
        ddb_ref[...] += jnp.sum(da, axis=0, keepdims=True)

    vec = pl.BlockSpec((1, 128), lambda i: (0, 0))
    seg = pl.BlockSpec((bt, 128), lambda i: (i, 0))
    gate = pl.BlockSpec((bt, 128), lambda i: (i, C_GATE // 128))
    return pl.pallas_call(
        body, grid=(T // bt,), in_specs=[seg, vec, vec, seg, _ANY], out_specs=[gate, vec, vec],
        out_shape=[jax.ShapeDtypeStruct((T, C_TOT), BF16), jax.ShapeDtypeStruct((1, 128), F32),
                   jax.ShapeDtypeStruct((1, 128), F32)],
        input_output_aliases={4: 0}, name="gdn_gates_bwd",
        compiler_params=_params(("arbitrary",)))(p, alog_row, dtb_row, dbg, dp_in)


def _ssd_dt_fwd(p, dtb_row, alog_row, *, bt=512):
    T = p.shape[0]
    bt = min(bt, T)

    def body(d_ref, db_ref, al_ref, da_ref):
        lc, _ = _cum_mats(SSM_L)
        dt_l = _softplus(d_ref[...] + db_ref[...])
        alpha_l = _chunk_cumsum(dt_l * (-jnp.exp(al_ref[...])), lc, SSM_L)
        da_ref[...] = jnp.where(_iota2((bt, 128), 1) < SSM_H, dt_l, pltpu.roll(alpha_l, SSM_H, 1))

    v128 = pl.BlockSpec((1, 128), lambda i: (0, 0))
    return pl.pallas_call(
        body, grid=(T // bt,), in_specs=[pl.BlockSpec((bt, 128), lambda i: (i, 1)), v128, v128],
        out_specs=pl.BlockSpec((bt, 128), lambda i: (i, 0)), out_shape=jax.ShapeDtypeStruct((T, 128), F32),
        name="ssd_dt_fwd", compiler_params=_params(("parallel",)))(p, dtb_row, alog_row)


def _ssd_dt_bwd(p, dtb_row, alog_row, dda, dp_in, *, bt=512):
    T = p.shape[0]
    bt = min(bt, T)

    def body(d_ref, db_ref, al_ref, dda_ref, dpin_ref, dd_out, ddb_ref, dalog_ref):
        i = pl.program_id(0)
        heads = _iota2((bt, 128), 1) < SSM_H
        _, uc = _cum_mats(SSM_L)
        zz = d_ref[...] + db_ref[...]
        dt_l = _softplus(zz)
        a_row = -jnp.exp(al_ref[...])
        dda_v = dda_ref[...]
        da_l = _chunk_cumsum(jnp.where(heads, pltpu.roll(dda_v, 128 - SSM_H, 1), 0.0), uc, SSM_L)
        draw = jnp.where(heads, (dda_v + da_l * a_row) * _sigmoid(zz), 0.0)
        dd_out[...] = draw.astype(dd_out.dtype)

        @pl.when(i == 0)
        def _():
            ddb_ref[...] = jnp.zeros_like(ddb_ref)
            dalog_ref[...] = jnp.zeros_like(dalog_ref)

        ddb_ref[...] += jnp.sum(draw, axis=0, keepdims=True)
        dalog_ref[...] += jnp.sum(da_l * dt_l, axis=0, keepdims=True) * a_row

    seg = pl.BlockSpec((bt, 128), lambda i: (i, C_DT // 128))
    v128 = pl.BlockSpec((1, 128), lambda i: (0, 0))
    return pl.pallas_call(
        body, grid=(T // bt,),
        in_specs=[pl.BlockSpec((bt, 128), lambda i: (i, 1)), v128, v128, pl.BlockSpec((bt, 128), lambda i: (i, 0)), _ANY],
        out_specs=[seg, v128, v128],
        out_shape=[jax.ShapeDtypeStruct((T, C_TOT), BF16), jax.ShapeDtypeStruct((1, 128), F32),
                   jax.ShapeDtypeStruct((1, 128), F32)],
        input_output_aliases={4: 0}, name="ssd_dt_bwd",
        compiler_params=_params(("arbitrary",)))(p, dtb_row, alog_row, dda, dp_in)


def _gdn_post_fwd(o, p, w_x, *, bt=1024):
    T = o.shape[0]
    bt = min(bt, T)

    def body(o_ref, z_ref, w_ref, out_ref):
        for h in range(GDN_H):
            sl = slice(h * 128, (h + 1) * 128)
            oh = o_ref[:, sl].astype(F32)
            r = lax.rsqrt(jnp.mean(oh * oh, axis=1, keepdims=True) + EPS)
            out_ref[:, sl] = (oh * r * w_ref[:, sl] * _silu(z_ref[:, sl].astype(F32))).astype(out_ref.dtype)

    row = pl.BlockSpec((bt, D), lambda i: (i, 0))
    return pl.pallas_call(
        body, grid=(T // bt,),
        in_specs=[row, pl.BlockSpec((bt, D), lambda i: (i, C_ZG // D)), pl.BlockSpec((1, D), lambda i: (0, 0))],
        out_specs=row, out_shape=jax.ShapeDtypeStruct((T, 2 * D), BF16), name="gdn_post_fwd",
        compiler_params=_params(("parallel",)))(o, p, w_x)


def _gdn_post_bwd(dx1, w_out, o, p, w_x, *, bt=512):
    T = o.shape[0]
    bt = min(bt, T)

    def body(dx_ref, wo_ref, o_ref, z_ref, w_ref, do_ref, dz_ref, dw_ref):
        i = pl.program_id(0)

        @pl.when(i == 0)
        def _():
            dw_ref[...] = jnp.zeros_like(dw_ref)

        dmix = _dot(dx_ref[...], wo_ref[...], _NT)
        for h in range(GDN_H):
            sl = slice(h * 128, (h + 1) * 128)
            oh, zh, wh = o_ref[:, sl].astype(F32), z_ref[:, sl].astype(F32), w_ref[:, sl]
            dm = dmix[:, sl]
            r = lax.rsqrt(jnp.mean(oh * oh, axis=1, keepdims=True) + EPS)
            ohat = oh * r
            dy = dm * _silu(zh)
            dz_ref[:, sl] = (dm * ohat * wh * _dsilu(zh)).astype(dz_ref.dtype)
            dohat = dy * wh
            do_ref[:, sl] = (r * (dohat - ohat * jnp.mean(dohat * ohat, axis=1, keepdims=True))).astype(do_ref.dtype)
            dw_ref[:, sl] += jnp.sum(dy * ohat, axis=0, keepdims=True)

    row = pl.BlockSpec((bt, D), lambda i: (i, 0))
    zcol = pl.BlockSpec((bt, D), lambda i: (i, C_ZG // D))
    vec = pl.BlockSpec((1, D), lambda i: (0, 0))
    return pl.pallas_call(
        body, grid=(T // bt,), in_specs=[row, pl.BlockSpec((D, D), lambda i: (0, 0)), row, zcol, vec],
        out_specs=[row, zcol, vec],
        out_shape=[jax.ShapeDtypeStruct((T, D), BF16), jax.ShapeDtypeStruct((T, C_TOT), BF16),
                   jax.ShapeDtypeStruct((1, D), F32)],
        name="gdn_post_bwd", compiler_params=_params(("arbitrary",)))(dx1, w_out, o, p, w_x)


def _ssd_post_fwd(y, xs, p, d_x, w, mix_in, *, bt=1024):
    T = y.shape[0]
    bt = min(bt, T)

    def body(y_ref, x_ref, z_ref, d_ref, w_ref, mix_ref, out_ref):
        yg = (y_ref[...].astype(F32) + x_ref[...] * d_ref[...]) * _silu(z_ref[...].astype(F32))
        for g in range(2):
            sl = slice(g * 512, (g + 1) * 512)
            a = yg[:, sl]
            r = lax.rsqrt(jnp.mean(a * a, axis=1, keepdims=True) + EPS)
            out_ref[:, sl] = (a * r * w_ref[:, sl]).astype(out_ref.dtype)

    row = pl.BlockSpec((bt, D), lambda i: (i, 0))
    vec = pl.BlockSpec((1, D), lambda i: (0, 0))
    return pl.pallas_call(
        body, grid=(T // bt,),
        in_specs=[row, row, pl.BlockSpec((bt, D), lambda i: (i, C_ZS // D)), vec, vec, _ANY],
        out_specs=pl.BlockSpec((bt, D), lambda i: (i, 1)), out_shape=jax.ShapeDtypeStruct((T, 2 * D), BF16),
        input_output_aliases={5: 0}, name="ssd_post_fwd",
        compiler_params=_params(("parallel",)))(y, xs, p, d_x, w, mix_in)


def _ssd_post_bwd(dx1, w_out, y, xs, p, d_x, w, dp_in, *, bt=512):
    T = y.shape[0]
    bt = min(bt, T)

    def body(dx_ref, wo_ref, y_ref, x_ref, z_ref, d_ref, w_ref, dpin_ref, dyy_ref, dz_ref, dd_ref, dw_ref):
        i = pl.program_id(0)

        @pl.when(i == 0)
        def _():
            dd_ref[...] = jnp.zeros_like(dd_ref)
            dw_ref[...] = jnp.zeros_like(dw_ref)

        dmix = _dot(dx_ref[...], wo_ref[...], _NT)
        xv, zv = x_ref[...], z_ref[...].astype(F32)
        yy = y_ref[...].astype(F32) + xv * d_ref[...]
        sz = _silu(zv)
        yg = yy * sz
        parts = []
        for g in range(2):
            sl = slice(g * 512, (g + 1) * 512)
            a = yg[:, sl]
            r = lax.rsqrt(jnp.mean(a * a, axis=1, keepdims=True) + EPS)
            ah = a * r
            dout = dmix[:, sl]
            dah = dout * w_ref[:, sl]
            dw_ref[:, sl] += jnp.sum(dout * ah, axis=0, keepdims=True)
            parts.append(r * (dah - ah * jnp.mean(dah * ah, axis=1, keepdims=True)))
        dyg = jnp.concatenate(parts, axis=1)
        dyy = dyg * sz
        dyy_ref[...] = dyy
        dz_ref[...] = (dyg * yy * _dsilu(zv)).astype(dz_ref.dtype)
        dd_ref[...] += jnp.sum(dyy * xv, axis=0, keepdims=True)

    row = pl.BlockSpec((bt, D), lambda i: (i, 0))
    zcol = pl.BlockSpec((bt, D), lambda i: (i, C_ZS // D))
    vec = pl.BlockSpec((1, D), lambda i: (0, 0))
    return pl.pallas_call(
        body, grid=(T // bt,),
        in_specs=[row, pl.BlockSpec((D, D), lambda i: (1, 0)), row, row, zcol, vec, vec, _ANY],
        out_specs=[row, zcol, vec, vec],
        out_shape=[jax.ShapeDtypeStruct((T, D), F32), jax.ShapeDtypeStruct((T, C_TOT), BF16),
                   jax.ShapeDtypeStruct((1, D), F32), jax.ShapeDtypeStruct((1, D), F32)],
        input_output_aliases={7: 1}, name="ssd_post_bwd",
        compiler_params=_params(("arbitrary",)))(dx1, w_out, y, xs, p, d_x, w, dp_in)


_NEG = -1e30


def _gdn_terms(q, k, v, bx, gam_c):
    C = GDN_C
    ri, ci = _iota2((C, C), 0), _iota2((C, C), 1)
    eye, low, strict = ri == ci, ri >= ci, ri > ci
    gam_r = jnp.sum(jnp.where(eye, gam_c, 0.0), axis=0, keepdims=True)
    G = jnp.exp(jnp.where(low, gam_c - gam_r, _NEG))
    glast = jnp.sum(jnp.where(_iota2((C, 1), 0) == C - 1, gam_c, 0.0), axis=0, keepdims=True)
    eg, egl, eL = jnp.exp(gam_c), jnp.exp(glast - gam_c), jnp.exp(glast)
    kb, vb = k * bx, v * bx
    M = _dot(kb, k, _NT)
    return dict(eye=eye, low=low, strict=strict, G=G, eg=eg, egl=egl, eL=eL, kb=kb, vb=vb, M=M,
                kbg=kb * eg, qd=q * eg, kd=k * egl, q=q, k=k, v=v, bx=bx)


def _split(a):
    hi = a.astype(_MXU)
    return hi, (a - hi.astype(F32)).astype(_MXU)


def _dot3s(a, b):
    d = lambda p, q: lax.dot_general(p, q, _NN, preferred_element_type=F32)
    return d(a[0], b[0]) + d(a[0], b[1]) + d(a[1], b[0])


def _tri_inv_many(Ls, eye):
    eyef = jnp.where(eye, 1.0, 0.0)
    Ts = [eyef - L for L in Ls]
    Ps = [-L for L in Ls]
    for _ in range(5):
        sp = [_split(p) for p in Ps]
        Ps = [_dot3s(s, s) for s in sp]
        sp = [_split(p) for p in Ps]
        st = [_split(t) for t in Ts]
        Ts = [t + _dot3s(a, b) for t, a, b in zip(Ts, st, sp)]
    return Ts


def _lane_col(tile, idx):
    return jnp.sum(jnp.where(_iota2(tile.shape, 1) == idx, tile, 0.0), axis=1, keepdims=True)


def _gdn_heads(q_ref, k_ref, v_ref, bg_ref, heads):
    out = []
    bg = bg_ref[...]
    for h in heads:
        sl = slice(h * 128, (h + 1) * 128)
        out.append(_gdn_terms(q_ref[:, sl], k_ref[:, sl], v_ref[:, sl], _lane_col(bg, h), _lane_col(bg, GDN_H + h)))
    return out


def _gdn_prep(qk, v, bg, ride=None):
    T = qk.shape[0]
    N = T // GDN_C
    C, CS = GDN_C, GDN_LOCAL_CHUNKS
    NB = N // CS
    n_ride = ride.n if ride else 0

    def body(q_ref, k_ref, v_ref, bg_ref, *rest):
        ride_in = rest[:n_ride]
        u_ref, w_ref, qd_ref, kd_ref, p_ref, t_ref = rest[n_ride:n_ride + 6]
        ride_out = rest[n_ride + 6:2 * n_ride + 6]
        if ride:
            @pl.when(pl.program_id(0) == 0)
            def _():
                ride.start(ride_in, ride_out, rest[-3:])

            @pl.when(pl.program_id(0) == NB - 1)
            def _():
                ride.wait(ride_in, ride_out, rest[-3:])

        items = [(c, h) for c in range(CS) for h in range(GDN_H)]
        views = [[r.at[pl.ds(c * C, C)] for r in (q_ref, k_ref, v_ref, bg_ref)] for c in range(CS)]
        ts = [_gdn_heads(*views[c], [h])[0] for c, h in items]
        Ts = _tri_inv_many([jnp.where(t["strict"], t["M"] * t["G"], 0.0) for t in ts], ts[0]["eye"])
        for (c, h), t, Tm in zip(items, ts, Ts):
            tok = slice(c * C, (c + 1) * C)
            sl = slice(h * 128, (h + 1) * 128)
            rows = slice(h * C, (h + 1) * C)
            u_ref[tok, sl] = _dot(Tm, t["vb"])
            w_ref[tok, sl] = _dot(Tm, t["kbg"]).astype(w_ref.dtype)
            qd_ref[tok, sl] = t["qd"].astype(qd_ref.dtype)
            kd_ref[tok, sl] = t["kd"].astype(kd_ref.dtype)
            p_ref[c, rows, :] = _dot(t["q"], t["k"], _NT) * t["G"]
            t_ref[c, rows, :] = Tm

    blk = lambda c: pl.BlockSpec((CS * C, D), lambda n: (n, c))
    sq = pl.BlockSpec((CS, GDN_H * C, C), lambda n: (n, 0, 0))
    in_specs = [blk(0), blk(1), blk(0), pl.BlockSpec((CS * C, 128), lambda n: (n, 0))]
    out_specs = [blk(0), blk(0), blk(0), blk(0), sq, sq]
    out_shape = [jax.ShapeDtypeStruct((T, D), F32), jax.ShapeDtypeStruct((T, D), BF16),
                 jax.ShapeDtypeStruct((T, D), BF16), jax.ShapeDtypeStruct((T, D), BF16),
                 jax.ShapeDtypeStruct((N, GDN_H * C, C), F32), jax.ShapeDtypeStruct((N, GDN_H * C, C), F32)]
    ins = [qk, qk, v, bg]
    if ride:
        ins, in_specs = ins + ride.srcs, in_specs + ride.specs
        out_shape, out_specs = out_shape + ride.out_shape, out_specs + ride.specs
    res = pl.pallas_call(
        body, grid=(NB,), in_specs=in_specs, out_specs=out_specs, out_shape=out_shape,
        scratch_shapes=ride.scratch if ride else [], name="gdn_prep",
        compiler_params=_params(("arbitrary",) if ride else ("parallel",)))(*ins)
    return (list(res[:6]), list(res[6:])) if ride else list(res)


def _gdn_scan_fwd(u, w, qd, kd, pm, bg):
    T = u.shape[0]
    N = T // GDN_C
    C, CS = GDN_C, GDN_SCAN_CHUNKS

    def body(u_ref, w_ref, qd_ref, kd_ref, p_ref, bg_ref, o_ref, vn_ref, ss_ref, S_scr):
        n = pl.program_id(0)

        @pl.when(n == 0)
        def _():
            S_scr[...] = jnp.zeros_like(S_scr)

        sls = [slice(h * 128, (h + 1) * 128) for h in range(GDN_H)]
        for c in range(CS):
            rows = slice(c * C, (c + 1) * C)
            glast = bg_ref[(c + 1) * C - 1:(c + 1) * C, :]
            Ss = [S_scr[:, sl] for sl in sls]
            vns = [u_ref[rows, sl] - _dot(w_ref[rows, sl], S) for sl, S in zip(sls, Ss)]
            for h, (sl, S, vn) in enumerate(zip(sls, Ss, vns)):
                ss_ref[c, :, sl] = S.astype(ss_ref.dtype)
                vn_ref[rows, sl] = vn.astype(vn_ref.dtype)
                o_ref[rows, sl] = (_dot(qd_ref[rows, sl], S)
                                   + _dot(p_ref[c, h * C:(h + 1) * C, :], vn)).astype(o_ref.dtype)
                S_scr[:, sl] = S * jnp.exp(_lane_col(glast, GDN_H + h)) + _dot(kd_ref[rows, sl], vn, _TN)

    blk = pl.BlockSpec((CS * C, D), lambda n: (n, 0))
    return dict(
        body=body, steps=N // CS, ins=[u, w, qd, kd, pm, bg],
        in_specs=[blk, blk, blk, blk, pl.BlockSpec((CS, GDN_H * C, C), lambda n: (n, 0, 0)),
                  pl.BlockSpec((CS * C, 128), lambda n: (n, 0))],
        out_specs=[blk, blk, pl.BlockSpec((CS, GDN_DK, D), lambda n: (n, 0, 0))],
        out_shape=[jax.ShapeDtypeStruct((T, D), BF16), jax.ShapeDtypeStruct((T, D), BF16),
                   jax.ShapeDtypeStruct((N, GDN_DK, D), BF16)],
        scratch=[pltpu.VMEM((GDN_DK, D), F32)])


def _gdn_scan_bwd(w, qd, kd, pm, bg, do):
    T = w.shape[0]
    N = T // GDN_C
    C, CS = GDN_C, GDN_SCAN_CHUNKS
    NB = N // CS

    def body(w_ref, qd_ref, kd_ref, p_ref, bg_ref, do_ref, dvn_ref, ds_ref, dS_scr):
        n = pl.program_id(0)

        @pl.when(n == 0)
        def _():
            dS_scr[...] = jnp.zeros_like(dS_scr)

        sls = [slice(h * 128, (h + 1) * 128) for h in range(GDN_H)]
        for c in reversed(range(CS)):
            rows = slice(c * C, (c + 1) * C)
            glast = bg_ref[(c + 1) * C - 1:(c + 1) * C, :]
            dSs = [dS_scr[:, sl] for sl in sls]
            dvns = [_dot(p_ref[c, h * C:(h + 1) * C, :], do_ref[rows, sl], _TN) + _dot(kd_ref[rows, sl], dS2)
                    for h, (sl, dS2) in enumerate(zip(sls, dSs))]
            for h, (sl, dS2, dvn) in enumerate(zip(sls, dSs, dvns)):
                ds_ref[c, :, sl] = dS2.astype(ds_ref.dtype)
                dvn_ref[rows, sl] = dvn.astype(dvn_ref.dtype)
                dS_scr[:, sl] = (dS2 * jnp.exp(_lane_col(glast, GDN_H + h))
                                 + _dot(qd_ref[rows, sl], do_ref[rows, sl], _TN) - _dot(w_ref[rows, sl], dvn, _TN))

    blk = pl.BlockSpec((CS * C, D), lambda n: (NB - 1 - n, 0))
    return dict(
        body=body, steps=NB, ins=[w, qd, kd, pm, bg, do],
        in_specs=[blk, blk, blk, pl.BlockSpec((CS, GDN_H * C, C), lambda n: (NB - 1 - n, 0, 0)),
                  pl.BlockSpec((CS * C, 128), lambda n: (NB - 1 - n, 0)), blk],
        out_specs=[blk, pl.BlockSpec((CS, GDN_DK, D), lambda n: (NB - 1 - n, 0, 0))],
        out_shape=[jax.ShapeDtypeStruct((T, D), BF16), jax.ShapeDtypeStruct((N, GDN_DK, D), BF16)],
        scratch=[pltpu.VMEM((GDN_DK, D), F32)])


def _gdn_rest_bwd(qk, v, bg, s_save, t_save, vn, dvn, ds_save, do, ride=None):
    T = qk.shape[0]
    N = T // GDN_C
    C, CS = GDN_C, GDN_REST_CHUNKS
    NB = N // CS
    n_ride = ride.n if ride else 0

    def body(q_ref, k_ref, v_ref, bg_ref, ss_ref, ts_ref, vn_ref, dvn_ref, ds_ref, do_ref, *rest):
        ride_in = rest[:n_ride]
        dqkv_ref, dbg_ref = rest[n_ride:n_ride + 2]
        ride_out = rest[n_ride + 2:2 * n_ride + 2]
        if ride:
            @pl.when(pl.program_id(0) == 0)
            def _():
                ride.start(ride_in, ride_out, rest[-3:])

            @pl.when(pl.program_id(0) == NB - 1)
            def _():
                ride.wait(ride_in, ride_out, rest[-3:])

        items = [(c, h) for c in range(CS) for h in range(GDN_H)]
        toks = [slice(c * C, (c + 1) * C) for c, _ in items]
        sls = [slice(h * 128, (h + 1) * 128) for _, h in items]
        views = [[r.at[pl.ds(c * C, C)] for r in (q_ref, k_ref, v_ref, bg_ref)] for c in range(CS)]
        ts = [_gdn_heads(*views[c], [h])[0] for c, h in items]
        Ss = [ss_ref[c, :, sl] for (c, _), sl in zip(items, sls)]
        Tms = [ts_ref[c, h * C:(h + 1) * C, :] for c, h in items]
        dS2s = [ds_ref[c, :, sl] for (c, _), sl in zip(items, sls)]
        dos = [do_ref[tok, sl] for tok, sl in zip(toks, sls)]
        vns = [vn_ref[tok, sl] for tok, sl in zip(toks, sls)]
        dvns = [dvn_ref[tok, sl] for tok, sl in zip(toks, sls)]
        Qs = [_dot(t["q"], t["k"], _NT) for t in ts]
        dws = [-_dot(dvn, S, _NT) for dvn, S in zip(dvns, Ss)]
        dqds = [_dot(do, S, _NT) for do, S in zip(dos, Ss)]
        dPs = [jnp.where(t["low"], _dot(do, vn, _NT), 0.0) for t, do, vn in zip(ts, dos, vns)]
        dkds = [_dot(vn, dS2, _NT) for vn, dS2 in zip(vns, dS2s)]
        dTs = [_dot(dvn, t["vb"], _NT) + _dot(dw, t["kbg"], _NT) for t, dvn, dw in zip(ts, dvns, dws)]
        dvbs = [_dot(Tm, dvn, _TN) for Tm, dvn in zip(Tms, dvns)]
        dkbgs = [_dot(Tm, dw, _TN) for Tm, dw in zip(Tms, dws)]
        TdTs = [_dot(Tm, dT, _TN) for Tm, dT in zip(Tms, dTs)]
        dLs = [jnp.where(t["strict"], -_dot(TdT, Tm, _NT), 0.0) for t, TdT, Tm in zip(ts, TdTs, Tms)]
        dMs = [dL * t["G"] for t, dL in zip(ts, dLs)]
        dQs = [dP * t["G"] for t, dP in zip(ts, dPs)]
        dkbs = [_dot(dM, t["k"]) + dkbg * t["eg"] for t, dM, dkbg in zip(ts, dMs, dkbgs)]
        rs = lambda a: jnp.sum(a, axis=1, keepdims=True)
        lane = _iota2((C, 128), 1)
        last = _iota2((C, 1), 0) == C - 1
        dbg = [jnp.zeros((C, 128), F32) for _ in range(CS)]
        for i, (c, h) in enumerate(items):
            t, sl, tok = ts[i], sls[i], toks[i]
            E = (dLs[i] * t["M"] + dPs[i] * Qs[i]) * t["G"]
            dqkv_ref[tok, sl] = _dot(dQs[i], t["k"]) + dqds[i] * t["eg"]
            dqkv_ref[tok, D + h * 128:D + (h + 1) * 128] = (
                _dot(dQs[i], t["q"], _TN) + _dot(dMs[i], t["kb"], _TN) + dkds[i] * t["egl"] + dkbs[i] * t["bx"])
            dqkv_ref[tok, 2 * D + h * 128:2 * D + (h + 1) * 128] = dvbs[i] * t["bx"]
            dbeta_c = rs(dkbs[i] * t["k"] + dvbs[i] * t["v"])
            dkd_kd = dkds[i] * t["kd"]
            dgam_c = rs(dqds[i] * t["qd"]) + rs(dkbgs[i] * t["kbg"]) - rs(dkd_kd) + rs(E)
            dgam_r = -jnp.sum(E, axis=0, keepdims=True)
            dgam_c = dgam_c + jnp.sum(jnp.where(t["eye"], dgam_r, 0.0), axis=1, keepdims=True)
            dlast = _sum_all(dkd_kd) + t["eL"] * _sum_all(Ss[i].astype(F32) * dS2s[i].astype(F32))
            dgam_c = dgam_c + jnp.where(last, dlast, 0.0)
            dbg[c] = dbg[c] + jnp.where(lane == h, dbeta_c, 0.0) + jnp.where(lane == GDN_H + h, dgam_c, 0.0)
        for c in range(CS):
            dbg_ref[c * C:(c + 1) * C, :] = dbg[c]

    blk = lambda c: pl.BlockSpec((CS * C, D), lambda n: (n, c))
    st = pl.BlockSpec((CS, GDN_DK, D), lambda n: (n, 0, 0))
    seg = pl.BlockSpec((CS * C, 128), lambda n: (n, 0))
    in_specs = [blk(0), blk(1), blk(0), seg, st,
                pl.BlockSpec((CS, GDN_H * C, C), lambda n: (n, 0, 0)), blk(0), blk(0), st, blk(0)]
    out_specs = [pl.BlockSpec((CS * C, 3 * D), lambda n: (n, 0)), seg]
    out_shape = [jax.ShapeDtypeStruct((T, 3 * D), F32), jax.ShapeDtypeStruct((T, 128), F32)]
    ins = [qk, qk, v, bg, s_save, t_save, vn, dvn, ds_save, do]
    if ride:
        ins, in_specs = ins + ride.srcs, in_specs + ride.specs
        out_shape, out_specs = out_shape + ride.out_shape, out_specs + ride.specs
    res = pl.pallas_call(
        body, grid=(NB,), in_specs=in_specs, out_specs=out_specs, out_shape=out_shape,
        scratch_shapes=ride.scratch if ride else [], name="gdn_rest_bwd",
        compiler_params=_params(("arbitrary",) if ride else ("parallel",)))(*ins)
    return (list(res[:2]), list(res[2:])) if ride else list(res)


def _ssd_seg(al_pair, half, s):
    L = SSM_L
    ri, ci = _iota2((L, L), 0), _iota2((L, L), 1)
    ac = jnp.max(jnp.where(half == s, al_pair, _NEG), axis=1, keepdims=True)
    ar = jnp.sum(jnp.where(ri == ci, ac, 0.0), axis=0, keepdims=True)
    return jnp.exp(jnp.where(ri >= ci, ac - ar, _NEG))


def _last_row(a):
    return jnp.sum(jnp.where(_iota2((a.shape[0], 1), 0) == a.shape[0] - 1, a, 0.0), axis=0, keepdims=True)


def _ssd_expand(da_ref):
    da = da_ref[...]
    return _dot_sel(da, _expand_mats(6, 0)[0]), _dot_sel(da, _expand_mats(6, SSM_H)[0])


def _ssd_core_fwd(xbc, da):
    T = xbc.shape[0]
    L, CS = SSM_L, SSM_SCAN_CHUNKS
    Nc = T // L

    def body(x_all, bc_all, da_all, y_all, hs_all, H_scr):
        @pl.when(pl.program_id(0) == 0)
        def _():
            H_scr[...] = jnp.zeros_like(H_scr)

        for cc in range(CS):
            rows = pl.ds(cc * L, L)
            chunk(x_all.at[rows], bc_all.at[rows], da_all.at[rows], y_all.at[rows], hs_all.at[cc], H_scr)

    def chunk(x_ref, bc_ref, da_ref, y_ref, hs_ref, H_scr):
        dt_ref, al_ref = _ssd_expand(da_ref)
        half = _iota2((L, 128), 1) >> 6
        for g in range(2):
            gs = slice(g * 512, (g + 1) * 512)
            Bg = bc_ref[:, g * 128:(g + 1) * 128]
            Cg = bc_ref[:, 256 + g * 128:256 + (g + 1) * 128]
            alg = al_ref[:, gs]
            alast = _last_row(alg)
            xdt = x_ref[:, gs] * dt_ref[:, gs]
            Hg = H_scr[:, gs]
            hs_ref[:, gs] = Hg
            CB = _dot(Cg, Bg, _NT)
            y_off = jnp.exp(alg) * _dot(Cg, Hg)
            H_scr[:, gs] = Hg * jnp.exp(alast) + _dot(Bg, jnp.exp(alast - alg) * xdt, _TN)
            for j in range(4):
                ps = slice(g * 512 + j * 128, g * 512 + (j + 1) * 128)
                al_pair = al_ref[:, ps]
                xp = x_ref[:, ps] * dt_ref[:, ps]
                ys = [_dot(_ssd_seg(al_pair, half, s) * CB, xp) for s in range(2)]
                y_ref[:, ps] = (y_off[:, j * 128:(j + 1) * 128]
                                + jnp.where(half == 0, ys[0], ys[1])).astype(y_ref.dtype)

    row = pl.BlockSpec((CS * L, D), lambda c: (c, 0))
    return dict(
        body=body, steps=Nc // CS, ins=[xbc, xbc, da],
        in_specs=[row, pl.BlockSpec((CS * L, 512), lambda c: (c, 2)), pl.BlockSpec((CS * L, 128), lambda c: (c, 0))],
        out_specs=[row, pl.BlockSpec((CS, SSM_N, D), lambda c: (c, 0, 0))],
        out_shape=[jax.ShapeDtypeStruct((T, D), BF16), jax.ShapeDtypeStruct((Nc, SSM_N, D), F32)],
        scratch=[pltpu.VMEM((SSM_N, D), F32)])


def _ssd_core_bwd(xbc, da, h_save, dyy, d_x):
    T = xbc.shape[0]
    L, CS = SSM_L, SSM_SCAN_CHUNKS
    Nc = T // L
    NB = Nc // CS

    def body(x_all, bc_all, da_all, hs_all, dy_all, d_ref, dx_all, dda_all, dH_scr, ddt_ref, dal_ref):
        @pl.when(pl.program_id(0) == 0)
        def _():
            dH_scr[...] = jnp.zeros_like(dH_scr)

        for cc in reversed(range(CS)):
            rows = pl.ds(cc * L, L)
            chunk(x_all.at[rows], bc_all.at[rows], da_all.at[rows], hs_all.at[cc], dy_all.at[rows],
                  d_ref, dx_all.at[rows], ddt_ref, dal_ref, dH_scr)
            dda_all[rows, :] = (_dot_sel(ddt_ref[...], _expand_mats(6, 0)[1])
                                + _dot_sel(dal_ref[...], _expand_mats(6, SSM_H)[1]))

    def chunk(x_ref, bc_ref, da_ref, hs_ref, dy_ref, d_ref, dx_ref, ddt_ref, dal_ref, dH_scr):
        dt_ref, al_ref = _ssd_expand(da_ref)
        lane = _iota2((L, 128), 1)
        half = lane >> 6
        rowi = _iota2((L, 1), 0)
        ri, ci = _iota2((L, L), 0), _iota2((L, L), 1)
        for g in range(2):
            gs = slice(g * 512, (g + 1) * 512)
            Bg = bc_ref[:, g * 128:(g + 1) * 128]
            Cg = bc_ref[:, 256 + g * 128:256 + (g + 1) * 128]
            alg = al_ref[:, gs]
            alast = _last_row(alg)
            eal, edec, eL = jnp.exp(alg), jnp.exp(alast - alg), jnp.exp(alast)
            xg, dtg, dYg = x_ref[:, gs], dt_ref[:, gs], dy_ref[:, gs]
            xdt = xg * dtg
            Hg = hs_ref[:, gs]
            dH2 = dH_scr[:, gs]
            CB = _dot(Cg, Bg, _NT)
            dYe = eal * dYg
            dH_scr[:, gs] = dH2 * eL + _dot(Cg, dYe, _TN)
            dC = _dot(dYe, Hg, _NT)
            zg = edec * xdt
            dz = _dot(Bg, dH2)
            dB = _dot(zg, dH2, _NT)
            tz = dz * zg
            dal = dYe * _dot(Cg, Hg) - tz
            dalast = jnp.sum(tz, axis=0, keepdims=True) + eL * jnp.sum(Hg * dH2, axis=0, keepdims=True)
            dal = dal + jnp.where(rowi == L - 1, dalast, 0.0)
            dxdt_g = edec * dz
            dx_ref[:, gs] = dxdt_g * dtg + dYg * d_ref[:, gs]
            ddt_ref[:, gs] = dxdt_g * xg
            dal_ref[:, gs] = dal
            dCB = jnp.zeros((L, L), F32)
            for j in range(4):
                ps = slice(g * 512 + j * 128, g * 512 + (j + 1) * 128)
                al_pair = al_ref[:, ps]
                xp = x_ref[:, ps] * dt_ref[:, ps]
                dYp = dy_ref[:, ps]
                dxp = []
                dal_p = jnp.zeros((L, 128), F32)
                for s in range(2):
                    seg = _ssd_seg(al_pair, half, s)
                    W = seg * CB
                    dW = _dot(jnp.where(half == s, dYp, 0.0), xp, _NT)
                    dxp.append(_dot(W, dYp, _TN))
                    dCB = dCB + dW * seg
                    Es = dW * W
                    dac = jnp.sum(Es, axis=1, keepdims=True) - jnp.sum(
                        jnp.where(ri == ci, jnp.sum(Es, axis=0, keepdims=True), 0.0), axis=1, keepdims=True)
                    dal_p = dal_p + jnp.where(lane == 64 * s, dac, 0.0)
                dxdt_p = jnp.where(half == 0, dxp[0], dxp[1])
                dx_ref[:, ps] += dxdt_p * dt_ref[:, ps]
                ddt_ref[:, ps] += dxdt_p * x_ref[:, ps]
                dal_ref[:, ps] += dal_p
            dx_ref[:, D + g * 128:D + (g + 1) * 128] = dB + _dot(dCB, Cg, _TN)
            dx_ref[:, D + 256 + g * 128:D + 256 + (g + 1) * 128] = dC + _dot(dCB, Bg)

    row = pl.BlockSpec((CS * L, D), lambda c: (NB - 1 - c, 0))
    bcs = pl.BlockSpec((CS * L, 512), lambda c: (NB - 1 - c, 2))
    seg = pl.BlockSpec((CS * L, 128), lambda c: (NB - 1 - c, 0))
    return dict(
        body=body, steps=NB, ins=[xbc, xbc, da, h_save, dyy, d_x],
        in_specs=[row, bcs, seg, pl.BlockSpec((CS, SSM_N, D), lambda c: (NB - 1 - c, 0, 0)), row,
                  pl.BlockSpec((1, D), lambda c: (0, 0))],
        out_specs=[pl.BlockSpec((CS * L, D + 512), lambda c: (NB - 1 - c, 0)), seg],
        out_shape=[jax.ShapeDtypeStruct((T, D + 512), F32), jax.ShapeDtypeStruct((T, 128), F32)],
        scratch=[pltpu.VMEM((SSM_N, D), F32), pltpu.VMEM((L, D), F32), pltpu.VMEM((L, D), F32)])


def _run_scans(parts, *, name):
    steps = parts[0]["steps"]
    assert all(p["steps"] == steps for p in parts)
    cnt = lambda key: [len(p[key]) for p in parts]
    n_in, n_out, n_scr = cnt("ins"), cnt("out_shape"), cnt("scratch")

    def body(*refs):
        ins, outs, scr = refs[:sum(n_in)], refs[sum(n_in):sum(n_in) + sum(n_out)], refs[sum(n_in) + sum(n_out):]
        oi = oo = os_ = 0
        for p, a, b, c in zip(parts, n_in, n_out, n_scr):
            p["body"](*ins[oi:oi + a], *outs[oo:oo + b], *scr[os_:os_ + c])
            oi, oo, os_ = oi + a, oo + b, os_ + c

    cat = lambda key: [v for p in parts for v in p[key]]
    res = pl.pallas_call(
        body, grid=(steps,), in_specs=cat("in_specs"), out_specs=cat("out_specs"), out_shape=cat("out_shape"),
        scratch_shapes=cat("scratch"), name=name, compiler_params=_params(("arbitrary",)))(*cat("ins"))
    out, o = [], 0
    for b in n_out:
        out.append(list(res[o:o + b]))
        o += b
    return out


_EARLY = ("w_out", "wq_mem", "wk_mem", "wv_mem", "wo_mem")
_LATE = ("w_up", "w_down")
_GRADS_MLP = ("w_down", "w_up")
_GRADS_MID = ("wo_mem", "wq_mem", "wk_mem", "wv_mem", "w_out")


def _gather_ride(shards, names):
    return None if shards is None else _Ride([shards[n] for n in names], shard=True)


def _grad_ride(shards, G, names):
    return None if shards is None else _Ride([_slots_from_full(n, G[n]) for n in names], shard=False)


def _local_step(x, mem, tgt, W, shards=None):
    T = x.shape[0]
    W = dict(W)
    cw_qk, cw_v = W["gdn_conv_w"][:, :2 * D], W["gdn_conv_w"][:, 2 * D:]
    h1 = _rmsnorm_fwd(x, W["norm1_w"], name="norm1_fwd")
    ride = _gather_ride(shards, _EARLY)
    pg = _mm(h1, W["w_in_pad"], b_cols=(C_GATE, C_TOT - C_GATE), name="in_proj_gates")
    p = _mm(h1, W["w_in_pad"], b_cols=(0, C_GATE), out_dtype=BF16, bn_cap=1664, name="in_proj", ride=ride)
    if ride:
        p, got = p
        W.update({n: _full_from_slots(n, g) for n, g in zip(_EARLY, got)})
    qk = _conv_fwd(p, C_QKV, 2 * D, cw_qk, None, l2=True, name="gdn_conv_qk_fwd")
    v_g = _conv_fwd(p, C_QKV + 2 * D, D, cw_v, None, l2=False, name="gdn_conv_v_fwd")
    bg = _gdn_gates_fwd(pg, W["gdn_alog_row"], W["gdn_dtb_row"])
    ride = _gather_ride(shards, _LATE)
    prep = _gdn_prep(qk, v_g, bg, ride)
    if ride:
        prep, got = prep
        W.update({n: _full_from_slots(n, g) for n, g in zip(_LATE, got)})
    u_g, w_g, qd_g, kd_g, p_g, t_save = prep
    xbc = _conv_fwd(p, C_XBC, D + 512, W["ssm_conv_w"], W["ssm_conv_b"], l2=False, name="ssm_conv_fwd", bc=512)
    da_s = _ssd_dt_fwd(pg, W["ssm_dtb_row"], W["ssm_alog_row"])
    (o_g, vn_g, s_save), (y_s, h_save) = _run_scans(
        [_gdn_scan_fwd(u_g, w_g, qd_g, kd_g, p_g, bg), _ssd_core_fwd(xbc, da_s)], name="scans_fwd")
    mix = _gdn_post_fwd(o_g, p, W["gdn_norm_x"])
    mix = _ssd_post_fwd(y_s, xbc, p, W["ssm_d_x"], W["ssm_norm_w"].reshape(1, D), mix)
    x1, h2 = _mm(mix, W["w_out"], epi="res_norm", extra=(x, W["norm2_w"]), bm=512, name="out_proj")
    qm = _mm(h2, W["wq_mem"], out_dtype=BF16, name="q_proj")
    m = _rmsnorm_fwd(mem, W["mem_norm_w"], name="mem_norm_fwd")
    km = _mm(m, W["wk_mem"], name="k_proj")
    vm = _mm(m, W["wv_mem"], name="v_proj")
    oa = _attn_fwd(qm, km, vm)
    x2, h3 = _mm(oa, W["wo_mem"], epi="res_norm", extra=(x1, W["norm3_w"]), bm=512, name="o_proj")
    u, act = _mm(h3, W["w_up"], epi="relu2", out_dtype=BF16, bn_cap=2048, b_resident=True, name="mlp_up")
    dx3, g_final, loss = _mm(act, W["w_down"], epi="res_loss", extra=(x2, tgt, W["final_norm_w"]), bk_cap=1024,
                             b_resident=True, name="mlp_down_loss")
    G = {"final_norm_w": g_final.reshape(D)}
    dpre = _mm(dx3, W["w_down"], dims="nt", epi="mul2", extra=u, out_dtype=BF16, b_resident=True, name="mlp_down_dx")
    G["w_down"] = _mm(act, dx3, dims="tn", out_dtype=BF16, name="mlp_down_dw")
    G["w_up"] = _mm(h3, dpre, dims="tn", out_dtype=BF16, bk_cap=4096, name="mlp_up_dw")
    dx2, gw = _mm(dpre, W["w_up"], dims="nt", epi="norm_bwd", extra=(x2, dx3, W["norm3_w"]), bk_cap=1024,
                  b_resident=True, name="mlp_up_dx")
    G["norm3_w"] = gw.reshape(D)
    do_a = _mm(dx2, W["wo_mem"], dims="nt", out_dtype=BF16, name="o_proj_dx")
    G["wo_mem"] = _mm(oa, dx2, dims="tn", out_dtype=BF16, name="o_proj_dw")
    dq, dk, dv = _attn_bwd(qm, km, vm, do_a)
    G["wq_mem"] = _mm(h2, dq, dims="tn", out_dtype=BF16, name="q_proj_dw")
    dx1, gw = _mm(dq, W["wq_mem"], dims="nt", epi="norm_bwd", extra=(x1, dx2, W["norm2_w"]), bm=512,
                  name="q_proj_dx")
    G["norm2_w"] = gw.reshape(D)
    G["wk_mem"] = _mm(m, dk, dims="tn", out_dtype=BF16, name="k_proj_dw")
    G["wv_mem"] = _mm(m, dv, dims="tn", out_dtype=BF16, name="v_proj_dw")
    dm = _mm(dk, W["wk_mem"], dims="nt", name="k_proj_dx")
    dm = _mm(dv, W["wv_mem"], dims="nt", epi="res", extra=dm, name="v_proj_dx")
    _, G["mem_norm_w"] = _rmsnorm_bwd(mem, W["mem_norm_w"], dm, None, name="mem_norm_bwd")
    G["w_out"] = _mm(mix, dx1, dims="tn", out_dtype=BF16, name="out_proj_dw")
    do_g, dp, G["gdn_norm_x"] = _gdn_post_bwd(dx1, W["w_out"], o_g, p, W["gdn_norm_x"])
    dyy, dp, G["ssm_d_x"], G["ssm_norm_w"] = _ssd_post_bwd(dx1, W["w_out"], y_s, xbc, p, W["ssm_d_x"],
                                                          W["ssm_norm_w"].reshape(1, D), dp)
    (dvn_g, ds_save), (dxbc, dda_s) = _run_scans(
        [_gdn_scan_bwd(w_g, qd_g, kd_g, p_g, bg, do_g), _ssd_core_bwd(xbc, da_s, h_save, dyy, W["ssm_d_x"])],
        name="scans_bwd")
    ride = _grad_ride(shards, G, _GRADS_MLP)
    rest = _gdn_rest_bwd(qk, v_g, bg, s_save, t_save, vn_g, dvn_g, ds_save, do_g, ride)
    if ride:
        rest, got = rest
        G.update(zip(_GRADS_MLP, got))
    dqkvn, dbg = rest
    dy_qk, gcw_qk, _ = _conv_bwd_act(p, C_QKV, 2 * D, cw_qk, None, dqkvn, 0, l2=True, name="gdn_conv_qk_bwd_act")
    dy_v, gcw_v, _ = _conv_bwd_act(p, C_QKV + 2 * D, D, cw_v, None, dqkvn, 2 * D, l2=False,
                                   name="gdn_conv_v_bwd_act")
    G["gdn_conv_w"] = jnp.concatenate([gcw_qk, gcw_v], axis=1)
    dp = _conv_bwd_in(dy_qk, cw_qk, dp, C_QKV, T, name="gdn_conv_qk_bwd_in")
    dp = _conv_bwd_in(dy_v, cw_v, dp, C_QKV + 2 * D, T, name="gdn_conv_v_bwd_in")
    dp, G["gdn_alog_row"], G["gdn_dtb_row"] = _gdn_gates_bwd(pg, W["gdn_alog_row"], W["gdn_dtb_row"], dbg, dp)
    dy_s, G["ssm_conv_w"], G["ssm_conv_b"] = _conv_bwd_act(p, C_XBC, D + 512, W["ssm_conv_w"], W["ssm_conv_b"],
                                                           dxbc, 0, l2=False, name="ssm_conv_bwd_act", bc=512)
    dp = _conv_bwd_in(dy_s, W["ssm_conv_w"], dp, C_XBC, T, name="ssm_conv_bwd_in", bc=512)
    dp, G["ssm_dtb_row"], G["ssm_alog_row"] = _ssd_dt_bwd(pg, W["ssm_dtb_row"], W["ssm_alog_row"], dda_s, dp)
    ride = _grad_ride(shards, G, _GRADS_MID)
    g_in = _mm(h1, dp, dims="tn", out_dtype=BF16, bn_cap=1152, bk_cap=4096, name="in_proj_dw", ride=ride)
    if ride:
        g_in, got = g_in
        G.update(zip(_GRADS_MID, got))
    G["w_in"] = _unpad_w_in(g_in)
    ride = _grad_ride(shards, G, ("w_in",))
    res = _mm(dp, W["w_in_pad"], dims="nt", epi="norm_bwd", extra=(x, dx1, W["norm1_w"]), b_resident=True,
              name="in_proj_dx", ride=ride)
    if ride:
        res, got = res
        G["w_in"] = got[0]
    dx, gw = res
    G["norm1_w"] = gw.reshape(D)
    return loss, dx, G


def _all_gather(shards, out_dtype, *, name):
    n = len(shards)

    def body(*refs):
        x_refs, out_refs, stage = refs[:n], refs[n:2 * n], refs[2 * n:3 * n]
        send_sems, recv_sems, local_sems = refs[3 * n:]
        x, y, c = _place()
        me, sibling = (x, y, c), (x, y, 1 - c)
        chips = [(1 - x, y), (x, 1 - y), (1 - x, 1 - y)]

        def slot(px, py, pc):
            return 4 * px + 2 * py + pc

        def copy(a, k, block, to, src=None):
            dst = out_refs[a].at[slot(*block)]
            return pltpu.make_async_remote_copy(
                src_ref=dst if src is None else src, dst_ref=dst, send_sem=send_sems.at[a, k],
                recv_sem=recv_sems.at[a, k], device_id=to, device_id_type=_MESH)

        for a in range(n):
            stage[a][...] = x_refs[a][...].astype(out_dtype)
        mine = [pltpu.make_async_copy(stage[a], out_refs[a].at[slot(*me)], local_sems.at[a]) for a in range(n)]
        for cp in mine:
            cp.start()
        first = []
        for a in range(n):
            first.append(copy(a, 0, me, sibling, src=stage[a]))
            first += [copy(a, 1 + j, me, (*chip, c), src=stage[a]) for j, chip in enumerate(chips)]
        for cp in first:
            cp.start()
        passed = [[copy(a, 4 + j, (*chip, c), sibling) for j, chip in enumerate(chips)] for a in range(n)]
        for j, chip in enumerate(chips):
            for a in range(n):
                copy(a, 1 + j, (*chip, c), me).wait_recv()
                passed[a][j].start()
        for a in range(n):
            copy(a, 0, sibling, me).wait_recv()
            for j, chip in enumerate(chips):
                copy(a, 4 + j, (*chip, 1 - c), me).wait_recv()
        for cp in first + [cp for row in passed for cp in row]:
            cp.wait_send()
        for cp in mine:
            cp.wait()

    outs = pl.pallas_call(
        body, in_specs=[_VM] * n, out_specs=[_ANY] * n,
        out_shape=[jax.ShapeDtypeStruct((N_DEV,) + s.shape, out_dtype) for s in shards],
        scratch_shapes=[pltpu.VMEM(s.shape, out_dtype) for s in shards]
        + [pltpu.SemaphoreType.DMA((n, 7)), pltpu.SemaphoreType.DMA((n, 7)), pltpu.SemaphoreType.DMA((n,))],
        name=name, compiler_params=pltpu.CompilerParams(vmem_limit_bytes=VMEM_LIMIT))(*shards)
    return list(outs)


def _cast_bf16(arrs, *, name):
    n = len(arrs)

    def body(*refs):
        for a in range(n):
            refs[n + a][...] = refs[a][...].astype(BF16)

    return list(pl.pallas_call(
        body, in_specs=[_VM] * n, out_specs=[_VM] * n,
        out_shape=[jax.ShapeDtypeStruct(s.shape, BF16) for s in arrs], name=name,
        compiler_params=pltpu.CompilerParams(vmem_limit_bytes=VMEM_LIMIT))(*arrs))


def _sum8(a, *, name):
    _, R, Cc = a.shape
    br = _pick_rows(R, 128)

    def body(a_ref, o_ref):
        s = a_ref[0].astype(F32)
        for k in range(1, N_DEV):
            s = s + a_ref[k].astype(F32)
        o_ref[...] = s

    return pl.pallas_call(
        body, grid=(R // br,), in_specs=[pl.BlockSpec((N_DEV, br, Cc), lambda i: (0, i, 0))],
        out_specs=pl.BlockSpec((br, Cc), lambda i: (i, 0)), out_shape=jax.ShapeDtypeStruct((R, Cc), F32),
        name=name, compiler_params=_params(("parallel",)))(a)


def _pick_rows(R, cap):
    if R <= cap:
        return R
    for d in range(cap, 7, -8):
        if R % d == 0:
            return d
    return R


def _adamw(w, g, m, v, *, name):
    shape = w.shape
    as2d = (lambda t: t.reshape(1, -1)) if w.ndim == 1 else (lambda t: t)
    w2, m2, v2 = as2d(w), as2d(m), as2d(v)
    R, Cc = w2.shape
    from_slabs = g.ndim == 3
    br = _pick_rows(R, 128 if from_slabs else 256)
    c1 = 1.0 - ADAM_B1 ** ADAM_STEP
    c2 = 1.0 - ADAM_B2 ** ADAM_STEP

    def body(w_ref, g_ref, m_ref, v_ref, go_ref, d_ref, nm_ref, nv_ref):
        if from_slabs:
            gv = g_ref[0].astype(F32)
            for k in range(1, N_DEV):
                gv = gv + g_ref[k].astype(F32)
        else:
            gv = g_ref[...]
        go_ref[...] = gv
        nm = ADAM_B1 * m_ref[...] + (1.0 - ADAM_B1) * gv
        nv = ADAM_B2 * v_ref[...] + (1.0 - ADAM_B2) * (gv * gv)
        nm_ref[...] = nm
        nv_ref[...] = nv
        d_ref[...] = -ADAM_LR * ((nm / c1) / (jnp.sqrt(nv / c2) + ADAM_EPS) + ADAM_WD * w_ref[...])

    blk = pl.BlockSpec((br, Cc), lambda i: (i, 0))
    g_spec = pl.BlockSpec((N_DEV, br, Cc), lambda i: (0, i, 0)) if from_slabs else blk
    outs = pl.pallas_call(
        body, grid=(R // br,), in_specs=[blk, g_spec, blk, blk], out_specs=[blk] * 4,
        out_shape=[jax.ShapeDtypeStruct((R, Cc), F32)] * 4, name=name,
        compiler_params=_params(("parallel",)))(w2, g if from_slabs else as2d(g), m2, v2)
    return tuple(o.reshape(shape) for o in outs)


_BIG = ("w_in", "w_out", "wq_mem", "wk_mem", "wv_mem", "wo_mem", "w_up", "w_down")
_COL_SHARDED = ("w_in", "w_up")
_WEIGHTS = ("norm1_w", "w_in", "gdn_conv_w", "gdn_a_log", "gdn_dt_bias", "gdn_norm_w", "ssm_conv_w", "ssm_conv_b",
            "ssm_a_log", "ssm_dt_bias", "ssm_d", "ssm_norm_w", "w_out", "norm2_w", "mem_norm_w", "wq_mem", "wk_mem",
            "wv_mem", "wo_mem", "norm3_w", "w_up", "w_down", "final_norm_w")
_IN_PAD = 112


def _move_col_slabs(a, to_slabs, *, name):
    n, R, c = (N_DEV, a.shape[0], a.shape[1] // N_DEV) if to_slabs else a.shape
    slab = pl.BlockSpec((None, R, c), lambda j: (j, 0, 0))
    cols = pl.BlockSpec((R, c), lambda j: (0, j))

    def body(a_ref, o_ref):
        o_ref[...] = a_ref[...]

    return pl.pallas_call(
        body, grid=(n,), in_specs=[cols if to_slabs else slab], out_specs=slab if to_slabs else cols,
        out_shape=jax.ShapeDtypeStruct((n, R, c) if to_slabs else (R, n * c), a.dtype), name=name,
        compiler_params=_params(("parallel",)))(a)


def _full_from_slots(name, g):
    if name in _COL_SHARDED:
        if g.shape[2] % 128 == 0:
            return _move_col_slabs(g, False, name="cols_" + name)
        return jnp.transpose(g, (1, 0, 2)).reshape(g.shape[1], N_DEV * g.shape[2])
    return g.reshape(N_DEV * g.shape[1], g.shape[2])


def _slots_from_full(name, f):
    if name in _COL_SHARDED:
        if (f.shape[1] // N_DEV) % 128 == 0:
            return _move_col_slabs(f, True, name="slabs_" + name)
        return jnp.transpose(f.reshape(f.shape[0], N_DEV, f.shape[1] // N_DEV), (1, 0, 2))
    return f.reshape(N_DEV, f.shape[0] // N_DEV, f.shape[1])


def _pad_w_in(w):
    z = jnp.zeros((w.shape[0], _IN_PAD), w.dtype)
    return jnp.concatenate([w[:, :4096], w[:, 4112:6672], w[:, 4096:4112], z, w[:, 6672:6688], z], axis=1)


def _unpad_w_in(gp):
    return jnp.concatenate([gp[:, :4096], gp[:, C_GATE:C_GATE + 16], gp[:, 4096:C_GATE], gp[:, C_DT:C_DT + 16]],
                           axis=1)


def _pack_rows(vals):
    rows, offs, r = [], [], 0
    for vflat in vals:
        nrow = 8 * -(-vflat.shape[0] // 1024)
        rows.append(jnp.pad(vflat, (0, nrow * 128 - vflat.shape[0])).reshape(nrow, 128))
        offs.append((r, vflat.shape[0]))
        r += nrow
    return jnp.concatenate(rows, axis=0), offs


def _unpack_rows(packed, offs, shapes):
    out = []
    for (r, nel), shp in zip(offs, shapes):
        nrow = -(-nel // 128)
        out.append(packed[r:r + nrow].reshape(-1)[:nel].reshape(shp))
    return out


def kernel(x, mem, norm1_w, w_in, gdn_conv_w, gdn_a_log, gdn_dt_bias, gdn_norm_w, ssm_conv_w, ssm_conv_b, ssm_a_log, ssm_dt_bias, ssm_d, ssm_norm_w, w_out, norm2_w, mem_norm_w, wq_mem, wk_mem, wv_mem, wo_mem, norm3_w, w_up, w_down, final_norm_w, loss_target, m_norm1_w, m_w_in, m_gdn_conv_w, m_gdn_a_log, m_gdn_dt_bias, m_gdn_norm_w, m_ssm_conv_w, m_ssm_conv_b, m_ssm_a_log, m_ssm_dt_bias, m_ssm_d, m_ssm_norm_w, m_w_out, m_norm2_w, m_mem_norm_w, m_wq_mem, m_wk_mem, m_wv_mem, m_wo_mem, m_norm3_w, m_w_up, m_w_down, m_final_norm_w, v_norm1_w, v_w_in, v_gdn_conv_w, v_gdn_a_log, v_gdn_dt_bias, v_gdn_norm_w, v_ssm_conv_w, v_ssm_conv_b, v_ssm_a_log, v_ssm_dt_bias, v_ssm_d, v_ssm_norm_w, v_w_out, v_norm2_w, v_mem_norm_w, v_wq_mem, v_wk_mem, v_wv_mem, v_wo_mem, v_norm3_w, v_w_up, v_w_down, v_final_norm_w):
    args = dict(locals())
    w_loc = {n: args[n] for n in _WEIGHTS}
    me = 4 * lax.axis_index("x") + 2 * lax.axis_index("y") + lax.axis_index("c")

    w_in_full = _full_from_slots("w_in", _all_gather([w_in], BF16, name="gather_w_in")[0])
    later = _EARLY + _LATE
    shards = dict(zip(later, _cast_bf16([w_loc[n] for n in later], name="cast_shards")))
    conv_pack, conv_offs = _pack_rows([gdn_conv_w.reshape(-1), ssm_conv_w.reshape(-1)])
    conv_all = _all_gather([conv_pack], F32, name="gather_conv")[0]
    gdn_cw, ssm_cw = [], []
    for k in range(N_DEV):
        a, b = _unpack_rows(conv_all[k], conv_offs, [gdn_conv_w.shape, ssm_conv_w.shape])
        gdn_cw.append(a)
        ssm_cw.append(b)
    W = {
        "w_in_pad": _pad_w_in(w_in_full),
        "norm1_w": norm1_w, "norm2_w": norm2_w, "norm3_w": norm3_w, "mem_norm_w": mem_norm_w,
        "final_norm_w": final_norm_w, "ssm_norm_w": ssm_norm_w, "ssm_conv_b": ssm_conv_b,
        "gdn_conv_w": jnp.concatenate(gdn_cw, axis=1), "ssm_conv_w": jnp.concatenate(ssm_cw, axis=1),
        "gdn_alog_row": jnp.pad(gdn_a_log, (GDN_H, 128 - 2 * GDN_H)).reshape(1, 128),
        "gdn_dtb_row": jnp.pad(gdn_dt_bias, (GDN_H, 128 - 2 * GDN_H)).reshape(1, 128),
        "gdn_norm_x": jnp.tile(gdn_norm_w, GDN_H).reshape(1, D),
        "ssm_dtb_row": jnp.pad(ssm_dt_bias, (0, 128 - SSM_H)).reshape(1, 128),
        "ssm_alog_row": jnp.pad(ssm_a_log, (0, 128 - SSM_H)).reshape(1, 128),
        "ssm_d_x": jnp.repeat(ssm_d, SSM_P).reshape(1, D),
    }

    loss_part, grad_x, G = _local_step(x[0], mem[0], loss_target[0], W, shards)

    grads = {n: G[n] for n in _BIG}

    small = {
        "norm1_w": G["norm1_w"], "gdn_conv_w": G["gdn_conv_w"], "gdn_a_log": G["gdn_alog_row"][0, GDN_H:2 * GDN_H],
        "gdn_dt_bias": G["gdn_dtb_row"][0, GDN_H:2 * GDN_H], "gdn_norm_w": G["gdn_norm_x"].reshape(GDN_H, 128).sum(0),
        "ssm_conv_w": G["ssm_conv_w"], "ssm_conv_b": G["ssm_conv_b"],
        "ssm_a_log": G["ssm_alog_row"][0, :SSM_H], "ssm_dt_bias": G["ssm_dtb_row"][0, :SSM_H],
        "ssm_d": G["ssm_d_x"].reshape(SSM_H, SSM_P).sum(1), "ssm_norm_w": G["ssm_norm_w"].reshape(D),
        "norm2_w": G["norm2_w"], "mem_norm_w": G["mem_norm_w"], "norm3_w": G["norm3_w"],
        "final_norm_w": G["final_norm_w"], "loss": loss_part[0, :1],
    }
    names = list(small)
    pack, offs = _pack_rows([small[n].reshape(-1) for n in names])
    tot = _sum8(_all_gather([pack], F32, name="gather_small")[0], name="sum_small")
    summed = dict(zip(names, _unpack_rows(tot, offs, [small[n].shape for n in names])))
    loss = summed.pop("loss")[0]
    for n in ("gdn_conv_w", "ssm_conv_w"):
        width = w_loc[n].shape[1]
        summed[n] = lax.dynamic_slice_in_dim(summed[n], me * width, width, axis=1)
    grads.update(summed)

    upd = {n: _adamw(w_loc[n], grads[n], args["m_" + n], args["v_" + n], name="adamw_" + n) for n in _WEIGHTS}
    return (loss, grad_x[None], *[upd[n][0] for n in _WEIGHTS], *[upd[n][1] for n in _WEIGHTS],
            *[upd[n][2] for n in _WEIGHTS], *[upd[n][3] for n in _WEIGHTS])
```

```python
import jax
import jax.numpy as jnp
from jax import lax
from jax.experimental import pallas as pl
from jax.experimental.pallas import tpu as pltpu

F32 = jnp.float32
BF16 = jnp.bfloat16
_MXU = BF16

D = 1024
EPS = 1e-6
CONV_K = 4
GDN_H, GDN_DK, GDN_C = 8, 128, 64
GDN_SCAN_CHUNKS = 8
GDN_LOCAL_CHUNKS = 4
GDN_REST_CHUNKS = 4
SSM_H, SSM_P, SSM_L, SSM_N = 16, 64, 128, 128
SSM_SCAN_CHUNKS = 4
MEM_H, MEM_HD = 4, 256
D_FF = 4096
N_DEV = 8

C_QKV, C_ZG, C_ZS, C_XBC, C_GATE, C_DT, C_TOT = 0, 3072, 4096, 5120, 6656, 6784, 6912
P_HALO = 16

ADAM_LR, ADAM_B1, ADAM_B2, ADAM_EPS, ADAM_WD, ADAM_STEP = 0.001, 0.9, 0.999, 1e-08, 0.01, 10

VMEM_LIMIT = 56 * 1024 * 1024

_NN = (((1,), (0,)), ((), ()))
_NT = (((1,), (1,)), ((), ()))
_TN = (((0,), (0,)), ((), ()))


def _dot(a, b, dims=_NN):
    return lax.dot_general(a.astype(_MXU), b.astype(_MXU), dims, preferred_element_type=F32)


def _split3(a):
    a1 = a.astype(BF16)
    r1 = a - a1.astype(F32)
    a2 = r1.astype(BF16)
    return a1, a2, (r1 - a2.astype(F32)).astype(BF16)


def _dot_sel(a, e):
    eb = e.astype(BF16)
    return sum(lax.dot_general(p, eb, _NN, preferred_element_type=F32) for p in _split3(a))


def _sel_dot(e, a):
    eb = e.astype(BF16)
    return sum(lax.dot_general(eb, p, _NN, preferred_element_type=F32) for p in _split3(a))


def _chunk_cumsum(a, tri, chunk):
    return jnp.concatenate([_sel_dot(tri, a[r:r + chunk]) for r in range(0, a.shape[0], chunk)], axis=0)


def _params(sem):
    return pltpu.CompilerParams(dimension_semantics=sem, vmem_limit_bytes=VMEM_LIMIT)


def _pick(n, cap):
    for d in range(min(cap, n), 0, -128):
        if n % d == 0 and d % 128 == 0:
            return d
    return n


def _sigmoid(x):
    return 0.5 * jnp.tanh(0.5 * x) + 0.5


def _silu(x):
    return x * _sigmoid(x)


def _dsilu(x):
    s = _sigmoid(x)
    return s * (1.0 + x * (1.0 - s))


def _softplus(x):
    return jnp.maximum(x, 0.0) + jnp.log(1.0 + jnp.exp(-jnp.abs(x)))


def _iota2(shape, axis):
    return lax.broadcasted_iota(jnp.int32, shape, axis)


def _sum_all(x):
    return jnp.sum(jnp.sum(x, axis=1, keepdims=True), axis=0, keepdims=True)


_MESH = pl.DeviceIdType.MESH
_ANY = pl.BlockSpec(memory_space=pl.ANY)
_VM = pl.BlockSpec(memory_space=pltpu.VMEM)
_REL = [(r >> 2 & 1, r >> 1 & 1, r & 1) for r in range(1, N_DEV)]


def _place():
    return lax.axis_index("x"), lax.axis_index("y"), lax.axis_index("c")


class _Ride:
    def __init__(self, srcs, shard):
        self.srcs, self.shard, self.n = list(srcs), shard, len(srcs)
        self.out_shape = [jax.ShapeDtypeStruct(((N_DEV,) + s.shape) if shard else s.shape, s.dtype)
                          for s in self.srcs]
        self.specs = [_ANY] * self.n
        self.scratch = [pltpu.SemaphoreType.DMA((self.n, N_DEV - 1)), pltpu.SemaphoreType.DMA((self.n, N_DEV - 1)),
                        pltpu.SemaphoreType.DMA((self.n,))]

    def _copies(self, in_refs, out_refs, sems):
        send, recv, loc = sems
        x, y, c = _place()
        me = 4 * x + 2 * y + c
        local, remote, arrive = [], [], []
        for a in range(self.n):
            src = in_refs[a] if self.shard else in_refs[a].at[me]
            local.append(pltpu.make_async_copy(src, out_refs[a].at[me], loc.at[a]))
        for k, (rx, ry, rc) in enumerate(_REL):
            peer = (lax.rem(x + rx, 2), lax.rem(y + ry, 2), lax.rem(c + rc, 2))
            ps = 4 * peer[0] + 2 * peer[1] + peer[2]
            for a in range(self.n):
                src = in_refs[a] if self.shard else in_refs[a].at[ps]
                remote.append(pltpu.make_async_remote_copy(
                    src_ref=src, dst_ref=out_refs[a].at[me], send_sem=send.at[a, k], recv_sem=recv.at[a, k],
                    device_id=peer, device_id_type=_MESH))
                slot = out_refs[a].at[ps]
                arrive.append(pltpu.make_async_remote_copy(
                    src_ref=slot, dst_ref=slot, send_sem=send.at[a, k], recv_sem=recv.at[a, k],
                    device_id=peer, device_id_type=_MESH))
        return local, remote, arrive

    def start(self, in_refs, out_refs, sems):
        local, remote, _ = self._copies(in_refs, out_refs, sems)
        for cp in local + remote:
            cp.start()

    def wait(self, in_refs, out_refs, sems):
        local, remote, arrive = self._copies(in_refs, out_refs, sems)
        for cp in arrive:
            cp.wait_recv()
        for cp in remote:
            cp.wait_send()
        for cp in local:
            cp.wait()


_EPI = {
    "none": ((), ("tile",)),
    "res": (("tile",), ("tile",)),
    "mul2": (("tile",), ("tile",)),
    "relu2": ((), ("tile", "tile")),
    "res_norm": (("tile", "row"), ("tile", "tile")),
    "norm_bwd": (("tile", "tile", "row"), ("tile", "row")),
    "res_loss": (("tile", "tile", "row"), ("tile", "row", "row")),
}


def _mm(a, b, *, dims="nn", epi="none", extra=(), out_dtype=F32, name, bm=1024, bn_cap=1024, bk_cap=2048,
        ride=None, b_cols=None, b_resident=False):
    if dims == "nn":
        (M, K), (K2, N) = a.shape, b.shape
    elif dims == "nt":
        (M, K), (N, K2) = a.shape, b.shape
    else:
        (K, M), (K2, N) = a.shape, b.shape
    jb0 = 0
    if b_cols is not None:
        N = b_cols[1]
    assert K == K2, (a.shape, b.shape, dims)
    bm = _pick(M, bm)
    bn = _pick(N, bn_cap)
    bk = _pick(K, bk_cap)
    nk = K // bk
    if b_cols is not None:
        assert dims == "nn" and b_cols[0] % bn == 0
        jb0 = b_cols[0] // bn
    dn = {"nn": _NN, "nt": _NT, "tn": _TN}[dims]
    a_spec = (pl.BlockSpec((bk, bm), lambda i, j, k: (k, i)) if dims == "tn"
              else pl.BlockSpec((bm, bk), lambda i, j, k: (i, k)))
    if b_resident:
        b_spec = pl.BlockSpec(b.shape, lambda i, j, k: (0, 0), pipeline_mode=pl.Buffered(1))
    else:
        b_spec = (pl.BlockSpec((bn, bk), lambda i, j, k: (j, k)) if dims == "nt"
                  else pl.BlockSpec((bk, bn), lambda i, j, k: (k, j + jb0)))
    o_spec = pl.BlockSpec((bm, bn), lambda i, j, k: (i, j))
    r_spec = pl.BlockSpec((1, bn), lambda i, j, k: (0, j))
    extra = list(extra) if isinstance(extra, (tuple, list)) else [extra]
    ekinds, okinds = _EPI[epi]
    assert len(extra) == len(ekinds) and (epi not in ("res_norm", "norm_bwd", "res_loss") or bn == N)
    n_extra, n_out = len(ekinds), len(okinds)
    n_ride = ride.n if ride else 0
    gi, gj = M // bm, N // bn

    def body(a_ref, b_ref, *rest):
        ex = rest[:n_extra]
        first = pl.program_id(0) == 0
        ride_in = rest[n_extra:n_extra + n_ride]
        outs = rest[n_extra + n_ride:n_extra + n_ride + n_out]
        ride_out = rest[n_extra + n_ride + n_out:n_extra + 2 * n_ride + n_out]
        if ride:
            at = lambda i, j, k: ((pl.program_id(0) == i) & (pl.program_id(1) == j) & (pl.program_id(2) == k))

            @pl.when(at(0, 0, 0))
            def _():
                ride.start(ride_in, ride_out, rest[-3:])

        def finish(r):
            if epi == "res":
                outs[0][...] = (r + ex[0][...].astype(F32)).astype(outs[0].dtype)
            elif epi == "mul2":
                outs[0][...] = (2.0 * r * ex[0][...].astype(F32)).astype(outs[0].dtype)
            elif epi == "relu2":
                u = jnp.maximum(r, 0.0)
                outs[0][...] = u.astype(outs[0].dtype)
                outs[1][...] = (u * u).astype(outs[1].dtype)
            elif epi == "res_norm":
                y = r + ex[0][...]
                outs[0][...] = y
                rstd = lax.rsqrt(jnp.mean(y * y, axis=1, keepdims=True) + EPS)
                outs[1][...] = (y * rstd * ex[1][...]).astype(outs[1].dtype)
            elif epi == "norm_bwd":
                xv = ex[0][...]
                rstd = lax.rsqrt(jnp.mean(xv * xv, axis=1, keepdims=True) + EPS)
                xh = xv * rstd
                dxh = r * ex[2][...]
                outs[0][...] = ex[1][...] + rstd * (dxh - xh * jnp.mean(dxh * xh, axis=1, keepdims=True))
                dw = jnp.sum(r * xh, axis=0, keepdims=True)

                @pl.when(first)
                def _():
                    outs[1][...] = dw

                @pl.when(jnp.logical_not(first))
                def _():
                    outs[1][...] += dw
            elif epi == "res_loss":
                y = r + ex[0][...]
                wv = ex[2][...]
                rstd = lax.rsqrt(jnp.mean(y * y, axis=1, keepdims=True) + EPS)
                yh = y * rstd
                err = yh * wv - ex[1][...]
                part_loss = 0.5 * jnp.sum(jnp.mean(err * err, axis=1, keepdims=True), axis=0, keepdims=True)
                dyn = err * (1.0 / N)
                dyh = dyn * wv
                outs[0][...] = rstd * (dyh - yh * jnp.mean(dyh * yh, axis=1, keepdims=True))
                dw = jnp.sum(dyn * yh, axis=0, keepdims=True)
                lrow = jnp.broadcast_to(part_loss, (1, N))

                @pl.when(first)
                def _():
                    outs[1][...] = dw
                    outs[2][...] = lrow

                @pl.when(jnp.logical_not(first))
                def _():
                    outs[1][...] += dw
                    outs[2][...] += lrow
            else:
                outs[0][...] = r.astype(outs[0].dtype)

        if b_resident:
            jo = pl.multiple_of((pl.program_id(1) + jb0) * bn, bn)
            ko = pl.multiple_of(pl.program_id(2) * bk, bk)
            b_blk = b_ref[pl.ds(jo, bn), pl.ds(ko, bk)] if dims == "nt" else b_ref[pl.ds(ko, bk), pl.ds(jo, bn)]
        else:
            b_blk = b_ref[...]
        part = _dot(a_ref[...], b_blk, dn)
        if nk == 1:
            finish(part)
        else:
            acc = rest[n_extra + 2 * n_ride + n_out]
            k = pl.program_id(2)

            @pl.when(k == 0)
            def _():
                acc[...] = part

            @pl.when((k > 0) & (k < nk - 1))
            def _():
                acc[...] += part

            @pl.when(k == nk - 1)
            def _():
                finish(acc[...] + part)

        if ride:
            @pl.when(at(gi - 1, gj - 1, nk - 1))
            def _():
                ride.wait(ride_in, ride_out, rest[-3:])

    kind_spec = {"tile": o_spec, "row": r_spec}
    ins = [a, b] + [e.reshape(1, N) if k == "row" else e for e, k in zip(extra, ekinds)]
    in_specs = [a_spec, b_spec] + [kind_spec[k] for k in ekinds]
    out_dtypes = {"res_norm": (F32, BF16), "norm_bwd": (F32, F32), "res_loss": (F32, F32, F32)}.get(
        epi, (out_dtype,) * n_out)
    out_shape = [jax.ShapeDtypeStruct((M, N) if k == "tile" else (1, N), dt) for k, dt in zip(okinds, out_dtypes)]
    out_specs = [kind_spec[k] for k in okinds]
    scratch = [pltpu.VMEM((bm, bn), F32)] if nk > 1 else []
    sem = ("arbitrary" if epi in ("norm_bwd", "res_loss") else "parallel", "parallel", "arbitrary")
    if ride:
        ins, in_specs = ins + ride.srcs, in_specs + ride.specs
        out_shape, out_specs = out_shape + ride.out_shape, out_specs + ride.specs
        scratch, sem = scratch + ride.scratch, ("arbitrary",) * 3
    res = pl.pallas_call(
        body, grid=(gi, gj, nk), in_specs=in_specs, out_specs=out_specs, out_shape=out_shape,
        scratch_shapes=scratch, name=name, compiler_params=_params(sem))(*ins)
    main = res[:n_out] if n_out > 1 else res[0]
    return (main, list(res[n_out:])) if ride else main


def _rmsnorm_fwd(x, w, *, name, bt=1024):
    T, Dm = x.shape
    bt = min(bt, T)

    def body(x_ref, w_ref, h_ref):
        xv = x_ref[...]
        r = lax.rsqrt(jnp.mean(xv * xv, axis=1, keepdims=True) + EPS)
        h_ref[...] = (xv * r * w_ref[...]).astype(h_ref.dtype)

    return pl.pallas_call(
        body, grid=(T // bt,),
        in_specs=[pl.BlockSpec((bt, Dm), lambda i: (i, 0)), pl.BlockSpec((1, Dm), lambda i: (0, 0))],
        out_specs=pl.BlockSpec((bt, Dm), lambda i: (i, 0)),
        out_shape=jax.ShapeDtypeStruct((T, Dm), BF16), name=name,
        compiler_params=_params(("parallel",)))(x, w.reshape(1, Dm))


def _rmsnorm_bwd(x, w, dh, dres, *, name, bt=256):
    T, Dm = x.shape
    bt = min(bt, T)
    has_res = dres is not None

    def body(x_ref, w_ref, dh_ref, *rest):
        dres_ref = rest[0] if has_res else None
        dx_ref, dw_ref = rest[-2], rest[-1]
        i = pl.program_id(0)
        xv = x_ref[...]
        r = lax.rsqrt(jnp.mean(xv * xv, axis=1, keepdims=True) + EPS)
        xh = xv * r
        dhv = dh_ref[...].astype(F32)
        dxh = dhv * w_ref[...]
        dx = r * (dxh - xh * jnp.mean(dxh * xh, axis=1, keepdims=True))
        if has_res:
            dx = dx + dres_ref[...]
        dx_ref[...] = dx

        @pl.when(i == 0)
        def _():
            dw_ref[...] = jnp.zeros_like(dw_ref)

        dw_ref[...] += jnp.sum(dhv * xh, axis=0, keepdims=True)

    row = pl.BlockSpec((bt, Dm), lambda i: (i, 0))
    vec = pl.BlockSpec((1, Dm), lambda i: (0, 0))
    ins = [x, w.reshape(1, Dm), dh] + ([dres] if has_res else [])
    dx, dw = pl.pallas_call(
        body, grid=(T // bt,), in_specs=[row, vec, row] + ([row] if has_res else []),
        out_specs=[row, vec],
        out_shape=[jax.ShapeDtypeStruct((T, Dm), F32), jax.ShapeDtypeStruct((1, Dm), F32)],
        name=name, compiler_params=_params(("arbitrary",)))(*ins)
    return dx, dw.reshape(Dm)


def _attn_fwd(q, km, vm, *, bt=1024):
    T = q.shape[0]
    M = km.shape[0]
    bt = min(bt, T)
    scale = MEM_HD ** -0.5

    def body(q_ref, k_ref, v_ref, o_ref):
        sls = [slice(h * MEM_HD, (h + 1) * MEM_HD) for h in range(MEM_H)]
        ss = [_dot(q_ref[:, sl], k_ref[:, sl], _NT) * scale for sl in sls]
        es = [jnp.exp(s - jnp.max(s, axis=1, keepdims=True)) for s in ss]
        ps = [e / jnp.sum(e, axis=1, keepdims=True) for e in es]
        for sl, p in zip(sls, ps):
            o_ref[:, sl] = _dot(p, v_ref[:, sl]).astype(o_ref.dtype)

    row = pl.BlockSpec((bt, D), lambda i: (i, 0))
    mem = pl.BlockSpec((M, D), lambda i: (0, 0))
    return pl.pallas_call(
        body, grid=(T // bt,), in_specs=[row, mem, mem], out_specs=row,
        out_shape=jax.ShapeDtypeStruct((T, D), BF16), name="attn_fwd",
        compiler_params=_params(("parallel",)))(q, km, vm)


def _attn_bwd(q, km, vm, do, *, bt=1024):
    T = q.shape[0]
    M = km.shape[0]
    bt = min(bt, T)
    scale = MEM_HD ** -0.5

    def body(q_ref, k_ref, v_ref, do_ref, dq_ref, dk_ref, dv_ref):
        i = pl.program_id(0)

        @pl.when(i == 0)
        def _():
            dk_ref[...] = jnp.zeros_like(dk_ref)
            dv_ref[...] = jnp.zeros_like(dv_ref)

        sls = [slice(h * MEM_HD, (h + 1) * MEM_HD) for h in range(MEM_H)]
        ss = [_dot(q_ref[:, sl], k_ref[:, sl], _NT) * scale for sl in sls]
        dps = [_dot(do_ref[:, sl], v_ref[:, sl], _NT) for sl in sls]
        es = [jnp.exp(s - jnp.max(s, axis=1, keepdims=True)) for s in ss]
        ps = [e / jnp.sum(e, axis=1, keepdims=True) for e in es]
        dss = [p * (dp - jnp.sum(dp * p, axis=1, keepdims=True)) * scale for p, dp in zip(ps, dps)]
        for sl, p, ds in zip(sls, ps, dss):
            dq_ref[:, sl] = _dot(ds, k_ref[:, sl]).astype(dq_ref.dtype)
            dk_ref[:, sl] += _dot(ds, q_ref[:, sl], _TN)
            dv_ref[:, sl] += _dot(p, do_ref[:, sl], _TN)

    row = pl.BlockSpec((bt, D), lambda i: (i, 0))
    mem = pl.BlockSpec((M, D), lambda i: (0, 0))
    return pl.pallas_call(
        body, grid=(T // bt,), in_specs=[row, mem, mem, row], out_specs=[row, mem, mem],
        out_shape=[jax.ShapeDtypeStruct((T, D), BF16), jax.ShapeDtypeStruct((M, D), F32),
                   jax.ShapeDtypeStruct((M, D), F32)],
        name="attn_bwd", compiler_params=_params(("arbitrary",)))(q, km, vm, do)


def _conv_apply(halo, x, w_ref, b_ref):
    bt, hr = x.shape[0], halo.shape[0]
    cat = jnp.concatenate([halo, x], axis=0)
    y = x * w_ref[3:4, :]
    for k in range(CONV_K - 1):
        y = y + pltpu.roll(cat, CONV_K - 1 - k, 0)[hr:hr + bt] * w_ref[k:k + 1, :]
    if b_ref is not None:
        y = y + b_ref[...]
    return y


def _l2_parts(act, bc):
    out = []
    for s in range(bc // 128):
        a = act[:, s * 128:(s + 1) * 128]
        r = lax.rsqrt(jnp.sum(a * a, axis=1, keepdims=True) + EPS)
        out.append((a, r))
    return out


def _conv_fwd(p, col0, C, w, b, *, l2, name, bt=512, bc=1024):
    T = p.shape[0]
    bt = min(bt, T)
    c0, hb = col0 // bc, bt // P_HALO
    has_b = b is not None
    assert not l2 or (bc == D and C == 2 * D)

    def body(x_ref, halo_ref, w_ref, *rest):
        b_ref = rest[0] if has_b else None
        o_ref = rest[-1]
        i, j = pl.program_id(0), pl.program_id(1)
        x = x_ref[...].astype(F32)
        halo = jnp.where(i > 0, halo_ref[...].astype(F32), 0.0)
        act = _silu(_conv_apply(halo, x, w_ref, b_ref))
        if l2:
            sc = jnp.where(j == 0, GDN_DK ** -0.5, 1.0)
            o_ref[...] = jnp.concatenate([a * (r * sc) for a, r in _l2_parts(act, bc)], axis=1)
        else:
            o_ref[...] = act

    in_specs = [pl.BlockSpec((bt, bc), lambda i, j: (i, c0 + j)),
                pl.BlockSpec((P_HALO, bc), lambda i, j: (jnp.maximum(i * hb - 1, 0), c0 + j)),
                pl.BlockSpec((CONV_K, bc), lambda i, j: (0, j))]
    ins = [p, p, w]
    if has_b:
        in_specs.append(pl.BlockSpec((1, bc), lambda i, j: (0, j)))
        ins.append(b.reshape(1, C))
    return pl.pallas_call(
        body, grid=(T // bt, C // bc), in_specs=in_specs,
        out_specs=pl.BlockSpec((bt, bc), lambda i, j: (i, j)),
        out_shape=jax.ShapeDtypeStruct((T, C), F32), name=name,
        compiler_params=_params(("parallel", "parallel")))(*ins)


def _conv_bwd_act(p, col0, C, w, b, dact, dcol0, *, l2, name, bt=512, bc=1024):
    T = p.shape[0]
    bt = min(bt, T)
    c0, d0, hb = col0 // bc, dcol0 // bc, bt // P_HALO
    has_b = b is not None
    assert not l2 or (bc == D and C == 2 * D)

    def body(x_ref, halo_ref, w_ref, *rest):
        b_ref = rest[0] if has_b else None
        dact_ref, dy_ref, dw_ref, db_ref = rest[-4:]
        j, i = pl.program_id(0), pl.program_id(1)
        x = x_ref[...].astype(F32)
        halo = jnp.where(i > 0, halo_ref[...].astype(F32), 0.0)
        y = _conv_apply(halo, x, w_ref, b_ref)
        dact = dact_ref[...]
        sg = _sigmoid(y)
        if l2:
            sc = jnp.where(j == 0, GDN_DK ** -0.5, 1.0)
            parts = []
            for s, (a, r) in enumerate(_l2_parts(y * sg, bc)):
                n = a * r
                dn = dact[:, s * 128:(s + 1) * 128]
                parts.append((r * sc) * (dn - n * jnp.sum(dn * n, axis=1, keepdims=True)))
            dact = jnp.concatenate(parts, axis=1)
        dy = dact * (sg * (1.0 + y * (1.0 - sg)))
        dy_ref[...] = dy

        @pl.when(i == 0)
        def _():
            dw_ref[...] = jnp.zeros_like(dw_ref)
            db_ref[...] = jnp.zeros_like(db_ref)

        db_ref[...] += jnp.sum(dy, axis=0, keepdims=True)
        cat = jnp.concatenate([halo, x], axis=0)
        dw_ref[3:4, :] += jnp.sum(dy * x, axis=0, keepdims=True)
        for k in range(CONV_K - 1):
            xs = pltpu.roll(cat, CONV_K - 1 - k, 0)[P_HALO:P_HALO + bt]
            dw_ref[k:k + 1, :] += jnp.sum(dy * xs, axis=0, keepdims=True)

    in_specs = [pl.BlockSpec((bt, bc), lambda j, i: (i, c0 + j)),
                pl.BlockSpec((P_HALO, bc), lambda j, i: (jnp.maximum(i * hb - 1, 0), c0 + j)),
                pl.BlockSpec((CONV_K, bc), lambda j, i: (0, j))]
    ins = [p, p, w]
    if has_b:
        in_specs.append(pl.BlockSpec((1, bc), lambda j, i: (0, j)))
        ins.append(b.reshape(1, C))
    in_specs.append(pl.BlockSpec((bt, bc), lambda j, i: (i, d0 + j)))
    ins.append(dact)
    dy, dw, db = pl.pallas_call(
        body, grid=(C // bc, T // bt), in_specs=in_specs,
        out_specs=[pl.BlockSpec((bt, bc), lambda j, i: (i, j)),
                   pl.BlockSpec((CONV_K, bc), lambda j, i: (0, j)),
                   pl.BlockSpec((1, bc), lambda j, i: (0, j))],
        out_shape=[jax.ShapeDtypeStruct((T, C), F32), jax.ShapeDtypeStruct((CONV_K, C), F32),
                   jax.ShapeDtypeStruct((1, C), F32)],
        name=name, compiler_params=_params(("parallel", "arbitrary")))(*ins)
    return dy, dw, db.reshape(C)


def _conv_bwd_in(dy, w, dp_in, col0, T, *, name, bt=1024, bc=1024):
    C = dy.shape[1]
    bt = min(bt, T)
    c0, hb, nb = col0 // bc, bt // 8, T // bt

    def body(dy_ref, nxt_ref, w_ref, *rest):
        o_ref = rest[-1]
        i = pl.program_id(0)
        dy_v = dy_ref[...]
        nxt = jnp.where(i < nb - 1, nxt_ref[...], 0.0)
        cat = jnp.concatenate([dy_v, nxt], axis=0)
        dx = dy_v * w_ref[3:4, :]
        for k in range(CONV_K - 1):
            s = CONV_K - 1 - k
            dx = dx + pltpu.roll(cat, bt + 8 - s, 0)[0:bt] * w_ref[k:k + 1, :]
        o_ref[...] = dx.astype(o_ref.dtype)

    in_specs = [pl.BlockSpec((bt, bc), lambda i, j: (i, j)),
                pl.BlockSpec((8, bc), lambda i, j: (jnp.minimum((i + 1) * hb, T // 8 - 1), j)),
                pl.BlockSpec((CONV_K, bc), lambda i, j: (0, j))]
    ins = [dy, dy, w]
    alias = {}
    if dp_in is not None:
        in_specs.append(pl.BlockSpec(memory_space=pl.ANY))
        ins.append(dp_in)
        alias = {3: 0}
    return pl.pallas_call(
        body, grid=(nb, C // bc), in_specs=in_specs,
        out_specs=pl.BlockSpec((bt, bc), lambda i, j: (i, c0 + j)),
        out_shape=jax.ShapeDtypeStruct((T, C_TOT), BF16), input_output_aliases=alias, name=name,
        compiler_params=_params(("parallel", "parallel")))(*ins)


def _expand_mats(shift, row0):
    e = (_iota2((128, D), 0) - row0 == (_iota2((128, D), 1) >> shift)).astype(F32)
    et = ((_iota2((D, 128), 0) >> shift) == _iota2((D, 128), 1) - row0).astype(F32)
    return e, et


def _cum_mats(chunk):
    ri, ci = _iota2((chunk, chunk), 0), _iota2((chunk, chunk), 1)
    return (ri >= ci).astype(F32), (ri <= ci).astype(F32)


def _gdn_gates_fwd(p, alog_row, dtb_row, *, bt=1024):
    T = p.shape[0]
    bt = min(bt, T)

    def body(g_ref, al_ref, db_ref, bg_ref):
        gt = g_ref[...]
        lc, _ = _cum_mats(GDN_C)
        g_l = -jnp.exp(al_ref[...]) * _softplus(gt + db_ref[...])
        bg_ref[...] = jnp.where(_iota2((bt, 128), 1) < GDN_H, _sigmoid(gt), _chunk_cumsum(g_l, lc, GDN_C))

    vec = pl.BlockSpec((1, 128), lambda i: (0, 0))
    seg = pl.BlockSpec((bt, 128), lambda i: (i, 0))
    return pl.pallas_call(
        body, grid=(T // bt,), in_specs=[seg, vec, vec], out_specs=seg,
        out_shape=jax.ShapeDtypeStruct((T, 128), F32), name="gdn_gates_fwd",
        compiler_params=_params(("parallel",)))(p, alog_row, dtb_row)


def _gdn_gates_bwd(p, alog_row, dtb_row, dbg, dp_in, *, bt=1024):
    T = p.shape[0]
    bt = min(bt, T)

    def body(g_ref, al_ref, db_ref, dbg_ref, dpin_ref, dg_out, dal_ref, ddb_ref):
        i = pl.program_id(0)
        gt = g_ref[...]
        lane = _iota2((bt, 128), 1)
        _, uc = _cum_mats(GDN_C)
        ea = jnp.exp(al_ref[...])
        zz = gt + db_ref[...]
        g_l = -ea * _softplus(zz)
        beta_l = _sigmoid(gt)
        dbg_v = dbg_ref[...]
        dg_l = jnp.where((lane >= GDN_H) & (lane < 2 * GDN_H), _chunk_cumsum(dbg_v, uc, GDN_C), 0.0)
        dbeta_l = jnp.where(lane < GDN_H, dbg_v, 0.0)
        da = dg_l * (-ea) * _sigmoid(zz)
        dg_out[...] = (da + dbeta_l * beta_l * (1.0 - beta_l)).astype(dg_out.dtype)

        @pl.when(i == 0)
        def _():
            dal_ref[...] = jnp.zeros_like(dal_ref)
            ddb_ref[...] = jnp.zeros_like(ddb_ref)

        dal_ref[...] += jnp.sum(dg_l * g_l, axis=0, keepdims=True)
        ddb_ref[...] += jnp.sum(da, axis=0, keepdims=True)

    vec = pl.BlockSpec((1, 128), lambda i: (0, 0))
    seg = pl.BlockSpec((bt, 128), lambda i: (i, 0))
    gate = pl.BlockSpec((bt, 128), lambda i: (i, C_GATE // 128))
    return pl.pallas_call(
        body, grid=(T // bt,), in_specs=[seg, vec, vec, seg, _ANY], out_specs=[gate, vec, vec],
        out_shape=[jax.ShapeDtypeStruct((T, C_TOT), BF16), jax.ShapeDtypeStruct((1, 128), F32),
                   jax.ShapeDtypeStruct((1, 128), F32)],
        input_output_aliases={4: 0}, name="gdn_gates_bwd",
        compiler_params=_params(("arbitrary",)))(p, alog_row, dtb_row, dbg, dp_in)


def _ssd_dt_fwd(p, dtb_row, alog_row, *, bt=1024):
    T = p.shape[0]
    bt = min(bt, T)

    def body(d_ref, db_ref, al_ref, da_ref):
        lc, _ = _cum_mats(SSM_L)
        dt_l = _softplus(d_ref[...] + db_ref[...])
        alpha_l = _chunk_cumsum(dt_l * (-jnp.exp(al_ref[...])), lc, SSM_L)
        da_ref[...] = jnp.where(_iota2((bt, 128), 1) < SSM_H, dt_l, pltpu.roll(alpha_l, SSM_H, 1))

    v128 = pl.BlockSpec((1, 128), lambda i: (0, 0))
    return pl.pallas_call(
        body, grid=(T // bt,), in_specs=[pl.BlockSpec((bt, 128), lambda i: (i, 1)), v128, v128],
        out_specs=pl.BlockSpec((bt, 128), lambda i: (i, 0)), out_shape=jax.ShapeDtypeStruct((T, 128), F32),
        name="ssd_dt_fwd", compiler_params=_params(("parallel",)))(p, dtb_row, alog_row)


def _ssd_dt_bwd(p, dtb_row, alog_row, dda, dp_in, *, bt=1024):
    T = p.shape[0]
    bt = min(bt, T)

    def body(d_ref, db_ref, al_ref, dda_ref, dpin_ref, dd_out, ddb_ref, dalog_ref):
        i = pl.program_id(0)
        heads = _iota2((bt, 128), 1) < SSM_H
        _, uc = _cum_mats(SSM_L)
        zz = d_ref[...] + db_ref[...]
        dt_l = _softplus(zz)
        a_row = -jnp.exp(al_ref[...])
        dda_v = dda_ref[...]
        da_l = _chunk_cumsum(jnp.where(heads, pltpu.roll(dda_v, 128 - SSM_H, 1), 0.0), uc, SSM_L)
        draw = jnp.where(heads, (dda_v + da_l * a_row) * _sigmoid(zz), 0.0)
        dd_out[...] = draw.astype(dd_out.dtype)

        @pl.when(i == 0)
        def _():
            ddb_ref[...] = jnp.zeros_like(ddb_ref)
            dalog_ref[...] = jnp.zeros_like(dalog_ref)

        ddb_ref[...] += jnp.sum(draw, axis=0, keepdims=True)
        dalog_ref[...] += jnp.sum(da_l * dt_l, axis=0, keepdims=True) * a_row

    seg = pl.BlockSpec((bt, 128), lambda i: (i, C_DT // 128))
    v128 = pl.BlockSpec((1, 128), lambda i: (0, 0))
    return pl.pallas_call(
        body, grid=(T // bt,),
        in_specs=[pl.BlockSpec((bt, 128), lambda i: (i, 1)), v128, v128, pl.BlockSpec((bt, 128), lambda i: (i, 0)), _ANY],
        out_specs=[seg, v128, v128],
        out_shape=[jax.ShapeDtypeStruct((T, C_TOT), BF16), jax.ShapeDtypeStruct((1, 128), F32),
                   jax.ShapeDtypeStruct((1, 128), F32)],
        input_output_aliases={4: 0}, name="ssd_dt_bwd",
        compiler_params=_params(("arbitrary",)))(p, dtb_row, alog_row, dda, dp_in)


def _gdn_post_fwd(o, p, w_x, *, bt=1024):
    T = o.shape[0]
    bt = min(bt, T)

    def body(o_ref, z_ref, w_ref, out_ref):
        for h in range(GDN_H):
            sl = slice(h * 128, (h + 1) * 128)
            oh = o_ref[:, sl].astype(F32)
            r = lax.rsqrt(jnp.mean(oh * oh, axis=1, keepdims=True) + EPS)
            out_ref[:, sl] = (oh * r * w_ref[:, sl] * _silu(z_ref[:, sl].astype(F32))).astype(out_ref.dtype)

    row = pl.BlockSpec((bt, D), lambda i: (i, 0))
    return pl.pallas_call(
        body, grid=(T // bt,),
        in_specs=[row, pl.BlockSpec((bt, D), lambda i: (i, C_ZG // D)), pl.BlockSpec((1, D), lambda i: (0, 0))],
        out_specs=row, out_shape=jax.ShapeDtypeStruct((T, 2 * D), BF16), name="gdn_post_fwd",
        compiler_params=_params(("parallel",)))(o, p, w_x)


def _gdn_post_bwd(dx1, w_out, o, p, w_x, *, bt=512):
    T = o.shape[0]
    bt = min(bt, T)

    def body(dx_ref, wo_ref, o_ref, z_ref, w_ref, do_ref, dz_ref, dw_ref):
        i = pl.program_id(0)

        @pl.when(i == 0)
        def _():
            dw_ref[...] = jnp.zeros_like(dw_ref)

        dmix = _dot(dx_ref[...], wo_ref[...], _NT)
        for h in range(GDN_H):
            sl = slice(h * 128, (h + 1) * 128)
            oh, zh, wh = o_ref[:, sl].astype(F32), z_ref[:, sl].astype(F32), w_ref[:, sl]
            dm = dmix[:, sl]
            r = lax.rsqrt(jnp.mean(oh * oh, axis=1, keepdims=True) + EPS)
            ohat = oh * r
            dy = dm * _silu(zh)
            dz_ref[:, sl] = (dm * ohat * wh * _dsilu(zh)).astype(dz_ref.dtype)
            dohat = dy * wh
            do_ref[:, sl] = (r * (dohat - ohat * jnp.mean(dohat * ohat, axis=1, keepdims=True))).astype(do_ref.dtype)
            dw_ref[:, sl] += jnp.sum(dy * ohat, axis=0, keepdims=True)

    row = pl.BlockSpec((bt, D), lambda i: (i, 0))
    zcol = pl.BlockSpec((bt, D), lambda i: (i, C_ZG // D))
    vec = pl.BlockSpec((1, D), lambda i: (0, 0))
    return pl.pallas_call(
        body, grid=(T // bt,), in_specs=[row, pl.BlockSpec((D, D), lambda i: (0, 0)), row, zcol, vec],
        out_specs=[row, zcol, vec],
        out_shape=[jax.ShapeDtypeStruct((T, D), BF16), jax.ShapeDtypeStruct((T, C_TOT), BF16),
                   jax.ShapeDtypeStruct((1, D), F32)],
        name="gdn_post_bwd", compiler_params=_params(("arbitrary",)))(dx1, w_out, o, p, w_x)


def _ssd_post_fwd(y, xs, p, d_x, w, mix_in, *, bt=1024):
    T = y.shape[0]
    bt = min(bt, T)

    def body(y_ref, x_ref, z_ref, d_ref, w_ref, mix_ref, out_ref):
        yg = (y_ref[...].astype(F32) + x_ref[...] * d_ref[...]) * _silu(z_ref[...].astype(F32))
        for g in range(2):
            sl = slice(g * 512, (g + 1) * 512)
            a = yg[:, sl]
            r = lax.rsqrt(jnp.mean(a * a, axis=1, keepdims=True) + EPS)
            out_ref[:, sl] = (a * r * w_ref[:, sl]).astype(out_ref.dtype)

    row = pl.BlockSpec((bt, D), lambda i: (i, 0))
    vec = pl.BlockSpec((1, D), lambda i: (0, 0))
    return pl.pallas_call(
        body, grid=(T // bt,),
        in_specs=[row, row, pl.BlockSpec((bt, D), lambda i: (i, C_ZS // D)), vec, vec, _ANY],
        out_specs=pl.BlockSpec((bt, D), lambda i: (i, 1)), out_shape=jax.ShapeDtypeStruct((T, 2 * D), BF16),
        input_output_aliases={5: 0}, name="ssd_post_fwd",
        compiler_params=_params(("parallel",)))(y, xs, p, d_x, w, mix_in)


def _ssd_post_bwd(dx1, w_out, y, xs, p, d_x, w, dp_in, *, bt=512):
    T = y.shape[0]
    bt = min(bt, T)

    def body(dx_ref, wo_ref, y_ref, x_ref, z_ref, d_ref, w_ref, dpin_ref, dyy_ref, dz_ref, dd_ref, dw_ref):
        i = pl.program_id(0)

        @pl.when(i == 0)
        def _():
            dd_ref[...] = jnp.zeros_like(dd_ref)
            dw_ref[...] = jnp.zeros_like(dw_ref)

        dmix = _dot(dx_ref[...], wo_ref[...], _NT)
        xv, zv = x_ref[...], z_ref[...].astype(F32)
        yy = y_ref[...].astype(F32) + xv * d_ref[...]
        sz = _silu(zv)
        yg = yy * sz
        parts = []
        for g in range(2):
            sl = slice(g * 512, (g + 1) * 512)
            a = yg[:, sl]
            r = lax.rsqrt(jnp.mean(a * a, axis=1, keepdims=True) + EPS)
            ah = a * r
            dout = dmix[:, sl]
            dah = dout * w_ref[:, sl]
            dw_ref[:, sl] += jnp.sum(dout * ah, axis=0, keepdims=True)
            parts.append(r * (dah - ah * jnp.mean(dah * ah, axis=1, keepdims=True)))
        dyg = jnp.concatenate(parts, axis=1)
        dyy = dyg * sz
        dyy_ref[...] = dyy
        dz_ref[...] = (dyg * yy * _dsilu(zv)).astype(dz_ref.dtype)
        dd_ref[...] += jnp.sum(dyy * xv, axis=0, keepdims=True)

    row = pl.BlockSpec((bt, D), lambda i: (i, 0))
    zcol = pl.BlockSpec((bt, D), lambda i: (i, C_ZS // D))
    vec = pl.BlockSpec((1, D), lambda i: (0, 0))
    return pl.pallas_call(
        body, grid=(T // bt,),
        in_specs=[row, pl.BlockSpec((D, D), lambda i: (1, 0)), row, row, zcol, vec, vec, _ANY],
        out_specs=[row, zcol, vec, vec],
        out_shape=[jax.ShapeDtypeStruct((T, D), F32), jax.ShapeDtypeStruct((T, C_TOT), BF16),
                   jax.ShapeDtypeStruct((1, D), F32), jax.ShapeDtypeStruct((1, D), F32)],
        input_output_aliases={7: 1}, name="ssd_post_bwd",
        compiler_params=_params(("arbitrary",)))(dx1, w_out, y, xs, p, d_x, w, dp_in)


_NEG = -1e30


def _gdn_terms(q, k, v, bx, gam_c):
    C = GDN_C
    ri, ci = _iota2((C, C), 0), _iota2((C, C), 1)
    eye, low, strict = ri == ci, ri >= ci, ri > ci
    gam_r = jnp.sum(jnp.where(eye, gam_c, 0.0), axis=0, keepdims=True)
    G = jnp.exp(jnp.where(low, gam_c - gam_r, _NEG))
    glast = jnp.sum(jnp.where(_iota2((C, 1), 0) == C - 1, gam_c, 0.0), axis=0, keepdims=True)
    eg, egl, eL = jnp.exp(gam_c), jnp.exp(glast - gam_c), jnp.exp(glast)
    kb, vb = k * bx, v * bx
    M = _dot(kb, k, _NT)
    return dict(eye=eye, low=low, strict=strict, G=G, eg=eg, egl=egl, eL=eL, kb=kb, vb=vb, M=M,
                kbg=kb * eg, qd=q * eg, kd=k * egl, q=q, k=k, v=v, bx=bx)


def _split(a):
    hi = a.astype(_MXU)
    return hi, (a - hi.astype(F32)).astype(_MXU)


def _dot3s(a, b):
    d = lambda p, q: lax.dot_general(p, q, _NN, preferred_element_type=F32)
    return d(a[0], b[0]) + d(a[0], b[1]) + d(a[1], b[0])


def _tri_inv_many(Ls, eye):
    eyef = jnp.where(eye, 1.0, 0.0)
    Ts = [eyef - L for L in Ls]
    Ps = [-L for L in Ls]
    for _ in range(5):
        sp = [_split(p) for p in Ps]
        Ps = [_dot3s(s, s) for s in sp]
        sp = [_split(p) for p in Ps]
        st = [_split(t) for t in Ts]
        Ts = [t + _dot3s(a, b) for t, a, b in zip(Ts, st, sp)]
    return Ts


def _lane_col(tile, idx):
    return jnp.sum(jnp.where(_iota2(tile.shape, 1) == idx, tile, 0.0), axis=1, keepdims=True)


def _gdn_heads(q_ref, k_ref, v_ref, bg_ref, heads):
    out = []
    bg = bg_ref[...]
    for h in heads:
        sl = slice(h * 128, (h + 1) * 128)
        out.append(_gdn_terms(q_ref[:, sl], k_ref[:, sl], v_ref[:, sl], _lane_col(bg, h), _lane_col(bg, GDN_H + h)))
    return out


def _gdn_prep(qk, v, bg, ride=None):
    T = qk.shape[0]
    N = T // GDN_C
    C, CS = GDN_C, GDN_LOCAL_CHUNKS
    NB = N // CS
    n_ride = ride.n if ride else 0

    def body(q_ref, k_ref, v_ref, bg_ref, *rest):
        ride_in = rest[:n_ride]
        u_ref, w_ref, qd_ref, kd_ref, p_ref, t_ref = rest[n_ride:n_ride + 6]
        ride_out = rest[n_ride + 6:2 * n_ride + 6]
        if ride:
            @pl.when(pl.program_id(0) == 0)
            def _():
                ride.start(ride_in, ride_out, rest[-3:])

            @pl.when(pl.program_id(0) == NB - 1)
            def _():
                ride.wait(ride_in, ride_out, rest[-3:])

        items = [(c, h) for c in range(CS) for h in range(GDN_H)]
        views = [[r.at[pl.ds(c * C, C)] for r in (q_ref, k_ref, v_ref, bg_ref)] for c in range(CS)]
        ts = [_gdn_heads(*views[c], [h])[0] for c, h in items]
        Ts = _tri_inv_many([jnp.where(t["strict"], t["M"] * t["G"], 0.0) for t in ts], ts[0]["eye"])
        for (c, h), t, Tm in zip(items, ts, Ts):
            tok = slice(c * C, (c + 1) * C)
            sl = slice(h * 128, (h + 1) * 128)
            rows = slice(h * C, (h + 1) * C)
            u_ref[tok, sl] = _dot(Tm, t["vb"])
            w_ref[tok, sl] = _dot(Tm, t["kbg"]).astype(w_ref.dtype)
            qd_ref[tok, sl] = t["qd"].astype(qd_ref.dtype)
            kd_ref[tok, sl] = t["kd"].astype(kd_ref.dtype)
            p_ref[c, rows, :] = _dot(t["q"], t["k"], _NT) * t["G"]
            t_ref[c, rows, :] = Tm

    blk = lambda c: pl.BlockSpec((CS * C, D), lambda n: (n, c))
    sq = pl.BlockSpec((CS, GDN_H * C, C), lambda n: (n, 0, 0))
    in_specs = [blk(0), blk(1), blk(0), pl.BlockSpec((CS * C, 128), lambda n: (n, 0))]
    out_specs = [blk(0), blk(0), blk(0), blk(0), sq, sq]
    out_shape = [jax.ShapeDtypeStruct((T, D), F32), jax.ShapeDtypeStruct((T, D), BF16),
                 jax.ShapeDtypeStruct((T, D), BF16), jax.ShapeDtypeStruct((T, D), BF16),
                 jax.ShapeDtypeStruct((N, GDN_H * C, C), F32), jax.ShapeDtypeStruct((N, GDN_H * C, C), F32)]
    ins = [qk, qk, v, bg]
    if ride:
        ins, in_specs = ins + ride.srcs, in_specs + ride.specs
        out_shape, out_specs = out_shape + ride.out_shape, out_specs + ride.specs
    res = pl.pallas_call(
        body, grid=(NB,), in_specs=in_specs, out_specs=out_specs, out_shape=out_shape,
        scratch_shapes=ride.scratch if ride else [], name="gdn_prep",
        compiler_params=_params(("arbitrary",) if ride else ("parallel",)))(*ins)
    return (list(res[:6]), list(res[6:])) if ride else list(res)


def _gdn_scan_fwd(u, w, qd, kd, pm, bg):
    T = u.shape[0]
    N = T // GDN_C
    C, CS = GDN_C, GDN_SCAN_CHUNKS

    def body(u_ref, w_ref, qd_ref, kd_ref, p_ref, bg_ref, o_ref, vn_ref, ss_ref, S_scr):
        n = pl.program_id(0)

        @pl.when(n == 0)
        def _():
            S_scr[...] = jnp.zeros_like(S_scr)

        sls = [slice(h * 128, (h + 1) * 128) for h in range(GDN_H)]
        for c in range(CS):
            rows = slice(c * C, (c + 1) * C)
            glast = bg_ref[(c + 1) * C - 1:(c + 1) * C, :]
            Ss = [S_scr[:, sl] for sl in sls]
            vns = [u_ref[rows, sl] - _dot(w_ref[rows, sl], S) for sl, S in zip(sls, Ss)]
            for h, (sl, S, vn) in enumerate(zip(sls, Ss, vns)):
                ss_ref[c, :, sl] = S.astype(ss_ref.dtype)
                vn_ref[rows, sl] = vn.astype(vn_ref.dtype)
                o_ref[rows, sl] = (_dot(qd_ref[rows, sl], S)
                                   + _dot(p_ref[c, h * C:(h + 1) * C, :], vn)).astype(o_ref.dtype)
                S_scr[:, sl] = S * jnp.exp(_lane_col(glast, GDN_H + h)) + _dot(kd_ref[rows, sl], vn, _TN)

    blk = pl.BlockSpec((CS * C, D), lambda n: (n, 0))
    return dict(
        body=body, steps=N // CS, ins=[u, w, qd, kd, pm, bg],
        in_specs=[blk, blk, blk, blk, pl.BlockSpec((CS, GDN_H * C, C), lambda n: (n, 0, 0)),
                  pl.BlockSpec((CS * C, 128), lambda n: (n, 0))],
        out_specs=[blk, blk, pl.BlockSpec((CS, GDN_DK, D), lambda n: (n, 0, 0))],
        out_shape=[jax.ShapeDtypeStruct((T, D), BF16), jax.ShapeDtypeStruct((T, D), BF16),
                   jax.ShapeDtypeStruct((N, GDN_DK, D), BF16)],
        scratch=[pltpu.VMEM((GDN_DK, D), F32)])


def _gdn_scan_bwd(w, qd, kd, pm, bg, do):
    T = w.shape[0]
    N = T // GDN_C
    C, CS = GDN_C, GDN_SCAN_CHUNKS
    NB = N // CS

    def body(w_ref, qd_ref, kd_ref, p_ref, bg_ref, do_ref, dvn_ref, ds_ref, dS_scr):
        n = pl.program_id(0)

        @pl.when(n == 0)
        def _():
            dS_scr[...] = jnp.zeros_like(dS_scr)

        sls = [slice(h * 128, (h + 1) * 128) for h in range(GDN_H)]
        for c in reversed(range(CS)):
            rows = slice(c * C, (c + 1) * C)
            glast = bg_ref[(c + 1) * C - 1:(c + 1) * C, :]
            dSs = [dS_scr[:, sl] for sl in sls]
            dvns = [_dot(p_ref[c, h * C:(h + 1) * C, :], do_ref[rows, sl], _TN) + _dot(kd_ref[rows, sl], dS2)
                    for h, (sl, dS2) in enumerate(zip(sls, dSs))]
            for h, (sl, dS2, dvn) in enumerate(zip(sls, dSs, dvns)):
                ds_ref[c, :, sl] = dS2.astype(ds_ref.dtype)
                dvn_ref[rows, sl] = dvn.astype(dvn_ref.dtype)
                dS_scr[:, sl] = (dS2 * jnp.exp(_lane_col(glast, GDN_H + h))
                                 + _dot(qd_ref[rows, sl], do_ref[rows, sl], _TN) - _dot(w_ref[rows, sl], dvn, _TN))

    blk = pl.BlockSpec((CS * C, D), lambda n: (NB - 1 - n, 0))
    return dict(
        body=body, steps=NB, ins=[w, qd, kd, pm, bg, do],
        in_specs=[blk, blk, blk, pl.BlockSpec((CS, GDN_H * C, C), lambda n: (NB - 1 - n, 0, 0)),
                  pl.BlockSpec((CS * C, 128), lambda n: (NB - 1 - n, 0)), blk],
        out_specs=[blk, pl.BlockSpec((CS, GDN_DK, D), lambda n: (NB - 1 - n, 0, 0))],
        out_shape=[jax.ShapeDtypeStruct((T, D), BF16), jax.ShapeDtypeStruct((N, GDN_DK, D), BF16)],
        scratch=[pltpu.VMEM((GDN_DK, D), F32)])


def _gdn_rest_bwd(qk, v, bg, s_save, t_save, vn, dvn, ds_save, do, ride=None):
    T = qk.shape[0]
    N = T // GDN_C
    C, CS = GDN_C, GDN_REST_CHUNKS
    NB = N // CS
    n_ride = ride.n if ride else 0

    def body(q_ref, k_ref, v_ref, bg_ref, ss_ref, ts_ref, vn_ref, dvn_ref, ds_ref, do_ref, *rest):
        ride_in = rest[:n_ride]
        dqkv_ref, dbg_ref = rest[n_ride:n_ride + 2]
        ride_out = rest[n_ride + 2:2 * n_ride + 2]
        if ride:
            @pl.when(pl.program_id(0) == 0)
            def _():
                ride.start(ride_in, ride_out, rest[-3:])

            @pl.when(pl.program_id(0) == NB - 1)
            def _():
                ride.wait(ride_in, ride_out, rest[-3:])

        items = [(c, h) for c in range(CS) for h in range(GDN_H)]
        toks = [slice(c * C, (c + 1) * C) for c, _ in items]
        sls = [slice(h * 128, (h + 1) * 128) for _, h in items]
        views = [[r.at[pl.ds(c * C, C)] for r in (q_ref, k_ref, v_ref, bg_ref)] for c in range(CS)]
        ts = [_gdn_heads(*views[c], [h])[0] for c, h in items]
        Ss = [ss_ref[c, :, sl] for (c, _), sl in zip(items, sls)]
        Tms = [ts_ref[c, h * C:(h + 1) * C, :] for c, h in items]
        dS2s = [ds_ref[c, :, sl] for (c, _), sl in zip(items, sls)]
        dos = [do_ref[tok, sl] for tok, sl in zip(toks, sls)]
        vns = [vn_ref[tok, sl] for tok, sl in zip(toks, sls)]
        dvns = [dvn_ref[tok, sl] for tok, sl in zip(toks, sls)]
        Qs = [_dot(t["q"], t["k"], _NT) for t in ts]
        dws = [-_dot(dvn, S, _NT) for dvn, S in zip(dvns, Ss)]
        dqds = [_dot(do, S, _NT) for do, S in zip(dos, Ss)]
        dPs = [jnp.where(t["low"], _dot(do, vn, _NT), 0.0) for t, do, vn in zip(ts, dos, vns)]
        dkds = [_dot(vn, dS2, _NT) for vn, dS2 in zip(vns, dS2s)]
        dTs = [_dot(dvn, t["vb"], _NT) + _dot(dw, t["kbg"], _NT) for t, dvn, dw in zip(ts, dvns, dws)]
        dvbs = [_dot(Tm, dvn, _TN) for Tm, dvn in zip(Tms, dvns)]
        dkbgs = [_dot(Tm, dw, _TN) for Tm, dw in zip(Tms, dws)]
        TdTs = [_dot(Tm, dT, _TN) for Tm, dT in zip(Tms, dTs)]
        dLs = [jnp.where(t["strict"], -_dot(TdT, Tm, _NT), 0.0) for t, TdT, Tm in zip(ts, TdTs, Tms)]
        dMs = [dL * t["G"] for t, dL in zip(ts, dLs)]
        dQs = [dP * t["G"] for t, dP in zip(ts, dPs)]
        dkbs = [_dot(dM, t["k"]) + dkbg * t["eg"] for t, dM, dkbg in zip(ts, dMs, dkbgs)]
        rs = lambda a: jnp.sum(a, axis=1, keepdims=True)
        lane = _iota2((C, 128), 1)
        last = _iota2((C, 1), 0) == C - 1
        dbg = [jnp.zeros((C, 128), F32) for _ in range(CS)]
        for i, (c, h) in enumerate(items):
            t, sl, tok = ts[i], sls[i], toks[i]
            E = (dLs[i] * t["M"] + dPs[i] * Qs[i]) * t["G"]
            dqkv_ref[tok, sl] = _dot(dQs[i], t["k"]) + dqds[i] * t["eg"]
            dqkv_ref[tok, D + h * 128:D + (h + 1) * 128] = (
                _dot(dQs[i], t["q"], _TN) + _dot(dMs[i], t["kb"], _TN) + dkds[i] * t["egl"] + dkbs[i] * t["bx"])
            dqkv_ref[tok, 2 * D + h * 128:2 * D + (h + 1) * 128] = dvbs[i] * t["bx"]
            dbeta_c = rs(dkbs[i] * t["k"] + dvbs[i] * t["v"])
            dkd_kd = dkds[i] * t["kd"]
            dgam_c = rs(dqds[i] * t["qd"]) + rs(dkbgs[i] * t["kbg"]) - rs(dkd_kd) + rs(E)
            dgam_r = -jnp.sum(E, axis=0, keepdims=True)
            dgam_c = dgam_c + jnp.sum(jnp.where(t["eye"], dgam_r, 0.0), axis=1, keepdims=True)
            dlast = _sum_all(dkd_kd) + t["eL"] * _sum_all(Ss[i].astype(F32) * dS2s[i].astype(F32))
            dgam_c = dgam_c + jnp.where(last, dlast, 0.0)
            dbg[c] = dbg[c] + jnp.where(lane == h, dbeta_c, 0.0) + jnp.where(lane == GDN_H + h, dgam_c, 0.0)
        for c in range(CS):
            dbg_ref[c * C:(c + 1) * C, :] = dbg[c]

    blk = lambda c: pl.BlockSpec((CS * C, D), lambda n: (n, c))
    st = pl.BlockSpec((CS, GDN_DK, D), lambda n: (n, 0, 0))
    seg = pl.BlockSpec((CS * C, 128), lambda n: (n, 0))
    in_specs = [blk(0), blk(1), blk(0), seg, st,
                pl.BlockSpec((CS, GDN_H * C, C), lambda n: (n, 0, 0)), blk(0), blk(0), st, blk(0)]
    out_specs = [pl.BlockSpec((CS * C, 3 * D), lambda n: (n, 0)), seg]
    out_shape = [jax.ShapeDtypeStruct((T, 3 * D), F32), jax.ShapeDtypeStruct((T, 128), F32)]
    ins = [qk, qk, v, bg, s_save, t_save, vn, dvn, ds_save, do]
    if ride:
        ins, in_specs = ins + ride.srcs, in_specs + ride.specs
        out_shape, out_specs = out_shape + ride.out_shape, out_specs + ride.specs
    res = pl.pallas_call(
        body, grid=(NB,), in_specs=in_specs, out_specs=out_specs, out_shape=out_shape,
        scratch_shapes=ride.scratch if ride else [], name="gdn_rest_bwd",
        compiler_params=_params(("arbitrary",) if ride else ("parallel",)))(*ins)
    return (list(res[:2]), list(res[2:])) if ride else list(res)


def _ssd_seg(al_pair, half, s):
    L = SSM_L
    ri, ci = _iota2((L, L), 0), _iota2((L, L), 1)
    ac = jnp.max(jnp.where(half == s, al_pair, _NEG), axis=1, keepdims=True)
    ar = jnp.sum(jnp.where(ri == ci, ac, 0.0), axis=0, keepdims=True)
    return jnp.exp(jnp.where(ri >= ci, ac - ar, _NEG))


def _last_row(a):
    return jnp.sum(jnp.where(_iota2((a.shape[0], 1), 0) == a.shape[0] - 1, a, 0.0), axis=0, keepdims=True)


def _ssd_expand(da_ref):
    da = da_ref[...]
    return _dot_sel(da, _expand_mats(6, 0)[0]), _dot_sel(da, _expand_mats(6, SSM_H)[0])


def _ssd_core_fwd(xbc, da):
    T = xbc.shape[0]
    L, CS = SSM_L, SSM_SCAN_CHUNKS
    Nc = T // L

    def body(x_all, bc_all, da_all, y_all, hs_all, H_scr):
        @pl.when(pl.program_id(0) == 0)
        def _():
            H_scr[...] = jnp.zeros_like(H_scr)

        for cc in range(CS):
            rows = pl.ds(cc * L, L)
            chunk(x_all.at[rows], bc_all.at[rows], da_all.at[rows], y_all.at[rows], hs_all.at[cc], H_scr)

    def chunk(x_ref, bc_ref, da_ref, y_ref, hs_ref, H_scr):
        dt_ref, al_ref = _ssd_expand(da_ref)
        half = _iota2((L, 128), 1) >> 6
        for g in range(2):
            gs = slice(g * 512, (g + 1) * 512)
            Bg = bc_ref[:, g * 128:(g + 1) * 128]
            Cg = bc_ref[:, 256 + g * 128:256 + (g + 1) * 128]
            alg = al_ref[:, gs]
            alast = _last_row(alg)
            xdt = x_ref[:, gs] * dt_ref[:, gs]
            Hg = H_scr[:, gs]
            hs_ref[:, gs] = Hg
            CB = _dot(Cg, Bg, _NT)
            y_off = jnp.exp(alg) * _dot(Cg, Hg)
            H_scr[:, gs] = Hg * jnp.exp(alast) + _dot(Bg, jnp.exp(alast - alg) * xdt, _TN)
            for j in range(4):
                ps = slice(g * 512 + j * 128, g * 512 + (j + 1) * 128)
                al_pair = al_ref[:, ps]
                xp = x_ref[:, ps] * dt_ref[:, ps]
                ys = [_dot(_ssd_seg(al_pair, half, s) * CB, xp) for s in range(2)]
                y_ref[:, ps] = (y_off[:, j * 128:(j + 1) * 128]
                                + jnp.where(half == 0, ys[0], ys[1])).astype(y_ref.dtype)

    row = pl.BlockSpec((CS * L, D), lambda c: (c, 0))
    return dict(
        body=body, steps=Nc // CS, ins=[xbc, xbc, da],
        in_specs=[row, pl.BlockSpec((CS * L, 512), lambda c: (c, 2)), pl.BlockSpec((CS * L, 128), lambda c: (c, 0))],
        out_specs=[row, pl.BlockSpec((CS, SSM_N, D), lambda c: (c, 0, 0))],
        out_shape=[jax.ShapeDtypeStruct((T, D), BF16), jax.ShapeDtypeStruct((Nc, SSM_N, D), F32)],
        scratch=[pltpu.VMEM((SSM_N, D), F32)])


def _ssd_core_bwd(xbc, da, h_save, dyy, d_x):
    T = xbc.shape[0]
    L, CS = SSM_L, SSM_SCAN_CHUNKS
    Nc = T // L
    NB = Nc // CS

    def body(x_all, bc_all, da_all, hs_all, dy_all, d_ref, dx_all, dda_all, dH_scr, ddt_ref, dal_ref):
        @pl.when(pl.program_id(0) == 0)
        def _():
            dH_scr[...] = jnp.zeros_like(dH_scr)

        for cc in reversed(range(CS)):
            rows = pl.ds(cc * L, L)
            chunk(x_all.at[rows], bc_all.at[rows], da_all.at[rows], hs_all.at[cc], dy_all.at[rows],
                  d_ref, dx_all.at[rows], ddt_ref, dal_ref, dH_scr)
            dda_all[rows, :] = (_dot_sel(ddt_ref[...], _expand_mats(6, 0)[1])
                                + _dot_sel(dal_ref[...], _expand_mats(6, SSM_H)[1]))

    def chunk(x_ref, bc_ref, da_ref, hs_ref, dy_ref, d_ref, dx_ref, ddt_ref, dal_ref, dH_scr):
        dt_ref, al_ref = _ssd_expand(da_ref)
        lane = _iota2((L, 128), 1)
        half = lane >> 6
        rowi = _iota2((L, 1), 0)
        ri, ci = _iota2((L, L), 0), _iota2((L, L), 1)
        for g in range(2):
            gs = slice(g * 512, (g + 1) * 512)
            Bg = bc_ref[:, g * 128:(g + 1) * 128]
            Cg = bc_ref[:, 256 + g * 128:256 + (g + 1) * 128]
            alg = al_ref[:, gs]
            alast = _last_row(alg)
            eal, edec, eL = jnp.exp(alg), jnp.exp(alast - alg), jnp.exp(alast)
            xg, dtg, dYg = x_ref[:, gs], dt_ref[:, gs], dy_ref[:, gs]
            xdt = xg * dtg
            Hg = hs_ref[:, gs]
            dH2 = dH_scr[:, gs]
            CB = _dot(Cg, Bg, _NT)
            dYe = eal * dYg
            dH_scr[:, gs] = dH2 * eL + _dot(Cg, dYe, _TN)
            dC = _dot(dYe, Hg, _NT)
            zg = edec * xdt
            dz = _dot(Bg, dH2)
            dB = _dot(zg, dH2, _NT)
            tz = dz * zg
            dal = dYe * _dot(Cg, Hg) - tz
            dalast = jnp.sum(tz, axis=0, keepdims=True) + eL * jnp.sum(Hg * dH2, axis=0, keepdims=True)
            dal = dal + jnp.where(rowi == L - 1, dalast, 0.0)
            dxdt_g = edec * dz
            dx_ref[:, gs] = dxdt_g * dtg + dYg * d_ref[:, gs]
            ddt_ref[:, gs] = dxdt_g * xg
            dal_ref[:, gs] = dal
            dCB = jnp.zeros((L, L), F32)
            for j in range(4):
                ps = slice(g * 512 + j * 128, g * 512 + (j + 1) * 128)
                al_pair = al_ref[:, ps]
                xp = x_ref[:, ps] * dt_ref[:, ps]
                dYp = dy_ref[:, ps]
                dxp = []
                dal_p = jnp.zeros((L, 128), F32)
                for s in range(2):
                    seg = _ssd_seg(al_pair, half, s)
                    W = seg * CB
                    dW = _dot(jnp.where(half == s, dYp, 0.0), xp, _NT)
                    dxp.append(_dot(W, dYp, _TN))
                    dCB = dCB + dW * seg
                    Es = dW * W
                    dac = jnp.sum(Es, axis=1, keepdims=True) - jnp.sum(
                        jnp.where(ri == ci, jnp.sum(Es, axis=0, keepdims=True), 0.0), axis=1, keepdims=True)
                    dal_p = dal_p + jnp.where(lane == 64 * s, dac, 0.0)
                dxdt_p = jnp.where(half == 0, dxp[0], dxp[1])
                dx_ref[:, ps] += dxdt_p * dt_ref[:, ps]
                ddt_ref[:, ps] += dxdt_p * x_ref[:, ps]
                dal_ref[:, ps] += dal_p
            dx_ref[:, D + g * 128:D + (g + 1) * 128] = dB + _dot(dCB, Cg, _TN)
            dx_ref[:, D + 256 + g * 128:D + 256 + (g + 1) * 128] = dC + _dot(dCB, Bg)

    row = pl.BlockSpec((CS * L, D), lambda c: (NB - 1 - c, 0))
    bcs = pl.BlockSpec((CS * L, 512), lambda c: (NB - 1 - c, 2))
    seg = pl.BlockSpec((CS * L, 128), lambda c: (NB - 1 - c, 0))
    return dict(
        body=body, steps=NB, ins=[xbc, xbc, da, h_save, dyy, d_x],
        in_specs=[row, bcs, seg, pl.BlockSpec((CS, SSM_N, D), lambda c: (NB - 1 - c, 0, 0)), row,
                  pl.BlockSpec((1, D), lambda c: (0, 0))],
        out_specs=[pl.BlockSpec((CS * L, D + 512), lambda c: (NB - 1 - c, 0)), seg],
        out_shape=[jax.ShapeDtypeStruct((T, D + 512), F32), jax.ShapeDtypeStruct((T, 128), F32)],
        scratch=[pltpu.VMEM((SSM_N, D), F32), pltpu.VMEM((L, D), F32), pltpu.VMEM((L, D), F32)])


def _run_scans(parts, *, name):
    steps = parts[0]["steps"]
    assert all(p["steps"] == steps for p in parts)
    cnt = lambda key: [len(p[key]) for p in parts]
    n_in, n_out, n_scr = cnt("ins"), cnt("out_shape"), cnt("scratch")

    def body(*refs):
        ins, outs, scr = refs[:sum(n_in)], refs[sum(n_in):sum(n_in) + sum(n_out)], refs[sum(n_in) + sum(n_out):]
        oi = oo = os_ = 0
        for p, a, b, c in zip(parts, n_in, n_out, n_scr):
            p["body"](*ins[oi:oi + a], *outs[oo:oo + b], *scr[os_:os_ + c])
            oi, oo, os_ = oi + a, oo + b, os_ + c

    cat = lambda key: [v for p in parts for v in p[key]]
    res = pl.pallas_call(
        body, grid=(steps,), in_specs=cat("in_specs"), out_specs=cat("out_specs"), out_shape=cat("out_shape"),
        scratch_shapes=cat("scratch"), name=name, compiler_params=_params(("arbitrary",)))(*cat("ins"))
    out, o = [], 0
    for b in n_out:
        out.append(list(res[o:o + b]))
        o += b
    return out


_EARLY = ("w_out", "wq_mem", "wk_mem", "wv_mem", "wo_mem")
_LATE = ("w_up", "w_down")
_GRADS_MLP = ("w_down", "w_up")
_GRADS_MID = ("wo_mem", "wq_mem", "wk_mem", "wv_mem", "w_out")


def _gather_ride(shards, names):
    return None if shards is None else _Ride([shards[n] for n in names], shard=True)


def _grad_ride(shards, G, names):
    return None if shards is None else _Ride([_slots_from_full(n, G[n]) for n in names], shard=False)


def _local_step(x, mem, tgt, W, shards=None):
    T = x.shape[0]
    W = dict(W)
    cw_qk, cw_v = W["gdn_conv_w"][:, :2 * D], W["gdn_conv_w"][:, 2 * D:]
    h1 = _rmsnorm_fwd(x, W["norm1_w"], name="norm1_fwd")
    ride = _gather_ride(shards, _EARLY)
    pg = _mm(h1, W["w_in_pad"], b_cols=(C_GATE, C_TOT - C_GATE), name="in_proj_gates")
    p = _mm(h1, W["w_in_pad"], b_cols=(0, C_GATE), out_dtype=BF16, bn_cap=1664, name="in_proj", ride=ride)
    if ride:
        p, got = p
        W.update({n: _full_from_slots(n, g) for n, g in zip(_EARLY, got)})
    qk = _conv_fwd(p, C_QKV, 2 * D, cw_qk, None, l2=True, name="gdn_conv_qk_fwd")
    v_g = _conv_fwd(p, C_QKV + 2 * D, D, cw_v, None, l2=False, name="gdn_conv_v_fwd")
    bg = _gdn_gates_fwd(pg, W["gdn_alog_row"], W["gdn_dtb_row"])
    ride = _gather_ride(shards, _LATE)
    prep = _gdn_prep(qk, v_g, bg, ride)
    if ride:
        prep, got = prep
        W.update({n: _full_from_slots(n, g) for n, g in zip(_LATE, got)})
    u_g, w_g, qd_g, kd_g, p_g, t_save = prep
    xbc = _conv_fwd(p, C_XBC, D + 512, W["ssm_conv_w"], W["ssm_conv_b"], l2=False, name="ssm_conv_fwd", bc=512)
    da_s = _ssd_dt_fwd(pg, W["ssm_dtb_row"], W["ssm_alog_row"])
    (o_g, vn_g, s_save), (y_s, h_save) = _run_scans(
        [_gdn_scan_fwd(u_g, w_g, qd_g, kd_g, p_g, bg), _ssd_core_fwd(xbc, da_s)], name="scans_fwd")
    mix = _gdn_post_fwd(o_g, p, W["gdn_norm_x"])
    mix = _ssd_post_fwd(y_s, xbc, p, W["ssm_d_x"], W["ssm_norm_w"].reshape(1, D), mix)
    x1, h2 = _mm(mix, W["w_out"], epi="res_norm", extra=(x, W["norm2_w"]), bm=512, name="out_proj")
    qm = _mm(h2, W["wq_mem"], out_dtype=BF16, name="q_proj")
    m = _rmsnorm_fwd(mem, W["mem_norm_w"], name="mem_norm_fwd")
    km = _mm(m, W["wk_mem"], name="k_proj")
    vm = _mm(m, W["wv_mem"], name="v_proj")
    oa = _attn_fwd(qm, km, vm)
    x2, h3 = _mm(oa, W["wo_mem"], epi="res_norm", extra=(x1, W["norm3_w"]), bm=512, name="o_proj")
    u, act = _mm(h3, W["w_up"], epi="relu2", out_dtype=BF16, bn_cap=2048, b_resident=True, name="mlp_up")
    dx3, g_final, loss = _mm(act, W["w_down"], epi="res_loss", extra=(x2, tgt, W["final_norm_w"]), bk_cap=1024,
                             b_resident=True, name="mlp_down_loss")
    G = {"final_norm_w": g_final.reshape(D)}
    dpre = _mm(dx3, W["w_down"], dims="nt", epi="mul2", extra=u, out_dtype=BF16, b_resident=True, name="mlp_down_dx")
    G["w_down"] = _mm(act, dx3, dims="tn", out_dtype=BF16, name="mlp_down_dw")
    G["w_up"] = _mm(h3, dpre, dims="tn", out_dtype=BF16, bk_cap=4096, name="mlp_up_dw")
    dx2, gw = _mm(dpre, W["w_up"], dims="nt", epi="norm_bwd", extra=(x2, dx3, W["norm3_w"]), bk_cap=1024,
                  b_resident=True, name="mlp_up_dx")
    G["norm3_w"] = gw.reshape(D)
    do_a = _mm(dx2, W["wo_mem"], dims="nt", out_dtype=BF16, name="o_proj_dx")
    G["wo_mem"] = _mm(oa, dx2, dims="tn", out_dtype=BF16, name="o_proj_dw")
    dq, dk, dv = _attn_bwd(qm, km, vm, do_a)
    G["wq_mem"] = _mm(h2, dq, dims="tn", out_dtype=BF16, name="q_proj_dw")
    dx1, gw = _mm(dq, W["wq_mem"], dims="nt", epi="norm_bwd", extra=(x1, dx2, W["norm2_w"]), bm=512,
                  name="q_proj_dx")
    G["norm2_w"] = gw.reshape(D)
    G["wk_mem"] = _mm(m, dk, dims="tn", out_dtype=BF16, name="k_proj_dw")
    G["wv_mem"] = _mm(m, dv, dims="tn", out_dtype=BF16, name="v_proj_dw")
    dm = _mm(dk, W["wk_mem"], dims="nt", name="k_proj_dx")
    dm = _mm(dv, W["wv_mem"], dims="nt", epi="res", extra=dm, name="v_proj_dx")
    _, G["mem_norm_w"] = _rmsnorm_bwd(mem, W["mem_norm_w"], dm, None, name="mem_norm_bwd")
    G["w_out"] = _mm(mix, dx1, dims="tn", out_dtype=BF16, name="out_proj_dw")
    do_g, dp, G["gdn_norm_x"] = _gdn_post_bwd(dx1, W["w_out"], o_g, p, W["gdn_norm_x"])
    dyy, dp, G["ssm_d_x"], G["ssm_norm_w"] = _ssd_post_bwd(dx1, W["w_out"], y_s, xbc, p, W["ssm_d_x"],
                                                          W["ssm_norm_w"].reshape(1, D), dp)
    (dvn_g, ds_save), (dxbc, dda_s) = _run_scans(
        [_gdn_scan_bwd(w_g, qd_g, kd_g, p_g, bg, do_g), _ssd_core_bwd(xbc, da_s, h_save, dyy, W["ssm_d_x"])],
        name="scans_bwd")
    ride = _grad_ride(shards, G, _GRADS_MLP)
    rest = _gdn_rest_bwd(qk, v_g, bg, s_save, t_save, vn_g, dvn_g, ds_save, do_g, ride)
    if ride:
        rest, got = rest
        G.update(zip(_GRADS_MLP, got))
    dqkvn, dbg = rest
    dy_qk, gcw_qk, _ = _conv_bwd_act(p, C_QKV, 2 * D, cw_qk, None, dqkvn, 0, l2=True, name="gdn_conv_qk_bwd_act")
    dy_v, gcw_v, _ = _conv_bwd_act(p, C_QKV + 2 * D, D, cw_v, None, dqkvn, 2 * D, l2=False,
                                   name="gdn_conv_v_bwd_act")
    G["gdn_conv_w"] = jnp.concatenate([gcw_qk, gcw_v], axis=1)
    dp = _conv_bwd_in(dy_qk, cw_qk, dp, C_QKV, T, name="gdn_conv_qk_bwd_in")
    dp = _conv_bwd_in(dy_v, cw_v, dp, C_QKV + 2 * D, T, name="gdn_conv_v_bwd_in")
    dp, G["gdn_alog_row"], G["gdn_dtb_row"] = _gdn_gates_bwd(pg, W["gdn_alog_row"], W["gdn_dtb_row"], dbg, dp)
    dy_s, G["ssm_conv_w"], G["ssm_conv_b"] = _conv_bwd_act(p, C_XBC, D + 512, W["ssm_conv_w"], W["ssm_conv_b"],
                                                           dxbc, 0, l2=False, name="ssm_conv_bwd_act", bc=512)
    dp = _conv_bwd_in(dy_s, W["ssm_conv_w"], dp, C_XBC, T, name="ssm_conv_bwd_in", bc=512)
    dp, G["ssm_dtb_row"], G["ssm_alog_row"] = _ssd_dt_bwd(pg, W["ssm_dtb_row"], W["ssm_alog_row"], dda_s, dp)
    ride = _grad_ride(shards, G, _GRADS_MID)
    g_in = _mm(h1, dp, dims="tn", out_dtype=BF16, bn_cap=1152, bk_cap=4096, name="in_proj_dw", ride=ride)
    if ride:
        g_in, got = g_in
        G.update(zip(_GRADS_MID, got))
    G["w_in"] = _unpad_w_in(g_in)
    ride = _grad_ride(shards, G, ("w_in",))
    res = _mm(dp, W["w_in_pad"], dims="nt", epi="norm_bwd", extra=(x, dx1, W["norm1_w"]), b_resident=True,
              name="in_proj_dx", ride=ride)
    if ride:
        res, got = res
        G["w_in"] = got[0]
    dx, gw = res
    G["norm1_w"] = gw.reshape(D)
    return loss, dx, G


def _all_gather(shards, out_dtype, *, name):
    n = len(shards)

    def body(*refs):
        x_refs, out_refs, stage = refs[:n], refs[n:2 * n], refs[2 * n:3 * n]
        send_sems, recv_sems, local_sems = refs[3 * n:]
        x, y, c = _place()
        me, sibling = (x, y, c), (x, y, 1 - c)
        chips = [(1 - x, y), (x, 1 - y), (1 - x, 1 - y)]

        def slot(px, py, pc):
            return 4 * px + 2 * py + pc

        def copy(a, k, block, to, src=None):
            dst = out_refs[a].at[slot(*block)]
            return pltpu.make_async_remote_copy(
                src_ref=dst if src is None else src, dst_ref=dst, send_sem=send_sems.at[a, k],
                recv_sem=recv_sems.at[a, k], device_id=to, device_id_type=_MESH)

        for a in range(n):
            stage[a][...] = x_refs[a][...].astype(out_dtype)
        mine = [pltpu.make_async_copy(stage[a], out_refs[a].at[slot(*me)], local_sems.at[a]) for a in range(n)]
        for cp in mine:
            cp.start()
        first = []
        for a in range(n):
            first.append(copy(a, 0, me, sibling, src=stage[a]))
            first += [copy(a, 1 + j, me, (*chip, c), src=stage[a]) for j, chip in enumerate(chips)]
        for cp in first:
            cp.start()
        passed = [[copy(a, 4 + j, (*chip, c), sibling) for j, chip in enumerate(chips)] for a in range(n)]
        for j, chip in enumerate(chips):
            for a in range(n):
                copy(a, 1 + j, (*chip, c), me).wait_recv()
                passed[a][j].start()
        for a in range(n):
            copy(a, 0, sibling, me).wait_recv()
            for j, chip in enumerate(chips):
                copy(a, 4 + j, (*chip, 1 - c), me).wait_recv()
        for cp in first + [cp for row in passed for cp in row]:
            cp.wait_send()
        for cp in mine:
            cp.wait()

    outs = pl.pallas_call(
        body, in_specs=[_VM] * n, out_specs=[_ANY] * n,
        out_shape=[jax.ShapeDtypeStruct((N_DEV,) + s.shape, out_dtype) for s in shards],
        scratch_shapes=[pltpu.VMEM(s.shape, out_dtype) for s in shards]
        + [pltpu.SemaphoreType.DMA((n, 7)), pltpu.SemaphoreType.DMA((n, 7)), pltpu.SemaphoreType.DMA((n,))],
        name=name, compiler_params=pltpu.CompilerParams(vmem_limit_bytes=VMEM_LIMIT))(*shards)
    return list(outs)


def _cast_bf16(arrs, *, name):
    n = len(arrs)

    def body(*refs):
        for a in range(n):
            refs[n + a][...] = refs[a][...].astype(BF16)

    return list(pl.pallas_call(
        body, in_specs=[_VM] * n, out_specs=[_VM] * n,
        out_shape=[jax.ShapeDtypeStruct(s.shape, BF16) for s in arrs], name=name,
        compiler_params=pltpu.CompilerParams(vmem_limit_bytes=VMEM_LIMIT))(*arrs))


def _sum8(a, *, name):
    _, R, Cc = a.shape
    br = _pick_rows(R, 128)

    def body(a_ref, o_ref):
        s = a_ref[0].astype(F32)
        for k in range(1, N_DEV):
            s = s + a_ref[k].astype(F32)
        o_ref[...] = s

    return pl.pallas_call(
        body, grid=(R // br,), in_specs=[pl.BlockSpec((N_DEV, br, Cc), lambda i: (0, i, 0))],
        out_specs=pl.BlockSpec((br, Cc), lambda i: (i, 0)), out_shape=jax.ShapeDtypeStruct((R, Cc), F32),
        name=name, compiler_params=_params(("parallel",)))(a)


def _pick_rows(R, cap):
    if R <= cap:
        return R
    for d in range(cap, 7, -8):
        if R % d == 0:
            return d
    return R


def _adamw(w, g, m, v, *, name):
    shape = w.shape
    as2d = (lambda t: t.reshape(1, -1)) if w.ndim == 1 else (lambda t: t)
    w2, m2, v2 = as2d(w), as2d(m), as2d(v)
    R, Cc = w2.shape
    from_slabs = g.ndim == 3
    br = _pick_rows(R, 128 if from_slabs else 256)
    c1 = 1.0 - ADAM_B1 ** ADAM_STEP
    c2 = 1.0 - ADAM_B2 ** ADAM_STEP

    def body(w_ref, g_ref, m_ref, v_ref, go_ref, d_ref, nm_ref, nv_ref):
        if from_slabs:
            gv = g_ref[0].astype(F32)
            for k in range(1, N_DEV):
                gv = gv + g_ref[k].astype(F32)
        else:
            gv = g_ref[...]
        go_ref[...] = gv
        nm = ADAM_B1 * m_ref[...] + (1.0 - ADAM_B1) * gv
        nv = ADAM_B2 * v_ref[...] + (1.0 - ADAM_B2) * (gv * gv)
        nm_ref[...] = nm
        nv_ref[...] = nv
        d_ref[...] = -ADAM_LR * ((nm / c1) / (jnp.sqrt(nv / c2) + ADAM_EPS) + ADAM_WD * w_ref[...])

    blk = pl.BlockSpec((br, Cc), lambda i: (i, 0))
    g_spec = pl.BlockSpec((N_DEV, br, Cc), lambda i: (0, i, 0)) if from_slabs else blk
    outs = pl.pallas_call(
        body, grid=(R // br,), in_specs=[blk, g_spec, blk, blk], out_specs=[blk] * 4,
        out_shape=[jax.ShapeDtypeStruct((R, Cc), F32)] * 4, name=name,
        compiler_params=_params(("parallel",)))(w2, g if from_slabs else as2d(g), m2, v2)
    return tuple(o.reshape(shape) for o in outs)


_BIG = ("w_in", "w_out", "wq_mem", "wk_mem", "wv_mem", "wo_mem", "w_up", "w_down")
_COL_SHARDED = ("w_in", "w_up")
_WEIGHTS = ("norm1_w", "w_in", "gdn_conv_w", "gdn_a_log", "gdn_dt_bias", "gdn_norm_w", "ssm_conv_w", "ssm_conv_b",
            "ssm_a_log", "ssm_dt_bias", "ssm_d", "ssm_norm_w", "w_out", "norm2_w", "mem_norm_w", "wq_mem", "wk_mem",
            "wv_mem", "wo_mem", "norm3_w", "w_up", "w_down", "final_norm_w")
_IN_PAD = 112


def _move_col_slabs(a, to_slabs, *, name):
    n, R, c = (N_DEV, a.shape[0], a.shape[1] // N_DEV) if to_slabs else a.shape
    slab = pl.BlockSpec((None, R, c), lambda j: (j, 0, 0))
    cols = pl.BlockSpec((R, c), lambda j: (0, j))

    def body(a_ref, o_ref):
        o_ref[...] = a_ref[...]

    return pl.pallas_call(
        body, grid=(n,), in_specs=[cols if to_slabs else slab], out_specs=slab if to_slabs else cols,
        out_shape=jax.ShapeDtypeStruct((n, R, c) if to_slabs else (R, n * c), a.dtype), name=name,
        compiler_params=_params(("parallel",)))(a)


def _full_from_slots(name, g):
    if name in _COL_SHARDED:
        if g.shape[2] % 128 == 0:
            return _move_col_slabs(g, False, name="cols_" + name)
        return jnp.transpose(g, (1, 0, 2)).reshape(g.shape[1], N_DEV * g.shape[2])
    return g.reshape(N_DEV * g.shape[1], g.shape[2])


def _slots_from_full(name, f):
    if name in _COL_SHARDED:
        if (f.shape[1] // N_DEV) % 128 == 0:
            return _move_col_slabs(f, True, name="slabs_" + name)
        return jnp.transpose(f.reshape(f.shape[0], N_DEV, f.shape[1] // N_DEV), (1, 0, 2))
    return f.reshape(N_DEV, f.shape[0] // N_DEV, f.shape[1])


def _pad_w_in(w):
    z = jnp.zeros((w.shape[0], _IN_PAD), w.dtype)
    return jnp.concatenate([w[:, :4096], w[:, 4112:6672], w[:, 4096:4112], z, w[:, 6672:6688], z], axis=1)


def _unpad_w_in(gp):
    return jnp.concatenate([gp[:, :4096], gp[:, C_GATE:C_GATE + 16], gp[:, 4096:C_GATE], gp[:, C_DT:C_DT + 16]],
                           axis=1)


def _pack_rows(vals):
    rows, offs, r = [], [], 0
    for vflat in vals:
        nrow = 8 * -(-vflat.shape[0] // 1024)
        rows.append(jnp.pad(vflat, (0, nrow * 128 - vflat.shape[0])).reshape(nrow, 128))
        offs.append((r, vflat.shape[0]))
        r += nrow
    return jnp.concatenate(rows, axis=0), offs


def _unpack_rows(packed, offs, shapes):
    out = []
    for (r, nel), shp in zip(offs, shapes):
        nrow = -(-nel // 128)
        out.append(packed[r:r + nrow].reshape(-1)[:nel].reshape(shp))
    return out


def kernel(x, mem, norm1_w, w_in, gdn_conv_w, gdn_a_log, gdn_dt_bias, gdn_norm_w, ssm_conv_w, ssm_conv_b, ssm_a_log, ssm_dt_bias, ssm_d, ssm_norm_w, w_out, norm2_w, mem_norm_w, wq_mem, wk_mem, wv_mem, wo_mem, norm3_w, w_up, w_down, final_norm_w, loss_target, m_norm1_w, m_w_in, m_gdn_conv_w, m_gdn_a_log, m_gdn_dt_bias, m_gdn_norm_w, m_ssm_conv_w, m_ssm_conv_b, m_ssm_a_log, m_ssm_dt_bias, m_ssm_d, m_ssm_norm_w, m_w_out, m_norm2_w, m_mem_norm_w, m_wq_mem, m_wk_mem, m_wv_mem, m_wo_mem, m_norm3_w, m_w_up, m_w_down, m_final_norm_w, v_norm1_w, v_w_in, v_gdn_conv_w, v_gdn_a_log, v_gdn_dt_bias, v_gdn_norm_w, v_ssm_conv_w, v_ssm_conv_b, v_ssm_a_log, v_ssm_dt_bias, v_ssm_d, v_ssm_norm_w, v_w_out, v_norm2_w, v_mem_norm_w, v_wq_mem, v_wk_mem, v_wv_mem, v_wo_mem, v_norm3_w, v_w_up, v_w_down, v_final_norm_w):
    args = dict(locals())
    w_loc = {n: args[n] for n in _WEIGHTS}
    me = 4 * lax.axis_index("x") + 2 * lax.axis_index("y") + lax.axis_index("c")

    w_in_full = _full_from_slots("w_in", _all_gather([w_in], BF16, name="gather_w_in")[0])
    later = _EARLY + _LATE
    shards = dict(zip(later, _cast_bf16([w_loc[n] for n in later], name="cast_shards")))
    conv_pack, conv_offs = _pack_rows([gdn_conv_w.reshape(-1), ssm_conv_w.reshape(-1)])
    conv_all = _all_gather([conv_pack], F32, name="gather_conv")[0]
    gdn_cw, ssm_cw = [], []
    for k in range(N_DEV):
        a, b = _unpack_rows(conv_all[k], conv_offs, [gdn_conv_w.shape, ssm_conv_w.shape])
        gdn_cw.append(a)
        ssm_cw.append(b)
    W = {
        "w_in_pad": _pad_w_in(w_in_full),
        "norm1_w": norm1_w, "norm2_w": norm2_w, "norm3_w": norm3_w, "mem_norm_w": mem_norm_w,
        "final_norm_w": final_norm_w, "ssm_norm_w": ssm_norm_w, "ssm_conv_b": ssm_conv_b,
        "gdn_conv_w": jnp.concatenate(gdn_cw, axis=1), "ssm_conv_w": jnp.concatenate(ssm_cw, axis=1),
        "gdn_alog_row": jnp.pad(gdn_a_log, (GDN_H, 128 - 2 * GDN_H)).reshape(1, 128),
        "gdn_dtb_row": jnp.pad(gdn_dt_bias, (GDN_H, 128 - 2 * GDN_H)).reshape(1, 128),
        "gdn_norm_x": jnp.tile(gdn_norm_w, GDN_H).reshape(1, D),
        "ssm_dtb_row": jnp.pad(ssm_dt_bias, (0, 128 - SSM_H)).reshape(1, 128),
        "ssm_alog_row": jnp.pad(ssm_a_log, (0, 128 - SSM_H)).reshape(1, 128),
        "ssm_d_x": jnp.repeat(ssm_d, SSM_P).reshape(1, D),
    }

    loss_part, grad_x, G = _local_step(x[0], mem[0], loss_target[0], W, shards)

    grads = {n: G[n] for n in _BIG}

    small = {
        "norm1_w": G["norm1_w"], "gdn_conv_w": G["gdn_conv_w"], "gdn_a_log": G["gdn_alog_row"][0, GDN_H:2 * GDN_H],
        "gdn_dt_bias": G["gdn_dtb_row"][0, GDN_H:2 * GDN_H], "gdn_norm_w": G["gdn_norm_x"].reshape(GDN_H, 128).sum(0),
        "ssm_conv_w": G["ssm_conv_w"], "ssm_conv_b": G["ssm_conv_b"],
        "ssm_a_log": G["ssm_alog_row"][0, :SSM_H], "ssm_dt_bias": G["ssm_dtb_row"][0, :SSM_H],
        "ssm_d": G["ssm_d_x"].reshape(SSM_H, SSM_P).sum(1), "ssm_norm_w": G["ssm_norm_w"].reshape(D),
        "norm2_w": G["norm2_w"], "mem_norm_w": G["mem_norm_w"], "norm3_w": G["norm3_w"],
        "final_norm_w": G["final_norm_w"], "loss": loss_part[0, :1],
    }
    names = list(small)
    pack, offs = _pack_rows([small[n].reshape(-1) for n in names])
    tot = _sum8(_all_gather([pack], F32, name="gather_small")[0], name="sum_small")
    summed = dict(zip(names, _unpack_rows(tot, offs, [small[n].shape for n in names])))
    loss = summed.pop("loss")[0]
    for n in ("gdn_conv_w", "ssm_conv_w"):
        width = w_loc[n].shape[1]
        summed[n] = lax.dynamic_slice_in_dim(summed[n], me * width, width, axis=1)
    grads.update(summed)

    upd = {n: _adamw(w_loc[n], grads[n], args["m_" + n], args["v_" + n], name="adamw_" + n) for n in _WEIGHTS}
    return (loss, grad_x[None], *[upd[n][0] for n in _WEIGHTS], *[upd[n][1] for n in _WEIGHTS],
            *[upd[n][2] for n in _WEIGHTS], *[upd[n][3] for n in _WEIGHTS])
```
